```python
import jax, jax.numpy as jnp
from jax import lax
import numpy as np

D_MODEL = 1024
BATCH = 8
SEQ = 4096
DEPTH = 1

CHUNK = 64
Q_BLOCK = 128

HEAD_DIM = 64
N_SB_HEADS = 8
D_SB = N_SB_HEADS * HEAD_DIM
N_MLA_HEADS = 8
QK_NOPE_DIM = 64
QK_ROPE_DIM = 32
V_HEAD_DIM = 64
Q_LORA_RANK = 256
KV_LORA_RANK = 128
D_MLA = N_MLA_HEADS * V_HEAD_DIM
D_MIX = D_SB + D_MLA
ROPE_THETA = 10000.0
PLE_DIM = 256
EPS = 1e-6

IN_SPLITS = (D_SB, D_SB, D_SB, D_SB, Q_LORA_RANK, KV_LORA_RANK, QK_ROPE_DIM, D_MLA)
D_IN = sum(IN_SPLITS)
IN_SPLIT_IDX = tuple(int(v) for v in np.cumsum(IN_SPLITS)[:-1])

kernel_name = "hybrid_stickbreak_mla_block"


def rms_norm(x, g):
    xf = x.astype(jnp.float32)
    y = xf * lax.rsqrt(jnp.mean(xf * xf, axis=-1, keepdims=True) + EPS)
    return (y * g.astype(jnp.float32)).astype(x.dtype)


def head_rms_norm(o, g):
    B, S, H, d = o.shape
    return rms_norm(o, g.reshape(H, d)).reshape(B, S, H * d)


def to_blocks(a):
    B, S = a.shape[:2]
    return a.reshape(B, S // Q_BLOCK, Q_BLOCK, *a.shape[2:]).swapaxes(0, 1)


def from_blocks(a):
    a = a.swapaxes(0, 1)
    return a.reshape(a.shape[0], a.shape[1] * a.shape[2], *a.shape[3:])


def apply_rope(x, positions):
    half = x.shape[-1] // 2
    freq = ROPE_THETA ** (-jnp.arange(half, dtype=jnp.float32) / half)
    ang = positions.astype(jnp.float32)[..., None] * freq
    ang = ang.reshape(ang.shape[:2] + (1,) * (x.ndim - 3) + (half,))
    cos, sin = jnp.cos(ang).astype(x.dtype), jnp.sin(ang).astype(x.dtype)
    x1, x2 = x[..., :half], x[..., half:]
    return jnp.concatenate([x1 * cos - x2 * sin, x2 * cos + x1 * sin], axis=-1)


def stick_breaking_attention(q, k, v):
    S = k.shape[1]
    scale = HEAD_DIM ** -0.5
    key_idx = jnp.arange(S)

    def block(args):
        b_idx, q_blk = args
        z = jnp.einsum('bqhd,bkhd->bhqk', q_blk, k).astype(jnp.float32) * scale
        t_idx = b_idx * Q_BLOCK + jnp.arange(Q_BLOCK)
        past = key_idx[None, :] < t_idx[:, None]
        log_fail = jnp.where(past, jax.nn.log_sigmoid(-z), 0.0)
        suffix = lax.cumsum(log_fail, axis=3, reverse=True) - log_fail
        w = jnp.where(past, jnp.exp(jax.nn.log_sigmoid(z) + suffix), 0.0)
        return jnp.einsum('bhqk,bkhd->bqhd', w.astype(v.dtype), v)

    out = lax.map(block, (jnp.arange(S // Q_BLOCK), to_blocks(q)))
    return from_blocks(out)


def latent_attention(q_nope, q_rope, k_nope, k_rope, v):
    S = k_nope.shape[1]
    scale = (QK_NOPE_DIM + QK_ROPE_DIM) ** -0.5
    key_chunk = jnp.arange(S) // CHUNK

    def block(args):
        b_idx, qn, qr = args
        z = (jnp.einsum('bqhd,bkhd->bhqk', qn, k_nope)
             + jnp.einsum('bqhr,bkr->bhqk', qr, k_rope)).astype(jnp.float32) * scale
        q_chunk = (b_idx * Q_BLOCK + jnp.arange(Q_BLOCK)) // CHUNK
        visible = key_chunk[None, :] <= q_chunk[:, None]
        z = jnp.where(visible, z, -jnp.inf)
        w = jax.nn.softmax(z, axis=-1)
        return jnp.einsum('bhqk,bkhd->bqhd', w.astype(v.dtype), v)

    out = lax.map(block, (jnp.arange(S // Q_BLOCK), to_blocks(q_nope), to_blocks(q_rope)))
    return from_blocks(out)


def _fwd_setup_inputs(seed: int = 0) -> dict:
    key = jax.random.key(seed)
    ks = jax.random.split(key, 20)

    def w(k, shape, fan_in):
        return jax.random.normal(k, shape, jnp.float32) * fan_in ** -0.5

    def gain(k, n):
        return 1.0 + 0.05 * jax.random.normal(k, (DEPTH, n), jnp.float32)

    x = jax.random.normal(ks[0], (BATCH, SEQ, D_MODEL), jnp.float32)
    p = jax.random.normal(ks[1], (DEPTH, BATCH, SEQ, PLE_DIM), jnp.float32)
    start = jax.random.randint(ks[2], (BATCH, 1), 0, 4096, dtype=jnp.int32)
    positions = start + jnp.arange(SEQ, dtype=jnp.int32)[None, :]
    return {
        'x': x,
        'p': p,
        'positions': positions,
        'norm_pre_g': gain(ks[3], D_MODEL),
        'w_in': w(ks[4], (DEPTH, D_MODEL, D_IN), D_MODEL),
        'q_norm_g': gain(ks[5], Q_LORA_RANK),
        'w_uq': w(ks[6], (DEPTH, Q_LORA_RANK, N_MLA_HEADS * (QK_NOPE_DIM + QK_ROPE_DIM)), Q_LORA_RANK),
        'kv_norm_g': gain(ks[7], KV_LORA_RANK),
        'w_ukv': w(ks[8], (DEPTH, KV_LORA_RANK, N_MLA_HEADS * (QK_NOPE_DIM + V_HEAD_DIM)), KV_LORA_RANK),
        'sb_out_norm_g': gain(ks[9], D_SB),
        'mla_out_norm_g': gain(ks[10], D_MLA),
        'w_out': w(ks[11], (DEPTH, D_MIX, D_MODEL), D_MIX),
        'norm_post_g': gain(ks[12], D_MODEL),
        'w_ple': w(ks[13], (DEPTH, PLE_DIM, D_MODEL), PLE_DIM),
        'ple_norm_g': gain(ks[14], D_MODEL),
        'w_ple_gate': w(ks[15], (DEPTH, D_MODEL, D_MODEL), D_MODEL),
        'b_ple_gate': 0.02 * jax.random.normal(ks[16], (DEPTH, D_MODEL), jnp.float32),
    }


def _fwd_reference(x, p, positions, norm_pre_g, w_in, q_norm_g, w_uq, kv_norm_g, w_ukv,
              sb_out_norm_g, mla_out_norm_g, w_out, norm_post_g, w_ple, ple_norm_g,
              w_ple_gate, b_ple_gate):
    B, S, _ = x.shape
    for i in range(DEPTH):
        h = rms_norm(x, norm_pre_g[i])
        proj = h @ w_in[i]
        sb_q, sb_k, sb_v, sb_g, c_q, c_kv, k_rope, mla_g = jnp.split(proj, IN_SPLIT_IDX, axis=-1)

        sb_o = stick_breaking_attention(sb_q.reshape(B, S, N_SB_HEADS, HEAD_DIM),
                                        sb_k.reshape(B, S, N_SB_HEADS, HEAD_DIM),
                                        sb_v.reshape(B, S, N_SB_HEADS, HEAD_DIM))
        sb_y = head_rms_norm(sb_o, sb_out_norm_g[i]) * jax.nn.silu(sb_g)

        q = (rms_norm(c_q, q_norm_g[i]) @ w_uq[i]).reshape(B, S, N_MLA_HEADS, QK_NOPE_DIM + QK_ROPE_DIM)
        q_nope, q_rope = q[..., :QK_NOPE_DIM], apply_rope(q[..., QK_NOPE_DIM:], positions)
        kv = (rms_norm(c_kv, kv_norm_g[i]) @ w_ukv[i]).reshape(B, S, N_MLA_HEADS, QK_NOPE_DIM + V_HEAD_DIM)
        k_nope, v = kv[..., :QK_NOPE_DIM], kv[..., QK_NOPE_DIM:]
        k_rope = apply_rope(k_rope, positions)
        mla_o = latent_attention(q_nope, q_rope, k_nope, k_rope, v)
        mla_y = head_rms_norm(mla_o, mla_out_norm_g[i]) * jax.nn.silu(mla_g)

        y = jnp.concatenate([sb_y, mla_y], axis=-1) @ w_out[i]
        x = x + rms_norm(y, norm_post_g[i])

        ple = rms_norm(p[i] @ w_ple[i], ple_norm_g[i])
        x = x + ple * jax.nn.sigmoid(x @ w_ple_gate[i] + b_ple_gate[i])
    return x


import jax as _jax
import jax.numpy as _jnp

TWIN_FORMAT = 'train_step'
FWD_PARAMS = ['x', 'p', 'positions', 'norm_pre_g', 'w_in', 'q_norm_g', 'w_uq', 'kv_norm_g', 'w_ukv', 'sb_out_norm_g', 'mla_out_norm_g', 'w_out', 'norm_post_g', 'w_ple', 'ple_norm_g', 'w_ple_gate', 'b_ple_gate']
TWIN_WEIGHTS = ['norm_pre_g', 'w_in', 'q_norm_g', 'w_uq', 'kv_norm_g', 'w_ukv', 'sb_out_norm_g', 'mla_out_norm_g', 'w_out', 'norm_post_g', 'w_ple', 'ple_norm_g', 'w_ple_gate', 'b_ple_gate']
TWIN_DIFF_INPUT = 'x'
TWIN_INPUTS = ['x', 'p', 'positions', 'norm_pre_g', 'w_in', 'q_norm_g', 'w_uq', 'kv_norm_g', 'w_ukv', 'sb_out_norm_g', 'mla_out_norm_g', 'w_out', 'norm_post_g', 'w_ple', 'ple_norm_g', 'w_ple_gate', 'b_ple_gate', 'loss_target', 'm_norm_pre_g', 'm_w_in', 'm_q_norm_g', 'm_w_uq', 'm_kv_norm_g', 'm_w_ukv', 'm_sb_out_norm_g', 'm_mla_out_norm_g', 'm_w_out', 'm_norm_post_g', 'm_w_ple', 'm_ple_norm_g', 'm_w_ple_gate', 'm_b_ple_gate', 'v_norm_pre_g', 'v_w_in', 'v_q_norm_g', 'v_w_uq', 'v_kv_norm_g', 'v_w_ukv', 'v_sb_out_norm_g', 'v_mla_out_norm_g', 'v_w_out', 'v_norm_post_g', 'v_w_ple', 'v_ple_norm_g', 'v_w_ple_gate', 'v_b_ple_gate']
TWIN_OUTPUTS = ['loss', 'grad_x', 'grad_norm_pre_g', 'grad_w_in', 'grad_q_norm_g', 'grad_w_uq', 'grad_kv_norm_g', 'grad_w_ukv', 'grad_sb_out_norm_g', 'grad_mla_out_norm_g', 'grad_w_out', 'grad_norm_post_g', 'grad_w_ple', 'grad_ple_norm_g', 'grad_w_ple_gate', 'grad_b_ple_gate', 'delta_norm_pre_g', 'delta_w_in', 'delta_q_norm_g', 'delta_w_uq', 'delta_kv_norm_g', 'delta_w_ukv', 'delta_sb_out_norm_g', 'delta_mla_out_norm_g', 'delta_w_out', 'delta_norm_post_g', 'delta_w_ple', 'delta_ple_norm_g', 'delta_w_ple_gate', 'delta_b_ple_gate', 'new_m_norm_pre_g', 'new_m_w_in', 'new_m_q_norm_g', 'new_m_w_uq', 'new_m_kv_norm_g', 'new_m_w_ukv', 'new_m_sb_out_norm_g', 'new_m_mla_out_norm_g', 'new_m_w_out', 'new_m_norm_post_g', 'new_m_w_ple', 'new_m_ple_norm_g', 'new_m_w_ple_gate', 'new_m_b_ple_gate', 'new_v_norm_pre_g', 'new_v_w_in', 'new_v_q_norm_g', 'new_v_w_uq', 'new_v_kv_norm_g', 'new_v_w_ukv', 'new_v_sb_out_norm_g', 'new_v_mla_out_norm_g', 'new_v_w_out', 'new_v_norm_post_g', 'new_v_w_ple', 'new_v_ple_norm_g', 'new_v_w_ple_gate', 'new_v_b_ple_gate']
TWIN_LEAF_KINDS = {'loss': 'loss', 'grad_x': 'grad_x', 'grad_norm_pre_g': 'grad_w', 'grad_w_in': 'grad_w', 'grad_q_norm_g': 'grad_w', 'grad_w_uq': 'grad_w', 'grad_kv_norm_g': 'grad_w', 'grad_w_ukv': 'grad_w', 'grad_sb_out_norm_g': 'grad_w', 'grad_mla_out_norm_g': 'grad_w', 'grad_w_out': 'grad_w', 'grad_norm_post_g': 'grad_w', 'grad_w_ple': 'grad_w', 'grad_ple_norm_g': 'grad_w', 'grad_w_ple_gate': 'grad_w', 'grad_b_ple_gate': 'grad_w', 'delta_norm_pre_g': 'delta_w', 'delta_w_in': 'delta_w', 'delta_q_norm_g': 'delta_w', 'delta_w_uq': 'delta_w', 'delta_kv_norm_g': 'delta_w', 'delta_w_ukv': 'delta_w', 'delta_sb_out_norm_g': 'delta_w', 'delta_mla_out_norm_g': 'delta_w', 'delta_w_out': 'delta_w', 'delta_norm_post_g': 'delta_w', 'delta_w_ple': 'delta_w', 'delta_ple_norm_g': 'delta_w', 'delta_w_ple_gate': 'delta_w', 'delta_b_ple_gate': 'delta_w', 'new_m_norm_pre_g': 'new_m', 'new_m_w_in': 'new_m', 'new_m_q_norm_g': 'new_m', 'new_m_w_uq': 'new_m', 'new_m_kv_norm_g': 'new_m', 'new_m_w_ukv': 'new_m', 'new_m_sb_out_norm_g': 'new_m', 'new_m_mla_out_norm_g': 'new_m', 'new_m_w_out': 'new_m', 'new_m_norm_post_g': 'new_m', 'new_m_w_ple': 'new_m', 'new_m_ple_norm_g': 'new_m', 'new_m_w_ple_gate': 'new_m', 'new_m_b_ple_gate': 'new_m', 'new_v_norm_pre_g': 'new_v', 'new_v_w_in': 'new_v', 'new_v_q_norm_g': 'new_v', 'new_v_w_uq': 'new_v', 'new_v_kv_norm_g': 'new_v', 'new_v_w_ukv': 'new_v', 'new_v_sb_out_norm_g': 'new_v', 'new_v_mla_out_norm_g': 'new_v', 'new_v_w_out': 'new_v', 'new_v_norm_post_g': 'new_v', 'new_v_w_ple': 'new_v', 'new_v_ple_norm_g': 'new_v', 'new_v_w_ple_gate': 'new_v', 'new_v_b_ple_gate': 'new_v'}


def _forward(args):
    return _fwd_reference(*[args[k] for k in FWD_PARAMS])


def _output_shape():
    out = _jax.eval_shape(lambda: _forward(_fwd_setup_inputs(0)))
    return out.shape, out.dtype

N_MICROBATCH = 1
ADAM_LR = 0.001
ADAM_B1 = 0.9
ADAM_B2 = 0.999
ADAM_EPS = 1e-08
ADAM_WD = 0.01
ADAM_STEP = 10
PER_EXAMPLE_BATCH_AXIS = {'x': 0, 'p': 1, 'positions': 0, 'loss_target': 0}
SHARED_INPUTS = []
_WEIGHT_DTYPES = {'norm_pre_g': _jnp.float32, 'w_in': _jnp.float32, 'q_norm_g': _jnp.float32, 'w_uq': _jnp.float32, 'kv_norm_g': _jnp.float32, 'w_ukv': _jnp.float32, 'sb_out_norm_g': _jnp.float32, 'mla_out_norm_g': _jnp.float32, 'w_out': _jnp.float32, 'norm_post_g': _jnp.float32, 'w_ple': _jnp.float32, 'ple_norm_g': _jnp.float32, 'w_ple_gate': _jnp.float32, 'b_ple_gate': _jnp.float32}
MOMENT_SCALE = {'norm_pre_g': 4.960096e-01, 'w_in': 2.882087e-01, 'q_norm_g': 4.510577e-01, 'w_uq': 2.686253e-01, 'kv_norm_g': 1.228133e+00, 'w_ukv': 3.291302e-01, 'sb_out_norm_g': 2.786464e-01, 'mla_out_norm_g': 3.636730e-01, 'w_out': 3.125601e-01, 'norm_post_g': 3.270551e+01, 'w_ple': 1.818982e-01, 'ple_norm_g': 1.015494e+01, 'w_ple_gate': 1.142887e-01, 'b_ple_gate': 2.910933e+00}


def _to_microbatches(a, axis):
    t = _jnp.moveaxis(a, axis, 0)
    t = t.reshape((N_MICROBATCH, t.shape[0] // N_MICROBATCH) + t.shape[1:])
    return _jnp.moveaxis(t, 1, axis + 1)


def setup_inputs(seed: int = 0) -> dict:
    inp = _fwd_setup_inputs(seed)
    key = _jax.random.fold_in(_jax.random.key(seed), 7919)
    shape, _ = _output_shape()
    out = dict(inp)
    out["loss_target"] = _jax.random.normal(_jax.random.fold_in(key, 0), shape, _jnp.float32)
    for i, name in enumerate(TWIN_WEIGHTS):
        w = inp[name].astype(_jnp.float32)
        if MOMENT_SCALE is None:
            s = _jnp.sqrt(_jnp.mean(_jnp.square(w)) + 1e-30)
        else:
            s = MOMENT_SCALE[name]
        km, kv = _jax.random.split(_jax.random.fold_in(key, i + 1))
        out[name] = w
        out["m_" + name] = s * _jax.random.normal(km, w.shape, _jnp.float32)
        out["v_" + name] = (s * s) * _jax.random.uniform(kv, w.shape, _jnp.float32, 0.5, 1.5)
    if N_MICROBATCH > 1:
        for name, axis in PER_EXAMPLE_BATCH_AXIS.items():
            out[name] = _to_microbatches(out[name], axis)
    return {'x': out['x'], 'p': out['p'], 'positions': out['positions'], 'norm_pre_g': out['norm_pre_g'], 'w_in': out['w_in'], 'q_norm_g': out['q_norm_g'], 'w_uq': out['w_uq'], 'kv_norm_g': out['kv_norm_g'], 'w_ukv': out['w_ukv'], 'sb_out_norm_g': out['sb_out_norm_g'], 'mla_out_norm_g': out['mla_out_norm_g'], 'w_out': out['w_out'], 'norm_post_g': out['norm_post_g'], 'w_ple': out['w_ple'], 'ple_norm_g': out['ple_norm_g'], 'w_ple_gate': out['w_ple_gate'], 'b_ple_gate': out['b_ple_gate'], 'loss_target': out['loss_target'], 'm_norm_pre_g': out['m_norm_pre_g'], 'm_w_in': out['m_w_in'], 'm_q_norm_g': out['m_q_norm_g'], 'm_w_uq': out['m_w_uq'], 'm_kv_norm_g': out['m_kv_norm_g'], 'm_w_ukv': out['m_w_ukv'], 'm_sb_out_norm_g': out['m_sb_out_norm_g'], 'm_mla_out_norm_g': out['m_mla_out_norm_g'], 'm_w_out': out['m_w_out'], 'm_norm_post_g': out['m_norm_post_g'], 'm_w_ple': out['m_w_ple'], 'm_ple_norm_g': out['m_ple_norm_g'], 'm_w_ple_gate': out['m_w_ple_gate'], 'm_b_ple_gate': out['m_b_ple_gate'], 'v_norm_pre_g': out['v_norm_pre_g'], 'v_w_in': out['v_w_in'], 'v_q_norm_g': out['v_q_norm_g'], 'v_w_uq': out['v_w_uq'], 'v_kv_norm_g': out['v_kv_norm_g'], 'v_w_ukv': out['v_w_ukv'], 'v_sb_out_norm_g': out['v_sb_out_norm_g'], 'v_mla_out_norm_g': out['v_mla_out_norm_g'], 'v_w_out': out['v_w_out'], 'v_norm_post_g': out['v_norm_post_g'], 'v_w_ple': out['v_w_ple'], 'v_ple_norm_g': out['v_ple_norm_g'], 'v_w_ple_gate': out['v_w_ple_gate'], 'v_b_ple_gate': out['v_b_ple_gate']}


def _loss(weights, diff, rest, loss_target):
    with _jax.named_scope("forward"):
        args = {**rest, TWIN_DIFF_INPUT: diff, **{k: w.astype(_WEIGHT_DTYPES[k]) for k, w in weights.items()}}
        y = _forward(args)
    with _jax.named_scope("loss_head"):
        err = _jnp.square(y.astype(_jnp.float32) - loss_target)
        return 0.5 * _jnp.sum(_jnp.mean(err, axis=-1)) if err.ndim else 0.5 * err


def _adamw(w, g, m, v):
    m = ADAM_B1 * m + (1.0 - ADAM_B1) * g
    v = ADAM_B2 * v + (1.0 - ADAM_B2) * _jnp.square(g)
    m_hat = m / (1.0 - ADAM_B1 ** ADAM_STEP)
    v_hat = v / (1.0 - ADAM_B2 ** ADAM_STEP)
    delta = -ADAM_LR * (m_hat / (_jnp.sqrt(v_hat) + ADAM_EPS) + ADAM_WD * w)
    return delta, m, v


def reference(x, p, positions, norm_pre_g, w_in, q_norm_g, w_uq, kv_norm_g, w_ukv, sb_out_norm_g, mla_out_norm_g, w_out, norm_post_g, w_ple, ple_norm_g, w_ple_gate, b_ple_gate, loss_target, m_norm_pre_g, m_w_in, m_q_norm_g, m_w_uq, m_kv_norm_g, m_w_ukv, m_sb_out_norm_g, m_mla_out_norm_g, m_w_out, m_norm_post_g, m_w_ple, m_ple_norm_g, m_w_ple_gate, m_b_ple_gate, v_norm_pre_g, v_w_in, v_q_norm_g, v_w_uq, v_kv_norm_g, v_w_ukv, v_sb_out_norm_g, v_mla_out_norm_g, v_w_out, v_norm_post_g, v_w_ple, v_ple_norm_g, v_w_ple_gate, v_b_ple_gate):
    given = dict(x=x, p=p, positions=positions, norm_pre_g=norm_pre_g, w_in=w_in, q_norm_g=q_norm_g, w_uq=w_uq, kv_norm_g=kv_norm_g, w_ukv=w_ukv, sb_out_norm_g=sb_out_norm_g, mla_out_norm_g=mla_out_norm_g, w_out=w_out, norm_post_g=norm_post_g, w_ple=w_ple, ple_norm_g=ple_norm_g, w_ple_gate=w_ple_gate, b_ple_gate=b_ple_gate, loss_target=loss_target, m_norm_pre_g=m_norm_pre_g, m_w_in=m_w_in, m_q_norm_g=m_q_norm_g, m_w_uq=m_w_uq, m_kv_norm_g=m_kv_norm_g, m_w_ukv=m_w_ukv, m_sb_out_norm_g=m_sb_out_norm_g, m_mla_out_norm_g=m_mla_out_norm_g, m_w_out=m_w_out, m_norm_post_g=m_norm_post_g, m_w_ple=m_w_ple, m_ple_norm_g=m_ple_norm_g, m_w_ple_gate=m_w_ple_gate, m_b_ple_gate=m_b_ple_gate, v_norm_pre_g=v_norm_pre_g, v_w_in=v_w_in, v_q_norm_g=v_q_norm_g, v_w_uq=v_w_uq, v_kv_norm_g=v_kv_norm_g, v_w_ukv=v_w_ukv, v_sb_out_norm_g=v_sb_out_norm_g, v_mla_out_norm_g=v_mla_out_norm_g, v_w_out=v_w_out, v_norm_post_g=v_norm_post_g, v_w_ple=v_w_ple, v_ple_norm_g=v_ple_norm_g, v_w_ple_gate=v_w_ple_gate, v_b_ple_gate=v_b_ple_gate)
    weights = {n: given[n] for n in TWIN_WEIGHTS}
    shared = {n: given[n] for n in SHARED_INPUTS}
    per_example = {n: given[n] for n in ['x', 'p', 'positions']}
    grad_fn = _jax.value_and_grad(_loss, argnums=(0, 1))

    def one_microbatch(ex, loss_target):
        ex = dict(ex)
        diff = ex.pop(TWIN_DIFF_INPUT)
        return grad_fn(weights, diff, {**shared, **ex}, loss_target)

    if N_MICROBATCH == 1:
        loss, (grad_w, grad_x) = one_microbatch(per_example, given["loss_target"])
    else:
        def body(carry, xs):
            loss_sum, grad_sum = carry
            l_k, (gw_k, gx_k) = one_microbatch(xs[0], xs[1])
            with _jax.named_scope("update"):
                return (loss_sum + l_k, _jax.tree.map(_jnp.add, grad_sum, gw_k)), gx_k

        init = (_jnp.zeros((), _jnp.float32), _jax.tree.map(_jnp.zeros_like, weights))
        (loss, grad_w), grad_x = _jax.lax.scan(body, init, (per_example, given["loss_target"]))
    with _jax.named_scope("update"):
        delta_w, new_m, new_v = {}, {}, {}
        for n in TWIN_WEIGHTS:
            delta_w[n], new_m[n], new_v[n] = _adamw(weights[n], grad_w[n], given["m_" + n], given["v_" + n])
    return (loss, grad_x, *[grad_w[n] for n in TWIN_WEIGHTS], *[delta_w[n] for n in TWIN_WEIGHTS],
            *[new_m[n] for n in TWIN_WEIGHTS], *[new_v[n] for n in TWIN_WEIGHTS])
```

```python
import functools

import jax
import jax.numpy as jnp
from jax import lax
from jax.experimental import pallas as pl
from jax.experimental.pallas import tpu as pltpu

F32 = jnp.float32
BF16 = jnp.bfloat16
MESH = pl.DeviceIdType.MESH

D_MODEL = 1024
HEAD_DIM = 64
D_SB = 512
D_MLA = 512
Q_LORA = 256
KV_LORA = 128
QK_NOPE = 64
QK_ROPE = 32
PLE_DIM = 256
D_IN = 2976
D_EXT = 3072
ROPE_THETA = 10000.0
EPS = 1e-6
N_SHARD = 4

ADAM_LR = 0.001
ADAM_B1 = 0.9
ADAM_B2 = 0.999
ADAM_EPS = 1e-08
ADAM_WD = 0.01
ADAM_STEP = 10

LANES = 128
BQ = 128
BK = 128
TM = 256
PACK_ROWS = 1408
HALF_ROWS = PACK_ROWS // 2
VEC_ROWS = 16
VMEM_DENSE = 52 * 1024 * 1024
VMEM_ATTN = 40 * 1024 * 1024


def _mm(a, b):
    return jnp.dot(a, b, preferred_element_type=F32)


def _mm_nt(a, b):
    return lax.dot_general(a, b, (((1,), (1,)), ((), ())), preferred_element_type=F32)


def _mm_tn(a, b):
    return lax.dot_general(a, b, (((0,), (0,)), ((), ())), preferred_element_type=F32)


def _tri(n, kind):
    r = lax.broadcasted_iota(jnp.int32, (n, n), 0)
    c = lax.broadcasted_iota(jnp.int32, (n, n), 1)
    m = {"ge": r >= c, "lt": r < c, "le": r <= c}[kind]
    return jnp.where(m, 1.0, 0.0).astype(BF16)


def _cum(a, u):
    hi = a.astype(BF16)
    lo = (a - hi.astype(F32)).astype(BF16)
    return _mm(hi, u) + _mm(lo, u)


def _seg(a, bd):
    a1 = a.astype(BF16)
    r1 = a - a1.astype(F32)
    a2 = r1.astype(BF16)
    a3 = (r1 - a2.astype(F32)).astype(BF16)
    return _mm(a1, bd) + _mm(a2, bd) + _mm(a3, bd)


def _blockdiag(n, seg):
    r = lax.broadcasted_iota(jnp.int32, (n, n), 0) // seg
    c = lax.broadcasted_iota(jnp.int32, (n, n), 1) // seg
    return jnp.where(r == c, 1.0, 0.0).astype(BF16)


def _sigmoid(a):
    return 1.0 / (1.0 + jnp.exp(-a))


def _rowmean(a):
    return jnp.mean(a, axis=-1, keepdims=True)


def _colsum(a):
    return jnp.sum(a, axis=0, keepdims=True)


def _rope_fwd(a, c, sa, sb):
    w = a.shape[-1]
    return a * c + pltpu.roll(a, w - 16, 1) * sa + pltpu.roll(a, 16, 1) * sb


def _rope_bwd(g, c, sa, sb):
    w = g.shape[-1]
    return g * c + pltpu.roll(g * sa, 16, 1) + pltpu.roll(g * sb, w - 16, 1)


def _full(shape):
    return pl.BlockSpec(shape, lambda *_: (0,) * len(shape))


def _rows(width, tm=TM):
    return pl.BlockSpec((tm, width), lambda i: (i, 0))


def _pre_fwd(x, tabs, gpre, win, gq, wuq, gkv, wk, wv):
    s = x.shape[0]
    c_t, sa_t, sb_t = tabs

    def body(x_ref, c_ref, sa_ref, sb_ref, gpre_ref, win_ref, gq_ref, wuq_ref, gkv_ref, wk_ref, wv_ref,
             sbq_ref, sbk_ref, sbv_ref, sbg_ref, mlag_ref, cq_ref, ckv_ref, qc_ref, kc_ref, mv_ref):
        xv = x_ref[...]
        r1 = lax.rsqrt(_rowmean(xv * xv) + EPS)
        h = (xv * r1 * gpre_ref[...]).astype(BF16)
        proj = _mm(h, win_ref[...])
        sbq_ref[...] = proj[:, 0:512].astype(BF16)
        sbk_ref[...] = proj[:, 512:1024].astype(BF16)
        sbv_ref[...] = proj[:, 1024:1536].astype(BF16)
        sbg_ref[...] = proj[:, 1536:2048]
        cq = proj[:, 2048:2304]
        ckv = proj[:, 2304:2432]
        kr = proj[:, 2432:2560]
        mlag_ref[...] = proj[:, 2560:3072]
        cq_ref[...] = cq
        ckv_ref[...] = ckv
        c1, sa1, sb1 = c_ref[...], sa_ref[...], sb_ref[...]
        c8, sa8, sb8 = jnp.tile(c1, (1, 8)), jnp.tile(sa1, (1, 8)), jnp.tile(sb1, (1, 8))
        cqn = (cq * lax.rsqrt(_rowmean(cq * cq) + EPS) * gq_ref[...]).astype(BF16)
        qe = _mm(cqn, wuq_ref[...])
        qc_ref[...] = _rope_fwd(qe, c8, sa8, sb8).astype(BF16)
        ckvn = (ckv * lax.rsqrt(_rowmean(ckv * ckv) + EPS) * gkv_ref[...]).astype(BF16)
        ke = _mm(ckvn, wk_ref[...])
        krr = _rope_fwd(kr, c1, sa1, sb1)
        kc_ref[...] = (ke + jnp.tile(krr, (1, 8))).astype(BF16)
        mv_ref[...] = _mm(ckvn, wv_ref[...]).astype(BF16)

    out_shape = (
        jax.ShapeDtypeStruct((s, 512), BF16), jax.ShapeDtypeStruct((s, 512), BF16), jax.ShapeDtypeStruct((s, 512), BF16),
        jax.ShapeDtypeStruct((s, 512), F32), jax.ShapeDtypeStruct((s, 512), F32),
        jax.ShapeDtypeStruct((s, Q_LORA), F32), jax.ShapeDtypeStruct((s, KV_LORA), F32),
        jax.ShapeDtypeStruct((s, 1024), BF16), jax.ShapeDtypeStruct((s, 1024), BF16), jax.ShapeDtypeStruct((s, 512), BF16),
    )
    return pl.pallas_call(
        body, name="pre_fwd", grid=(s // TM,), out_shape=out_shape,
        in_specs=[_rows(D_MODEL), _rows(LANES), _rows(LANES), _rows(LANES), _full((1, D_MODEL)), _full((D_MODEL, D_EXT)),
                  _full((1, Q_LORA)), _full((Q_LORA, 1024)), _full((1, KV_LORA)), _full((KV_LORA, 1024)), _full((KV_LORA, 512))],
        out_specs=(_rows(512), _rows(512), _rows(512), _rows(512), _rows(512), _rows(Q_LORA), _rows(KV_LORA),
                   _rows(1024), _rows(1024), _rows(512)),
        compiler_params=pltpu.CompilerParams(vmem_limit_bytes=VMEM_DENSE),
    )(x, c_t, sa_t, sb_t, gpre, win, gq, wuq, gkv, wk, wv)


def _logsig_neg(z):
    u = jnp.exp(-jnp.abs(z))
    return jnp.minimum(-z, 0.0) - jnp.log(1.0 + u), u


def _sb_fwd(q, k, v):
    s = q.shape[0]
    scale = HEAD_DIM ** -0.5

    def body(q_ref, k_ref, v_ref, o_ref, t_ref):
        i = pl.program_id(1)
        lane = lax.broadcasted_iota(jnp.int32, (1, LANES), 1)
        row = lax.broadcasted_iota(jnp.int32, (BQ, BK), 0) + i * BQ
        col = lax.broadcasted_iota(jnp.int32, (BQ, BK), 1)
        u_ge = _tri(BK, "ge")
        qp = q_ref[...]
        accs, tots = [], []
        for hh in range(2):
            hmask = (lane < 64) if hh == 0 else (lane >= 64)
            qh = jnp.where(hmask, qp, jnp.zeros_like(qp))

            def step(jj, carry, qh=qh):
                acc, run = carry
                j = i - jj
                ks = pl.multiple_of(j * BK, BK)
                kb = k_ref[pl.ds(ks, BK), :]
                vb = v_ref[pl.ds(ks, BK), :]
                z = _mm_nt(qh, kb) * scale
                valid = (col + j * BK) < row
                lg, _ = _logsig_neg(z)
                lg = jnp.where(valid, lg, 0.0)
                cin = _cum(lg, u_ge)
                w = jnp.where(valid, jnp.exp(z + cin + run), 0.0)
                acc = acc + _mm(w.astype(BF16), vb)
                run = run + cin[:, 0:1]
                return acc, run

            acc, run = lax.fori_loop(0, i + 1, step, (jnp.zeros((BQ, LANES), F32), jnp.zeros((BQ, 1), F32)))
            accs.append(acc)
            tots.append(run)
        o_ref[...] = jnp.where(lane < 64, accs[0], accs[1])
        t_ref[...] = jnp.where(lane < 64, tots[0], tots[1])

    qspec = pl.BlockSpec((BQ, LANES), lambda p, i: (i, p))
    kspec = pl.BlockSpec((s, LANES), lambda p, i: (0, p))
    return pl.pallas_call(
        body, name="sb_fwd", grid=(4, s // BQ),
        out_shape=(jax.ShapeDtypeStruct((s, 512), F32), jax.ShapeDtypeStruct((s, 512), F32)),
        in_specs=[qspec, kspec, kspec], out_specs=(qspec, qspec),
        compiler_params=pltpu.CompilerParams(vmem_limit_bytes=VMEM_ATTN),
    )(q, k, v)


def _sb_bwd(q, k, v, do, tstat):
    s = q.shape[0]
    scale = HEAD_DIM ** -0.5

    def body(q_ref, k_ref, v_ref, do_ref, t_ref, dq_ref, dk_ref, dv_ref):
        i = pl.program_id(1)

        @pl.when(i == 0)
        def _():
            dk_ref[...] = jnp.zeros_like(dk_ref)
            dv_ref[...] = jnp.zeros_like(dv_ref)

        lane = lax.broadcasted_iota(jnp.int32, (1, LANES), 1)
        row = lax.broadcasted_iota(jnp.int32, (BQ, BK), 0) + i * BQ
        col = lax.broadcasted_iota(jnp.int32, (BQ, BK), 1)
        u_lt = _tri(BK, "lt")
        u_le = _tri(BK, "le")
        qp, dop, tp = q_ref[...], do_ref[...], t_ref[...]
        dqs = []
        for hh in range(2):
            hmask = (lane < 64) if hh == 0 else (lane >= 64)
            qh = jnp.where(hmask, qp, jnp.zeros_like(qp))
            doh = jnp.where(hmask, dop, jnp.zeros_like(dop))
            tot = tp[:, 64 * hh:64 * hh + 1]

            def step(j, carry, qh=qh, doh=doh, tot=tot, hmask=hmask):
                dq, prun, erun = carry
                ks = pl.multiple_of(j * BK, BK)
                kb = k_ref[pl.ds(ks, BK), :]
                vb = v_ref[pl.ds(ks, BK), :]
                kbh = jnp.where(hmask, kb, jnp.zeros_like(kb))
                z = _mm_nt(qh, kb) * scale
                valid = (col + j * BK) < row
                lg, u = _logsig_neg(z)
                lg = jnp.where(valid, lg, 0.0)
                sig = jnp.where(z >= 0.0, 1.0, u) / (1.0 + u)
                pex = _cum(lg, u_lt) + prun
                w = jnp.where(valid, jnp.exp(z + (tot - pex)), 0.0)
                dw = _mm_nt(doh, vb)
                e = dw * w
                ecum = _cum(e, u_le) + erun
                dz = jnp.where(valid, (e - sig * ecum) * scale, 0.0).astype(BF16)
                dq = dq + _mm(dz, kbh)
                dk_ref[pl.ds(ks, BK), :] += _mm_tn(dz, qh)
                dv_ref[pl.ds(ks, BK), :] += _mm_tn(w.astype(BF16), doh)
                prun = pex[:, BK - 1:BK] + lg[:, BK - 1:BK]
                erun = ecum[:, BK - 1:BK]
                return dq, prun, erun

            dq, _, _ = lax.fori_loop(0, i + 1, step, (jnp.zeros((BQ, LANES), F32), jnp.zeros((BQ, 1), F32),
                                                      jnp.zeros((BQ, 1), F32)))
            dqs.append(dq)
        dq_ref[...] = (dqs[0] + dqs[1]).astype(BF16)

    qspec = pl.BlockSpec((BQ, LANES), lambda p, i: (i, p))
    kspec = pl.BlockSpec((s, LANES), lambda p, i: (0, p))
    return pl.pallas_call(
        body, name="sb_bwd", grid=(4, s // BQ),
        out_shape=(jax.ShapeDtypeStruct((s, 512), BF16), jax.ShapeDtypeStruct((s, 512), F32),
                   jax.ShapeDtypeStruct((s, 512), F32)),
        in_specs=[qspec, kspec, kspec, qspec, qspec], out_specs=(qspec, kspec, kspec),
        compiler_params=pltpu.CompilerParams(vmem_limit_bytes=VMEM_ATTN),
    )(q, k, v, do, tstat)


def _mla_fwd(qc, kc, v):
    s = qc.shape[0]
    scale = (QK_NOPE + QK_ROPE) ** -0.5

    def body(q_ref, k_ref, v_ref, o_ref, l_ref):
        i = pl.program_id(1)
        lane = lax.broadcasted_iota(jnp.int32, (1, LANES), 1)
        rowc = (lax.broadcasted_iota(jnp.int32, (BQ, BK), 0) + i * BQ) // 64
        col = lax.broadcasted_iota(jnp.int32, (BQ, BK), 1)
        qp = q_ref[...]
        outs, lses = [], []
        for hh in range(2):
            qh = qp[:, LANES * hh:LANES * (hh + 1)]

            def step(j, carry, qh=qh, hh=hh):
                m, l, acc = carry
                ks = pl.multiple_of(j * BK, BK)
                kb = k_ref[pl.ds(ks, BK), LANES * hh:LANES * (hh + 1)]
                vb = v_ref[pl.ds(ks, BK), :]
                z = _mm_nt(qh, kb) * scale
                valid = ((col + j * BK) // 64) <= rowc
                z = jnp.where(valid, z, -1e30)
                m_new = jnp.maximum(m, jnp.max(z, axis=-1, keepdims=True))
                a = jnp.exp(m - m_new)
                p = jnp.exp(z - m_new)
                l = a * l + jnp.sum(p, axis=-1, keepdims=True)
                acc = a * acc + _mm(p.astype(BF16), vb)
                return m_new, l, acc

            m, l, acc = lax.fori_loop(0, i + 1, step, (jnp.full((BQ, 1), -1e30, F32), jnp.zeros((BQ, 1), F32),
                                                       jnp.zeros((BQ, LANES), F32)))
            outs.append(acc / l)
            lses.append(m + jnp.log(l))
        o_ref[...] = jnp.where(lane < 64, outs[0], outs[1])
        l_ref[...] = jnp.where(lane < 64, lses[0], lses[1])

    qspec = pl.BlockSpec((BQ, 2 * LANES), lambda p, i: (i, p))
    kspec = pl.BlockSpec((s, 2 * LANES), lambda p, i: (0, p))
    vspec = pl.BlockSpec((s, LANES), lambda p, i: (0, p))
    ospec = pl.BlockSpec((BQ, LANES), lambda p, i: (i, p))
    return pl.pallas_call(
        body, name="mla_fwd", grid=(4, s // BQ),
        out_shape=(jax.ShapeDtypeStruct((s, 512), F32), jax.ShapeDtypeStruct((s, 512), F32)),
        in_specs=[qspec, kspec, vspec], out_specs=(ospec, ospec),
        compiler_params=pltpu.CompilerParams(vmem_limit_bytes=VMEM_ATTN),
    )(qc, kc, v)


def _mla_bwd(qc, kc, v, do, lse, delta):
    s = qc.shape[0]
    scale = (QK_NOPE + QK_ROPE) ** -0.5

    def body(q_ref, k_ref, v_ref, do_ref, l_ref, d_ref, dq_ref, dk_ref, dv_ref):
        i = pl.program_id(1)

        @pl.when(i == 0)
        def _():
            dk_ref[...] = jnp.zeros_like(dk_ref)
            dv_ref[...] = jnp.zeros_like(dv_ref)

        lane = lax.broadcasted_iota(jnp.int32, (1, LANES), 1)
        rowc = (lax.broadcasted_iota(jnp.int32, (BQ, BK), 0) + i * BQ) // 64
        col = lax.broadcasted_iota(jnp.int32, (BQ, BK), 1)
        qp, dop, lp, dp = q_ref[...], do_ref[...], l_ref[...], d_ref[...]
        for hh in range(2):
            hmask = (lane < 64) if hh == 0 else (lane >= 64)
            qh = qp[:, LANES * hh:LANES * (hh + 1)]
            doh = jnp.where(hmask, dop, jnp.zeros_like(dop))
            lse_h = lp[:, 64 * hh:64 * hh + 1]
            dl_h = dp[:, 64 * hh:64 * hh + 1]

            def step(j, dq, qh=qh, doh=doh, lse_h=lse_h, dl_h=dl_h, hh=hh):
                ks = pl.multiple_of(j * BK, BK)
                kb = k_ref[pl.ds(ks, BK), LANES * hh:LANES * (hh + 1)]
                vb = v_ref[pl.ds(ks, BK), :]
                z = _mm_nt(qh, kb) * scale
                valid = ((col + j * BK) // 64) <= rowc
                p = jnp.where(valid, jnp.exp(z - lse_h), 0.0)
                dw = _mm_nt(doh, vb)
                dz = (p * (dw - dl_h) * scale).astype(BF16)
                dq = dq + _mm(dz, kb)
                dk_ref[pl.ds(ks, BK), LANES * hh:LANES * (hh + 1)] += _mm_tn(dz, qh)
                dv_ref[pl.ds(ks, BK), :] += _mm_tn(p.astype(BF16), doh)
                return dq

            dq = lax.fori_loop(0, i + 1, step, jnp.zeros((BQ, LANES), F32))
            dq_ref[:, LANES * hh:LANES * (hh + 1)] = dq

    qspec = pl.BlockSpec((BQ, 2 * LANES), lambda p, i: (i, p))
    kspec = pl.BlockSpec((s, 2 * LANES), lambda p, i: (0, p))
    vspec = pl.BlockSpec((s, LANES), lambda p, i: (0, p))
    ospec = pl.BlockSpec((BQ, LANES), lambda p, i: (i, p))
    return pl.pallas_call(
        body, name="mla_bwd", grid=(4, s // BQ),
        out_shape=(jax.ShapeDtypeStruct((s, 1024), F32), jax.ShapeDtypeStruct((s, 1024), F32),
                   jax.ShapeDtypeStruct((s, 512), F32)),
        in_specs=[qspec, kspec, vspec, ospec, ospec, ospec], out_specs=(qspec, kspec, vspec),
        compiler_params=pltpu.CompilerParams(vmem_limit_bytes=VMEM_ATTN),
    )(qc, kc, v, do, lse, delta)


def _post(x, p, tgt, sbo, mlao, sbg, mlag, gsb, gmla, wout, wout_t, gpost, wple, gple, wpg, wpg_t, bpg):
    s = x.shape[0]

    def body(x_ref, p_ref, t_ref, sbo_ref, mlao_ref, sbg_ref, mlag_ref, gsb_ref, gmla_ref, wout_ref, woutt_ref,
             gpost_ref, wple_ref, gple_ref, wpg_ref, wpgt_ref, bpg_ref,
             dsbo_ref, dmlao_ref, delta_ref, dsbg_ref, dmlag_ref, dxres_ref, dwout_ref, dwpg_ref, dwple_ref, vec_ref):
        i = pl.program_id(0)

        @pl.when(i == 0)
        def _():
            dwout_ref[...] = jnp.zeros_like(dwout_ref)
            dwpg_ref[...] = jnp.zeros_like(dwpg_ref)
            dwple_ref[...] = jnp.zeros_like(dwple_ref)
            vec_ref[...] = jnp.zeros_like(vec_ref)

        bd = _blockdiag(512, HEAD_DIM)
        inv_hd = 1.0 / HEAD_DIM

        def head_fwd(o, g, gate):
            r = lax.rsqrt(_seg(o * o, bd) * inv_hd + EPS)
            hat = o * r
            n = hat * g
            sg = _sigmoid(gate)
            return hat, r, n, sg, n * (gate * sg)

        sbo, mlao, sbg_v, mlag_v = sbo_ref[...], mlao_ref[...], sbg_ref[...], mlag_ref[...]
        gsb_v, gmla_v = gsb_ref[...], gmla_ref[...]
        sb_hat, sb_r, sb_n, sb_sg, sb_y = head_fwd(sbo, gsb_v, sbg_v)
        ml_hat, ml_r, ml_n, ml_sg, ml_y = head_fwd(mlao, gmla_v, mlag_v)
        mix = jnp.concatenate([sb_y, ml_y], axis=1).astype(BF16)
        y = _mm(mix, wout_ref[...])
        ry = lax.rsqrt(_rowmean(y * y) + EPS)
        y_hat = y * ry
        gpost_v = gpost_ref[...]
        x1 = x_ref[...] + y_hat * gpost_v
        pb = p_ref[...].astype(BF16)
        pl_ = _mm(pb, wple_ref[...])
        rp = lax.rsqrt(_rowmean(pl_ * pl_) + EPS)
        pl_hat = pl_ * rp
        gple_v = gple_ref[...]
        ple = pl_hat * gple_v
        x1b = x1.astype(BF16)
        gate = _sigmoid(_mm(x1b, wpg_ref[...]) + bpg_ref[...])
        err = x1 + ple * gate - t_ref[...]
        loss = 0.5 * jnp.sum(_rowmean(err * err))
        dout = err * (1.0 / D_MODEL)

        du = dout * ple * gate * (1.0 - gate)
        dub = du.astype(BF16)
        dple = dout * gate
        dx1 = dout + _mm(dub, wpgt_ref[...])
        dwpg_ref[...] += _mm_tn(x1b, dub)
        dplh = dple * gple_v
        dpl = rp * (dplh - pl_hat * _rowmean(dplh * pl_hat))
        dwple_ref[...] += _mm_tn(pb, dpl.astype(BF16))
        dxres_ref[...] = dx1
        dyh = dx1 * gpost_v
        dy = ry * (dyh - y_hat * _rowmean(dyh * y_hat))
        dyb = dy.astype(BF16)
        dwout_ref[...] += _mm_tn(mix, dyb)
        dmix = _mm(dyb, woutt_ref[...])

        def head_bwd(dyv, hat, r, n, sg, g, gate):
            dn = dyv * (gate * sg)
            dgate = dyv * n * (sg * (1.0 + gate * (1.0 - sg)))
            dhat = dn * g
            do = r * (dhat - hat * (_seg(dhat * hat, bd) * inv_hd))
            return do, dgate, _colsum(dn * hat)

        dsbo, dsbg, dg_sb = head_bwd(dmix[:, 0:512], sb_hat, sb_r, sb_n, sb_sg, gsb_v, sbg_v)
        dmlao, dmlag, dg_ml = head_bwd(dmix[:, 512:1024], ml_hat, ml_r, ml_n, ml_sg, gmla_v, mlag_v)
        dsbo_ref[...] = dsbo.astype(BF16)
        dmlao_ref[...] = dmlao.astype(BF16)
        delta_ref[...] = _seg(dmlao * mlao, bd)
        dsbg_ref[...] = dsbg.astype(BF16)
        dmlag_ref[...] = dmlag.astype(BF16)
        vec_ref[pl.ds(0, 1), :] += _colsum(dx1 * y_hat)
        vec_ref[pl.ds(1, 1), :] += _colsum(dple * pl_hat)
        vec_ref[pl.ds(2, 1), :] += _colsum(du)
        vec_ref[pl.ds(3, 1), :] += jnp.concatenate([dg_sb, dg_ml], axis=1)
        vec_ref[pl.ds(4, 1), :] += jnp.full((1, D_MODEL), loss, F32)

    out_shape = (
        jax.ShapeDtypeStruct((s, 512), BF16), jax.ShapeDtypeStruct((s, 512), BF16), jax.ShapeDtypeStruct((s, 512), F32),
        jax.ShapeDtypeStruct((s, 512), BF16), jax.ShapeDtypeStruct((s, 512), BF16), jax.ShapeDtypeStruct((s, D_MODEL), F32),
        jax.ShapeDtypeStruct((D_MODEL, D_MODEL), F32), jax.ShapeDtypeStruct((D_MODEL, D_MODEL), F32),
        jax.ShapeDtypeStruct((PLE_DIM, D_MODEL), F32), jax.ShapeDtypeStruct((8, D_MODEL), F32),
    )
    return pl.pallas_call(
        body, name="post_fwd_bwd", grid=(s // TM,), out_shape=out_shape,
        in_specs=[_rows(D_MODEL), _rows(PLE_DIM), _rows(D_MODEL), _rows(512), _rows(512), _rows(512), _rows(512),
                  _full((1, 512)), _full((1, 512)), _full((D_MODEL, D_MODEL)), _full((D_MODEL, D_MODEL)),
                  _full((1, D_MODEL)), _full((PLE_DIM, D_MODEL)), _full((1, D_MODEL)), _full((D_MODEL, D_MODEL)),
                  _full((D_MODEL, D_MODEL)), _full((1, D_MODEL))],
        out_specs=(_rows(512), _rows(512), _rows(512), _rows(512), _rows(512), _rows(D_MODEL),
                   _full((D_MODEL, D_MODEL)), _full((D_MODEL, D_MODEL)), _full((PLE_DIM, D_MODEL)), _full((8, D_MODEL))),
        compiler_params=pltpu.CompilerParams(vmem_limit_bytes=VMEM_DENSE),
    )(x, p, tgt, sbo, mlao, sbg, mlag, gsb, gmla, wout, wout_t, gpost, wple, gple, wpg, wpg_t, bpg)


def _pre_bwd(x, dxres, dsbq, dsbk, dsbv, dsbg, dmlag, dqc, dkc, dmv, cq, ckv, tabs, gpre, win_t, gq, wuq_t, gkv,
             wk_t, wv_t):
    s = x.shape[0]
    c_t, sa_t, sb_t = tabs

    def body(x_ref, dxres_ref, dsbq_ref, dsbk_ref, dsbv_ref, dsbg_ref, dmlag_ref, dqc_ref, dkc_ref, dmv_ref, cq_ref,
             ckv_ref, c_ref, sa_ref, sb_ref, gpre_ref, wint_ref, gq_ref, wuqt_ref, gkv_ref, wkt_ref, wvt_ref,
             gx_ref, dwin_ref, dwuq_ref, dwk_ref, dwv_ref, vec_ref, dwin_acc):
        i = pl.program_id(0)

        @pl.when(i == 0)
        def _():
            dwin_acc[...] = jnp.zeros_like(dwin_acc)
            dwuq_ref[...] = jnp.zeros_like(dwuq_ref)
            dwk_ref[...] = jnp.zeros_like(dwk_ref)
            dwv_ref[...] = jnp.zeros_like(dwv_ref)
            vec_ref[...] = jnp.zeros_like(vec_ref)

        lane = lax.broadcasted_iota(jnp.int32, (1, LANES), 1)
        c1, sa1, sb1 = c_ref[...], sa_ref[...], sb_ref[...]
        c8, sa8, sb8 = jnp.tile(c1, (1, 8)), jnp.tile(sa1, (1, 8)), jnp.tile(sb1, (1, 8))

        def norm_bwd(dn, hat, r, g):
            t = dn * g
            return r * (t - hat * _rowmean(t * hat)), _colsum(dn * hat)

        dqeb = _rope_bwd(dqc_ref[...], c8, sa8, sb8).astype(BF16)
        cq = cq_ref[...]
        rq = lax.rsqrt(_rowmean(cq * cq) + EPS)
        cq_hat = cq * rq
        gq_v = gq_ref[...]
        dwuq_ref[...] += _mm_tn((cq_hat * gq_v).astype(BF16), dqeb)
        dcq, dg_q = norm_bwd(_mm(dqeb, wuqt_ref[...]), cq_hat, rq, gq_v)

        dkc = dkc_ref[...]
        dkcb = dkc.astype(BF16)
        dmvb = dmv_ref[...].astype(BF16)
        ckv = ckv_ref[...]
        rkv = lax.rsqrt(_rowmean(ckv * ckv) + EPS)
        ckv_hat = ckv * rkv
        gkv_v = gkv_ref[...]
        ckvnb = (ckv_hat * gkv_v).astype(BF16)
        dwk_ref[...] += _mm_tn(ckvnb, dkcb)
        dwv_ref[...] += _mm_tn(ckvnb, dmvb)
        dckv, dg_kv = norm_bwd(_mm(dkcb, wkt_ref[...]) + _mm(dmvb, wvt_ref[...]), ckv_hat, rkv, gkv_v)

        dkr = dkc[:, 0:LANES]
        for hh in range(1, 8):
            dkr = dkr + dkc[:, LANES * hh:LANES * (hh + 1)]
        dkr = _rope_bwd(dkr, c1, sa1, sb1)
        dkr = jnp.where((lane >= 64) & (lane < 96), dkr, 0.0)

        dproj = jnp.concatenate([dsbq_ref[...], dsbk_ref[...].astype(BF16), dsbv_ref[...].astype(BF16), dsbg_ref[...],
                                 dcq.astype(BF16), dckv.astype(BF16), dkr.astype(BF16), dmlag_ref[...]], axis=1)
        xv = x_ref[...]
        r1 = lax.rsqrt(_rowmean(xv * xv) + EPS)
        x_hat = xv * r1
        gpre_v = gpre_ref[...]
        dwin_acc[...] += _mm_tn((x_hat * gpre_v).astype(BF16), dproj)
        dx, dg_pre = norm_bwd(_mm(dproj, wint_ref[...]), x_hat, r1, gpre_v)
        gx_ref[...] = dxres_ref[...] + dx
        vec_ref[pl.ds(0, 1), :] += dg_pre
        vec_ref[pl.ds(1, 1), :] += jnp.concatenate([dg_q, dg_kv, jnp.zeros((1, D_MODEL - Q_LORA - KV_LORA), F32)], axis=1)

        @pl.when(i == pl.num_programs(0) - 1)
        def _():
            pltpu.sync_copy(dwin_acc, dwin_ref)

    out_shape = (
        jax.ShapeDtypeStruct((s, D_MODEL), F32), jax.ShapeDtypeStruct((D_MODEL, D_EXT), F32),
        jax.ShapeDtypeStruct((Q_LORA, 1024), F32), jax.ShapeDtypeStruct((KV_LORA, 1024), F32),
        jax.ShapeDtypeStruct((KV_LORA, 512), F32), jax.ShapeDtypeStruct((8, D_MODEL), F32),
    )
    return pl.pallas_call(
        body, name="pre_bwd", grid=(s // TM,), out_shape=out_shape,
        in_specs=[_rows(D_MODEL), _rows(D_MODEL), _rows(512), _rows(512), _rows(512), _rows(512), _rows(512),
                  _rows(1024), _rows(1024), _rows(512), _rows(Q_LORA), _rows(KV_LORA), _rows(LANES), _rows(LANES),
                  _rows(LANES), _full((1, D_MODEL)), _full((D_EXT, D_MODEL)), _full((1, Q_LORA)), _full((1024, Q_LORA)),
                  _full((1, KV_LORA)), _full((1024, KV_LORA)), _full((512, KV_LORA))],
        out_specs=(_rows(D_MODEL), pl.BlockSpec(memory_space=pl.ANY), _full((Q_LORA, 1024)), _full((KV_LORA, 1024)),
                   _full((KV_LORA, 512)), _full((8, D_MODEL))),
        scratch_shapes=[pltpu.VMEM((D_MODEL, D_EXT), F32)],
        compiler_params=pltpu.CompilerParams(vmem_limit_bytes=VMEM_DENSE),
    )(x, dxres, dsbq, dsbk, dsbv, dsbg, dmlag, dqc, dkc, dmv, cq, ckv, c_t, sa_t, sb_t, gpre, win_t, gq, wuq_t, gkv,
      wk_t, wv_t)


def _place():
    return lax.axis_index("x"), lax.axis_index("y"), lax.axis_index("c")


def _allgather_weights(pack):
    def body(p_ref, out_ref, send_sems, recv_sems):
        x, y, c = _place()
        me, sib = (x, y, c), (x, y, 1 - c)
        chips = [(1 - x, y), (x, 1 - y), (1 - x, 1 - y)]

        def half(chip, hc):
            start = pl.multiple_of(hc * HALF_ROWS, 8)
            return out_ref.at[2 * chip[0] + chip[1], pl.ds(start, HALF_ROWS), :]

        def copy(k, chip, hc, to):
            return pltpu.make_async_remote_copy(src_ref=half(chip, hc), dst_ref=half(chip, hc), send_sem=send_sems.at[k],
                                                recv_sem=recv_sems.at[k], device_id=to, device_id_type=MESH)

        out_ref[2 * x + y] = p_ref[...].astype(BF16)
        first = [copy(j, (x, y), c, (*chip, c)) for j, chip in enumerate(chips)]
        for cp in first:
            cp.start()
        passed = [copy(3 + j, chip, c, sib) for j, chip in enumerate(chips)]
        for j, chip in enumerate(chips):
            copy(j, chip, c, me).wait_recv()
            passed[j].start()
        for j, chip in enumerate(chips):
            copy(3 + j, chip, 1 - c, me).wait_recv()
        for cp in first + passed:
            cp.wait_send()

    return pl.pallas_call(
        body, name="allgather_weights",
        out_shape=jax.ShapeDtypeStruct((N_SHARD, PACK_ROWS, 1024), BF16),
        in_specs=[pl.BlockSpec(memory_space=pltpu.VMEM)], out_specs=pl.BlockSpec(memory_space=pltpu.VMEM),
        scratch_shapes=[pltpu.SemaphoreType.DMA((6,)), pltpu.SemaphoreType.DMA((6,))],
        compiler_params=pltpu.CompilerParams(vmem_limit_bytes=VMEM_ATTN),
    )(pack)


def _reduce_scatter_grads(gpack, vec):
    def body(g_ref, vec_ref, f_ref, vsum_ref, acc, sib_half, send_buf, recv_buf, vrecv, local_sem, send_sems, recv_sems):
        x, y, c = _place()
        me, sib = (x, y, c), (x, y, 1 - c)
        mine = 2 * x + y
        chips = [(1 - x, y), (x, 1 - y), (1 - x, 1 - y)]

        def remote(k, src, dst, to):
            return pltpu.make_async_remote_copy(src_ref=src, dst_ref=dst, send_sem=send_sems.at[k], recv_sem=recv_sems.at[k],
                                                device_id=to, device_id_type=MESH)

        load = pltpu.make_async_copy(g_ref.at[c], acc, local_sem)
        load.start()
        to_sib = remote(0, g_ref.at[1 - c], sib_half, sib)
        to_sib.start()

        my_dev = 4 * x + 2 * y + c
        vrecv[my_dev] = vec_ref[...]
        vec_sends = []
        for k in range(1, 8):
            to = (x ^ ((k >> 2) & 1), y ^ ((k >> 1) & 1), c ^ (k & 1))
            cp = remote(k, vec_ref, vrecv.at[my_dev], to)
            cp.start()
            vec_sends.append(cp)

        load.wait()
        remote(0, g_ref.at[1 - c], sib_half, me).wait_recv()
        for k in range(N_SHARD):
            acc[k] = acc[k] + sib_half[k]

        sends = []
        for j, chip in enumerate(chips):
            idx = 2 * chip[0] + chip[1]
            send_buf[idx] = acc[idx].astype(BF16)
            cp = remote(8 + j, send_buf.at[idx], recv_buf.at[mine], (*chip, c))
            cp.start()
            sends.append(cp)
        total = acc[mine]
        for j, chip in enumerate(chips):
            idx = 2 * chip[0] + chip[1]
            remote(8 + j, send_buf.at[idx], recv_buf.at[idx], me).wait_recv()
            total = total + recv_buf[idx].astype(F32)
        start = pl.multiple_of(c * HALF_ROWS, 8)
        f_ref[pl.ds(start, HALF_ROWS), :] = total

        other = pl.multiple_of((1 - c) * HALF_ROWS, 8)
        swap = remote(11, f_ref.at[pl.ds(start, HALF_ROWS), :], f_ref.at[pl.ds(start, HALF_ROWS), :], sib)
        swap.start()
        remote(11, f_ref.at[pl.ds(other, HALF_ROWS), :], f_ref.at[pl.ds(other, HALF_ROWS), :], me).wait_recv()

        for k in range(1, 8):
            src_dev = 4 * (x ^ ((k >> 2) & 1)) + 2 * (y ^ ((k >> 1) & 1)) + (c ^ (k & 1))
            remote(k, vec_ref, vrecv.at[src_dev], me).wait_recv()
        vs = vrecv[0]
        for d in range(1, 8):
            vs = vs + vrecv[d]
        vsum_ref[...] = vs

        for cp in [to_sib, swap] + vec_sends + sends:
            cp.wait_send()

    return pl.pallas_call(
        body, name="reduce_scatter_grads",
        out_shape=(jax.ShapeDtypeStruct((PACK_ROWS, 1024), F32), jax.ShapeDtypeStruct((VEC_ROWS, 1024), F32)),
        in_specs=[pl.BlockSpec(memory_space=pl.ANY), pl.BlockSpec(memory_space=pltpu.VMEM)],
        out_specs=(pl.BlockSpec(memory_space=pltpu.VMEM), pl.BlockSpec(memory_space=pltpu.VMEM)),
        scratch_shapes=[
            pltpu.VMEM((N_SHARD, HALF_ROWS, 1024), F32), pltpu.VMEM((N_SHARD, HALF_ROWS, 1024), F32),
            pltpu.VMEM((N_SHARD, HALF_ROWS, 1024), BF16), pltpu.VMEM((N_SHARD, HALF_ROWS, 1024), BF16),
            pltpu.VMEM((8, VEC_ROWS, 1024), F32),
            pltpu.SemaphoreType.DMA, pltpu.SemaphoreType.DMA((12,)), pltpu.SemaphoreType.DMA((12,)),
        ],
        compiler_params=pltpu.CompilerParams(vmem_limit_bytes=56 * 1024 * 1024),
    )(gpack, vec)


def _adamw(w, g, m, v):
    rows, cols = w.shape
    tr = rows if rows <= 256 else 256

    def body(w_ref, g_ref, m_ref, v_ref, d_ref, nm_ref, nv_ref):
        gv = g_ref[...]
        m2 = ADAM_B1 * m_ref[...] + (1.0 - ADAM_B1) * gv
        v2 = ADAM_B2 * v_ref[...] + (1.0 - ADAM_B2) * (gv * gv)
        m_hat = m2 / (1.0 - ADAM_B1 ** ADAM_STEP)
        v_hat = v2 / (1.0 - ADAM_B2 ** ADAM_STEP)
        d_ref[...] = -ADAM_LR * (m_hat / (jnp.sqrt(v_hat) + ADAM_EPS) + ADAM_WD * w_ref[...])
        nm_ref[...] = m2
        nv_ref[...] = v2

    spec = pl.BlockSpec((tr, cols), lambda i: (i, 0))
    shp = jax.ShapeDtypeStruct((rows, cols), F32)
    return pl.pallas_call(body, name="adamw", grid=(rows // tr,), out_shape=(shp, shp, shp),
                          in_specs=[spec] * 4, out_specs=(spec,) * 3)(w, g, m, v)


_PACK_LAYOUT = (("w_in", 1024, 744), ("w_uq", 256, 192), ("w_ukv", 128, 256), ("w_out", 256, 1024), ("w_ple", 256, 256),
                ("w_ple_gate", 256, 1024))


def _pack_shard(parts):
    rows = [parts[n].reshape(-1, 1024) for n, _, _ in _PACK_LAYOUT]
    used = sum(r.shape[0] for r in rows)
    return jnp.concatenate(rows + [jnp.zeros((PACK_ROWS - used, 1024), rows[0].dtype)], axis=0)


def _unpack_shard(pack):
    out, off = {}, 0
    for n, r, cdim in _PACK_LAYOUT:
        nr = r * cdim // 1024
        out[n] = pack[..., off:off + nr, :].reshape(pack.shape[:-2] + (r, cdim))
        off += nr
    return out


def _join_shards(parts):
    cat_cols = lambda a: jnp.concatenate([a[k] for k in range(N_SHARD)], axis=1)
    cat_rows = lambda a: jnp.concatenate([a[k] for k in range(N_SHARD)], axis=0)
    return {"w_in": cat_cols(parts["w_in"]), "w_uq": cat_cols(parts["w_uq"]), "w_ukv": cat_cols(parts["w_ukv"]),
            "w_out": cat_rows(parts["w_out"]), "w_ple": cat_cols(parts["w_ple"]), "w_ple_gate": cat_rows(parts["w_ple_gate"])}


def _split_shards(full):
    cols = lambda a: jnp.stack(jnp.split(a, N_SHARD, axis=1))
    rows = lambda a: jnp.stack(jnp.split(a, N_SHARD, axis=0))
    return {"w_in": cols(full["w_in"]), "w_uq": cols(full["w_uq"]), "w_ukv": cols(full["w_ukv"]),
            "w_out": rows(full["w_out"]), "w_ple": cols(full["w_ple"]), "w_ple_gate": rows(full["w_ple_gate"])}


def _extend_weights(w):
    win = w["w_in"]
    zeros = lambda r, c: jnp.zeros((r, c), win.dtype)
    win_ext = jnp.concatenate([win[:, :2432], zeros(D_MODEL, 64), win[:, 2432:2464], zeros(D_MODEL, 32), win[:, 2464:]], axis=1)
    wuq_ext = jnp.pad(w["w_uq"].reshape(Q_LORA, 8, 96), ((0, 0), (0, 0), (0, 32))).reshape(Q_LORA, 1024)
    wukv = w["w_ukv"].reshape(KV_LORA, 8, 128)
    wk_ext = jnp.pad(wukv[:, :, :64], ((0, 0), (0, 0), (0, 64))).reshape(KV_LORA, 1024)
    wv = wukv[:, :, 64:].reshape(KV_LORA, 512)
    return win_ext, wuq_ext, wk_ext, wv


def _contract_grads(dwin_ext, dwuq_ext, dwk_ext, dwv):
    dwin = jnp.concatenate([dwin_ext[:, :2432], dwin_ext[:, 2496:2528], dwin_ext[:, 2560:]], axis=1)
    dwuq = dwuq_ext.reshape(Q_LORA, 8, 128)[:, :, :96].reshape(Q_LORA, 768)
    dwukv = jnp.concatenate([dwk_ext.reshape(KV_LORA, 8, 128)[:, :, :64], dwv.reshape(KV_LORA, 8, 64)], axis=2)
    return dwin, dwuq, dwukv.reshape(KV_LORA, 1024)


def _rope_tables(positions):
    half = QK_ROPE // 2
    freq = ROPE_THETA ** (-jnp.arange(half, dtype=F32) / half)
    ang = positions.astype(F32)[:, None] * freq
    cos, sin = jnp.cos(ang), jnp.sin(ang)
    s = positions.shape[0]
    z = lambda n: jnp.zeros((s, n), F32)
    c_t = jnp.concatenate([jnp.ones((s, 64), F32), cos, cos, z(32)], axis=1)
    sa_t = jnp.concatenate([z(64), -sin, z(16), z(32)], axis=1)
    sb_t = jnp.concatenate([z(64), z(16), sin, z(32)], axis=1)
    return c_t, sa_t, sb_t


def _local_grads(x, p, positions, tgt, gains, wfull):
    win_ext, wuq_ext, wk_ext, wv = _extend_weights(wfull)
    wout, wple, wpg = wfull["w_out"], wfull["w_ple"], wfull["w_ple_gate"]
    tabs = _rope_tables(positions)
    g = gains
    sbq, sbk, sbv, sbg, mlag, cq, ckv, qc, kc, mv = _pre_fwd(x, tabs, g["norm_pre_g"], win_ext, g["q_norm_g"], wuq_ext,
                                                             g["kv_norm_g"], wk_ext, wv)
    sbo, sbt = _sb_fwd(sbq, sbk, sbv)
    mlao, lse = _mla_fwd(qc, kc, mv)
    dsbo, dmlao, delta, dsbg, dmlag, dxres, dwout, dwpg, dwple, vec_c = _post(
        x, p, tgt, sbo, mlao, sbg, mlag, g["sb_out_norm_g"], g["mla_out_norm_g"], wout, wout.T, g["norm_post_g"], wple,
        g["ple_norm_g"], wpg, wpg.T, g["b_ple_gate"])
    dsbq, dsbk, dsbv = _sb_bwd(sbq, sbk, sbv, dsbo, sbt)
    dqc, dkc, dmv = _mla_bwd(qc, kc, mv, dmlao, lse, delta)
    gx, dwin_ext, dwuq_ext, dwk_ext, dwv, vec_d = _pre_bwd(
        x, dxres, dsbq, dsbk, dsbv, dsbg, dmlag, dqc, dkc, dmv, cq, ckv, tabs, g["norm_pre_g"], win_ext.T, g["q_norm_g"],
        wuq_ext.T, g["kv_norm_g"], wk_ext.T, wv.T)
    dwin, dwuq, dwukv = _contract_grads(dwin_ext, dwuq_ext, dwk_ext, dwv)
    grads = {"w_in": dwin, "w_uq": dwuq, "w_ukv": dwukv, "w_out": dwout, "w_ple": dwple, "w_ple_gate": dwpg}
    return gx, grads, jnp.concatenate([vec_c, vec_d], axis=0)


_VEC_LAYOUT = (("norm_post_g", 0, 0, 1024), ("ple_norm_g", 1, 0, 1024), ("b_ple_gate", 2, 0, 1024), ("sb_out_norm_g", 3, 0, 512),
               ("mla_out_norm_g", 3, 512, 512), ("norm_pre_g", 8, 0, 1024), ("q_norm_g", 9, 0, 256), ("kv_norm_g", 9, 256, 128))
_LOSS_ROW = 4
_WEIGHT_ORDER = ("norm_pre_g", "w_in", "q_norm_g", "w_uq", "kv_norm_g", "w_ukv", "sb_out_norm_g", "mla_out_norm_g", "w_out",
                 "norm_post_g", "w_ple", "ple_norm_g", "w_ple_gate", "b_ple_gate")


def _vec_block(named):
    blk = jnp.zeros((VEC_ROWS, 1024), F32)
    for n, r, c0, width in _VEC_LAYOUT:
        blk = blk.at[r, c0:c0 + width].set(named[n][0])
    return blk


def kernel(x, p, positions, norm_pre_g, w_in, q_norm_g, w_uq, kv_norm_g, w_ukv, sb_out_norm_g, mla_out_norm_g, w_out, norm_post_g, w_ple, ple_norm_g, w_ple_gate, b_ple_gate, loss_target, m_norm_pre_g, m_w_in, m_q_norm_g, m_w_uq, m_kv_norm_g, m_w_ukv, m_sb_out_norm_g, m_mla_out_norm_g, m_w_out, m_norm_post_g, m_w_ple, m_ple_norm_g, m_w_ple_gate, m_b_ple_gate, v_norm_pre_g, v_w_in, v_q_norm_g, v_w_uq, v_kv_norm_g, v_w_ukv, v_sb_out_norm_g, v_mla_out_norm_g, v_w_out, v_norm_post_g, v_w_ple, v_ple_norm_g, v_w_ple_gate, v_b_ple_gate):
    w = {"norm_pre_g": norm_pre_g, "w_in": w_in[0], "q_norm_g": q_norm_g, "w_uq": w_uq[0], "kv_norm_g": kv_norm_g, "w_ukv": w_ukv[0],
         "sb_out_norm_g": sb_out_norm_g, "mla_out_norm_g": mla_out_norm_g, "w_out": w_out[0], "norm_post_g": norm_post_g,
         "w_ple": w_ple[0], "ple_norm_g": ple_norm_g, "w_ple_gate": w_ple_gate[0], "b_ple_gate": b_ple_gate}
    m = {"norm_pre_g": m_norm_pre_g, "w_in": m_w_in[0], "q_norm_g": m_q_norm_g, "w_uq": m_w_uq[0], "kv_norm_g": m_kv_norm_g,
         "w_ukv": m_w_ukv[0], "sb_out_norm_g": m_sb_out_norm_g, "mla_out_norm_g": m_mla_out_norm_g, "w_out": m_w_out[0],
         "norm_post_g": m_norm_post_g, "w_ple": m_w_ple[0], "ple_norm_g": m_ple_norm_g, "w_ple_gate": m_w_ple_gate[0],
         "b_ple_gate": m_b_ple_gate}
    v = {"norm_pre_g": v_norm_pre_g, "w_in": v_w_in[0], "q_norm_g": v_q_norm_g, "w_uq": v_w_uq[0], "kv_norm_g": v_kv_norm_g,
         "w_ukv": v_w_ukv[0], "sb_out_norm_g": v_sb_out_norm_g, "mla_out_norm_g": v_mla_out_norm_g, "w_out": v_w_out[0],
         "norm_post_g": v_norm_post_g, "w_ple": v_w_ple[0], "ple_norm_g": v_ple_norm_g, "w_ple_gate": v_w_ple_gate[0],
         "b_ple_gate": v_b_ple_gate}
    big = [n for n, _, _ in _PACK_LAYOUT]

    gathered = _allgather_weights(_pack_shard({n: w[n] for n in big}))
    wfull = _join_shards(_unpack_shard(gathered))

    gx, grads, vec = _local_grads(x[0], p[0, 0], positions[0], loss_target[0], w, wfull)

    gsh = _split_shards(grads)
    gpack = jnp.stack([_pack_shard({n: gsh[n][k] for n in big}) for k in range(N_SHARD)])
    gpack = gpack.reshape(N_SHARD, 2, HALF_ROWS, 1024).transpose(1, 0, 2, 3)
    gred, vsum = _reduce_scatter_grads(gpack, vec)
    g = _unpack_shard(gred)
    for n, r, c0, width in _VEC_LAYOUT:
        g[n] = vsum[r:r + 1, c0:c0 + width]
    loss = vsum[_LOSS_ROW, 0]

    delta, new_m, new_v = {}, {}, {}
    for n in big:
        delta[n], new_m[n], new_v[n] = _adamw(w[n], g[n], m[n], v[n])
    small = [n for n in _WEIGHT_ORDER if n not in big]
    dv, mv_, vv = _adamw(_vec_block({n: w[n] for n in small}), vsum * _vec_block({n: jnp.ones_like(w[n]) for n in small}),
                         _vec_block({n: m[n] for n in small}), _vec_block({n: v[n] for n in small}))
    for n, r, c0, width in _VEC_LAYOUT:
        delta[n], new_m[n], new_v[n] = dv[r:r + 1, c0:c0 + width], mv_[r:r + 1, c0:c0 + width], vv[r:r + 1, c0:c0 + width]

    lead = lambda n, a: a[None] if n in big else a
    return (loss, gx[None],
            *[lead(n, g[n]) for n in _WEIGHT_ORDER], *[lead(n, delta[n]) for n in _WEIGHT_ORDER],
            *[lead(n, new_m[n]) for n in _WEIGHT_ORDER], *[lead(n, new_v[n]) for n in _WEIGHT_ORDER])
```

```python
import functools

import jax
import jax.numpy as jnp
from jax import lax
from jax.experimental import pallas as pl
from jax.experimental.pallas import tpu as pltpu

F32 = jnp.float32
BF16 = jnp.bfloat16
MESH = pl.DeviceIdType.MESH

D_MODEL = 1024
HEAD_DIM = 64
D_SB = 512
D_MLA = 512
Q_LORA = 256
KV_LORA = 128
QK_NOPE = 64
QK_ROPE = 32
PLE_DIM = 256
D_IN = 2976
D_EXT = 3072
ROPE_THETA = 10000.0
EPS = 1e-6
N_SHARD = 4

ADAM_LR = 0.001
ADAM_B1 = 0.9
ADAM_B2 = 0.999
ADAM_EPS = 1e-08
ADAM_WD = 0.01
ADAM_STEP = 10

LANES = 128
BK = 128
WQ = 256
TM = 256
PACK_ROWS = 1408
HALF_ROWS = PACK_ROWS // 2
VEC_ROWS = 16
VMEM_DENSE = 52 * 1024 * 1024
VMEM_ATTN = 40 * 1024 * 1024


def _mm(a, b):
    return jnp.dot(a, b, preferred_element_type=F32)


def _mm_nt(a, b):
    return lax.dot_general(a, b, (((1,), (1,)), ((), ())), preferred_element_type=F32)


def _mm_tn(a, b):
    return lax.dot_general(a, b, (((0,), (0,)), ((), ())), preferred_element_type=F32)


def _seg(a, bd):
    a1 = a.astype(BF16)
    r1 = a - a1.astype(F32)
    a2 = r1.astype(BF16)
    a3 = (r1 - a2.astype(F32)).astype(BF16)
    return _mm(a1, bd) + _mm(a2, bd) + _mm(a3, bd)


def _blockdiag(n, seg):
    r = lax.broadcasted_iota(jnp.int32, (n, n), 0) // seg
    c = lax.broadcasted_iota(jnp.int32, (n, n), 1) // seg
    return jnp.where(r == c, 1.0, 0.0).astype(BF16)


def _sigmoid(a):
    return 1.0 / (1.0 + jnp.exp(-a))


def _rowmean(a):
    return jnp.mean(a, axis=-1, keepdims=True)


def _colsum(a):
    return jnp.sum(a, axis=0, keepdims=True)


def _rope_fwd(a, c, sa, sb):
    w = a.shape[-1]
    return a * c + pltpu.roll(a, w - 16, 1) * sa + pltpu.roll(a, 16, 1) * sb


def _rope_bwd(g, c, sa, sb):
    w = g.shape[-1]
    return g * c + pltpu.roll(g * sa, 16, 1) + pltpu.roll(g * sb, w - 16, 1)


def _full(shape):
    return pl.BlockSpec(shape, lambda *_: (0,) * len(shape))


def _rows(width, tm=TM):
    return pl.BlockSpec((tm, width), lambda i: (i, 0))


def _pre_fwd(x, tabs, gpre, win, gq, wuq, gkv, wk, wv):
    s = x.shape[0]
    c_t, sa_t, sb_t = tabs

    def body(x_ref, c_ref, sa_ref, sb_ref, gpre_ref, win_ref, gq_ref, wuq_ref, gkv_ref, wk_ref, wv_ref,
             sbq_ref, sbk_ref, sbv_ref, sbg_ref, mlag_ref, cq_ref, ckv_ref, qc_ref, kc_ref, mv_ref):
        xv = x_ref[...]
        r1 = lax.rsqrt(_rowmean(xv * xv) + EPS)
        h = (xv * r1 * gpre_ref[...]).astype(BF16)
        proj = _mm(h, win_ref[...])
        sbq_ref[...] = proj[:, 0:512].astype(BF16)
        sbk_ref[...] = proj[:, 512:1024].astype(BF16)
        sbv_ref[...] = proj[:, 1024:1536].astype(BF16)
        sbg_ref[...] = proj[:, 1536:2048]
        cq = proj[:, 2048:2304]
        ckv = proj[:, 2304:2432]
        kr = proj[:, 2432:2560]
        mlag_ref[...] = proj[:, 2560:3072]
        cq_ref[...] = cq
        ckv_ref[...] = ckv
        c1, sa1, sb1 = c_ref[...], sa_ref[...], sb_ref[...]
        c8, sa8, sb8 = jnp.tile(c1, (1, 8)), jnp.tile(sa1, (1, 8)), jnp.tile(sb1, (1, 8))
        cqn = (cq * lax.rsqrt(_rowmean(cq * cq) + EPS) * gq_ref[...]).astype(BF16)
        qe = _mm(cqn, wuq_ref[...])
        qc_ref[...] = _rope_fwd(qe, c8, sa8, sb8).astype(BF16)
        ckvn = (ckv * lax.rsqrt(_rowmean(ckv * ckv) + EPS) * gkv_ref[...]).astype(BF16)
        ke = _mm(ckvn, wk_ref[...])
        krr = _rope_fwd(kr, c1, sa1, sb1)
        kc_ref[...] = (ke + jnp.tile(krr, (1, 8))).astype(BF16)
        mv_ref[...] = _mm(ckvn, wv_ref[...]).astype(BF16)

    out_shape = (
        jax.ShapeDtypeStruct((s, 512), BF16), jax.ShapeDtypeStruct((s, 512), BF16), jax.ShapeDtypeStruct((s, 512), BF16),
        jax.ShapeDtypeStruct((s, 512), F32), jax.ShapeDtypeStruct((s, 512), F32),
        jax.ShapeDtypeStruct((s, Q_LORA), F32), jax.ShapeDtypeStruct((s, KV_LORA), F32),
        jax.ShapeDtypeStruct((s, 1024), BF16), jax.ShapeDtypeStruct((s, 1024), BF16), jax.ShapeDtypeStruct((s, 512), BF16),
    )
    return pl.pallas_call(
        body, name="pre_fwd", grid=(s // TM,), out_shape=out_shape,
        in_specs=[_rows(D_MODEL), _rows(LANES), _rows(LANES), _rows(LANES), _full((1, D_MODEL)), _full((D_MODEL, D_EXT)),
                  _full((1, Q_LORA)), _full((Q_LORA, 1024)), _full((1, KV_LORA)), _full((KV_LORA, 1024)), _full((KV_LORA, 512))],
        out_specs=(_rows(512), _rows(512), _rows(512), _rows(512), _rows(512), _rows(Q_LORA), _rows(KV_LORA),
                   _rows(1024), _rows(1024), _rows(512)),
        compiler_params=pltpu.CompilerParams(vmem_limit_bytes=VMEM_DENSE),
    )(x, c_t, sa_t, sb_t, gpre, win, gq, wuq, gkv, wk, wv)


def _softplus(z):
    neg_abs = lax.bitcast_convert_type(lax.bitcast_convert_type(z, jnp.uint32) | jnp.uint32(0x80000000), F32)
    return jnp.maximum(z, 0.0) + jnp.log(1.0 + jnp.exp(neg_abs))


def _pair_tri(kind):
    r = lax.broadcasted_iota(jnp.int32, (512, 256), 0)
    c = lax.broadcasted_iota(jnp.int32, (512, 256), 1)
    same = ((r // BK) % 2) == (c // BK)
    rk, ck = r % BK, c % BK
    m = {"ge": rk >= ck, "lt": rk < ck, "le": rk <= ck}[kind]
    return jnp.where(same & m, 1.0, 0.0).astype(BF16)


def _split2(a):
    hi = a.astype(BF16)
    lo = (a - hi.astype(F32)).astype(BF16)
    return jnp.concatenate([hi, lo], axis=1)


def _pair_stack(b, lane):
    zero = jnp.zeros_like(b)
    return jnp.concatenate([jnp.where(lane < 64, b, zero), jnp.where(lane >= 64, b, zero)], axis=0)


def _sb_fwd(q, k, v):
    s = q.shape[0]

    def body(q_ref, k_ref, v_ref, o_ref, t_ref):
        i = pl.program_id(1)
        lane = lax.broadcasted_iota(jnp.int32, (1, LANES), 1)
        row = lax.broadcasted_iota(jnp.int32, (WQ, 2 * BK), 0) + i * WQ
        col = lax.broadcasted_iota(jnp.int32, (WQ, 2 * BK), 1) % BK
        u_ge = _pair_tri("ge")
        qs = q_ref[...] * (HEAD_DIM ** -0.5)

        def tile(j, masked):
            ks = pl.multiple_of(j * BK, BK)
            kbd = _pair_stack(k_ref[pl.ds(ks, BK), :], lane)
            vbd = _pair_stack(v_ref[pl.ds(ks, BK), :], lane)
            z = _mm_nt(qs, kbd)
            sp = _softplus(z)
            if masked:
                valid = (col + j * BK) < row
                sp = jnp.where(valid, sp, 0.0)
            cum = _mm(_split2(sp), u_ge)
            w = jnp.exp(z - cum)
            if masked:
                w = jnp.where(valid, w, 0.0)
            rs = jnp.where(lane < 64, cum[:, 0:1], cum[:, BK:BK + 1])
            return _mm(w.astype(BF16), vbd), rs

        def fold(carry, part):
            acc, run = carry
            return acc + jnp.exp(-run) * part[0], run + part[1]

        carry = (jnp.zeros((WQ, LANES), F32), jnp.zeros((WQ, LANES), F32))
        nsub = WQ // BK
        for d in range(nsub):
            carry = fold(carry, tile(nsub * i + (nsub - 1 - d), True))

        def pair(jj, carry):
            j = nsub * i - 1 - 2 * jj
            first, second = tile(j, False), tile(j - 1, False)
            return fold(fold(carry, first), second)

        acc, run = lax.fori_loop(0, i, pair, carry)
        o_ref[...] = acc
        t_ref[...] = -run

    qspec = pl.BlockSpec((WQ, LANES), lambda p, i: (i, p))
    kspec = pl.BlockSpec((s, LANES), lambda p, i: (0, p))
    return pl.pallas_call(
        body, name="sb_fwd", grid=(4, s // WQ),
        out_shape=(jax.ShapeDtypeStruct((s, 512), F32), jax.ShapeDtypeStruct((s, 512), F32)),
        in_specs=[qspec, kspec, kspec], out_specs=(qspec, qspec),
        compiler_params=pltpu.CompilerParams(vmem_limit_bytes=VMEM_ATTN),
    )(q, k, v)


def _pair_tri1(kind):
    r = lax.broadcasted_iota(jnp.int32, (2 * BK, 2 * BK), 0)
    c = lax.broadcasted_iota(jnp.int32, (2 * BK, 2 * BK), 1)
    rk, ck = r % BK, c % BK
    m = {"ge": rk >= ck, "lt": rk < ck, "le": rk <= ck}[kind]
    return jnp.where(((r // BK) == (c // BK)) & m, 1.0, 0.0).astype(BF16)


def _heads_wide(a):
    m = a.shape[0]
    return jnp.concatenate([jnp.broadcast_to(a[:, 0:1], (m, BK)), jnp.broadcast_to(a[:, 64:65], (m, BK))], axis=1)


def _heads_narrow(a, col, lane):
    return jnp.where(lane < 64, a[:, col:col + 1], a[:, BK + col:BK + col + 1])


def _sb_bwd(q, k, v, do, tstat):
    s = q.shape[0]
    assert WQ == 2 * BK

    def body(q_ref, k_ref, v_ref, do_ref, t_ref, dq_ref, dk_ref, dv_ref):
        i = pl.program_id(1)

        @pl.when(i == 0)
        def _():
            dk_ref[...] = jnp.zeros_like(dk_ref)
            dv_ref[...] = jnp.zeros_like(dv_ref)

        lane = lax.broadcasted_iota(jnp.int32, (1, LANES), 1)
        row = lax.broadcasted_iota(jnp.int32, (WQ, 2 * BK), 0) + i * WQ
        col = lax.broadcasted_iota(jnp.int32, (WQ, 2 * BK), 1) % BK
        u_ge = _pair_tri("ge")
        u_le = _pair_tri1("le")
        qs = q_ref[...] * (HEAD_DIM ** -0.5)
        dof = do_ref[...].astype(F32)
        tot = -t_ref[...]

        def tile(j, carry, masked):
            dq, prun, erun = carry
            ks = pl.multiple_of(j * BK, BK)
            kbd = _pair_stack(k_ref[pl.ds(ks, BK), :], lane)
            vbd = _pair_stack(v_ref[pl.ds(ks, BK), :], lane)
            z = _mm_nt(qs, kbd)
            neg_abs = lax.bitcast_convert_type(lax.bitcast_convert_type(z, jnp.uint32) | jnp.uint32(0x80000000), F32)
            u = jnp.exp(neg_abs)
            opu = 1.0 + u
            sp = jnp.maximum(z, 0.0) + jnp.log(opu)
            sig = jnp.where(z >= 0.0, 1.0, u) / opu
            if masked:
                valid = (col + j * BK) < row
                sp = jnp.where(valid, sp, 0.0)
            cum = _mm(_split2(sp), u_ge)
            wl = jnp.exp(z - cum)
            if masked:
                wl = jnp.where(valid, wl, 0.0)
            prun = prun + _heads_narrow(cum, 0, lane)
            dfo = (jnp.exp(prun - tot) * dof).astype(BF16)
            e = _mm_nt(dfo, vbd) * wl
            ecum = _mm(e.astype(BF16), u_le) + _heads_wide(erun)
            dz = e - sig * ecum
            if masked:
                dz = jnp.where(valid, dz, 0.0)
            dzb = dz.astype(BF16)
            dq = dq + _mm(dzb, kbd)
            rk = _mm_tn(dzb, qs)
            dk_ref[pl.ds(ks, BK), :] += jnp.where(lane < 64, rk[0:BK], rk[BK:2 * BK])
            rv = _mm_tn(wl.astype(BF16), dfo)
            dv_ref[pl.ds(ks, BK), :] += jnp.where(lane < 64, rv[0:BK], rv[BK:2 * BK])
            return dq, prun, _heads_narrow(ecum, BK - 1, lane)

        zero = jnp.zeros((WQ, LANES), F32)
        carry = lax.fori_loop(0, i, lambda jj, c: tile(2 * jj + 1, tile(2 * jj, c, False), False), (zero, zero, zero))
        carry = tile(2 * i, carry, True)
        dq, _, _ = tile(2 * i + 1, carry, True)
        dq_ref[...] = (dq * (HEAD_DIM ** -0.5)).astype(BF16)

    qspec = pl.BlockSpec((WQ, LANES), lambda p, i: (i, p))
    kspec = pl.BlockSpec((s, LANES), lambda p, i: (0, p))
    return pl.pallas_call(
        body, name="sb_bwd", grid=(4, s // WQ),
        out_shape=(jax.ShapeDtypeStruct((s, 512), BF16), jax.ShapeDtypeStruct((s, 512), F32),
                   jax.ShapeDtypeStruct((s, 512), F32)),
        in_specs=[qspec, kspec, kspec, qspec, qspec], out_specs=(qspec, kspec, kspec),
        compiler_params=pltpu.CompilerParams(vmem_limit_bytes=VMEM_ATTN),
    )(q, k, v, do, tstat)


MLA_SCALE = (QK_NOPE + QK_ROPE) ** -0.5
LOG2E = 1.4426950408889634


def _mla_keys(kb):
    zero = jnp.zeros((BK, LANES), kb.dtype)
    return jnp.concatenate([jnp.concatenate([kb[:, 0:LANES], zero], axis=1),
                            jnp.concatenate([zero, kb[:, LANES:2 * LANES]], axis=1)], axis=0)


def _mla_fwd(qc, kc, v):
    s = qc.shape[0]

    def body(q_ref, k_ref, v_ref, o_ref, l_ref):
        i = pl.program_id(1)
        lane = lax.broadcasted_iota(jnp.int32, (1, LANES), 1)
        rowc = (lax.broadcasted_iota(jnp.int32, (WQ, BK), 0) + i * WQ) // 64
        col = lax.broadcasted_iota(jnp.int32, (WQ, BK), 1)
        qw = q_ref[...]
        ind0 = jnp.broadcast_to(jnp.where(lane < 64, 1.0, 0.0).astype(BF16), (BK, LANES))
        ind1 = jnp.broadcast_to(jnp.where(lane >= 64, 1.0, 0.0).astype(BF16), (BK, LANES))

        def tile(j, carry, masked):
            m0, m1, l, acc = carry
            ks = pl.multiple_of(j * BK, BK)
            kbd = _mla_keys(k_ref[pl.ds(ks, BK), :])
            vb = v_ref[pl.ds(ks, BK), :]
            zero = jnp.zeros_like(vb)
            vext = jnp.concatenate([jnp.concatenate([jnp.where(lane < 64, vb, zero), ind0], axis=1),
                                    jnp.concatenate([jnp.where(lane >= 64, vb, zero), ind1], axis=1)], axis=0)
            z = _mm_nt(qw, kbd) * (MLA_SCALE * LOG2E)
            z0, z1 = z[:, 0:BK], z[:, BK:2 * BK]
            if masked:
                valid = ((col + j * BK) // 64) <= rowc
                z0 = jnp.where(valid, z0, -1e30)
                z1 = jnp.where(valid, z1, -1e30)
            n0 = jnp.maximum(m0, jnp.max(z0, axis=1, keepdims=True))
            n1 = jnp.maximum(m1, jnp.max(z1, axis=1, keepdims=True))
            p = jnp.concatenate([jnp.exp2(z0 - n0), jnp.exp2(z1 - n1)], axis=1).astype(BF16)
            a = jnp.where(lane < 64, jnp.exp2(m0 - n0), jnp.exp2(m1 - n1))
            pv = _mm(p, vext)
            return n0, n1, a * l + pv[:, LANES:2 * LANES], a * acc + pv[:, 0:LANES]

        neg = jnp.full((WQ, 1), -1e30, F32)
        zero = jnp.zeros((WQ, LANES), F32)
        carry = lax.fori_loop(0, i, lambda jj, c: tile(2 * jj + 1, tile(2 * jj, c, False), False), (neg, neg, zero, zero))
        carry = tile(2 * i, carry, True)
        m0, m1, l, acc = tile(2 * i + 1, carry, True)
        o_ref[...] = acc / l
        l_ref[...] = jnp.where(lane < 64, m0, m1) + jnp.log2(l)

    qspec = pl.BlockSpec((WQ, 2 * LANES), lambda p, i: (i, p))
    kspec = pl.BlockSpec((s, 2 * LANES), lambda p, i: (0, p))
    vspec = pl.BlockSpec((s, LANES), lambda p, i: (0, p))
    ospec = pl.BlockSpec((WQ, LANES), lambda p, i: (i, p))
    return pl.pallas_call(
        body, name="mla_fwd", grid=(4, s // WQ),
        out_shape=(jax.ShapeDtypeStruct((s, 512), F32), jax.ShapeDtypeStruct((s, 512), F32)),
        in_specs=[qspec, kspec, vspec], out_specs=(ospec, ospec),
        compiler_params=pltpu.CompilerParams(vmem_limit_bytes=VMEM_ATTN),
    )(qc, kc, v)


def _mla_bwd(qc, kc, v, do, lse, delta):
    s = qc.shape[0]

    def body(q_ref, k_ref, v_ref, do_ref, l_ref, d_ref, dq_ref, dk_ref, dv_ref):
        i = pl.program_id(1)

        @pl.when(i == 0)
        def _():
            dk_ref[...] = jnp.zeros_like(dk_ref)
            dv_ref[...] = jnp.zeros_like(dv_ref)

        lane = lax.broadcasted_iota(jnp.int32, (1, LANES), 1)
        rowc = (lax.broadcasted_iota(jnp.int32, (WQ, BK), 0) + i * WQ) // 64
        col = lax.broadcasted_iota(jnp.int32, (WQ, BK), 1)
        qw = q_ref[...]
        dob = do_ref[...]
        dos = (dob.astype(F32) * MLA_SCALE).astype(BF16)
        lp = l_ref[...]
        dp = d_ref[...] * MLA_SCALE
        lse0, lse1 = lp[:, 0:1], lp[:, 64:65]
        dl0, dl1 = dp[:, 0:1], dp[:, 64:65]

        def tile(j, dq, masked):
            ks = pl.multiple_of(j * BK, BK)
            kbd = _mla_keys(k_ref[pl.ds(ks, BK), :])
            vbd = _pair_stack(v_ref[pl.ds(ks, BK), :], lane)
            z = _mm_nt(qw, kbd) * (MLA_SCALE * LOG2E)
            p0 = jnp.exp2(z[:, 0:BK] - lse0)
            p1 = jnp.exp2(z[:, BK:2 * BK] - lse1)
            if masked:
                valid = ((col + j * BK) // 64) <= rowc
                p0 = jnp.where(valid, p0, 0.0)
                p1 = jnp.where(valid, p1, 0.0)
            dw = _mm_nt(dos, vbd)
            dzb = jnp.concatenate([p0 * (dw[:, 0:BK] - dl0), p1 * (dw[:, BK:2 * BK] - dl1)], axis=1).astype(BF16)
            dq = dq + _mm(dzb, kbd)
            rk = _mm_tn(dzb, qw)
            dk_ref[pl.ds(ks, BK), :] += jnp.concatenate([rk[0:BK, 0:LANES], rk[BK:2 * BK, LANES:2 * LANES]], axis=1)
            rv = _mm_tn(jnp.concatenate([p0, p1], axis=1).astype(BF16), dob)
            dv_ref[pl.ds(ks, BK), :] += jnp.where(lane < 64, rv[0:BK], rv[BK:2 * BK])
            return dq

        dq = lax.fori_loop(0, i, lambda jj, c: tile(2 * jj + 1, tile(2 * jj, c, False), False),
                           jnp.zeros((WQ, 2 * LANES), F32))
        dq_ref[...] = tile(2 * i + 1, tile(2 * i, dq, True), True)

    qspec = pl.BlockSpec((WQ, 2 * LANES), lambda p, i: (i, p))
    kspec = pl.BlockSpec((s, 2 * LANES), lambda p, i: (0, p))
    vspec = pl.BlockSpec((s, LANES), lambda p, i: (0, p))
    ospec = pl.BlockSpec((WQ, LANES), lambda p, i: (i, p))
    return pl.pallas_call(
        body, name="mla_bwd", grid=(4, s // WQ),
        out_shape=(jax.ShapeDtypeStruct((s, 1024), F32), jax.ShapeDtypeStruct((s, 1024), F32),
                   jax.ShapeDtypeStruct((s, 512), F32)),
        in_specs=[qspec, kspec, vspec, ospec, ospec, ospec], out_specs=(qspec, kspec, vspec),
        compiler_params=pltpu.CompilerParams(vmem_limit_bytes=VMEM_ATTN),
    )(qc, kc, v, do, lse, delta)


def _post(x, p, tgt, sbo, mlao, sbg, mlag, gsb, gmla, wout, wout_t, gpost, wple, gple, wpg, wpg_t, bpg):
    s = x.shape[0]

    def body(x_ref, p_ref, t_ref, sbo_ref, mlao_ref, sbg_ref, mlag_ref, gsb_ref, gmla_ref, wout_ref, woutt_ref,
             gpost_ref, wple_ref, gple_ref, wpg_ref, wpgt_ref, bpg_ref,
             dsbo_ref, dmlao_ref, delta_ref, dsbg_ref, dmlag_ref, dxres_ref, dwout_ref, dwpg_ref, dwple_ref, vec_ref):
        i = pl.program_id(0)

        @pl.when(i == 0)
        def _():
            dwout_ref[...] = jnp.zeros_like(dwout_ref)
            dwpg_ref[...] = jnp.zeros_like(dwpg_ref)
            dwple_ref[...] = jnp.zeros_like(dwple_ref)
            vec_ref[...] = jnp.zeros_like(vec_ref)

        bd = _blockdiag(512, HEAD_DIM)
        inv_hd = 1.0 / HEAD_DIM

        def head_fwd(o, g, gate):
            r = lax.rsqrt(_seg(o * o, bd) * inv_hd + EPS)
            hat = o * r
            n = hat * g
            sg = _sigmoid(gate)
            return hat, r, n, sg, n * (gate * sg)

        sbo, mlao, sbg_v, mlag_v = sbo_ref[...], mlao_ref[...], sbg_ref[...], mlag_ref[...]
        gsb_v, gmla_v = gsb_ref[...], gmla_ref[...]
        sb_hat, sb_r, sb_n, sb_sg, sb_y = head_fwd(sbo, gsb_v, sbg_v)
        ml_hat, ml_r, ml_n, ml_sg, ml_y = head_fwd(mlao, gmla_v, mlag_v)
        mix = jnp.concatenate([sb_y, ml_y], axis=1).astype(BF16)
        y = _mm(mix, wout_ref[...])
        ry = lax.rsqrt(_rowmean(y * y) + EPS)
        y_hat = y * ry
        gpost_v = gpost_ref[...]
        x1 = x_ref[...] + y_hat * gpost_v
        pb = p_ref[...].astype(BF16)
        pl_ = _mm(pb, wple_ref[...])
        rp = lax.rsqrt(_rowmean(pl_ * pl_) + EPS)
        pl_hat = pl_ * rp
        gple_v = gple_ref[...]
        ple = pl_hat * gple_v
        x1b = x1.astype(BF16)
        gate = _sigmoid(_mm(x1b, wpg_ref[...]) + bpg_ref[...])
        err = x1 + ple * gate - t_ref[...]
        loss = 0.5 * jnp.sum(_rowmean(err * err))
        dout = err * (1.0 / D_MODEL)

        du = dout * ple * gate * (1.0 - gate)
        dub = du.astype(BF16)
        dple = dout * gate
        dx1 = dout + _mm(dub, wpgt_ref[...])
        dwpg_ref[...] += _mm_tn(x1b, dub)
        dplh = dple * gple_v
        dpl = rp * (dplh - pl_hat * _rowmean(dplh * pl_hat))
        dwple_ref[...] += _mm_tn(pb, dpl.astype(BF16))
        dxres_ref[...] = dx1
        dyh = dx1 * gpost_v
        dy = ry * (dyh - y_hat * _rowmean(dyh * y_hat))
        dyb = dy.astype(BF16)
        dwout_ref[...] += _mm_tn(mix, dyb)
        dmix = _mm(dyb, woutt_ref[...])

        def head_bwd(dyv, hat, r, n, sg, g, gate):
            dn = dyv * (gate * sg)
            dgate = dyv * n * (sg * (1.0 + gate * (1.0 - sg)))
            dhat = dn * g
            do = r * (dhat - hat * (_seg(dhat * hat, bd) * inv_hd))
            return do, dgate, _colsum(dn * hat)

        dsbo, dsbg, dg_sb = head_bwd(dmix[:, 0:512], sb_hat, sb_r, sb_n, sb_sg, gsb_v, sbg_v)
        dmlao, dmlag, dg_ml = head_bwd(dmix[:, 512:1024], ml_hat, ml_r, ml_n, ml_sg, gmla_v, mlag_v)
        dsbo_ref[...] = dsbo.astype(BF16)
        dmlao_ref[...] = dmlao.astype(BF16)
        delta_ref[...] = _seg(dmlao * mlao, bd)
        dsbg_ref[...] = dsbg.astype(BF16)
        dmlag_ref[...] = dmlag.astype(BF16)
        vec_ref[pl.ds(0, 1), :] += _colsum(dx1 * y_hat)
        vec_ref[pl.ds(1, 1), :] += _colsum(dple * pl_hat)
        vec_ref[pl.ds(2, 1), :] += _colsum(du)
        vec_ref[pl.ds(3, 1), :] += jnp.concatenate([dg_sb, dg_ml], axis=1)
        vec_ref[pl.ds(4, 1), :] += jnp.full((1, D_MODEL), loss, F32)

    out_shape = (
        jax.ShapeDtypeStruct((s, 512), BF16), jax.ShapeDtypeStruct((s, 512), BF16), jax.ShapeDtypeStruct((s, 512), F32),
        jax.ShapeDtypeStruct((s, 512), BF16), jax.ShapeDtypeStruct((s, 512), BF16), jax.ShapeDtypeStruct((s, D_MODEL), F32),
        jax.ShapeDtypeStruct((D_MODEL, D_MODEL), F32), jax.ShapeDtypeStruct((D_MODEL, D_MODEL), F32),
        jax.ShapeDtypeStruct((PLE_DIM, D_MODEL), F32), jax.ShapeDtypeStruct((8, D_MODEL), F32),
    )
    return pl.pallas_call(
        body, name="post_fwd_bwd", grid=(s // TM,), out_shape=out_shape,
        in_specs=[_rows(D_MODEL), _rows(PLE_DIM), _rows(D_MODEL), _rows(512), _rows(512), _rows(512), _rows(512),
                  _full((1, 512)), _full((1, 512)), _full((D_MODEL, D_MODEL)), _full((D_MODEL, D_MODEL)),
                  _full((1, D_MODEL)), _full((PLE_DIM, D_MODEL)), _full((1, D_MODEL)), _full((D_MODEL, D_MODEL)),
                  _full((D_MODEL, D_MODEL)), _full((1, D_MODEL))],
        out_specs=(_rows(512), _rows(512), _rows(512), _rows(512), _rows(512), _rows(D_MODEL),
                   _full((D_MODEL, D_MODEL)), _full((D_MODEL, D_MODEL)), _full((PLE_DIM, D_MODEL)), _full((8, D_MODEL))),
        compiler_params=pltpu.CompilerParams(vmem_limit_bytes=VMEM_DENSE),
    )(x, p, tgt, sbo, mlao, sbg, mlag, gsb, gmla, wout, wout_t, gpost, wple, gple, wpg, wpg_t, bpg)


def _pre_bwd(x, dxres, dsbq, dsbk, dsbv, dsbg, dmlag, dqc, dkc, dmv, cq, ckv, tabs, gpre, win_t, gq, wuq_t, gkv,
             wk_t, wv_t):
    s = x.shape[0]
    c_t, sa_t, sb_t = tabs

    def body(x_ref, dxres_ref, dsbq_ref, dsbk_ref, dsbv_ref, dsbg_ref, dmlag_ref, dqc_ref, dkc_ref, dmv_ref, cq_ref,
             ckv_ref, c_ref, sa_ref, sb_ref, gpre_ref, wint_ref, gq_ref, wuqt_ref, gkv_ref, wkt_ref, wvt_ref,
             gx_ref, dwin_ref, dwuq_ref, dwk_ref, dwv_ref, vec_ref, dwin_acc):
        i = pl.program_id(0)

        @pl.when(i == 0)
        def _():
            dwin_acc[...] = jnp.zeros_like(dwin_acc)
            dwuq_ref[...] = jnp.zeros_like(dwuq_ref)
            dwk_ref[...] = jnp.zeros_like(dwk_ref)
            dwv_ref[...] = jnp.zeros_like(dwv_ref)
            vec_ref[...] = jnp.zeros_like(vec_ref)

        lane = lax.broadcasted_iota(jnp.int32, (1, LANES), 1)
        c1, sa1, sb1 = c_ref[...], sa_ref[...], sb_ref[...]
        c8, sa8, sb8 = jnp.tile(c1, (1, 8)), jnp.tile(sa1, (1, 8)), jnp.tile(sb1, (1, 8))

        def norm_bwd(dn, hat, r, g):
            t = dn * g
            return r * (t - hat * _rowmean(t * hat)), _colsum(dn * hat)

        dqeb = _rope_bwd(dqc_ref[...], c8, sa8, sb8).astype(BF16)
        cq = cq_ref[...]
        rq = lax.rsqrt(_rowmean(cq * cq) + EPS)
        cq_hat = cq * rq
        gq_v = gq_ref[...]
        dwuq_ref[...] += _mm_tn((cq_hat * gq_v).astype(BF16), dqeb)
        dcq, dg_q = norm_bwd(_mm(dqeb, wuqt_ref[...]), cq_hat, rq, gq_v)

        dkc = dkc_ref[...]
        dkcb = dkc.astype(BF16)
        dmvb = dmv_ref[...].astype(BF16)
        ckv = ckv_ref[...]
        rkv = lax.rsqrt(_rowmean(ckv * ckv) + EPS)
        ckv_hat = ckv * rkv
        gkv_v = gkv_ref[...]
        ckvnb = (ckv_hat * gkv_v).astype(BF16)
        dwk_ref[...] += _mm_tn(ckvnb, dkcb)
        dwv_ref[...] += _mm_tn(ckvnb, dmvb)
        dckv, dg_kv = norm_bwd(_mm(dkcb, wkt_ref[...]) + _mm(dmvb, wvt_ref[...]), ckv_hat, rkv, gkv_v)

        dkr = dkc[:, 0:LANES]
        for hh in range(1, 8):
            dkr = dkr + dkc[:, LANES * hh:LANES * (hh + 1)]
        dkr = _rope_bwd(dkr, c1, sa1, sb1)
        dkr = jnp.where((lane >= 64) & (lane < 96), dkr, 0.0)

        dproj = jnp.concatenate([dsbq_ref[...], dsbk_ref[...].astype(BF16), dsbv_ref[...].astype(BF16), dsbg_ref[...],
                                 dcq.astype(BF16), dckv.astype(BF16), dkr.astype(BF16), dmlag_ref[...]], axis=1)
        xv = x_ref[...]
        r1 = lax.rsqrt(_rowmean(xv * xv) + EPS)
        x_hat = xv * r1
        gpre_v = gpre_ref[...]
        dwin_acc[...] += _mm_tn((x_hat * gpre_v).astype(BF16), dproj)
        dx, dg_pre = norm_bwd(_mm(dproj, wint_ref[...]), x_hat, r1, gpre_v)
        gx_ref[...] = dxres_ref[...] + dx
        vec_ref[pl.ds(0, 1), :] += dg_pre
        vec_ref[pl.ds(1, 1), :] += jnp.concatenate([dg_q, dg_kv, jnp.zeros((1, D_MODEL - Q_LORA - KV_LORA), F32)], axis=1)

        @pl.when(i == pl.num_programs(0) - 1)
        def _():
            pltpu.sync_copy(dwin_acc, dwin_ref)

    out_shape = (
        jax.ShapeDtypeStruct((s, D_MODEL), F32), jax.ShapeDtypeStruct((D_MODEL, D_EXT), F32),
        jax.ShapeDtypeStruct((Q_LORA, 1024), F32), jax.ShapeDtypeStruct((KV_LORA, 1024), F32),
        jax.ShapeDtypeStruct((KV_LORA, 512), F32), jax.ShapeDtypeStruct((8, D_MODEL), F32),
    )
    return pl.pallas_call(
        body, name="pre_bwd", grid=(s // TM,), out_shape=out_shape,
        in_specs=[_rows(D_MODEL), _rows(D_MODEL), _rows(512), _rows(512), _rows(512), _rows(512), _rows(512),
                  _rows(1024), _rows(1024), _rows(512), _rows(Q_LORA), _rows(KV_LORA), _rows(LANES), _rows(LANES),
                  _rows(LANES), _full((1, D_MODEL)), _full((D_EXT, D_MODEL)), _full((1, Q_LORA)), _full((1024, Q_LORA)),
                  _full((1, KV_LORA)), _full((1024, KV_LORA)), _full((512, KV_LORA))],
        out_specs=(_rows(D_MODEL), pl.BlockSpec(memory_space=pl.ANY), _full((Q_LORA, 1024)), _full((KV_LORA, 1024)),
                   _full((KV_LORA, 512)), _full((8, D_MODEL))),
        scratch_shapes=[pltpu.VMEM((D_MODEL, D_EXT), F32)],
        compiler_params=pltpu.CompilerParams(vmem_limit_bytes=VMEM_DENSE),
    )(x, dxres, dsbq, dsbk, dsbv, dsbg, dmlag, dqc, dkc, dmv, cq, ckv, c_t, sa_t, sb_t, gpre, win_t, gq, wuq_t, gkv,
      wk_t, wv_t)


def _place():
    return lax.axis_index("x"), lax.axis_index("y"), lax.axis_index("c")


def _allgather_weights(pack):
    def body(p_ref, out_ref, send_sems, recv_sems):
        x, y, c = _place()
        me, sib = (x, y, c), (x, y, 1 - c)
        chips = [(1 - x, y), (x, 1 - y), (1 - x, 1 - y)]

        def half(chip, hc):
            start = pl.multiple_of(hc * HALF_ROWS, 8)
            return out_ref.at[2 * chip[0] + chip[1], pl.ds(start, HALF_ROWS), :]

        def copy(k, chip, hc, to):
            return pltpu.make_async_remote_copy(src_ref=half(chip, hc), dst_ref=half(chip, hc), send_sem=send_sems.at[k],
                                                recv_sem=recv_sems.at[k], device_id=to, device_id_type=MESH)

        out_ref[2 * x + y] = p_ref[...].astype(BF16)
        first = [copy(j, (x, y), c, (*chip, c)) for j, chip in enumerate(chips)]
        for cp in first:
            cp.start()
        passed = [copy(3 + j, chip, c, sib) for j, chip in enumerate(chips)]
        for j, chip in enumerate(chips):
            copy(j, chip, c, me).wait_recv()
            passed[j].start()
        for j, chip in enumerate(chips):
            copy(3 + j, chip, 1 - c, me).wait_recv()
        for cp in first + passed:
            cp.wait_send()

    return pl.pallas_call(
        body, name="allgather_weights",
        out_shape=jax.ShapeDtypeStruct((N_SHARD, PACK_ROWS, 1024), BF16),
        in_specs=[pl.BlockSpec(memory_space=pltpu.VMEM)], out_specs=pl.BlockSpec(memory_space=pltpu.VMEM),
        scratch_shapes=[pltpu.SemaphoreType.DMA((6,)), pltpu.SemaphoreType.DMA((6,))],
        compiler_params=pltpu.CompilerParams(vmem_limit_bytes=VMEM_ATTN),
    )(pack)


def _reduce_scatter_grads(gpack, vec):
    def body(g_ref, vec_ref, f_ref, vsum_ref, acc, sib_half, send_buf, recv_buf, vrecv, local_sem, send_sems, recv_sems):
        x, y, c = _place()
        me, sib = (x, y, c), (x, y, 1 - c)
        mine = 2 * x + y
        chips = [(1 - x, y), (x, 1 - y), (1 - x, 1 - y)]

        def remote(k, src, dst, to):
            return pltpu.make_async_remote_copy(src_ref=src, dst_ref=dst, send_sem=send_sems.at[k], recv_sem=recv_sems.at[k],
                                                device_id=to, device_id_type=MESH)

        load = pltpu.make_async_copy(g_ref.at[c], acc, local_sem)
        load.start()
        to_sib = remote(0, g_ref.at[1 - c], sib_half, sib)
        to_sib.start()

        my_dev = 4 * x + 2 * y + c
        vrecv[my_dev] = vec_ref[...]
        vec_sends = []
        for k in range(1, 8):
            to = (x ^ ((k >> 2) & 1), y ^ ((k >> 1) & 1), c ^ (k & 1))
            cp = remote(k, vec_ref, vrecv.at[my_dev], to)
            cp.start()
            vec_sends.append(cp)

        load.wait()
        remote(0, g_ref.at[1 - c], sib_half, me).wait_recv()
        for k in range(N_SHARD):
            acc[k] = acc[k] + sib_half[k]

        sends = []
        for j, chip in enumerate(chips):
            idx = 2 * chip[0] + chip[1]
            send_buf[idx] = acc[idx].astype(BF16)
            cp = remote(8 + j, send_buf.at[idx], recv_buf.at[mine], (*chip, c))
            cp.start()
            sends.append(cp)
        total = acc[mine]
        for j, chip in enumerate(chips):
            idx = 2 * chip[0] + chip[1]
            remote(8 + j, send_buf.at[idx], recv_buf.at[idx], me).wait_recv()
            total = total + recv_buf[idx].astype(F32)
        start = pl.multiple_of(c * HALF_ROWS, 8)
        f_ref[pl.ds(start, HALF_ROWS), :] = total

        other = pl.multiple_of((1 - c) * HALF_ROWS, 8)
        swap = remote(11, f_ref.at[pl.ds(start, HALF_ROWS), :], f_ref.at[pl.ds(start, HALF_ROWS), :], sib)
        swap.start()
        remote(11, f_ref.at[pl.ds(other, HALF_ROWS), :], f_ref.at[pl.ds(other, HALF_ROWS), :], me).wait_recv()

        for k in range(1, 8):
            src_dev = 4 * (x ^ ((k >> 2) & 1)) + 2 * (y ^ ((k >> 1) & 1)) + (c ^ (k & 1))
            remote(k, vec_ref, vrecv.at[src_dev], me).wait_recv()
        vs = vrecv[0]
        for d in range(1, 8):
            vs = vs + vrecv[d]
        vsum_ref[...] = vs

        for cp in [to_sib, swap] + vec_sends + sends:
            cp.wait_send()

    return pl.pallas_call(
        body, name="reduce_scatter_grads",
        out_shape=(jax.ShapeDtypeStruct((PACK_ROWS, 1024), F32), jax.ShapeDtypeStruct((VEC_ROWS, 1024), F32)),
        in_specs=[pl.BlockSpec(memory_space=pl.ANY), pl.BlockSpec(memory_space=pltpu.VMEM)],
        out_specs=(pl.BlockSpec(memory_space=pltpu.VMEM), pl.BlockSpec(memory_space=pltpu.VMEM)),
        scratch_shapes=[
            pltpu.VMEM((N_SHARD, HALF_ROWS, 1024), F32), pltpu.VMEM((N_SHARD, HALF_ROWS, 1024), F32),
            pltpu.VMEM((N_SHARD, HALF_ROWS, 1024), BF16), pltpu.VMEM((N_SHARD, HALF_ROWS, 1024), BF16),
            pltpu.VMEM((8, VEC_ROWS, 1024), F32),
            pltpu.SemaphoreType.DMA, pltpu.SemaphoreType.DMA((12,)), pltpu.SemaphoreType.DMA((12,)),
        ],
        compiler_params=pltpu.CompilerParams(vmem_limit_bytes=56 * 1024 * 1024),
    )(gpack, vec)


def _adamw(w, g, m, v):
    rows, cols = w.shape
    tr = rows if rows <= 256 else 256

    def body(w_ref, g_ref, m_ref, v_ref, d_ref, nm_ref, nv_ref):
        gv = g_ref[...]
        m2 = ADAM_B1 * m_ref[...] + (1.0 - ADAM_B1) * gv
        v2 = ADAM_B2 * v_ref[...] + (1.0 - ADAM_B2) * (gv * gv)
        m_hat = m2 / (1.0 - ADAM_B1 ** ADAM_STEP)
        v_hat = v2 / (1.0 - ADAM_B2 ** ADAM_STEP)
        d_ref[...] = -ADAM_LR * (m_hat / (jnp.sqrt(v_hat) + ADAM_EPS) + ADAM_WD * w_ref[...])
        nm_ref[...] = m2
        nv_ref[...] = v2

    spec = pl.BlockSpec((tr, cols), lambda i: (i, 0))
    shp = jax.ShapeDtypeStruct((rows, cols), F32)
    return pl.pallas_call(body, name="adamw", grid=(rows // tr,), out_shape=(shp, shp, shp),
                          in_specs=[spec] * 4, out_specs=(spec,) * 3)(w, g, m, v)


_PACK_LAYOUT = (("w_in", 1024, 744), ("w_uq", 256, 192), ("w_ukv", 128, 256), ("w_out", 256, 1024), ("w_ple", 256, 256),
                ("w_ple_gate", 256, 1024))


def _pack_shard(parts):
    rows = [parts[n].reshape(-1, 1024) for n, _, _ in _PACK_LAYOUT]
    used = sum(r.shape[0] for r in rows)
    return jnp.concatenate(rows + [jnp.zeros((PACK_ROWS - used, 1024), rows[0].dtype)], axis=0)


def _unpack_shard(pack):
    out, off = {}, 0
    for n, r, cdim in _PACK_LAYOUT:
        nr = r * cdim // 1024
        out[n] = pack[..., off:off + nr, :].reshape(pack.shape[:-2] + (r, cdim))
        off += nr
    return out


def _join_shards(parts):
    cat_cols = lambda a: jnp.concatenate([a[k] for k in range(N_SHARD)], axis=1)
    cat_rows = lambda a: jnp.concatenate([a[k] for k in range(N_SHARD)], axis=0)
    return {"w_in": cat_cols(parts["w_in"]), "w_uq": cat_cols(parts["w_uq"]), "w_ukv": cat_cols(parts["w_ukv"]),
            "w_out": cat_rows(parts["w_out"]), "w_ple": cat_cols(parts["w_ple"]), "w_ple_gate": cat_rows(parts["w_ple_gate"])}


def _split_shards(full):
    cols = lambda a: jnp.stack(jnp.split(a, N_SHARD, axis=1))
    rows = lambda a: jnp.stack(jnp.split(a, N_SHARD, axis=0))
    return {"w_in": cols(full["w_in"]), "w_uq": cols(full["w_uq"]), "w_ukv": cols(full["w_ukv"]),
            "w_out": rows(full["w_out"]), "w_ple": cols(full["w_ple"]), "w_ple_gate": rows(full["w_ple_gate"])}


def _extend_weights(w):
    win = w["w_in"]
    zeros = lambda r, c: jnp.zeros((r, c), win.dtype)
    win_ext = jnp.concatenate([win[:, :2432], zeros(D_MODEL, 64), win[:, 2432:2464], zeros(D_MODEL, 32), win[:, 2464:]], axis=1)
    wuq_ext = jnp.pad(w["w_uq"].reshape(Q_LORA, 8, 96), ((0, 0), (0, 0), (0, 32))).reshape(Q_LORA, 1024)
    wukv = w["w_ukv"].reshape(KV_LORA, 8, 128)
    wk_ext = jnp.pad(wukv[:, :, :64], ((0, 0), (0, 0), (0, 64))).reshape(KV_LORA, 1024)
    wv = wukv[:, :, 64:].reshape(KV_LORA, 512)
    return win_ext, wuq_ext, wk_ext, wv


def _contract_grads(dwin_ext, dwuq_ext, dwk_ext, dwv):
    dwin = jnp.concatenate([dwin_ext[:, :2432], dwin_ext[:, 2496:2528], dwin_ext[:, 2560:]], axis=1)
    dwuq = dwuq_ext.reshape(Q_LORA, 8, 128)[:, :, :96].reshape(Q_LORA, 768)
    dwukv = jnp.concatenate([dwk_ext.reshape(KV_LORA, 8, 128)[:, :, :64], dwv.reshape(KV_LORA, 8, 64)], axis=2)
    return dwin, dwuq, dwukv.reshape(KV_LORA, 1024)


def _rope_tables(positions):
    half = QK_ROPE // 2
    freq = ROPE_THETA ** (-jnp.arange(half, dtype=F32) / half)
    ang = positions.astype(F32)[:, None] * freq
    cos, sin = jnp.cos(ang), jnp.sin(ang)
    s = positions.shape[0]
    z = lambda n: jnp.zeros((s, n), F32)
    c_t = jnp.concatenate([jnp.ones((s, 64), F32), cos, cos, z(32)], axis=1)
    sa_t = jnp.concatenate([z(64), -sin, z(16), z(32)], axis=1)
    sb_t = jnp.concatenate([z(64), z(16), sin, z(32)], axis=1)
    return c_t, sa_t, sb_t


def _local_grads(x, p, positions, tgt, gains, wfull):
    win_ext, wuq_ext, wk_ext, wv = _extend_weights(wfull)
    wout, wple, wpg = wfull["w_out"], wfull["w_ple"], wfull["w_ple_gate"]
    tabs = _rope_tables(positions)
    g = gains
    sbq, sbk, sbv, sbg, mlag, cq, ckv, qc, kc, mv = _pre_fwd(x, tabs, g["norm_pre_g"], win_ext, g["q_norm_g"], wuq_ext,
                                                             g["kv_norm_g"], wk_ext, wv)
    sbo, sbt = _sb_fwd(sbq, sbk, sbv)
    mlao, lse = _mla_fwd(qc, kc, mv)
    dsbo, dmlao, delta, dsbg, dmlag, dxres, dwout, dwpg, dwple, vec_c = _post(
        x, p, tgt, sbo, mlao, sbg, mlag, g["sb_out_norm_g"], g["mla_out_norm_g"], wout, wout.T, g["norm_post_g"], wple,
        g["ple_norm_g"], wpg, wpg.T, g["b_ple_gate"])
    dsbq, dsbk, dsbv = _sb_bwd(sbq, sbk, sbv, dsbo, sbt)
    dqc, dkc, dmv = _mla_bwd(qc, kc, mv, dmlao, lse, delta)
    gx, dwin_ext, dwuq_ext, dwk_ext, dwv, vec_d = _pre_bwd(
        x, dxres, dsbq, dsbk, dsbv, dsbg, dmlag, dqc, dkc, dmv, cq, ckv, tabs, g["norm_pre_g"], win_ext.T, g["q_norm_g"],
        wuq_ext.T, g["kv_norm_g"], wk_ext.T, wv.T)
    dwin, dwuq, dwukv = _contract_grads(dwin_ext, dwuq_ext, dwk_ext, dwv)
    grads = {"w_in": dwin, "w_uq": dwuq, "w_ukv": dwukv, "w_out": dwout, "w_ple": dwple, "w_ple_gate": dwpg}
    return gx, grads, jnp.concatenate([vec_c, vec_d], axis=0)


_VEC_LAYOUT = (("norm_post_g", 0, 0, 1024), ("ple_norm_g", 1, 0, 1024), ("b_ple_gate", 2, 0, 1024), ("sb_out_norm_g", 3, 0, 512),
               ("mla_out_norm_g", 3, 512, 512), ("norm_pre_g", 8, 0, 1024), ("q_norm_g", 9, 0, 256), ("kv_norm_g", 9, 256, 128))
_LOSS_ROW = 4
_WEIGHT_ORDER = ("norm_pre_g", "w_in", "q_norm_g", "w_uq", "kv_norm_g", "w_ukv", "sb_out_norm_g", "mla_out_norm_g", "w_out",
                 "norm_post_g", "w_ple", "ple_norm_g", "w_ple_gate", "b_ple_gate")


def _vec_block(named):
    blk = jnp.zeros((VEC_ROWS, 1024), F32)
    for n, r, c0, width in _VEC_LAYOUT:
        blk = blk.at[r, c0:c0 + width].set(named[n][0])
    return blk


def kernel(x, p, positions, norm_pre_g, w_in, q_norm_g, w_uq, kv_norm_g, w_ukv, sb_out_norm_g, mla_out_norm_g, w_out, norm_post_g, w_ple, ple_norm_g, w_ple_gate, b_ple_gate, loss_target, m_norm_pre_g, m_w_in, m_q_norm_g, m_w_uq, m_kv_norm_g, m_w_ukv, m_sb_out_norm_g, m_mla_out_norm_g, m_w_out, m_norm_post_g, m_w_ple, m_ple_norm_g, m_w_ple_gate, m_b_ple_gate, v_norm_pre_g, v_w_in, v_q_norm_g, v_w_uq, v_kv_norm_g, v_w_ukv, v_sb_out_norm_g, v_mla_out_norm_g, v_w_out, v_norm_post_g, v_w_ple, v_ple_norm_g, v_w_ple_gate, v_b_ple_gate):
    w = {"norm_pre_g": norm_pre_g, "w_in": w_in[0], "q_norm_g": q_norm_g, "w_uq": w_uq[0], "kv_norm_g": kv_norm_g, "w_ukv": w_ukv[0],
         "sb_out_norm_g": sb_out_norm_g, "mla_out_norm_g": mla_out_norm_g, "w_out": w_out[0], "norm_post_g": norm_post_g,
         "w_ple": w_ple[0], "ple_norm_g": ple_norm_g, "w_ple_gate": w_ple_gate[0], "b_ple_gate": b_ple_gate}
    m = {"norm_pre_g": m_norm_pre_g, "w_in": m_w_in[0], "q_norm_g": m_q_norm_g, "w_uq": m_w_uq[0], "kv_norm_g": m_kv_norm_g,
         "w_ukv": m_w_ukv[0], "sb_out_norm_g": m_sb_out_norm_g, "mla_out_norm_g": m_mla_out_norm_g, "w_out": m_w_out[0],
         "norm_post_g": m_norm_post_g, "w_ple": m_w_ple[0], "ple_norm_g": m_ple_norm_g, "w_ple_gate": m_w_ple_gate[0],
         "b_ple_gate": m_b_ple_gate}
    v = {"norm_pre_g": v_norm_pre_g, "w_in": v_w_in[0], "q_norm_g": v_q_norm_g, "w_uq": v_w_uq[0], "kv_norm_g": v_kv_norm_g,
         "w_ukv": v_w_ukv[0], "sb_out_norm_g": v_sb_out_norm_g, "mla_out_norm_g": v_mla_out_norm_g, "w_out": v_w_out[0],
         "norm_post_g": v_norm_post_g, "w_ple": v_w_ple[0], "ple_norm_g": v_ple_norm_g, "w_ple_gate": v_w_ple_gate[0],
         "b_ple_gate": v_b_ple_gate}
    big = [n for n, _, _ in _PACK_LAYOUT]

    gathered = _allgather_weights(_pack_shard({n: w[n] for n in big}))
    wfull = _join_shards(_unpack_shard(gathered))

    gx, grads, vec = _local_grads(x[0], p[0, 0], positions[0], loss_target[0], w, wfull)

    gsh = _split_shards(grads)
    gpack = jnp.stack([_pack_shard({n: gsh[n][k] for n in big}) for k in range(N_SHARD)])
    gpack = gpack.reshape(N_SHARD, 2, HALF_ROWS, 1024).transpose(1, 0, 2, 3)
    gred, vsum = _reduce_scatter_grads(gpack, vec)
    g = _unpack_shard(gred)
    for n, r, c0, width in _VEC_LAYOUT:
        g[n] = vsum[r:r + 1, c0:c0 + width]
    loss = vsum[_LOSS_ROW, 0]

    delta, new_m, new_v = {}, {}, {}
    for n in big:
        delta[n], new_m[n], new_v[n] = _adamw(w[n], g[n], m[n], v[n])
    small = [n for n in _WEIGHT_ORDER if n not in big]
    dv, mv_, vv = _adamw(_vec_block({n: w[n] for n in small}), vsum * _vec_block({n: jnp.ones_like(w[n]) for n in small}),
                         _vec_block({n: m[n] for n in small}), _vec_block({n: v[n] for n in small}))
    for n, r, c0, width in _VEC_LAYOUT:
        delta[n], new_m[n], new_v[n] = dv[r:r + 1, c0:c0 + width], mv_[r:r + 1, c0:c0 + width], vv[r:r + 1, c0:c0 + width]

    lead = lambda n, a: a[None] if n in big else a
    return (loss, gx[None],
            *[lead(n, g[n]) for n in _WEIGHT_ORDER], *[lead(n, delta[n]) for n in _WEIGHT_ORDER],
            *[lead(n, new_m[n]) for n in _WEIGHT_ORDER], *[lead(n, new_v[n]) for n in _WEIGHT_ORDER])
```

```python
import functools

import jax
import jax.numpy as jnp
from jax import lax
from jax.experimental import pallas as pl
from jax.experimental.pallas import tpu as pltpu

F32 = jnp.float32
BF16 = jnp.bfloat16
MESH = pl.DeviceIdType.MESH

D_MODEL = 1024
HEAD_DIM = 64
D_SB = 512
D_MLA = 512
Q_LORA = 256
KV_LORA = 128
QK_NOPE = 64
QK_ROPE = 32
PLE_DIM = 256
D_IN = 2976
D_EXT = 3072
ROPE_THETA = 10000.0
EPS = 1e-6
N_SHARD = 4

ADAM_LR = 0.001
ADAM_B1 = 0.9
ADAM_B2 = 0.999
ADAM_EPS = 1e-08
ADAM_WD = 0.01
ADAM_STEP = 10

LANES = 128
BK = 128
WQ = 256
TM = 256
PACK_ROWS = 1408
HALF_ROWS = PACK_ROWS // 2
VEC_ROWS = 16
VMEM_DENSE = 52 * 1024 * 1024
VMEM_ATTN = 40 * 1024 * 1024


def _mm(a, b):
    return jnp.dot(a, b, preferred_element_type=F32)


def _mm_nt(a, b):
    return lax.dot_general(a, b, (((1,), (1,)), ((), ())), preferred_element_type=F32)


def _mm_tn(a, b):
    return lax.dot_general(a, b, (((0,), (0,)), ((), ())), preferred_element_type=F32)


def _seg(a, bd):
    a1 = a.astype(BF16)
    r1 = a - a1.astype(F32)
    a2 = r1.astype(BF16)
    a3 = (r1 - a2.astype(F32)).astype(BF16)
    return _mm(a1, bd) + _mm(a2, bd) + _mm(a3, bd)


def _blockdiag(n, seg):
    r = lax.broadcasted_iota(jnp.int32, (n, n), 0) // seg
    c = lax.broadcasted_iota(jnp.int32, (n, n), 1) // seg
    return jnp.where(r == c, 1.0, 0.0).astype(BF16)


def _sigmoid(a):
    return 1.0 / (1.0 + jnp.exp(-a))


def _rowmean(a):
    return jnp.mean(a, axis=-1, keepdims=True)


def _colsum(a):
    return jnp.sum(a, axis=0, keepdims=True)


def _rope_fwd(a, c, sa, sb):
    w = a.shape[-1]
    return a * c + pltpu.roll(a, w - 16, 1) * sa + pltpu.roll(a, 16, 1) * sb


def _rope_bwd(g, c, sa, sb):
    w = g.shape[-1]
    return g * c + pltpu.roll(g * sa, 16, 1) + pltpu.roll(g * sb, w - 16, 1)


def _full(shape):
    return pl.BlockSpec(shape, lambda *_: (0,) * len(shape))


def _rows(width, tm=TM):
    return pl.BlockSpec((tm, width), lambda i: (i, 0))


def _pre_fwd(x, tabs, gpre, win, gq, wuq, gkv, wk, wv):
    s = x.shape[0]
    c_t, sa_t, sb_t = tabs

    def body(x_ref, c_ref, sa_ref, sb_ref, gpre_ref, win_ref, gq_ref, wuq_ref, gkv_ref, wk_ref, wv_ref,
             sbq_ref, sbk_ref, sbv_ref, sbg_ref, mlag_ref, cq_ref, ckv_ref, qc_ref, kc_ref, mv_ref):
        xv = x_ref[...]
        r1 = lax.rsqrt(_rowmean(xv * xv) + EPS)
        h = (xv * r1 * gpre_ref[...]).astype(BF16)
        proj = _mm(h, win_ref[...])
        sbq_ref[...] = proj[:, 0:512].astype(BF16)
        sbk_ref[...] = proj[:, 512:1024].astype(BF16)
        sbv_ref[...] = proj[:, 1024:1536].astype(BF16)
        sbg_ref[...] = proj[:, 1536:2048]
        cq = proj[:, 2048:2304]
        ckv = proj[:, 2304:2432]
        kr = proj[:, 2432:2560]
        mlag_ref[...] = proj[:, 2560:3072]
        cq_ref[...] = cq
        ckv_ref[...] = ckv
        c1, sa1, sb1 = c_ref[...], sa_ref[...], sb_ref[...]
        c8, sa8, sb8 = jnp.tile(c1, (1, 8)), jnp.tile(sa1, (1, 8)), jnp.tile(sb1, (1, 8))
        cqn = (cq * lax.rsqrt(_rowmean(cq * cq) + EPS) * gq_ref[...]).astype(BF16)
        qe = _mm(cqn, wuq_ref[...])
        qc_ref[...] = _rope_fwd(qe, c8, sa8, sb8).astype(BF16)
        ckvn = (ckv * lax.rsqrt(_rowmean(ckv * ckv) + EPS) * gkv_ref[...]).astype(BF16)
        ke = _mm(ckvn, wk_ref[...])
        krr = _rope_fwd(kr, c1, sa1, sb1)
        kc_ref[...] = (ke + jnp.tile(krr, (1, 8))).astype(BF16)
        mv_ref[...] = _mm(ckvn, wv_ref[...]).astype(BF16)

    out_shape = (
        jax.ShapeDtypeStruct((s, 512), BF16), jax.ShapeDtypeStruct((s, 512), BF16), jax.ShapeDtypeStruct((s, 512), BF16),
        jax.ShapeDtypeStruct((s, 512), F32), jax.ShapeDtypeStruct((s, 512), F32),
        jax.ShapeDtypeStruct((s, Q_LORA), F32), jax.ShapeDtypeStruct((s, KV_LORA), F32),
        jax.ShapeDtypeStruct((s, 1024), BF16), jax.ShapeDtypeStruct((s, 1024), BF16), jax.ShapeDtypeStruct((s, 512), BF16),
    )
    return pl.pallas_call(
        body, name="pre_fwd", grid=(s // TM,), out_shape=out_shape,
        in_specs=[_rows(D_MODEL), _rows(LANES), _rows(LANES), _rows(LANES), _full((1, D_MODEL)), _full((D_MODEL, D_EXT)),
                  _full((1, Q_LORA)), _full((Q_LORA, 1024)), _full((1, KV_LORA)), _full((KV_LORA, 1024)), _full((KV_LORA, 512))],
        out_specs=(_rows(512), _rows(512), _rows(512), _rows(512), _rows(512), _rows(Q_LORA), _rows(KV_LORA),
                   _rows(1024), _rows(1024), _rows(512)),
        compiler_params=pltpu.CompilerParams(vmem_limit_bytes=VMEM_DENSE),
    )(x, c_t, sa_t, sb_t, gpre, win, gq, wuq, gkv, wk, wv)


def _softplus(z):
    neg_abs = lax.bitcast_convert_type(lax.bitcast_convert_type(z, jnp.uint32) | jnp.uint32(0x80000000), F32)
    return jnp.maximum(z, 0.0) + jnp.log(1.0 + jnp.exp(neg_abs))


def _pair_tri(kind):
    r = lax.broadcasted_iota(jnp.int32, (512, 256), 0)
    c = lax.broadcasted_iota(jnp.int32, (512, 256), 1)
    same = ((r // BK) % 2) == (c // BK)
    rk, ck = r % BK, c % BK
    m = {"ge": rk >= ck, "lt": rk < ck, "le": rk <= ck}[kind]
    return jnp.where(same & m, 1.0, 0.0).astype(BF16)


def _split2(a):
    hi = a.astype(BF16)
    lo = (a - hi.astype(F32)).astype(BF16)
    return jnp.concatenate([hi, lo], axis=1)


def _pair_stack(b, lane):
    zero = jnp.zeros_like(b)
    return jnp.concatenate([jnp.where(lane < 64, b, zero), jnp.where(lane >= 64, b, zero)], axis=0)


def _sb_fwd(q, k, v):
    s = q.shape[0]

    def body(q_ref, k_ref, v_ref, o_ref, t_ref):
        i = pl.program_id(1)
        lane = lax.broadcasted_iota(jnp.int32, (1, LANES), 1)
        row = lax.broadcasted_iota(jnp.int32, (WQ, 2 * BK), 0) + i * WQ
        col = lax.broadcasted_iota(jnp.int32, (WQ, 2 * BK), 1) % BK
        u_ge = _pair_tri("ge")
        qs = q_ref[...] * (HEAD_DIM ** -0.5)

        def scores(j):
            ks = pl.multiple_of(j * BK, BK)
            return _mm_nt(qs, _pair_stack(k_ref[pl.ds(ks, BK), :], lane))

        def decay(z, j, masked):
            sp = _softplus(z)
            if masked:
                sp = jnp.where((col + j * BK) < row, sp, 0.0)
            return _mm(_split2(sp), u_ge)

        def weights(z, cum, j, masked):
            w = jnp.exp(z - cum)
            if masked:
                w = jnp.where((col + j * BK) < row, w, 0.0)
            ks = pl.multiple_of(j * BK, BK)
            return _mm(w.astype(BF16), _pair_stack(v_ref[pl.ds(ks, BK), :], lane))

        def fold(carry, pv, cum):
            acc, run = carry
            return acc + jnp.exp(-run) * pv, run + _heads_narrow(cum, 0, lane)

        def pair(ja, jb, carry, masked):
            za, zb = scores(ja), scores(jb)
            ca = decay(za, ja, masked)
            cb = decay(zb, jb, masked)
            pa = weights(za, ca, ja, masked)
            pb = weights(zb, cb, jb, masked)
            return fold(fold(carry, pa, ca), pb, cb)

        assert WQ == 2 * BK
        carry = pair(2 * i + 1, 2 * i, (jnp.zeros((WQ, LANES), F32), jnp.zeros((WQ, LANES), F32)), True)
        acc, run = lax.fori_loop(0, i, lambda jj, c: pair(2 * i - 1 - 2 * jj, 2 * i - 2 - 2 * jj, c, False), carry)
        o_ref[...] = acc
        t_ref[...] = -run

    qspec = pl.BlockSpec((WQ, LANES), lambda p, i: (i, p))
    kspec = pl.BlockSpec((s, LANES), lambda p, i: (0, p))
    return pl.pallas_call(
        body, name="sb_fwd", grid=(4, s // WQ),
        out_shape=(jax.ShapeDtypeStruct((s, 512), F32), jax.ShapeDtypeStruct((s, 512), F32)),
        in_specs=[qspec, kspec, kspec], out_specs=(qspec, qspec),
        compiler_params=pltpu.CompilerParams(vmem_limit_bytes=VMEM_ATTN),
    )(q, k, v)


def _pair_tri1(kind):
    r = lax.broadcasted_iota(jnp.int32, (2 * BK, 2 * BK), 0)
    c = lax.broadcasted_iota(jnp.int32, (2 * BK, 2 * BK), 1)
    rk, ck = r % BK, c % BK
    m = {"ge": rk >= ck, "lt": rk < ck, "le": rk <= ck}[kind]
    return jnp.where(((r // BK) == (c // BK)) & m, 1.0, 0.0).astype(BF16)


def _heads_wide(a):
    m = a.shape[0]
    return jnp.concatenate([jnp.broadcast_to(a[:, 0:1], (m, BK)), jnp.broadcast_to(a[:, 64:65], (m, BK))], axis=1)


def _heads_narrow(a, col, lane):
    return jnp.where(lane < 64, a[:, col:col + 1], a[:, BK + col:BK + col + 1])


def _sb_bwd(q, k, v, do, tstat):
    s = q.shape[0]
    assert WQ == 2 * BK

    def body(q_ref, k_ref, v_ref, do_ref, t_ref, dq_ref, dk_ref, dv_ref):
        i = pl.program_id(1)

        @pl.when(i == 0)
        def _():
            dk_ref[...] = jnp.zeros_like(dk_ref)
            dv_ref[...] = jnp.zeros_like(dv_ref)

        lane = lax.broadcasted_iota(jnp.int32, (1, LANES), 1)
        row = lax.broadcasted_iota(jnp.int32, (WQ, 2 * BK), 0) + i * WQ
        col = lax.broadcasted_iota(jnp.int32, (WQ, 2 * BK), 1) % BK
        u_ge = _pair_tri("ge")
        u_le = _pair_tri1("le")
        qs = q_ref[...] * (HEAD_DIM ** -0.5)
        dof = do_ref[...].astype(F32)
        tot = -t_ref[...]

        def keys(j):
            ks = pl.multiple_of(j * BK, BK)
            return ks, _pair_stack(k_ref[pl.ds(ks, BK), :], lane)

        def gates(z, j, masked):
            neg_abs = lax.bitcast_convert_type(lax.bitcast_convert_type(z, jnp.uint32) | jnp.uint32(0x80000000), F32)
            u = jnp.exp(neg_abs)
            opu = 1.0 + u
            sp = jnp.maximum(z, 0.0) + jnp.log(opu)
            if masked:
                sp = jnp.where((col + j * BK) < row, sp, 0.0)
            return jnp.where(z >= 0.0, 1.0, u) / opu, _mm(_split2(sp), u_ge)

        def weights(z, cum, prun, ks, j, masked):
            wl = jnp.exp(z - cum)
            if masked:
                wl = jnp.where((col + j * BK) < row, wl, 0.0)
            dfo = (jnp.exp(prun - tot) * dof).astype(BF16)
            e = _mm_nt(dfo, _pair_stack(v_ref[pl.ds(ks, BK), :], lane)) * wl
            return wl.astype(BF16), dfo, e

        def scores_grad(e, sig, erun, j, masked):
            ecum = _mm(e.astype(BF16), u_le) + _heads_wide(erun)
            dz = e - sig * ecum
            if masked:
                dz = jnp.where((col + j * BK) < row, dz, 0.0)
            return dz.astype(BF16), _heads_narrow(ecum, BK - 1, lane)

        def scatter(ks, dzb, wlb, dfo):
            rk = _mm_tn(dzb, qs)
            dk_ref[pl.ds(ks, BK), :] += jnp.where(lane < 64, rk[0:BK], rk[BK:2 * BK])
            rv = _mm_tn(wlb, dfo)
            dv_ref[pl.ds(ks, BK), :] += jnp.where(lane < 64, rv[0:BK], rv[BK:2 * BK])

        def pair(ja, carry, masked):
            dq, prun, erun = carry
            jb = ja + 1
            ksa, kbda = keys(ja)
            ksb, kbdb = keys(jb)
            za, zb = _mm_nt(qs, kbda), _mm_nt(qs, kbdb)
            siga, cuma = gates(za, ja, masked)
            sigb, cumb = gates(zb, jb, masked)
            pruna = prun + _heads_narrow(cuma, 0, lane)
            prunb = pruna + _heads_narrow(cumb, 0, lane)
            wla, dfoa, ea = weights(za, cuma, pruna, ksa, ja, masked)
            wlb, dfob, eb = weights(zb, cumb, prunb, ksb, jb, masked)
            dza, eruna = scores_grad(ea, siga, erun, ja, masked)
            dzb, erunb = scores_grad(eb, sigb, eruna, jb, masked)
            dq = dq + _mm(jnp.concatenate([dza, dzb], axis=1), jnp.concatenate([kbda, kbdb], axis=0))
            scatter(ksa, dza, wla, dfoa)
            scatter(ksb, dzb, wlb, dfob)
            return dq, prunb, erunb

        zero = jnp.zeros((WQ, LANES), F32)
        carry = lax.fori_loop(0, i, lambda jj, c: pair(2 * jj, c, False), (zero, zero, zero))
        dq, _, _ = pair(2 * i, carry, True)
        dq_ref[...] = (dq * (HEAD_DIM ** -0.5)).astype(BF16)

    qspec = pl.BlockSpec((WQ, LANES), lambda p, i: (i, p))
    kspec = pl.BlockSpec((s, LANES), lambda p, i: (0, p))
    return pl.pallas_call(
        body, name="sb_bwd", grid=(4, s // WQ),
        out_shape=(jax.ShapeDtypeStruct((s, 512), BF16), jax.ShapeDtypeStruct((s, 512), F32),
                   jax.ShapeDtypeStruct((s, 512), F32)),
        in_specs=[qspec, kspec, kspec, qspec, qspec], out_specs=(qspec, kspec, kspec),
        compiler_params=pltpu.CompilerParams(vmem_limit_bytes=VMEM_ATTN),
    )(q, k, v, do, tstat)


MLA_SCALE = (QK_NOPE + QK_ROPE) ** -0.5
LOG2E = 1.4426950408889634


def _mla_keys(kb):
    zero = jnp.zeros((BK, LANES), kb.dtype)
    return jnp.concatenate([jnp.concatenate([kb[:, 0:LANES], zero], axis=1),
                            jnp.concatenate([zero, kb[:, LANES:2 * LANES]], axis=1)], axis=0)


def _mla_fwd(qc, kc, v):
    s = qc.shape[0]

    def body(q_ref, k_ref, v_ref, o_ref, l_ref):
        i = pl.program_id(1)
        lane = lax.broadcasted_iota(jnp.int32, (1, LANES), 1)
        rowc = (lax.broadcasted_iota(jnp.int32, (WQ, BK), 0) + i * WQ) // 64
        col = lax.broadcasted_iota(jnp.int32, (WQ, BK), 1)
        qw = q_ref[...]
        ind0 = jnp.broadcast_to(jnp.where(lane < 64, 1.0, 0.0).astype(BF16), (BK, LANES))
        ind1 = jnp.broadcast_to(jnp.where(lane >= 64, 1.0, 0.0).astype(BF16), (BK, LANES))

        def scores(j, masked):
            ks = pl.multiple_of(j * BK, BK)
            z = _mm_nt(qw, _mla_keys(k_ref[pl.ds(ks, BK), :])) * (MLA_SCALE * LOG2E)
            z0, z1 = z[:, 0:BK], z[:, BK:2 * BK]
            if masked:
                valid = ((col + j * BK) // 64) <= rowc
                z0, z1 = jnp.where(valid, z0, -1e30), jnp.where(valid, z1, -1e30)
            return z0, z1

        def values(j):
            ks = pl.multiple_of(j * BK, BK)
            vb = v_ref[pl.ds(ks, BK), :]
            zero = jnp.zeros_like(vb)
            return jnp.concatenate([jnp.concatenate([jnp.where(lane < 64, vb, zero), ind0], axis=1),
                                    jnp.concatenate([jnp.where(lane >= 64, vb, zero), ind1], axis=1)], axis=0)

        def pair(ja, carry, masked):
            m0, m1, l, acc = carry
            a0, a1 = scores(ja, masked)
            b0, b1 = scores(ja + 1, masked)
            n0 = jnp.maximum(m0, jnp.maximum(jnp.max(a0, axis=1, keepdims=True), jnp.max(b0, axis=1, keepdims=True)))
            n1 = jnp.maximum(m1, jnp.maximum(jnp.max(a1, axis=1, keepdims=True), jnp.max(b1, axis=1, keepdims=True)))
            p = jnp.concatenate([jnp.exp2(a0 - n0), jnp.exp2(a1 - n1), jnp.exp2(b0 - n0), jnp.exp2(b1 - n1)], axis=1)
            pv = _mm(p.astype(BF16), jnp.concatenate([values(ja), values(ja + 1)], axis=0))
            a = jnp.where(lane < 64, jnp.exp2(m0 - n0), jnp.exp2(m1 - n1))
            return n0, n1, a * l + pv[:, LANES:2 * LANES], a * acc + pv[:, 0:LANES]

        neg = jnp.full((WQ, 1), -1e30, F32)
        zero = jnp.zeros((WQ, LANES), F32)
        carry = lax.fori_loop(0, i, lambda jj, c: pair(2 * jj, c, False), (neg, neg, zero, zero))
        m0, m1, l, acc = pair(2 * i, carry, True)
        o_ref[...] = acc / l
        l_ref[...] = jnp.where(lane < 64, m0, m1) + jnp.log2(l)

    qspec = pl.BlockSpec((WQ, 2 * LANES), lambda p, i: (i, p))
    kspec = pl.BlockSpec((s, 2 * LANES), lambda p, i: (0, p))
    vspec = pl.BlockSpec((s, LANES), lambda p, i: (0, p))
    ospec = pl.BlockSpec((WQ, LANES), lambda p, i: (i, p))
    return pl.pallas_call(
        body, name="mla_fwd", grid=(4, s // WQ),
        out_shape=(jax.ShapeDtypeStruct((s, 512), F32), jax.ShapeDtypeStruct((s, 512), F32)),
        in_specs=[qspec, kspec, vspec], out_specs=(ospec, ospec),
        compiler_params=pltpu.CompilerParams(vmem_limit_bytes=VMEM_ATTN),
    )(qc, kc, v)


def _mla_bwd(qc, kc, v, do, lse, delta):
    s = qc.shape[0]

    def body(q_ref, k_ref, v_ref, do_ref, l_ref, d_ref, dq_ref, dk_ref, dv_ref):
        i = pl.program_id(1)

        @pl.when(i == 0)
        def _():
            dk_ref[...] = jnp.zeros_like(dk_ref)
            dv_ref[...] = jnp.zeros_like(dv_ref)

        lane = lax.broadcasted_iota(jnp.int32, (1, LANES), 1)
        rowc = (lax.broadcasted_iota(jnp.int32, (WQ, BK), 0) + i * WQ) // 64
        col = lax.broadcasted_iota(jnp.int32, (WQ, BK), 1)
        qw = q_ref[...]
        dob = do_ref[...]
        dos = (dob.astype(F32) * MLA_SCALE).astype(BF16)
        lp = l_ref[...]
        dp = d_ref[...] * MLA_SCALE
        lse0, lse1 = lp[:, 0:1], lp[:, 64:65]
        dl0, dl1 = dp[:, 0:1], dp[:, 64:65]

        def probs(j, kbd, masked):
            z = _mm_nt(qw, kbd) * (MLA_SCALE * LOG2E)
            p0 = jnp.exp2(z[:, 0:BK] - lse0)
            p1 = jnp.exp2(z[:, BK:2 * BK] - lse1)
            if masked:
                valid = ((col + j * BK) // 64) <= rowc
                p0, p1 = jnp.where(valid, p0, 0.0), jnp.where(valid, p1, 0.0)
            return p0, p1

        def scores_grad(ks, p0, p1):
            dw = _mm_nt(dos, _pair_stack(v_ref[pl.ds(ks, BK), :], lane))
            return jnp.concatenate([p0 * (dw[:, 0:BK] - dl0), p1 * (dw[:, BK:2 * BK] - dl1)], axis=1).astype(BF16)

        def scatter(ks, dzb, p0, p1):
            rk = _mm_tn(dzb, qw)
            dk_ref[pl.ds(ks, BK), :] += jnp.concatenate([rk[0:BK, 0:LANES], rk[BK:2 * BK, LANES:2 * LANES]], axis=1)
            rv = _mm_tn(jnp.concatenate([p0, p1], axis=1).astype(BF16), dob)
            dv_ref[pl.ds(ks, BK), :] += jnp.where(lane < 64, rv[0:BK], rv[BK:2 * BK])

        def pair(ja, dq, masked):
            ksa, ksb = pl.multiple_of(ja * BK, BK), pl.multiple_of((ja + 1) * BK, BK)
            kbda, kbdb = _mla_keys(k_ref[pl.ds(ksa, BK), :]), _mla_keys(k_ref[pl.ds(ksb, BK), :])
            pa0, pa1 = probs(ja, kbda, masked)
            pb0, pb1 = probs(ja + 1, kbdb, masked)
            dza = scores_grad(ksa, pa0, pa1)
            dzb = scores_grad(ksb, pb0, pb1)
            dq = dq + _mm(jnp.concatenate([dza, dzb], axis=1), jnp.concatenate([kbda, kbdb], axis=0))
            scatter(ksa, dza, pa0, pa1)
            scatter(ksb, dzb, pb0, pb1)
            return dq

        dq = lax.fori_loop(0, i, lambda jj, c: pair(2 * jj, c, False), jnp.zeros((WQ, 2 * LANES), F32))
        dq_ref[...] = pair(2 * i, dq, True)

    qspec = pl.BlockSpec((WQ, 2 * LANES), lambda p, i: (i, p))
    kspec = pl.BlockSpec((s, 2 * LANES), lambda p, i: (0, p))
    vspec = pl.BlockSpec((s, LANES), lambda p, i: (0, p))
    ospec = pl.BlockSpec((WQ, LANES), lambda p, i: (i, p))
    return pl.pallas_call(
        body, name="mla_bwd", grid=(4, s // WQ),
        out_shape=(jax.ShapeDtypeStruct((s, 1024), F32), jax.ShapeDtypeStruct((s, 1024), F32),
                   jax.ShapeDtypeStruct((s, 512), F32)),
        in_specs=[qspec, kspec, vspec, ospec, ospec, ospec], out_specs=(qspec, kspec, vspec),
        compiler_params=pltpu.CompilerParams(vmem_limit_bytes=VMEM_ATTN),
    )(qc, kc, v, do, lse, delta)


def _post(x, p, tgt, sbo, mlao, sbg, mlag, gsb, gmla, wout, wout_t, gpost, wple, gple, wpg, wpg_t, bpg):
    s = x.shape[0]

    def body(x_ref, p_ref, t_ref, sbo_ref, mlao_ref, sbg_ref, mlag_ref, gsb_ref, gmla_ref, wout_ref, woutt_ref,
             gpost_ref, wple_ref, gple_ref, wpg_ref, wpgt_ref, bpg_ref,
             dsbo_ref, dmlao_ref, delta_ref, dsbg_ref, dmlag_ref, dxres_ref, dwout_ref, dwpg_ref, dwple_ref, vec_ref):
        i = pl.program_id(0)

        @pl.when(i == 0)
        def _():
            dwout_ref[...] = jnp.zeros_like(dwout_ref)
            dwpg_ref[...] = jnp.zeros_like(dwpg_ref)
            dwple_ref[...] = jnp.zeros_like(dwple_ref)
            vec_ref[...] = jnp.zeros_like(vec_ref)

        bd = _blockdiag(512, HEAD_DIM)
        inv_hd = 1.0 / HEAD_DIM

        def head_fwd(o, g, gate):
            r = lax.rsqrt(_seg(o * o, bd) * inv_hd + EPS)
            hat = o * r
            n = hat * g
            sg = _sigmoid(gate)
            return hat, r, n, sg, n * (gate * sg)

        sbo, mlao, sbg_v, mlag_v = sbo_ref[...], mlao_ref[...], sbg_ref[...], mlag_ref[...]
        gsb_v, gmla_v = gsb_ref[...], gmla_ref[...]
        sb_hat, sb_r, sb_n, sb_sg, sb_y = head_fwd(sbo, gsb_v, sbg_v)
        ml_hat, ml_r, ml_n, ml_sg, ml_y = head_fwd(mlao, gmla_v, mlag_v)
        mix = jnp.concatenate([sb_y, ml_y], axis=1).astype(BF16)
        y = _mm(mix, wout_ref[...])
        ry = lax.rsqrt(_rowmean(y * y) + EPS)
        y_hat = y * ry
        gpost_v = gpost_ref[...]
        x1 = x_ref[...] + y_hat * gpost_v
        pb = p_ref[...].astype(BF16)
        pl_ = _mm(pb, wple_ref[...])
        rp = lax.rsqrt(_rowmean(pl_ * pl_) + EPS)
        pl_hat = pl_ * rp
        gple_v = gple_ref[...]
        ple = pl_hat * gple_v
        x1b = x1.astype(BF16)
        gate = _sigmoid(_mm(x1b, wpg_ref[...]) + bpg_ref[...])
        err = x1 + ple * gate - t_ref[...]
        loss = 0.5 * jnp.sum(_rowmean(err * err))
        dout = err * (1.0 / D_MODEL)

        du = dout * ple * gate * (1.0 - gate)
        dub = du.astype(BF16)
        dple = dout * gate
        dx1 = dout + _mm(dub, wpgt_ref[...])
        dwpg_ref[...] += _mm_tn(x1b, dub)
        dplh = dple * gple_v
        dpl = rp * (dplh - pl_hat * _rowmean(dplh * pl_hat))
        dwple_ref[...] += _mm_tn(pb, dpl.astype(BF16))
        dxres_ref[...] = dx1
        dyh = dx1 * gpost_v
        dy = ry * (dyh - y_hat * _rowmean(dyh * y_hat))
        dyb = dy.astype(BF16)
        dwout_ref[...] += _mm_tn(mix, dyb)
        dmix = _mm(dyb, woutt_ref[...])

        def head_bwd(dyv, hat, r, n, sg, g, gate):
            dn = dyv * (gate * sg)
            dgate = dyv * n * (sg * (1.0 + gate * (1.0 - sg)))
            dhat = dn * g
            do = r * (dhat - hat * (_seg(dhat * hat, bd) * inv_hd))
            return do, dgate, _colsum(dn * hat)

        dsbo, dsbg, dg_sb = head_bwd(dmix[:, 0:512], sb_hat, sb_r, sb_n, sb_sg, gsb_v, sbg_v)
        dmlao, dmlag, dg_ml = head_bwd(dmix[:, 512:1024], ml_hat, ml_r, ml_n, ml_sg, gmla_v, mlag_v)
        dsbo_ref[...] = dsbo.astype(BF16)
        dmlao_ref[...] = dmlao.astype(BF16)
        delta_ref[...] = _seg(dmlao * mlao, bd)
        dsbg_ref[...] = dsbg.astype(BF16)
        dmlag_ref[...] = dmlag.astype(BF16)
        vec_ref[pl.ds(0, 1), :] += _colsum(dx1 * y_hat)
        vec_ref[pl.ds(1, 1), :] += _colsum(dple * pl_hat)
        vec_ref[pl.ds(2, 1), :] += _colsum(du)
        vec_ref[pl.ds(3, 1), :] += jnp.concatenate([dg_sb, dg_ml], axis=1)
        vec_ref[pl.ds(4, 1), :] += jnp.full((1, D_MODEL), loss, F32)

    out_shape = (
        jax.ShapeDtypeStruct((s, 512), BF16), jax.ShapeDtypeStruct((s, 512), BF16), jax.ShapeDtypeStruct((s, 512), F32),
        jax.ShapeDtypeStruct((s, 512), BF16), jax.ShapeDtypeStruct((s, 512), BF16), jax.ShapeDtypeStruct((s, D_MODEL), F32),
        jax.ShapeDtypeStruct((D_MODEL, D_MODEL), F32), jax.ShapeDtypeStruct((D_MODEL, D_MODEL), F32),
        jax.ShapeDtypeStruct((PLE_DIM, D_MODEL), F32), jax.ShapeDtypeStruct((8, D_MODEL), F32),
    )
    return pl.pallas_call(
        body, name="post_fwd_bwd", grid=(s // TM,), out_shape=out_shape,
        in_specs=[_rows(D_MODEL), _rows(PLE_DIM), _rows(D_MODEL), _rows(512), _rows(512), _rows(512), _rows(512),
                  _full((1, 512)), _full((1, 512)), _full((D_MODEL, D_MODEL)), _full((D_MODEL, D_MODEL)),
                  _full((1, D_MODEL)), _full((PLE_DIM, D_MODEL)), _full((1, D_MODEL)), _full((D_MODEL, D_MODEL)),
                  _full((D_MODEL, D_MODEL)), _full((1, D_MODEL))],
        out_specs=(_rows(512), _rows(512), _rows(512), _rows(512), _rows(512), _rows(D_MODEL),
                   _full((D_MODEL, D_MODEL)), _full((D_MODEL, D_MODEL)), _full((PLE_DIM, D_MODEL)), _full((8, D_MODEL))),
        compiler_params=pltpu.CompilerParams(vmem_limit_bytes=VMEM_DENSE),
    )(x, p, tgt, sbo, mlao, sbg, mlag, gsb, gmla, wout, wout_t, gpost, wple, gple, wpg, wpg_t, bpg)


def _pre_bwd(x, dxres, dsbq, dsbk, dsbv, dsbg, dmlag, dqc, dkc, dmv, cq, ckv, tabs, gpre, win_t, gq, wuq_t, gkv,
             wk_t, wv_t):
    s = x.shape[0]
    c_t, sa_t, sb_t = tabs

    def body(x_ref, dxres_ref, dsbq_ref, dsbk_ref, dsbv_ref, dsbg_ref, dmlag_ref, dqc_ref, dkc_ref, dmv_ref, cq_ref,
             ckv_ref, c_ref, sa_ref, sb_ref, gpre_ref, wint_ref, gq_ref, wuqt_ref, gkv_ref, wkt_ref, wvt_ref,
             gx_ref, dwin_ref, dwuq_ref, dwk_ref, dwv_ref, vec_ref, dwin_acc):
        i = pl.program_id(0)

        @pl.when(i == 0)
        def _():
            dwin_acc[...] = jnp.zeros_like(dwin_acc)
            dwuq_ref[...] = jnp.zeros_like(dwuq_ref)
            dwk_ref[...] = jnp.zeros_like(dwk_ref)
            dwv_ref[...] = jnp.zeros_like(dwv_ref)
            vec_ref[...] = jnp.zeros_like(vec_ref)

        lane = lax.broadcasted_iota(jnp.int32, (1, LANES), 1)
        c1, sa1, sb1 = c_ref[...], sa_ref[...], sb_ref[...]
        c8, sa8, sb8 = jnp.tile(c1, (1, 8)), jnp.tile(sa1, (1, 8)), jnp.tile(sb1, (1, 8))

        def norm_bwd(dn, hat, r, g):
            t = dn * g
            return r * (t - hat * _rowmean(t * hat)), _colsum(dn * hat)

        dqeb = _rope_bwd(dqc_ref[...], c8, sa8, sb8).astype(BF16)
        cq = cq_ref[...]
        rq = lax.rsqrt(_rowmean(cq * cq) + EPS)
        cq_hat = cq * rq
        gq_v = gq_ref[...]
        dwuq_ref[...] += _mm_tn((cq_hat * gq_v).astype(BF16), dqeb)
        dcq, dg_q = norm_bwd(_mm(dqeb, wuqt_ref[...]), cq_hat, rq, gq_v)

        dkc = dkc_ref[...]
        dkcb = dkc.astype(BF16)
        dmvb = dmv_ref[...].astype(BF16)
        ckv = ckv_ref[...]
        rkv = lax.rsqrt(_rowmean(ckv * ckv) + EPS)
        ckv_hat = ckv * rkv
        gkv_v = gkv_ref[...]
        ckvnb = (ckv_hat * gkv_v).astype(BF16)
        dwk_ref[...] += _mm_tn(ckvnb, dkcb)
        dwv_ref[...] += _mm_tn(ckvnb, dmvb)
        dckv, dg_kv = norm_bwd(_mm(dkcb, wkt_ref[...]) + _mm(dmvb, wvt_ref[...]), ckv_hat, rkv, gkv_v)

        dkr = dkc[:, 0:LANES]
        for hh in range(1, 8):
            dkr = dkr + dkc[:, LANES * hh:LANES * (hh + 1)]
        dkr = _rope_bwd(dkr, c1, sa1, sb1)
        dkr = jnp.where((lane >= 64) & (lane < 96), dkr, 0.0)

        dproj = jnp.concatenate([dsbq_ref[...], dsbk_ref[...].astype(BF16), dsbv_ref[...].astype(BF16), dsbg_ref[...],
                                 dcq.astype(BF16), dckv.astype(BF16), dkr.astype(BF16), dmlag_ref[...]], axis=1)
        xv = x_ref[...]
        r1 = lax.rsqrt(_rowmean(xv * xv) + EPS)
        x_hat = xv * r1
        gpre_v = gpre_ref[...]
        dwin_acc[...] += _mm_tn((x_hat * gpre_v).astype(BF16), dproj)
        dx, dg_pre = norm_bwd(_mm(dproj, wint_ref[...]), x_hat, r1, gpre_v)
        gx_ref[...] = dxres_ref[...] + dx
        vec_ref[pl.ds(0, 1), :] += dg_pre
        vec_ref[pl.ds(1, 1), :] += jnp.concatenate([dg_q, dg_kv, jnp.zeros((1, D_MODEL - Q_LORA - KV_LORA), F32)], axis=1)

        @pl.when(i == pl.num_programs(0) - 1)
        def _():
            pltpu.sync_copy(dwin_acc, dwin_ref)

    out_shape = (
        jax.ShapeDtypeStruct((s, D_MODEL), F32), jax.ShapeDtypeStruct((D_MODEL, D_EXT), F32),
        jax.ShapeDtypeStruct((Q_LORA, 1024), F32), jax.ShapeDtypeStruct((KV_LORA, 1024), F32),
        jax.ShapeDtypeStruct((KV_LORA, 512), F32), jax.ShapeDtypeStruct((8, D_MODEL), F32),
    )
    return pl.pallas_call(
        body, name="pre_bwd", grid=(s // TM,), out_shape=out_shape,
        in_specs=[_rows(D_MODEL), _rows(D_MODEL), _rows(512), _rows(512), _rows(512), _rows(512), _rows(512),
                  _rows(1024), _rows(1024), _rows(512), _rows(Q_LORA), _rows(KV_LORA), _rows(LANES), _rows(LANES),
                  _rows(LANES), _full((1, D_MODEL)), _full((D_EXT, D_MODEL)), _full((1, Q_LORA)), _full((1024, Q_LORA)),
                  _full((1, KV_LORA)), _full((1024, KV_LORA)), _full((512, KV_LORA))],
        out_specs=(_rows(D_MODEL), pl.BlockSpec(memory_space=pl.ANY), _full((Q_LORA, 1024)), _full((KV_LORA, 1024)),
                   _full((KV_LORA, 512)), _full((8, D_MODEL))),
        scratch_shapes=[pltpu.VMEM((D_MODEL, D_EXT), F32)],
        compiler_params=pltpu.CompilerParams(vmem_limit_bytes=VMEM_DENSE),
    )(x, dxres, dsbq, dsbk, dsbv, dsbg, dmlag, dqc, dkc, dmv, cq, ckv, c_t, sa_t, sb_t, gpre, win_t, gq, wuq_t, gkv,
      wk_t, wv_t)


def _place():
    return lax.axis_index("x"), lax.axis_index("y"), lax.axis_index("c")


def _allgather_weights(pack):
    def body(p_ref, out_ref, send_sems, recv_sems):
        x, y, c = _place()
        me, sib = (x, y, c), (x, y, 1 - c)
        chips = [(1 - x, y), (x, 1 - y), (1 - x, 1 - y)]

        def half(chip, hc):
            start = pl.multiple_of(hc * HALF_ROWS, 8)
            return out_ref.at[2 * chip[0] + chip[1], pl.ds(start, HALF_ROWS), :]

        def copy(k, chip, hc, to):
            return pltpu.make_async_remote_copy(src_ref=half(chip, hc), dst_ref=half(chip, hc), send_sem=send_sems.at[k],
                                                recv_sem=recv_sems.at[k], device_id=to, device_id_type=MESH)

        out_ref[2 * x + y] = p_ref[...].astype(BF16)
        first = [copy(j, (x, y), c, (*chip, c)) for j, chip in enumerate(chips)]
        for cp in first:
            cp.start()
        passed = [copy(3 + j, chip, c, sib) for j, chip in enumerate(chips)]
        for j, chip in enumerate(chips):
            copy(j, chip, c, me).wait_recv()
            passed[j].start()
        for j, chip in enumerate(chips):
            copy(3 + j, chip, 1 - c, me).wait_recv()
        for cp in first + passed:
            cp.wait_send()

    return pl.pallas_call(
        body, name="allgather_weights",
        out_shape=jax.ShapeDtypeStruct((N_SHARD, PACK_ROWS, 1024), BF16),
        in_specs=[pl.BlockSpec(memory_space=pltpu.VMEM)], out_specs=pl.BlockSpec(memory_space=pltpu.VMEM),
        scratch_shapes=[pltpu.SemaphoreType.DMA((6,)), pltpu.SemaphoreType.DMA((6,))],
        compiler_params=pltpu.CompilerParams(vmem_limit_bytes=VMEM_ATTN),
    )(pack)


def _reduce_scatter_grads(gpack, vec):
    def body(g_ref, vec_ref, f_ref, vsum_ref, acc, sib_half, send_buf, recv_buf, vrecv, local_sem, send_sems, recv_sems):
        x, y, c = _place()
        me, sib = (x, y, c), (x, y, 1 - c)
        mine = 2 * x + y
        chips = [(1 - x, y), (x, 1 - y), (1 - x, 1 - y)]

        def remote(k, src, dst, to):
            return pltpu.make_async_remote_copy(src_ref=src, dst_ref=dst, send_sem=send_sems.at[k], recv_sem=recv_sems.at[k],
                                                device_id=to, device_id_type=MESH)

        load = pltpu.make_async_copy(g_ref.at[c], acc, local_sem)
        load.start()
        to_sib = remote(0, g_ref.at[1 - c], sib_half, sib)
        to_sib.start()

        my_dev = 4 * x + 2 * y + c
        vrecv[my_dev] = vec_ref[...]
        vec_sends = []
        for k in range(1, 8):
            to = (x ^ ((k >> 2) & 1), y ^ ((k >> 1) & 1), c ^ (k & 1))
            cp = remote(k, vec_ref, vrecv.at[my_dev], to)
            cp.start()
            vec_sends.append(cp)

        load.wait()
        remote(0, g_ref.at[1 - c], sib_half, me).wait_recv()
        for k in range(N_SHARD):
            acc[k] = acc[k] + sib_half[k]

        sends = []
        for j, chip in enumerate(chips):
            idx = 2 * chip[0] + chip[1]
            send_buf[idx] = acc[idx].astype(BF16)
            cp = remote(8 + j, send_buf.at[idx], recv_buf.at[mine], (*chip, c))
            cp.start()
            sends.append(cp)
        total = acc[mine]
        for j, chip in enumerate(chips):
            idx = 2 * chip[0] + chip[1]
            remote(8 + j, send_buf.at[idx], recv_buf.at[idx], me).wait_recv()
            total = total + recv_buf[idx].astype(F32)
        start = pl.multiple_of(c * HALF_ROWS, 8)
        f_ref[pl.ds(start, HALF_ROWS), :] = total

        other = pl.multiple_of((1 - c) * HALF_ROWS, 8)
        swap = remote(11, f_ref.at[pl.ds(start, HALF_ROWS), :], f_ref.at[pl.ds(start, HALF_ROWS), :], sib)
        swap.start()
        remote(11, f_ref.at[pl.ds(other, HALF_ROWS), :], f_ref.at[pl.ds(other, HALF_ROWS), :], me).wait_recv()

        for k in range(1, 8):
            src_dev = 4 * (x ^ ((k >> 2) & 1)) + 2 * (y ^ ((k >> 1) & 1)) + (c ^ (k & 1))
            remote(k, vec_ref, vrecv.at[src_dev], me).wait_recv()
        vs = vrecv[0]
        for d in range(1, 8):
            vs = vs + vrecv[d]
        vsum_ref[...] = vs

        for cp in [to_sib, swap] + vec_sends + sends:
            cp.wait_send()

    return pl.pallas_call(
        body, name="reduce_scatter_grads",
        out_shape=(jax.ShapeDtypeStruct((PACK_ROWS, 1024), F32), jax.ShapeDtypeStruct((VEC_ROWS, 1024), F32)),
        in_specs=[pl.BlockSpec(memory_space=pl.ANY), pl.BlockSpec(memory_space=pltpu.VMEM)],
        out_specs=(pl.BlockSpec(memory_space=pltpu.VMEM), pl.BlockSpec(memory_space=pltpu.VMEM)),
        scratch_shapes=[
            pltpu.VMEM((N_SHARD, HALF_ROWS, 1024), F32), pltpu.VMEM((N_SHARD, HALF_ROWS, 1024), F32),
            pltpu.VMEM((N_SHARD, HALF_ROWS, 1024), BF16), pltpu.VMEM((N_SHARD, HALF_ROWS, 1024), BF16),
            pltpu.VMEM((8, VEC_ROWS, 1024), F32),
            pltpu.SemaphoreType.DMA, pltpu.SemaphoreType.DMA((12,)), pltpu.SemaphoreType.DMA((12,)),
        ],
        compiler_params=pltpu.CompilerParams(vmem_limit_bytes=56 * 1024 * 1024),
    )(gpack, vec)


def _adamw(w, g, m, v):
    rows, cols = w.shape
    tr = rows if rows <= 256 else 256

    def body(w_ref, g_ref, m_ref, v_ref, d_ref, nm_ref, nv_ref):
        gv = g_ref[...]
        m2 = ADAM_B1 * m_ref[...] + (1.0 - ADAM_B1) * gv
        v2 = ADAM_B2 * v_ref[...] + (1.0 - ADAM_B2) * (gv * gv)
        m_hat = m2 / (1.0 - ADAM_B1 ** ADAM_STEP)
        v_hat = v2 / (1.0 - ADAM_B2 ** ADAM_STEP)
        d_ref[...] = -ADAM_LR * (m_hat / (jnp.sqrt(v_hat) + ADAM_EPS) + ADAM_WD * w_ref[...])
        nm_ref[...] = m2
        nv_ref[...] = v2

    spec = pl.BlockSpec((tr, cols), lambda i: (i, 0))
    shp = jax.ShapeDtypeStruct((rows, cols), F32)
    return pl.pallas_call(body, name="adamw", grid=(rows // tr,), out_shape=(shp, shp, shp),
                          in_specs=[spec] * 4, out_specs=(spec,) * 3)(w, g, m, v)


_PACK_LAYOUT = (("w_in", 1024, 744), ("w_uq", 256, 192), ("w_ukv", 128, 256), ("w_out", 256, 1024), ("w_ple", 256, 256),
                ("w_ple_gate", 256, 1024))


def _pack_shard(parts):
    rows = [parts[n].reshape(-1, 1024) for n, _, _ in _PACK_LAYOUT]
    used = sum(r.shape[0] for r in rows)
    return jnp.concatenate(rows + [jnp.zeros((PACK_ROWS - used, 1024), rows[0].dtype)], axis=0)


def _unpack_shard(pack):
    out, off = {}, 0
    for n, r, cdim in _PACK_LAYOUT:
        nr = r * cdim // 1024
        out[n] = pack[..., off:off + nr, :].reshape(pack.shape[:-2] + (r, cdim))
        off += nr
    return out


def _join_shards(parts):
    cat_cols = lambda a: jnp.concatenate([a[k] for k in range(N_SHARD)], axis=1)
    cat_rows = lambda a: jnp.concatenate([a[k] for k in range(N_SHARD)], axis=0)
    return {"w_in": cat_cols(parts["w_in"]), "w_uq": cat_cols(parts["w_uq"]), "w_ukv": cat_cols(parts["w_ukv"]),
            "w_out": cat_rows(parts["w_out"]), "w_ple": cat_cols(parts["w_ple"]), "w_ple_gate": cat_rows(parts["w_ple_gate"])}


def _split_shards(full):
    cols = lambda a: jnp.stack(jnp.split(a, N_SHARD, axis=1))
    rows = lambda a: jnp.stack(jnp.split(a, N_SHARD, axis=0))
    return {"w_in": cols(full["w_in"]), "w_uq": cols(full["w_uq"]), "w_ukv": cols(full["w_ukv"]),
            "w_out": rows(full["w_out"]), "w_ple": cols(full["w_ple"]), "w_ple_gate": rows(full["w_ple_gate"])}


def _extend_weights(w):
    win = w["w_in"]
    zeros = lambda r, c: jnp.zeros((r, c), win.dtype)
    win_ext = jnp.concatenate([win[:, :2432], zeros(D_MODEL, 64), win[:, 2432:2464], zeros(D_MODEL, 32), win[:, 2464:]], axis=1)
    wuq_ext = jnp.pad(w["w_uq"].reshape(Q_LORA, 8, 96), ((0, 0), (0, 0), (0, 32))).reshape(Q_LORA, 1024)
    wukv = w["w_ukv"].reshape(KV_LORA, 8, 128)
    wk_ext = jnp.pad(wukv[:, :, :64], ((0, 0), (0, 0), (0, 64))).reshape(KV_LORA, 1024)
    wv = wukv[:, :, 64:].reshape(KV_LORA, 512)
    return win_ext, wuq_ext, wk_ext, wv


def _contract_grads(dwin_ext, dwuq_ext, dwk_ext, dwv):
    dwin = jnp.concatenate([dwin_ext[:, :2432], dwin_ext[:, 2496:2528], dwin_ext[:, 2560:]], axis=1)
    dwuq = dwuq_ext.reshape(Q_LORA, 8, 128)[:, :, :96].reshape(Q_LORA, 768)
    dwukv = jnp.concatenate([dwk_ext.reshape(KV_LORA, 8, 128)[:, :, :64], dwv.reshape(KV_LORA, 8, 64)], axis=2)
    return dwin, dwuq, dwukv.reshape(KV_LORA, 1024)


def _rope_tables(positions):
    half = QK_ROPE // 2
    freq = ROPE_THETA ** (-jnp.arange(half, dtype=F32) / half)
    ang = positions.astype(F32)[:, None] * freq
    cos, sin = jnp.cos(ang), jnp.sin(ang)
    s = positions.shape[0]
    z = lambda n: jnp.zeros((s, n), F32)
    c_t = jnp.concatenate([jnp.ones((s, 64), F32), cos, cos, z(32)], axis=1)
    sa_t = jnp.concatenate([z(64), -sin, z(16), z(32)], axis=1)
    sb_t = jnp.concatenate([z(64), z(16), sin, z(32)], axis=1)
    return c_t, sa_t, sb_t


def _local_grads(x, p, positions, tgt, gains, wfull):
    win_ext, wuq_ext, wk_ext, wv = _extend_weights(wfull)
    wout, wple, wpg = wfull["w_out"], wfull["w_ple"], wfull["w_ple_gate"]
    tabs = _rope_tables(positions)
    g = gains
    sbq, sbk, sbv, sbg, mlag, cq, ckv, qc, kc, mv = _pre_fwd(x, tabs, g["norm_pre_g"], win_ext, g["q_norm_g"], wuq_ext,
                                                             g["kv_norm_g"], wk_ext, wv)
    sbo, sbt = _sb_fwd(sbq, sbk, sbv)
    mlao, lse = _mla_fwd(qc, kc, mv)
    dsbo, dmlao, delta, dsbg, dmlag, dxres, dwout, dwpg, dwple, vec_c = _post(
        x, p, tgt, sbo, mlao, sbg, mlag, g["sb_out_norm_g"], g["mla_out_norm_g"], wout, wout.T, g["norm_post_g"], wple,
        g["ple_norm_g"], wpg, wpg.T, g["b_ple_gate"])
    dsbq, dsbk, dsbv = _sb_bwd(sbq, sbk, sbv, dsbo, sbt)
    dqc, dkc, dmv = _mla_bwd(qc, kc, mv, dmlao, lse, delta)
    gx, dwin_ext, dwuq_ext, dwk_ext, dwv, vec_d = _pre_bwd(
        x, dxres, dsbq, dsbk, dsbv, dsbg, dmlag, dqc, dkc, dmv, cq, ckv, tabs, g["norm_pre_g"], win_ext.T, g["q_norm_g"],
        wuq_ext.T, g["kv_norm_g"], wk_ext.T, wv.T)
    dwin, dwuq, dwukv = _contract_grads(dwin_ext, dwuq_ext, dwk_ext, dwv)
    grads = {"w_in": dwin, "w_uq": dwuq, "w_ukv": dwukv, "w_out": dwout, "w_ple": dwple, "w_ple_gate": dwpg}
    return gx, grads, jnp.concatenate([vec_c, vec_d], axis=0)


_VEC_LAYOUT = (("norm_post_g", 0, 0, 1024), ("ple_norm_g", 1, 0, 1024), ("b_ple_gate", 2, 0, 1024), ("sb_out_norm_g", 3, 0, 512),
               ("mla_out_norm_g", 3, 512, 512), ("norm_pre_g", 8, 0, 1024), ("q_norm_g", 9, 0, 256), ("kv_norm_g", 9, 256, 128))
_LOSS_ROW = 4
_WEIGHT_ORDER = ("norm_pre_g", "w_in", "q_norm_g", "w_uq", "kv_norm_g", "w_ukv", "sb_out_norm_g", "mla_out_norm_g", "w_out",
                 "norm_post_g", "w_ple", "ple_norm_g", "w_ple_gate", "b_ple_gate")


def _vec_block(named):
    blk = jnp.zeros((VEC_ROWS, 1024), F32)
    for n, r, c0, width in _VEC_LAYOUT:
        blk = blk.at[r, c0:c0 + width].set(named[n][0])
    return blk


def kernel(x, p, positions, norm_pre_g, w_in, q_norm_g, w_uq, kv_norm_g, w_ukv, sb_out_norm_g, mla_out_norm_g, w_out, norm_post_g, w_ple, ple_norm_g, w_ple_gate, b_ple_gate, loss_target, m_norm_pre_g, m_w_in, m_q_norm_g, m_w_uq, m_kv_norm_g, m_w_ukv, m_sb_out_norm_g, m_mla_out_norm_g, m_w_out, m_norm_post_g, m_w_ple, m_ple_norm_g, m_w_ple_gate, m_b_ple_gate, v_norm_pre_g, v_w_in, v_q_norm_g, v_w_uq, v_kv_norm_g, v_w_ukv, v_sb_out_norm_g, v_mla_out_norm_g, v_w_out, v_norm_post_g, v_w_ple, v_ple_norm_g, v_w_ple_gate, v_b_ple_gate):
    w = {"norm_pre_g": norm_pre_g, "w_in": w_in[0], "q_norm_g": q_norm_g, "w_uq": w_uq[0], "kv_norm_g": kv_norm_g, "w_ukv": w_ukv[0],
         "sb_out_norm_g": sb_out_norm_g, "mla_out_norm_g": mla_out_norm_g, "w_out": w_out[0], "norm_post_g": norm_post_g,
         "w_ple": w_ple[0], "ple_norm_g": ple_norm_g, "w_ple_gate": w_ple_gate[0], "b_ple_gate": b_ple_gate}
    m = {"norm_pre_g": m_norm_pre_g, "w_in": m_w_in[0], "q_norm_g": m_q_norm_g, "w_uq": m_w_uq[0], "kv_norm_g": m_kv_norm_g,
         "w_ukv": m_w_ukv[0], "sb_out_norm_g": m_sb_out_norm_g, "mla_out_norm_g": m_mla_out_norm_g, "w_out": m_w_out[0],
         "norm_post_g": m_norm_post_g, "w_ple": m_w_ple[0], "ple_norm_g": m_ple_norm_g, "w_ple_gate": m_w_ple_gate[0],
         "b_ple_gate": m_b_ple_gate}
    v = {"norm_pre_g": v_norm_pre_g, "w_in": v_w_in[0], "q_norm_g": v_q_norm_g, "w_uq": v_w_uq[0], "kv_norm_g": v_kv_norm_g,
         "w_ukv": v_w_ukv[0], "sb_out_norm_g": v_sb_out_norm_g, "mla_out_norm_g": v_mla_out_norm_g, "w_out": v_w_out[0],
         "norm_post_g": v_norm_post_g, "w_ple": v_w_ple[0], "ple_norm_g": v_ple_norm_g, "w_ple_gate": v_w_ple_gate[0],
         "b_ple_gate": v_b_ple_gate}
    big = [n for n, _, _ in _PACK_LAYOUT]

    gathered = _allgather_weights(_pack_shard({n: w[n] for n in big}))
    wfull = _join_shards(_unpack_shard(gathered))

    gx, grads, vec = _local_grads(x[0], p[0, 0], positions[0], loss_target[0], w, wfull)

    gsh = _split_shards(grads)
    gpack = jnp.stack([_pack_shard({n: gsh[n][k] for n in big}) for k in range(N_SHARD)])
    gpack = gpack.reshape(N_SHARD, 2, HALF_ROWS, 1024).transpose(1, 0, 2, 3)
    gred, vsum = _reduce_scatter_grads(gpack, vec)
    g = _unpack_shard(gred)
    for n, r, c0, width in _VEC_LAYOUT:
        g[n] = vsum[r:r + 1, c0:c0 + width]
    loss = vsum[_LOSS_ROW, 0]

    delta, new_m, new_v = {}, {}, {}
    for n in big:
        delta[n], new_m[n], new_v[n] = _adamw(w[n], g[n], m[n], v[n])
    small = [n for n in _WEIGHT_ORDER if n not in big]
    dv, mv_, vv = _adamw(_vec_block({n: w[n] for n in small}), vsum * _vec_block({n: jnp.ones_like(w[n]) for n in small}),
                         _vec_block({n: m[n] for n in small}), _vec_block({n: v[n] for n in small}))
    for n, r, c0, width in _VEC_LAYOUT:
        delta[n], new_m[n], new_v[n] = dv[r:r + 1, c0:c0 + width], mv_[r:r + 1, c0:c0 + width], vv[r:r + 1, c0:c0 + width]

    lead = lambda n, a: a[None] if n in big else a
    return (loss, gx[None],
            *[lead(n, g[n]) for n in _WEIGHT_ORDER], *[lead(n, delta[n]) for n in _WEIGHT_ORDER],
            *[lead(n, new_m[n]) for n in _WEIGHT_ORDER], *[lead(n, new_v[n]) for n in _WEIGHT_ORDER])
```

```python
import functools

import jax
import jax.numpy as jnp
from jax import lax
from jax.experimental import pallas as pl
from jax.experimental.pallas import tpu as pltpu

F32 = jnp.float32
BF16 = jnp.bfloat16
MESH = pl.DeviceIdType.MESH

D_MODEL = 1024
HEAD_DIM = 64
D_SB = 512
D_MLA = 512
Q_LORA = 256
KV_LORA = 128
QK_NOPE = 64
QK_ROPE = 32
PLE_DIM = 256
D_IN = 2976
D_EXT = 3072
ROPE_THETA = 10000.0
EPS = 1e-6
N_SHARD = 4

ADAM_LR = 0.001
ADAM_B1 = 0.9
ADAM_B2 = 0.999
ADAM_EPS = 1e-08
ADAM_WD = 0.01
ADAM_STEP = 10

LANES = 128
BK = 128
WQ = 256
SB_CUTOFF = 120.0
TM = 256
PACK_ROWS = 1408
HALF_ROWS = PACK_ROWS // 2
VEC_ROWS = 16
VMEM_DENSE = 52 * 1024 * 1024
VMEM_ATTN = 40 * 1024 * 1024


def _mm(a, b):
    return jnp.dot(a, b, preferred_element_type=F32)


def _mm_nt(a, b):
    return lax.dot_general(a, b, (((1,), (1,)), ((), ())), preferred_element_type=F32)


def _mm_tn(a, b):
    return lax.dot_general(a, b, (((0,), (0,)), ((), ())), preferred_element_type=F32)


def _seg(a, bd):
    a1 = a.astype(BF16)
    r1 = a - a1.astype(F32)
    a2 = r1.astype(BF16)
    a3 = (r1 - a2.astype(F32)).astype(BF16)
    return _mm(a1, bd) + _mm(a2, bd) + _mm(a3, bd)


def _blockdiag(n, seg):
    r = lax.broadcasted_iota(jnp.int32, (n, n), 0) // seg
    c = lax.broadcasted_iota(jnp.int32, (n, n), 1) // seg
    return jnp.where(r == c, 1.0, 0.0).astype(BF16)


def _sigmoid(a):
    return 1.0 / (1.0 + jnp.exp(-a))


def _rowmean(a):
    return jnp.mean(a, axis=-1, keepdims=True)


def _colsum(a):
    return jnp.sum(a, axis=0, keepdims=True)


def _rope_fwd(a, c, sa, sb):
    w = a.shape[-1]
    return a * c + pltpu.roll(a, w - 16, 1) * sa + pltpu.roll(a, 16, 1) * sb


def _rope_bwd(g, c, sa, sb):
    w = g.shape[-1]
    return g * c + pltpu.roll(g * sa, 16, 1) + pltpu.roll(g * sb, w - 16, 1)


def _full(shape):
    return pl.BlockSpec(shape, lambda *_: (0,) * len(shape))


def _rows(width, tm=TM):
    return pl.BlockSpec((tm, width), lambda i: (i, 0))


def _pre_fwd(x, tabs, gpre, win, gq, wuq, gkv, wk, wv):
    s = x.shape[0]
    c_t, sa_t, sb_t = tabs

    def body(x_ref, c_ref, sa_ref, sb_ref, gpre_ref, win_ref, gq_ref, wuq_ref, gkv_ref, wk_ref, wv_ref,
             sbq_ref, sbk_ref, sbv_ref, sbg_ref, mlag_ref, cq_ref, ckv_ref, qc_ref, kc_ref, mv_ref):
        xv = x_ref[...]
        r1 = lax.rsqrt(_rowmean(xv * xv) + EPS)
        h = (xv * r1 * gpre_ref[...]).astype(BF16)
        proj = _mm(h, win_ref[...])
        sbq_ref[...] = proj[:, 0:512].astype(BF16)
        sbk_ref[...] = proj[:, 512:1024].astype(BF16)
        sbv_ref[...] = proj[:, 1024:1536].astype(BF16)
        sbg_ref[...] = proj[:, 1536:2048]
        cq = proj[:, 2048:2304]
        ckv = proj[:, 2304:2432]
        kr = proj[:, 2432:2560]
        mlag_ref[...] = proj[:, 2560:3072]
        cq_ref[...] = cq
        ckv_ref[...] = ckv
        c1, sa1, sb1 = c_ref[...], sa_ref[...], sb_ref[...]
        c8, sa8, sb8 = jnp.tile(c1, (1, 8)), jnp.tile(sa1, (1, 8)), jnp.tile(sb1, (1, 8))
        cqn = (cq * lax.rsqrt(_rowmean(cq * cq) + EPS) * gq_ref[...]).astype(BF16)
        qe = _mm(cqn, wuq_ref[...])
        qc_ref[...] = _rope_fwd(qe, c8, sa8, sb8).astype(BF16)
        ckvn = (ckv * lax.rsqrt(_rowmean(ckv * ckv) + EPS) * gkv_ref[...]).astype(BF16)
        ke = _mm(ckvn, wk_ref[...])
        krr = _rope_fwd(kr, c1, sa1, sb1)
        kc_ref[...] = (ke + jnp.tile(krr, (1, 8))).astype(BF16)
        mv_ref[...] = _mm(ckvn, wv_ref[...]).astype(BF16)

    out_shape = (
        jax.ShapeDtypeStruct((s, 512), BF16), jax.ShapeDtypeStruct((s, 512), BF16), jax.ShapeDtypeStruct((s, 512), BF16),
        jax.ShapeDtypeStruct((s, 512), F32), jax.ShapeDtypeStruct((s, 512), F32),
        jax.ShapeDtypeStruct((s, Q_LORA), F32), jax.ShapeDtypeStruct((s, KV_LORA), F32),
        jax.ShapeDtypeStruct((s, 1024), BF16), jax.ShapeDtypeStruct((s, 1024), BF16), jax.ShapeDtypeStruct((s, 512), BF16),
    )
    return pl.pallas_call(
        body, name="pre_fwd", grid=(s // TM,), out_shape=out_shape,
        in_specs=[_rows(D_MODEL), _rows(LANES), _rows(LANES), _rows(LANES), _full((1, D_MODEL)), _full((D_MODEL, D_EXT)),
                  _full((1, Q_LORA)), _full((Q_LORA, 1024)), _full((1, KV_LORA)), _full((KV_LORA, 1024)), _full((KV_LORA, 512))],
        out_specs=(_rows(512), _rows(512), _rows(512), _rows(512), _rows(512), _rows(Q_LORA), _rows(KV_LORA),
                   _rows(1024), _rows(1024), _rows(512)),
        compiler_params=pltpu.CompilerParams(vmem_limit_bytes=VMEM_DENSE),
    )(x, c_t, sa_t, sb_t, gpre, win, gq, wuq, gkv, wk, wv)


def _softplus(z):
    neg_abs = lax.bitcast_convert_type(lax.bitcast_convert_type(z, jnp.uint32) | jnp.uint32(0x80000000), F32)
    return jnp.maximum(z, 0.0) + jnp.log(1.0 + jnp.exp(neg_abs))


def _pair_tri(kind):
    r = lax.broadcasted_iota(jnp.int32, (512, 256), 0)
    c = lax.broadcasted_iota(jnp.int32, (512, 256), 1)
    same = ((r // BK) % 2) == (c // BK)
    rk, ck = r % BK, c % BK
    m = {"ge": rk >= ck, "lt": rk < ck, "le": rk <= ck}[kind]
    return jnp.where(same & m, 1.0, 0.0).astype(BF16)


def _split2(a):
    hi = a.astype(BF16)
    lo = (a - hi.astype(F32)).astype(BF16)
    return jnp.concatenate([hi, lo], axis=1)


def _pair_stack(b, lane):
    zero = jnp.zeros_like(b)
    return jnp.concatenate([jnp.where(lane < 64, b, zero), jnp.where(lane >= 64, b, zero)], axis=0)


def _sb_fwd(q, k, v):
    s = q.shape[0]

    def body(q_ref, k_ref, v_ref, o_ref):
        i = pl.program_id(1)
        lane = lax.broadcasted_iota(jnp.int32, (1, LANES), 1)
        row = lax.broadcasted_iota(jnp.int32, (WQ, 2 * BK), 0) + i * WQ
        col = lax.broadcasted_iota(jnp.int32, (WQ, 2 * BK), 1) % BK
        u_ge = _pair_tri("ge")
        qs = q_ref[...] * (HEAD_DIM ** -0.5)

        def scores(j):
            ks = pl.multiple_of(j * BK, BK)
            return _mm_nt(qs, _pair_stack(k_ref[pl.ds(ks, BK), :], lane))

        def decay(z, j, masked):
            sp = _softplus(z)
            if masked:
                sp = jnp.where((col + j * BK) < row, sp, 0.0)
            return _mm(_split2(sp), u_ge)

        def weights(z, cum, j, masked):
            w = jnp.exp(z - cum)
            if masked:
                w = jnp.where((col + j * BK) < row, w, 0.0)
            ks = pl.multiple_of(j * BK, BK)
            return _mm(w.astype(BF16), _pair_stack(v_ref[pl.ds(ks, BK), :], lane))

        def fold(carry, pv, cum):
            acc, run = carry
            return acc + jnp.exp(-run) * pv, run + _heads_narrow(cum, 0, lane)

        def pair(ja, jb, carry, masked):
            za, zb = scores(ja), scores(jb)
            ca = decay(za, ja, masked)
            cb = decay(zb, jb, masked)
            pa = weights(za, ca, ja, masked)
            pb = weights(zb, cb, jb, masked)
            return fold(fold(carry, pa, ca), pb, cb)

        assert WQ == 2 * BK
        acc, run = pair(2 * i + 1, 2 * i, (jnp.zeros((WQ, LANES), F32), jnp.zeros((WQ, LANES), F32)), True)

        def more(c):
            return (c[0] < i) & (c[1] > 0)

        def step(c):
            jj, _, acc, run = c
            acc, run = pair(2 * i - 1 - 2 * jj, 2 * i - 2 - 2 * jj, (acc, run), False)
            return jj + 1, (jnp.min(run) < SB_CUTOFF).astype(jnp.int32), acc, run

        _, _, acc, _ = lax.while_loop(more, step, (jnp.int32(0), (jnp.min(run) < SB_CUTOFF).astype(jnp.int32), acc, run))
        o_ref[...] = acc

    qspec = pl.BlockSpec((WQ, LANES), lambda p, i: (i, p))
    kspec = pl.BlockSpec((s, LANES), lambda p, i: (0, p))
    return pl.pallas_call(
        body, name="sb_fwd", grid=(4, s // WQ),
        out_shape=jax.ShapeDtypeStruct((s, 512), F32),
        in_specs=[qspec, kspec, kspec], out_specs=qspec,
        compiler_params=pltpu.CompilerParams(vmem_limit_bytes=VMEM_ATTN),
    )(q, k, v)


def _pair_tri1(kind):
    r = lax.broadcasted_iota(jnp.int32, (2 * BK, 2 * BK), 0)
    c = lax.broadcasted_iota(jnp.int32, (2 * BK, 2 * BK), 1)
    rk, ck = r % BK, c % BK
    m = {"ge": rk >= ck, "lt": rk < ck, "le": rk <= ck}[kind]
    return jnp.where(((r // BK) == (c // BK)) & m, 1.0, 0.0).astype(BF16)


def _heads_wide(a):
    m = a.shape[0]
    return jnp.concatenate([jnp.broadcast_to(a[:, 0:1], (m, BK)), jnp.broadcast_to(a[:, 64:65], (m, BK))], axis=1)


def _heads_narrow(a, col, lane):
    return jnp.where(lane < 64, a[:, col:col + 1], a[:, BK + col:BK + col + 1])


def _pair_rowsum():
    r = lax.broadcasted_iota(jnp.int32, (512, LANES), 0)
    c = lax.broadcasted_iota(jnp.int32, (512, LANES), 1)
    return jnp.where(((r // BK) % 2) == (c // 64), 1.0, 0.0).astype(BF16)


def _sb_bwd(q, k, v, do):
    s = q.shape[0]
    assert WQ == 2 * BK

    def body(q_ref, k_ref, v_ref, do_ref, dq_ref, dk_ref, dv_ref, later_ref):
        i = pl.program_id(1)

        @pl.when(i == 0)
        def _():
            dk_ref[...] = jnp.zeros_like(dk_ref)
            dv_ref[...] = jnp.zeros_like(dv_ref)

        lane = lax.broadcasted_iota(jnp.int32, (1, LANES), 1)
        row = lax.broadcasted_iota(jnp.int32, (WQ, 2 * BK), 0) + i * WQ
        col = lax.broadcasted_iota(jnp.int32, (WQ, 2 * BK), 1) % BK
        u_ge = _pair_tri("ge")
        u_le = _pair_tri1("le")
        r_sum = _pair_rowsum()
        qs = q_ref[...] * (HEAD_DIM ** -0.5)
        dof = do_ref[...].astype(F32)

        def scan(ja, jb, run, masked):
            za = _mm_nt(qs, _pair_stack(k_ref[pl.ds(pl.multiple_of(ja * BK, BK), BK), :], lane))
            zb = _mm_nt(qs, _pair_stack(k_ref[pl.ds(pl.multiple_of(jb * BK, BK), BK), :], lane))
            spa, spb = _softplus(za), _softplus(zb)
            if masked:
                spa = jnp.where((col + ja * BK) < row, spa, 0.0)
                spb = jnp.where((col + jb * BK) < row, spb, 0.0)
            rsa, rsb = _mm(_split2(spa), r_sum), _mm(_split2(spb), r_sum)
            later_ref[ja] = run
            later_ref[jb] = run + rsa
            return run + rsa + rsb

        run = scan(2 * i + 1, 2 * i, jnp.zeros((WQ, LANES), F32), True)

        def more(c):
            return (c[0] < i) & (c[1] > 0)

        def step(c):
            run = scan(2 * i - 1 - 2 * c[0], 2 * i - 2 - 2 * c[0], c[2], False)
            return c[0] + 1, (jnp.min(run) < SB_CUTOFF).astype(jnp.int32), run

        npairs, _, _ = lax.while_loop(more, step, (jnp.int32(0), (jnp.min(run) < SB_CUTOFF).astype(jnp.int32), run))

        def keys(j):
            ks = pl.multiple_of(j * BK, BK)
            return ks, _pair_stack(k_ref[pl.ds(ks, BK), :], lane)

        def gates(z, j, masked):
            neg_abs = lax.bitcast_convert_type(lax.bitcast_convert_type(z, jnp.uint32) | jnp.uint32(0x80000000), F32)
            u = jnp.exp(neg_abs)
            opu = 1.0 + u
            sp = jnp.maximum(z, 0.0) + jnp.log(opu)
            if masked:
                sp = jnp.where((col + j * BK) < row, sp, 0.0)
            return jnp.where(z >= 0.0, 1.0, u) / opu, _mm(_split2(sp), u_ge)

        def weights(z, cum, ks, j, masked):
            wl = jnp.exp(z - cum)
            if masked:
                wl = jnp.where((col + j * BK) < row, wl, 0.0)
            dfo = (jnp.exp(-later_ref[j]) * dof).astype(BF16)
            e = _mm_nt(dfo, _pair_stack(v_ref[pl.ds(ks, BK), :], lane)) * wl
            return wl.astype(BF16), dfo, e

        def scores_grad(e, sig, erun, j, masked):
            ecum = _mm(e.astype(BF16), u_le) + _heads_wide(erun)
            dz = e - sig * ecum
            if masked:
                dz = jnp.where((col + j * BK) < row, dz, 0.0)
            return dz.astype(BF16), _heads_narrow(ecum, BK - 1, lane)

        def scatter(ks, dzb, wlb, dfo):
            rk = _mm_tn(dzb, qs)
            dk_ref[pl.ds(ks, BK), :] += jnp.where(lane < 64, rk[0:BK], rk[BK:2 * BK])
            rv = _mm_tn(wlb, dfo)
            dv_ref[pl.ds(ks, BK), :] += jnp.where(lane < 64, rv[0:BK], rv[BK:2 * BK])

        def pair(ja, carry, masked):
            dq, erun = carry
            jb = ja + 1
            ksa, kbda = keys(ja)
            ksb, kbdb = keys(jb)
            za, zb = _mm_nt(qs, kbda), _mm_nt(qs, kbdb)
            siga, cuma = gates(za, ja, masked)
            sigb, cumb = gates(zb, jb, masked)
            wla, dfoa, ea = weights(za, cuma, ksa, ja, masked)
            wlb, dfob, eb = weights(zb, cumb, ksb, jb, masked)
            dza, eruna = scores_grad(ea, siga, erun, ja, masked)
            dzb, erunb = scores_grad(eb, sigb, eruna, jb, masked)
            dq = dq + _mm(jnp.concatenate([dza, dzb], axis=1), jnp.concatenate([kbda, kbdb], axis=0))
            scatter(ksa, dza, wla, dfoa)
            scatter(ksb, dzb, wlb, dfob)
            return dq, erunb

        zero = jnp.zeros((WQ, LANES), F32)
        first = 2 * (i - npairs)
        carry = lax.fori_loop(0, npairs, lambda t, c: pair(first + 2 * t, c, False), (zero, zero))
        dq, _ = pair(2 * i, carry, True)
        dq_ref[...] = (dq * (HEAD_DIM ** -0.5)).astype(BF16)

    qspec = pl.BlockSpec((WQ, LANES), lambda p, i: (i, p))
    kspec = pl.BlockSpec((s, LANES), lambda p, i: (0, p))
    return pl.pallas_call(
        body, name="sb_bwd", grid=(4, s // WQ),
        out_shape=(jax.ShapeDtypeStruct((s, 512), BF16), jax.ShapeDtypeStruct((s, 512), F32),
                   jax.ShapeDtypeStruct((s, 512), F32)),
        in_specs=[qspec, kspec, kspec, qspec], out_specs=(qspec, kspec, kspec),
        scratch_shapes=[pltpu.VMEM((s // BK, WQ, LANES), F32)],
        compiler_params=pltpu.CompilerParams(vmem_limit_bytes=VMEM_ATTN),
    )(q, k, v, do)


MLA_SCALE = (QK_NOPE + QK_ROPE) ** -0.5
LOG2E = 1.4426950408889634


def _mla_keys(kb):
    zero = jnp.zeros((BK, LANES), kb.dtype)
    return jnp.concatenate([jnp.concatenate([kb[:, 0:LANES], zero], axis=1),
                            jnp.concatenate([zero, kb[:, LANES:2 * LANES]], axis=1)], axis=0)


def _mla_fwd(qc, kc, v):
    s = qc.shape[0]

    def body(q_ref, k_ref, v_ref, o_ref, l_ref):
        i = pl.program_id(1)
        lane = lax.broadcasted_iota(jnp.int32, (1, LANES), 1)
        rowc = (lax.broadcasted_iota(jnp.int32, (WQ, BK), 0) + i * WQ) // 64
        col = lax.broadcasted_iota(jnp.int32, (WQ, BK), 1)
        qw = q_ref[...]
        ind0 = jnp.broadcast_to(jnp.where(lane < 64, 1.0, 0.0).astype(BF16), (BK, LANES))
        ind1 = jnp.broadcast_to(jnp.where(lane >= 64, 1.0, 0.0).astype(BF16), (BK, LANES))

        def scores(j, masked):
            ks = pl.multiple_of(j * BK, BK)
            z = _mm_nt(qw, _mla_keys(k_ref[pl.ds(ks, BK), :])) * (MLA_SCALE * LOG2E)
            z0, z1 = z[:, 0:BK], z[:, BK:2 * BK]
            if masked:
                valid = ((col + j * BK) // 64) <= rowc
                z0, z1 = jnp.where(valid, z0, -1e30), jnp.where(valid, z1, -1e30)
            return z0, z1

        def values(j):
            ks = pl.multiple_of(j * BK, BK)
            vb = v_ref[pl.ds(ks, BK), :]
            zero = jnp.zeros_like(vb)
            return jnp.concatenate([jnp.concatenate([jnp.where(lane < 64, vb, zero), ind0], axis=1),
                                    jnp.concatenate([jnp.where(lane >= 64, vb, zero), ind1], axis=1)], axis=0)

        def pair(ja, carry, masked):
            m0, m1, l, acc = carry
            a0, a1 = scores(ja, masked)
            b0, b1 = scores(ja + 1, masked)
            n0 = jnp.maximum(m0, jnp.maximum(jnp.max(a0, axis=1, keepdims=True), jnp.max(b0, axis=1, keepdims=True)))
            n1 = jnp.maximum(m1, jnp.maximum(jnp.max(a1, axis=1, keepdims=True), jnp.max(b1, axis=1, keepdims=True)))
            p = jnp.concatenate([jnp.exp2(a0 - n0), jnp.exp2(a1 - n1), jnp.exp2(b0 - n0), jnp.exp2(b1 - n1)], axis=1)
            pv = _mm(p.astype(BF16), jnp.concatenate([values(ja), values(ja + 1)], axis=0))
            a = jnp.where(lane < 64, jnp.exp2(m0 - n0), jnp.exp2(m1 - n1))
            return n0, n1, a * l + pv[:, LANES:2 * LANES], a * acc + pv[:, 0:LANES]

        neg = jnp.full((WQ, 1), -1e30, F32)
        zero = jnp.zeros((WQ, LANES), F32)
        carry = lax.fori_loop(0, i, lambda jj, c: pair(2 * jj, c, False), (neg, neg, zero, zero))
        m0, m1, l, acc = pair(2 * i, carry, True)
        o_ref[...] = acc / l
        l_ref[...] = jnp.where(lane < 64, m0, m1) + jnp.log2(l)

    qspec = pl.BlockSpec((WQ, 2 * LANES), lambda p, i: (i, p))
    kspec = pl.BlockSpec((s, 2 * LANES), lambda p, i: (0, p))
    vspec = pl.BlockSpec((s, LANES), lambda p, i: (0, p))
    ospec = pl.BlockSpec((WQ, LANES), lambda p, i: (i, p))
    return pl.pallas_call(
        body, name="mla_fwd", grid=(4, s // WQ),
        out_shape=(jax.ShapeDtypeStruct((s, 512), F32), jax.ShapeDtypeStruct((s, 512), F32)),
        in_specs=[qspec, kspec, vspec], out_specs=(ospec, ospec),
        compiler_params=pltpu.CompilerParams(vmem_limit_bytes=VMEM_ATTN),
    )(qc, kc, v)


def _mla_bwd(qc, kc, v, do, lse, delta):
    s = qc.shape[0]

    def body(q_ref, k_ref, v_ref, do_ref, l_ref, d_ref, dq_ref, dk_ref, dv_ref):
        i = pl.program_id(1)

        @pl.when(i == 0)
        def _():
            dk_ref[...] = jnp.zeros_like(dk_ref)
            dv_ref[...] = jnp.zeros_like(dv_ref)

        lane = lax.broadcasted_iota(jnp.int32, (1, LANES), 1)
        rowc = (lax.broadcasted_iota(jnp.int32, (WQ, BK), 0) + i * WQ) // 64
        col = lax.broadcasted_iota(jnp.int32, (WQ, BK), 1)
        qw = q_ref[...]
        dob = do_ref[...]
        dos = (dob.astype(F32) * MLA_SCALE).astype(BF16)
        lp = l_ref[...]
        dp = d_ref[...] * MLA_SCALE
        lse0, lse1 = lp[:, 0:1], lp[:, 64:65]
        dl0, dl1 = dp[:, 0:1], dp[:, 64:65]

        def probs(j, kbd, masked):
            z = _mm_nt(qw, kbd) * (MLA_SCALE * LOG2E)
            p0 = jnp.exp2(z[:, 0:BK] - lse0)
            p1 = jnp.exp2(z[:, BK:2 * BK] - lse1)
            if masked:
                valid = ((col + j * BK) // 64) <= rowc
                p0, p1 = jnp.where(valid, p0, 0.0), jnp.where(valid, p1, 0.0)
            return p0, p1

        def scores_grad(ks, p0, p1):
            dw = _mm_nt(dos, _pair_stack(v_ref[pl.ds(ks, BK), :], lane))
            return jnp.concatenate([p0 * (dw[:, 0:BK] - dl0), p1 * (dw[:, BK:2 * BK] - dl1)], axis=1).astype(BF16)

        def scatter(ks, dzb, p0, p1):
            rk = _mm_tn(dzb, qw)
            dk_ref[pl.ds(ks, BK), :] += jnp.concatenate([rk[0:BK, 0:LANES], rk[BK:2 * BK, LANES:2 * LANES]], axis=1)
            rv = _mm_tn(jnp.concatenate([p0, p1], axis=1).astype(BF16), dob)
            dv_ref[pl.ds(ks, BK), :] += jnp.where(lane < 64, rv[0:BK], rv[BK:2 * BK])

        def pair(ja, dq, masked):
            ksa, ksb = pl.multiple_of(ja * BK, BK), pl.multiple_of((ja + 1) * BK, BK)
            kbda, kbdb = _mla_keys(k_ref[pl.ds(ksa, BK), :]), _mla_keys(k_ref[pl.ds(ksb, BK), :])
            pa0, pa1 = probs(ja, kbda, masked)
            pb0, pb1 = probs(ja + 1, kbdb, masked)
            dza = scores_grad(ksa, pa0, pa1)
            dzb = scores_grad(ksb, pb0, pb1)
            dq = dq + _mm(jnp.concatenate([dza, dzb], axis=1), jnp.concatenate([kbda, kbdb], axis=0))
            scatter(ksa, dza, pa0, pa1)
            scatter(ksb, dzb, pb0, pb1)
            return dq

        dq = lax.fori_loop(0, i, lambda jj, c: pair(2 * jj, c, False), jnp.zeros((WQ, 2 * LANES), F32))
        dq_ref[...] = pair(2 * i, dq, True)

    qspec = pl.BlockSpec((WQ, 2 * LANES), lambda p, i: (i, p))
    kspec = pl.BlockSpec((s, 2 * LANES), lambda p, i: (0, p))
    vspec = pl.BlockSpec((s, LANES), lambda p, i: (0, p))
    ospec = pl.BlockSpec((WQ, LANES), lambda p, i: (i, p))
    return pl.pallas_call(
        body, name="mla_bwd", grid=(4, s // WQ),
        out_shape=(jax.ShapeDtypeStruct((s, 1024), F32), jax.ShapeDtypeStruct((s, 1024), F32),
                   jax.ShapeDtypeStruct((s, 512), F32)),
        in_specs=[qspec, kspec, vspec, ospec, ospec, ospec], out_specs=(qspec, kspec, vspec),
        compiler_params=pltpu.CompilerParams(vmem_limit_bytes=VMEM_ATTN),
    )(qc, kc, v, do, lse, delta)


def _post(x, p, tgt, sbo, mlao, sbg, mlag, gsb, gmla, wout, wout_t, gpost, wple, gple, wpg, wpg_t, bpg):
    s = x.shape[0]

    def body(x_ref, p_ref, t_ref, sbo_ref, mlao_ref, sbg_ref, mlag_ref, gsb_ref, gmla_ref, wout_ref, woutt_ref,
             gpost_ref, wple_ref, gple_ref, wpg_ref, wpgt_ref, bpg_ref,
             dsbo_ref, dmlao_ref, delta_ref, dsbg_ref, dmlag_ref, dxres_ref, dwout_ref, dwpg_ref, dwple_ref, vec_ref):
        i = pl.program_id(0)

        @pl.when(i == 0)
        def _():
            dwout_ref[...] = jnp.zeros_like(dwout_ref)
            dwpg_ref[...] = jnp.zeros_like(dwpg_ref)
            dwple_ref[...] = jnp.zeros_like(dwple_ref)
            vec_ref[...] = jnp.zeros_like(vec_ref)

        bd = _blockdiag(512, HEAD_DIM)
        inv_hd = 1.0 / HEAD_DIM

        def head_fwd(o, g, gate):
            r = lax.rsqrt(_seg(o * o, bd) * inv_hd + EPS)
            hat = o * r
            n = hat * g
            sg = _sigmoid(gate)
            return hat, r, n, sg, n * (gate * sg)

        sbo, mlao, sbg_v, mlag_v = sbo_ref[...], mlao_ref[...], sbg_ref[...], mlag_ref[...]
        gsb_v, gmla_v = gsb_ref[...], gmla_ref[...]
        sb_hat, sb_r, sb_n, sb_sg, sb_y = head_fwd(sbo, gsb_v, sbg_v)
        ml_hat, ml_r, ml_n, ml_sg, ml_y = head_fwd(mlao, gmla_v, mlag_v)
        mix = jnp.concatenate([sb_y, ml_y], axis=1).astype(BF16)
        y = _mm(mix, wout_ref[...])
        ry = lax.rsqrt(_rowmean(y * y) + EPS)
        y_hat = y * ry
        gpost_v = gpost_ref[...]
        x1 = x_ref[...] + y_hat * gpost_v
        pb = p_ref[...].astype(BF16)
        pl_ = _mm(pb, wple_ref[...])
        rp = lax.rsqrt(_rowmean(pl_ * pl_) + EPS)
        pl_hat = pl_ * rp
        gple_v = gple_ref[...]
        ple = pl_hat * gple_v
        x1b = x1.astype(BF16)
        gate = _sigmoid(_mm(x1b, wpg_ref[...]) + bpg_ref[...])
        err = x1 + ple * gate - t_ref[...]
        loss = 0.5 * jnp.sum(_rowmean(err * err))
        dout = err * (1.0 / D_MODEL)

        du = dout * ple * gate * (1.0 - gate)
        dub = du.astype(BF16)
        dple = dout * gate
        dx1 = dout + _mm(dub, wpgt_ref[...])
        dwpg_ref[...] += _mm_tn(x1b, dub)
        dplh = dple * gple_v
        dpl = rp * (dplh - pl_hat * _rowmean(dplh * pl_hat))
        dwple_ref[...] += _mm_tn(pb, dpl.astype(BF16))
        dxres_ref[...] = dx1
        dyh = dx1 * gpost_v
        dy = ry * (dyh - y_hat * _rowmean(dyh * y_hat))
        dyb = dy.astype(BF16)
        dwout_ref[...] += _mm_tn(mix, dyb)
        dmix = _mm(dyb, woutt_ref[...])

        def head_bwd(dyv, hat, r, n, sg, g, gate):
            dn = dyv * (gate * sg)
            dgate = dyv * n * (sg * (1.0 + gate * (1.0 - sg)))
            dhat = dn * g
            do = r * (dhat - hat * (_seg(dhat * hat, bd) * inv_hd))
            return do, dgate, _colsum(dn * hat)

        dsbo, dsbg, dg_sb = head_bwd(dmix[:, 0:512], sb_hat, sb_r, sb_n, sb_sg, gsb_v, sbg_v)
        dmlao, dmlag, dg_ml = head_bwd(dmix[:, 512:1024], ml_hat, ml_r, ml_n, ml_sg, gmla_v, mlag_v)
        dsbo_ref[...] = dsbo.astype(BF16)
        dmlao_ref[...] = dmlao.astype(BF16)
        delta_ref[...] = _seg(dmlao * mlao, bd)
        dsbg_ref[...] = dsbg.astype(BF16)
        dmlag_ref[...] = dmlag.astype(BF16)
        vec_ref[pl.ds(0, 1), :] += _colsum(dx1 * y_hat)
        vec_ref[pl.ds(1, 1), :] += _colsum(dple * pl_hat)
        vec_ref[pl.ds(2, 1), :] += _colsum(du)
        vec_ref[pl.ds(3, 1), :] += jnp.concatenate([dg_sb, dg_ml], axis=1)
        vec_ref[pl.ds(4, 1), :] += jnp.full((1, D_MODEL), loss, F32)

    out_shape = (
        jax.ShapeDtypeStruct((s, 512), BF16), jax.ShapeDtypeStruct((s, 512), BF16), jax.ShapeDtypeStruct((s, 512), F32),
        jax.ShapeDtypeStruct((s, 512), BF16), jax.ShapeDtypeStruct((s, 512), BF16), jax.ShapeDtypeStruct((s, D_MODEL), F32),
        jax.ShapeDtypeStruct((D_MODEL, D_MODEL), F32), jax.ShapeDtypeStruct((D_MODEL, D_MODEL), F32),
        jax.ShapeDtypeStruct((PLE_DIM, D_MODEL), F32), jax.ShapeDtypeStruct((8, D_MODEL), F32),
    )
    return pl.pallas_call(
        body, name="post_fwd_bwd", grid=(s // TM,), out_shape=out_shape,
        in_specs=[_rows(D_MODEL), _rows(PLE_DIM), _rows(D_MODEL), _rows(512), _rows(512), _rows(512), _rows(512),
                  _full((1, 512)), _full((1, 512)), _full((D_MODEL, D_MODEL)), _full((D_MODEL, D_MODEL)),
                  _full((1, D_MODEL)), _full((PLE_DIM, D_MODEL)), _full((1, D_MODEL)), _full((D_MODEL, D_MODEL)),
                  _full((D_MODEL, D_MODEL)), _full((1, D_MODEL))],
        out_specs=(_rows(512), _rows(512), _rows(512), _rows(512), _rows(512), _rows(D_MODEL),
                   _full((D_MODEL, D_MODEL)), _full((D_MODEL, D_MODEL)), _full((PLE_DIM, D_MODEL)), _full((8, D_MODEL))),
        compiler_params=pltpu.CompilerParams(vmem_limit_bytes=VMEM_DENSE),
    )(x, p, tgt, sbo, mlao, sbg, mlag, gsb, gmla, wout, wout_t, gpost, wple, gple, wpg, wpg_t, bpg)


def _pre_bwd(x, dxres, dsbq, dsbk, dsbv, dsbg, dmlag, dqc, dkc, dmv, cq, ckv, tabs, gpre, win_t, gq, wuq_t, gkv,
             wk_t, wv_t):
    s = x.shape[0]
    c_t, sa_t, sb_t = tabs

    def body(x_ref, dxres_ref, dsbq_ref, dsbk_ref, dsbv_ref, dsbg_ref, dmlag_ref, dqc_ref, dkc_ref, dmv_ref, cq_ref,
             ckv_ref, c_ref, sa_ref, sb_ref, gpre_ref, wint_ref, gq_ref, wuqt_ref, gkv_ref, wkt_ref, wvt_ref,
             gx_ref, dwin_ref, dwuq_ref, dwk_ref, dwv_ref, vec_ref, dwin_acc):
        i = pl.program_id(0)

        @pl.when(i == 0)
        def _():
            dwin_acc[...] = jnp.zeros_like(dwin_acc)
            dwuq_ref[...] = jnp.zeros_like(dwuq_ref)
            dwk_ref[...] = jnp.zeros_like(dwk_ref)
            dwv_ref[...] = jnp.zeros_like(dwv_ref)
            vec_ref[...] = jnp.zeros_like(vec_ref)

        lane = lax.broadcasted_iota(jnp.int32, (1, LANES), 1)
        c1, sa1, sb1 = c_ref[...], sa_ref[...], sb_ref[...]
        c8, sa8, sb8 = jnp.tile(c1, (1, 8)), jnp.tile(sa1, (1, 8)), jnp.tile(sb1, (1, 8))

        def norm_bwd(dn, hat, r, g):
            t = dn * g
            return r * (t - hat * _rowmean(t * hat)), _colsum(dn * hat)

        dqeb = _rope_bwd(dqc_ref[...], c8, sa8, sb8).astype(BF16)
        cq = cq_ref[...]
        rq = lax.rsqrt(_rowmean(cq * cq) + EPS)
        cq_hat = cq * rq
        gq_v = gq_ref[...]
        dwuq_ref[...] += _mm_tn((cq_hat * gq_v).astype(BF16), dqeb)
        dcq, dg_q = norm_bwd(_mm(dqeb, wuqt_ref[...]), cq_hat, rq, gq_v)

        dkc = dkc_ref[...]
        dkcb = dkc.astype(BF16)
        dmvb = dmv_ref[...].astype(BF16)
        ckv = ckv_ref[...]
        rkv = lax.rsqrt(_rowmean(ckv * ckv) + EPS)
        ckv_hat = ckv * rkv
        gkv_v = gkv_ref[...]
        ckvnb = (ckv_hat * gkv_v).astype(BF16)
        dwk_ref[...] += _mm_tn(ckvnb, dkcb)
        dwv_ref[...] += _mm_tn(ckvnb, dmvb)
        dckv, dg_kv = norm_bwd(_mm(dkcb, wkt_ref[...]) + _mm(dmvb, wvt_ref[...]), ckv_hat, rkv, gkv_v)

        dkr = dkc[:, 0:LANES]
        for hh in range(1, 8):
            dkr = dkr + dkc[:, LANES * hh:LANES * (hh + 1)]
        dkr = _rope_bwd(dkr, c1, sa1, sb1)
        dkr = jnp.where((lane >= 64) & (lane < 96), dkr, 0.0)

        dproj = jnp.concatenate([dsbq_ref[...], dsbk_ref[...].astype(BF16), dsbv_ref[...].astype(BF16), dsbg_ref[...],
                                 dcq.astype(BF16), dckv.astype(BF16), dkr.astype(BF16), dmlag_ref[...]], axis=1)
        xv = x_ref[...]
        r1 = lax.rsqrt(_rowmean(xv * xv) + EPS)
        x_hat = xv * r1
        gpre_v = gpre_ref[...]
        dwin_acc[...] += _mm_tn((x_hat * gpre_v).astype(BF16), dproj)
        dx, dg_pre = norm_bwd(_mm(dproj, wint_ref[...]), x_hat, r1, gpre_v)
        gx_ref[...] = dxres_ref[...] + dx
        vec_ref[pl.ds(0, 1), :] += dg_pre
        vec_ref[pl.ds(1, 1), :] += jnp.concatenate([dg_q, dg_kv, jnp.zeros((1, D_MODEL - Q_LORA - KV_LORA), F32)], axis=1)

        @pl.when(i == pl.num_programs(0) - 1)
        def _():
            pltpu.sync_copy(dwin_acc, dwin_ref)

    out_shape = (
        jax.ShapeDtypeStruct((s, D_MODEL), F32), jax.ShapeDtypeStruct((D_MODEL, D_EXT), F32),
        jax.ShapeDtypeStruct((Q_LORA, 1024), F32), jax.ShapeDtypeStruct((KV_LORA, 1024), F32),
        jax.ShapeDtypeStruct((KV_LORA, 512), F32), jax.ShapeDtypeStruct((8, D_MODEL), F32),
    )
    return pl.pallas_call(
        body, name="pre_bwd", grid=(s // TM,), out_shape=out_shape,
        in_specs=[_rows(D_MODEL), _rows(D_MODEL), _rows(512), _rows(512), _rows(512), _rows(512), _rows(512),
                  _rows(1024), _rows(1024), _rows(512), _rows(Q_LORA), _rows(KV_LORA), _rows(LANES), _rows(LANES),
                  _rows(LANES), _full((1, D_MODEL)), _full((D_EXT, D_MODEL)), _full((1, Q_LORA)), _full((1024, Q_LORA)),
                  _full((1, KV_LORA)), _full((1024, KV_LORA)), _full((512, KV_LORA))],
        out_specs=(_rows(D_MODEL), pl.BlockSpec(memory_space=pl.ANY), _full((Q_LORA, 1024)), _full((KV_LORA, 1024)),
                   _full((KV_LORA, 512)), _full((8, D_MODEL))),
        scratch_shapes=[pltpu.VMEM((D_MODEL, D_EXT), F32)],
        compiler_params=pltpu.CompilerParams(vmem_limit_bytes=VMEM_DENSE),
    )(x, dxres, dsbq, dsbk, dsbv, dsbg, dmlag, dqc, dkc, dmv, cq, ckv, c_t, sa_t, sb_t, gpre, win_t, gq, wuq_t, gkv,
      wk_t, wv_t)


def _place():
    return lax.axis_index("x"), lax.axis_index("y"), lax.axis_index("c")


def _allgather_weights(pack):
    def body(p_ref, out_ref, send_sems, recv_sems):
        x, y, c = _place()
        me, sib = (x, y, c), (x, y, 1 - c)
        chips = [(1 - x, y), (x, 1 - y), (1 - x, 1 - y)]

        def half(chip, hc):
            start = pl.multiple_of(hc * HALF_ROWS, 8)
            return out_ref.at[2 * chip[0] + chip[1], pl.ds(start, HALF_ROWS), :]

        def copy(k, chip, hc, to):
            return pltpu.make_async_remote_copy(src_ref=half(chip, hc), dst_ref=half(chip, hc), send_sem=send_sems.at[k],
                                                recv_sem=recv_sems.at[k], device_id=to, device_id_type=MESH)

        out_ref[2 * x + y] = p_ref[...].astype(BF16)
        first = [copy(j, (x, y), c, (*chip, c)) for j, chip in enumerate(chips)]
        for cp in first:
            cp.start()
        passed = [copy(3 + j, chip, c, sib) for j, chip in enumerate(chips)]
        for j, chip in enumerate(chips):
            copy(j, chip, c, me).wait_recv()
            passed[j].start()
        for j, chip in enumerate(chips):
            copy(3 + j, chip, 1 - c, me).wait_recv()
        for cp in first + passed:
            cp.wait_send()

    return pl.pallas_call(
        body, name="allgather_weights",
        out_shape=jax.ShapeDtypeStruct((N_SHARD, PACK_ROWS, 1024), BF16),
        in_specs=[pl.BlockSpec(memory_space=pltpu.VMEM)], out_specs=pl.BlockSpec(memory_space=pltpu.VMEM),
        scratch_shapes=[pltpu.SemaphoreType.DMA((6,)), pltpu.SemaphoreType.DMA((6,))],
        compiler_params=pltpu.CompilerParams(vmem_limit_bytes=VMEM_ATTN),
    )(pack)


def _reduce_scatter_grads(gpack, vec):
    def body(g_ref, vec_ref, f_ref, vsum_ref, acc, sib_half, send_buf, recv_buf, vrecv, local_sem, send_sems, recv_sems):
        x, y, c = _place()
        me, sib = (x, y, c), (x, y, 1 - c)
        mine = 2 * x + y
        chips = [(1 - x, y), (x, 1 - y), (1 - x, 1 - y)]

        def remote(k, src, dst, to):
            return pltpu.make_async_remote_copy(src_ref=src, dst_ref=dst, send_sem=send_sems.at[k], recv_sem=recv_sems.at[k],
                                                device_id=to, device_id_type=MESH)

        load = pltpu.make_async_copy(g_ref.at[c], acc, local_sem)
        load.start()
        to_sib = remote(0, g_ref.at[1 - c], sib_half, sib)
        to_sib.start()

        my_dev = 4 * x + 2 * y + c
        vrecv[my_dev] = vec_ref[...]
        vec_sends = []
        for k in range(1, 8):
            to = (x ^ ((k >> 2) & 1), y ^ ((k >> 1) & 1), c ^ (k & 1))
            cp = remote(k, vec_ref, vrecv.at[my_dev], to)
            cp.start()
            vec_sends.append(cp)

        load.wait()
        remote(0, g_ref.at[1 - c], sib_half, me).wait_recv()
        for k in range(N_SHARD):
            acc[k] = acc[k] + sib_half[k]

        sends = []
        for j, chip in enumerate(chips):
            idx = 2 * chip[0] + chip[1]
            send_buf[idx] = acc[idx].astype(BF16)
            cp = remote(8 + j, send_buf.at[idx], recv_buf.at[mine], (*chip, c))
            cp.start()
            sends.append(cp)
        total = acc[mine]
        for j, chip in enumerate(chips):
            idx = 2 * chip[0] + chip[1]
            remote(8 + j, send_buf.at[idx], recv_buf.at[idx], me).wait_recv()
            total = total + recv_buf[idx].astype(F32)
        start = pl.multiple_of(c * HALF_ROWS, 8)
        f_ref[pl.ds(start, HALF_ROWS), :] = total

        other = pl.multiple_of((1 - c) * HALF_ROWS, 8)
        swap = remote(11, f_ref.at[pl.ds(start, HALF_ROWS), :], f_ref.at[pl.ds(start, HALF_ROWS), :], sib)
        swap.start()
        remote(11, f_ref.at[pl.ds(other, HALF_ROWS), :], f_ref.at[pl.ds(other, HALF_ROWS), :], me).wait_recv()

        for k in range(1, 8):
            src_dev = 4 * (x ^ ((k >> 2) & 1)) + 2 * (y ^ ((k >> 1) & 1)) + (c ^ (k & 1))
            remote(k, vec_ref, vrecv.at[src_dev], me).wait_recv()
        vs = vrecv[0]
        for d in range(1, 8):
            vs = vs + vrecv[d]
        vsum_ref[...] = vs

        for cp in [to_sib, swap] + vec_sends + sends:
            cp.wait_send()

    return pl.pallas_call(
        body, name="reduce_scatter_grads",
        out_shape=(jax.ShapeDtypeStruct((PACK_ROWS, 1024), F32), jax.ShapeDtypeStruct((VEC_ROWS, 1024), F32)),
        in_specs=[pl.BlockSpec(memory_space=pl.ANY), pl.BlockSpec(memory_space=pltpu.VMEM)],
        out_specs=(pl.BlockSpec(memory_space=pltpu.VMEM), pl.BlockSpec(memory_space=pltpu.VMEM)),
        scratch_shapes=[
            pltpu.VMEM((N_SHARD, HALF_ROWS, 1024), F32), pltpu.VMEM((N_SHARD, HALF_ROWS, 1024), F32),
            pltpu.VMEM((N_SHARD, HALF_ROWS, 1024), BF16), pltpu.VMEM((N_SHARD, HALF_ROWS, 1024), BF16),
            pltpu.VMEM((8, VEC_ROWS, 1024), F32),
            pltpu.SemaphoreType.DMA, pltpu.SemaphoreType.DMA((12,)), pltpu.SemaphoreType.DMA((12,)),
        ],
        compiler_params=pltpu.CompilerParams(vmem_limit_bytes=56 * 1024 * 1024),
    )(gpack, vec)


def _adamw(w, g, m, v):
    rows, cols = w.shape
    tr = rows if rows <= 256 else 256

    def body(w_ref, g_ref, m_ref, v_ref, d_ref, nm_ref, nv_ref):
        gv = g_ref[...]
        m2 = ADAM_B1 * m_ref[...] + (1.0 - ADAM_B1) * gv
        v2 = ADAM_B2 * v_ref[...] + (1.0 - ADAM_B2) * (gv * gv)
        m_hat = m2 / (1.0 - ADAM_B1 ** ADAM_STEP)
        v_hat = v2 / (1.0 - ADAM_B2 ** ADAM_STEP)
        d_ref[...] = -ADAM_LR * (m_hat / (jnp.sqrt(v_hat) + ADAM_EPS) + ADAM_WD * w_ref[...])
        nm_ref[...] = m2
        nv_ref[...] = v2

    spec = pl.BlockSpec((tr, cols), lambda i: (i, 0))
    shp = jax.ShapeDtypeStruct((rows, cols), F32)
    return pl.pallas_call(body, name="adamw", grid=(rows // tr,), out_shape=(shp, shp, shp),
                          in_specs=[spec] * 4, out_specs=(spec,) * 3)(w, g, m, v)


_PACK_LAYOUT = (("w_in", 1024, 744), ("w_uq", 256, 192), ("w_ukv", 128, 256), ("w_out", 256, 1024), ("w_ple", 256, 256),
                ("w_ple_gate", 256, 1024))


def _pack_shard(parts):
    rows = [parts[n].reshape(-1, 1024) for n, _, _ in _PACK_LAYOUT]
    used = sum(r.shape[0] for r in rows)
    return jnp.concatenate(rows + [jnp.zeros((PACK_ROWS - used, 1024), rows[0].dtype)], axis=0)


def _unpack_shard(pack):
    out, off = {}, 0
    for n, r, cdim in _PACK_LAYOUT:
        nr = r * cdim // 1024
        out[n] = pack[..., off:off + nr, :].reshape(pack.shape[:-2] + (r, cdim))
        off += nr
    return out


def _join_shards(parts):
    cat_cols = lambda a: jnp.concatenate([a[k] for k in range(N_SHARD)], axis=1)
    cat_rows = lambda a: jnp.concatenate([a[k] for k in range(N_SHARD)], axis=0)
    return {"w_in": cat_cols(parts["w_in"]), "w_uq": cat_cols(parts["w_uq"]), "w_ukv": cat_cols(parts["w_ukv"]),
            "w_out": cat_rows(parts["w_out"]), "w_ple": cat_cols(parts["w_ple"]), "w_ple_gate": cat_rows(parts["w_ple_gate"])}


def _split_shards(full):
    cols = lambda a: jnp.stack(jnp.split(a, N_SHARD, axis=1))
    rows = lambda a: jnp.stack(jnp.split(a, N_SHARD, axis=0))
    return {"w_in": cols(full["w_in"]), "w_uq": cols(full["w_uq"]), "w_ukv": cols(full["w_ukv"]),
            "w_out": rows(full["w_out"]), "w_ple": cols(full["w_ple"]), "w_ple_gate": rows(full["w_ple_gate"])}


def _extend_weights(w):
    win = w["w_in"]
    zeros = lambda r, c: jnp.zeros((r, c), win.dtype)
    win_ext = jnp.concatenate([win[:, :2432], zeros(D_MODEL, 64), win[:, 2432:2464], zeros(D_MODEL, 32), win[:, 2464:]], axis=1)
    wuq_ext = jnp.pad(w["w_uq"].reshape(Q_LORA, 8, 96), ((0, 0), (0, 0), (0, 32))).reshape(Q_LORA, 1024)
    wukv = w["w_ukv"].reshape(KV_LORA, 8, 128)
    wk_ext = jnp.pad(wukv[:, :, :64], ((0, 0), (0, 0), (0, 64))).reshape(KV_LORA, 1024)
    wv = wukv[:, :, 64:].reshape(KV_LORA, 512)
    return win_ext, wuq_ext, wk_ext, wv


def _contract_grads(dwin_ext, dwuq_ext, dwk_ext, dwv):
    dwin = jnp.concatenate([dwin_ext[:, :2432], dwin_ext[:, 2496:2528], dwin_ext[:, 2560:]], axis=1)
    dwuq = dwuq_ext.reshape(Q_LORA, 8, 128)[:, :, :96].reshape(Q_LORA, 768)
    dwukv = jnp.concatenate([dwk_ext.reshape(KV_LORA, 8, 128)[:, :, :64], dwv.reshape(KV_LORA, 8, 64)], axis=2)
    return dwin, dwuq, dwukv.reshape(KV_LORA, 1024)


def _rope_tables(positions):
    half = QK_ROPE // 2
    freq = ROPE_THETA ** (-jnp.arange(half, dtype=F32) / half)
    ang = positions.astype(F32)[:, None] * freq
    cos, sin = jnp.cos(ang), jnp.sin(ang)
    s = positions.shape[0]
    z = lambda n: jnp.zeros((s, n), F32)
    c_t = jnp.concatenate([jnp.ones((s, 64), F32), cos, cos, z(32)], axis=1)
    sa_t = jnp.concatenate([z(64), -sin, z(16), z(32)], axis=1)
    sb_t = jnp.concatenate([z(64), z(16), sin, z(32)], axis=1)
    return c_t, sa_t, sb_t


def _local_grads(x, p, positions, tgt, gains, wfull):
    win_ext, wuq_ext, wk_ext, wv = _extend_weights(wfull)
    wout, wple, wpg = wfull["w_out"], wfull["w_ple"], wfull["w_ple_gate"]
    tabs = _rope_tables(positions)
    g = gains
    sbq, sbk, sbv, sbg, mlag, cq, ckv, qc, kc, mv = _pre_fwd(x, tabs, g["norm_pre_g"], win_ext, g["q_norm_g"], wuq_ext,
                                                             g["kv_norm_g"], wk_ext, wv)
    sbo = _sb_fwd(sbq, sbk, sbv)
    mlao, lse = _mla_fwd(qc, kc, mv)
    dsbo, dmlao, delta, dsbg, dmlag, dxres, dwout, dwpg, dwple, vec_c = _post(
        x, p, tgt, sbo, mlao, sbg, mlag, g["sb_out_norm_g"], g["mla_out_norm_g"], wout, wout.T, g["norm_post_g"], wple,
        g["ple_norm_g"], wpg, wpg.T, g["b_ple_gate"])
    dsbq, dsbk, dsbv = _sb_bwd(sbq, sbk, sbv, dsbo)
    dqc, dkc, dmv = _mla_bwd(qc, kc, mv, dmlao, lse, delta)
    gx, dwin_ext, dwuq_ext, dwk_ext, dwv, vec_d = _pre_bwd(
        x, dxres, dsbq, dsbk, dsbv, dsbg, dmlag, dqc, dkc, dmv, cq, ckv, tabs, g["norm_pre_g"], win_ext.T, g["q_norm_g"],
        wuq_ext.T, g["kv_norm_g"], wk_ext.T, wv.T)
    dwin, dwuq, dwukv = _contract_grads(dwin_ext, dwuq_ext, dwk_ext, dwv)
    grads = {"w_in": dwin, "w_uq": dwuq, "w_ukv": dwukv, "w_out": dwout, "w_ple": dwple, "w_ple_gate": dwpg}
    return gx, grads, jnp.concatenate([vec_c, vec_d], axis=0)


_VEC_LAYOUT = (("norm_post_g", 0, 0, 1024), ("ple_norm_g", 1, 0, 1024), ("b_ple_gate", 2, 0, 1024), ("sb_out_norm_g", 3, 0, 512),
               ("mla_out_norm_g", 3, 512, 512), ("norm_pre_g", 8, 0, 1024), ("q_norm_g", 9, 0, 256), ("kv_norm_g", 9, 256, 128))
_LOSS_ROW = 4
_WEIGHT_ORDER = ("norm_pre_g", "w_in", "q_norm_g", "w_uq", "kv_norm_g", "w_ukv", "sb_out_norm_g", "mla_out_norm_g", "w_out",
                 "norm_post_g", "w_ple", "ple_norm_g", "w_ple_gate", "b_ple_gate")


def _vec_block(named):
    blk = jnp.zeros((VEC_ROWS, 1024), F32)
    for n, r, c0, width in _VEC_LAYOUT:
        blk = blk.at[r, c0:c0 + width].set(named[n][0])
    return blk


def kernel(x, p, positions, norm_pre_g, w_in, q_norm_g, w_uq, kv_norm_g, w_ukv, sb_out_norm_g, mla_out_norm_g, w_out, norm_post_g, w_ple, ple_norm_g, w_ple_gate, b_ple_gate, loss_target, m_norm_pre_g, m_w_in, m_q_norm_g, m_w_uq, m_kv_norm_g, m_w_ukv, m_sb_out_norm_g, m_mla_out_norm_g, m_w_out, m_norm_post_g, m_w_ple, m_ple_norm_g, m_w_ple_gate, m_b_ple_gate, v_norm_pre_g, v_w_in, v_q_norm_g, v_w_uq, v_kv_norm_g, v_w_ukv, v_sb_out_norm_g, v_mla_out_norm_g, v_w_out, v_norm_post_g, v_w_ple, v_ple_norm_g, v_w_ple_gate, v_b_ple_gate):
    w = {"norm_pre_g": norm_pre_g, "w_in": w_in[0], "q_norm_g": q_norm_g, "w_uq": w_uq[0], "kv_norm_g": kv_norm_g, "w_ukv": w_ukv[0],
         "sb_out_norm_g": sb_out_norm_g, "mla_out_norm_g": mla_out_norm_g, "w_out": w_out[0], "norm_post_g": norm_post_g,
         "w_ple": w_ple[0], "ple_norm_g": ple_norm_g, "w_ple_gate": w_ple_gate[0], "b_ple_gate": b_ple_gate}
    m = {"norm_pre_g": m_norm_pre_g, "w_in": m_w_in[0], "q_norm_g": m_q_norm_g, "w_uq": m_w_uq[0], "kv_norm_g": m_kv_norm_g,
         "w_ukv": m_w_ukv[0], "sb_out_norm_g": m_sb_out_norm_g, "mla_out_norm_g": m_mla_out_norm_g, "w_out": m_w_out[0],
         "norm_post_g": m_norm_post_g, "w_ple": m_w_ple[0], "ple_norm_g": m_ple_norm_g, "w_ple_gate": m_w_ple_gate[0],
         "b_ple_gate": m_b_ple_gate}
    v = {"norm_pre_g": v_norm_pre_g, "w_in": v_w_in[0], "q_norm_g": v_q_norm_g, "w_uq": v_w_uq[0], "kv_norm_g": v_kv_norm_g,
         "w_ukv": v_w_ukv[0], "sb_out_norm_g": v_sb_out_norm_g, "mla_out_norm_g": v_mla_out_norm_g, "w_out": v_w_out[0],
         "norm_post_g": v_norm_post_g, "w_ple": v_w_ple[0], "ple_norm_g": v_ple_norm_g, "w_ple_gate": v_w_ple_gate[0],
         "b_ple_gate": v_b_ple_gate}
    big = [n for n, _, _ in _PACK_LAYOUT]

    gathered = _allgather_weights(_pack_shard({n: w[n] for n in big}))
    wfull = _join_shards(_unpack_shard(gathered))

    gx, grads, vec = _local_grads(x[0], p[0, 0], positions[0], loss_target[0], w, wfull)

    gsh = _split_shards(grads)
    gpack = jnp.stack([_pack_shard({n: gsh[n][k] for n in big}) for k in range(N_SHARD)])
    gpack = gpack.reshape(N_SHARD, 2, HALF_ROWS, 1024).transpose(1, 0, 2, 3)
    gred, vsum = _reduce_scatter_grads(gpack, vec)
    g = _unpack_shard(gred)
    for n, r, c0, width in _VEC_LAYOUT:
        g[n] = vsum[r:r + 1, c0:c0 + width]
    loss = vsum[_LOSS_ROW, 0]

    delta, new_m, new_v = {}, {}, {}
    for n in big:
        delta[n], new_m[n], new_v[n] = _adamw(w[n], g[n], m[n], v[n])
    small = [n for n in _WEIGHT_ORDER if n not in big]
    dv, mv_, vv = _adamw(_vec_block({n: w[n] for n in small}), vsum * _vec_block({n: jnp.ones_like(w[n]) for n in small}),
                         _vec_block({n: m[n] for n in small}), _vec_block({n: v[n] for n in small}))
    for n, r, c0, width in _VEC_LAYOUT:
        delta[n], new_m[n], new_v[n] = dv[r:r + 1, c0:c0 + width], mv_[r:r + 1, c0:c0 + width], vv[r:r + 1, c0:c0 + width]

    lead = lambda n, a: a[None] if n in big else a
    return (loss, gx[None],
            *[lead(n, g[n]) for n in _WEIGHT_ORDER], *[lead(n, delta[n]) for n in _WEIGHT_ORDER],
            *[lead(n, new_m[n]) for n in _WEIGHT_ORDER], *[lead(n, new_v[n]) for n in _WEIGHT_ORDER])
```

```python
import jax
import jax.numpy as jnp
from jax import lax
from jax.experimental import pallas as pl
from jax.experimental.pallas import tpu as pltpu

F32 = jnp.float32
BF16 = jnp.bfloat16
MESH = pl.DeviceIdType.MESH

D_MODEL = 1024
HEAD_DIM = 64
D_SB = 512
D_MLA = 512
Q_LORA = 256
KV_LORA = 128
QK_NOPE = 64
QK_ROPE = 32
PLE_DIM = 256
D_IN = 2976
D_EXT = 3072
ROPE_THETA = 10000.0
EPS = 1e-6
N_SHARD = 4

ADAM_LR = 0.001
ADAM_B1 = 0.9
ADAM_B2 = 0.999
ADAM_EPS = 1e-08
ADAM_WD = 0.01
ADAM_STEP = 10

LANES = 128
BK = 128
WQ = 256
SB_CUTOFF = 120.0
TM = 256
VEC_ROWS = 16
VMEM_DENSE = 52 * 1024 * 1024
VMEM_ATTN = 40 * 1024 * 1024


def _mm(a, b):
    return jnp.dot(a, b, preferred_element_type=F32)


def _mm_nt(a, b):
    return lax.dot_general(a, b, (((1,), (1,)), ((), ())), preferred_element_type=F32)


def _mm_tn(a, b):
    return lax.dot_general(a, b, (((0,), (0,)), ((), ())), preferred_element_type=F32)


def _seg(a, bd):
    a1 = a.astype(BF16)
    r1 = a - a1.astype(F32)
    a2 = r1.astype(BF16)
    a3 = (r1 - a2.astype(F32)).astype(BF16)
    return _mm(a1, bd) + _mm(a2, bd) + _mm(a3, bd)


def _blockdiag(n, seg):
    r = lax.broadcasted_iota(jnp.int32, (n, n), 0) // seg
    c = lax.broadcasted_iota(jnp.int32, (n, n), 1) // seg
    return jnp.where(r == c, 1.0, 0.0).astype(BF16)


def _sigmoid(a):
    return 1.0 / (1.0 + jnp.exp(-a))


def _rowmean(a):
    return jnp.mean(a, axis=-1, keepdims=True)


def _colsum(a):
    return jnp.sum(a, axis=0, keepdims=True)


def _rope_fwd(a, c, sa, sb):
    w = a.shape[-1]
    return a * c + pltpu.roll(a, w - 16, 1) * sa + pltpu.roll(a, 16, 1) * sb


def _rope_bwd(g, c, sa, sb):
    w = g.shape[-1]
    return g * c + pltpu.roll(g * sa, 16, 1) + pltpu.roll(g * sb, w - 16, 1)


def _full(shape):
    return pl.BlockSpec(shape, lambda *_: (0,) * len(shape))


def _rows(width, tm=TM):
    return pl.BlockSpec((tm, width), lambda i: (i, 0))


def _pre_fwd(x, tabs, gpre, win, gq, wuq, gkv, wk, wv):
    s = x.shape[0]
    c_t, sa_t, sb_t = tabs

    def body(x_ref, c_ref, sa_ref, sb_ref, gpre_ref, win_ref, gq_ref, wuq_ref, gkv_ref, wk_ref, wv_ref,
             sbq_ref, sbk_ref, sbv_ref, sbg_ref, mlag_ref, cq_ref, ckv_ref, qc_ref, kc_ref, mv_ref):
        xv = x_ref[...]
        r1 = lax.rsqrt(_rowmean(xv * xv) + EPS)
        h = (xv * r1 * gpre_ref[...]).astype(BF16)
        proj = _mm(h, win_ref[...])
        sbq_ref[...] = proj[:, 0:512].astype(BF16)
        sbk_ref[...] = proj[:, 512:1024].astype(BF16)
        sbv_ref[...] = proj[:, 1024:1536].astype(BF16)
        sbg_ref[...] = proj[:, 1536:2048]
        cq = proj[:, 2048:2304]
        ckv = proj[:, 2304:2432]
        kr = proj[:, 2432:2560]
        mlag_ref[...] = proj[:, 2560:3072]
        cq_ref[...] = cq
        ckv_ref[...] = ckv
        c1, sa1, sb1 = c_ref[...], sa_ref[...], sb_ref[...]
        c8, sa8, sb8 = jnp.tile(c1, (1, 8)), jnp.tile(sa1, (1, 8)), jnp.tile(sb1, (1, 8))
        cqn = (cq * lax.rsqrt(_rowmean(cq * cq) + EPS) * gq_ref[...]).astype(BF16)
        qe = _mm(cqn, wuq_ref[...])
        qc_ref[...] = _rope_fwd(qe, c8, sa8, sb8).astype(BF16)
        ckvn = (ckv * lax.rsqrt(_rowmean(ckv * ckv) + EPS) * gkv_ref[...]).astype(BF16)
        ke = _mm(ckvn, wk_ref[...])
        krr = _rope_fwd(kr, c1, sa1, sb1)
        kc_ref[...] = (ke + jnp.tile(krr, (1, 8))).astype(BF16)
        mv_ref[...] = _mm(ckvn, wv_ref[...]).astype(BF16)

    out_shape = (
        jax.ShapeDtypeStruct((s, 512), BF16), jax.ShapeDtypeStruct((s, 512), BF16), jax.ShapeDtypeStruct((s, 512), BF16),
        jax.ShapeDtypeStruct((s, 512), F32), jax.ShapeDtypeStruct((s, 512), F32),
        jax.ShapeDtypeStruct((s, Q_LORA), F32), jax.ShapeDtypeStruct((s, KV_LORA), F32),
        jax.ShapeDtypeStruct((s, 1024), BF16), jax.ShapeDtypeStruct((s, 1024), BF16), jax.ShapeDtypeStruct((s, 512), BF16),
    )
    return pl.pallas_call(
        body, name="pre_fwd", grid=(s // TM,), out_shape=out_shape,
        in_specs=[_rows(D_MODEL), _rows(LANES), _rows(LANES), _rows(LANES), _full((1, D_MODEL)), _full((D_MODEL, D_EXT)),
                  _full((1, Q_LORA)), _full((Q_LORA, 1024)), _full((1, KV_LORA)), _full((KV_LORA, 1024)), _full((KV_LORA, 512))],
        out_specs=(_rows(512), _rows(512), _rows(512), _rows(512), _rows(512), _rows(Q_LORA), _rows(KV_LORA),
                   _rows(1024), _rows(1024), _rows(512)),
        compiler_params=pltpu.CompilerParams(vmem_limit_bytes=VMEM_DENSE),
    )(x, c_t, sa_t, sb_t, gpre, win, gq, wuq, gkv, wk, wv)


def _softplus(z):
    neg_abs = lax.bitcast_convert_type(lax.bitcast_convert_type(z, jnp.uint32) | jnp.uint32(0x80000000), F32)
    return jnp.maximum(z, 0.0) + jnp.log(1.0 + jnp.exp(neg_abs))


def _pair_tri(kind):
    r = lax.broadcasted_iota(jnp.int32, (512, 256), 0)
    c = lax.broadcasted_iota(jnp.int32, (512, 256), 1)
    same = ((r // BK) % 2) == (c // BK)
    rk, ck = r % BK, c % BK
    m = {"ge": rk >= ck, "lt": rk < ck, "le": rk <= ck}[kind]
    return jnp.where(same & m, 1.0, 0.0).astype(BF16)


def _split2(a):
    hi = a.astype(BF16)
    lo = (a - hi.astype(F32)).astype(BF16)
    return jnp.concatenate([hi, lo], axis=1)


def _pair_stack(b, lane):
    zero = jnp.zeros_like(b)
    return jnp.concatenate([jnp.where(lane < 64, b, zero), jnp.where(lane >= 64, b, zero)], axis=0)


def _sb_fwd(q, k, v):
    s = q.shape[0]

    def body(q_ref, k_ref, v_ref, o_ref):
        i = pl.program_id(1)
        lane = lax.broadcasted_iota(jnp.int32, (1, LANES), 1)
        row = lax.broadcasted_iota(jnp.int32, (WQ, 2 * BK), 0) + i * WQ
        col = lax.broadcasted_iota(jnp.int32, (WQ, 2 * BK), 1) % BK
        u_ge = _pair_tri("ge")
        qs = q_ref[...] * (HEAD_DIM ** -0.5)

        def scores(j):
            ks = pl.multiple_of(j * BK, BK)
            return _mm_nt(qs, _pair_stack(k_ref[pl.ds(ks, BK), :], lane))

        def decay(z, j, masked):
            sp = _softplus(z)
            if masked:
                sp = jnp.where((col + j * BK) < row, sp, 0.0)
            return _mm(_split2(sp), u_ge)

        def weights(z, cum, j, masked):
            w = jnp.exp(z - cum)
            if masked:
                w = jnp.where((col + j * BK) < row, w, 0.0)
            ks = pl.multiple_of(j * BK, BK)
            return _mm(w.astype(BF16), _pair_stack(v_ref[pl.ds(ks, BK), :], lane))

        def fold(carry, pv, cum):
            acc, run = carry
            return acc + jnp.exp(-run) * pv, run + _heads_narrow(cum, 0, lane)

        def pair(ja, jb, carry, masked):
            za, zb = scores(ja), scores(jb)
            ca = decay(za, ja, masked)
            cb = decay(zb, jb, masked)
            pa = weights(za, ca, ja, masked)
            pb = weights(zb, cb, jb, masked)
            return fold(fold(carry, pa, ca), pb, cb)

        assert WQ == 2 * BK
        acc, run = pair(2 * i + 1, 2 * i, (jnp.zeros((WQ, LANES), F32), jnp.zeros((WQ, LANES), F32)), True)

        def more(c):
            return (c[0] < i) & (c[1] > 0)

        def step(c):
            jj, _, acc, run = c
            acc, run = pair(2 * i - 1 - 2 * jj, 2 * i - 2 - 2 * jj, (acc, run), False)
            return jj + 1, (jnp.min(run) < SB_CUTOFF).astype(jnp.int32), acc, run

        _, _, acc, _ = lax.while_loop(more, step, (jnp.int32(0), (jnp.min(run) < SB_CUTOFF).astype(jnp.int32), acc, run))
        o_ref[...] = acc

    qspec = pl.BlockSpec((WQ, LANES), lambda p, i: (i, p))
    kspec = pl.BlockSpec((s, LANES), lambda p, i: (0, p))
    return pl.pallas_call(
        body, name="sb_fwd", grid=(4, s // WQ),
        out_shape=jax.ShapeDtypeStruct((s, 512), F32),
        in_specs=[qspec, kspec, kspec], out_specs=qspec,
        compiler_params=pltpu.CompilerParams(vmem_limit_bytes=VMEM_ATTN),
    )(q, k, v)


def _pair_tri1(kind):
    r = lax.broadcasted_iota(jnp.int32, (2 * BK, 2 * BK), 0)
    c = lax.broadcasted_iota(jnp.int32, (2 * BK, 2 * BK), 1)
    rk, ck = r % BK, c % BK
    m = {"ge": rk >= ck, "lt": rk < ck, "le": rk <= ck}[kind]
    return jnp.where(((r // BK) == (c // BK)) & m, 1.0, 0.0).astype(BF16)


def _heads_wide(a):
    m = a.shape[0]
    return jnp.concatenate([jnp.broadcast_to(a[:, 0:1], (m, BK)), jnp.broadcast_to(a[:, 64:65], (m, BK))], axis=1)


def _heads_narrow(a, col, lane):
    return jnp.where(lane < 64, a[:, col:col + 1], a[:, BK + col:BK + col + 1])


def _pair_rowsum():
    r = lax.broadcasted_iota(jnp.int32, (512, LANES), 0)
    c = lax.broadcasted_iota(jnp.int32, (512, LANES), 1)
    return jnp.where(((r // BK) % 2) == (c // 64), 1.0, 0.0).astype(BF16)


def _sb_bwd(q, k, v, do):
    s = q.shape[0]
    assert WQ == 2 * BK

    def body(q_ref, k_ref, v_ref, do_ref, dq_ref, dk_ref, dv_ref, later_ref):
        i = pl.program_id(1)

        @pl.when(i == 0)
        def _():
            dk_ref[...] = jnp.zeros_like(dk_ref)
            dv_ref[...] = jnp.zeros_like(dv_ref)

        lane = lax.broadcasted_iota(jnp.int32, (1, LANES), 1)
        row = lax.broadcasted_iota(jnp.int32, (WQ, 2 * BK), 0) + i * WQ
        col = lax.broadcasted_iota(jnp.int32, (WQ, 2 * BK), 1) % BK
        u_ge = _pair_tri("ge")
        u_le = _pair_tri1("le")
        r_sum = _pair_rowsum()
        qs = q_ref[...] * (HEAD_DIM ** -0.5)
        dof = do_ref[...].astype(F32)

        def scan(ja, jb, run, masked):
            za = _mm_nt(qs, _pair_stack(k_ref[pl.ds(pl.multiple_of(ja * BK, BK), BK), :], lane))
            zb = _mm_nt(qs, _pair_stack(k_ref[pl.ds(pl.multiple_of(jb * BK, BK), BK), :], lane))
            spa, spb = _softplus(za), _softplus(zb)
            if masked:
                spa = jnp.where((col + ja * BK) < row, spa, 0.0)
                spb = jnp.where((col + jb * BK) < row, spb, 0.0)
            rsa, rsb = _mm(_split2(spa), r_sum), _mm(_split2(spb), r_sum)
            later_ref[ja] = run
            later_ref[jb] = run + rsa
            return run + rsa + rsb

        run = scan(2 * i + 1, 2 * i, jnp.zeros((WQ, LANES), F32), True)

        def more(c):
            return (c[0] < i) & (c[1] > 0)

        def step(c):
            run = scan(2 * i - 1 - 2 * c[0], 2 * i - 2 - 2 * c[0], c[2], False)
            return c[0] + 1, (jnp.min(run) < SB_CUTOFF).astype(jnp.int32), run

        npairs, _, _ = lax.while_loop(more, step, (jnp.int32(0), (jnp.min(run) < SB_CUTOFF).astype(jnp.int32), run))

        def keys(j):
            ks = pl.multiple_of(j * BK, BK)
            return ks, _pair_stack(k_ref[pl.ds(ks, BK), :], lane)

        def gates(z, j, masked):
            neg_abs = lax.bitcast_convert_type(lax.bitcast_convert_type(z, jnp.uint32) | jnp.uint32(0x80000000), F32)
            u = jnp.exp(neg_abs)
            opu = 1.0 + u
            sp = jnp.maximum(z, 0.0) + jnp.log(opu)
            if masked:
                sp = jnp.where((col + j * BK) < row, sp, 0.0)
            return jnp.where(z >= 0.0, 1.0, u) / opu, _mm(_split2(sp), u_ge)

        def weights(z, cum, ks, j, masked):
            wl = jnp.exp(z - cum)
            if masked:
                wl = jnp.where((col + j * BK) < row, wl, 0.0)
            dfo = (jnp.exp(-later_ref[j]) * dof).astype(BF16)
            e = _mm_nt(dfo, _pair_stack(v_ref[pl.ds(ks, BK), :], lane)) * wl
            return wl.astype(BF16), dfo, e

        def scores_grad(e, sig, erun, j, masked):
            ecum = _mm(e.astype(BF16), u_le) + _heads_wide(erun)
            dz = e - sig * ecum
            if masked:
                dz = jnp.where((col + j * BK) < row, dz, 0.0)
            return dz.astype(BF16), _heads_narrow(ecum, BK - 1, lane)

        def scatter(ks, dzb, wlb, dfo):
            rk = _mm_tn(dzb, qs)
            dk_ref[pl.ds(ks, BK), :] += jnp.where(lane < 64, rk[0:BK], rk[BK:2 * BK])
            rv = _mm_tn(wlb, dfo)
            dv_ref[pl.ds(ks, BK), :] += jnp.where(lane < 64, rv[0:BK], rv[BK:2 * BK])

        def pair(ja, carry, masked):
            dq, erun = carry
            jb = ja + 1
            ksa, kbda = keys(ja)
            ksb, kbdb = keys(jb)
            za, zb = _mm_nt(qs, kbda), _mm_nt(qs, kbdb)
            siga, cuma = gates(za, ja, masked)
            sigb, cumb = gates(zb, jb, masked)
            wla, dfoa, ea = weights(za, cuma, ksa, ja, masked)
            wlb, dfob, eb = weights(zb, cumb, ksb, jb, masked)
            dza, eruna = scores_grad(ea, siga, erun, ja, masked)
            dzb, erunb = scores_grad(eb, sigb, eruna, jb, masked)
            dq = dq + _mm(jnp.concatenate([dza, dzb], axis=1), jnp.concatenate([kbda, kbdb], axis=0))
            scatter(ksa, dza, wla, dfoa)
            scatter(ksb, dzb, wlb, dfob)
            return dq, erunb

        zero = jnp.zeros((WQ, LANES), F32)
        first = 2 * (i - npairs)
        carry = lax.fori_loop(0, npairs, lambda t, c: pair(first + 2 * t, c, False), (zero, zero))
        dq, _ = pair(2 * i, carry, True)
        dq_ref[...] = (dq * (HEAD_DIM ** -0.5)).astype(BF16)

    qspec = pl.BlockSpec((WQ, LANES), lambda p, i: (i, p))
    kspec = pl.BlockSpec((s, LANES), lambda p, i: (0, p))
    return pl.pallas_call(
        body, name="sb_bwd", grid=(4, s // WQ),
        out_shape=(jax.ShapeDtypeStruct((s, 512), BF16), jax.ShapeDtypeStruct((s, 512), F32),
                   jax.ShapeDtypeStruct((s, 512), F32)),
        in_specs=[qspec, kspec, kspec, qspec], out_specs=(qspec, kspec, kspec),
        scratch_shapes=[pltpu.VMEM((s // BK, WQ, LANES), F32)],
        compiler_params=pltpu.CompilerParams(vmem_limit_bytes=VMEM_ATTN),
    )(q, k, v, do)


MLA_SCALE = (QK_NOPE + QK_ROPE) ** -0.5
LOG2E = 1.4426950408889634


def _mla_keys(kb):
    zero = jnp.zeros((BK, LANES), kb.dtype)
    return jnp.concatenate([jnp.concatenate([kb[:, 0:LANES], zero], axis=1),
                            jnp.concatenate([zero, kb[:, LANES:2 * LANES]], axis=1)], axis=0)


def _mla_fwd(qc, kc, v):
    s = qc.shape[0]

    def body(q_ref, k_ref, v_ref, o_ref, l_ref):
        i = pl.program_id(1)
        lane = lax.broadcasted_iota(jnp.int32, (1, LANES), 1)
        rowc = (lax.broadcasted_iota(jnp.int32, (WQ, BK), 0) + i * WQ) // 64
        col = lax.broadcasted_iota(jnp.int32, (WQ, BK), 1)
        qw = q_ref[...]
        ind0 = jnp.broadcast_to(jnp.where(lane < 64, 1.0, 0.0).astype(BF16), (BK, LANES))
        ind1 = jnp.broadcast_to(jnp.where(lane >= 64, 1.0, 0.0).astype(BF16), (BK, LANES))

        def raw(j):
            ks = pl.multiple_of(j * BK, BK)
            return _mm_nt(qw, _mla_keys(k_ref[pl.ds(ks, BK), :]))

        def scores(z, j, masked):
            z = z * (MLA_SCALE * LOG2E)
            z0, z1 = z[:, 0:BK], z[:, BK:2 * BK]
            if masked:
                valid = ((col + j * BK) // 64) <= rowc
                z0, z1 = jnp.where(valid, z0, -1e30), jnp.where(valid, z1, -1e30)
            return z0, z1

        def values(j):
            ks = pl.multiple_of(j * BK, BK)
            vb = v_ref[pl.ds(ks, BK), :]
            zero = jnp.zeros_like(vb)
            return jnp.concatenate([jnp.concatenate([jnp.where(lane < 64, vb, zero), ind0], axis=1),
                                    jnp.concatenate([jnp.where(lane >= 64, vb, zero), ind1], axis=1)], axis=0)

        def pair(ja, za, zb, carry, masked):
            m0, m1, l, acc = carry
            a0, a1 = scores(za, ja, masked)
            b0, b1 = scores(zb, ja + 1, masked)
            n0 = jnp.maximum(m0, jnp.max(jnp.maximum(a0, b0), axis=1, keepdims=True))
            n1 = jnp.maximum(m1, jnp.max(jnp.maximum(a1, b1), axis=1, keepdims=True))
            p = jnp.concatenate([jnp.exp2(a0 - n0), jnp.exp2(a1 - n1), jnp.exp2(b0 - n0), jnp.exp2(b1 - n1)], axis=1)
            pv = _mm(p.astype(BF16), jnp.concatenate([values(ja), values(ja + 1)], axis=0))
            a = jnp.where(lane < 64, jnp.exp2(m0 - n0), jnp.exp2(m1 - n1))
            return n0, n1, a * l + pv[:, LANES:2 * LANES], a * acc + pv[:, 0:LANES]

        def step(jj, c):
            zna, znb = raw(2 * jj + 2), raw(2 * jj + 3)
            return pair(2 * jj, c[4], c[5], c[0:4], False) + (zna, znb)

        neg = jnp.full((WQ, 1), -1e30, F32)
        zero = jnp.zeros((WQ, LANES), F32)
        c = lax.fori_loop(0, i, step, (neg, neg, zero, zero, raw(0), raw(1)))
        m0, m1, l, acc = pair(2 * i, c[4], c[5], c[0:4], True)
        o_ref[...] = acc / l
        l_ref[...] = jnp.where(lane < 64, m0, m1) + jnp.log2(l)

    qspec = pl.BlockSpec((WQ, 2 * LANES), lambda p, i: (i, p))
    kspec = pl.BlockSpec((s, 2 * LANES), lambda p, i: (0, p))
    vspec = pl.BlockSpec((s, LANES), lambda p, i: (0, p))
    ospec = pl.BlockSpec((WQ, LANES), lambda p, i: (i, p))
    return pl.pallas_call(
        body, name="mla_fwd", grid=(4, s // WQ),
        out_shape=(jax.ShapeDtypeStruct((s, 512), F32), jax.ShapeDtypeStruct((s, 512), F32)),
        in_specs=[qspec, kspec, vspec], out_specs=(ospec, ospec),
        compiler_params=pltpu.CompilerParams(vmem_limit_bytes=VMEM_ATTN),
    )(qc, kc, v)


def _mla_bwd(qc, kc, v, do, lse, delta):
    s = qc.shape[0]

    def body(q_ref, k_ref, v_ref, do_ref, l_ref, d_ref, dq_ref, dk_ref, dv_ref):
        i = pl.program_id(1)

        @pl.when(i == 0)
        def _():
            dk_ref[...] = jnp.zeros_like(dk_ref)
            dv_ref[...] = jnp.zeros_like(dv_ref)

        lane = lax.broadcasted_iota(jnp.int32, (1, LANES), 1)
        rowc = (lax.broadcasted_iota(jnp.int32, (WQ, BK), 0) + i * WQ) // 64
        col = lax.broadcasted_iota(jnp.int32, (WQ, BK), 1)
        qw = q_ref[...]
        dob = do_ref[...]
        dos = (dob.astype(F32) * MLA_SCALE).astype(BF16)
        lp = l_ref[...]
        dp = d_ref[...] * MLA_SCALE
        lse0, lse1 = lp[:, 0:1], lp[:, 64:65]
        dl0, dl1 = dp[:, 0:1], dp[:, 64:65]

        def raw(j):
            return _mm_nt(qw, _mla_keys(k_ref[pl.ds(pl.multiple_of(j * BK, BK), BK), :]))

        def probs(j, z, masked):
            z = z * (MLA_SCALE * LOG2E)
            p0 = jnp.exp2(z[:, 0:BK] - lse0)
            p1 = jnp.exp2(z[:, BK:2 * BK] - lse1)
            if masked:
                valid = ((col + j * BK) // 64) <= rowc
                p0, p1 = jnp.where(valid, p0, 0.0), jnp.where(valid, p1, 0.0)
            return p0, p1

        def scores_grad(ks, p0, p1):
            dw = _mm_nt(dos, _pair_stack(v_ref[pl.ds(ks, BK), :], lane))
            return jnp.concatenate([p0 * (dw[:, 0:BK] - dl0), p1 * (dw[:, BK:2 * BK] - dl1)], axis=1).astype(BF16)

        def scatter(ks, dzb, p0, p1):
            rk = _mm_tn(dzb, qw)
            dk_ref[pl.ds(ks, BK), :] += jnp.concatenate([rk[0:BK, 0:LANES], rk[BK:2 * BK, LANES:2 * LANES]], axis=1)
            rv = _mm_tn(jnp.concatenate([p0, p1], axis=1).astype(BF16), dob)
            dv_ref[pl.ds(ks, BK), :] += jnp.where(lane < 64, rv[0:BK], rv[BK:2 * BK])

        def pair(ja, za, zb, dq, masked):
            ksa, ksb = pl.multiple_of(ja * BK, BK), pl.multiple_of((ja + 1) * BK, BK)
            pa0, pa1 = probs(ja, za, masked)
            pb0, pb1 = probs(ja + 1, zb, masked)
            dza = scores_grad(ksa, pa0, pa1)
            dzb = scores_grad(ksb, pb0, pb1)
            kbd = jnp.concatenate([_mla_keys(k_ref[pl.ds(ksa, BK), :]), _mla_keys(k_ref[pl.ds(ksb, BK), :])], axis=0)
            dq = dq + _mm(jnp.concatenate([dza, dzb], axis=1), kbd)
            scatter(ksa, dza, pa0, pa1)
            scatter(ksb, dzb, pb0, pb1)
            return dq

        dq = lax.fori_loop(0, i, lambda jj, c: pair(2 * jj, raw(2 * jj), raw(2 * jj + 1), c, False),
                           jnp.zeros((WQ, 2 * LANES), F32))
        dq_ref[...] = pair(2 * i, raw(2 * i), raw(2 * i + 1), dq, True)

    qspec = pl.BlockSpec((WQ, 2 * LANES), lambda p, i: (i, p))
    kspec = pl.BlockSpec((s, 2 * LANES), lambda p, i: (0, p))
    vspec = pl.BlockSpec((s, LANES), lambda p, i: (0, p))
    ospec = pl.BlockSpec((WQ, LANES), lambda p, i: (i, p))
    return pl.pallas_call(
        body, name="mla_bwd", grid=(4, s // WQ),
        out_shape=(jax.ShapeDtypeStruct((s, 1024), F32), jax.ShapeDtypeStruct((s, 1024), F32),
                   jax.ShapeDtypeStruct((s, 512), F32)),
        in_specs=[qspec, kspec, vspec, ospec, ospec, ospec], out_specs=(qspec, kspec, vspec),
        compiler_params=pltpu.CompilerParams(vmem_limit_bytes=VMEM_ATTN),
    )(qc, kc, v, do, lse, delta)


def _post(x, p, tgt, sbo, mlao, sbg, mlag, gsb, gmla, wout, gpost, wple, gple, wpg, bpg):
    s = x.shape[0]

    def body(x_ref, p_ref, t_ref, sbo_ref, mlao_ref, sbg_ref, mlag_ref, gsb_ref, gmla_ref, wout_ref,
             gpost_ref, wple_ref, gple_ref, wpg_ref, bpg_ref,
             dsbo_ref, dmlao_ref, delta_ref, dsbg_ref, dmlag_ref, dxres_ref, dwout_ref, dwpg_ref, dwple_ref, vec_ref):
        i = pl.program_id(0)

        @pl.when(i == 0)
        def _():
            dwout_ref[...] = jnp.zeros_like(dwout_ref)
            dwpg_ref[...] = jnp.zeros_like(dwpg_ref)
            dwple_ref[...] = jnp.zeros_like(dwple_ref)
            vec_ref[...] = jnp.zeros_like(vec_ref)

        bd = _blockdiag(512, HEAD_DIM)
        inv_hd = 1.0 / HEAD_DIM

        def head_fwd(o, g, gate):
            r = lax.rsqrt(_seg(o * o, bd) * inv_hd + EPS)
            hat = o * r
            n = hat * g
            sg = _sigmoid(gate)
            return hat, r, n, sg, n * (gate * sg)

        sbo, mlao, sbg_v, mlag_v = sbo_ref[...], mlao_ref[...], sbg_ref[...], mlag_ref[...]
        gsb_v, gmla_v = gsb_ref[...], gmla_ref[...]
        sb_hat, sb_r, sb_n, sb_sg, sb_y = head_fwd(sbo, gsb_v, sbg_v)
        ml_hat, ml_r, ml_n, ml_sg, ml_y = head_fwd(mlao, gmla_v, mlag_v)
        mix = jnp.concatenate([sb_y, ml_y], axis=1).astype(BF16)
        y = _mm(mix, wout_ref[...])
        ry = lax.rsqrt(_rowmean(y * y) + EPS)
        y_hat = y * ry
        gpost_v = gpost_ref[...]
        x1 = x_ref[...] + y_hat * gpost_v
        pb = p_ref[...].astype(BF16)
        pl_ = _mm(pb, wple_ref[...])
        rp = lax.rsqrt(_rowmean(pl_ * pl_) + EPS)
        pl_hat = pl_ * rp
        gple_v = gple_ref[...]
        ple = pl_hat * gple_v
        x1b = x1.astype(BF16)
        gate = _sigmoid(_mm(x1b, wpg_ref[...]) + bpg_ref[...])
        err = x1 + ple * gate - t_ref[...]
        loss = 0.5 * jnp.sum(_rowmean(err * err))
        dout = err * (1.0 / D_MODEL)

        du = dout * ple * gate * (1.0 - gate)
        dub = du.astype(BF16)
        dple = dout * gate
        dx1 = dout + _mm_nt(dub, wpg_ref[...])
        dwpg_ref[...] += _mm_tn(x1b, dub)
        dplh = dple * gple_v
        dpl = rp * (dplh - pl_hat * _rowmean(dplh * pl_hat))
        dwple_ref[...] += _mm_tn(pb, dpl.astype(BF16))
        dxres_ref[...] = dx1
        dyh = dx1 * gpost_v
        dy = ry * (dyh - y_hat * _rowmean(dyh * y_hat))
        dyb = dy.astype(BF16)
        dwout_ref[...] += _mm_tn(mix, dyb)
        dmix = _mm_nt(dyb, wout_ref[...])

        def head_bwd(dyv, hat, r, n, sg, g, gate):
            dn = dyv * (gate * sg)
            dgate = dyv * n * (sg * (1.0 + gate * (1.0 - sg)))
            dhat = dn * g
            do = r * (dhat - hat * (_seg(dhat * hat, bd) * inv_hd))
            return do, dgate, _colsum(dn * hat)

        dsbo, dsbg, dg_sb = head_bwd(dmix[:, 0:512], sb_hat, sb_r, sb_n, sb_sg, gsb_v, sbg_v)
        dmlao, dmlag, dg_ml = head_bwd(dmix[:, 512:1024], ml_hat, ml_r, ml_n, ml_sg, gmla_v, mlag_v)
        dsbo_ref[...] = dsbo.astype(BF16)
        dmlao_ref[...] = dmlao.astype(BF16)
        delta_ref[...] = _seg(dmlao * mlao, bd)
        dsbg_ref[...] = dsbg.astype(BF16)
        dmlag_ref[...] = dmlag.astype(BF16)
        vec_ref[pl.ds(0, 1), :] += _colsum(dx1 * y_hat)
        vec_ref[pl.ds(1, 1), :] += _colsum(dple * pl_hat)
        vec_ref[pl.ds(2, 1), :] += _colsum(du)
        vec_ref[pl.ds(3, 1), :] += jnp.concatenate([dg_sb, dg_ml], axis=1)
        vec_ref[pl.ds(4, 1), :] += jnp.full((1, D_MODEL), loss, F32)

    out_shape = (
        jax.ShapeDtypeStruct((s, 512), BF16), jax.ShapeDtypeStruct((s, 512), BF16), jax.ShapeDtypeStruct((s, 512), F32),
        jax.ShapeDtypeStruct((s, 512), BF16), jax.ShapeDtypeStruct((s, 512), BF16), jax.ShapeDtypeStruct((s, D_MODEL), F32),
        jax.ShapeDtypeStruct((D_MODEL, D_MODEL), F32), jax.ShapeDtypeStruct((D_MODEL, D_MODEL), F32),
        jax.ShapeDtypeStruct((PLE_DIM, D_MODEL), F32), jax.ShapeDtypeStruct((8, D_MODEL), F32),
    )
    return pl.pallas_call(
        body, name="post_fwd_bwd", grid=(s // TM,), out_shape=out_shape,
        in_specs=[_rows(D_MODEL), _rows(PLE_DIM), _rows(D_MODEL), _rows(512), _rows(512), _rows(512), _rows(512),
                  _full((1, 512)), _full((1, 512)), _full((D_MODEL, D_MODEL)),
                  _full((1, D_MODEL)), _full((PLE_DIM, D_MODEL)), _full((1, D_MODEL)), _full((D_MODEL, D_MODEL)),
                  _full((1, D_MODEL))],
        out_specs=(_rows(512), _rows(512), _rows(512), _rows(512), _rows(512), _rows(D_MODEL),
                   _full((D_MODEL, D_MODEL)), _full((D_MODEL, D_MODEL)), _full((PLE_DIM, D_MODEL)), _full((8, D_MODEL))),
        compiler_params=pltpu.CompilerParams(vmem_limit_bytes=VMEM_DENSE),
    )(x, p, tgt, sbo, mlao, sbg, mlag, gsb, gmla, wout, gpost, wple, gple, wpg, bpg)


def _pre_bwd(x, dxres, dsbq, dsbk, dsbv, dsbg, dmlag, dqc, dkc, dmv, cq, ckv, tabs, gpre, win, gq, wuq, gkv, wk, wv):
    s = x.shape[0]
    c_t, sa_t, sb_t = tabs

    def body(x_ref, dxres_ref, dsbq_ref, dsbk_ref, dsbv_ref, dsbg_ref, dmlag_ref, dqc_ref, dkc_ref, dmv_ref, cq_ref,
             ckv_ref, c_ref, sa_ref, sb_ref, gpre_ref, win_ref, gq_ref, wuq_ref, gkv_ref, wk_ref, wv_ref,
             gx_ref, dwin_ref, dwuq_ref, dwk_ref, dwv_ref, vec_ref, dwin_acc):
        i = pl.program_id(0)

        @pl.when(i == 0)
        def _():
            dwin_acc[...] = jnp.zeros_like(dwin_acc)
            dwuq_ref[...] = jnp.zeros_like(dwuq_ref)
            dwk_ref[...] = jnp.zeros_like(dwk_ref)
            dwv_ref[...] = jnp.zeros_like(dwv_ref)
            vec_ref[...] = jnp.zeros_like(vec_ref)

        lane = lax.broadcasted_iota(jnp.int32, (1, LANES), 1)
        c1, sa1, sb1 = c_ref[...], sa_ref[...], sb_ref[...]
        c8, sa8, sb8 = jnp.tile(c1, (1, 8)), jnp.tile(sa1, (1, 8)), jnp.tile(sb1, (1, 8))

        def norm_bwd(dn, hat, r, g):
            t = dn * g
            return r * (t - hat * _rowmean(t * hat)), _colsum(dn * hat)

        dqeb = _rope_bwd(dqc_ref[...], c8, sa8, sb8).astype(BF16)
        cq = cq_ref[...]
        rq = lax.rsqrt(_rowmean(cq * cq) + EPS)
        cq_hat = cq * rq
        gq_v = gq_ref[...]
        dwuq_ref[...] += _mm_tn((cq_hat * gq_v).astype(BF16), dqeb)
        dcq, dg_q = norm_bwd(_mm_nt(dqeb, wuq_ref[...]), cq_hat, rq, gq_v)

        dkc = dkc_ref[...]
        dkcb = dkc.astype(BF16)
        dmvb = dmv_ref[...].astype(BF16)
        ckv = ckv_ref[...]
        rkv = lax.rsqrt(_rowmean(ckv * ckv) + EPS)
        ckv_hat = ckv * rkv
        gkv_v = gkv_ref[...]
        ckvnb = (ckv_hat * gkv_v).astype(BF16)
        dwk_ref[...] += _mm_tn(ckvnb, dkcb)
        dwv_ref[...] += _mm_tn(ckvnb, dmvb)
        dckv, dg_kv = norm_bwd(_mm_nt(dkcb, wk_ref[...]) + _mm_nt(dmvb, wv_ref[...]), ckv_hat, rkv, gkv_v)

        dkr = dkc[:, 0:LANES]
        for hh in range(1, 8):
            dkr = dkr + dkc[:, LANES * hh:LANES * (hh + 1)]
        dkr = _rope_bwd(dkr, c1, sa1, sb1)
        dkr = jnp.where((lane >= 64) & (lane < 96), dkr, 0.0)

        dproj = jnp.concatenate([dsbq_ref[...], dsbk_ref[...].astype(BF16), dsbv_ref[...].astype(BF16), dsbg_ref[...],
                                 dcq.astype(BF16), dckv.astype(BF16), dkr.astype(BF16), dmlag_ref[...]], axis=1)
        xv = x_ref[...]
        r1 = lax.rsqrt(_rowmean(xv * xv) + EPS)
        x_hat = xv * r1
        gpre_v = gpre_ref[...]
        dwin_acc[...] += _mm_tn((x_hat * gpre_v).astype(BF16), dproj)
        dx, dg_pre = norm_bwd(_mm_nt(dproj, win_ref[...]), x_hat, r1, gpre_v)
        gx_ref[...] = dxres_ref[...] + dx
        vec_ref[pl.ds(0, 1), :] += dg_pre
        vec_ref[pl.ds(1, 1), :] += jnp.concatenate([dg_q, dg_kv, jnp.zeros((1, D_MODEL - Q_LORA - KV_LORA), F32)], axis=1)

        @pl.when(i == pl.num_programs(0) - 1)
        def _():
            pltpu.sync_copy(dwin_acc, dwin_ref)

    out_shape = (
        jax.ShapeDtypeStruct((s, D_MODEL), F32), jax.ShapeDtypeStruct((D_MODEL, D_EXT), F32),
        jax.ShapeDtypeStruct((Q_LORA, 1024), F32), jax.ShapeDtypeStruct((KV_LORA, 1024), F32),
        jax.ShapeDtypeStruct((KV_LORA, 512), F32), jax.ShapeDtypeStruct((8, D_MODEL), F32),
    )
    return pl.pallas_call(
        body, name="pre_bwd", grid=(s // TM,), out_shape=out_shape,
        in_specs=[_rows(D_MODEL), _rows(D_MODEL), _rows(512), _rows(512), _rows(512), _rows(512), _rows(512),
                  _rows(1024), _rows(1024), _rows(512), _rows(Q_LORA), _rows(KV_LORA), _rows(LANES), _rows(LANES),
                  _rows(LANES), _full((1, D_MODEL)), _full((D_MODEL, D_EXT)), _full((1, Q_LORA)), _full((Q_LORA, 1024)),
                  _full((1, KV_LORA)), _full((KV_LORA, 1024)), _full((KV_LORA, 512))],
        out_specs=(_rows(D_MODEL), pl.BlockSpec(memory_space=pl.ANY), _full((Q_LORA, 1024)), _full((KV_LORA, 1024)),
                   _full((KV_LORA, 512)), _full((8, D_MODEL))),
        scratch_shapes=[pltpu.VMEM((D_MODEL, D_EXT), F32)],
        compiler_params=pltpu.CompilerParams(vmem_limit_bytes=VMEM_DENSE),
    )(x, dxres, dsbq, dsbk, dsbv, dsbg, dmlag, dqc, dkc, dmv, cq, ckv, c_t, sa_t, sb_t, gpre, win, gq, wuq, gkv, wk, wv)


def _place():
    return lax.axis_index("x"), lax.axis_index("y"), lax.axis_index("c")


def _allgather_weights(shards):
    n = len(shards)

    def body(*refs):
        ins, outs, send_sems, recv_sems = refs[:n], refs[n:2 * n], refs[2 * n], refs[2 * n + 1]
        x, y, c = _place()
        me, sib = (x, y, c), (x, y, 1 - c)
        chips = [(1 - x, y), (x, 1 - y), (1 - x, 1 - y)]

        def half(t, chip, hc):
            rows = shards[t].shape[0] // 2
            return outs[t].at[2 * chip[0] + chip[1], pl.ds(pl.multiple_of(hc * rows, 16), rows), :]

        def copy(k, t, chip, hc, to):
            return pltpu.make_async_remote_copy(src_ref=half(t, chip, hc), dst_ref=half(t, chip, hc), send_sem=send_sems.at[k],
                                                recv_sem=recv_sems.at[k], device_id=to, device_id_type=MESH)

        first, passed = [], []
        for t in range(n):
            outs[t][2 * x + y] = ins[t][...].astype(BF16)
            for j, chip in enumerate(chips):
                cp = copy(6 * t + j, t, (x, y), c, (*chip, c))
                cp.start()
                first.append(cp)
        for t in range(n):
            for j, chip in enumerate(chips):
                copy(6 * t + j, t, chip, c, me).wait_recv()
                cp = copy(6 * t + 3 + j, t, chip, c, sib)
                cp.start()
                passed.append(cp)
        for t in range(n):
            for j, chip in enumerate(chips):
                copy(6 * t + 3 + j, t, chip, 1 - c, me).wait_recv()
        for cp in first + passed:
            cp.wait_send()

    return pl.pallas_call(
        body, name="allgather_weights",
        out_shape=tuple(jax.ShapeDtypeStruct((N_SHARD,) + a.shape, BF16) for a in shards),
        in_specs=[pl.BlockSpec(memory_space=pltpu.VMEM)] * n, out_specs=(pl.BlockSpec(memory_space=pltpu.VMEM),) * n,
        scratch_shapes=[pltpu.SemaphoreType.DMA((6 * n,)), pltpu.SemaphoreType.DMA((6 * n,))],
        compiler_params=pltpu.CompilerParams(vmem_limit_bytes=VMEM_ATTN),
    )(*shards)


def _reduce_scatter_grads(gsh, vec):
    n = len(gsh)
    halves = [a.shape[1] // 2 for a in gsh]

    def body(*refs):
        g_refs, vec_ref, f_refs, vsum_ref = refs[:n], refs[n], refs[n + 1:2 * n + 1], refs[2 * n + 1]
        scr = refs[2 * n + 2:]
        accs, sibs, sbufs, rbufs = scr[0:n], scr[n:2 * n], scr[2 * n:3 * n], scr[3 * n:4 * n]
        vrecv, local_sems, send_sems, recv_sems = scr[4 * n:4 * n + 4]
        x, y, c = _place()
        me, sib = (x, y, c), (x, y, 1 - c)
        mine = 2 * x + y
        chips = [(1 - x, y), (x, 1 - y), (1 - x, 1 - y)]

        def remote(k, src, dst, to):
            return pltpu.make_async_remote_copy(src_ref=src, dst_ref=dst, send_sem=send_sems.at[k], recv_sem=recv_sems.at[k],
                                                device_id=to, device_id_type=MESH)

        def half3(ref, t, hc):
            return ref.at[:, pl.ds(pl.multiple_of(hc * halves[t], 8), halves[t]), :]

        def half2(ref, t, hc):
            return ref.at[pl.ds(pl.multiple_of(hc * halves[t], 8), halves[t]), :]

        loads, sends = [], []
        for t in range(n):
            ld = pltpu.make_async_copy(half3(g_refs[t], t, c), accs[t], local_sems.at[t])
            ld.start()
            loads.append(ld)
            cp = remote(t, half3(g_refs[t], t, 1 - c), sibs[t], sib)
            cp.start()
            sends.append(cp)

        my_dev = 4 * x + 2 * y + c
        vrecv[my_dev] = vec_ref[...]
        for k in range(1, 8):
            to = (x ^ ((k >> 2) & 1), y ^ ((k >> 1) & 1), c ^ (k & 1))
            cp = remote(n + k - 1, vec_ref, vrecv.at[my_dev], to)
            cp.start()
            sends.append(cp)

        for t in range(n):
            loads[t].wait()
            remote(t, half3(g_refs[t], t, 1 - c), sibs[t], me).wait_recv()
            for k in range(N_SHARD):
                accs[t][k] = accs[t][k] + sibs[t][k]
            for j, chip in enumerate(chips):
                idx = 2 * chip[0] + chip[1]
                sbufs[t][idx] = accs[t][idx].astype(BF16)
                cp = remote(n + 7 + 3 * t + j, sbufs[t].at[idx], rbufs[t].at[mine], (*chip, c))
                cp.start()
                sends.append(cp)

        for t in range(n):
            total = accs[t][mine]
            for j, chip in enumerate(chips):
                idx = 2 * chip[0] + chip[1]
                remote(n + 7 + 3 * t + j, sbufs[t].at[idx], rbufs[t].at[idx], me).wait_recv()
                total = total + rbufs[t][idx].astype(F32)
            half2(f_refs[t], t, c)[...] = total
            cp = remote(4 * n + 7 + t, half2(f_refs[t], t, c), half2(f_refs[t], t, c), sib)
            cp.start()
            sends.append(cp)
        for t in range(n):
            remote(4 * n + 7 + t, half2(f_refs[t], t, 1 - c), half2(f_refs[t], t, 1 - c), me).wait_recv()

        for k in range(1, 8):
            src_dev = 4 * (x ^ ((k >> 2) & 1)) + 2 * (y ^ ((k >> 1) & 1)) + (c ^ (k & 1))
            remote(n + k - 1, vec_ref, vrecv.at[src_dev], me).wait_recv()
        vs = vrecv[0]
        for d in range(1, 8):
            vs = vs + vrecv[d]
        vsum_ref[...] = vs

        for cp in sends:
            cp.wait_send()

    nsem = 5 * n + 7
    half_shapes = [(N_SHARD, h, a.shape[2]) for h, a in zip(halves, gsh)]
    return pl.pallas_call(
        body, name="reduce_scatter_grads",
        out_shape=tuple(jax.ShapeDtypeStruct(a.shape[1:], F32) for a in gsh) + (jax.ShapeDtypeStruct((VEC_ROWS, 1024), F32),),
        in_specs=[pl.BlockSpec(memory_space=pl.ANY)] * n + [pl.BlockSpec(memory_space=pltpu.VMEM)],
        out_specs=(pl.BlockSpec(memory_space=pltpu.VMEM),) * (n + 1),
        scratch_shapes=([pltpu.VMEM(s_, F32) for s_ in half_shapes] * 2 + [pltpu.VMEM(s_, BF16) for s_ in half_shapes] * 2
                        + [pltpu.VMEM((8, VEC_ROWS, 1024), F32), pltpu.SemaphoreType.DMA((n,)),
                           pltpu.SemaphoreType.DMA((nsem,)), pltpu.SemaphoreType.DMA((nsem,))]),
        compiler_params=pltpu.CompilerParams(vmem_limit_bytes=56 * 1024 * 1024),
    )(*gsh, vec)


def _adamw(w, g, m, v):
    rows, cols = w.shape
    tr = rows if rows <= 256 else 256

    def body(w_ref, g_ref, m_ref, v_ref, d_ref, nm_ref, nv_ref):
        d_ref[...], nm_ref[...], nv_ref[...] = _adam_math(w_ref[...], g_ref[...], m_ref[...], v_ref[...])

    spec = pl.BlockSpec((tr, cols), lambda i: (i, 0))
    shp = jax.ShapeDtypeStruct((rows, cols), F32)
    return pl.pallas_call(body, name="adamw", grid=(rows // tr,), out_shape=(shp, shp, shp),
                          in_specs=[spec] * 4, out_specs=(spec,) * 3)(w, g, m, v)


def _adam_math(w, g, m, v):
    m2 = ADAM_B1 * m + (1.0 - ADAM_B1) * g
    v2 = ADAM_B2 * v + (1.0 - ADAM_B2) * (g * g)
    m_hat = m2 / (1.0 - ADAM_B1 ** ADAM_STEP)
    v_hat = v2 / (1.0 - ADAM_B2 ** ADAM_STEP)
    return -ADAM_LR * (m_hat / (jnp.sqrt(v_hat) + ADAM_EPS) + ADAM_WD * w), m2, v2


def _adamw_small(vsum, w, m, v):
    names = [name for name, _, _, _ in _VEC_LAYOUT]
    k = len(names)

    def body(*refs):
        vs_ref, w_refs, m_refs, v_refs = refs[0], refs[1:1 + k], refs[1 + k:1 + 2 * k], refs[1 + 2 * k:1 + 3 * k]
        outs = refs[1 + 3 * k:]
        for idx, (_, r, c0, width) in enumerate(_VEC_LAYOUT):
            gv = vs_ref[pl.ds(r, 1), pl.ds(c0, width)]
            d, m2, v2 = _adam_math(w_refs[idx][...], gv, m_refs[idx][...], v_refs[idx][...])
            outs[idx][...], outs[k + idx][...], outs[2 * k + idx][...], outs[3 * k + idx][...] = gv, d, m2, v2

    shapes = tuple(jax.ShapeDtypeStruct(w[name].shape, F32) for name in names)
    res = pl.pallas_call(
        body, name="adamw_small", out_shape=shapes * 4,
        in_specs=[pl.BlockSpec(memory_space=pltpu.VMEM)] * (1 + 3 * k), out_specs=(pl.BlockSpec(memory_space=pltpu.VMEM),) * (4 * k),
    )(vsum, *[w[name] for name in names], *[m[name] for name in names], *[v[name] for name in names])
    return tuple({name: res[part * k + idx] for idx, name in enumerate(names)} for part in range(4))


_BIG = ("w_in", "w_uq", "w_ukv", "w_out", "w_ple", "w_ple_gate")
_COL_SHARDED = ("w_in", "w_uq", "w_ukv", "w_ple")


def _join_shards(parts):
    cols = lambda a: a.transpose(1, 0, 2).reshape(a.shape[1], N_SHARD * a.shape[2])
    rows = lambda a: a.reshape(N_SHARD * a.shape[1], a.shape[2])
    return {n: (cols if n in _COL_SHARDED else rows)(parts[n]) for n in _BIG}


def _split_shards(full):
    cols = lambda a: a.reshape(a.shape[0], N_SHARD, a.shape[1] // N_SHARD).transpose(1, 0, 2)
    rows = lambda a: a.reshape(N_SHARD, a.shape[0] // N_SHARD, a.shape[1])
    return {n: (cols if n in _COL_SHARDED else rows)(full[n]) for n in _BIG}


def _extend_weights(w):
    win = w["w_in"]
    zeros = lambda r, c: jnp.zeros((r, c), win.dtype)
    win_ext = jnp.concatenate([win[:, :2432], zeros(D_MODEL, 64), win[:, 2432:2464], zeros(D_MODEL, 32), win[:, 2464:]], axis=1)
    wuq_ext = jnp.pad(w["w_uq"].reshape(Q_LORA, 8, 96), ((0, 0), (0, 0), (0, 32))).reshape(Q_LORA, 1024)
    wukv = w["w_ukv"].reshape(KV_LORA, 8, 128)
    wk_ext = jnp.pad(wukv[:, :, :64], ((0, 0), (0, 0), (0, 64))).reshape(KV_LORA, 1024)
    wv = wukv[:, :, 64:].reshape(KV_LORA, 512)
    return win_ext, wuq_ext, wk_ext, wv


def _contract_grads(dwin_ext, dwuq_ext, dwk_ext, dwv):
    dwin = jnp.concatenate([dwin_ext[:, :2432], dwin_ext[:, 2496:2528], dwin_ext[:, 2560:]], axis=1)
    dwuq = dwuq_ext.reshape(Q_LORA, 8, 128)[:, :, :96].reshape(Q_LORA, 768)
    dwukv = jnp.concatenate([dwk_ext.reshape(KV_LORA, 8, 128)[:, :, :64], dwv.reshape(KV_LORA, 8, 64)], axis=2)
    return dwin, dwuq, dwukv.reshape(KV_LORA, 1024)


def _rope_tables(positions):
    half = QK_ROPE // 2
    freq = ROPE_THETA ** (-jnp.arange(half, dtype=F32) / half)
    ang = positions.astype(F32)[:, None] * freq
    cos, sin = jnp.cos(ang), jnp.sin(ang)
    s = positions.shape[0]
    z = lambda n: jnp.zeros((s, n), F32)
    c_t = jnp.concatenate([jnp.ones((s, 64), F32), cos, cos, z(32)], axis=1)
    sa_t = jnp.concatenate([z(64), -sin, z(16), z(32)], axis=1)
    sb_t = jnp.concatenate([z(64), z(16), sin, z(32)], axis=1)
    return c_t, sa_t, sb_t


def _local_grads(x, p, positions, tgt, gains, wfull):
    win_ext, wuq_ext, wk_ext, wv = _extend_weights(wfull)
    wout, wple, wpg = wfull["w_out"], wfull["w_ple"], wfull["w_ple_gate"]
    tabs = _rope_tables(positions)
    g = gains
    sbq, sbk, sbv, sbg, mlag, cq, ckv, qc, kc, mv = _pre_fwd(x, tabs, g["norm_pre_g"], win_ext, g["q_norm_g"], wuq_ext,
                                                             g["kv_norm_g"], wk_ext, wv)
    sbo = _sb_fwd(sbq, sbk, sbv)
    mlao, lse = _mla_fwd(qc, kc, mv)
    dsbo, dmlao, delta, dsbg, dmlag, dxres, dwout, dwpg, dwple, vec_c = _post(
        x, p, tgt, sbo, mlao, sbg, mlag, g["sb_out_norm_g"], g["mla_out_norm_g"], wout, g["norm_post_g"], wple,
        g["ple_norm_g"], wpg, g["b_ple_gate"])
    dsbq, dsbk, dsbv = _sb_bwd(sbq, sbk, sbv, dsbo)
    dqc, dkc, dmv = _mla_bwd(qc, kc, mv, dmlao, lse, delta)
    gx, dwin_ext, dwuq_ext, dwk_ext, dwv, vec_d = _pre_bwd(
        x, dxres, dsbq, dsbk, dsbv, dsbg, dmlag, dqc, dkc, dmv, cq, ckv, tabs, g["norm_pre_g"], win_ext, g["q_norm_g"],
        wuq_ext, g["kv_norm_g"], wk_ext, wv)
    dwin, dwuq, dwukv = _contract_grads(dwin_ext, dwuq_ext, dwk_ext, dwv)
    grads = {"w_in": dwin, "w_uq": dwuq, "w_ukv": dwukv, "w_out": dwout, "w_ple": dwple, "w_ple_gate": dwpg}
    return gx, grads, jnp.concatenate([vec_c, vec_d], axis=0)


_VEC_LAYOUT = (("norm_post_g", 0, 0, 1024), ("ple_norm_g", 1, 0, 1024), ("b_ple_gate", 2, 0, 1024), ("sb_out_norm_g", 3, 0, 512),
               ("mla_out_norm_g", 3, 512, 512), ("norm_pre_g", 8, 0, 1024), ("q_norm_g", 9, 0, 256), ("kv_norm_g", 9, 256, 128))
_LOSS_ROW = 4
_WEIGHT_ORDER = ("norm_pre_g", "w_in", "q_norm_g", "w_uq", "kv_norm_g", "w_ukv", "sb_out_norm_g", "mla_out_norm_g", "w_out",
                 "norm_post_g", "w_ple", "ple_norm_g", "w_ple_gate", "b_ple_gate")


def kernel(x, p, positions, norm_pre_g, w_in, q_norm_g, w_uq, kv_norm_g, w_ukv, sb_out_norm_g, mla_out_norm_g, w_out, norm_post_g, w_ple, ple_norm_g, w_ple_gate, b_ple_gate, loss_target, m_norm_pre_g, m_w_in, m_q_norm_g, m_w_uq, m_kv_norm_g, m_w_ukv, m_sb_out_norm_g, m_mla_out_norm_g, m_w_out, m_norm_post_g, m_w_ple, m_ple_norm_g, m_w_ple_gate, m_b_ple_gate, v_norm_pre_g, v_w_in, v_q_norm_g, v_w_uq, v_kv_norm_g, v_w_ukv, v_sb_out_norm_g, v_mla_out_norm_g, v_w_out, v_norm_post_g, v_w_ple, v_ple_norm_g, v_w_ple_gate, v_b_ple_gate):
    w = {"norm_pre_g": norm_pre_g, "w_in": w_in[0], "q_norm_g": q_norm_g, "w_uq": w_uq[0], "kv_norm_g": kv_norm_g, "w_ukv": w_ukv[0],
         "sb_out_norm_g": sb_out_norm_g, "mla_out_norm_g": mla_out_norm_g, "w_out": w_out[0], "norm_post_g": norm_post_g,
         "w_ple": w_ple[0], "ple_norm_g": ple_norm_g, "w_ple_gate": w_ple_gate[0], "b_ple_gate": b_ple_gate}
    m = {"norm_pre_g": m_norm_pre_g, "w_in": m_w_in[0], "q_norm_g": m_q_norm_g, "w_uq": m_w_uq[0], "kv_norm_g": m_kv_norm_g,
         "w_ukv": m_w_ukv[0], "sb_out_norm_g": m_sb_out_norm_g, "mla_out_norm_g": m_mla_out_norm_g, "w_out": m_w_out[0],
         "norm_post_g": m_norm_post_g, "w_ple": m_w_ple[0], "ple_norm_g": m_ple_norm_g, "w_ple_gate": m_w_ple_gate[0],
         "b_ple_gate": m_b_ple_gate}
    v = {"norm_pre_g": v_norm_pre_g, "w_in": v_w_in[0], "q_norm_g": v_q_norm_g, "w_uq": v_w_uq[0], "kv_norm_g": v_kv_norm_g,
         "w_ukv": v_w_ukv[0], "sb_out_norm_g": v_sb_out_norm_g, "mla_out_norm_g": v_mla_out_norm_g, "w_out": v_w_out[0],
         "norm_post_g": v_norm_post_g, "w_ple": v_w_ple[0], "ple_norm_g": v_ple_norm_g, "w_ple_gate": v_w_ple_gate[0],
         "b_ple_gate": v_b_ple_gate}
    gathered = _allgather_weights([w[n] for n in _BIG])
    wfull = _join_shards(dict(zip(_BIG, gathered)))

    gx, grads, vec = _local_grads(x[0], p[0, 0], positions[0], loss_target[0], w, wfull)

    gsh = _split_shards(grads)
    *gred, vsum = _reduce_scatter_grads([gsh[n] for n in _BIG], vec)
    loss = vsum[_LOSS_ROW, 0]

    g, delta, new_m, new_v = _adamw_small(vsum, w, m, v)
    for n, gn in zip(_BIG, gred):
        g[n] = gn
        delta[n], new_m[n], new_v[n] = _adamw(w[n], gn, m[n], v[n])

    lead = lambda n, a: a[None] if n in _BIG else a
    return (loss, gx[None],
            *[lead(n, g[n]) for n in _WEIGHT_ORDER], *[lead(n, delta[n]) for n in _WEIGHT_ORDER],
            *[lead(n, new_m[n]) for n in _WEIGHT_ORDER], *[lead(n, new_v[n]) for n in _WEIGHT_ORDER])
```

```python
import jax
import jax.numpy as jnp
from jax import lax
from jax.experimental import pallas as pl
from jax.experimental.pallas import tpu as pltpu

F32 = jnp.float32
BF16 = jnp.bfloat16
MESH = pl.DeviceIdType.MESH

D_MODEL = 1024
HEAD_DIM = 64
D_SB = 512
D_MLA = 512
Q_LORA = 256
KV_LORA = 128
QK_NOPE = 64
QK_ROPE = 32
PLE_DIM = 256
D_IN = 2976
D_EXT = 3072
ROPE_THETA = 10000.0
EPS = 1e-6
N_SHARD = 4

ADAM_LR = 0.001
ADAM_B1 = 0.9
ADAM_B2 = 0.999
ADAM_EPS = 1e-08
ADAM_WD = 0.01
ADAM_STEP = 10

LANES = 128
BK = 128
WQ = 256
SB_CUTOFF = 120.0
TM = 256
VEC_ROWS = 16
VMEM_DENSE = 52 * 1024 * 1024
VMEM_ATTN = 40 * 1024 * 1024


def _mm(a, b):
    return jnp.dot(a, b, preferred_element_type=F32)


def _mm_nt(a, b):
    return lax.dot_general(a, b, (((1,), (1,)), ((), ())), preferred_element_type=F32)


def _mm_tn(a, b):
    return lax.dot_general(a, b, (((0,), (0,)), ((), ())), preferred_element_type=F32)


def _seg(a, bd):
    a1 = a.astype(BF16)
    r1 = a - a1.astype(F32)
    a2 = r1.astype(BF16)
    a3 = (r1 - a2.astype(F32)).astype(BF16)
    return _mm(a1, bd) + _mm(a2, bd) + _mm(a3, bd)


def _blockdiag(n, seg):
    r = lax.broadcasted_iota(jnp.int32, (n, n), 0) // seg
    c = lax.broadcasted_iota(jnp.int32, (n, n), 1) // seg
    return jnp.where(r == c, 1.0, 0.0).astype(BF16)


def _sigmoid(a):
    return 1.0 / (1.0 + jnp.exp(-a))


def _rowmean(a):
    return jnp.mean(a, axis=-1, keepdims=True)


def _colsum(a):
    return jnp.sum(a, axis=0, keepdims=True)


def _rope_fwd(a, c, sa, sb):
    w = a.shape[-1]
    return a * c + pltpu.roll(a, w - 16, 1) * sa + pltpu.roll(a, 16, 1) * sb


def _rope_bwd(g, c, sa, sb):
    w = g.shape[-1]
    return g * c + pltpu.roll(g * sa, 16, 1) + pltpu.roll(g * sb, w - 16, 1)


def _full(shape):
    return pl.BlockSpec(shape, lambda *_: (0,) * len(shape))


def _rows(width, tm=TM):
    return pl.BlockSpec((tm, width), lambda i: (i, 0))


def _pre_fwd(x, tabs, gpre, win, gq, wuq, gkv, wk, wv):
    s = x.shape[0]
    c_t, sa_t, sb_t = tabs

    def body(x_ref, c_ref, sa_ref, sb_ref, gpre_ref, win_ref, gq_ref, wuq_ref, gkv_ref, wk_ref, wv_ref,
             sbq_ref, sbk_ref, sbv_ref, sbg_ref, mlag_ref, cq_ref, ckv_ref, qc_ref, kc_ref, mv_ref):
        xv = x_ref[...]
        r1 = lax.rsqrt(_rowmean(xv * xv) + EPS)
        h = (xv * r1 * gpre_ref[...]).astype(BF16)
        proj = _mm(h, win_ref[...])
        sbq_ref[...] = proj[:, 0:512].astype(BF16)
        sbk_ref[...] = proj[:, 512:1024].astype(BF16)
        sbv_ref[...] = proj[:, 1024:1536].astype(BF16)
        sbg_ref[...] = proj[:, 1536:2048]
        cq = proj[:, 2048:2304]
        ckv = proj[:, 2304:2432]
        kr = proj[:, 2432:2560]
        mlag_ref[...] = proj[:, 2560:3072]
        cq_ref[...] = cq
        ckv_ref[...] = ckv
        c1, sa1, sb1 = c_ref[...], sa_ref[...], sb_ref[...]
        c8, sa8, sb8 = jnp.tile(c1, (1, 8)), jnp.tile(sa1, (1, 8)), jnp.tile(sb1, (1, 8))
        cqn = (cq * lax.rsqrt(_rowmean(cq * cq) + EPS) * gq_ref[...]).astype(BF16)
        qe = _mm(cqn, wuq_ref[...])
        qc_ref[...] = _rope_fwd(qe, c8, sa8, sb8).astype(BF16)
        ckvn = (ckv * lax.rsqrt(_rowmean(ckv * ckv) + EPS) * gkv_ref[...]).astype(BF16)
        ke = _mm(ckvn, wk_ref[...])
        krr = _rope_fwd(kr, c1, sa1, sb1)
        kc_ref[...] = (ke + jnp.tile(krr, (1, 8))).astype(BF16)
        mv_ref[...] = _mm(ckvn, wv_ref[...]).astype(BF16)

    out_shape = (
        jax.ShapeDtypeStruct((s, 512), BF16), jax.ShapeDtypeStruct((s, 512), BF16), jax.ShapeDtypeStruct((s, 512), BF16),
        jax.ShapeDtypeStruct((s, 512), F32), jax.ShapeDtypeStruct((s, 512), F32),
        jax.ShapeDtypeStruct((s, Q_LORA), F32), jax.ShapeDtypeStruct((s, KV_LORA), F32),
        jax.ShapeDtypeStruct((s, 1024), BF16), jax.ShapeDtypeStruct((s, 1024), BF16), jax.ShapeDtypeStruct((s, 512), BF16),
    )
    return pl.pallas_call(
        body, name="pre_fwd", grid=(s // TM,), out_shape=out_shape,
        in_specs=[_rows(D_MODEL), _rows(LANES), _rows(LANES), _rows(LANES), _full((1, D_MODEL)), _full((D_MODEL, D_EXT)),
                  _full((1, Q_LORA)), _full((Q_LORA, 1024)), _full((1, KV_LORA)), _full((KV_LORA, 1024)), _full((KV_LORA, 512))],
        out_specs=(_rows(512), _rows(512), _rows(512), _rows(512), _rows(512), _rows(Q_LORA), _rows(KV_LORA),
                   _rows(1024), _rows(1024), _rows(512)),
        compiler_params=pltpu.CompilerParams(vmem_limit_bytes=VMEM_DENSE),
    )(x, c_t, sa_t, sb_t, gpre, win, gq, wuq, gkv, wk, wv)


def _softplus(z):
    neg_abs = lax.bitcast_convert_type(lax.bitcast_convert_type(z, jnp.uint32) | jnp.uint32(0x80000000), F32)
    return jnp.maximum(z, 0.0) + jnp.log(1.0 + jnp.exp(neg_abs))


def _pair_tri(kind):
    r = lax.broadcasted_iota(jnp.int32, (512, 256), 0)
    c = lax.broadcasted_iota(jnp.int32, (512, 256), 1)
    same = ((r // BK) % 2) == (c // BK)
    rk, ck = r % BK, c % BK
    m = {"ge": rk >= ck, "lt": rk < ck, "le": rk <= ck}[kind]
    return jnp.where(same & m, 1.0, 0.0).astype(BF16)


def _split2(a):
    hi = a.astype(BF16)
    lo = (a - hi.astype(F32)).astype(BF16)
    return jnp.concatenate([hi, lo], axis=1)


def _pair_stack(b, lane):
    zero = jnp.zeros_like(b)
    return jnp.concatenate([jnp.where(lane < 64, b, zero), jnp.where(lane >= 64, b, zero)], axis=0)


def _sb_fwd(q, k, v):
    s = q.shape[0]

    def body(q_ref, k_ref, v_ref, o_ref):
        i = pl.program_id(1)
        lane = lax.broadcasted_iota(jnp.int32, (1, LANES), 1)
        row = lax.broadcasted_iota(jnp.int32, (WQ, 2 * BK), 0) + i * WQ
        col = lax.broadcasted_iota(jnp.int32, (WQ, 2 * BK), 1) % BK
        u_ge = _pair_tri("ge")
        qs = q_ref[...] * (HEAD_DIM ** -0.5)

        def scores(j):
            ks = pl.multiple_of(j * BK, BK)
            return _mm_nt(qs, _pair_stack(k_ref[pl.ds(ks, BK), :], lane))

        def decay(z, j, masked):
            sp = _softplus(z)
            if masked:
                sp = jnp.where((col + j * BK) < row, sp, 0.0)
            return _mm(_split2(sp), u_ge)

        def weights(z, cum, j, masked):
            w = jnp.exp(z - cum)
            if masked:
                w = jnp.where((col + j * BK) < row, w, 0.0)
            ks = pl.multiple_of(j * BK, BK)
            return _mm(w.astype(BF16), _pair_stack(v_ref[pl.ds(ks, BK), :], lane))

        def fold(carry, pv, cum):
            acc, run = carry
            return acc + jnp.exp(-run) * pv, run + _heads_narrow(cum, 0, lane)

        def pair(ja, jb, carry, masked):
            za, zb = scores(ja), scores(jb)
            ca = decay(za, ja, masked)
            cb = decay(zb, jb, masked)
            pa = weights(za, ca, ja, masked)
            pb = weights(zb, cb, jb, masked)
            return fold(fold(carry, pa, ca), pb, cb)

        assert WQ == 2 * BK
        acc, run = pair(2 * i + 1, 2 * i, (jnp.zeros((WQ, LANES), F32), jnp.zeros((WQ, LANES), F32)), True)

        def more(c):
            return (c[0] < i) & (c[1] > 0)

        def step(c):
            jj, _, acc, run = c
            acc, run = pair(2 * i - 1 - 2 * jj, 2 * i - 2 - 2 * jj, (acc, run), False)
            return jj + 1, (jnp.min(run) < SB_CUTOFF).astype(jnp.int32), acc, run

        _, _, acc, _ = lax.while_loop(more, step, (jnp.int32(0), (jnp.min(run) < SB_CUTOFF).astype(jnp.int32), acc, run))
        o_ref[...] = acc

    qspec = pl.BlockSpec((WQ, LANES), lambda p, i: (i, p))
    kspec = pl.BlockSpec((s, LANES), lambda p, i: (0, p))
    return pl.pallas_call(
        body, name="sb_fwd", grid=(4, s // WQ),
        out_shape=jax.ShapeDtypeStruct((s, 512), F32),
        in_specs=[qspec, kspec, kspec], out_specs=qspec,
        compiler_params=pltpu.CompilerParams(vmem_limit_bytes=VMEM_ATTN),
    )(q, k, v)


def _pair_tri1(kind):
    r = lax.broadcasted_iota(jnp.int32, (2 * BK, 2 * BK), 0)
    c = lax.broadcasted_iota(jnp.int32, (2 * BK, 2 * BK), 1)
    rk, ck = r % BK, c % BK
    m = {"ge": rk >= ck, "lt": rk < ck, "le": rk <= ck}[kind]
    return jnp.where(((r // BK) == (c // BK)) & m, 1.0, 0.0).astype(BF16)


def _heads_wide(a):
    m = a.shape[0]
    return jnp.concatenate([jnp.broadcast_to(a[:, 0:1], (m, BK)), jnp.broadcast_to(a[:, 64:65], (m, BK))], axis=1)


def _heads_narrow(a, col, lane):
    return jnp.where(lane < 64, a[:, col:col + 1], a[:, BK + col:BK + col + 1])


def _pair_rowsum():
    r = lax.broadcasted_iota(jnp.int32, (512, LANES), 0)
    c = lax.broadcasted_iota(jnp.int32, (512, LANES), 1)
    return jnp.where(((r // BK) % 2) == (c // 64), 1.0, 0.0).astype(BF16)


def _sb_bwd(q, k, v, do):
    s = q.shape[0]
    assert WQ == 2 * BK

    def body(q_ref, k_ref, v_ref, do_ref, dq_ref, dk_ref, dv_ref, later_ref):
        i = pl.program_id(1)

        @pl.when(i == 0)
        def _():
            dk_ref[...] = jnp.zeros_like(dk_ref)
            dv_ref[...] = jnp.zeros_like(dv_ref)

        lane = lax.broadcasted_iota(jnp.int32, (1, LANES), 1)
        row = lax.broadcasted_iota(jnp.int32, (WQ, 2 * BK), 0) + i * WQ
        col = lax.broadcasted_iota(jnp.int32, (WQ, 2 * BK), 1) % BK
        u_ge = _pair_tri("ge")
        u_le = _pair_tri1("le")
        r_sum = _pair_rowsum()
        qs = q_ref[...] * (HEAD_DIM ** -0.5)
        dof = do_ref[...].astype(F32)

        def scan(ja, jb, run, masked):
            za = _mm_nt(qs, _pair_stack(k_ref[pl.ds(pl.multiple_of(ja * BK, BK), BK), :], lane))
            zb = _mm_nt(qs, _pair_stack(k_ref[pl.ds(pl.multiple_of(jb * BK, BK), BK), :], lane))
            spa, spb = _softplus(za), _softplus(zb)
            if masked:
                spa = jnp.where((col + ja * BK) < row, spa, 0.0)
                spb = jnp.where((col + jb * BK) < row, spb, 0.0)
            rsa, rsb = _mm(_split2(spa), r_sum), _mm(_split2(spb), r_sum)
            later_ref[ja] = run
            later_ref[jb] = run + rsa
            return run + rsa + rsb

        run = scan(2 * i + 1, 2 * i, jnp.zeros((WQ, LANES), F32), True)

        def more(c):
            return (c[0] < i) & (c[1] > 0)

        def step(c):
            run = scan(2 * i - 1 - 2 * c[0], 2 * i - 2 - 2 * c[0], c[2], False)
            return c[0] + 1, (jnp.min(run) < SB_CUTOFF).astype(jnp.int32), run

        npairs, _, _ = lax.while_loop(more, step, (jnp.int32(0), (jnp.min(run) < SB_CUTOFF).astype(jnp.int32), run))

        def keys(j):
            ks = pl.multiple_of(j * BK, BK)
            return ks, _pair_stack(k_ref[pl.ds(ks, BK), :], lane)

        def gates(z, j, masked):
            neg_abs = lax.bitcast_convert_type(lax.bitcast_convert_type(z, jnp.uint32) | jnp.uint32(0x80000000), F32)
            u = jnp.exp(neg_abs)
            opu = 1.0 + u
            sp = jnp.maximum(z, 0.0) + jnp.log(opu)
            if masked:
                sp = jnp.where((col + j * BK) < row, sp, 0.0)
            return jnp.where(z >= 0.0, 1.0, u) / opu, _mm(_split2(sp), u_ge)

        def weights(z, cum, ks, j, masked):
            wl = jnp.exp(z - cum)
            if masked:
                wl = jnp.where((col + j * BK) < row, wl, 0.0)
            dfo = (jnp.exp(-later_ref[j]) * dof).astype(BF16)
            e = _mm_nt(dfo, _pair_stack(v_ref[pl.ds(ks, BK), :], lane)) * wl
            return wl.astype(BF16), dfo, e

        def scores_grad(e, sig, erun, j, masked):
            ecum = _mm(e.astype(BF16), u_le) + _heads_wide(erun)
            dz = e - sig * ecum
            if masked:
                dz = jnp.where((col + j * BK) < row, dz, 0.0)
            return dz.astype(BF16), _heads_narrow(ecum, BK - 1, lane)

        def scatter(ks, dzb, wlb, dfo):
            rk = _mm_tn(dzb, qs)
            dk_ref[pl.ds(ks, BK), :] += jnp.where(lane < 64, rk[0:BK], rk[BK:2 * BK])
            rv = _mm_tn(wlb, dfo)
            dv_ref[pl.ds(ks, BK), :] += jnp.where(lane < 64, rv[0:BK], rv[BK:2 * BK])

        def pair(ja, carry, masked):
            dq, erun = carry
            jb = ja + 1
            ksa, kbda = keys(ja)
            ksb, kbdb = keys(jb)
            za, zb = _mm_nt(qs, kbda), _mm_nt(qs, kbdb)
            siga, cuma = gates(za, ja, masked)
            sigb, cumb = gates(zb, jb, masked)
            wla, dfoa, ea = weights(za, cuma, ksa, ja, masked)
            wlb, dfob, eb = weights(zb, cumb, ksb, jb, masked)
            dza, eruna = scores_grad(ea, siga, erun, ja, masked)
            dzb, erunb = scores_grad(eb, sigb, eruna, jb, masked)
            dq = dq + _mm(jnp.concatenate([dza, dzb], axis=1), jnp.concatenate([kbda, kbdb], axis=0))
            scatter(ksa, dza, wla, dfoa)
            scatter(ksb, dzb, wlb, dfob)
            return dq, erunb

        zero = jnp.zeros((WQ, LANES), F32)
        first = 2 * (i - npairs)
        carry = lax.fori_loop(0, npairs, lambda t, c: pair(first + 2 * t, c, False), (zero, zero))
        dq, _ = pair(2 * i, carry, True)
        dq_ref[...] = (dq * (HEAD_DIM ** -0.5)).astype(BF16)

    qspec = pl.BlockSpec((WQ, LANES), lambda p, i: (i, p))
    kspec = pl.BlockSpec((s, LANES), lambda p, i: (0, p))
    return pl.pallas_call(
        body, name="sb_bwd", grid=(4, s // WQ),
        out_shape=(jax.ShapeDtypeStruct((s, 512), BF16), jax.ShapeDtypeStruct((s, 512), F32),
                   jax.ShapeDtypeStruct((s, 512), F32)),
        in_specs=[qspec, kspec, kspec, qspec], out_specs=(qspec, kspec, kspec),
        scratch_shapes=[pltpu.VMEM((s // BK, WQ, LANES), F32)],
        compiler_params=pltpu.CompilerParams(vmem_limit_bytes=VMEM_ATTN),
    )(q, k, v, do)


MLA_SCALE = (QK_NOPE + QK_ROPE) ** -0.5
LOG2E = 1.4426950408889634


def _mla_keys(kb):
    zero = jnp.zeros((BK, LANES), kb.dtype)
    return jnp.concatenate([jnp.concatenate([kb[:, 0:LANES], zero], axis=1),
                            jnp.concatenate([zero, kb[:, LANES:2 * LANES]], axis=1)], axis=0)


def _mla_fwd(qc, kc, v):
    s = qc.shape[0]

    def body(q_ref, k_ref, v_ref, o_ref, l_ref, z_scr, m_scr):
        i = pl.program_id(1)
        lane = lax.broadcasted_iota(jnp.int32, (1, LANES), 1)
        rowc = (lax.broadcasted_iota(jnp.int32, (WQ, BK), 0) + i * WQ) // 64
        col = lax.broadcasted_iota(jnp.int32, (WQ, BK), 1)
        qw = q_ref[...]
        ind0 = jnp.broadcast_to(jnp.where(lane < 64, 1.0, 0.0).astype(BF16), (BK, LANES))
        ind1 = jnp.broadcast_to(jnp.where(lane >= 64, 1.0, 0.0).astype(BF16), (BK, LANES))

        def raw(j):
            ks = pl.multiple_of(j * BK, BK)
            return _mm_nt(qw, _mla_keys(k_ref[pl.ds(ks, BK), :]))

        def scores(z, j, masked):
            z = z * (MLA_SCALE * LOG2E)
            z0, z1 = z[:, 0:BK], z[:, BK:2 * BK]
            if masked:
                valid = ((col + j * BK) // 64) <= rowc
                z0, z1 = jnp.where(valid, z0, -1e30), jnp.where(valid, z1, -1e30)
            return z0, z1

        def values(j):
            ks = pl.multiple_of(j * BK, BK)
            vb = v_ref[pl.ds(ks, BK), :]
            zero = jnp.zeros_like(vb)
            return jnp.concatenate([jnp.concatenate([jnp.where(lane < 64, vb, zero), ind0], axis=1),
                                    jnp.concatenate([jnp.where(lane >= 64, vb, zero), ind1], axis=1)], axis=0)

        def pair(ja, za, zb, masked):
            m0, m1 = m_scr[0], m_scr[1]
            a0, a1 = scores(za, ja, masked)
            b0, b1 = scores(zb, ja + 1, masked)
            n0 = jnp.maximum(m0, jnp.max(jnp.maximum(a0, b0), axis=1, keepdims=True))
            n1 = jnp.maximum(m1, jnp.max(jnp.maximum(a1, b1), axis=1, keepdims=True))
            p = jnp.concatenate([jnp.exp2(a0 - n0), jnp.exp2(a1 - n1), jnp.exp2(b0 - n0), jnp.exp2(b1 - n1)], axis=1)
            pv = _mm(p.astype(BF16), jnp.concatenate([values(ja), values(ja + 1)], axis=0))
            a = jnp.where(lane < 64, jnp.exp2(m0 - n0), jnp.exp2(m1 - n1))
            m_scr[0], m_scr[1] = n0, n1
            l_ref[...] = a * l_ref[...] + pv[:, LANES:2 * LANES]
            o_ref[...] = a * o_ref[...] + pv[:, 0:LANES]

        def step(jj, carry):
            slot = jj % 2
            za, zb = z_scr[slot, 0], z_scr[slot, 1]
            z_scr[1 - slot, 0] = raw(2 * jj + 2)
            z_scr[1 - slot, 1] = raw(2 * jj + 3)
            pair(2 * jj, za, zb, False)
            return carry

        m_scr[...] = jnp.full(m_scr.shape, -1e30, F32)
        l_ref[...] = jnp.zeros_like(l_ref)
        o_ref[...] = jnp.zeros_like(o_ref)
        z_scr[0, 0] = raw(0)
        z_scr[0, 1] = raw(1)
        lax.fori_loop(0, i, step, 0)
        pair(2 * i, z_scr[i % 2, 0], z_scr[i % 2, 1], True)
        l = l_ref[...]
        o_ref[...] = o_ref[...] / l
        l_ref[...] = jnp.where(lane < 64, m_scr[0], m_scr[1]) + jnp.log2(l)

    qspec = pl.BlockSpec((WQ, 2 * LANES), lambda p, i: (i, p))
    kspec = pl.BlockSpec((s, 2 * LANES), lambda p, i: (0, p))
    vspec = pl.BlockSpec((s, LANES), lambda p, i: (0, p))
    ospec = pl.BlockSpec((WQ, LANES), lambda p, i: (i, p))
    return pl.pallas_call(
        body, name="mla_fwd", grid=(4, s // WQ),
        out_shape=(jax.ShapeDtypeStruct((s, 512), F32), jax.ShapeDtypeStruct((s, 512), F32)),
        in_specs=[qspec, kspec, vspec], out_specs=(ospec, ospec),
        scratch_shapes=[pltpu.VMEM((2, 2, WQ, 2 * BK), F32), pltpu.VMEM((2, WQ, 1), F32)],
        compiler_params=pltpu.CompilerParams(vmem_limit_bytes=VMEM_ATTN),
    )(qc, kc, v)


def _mla_bwd(qc, kc, kct, v, do, lse, delta):
    s = qc.shape[0]

    def body(q_ref, k_ref, kt_ref, v_ref, do_ref, l_ref, d_ref, dq_ref, dk_ref, dv_ref, dqt_scr, p_scr, dz_scr):
        i = pl.program_id(1)

        @pl.when(i == 0)
        def _():
            dk_ref[...] = jnp.zeros_like(dk_ref)
            dv_ref[...] = jnp.zeros_like(dv_ref)

        lane = lax.broadcasted_iota(jnp.int32, (1, LANES), 1)
        keyc = lax.broadcasted_iota(jnp.int32, (BK, WQ), 0)
        qryc = (lax.broadcasted_iota(jnp.int32, (BK, WQ), 1) + i * WQ) // 64
        qw = q_ref[...]
        dob = do_ref[...]
        dost = (dob.astype(F32) * MLA_SCALE).T.astype(BF16)
        lt = l_ref[...].T
        dt = (d_ref[...] * MLA_SCALE).T
        lse0, lse1 = lt[0:1], lt[64:65]
        dl0, dl1 = dt[0:1], dt[64:65]
        dqt_scr[...] = jnp.zeros_like(dqt_scr)

        def products(j):
            ks = pl.multiple_of(j * BK, BK)
            return (_mm_nt(_mla_keys(k_ref[pl.ds(ks, BK), :]), qw), _mm(_pair_stack(v_ref[pl.ds(ks, BK), :], lane), dost))

        def grads(j, slot, zt, dwt, masked):
            zt = zt * (MLA_SCALE * LOG2E)
            p0 = jnp.exp2(zt[0:BK] - lse0)
            p1 = jnp.exp2(zt[BK:2 * BK] - lse1)
            if masked:
                valid = ((keyc + j * BK) // 64) <= qryc
                p0, p1 = jnp.where(valid, p0, 0.0), jnp.where(valid, p1, 0.0)
            p_scr[slot] = jnp.concatenate([p0, p1], axis=0).astype(BF16)
            dz_scr[slot] = jnp.concatenate([p0 * (dwt[0:BK] - dl0), p1 * (dwt[BK:2 * BK] - dl1)], axis=0).astype(BF16)

        def keys_t(ks):
            ktb = kt_ref[:, pl.ds(ks, BK)]
            zero = jnp.zeros((LANES, BK), ktb.dtype)
            return jnp.concatenate([jnp.concatenate([ktb[0:LANES], zero], axis=1),
                                    jnp.concatenate([zero, ktb[LANES:2 * LANES]], axis=1)], axis=0)

        def scatter(ja):
            ksa, ksb = pl.multiple_of(ja * BK, BK), pl.multiple_of((ja + 1) * BK, BK)
            dqt_scr[...] += _mm(jnp.concatenate([keys_t(ksa), keys_t(ksb)], axis=1),
                                jnp.concatenate([dz_scr[0], dz_scr[1]], axis=0))
            for slot, ks in ((0, ksa), (1, ksb)):
                rk = _mm(dz_scr[slot], qw)
                dk_ref[pl.ds(ks, BK), :] += jnp.concatenate([rk[0:BK, 0:LANES], rk[BK:2 * BK, LANES:2 * LANES]], axis=1)
                rv = _mm(p_scr[slot], dob)
                dv_ref[pl.ds(ks, BK), :] += jnp.where(lane < 64, rv[0:BK], rv[BK:2 * BK])

        def step(n, masked):
            za, wa = products(2 * n)
            zb, wb = products(2 * n + 1)
            scatter(2 * n - 2)
            grads(2 * n, 0, za, wa, masked)
            grads(2 * n + 1, 1, zb, wb, masked)

        def first(masked):
            za, wa = products(0)
            zb, wb = products(1)
            grads(0, 0, za, wa, masked)
            grads(1, 1, zb, wb, masked)

        @pl.when(i == 0)
        def _():
            first(True)

        @pl.when(i > 0)
        def _():
            first(False)
            lax.fori_loop(1, i, lambda n, c: (step(n, False), c)[1], 0)
            step(i, True)

        scatter(2 * i)
        dq_ref[...] = dqt_scr[...].T

    qspec = pl.BlockSpec((WQ, 2 * LANES), lambda p, i: (i, p))
    kspec = pl.BlockSpec((s, 2 * LANES), lambda p, i: (0, p))
    ktspec = pl.BlockSpec((2 * LANES, s), lambda p, i: (p, 0))
    vspec = pl.BlockSpec((s, LANES), lambda p, i: (0, p))
    ospec = pl.BlockSpec((WQ, LANES), lambda p, i: (i, p))
    return pl.pallas_call(
        body, name="mla_bwd", grid=(4, s // WQ),
        out_shape=(jax.ShapeDtypeStruct((s, 1024), F32), jax.ShapeDtypeStruct((s, 1024), F32),
                   jax.ShapeDtypeStruct((s, 512), F32)),
        in_specs=[qspec, kspec, ktspec, vspec, ospec, ospec, ospec], out_specs=(qspec, kspec, vspec),
        scratch_shapes=[pltpu.VMEM((2 * LANES, WQ), F32), pltpu.VMEM((2, 2 * BK, WQ), BF16), pltpu.VMEM((2, 2 * BK, WQ), BF16)],
        compiler_params=pltpu.CompilerParams(vmem_limit_bytes=VMEM_ATTN),
    )(qc, kc, kct, v, do, lse, delta)


def _post(x, p, tgt, sbo, mlao, sbg, mlag, gsb, gmla, wout, gpost, wple, gple, wpg, bpg):
    s = x.shape[0]

    def body(x_ref, p_ref, t_ref, sbo_ref, mlao_ref, sbg_ref, mlag_ref, gsb_ref, gmla_ref, wout_ref,
             gpost_ref, wple_ref, gple_ref, wpg_ref, bpg_ref,
             dsbo_ref, dmlao_ref, delta_ref, dsbg_ref, dmlag_ref, dxres_ref, dwout_ref, dwpg_ref, dwple_ref, vec_ref):
        i = pl.program_id(0)

        @pl.when(i == 0)
        def _():
            dwout_ref[...] = jnp.zeros_like(dwout_ref)
            dwpg_ref[...] = jnp.zeros_like(dwpg_ref)
            dwple_ref[...] = jnp.zeros_like(dwple_ref)
            vec_ref[...] = jnp.zeros_like(vec_ref)

        bd = _blockdiag(512, HEAD_DIM)
        inv_hd = 1.0 / HEAD_DIM

        def head_fwd(o, g, gate):
            r = lax.rsqrt(_seg(o * o, bd) * inv_hd + EPS)
            hat = o * r
            n = hat * g
            sg = _sigmoid(gate)
            return hat, r, n, sg, n * (gate * sg)

        sbo, mlao, sbg_v, mlag_v = sbo_ref[...], mlao_ref[...], sbg_ref[...], mlag_ref[...]
        gsb_v, gmla_v = gsb_ref[...], gmla_ref[...]
        sb_hat, sb_r, sb_n, sb_sg, sb_y = head_fwd(sbo, gsb_v, sbg_v)
        ml_hat, ml_r, ml_n, ml_sg, ml_y = head_fwd(mlao, gmla_v, mlag_v)
        mix = jnp.concatenate([sb_y, ml_y], axis=1).astype(BF16)
        y = _mm(mix, wout_ref[...])
        ry = lax.rsqrt(_rowmean(y * y) + EPS)
        y_hat = y * ry
        gpost_v = gpost_ref[...]
        x1 = x_ref[...] + y_hat * gpost_v
        pb = p_ref[...].astype(BF16)
        pl_ = _mm(pb, wple_ref[...])
        rp = lax.rsqrt(_rowmean(pl_ * pl_) + EPS)
        pl_hat = pl_ * rp
        gple_v = gple_ref[...]
        ple = pl_hat * gple_v
        x1b = x1.astype(BF16)
        gate = _sigmoid(_mm(x1b, wpg_ref[...]) + bpg_ref[...])
        err = x1 + ple * gate - t_ref[...]
        loss = 0.5 * jnp.sum(_rowmean(err * err))
        dout = err * (1.0 / D_MODEL)

        du = dout * ple * gate * (1.0 - gate)
        dub = du.astype(BF16)
        dple = dout * gate
        dx1 = dout + _mm_nt(dub, wpg_ref[...])
        dwpg_ref[...] += _mm_tn(x1b, dub)
        dplh = dple * gple_v
        dpl = rp * (dplh - pl_hat * _rowmean(dplh * pl_hat))
        dwple_ref[...] += _mm_tn(pb, dpl.astype(BF16))
        dxres_ref[...] = dx1
        dyh = dx1 * gpost_v
        dy = ry * (dyh - y_hat * _rowmean(dyh * y_hat))
        dyb = dy.astype(BF16)
        dwout_ref[...] += _mm_tn(mix, dyb)
        dmix = _mm_nt(dyb, wout_ref[...])

        def head_bwd(dyv, hat, r, n, sg, g, gate):
            dn = dyv * (gate * sg)
            dgate = dyv * n * (sg * (1.0 + gate * (1.0 - sg)))
            dhat = dn * g
            do = r * (dhat - hat * (_seg(dhat * hat, bd) * inv_hd))
            return do, dgate, _colsum(dn * hat)

        dsbo, dsbg, dg_sb = head_bwd(dmix[:, 0:512], sb_hat, sb_r, sb_n, sb_sg, gsb_v, sbg_v)
        dmlao, dmlag, dg_ml = head_bwd(dmix[:, 512:1024], ml_hat, ml_r, ml_n, ml_sg, gmla_v, mlag_v)
        dsbo_ref[...] = dsbo.astype(BF16)
        dmlao_ref[...] = dmlao.astype(BF16)
        delta_ref[...] = _seg(dmlao * mlao, bd)
        dsbg_ref[...] = dsbg.astype(BF16)
        dmlag_ref[...] = dmlag.astype(BF16)
        vec_ref[pl.ds(0, 1), :] += _colsum(dx1 * y_hat)
        vec_ref[pl.ds(1, 1), :] += _colsum(dple * pl_hat)
        vec_ref[pl.ds(2, 1), :] += _colsum(du)
        vec_ref[pl.ds(3, 1), :] += jnp.concatenate([dg_sb, dg_ml], axis=1)
        vec_ref[pl.ds(4, 1), :] += jnp.full((1, D_MODEL), loss, F32)

    out_shape = (
        jax.ShapeDtypeStruct((s, 512), BF16), jax.ShapeDtypeStruct((s, 512), BF16), jax.ShapeDtypeStruct((s, 512), F32),
        jax.ShapeDtypeStruct((s, 512), BF16), jax.ShapeDtypeStruct((s, 512), BF16), jax.ShapeDtypeStruct((s, D_MODEL), F32),
        jax.ShapeDtypeStruct((D_MODEL, D_MODEL), F32), jax.ShapeDtypeStruct((D_MODEL, D_MODEL), F32),
        jax.ShapeDtypeStruct((PLE_DIM, D_MODEL), F32), jax.ShapeDtypeStruct((8, D_MODEL), F32),
    )
    return pl.pallas_call(
        body, name="post_fwd_bwd", grid=(s // TM,), out_shape=out_shape,
        in_specs=[_rows(D_MODEL), _rows(PLE_DIM), _rows(D_MODEL), _rows(512), _rows(512), _rows(512), _rows(512),
                  _full((1, 512)), _full((1, 512)), _full((D_MODEL, D_MODEL)),
                  _full((1, D_MODEL)), _full((PLE_DIM, D_MODEL)), _full((1, D_MODEL)), _full((D_MODEL, D_MODEL)),
                  _full((1, D_MODEL))],
        out_specs=(_rows(512), _rows(512), _rows(512), _rows(512), _rows(512), _rows(D_MODEL),
                   _full((D_MODEL, D_MODEL)), _full((D_MODEL, D_MODEL)), _full((PLE_DIM, D_MODEL)), _full((8, D_MODEL))),
        compiler_params=pltpu.CompilerParams(vmem_limit_bytes=VMEM_DENSE),
    )(x, p, tgt, sbo, mlao, sbg, mlag, gsb, gmla, wout, gpost, wple, gple, wpg, bpg)


def _pre_bwd(x, dxres, dsbq, dsbk, dsbv, dsbg, dmlag, dqc, dkc, dmv, cq, ckv, tabs, gpre, win, gq, wuq, gkv, wk, wv):
    s = x.shape[0]
    c_t, sa_t, sb_t = tabs

    def body(x_ref, dxres_ref, dsbq_ref, dsbk_ref, dsbv_ref, dsbg_ref, dmlag_ref, dqc_ref, dkc_ref, dmv_ref, cq_ref,
             ckv_ref, c_ref, sa_ref, sb_ref, gpre_ref, win_ref, gq_ref, wuq_ref, gkv_ref, wk_ref, wv_ref,
             gx_ref, dwin_ref, dwuq_ref, dwk_ref, dwv_ref, vec_ref, dwin_acc):
        i = pl.program_id(0)

        @pl.when(i == 0)
        def _():
            dwin_acc[...] = jnp.zeros_like(dwin_acc)
            dwuq_ref[...] = jnp.zeros_like(dwuq_ref)
            dwk_ref[...] = jnp.zeros_like(dwk_ref)
            dwv_ref[...] = jnp.zeros_like(dwv_ref)
            vec_ref[...] = jnp.zeros_like(vec_ref)

        lane = lax.broadcasted_iota(jnp.int32, (1, LANES), 1)
        c1, sa1, sb1 = c_ref[...], sa_ref[...], sb_ref[...]
        c8, sa8, sb8 = jnp.tile(c1, (1, 8)), jnp.tile(sa1, (1, 8)), jnp.tile(sb1, (1, 8))

        def norm_bwd(dn, hat, r, g):
            t = dn * g
            return r * (t - hat * _rowmean(t * hat)), _colsum(dn * hat)

        dqeb = _rope_bwd(dqc_ref[...], c8, sa8, sb8).astype(BF16)
        cq = cq_ref[...]
        rq = lax.rsqrt(_rowmean(cq * cq) + EPS)
        cq_hat = cq * rq
        gq_v = gq_ref[...]
        dwuq_ref[...] += _mm_tn((cq_hat * gq_v).astype(BF16), dqeb)
        dcq, dg_q = norm_bwd(_mm_nt(dqeb, wuq_ref[...]), cq_hat, rq, gq_v)

        dkc = dkc_ref[...]
        dkcb = dkc.astype(BF16)
        dmvb = dmv_ref[...].astype(BF16)
        ckv = ckv_ref[...]
        rkv = lax.rsqrt(_rowmean(ckv * ckv) + EPS)
        ckv_hat = ckv * rkv
        gkv_v = gkv_ref[...]
        ckvnb = (ckv_hat * gkv_v).astype(BF16)
        dwk_ref[...] += _mm_tn(ckvnb, dkcb)
        dwv_ref[...] += _mm_tn(ckvnb, dmvb)
        dckv, dg_kv = norm_bwd(_mm_nt(dkcb, wk_ref[...]) + _mm_nt(dmvb, wv_ref[...]), ckv_hat, rkv, gkv_v)

        dkr = dkc[:, 0:LANES]
        for hh in range(1, 8):
            dkr = dkr + dkc[:, LANES * hh:LANES * (hh + 1)]
        dkr = _rope_bwd(dkr, c1, sa1, sb1)
        dkr = jnp.where((lane >= 64) & (lane < 96), dkr, 0.0)

        dproj = jnp.concatenate([dsbq_ref[...], dsbk_ref[...].astype(BF16), dsbv_ref[...].astype(BF16), dsbg_ref[...],
                                 dcq.astype(BF16), dckv.astype(BF16), dkr.astype(BF16), dmlag_ref[...]], axis=1)
        xv = x_ref[...]
        r1 = lax.rsqrt(_rowmean(xv * xv) + EPS)
        x_hat = xv * r1
        gpre_v = gpre_ref[...]
        dwin_acc[...] += _mm_tn((x_hat * gpre_v).astype(BF16), dproj)
        dx, dg_pre = norm_bwd(_mm_nt(dproj, win_ref[...]), x_hat, r1, gpre_v)
        gx_ref[...] = dxres_ref[...] + dx
        vec_ref[pl.ds(0, 1), :] += dg_pre
        vec_ref[pl.ds(1, 1), :] += jnp.concatenate([dg_q, dg_kv, jnp.zeros((1, D_MODEL - Q_LORA - KV_LORA), F32)], axis=1)

        @pl.when(i == pl.num_programs(0) - 1)
        def _():
            pltpu.sync_copy(dwin_acc, dwin_ref)

    out_shape = (
        jax.ShapeDtypeStruct((s, D_MODEL), F32), jax.ShapeDtypeStruct((D_MODEL, D_EXT), F32),
        jax.ShapeDtypeStruct((Q_LORA, 1024), F32), jax.ShapeDtypeStruct((KV_LORA, 1024), F32),
        jax.ShapeDtypeStruct((KV_LORA, 512), F32), jax.ShapeDtypeStruct((8, D_MODEL), F32),
    )
    return pl.pallas_call(
        body, name="pre_bwd", grid=(s // TM,), out_shape=out_shape,
        in_specs=[_rows(D_MODEL), _rows(D_MODEL), _rows(512), _rows(512), _rows(512), _rows(512), _rows(512),
                  _rows(1024), _rows(1024), _rows(512), _rows(Q_LORA), _rows(KV_LORA), _rows(LANES), _rows(LANES),
                  _rows(LANES), _full((1, D_MODEL)), _full((D_MODEL, D_EXT)), _full((1, Q_LORA)), _full((Q_LORA, 1024)),
                  _full((1, KV_LORA)), _full((KV_LORA, 1024)), _full((KV_LORA, 512))],
        out_specs=(_rows(D_MODEL), pl.BlockSpec(memory_space=pl.ANY), _full((Q_LORA, 1024)), _full((KV_LORA, 1024)),
                   _full((KV_LORA, 512)), _full((8, D_MODEL))),
        scratch_shapes=[pltpu.VMEM((D_MODEL, D_EXT), F32)],
        compiler_params=pltpu.CompilerParams(vmem_limit_bytes=VMEM_DENSE),
    )(x, dxres, dsbq, dsbk, dsbv, dsbg, dmlag, dqc, dkc, dmv, cq, ckv, c_t, sa_t, sb_t, gpre, win, gq, wuq, gkv, wk, wv)


def _place():
    return lax.axis_index("x"), lax.axis_index("y"), lax.axis_index("c")


def _allgather_weights(shards):
    n = len(shards)

    def body(*refs):
        ins, outs, send_sems, recv_sems = refs[:n], refs[n:2 * n], refs[2 * n], refs[2 * n + 1]
        x, y, c = _place()
        me, sib = (x, y, c), (x, y, 1 - c)
        chips = [(1 - x, y), (x, 1 - y), (1 - x, 1 - y)]

        def half(t, chip, hc):
            rows = shards[t].shape[0] // 2
            return outs[t].at[2 * chip[0] + chip[1], pl.ds(pl.multiple_of(hc * rows, 16), rows), :]

        def copy(k, t, chip, hc, to):
            return pltpu.make_async_remote_copy(src_ref=half(t, chip, hc), dst_ref=half(t, chip, hc), send_sem=send_sems.at[k],
                                                recv_sem=recv_sems.at[k], device_id=to, device_id_type=MESH)

        first, passed = [], []
        for t in range(n):
            outs[t][2 * x + y] = ins[t][...].astype(BF16)
            for j, chip in enumerate(chips):
                cp = copy(6 * t + j, t, (x, y), c, (*chip, c))
                cp.start()
                first.append(cp)
        for t in range(n):
            for j, chip in enumerate(chips):
                copy(6 * t + j, t, chip, c, me).wait_recv()
                cp = copy(6 * t + 3 + j, t, chip, c, sib)
                cp.start()
                passed.append(cp)
        for t in range(n):
            for j, chip in enumerate(chips):
                copy(6 * t + 3 + j, t, chip, 1 - c, me).wait_recv()
        for cp in first + passed:
            cp.wait_send()

    return pl.pallas_call(
        body, name="allgather_weights",
        out_shape=tuple(jax.ShapeDtypeStruct((N_SHARD,) + a.shape, BF16) for a in shards),
        in_specs=[pl.BlockSpec(memory_space=pltpu.VMEM)] * n, out_specs=(pl.BlockSpec(memory_space=pltpu.VMEM),) * n,
        scratch_shapes=[pltpu.SemaphoreType.DMA((6 * n,)), pltpu.SemaphoreType.DMA((6 * n,))],
        compiler_params=pltpu.CompilerParams(vmem_limit_bytes=VMEM_ATTN),
    )(*shards)


def _reduce_scatter_grads(gsh, vec):
    n = len(gsh)
    halves = [a.shape[1] // 2 for a in gsh]

    def body(*refs):
        g_refs, vec_ref, f_refs, vsum_ref = refs[:n], refs[n], refs[n + 1:2 * n + 1], refs[2 * n + 1]
        scr = refs[2 * n + 2:]
        accs, sibs, sbufs, rbufs = scr[0:n], scr[n:2 * n], scr[2 * n:3 * n], scr[3 * n:4 * n]
        vrecv, local_sems, send_sems, recv_sems = scr[4 * n:4 * n + 4]
        x, y, c = _place()
        me, sib = (x, y, c), (x, y, 1 - c)
        mine = 2 * x + y
        chips = [(1 - x, y), (x, 1 - y), (1 - x, 1 - y)]

        def remote(k, src, dst, to):
            return pltpu.make_async_remote_copy(src_ref=src, dst_ref=dst, send_sem=send_sems.at[k], recv_sem=recv_sems.at[k],
                                                device_id=to, device_id_type=MESH)

        def half3(ref, t, hc):
            return ref.at[:, pl.ds(pl.multiple_of(hc * halves[t], 8), halves[t]), :]

        def half2(ref, t, hc):
            return ref.at[pl.ds(pl.multiple_of(hc * halves[t], 8), halves[t]), :]

        loads, sends = [], []
        for t in range(n):
            ld = pltpu.make_async_copy(half3(g_refs[t], t, c), accs[t], local_sems.at[t])
            ld.start()
            loads.append(ld)
            cp = remote(t, half3(g_refs[t], t, 1 - c), sibs[t], sib)
            cp.start()
            sends.append(cp)

        my_dev = 4 * x + 2 * y + c
        vrecv[my_dev] = vec_ref[...]
        for k in range(1, 8):
            to = (x ^ ((k >> 2) & 1), y ^ ((k >> 1) & 1), c ^ (k & 1))
            cp = remote(n + k - 1, vec_ref, vrecv.at[my_dev], to)
            cp.start()
            sends.append(cp)

        for t in range(n):
            loads[t].wait()
            remote(t, half3(g_refs[t], t, 1 - c), sibs[t], me).wait_recv()
            for k in range(N_SHARD):
                accs[t][k] = accs[t][k] + sibs[t][k]
            for j, chip in enumerate(chips):
                idx = 2 * chip[0] + chip[1]
                sbufs[t][idx] = accs[t][idx].astype(BF16)
                cp = remote(n + 7 + 3 * t + j, sbufs[t].at[idx], rbufs[t].at[mine], (*chip, c))
                cp.start()
                sends.append(cp)

        for t in range(n):
            total = accs[t][mine]
            for j, chip in enumerate(chips):
                idx = 2 * chip[0] + chip[1]
                remote(n + 7 + 3 * t + j, sbufs[t].at[idx], rbufs[t].at[idx], me).wait_recv()
                total = total + rbufs[t][idx].astype(F32)
            half2(f_refs[t], t, c)[...] = total
            cp = remote(4 * n + 7 + t, half2(f_refs[t], t, c), half2(f_refs[t], t, c), sib)
            cp.start()
            sends.append(cp)
        for t in range(n):
            remote(4 * n + 7 + t, half2(f_refs[t], t, 1 - c), half2(f_refs[t], t, 1 - c), me).wait_recv()

        for k in range(1, 8):
            src_dev = 4 * (x ^ ((k >> 2) & 1)) + 2 * (y ^ ((k >> 1) & 1)) + (c ^ (k & 1))
            remote(n + k - 1, vec_ref, vrecv.at[src_dev], me).wait_recv()
        vs = vrecv[0]
        for d in range(1, 8):
            vs = vs + vrecv[d]
        vsum_ref[...] = vs

        for cp in sends:
            cp.wait_send()

    nsem = 5 * n + 7
    half_shapes = [(N_SHARD, h, a.shape[2]) for h, a in zip(halves, gsh)]
    return pl.pallas_call(
        body, name="reduce_scatter_grads",
        out_shape=tuple(jax.ShapeDtypeStruct(a.shape[1:], F32) for a in gsh) + (jax.ShapeDtypeStruct((VEC_ROWS, 1024), F32),),
        in_specs=[pl.BlockSpec(memory_space=pl.ANY)] * n + [pl.BlockSpec(memory_space=pltpu.VMEM)],
        out_specs=(pl.BlockSpec(memory_space=pltpu.VMEM),) * (n + 1),
        scratch_shapes=([pltpu.VMEM(s_, F32) for s_ in half_shapes] * 2 + [pltpu.VMEM(s_, BF16) for s_ in half_shapes] * 2
                        + [pltpu.VMEM((8, VEC_ROWS, 1024), F32), pltpu.SemaphoreType.DMA((n,)),
                           pltpu.SemaphoreType.DMA((nsem,)), pltpu.SemaphoreType.DMA((nsem,))]),
        compiler_params=pltpu.CompilerParams(vmem_limit_bytes=56 * 1024 * 1024),
    )(*gsh, vec)


def _adamw(w, g, m, v):
    rows, cols = w.shape
    tr = rows if rows <= 256 else 256

    def body(w_ref, g_ref, m_ref, v_ref, d_ref, nm_ref, nv_ref):
        d_ref[...], nm_ref[...], nv_ref[...] = _adam_math(w_ref[...], g_ref[...], m_ref[...], v_ref[...])

    spec = pl.BlockSpec((tr, cols), lambda i: (i, 0))
    shp = jax.ShapeDtypeStruct((rows, cols), F32)
    return pl.pallas_call(body, name="adamw", grid=(rows // tr,), out_shape=(shp, shp, shp),
                          in_specs=[spec] * 4, out_specs=(spec,) * 3)(w, g, m, v)


def _adam_math(w, g, m, v):
    m2 = ADAM_B1 * m + (1.0 - ADAM_B1) * g
    v2 = ADAM_B2 * v + (1.0 - ADAM_B2) * (g * g)
    m_hat = m2 / (1.0 - ADAM_B1 ** ADAM_STEP)
    v_hat = v2 / (1.0 - ADAM_B2 ** ADAM_STEP)
    return -ADAM_LR * (m_hat / (jnp.sqrt(v_hat) + ADAM_EPS) + ADAM_WD * w), m2, v2


def _adamw_small(vsum, w, m, v):
    names = [name for name, _, _, _ in _VEC_LAYOUT]
    k = len(names)

    def body(*refs):
        vs_ref, w_refs, m_refs, v_refs = refs[0], refs[1:1 + k], refs[1 + k:1 + 2 * k], refs[1 + 2 * k:1 + 3 * k]
        outs = refs[1 + 3 * k:]
        for idx, (_, r, c0, width) in enumerate(_VEC_LAYOUT):
            gv = vs_ref[pl.ds(r, 1), pl.ds(c0, width)]
            d, m2, v2 = _adam_math(w_refs[idx][...], gv, m_refs[idx][...], v_refs[idx][...])
            outs[idx][...], outs[k + idx][...], outs[2 * k + idx][...], outs[3 * k + idx][...] = gv, d, m2, v2

    shapes = tuple(jax.ShapeDtypeStruct(w[name].shape, F32) for name in names)
    res = pl.pallas_call(
        body, name="adamw_small", out_shape=shapes * 4,
        in_specs=[pl.BlockSpec(memory_space=pltpu.VMEM)] * (1 + 3 * k), out_specs=(pl.BlockSpec(memory_space=pltpu.VMEM),) * (4 * k),
    )(vsum, *[w[name] for name in names], *[m[name] for name in names], *[v[name] for name in names])
    return tuple({name: res[part * k + idx] for idx, name in enumerate(names)} for part in range(4))


_BIG = ("w_in", "w_uq", "w_ukv", "w_out", "w_ple", "w_ple_gate")
_COL_SHARDED = ("w_in", "w_uq", "w_ukv", "w_ple")


def _join_shards(parts):
    cols = lambda a: a.transpose(1, 0, 2).reshape(a.shape[1], N_SHARD * a.shape[2])
    rows = lambda a: a.reshape(N_SHARD * a.shape[1], a.shape[2])
    return {n: (cols if n in _COL_SHARDED else rows)(parts[n]) for n in _BIG}


def _split_shards(full):
    cols = lambda a: a.reshape(a.shape[0], N_SHARD, a.shape[1] // N_SHARD).transpose(1, 0, 2)
    rows = lambda a: a.reshape(N_SHARD, a.shape[0] // N_SHARD, a.shape[1])
    return {n: (cols if n in _COL_SHARDED else rows)(full[n]) for n in _BIG}


def _extend_weights(w):
    win = w["w_in"]
    zeros = lambda r, c: jnp.zeros((r, c), win.dtype)
    win_ext = jnp.concatenate([win[:, :2432], zeros(D_MODEL, 64), win[:, 2432:2464], zeros(D_MODEL, 32), win[:, 2464:]], axis=1)
    wuq_ext = jnp.pad(w["w_uq"].reshape(Q_LORA, 8, 96), ((0, 0), (0, 0), (0, 32))).reshape(Q_LORA, 1024)
    wukv = w["w_ukv"].reshape(KV_LORA, 8, 128)
    wk_ext = jnp.pad(wukv[:, :, :64], ((0, 0), (0, 0), (0, 64))).reshape(KV_LORA, 1024)
    wv = wukv[:, :, 64:].reshape(KV_LORA, 512)
    return win_ext, wuq_ext, wk_ext, wv


def _contract_grads(dwin_ext, dwuq_ext, dwk_ext, dwv):
    dwin = jnp.concatenate([dwin_ext[:, :2432], dwin_ext[:, 2496:2528], dwin_ext[:, 2560:]], axis=1)
    dwuq = dwuq_ext.reshape(Q_LORA, 8, 128)[:, :, :96].reshape(Q_LORA, 768)
    dwukv = jnp.concatenate([dwk_ext.reshape(KV_LORA, 8, 128)[:, :, :64], dwv.reshape(KV_LORA, 8, 64)], axis=2)
    return dwin, dwuq, dwukv.reshape(KV_LORA, 1024)


def _rope_tables(positions):
    half = QK_ROPE // 2
    freq = ROPE_THETA ** (-jnp.arange(half, dtype=F32) / half)
    ang = positions.astype(F32)[:, None] * freq
    cos, sin = jnp.cos(ang), jnp.sin(ang)
    s = positions.shape[0]
    z = lambda n: jnp.zeros((s, n), F32)
    c_t = jnp.concatenate([jnp.ones((s, 64), F32), cos, cos, z(32)], axis=1)
    sa_t = jnp.concatenate([z(64), -sin, z(16), z(32)], axis=1)
    sb_t = jnp.concatenate([z(64), z(16), sin, z(32)], axis=1)
    return c_t, sa_t, sb_t


def _local_grads(x, p, positions, tgt, gains, wfull):
    win_ext, wuq_ext, wk_ext, wv = _extend_weights(wfull)
    wout, wple, wpg = wfull["w_out"], wfull["w_ple"], wfull["w_ple_gate"]
    tabs = _rope_tables(positions)
    g = gains
    sbq, sbk, sbv, sbg, mlag, cq, ckv, qc, kc, mv = _pre_fwd(x, tabs, g["norm_pre_g"], win_ext, g["q_norm_g"], wuq_ext,
                                                             g["kv_norm_g"], wk_ext, wv)
    sbo = _sb_fwd(sbq, sbk, sbv)
    mlao, lse = _mla_fwd(qc, kc, mv)
    dsbo, dmlao, delta, dsbg, dmlag, dxres, dwout, dwpg, dwple, vec_c = _post(
        x, p, tgt, sbo, mlao, sbg, mlag, g["sb_out_norm_g"], g["mla_out_norm_g"], wout, g["norm_post_g"], wple,
        g["ple_norm_g"], wpg, g["b_ple_gate"])
    dsbq, dsbk, dsbv = _sb_bwd(sbq, sbk, sbv, dsbo)
    dqc, dkc, dmv = _mla_bwd(qc, kc, kc.T, mv, dmlao, lse, delta)
    gx, dwin_ext, dwuq_ext, dwk_ext, dwv, vec_d = _pre_bwd(
        x, dxres, dsbq, dsbk, dsbv, dsbg, dmlag, dqc, dkc, dmv, cq, ckv, tabs, g["norm_pre_g"], win_ext, g["q_norm_g"],
        wuq_ext, g["kv_norm_g"], wk_ext, wv)
    dwin, dwuq, dwukv = _contract_grads(dwin_ext, dwuq_ext, dwk_ext, dwv)
    grads = {"w_in": dwin, "w_uq": dwuq, "w_ukv": dwukv, "w_out": dwout, "w_ple": dwple, "w_ple_gate": dwpg}
    return gx, grads, jnp.concatenate([vec_c, vec_d], axis=0)


_VEC_LAYOUT = (("norm_post_g", 0, 0, 1024), ("ple_norm_g", 1, 0, 1024), ("b_ple_gate", 2, 0, 1024), ("sb_out_norm_g", 3, 0, 512),
               ("mla_out_norm_g", 3, 512, 512), ("norm_pre_g", 8, 0, 1024), ("q_norm_g", 9, 0, 256), ("kv_norm_g", 9, 256, 128))
_LOSS_ROW = 4
_WEIGHT_ORDER = ("norm_pre_g", "w_in", "q_norm_g", "w_uq", "kv_norm_g", "w_ukv", "sb_out_norm_g", "mla_out_norm_g", "w_out",
                 "norm_post_g", "w_ple", "ple_norm_g", "w_ple_gate", "b_ple_gate")


def kernel(x, p, positions, norm_pre_g, w_in, q_norm_g, w_uq, kv_norm_g, w_ukv, sb_out_norm_g, mla_out_norm_g, w_out, norm_post_g, w_ple, ple_norm_g, w_ple_gate, b_ple_gate, loss_target, m_norm_pre_g, m_w_in, m_q_norm_g, m_w_uq, m_kv_norm_g, m_w_ukv, m_sb_out_norm_g, m_mla_out_norm_g, m_w_out, m_norm_post_g, m_w_ple, m_ple_norm_g, m_w_ple_gate, m_b_ple_gate, v_norm_pre_g, v_w_in, v_q_norm_g, v_w_uq, v_kv_norm_g, v_w_ukv, v_sb_out_norm_g, v_mla_out_norm_g, v_w_out, v_norm_post_g, v_w_ple, v_ple_norm_g, v_w_ple_gate, v_b_ple_gate):
    w = {"norm_pre_g": norm_pre_g, "w_in": w_in[0], "q_norm_g": q_norm_g, "w_uq": w_uq[0], "kv_norm_g": kv_norm_g, "w_ukv": w_ukv[0],
         "sb_out_norm_g": sb_out_norm_g, "mla_out_norm_g": mla_out_norm_g, "w_out": w_out[0], "norm_post_g": norm_post_g,
         "w_ple": w_ple[0], "ple_norm_g": ple_norm_g, "w_ple_gate": w_ple_gate[0], "b_ple_gate": b_ple_gate}
    m = {"norm_pre_g": m_norm_pre_g, "w_in": m_w_in[0], "q_norm_g": m_q_norm_g, "w_uq": m_w_uq[0], "kv_norm_g": m_kv_norm_g,
         "w_ukv": m_w_ukv[0], "sb_out_norm_g": m_sb_out_norm_g, "mla_out_norm_g": m_mla_out_norm_g, "w_out": m_w_out[0],
         "norm_post_g": m_norm_post_g, "w_ple": m_w_ple[0], "ple_norm_g": m_ple_norm_g, "w_ple_gate": m_w_ple_gate[0],
         "b_ple_gate": m_b_ple_gate}
    v = {"norm_pre_g": v_norm_pre_g, "w_in": v_w_in[0], "q_norm_g": v_q_norm_g, "w_uq": v_w_uq[0], "kv_norm_g": v_kv_norm_g,
         "w_ukv": v_w_ukv[0], "sb_out_norm_g": v_sb_out_norm_g, "mla_out_norm_g": v_mla_out_norm_g, "w_out": v_w_out[0],
         "norm_post_g": v_norm_post_g, "w_ple": v_w_ple[0], "ple_norm_g": v_ple_norm_g, "w_ple_gate": v_w_ple_gate[0],
         "b_ple_gate": v_b_ple_gate}
    gathered = _allgather_weights([w[n] for n in _BIG])
    wfull = _join_shards(dict(zip(_BIG, gathered)))

    gx, grads, vec = _local_grads(x[0], p[0, 0], positions[0], loss_target[0], w, wfull)

    gsh = _split_shards(grads)
    *gred, vsum = _reduce_scatter_grads([gsh[n] for n in _BIG], vec)
    loss = vsum[_LOSS_ROW, 0]

    g, delta, new_m, new_v = _adamw_small(vsum, w, m, v)
    for n, gn in zip(_BIG, gred):
        g[n] = gn
        delta[n], new_m[n], new_v[n] = _adamw(w[n], gn, m[n], v[n])

    lead = lambda n, a: a[None] if n in _BIG else a
    return (loss, gx[None],
            *[lead(n, g[n]) for n in _WEIGHT_ORDER], *[lead(n, delta[n]) for n in _WEIGHT_ORDER],
            *[lead(n, new_m[n]) for n in _WEIGHT_ORDER], *[lead(n, new_v[n]) for n in _WEIGHT_ORDER])
```

```python
import jax
import jax.numpy as jnp
from jax import lax
from jax.experimental import pallas as pl
from jax.experimental.pallas import tpu as pltpu

F32 = jnp.float32
BF16 = jnp.bfloat16
MESH = pl.DeviceIdType.MESH

D_MODEL = 1024
HEAD_DIM = 64
D_SB = 512
D_MLA = 512
Q_LORA = 256
KV_LORA = 128
QK_NOPE = 64
QK_ROPE = 32
PLE_DIM = 256
D_IN = 2976
D_EXT = 3072
ROPE_THETA = 10000.0
EPS = 1e-6
N_SHARD = 4

ADAM_LR = 0.001
ADAM_B1 = 0.9
ADAM_B2 = 0.999
ADAM_EPS = 1e-08
ADAM_WD = 0.01
ADAM_STEP = 10

LANES = 128
BK = 128
WQ = 256
SB_CUTOFF = 120.0
TM = 256
VEC_ROWS = 16
VMEM_DENSE = 52 * 1024 * 1024
VMEM_ATTN = 40 * 1024 * 1024


def _mm(a, b):
    return jnp.dot(a, b, preferred_element_type=F32)


def _mm_nt(a, b):
    return lax.dot_general(a, b, (((1,), (1,)), ((), ())), preferred_element_type=F32)


def _mm_tn(a, b):
    return lax.dot_general(a, b, (((0,), (0,)), ((), ())), preferred_element_type=F32)


def _seg(a, bd):
    a1 = a.astype(BF16)
    r1 = a - a1.astype(F32)
    a2 = r1.astype(BF16)
    a3 = (r1 - a2.astype(F32)).astype(BF16)
    return _mm(a1, bd) + _mm(a2, bd) + _mm(a3, bd)


def _blockdiag(n, seg):
    r = lax.broadcasted_iota(jnp.int32, (n, n), 0) // seg
    c = lax.broadcasted_iota(jnp.int32, (n, n), 1) // seg
    return jnp.where(r == c, 1.0, 0.0).astype(BF16)


def _sigmoid(a):
    return 1.0 / (1.0 + jnp.exp(-a))


def _rowmean(a):
    return jnp.mean(a, axis=-1, keepdims=True)


def _colsum(a):
    return jnp.sum(a, axis=0, keepdims=True)


def _rope_fwd(a, c, sa, sb):
    w = a.shape[-1]
    return a * c + pltpu.roll(a, w - 16, 1) * sa + pltpu.roll(a, 16, 1) * sb


def _rope_bwd(g, c, sa, sb):
    w = g.shape[-1]
    return g * c + pltpu.roll(g * sa, 16, 1) + pltpu.roll(g * sb, w - 16, 1)


def _full(shape):
    return pl.BlockSpec(shape, lambda *_: (0,) * len(shape))


def _rows(width, tm=TM):
    return pl.BlockSpec((tm, width), lambda i: (i, 0))


def _pre_fwd(x, tabs, gpre, win, gq, wuq, gkv, wk, wv):
    s = x.shape[0]
    c_t, sa_t, sb_t = tabs

    def body(x_ref, c_ref, sa_ref, sb_ref, gpre_ref, win_ref, gq_ref, wuq_ref, gkv_ref, wk_ref, wv_ref,
             sbq_ref, sbk_ref, sbv_ref, sbg_ref, mlag_ref, cq_ref, ckv_ref, qc_ref, kc_ref, mv_ref):
        xv = x_ref[...]
        r1 = lax.rsqrt(_rowmean(xv * xv) + EPS)
        h = (xv * r1 * gpre_ref[...]).astype(BF16)
        proj = _mm(h, win_ref[...])
        sbq_ref[...] = proj[:, 0:512].astype(BF16)
        sbk_ref[...] = proj[:, 512:1024].astype(BF16)
        sbv_ref[...] = proj[:, 1024:1536].astype(BF16)
        sbg_ref[...] = proj[:, 1536:2048]
        cq = proj[:, 2048:2304]
        ckv = proj[:, 2304:2432]
        kr = proj[:, 2432:2560]
        mlag_ref[...] = proj[:, 2560:3072]
        cq_ref[...] = cq
        ckv_ref[...] = ckv
        c1, sa1, sb1 = c_ref[...], sa_ref[...], sb_ref[...]
        c8, sa8, sb8 = jnp.tile(c1, (1, 8)), jnp.tile(sa1, (1, 8)), jnp.tile(sb1, (1, 8))
        cqn = (cq * lax.rsqrt(_rowmean(cq * cq) + EPS) * gq_ref[...]).astype(BF16)
        qe = _mm(cqn, wuq_ref[...])
        qc_ref[...] = _rope_fwd(qe, c8, sa8, sb8).astype(BF16)
        ckvn = (ckv * lax.rsqrt(_rowmean(ckv * ckv) + EPS) * gkv_ref[...]).astype(BF16)
        ke = _mm(ckvn, wk_ref[...])
        krr = _rope_fwd(kr, c1, sa1, sb1)
        kc_ref[...] = (ke + jnp.tile(krr, (1, 8))).astype(BF16)
        mv_ref[...] = _mm(ckvn, wv_ref[...]).astype(BF16)

    out_shape = (
        jax.ShapeDtypeStruct((s, 512), BF16), jax.ShapeDtypeStruct((s, 512), BF16), jax.ShapeDtypeStruct((s, 512), BF16),
        jax.ShapeDtypeStruct((s, 512), F32), jax.ShapeDtypeStruct((s, 512), F32),
        jax.ShapeDtypeStruct((s, Q_LORA), F32), jax.ShapeDtypeStruct((s, KV_LORA), F32),
        jax.ShapeDtypeStruct((s, 1024), BF16), jax.ShapeDtypeStruct((s, 1024), BF16), jax.ShapeDtypeStruct((s, 512), BF16),
    )
    return pl.pallas_call(
        body, name="pre_fwd", grid=(s // TM,), out_shape=out_shape,
        in_specs=[_rows(D_MODEL), _rows(LANES), _rows(LANES), _rows(LANES), _full((1, D_MODEL)), _full((D_MODEL, D_EXT)),
                  _full((1, Q_LORA)), _full((Q_LORA, 1024)), _full((1, KV_LORA)), _full((KV_LORA, 1024)), _full((KV_LORA, 512))],
        out_specs=(_rows(512), _rows(512), _rows(512), _rows(512), _rows(512), _rows(Q_LORA), _rows(KV_LORA),
                   _rows(1024), _rows(1024), _rows(512)),
        compiler_params=pltpu.CompilerParams(vmem_limit_bytes=VMEM_DENSE),
    )(x, c_t, sa_t, sb_t, gpre, win, gq, wuq, gkv, wk, wv)


def _softplus(z):
    neg_abs = lax.bitcast_convert_type(lax.bitcast_convert_type(z, jnp.uint32) | jnp.uint32(0x80000000), F32)
    return jnp.maximum(z, 0.0) + jnp.log(1.0 + jnp.exp(neg_abs))


def _pair_tri(kind):
    r = lax.broadcasted_iota(jnp.int32, (512, 256), 0)
    c = lax.broadcasted_iota(jnp.int32, (512, 256), 1)
    same = ((r // BK) % 2) == (c // BK)
    rk, ck = r % BK, c % BK
    m = {"ge": rk >= ck, "lt": rk < ck, "le": rk <= ck}[kind]
    return jnp.where(same & m, 1.0, 0.0).astype(BF16)


def _split2(a):
    hi = a.astype(BF16)
    lo = (a - hi.astype(F32)).astype(BF16)
    return jnp.concatenate([hi, lo], axis=1)


def _pair_stack(b, lane):
    zero = jnp.zeros_like(b)
    return jnp.concatenate([jnp.where(lane < 64, b, zero), jnp.where(lane >= 64, b, zero)], axis=0)


def _sb_fwd(q, k, v):
    s = q.shape[0]

    def body(q_ref, k_ref, v_ref, o_ref):
        i = pl.program_id(1)
        lane = lax.broadcasted_iota(jnp.int32, (1, LANES), 1)
        row = lax.broadcasted_iota(jnp.int32, (WQ, 2 * BK), 0) + i * WQ
        col = lax.broadcasted_iota(jnp.int32, (WQ, 2 * BK), 1) % BK
        u_ge = _pair_tri("ge")
        qs = q_ref[...] * (HEAD_DIM ** -0.5)

        def scores(j):
            ks = pl.multiple_of(j * BK, BK)
            return _mm_nt(qs, _pair_stack(k_ref[pl.ds(ks, BK), :], lane))

        def decay(z, j, masked):
            sp = _softplus(z)
            if masked:
                sp = jnp.where((col + j * BK) < row, sp, 0.0)
            return _mm(_split2(sp), u_ge)

        def weights(z, cum, j, masked):
            w = jnp.exp(z - cum)
            if masked:
                w = jnp.where((col + j * BK) < row, w, 0.0)
            ks = pl.multiple_of(j * BK, BK)
            return _mm(w.astype(BF16), _pair_stack(v_ref[pl.ds(ks, BK), :], lane))

        def fold(carry, pv, cum):
            acc, run = carry
            return acc + jnp.exp(-run) * pv, run + _heads_narrow(cum, 0, lane)

        def pair(ja, jb, carry, masked):
            za, zb = scores(ja), scores(jb)
            ca = decay(za, ja, masked)
            cb = decay(zb, jb, masked)
            pa = weights(za, ca, ja, masked)
            pb = weights(zb, cb, jb, masked)
            return fold(fold(carry, pa, ca), pb, cb)

        assert WQ == 2 * BK
        acc, run = pair(2 * i + 1, 2 * i, (jnp.zeros((WQ, LANES), F32), jnp.zeros((WQ, LANES), F32)), True)

        def more(c):
            return (c[0] < i) & (c[1] > 0)

        def step(c):
            jj, _, acc, run = c
            acc, run = pair(2 * i - 1 - 2 * jj, 2 * i - 2 - 2 * jj, (acc, run), False)
            return jj + 1, (jnp.min(run) < SB_CUTOFF).astype(jnp.int32), acc, run

        _, _, acc, _ = lax.while_loop(more, step, (jnp.int32(0), (jnp.min(run) < SB_CUTOFF).astype(jnp.int32), acc, run))
        o_ref[...] = acc

    qspec = pl.BlockSpec((WQ, LANES), lambda p, i: (i, p))
    kspec = pl.BlockSpec((s, LANES), lambda p, i: (0, p))
    return pl.pallas_call(
        body, name="sb_fwd", grid=(4, s // WQ),
        out_shape=jax.ShapeDtypeStruct((s, 512), F32),
        in_specs=[qspec, kspec, kspec], out_specs=qspec,
        compiler_params=pltpu.CompilerParams(vmem_limit_bytes=VMEM_ATTN),
    )(q, k, v)


def _pair_tri1(kind):
    r = lax.broadcasted_iota(jnp.int32, (2 * BK, 2 * BK), 0)
    c = lax.broadcasted_iota(jnp.int32, (2 * BK, 2 * BK), 1)
    rk, ck = r % BK, c % BK
    m = {"ge": rk >= ck, "lt": rk < ck, "le": rk <= ck}[kind]
    return jnp.where(((r // BK) == (c // BK)) & m, 1.0, 0.0).astype(BF16)


def _heads_wide(a):
    m = a.shape[0]
    return jnp.concatenate([jnp.broadcast_to(a[:, 0:1], (m, BK)), jnp.broadcast_to(a[:, 64:65], (m, BK))], axis=1)


def _heads_narrow(a, col, lane):
    return jnp.where(lane < 64, a[:, col:col + 1], a[:, BK + col:BK + col + 1])


def _pair_rowsum():
    r = lax.broadcasted_iota(jnp.int32, (512, LANES), 0)
    c = lax.broadcasted_iota(jnp.int32, (512, LANES), 1)
    return jnp.where(((r // BK) % 2) == (c // 64), 1.0, 0.0).astype(BF16)


def _sb_bwd(q, k, v, do):
    s = q.shape[0]
    assert WQ == 2 * BK

    def body(q_ref, k_ref, v_ref, do_ref, dq_ref, dk_ref, dv_ref, later_ref):
        i = pl.program_id(1)

        @pl.when(i == 0)
        def _():
            dk_ref[...] = jnp.zeros_like(dk_ref)
            dv_ref[...] = jnp.zeros_like(dv_ref)

        lane = lax.broadcasted_iota(jnp.int32, (1, LANES), 1)
        row = lax.broadcasted_iota(jnp.int32, (WQ, 2 * BK), 0) + i * WQ
        col = lax.broadcasted_iota(jnp.int32, (WQ, 2 * BK), 1) % BK
        u_ge = _pair_tri("ge")
        u_le = _pair_tri1("le")
        r_sum = _pair_rowsum()
        qs = q_ref[...] * (HEAD_DIM ** -0.5)
        dof = do_ref[...].astype(F32)

        def scan(ja, jb, run, masked):
            za = _mm_nt(qs, _pair_stack(k_ref[pl.ds(pl.multiple_of(ja * BK, BK), BK), :], lane))
            zb = _mm_nt(qs, _pair_stack(k_ref[pl.ds(pl.multiple_of(jb * BK, BK), BK), :], lane))
            spa, spb = _softplus(za), _softplus(zb)
            if masked:
                spa = jnp.where((col + ja * BK) < row, spa, 0.0)
                spb = jnp.where((col + jb * BK) < row, spb, 0.0)
            rsa, rsb = _mm(_split2(spa), r_sum), _mm(_split2(spb), r_sum)
            later_ref[ja] = run
            later_ref[jb] = run + rsa
            return run + rsa + rsb

        run = scan(2 * i + 1, 2 * i, jnp.zeros((WQ, LANES), F32), True)

        def more(c):
            return (c[0] < i) & (c[1] > 0)

        def step(c):
            run = scan(2 * i - 1 - 2 * c[0], 2 * i - 2 - 2 * c[0], c[2], False)
            return c[0] + 1, (jnp.min(run) < SB_CUTOFF).astype(jnp.int32), run

        npairs, _, _ = lax.while_loop(more, step, (jnp.int32(0), (jnp.min(run) < SB_CUTOFF).astype(jnp.int32), run))

        def keys(j):
            ks = pl.multiple_of(j * BK, BK)
            return ks, _pair_stack(k_ref[pl.ds(ks, BK), :], lane)

        def gates(z, j, masked):
            neg_abs = lax.bitcast_convert_type(lax.bitcast_convert_type(z, jnp.uint32) | jnp.uint32(0x80000000), F32)
            u = jnp.exp(neg_abs)
            opu = 1.0 + u
            sp = jnp.maximum(z, 0.0) + jnp.log(opu)
            if masked:
                sp = jnp.where((col + j * BK) < row, sp, 0.0)
            return jnp.where(z >= 0.0, 1.0, u) / opu, _mm(_split2(sp), u_ge)

        def weights(z, cum, ks, j, masked):
            wl = jnp.exp(z - cum)
            if masked:
                wl = jnp.where((col + j * BK) < row, wl, 0.0)
            dfo = (jnp.exp(-later_ref[j]) * dof).astype(BF16)
            e = _mm_nt(dfo, _pair_stack(v_ref[pl.ds(ks, BK), :], lane)) * wl
            return wl.astype(BF16), dfo, e

        def scores_grad(e, sig, erun, j, masked):
            ecum = _mm(e.astype(BF16), u_le) + _heads_wide(erun)
            dz = e - sig * ecum
            if masked:
                dz = jnp.where((col + j * BK) < row, dz, 0.0)
            return dz.astype(BF16), _heads_narrow(ecum, BK - 1, lane)

        def scatter(ks, dzb, wlb, dfo):
            rk = _mm_tn(dzb, qs)
            dk_ref[pl.ds(ks, BK), :] += jnp.where(lane < 64, rk[0:BK], rk[BK:2 * BK])
            rv = _mm_tn(wlb, dfo)
            dv_ref[pl.ds(ks, BK), :] += jnp.where(lane < 64, rv[0:BK], rv[BK:2 * BK])

        def pair(ja, carry, masked):
            dq, erun = carry
            jb = ja + 1
            ksa, kbda = keys(ja)
            ksb, kbdb = keys(jb)
            za, zb = _mm_nt(qs, kbda), _mm_nt(qs, kbdb)
            siga, cuma = gates(za, ja, masked)
            sigb, cumb = gates(zb, jb, masked)
            wla, dfoa, ea = weights(za, cuma, ksa, ja, masked)
            wlb, dfob, eb = weights(zb, cumb, ksb, jb, masked)
            dza, eruna = scores_grad(ea, siga, erun, ja, masked)
            dzb, erunb = scores_grad(eb, sigb, eruna, jb, masked)
            dq = dq + _mm(jnp.concatenate([dza, dzb], axis=1), jnp.concatenate([kbda, kbdb], axis=0))
            scatter(ksa, dza, wla, dfoa)
            scatter(ksb, dzb, wlb, dfob)
            return dq, erunb

        zero = jnp.zeros((WQ, LANES), F32)
        first = 2 * (i - npairs)
        carry = lax.fori_loop(0, npairs, lambda t, c: pair(first + 2 * t, c, False), (zero, zero))
        dq, _ = pair(2 * i, carry, True)
        dq_ref[...] = (dq * (HEAD_DIM ** -0.5)).astype(BF16)

    qspec = pl.BlockSpec((WQ, LANES), lambda p, i: (i, p))
    kspec = pl.BlockSpec((s, LANES), lambda p, i: (0, p))
    return pl.pallas_call(
        body, name="sb_bwd", grid=(4, s // WQ),
        out_shape=(jax.ShapeDtypeStruct((s, 512), BF16), jax.ShapeDtypeStruct((s, 512), F32),
                   jax.ShapeDtypeStruct((s, 512), F32)),
        in_specs=[qspec, kspec, kspec, qspec], out_specs=(qspec, kspec, kspec),
        scratch_shapes=[pltpu.VMEM((s // BK, WQ, LANES), F32)],
        compiler_params=pltpu.CompilerParams(vmem_limit_bytes=VMEM_ATTN),
    )(q, k, v, do)


MLA_SCALE = (QK_NOPE + QK_ROPE) ** -0.5
LOG2E = 1.4426950408889634


def _mla_keys(kb):
    zero = jnp.zeros((BK, LANES), kb.dtype)
    return jnp.concatenate([jnp.concatenate([kb[:, 0:LANES], zero], axis=1),
                            jnp.concatenate([zero, kb[:, LANES:2 * LANES]], axis=1)], axis=0)


def _mla_fwd(qc, kc, vt):
    s = qc.shape[0]
    rows_l = 16

    def body(q_ref, k_ref, vt_ref, o_ref, l_ref, p_scr, ot_scr, st_scr):
        i = pl.program_id(1)
        keyc = lax.broadcasted_iota(jnp.int32, (BK, WQ), 0)
        qryc = (lax.broadcasted_iota(jnp.int32, (BK, WQ), 1) + i * WQ) // 64
        row = lax.broadcasted_iota(jnp.int32, (LANES, 1), 0)
        qw = q_ref[...]
        orow = lax.broadcasted_iota(jnp.int32, (rows_l, 2 * BK), 0)
        ocol = lax.broadcasted_iota(jnp.int32, (rows_l, 2 * BK), 1)
        ones = jnp.where(((orow == 0) & (ocol < BK)) | ((orow == 1) & (ocol >= BK)), 1.0, 0.0).astype(BF16)

        def scores(j):
            ks = pl.multiple_of(j * BK, BK)
            return _mm_nt(_mla_keys(k_ref[pl.ds(ks, BK), :]), qw)

        def values_t(j):
            vtb = vt_ref[:, pl.ds(pl.multiple_of(j * BK, BK), BK)]
            zero = jnp.zeros_like(vtb)
            top = jnp.concatenate([jnp.where(row < 64, vtb, zero), jnp.where(row >= 64, vtb, zero)], axis=1)
            return jnp.concatenate([top, ones], axis=0)

        def softmax(ja, za, zb, masked):
            c = MLA_SCALE * LOG2E
            parts = [za[0:BK] * c, za[BK:2 * BK] * c, zb[0:BK] * c, zb[BK:2 * BK] * c]
            if masked:
                va = ((keyc + ja * BK) // 64) <= qryc
                vb = ((keyc + (ja + 1) * BK) // 64) <= qryc
                parts = [jnp.where(va, parts[0], -1e30), jnp.where(va, parts[1], -1e30),
                         jnp.where(vb, parts[2], -1e30), jnp.where(vb, parts[3], -1e30)]
            m0, m1 = st_scr[0:1, :], st_scr[1:2, :]
            n0 = jnp.maximum(m0, jnp.max(jnp.maximum(parts[0], parts[2]), axis=0, keepdims=True))
            n1 = jnp.maximum(m1, jnp.max(jnp.maximum(parts[1], parts[3]), axis=0, keepdims=True))
            st_scr[2:3, :] = jnp.exp2(m0 - n0)
            st_scr[3:4, :] = jnp.exp2(m1 - n1)
            st_scr[0:1, :] = n0
            st_scr[1:2, :] = n1
            p_scr[...] = jnp.concatenate([jnp.exp2(parts[0] - n0), jnp.exp2(parts[1] - n1),
                                          jnp.exp2(parts[2] - n0), jnp.exp2(parts[3] - n1)], axis=0).astype(BF16)

        def accumulate(ja):
            pv = _mm(jnp.concatenate([values_t(ja), values_t(ja + 1)], axis=1), p_scr[...])
            a = jnp.where(row < 64, st_scr[2:3, :], st_scr[3:4, :])
            ot_scr[0:LANES, :] = a * ot_scr[0:LANES, :] + pv[0:LANES]
            ot_scr[LANES:LANES + 8, :] = st_scr[2:10, :] * ot_scr[LANES:LANES + 8, :] + pv[LANES:LANES + 8]

        def step(n, masked):
            za, zb = scores(2 * n), scores(2 * n + 1)
            accumulate(2 * n - 2)
            softmax(2 * n, za, zb, masked)

        def first(masked):
            softmax(0, scores(0), scores(1), masked)

        st_scr[...] = jnp.concatenate([jnp.full((2, WQ), -1e30, F32), jnp.ones((14, WQ), F32)], axis=0)
        ot_scr[...] = jnp.zeros_like(ot_scr)

        @pl.when(i == 0)
        def _():
            first(True)

        @pl.when(i > 0)
        def _():
            first(False)
            lax.fori_loop(1, i, lambda n, c: (step(n, False), c)[1], 0)
            step(i, True)

        accumulate(2 * i)
        l0, l1 = ot_scr[LANES:LANES + 1, :], ot_scr[LANES + 1:LANES + 2, :]
        o_ref[...] = (ot_scr[0:LANES, :] / jnp.where(row < 64, l0, l1)).T
        l_ref[...] = jnp.where(row < 64, st_scr[0:1, :] + jnp.log2(l0), st_scr[1:2, :] + jnp.log2(l1)).T

    qspec = pl.BlockSpec((WQ, 2 * LANES), lambda p, i: (i, p))
    kspec = pl.BlockSpec((s, 2 * LANES), lambda p, i: (0, p))
    vtspec = pl.BlockSpec((LANES, s), lambda p, i: (p, 0))
    ospec = pl.BlockSpec((WQ, LANES), lambda p, i: (i, p))
    return pl.pallas_call(
        body, name="mla_fwd", grid=(4, s // WQ),
        out_shape=(jax.ShapeDtypeStruct((s, 512), F32), jax.ShapeDtypeStruct((s, 512), F32)),
        in_specs=[qspec, kspec, vtspec], out_specs=(ospec, ospec),
        scratch_shapes=[pltpu.VMEM((4 * BK, WQ), BF16), pltpu.VMEM((LANES + 8, WQ), F32), pltpu.VMEM((16, WQ), F32)],
        compiler_params=pltpu.CompilerParams(vmem_limit_bytes=VMEM_ATTN),
    )(qc, kc, vt)


def _mla_bwd(qc, kc, kct, v, do, lse, delta):
    s = qc.shape[0]

    def body(q_ref, k_ref, kt_ref, v_ref, do_ref, l_ref, d_ref, dq_ref, dk_ref, dv_ref, dqt_scr, p_scr, dz_scr):
        i = pl.program_id(1)

        @pl.when(i == 0)
        def _():
            dk_ref[...] = jnp.zeros_like(dk_ref)
            dv_ref[...] = jnp.zeros_like(dv_ref)

        lane = lax.broadcasted_iota(jnp.int32, (1, LANES), 1)
        keyc = lax.broadcasted_iota(jnp.int32, (BK, WQ), 0)
        qryc = (lax.broadcasted_iota(jnp.int32, (BK, WQ), 1) + i * WQ) // 64
        qw = q_ref[...]
        dob = do_ref[...]
        dost = (dob.astype(F32) * MLA_SCALE).T.astype(BF16)
        lt = l_ref[...].T
        dt = (d_ref[...] * MLA_SCALE).T
        lse0, lse1 = lt[0:1], lt[64:65]
        dl0, dl1 = dt[0:1], dt[64:65]
        dqt_scr[...] = jnp.zeros_like(dqt_scr)

        def products(j):
            ks = pl.multiple_of(j * BK, BK)
            return (_mm_nt(_mla_keys(k_ref[pl.ds(ks, BK), :]), qw), _mm(_pair_stack(v_ref[pl.ds(ks, BK), :], lane), dost))

        def grads(j, slot, zt, dwt, masked):
            zt = zt * (MLA_SCALE * LOG2E)
            p0 = jnp.exp2(zt[0:BK] - lse0)
            p1 = jnp.exp2(zt[BK:2 * BK] - lse1)
            if masked:
                valid = ((keyc + j * BK) // 64) <= qryc
                p0, p1 = jnp.where(valid, p0, 0.0), jnp.where(valid, p1, 0.0)
            p_scr[slot] = jnp.concatenate([p0, p1], axis=0).astype(BF16)
            dz_scr[slot] = jnp.concatenate([p0 * (dwt[0:BK] - dl0), p1 * (dwt[BK:2 * BK] - dl1)], axis=0).astype(BF16)

        def keys_t(ks):
            ktb = kt_ref[:, pl.ds(ks, BK)]
            zero = jnp.zeros((LANES, BK), ktb.dtype)
            return jnp.concatenate([jnp.concatenate([ktb[0:LANES], zero], axis=1),
                                    jnp.concatenate([zero, ktb[LANES:2 * LANES]], axis=1)], axis=0)

        def scatter(ja):
            ksa, ksb = pl.multiple_of(ja * BK, BK), pl.multiple_of((ja + 1) * BK, BK)
            dqt_scr[...] += _mm(jnp.concatenate([keys_t(ksa), keys_t(ksb)], axis=1),
                                jnp.concatenate([dz_scr[0], dz_scr[1]], axis=0))
            for slot, ks in ((0, ksa), (1, ksb)):
                rk = _mm(dz_scr[slot], qw)
                dk_ref[pl.ds(ks, BK), :] += jnp.concatenate([rk[0:BK, 0:LANES], rk[BK:2 * BK, LANES:2 * LANES]], axis=1)
                rv = _mm(p_scr[slot], dob)
                dv_ref[pl.ds(ks, BK), :] += jnp.where(lane < 64, rv[0:BK], rv[BK:2 * BK])

        def step(n, masked):
            za, wa = products(2 * n)
            zb, wb = products(2 * n + 1)
            scatter(2 * n - 2)
            grads(2 * n, 0, za, wa, masked)
            grads(2 * n + 1, 1, zb, wb, masked)

        def first(masked):
            za, wa = products(0)
            zb, wb = products(1)
            grads(0, 0, za, wa, masked)
            grads(1, 1, zb, wb, masked)

        @pl.when(i == 0)
        def _():
            first(True)

        @pl.when(i > 0)
        def _():
            first(False)
            lax.fori_loop(1, i, lambda n, c: (step(n, False), c)[1], 0)
            step(i, True)

        scatter(2 * i)
        dq_ref[...] = dqt_scr[...].T

    qspec = pl.BlockSpec((WQ, 2 * LANES), lambda p, i: (i, p))
    kspec = pl.BlockSpec((s, 2 * LANES), lambda p, i: (0, p))
    ktspec = pl.BlockSpec((2 * LANES, s), lambda p, i: (p, 0))
    vspec = pl.BlockSpec((s, LANES), lambda p, i: (0, p))
    ospec = pl.BlockSpec((WQ, LANES), lambda p, i: (i, p))
    return pl.pallas_call(
        body, name="mla_bwd", grid=(4, s // WQ),
        out_shape=(jax.ShapeDtypeStruct((s, 1024), F32), jax.ShapeDtypeStruct((s, 1024), F32),
                   jax.ShapeDtypeStruct((s, 512), F32)),
        in_specs=[qspec, kspec, ktspec, vspec, ospec, ospec, ospec], out_specs=(qspec, kspec, vspec),
        scratch_shapes=[pltpu.VMEM((2 * LANES, WQ), F32), pltpu.VMEM((2, 2 * BK, WQ), BF16), pltpu.VMEM((2, 2 * BK, WQ), BF16)],
        compiler_params=pltpu.CompilerParams(vmem_limit_bytes=VMEM_ATTN),
    )(qc, kc, kct, v, do, lse, delta)


def _post(x, p, tgt, sbo, mlao, sbg, mlag, gsb, gmla, wout, gpost, wple, gple, wpg, bpg):
    s = x.shape[0]

    def body(x_ref, p_ref, t_ref, sbo_ref, mlao_ref, sbg_ref, mlag_ref, gsb_ref, gmla_ref, wout_ref,
             gpost_ref, wple_ref, gple_ref, wpg_ref, bpg_ref,
             dsbo_ref, dmlao_ref, delta_ref, dsbg_ref, dmlag_ref, dxres_ref, dwout_ref, dwpg_ref, dwple_ref, vec_ref):
        i = pl.program_id(0)

        @pl.when(i == 0)
        def _():
            dwout_ref[...] = jnp.zeros_like(dwout_ref)
            dwpg_ref[...] = jnp.zeros_like(dwpg_ref)
            dwple_ref[...] = jnp.zeros_like(dwple_ref)
            vec_ref[...] = jnp.zeros_like(vec_ref)

        bd = _blockdiag(512, HEAD_DIM)
        inv_hd = 1.0 / HEAD_DIM

        def head_fwd(o, g, gate):
            r = lax.rsqrt(_seg(o * o, bd) * inv_hd + EPS)
            hat = o * r
            n = hat * g
            sg = _sigmoid(gate)
            return hat, r, n, sg, n * (gate * sg)

        sbo, mlao, sbg_v, mlag_v = sbo_ref[...], mlao_ref[...], sbg_ref[...], mlag_ref[...]
        gsb_v, gmla_v = gsb_ref[...], gmla_ref[...]
        sb_hat, sb_r, sb_n, sb_sg, sb_y = head_fwd(sbo, gsb_v, sbg_v)
        ml_hat, ml_r, ml_n, ml_sg, ml_y = head_fwd(mlao, gmla_v, mlag_v)
        mix = jnp.concatenate([sb_y, ml_y], axis=1).astype(BF16)
        y = _mm(mix, wout_ref[...])
        ry = lax.rsqrt(_rowmean(y * y) + EPS)
        y_hat = y * ry
        gpost_v = gpost_ref[...]
        x1 = x_ref[...] + y_hat * gpost_v
        pb = p_ref[...].astype(BF16)
        pl_ = _mm(pb, wple_ref[...])
        rp = lax.rsqrt(_rowmean(pl_ * pl_) + EPS)
        pl_hat = pl_ * rp
        gple_v = gple_ref[...]
        ple = pl_hat * gple_v
        x1b = x1.astype(BF16)
        gate = _sigmoid(_mm(x1b, wpg_ref[...]) + bpg_ref[...])
        err = x1 + ple * gate - t_ref[...]
        loss = 0.5 * jnp.sum(_rowmean(err * err))
        dout = err * (1.0 / D_MODEL)

        du = dout * ple * gate * (1.0 - gate)
        dub = du.astype(BF16)
        dple = dout * gate
        dx1 = dout + _mm_nt(dub, wpg_ref[...])
        dwpg_ref[...] += _mm_tn(x1b, dub)
        dplh = dple * gple_v
        dpl = rp * (dplh - pl_hat * _rowmean(dplh * pl_hat))
        dwple_ref[...] += _mm_tn(pb, dpl.astype(BF16))
        dxres_ref[...] = dx1
        dyh = dx1 * gpost_v
        dy = ry * (dyh - y_hat * _rowmean(dyh * y_hat))
        dyb = dy.astype(BF16)
        dwout_ref[...] += _mm_tn(mix, dyb)
        dmix = _mm_nt(dyb, wout_ref[...])

        def head_bwd(dyv, hat, r, n, sg, g, gate):
            dn = dyv * (gate * sg)
            dgate = dyv * n * (sg * (1.0 + gate * (1.0 - sg)))
            dhat = dn * g
            do = r * (dhat - hat * (_seg(dhat * hat, bd) * inv_hd))
            return do, dgate, _colsum(dn * hat)

        dsbo, dsbg, dg_sb = head_bwd(dmix[:, 0:512], sb_hat, sb_r, sb_n, sb_sg, gsb_v, sbg_v)
        dmlao, dmlag, dg_ml = head_bwd(dmix[:, 512:1024], ml_hat, ml_r, ml_n, ml_sg, gmla_v, mlag_v)
        dsbo_ref[...] = dsbo.astype(BF16)
        dmlao_ref[...] = dmlao.astype(BF16)
        delta_ref[...] = _seg(dmlao * mlao, bd)
        dsbg_ref[...] = dsbg.astype(BF16)
        dmlag_ref[...] = dmlag.astype(BF16)
        vec_ref[pl.ds(0, 1), :] += _colsum(dx1 * y_hat)
        vec_ref[pl.ds(1, 1), :] += _colsum(dple * pl_hat)
        vec_ref[pl.ds(2, 1), :] += _colsum(du)
        vec_ref[pl.ds(3, 1), :] += jnp.concatenate([dg_sb, dg_ml], axis=1)
        vec_ref[pl.ds(4, 1), :] += jnp.full((1, D_MODEL), loss, F32)

    out_shape = (
        jax.ShapeDtypeStruct((s, 512), BF16), jax.ShapeDtypeStruct((s, 512), BF16), jax.ShapeDtypeStruct((s, 512), F32),
        jax.ShapeDtypeStruct((s, 512), BF16), jax.ShapeDtypeStruct((s, 512), BF16), jax.ShapeDtypeStruct((s, D_MODEL), F32),
        jax.ShapeDtypeStruct((D_MODEL, D_MODEL), F32), jax.ShapeDtypeStruct((D_MODEL, D_MODEL), F32),
        jax.ShapeDtypeStruct((PLE_DIM, D_MODEL), F32), jax.ShapeDtypeStruct((8, D_MODEL), F32),
    )
    return pl.pallas_call(
        body, name="post_fwd_bwd", grid=(s // TM,), out_shape=out_shape,
        in_specs=[_rows(D_MODEL), _rows(PLE_DIM), _rows(D_MODEL), _rows(512), _rows(512), _rows(512), _rows(512),
                  _full((1, 512)), _full((1, 512)), _full((D_MODEL, D_MODEL)),
                  _full((1, D_MODEL)), _full((PLE_DIM, D_MODEL)), _full((1, D_MODEL)), _full((D_MODEL, D_MODEL)),
                  _full((1, D_MODEL))],
        out_specs=(_rows(512), _rows(512), _rows(512), _rows(512), _rows(512), _rows(D_MODEL),
                   _full((D_MODEL, D_MODEL)), _full((D_MODEL, D_MODEL)), _full((PLE_DIM, D_MODEL)), _full((8, D_MODEL))),
        compiler_params=pltpu.CompilerParams(vmem_limit_bytes=VMEM_DENSE),
    )(x, p, tgt, sbo, mlao, sbg, mlag, gsb, gmla, wout, gpost, wple, gple, wpg, bpg)


def _pre_bwd(x, dxres, dsbq, dsbk, dsbv, dsbg, dmlag, dqc, dkc, dmv, cq, ckv, tabs, gpre, win, gq, wuq, gkv, wk, wv):
    s = x.shape[0]
    c_t, sa_t, sb_t = tabs

    def body(x_ref, dxres_ref, dsbq_ref, dsbk_ref, dsbv_ref, dsbg_ref, dmlag_ref, dqc_ref, dkc_ref, dmv_ref, cq_ref,
             ckv_ref, c_ref, sa_ref, sb_ref, gpre_ref, win_ref, gq_ref, wuq_ref, gkv_ref, wk_ref, wv_ref,
             gx_ref, dwin_ref, dwuq_ref, dwk_ref, dwv_ref, vec_ref, dwin_acc):
        i = pl.program_id(0)

        @pl.when(i == 0)
        def _():
            dwin_acc[...] = jnp.zeros_like(dwin_acc)
            dwuq_ref[...] = jnp.zeros_like(dwuq_ref)
            dwk_ref[...] = jnp.zeros_like(dwk_ref)
            dwv_ref[...] = jnp.zeros_like(dwv_ref)
            vec_ref[...] = jnp.zeros_like(vec_ref)

        lane = lax.broadcasted_iota(jnp.int32, (1, LANES), 1)
        c1, sa1, sb1 = c_ref[...], sa_ref[...], sb_ref[...]
        c8, sa8, sb8 = jnp.tile(c1, (1, 8)), jnp.tile(sa1, (1, 8)), jnp.tile(sb1, (1, 8))

        def norm_bwd(dn, hat, r, g):
            t = dn * g
            return r * (t - hat * _rowmean(t * hat)), _colsum(dn * hat)

        dqeb = _rope_bwd(dqc_ref[...], c8, sa8, sb8).astype(BF16)
        cq = cq_ref[...]
        rq = lax.rsqrt(_rowmean(cq * cq) + EPS)
        cq_hat = cq * rq
        gq_v = gq_ref[...]
        dwuq_ref[...] += _mm_tn((cq_hat * gq_v).astype(BF16), dqeb)
        dcq, dg_q = norm_bwd(_mm_nt(dqeb, wuq_ref[...]), cq_hat, rq, gq_v)

        dkc = dkc_ref[...]
        dkcb = dkc.astype(BF16)
        dmvb = dmv_ref[...].astype(BF16)
        ckv = ckv_ref[...]
        rkv = lax.rsqrt(_rowmean(ckv * ckv) + EPS)
        ckv_hat = ckv * rkv
        gkv_v = gkv_ref[...]
        ckvnb = (ckv_hat * gkv_v).astype(BF16)
        dwk_ref[...] += _mm_tn(ckvnb, dkcb)
        dwv_ref[...] += _mm_tn(ckvnb, dmvb)
        dckv, dg_kv = norm_bwd(_mm_nt(dkcb, wk_ref[...]) + _mm_nt(dmvb, wv_ref[...]), ckv_hat, rkv, gkv_v)

        dkr = dkc[:, 0:LANES]
        for hh in range(1, 8):
            dkr = dkr + dkc[:, LANES * hh:LANES * (hh + 1)]
        dkr = _rope_bwd(dkr, c1, sa1, sb1)
        dkr = jnp.where((lane >= 64) & (lane < 96), dkr, 0.0)

        dproj = jnp.concatenate([dsbq_ref[...], dsbk_ref[...].astype(BF16), dsbv_ref[...].astype(BF16), dsbg_ref[...],
                                 dcq.astype(BF16), dckv.astype(BF16), dkr.astype(BF16), dmlag_ref[...]], axis=1)
        xv = x_ref[...]
        r1 = lax.rsqrt(_rowmean(xv * xv) + EPS)
        x_hat = xv * r1
        gpre_v = gpre_ref[...]
        dwin_acc[...] += _mm_tn((x_hat * gpre_v).astype(BF16), dproj)
        dx, dg_pre = norm_bwd(_mm_nt(dproj, win_ref[...]), x_hat, r1, gpre_v)
        gx_ref[...] = dxres_ref[...] + dx
        vec_ref[pl.ds(0, 1), :] += dg_pre
        vec_ref[pl.ds(1, 1), :] += jnp.concatenate([dg_q, dg_kv, jnp.zeros((1, D_MODEL - Q_LORA - KV_LORA), F32)], axis=1)

        @pl.when(i == pl.num_programs(0) - 1)
        def _():
            pltpu.sync_copy(dwin_acc, dwin_ref)

    out_shape = (
        jax.ShapeDtypeStruct((s, D_MODEL), F32), jax.ShapeDtypeStruct((D_MODEL, D_EXT), F32),
        jax.ShapeDtypeStruct((Q_LORA, 1024), F32), jax.ShapeDtypeStruct((KV_LORA, 1024), F32),
        jax.ShapeDtypeStruct((KV_LORA, 512), F32), jax.ShapeDtypeStruct((8, D_MODEL), F32),
    )
    return pl.pallas_call(
        body, name="pre_bwd", grid=(s // TM,), out_shape=out_shape,
        in_specs=[_rows(D_MODEL), _rows(D_MODEL), _rows(512), _rows(512), _rows(512), _rows(512), _rows(512),
                  _rows(1024), _rows(1024), _rows(512), _rows(Q_LORA), _rows(KV_LORA), _rows(LANES), _rows(LANES),
                  _rows(LANES), _full((1, D_MODEL)), _full((D_MODEL, D_EXT)), _full((1, Q_LORA)), _full((Q_LORA, 1024)),
                  _full((1, KV_LORA)), _full((KV_LORA, 1024)), _full((KV_LORA, 512))],
        out_specs=(_rows(D_MODEL), pl.BlockSpec(memory_space=pl.ANY), _full((Q_LORA, 1024)), _full((KV_LORA, 1024)),
                   _full((KV_LORA, 512)), _full((8, D_MODEL))),
        scratch_shapes=[pltpu.VMEM((D_MODEL, D_EXT), F32)],
        compiler_params=pltpu.CompilerParams(vmem_limit_bytes=VMEM_DENSE),
    )(x, dxres, dsbq, dsbk, dsbv, dsbg, dmlag, dqc, dkc, dmv, cq, ckv, c_t, sa_t, sb_t, gpre, win, gq, wuq, gkv, wk, wv)


def _place():
    return lax.axis_index("x"), lax.axis_index("y"), lax.axis_index("c")


def _allgather_weights(shards):
    n = len(shards)

    def body(*refs):
        ins, outs, send_sems, recv_sems = refs[:n], refs[n:2 * n], refs[2 * n], refs[2 * n + 1]
        x, y, c = _place()
        me, sib = (x, y, c), (x, y, 1 - c)
        chips = [(1 - x, y), (x, 1 - y), (1 - x, 1 - y)]

        def half(t, chip, hc):
            rows = shards[t].shape[0] // 2
            return outs[t].at[2 * chip[0] + chip[1], pl.ds(pl.multiple_of(hc * rows, 16), rows), :]

        def copy(k, t, chip, hc, to):
            return pltpu.make_async_remote_copy(src_ref=half(t, chip, hc), dst_ref=half(t, chip, hc), send_sem=send_sems.at[k],
                                                recv_sem=recv_sems.at[k], device_id=to, device_id_type=MESH)

        first, passed = [], []
        for t in range(n):
            outs[t][2 * x + y] = ins[t][...].astype(BF16)
            for j, chip in enumerate(chips):
                cp = copy(6 * t + j, t, (x, y), c, (*chip, c))
                cp.start()
                first.append(cp)
        for t in range(n):
            for j, chip in enumerate(chips):
                copy(6 * t + j, t, chip, c, me).wait_recv()
                cp = copy(6 * t + 3 + j, t, chip, c, sib)
                cp.start()
                passed.append(cp)
        for t in range(n):
            for j, chip in enumerate(chips):
                copy(6 * t + 3 + j, t, chip, 1 - c, me).wait_recv()
        for cp in first + passed:
            cp.wait_send()

    return pl.pallas_call(
        body, name="allgather_weights",
        out_shape=tuple(jax.ShapeDtypeStruct((N_SHARD,) + a.shape, BF16) for a in shards),
        in_specs=[pl.BlockSpec(memory_space=pltpu.VMEM)] * n, out_specs=(pl.BlockSpec(memory_space=pltpu.VMEM),) * n,
        scratch_shapes=[pltpu.SemaphoreType.DMA((6 * n,)), pltpu.SemaphoreType.DMA((6 * n,))],
        compiler_params=pltpu.CompilerParams(vmem_limit_bytes=VMEM_ATTN),
    )(*shards)


def _reduce_scatter_grads(gsh, vec):
    n = len(gsh)
    halves = [a.shape[1] // 2 for a in gsh]

    def body(*refs):
        g_refs, vec_ref, f_refs, vsum_ref = refs[:n], refs[n], refs[n + 1:2 * n + 1], refs[2 * n + 1]
        scr = refs[2 * n + 2:]
        accs, sibs, sbufs, rbufs = scr[0:n], scr[n:2 * n], scr[2 * n:3 * n], scr[3 * n:4 * n]
        vrecv, local_sems, send_sems, recv_sems = scr[4 * n:4 * n + 4]
        x, y, c = _place()
        me, sib = (x, y, c), (x, y, 1 - c)
        mine = 2 * x + y
        chips = [(1 - x, y), (x, 1 - y), (1 - x, 1 - y)]

        def remote(k, src, dst, to):
            return pltpu.make_async_remote_copy(src_ref=src, dst_ref=dst, send_sem=send_sems.at[k], recv_sem=recv_sems.at[k],
                                                device_id=to, device_id_type=MESH)

        def half3(ref, t, hc):
            return ref.at[:, pl.ds(pl.multiple_of(hc * halves[t], 8), halves[t]), :]

        def half2(ref, t, hc):
            return ref.at[pl.ds(pl.multiple_of(hc * halves[t], 8), halves[t]), :]

        loads, sends = [], []
        for t in range(n):
            ld = pltpu.make_async_copy(half3(g_refs[t], t, c), accs[t], local_sems.at[t])
            ld.start()
            loads.append(ld)
            cp = remote(t, half3(g_refs[t], t, 1 - c), sibs[t], sib)
            cp.start()
            sends.append(cp)

        my_dev = 4 * x + 2 * y + c
        vrecv[my_dev] = vec_ref[...]
        for k in range(1, 8):
            to = (x ^ ((k >> 2) & 1), y ^ ((k >> 1) & 1), c ^ (k & 1))
            cp = remote(n + k - 1, vec_ref, vrecv.at[my_dev], to)
            cp.start()
            sends.append(cp)

        for t in range(n):
            loads[t].wait()
            remote(t, half3(g_refs[t], t, 1 - c), sibs[t], me).wait_recv()
            for k in range(N_SHARD):
                accs[t][k] = accs[t][k] + sibs[t][k]
            for j, chip in enumerate(chips):
                idx = 2 * chip[0] + chip[1]
                sbufs[t][idx] = accs[t][idx].astype(BF16)
                cp = remote(n + 7 + 3 * t + j, sbufs[t].at[idx], rbufs[t].at[mine], (*chip, c))
                cp.start()
                sends.append(cp)

        for t in range(n):
            total = accs[t][mine]
            for j, chip in enumerate(chips):
                idx = 2 * chip[0] + chip[1]
                remote(n + 7 + 3 * t + j, sbufs[t].at[idx], rbufs[t].at[idx], me).wait_recv()
                total = total + rbufs[t][idx].astype(F32)
            half2(f_refs[t], t, c)[...] = total
            cp = remote(4 * n + 7 + t, half2(f_refs[t], t, c), half2(f_refs[t], t, c), sib)
            cp.start()
            sends.append(cp)
        for t in range(n):
            remote(4 * n + 7 + t, half2(f_refs[t], t, 1 - c), half2(f_refs[t], t, 1 - c), me).wait_recv()

        for k in range(1, 8):
            src_dev = 4 * (x ^ ((k >> 2) & 1)) + 2 * (y ^ ((k >> 1) & 1)) + (c ^ (k & 1))
            remote(n + k - 1, vec_ref, vrecv.at[src_dev], me).wait_recv()
        vs = vrecv[0]
        for d in range(1, 8):
            vs = vs + vrecv[d]
        vsum_ref[...] = vs

        for cp in sends:
            cp.wait_send()

    nsem = 5 * n + 7
    half_shapes = [(N_SHARD, h, a.shape[2]) for h, a in zip(halves, gsh)]
    return pl.pallas_call(
        body, name="reduce_scatter_grads",
        out_shape=tuple(jax.ShapeDtypeStruct(a.shape[1:], F32) for a in gsh) + (jax.ShapeDtypeStruct((VEC_ROWS, 1024), F32),),
        in_specs=[pl.BlockSpec(memory_space=pl.ANY)] * n + [pl.BlockSpec(memory_space=pltpu.VMEM)],
        out_specs=(pl.BlockSpec(memory_space=pltpu.VMEM),) * (n + 1),
        scratch_shapes=([pltpu.VMEM(s_, F32) for s_ in half_shapes] * 2 + [pltpu.VMEM(s_, BF16) for s_ in half_shapes] * 2
                        + [pltpu.VMEM((8, VEC_ROWS, 1024), F32), pltpu.SemaphoreType.DMA((n,)),
                           pltpu.SemaphoreType.DMA((nsem,)), pltpu.SemaphoreType.DMA((nsem,))]),
        compiler_params=pltpu.CompilerParams(vmem_limit_bytes=56 * 1024 * 1024),
    )(*gsh, vec)


def _adamw(w, g, m, v):
    rows, cols = w.shape
    tr = rows if rows <= 256 else 256

    def body(w_ref, g_ref, m_ref, v_ref, d_ref, nm_ref, nv_ref):
        d_ref[...], nm_ref[...], nv_ref[...] = _adam_math(w_ref[...], g_ref[...], m_ref[...], v_ref[...])

    spec = pl.BlockSpec((tr, cols), lambda i: (i, 0))
    shp = jax.ShapeDtypeStruct((rows, cols), F32)
    return pl.pallas_call(body, name="adamw", grid=(rows // tr,), out_shape=(shp, shp, shp),
                          in_specs=[spec] * 4, out_specs=(spec,) * 3)(w, g, m, v)


def _adam_math(w, g, m, v):
    m2 = ADAM_B1 * m + (1.0 - ADAM_B1) * g
    v2 = ADAM_B2 * v + (1.0 - ADAM_B2) * (g * g)
    m_hat = m2 / (1.0 - ADAM_B1 ** ADAM_STEP)
    v_hat = v2 / (1.0 - ADAM_B2 ** ADAM_STEP)
    return -ADAM_LR * (m_hat / (jnp.sqrt(v_hat) + ADAM_EPS) + ADAM_WD * w), m2, v2


def _adamw_small(vsum, w, m, v):
    names = [name for name, _, _, _ in _VEC_LAYOUT]
    k = len(names)

    def body(*refs):
        vs_ref, w_refs, m_refs, v_refs = refs[0], refs[1:1 + k], refs[1 + k:1 + 2 * k], refs[1 + 2 * k:1 + 3 * k]
        outs = refs[1 + 3 * k:]
        for idx, (_, r, c0, width) in enumerate(_VEC_LAYOUT):
            gv = vs_ref[pl.ds(r, 1), pl.ds(c0, width)]
            d, m2, v2 = _adam_math(w_refs[idx][...], gv, m_refs[idx][...], v_refs[idx][...])
            outs[idx][...], outs[k + idx][...], outs[2 * k + idx][...], outs[3 * k + idx][...] = gv, d, m2, v2

    shapes = tuple(jax.ShapeDtypeStruct(w[name].shape, F32) for name in names)
    res = pl.pallas_call(
        body, name="adamw_small", out_shape=shapes * 4,
        in_specs=[pl.BlockSpec(memory_space=pltpu.VMEM)] * (1 + 3 * k), out_specs=(pl.BlockSpec(memory_space=pltpu.VMEM),) * (4 * k),
    )(vsum, *[w[name] for name in names], *[m[name] for name in names], *[v[name] for name in names])
    return tuple({name: res[part * k + idx] for idx, name in enumerate(names)} for part in range(4))


_BIG = ("w_in", "w_uq", "w_ukv", "w_out", "w_ple", "w_ple_gate")
_COL_SHARDED = ("w_in", "w_uq", "w_ukv", "w_ple")


def _join_shards(parts):
    cols = lambda a: a.transpose(1, 0, 2).reshape(a.shape[1], N_SHARD * a.shape[2])
    rows = lambda a: a.reshape(N_SHARD * a.shape[1], a.shape[2])
    return {n: (cols if n in _COL_SHARDED else rows)(parts[n]) for n in _BIG}


def _split_shards(full):
    cols = lambda a: a.reshape(a.shape[0], N_SHARD, a.shape[1] // N_SHARD).transpose(1, 0, 2)
    rows = lambda a: a.reshape(N_SHARD, a.shape[0] // N_SHARD, a.shape[1])
    return {n: (cols if n in _COL_SHARDED else rows)(full[n]) for n in _BIG}


def _extend_weights(w):
    win = w["w_in"]
    zeros = lambda r, c: jnp.zeros((r, c), win.dtype)
    win_ext = jnp.concatenate([win[:, :2432], zeros(D_MODEL, 64), win[:, 2432:2464], zeros(D_MODEL, 32), win[:, 2464:]], axis=1)
    wuq_ext = jnp.pad(w["w_uq"].reshape(Q_LORA, 8, 96), ((0, 0), (0, 0), (0, 32))).reshape(Q_LORA, 1024)
    wukv = w["w_ukv"].reshape(KV_LORA, 8, 128)
    wk_ext = jnp.pad(wukv[:, :, :64], ((0, 0), (0, 0), (0, 64))).reshape(KV_LORA, 1024)
    wv = wukv[:, :, 64:].reshape(KV_LORA, 512)
    return win_ext, wuq_ext, wk_ext, wv


def _contract_grads(dwin_ext, dwuq_ext, dwk_ext, dwv):
    dwin = jnp.concatenate([dwin_ext[:, :2432], dwin_ext[:, 2496:2528], dwin_ext[:, 2560:]], axis=1)
    dwuq = dwuq_ext.reshape(Q_LORA, 8, 128)[:, :, :96].reshape(Q_LORA, 768)
    dwukv = jnp.concatenate([dwk_ext.reshape(KV_LORA, 8, 128)[:, :, :64], dwv.reshape(KV_LORA, 8, 64)], axis=2)
    return dwin, dwuq, dwukv.reshape(KV_LORA, 1024)


def _rope_tables(positions):
    half = QK_ROPE // 2
    freq = ROPE_THETA ** (-jnp.arange(half, dtype=F32) / half)
    ang = positions.astype(F32)[:, None] * freq
    cos, sin = jnp.cos(ang), jnp.sin(ang)
    s = positions.shape[0]
    z = lambda n: jnp.zeros((s, n), F32)
    c_t = jnp.concatenate([jnp.ones((s, 64), F32), cos, cos, z(32)], axis=1)
    sa_t = jnp.concatenate([z(64), -sin, z(16), z(32)], axis=1)
    sb_t = jnp.concatenate([z(64), z(16), sin, z(32)], axis=1)
    return c_t, sa_t, sb_t


def _local_grads(x, p, positions, tgt, gains, wfull):
    win_ext, wuq_ext, wk_ext, wv = _extend_weights(wfull)
    wout, wple, wpg = wfull["w_out"], wfull["w_ple"], wfull["w_ple_gate"]
    tabs = _rope_tables(positions)
    g = gains
    sbq, sbk, sbv, sbg, mlag, cq, ckv, qc, kc, mv = _pre_fwd(x, tabs, g["norm_pre_g"], win_ext, g["q_norm_g"], wuq_ext,
                                                             g["kv_norm_g"], wk_ext, wv)
    sbo = _sb_fwd(sbq, sbk, sbv)
    mlao, lse = _mla_fwd(qc, kc, mv.T)
    dsbo, dmlao, delta, dsbg, dmlag, dxres, dwout, dwpg, dwple, vec_c = _post(
        x, p, tgt, sbo, mlao, sbg, mlag, g["sb_out_norm_g"], g["mla_out_norm_g"], wout, g["norm_post_g"], wple,
        g["ple_norm_g"], wpg, g["b_ple_gate"])
    dsbq, dsbk, dsbv = _sb_bwd(sbq, sbk, sbv, dsbo)
    dqc, dkc, dmv = _mla_bwd(qc, kc, kc.T, mv, dmlao, lse, delta)
    gx, dwin_ext, dwuq_ext, dwk_ext, dwv, vec_d = _pre_bwd(
        x, dxres, dsbq, dsbk, dsbv, dsbg, dmlag, dqc, dkc, dmv, cq, ckv, tabs, g["norm_pre_g"], win_ext, g["q_norm_g"],
        wuq_ext, g["kv_norm_g"], wk_ext, wv)
    dwin, dwuq, dwukv = _contract_grads(dwin_ext, dwuq_ext, dwk_ext, dwv)
    grads = {"w_in": dwin, "w_uq": dwuq, "w_ukv": dwukv, "w_out": dwout, "w_ple": dwple, "w_ple_gate": dwpg}
    return gx, grads, jnp.concatenate([vec_c, vec_d], axis=0)


_VEC_LAYOUT = (("norm_post_g", 0, 0, 1024), ("ple_norm_g", 1, 0, 1024), ("b_ple_gate", 2, 0, 1024), ("sb_out_norm_g", 3, 0, 512),
               ("mla_out_norm_g", 3, 512, 512), ("norm_pre_g", 8, 0, 1024), ("q_norm_g", 9, 0, 256), ("kv_norm_g", 9, 256, 128))
_LOSS_ROW = 4
_WEIGHT_ORDER = ("norm_pre_g", "w_in", "q_norm_g", "w_uq", "kv_norm_g", "w_ukv", "sb_out_norm_g", "mla_out_norm_g", "w_out",
                 "norm_post_g", "w_ple", "ple_norm_g", "w_ple_gate", "b_ple_gate")


def kernel(x, p, positions, norm_pre_g, w_in, q_norm_g, w_uq, kv_norm_g, w_ukv, sb_out_norm_g, mla_out_norm_g, w_out, norm_post_g, w_ple, ple_norm_g, w_ple_gate, b_ple_gate, loss_target, m_norm_pre_g, m_w_in, m_q_norm_g, m_w_uq, m_kv_norm_g, m_w_ukv, m_sb_out_norm_g, m_mla_out_norm_g, m_w_out, m_norm_post_g, m_w_ple, m_ple_norm_g, m_w_ple_gate, m_b_ple_gate, v_norm_pre_g, v_w_in, v_q_norm_g, v_w_uq, v_kv_norm_g, v_w_ukv, v_sb_out_norm_g, v_mla_out_norm_g, v_w_out, v_norm_post_g, v_w_ple, v_ple_norm_g, v_w_ple_gate, v_b_ple_gate):
    w = {"norm_pre_g": norm_pre_g, "w_in": w_in[0], "q_norm_g": q_norm_g, "w_uq": w_uq[0], "kv_norm_g": kv_norm_g, "w_ukv": w_ukv[0],
         "sb_out_norm_g": sb_out_norm_g, "mla_out_norm_g": mla_out_norm_g, "w_out": w_out[0], "norm_post_g": norm_post_g,
         "w_ple": w_ple[0], "ple_norm_g": ple_norm_g, "w_ple_gate": w_ple_gate[0], "b_ple_gate": b_ple_gate}
    m = {"norm_pre_g": m_norm_pre_g, "w_in": m_w_in[0], "q_norm_g": m_q_norm_g, "w_uq": m_w_uq[0], "kv_norm_g": m_kv_norm_g,
         "w_ukv": m_w_ukv[0], "sb_out_norm_g": m_sb_out_norm_g, "mla_out_norm_g": m_mla_out_norm_g, "w_out": m_w_out[0],
         "norm_post_g": m_norm_post_g, "w_ple": m_w_ple[0], "ple_norm_g": m_ple_norm_g, "w_ple_gate": m_w_ple_gate[0],
         "b_ple_gate": m_b_ple_gate}
    v = {"norm_pre_g": v_norm_pre_g, "w_in": v_w_in[0], "q_norm_g": v_q_norm_g, "w_uq": v_w_uq[0], "kv_norm_g": v_kv_norm_g,
         "w_ukv": v_w_ukv[0], "sb_out_norm_g": v_sb_out_norm_g, "mla_out_norm_g": v_mla_out_norm_g, "w_out": v_w_out[0],
         "norm_post_g": v_norm_post_g, "w_ple": v_w_ple[0], "ple_norm_g": v_ple_norm_g, "w_ple_gate": v_w_ple_gate[0],
         "b_ple_gate": v_b_ple_gate}
    gathered = _allgather_weights([w[n] for n in _BIG])
    wfull = _join_shards(dict(zip(_BIG, gathered)))

    gx, grads, vec = _local_grads(x[0], p[0, 0], positions[0], loss_target[0], w, wfull)

    gsh = _split_shards(grads)
    *gred, vsum = _reduce_scatter_grads([gsh[n] for n in _BIG], vec)
    loss = vsum[_LOSS_ROW, 0]

    g, delta, new_m, new_v = _adamw_small(vsum, w, m, v)
    for n, gn in zip(_BIG, gred):
        g[n] = gn
        delta[n], new_m[n], new_v[n] = _adamw(w[n], gn, m[n], v[n])

    lead = lambda n, a: a[None] if n in _BIG else a
    return (loss, gx[None],
            *[lead(n, g[n]) for n in _WEIGHT_ORDER], *[lead(n, delta[n]) for n in _WEIGHT_ORDER],
            *[lead(n, new_m[n]) for n in _WEIGHT_ORDER], *[lead(n, new_v[n]) for n in _WEIGHT_ORDER])
```

```python
import numpy as np
import jax
import jax.numpy as jnp
from jax import lax
from jax.experimental import pallas as pl
from jax.experimental.pallas import tpu as pltpu

F32 = jnp.float32
BF16 = jnp.bfloat16
MESH = pl.DeviceIdType.MESH

D_MODEL = 1024
HEAD_DIM = 64
D_SB = 512
D_MLA = 512
Q_LORA = 256
KV_LORA = 128
QK_NOPE = 64
QK_ROPE = 32
PLE_DIM = 256
D_IN = 2976
D_EXT = 3072
ROPE_THETA = 10000.0
EPS = 1e-6
N_SHARD = 4

ADAM_LR = 0.001
ADAM_B1 = 0.9
ADAM_B2 = 0.999
ADAM_EPS = 1e-08
ADAM_WD = 0.01
ADAM_STEP = 10

LANES = 128
BK = 128
WQ = 256
SB_CUTOFF = 120.0
TM = 256
VEC_ROWS = 16
VMEM_DENSE = 52 * 1024 * 1024
VMEM_ATTN = 40 * 1024 * 1024


def _mm(a, b):
    return jnp.dot(a, b, preferred_element_type=F32)


def _mm_nt(a, b):
    return lax.dot_general(a, b, (((1,), (1,)), ((), ())), preferred_element_type=F32)


def _mm_tn(a, b):
    return lax.dot_general(a, b, (((0,), (0,)), ((), ())), preferred_element_type=F32)


def _seg(a, bd2):
    return _mm(_split2(a), bd2)


def _const(mask):
    return jnp.asarray(np.asarray(mask, np.float32), dtype=BF16)


def _blockdiag2(n, seg):
    r = (np.arange(2 * n)[:, None] % n) // seg
    c = np.arange(n)[None, :] // seg
    return _const(r == c)


def _sigmoid(a):
    return 1.0 / (1.0 + jnp.exp(-a))


def _rowmean(a):
    return jnp.mean(a, axis=-1, keepdims=True)


def _colsum(a):
    return jnp.sum(a, axis=0, keepdims=True)


def _rope_fwd(a, c, sa, sb):
    w = a.shape[-1]
    return a * c + pltpu.roll(a, w - 16, 1) * sa + pltpu.roll(a, 16, 1) * sb


def _rope_bwd(g, c, sa, sb):
    w = g.shape[-1]
    return g * c + pltpu.roll(g * sa, 16, 1) + pltpu.roll(g * sb, w - 16, 1)


def _full(shape):
    return pl.BlockSpec(shape, lambda *_: (0,) * len(shape))


def _full2(shape):
    return pl.BlockSpec(shape, lambda p, i: (0, 0))


def _rows(width, tm=TM):
    return pl.BlockSpec((tm, width), lambda i: (i, 0))


def _pre_fwd(x, tabs, gpre, win, gq, wuq, gkv, wk, wv):
    s = x.shape[0]
    c_t, sa_t, sb_t = tabs

    def body(x_ref, c_ref, sa_ref, sb_ref, gpre_ref, win_ref, gq_ref, wuq_ref, gkv_ref, wk_ref, wv_ref,
             sbq_ref, sbk_ref, sbv_ref, sbg_ref, mlag_ref, cq_ref, ckv_ref, qc_ref, kc_ref, mv_ref):
        xv = x_ref[...]
        r1 = lax.rsqrt(_rowmean(xv * xv) + EPS)
        h = (xv * r1 * gpre_ref[...]).astype(BF16)
        proj = _mm(h, win_ref[...])
        sbq_ref[...] = proj[:, 0:512].astype(BF16)
        sbk_ref[...] = proj[:, 512:1024].astype(BF16)
        sbv_ref[...] = proj[:, 1024:1536].astype(BF16)
        sbg_ref[...] = proj[:, 1536:2048]
        cq = proj[:, 2048:2304]
        ckv = proj[:, 2304:2432]
        kr = proj[:, 2432:2560]
        mlag_ref[...] = proj[:, 2560:3072]
        cq_ref[...] = cq
        ckv_ref[...] = ckv
        c1, sa1, sb1 = c_ref[...], sa_ref[...], sb_ref[...]
        c8, sa8, sb8 = jnp.tile(c1, (1, 8)), jnp.tile(sa1, (1, 8)), jnp.tile(sb1, (1, 8))
        cqn = (cq * lax.rsqrt(_rowmean(cq * cq) + EPS) * gq_ref[...]).astype(BF16)
        qe = _mm(cqn, wuq_ref[...])
        qc_ref[...] = _rope_fwd(qe, c8, sa8, sb8).astype(BF16)
        ckvn = (ckv * lax.rsqrt(_rowmean(ckv * ckv) + EPS) * gkv_ref[...]).astype(BF16)
        ke = _mm(ckvn, wk_ref[...])
        krr = _rope_fwd(kr, c1, sa1, sb1)
        kc_ref[...] = (ke + jnp.tile(krr, (1, 8))).astype(BF16)
        mv_ref[...] = _mm(ckvn, wv_ref[...]).astype(BF16)

    out_shape = (
        jax.ShapeDtypeStruct((s, 512), BF16), jax.ShapeDtypeStruct((s, 512), BF16), jax.ShapeDtypeStruct((s, 512), BF16),
        jax.ShapeDtypeStruct((s, 512), F32), jax.ShapeDtypeStruct((s, 512), F32),
        jax.ShapeDtypeStruct((s, Q_LORA), F32), jax.ShapeDtypeStruct((s, KV_LORA), F32),
        jax.ShapeDtypeStruct((s, 1024), BF16), jax.ShapeDtypeStruct((s, 1024), BF16), jax.ShapeDtypeStruct((s, 512), BF16),
    )
    return pl.pallas_call(
        body, name="pre_fwd", grid=(s // TM,), out_shape=out_shape,
        in_specs=[_rows(D_MODEL), _rows(LANES), _rows(LANES), _rows(LANES), _full((1, D_MODEL)), _full((D_MODEL, D_EXT)),
                  _full((1, Q_LORA)), _full((Q_LORA, 1024)), _full((1, KV_LORA)), _full((KV_LORA, 1024)), _full((KV_LORA, 512))],
        out_specs=(_rows(512), _rows(512), _rows(512), _rows(512), _rows(512), _rows(Q_LORA), _rows(KV_LORA),
                   _rows(1024), _rows(1024), _rows(512)),
        compiler_params=pltpu.CompilerParams(vmem_limit_bytes=VMEM_DENSE),
    )(x, c_t, sa_t, sb_t, gpre, win, gq, wuq, gkv, wk, wv)


def _softplus(z):
    neg_abs = lax.bitcast_convert_type(lax.bitcast_convert_type(z, jnp.uint32) | jnp.uint32(0x80000000), F32)
    return jnp.maximum(z, 0.0) + jnp.log(1.0 + jnp.exp(neg_abs))


def _pair_tri(kind):
    r, c = np.arange(512)[:, None], np.arange(256)[None, :]
    same = ((r // BK) % 2) == (c // BK)
    rk, ck = r % BK, c % BK
    m = {"ge": rk >= ck, "lt": rk < ck, "le": rk <= ck}[kind]
    return _const(same & m)


def _split2(a):
    hi = a.astype(BF16)
    lo = (a - hi.astype(F32)).astype(BF16)
    return jnp.concatenate([hi, lo], axis=1)


def _pair_stack(b, lane):
    zero = jnp.zeros_like(b)
    return jnp.concatenate([jnp.where(lane < 64, b, zero), jnp.where(lane >= 64, b, zero)], axis=0)


def _sb_fwd(q, k, v):
    s = q.shape[0]

    def body(q_ref, k_ref, v_ref, uge_ref, o_ref):
        i = pl.program_id(1)
        lane = lax.broadcasted_iota(jnp.int32, (1, LANES), 1)
        row = lax.broadcasted_iota(jnp.int32, (WQ, 2 * BK), 0) + i * WQ
        col = lax.broadcasted_iota(jnp.int32, (WQ, 2 * BK), 1) % BK
        qs = q_ref[...] * (HEAD_DIM ** -0.5)

        def scores(j):
            ks = pl.multiple_of(j * BK, BK)
            return _mm_nt(qs, _pair_stack(k_ref[pl.ds(ks, BK), :], lane))

        def decay(z, j, masked):
            sp = _softplus(z)
            if masked:
                sp = jnp.where((col + j * BK) < row, sp, 0.0)
            return _mm(_split2(sp), uge_ref[...])

        def weights(z, cum, j, masked):
            w = jnp.exp(z - cum)
            if masked:
                w = jnp.where((col + j * BK) < row, w, 0.0)
            ks = pl.multiple_of(j * BK, BK)
            return _mm(w.astype(BF16), _pair_stack(v_ref[pl.ds(ks, BK), :], lane))

        def fold(carry, pv, cum):
            acc, run = carry
            return acc + jnp.exp(-run) * pv, run + _heads_narrow(cum, 0, lane)

        def pair(ja, jb, carry, masked):
            za, zb = scores(ja), scores(jb)
            ca = decay(za, ja, masked)
            cb = decay(zb, jb, masked)
            pa = weights(za, ca, ja, masked)
            pb = weights(zb, cb, jb, masked)
            return fold(fold(carry, pa, ca), pb, cb)

        assert WQ == 2 * BK
        acc, run = pair(2 * i + 1, 2 * i, (jnp.zeros((WQ, LANES), F32), jnp.zeros((WQ, LANES), F32)), True)

        def more(c):
            return (c[0] < i) & (c[1] > 0)

        def step(c):
            jj, _, acc, run = c
            acc, run = pair(2 * i - 1 - 2 * jj, 2 * i - 2 - 2 * jj, (acc, run), False)
            return jj + 1, (jnp.min(run) < SB_CUTOFF).astype(jnp.int32), acc, run

        _, _, acc, _ = lax.while_loop(more, step, (jnp.int32(0), (jnp.min(run) < SB_CUTOFF).astype(jnp.int32), acc, run))
        o_ref[...] = acc

    qspec = pl.BlockSpec((WQ, LANES), lambda p, i: (i, p))
    kspec = pl.BlockSpec((s, LANES), lambda p, i: (0, p))
    return pl.pallas_call(
        body, name="sb_fwd", grid=(4, s // WQ),
        out_shape=jax.ShapeDtypeStruct((s, 512), F32),
        in_specs=[qspec, kspec, kspec, _full2((4 * BK, 2 * BK))], out_specs=qspec,
        compiler_params=pltpu.CompilerParams(vmem_limit_bytes=VMEM_ATTN),
    )(q, k, v, _pair_tri("ge"))


def _pair_tri1(kind):
    r, c = np.arange(2 * BK)[:, None], np.arange(2 * BK)[None, :]
    rk, ck = r % BK, c % BK
    m = {"ge": rk >= ck, "lt": rk < ck, "le": rk <= ck}[kind]
    return _const(((r // BK) == (c // BK)) & m)


def _heads_wide(a):
    m = a.shape[0]
    return jnp.concatenate([jnp.broadcast_to(a[:, 0:1], (m, BK)), jnp.broadcast_to(a[:, 64:65], (m, BK))], axis=1)


def _heads_narrow(a, col, lane):
    return jnp.where(lane < 64, a[:, col:col + 1], a[:, BK + col:BK + col + 1])


def _pair_rowsum():
    r, c = np.arange(512)[:, None], np.arange(LANES)[None, :]
    return _const(((r // BK) % 2) == (c // 64))


def _sb_bwd(q, k, v, do):
    s = q.shape[0]
    assert WQ == 2 * BK

    def body(q_ref, k_ref, v_ref, do_ref, uge_ref, ule_ref, rsum_ref, dq_ref, dk_ref, dv_ref, later_ref):
        i = pl.program_id(1)

        @pl.when(i == 0)
        def _():
            dk_ref[...] = jnp.zeros_like(dk_ref)
            dv_ref[...] = jnp.zeros_like(dv_ref)

        lane = lax.broadcasted_iota(jnp.int32, (1, LANES), 1)
        row = lax.broadcasted_iota(jnp.int32, (WQ, 2 * BK), 0) + i * WQ
        col = lax.broadcasted_iota(jnp.int32, (WQ, 2 * BK), 1) % BK
        qs = q_ref[...] * (HEAD_DIM ** -0.5)
        dof = do_ref[...].astype(F32)

        def scan(ja, jb, run, masked):
            za = _mm_nt(qs, _pair_stack(k_ref[pl.ds(pl.multiple_of(ja * BK, BK), BK), :], lane))
            zb = _mm_nt(qs, _pair_stack(k_ref[pl.ds(pl.multiple_of(jb * BK, BK), BK), :], lane))
            spa, spb = _softplus(za), _softplus(zb)
            if masked:
                spa = jnp.where((col + ja * BK) < row, spa, 0.0)
                spb = jnp.where((col + jb * BK) < row, spb, 0.0)
            rsa, rsb = _mm(_split2(spa), rsum_ref[...]), _mm(_split2(spb), rsum_ref[...])
            later_ref[ja] = run
            later_ref[jb] = run + rsa
            return run + rsa + rsb

        run = scan(2 * i + 1, 2 * i, jnp.zeros((WQ, LANES), F32), True)

        def more(c):
            return (c[0] < i) & (c[1] > 0)

        def step(c):
            run = scan(2 * i - 1 - 2 * c[0], 2 * i - 2 - 2 * c[0], c[2], False)
            return c[0] + 1, (jnp.min(run) < SB_CUTOFF).astype(jnp.int32), run

        npairs, _, _ = lax.while_loop(more, step, (jnp.int32(0), (jnp.min(run) < SB_CUTOFF).astype(jnp.int32), run))

        def keys(j):
            ks = pl.multiple_of(j * BK, BK)
            return ks, _pair_stack(k_ref[pl.ds(ks, BK), :], lane)

        def gates(z, j, masked):
            neg_abs = lax.bitcast_convert_type(lax.bitcast_convert_type(z, jnp.uint32) | jnp.uint32(0x80000000), F32)
            u = jnp.exp(neg_abs)
            opu = 1.0 + u
            sp = jnp.maximum(z, 0.0) + jnp.log(opu)
            if masked:
                sp = jnp.where((col + j * BK) < row, sp, 0.0)
            return jnp.where(z >= 0.0, 1.0, u) / opu, _mm(_split2(sp), uge_ref[...])

        def weights(z, cum, ks, j, masked):
            wl = jnp.exp(z - cum)
            if masked:
                wl = jnp.where((col + j * BK) < row, wl, 0.0)
            dfo = (jnp.exp(-later_ref[j]) * dof).astype(BF16)
            e = _mm_nt(dfo, _pair_stack(v_ref[pl.ds(ks, BK), :], lane)) * wl
            return wl.astype(BF16), dfo, e

        def scores_grad(e, sig, erun, j, masked):
            ecum = _mm(e.astype(BF16), ule_ref[...]) + _heads_wide(erun)
            dz = e - sig * ecum
            if masked:
                dz = jnp.where((col + j * BK) < row, dz, 0.0)
            return dz.astype(BF16), _heads_narrow(ecum, BK - 1, lane)

        def scatter(ks, dzb, wlb, dfo):
            rk = _mm_tn(dzb, qs)
            dk_ref[pl.ds(ks, BK), :] += jnp.where(lane < 64, rk[0:BK], rk[BK:2 * BK])
            rv = _mm_tn(wlb, dfo)
            dv_ref[pl.ds(ks, BK), :] += jnp.where(lane < 64, rv[0:BK], rv[BK:2 * BK])

        def pair(ja, carry, masked):
            dq, erun = carry
            jb = ja + 1
            ksa, kbda = keys(ja)
            ksb, kbdb = keys(jb)
            za, zb = _mm_nt(qs, kbda), _mm_nt(qs, kbdb)
            siga, cuma = gates(za, ja, masked)
            sigb, cumb = gates(zb, jb, masked)
            wla, dfoa, ea = weights(za, cuma, ksa, ja, masked)
            wlb, dfob, eb = weights(zb, cumb, ksb, jb, masked)
            dza, eruna = scores_grad(ea, siga, erun, ja, masked)
            dzb, erunb = scores_grad(eb, sigb, eruna, jb, masked)
            dq = dq + _mm(jnp.concatenate([dza, dzb], axis=1), jnp.concatenate([kbda, kbdb], axis=0))
            scatter(ksa, dza, wla, dfoa)
            scatter(ksb, dzb, wlb, dfob)
            return dq, erunb

        zero = jnp.zeros((WQ, LANES), F32)
        first = 2 * (i - npairs)
        carry = lax.fori_loop(0, npairs, lambda t, c: pair(first + 2 * t, c, False), (zero, zero))
        dq, _ = pair(2 * i, carry, True)
        dq_ref[...] = (dq * (HEAD_DIM ** -0.5)).astype(BF16)

    qspec = pl.BlockSpec((WQ, LANES), lambda p, i: (i, p))
    kspec = pl.BlockSpec((s, LANES), lambda p, i: (0, p))
    return pl.pallas_call(
        body, name="sb_bwd", grid=(4, s // WQ),
        out_shape=(jax.ShapeDtypeStruct((s, 512), BF16), jax.ShapeDtypeStruct((s, 512), F32),
                   jax.ShapeDtypeStruct((s, 512), F32)),
        in_specs=[qspec, kspec, kspec, qspec, _full2((4 * BK, 2 * BK)), _full2((2 * BK, 2 * BK)), _full2((4 * BK, LANES))],
        out_specs=(qspec, kspec, kspec),
        scratch_shapes=[pltpu.VMEM((s // BK, WQ, LANES), F32)],
        compiler_params=pltpu.CompilerParams(vmem_limit_bytes=VMEM_ATTN),
    )(q, k, v, do, _pair_tri("ge"), _pair_tri1("le"), _pair_rowsum())


MLA_SCALE = (QK_NOPE + QK_ROPE) ** -0.5
LOG2E = 1.4426950408889634


def _mla_keys(kb):
    zero = jnp.zeros((BK, LANES), kb.dtype)
    return jnp.concatenate([jnp.concatenate([kb[:, 0:LANES], zero], axis=1),
                            jnp.concatenate([zero, kb[:, LANES:2 * LANES]], axis=1)], axis=0)


def _mla_fwd(qc, kc, vt):
    s = qc.shape[0]
    rows_l = 16

    def body(q_ref, k_ref, vt_ref, o_ref, l_ref, p_scr, ot_scr, st_scr):
        i = pl.program_id(1)
        keyc = lax.broadcasted_iota(jnp.int32, (BK, WQ), 0)
        qryc = (lax.broadcasted_iota(jnp.int32, (BK, WQ), 1) + i * WQ) // 64
        row = lax.broadcasted_iota(jnp.int32, (LANES, 1), 0)
        qw = q_ref[...]
        orow = lax.broadcasted_iota(jnp.int32, (rows_l, 2 * BK), 0)
        ocol = lax.broadcasted_iota(jnp.int32, (rows_l, 2 * BK), 1)
        ones = jnp.where(((orow == 0) & (ocol < BK)) | ((orow == 1) & (ocol >= BK)), 1.0, 0.0).astype(BF16)

        def scores(j):
            ks = pl.multiple_of(j * BK, BK)
            return _mm_nt(_mla_keys(k_ref[pl.ds(ks, BK), :]), qw)

        def values_t(j):
            vtb = vt_ref[:, pl.ds(pl.multiple_of(j * BK, BK), BK)]
            zero = jnp.zeros_like(vtb)
            top = jnp.concatenate([jnp.where(row < 64, vtb, zero), jnp.where(row >= 64, vtb, zero)], axis=1)
            return jnp.concatenate([top, ones], axis=0)

        def softmax(ja, za, zb, masked):
            c = MLA_SCALE * LOG2E
            parts = [za[0:BK] * c, za[BK:2 * BK] * c, zb[0:BK] * c, zb[BK:2 * BK] * c]
            if masked:
                va = ((keyc + ja * BK) // 64) <= qryc
                vb = ((keyc + (ja + 1) * BK) // 64) <= qryc
                parts = [jnp.where(va, parts[0], -1e30), jnp.where(va, parts[1], -1e30),
                         jnp.where(vb, parts[2], -1e30), jnp.where(vb, parts[3], -1e30)]
            m0, m1 = st_scr[0:1, :], st_scr[1:2, :]
            n0 = jnp.maximum(m0, jnp.max(jnp.maximum(parts[0], parts[2]), axis=0, keepdims=True))
            n1 = jnp.maximum(m1, jnp.max(jnp.maximum(parts[1], parts[3]), axis=0, keepdims=True))
            st_scr[2:3, :] = jnp.exp2(m0 - n0)
            st_scr[3:4, :] = jnp.exp2(m1 - n1)
            st_scr[0:1, :] = n0
            st_scr[1:2, :] = n1
            p_scr[...] = jnp.concatenate([jnp.exp2(parts[0] - n0), jnp.exp2(parts[1] - n1),
                                          jnp.exp2(parts[2] - n0), jnp.exp2(parts[3] - n1)], axis=0).astype(BF16)

        def accumulate(ja):
            pv = _mm(jnp.concatenate([values_t(ja), values_t(ja + 1)], axis=1), p_scr[...])
            a = jnp.where(row < 64, st_scr[2:3, :], st_scr[3:4, :])
            ot_scr[0:LANES, :] = a * ot_scr[0:LANES, :] + pv[0:LANES]
            ot_scr[LANES:LANES + 8, :] = st_scr[2:10, :] * ot_scr[LANES:LANES + 8, :] + pv[LANES:LANES + 8]

        def step(n, masked):
            za, zb = scores(2 * n), scores(2 * n + 1)
            accumulate(2 * n - 2)
            softmax(2 * n, za, zb, masked)

        def first(masked):
            softmax(0, scores(0), scores(1), masked)

        st_scr[...] = jnp.concatenate([jnp.full((2, WQ), -1e30, F32), jnp.ones((14, WQ), F32)], axis=0)
        ot_scr[...] = jnp.zeros_like(ot_scr)

        @pl.when(i == 0)
        def _():
            first(True)

        @pl.when(i > 0)
        def _():
            first(False)
            lax.fori_loop(1, i, lambda n, c: (step(n, False), c)[1], 0)
            step(i, True)

        accumulate(2 * i)
        l0, l1 = ot_scr[LANES:LANES + 1, :], ot_scr[LANES + 1:LANES + 2, :]
        o_ref[...] = (ot_scr[0:LANES, :] / jnp.where(row < 64, l0, l1)).T
        l_ref[...] = jnp.where(row < 64, st_scr[0:1, :] + jnp.log2(l0), st_scr[1:2, :] + jnp.log2(l1)).T

    qspec = pl.BlockSpec((WQ, 2 * LANES), lambda p, i: (i, p))
    kspec = pl.BlockSpec((s, 2 * LANES), lambda p, i: (0, p))
    vtspec = pl.BlockSpec((LANES, s), lambda p, i: (p, 0))
    ospec = pl.BlockSpec((WQ, LANES), lambda p, i: (i, p))
    return pl.pallas_call(
        body, name="mla_fwd", grid=(4, s // WQ),
        out_shape=(jax.ShapeDtypeStruct((s, 512), F32), jax.ShapeDtypeStruct((s, 512), F32)),
        in_specs=[qspec, kspec, vtspec], out_specs=(ospec, ospec),
        scratch_shapes=[pltpu.VMEM((4 * BK, WQ), BF16), pltpu.VMEM((LANES + 8, WQ), F32), pltpu.VMEM((16, WQ), F32)],
        compiler_params=pltpu.CompilerParams(vmem_limit_bytes=VMEM_ATTN),
    )(qc, kc, vt)


def _mla_bwd(qc, kc, kct, v, do, lse, delta):
    s = qc.shape[0]

    def body(q_ref, k_ref, kt_ref, v_ref, do_ref, l_ref, d_ref, dq_ref, dk_ref, dv_ref, dqt_scr, p_scr, dz_scr):
        i = pl.program_id(1)

        @pl.when(i == 0)
        def _():
            dk_ref[...] = jnp.zeros_like(dk_ref)
            dv_ref[...] = jnp.zeros_like(dv_ref)

        lane = lax.broadcasted_iota(jnp.int32, (1, LANES), 1)
        keyc = lax.broadcasted_iota(jnp.int32, (BK, WQ), 0)
        qryc = (lax.broadcasted_iota(jnp.int32, (BK, WQ), 1) + i * WQ) // 64
        qw = q_ref[...]
        dob = do_ref[...]
        dost = (dob.astype(F32) * MLA_SCALE).T.astype(BF16)
        lt = l_ref[...].T
        dt = (d_ref[...] * MLA_SCALE).T
        lse0, lse1 = lt[0:1], lt[64:65]
        dl0, dl1 = dt[0:1], dt[64:65]
        dqt_scr[...] = jnp.zeros_like(dqt_scr)

        def products(j):
            ks = pl.multiple_of(j * BK, BK)
            return (_mm_nt(_mla_keys(k_ref[pl.ds(ks, BK), :]), qw), _mm(_pair_stack(v_ref[pl.ds(ks, BK), :], lane), dost))

        def grads(j, slot, zt, dwt, masked):
            zt = zt * (MLA_SCALE * LOG2E)
            p0 = jnp.exp2(zt[0:BK] - lse0)
            p1 = jnp.exp2(zt[BK:2 * BK] - lse1)
            if masked:
                valid = ((keyc + j * BK) // 64) <= qryc
                p0, p1 = jnp.where(valid, p0, 0.0), jnp.where(valid, p1, 0.0)
            p_scr[slot] = jnp.concatenate([p0, p1], axis=0).astype(BF16)
            dz_scr[slot] = jnp.concatenate([p0 * (dwt[0:BK] - dl0), p1 * (dwt[BK:2 * BK] - dl1)], axis=0).astype(BF16)

        def keys_t(ks):
            ktb = kt_ref[:, pl.ds(ks, BK)]
            zero = jnp.zeros((LANES, BK), ktb.dtype)
            return jnp.concatenate([jnp.concatenate([ktb[0:LANES], zero], axis=1),
                                    jnp.concatenate([zero, ktb[LANES:2 * LANES]], axis=1)], axis=0)

        def scatter(ja):
            ksa, ksb = pl.multiple_of(ja * BK, BK), pl.multiple_of((ja + 1) * BK, BK)
            dqt_scr[...] += _mm(jnp.concatenate([keys_t(ksa), keys_t(ksb)], axis=1),
                                jnp.concatenate([dz_scr[0], dz_scr[1]], axis=0))
            for slot, ks in ((0, ksa), (1, ksb)):
                rk = _mm(dz_scr[slot], qw)
                dk_ref[pl.ds(ks, BK), :] += jnp.concatenate([rk[0:BK, 0:LANES], rk[BK:2 * BK, LANES:2 * LANES]], axis=1)
                rv = _mm(p_scr[slot], dob)
                dv_ref[pl.ds(ks, BK), :] += jnp.where(lane < 64, rv[0:BK], rv[BK:2 * BK])

        def step(n, masked):
            za, wa = products(2 * n)
            zb, wb = products(2 * n + 1)
            scatter(2 * n - 2)
            grads(2 * n, 0, za, wa, masked)
            grads(2 * n + 1, 1, zb, wb, masked)

        def first(masked):
            za, wa = products(0)
            zb, wb = products(1)
            grads(0, 0, za, wa, masked)
            grads(1, 1, zb, wb, masked)

        @pl.when(i == 0)
        def _():
            first(True)

        @pl.when(i > 0)
        def _():
            first(False)
            lax.fori_loop(1, i, lambda n, c: (step(n, False), c)[1], 0)
            step(i, True)

        scatter(2 * i)
        dq_ref[...] = dqt_scr[...].T

    qspec = pl.BlockSpec((WQ, 2 * LANES), lambda p, i: (i, p))
    kspec = pl.BlockSpec((s, 2 * LANES), lambda p, i: (0, p))
    ktspec = pl.BlockSpec((2 * LANES, s), lambda p, i: (p, 0))
    vspec = pl.BlockSpec((s, LANES), lambda p, i: (0, p))
    ospec = pl.BlockSpec((WQ, LANES), lambda p, i: (i, p))
    return pl.pallas_call(
        body, name="mla_bwd", grid=(4, s // WQ),
        out_shape=(jax.ShapeDtypeStruct((s, 1024), F32), jax.ShapeDtypeStruct((s, 1024), F32),
                   jax.ShapeDtypeStruct((s, 512), F32)),
        in_specs=[qspec, kspec, ktspec, vspec, ospec, ospec, ospec], out_specs=(qspec, kspec, vspec),
        scratch_shapes=[pltpu.VMEM((2 * LANES, WQ), F32), pltpu.VMEM((2, 2 * BK, WQ), BF16), pltpu.VMEM((2, 2 * BK, WQ), BF16)],
        compiler_params=pltpu.CompilerParams(vmem_limit_bytes=VMEM_ATTN),
    )(qc, kc, kct, v, do, lse, delta)


def _post(x, p, tgt, sbo, mlao, sbg, mlag, gsb, gmla, wout, gpost, wple, gple, wpg, bpg):
    s = x.shape[0]

    def body(x_ref, p_ref, t_ref, sbo_ref, mlao_ref, sbg_ref, mlag_ref, gsb_ref, gmla_ref, wout_ref,
             gpost_ref, wple_ref, gple_ref, wpg_ref, bpg_ref, bd_ref,
             dsbo_ref, dmlao_ref, delta_ref, dsbg_ref, dmlag_ref, dxres_ref, dwout_ref, dwpg_ref, dwple_ref, vec_ref):
        i = pl.program_id(0)

        @pl.when(i == 0)
        def _():
            dwout_ref[...] = jnp.zeros_like(dwout_ref)
            dwpg_ref[...] = jnp.zeros_like(dwpg_ref)
            dwple_ref[...] = jnp.zeros_like(dwple_ref)
            vec_ref[...] = jnp.zeros_like(vec_ref)

        inv_hd = 1.0 / HEAD_DIM

        def head_fwd(o, g, gate):
            r = lax.rsqrt(_seg(o * o, bd_ref[...]) * inv_hd + EPS)
            hat = o * r
            n = hat * g
            sg = _sigmoid(gate)
            return hat, r, n, sg, n * (gate * sg)

        sbo, mlao, sbg_v, mlag_v = sbo_ref[...], mlao_ref[...], sbg_ref[...], mlag_ref[...]
        gsb_v, gmla_v = gsb_ref[...], gmla_ref[...]
        sb_hat, sb_r, sb_n, sb_sg, sb_y = head_fwd(sbo, gsb_v, sbg_v)
        ml_hat, ml_r, ml_n, ml_sg, ml_y = head_fwd(mlao, gmla_v, mlag_v)
        mix = jnp.concatenate([sb_y, ml_y], axis=1).astype(BF16)
        y = _mm(mix, wout_ref[...])
        ry = lax.rsqrt(_rowmean(y * y) + EPS)
        y_hat = y * ry
        gpost_v = gpost_ref[...]
        x1 = x_ref[...] + y_hat * gpost_v
        pb = p_ref[...].astype(BF16)
        pl_ = _mm(pb, wple_ref[...])
        rp = lax.rsqrt(_rowmean(pl_ * pl_) + EPS)
        pl_hat = pl_ * rp
        gple_v = gple_ref[...]
        ple = pl_hat * gple_v
        x1b = x1.astype(BF16)
        gate = _sigmoid(_mm(x1b, wpg_ref[...]) + bpg_ref[...])
        err = x1 + ple * gate - t_ref[...]
        loss = 0.5 * jnp.sum(_rowmean(err * err))
        dout = err * (1.0 / D_MODEL)

        du = dout * ple * gate * (1.0 - gate)
        dub = du.astype(BF16)
        dple = dout * gate
        dx1 = dout + _mm_nt(dub, wpg_ref[...])
        dwpg_ref[...] += _mm_tn(x1b, dub)
        dplh = dple * gple_v
        dpl = rp * (dplh - pl_hat * _rowmean(dplh * pl_hat))
        dwple_ref[...] += _mm_tn(pb, dpl.astype(BF16))
        dxres_ref[...] = dx1
        dyh = dx1 * gpost_v
        dy = ry * (dyh - y_hat * _rowmean(dyh * y_hat))
        dyb = dy.astype(BF16)
        dwout_ref[...] += _mm_tn(mix, dyb)
        dmix = _mm_nt(dyb, wout_ref[...])

        def head_bwd(dyv, hat, r, n, sg, g, gate):
            dn = dyv * (gate * sg)
            dgate = dyv * n * (sg * (1.0 + gate * (1.0 - sg)))
            dhat = dn * g
            do = r * (dhat - hat * (_seg(dhat * hat, bd_ref[...]) * inv_hd))
            return do, dgate, _colsum(dn * hat)

        dsbo, dsbg, dg_sb = head_bwd(dmix[:, 0:512], sb_hat, sb_r, sb_n, sb_sg, gsb_v, sbg_v)
        dmlao, dmlag, dg_ml = head_bwd(dmix[:, 512:1024], ml_hat, ml_r, ml_n, ml_sg, gmla_v, mlag_v)
        dsbo_ref[...] = dsbo.astype(BF16)
        dmlao_ref[...] = dmlao.astype(BF16)
        delta_ref[...] = _seg(dmlao * mlao, bd_ref[...])
        dsbg_ref[...] = dsbg.astype(BF16)
        dmlag_ref[...] = dmlag.astype(BF16)
        vec_ref[pl.ds(0, 1), :] += _colsum(dx1 * y_hat)
        vec_ref[pl.ds(1, 1), :] += _colsum(dple * pl_hat)
        vec_ref[pl.ds(2, 1), :] += _colsum(du)
        vec_ref[pl.ds(3, 1), :] += jnp.concatenate([dg_sb, dg_ml], axis=1)
        vec_ref[pl.ds(4, 1), :] += jnp.full((1, D_MODEL), loss, F32)

    out_shape = (
        jax.ShapeDtypeStruct((s, 512), BF16), jax.ShapeDtypeStruct((s, 512), BF16), jax.ShapeDtypeStruct((s, 512), F32),
        jax.ShapeDtypeStruct((s, 512), BF16), jax.ShapeDtypeStruct((s, 512), BF16), jax.ShapeDtypeStruct((s, D_MODEL), F32),
        jax.ShapeDtypeStruct((D_MODEL, D_MODEL), F32), jax.ShapeDtypeStruct((D_MODEL, D_MODEL), F32),
        jax.ShapeDtypeStruct((PLE_DIM, D_MODEL), F32), jax.ShapeDtypeStruct((8, D_MODEL), F32),
    )
    return pl.pallas_call(
        body, name="post_fwd_bwd", grid=(s // TM,), out_shape=out_shape,
        in_specs=[_rows(D_MODEL), _rows(PLE_DIM), _rows(D_MODEL), _rows(512), _rows(512), _rows(512), _rows(512),
                  _full((1, 512)), _full((1, 512)), _full((D_MODEL, D_MODEL)),
                  _full((1, D_MODEL)), _full((PLE_DIM, D_MODEL)), _full((1, D_MODEL)), _full((D_MODEL, D_MODEL)),
                  _full((1, D_MODEL)), _full((1024, 512))],
        out_specs=(_rows(512), _rows(512), _rows(512), _rows(512), _rows(512), _rows(D_MODEL),
                   _full((D_MODEL, D_MODEL)), _full((D_MODEL, D_MODEL)), _full((PLE_DIM, D_MODEL)), _full((8, D_MODEL))),
        compiler_params=pltpu.CompilerParams(vmem_limit_bytes=VMEM_DENSE),
    )(x, p, tgt, sbo, mlao, sbg, mlag, gsb, gmla, wout, gpost, wple, gple, wpg, bpg, _blockdiag2(512, HEAD_DIM))


def _pre_bwd(x, dxres, dsbq, dsbk, dsbv, dsbg, dmlag, dqc, dkc, dmv, cq, ckv, tabs, gpre, win, gq, wuq, gkv, wk, wv):
    s = x.shape[0]
    c_t, sa_t, sb_t = tabs

    def body(x_ref, dxres_ref, dsbq_ref, dsbk_ref, dsbv_ref, dsbg_ref, dmlag_ref, dqc_ref, dkc_ref, dmv_ref, cq_ref,
             ckv_ref, c_ref, sa_ref, sb_ref, gpre_ref, win_ref, gq_ref, wuq_ref, gkv_ref, wk_ref, wv_ref,
             gx_ref, dwin_ref, dwuq_ref, dwk_ref, dwv_ref, vec_ref, dwin_acc):
        i = pl.program_id(0)

        @pl.when(i == 0)
        def _():
            dwin_acc[...] = jnp.zeros_like(dwin_acc)
            dwuq_ref[...] = jnp.zeros_like(dwuq_ref)
            dwk_ref[...] = jnp.zeros_like(dwk_ref)
            dwv_ref[...] = jnp.zeros_like(dwv_ref)
            vec_ref[...] = jnp.zeros_like(vec_ref)

        lane = lax.broadcasted_iota(jnp.int32, (1, LANES), 1)
        c1, sa1, sb1 = c_ref[...], sa_ref[...], sb_ref[...]
        c8, sa8, sb8 = jnp.tile(c1, (1, 8)), jnp.tile(sa1, (1, 8)), jnp.tile(sb1, (1, 8))

        def norm_bwd(dn, hat, r, g):
            t = dn * g
            return r * (t - hat * _rowmean(t * hat)), _colsum(dn * hat)

        dqeb = _rope_bwd(dqc_ref[...], c8, sa8, sb8).astype(BF16)
        cq = cq_ref[...]
        rq = lax.rsqrt(_rowmean(cq * cq) + EPS)
        cq_hat = cq * rq
        gq_v = gq_ref[...]
        dwuq_ref[...] += _mm_tn((cq_hat * gq_v).astype(BF16), dqeb)
        dcq, dg_q = norm_bwd(_mm_nt(dqeb, wuq_ref[...]), cq_hat, rq, gq_v)

        dkc = dkc_ref[...]
        dkcb = dkc.astype(BF16)
        dmvb = dmv_ref[...].astype(BF16)
        ckv = ckv_ref[...]
        rkv = lax.rsqrt(_rowmean(ckv * ckv) + EPS)
        ckv_hat = ckv * rkv
        gkv_v = gkv_ref[...]
        ckvnb = (ckv_hat * gkv_v).astype(BF16)
        dwk_ref[...] += _mm_tn(ckvnb, dkcb)
        dwv_ref[...] += _mm_tn(ckvnb, dmvb)
        dckv, dg_kv = norm_bwd(_mm_nt(dkcb, wk_ref[...]) + _mm_nt(dmvb, wv_ref[...]), ckv_hat, rkv, gkv_v)

        dkr = dkc[:, 0:LANES]
        for hh in range(1, 8):
            dkr = dkr + dkc[:, LANES * hh:LANES * (hh + 1)]
        dkr = _rope_bwd(dkr, c1, sa1, sb1)
        dkr = jnp.where((lane >= 64) & (lane < 96), dkr, 0.0)

        dproj = jnp.concatenate([dsbq_ref[...], dsbk_ref[...].astype(BF16), dsbv_ref[...].astype(BF16), dsbg_ref[...],
                                 dcq.astype(BF16), dckv.astype(BF16), dkr.astype(BF16), dmlag_ref[...]], axis=1)
        xv = x_ref[...]
        r1 = lax.rsqrt(_rowmean(xv * xv) + EPS)
        x_hat = xv * r1
        gpre_v = gpre_ref[...]
        dwin_acc[...] += _mm_tn((x_hat * gpre_v).astype(BF16), dproj)
        dx, dg_pre = norm_bwd(_mm_nt(dproj, win_ref[...]), x_hat, r1, gpre_v)
        gx_ref[...] = dxres_ref[...] + dx
        vec_ref[pl.ds(0, 1), :] += dg_pre
        vec_ref[pl.ds(1, 1), :] += jnp.concatenate([dg_q, dg_kv, jnp.zeros((1, D_MODEL - Q_LORA - KV_LORA), F32)], axis=1)

        @pl.when(i == pl.num_programs(0) - 1)
        def _():
            pltpu.sync_copy(dwin_acc, dwin_ref)

    out_shape = (
        jax.ShapeDtypeStruct((s, D_MODEL), F32), jax.ShapeDtypeStruct((D_MODEL, D_EXT), F32),
        jax.ShapeDtypeStruct((Q_LORA, 1024), F32), jax.ShapeDtypeStruct((KV_LORA, 1024), F32),
        jax.ShapeDtypeStruct((KV_LORA, 512), F32), jax.ShapeDtypeStruct((8, D_MODEL), F32),
    )
    return pl.pallas_call(
        body, name="pre_bwd", grid=(s // TM,), out_shape=out_shape,
        in_specs=[_rows(D_MODEL), _rows(D_MODEL), _rows(512), _rows(512), _rows(512), _rows(512), _rows(512),
                  _rows(1024), _rows(1024), _rows(512), _rows(Q_LORA), _rows(KV_LORA), _rows(LANES), _rows(LANES),
                  _rows(LANES), _full((1, D_MODEL)), _full((D_MODEL, D_EXT)), _full((1, Q_LORA)), _full((Q_LORA, 1024)),
                  _full((1, KV_LORA)), _full((KV_LORA, 1024)), _full((KV_LORA, 512))],
        out_specs=(_rows(D_MODEL), pl.BlockSpec(memory_space=pl.ANY), _full((Q_LORA, 1024)), _full((KV_LORA, 1024)),
                   _full((KV_LORA, 512)), _full((8, D_MODEL))),
        scratch_shapes=[pltpu.VMEM((D_MODEL, D_EXT), F32)],
        compiler_params=pltpu.CompilerParams(vmem_limit_bytes=VMEM_DENSE),
    )(x, dxres, dsbq, dsbk, dsbv, dsbg, dmlag, dqc, dkc, dmv, cq, ckv, c_t, sa_t, sb_t, gpre, win, gq, wuq, gkv, wk, wv)


def _place():
    return lax.axis_index("x"), lax.axis_index("y"), lax.axis_index("c")


def _allgather_weights(shards):
    n = len(shards)

    def body(*refs):
        ins, outs, send_sems, recv_sems = refs[:n], refs[n:2 * n], refs[2 * n], refs[2 * n + 1]
        x, y, c = _place()
        me, sib = (x, y, c), (x, y, 1 - c)
        chips = [(1 - x, y), (x, 1 - y), (1 - x, 1 - y)]

        def half(t, chip, hc):
            rows = shards[t].shape[0] // 2
            return outs[t].at[2 * chip[0] + chip[1], pl.ds(pl.multiple_of(hc * rows, 16), rows), :]

        def copy(k, t, chip, hc, to):
            return pltpu.make_async_remote_copy(src_ref=half(t, chip, hc), dst_ref=half(t, chip, hc), send_sem=send_sems.at[k],
                                                recv_sem=recv_sems.at[k], device_id=to, device_id_type=MESH)

        first, passed = [], []
        for t in range(n):
            outs[t][2 * x + y] = ins[t][...].astype(BF16)
            for j, chip in enumerate(chips):
                cp = copy(6 * t + j, t, (x, y), c, (*chip, c))
                cp.start()
                first.append(cp)
        for t in range(n):
            for j, chip in enumerate(chips):
                copy(6 * t + j, t, chip, c, me).wait_recv()
                cp = copy(6 * t + 3 + j, t, chip, c, sib)
                cp.start()
                passed.append(cp)
        for t in range(n):
            for j, chip in enumerate(chips):
                copy(6 * t + 3 + j, t, chip, 1 - c, me).wait_recv()
        for cp in first + passed:
            cp.wait_send()

    return pl.pallas_call(
        body, name="allgather_weights",
        out_shape=tuple(jax.ShapeDtypeStruct((N_SHARD,) + a.shape, BF16) for a in shards),
        in_specs=[pl.BlockSpec(memory_space=pltpu.VMEM)] * n, out_specs=(pl.BlockSpec(memory_space=pltpu.VMEM),) * n,
        scratch_shapes=[pltpu.SemaphoreType.DMA((6 * n,)), pltpu.SemaphoreType.DMA((6 * n,))],
        compiler_params=pltpu.CompilerParams(vmem_limit_bytes=VMEM_ATTN),
    )(*shards)


def _reduce_scatter_grads(gsh, vec):
    n = len(gsh)
    halves = [a.shape[1] // 2 for a in gsh]

    def body(*refs):
        g_refs, vec_ref, f_refs, vsum_ref = refs[:n], refs[n], refs[n + 1:2 * n + 1], refs[2 * n + 1]
        scr = refs[2 * n + 2:]
        accs, sibs, sbufs, rbufs = scr[0:n], scr[n:2 * n], scr[2 * n:3 * n], scr[3 * n:4 * n]
        vrecv, local_sems, send_sems, recv_sems = scr[4 * n:4 * n + 4]
        x, y, c = _place()
        me, sib = (x, y, c), (x, y, 1 - c)
        mine = 2 * x + y
        chips = [(1 - x, y), (x, 1 - y), (1 - x, 1 - y)]

        def remote(k, src, dst, to):
            return pltpu.make_async_remote_copy(src_ref=src, dst_ref=dst, send_sem=send_sems.at[k], recv_sem=recv_sems.at[k],
                                                device_id=to, device_id_type=MESH)

        def half3(ref, t, hc):
            return ref.at[:, pl.ds(pl.multiple_of(hc * halves[t], 8), halves[t]), :]

        def half2(ref, t, hc):
            return ref.at[pl.ds(pl.multiple_of(hc * halves[t], 8), halves[t]), :]

        loads, sends = [], []
        for t in range(n):
            ld = pltpu.make_async_copy(half3(g_refs[t], t, c), accs[t], local_sems.at[t])
            ld.start()
            loads.append(ld)
            cp = remote(t, half3(g_refs[t], t, 1 - c), sibs[t], sib)
            cp.start()
            sends.append(cp)

        my_dev = 4 * x + 2 * y + c
        vrecv[my_dev] = vec_ref[...]
        for k in range(1, 8):
            to = (x ^ ((k >> 2) & 1), y ^ ((k >> 1) & 1), c ^ (k & 1))
            cp = remote(n + k - 1, vec_ref, vrecv.at[my_dev], to)
            cp.start()
            sends.append(cp)

        for t in range(n):
            loads[t].wait()
            remote(t, half3(g_refs[t], t, 1 - c), sibs[t], me).wait_recv()
            for k in range(N_SHARD):
                accs[t][k] = accs[t][k] + sibs[t][k]
            for j, chip in enumerate(chips):
                idx = 2 * chip[0] + chip[1]
                sbufs[t][idx] = accs[t][idx].astype(BF16)
                cp = remote(n + 7 + 3 * t + j, sbufs[t].at[idx], rbufs[t].at[mine], (*chip, c))
                cp.start()
                sends.append(cp)

        for t in range(n):
            total = accs[t][mine]
            for j, chip in enumerate(chips):
                idx = 2 * chip[0] + chip[1]
                remote(n + 7 + 3 * t + j, sbufs[t].at[idx], rbufs[t].at[idx], me).wait_recv()
                total = total + rbufs[t][idx].astype(F32)
            half2(f_refs[t], t, c)[...] = total
            cp = remote(4 * n + 7 + t, half2(f_refs[t], t, c), half2(f_refs[t], t, c), sib)
            cp.start()
            sends.append(cp)
        for t in range(n):
            remote(4 * n + 7 + t, half2(f_refs[t], t, 1 - c), half2(f_refs[t], t, 1 - c), me).wait_recv()

        for k in range(1, 8):
            src_dev = 4 * (x ^ ((k >> 2) & 1)) + 2 * (y ^ ((k >> 1) & 1)) + (c ^ (k & 1))
            remote(n + k - 1, vec_ref, vrecv.at[src_dev], me).wait_recv()
        vs = vrecv[0]
        for d in range(1, 8):
            vs = vs + vrecv[d]
        vsum_ref[...] = vs

        for cp in sends:
            cp.wait_send()

    nsem = 5 * n + 7
    half_shapes = [(N_SHARD, h, a.shape[2]) for h, a in zip(halves, gsh)]
    return pl.pallas_call(
        body, name="reduce_scatter_grads",
        out_shape=tuple(jax.ShapeDtypeStruct(a.shape[1:], F32) for a in gsh) + (jax.ShapeDtypeStruct((VEC_ROWS, 1024), F32),),
        in_specs=[pl.BlockSpec(memory_space=pl.ANY)] * n + [pl.BlockSpec(memory_space=pltpu.VMEM)],
        out_specs=(pl.BlockSpec(memory_space=pltpu.VMEM),) * (n + 1),
        scratch_shapes=([pltpu.VMEM(s_, F32) for s_ in half_shapes] * 2 + [pltpu.VMEM(s_, BF16) for s_ in half_shapes] * 2
                        + [pltpu.VMEM((8, VEC_ROWS, 1024), F32), pltpu.SemaphoreType.DMA((n,)),
                           pltpu.SemaphoreType.DMA((nsem,)), pltpu.SemaphoreType.DMA((nsem,))]),
        compiler_params=pltpu.CompilerParams(vmem_limit_bytes=56 * 1024 * 1024),
    )(*gsh, vec)


def _adamw(w, g, m, v):
    rows, cols = w.shape
    tr = rows if rows <= 256 else 256

    def body(w_ref, g_ref, m_ref, v_ref, d_ref, nm_ref, nv_ref):
        d_ref[...], nm_ref[...], nv_ref[...] = _adam_math(w_ref[...], g_ref[...], m_ref[...], v_ref[...])

    spec = pl.BlockSpec((tr, cols), lambda i: (i, 0))
    shp = jax.ShapeDtypeStruct((rows, cols), F32)
    return pl.pallas_call(body, name="adamw", grid=(rows // tr,), out_shape=(shp, shp, shp),
                          in_specs=[spec] * 4, out_specs=(spec,) * 3)(w, g, m, v)


def _adam_math(w, g, m, v):
    m2 = ADAM_B1 * m + (1.0 - ADAM_B1) * g
    v2 = ADAM_B2 * v + (1.0 - ADAM_B2) * (g * g)
    m_hat = m2 / (1.0 - ADAM_B1 ** ADAM_STEP)
    v_hat = v2 / (1.0 - ADAM_B2 ** ADAM_STEP)
    return -ADAM_LR * (m_hat / (jnp.sqrt(v_hat) + ADAM_EPS) + ADAM_WD * w), m2, v2


def _adamw_small(vsum, w, m, v):
    names = [name for name, _, _, _ in _VEC_LAYOUT]
    k = len(names)

    def body(*refs):
        vs_ref, w_refs, m_refs, v_refs = refs[0], refs[1:1 + k], refs[1 + k:1 + 2 * k], refs[1 + 2 * k:1 + 3 * k]
        outs = refs[1 + 3 * k:]
        for idx, (_, r, c0, width) in enumerate(_VEC_LAYOUT):
            gv = vs_ref[pl.ds(r, 1), pl.ds(c0, width)]
            d, m2, v2 = _adam_math(w_refs[idx][...], gv, m_refs[idx][...], v_refs[idx][...])
            outs[idx][...], outs[k + idx][...], outs[2 * k + idx][...], outs[3 * k + idx][...] = gv, d, m2, v2

    shapes = tuple(jax.ShapeDtypeStruct(w[name].shape, F32) for name in names)
    res = pl.pallas_call(
        body, name="adamw_small", out_shape=shapes * 4,
        in_specs=[pl.BlockSpec(memory_space=pltpu.VMEM)] * (1 + 3 * k), out_specs=(pl.BlockSpec(memory_space=pltpu.VMEM),) * (4 * k),
    )(vsum, *[w[name] for name in names], *[m[name] for name in names], *[v[name] for name in names])
    return tuple({name: res[part * k + idx] for idx, name in enumerate(names)} for part in range(4))


_BIG = ("w_in", "w_uq", "w_ukv", "w_out", "w_ple", "w_ple_gate")
_COL_SHARDED = ("w_in", "w_uq", "w_ukv", "w_ple")


def _join_shards(parts):
    cols = lambda a: a.transpose(1, 0, 2).reshape(a.shape[1], N_SHARD * a.shape[2])
    rows = lambda a: a.reshape(N_SHARD * a.shape[1], a.shape[2])
    return {n: (cols if n in _COL_SHARDED else rows)(parts[n]) for n in _BIG}


def _split_shards(full):
    cols = lambda a: a.reshape(a.shape[0], N_SHARD, a.shape[1] // N_SHARD).transpose(1, 0, 2)
    rows = lambda a: a.reshape(N_SHARD, a.shape[0] // N_SHARD, a.shape[1])
    return {n: (cols if n in _COL_SHARDED else rows)(full[n]) for n in _BIG}


def _extend_weights(w):
    win = w["w_in"]
    zeros = lambda r, c: jnp.zeros((r, c), win.dtype)
    win_ext = jnp.concatenate([win[:, :2432], zeros(D_MODEL, 64), win[:, 2432:2464], zeros(D_MODEL, 32), win[:, 2464:]], axis=1)
    wuq_ext = jnp.pad(w["w_uq"].reshape(Q_LORA, 8, 96), ((0, 0), (0, 0), (0, 32))).reshape(Q_LORA, 1024)
    wukv = w["w_ukv"].reshape(KV_LORA, 8, 128)
    wk_ext = jnp.pad(wukv[:, :, :64], ((0, 0), (0, 0), (0, 64))).reshape(KV_LORA, 1024)
    wv = wukv[:, :, 64:].reshape(KV_LORA, 512)
    return win_ext, wuq_ext, wk_ext, wv


def _contract_grads(dwin_ext, dwuq_ext, dwk_ext, dwv):
    dwin = jnp.concatenate([dwin_ext[:, :2432], dwin_ext[:, 2496:2528], dwin_ext[:, 2560:]], axis=1)
    dwuq = dwuq_ext.reshape(Q_LORA, 8, 128)[:, :, :96].reshape(Q_LORA, 768)
    dwukv = jnp.concatenate([dwk_ext.reshape(KV_LORA, 8, 128)[:, :, :64], dwv.reshape(KV_LORA, 8, 64)], axis=2)
    return dwin, dwuq, dwukv.reshape(KV_LORA, 1024)


def _rope_tables(positions):
    half = QK_ROPE // 2
    freq = ROPE_THETA ** (-jnp.arange(half, dtype=F32) / half)
    ang = positions.astype(F32)[:, None] * freq
    cos, sin = jnp.cos(ang), jnp.sin(ang)
    s = positions.shape[0]
    z = lambda n: jnp.zeros((s, n), F32)
    c_t = jnp.concatenate([jnp.ones((s, 64), F32), cos, cos, z(32)], axis=1)
    sa_t = jnp.concatenate([z(64), -sin, z(16), z(32)], axis=1)
    sb_t = jnp.concatenate([z(64), z(16), sin, z(32)], axis=1)
    return c_t, sa_t, sb_t


def _local_grads(x, p, positions, tgt, gains, wfull):
    win_ext, wuq_ext, wk_ext, wv = _extend_weights(wfull)
    wout, wple, wpg = wfull["w_out"], wfull["w_ple"], wfull["w_ple_gate"]
    tabs = _rope_tables(positions)
    g = gains
    sbq, sbk, sbv, sbg, mlag, cq, ckv, qc, kc, mv = _pre_fwd(x, tabs, g["norm_pre_g"], win_ext, g["q_norm_g"], wuq_ext,
                                                             g["kv_norm_g"], wk_ext, wv)
    sbo = _sb_fwd(sbq, sbk, sbv)
    mlao, lse = _mla_fwd(qc, kc, mv.T)
    dsbo, dmlao, delta, dsbg, dmlag, dxres, dwout, dwpg, dwple, vec_c = _post(
        x, p, tgt, sbo, mlao, sbg, mlag, g["sb_out_norm_g"], g["mla_out_norm_g"], wout, g["norm_post_g"], wple,
        g["ple_norm_g"], wpg, g["b_ple_gate"])
    dsbq, dsbk, dsbv = _sb_bwd(sbq, sbk, sbv, dsbo)
    dqc, dkc, dmv = _mla_bwd(qc, kc, kc.T, mv, dmlao, lse, delta)
    gx, dwin_ext, dwuq_ext, dwk_ext, dwv, vec_d = _pre_bwd(
        x, dxres, dsbq, dsbk, dsbv, dsbg, dmlag, dqc, dkc, dmv, cq, ckv, tabs, g["norm_pre_g"], win_ext, g["q_norm_g"],
        wuq_ext, g["kv_norm_g"], wk_ext, wv)
    dwin, dwuq, dwukv = _contract_grads(dwin_ext, dwuq_ext, dwk_ext, dwv)
    grads = {"w_in": dwin, "w_uq": dwuq, "w_ukv": dwukv, "w_out": dwout, "w_ple": dwple, "w_ple_gate": dwpg}
    return gx, grads, jnp.concatenate([vec_c, vec_d], axis=0)


_VEC_LAYOUT = (("norm_post_g", 0, 0, 1024), ("ple_norm_g", 1, 0, 1024), ("b_ple_gate", 2, 0, 1024), ("sb_out_norm_g", 3, 0, 512),
               ("mla_out_norm_g", 3, 512, 512), ("norm_pre_g", 8, 0, 1024), ("q_norm_g", 9, 0, 256), ("kv_norm_g", 9, 256, 128))
_LOSS_ROW = 4
_WEIGHT_ORDER = ("norm_pre_g", "w_in", "q_norm_g", "w_uq", "kv_norm_g", "w_ukv", "sb_out_norm_g", "mla_out_norm_g", "w_out",
                 "norm_post_g", "w_ple", "ple_norm_g", "w_ple_gate", "b_ple_gate")


def kernel(x, p, positions, norm_pre_g, w_in, q_norm_g, w_uq, kv_norm_g, w_ukv, sb_out_norm_g, mla_out_norm_g, w_out, norm_post_g, w_ple, ple_norm_g, w_ple_gate, b_ple_gate, loss_target, m_norm_pre_g, m_w_in, m_q_norm_g, m_w_uq, m_kv_norm_g, m_w_ukv, m_sb_out_norm_g, m_mla_out_norm_g, m_w_out, m_norm_post_g, m_w_ple, m_ple_norm_g, m_w_ple_gate, m_b_ple_gate, v_norm_pre_g, v_w_in, v_q_norm_g, v_w_uq, v_kv_norm_g, v_w_ukv, v_sb_out_norm_g, v_mla_out_norm_g, v_w_out, v_norm_post_g, v_w_ple, v_ple_norm_g, v_w_ple_gate, v_b_ple_gate):
    w = {"norm_pre_g": norm_pre_g, "w_in": w_in[0], "q_norm_g": q_norm_g, "w_uq": w_uq[0], "kv_norm_g": kv_norm_g, "w_ukv": w_ukv[0],
         "sb_out_norm_g": sb_out_norm_g, "mla_out_norm_g": mla_out_norm_g, "w_out": w_out[0], "norm_post_g": norm_post_g,
         "w_ple": w_ple[0], "ple_norm_g": ple_norm_g, "w_ple_gate": w_ple_gate[0], "b_ple_gate": b_ple_gate}
    m = {"norm_pre_g": m_norm_pre_g, "w_in": m_w_in[0], "q_norm_g": m_q_norm_g, "w_uq": m_w_uq[0], "kv_norm_g": m_kv_norm_g,
         "w_ukv": m_w_ukv[0], "sb_out_norm_g": m_sb_out_norm_g, "mla_out_norm_g": m_mla_out_norm_g, "w_out": m_w_out[0],
         "norm_post_g": m_norm_post_g, "w_ple": m_w_ple[0], "ple_norm_g": m_ple_norm_g, "w_ple_gate": m_w_ple_gate[0],
         "b_ple_gate": m_b_ple_gate}
    v = {"norm_pre_g": v_norm_pre_g, "w_in": v_w_in[0], "q_norm_g": v_q_norm_g, "w_uq": v_w_uq[0], "kv_norm_g": v_kv_norm_g,
         "w_ukv": v_w_ukv[0], "sb_out_norm_g": v_sb_out_norm_g, "mla_out_norm_g": v_mla_out_norm_g, "w_out": v_w_out[0],
         "norm_post_g": v_norm_post_g, "w_ple": v_w_ple[0], "ple_norm_g": v_ple_norm_g, "w_ple_gate": v_w_ple_gate[0],
         "b_ple_gate": v_b_ple_gate}
    gathered = _allgather_weights([w[n] for n in _BIG])
    wfull = _join_shards(dict(zip(_BIG, gathered)))

    gx, grads, vec = _local_grads(x[0], p[0, 0], positions[0], loss_target[0], w, wfull)

    gsh = _split_shards(grads)
    *gred, vsum = _reduce_scatter_grads([gsh[n] for n in _BIG], vec)
    loss = vsum[_LOSS_ROW, 0]

    g, delta, new_m, new_v = _adamw_small(vsum, w, m, v)
    for n, gn in zip(_BIG, gred):
        g[n] = gn
        delta[n], new_m[n], new_v[n] = _adamw(w[n], gn, m[n], v[n])

    lead = lambda n, a: a[None] if n in _BIG else a
    return (loss, gx[None],
            *[lead(n, g[n]) for n in _WEIGHT_ORDER], *[lead(n, delta[n]) for n in _WEIGHT_ORDER],
            *[lead(n, new_m[n]) for n in _WEIGHT_ORDER], *[lead(n, new_v[n]) for n in _WEIGHT_ORDER])
```

```python
import numpy as np
import jax
import jax.numpy as jnp
from jax import lax
from jax.experimental import pallas as pl
from jax.experimental.pallas import tpu as pltpu

F32 = jnp.float32
BF16 = jnp.bfloat16
MESH = pl.DeviceIdType.MESH

D_MODEL = 1024
HEAD_DIM = 64
D_SB = 512
D_MLA = 512
Q_LORA = 256
KV_LORA = 128
QK_NOPE = 64
QK_ROPE = 32
PLE_DIM = 256
D_IN = 2976
D_EXT = 3072
ROPE_THETA = 10000.0
EPS = 1e-6
N_SHARD = 4

ADAM_LR = 0.001
ADAM_B1 = 0.9
ADAM_B2 = 0.999
ADAM_EPS = 1e-08
ADAM_WD = 0.01
ADAM_STEP = 10

LANES = 128
BK = 128
WQ = 256
MQ = 512
SB_CUTOFF = 120.0
TM = 256
VEC_ROWS = 16
VMEM_DENSE = 52 * 1024 * 1024
VMEM_ATTN = 40 * 1024 * 1024


def _mm(a, b):
    return jnp.dot(a, b, preferred_element_type=F32)


def _mm_nt(a, b):
    return lax.dot_general(a, b, (((1,), (1,)), ((), ())), preferred_element_type=F32)


def _mm_tn(a, b):
    return lax.dot_general(a, b, (((0,), (0,)), ((), ())), preferred_element_type=F32)


def _seg(a, bd2):
    return _mm(_split2(a), bd2)


def _const(mask):
    return jnp.asarray(np.asarray(mask, np.float32), dtype=BF16)


def _blockdiag2(n, seg):
    r = (np.arange(2 * n)[:, None] % n) // seg
    c = np.arange(n)[None, :] // seg
    return _const(r == c)


def _sigmoid(a):
    return 1.0 / (1.0 + jnp.exp(-a))


def _rowmean(a):
    return jnp.mean(a, axis=-1, keepdims=True)


def _colsum(a):
    return jnp.sum(a, axis=0, keepdims=True)


def _rope_fwd(a, c, sa, sb):
    w = a.shape[-1]
    return a * c + pltpu.roll(a, w - 16, 1) * sa + pltpu.roll(a, 16, 1) * sb


def _rope_bwd(g, c, sa, sb):
    w = g.shape[-1]
    return g * c + pltpu.roll(g * sa, 16, 1) + pltpu.roll(g * sb, w - 16, 1)


def _full(shape):
    return pl.BlockSpec(shape, lambda *_: (0,) * len(shape))


def _full2(shape):
    return pl.BlockSpec(shape, lambda p, i: (0, 0))


def _rows(width, tm=TM):
    return pl.BlockSpec((tm, width), lambda i: (i, 0))


def _pre_fwd(x, tabs, gpre, win, gq, wuq, gkv, wk, wv):
    s = x.shape[0]
    c_t, sa_t, sb_t = tabs

    def body(x_ref, c_ref, sa_ref, sb_ref, gpre_ref, win_ref, gq_ref, wuq_ref, gkv_ref, wk_ref, wv_ref,
             sbq_ref, sbk_ref, sbv_ref, sbg_ref, mlag_ref, cq_ref, ckv_ref, qc_ref, kc_ref, mv_ref):
        xv = x_ref[...]
        r1 = lax.rsqrt(_rowmean(xv * xv) + EPS)
        h = (xv * r1 * gpre_ref[...]).astype(BF16)
        proj = _mm(h, win_ref[...])
        sbq_ref[...] = proj[:, 0:512].astype(BF16)
        sbk_ref[...] = proj[:, 512:1024].astype(BF16)
        sbv_ref[...] = proj[:, 1024:1536].astype(BF16)
        sbg_ref[...] = proj[:, 1536:2048]
        cq = proj[:, 2048:2304]
        ckv = proj[:, 2304:2432]
        kr = proj[:, 2432:2560]
        mlag_ref[...] = proj[:, 2560:3072]
        cq_ref[...] = cq
        ckv_ref[...] = ckv
        c1, sa1, sb1 = c_ref[...], sa_ref[...], sb_ref[...]
        c8, sa8, sb8 = jnp.tile(c1, (1, 8)), jnp.tile(sa1, (1, 8)), jnp.tile(sb1, (1, 8))
        cqn = (cq * lax.rsqrt(_rowmean(cq * cq) + EPS) * gq_ref[...]).astype(BF16)
        qe = _mm(cqn, wuq_ref[...])
        qc_ref[...] = _rope_fwd(qe, c8, sa8, sb8).astype(BF16)
        ckvn = (ckv * lax.rsqrt(_rowmean(ckv * ckv) + EPS) * gkv_ref[...]).astype(BF16)
        ke = _mm(ckvn, wk_ref[...])
        krr = _rope_fwd(kr, c1, sa1, sb1)
        kc_ref[...] = (ke + jnp.tile(krr, (1, 8))).astype(BF16)
        mv_ref[...] = _mm(ckvn, wv_ref[...]).astype(BF16)

    out_shape = (
        jax.ShapeDtypeStruct((s, 512), BF16), jax.ShapeDtypeStruct((s, 512), BF16), jax.ShapeDtypeStruct((s, 512), BF16),
        jax.ShapeDtypeStruct((s, 512), F32), jax.ShapeDtypeStruct((s, 512), F32),
        jax.ShapeDtypeStruct((s, Q_LORA), F32), jax.ShapeDtypeStruct((s, KV_LORA), F32),
        jax.ShapeDtypeStruct((s, 1024), BF16), jax.ShapeDtypeStruct((s, 1024), BF16), jax.ShapeDtypeStruct((s, 512), BF16),
    )
    return pl.pallas_call(
        body, name="pre_fwd", grid=(s // TM,), out_shape=out_shape,
        in_specs=[_rows(D_MODEL), _rows(LANES), _rows(LANES), _rows(LANES), _full((1, D_MODEL)), _full((D_MODEL, D_EXT)),
                  _full((1, Q_LORA)), _full((Q_LORA, 1024)), _full((1, KV_LORA)), _full((KV_LORA, 1024)), _full((KV_LORA, 512))],
        out_specs=(_rows(512), _rows(512), _rows(512), _rows(512), _rows(512), _rows(Q_LORA), _rows(KV_LORA),
                   _rows(1024), _rows(1024), _rows(512)),
        compiler_params=pltpu.CompilerParams(vmem_limit_bytes=VMEM_DENSE),
    )(x, c_t, sa_t, sb_t, gpre, win, gq, wuq, gkv, wk, wv)


def _softplus(z):
    neg_abs = lax.bitcast_convert_type(lax.bitcast_convert_type(z, jnp.uint32) | jnp.uint32(0x80000000), F32)
    return jnp.maximum(z, 0.0) + jnp.log(1.0 + jnp.exp(neg_abs))


def _pair_tri(kind):
    r, c = np.arange(512)[:, None], np.arange(256)[None, :]
    same = ((r // BK) % 2) == (c // BK)
    rk, ck = r % BK, c % BK
    m = {"ge": rk >= ck, "lt": rk < ck, "le": rk <= ck}[kind]
    return _const(same & m)


def _split2(a):
    hi = a.astype(BF16)
    lo = (a - hi.astype(F32)).astype(BF16)
    return jnp.concatenate([hi, lo], axis=1)


def _pair_stack(b, lane):
    zero = jnp.zeros_like(b)
    return jnp.concatenate([jnp.where(lane < 64, b, zero), jnp.where(lane >= 64, b, zero)], axis=0)


def _sb_fwd(q, k, v):
    s = q.shape[0]

    def body(q_ref, k_ref, v_ref, uge_ref, o_ref):
        i = pl.program_id(1)
        lane = lax.broadcasted_iota(jnp.int32, (1, LANES), 1)
        row = lax.broadcasted_iota(jnp.int32, (WQ, 2 * BK), 0) + i * WQ
        col = lax.broadcasted_iota(jnp.int32, (WQ, 2 * BK), 1) % BK
        qs = q_ref[...] * (HEAD_DIM ** -0.5)

        def scores(j):
            ks = pl.multiple_of(j * BK, BK)
            return _mm_nt(qs, _pair_stack(k_ref[pl.ds(ks, BK), :], lane))

        def decay(z, j, masked):
            sp = _softplus(z)
            if masked:
                sp = jnp.where((col + j * BK) < row, sp, 0.0)
            return _mm(_split2(sp), uge_ref[...])

        def weights(z, cum, j, masked):
            w = jnp.exp(z - cum)
            if masked:
                w = jnp.where((col + j * BK) < row, w, 0.0)
            ks = pl.multiple_of(j * BK, BK)
            return _mm(w.astype(BF16), _pair_stack(v_ref[pl.ds(ks, BK), :], lane))

        def fold(carry, pv, cum):
            acc, run = carry
            return acc + jnp.exp(-run) * pv, run + _heads_narrow(cum, 0, lane)

        def pair(ja, jb, carry, masked):
            za, zb = scores(ja), scores(jb)
            ca = decay(za, ja, masked)
            cb = decay(zb, jb, masked)
            pa = weights(za, ca, ja, masked)
            pb = weights(zb, cb, jb, masked)
            return fold(fold(carry, pa, ca), pb, cb)

        assert WQ == 2 * BK
        acc, run = pair(2 * i + 1, 2 * i, (jnp.zeros((WQ, LANES), F32), jnp.zeros((WQ, LANES), F32)), True)

        def more(c):
            return (c[0] < i) & (c[1] > 0)

        def step(c):
            jj, _, acc, run = c
            acc, run = pair(2 * i - 1 - 2 * jj, 2 * i - 2 - 2 * jj, (acc, run), False)
            return jj + 1, (jnp.min(run) < SB_CUTOFF).astype(jnp.int32), acc, run

        _, _, acc, _ = lax.while_loop(more, step, (jnp.int32(0), (jnp.min(run) < SB_CUTOFF).astype(jnp.int32), acc, run))
        o_ref[...] = acc

    qspec = pl.BlockSpec((WQ, LANES), lambda p, i: (i, p))
    kspec = pl.BlockSpec((s, LANES), lambda p, i: (0, p))
    return pl.pallas_call(
        body, name="sb_fwd", grid=(4, s // WQ),
        out_shape=jax.ShapeDtypeStruct((s, 512), F32),
        in_specs=[qspec, kspec, kspec, _full2((4 * BK, 2 * BK))], out_specs=qspec,
        compiler_params=pltpu.CompilerParams(vmem_limit_bytes=VMEM_ATTN),
    )(q, k, v, _pair_tri("ge"))


def _pair_tri1(kind):
    r, c = np.arange(2 * BK)[:, None], np.arange(2 * BK)[None, :]
    rk, ck = r % BK, c % BK
    m = {"ge": rk >= ck, "lt": rk < ck, "le": rk <= ck}[kind]
    return _const(((r // BK) == (c // BK)) & m)


def _heads_wide(a):
    m = a.shape[0]
    return jnp.concatenate([jnp.broadcast_to(a[:, 0:1], (m, BK)), jnp.broadcast_to(a[:, 64:65], (m, BK))], axis=1)


def _heads_narrow(a, col, lane):
    return jnp.where(lane < 64, a[:, col:col + 1], a[:, BK + col:BK + col + 1])


def _pair_rowsum():
    r, c = np.arange(512)[:, None], np.arange(LANES)[None, :]
    return _const(((r // BK) % 2) == (c // 64))


def _sb_bwd(q, k, v, do):
    s = q.shape[0]
    assert WQ == 2 * BK

    def body(q_ref, k_ref, v_ref, do_ref, uge_ref, ule_ref, rsum_ref, dq_ref, dk_ref, dv_ref, later_ref):
        i = pl.program_id(1)

        @pl.when(i == 0)
        def _():
            dk_ref[...] = jnp.zeros_like(dk_ref)
            dv_ref[...] = jnp.zeros_like(dv_ref)

        lane = lax.broadcasted_iota(jnp.int32, (1, LANES), 1)
        row = lax.broadcasted_iota(jnp.int32, (WQ, 2 * BK), 0) + i * WQ
        col = lax.broadcasted_iota(jnp.int32, (WQ, 2 * BK), 1) % BK
        qs = q_ref[...] * (HEAD_DIM ** -0.5)
        dof = do_ref[...].astype(F32)

        def scan(ja, jb, run, masked):
            za = _mm_nt(qs, _pair_stack(k_ref[pl.ds(pl.multiple_of(ja * BK, BK), BK), :], lane))
            zb = _mm_nt(qs, _pair_stack(k_ref[pl.ds(pl.multiple_of(jb * BK, BK), BK), :], lane))
            spa, spb = _softplus(za), _softplus(zb)
            if masked:
                spa = jnp.where((col + ja * BK) < row, spa, 0.0)
                spb = jnp.where((col + jb * BK) < row, spb, 0.0)
            rsa, rsb = _mm(_split2(spa), rsum_ref[...]), _mm(_split2(spb), rsum_ref[...])
            later_ref[ja] = run
            later_ref[jb] = run + rsa
            return run + rsa + rsb

        run = scan(2 * i + 1, 2 * i, jnp.zeros((WQ, LANES), F32), True)

        def more(c):
            return (c[0] < i) & (c[1] > 0)

        def step(c):
            run = scan(2 * i - 1 - 2 * c[0], 2 * i - 2 - 2 * c[0], c[2], False)
            return c[0] + 1, (jnp.min(run) < SB_CUTOFF).astype(jnp.int32), run

        npairs, _, _ = lax.while_loop(more, step, (jnp.int32(0), (jnp.min(run) < SB_CUTOFF).astype(jnp.int32), run))

        def keys(j):
            ks = pl.multiple_of(j * BK, BK)
            return ks, _pair_stack(k_ref[pl.ds(ks, BK), :], lane)

        def gates(z, j, masked):
            neg_abs = lax.bitcast_convert_type(lax.bitcast_convert_type(z, jnp.uint32) | jnp.uint32(0x80000000), F32)
            u = jnp.exp(neg_abs)
            opu = 1.0 + u
            sp = jnp.maximum(z, 0.0) + jnp.log(opu)
            if masked:
                sp = jnp.where((col + j * BK) < row, sp, 0.0)
            return jnp.where(z >= 0.0, 1.0, u) / opu, _mm(_split2(sp), uge_ref[...])

        def weights(z, cum, ks, j, masked):
            wl = jnp.exp(z - cum)
            if masked:
                wl = jnp.where((col + j * BK) < row, wl, 0.0)
            dfo = (jnp.exp(-later_ref[j]) * dof).astype(BF16)
            e = _mm_nt(dfo, _pair_stack(v_ref[pl.ds(ks, BK), :], lane)) * wl
            return wl.astype(BF16), dfo, e

        def scores_grad(e, sig, erun, j, masked):
            ecum = _mm(e.astype(BF16), ule_ref[...]) + _heads_wide(erun)
            dz = e - sig * ecum
            if masked:
                dz = jnp.where((col + j * BK) < row, dz, 0.0)
            return dz.astype(BF16), _heads_narrow(ecum, BK - 1, lane)

        def scatter(ks, dzb, wlb, dfo):
            rk = _mm_tn(dzb, qs)
            dk_ref[pl.ds(ks, BK), :] += jnp.where(lane < 64, rk[0:BK], rk[BK:2 * BK])
            rv = _mm_tn(wlb, dfo)
            dv_ref[pl.ds(ks, BK), :] += jnp.where(lane < 64, rv[0:BK], rv[BK:2 * BK])

        def pair(ja, carry, masked):
            dq, erun = carry
            jb = ja + 1
            ksa, kbda = keys(ja)
            ksb, kbdb = keys(jb)
            za, zb = _mm_nt(qs, kbda), _mm_nt(qs, kbdb)
            siga, cuma = gates(za, ja, masked)
            sigb, cumb = gates(zb, jb, masked)
            wla, dfoa, ea = weights(za, cuma, ksa, ja, masked)
            wlb, dfob, eb = weights(zb, cumb, ksb, jb, masked)
            dza, eruna = scores_grad(ea, siga, erun, ja, masked)
            dzb, erunb = scores_grad(eb, sigb, eruna, jb, masked)
            dq = dq + _mm(jnp.concatenate([dza, dzb], axis=1), jnp.concatenate([kbda, kbdb], axis=0))
            scatter(ksa, dza, wla, dfoa)
            scatter(ksb, dzb, wlb, dfob)
            return dq, erunb

        zero = jnp.zeros((WQ, LANES), F32)
        first = 2 * (i - npairs)
        carry = lax.fori_loop(0, npairs, lambda t, c: pair(first + 2 * t, c, False), (zero, zero))
        dq, _ = pair(2 * i, carry, True)
        dq_ref[...] = (dq * (HEAD_DIM ** -0.5)).astype(BF16)

    qspec = pl.BlockSpec((WQ, LANES), lambda p, i: (i, p))
    kspec = pl.BlockSpec((s, LANES), lambda p, i: (0, p))
    return pl.pallas_call(
        body, name="sb_bwd", grid=(4, s // WQ),
        out_shape=(jax.ShapeDtypeStruct((s, 512), BF16), jax.ShapeDtypeStruct((s, 512), F32),
                   jax.ShapeDtypeStruct((s, 512), F32)),
        in_specs=[qspec, kspec, kspec, qspec, _full2((4 * BK, 2 * BK)), _full2((2 * BK, 2 * BK)), _full2((4 * BK, LANES))],
        out_specs=(qspec, kspec, kspec),
        scratch_shapes=[pltpu.VMEM((s // BK, WQ, LANES), F32)],
        compiler_params=pltpu.CompilerParams(vmem_limit_bytes=VMEM_ATTN),
    )(q, k, v, do, _pair_tri("ge"), _pair_tri1("le"), _pair_rowsum())


MLA_SCALE = (QK_NOPE + QK_ROPE) ** -0.5
LOG2E = 1.4426950408889634


def _mla_keys(kb):
    zero = jnp.zeros((BK, LANES), kb.dtype)
    return jnp.concatenate([jnp.concatenate([kb[:, 0:LANES], zero], axis=1),
                            jnp.concatenate([zero, kb[:, LANES:2 * LANES]], axis=1)], axis=0)


def _mla_fwd(qc, kc, vt):
    s = qc.shape[0]
    rows_l = 16

    def body(q_ref, k_ref, vt_ref, o_ref, l_ref, p_scr, ot_scr, st_scr):
        i = pl.program_id(1)
        keyc = lax.broadcasted_iota(jnp.int32, (BK, MQ), 0)
        qryc = (lax.broadcasted_iota(jnp.int32, (BK, MQ), 1) + i * MQ) // 64
        row = lax.broadcasted_iota(jnp.int32, (LANES, 1), 0)
        qw = q_ref[...]
        orow = lax.broadcasted_iota(jnp.int32, (rows_l, 2 * BK), 0)
        ocol = lax.broadcasted_iota(jnp.int32, (rows_l, 2 * BK), 1)
        ones = jnp.where(((orow == 0) & (ocol < BK)) | ((orow == 1) & (ocol >= BK)), 1.0, 0.0).astype(BF16)

        def scores(j):
            ks = pl.multiple_of(j * BK, BK)
            return _mm_nt(_mla_keys(k_ref[pl.ds(ks, BK), :]), qw)

        def values_t(j):
            vtb = vt_ref[:, pl.ds(pl.multiple_of(j * BK, BK), BK)]
            zero = jnp.zeros_like(vtb)
            top = jnp.concatenate([jnp.where(row < 64, vtb, zero), jnp.where(row >= 64, vtb, zero)], axis=1)
            return jnp.concatenate([top, ones], axis=0)

        def softmax(ja, za, zb, masked):
            c = MLA_SCALE * LOG2E
            parts = [za[0:BK] * c, za[BK:2 * BK] * c, zb[0:BK] * c, zb[BK:2 * BK] * c]
            if masked:
                va = ((keyc + ja * BK) // 64) <= qryc
                vb = ((keyc + (ja + 1) * BK) // 64) <= qryc
                parts = [jnp.where(va, parts[0], -1e30), jnp.where(va, parts[1], -1e30),
                         jnp.where(vb, parts[2], -1e30), jnp.where(vb, parts[3], -1e30)]
            m0, m1 = st_scr[0:1, :], st_scr[1:2, :]
            n0 = jnp.maximum(m0, jnp.max(jnp.maximum(parts[0], parts[2]), axis=0, keepdims=True))
            n1 = jnp.maximum(m1, jnp.max(jnp.maximum(parts[1], parts[3]), axis=0, keepdims=True))
            st_scr[2:3, :] = jnp.exp2(m0 - n0)
            st_scr[3:4, :] = jnp.exp2(m1 - n1)
            st_scr[0:1, :] = n0
            st_scr[1:2, :] = n1
            p_scr[...] = jnp.concatenate([jnp.exp2(parts[0] - n0), jnp.exp2(parts[1] - n1),
                                          jnp.exp2(parts[2] - n0), jnp.exp2(parts[3] - n1)], axis=0).astype(BF16)

        def accumulate(ja):
            pv = _mm(jnp.concatenate([values_t(ja), values_t(ja + 1)], axis=1), p_scr[...])
            a = jnp.where(row < 64, st_scr[2:3, :], st_scr[3:4, :])
            ot_scr[0:LANES, :] = a * ot_scr[0:LANES, :] + pv[0:LANES]
            ot_scr[LANES:LANES + 8, :] = st_scr[2:10, :] * ot_scr[LANES:LANES + 8, :] + pv[LANES:LANES + 8]

        def step(n, masked):
            za, zb = scores(2 * n), scores(2 * n + 1)
            accumulate(2 * n - 2)
            softmax(2 * n, za, zb, masked)

        def first(masked):
            softmax(0, scores(0), scores(1), masked)

        st_scr[...] = jnp.concatenate([jnp.full((2, MQ), -1e30, F32), jnp.ones((14, MQ), F32)], axis=0)
        ot_scr[...] = jnp.zeros_like(ot_scr)

        npq = MQ // (2 * BK)

        @pl.when(i == 0)
        def _():
            first(True)
            for d in range(1, npq):
                step(d, True)

        @pl.when(i > 0)
        def _():
            first(False)
            lax.fori_loop(1, npq * i, lambda n, c: (step(n, False), c)[1], 0)
            for d in range(npq):
                step(npq * i + d, True)

        accumulate(2 * (npq * (i + 1) - 1))
        l0, l1 = ot_scr[LANES:LANES + 1, :], ot_scr[LANES + 1:LANES + 2, :]
        o_ref[...] = (ot_scr[0:LANES, :] / jnp.where(row < 64, l0, l1)).T
        l_ref[...] = jnp.where(row < 64, st_scr[0:1, :] + jnp.log2(l0), st_scr[1:2, :] + jnp.log2(l1)).T

    qspec = pl.BlockSpec((MQ, 2 * LANES), lambda p, i: (i, p))
    kspec = pl.BlockSpec((s, 2 * LANES), lambda p, i: (0, p))
    vtspec = pl.BlockSpec((LANES, s), lambda p, i: (p, 0))
    ospec = pl.BlockSpec((MQ, LANES), lambda p, i: (i, p))
    return pl.pallas_call(
        body, name="mla_fwd", grid=(4, s // MQ),
        out_shape=(jax.ShapeDtypeStruct((s, 512), F32), jax.ShapeDtypeStruct((s, 512), F32)),
        in_specs=[qspec, kspec, vtspec], out_specs=(ospec, ospec),
        scratch_shapes=[pltpu.VMEM((4 * BK, MQ), BF16), pltpu.VMEM((LANES + 8, MQ), F32), pltpu.VMEM((16, MQ), F32)],
        compiler_params=pltpu.CompilerParams(vmem_limit_bytes=VMEM_ATTN),
    )(qc, kc, vt)


def _mla_bwd(qc, kc, kct, v, do, lse, delta):
    s = qc.shape[0]

    def body(q_ref, k_ref, kt_ref, v_ref, do_ref, l_ref, d_ref, dq_ref, dk_ref, dv_ref, dqt_scr, p_scr, dz_scr):
        i = pl.program_id(1)

        @pl.when(i == 0)
        def _():
            dk_ref[...] = jnp.zeros_like(dk_ref)
            dv_ref[...] = jnp.zeros_like(dv_ref)

        lane = lax.broadcasted_iota(jnp.int32, (1, LANES), 1)
        keyc = lax.broadcasted_iota(jnp.int32, (BK, MQ), 0)
        qryc = (lax.broadcasted_iota(jnp.int32, (BK, MQ), 1) + i * MQ) // 64
        qw = q_ref[...]
        dob = do_ref[...]
        dost = (dob.astype(F32) * MLA_SCALE).T.astype(BF16)
        lt = l_ref[...].T
        dt = (d_ref[...] * MLA_SCALE).T
        lse0, lse1 = lt[0:1], lt[64:65]
        dl0, dl1 = dt[0:1], dt[64:65]
        dqt_scr[...] = jnp.zeros_like(dqt_scr)

        def products(j):
            ks = pl.multiple_of(j * BK, BK)
            return (_mm_nt(_mla_keys(k_ref[pl.ds(ks, BK), :]), qw), _mm(_pair_stack(v_ref[pl.ds(ks, BK), :], lane), dost))

        def grads(j, slot, zt, dwt, masked):
            zt = zt * (MLA_SCALE * LOG2E)
            p0 = jnp.exp2(zt[0:BK] - lse0)
            p1 = jnp.exp2(zt[BK:2 * BK] - lse1)
            if masked:
                valid = ((keyc + j * BK) // 64) <= qryc
                p0, p1 = jnp.where(valid, p0, 0.0), jnp.where(valid, p1, 0.0)
            p_scr[slot] = jnp.concatenate([p0, p1], axis=0).astype(BF16)
            dz_scr[slot] = jnp.concatenate([p0 * (dwt[0:BK] - dl0), p1 * (dwt[BK:2 * BK] - dl1)], axis=0).astype(BF16)

        def keys_t(ks):
            ktb = kt_ref[:, pl.ds(ks, BK)]
            zero = jnp.zeros((LANES, BK), ktb.dtype)
            return jnp.concatenate([jnp.concatenate([ktb[0:LANES], zero], axis=1),
                                    jnp.concatenate([zero, ktb[LANES:2 * LANES]], axis=1)], axis=0)

        def scatter(ja):
            ksa, ksb = pl.multiple_of(ja * BK, BK), pl.multiple_of((ja + 1) * BK, BK)
            dqt_scr[...] += _mm(jnp.concatenate([keys_t(ksa), keys_t(ksb)], axis=1),
                                jnp.concatenate([dz_scr[0], dz_scr[1]], axis=0))
            for slot, ks in ((0, ksa), (1, ksb)):
                rk = _mm(dz_scr[slot], qw)
                dk_ref[pl.ds(ks, BK), :] += jnp.concatenate([rk[0:BK, 0:LANES], rk[BK:2 * BK, LANES:2 * LANES]], axis=1)
                rv = _mm(p_scr[slot], dob)
                dv_ref[pl.ds(ks, BK), :] += jnp.where(lane < 64, rv[0:BK], rv[BK:2 * BK])

        def step(n, masked):
            za, wa = products(2 * n)
            zb, wb = products(2 * n + 1)
            scatter(2 * n - 2)
            grads(2 * n, 0, za, wa, masked)
            grads(2 * n + 1, 1, zb, wb, masked)

        def first(masked):
            za, wa = products(0)
            zb, wb = products(1)
            grads(0, 0, za, wa, masked)
            grads(1, 1, zb, wb, masked)

        npq = MQ // (2 * BK)

        @pl.when(i == 0)
        def _():
            first(True)
            for d in range(1, npq):
                step(d, True)

        @pl.when(i > 0)
        def _():
            first(False)
            lax.fori_loop(1, npq * i, lambda n, c: (step(n, False), c)[1], 0)
            for d in range(npq):
                step(npq * i + d, True)

        scatter(2 * (npq * (i + 1) - 1))
        dq_ref[...] = dqt_scr[...].T

    qspec = pl.BlockSpec((MQ, 2 * LANES), lambda p, i: (i, p))
    kspec = pl.BlockSpec((s, 2 * LANES), lambda p, i: (0, p))
    ktspec = pl.BlockSpec((2 * LANES, s), lambda p, i: (p, 0))
    vspec = pl.BlockSpec((s, LANES), lambda p, i: (0, p))
    ospec = pl.BlockSpec((MQ, LANES), lambda p, i: (i, p))
    return pl.pallas_call(
        body, name="mla_bwd", grid=(4, s // MQ),
        out_shape=(jax.ShapeDtypeStruct((s, 1024), F32), jax.ShapeDtypeStruct((s, 1024), F32),
                   jax.ShapeDtypeStruct((s, 512), F32)),
        in_specs=[qspec, kspec, ktspec, vspec, ospec, ospec, ospec], out_specs=(qspec, kspec, vspec),
        scratch_shapes=[pltpu.VMEM((2 * LANES, MQ), F32), pltpu.VMEM((2, 2 * BK, MQ), BF16), pltpu.VMEM((2, 2 * BK, MQ), BF16)],
        compiler_params=pltpu.CompilerParams(vmem_limit_bytes=VMEM_ATTN),
    )(qc, kc, kct, v, do, lse, delta)


def _post(x, p, tgt, sbo, mlao, sbg, mlag, gsb, gmla, wout, gpost, wple, gple, wpg, bpg):
    s = x.shape[0]

    def body(x_ref, p_ref, t_ref, sbo_ref, mlao_ref, sbg_ref, mlag_ref, gsb_ref, gmla_ref, wout_ref,
             gpost_ref, wple_ref, gple_ref, wpg_ref, bpg_ref, bd_ref,
             dsbo_ref, dmlao_ref, delta_ref, dsbg_ref, dmlag_ref, dxres_ref, dwout_ref, dwpg_ref, dwple_ref, vec_ref):
        i = pl.program_id(0)

        @pl.when(i == 0)
        def _():
            dwout_ref[...] = jnp.zeros_like(dwout_ref)
            dwpg_ref[...] = jnp.zeros_like(dwpg_ref)
            dwple_ref[...] = jnp.zeros_like(dwple_ref)
            vec_ref[...] = jnp.zeros_like(vec_ref)

        inv_hd = 1.0 / HEAD_DIM

        def head_fwd(o, g, gate):
            r = lax.rsqrt(_seg(o * o, bd_ref[...]) * inv_hd + EPS)
            hat = o * r
            n = hat * g
            sg = _sigmoid(gate)
            return hat, r, n, sg, n * (gate * sg)

        sbo, mlao, sbg_v, mlag_v = sbo_ref[...], mlao_ref[...], sbg_ref[...], mlag_ref[...]
        gsb_v, gmla_v = gsb_ref[...], gmla_ref[...]
        sb_hat, sb_r, sb_n, sb_sg, sb_y = head_fwd(sbo, gsb_v, sbg_v)
        ml_hat, ml_r, ml_n, ml_sg, ml_y = head_fwd(mlao, gmla_v, mlag_v)
        mix = jnp.concatenate([sb_y, ml_y], axis=1).astype(BF16)
        y = _mm(mix, wout_ref[...])
        ry = lax.rsqrt(_rowmean(y * y) + EPS)
        y_hat = y * ry
        gpost_v = gpost_ref[...]
        x1 = x_ref[...] + y_hat * gpost_v
        pb = p_ref[...].astype(BF16)
        pl_ = _mm(pb, wple_ref[...])
        rp = lax.rsqrt(_rowmean(pl_ * pl_) + EPS)
        pl_hat = pl_ * rp
        gple_v = gple_ref[...]
        ple = pl_hat * gple_v
        x1b = x1.astype(BF16)
        gate = _sigmoid(_mm(x1b, wpg_ref[...]) + bpg_ref[...])
        err = x1 + ple * gate - t_ref[...]
        loss = 0.5 * jnp.sum(_rowmean(err * err))
        dout = err * (1.0 / D_MODEL)

        du = dout * ple * gate * (1.0 - gate)
        dub = du.astype(BF16)
        dple = dout * gate
        dx1 = dout + _mm_nt(dub, wpg_ref[...])
        dwpg_ref[...] += _mm_tn(x1b, dub)
        dplh = dple * gple_v
        dpl = rp * (dplh - pl_hat * _rowmean(dplh * pl_hat))
        dwple_ref[...] += _mm_tn(pb, dpl.astype(BF16))
        dxres_ref[...] = dx1
        dyh = dx1 * gpost_v
        dy = ry * (dyh - y_hat * _rowmean(dyh * y_hat))
        dyb = dy.astype(BF16)
        dwout_ref[...] += _mm_tn(mix, dyb)
        dmix = _mm_nt(dyb, wout_ref[...])

        def head_bwd(dyv, hat, r, n, sg, g, gate):
            dn = dyv * (gate * sg)
            dgate = dyv * n * (sg * (1.0 + gate * (1.0 - sg)))
            dhat = dn * g
            do = r * (dhat - hat * (_seg(dhat * hat, bd_ref[...]) * inv_hd))
            return do, dgate, _colsum(dn * hat)

        dsbo, dsbg, dg_sb = head_bwd(dmix[:, 0:512], sb_hat, sb_r, sb_n, sb_sg, gsb_v, sbg_v)
        dmlao, dmlag, dg_ml = head_bwd(dmix[:, 512:1024], ml_hat, ml_r, ml_n, ml_sg, gmla_v, mlag_v)
        dsbo_ref[...] = dsbo.astype(BF16)
        dmlao_ref[...] = dmlao.astype(BF16)
        delta_ref[...] = _seg(dmlao * mlao, bd_ref[...])
        dsbg_ref[...] = dsbg.astype(BF16)
        dmlag_ref[...] = dmlag.astype(BF16)
        vec_ref[pl.ds(0, 1), :] += _colsum(dx1 * y_hat)
        vec_ref[pl.ds(1, 1), :] += _colsum(dple * pl_hat)
        vec_ref[pl.ds(2, 1), :] += _colsum(du)
        vec_ref[pl.ds(3, 1), :] += jnp.concatenate([dg_sb, dg_ml], axis=1)
        vec_ref[pl.ds(4, 1), :] += jnp.full((1, D_MODEL), loss, F32)

    out_shape = (
        jax.ShapeDtypeStruct((s, 512), BF16), jax.ShapeDtypeStruct((s, 512), BF16), jax.ShapeDtypeStruct((s, 512), F32),
        jax.ShapeDtypeStruct((s, 512), BF16), jax.ShapeDtypeStruct((s, 512), BF16), jax.ShapeDtypeStruct((s, D_MODEL), F32),
        jax.ShapeDtypeStruct((D_MODEL, D_MODEL), F32), jax.ShapeDtypeStruct((D_MODEL, D_MODEL), F32),
        jax.ShapeDtypeStruct((PLE_DIM, D_MODEL), F32), jax.ShapeDtypeStruct((8, D_MODEL), F32),
    )
    return pl.pallas_call(
        body, name="post_fwd_bwd", grid=(s // TM,), out_shape=out_shape,
        in_specs=[_rows(D_MODEL), _rows(PLE_DIM), _rows(D_MODEL), _rows(512), _rows(512), _rows(512), _rows(512),
                  _full((1, 512)), _full((1, 512)), _full((D_MODEL, D_MODEL)),
                  _full((1, D_MODEL)), _full((PLE_DIM, D_MODEL)), _full((1, D_MODEL)), _full((D_MODEL, D_MODEL)),
                  _full((1, D_MODEL)), _full((1024, 512))],
        out_specs=(_rows(512), _rows(512), _rows(512), _rows(512), _rows(512), _rows(D_MODEL),
                   _full((D_MODEL, D_MODEL)), _full((D_MODEL, D_MODEL)), _full((PLE_DIM, D_MODEL)), _full((8, D_MODEL))),
        compiler_params=pltpu.CompilerParams(vmem_limit_bytes=VMEM_DENSE),
    )(x, p, tgt, sbo, mlao, sbg, mlag, gsb, gmla, wout, gpost, wple, gple, wpg, bpg, _blockdiag2(512, HEAD_DIM))


def _pre_bwd(x, dxres, dsbq, dsbk, dsbv, dsbg, dmlag, dqc, dkc, dmv, cq, ckv, tabs, gpre, win, gq, wuq, gkv, wk, wv):
    s = x.shape[0]
    c_t, sa_t, sb_t = tabs

    def body(x_ref, dxres_ref, dsbq_ref, dsbk_ref, dsbv_ref, dsbg_ref, dmlag_ref, dqc_ref, dkc_ref, dmv_ref, cq_ref,
             ckv_ref, c_ref, sa_ref, sb_ref, gpre_ref, win_ref, gq_ref, wuq_ref, gkv_ref, wk_ref, wv_ref,
             gx_ref, dwin_ref, dwuq_ref, dwk_ref, dwv_ref, vec_ref, dwin_acc):
        i = pl.program_id(0)

        @pl.when(i == 0)
        def _():
            dwin_acc[...] = jnp.zeros_like(dwin_acc)
            dwuq_ref[...] = jnp.zeros_like(dwuq_ref)
            dwk_ref[...] = jnp.zeros_like(dwk_ref)
            dwv_ref[...] = jnp.zeros_like(dwv_ref)
            vec_ref[...] = jnp.zeros_like(vec_ref)

        lane = lax.broadcasted_iota(jnp.int32, (1, LANES), 1)
        c1, sa1, sb1 = c_ref[...], sa_ref[...], sb_ref[...]
        c8, sa8, sb8 = jnp.tile(c1, (1, 8)), jnp.tile(sa1, (1, 8)), jnp.tile(sb1, (1, 8))

        def norm_bwd(dn, hat, r, g):
            t = dn * g
            return r * (t - hat * _rowmean(t * hat)), _colsum(dn * hat)

        dqeb = _rope_bwd(dqc_ref[...], c8, sa8, sb8).astype(BF16)
        cq = cq_ref[...]
        rq = lax.rsqrt(_rowmean(cq * cq) + EPS)
        cq_hat = cq * rq
        gq_v = gq_ref[...]
        dwuq_ref[...] += _mm_tn((cq_hat * gq_v).astype(BF16), dqeb)
        dcq, dg_q = norm_bwd(_mm_nt(dqeb, wuq_ref[...]), cq_hat, rq, gq_v)

        dkc = dkc_ref[...]
        dkcb = dkc.astype(BF16)
        dmvb = dmv_ref[...].astype(BF16)
        ckv = ckv_ref[...]
        rkv = lax.rsqrt(_rowmean(ckv * ckv) + EPS)
        ckv_hat = ckv * rkv
        gkv_v = gkv_ref[...]
        ckvnb = (ckv_hat * gkv_v).astype(BF16)
        dwk_ref[...] += _mm_tn(ckvnb, dkcb)
        dwv_ref[...] += _mm_tn(ckvnb, dmvb)
        dckv, dg_kv = norm_bwd(_mm_nt(dkcb, wk_ref[...]) + _mm_nt(dmvb, wv_ref[...]), ckv_hat, rkv, gkv_v)

        dkr = dkc[:, 0:LANES]
        for hh in range(1, 8):
            dkr = dkr + dkc[:, LANES * hh:LANES * (hh + 1)]
        dkr = _rope_bwd(dkr, c1, sa1, sb1)
        dkr = jnp.where((lane >= 64) & (lane < 96), dkr, 0.0)

        dproj = jnp.concatenate([dsbq_ref[...], dsbk_ref[...].astype(BF16), dsbv_ref[...].astype(BF16), dsbg_ref[...],
                                 dcq.astype(BF16), dckv.astype(BF16), dkr.astype(BF16), dmlag_ref[...]], axis=1)
        xv = x_ref[...]
        r1 = lax.rsqrt(_rowmean(xv * xv) + EPS)
        x_hat = xv * r1
        gpre_v = gpre_ref[...]
        dwin_acc[...] += _mm_tn((x_hat * gpre_v).astype(BF16), dproj)
        dx, dg_pre = norm_bwd(_mm_nt(dproj, win_ref[...]), x_hat, r1, gpre_v)
        gx_ref[...] = dxres_ref[...] + dx
        vec_ref[pl.ds(0, 1), :] += dg_pre
        vec_ref[pl.ds(1, 1), :] += jnp.concatenate([dg_q, dg_kv, jnp.zeros((1, D_MODEL - Q_LORA - KV_LORA), F32)], axis=1)

        @pl.when(i == pl.num_programs(0) - 1)
        def _():
            pltpu.sync_copy(dwin_acc, dwin_ref)

    out_shape = (
        jax.ShapeDtypeStruct((s, D_MODEL), F32), jax.ShapeDtypeStruct((D_MODEL, D_EXT), F32),
        jax.ShapeDtypeStruct((Q_LORA, 1024), F32), jax.ShapeDtypeStruct((KV_LORA, 1024), F32),
        jax.ShapeDtypeStruct((KV_LORA, 512), F32), jax.ShapeDtypeStruct((8, D_MODEL), F32),
    )
    return pl.pallas_call(
        body, name="pre_bwd", grid=(s // TM,), out_shape=out_shape,
        in_specs=[_rows(D_MODEL), _rows(D_MODEL), _rows(512), _rows(512), _rows(512), _rows(512), _rows(512),
                  _rows(1024), _rows(1024), _rows(512), _rows(Q_LORA), _rows(KV_LORA), _rows(LANES), _rows(LANES),
                  _rows(LANES), _full((1, D_MODEL)), _full((D_MODEL, D_EXT)), _full((1, Q_LORA)), _full((Q_LORA, 1024)),
                  _full((1, KV_LORA)), _full((KV_LORA, 1024)), _full((KV_LORA, 512))],
        out_specs=(_rows(D_MODEL), pl.BlockSpec(memory_space=pl.ANY), _full((Q_LORA, 1024)), _full((KV_LORA, 1024)),
                   _full((KV_LORA, 512)), _full((8, D_MODEL))),
        scratch_shapes=[pltpu.VMEM((D_MODEL, D_EXT), F32)],
        compiler_params=pltpu.CompilerParams(vmem_limit_bytes=VMEM_DENSE),
    )(x, dxres, dsbq, dsbk, dsbv, dsbg, dmlag, dqc, dkc, dmv, cq, ckv, c_t, sa_t, sb_t, gpre, win, gq, wuq, gkv, wk, wv)


def _place():
    return lax.axis_index("x"), lax.axis_index("y"), lax.axis_index("c")


def _allgather_weights(shards):
    n = len(shards)

    def body(*refs):
        ins, outs, send_sems, recv_sems = refs[:n], refs[n:2 * n], refs[2 * n], refs[2 * n + 1]
        x, y, c = _place()
        me, sib = (x, y, c), (x, y, 1 - c)
        chips = [(1 - x, y), (x, 1 - y), (1 - x, 1 - y)]

        def half(t, chip, hc):
            rows = shards[t].shape[0] // 2
            return outs[t].at[2 * chip[0] + chip[1], pl.ds(pl.multiple_of(hc * rows, 16), rows), :]

        def copy(k, t, chip, hc, to):
            return pltpu.make_async_remote_copy(src_ref=half(t, chip, hc), dst_ref=half(t, chip, hc), send_sem=send_sems.at[k],
                                                recv_sem=recv_sems.at[k], device_id=to, device_id_type=MESH)

        first, passed = [], []
        for t in range(n):
            outs[t][2 * x + y] = ins[t][...].astype(BF16)
            for j, chip in enumerate(chips):
                cp = copy(6 * t + j, t, (x, y), c, (*chip, c))
                cp.start()
                first.append(cp)
        for t in range(n):
            for j, chip in enumerate(chips):
                copy(6 * t + j, t, chip, c, me).wait_recv()
                cp = copy(6 * t + 3 + j, t, chip, c, sib)
                cp.start()
                passed.append(cp)
        for t in range(n):
            for j, chip in enumerate(chips):
                copy(6 * t + 3 + j, t, chip, 1 - c, me).wait_recv()
        for cp in first + passed:
            cp.wait_send()

    return pl.pallas_call(
        body, name="allgather_weights",
        out_shape=tuple(jax.ShapeDtypeStruct((N_SHARD,) + a.shape, BF16) for a in shards),
        in_specs=[pl.BlockSpec(memory_space=pltpu.VMEM)] * n, out_specs=(pl.BlockSpec(memory_space=pltpu.VMEM),) * n,
        scratch_shapes=[pltpu.SemaphoreType.DMA((6 * n,)), pltpu.SemaphoreType.DMA((6 * n,))],
        compiler_params=pltpu.CompilerParams(vmem_limit_bytes=VMEM_ATTN),
    )(*shards)


def _reduce_scatter_grads(gsh, vec):
    n = len(gsh)
    halves = [a.shape[1] // 2 for a in gsh]

    def body(*refs):
        g_refs, vec_ref, f_refs, vsum_ref = refs[:n], refs[n], refs[n + 1:2 * n + 1], refs[2 * n + 1]
        scr = refs[2 * n + 2:]
        accs, sibs, sbufs, rbufs = scr[0:n], scr[n:2 * n], scr[2 * n:3 * n], scr[3 * n:4 * n]
        vrecv, local_sems, send_sems, recv_sems = scr[4 * n:4 * n + 4]
        x, y, c = _place()
        me, sib = (x, y, c), (x, y, 1 - c)
        mine = 2 * x + y
        chips = [(1 - x, y), (x, 1 - y), (1 - x, 1 - y)]

        def remote(k, src, dst, to):
            return pltpu.make_async_remote_copy(src_ref=src, dst_ref=dst, send_sem=send_sems.at[k], recv_sem=recv_sems.at[k],
                                                device_id=to, device_id_type=MESH)

        def half3(ref, t, hc):
            return ref.at[:, pl.ds(pl.multiple_of(hc * halves[t], 8), halves[t]), :]

        def half2(ref, t, hc):
            return ref.at[pl.ds(pl.multiple_of(hc * halves[t], 8), halves[t]), :]

        loads, sends = [], []
        for t in range(n):
            ld = pltpu.make_async_copy(half3(g_refs[t], t, c), accs[t], local_sems.at[t])
            ld.start()
            loads.append(ld)
            cp = remote(t, half3(g_refs[t], t, 1 - c), sibs[t], sib)
            cp.start()
            sends.append(cp)

        my_dev = 4 * x + 2 * y + c
        vrecv[my_dev] = vec_ref[...]
        for k in range(1, 8):
            to = (x ^ ((k >> 2) & 1), y ^ ((k >> 1) & 1), c ^ (k & 1))
            cp = remote(n + k - 1, vec_ref, vrecv.at[my_dev], to)
            cp.start()
            sends.append(cp)

        for t in range(n):
            loads[t].wait()
            remote(t, half3(g_refs[t], t, 1 - c), sibs[t], me).wait_recv()
            for k in range(N_SHARD):
                accs[t][k] = accs[t][k] + sibs[t][k]
            for j, chip in enumerate(chips):
                idx = 2 * chip[0] + chip[1]
                sbufs[t][idx] = accs[t][idx].astype(BF16)
                cp = remote(n + 7 + 3 * t + j, sbufs[t].at[idx], rbufs[t].at[mine], (*chip, c))
                cp.start()
                sends.append(cp)

        for t in range(n):
            total = accs[t][mine]
            for j, chip in enumerate(chips):
                idx = 2 * chip[0] + chip[1]
                remote(n + 7 + 3 * t + j, sbufs[t].at[idx], rbufs[t].at[idx], me).wait_recv()
                total = total + rbufs[t][idx].astype(F32)
            half2(f_refs[t], t, c)[...] = total
            cp = remote(4 * n + 7 + t, half2(f_refs[t], t, c), half2(f_refs[t], t, c), sib)
            cp.start()
            sends.append(cp)
        for t in range(n):
            remote(4 * n + 7 + t, half2(f_refs[t], t, 1 - c), half2(f_refs[t], t, 1 - c), me).wait_recv()

        for k in range(1, 8):
            src_dev = 4 * (x ^ ((k >> 2) & 1)) + 2 * (y ^ ((k >> 1) & 1)) + (c ^ (k & 1))
            remote(n + k - 1, vec_ref, vrecv.at[src_dev], me).wait_recv()
        vs = vrecv[0]
        for d in range(1, 8):
            vs = vs + vrecv[d]
        vsum_ref[...] = vs

        for cp in sends:
            cp.wait_send()

    nsem = 5 * n + 7
    half_shapes = [(N_SHARD, h, a.shape[2]) for h, a in zip(halves, gsh)]
    return pl.pallas_call(
        body, name="reduce_scatter_grads",
        out_shape=tuple(jax.ShapeDtypeStruct(a.shape[1:], F32) for a in gsh) + (jax.ShapeDtypeStruct((VEC_ROWS, 1024), F32),),
        in_specs=[pl.BlockSpec(memory_space=pl.ANY)] * n + [pl.BlockSpec(memory_space=pltpu.VMEM)],
        out_specs=(pl.BlockSpec(memory_space=pltpu.VMEM),) * (n + 1),
        scratch_shapes=([pltpu.VMEM(s_, F32) for s_ in half_shapes] * 2 + [pltpu.VMEM(s_, BF16) for s_ in half_shapes] * 2
                        + [pltpu.VMEM((8, VEC_ROWS, 1024), F32), pltpu.SemaphoreType.DMA((n,)),
                           pltpu.SemaphoreType.DMA((nsem,)), pltpu.SemaphoreType.DMA((nsem,))]),
        compiler_params=pltpu.CompilerParams(vmem_limit_bytes=56 * 1024 * 1024),
    )(*gsh, vec)


def _adamw(w, g, m, v):
    rows, cols = w.shape
    tr = rows if rows <= 256 else 256

    def body(w_ref, g_ref, m_ref, v_ref, d_ref, nm_ref, nv_ref):
        d_ref[...], nm_ref[...], nv_ref[...] = _adam_math(w_ref[...], g_ref[...], m_ref[...], v_ref[...])

    spec = pl.BlockSpec((tr, cols), lambda i: (i, 0))
    shp = jax.ShapeDtypeStruct((rows, cols), F32)
    return pl.pallas_call(body, name="adamw", grid=(rows // tr,), out_shape=(shp, shp, shp),
                          in_specs=[spec] * 4, out_specs=(spec,) * 3)(w, g, m, v)


def _adam_math(w, g, m, v):
    m2 = ADAM_B1 * m + (1.0 - ADAM_B1) * g
    v2 = ADAM_B2 * v + (1.0 - ADAM_B2) * (g * g)
    m_hat = m2 / (1.0 - ADAM_B1 ** ADAM_STEP)
    v_hat = v2 / (1.0 - ADAM_B2 ** ADAM_STEP)
    return -ADAM_LR * (m_hat / (jnp.sqrt(v_hat) + ADAM_EPS) + ADAM_WD * w), m2, v2


def _adamw_small(vsum, w, m, v):
    names = [name for name, _, _, _ in _VEC_LAYOUT]
    k = len(names)

    def body(*refs):
        vs_ref, w_refs, m_refs, v_refs = refs[0], refs[1:1 + k], refs[1 + k:1 + 2 * k], refs[1 + 2 * k:1 + 3 * k]
        outs = refs[1 + 3 * k:]
        for idx, (_, r, c0, width) in enumerate(_VEC_LAYOUT):
            gv = vs_ref[pl.ds(r, 1), pl.ds(c0, width)]
            d, m2, v2 = _adam_math(w_refs[idx][...], gv, m_refs[idx][...], v_refs[idx][...])
            outs[idx][...], outs[k + idx][...], outs[2 * k + idx][...], outs[3 * k + idx][...] = gv, d, m2, v2

    shapes = tuple(jax.ShapeDtypeStruct(w[name].shape, F32) for name in names)
    res = pl.pallas_call(
        body, name="adamw_small", out_shape=shapes * 4,
        in_specs=[pl.BlockSpec(memory_space=pltpu.VMEM)] * (1 + 3 * k), out_specs=(pl.BlockSpec(memory_space=pltpu.VMEM),) * (4 * k),
    )(vsum, *[w[name] for name in names], *[m[name] for name in names], *[v[name] for name in names])
    return tuple({name: res[part * k + idx] for idx, name in enumerate(names)} for part in range(4))


_BIG = ("w_in", "w_uq", "w_ukv", "w_out", "w_ple", "w_ple_gate")
_COL_SHARDED = ("w_in", "w_uq", "w_ukv", "w_ple")


def _join_shards(parts):
    cols = lambda a: a.transpose(1, 0, 2).reshape(a.shape[1], N_SHARD * a.shape[2])
    rows = lambda a: a.reshape(N_SHARD * a.shape[1], a.shape[2])
    return {n: (cols if n in _COL_SHARDED else rows)(parts[n]) for n in _BIG}


def _split_shards(full):
    cols = lambda a: a.reshape(a.shape[0], N_SHARD, a.shape[1] // N_SHARD).transpose(1, 0, 2)
    rows = lambda a: a.reshape(N_SHARD, a.shape[0] // N_SHARD, a.shape[1])
    return {n: (cols if n in _COL_SHARDED else rows)(full[n]) for n in _BIG}


def _extend_weights(w):
    win = w["w_in"]
    zeros = lambda r, c: jnp.zeros((r, c), win.dtype)
    win_ext = jnp.concatenate([win[:, :2432], zeros(D_MODEL, 64), win[:, 2432:2464], zeros(D_MODEL, 32), win[:, 2464:]], axis=1)
    wuq_ext = jnp.pad(w["w_uq"].reshape(Q_LORA, 8, 96), ((0, 0), (0, 0), (0, 32))).reshape(Q_LORA, 1024)
    wukv = w["w_ukv"].reshape(KV_LORA, 8, 128)
    wk_ext = jnp.pad(wukv[:, :, :64], ((0, 0), (0, 0), (0, 64))).reshape(KV_LORA, 1024)
    wv = wukv[:, :, 64:].reshape(KV_LORA, 512)
    return win_ext, wuq_ext, wk_ext, wv


def _contract_grads(dwin_ext, dwuq_ext, dwk_ext, dwv):
    dwin = jnp.concatenate([dwin_ext[:, :2432], dwin_ext[:, 2496:2528], dwin_ext[:, 2560:]], axis=1)
    dwuq = dwuq_ext.reshape(Q_LORA, 8, 128)[:, :, :96].reshape(Q_LORA, 768)
    dwukv = jnp.concatenate([dwk_ext.reshape(KV_LORA, 8, 128)[:, :, :64], dwv.reshape(KV_LORA, 8, 64)], axis=2)
    return dwin, dwuq, dwukv.reshape(KV_LORA, 1024)


def _rope_tables(positions):
    half = QK_ROPE // 2
    freq = ROPE_THETA ** (-jnp.arange(half, dtype=F32) / half)
    ang = positions.astype(F32)[:, None] * freq
    cos, sin = jnp.cos(ang), jnp.sin(ang)
    s = positions.shape[0]
    z = lambda n: jnp.zeros((s, n), F32)
    c_t = jnp.concatenate([jnp.ones((s, 64), F32), cos, cos, z(32)], axis=1)
    sa_t = jnp.concatenate([z(64), -sin, z(16), z(32)], axis=1)
    sb_t = jnp.concatenate([z(64), z(16), sin, z(32)], axis=1)
    return c_t, sa_t, sb_t


def _local_grads(x, p, positions, tgt, gains, wfull):
    win_ext, wuq_ext, wk_ext, wv = _extend_weights(wfull)
    wout, wple, wpg = wfull["w_out"], wfull["w_ple"], wfull["w_ple_gate"]
    tabs = _rope_tables(positions)
    g = gains
    sbq, sbk, sbv, sbg, mlag, cq, ckv, qc, kc, mv = _pre_fwd(x, tabs, g["norm_pre_g"], win_ext, g["q_norm_g"], wuq_ext,
                                                             g["kv_norm_g"], wk_ext, wv)
    sbo = _sb_fwd(sbq, sbk, sbv)
    mlao, lse = _mla_fwd(qc, kc, mv.T)
    dsbo, dmlao, delta, dsbg, dmlag, dxres, dwout, dwpg, dwple, vec_c = _post(
        x, p, tgt, sbo, mlao, sbg, mlag, g["sb_out_norm_g"], g["mla_out_norm_g"], wout, g["norm_post_g"], wple,
        g["ple_norm_g"], wpg, g["b_ple_gate"])
    dsbq, dsbk, dsbv = _sb_bwd(sbq, sbk, sbv, dsbo)
    dqc, dkc, dmv = _mla_bwd(qc, kc, kc.T, mv, dmlao, lse, delta)
    gx, dwin_ext, dwuq_ext, dwk_ext, dwv, vec_d = _pre_bwd(
        x, dxres, dsbq, dsbk, dsbv, dsbg, dmlag, dqc, dkc, dmv, cq, ckv, tabs, g["norm_pre_g"], win_ext, g["q_norm_g"],
        wuq_ext, g["kv_norm_g"], wk_ext, wv)
    dwin, dwuq, dwukv = _contract_grads(dwin_ext, dwuq_ext, dwk_ext, dwv)
    grads = {"w_in": dwin, "w_uq": dwuq, "w_ukv": dwukv, "w_out": dwout, "w_ple": dwple, "w_ple_gate": dwpg}
    return gx, grads, jnp.concatenate([vec_c, vec_d], axis=0)


_VEC_LAYOUT = (("norm_post_g", 0, 0, 1024), ("ple_norm_g", 1, 0, 1024), ("b_ple_gate", 2, 0, 1024), ("sb_out_norm_g", 3, 0, 512),
               ("mla_out_norm_g", 3, 512, 512), ("norm_pre_g", 8, 0, 1024), ("q_norm_g", 9, 0, 256), ("kv_norm_g", 9, 256, 128))
_LOSS_ROW = 4
_WEIGHT_ORDER = ("norm_pre_g", "w_in", "q_norm_g", "w_uq", "kv_norm_g", "w_ukv", "sb_out_norm_g", "mla_out_norm_g", "w_out",
                 "norm_post_g", "w_ple", "ple_norm_g", "w_ple_gate", "b_ple_gate")


def kernel(x, p, positions, norm_pre_g, w_in, q_norm_g, w_uq, kv_norm_g, w_ukv, sb_out_norm_g, mla_out_norm_g, w_out, norm_post_g, w_ple, ple_norm_g, w_ple_gate, b_ple_gate, loss_target, m_norm_pre_g, m_w_in, m_q_norm_g, m_w_uq, m_kv_norm_g, m_w_ukv, m_sb_out_norm_g, m_mla_out_norm_g, m_w_out, m_norm_post_g, m_w_ple, m_ple_norm_g, m_w_ple_gate, m_b_ple_gate, v_norm_pre_g, v_w_in, v_q_norm_g, v_w_uq, v_kv_norm_g, v_w_ukv, v_sb_out_norm_g, v_mla_out_norm_g, v_w_out, v_norm_post_g, v_w_ple, v_ple_norm_g, v_w_ple_gate, v_b_ple_gate):
    w = {"norm_pre_g": norm_pre_g, "w_in": w_in[0], "q_norm_g": q_norm_g, "w_uq": w_uq[0], "kv_norm_g": kv_norm_g, "w_ukv": w_ukv[0],
         "sb_out_norm_g": sb_out_norm_g, "mla_out_norm_g": mla_out_norm_g, "w_out": w_out[0], "norm_post_g": norm_post_g,
         "w_ple": w_ple[0], "ple_norm_g": ple_norm_g, "w_ple_gate": w_ple_gate[0], "b_ple_gate": b_ple_gate}
    m = {"norm_pre_g": m_norm_pre_g, "w_in": m_w_in[0], "q_norm_g": m_q_norm_g, "w_uq": m_w_uq[0], "kv_norm_g": m_kv_norm_g,
         "w_ukv": m_w_ukv[0], "sb_out_norm_g": m_sb_out_norm_g, "mla_out_norm_g": m_mla_out_norm_g, "w_out": m_w_out[0],
         "norm_post_g": m_norm_post_g, "w_ple": m_w_ple[0], "ple_norm_g": m_ple_norm_g, "w_ple_gate": m_w_ple_gate[0],
         "b_ple_gate": m_b_ple_gate}
    v = {"norm_pre_g": v_norm_pre_g, "w_in": v_w_in[0], "q_norm_g": v_q_norm_g, "w_uq": v_w_uq[0], "kv_norm_g": v_kv_norm_g,
         "w_ukv": v_w_ukv[0], "sb_out_norm_g": v_sb_out_norm_g, "mla_out_norm_g": v_mla_out_norm_g, "w_out": v_w_out[0],
         "norm_post_g": v_norm_post_g, "w_ple": v_w_ple[0], "ple_norm_g": v_ple_norm_g, "w_ple_gate": v_w_ple_gate[0],
         "b_ple_gate": v_b_ple_gate}
    gathered = _allgather_weights([w[n] for n in _BIG])
    wfull = _join_shards(dict(zip(_BIG, gathered)))

    gx, grads, vec = _local_grads(x[0], p[0, 0], positions[0], loss_target[0], w, wfull)

    gsh = _split_shards(grads)
    *gred, vsum = _reduce_scatter_grads([gsh[n] for n in _BIG], vec)
    loss = vsum[_LOSS_ROW, 0]

    g, delta, new_m, new_v = _adamw_small(vsum, w, m, v)
    for n, gn in zip(_BIG, gred):
        g[n] = gn
        delta[n], new_m[n], new_v[n] = _adamw(w[n], gn, m[n], v[n])

    lead = lambda n, a: a[None] if n in _BIG else a
    return (loss, gx[None],
            *[lead(n, g[n]) for n in _WEIGHT_ORDER], *[lead(n, delta[n]) for n in _WEIGHT_ORDER],
            *[lead(n, new_m[n]) for n in _WEIGHT_ORDER], *[lead(n, new_v[n]) for n in _WEIGHT_ORDER])
```

```python
import numpy as np
import jax
import jax.numpy as jnp
from jax import lax
from jax.experimental import pallas as pl
from jax.experimental.pallas import tpu as pltpu

F32 = jnp.float32
BF16 = jnp.bfloat16
MESH = pl.DeviceIdType.MESH

D_MODEL = 1024
HEAD_DIM = 64
D_SB = 512
D_MLA = 512
Q_LORA = 256
KV_LORA = 128
QK_NOPE = 64
QK_ROPE = 32
PLE_DIM = 256
D_IN = 2976
D_EXT = 3072
ROPE_THETA = 10000.0
EPS = 1e-6
N_SHARD = 4

ADAM_LR = 0.001
ADAM_B1 = 0.9
ADAM_B2 = 0.999
ADAM_EPS = 1e-08
ADAM_WD = 0.01
ADAM_STEP = 10

LANES = 128
BK = 128
WQ = 256
MQ = 512
SB_CUTOFF = 120.0
TM = 256
VEC_ROWS = 16
VMEM_DENSE = 52 * 1024 * 1024
VMEM_ATTN = 40 * 1024 * 1024


def _mm(a, b):
    return jnp.dot(a, b, preferred_element_type=F32)


def _mm_nt(a, b):
    return lax.dot_general(a, b, (((1,), (1,)), ((), ())), preferred_element_type=F32)


def _mm_tn(a, b):
    return lax.dot_general(a, b, (((0,), (0,)), ((), ())), preferred_element_type=F32)


def _seg(a, bd2):
    return _mm(_split2(a), bd2)


def _const(mask):
    return jnp.asarray(np.asarray(mask, np.float32), dtype=BF16)


def _blockdiag2(n, seg):
    r = (np.arange(2 * n)[:, None] % n) // seg
    c = np.arange(n)[None, :] // seg
    return _const(r == c)


def _sigmoid(a):
    return 1.0 / (1.0 + jnp.exp(-a))


def _rowmean(a):
    return jnp.mean(a, axis=-1, keepdims=True)


def _colsum(a):
    return jnp.sum(a, axis=0, keepdims=True)


def _rope_fwd(a, c, sa, sb):
    w = a.shape[-1]
    return a * c + pltpu.roll(a, w - 16, 1) * sa + pltpu.roll(a, 16, 1) * sb


def _rope_bwd(g, c, sa, sb):
    w = g.shape[-1]
    return g * c + pltpu.roll(g * sa, 16, 1) + pltpu.roll(g * sb, w - 16, 1)


def _full(shape):
    return pl.BlockSpec(shape, lambda *_: (0,) * len(shape))


def _full2(shape):
    return pl.BlockSpec(shape, lambda p, i: (0, 0))


def _rows(width, tm=TM):
    return pl.BlockSpec((tm, width), lambda i: (i, 0))


def _pre_fwd(x, tabs, gpre, win, gq, wuq, gkv, wk, wv):
    s = x.shape[0]
    c_t, sa_t, sb_t = tabs

    def body(x_ref, c_ref, sa_ref, sb_ref, gpre_ref, win_ref, gq_ref, wuq_ref, gkv_ref, wk_ref, wv_ref,
             sbq_ref, sbk_ref, sbv_ref, sbg_ref, mlag_ref, cq_ref, ckv_ref, qc_ref, kc_ref, mv_ref):
        xv = x_ref[...]
        r1 = lax.rsqrt(_rowmean(xv * xv) + EPS)
        h = (xv * r1 * gpre_ref[...]).astype(BF16)
        proj = _mm(h, win_ref[...])
        sbq_ref[...] = proj[:, 0:512].astype(BF16)
        sbk_ref[...] = proj[:, 512:1024].astype(BF16)
        sbv_ref[...] = proj[:, 1024:1536].astype(BF16)
        sbg_ref[...] = proj[:, 1536:2048]
        cq = proj[:, 2048:2304]
        ckv = proj[:, 2304:2432]
        kr = proj[:, 2432:2560]
        mlag_ref[...] = proj[:, 2560:3072]
        cq_ref[...] = cq
        ckv_ref[...] = ckv
        c1, sa1, sb1 = c_ref[...], sa_ref[...], sb_ref[...]
        c8, sa8, sb8 = jnp.tile(c1, (1, 8)), jnp.tile(sa1, (1, 8)), jnp.tile(sb1, (1, 8))
        cqn = (cq * lax.rsqrt(_rowmean(cq * cq) + EPS) * gq_ref[...]).astype(BF16)
        qe = _mm(cqn, wuq_ref[...])
        qc_ref[...] = _rope_fwd(qe, c8, sa8, sb8).astype(BF16)
        ckvn = (ckv * lax.rsqrt(_rowmean(ckv * ckv) + EPS) * gkv_ref[...]).astype(BF16)
        ke = _mm(ckvn, wk_ref[...])
        krr = _rope_fwd(kr, c1, sa1, sb1)
        kc_ref[...] = (ke + jnp.tile(krr, (1, 8))).astype(BF16)
        mv_ref[...] = _mm(ckvn, wv_ref[...]).astype(BF16)

    out_shape = (
        jax.ShapeDtypeStruct((s, 512), BF16), jax.ShapeDtypeStruct((s, 512), BF16), jax.ShapeDtypeStruct((s, 512), BF16),
        jax.ShapeDtypeStruct((s, 512), F32), jax.ShapeDtypeStruct((s, 512), F32),
        jax.ShapeDtypeStruct((s, Q_LORA), F32), jax.ShapeDtypeStruct((s, KV_LORA), F32),
        jax.ShapeDtypeStruct((s, 1024), BF16), jax.ShapeDtypeStruct((s, 1024), BF16), jax.ShapeDtypeStruct((s, 512), BF16),
    )
    return pl.pallas_call(
        body, name="pre_fwd", grid=(s // TM,), out_shape=out_shape,
        in_specs=[_rows(D_MODEL), _rows(LANES), _rows(LANES), _rows(LANES), _full((1, D_MODEL)), _full((D_MODEL, D_EXT)),
                  _full((1, Q_LORA)), _full((Q_LORA, 1024)), _full((1, KV_LORA)), _full((KV_LORA, 1024)), _full((KV_LORA, 512))],
        out_specs=(_rows(512), _rows(512), _rows(512), _rows(512), _rows(512), _rows(Q_LORA), _rows(KV_LORA),
                   _rows(1024), _rows(1024), _rows(512)),
        compiler_params=pltpu.CompilerParams(vmem_limit_bytes=VMEM_DENSE),
    )(x, c_t, sa_t, sb_t, gpre, win, gq, wuq, gkv, wk, wv)


def _softplus(z):
    neg_abs = lax.bitcast_convert_type(lax.bitcast_convert_type(z, jnp.uint32) | jnp.uint32(0x80000000), F32)
    return jnp.maximum(z, 0.0) + jnp.log(1.0 + jnp.exp(neg_abs))


def _sum_matrix(kind, terms):
    r, c = np.arange(2 * BK)[:, None], np.arange(2 * BK * terms)[None, :] % (2 * BK)
    rk, ck = r % BK, c % BK
    return _const(((r // BK) == (c // BK)) & {"suffix": ck >= rk, "prefix": ck <= rk}[kind])


def _split_rows(a):
    hi = a.astype(BF16)
    return jnp.concatenate([hi, (a - hi.astype(F32)).astype(BF16)], axis=0)


def _heads_t(blk, rowi):
    zero = jnp.zeros_like(blk)
    return jnp.concatenate([jnp.where(rowi < 64, blk, zero), jnp.where(rowi >= 64, blk, zero)], axis=1)


def _mask_keys(a, valid, fill=0.0):
    return jnp.concatenate([jnp.where(valid, a[0:BK], fill), jnp.where(valid, a[BK:2 * BK], fill)], axis=0)


def _split2(a):
    hi = a.astype(BF16)
    lo = (a - hi.astype(F32)).astype(BF16)
    return jnp.concatenate([hi, lo], axis=1)


def _pair_stack(b, lane):
    zero = jnp.zeros_like(b)
    return jnp.concatenate([jnp.where(lane < 64, b, zero), jnp.where(lane >= 64, b, zero)], axis=0)


def _sb_fwd(q, k, vt):
    s = q.shape[0]

    def body(q_ref, k_ref, vt_ref, usuf_ref, o_ref, acc_scr, run_scr):
        i = pl.program_id(1)
        lane = lax.broadcasted_iota(jnp.int32, (1, LANES), 1)
        rowi = lax.broadcasted_iota(jnp.int32, (LANES, 1), 0)
        keyi = lax.broadcasted_iota(jnp.int32, (BK, WQ), 0)
        qryi = lax.broadcasted_iota(jnp.int32, (BK, WQ), 1) + i * WQ
        qs = q_ref[...] * (HEAD_DIM ** -0.5)

        def group(blocks, masked):
            starts = [pl.multiple_of(j * BK, BK) for j in blocks]
            valid = [(keyi + j * BK) < qryi if m else None for j, m in zip(blocks, masked)]
            zs = [_mm_nt(_pair_stack(k_ref[pl.ds(ks, BK), :], lane), qs) for ks in starts]
            sps = [_softplus(z) for z in zs]
            sps = [sp if ok is None else _mask_keys(sp, ok) for sp, ok in zip(sps, valid)]
            cums = [_mm(usuf_ref[...], _split_rows(sp)) for sp in sps]
            ws = [jnp.exp(z - c) for z, c in zip(zs, cums)]
            ws = [w if ok is None else _mask_keys(w, ok) for w, ok in zip(ws, valid)]
            pvs = [_mm(_heads_t(vt_ref[:, pl.ds(ks, BK)], rowi), w.astype(BF16)) for ks, w in zip(starts, ws)]
            for pv, c in zip(pvs, cums):
                r0, r1 = run_scr[0:1, :], run_scr[1:2, :]
                acc_scr[...] += jnp.where(rowi < 64, jnp.exp(-r0), jnp.exp(-r1)) * pv
                run_scr[0:1, :] = r0 + c[0:1]
                run_scr[1:2, :] = r1 + c[BK:BK + 1]

        assert WQ == 2 * BK
        acc_scr[...] = jnp.zeros_like(acc_scr)
        run_scr[...] = jnp.zeros_like(run_scr)

        @pl.when(i == 0)
        def _():
            group([1, 0], [True, True])

        @pl.when(i > 0)
        def _():
            group([2 * i + 1, 2 * i, 2 * i - 1, 2 * i - 2], [True, True, False, False])

        def unfinished():
            return (jnp.min(run_scr[0:2, :]) < SB_CUTOFF).astype(jnp.int32)

        def step(c):
            group([2 * i - 1 - 2 * c[0], 2 * i - 2 - 2 * c[0]], [False, False])
            return c[0] + 1, unfinished()

        lax.while_loop(lambda c: (c[0] < i) & (c[1] > 0), step, (jnp.int32(1), unfinished()))
        o_ref[...] = acc_scr[...].T

    qspec = pl.BlockSpec((WQ, LANES), lambda p, i: (i, p))
    kspec = pl.BlockSpec((s, LANES), lambda p, i: (0, p))
    tspec = pl.BlockSpec((LANES, s), lambda p, i: (p, 0))
    return pl.pallas_call(
        body, name="sb_fwd", grid=(4, s // WQ),
        out_shape=jax.ShapeDtypeStruct((s, 512), F32),
        in_specs=[qspec, kspec, tspec, _full2((2 * BK, 4 * BK))], out_specs=qspec,
        scratch_shapes=[pltpu.VMEM((LANES, WQ), F32), pltpu.VMEM((8, WQ), F32)],
        compiler_params=pltpu.CompilerParams(vmem_limit_bytes=VMEM_ATTN),
    )(q, k, vt, _sum_matrix("suffix", 2))


def _sb_bwd(q, k, kt, v, do):
    s = q.shape[0]

    def body(q_ref, k_ref, kt_ref, v_ref, do_ref, usuf_ref, upre_ref, dq_ref, dk_ref, dv_ref, later_scr, dqt_scr, st_scr):
        i = pl.program_id(1)

        @pl.when(i == 0)
        def _():
            dk_ref[...] = jnp.zeros_like(dk_ref)
            dv_ref[...] = jnp.zeros_like(dv_ref)

        lane = lax.broadcasted_iota(jnp.int32, (1, LANES), 1)
        rowi = lax.broadcasted_iota(jnp.int32, (LANES, 1), 0)
        keyi = lax.broadcasted_iota(jnp.int32, (BK, WQ), 0)
        qryi = lax.broadcasted_iota(jnp.int32, (BK, WQ), 1) + i * WQ
        qs = q_ref[...] * (HEAD_DIM ** -0.5)
        dob = do_ref[...]
        dot = dob.astype(F32).T.astype(BF16)

        def scores(j):
            return _mm_nt(_pair_stack(k_ref[pl.ds(pl.multiple_of(j * BK, BK), BK), :], lane), qs)

        def scan(blocks, masked):
            sps = [_softplus(scores(j)) for j in blocks]
            sps = [_mask_keys(sp, (keyi + j * BK) < qryi) if m else sp for sp, j, m in zip(sps, blocks, masked)]
            for sp, j in zip(sps, blocks):
                run = st_scr[0:2, :]
                later_scr[j, 0:2, :] = run
                st_scr[0:2, :] = run + jnp.concatenate([jnp.sum(sp[0:BK], axis=0, keepdims=True),
                                                        jnp.sum(sp[BK:2 * BK], axis=0, keepdims=True)], axis=0)

        def sweep(blocks, masked):
            starts = [pl.multiple_of(j * BK, BK) for j in blocks]
            valid = [(keyi + j * BK) < qryi if m else None for j, m in zip(blocks, masked)]
            zs = [scores(j) for j in blocks]
            us = [jnp.exp(lax.bitcast_convert_type(lax.bitcast_convert_type(z, jnp.uint32) | jnp.uint32(0x80000000), F32))
                  for z in zs]
            sps = [jnp.maximum(z, 0.0) + jnp.log(1.0 + u) for z, u in zip(zs, us)]
            sps = [sp if ok is None else _mask_keys(sp, ok) for sp, ok in zip(sps, valid)]
            sigs = [jnp.where(z >= 0.0, 1.0, u) / (1.0 + u) for z, u in zip(zs, us)]
            cums = [_mm(usuf_ref[...], _split_rows(sp)) for sp in sps]
            dws = [_mm(_pair_stack(v_ref[pl.ds(ks, BK), :], lane), dot) for ks in starts]
            wfs = []
            for z, c, j, ok in zip(zs, cums, blocks, valid):
                f = jnp.exp(-later_scr[j, 0:2, :])
                wf = jnp.exp(z - c) * jnp.concatenate([jnp.broadcast_to(f[0:1], (BK, WQ)), jnp.broadcast_to(f[1:2], (BK, WQ))], axis=0)
                wfs.append(wf if ok is None else _mask_keys(wf, ok))
            es = [dw * wf for dw, wf in zip(dws, wfs)]
            pres = [_mm(upre_ref[...], e.astype(BF16)) for e in es]
            dzs = []
            for e, pre, sig, ok in zip(es, pres, sigs, valid):
                e0 = pre[0:BK] + st_scr[0:1, :]
                e1 = pre[BK:2 * BK] + st_scr[1:2, :]
                st_scr[0:1, :] = e0[BK - 1:BK]
                st_scr[1:2, :] = e1[BK - 1:BK]
                dz = e - sig * jnp.concatenate([e0, e1], axis=0)
                dzs.append((dz if ok is None else _mask_keys(dz, ok)).astype(BF16))
            dqt_scr[...] += _mm(jnp.concatenate([_heads_t(kt_ref[:, pl.ds(ks, BK)], rowi) for ks in starts], axis=1),
                                jnp.concatenate(dzs, axis=0))
            for ks, dz, wf in zip(starts, dzs, wfs):
                rk = _mm(dz, qs)
                dk_ref[pl.ds(ks, BK), :] += jnp.where(lane < 64, rk[0:BK], rk[BK:2 * BK])
                rv = _mm(wf.astype(BF16), dob)
                dv_ref[pl.ds(ks, BK), :] += jnp.where(lane < 64, rv[0:BK], rv[BK:2 * BK])

        assert WQ == 2 * BK
        st_scr[...] = jnp.zeros_like(st_scr)

        @pl.when(i == 0)
        def _():
            scan([1, 0], [True, True])

        @pl.when(i > 0)
        def _():
            scan([2 * i + 1, 2 * i, 2 * i - 1, 2 * i - 2], [True, True, False, False])

        def unfinished():
            return (jnp.min(st_scr[0:2, :]) < SB_CUTOFF).astype(jnp.int32)

        def step(c):
            scan([2 * i - 1 - 2 * c[0], 2 * i - 2 - 2 * c[0]], [False, False])
            return c[0] + 1, unfinished()

        npairs, _ = lax.while_loop(lambda c: (c[0] < i) & (c[1] > 0), step, (jnp.minimum(i, 1), unfinished()))

        st_scr[...] = jnp.zeros_like(st_scr)
        dqt_scr[...] = jnp.zeros_like(dqt_scr)
        first = 2 * (i - npairs)

        def early(t, carry):
            sweep([first + 2 * t, first + 2 * t + 1], [False, False])
            return carry

        lax.fori_loop(0, npairs - 1, early, 0)

        @pl.when(i == 0)
        def _():
            sweep([0, 1], [True, True])

        @pl.when(i > 0)
        def _():
            sweep([2 * i - 2, 2 * i - 1, 2 * i, 2 * i + 1], [False, False, True, True])

        dq_ref[...] = (dqt_scr[...].T * (HEAD_DIM ** -0.5)).astype(BF16)

    qspec = pl.BlockSpec((WQ, LANES), lambda p, i: (i, p))
    kspec = pl.BlockSpec((s, LANES), lambda p, i: (0, p))
    tspec = pl.BlockSpec((LANES, s), lambda p, i: (p, 0))
    return pl.pallas_call(
        body, name="sb_bwd", grid=(4, s // WQ),
        out_shape=(jax.ShapeDtypeStruct((s, 512), BF16), jax.ShapeDtypeStruct((s, 512), F32),
                   jax.ShapeDtypeStruct((s, 512), F32)),
        in_specs=[qspec, kspec, tspec, kspec, qspec, _full2((2 * BK, 4 * BK)), _full2((2 * BK, 2 * BK))],
        out_specs=(qspec, kspec, kspec),
        scratch_shapes=[pltpu.VMEM((s // BK, 8, WQ), F32), pltpu.VMEM((LANES, WQ), F32), pltpu.VMEM((8, WQ), F32)],
        compiler_params=pltpu.CompilerParams(vmem_limit_bytes=VMEM_ATTN),
    )(q, k, kt, v, do, _sum_matrix("suffix", 2), _sum_matrix("prefix", 1))


MLA_SCALE = (QK_NOPE + QK_ROPE) ** -0.5
LOG2E = 1.4426950408889634


def _mla_keys(kb):
    zero = jnp.zeros((BK, LANES), kb.dtype)
    return jnp.concatenate([jnp.concatenate([kb[:, 0:LANES], zero], axis=1),
                            jnp.concatenate([zero, kb[:, LANES:2 * LANES]], axis=1)], axis=0)


def _mla_fwd(qc, kc, vt):
    s = qc.shape[0]
    rows_l = 16

    def body(q_ref, k_ref, vt_ref, o_ref, l_ref, p_scr, ot_scr, st_scr):
        i = pl.program_id(1)
        keyc = lax.broadcasted_iota(jnp.int32, (BK, MQ), 0)
        qryc = (lax.broadcasted_iota(jnp.int32, (BK, MQ), 1) + i * MQ) // 64
        row = lax.broadcasted_iota(jnp.int32, (LANES, 1), 0)
        qw = q_ref[...]
        orow = lax.broadcasted_iota(jnp.int32, (rows_l, 2 * BK), 0)
        ocol = lax.broadcasted_iota(jnp.int32, (rows_l, 2 * BK), 1)
        ones = jnp.where(((orow == 0) & (ocol < BK)) | ((orow == 1) & (ocol >= BK)), 1.0, 0.0).astype(BF16)

        def scores(j):
            ks = pl.multiple_of(j * BK, BK)
            return _mm_nt(_mla_keys(k_ref[pl.ds(ks, BK), :]), qw)

        def values_t(j):
            vtb = vt_ref[:, pl.ds(pl.multiple_of(j * BK, BK), BK)]
            zero = jnp.zeros_like(vtb)
            top = jnp.concatenate([jnp.where(row < 64, vtb, zero), jnp.where(row >= 64, vtb, zero)], axis=1)
            return jnp.concatenate([top, ones], axis=0)

        def softmax(ja, za, zb, masked):
            c = MLA_SCALE * LOG2E
            parts = [za[0:BK] * c, za[BK:2 * BK] * c, zb[0:BK] * c, zb[BK:2 * BK] * c]
            if masked:
                va = ((keyc + ja * BK) // 64) <= qryc
                vb = ((keyc + (ja + 1) * BK) // 64) <= qryc
                parts = [jnp.where(va, parts[0], -1e30), jnp.where(va, parts[1], -1e30),
                         jnp.where(vb, parts[2], -1e30), jnp.where(vb, parts[3], -1e30)]
            m0, m1 = st_scr[0:1, :], st_scr[1:2, :]
            n0 = jnp.maximum(m0, jnp.max(jnp.maximum(parts[0], parts[2]), axis=0, keepdims=True))
            n1 = jnp.maximum(m1, jnp.max(jnp.maximum(parts[1], parts[3]), axis=0, keepdims=True))
            st_scr[2:3, :] = jnp.exp2(m0 - n0)
            st_scr[3:4, :] = jnp.exp2(m1 - n1)
            st_scr[0:1, :] = n0
            st_scr[1:2, :] = n1
            p_scr[...] = jnp.concatenate([jnp.exp2(parts[0] - n0), jnp.exp2(parts[1] - n1),
                                          jnp.exp2(parts[2] - n0), jnp.exp2(parts[3] - n1)], axis=0).astype(BF16)

        def accumulate(ja):
            pv = _mm(jnp.concatenate([values_t(ja), values_t(ja + 1)], axis=1), p_scr[...])
            a = jnp.where(row < 64, st_scr[2:3, :], st_scr[3:4, :])
            ot_scr[0:LANES, :] = a * ot_scr[0:LANES, :] + pv[0:LANES]
            ot_scr[LANES:LANES + 8, :] = st_scr[2:10, :] * ot_scr[LANES:LANES + 8, :] + pv[LANES:LANES + 8]

        def step(n, masked):
            za, zb = scores(2 * n), scores(2 * n + 1)
            accumulate(2 * n - 2)
            softmax(2 * n, za, zb, masked)

        def first(masked):
            softmax(0, scores(0), scores(1), masked)

        st_scr[...] = jnp.concatenate([jnp.full((2, MQ), -1e30, F32), jnp.ones((14, MQ), F32)], axis=0)
        ot_scr[...] = jnp.zeros_like(ot_scr)

        npq = MQ // (2 * BK)

        @pl.when(i == 0)
        def _():
            first(True)
            for d in range(1, npq):
                step(d, True)

        @pl.when(i > 0)
        def _():
            first(False)
            lax.fori_loop(1, npq * i, lambda n, c: (step(n, False), c)[1], 0)
            for d in range(npq):
                step(npq * i + d, True)

        accumulate(2 * (npq * (i + 1) - 1))
        l0, l1 = ot_scr[LANES:LANES + 1, :], ot_scr[LANES + 1:LANES + 2, :]
        o_ref[...] = (ot_scr[0:LANES, :] / jnp.where(row < 64, l0, l1)).T
        l_ref[...] = jnp.where(row < 64, st_scr[0:1, :] + jnp.log2(l0), st_scr[1:2, :] + jnp.log2(l1)).T

    qspec = pl.BlockSpec((MQ, 2 * LANES), lambda p, i: (i, p))
    kspec = pl.BlockSpec((s, 2 * LANES), lambda p, i: (0, p))
    vtspec = pl.BlockSpec((LANES, s), lambda p, i: (p, 0))
    ospec = pl.BlockSpec((MQ, LANES), lambda p, i: (i, p))
    return pl.pallas_call(
        body, name="mla_fwd", grid=(4, s // MQ),
        out_shape=(jax.ShapeDtypeStruct((s, 512), F32), jax.ShapeDtypeStruct((s, 512), F32)),
        in_specs=[qspec, kspec, vtspec], out_specs=(ospec, ospec),
        scratch_shapes=[pltpu.VMEM((4 * BK, MQ), BF16), pltpu.VMEM((LANES + 8, MQ), F32), pltpu.VMEM((16, MQ), F32)],
        compiler_params=pltpu.CompilerParams(vmem_limit_bytes=VMEM_ATTN),
    )(qc, kc, vt)


def _mla_bwd(qc, kc, kct, v, do, lse, delta):
    s = qc.shape[0]

    def body(q_ref, k_ref, kt_ref, v_ref, do_ref, l_ref, d_ref, dq_ref, dk_ref, dv_ref, dqt_scr, p_scr, dz_scr):
        i = pl.program_id(1)

        @pl.when(i == 0)
        def _():
            dk_ref[...] = jnp.zeros_like(dk_ref)
            dv_ref[...] = jnp.zeros_like(dv_ref)

        lane = lax.broadcasted_iota(jnp.int32, (1, LANES), 1)
        keyc = lax.broadcasted_iota(jnp.int32, (BK, MQ), 0)
        qryc = (lax.broadcasted_iota(jnp.int32, (BK, MQ), 1) + i * MQ) // 64
        qw = q_ref[...]
        dob = do_ref[...]
        dost = (dob.astype(F32) * MLA_SCALE).T.astype(BF16)
        lt = l_ref[...].T
        dt = (d_ref[...] * MLA_SCALE).T
        lse0, lse1 = lt[0:1], lt[64:65]
        dl0, dl1 = dt[0:1], dt[64:65]
        dqt_scr[...] = jnp.zeros_like(dqt_scr)

        def products(j):
            ks = pl.multiple_of(j * BK, BK)
            return (_mm_nt(_mla_keys(k_ref[pl.ds(ks, BK), :]), qw), _mm(_pair_stack(v_ref[pl.ds(ks, BK), :], lane), dost))

        def grads(j, slot, zt, dwt, masked):
            zt = zt * (MLA_SCALE * LOG2E)
            p0 = jnp.exp2(zt[0:BK] - lse0)
            p1 = jnp.exp2(zt[BK:2 * BK] - lse1)
            if masked:
                valid = ((keyc + j * BK) // 64) <= qryc
                p0, p1 = jnp.where(valid, p0, 0.0), jnp.where(valid, p1, 0.0)
            p_scr[slot] = jnp.concatenate([p0, p1], axis=0).astype(BF16)
            dz_scr[slot] = jnp.concatenate([p0 * (dwt[0:BK] - dl0), p1 * (dwt[BK:2 * BK] - dl1)], axis=0).astype(BF16)

        def keys_t(ks):
            ktb = kt_ref[:, pl.ds(ks, BK)]
            zero = jnp.zeros((LANES, BK), ktb.dtype)
            return jnp.concatenate([jnp.concatenate([ktb[0:LANES], zero], axis=1),
                                    jnp.concatenate([zero, ktb[LANES:2 * LANES]], axis=1)], axis=0)

        def scatter(ja):
            ksa, ksb = pl.multiple_of(ja * BK, BK), pl.multiple_of((ja + 1) * BK, BK)
            dqt_scr[...] += _mm(jnp.concatenate([keys_t(ksa), keys_t(ksb)], axis=1),
                                jnp.concatenate([dz_scr[0], dz_scr[1]], axis=0))
            for slot, ks in ((0, ksa), (1, ksb)):
                rk = _mm(dz_scr[slot], qw)
                dk_ref[pl.ds(ks, BK), :] += jnp.concatenate([rk[0:BK, 0:LANES], rk[BK:2 * BK, LANES:2 * LANES]], axis=1)
                rv = _mm(p_scr[slot], dob)
                dv_ref[pl.ds(ks, BK), :] += jnp.where(lane < 64, rv[0:BK], rv[BK:2 * BK])

        def step(n, masked):
            za, wa = products(2 * n)
            zb, wb = products(2 * n + 1)
            scatter(2 * n - 2)
            grads(2 * n, 0, za, wa, masked)
            grads(2 * n + 1, 1, zb, wb, masked)

        def first(masked):
            za, wa = products(0)
            zb, wb = products(1)
            grads(0, 0, za, wa, masked)
            grads(1, 1, zb, wb, masked)

        npq = MQ // (2 * BK)

        @pl.when(i == 0)
        def _():
            first(True)
            for d in range(1, npq):
                step(d, True)

        @pl.when(i > 0)
        def _():
            first(False)
            lax.fori_loop(1, npq * i, lambda n, c: (step(n, False), c)[1], 0)
            for d in range(npq):
                step(npq * i + d, True)

        scatter(2 * (npq * (i + 1) - 1))
        dq_ref[...] = dqt_scr[...].T

    qspec = pl.BlockSpec((MQ, 2 * LANES), lambda p, i: (i, p))
    kspec = pl.BlockSpec((s, 2 * LANES), lambda p, i: (0, p))
    ktspec = pl.BlockSpec((2 * LANES, s), lambda p, i: (p, 0))
    vspec = pl.BlockSpec((s, LANES), lambda p, i: (0, p))
    ospec = pl.BlockSpec((MQ, LANES), lambda p, i: (i, p))
    return pl.pallas_call(
        body, name="mla_bwd", grid=(4, s // MQ),
        out_shape=(jax.ShapeDtypeStruct((s, 1024), F32), jax.ShapeDtypeStruct((s, 1024), F32),
                   jax.ShapeDtypeStruct((s, 512), F32)),
        in_specs=[qspec, kspec, ktspec, vspec, ospec, ospec, ospec], out_specs=(qspec, kspec, vspec),
        scratch_shapes=[pltpu.VMEM((2 * LANES, MQ), F32), pltpu.VMEM((2, 2 * BK, MQ), BF16), pltpu.VMEM((2, 2 * BK, MQ), BF16)],
        compiler_params=pltpu.CompilerParams(vmem_limit_bytes=VMEM_ATTN),
    )(qc, kc, kct, v, do, lse, delta)


def _post(x, p, tgt, sbo, mlao, sbg, mlag, gsb, gmla, wout, gpost, wple, gple, wpg, bpg):
    s = x.shape[0]

    def body(x_ref, p_ref, t_ref, sbo_ref, mlao_ref, sbg_ref, mlag_ref, gsb_ref, gmla_ref, wout_ref,
             gpost_ref, wple_ref, gple_ref, wpg_ref, bpg_ref, bd_ref,
             dsbo_ref, dmlao_ref, delta_ref, dsbg_ref, dmlag_ref, dxres_ref, dwout_ref, dwpg_ref, dwple_ref, vec_ref):
        i = pl.program_id(0)

        @pl.when(i == 0)
        def _():
            dwout_ref[...] = jnp.zeros_like(dwout_ref)
            dwpg_ref[...] = jnp.zeros_like(dwpg_ref)
            dwple_ref[...] = jnp.zeros_like(dwple_ref)
            vec_ref[...] = jnp.zeros_like(vec_ref)

        inv_hd = 1.0 / HEAD_DIM

        def head_fwd(o, g, gate):
            r = lax.rsqrt(_seg(o * o, bd_ref[...]) * inv_hd + EPS)
            hat = o * r
            n = hat * g
            sg = _sigmoid(gate)
            return hat, r, n, sg, n * (gate * sg)

        sbo, mlao, sbg_v, mlag_v = sbo_ref[...], mlao_ref[...], sbg_ref[...], mlag_ref[...]
        gsb_v, gmla_v = gsb_ref[...], gmla_ref[...]
        sb_hat, sb_r, sb_n, sb_sg, sb_y = head_fwd(sbo, gsb_v, sbg_v)
        ml_hat, ml_r, ml_n, ml_sg, ml_y = head_fwd(mlao, gmla_v, mlag_v)
        mix = jnp.concatenate([sb_y, ml_y], axis=1).astype(BF16)
        y = _mm(mix, wout_ref[...])
        ry = lax.rsqrt(_rowmean(y * y) + EPS)
        y_hat = y * ry
        gpost_v = gpost_ref[...]
        x1 = x_ref[...] + y_hat * gpost_v
        pb = p_ref[...].astype(BF16)
        pl_ = _mm(pb, wple_ref[...])
        rp = lax.rsqrt(_rowmean(pl_ * pl_) + EPS)
        pl_hat = pl_ * rp
        gple_v = gple_ref[...]
        ple = pl_hat * gple_v
        x1b = x1.astype(BF16)
        gate = _sigmoid(_mm(x1b, wpg_ref[...]) + bpg_ref[...])
        err = x1 + ple * gate - t_ref[...]
        loss = 0.5 * jnp.sum(_rowmean(err * err))
        dout = err * (1.0 / D_MODEL)

        du = dout * ple * gate * (1.0 - gate)
        dub = du.astype(BF16)
        dple = dout * gate
        dx1 = dout + _mm_nt(dub, wpg_ref[...])
        dwpg_ref[...] += _mm_tn(x1b, dub)
        dplh = dple * gple_v
        dpl = rp * (dplh - pl_hat * _rowmean(dplh * pl_hat))
        dwple_ref[...] += _mm_tn(pb, dpl.astype(BF16))
        dxres_ref[...] = dx1
        dyh = dx1 * gpost_v
        dy = ry * (dyh - y_hat * _rowmean(dyh * y_hat))
        dyb = dy.astype(BF16)
        dwout_ref[...] += _mm_tn(mix, dyb)
        dmix = _mm_nt(dyb, wout_ref[...])

        def head_bwd(dyv, hat, r, n, sg, g, gate):
            dn = dyv * (gate * sg)
            dgate = dyv * n * (sg * (1.0 + gate * (1.0 - sg)))
            dhat = dn * g
            do = r * (dhat - hat * (_seg(dhat * hat, bd_ref[...]) * inv_hd))
            return do, dgate, _colsum(dn * hat)

        dsbo, dsbg, dg_sb = head_bwd(dmix[:, 0:512], sb_hat, sb_r, sb_n, sb_sg, gsb_v, sbg_v)
        dmlao, dmlag, dg_ml = head_bwd(dmix[:, 512:1024], ml_hat, ml_r, ml_n, ml_sg, gmla_v, mlag_v)
        dsbo_ref[...] = dsbo.astype(BF16)
        dmlao_ref[...] = dmlao.astype(BF16)
        delta_ref[...] = _seg(dmlao * mlao, bd_ref[...])
        dsbg_ref[...] = dsbg.astype(BF16)
        dmlag_ref[...] = dmlag.astype(BF16)
        vec_ref[pl.ds(0, 1), :] += _colsum(dx1 * y_hat)
        vec_ref[pl.ds(1, 1), :] += _colsum(dple * pl_hat)
        vec_ref[pl.ds(2, 1), :] += _colsum(du)
        vec_ref[pl.ds(3, 1), :] += jnp.concatenate([dg_sb, dg_ml], axis=1)
        vec_ref[pl.ds(4, 1), :] += jnp.full((1, D_MODEL), loss, F32)

    out_shape = (
        jax.ShapeDtypeStruct((s, 512), BF16), jax.ShapeDtypeStruct((s, 512), BF16), jax.ShapeDtypeStruct((s, 512), F32),
        jax.ShapeDtypeStruct((s, 512), BF16), jax.ShapeDtypeStruct((s, 512), BF16), jax.ShapeDtypeStruct((s, D_MODEL), F32),
        jax.ShapeDtypeStruct((D_MODEL, D_MODEL), F32), jax.ShapeDtypeStruct((D_MODEL, D_MODEL), F32),
        jax.ShapeDtypeStruct((PLE_DIM, D_MODEL), F32), jax.ShapeDtypeStruct((8, D_MODEL), F32),
    )
    return pl.pallas_call(
        body, name="post_fwd_bwd", grid=(s // TM,), out_shape=out_shape,
        in_specs=[_rows(D_MODEL), _rows(PLE_DIM), _rows(D_MODEL), _rows(512), _rows(512), _rows(512), _rows(512),
                  _full((1, 512)), _full((1, 512)), _full((D_MODEL, D_MODEL)),
                  _full((1, D_MODEL)), _full((PLE_DIM, D_MODEL)), _full((1, D_MODEL)), _full((D_MODEL, D_MODEL)),
                  _full((1, D_MODEL)), _full((1024, 512))],
        out_specs=(_rows(512), _rows(512), _rows(512), _rows(512), _rows(512), _rows(D_MODEL),
                   _full((D_MODEL, D_MODEL)), _full((D_MODEL, D_MODEL)), _full((PLE_DIM, D_MODEL)), _full((8, D_MODEL))),
        compiler_params=pltpu.CompilerParams(vmem_limit_bytes=VMEM_DENSE),
    )(x, p, tgt, sbo, mlao, sbg, mlag, gsb, gmla, wout, gpost, wple, gple, wpg, bpg, _blockdiag2(512, HEAD_DIM))


def _pre_bwd(x, dxres, dsbq, dsbk, dsbv, dsbg, dmlag, dqc, dkc, dmv, cq, ckv, tabs, gpre, win, gq, wuq, gkv, wk, wv):
    s = x.shape[0]
    c_t, sa_t, sb_t = tabs

    def body(x_ref, dxres_ref, dsbq_ref, dsbk_ref, dsbv_ref, dsbg_ref, dmlag_ref, dqc_ref, dkc_ref, dmv_ref, cq_ref,
             ckv_ref, c_ref, sa_ref, sb_ref, gpre_ref, win_ref, gq_ref, wuq_ref, gkv_ref, wk_ref, wv_ref,
             gx_ref, dwin_ref, dwuq_ref, dwk_ref, dwv_ref, vec_ref, dwin_acc):
        i = pl.program_id(0)

        @pl.when(i == 0)
        def _():
            dwin_acc[...] = jnp.zeros_like(dwin_acc)
            dwuq_ref[...] = jnp.zeros_like(dwuq_ref)
            dwk_ref[...] = jnp.zeros_like(dwk_ref)
            dwv_ref[...] = jnp.zeros_like(dwv_ref)
            vec_ref[...] = jnp.zeros_like(vec_ref)

        lane = lax.broadcasted_iota(jnp.int32, (1, LANES), 1)
        c1, sa1, sb1 = c_ref[...], sa_ref[...], sb_ref[...]
        c8, sa8, sb8 = jnp.tile(c1, (1, 8)), jnp.tile(sa1, (1, 8)), jnp.tile(sb1, (1, 8))

        def norm_bwd(dn, hat, r, g):
            t = dn * g
            return r * (t - hat * _rowmean(t * hat)), _colsum(dn * hat)

        dqeb = _rope_bwd(dqc_ref[...], c8, sa8, sb8).astype(BF16)
        cq = cq_ref[...]
        rq = lax.rsqrt(_rowmean(cq * cq) + EPS)
        cq_hat = cq * rq
        gq_v = gq_ref[...]
        dwuq_ref[...] += _mm_tn((cq_hat * gq_v).astype(BF16), dqeb)
        dcq, dg_q = norm_bwd(_mm_nt(dqeb, wuq_ref[...]), cq_hat, rq, gq_v)

        dkc = dkc_ref[...]
        dkcb = dkc.astype(BF16)
        dmvb = dmv_ref[...].astype(BF16)
        ckv = ckv_ref[...]
        rkv = lax.rsqrt(_rowmean(ckv * ckv) + EPS)
        ckv_hat = ckv * rkv
        gkv_v = gkv_ref[...]
        ckvnb = (ckv_hat * gkv_v).astype(BF16)
        dwk_ref[...] += _mm_tn(ckvnb, dkcb)
        dwv_ref[...] += _mm_tn(ckvnb, dmvb)
        dckv, dg_kv = norm_bwd(_mm_nt(dkcb, wk_ref[...]) + _mm_nt(dmvb, wv_ref[...]), ckv_hat, rkv, gkv_v)

        dkr = dkc[:, 0:LANES]
        for hh in range(1, 8):
            dkr = dkr + dkc[:, LANES * hh:LANES * (hh + 1)]
        dkr = _rope_bwd(dkr, c1, sa1, sb1)
        dkr = jnp.where((lane >= 64) & (lane < 96), dkr, 0.0)

        dproj = jnp.concatenate([dsbq_ref[...], dsbk_ref[...].astype(BF16), dsbv_ref[...].astype(BF16), dsbg_ref[...],
                                 dcq.astype(BF16), dckv.astype(BF16), dkr.astype(BF16), dmlag_ref[...]], axis=1)
        xv = x_ref[...]
        r1 = lax.rsqrt(_rowmean(xv * xv) + EPS)
        x_hat = xv * r1
        gpre_v = gpre_ref[...]
        dwin_acc[...] += _mm_tn((x_hat * gpre_v).astype(BF16), dproj)
        dx, dg_pre = norm_bwd(_mm_nt(dproj, win_ref[...]), x_hat, r1, gpre_v)
        gx_ref[...] = dxres_ref[...] + dx
        vec_ref[pl.ds(0, 1), :] += dg_pre
        vec_ref[pl.ds(1, 1), :] += jnp.concatenate([dg_q, dg_kv, jnp.zeros((1, D_MODEL - Q_LORA - KV_LORA), F32)], axis=1)

        @pl.when(i == pl.num_programs(0) - 1)
        def _():
            pltpu.sync_copy(dwin_acc, dwin_ref)

    out_shape = (
        jax.ShapeDtypeStruct((s, D_MODEL), F32), jax.ShapeDtypeStruct((D_MODEL, D_EXT), F32),
        jax.ShapeDtypeStruct((Q_LORA, 1024), F32), jax.ShapeDtypeStruct((KV_LORA, 1024), F32),
        jax.ShapeDtypeStruct((KV_LORA, 512), F32), jax.ShapeDtypeStruct((8, D_MODEL), F32),
    )
    return pl.pallas_call(
        body, name="pre_bwd", grid=(s // TM,), out_shape=out_shape,
        in_specs=[_rows(D_MODEL), _rows(D_MODEL), _rows(512), _rows(512), _rows(512), _rows(512), _rows(512),
                  _rows(1024), _rows(1024), _rows(512), _rows(Q_LORA), _rows(KV_LORA), _rows(LANES), _rows(LANES),
                  _rows(LANES), _full((1, D_MODEL)), _full((D_MODEL, D_EXT)), _full((1, Q_LORA)), _full((Q_LORA, 1024)),
                  _full((1, KV_LORA)), _full((KV_LORA, 1024)), _full((KV_LORA, 512))],
        out_specs=(_rows(D_MODEL), pl.BlockSpec(memory_space=pl.ANY), _full((Q_LORA, 1024)), _full((KV_LORA, 1024)),
                   _full((KV_LORA, 512)), _full((8, D_MODEL))),
        scratch_shapes=[pltpu.VMEM((D_MODEL, D_EXT), F32)],
        compiler_params=pltpu.CompilerParams(vmem_limit_bytes=VMEM_DENSE),
    )(x, dxres, dsbq, dsbk, dsbv, dsbg, dmlag, dqc, dkc, dmv, cq, ckv, c_t, sa_t, sb_t, gpre, win, gq, wuq, gkv, wk, wv)


def _place():
    return lax.axis_index("x"), lax.axis_index("y"), lax.axis_index("c")


def _allgather_weights(shards):
    n = len(shards)

    def body(*refs):
        ins, outs, send_sems, recv_sems = refs[:n], refs[n:2 * n], refs[2 * n], refs[2 * n + 1]
        x, y, c = _place()
        me, sib = (x, y, c), (x, y, 1 - c)
        chips = [(1 - x, y), (x, 1 - y), (1 - x, 1 - y)]

        def half(t, chip, hc):
            rows = shards[t].shape[0] // 2
            return outs[t].at[2 * chip[0] + chip[1], pl.ds(pl.multiple_of(hc * rows, 16), rows), :]

        def copy(k, t, chip, hc, to):
            return pltpu.make_async_remote_copy(src_ref=half(t, chip, hc), dst_ref=half(t, chip, hc), send_sem=send_sems.at[k],
                                                recv_sem=recv_sems.at[k], device_id=to, device_id_type=MESH)

        first, passed = [], []
        for t in range(n):
            outs[t][2 * x + y] = ins[t][...].astype(BF16)
            for j, chip in enumerate(chips):
                cp = copy(6 * t + j, t, (x, y), c, (*chip, c))
                cp.start()
                first.append(cp)
        for t in range(n):
            for j, chip in enumerate(chips):
                copy(6 * t + j, t, chip, c, me).wait_recv()
                cp = copy(6 * t + 3 + j, t, chip, c, sib)
                cp.start()
                passed.append(cp)
        for t in range(n):
            for j, chip in enumerate(chips):
                copy(6 * t + 3 + j, t, chip, 1 - c, me).wait_recv()
        for cp in first + passed:
            cp.wait_send()

    return pl.pallas_call(
        body, name="allgather_weights",
        out_shape=tuple(jax.ShapeDtypeStruct((N_SHARD,) + a.shape, BF16) for a in shards),
        in_specs=[pl.BlockSpec(memory_space=pltpu.VMEM)] * n, out_specs=(pl.BlockSpec(memory_space=pltpu.VMEM),) * n,
        scratch_shapes=[pltpu.SemaphoreType.DMA((6 * n,)), pltpu.SemaphoreType.DMA((6 * n,))],
        compiler_params=pltpu.CompilerParams(vmem_limit_bytes=VMEM_ATTN),
    )(*shards)


def _reduce_scatter_grads(gsh, vec):
    n = len(gsh)
    halves = [a.shape[1] // 2 for a in gsh]

    def body(*refs):
        g_refs, vec_ref, f_refs, vsum_ref = refs[:n], refs[n], refs[n + 1:2 * n + 1], refs[2 * n + 1]
        scr = refs[2 * n + 2:]
        accs, sibs, sbufs, rbufs = scr[0:n], scr[n:2 * n], scr[2 * n:3 * n], scr[3 * n:4 * n]
        vrecv, local_sems, send_sems, recv_sems = scr[4 * n:4 * n + 4]
        x, y, c = _place()
        me, sib = (x, y, c), (x, y, 1 - c)
        mine = 2 * x + y
        chips = [(1 - x, y), (x, 1 - y), (1 - x, 1 - y)]

        def remote(k, src, dst, to):
            return pltpu.make_async_remote_copy(src_ref=src, dst_ref=dst, send_sem=send_sems.at[k], recv_sem=recv_sems.at[k],
                                                device_id=to, device_id_type=MESH)

        def half3(ref, t, hc):
            return ref.at[:, pl.ds(pl.multiple_of(hc * halves[t], 8), halves[t]), :]

        def half2(ref, t, hc):
            return ref.at[pl.ds(pl.multiple_of(hc * halves[t], 8), halves[t]), :]

        loads, sends = [], []
        for t in range(n):
            ld = pltpu.make_async_copy(half3(g_refs[t], t, c), accs[t], local_sems.at[t])
            ld.start()
            loads.append(ld)
            cp = remote(t, half3(g_refs[t], t, 1 - c), sibs[t], sib)
            cp.start()
            sends.append(cp)

        my_dev = 4 * x + 2 * y + c
        vrecv[my_dev] = vec_ref[...]
        for k in range(1, 8):
            to = (x ^ ((k >> 2) & 1), y ^ ((k >> 1) & 1), c ^ (k & 1))
            cp = remote(n + k - 1, vec_ref, vrecv.at[my_dev], to)
            cp.start()
            sends.append(cp)

        for t in range(n):
            loads[t].wait()
            remote(t, half3(g_refs[t], t, 1 - c), sibs[t], me).wait_recv()
            for k in range(N_SHARD):
                accs[t][k] = accs[t][k] + sibs[t][k]
            for j, chip in enumerate(chips):
                idx = 2 * chip[0] + chip[1]
                sbufs[t][idx] = accs[t][idx].astype(BF16)
                cp = remote(n + 7 + 3 * t + j, sbufs[t].at[idx], rbufs[t].at[mine], (*chip, c))
                cp.start()
                sends.append(cp)

        for t in range(n):
            total = accs[t][mine]
            for j, chip in enumerate(chips):
                idx = 2 * chip[0] + chip[1]
                remote(n + 7 + 3 * t + j, sbufs[t].at[idx], rbufs[t].at[idx], me).wait_recv()
                total = total + rbufs[t][idx].astype(F32)
            half2(f_refs[t], t, c)[...] = total
            cp = remote(4 * n + 7 + t, half2(f_refs[t], t, c), half2(f_refs[t], t, c), sib)
            cp.start()
            sends.append(cp)
        for t in range(n):
            remote(4 * n + 7 + t, half2(f_refs[t], t, 1 - c), half2(f_refs[t], t, 1 - c), me).wait_recv()

        for k in range(1, 8):
            src_dev = 4 * (x ^ ((k >> 2) & 1)) + 2 * (y ^ ((k >> 1) & 1)) + (c ^ (k & 1))
            remote(n + k - 1, vec_ref, vrecv.at[src_dev], me).wait_recv()
        vs = vrecv[0]
        for d in range(1, 8):
            vs = vs + vrecv[d]
        vsum_ref[...] = vs

        for cp in sends:
            cp.wait_send()

    nsem = 5 * n + 7
    half_shapes = [(N_SHARD, h, a.shape[2]) for h, a in zip(halves, gsh)]
    return pl.pallas_call(
        body, name="reduce_scatter_grads",
        out_shape=tuple(jax.ShapeDtypeStruct(a.shape[1:], F32) for a in gsh) + (jax.ShapeDtypeStruct((VEC_ROWS, 1024), F32),),
        in_specs=[pl.BlockSpec(memory_space=pl.ANY)] * n + [pl.BlockSpec(memory_space=pltpu.VMEM)],
        out_specs=(pl.BlockSpec(memory_space=pltpu.VMEM),) * (n + 1),
        scratch_shapes=([pltpu.VMEM(s_, F32) for s_ in half_shapes] * 2 + [pltpu.VMEM(s_, BF16) for s_ in half_shapes] * 2
                        + [pltpu.VMEM((8, VEC_ROWS, 1024), F32), pltpu.SemaphoreType.DMA((n,)),
                           pltpu.SemaphoreType.DMA((nsem,)), pltpu.SemaphoreType.DMA((nsem,))]),
        compiler_params=pltpu.CompilerParams(vmem_limit_bytes=56 * 1024 * 1024),
    )(*gsh, vec)


def _adamw(w, g, m, v):
    rows, cols = w.shape
    tr = rows if rows <= 256 else 256

    def body(w_ref, g_ref, m_ref, v_ref, d_ref, nm_ref, nv_ref):
        d_ref[...], nm_ref[...], nv_ref[...] = _adam_math(w_ref[...], g_ref[...], m_ref[...], v_ref[...])

    spec = pl.BlockSpec((tr, cols), lambda i: (i, 0))
    shp = jax.ShapeDtypeStruct((rows, cols), F32)
    return pl.pallas_call(body, name="adamw", grid=(rows // tr,), out_shape=(shp, shp, shp),
                          in_specs=[spec] * 4, out_specs=(spec,) * 3)(w, g, m, v)


def _adam_math(w, g, m, v):
    m2 = ADAM_B1 * m + (1.0 - ADAM_B1) * g
    v2 = ADAM_B2 * v + (1.0 - ADAM_B2) * (g * g)
    m_hat = m2 / (1.0 - ADAM_B1 ** ADAM_STEP)
    v_hat = v2 / (1.0 - ADAM_B2 ** ADAM_STEP)
    return -ADAM_LR * (m_hat / (jnp.sqrt(v_hat) + ADAM_EPS) + ADAM_WD * w), m2, v2


def _adamw_small(vsum, w, m, v):
    names = [name for name, _, _, _ in _VEC_LAYOUT]
    k = len(names)

    def body(*refs):
        vs_ref, w_refs, m_refs, v_refs = refs[0], refs[1:1 + k], refs[1 + k:1 + 2 * k], refs[1 + 2 * k:1 + 3 * k]
        outs = refs[1 + 3 * k:]
        for idx, (_, r, c0, width) in enumerate(_VEC_LAYOUT):
            gv = vs_ref[pl.ds(r, 1), pl.ds(c0, width)]
            d, m2, v2 = _adam_math(w_refs[idx][...], gv, m_refs[idx][...], v_refs[idx][...])
            outs[idx][...], outs[k + idx][...], outs[2 * k + idx][...], outs[3 * k + idx][...] = gv, d, m2, v2

    shapes = tuple(jax.ShapeDtypeStruct(w[name].shape, F32) for name in names)
    res = pl.pallas_call(
        body, name="adamw_small", out_shape=shapes * 4,
        in_specs=[pl.BlockSpec(memory_space=pltpu.VMEM)] * (1 + 3 * k), out_specs=(pl.BlockSpec(memory_space=pltpu.VMEM),) * (4 * k),
    )(vsum, *[w[name] for name in names], *[m[name] for name in names], *[v[name] for name in names])
    return tuple({name: res[part * k + idx] for idx, name in enumerate(names)} for part in range(4))


_BIG = ("w_in", "w_uq", "w_ukv", "w_out", "w_ple", "w_ple_gate")
_COL_SHARDED = ("w_in", "w_uq", "w_ukv", "w_ple")


def _join_shards(parts):
    cols = lambda a: a.transpose(1, 0, 2).reshape(a.shape[1], N_SHARD * a.shape[2])
    rows = lambda a: a.reshape(N_SHARD * a.shape[1], a.shape[2])
    return {n: (cols if n in _COL_SHARDED else rows)(parts[n]) for n in _BIG}


def _split_shards(full):
    cols = lambda a: a.reshape(a.shape[0], N_SHARD, a.shape[1] // N_SHARD).transpose(1, 0, 2)
    rows = lambda a: a.reshape(N_SHARD, a.shape[0] // N_SHARD, a.shape[1])
    return {n: (cols if n in _COL_SHARDED else rows)(full[n]) for n in _BIG}


def _extend_weights(w):
    win = w["w_in"]
    zeros = lambda r, c: jnp.zeros((r, c), win.dtype)
    win_ext = jnp.concatenate([win[:, :2432], zeros(D_MODEL, 64), win[:, 2432:2464], zeros(D_MODEL, 32), win[:, 2464:]], axis=1)
    wuq_ext = jnp.pad(w["w_uq"].reshape(Q_LORA, 8, 96), ((0, 0), (0, 0), (0, 32))).reshape(Q_LORA, 1024)
    wukv = w["w_ukv"].reshape(KV_LORA, 8, 128)
    wk_ext = jnp.pad(wukv[:, :, :64], ((0, 0), (0, 0), (0, 64))).reshape(KV_LORA, 1024)
    wv = wukv[:, :, 64:].reshape(KV_LORA, 512)
    return win_ext, wuq_ext, wk_ext, wv


def _contract_grads(dwin_ext, dwuq_ext, dwk_ext, dwv):
    dwin = jnp.concatenate([dwin_ext[:, :2432], dwin_ext[:, 2496:2528], dwin_ext[:, 2560:]], axis=1)
    dwuq = dwuq_ext.reshape(Q_LORA, 8, 128)[:, :, :96].reshape(Q_LORA, 768)
    dwukv = jnp.concatenate([dwk_ext.reshape(KV_LORA, 8, 128)[:, :, :64], dwv.reshape(KV_LORA, 8, 64)], axis=2)
    return dwin, dwuq, dwukv.reshape(KV_LORA, 1024)


def _rope_tables(positions):
    half = QK_ROPE // 2
    freq = ROPE_THETA ** (-jnp.arange(half, dtype=F32) / half)
    ang = positions.astype(F32)[:, None] * freq
    cos, sin = jnp.cos(ang), jnp.sin(ang)
    s = positions.shape[0]
    z = lambda n: jnp.zeros((s, n), F32)
    c_t = jnp.concatenate([jnp.ones((s, 64), F32), cos, cos, z(32)], axis=1)
    sa_t = jnp.concatenate([z(64), -sin, z(16), z(32)], axis=1)
    sb_t = jnp.concatenate([z(64), z(16), sin, z(32)], axis=1)
    return c_t, sa_t, sb_t


def _local_grads(x, p, positions, tgt, gains, wfull):
    win_ext, wuq_ext, wk_ext, wv = _extend_weights(wfull)
    wout, wple, wpg = wfull["w_out"], wfull["w_ple"], wfull["w_ple_gate"]
    tabs = _rope_tables(positions)
    g = gains
    sbq, sbk, sbv, sbg, mlag, cq, ckv, qc, kc, mv = _pre_fwd(x, tabs, g["norm_pre_g"], win_ext, g["q_norm_g"], wuq_ext,
                                                             g["kv_norm_g"], wk_ext, wv)
    sbo = _sb_fwd(sbq, sbk, sbv.T)
    mlao, lse = _mla_fwd(qc, kc, mv.T)
    dsbo, dmlao, delta, dsbg, dmlag, dxres, dwout, dwpg, dwple, vec_c = _post(
        x, p, tgt, sbo, mlao, sbg, mlag, g["sb_out_norm_g"], g["mla_out_norm_g"], wout, g["norm_post_g"], wple,
        g["ple_norm_g"], wpg, g["b_ple_gate"])
    dsbq, dsbk, dsbv = _sb_bwd(sbq, sbk, sbk.T, sbv, dsbo)
    dqc, dkc, dmv = _mla_bwd(qc, kc, kc.T, mv, dmlao, lse, delta)
    gx, dwin_ext, dwuq_ext, dwk_ext, dwv, vec_d = _pre_bwd(
        x, dxres, dsbq, dsbk, dsbv, dsbg, dmlag, dqc, dkc, dmv, cq, ckv, tabs, g["norm_pre_g"], win_ext, g["q_norm_g"],
        wuq_ext, g["kv_norm_g"], wk_ext, wv)
    dwin, dwuq, dwukv = _contract_grads(dwin_ext, dwuq_ext, dwk_ext, dwv)
    grads = {"w_in": dwin, "w_uq": dwuq, "w_ukv": dwukv, "w_out": dwout, "w_ple": dwple, "w_ple_gate": dwpg}
    return gx, grads, jnp.concatenate([vec_c, vec_d], axis=0)


_VEC_LAYOUT = (("norm_post_g", 0, 0, 1024), ("ple_norm_g", 1, 0, 1024), ("b_ple_gate", 2, 0, 1024), ("sb_out_norm_g", 3, 0, 512),
               ("mla_out_norm_g", 3, 512, 512), ("norm_pre_g", 8, 0, 1024), ("q_norm_g", 9, 0, 256), ("kv_norm_g", 9, 256, 128))
_LOSS_ROW = 4
_WEIGHT_ORDER = ("norm_pre_g", "w_in", "q_norm_g", "w_uq", "kv_norm_g", "w_ukv", "sb_out_norm_g", "mla_out_norm_g", "w_out",
                 "norm_post_g", "w_ple", "ple_norm_g", "w_ple_gate", "b_ple_gate")


def kernel(x, p, positions, norm_pre_g, w_in, q_norm_g, w_uq, kv_norm_g, w_ukv, sb_out_norm_g, mla_out_norm_g, w_out, norm_post_g, w_ple, ple_norm_g, w_ple_gate, b_ple_gate, loss_target, m_norm_pre_g, m_w_in, m_q_norm_g, m_w_uq, m_kv_norm_g, m_w_ukv, m_sb_out_norm_g, m_mla_out_norm_g, m_w_out, m_norm_post_g, m_w_ple, m_ple_norm_g, m_w_ple_gate, m_b_ple_gate, v_norm_pre_g, v_w_in, v_q_norm_g, v_w_uq, v_kv_norm_g, v_w_ukv, v_sb_out_norm_g, v_mla_out_norm_g, v_w_out, v_norm_post_g, v_w_ple, v_ple_norm_g, v_w_ple_gate, v_b_ple_gate):
    w = {"norm_pre_g": norm_pre_g, "w_in": w_in[0], "q_norm_g": q_norm_g, "w_uq": w_uq[0], "kv_norm_g": kv_norm_g, "w_ukv": w_ukv[0],
         "sb_out_norm_g": sb_out_norm_g, "mla_out_norm_g": mla_out_norm_g, "w_out": w_out[0], "norm_post_g": norm_post_g,
         "w_ple": w_ple[0], "ple_norm_g": ple_norm_g, "w_ple_gate": w_ple_gate[0], "b_ple_gate": b_ple_gate}
    m = {"norm_pre_g": m_norm_pre_g, "w_in": m_w_in[0], "q_norm_g": m_q_norm_g, "w_uq": m_w_uq[0], "kv_norm_g": m_kv_norm_g,
         "w_ukv": m_w_ukv[0], "sb_out_norm_g": m_sb_out_norm_g, "mla_out_norm_g": m_mla_out_norm_g, "w_out": m_w_out[0],
         "norm_post_g": m_norm_post_g, "w_ple": m_w_ple[0], "ple_norm_g": m_ple_norm_g, "w_ple_gate": m_w_ple_gate[0],
         "b_ple_gate": m_b_ple_gate}
    v = {"norm_pre_g": v_norm_pre_g, "w_in": v_w_in[0], "q_norm_g": v_q_norm_g, "w_uq": v_w_uq[0], "kv_norm_g": v_kv_norm_g,
         "w_ukv": v_w_ukv[0], "sb_out_norm_g": v_sb_out_norm_g, "mla_out_norm_g": v_mla_out_norm_g, "w_out": v_w_out[0],
         "norm_post_g": v_norm_post_g, "w_ple": v_w_ple[0], "ple_norm_g": v_ple_norm_g, "w_ple_gate": v_w_ple_gate[0],
         "b_ple_gate": v_b_ple_gate}
    gathered = _allgather_weights([w[n] for n in _BIG])
    wfull = _join_shards(dict(zip(_BIG, gathered)))

    gx, grads, vec = _local_grads(x[0], p[0, 0], positions[0], loss_target[0], w, wfull)

    gsh = _split_shards(grads)
    *gred, vsum = _reduce_scatter_grads([gsh[n] for n in _BIG], vec)
    loss = vsum[_LOSS_ROW, 0]

    g, delta, new_m, new_v = _adamw_small(vsum, w, m, v)
    for n, gn in zip(_BIG, gred):
        g[n] = gn
        delta[n], new_m[n], new_v[n] = _adamw(w[n], gn, m[n], v[n])

    lead = lambda n, a: a[None] if n in _BIG else a
    return (loss, gx[None],
            *[lead(n, g[n]) for n in _WEIGHT_ORDER], *[lead(n, delta[n]) for n in _WEIGHT_ORDER],
            *[lead(n, new_m[n]) for n in _WEIGHT_ORDER], *[lead(n, new_v[n]) for n in _WEIGHT_ORDER])
```

```python
import numpy as np
import jax
import jax.numpy as jnp
from jax import lax
from jax.experimental import pallas as pl
from jax.experimental.pallas import tpu as pltpu

F32 = jnp.float32
BF16 = jnp.bfloat16
MESH = pl.DeviceIdType.MESH

D_MODEL = 1024
HEAD_DIM = 64
D_SB = 512
D_MLA = 512
Q_LORA = 256
KV_LORA = 128
QK_NOPE = 64
QK_ROPE = 32
PLE_DIM = 256
D_IN = 2976
D_EXT = 3072
ROPE_THETA = 10000.0
EPS = 1e-6
N_SHARD = 4

ADAM_LR = 0.001
ADAM_B1 = 0.9
ADAM_B2 = 0.999
ADAM_EPS = 1e-08
ADAM_WD = 0.01
ADAM_STEP = 10

LANES = 128
BK = 128
WQ = 256
MQ = 512
SB_CUTOFF = 120.0
TM = 256
VEC_ROWS = 16
VMEM_DENSE = 52 * 1024 * 1024
VMEM_ATTN = 40 * 1024 * 1024


def _mm(a, b):
    return jnp.dot(a, b, preferred_element_type=F32)


def _mm_nt(a, b):
    return lax.dot_general(a, b, (((1,), (1,)), ((), ())), preferred_element_type=F32)


def _mm_tn(a, b):
    return lax.dot_general(a, b, (((0,), (0,)), ((), ())), preferred_element_type=F32)


def _seg(a, bd2):
    return _mm(_split2(a), bd2)


def _const(mask):
    return jnp.asarray(np.asarray(mask, np.float32), dtype=BF16)


def _blockdiag2(n, seg):
    r = (np.arange(2 * n)[:, None] % n) // seg
    c = np.arange(n)[None, :] // seg
    return _const(r == c)


def _sigmoid(a):
    return 1.0 / (1.0 + jnp.exp(-a))


def _rowmean(a):
    return jnp.mean(a, axis=-1, keepdims=True)


def _colsum(a):
    return jnp.sum(a, axis=0, keepdims=True)


def _rope_fwd(a, c, sa, sb):
    w = a.shape[-1]
    return a * c + pltpu.roll(a, w - 16, 1) * sa + pltpu.roll(a, 16, 1) * sb


def _rope_bwd(g, c, sa, sb):
    w = g.shape[-1]
    return g * c + pltpu.roll(g * sa, 16, 1) + pltpu.roll(g * sb, w - 16, 1)


def _full(shape):
    return pl.BlockSpec(shape, lambda *_: (0,) * len(shape))


def _full2(shape):
    return pl.BlockSpec(shape, lambda p, i: (0, 0))


def _cols(height, tm=TM):
    return pl.BlockSpec((height, tm), lambda i: (0, i))


def _rows(width, tm=TM):
    return pl.BlockSpec((tm, width), lambda i: (i, 0))


def _pre_fwd(x, tabs, gpre, win, gq, wuq, gkv, wk, wv):
    s = x.shape[0]
    c_t, sa_t, sb_t = tabs

    def body(x_ref, c_ref, sa_ref, sb_ref, gpre_ref, win_ref, gq_ref, wuq_ref, gkv_ref, wk_ref, wv_ref,
             sbq_ref, sbk_ref, sbv_ref, sbg_ref, mlag_ref, cq_ref, ckv_ref, qc_ref, kc_ref, mv_ref,
             sbkt_ref, sbvt_ref, kct_ref, mvt_ref):
        xv = x_ref[...]
        r1 = lax.rsqrt(_rowmean(xv * xv) + EPS)
        h = (xv * r1 * gpre_ref[...]).astype(BF16)
        proj = _mm(h, win_ref[...])
        sbq_ref[...] = proj[:, 0:512].astype(BF16)
        sbk_ref[...] = proj[:, 512:1024].astype(BF16)
        sbv_ref[...] = proj[:, 1024:1536].astype(BF16)
        sbkt_ref[...] = proj[:, 512:1024].T.astype(BF16)
        sbvt_ref[...] = proj[:, 1024:1536].T.astype(BF16)
        sbg_ref[...] = proj[:, 1536:2048]
        cq = proj[:, 2048:2304]
        ckv = proj[:, 2304:2432]
        kr = proj[:, 2432:2560]
        mlag_ref[...] = proj[:, 2560:3072]
        cq_ref[...] = cq
        ckv_ref[...] = ckv
        c1, sa1, sb1 = c_ref[...], sa_ref[...], sb_ref[...]
        c8, sa8, sb8 = jnp.tile(c1, (1, 8)), jnp.tile(sa1, (1, 8)), jnp.tile(sb1, (1, 8))
        cqn = (cq * lax.rsqrt(_rowmean(cq * cq) + EPS) * gq_ref[...]).astype(BF16)
        qe = _mm(cqn, wuq_ref[...])
        qc_ref[...] = _rope_fwd(qe, c8, sa8, sb8).astype(BF16)
        ckvn = (ckv * lax.rsqrt(_rowmean(ckv * ckv) + EPS) * gkv_ref[...]).astype(BF16)
        ke = _mm(ckvn, wk_ref[...])
        krr = _rope_fwd(kr, c1, sa1, sb1)
        kcat = ke + jnp.tile(krr, (1, 8))
        kc_ref[...] = kcat.astype(BF16)
        kct_ref[...] = kcat.T.astype(BF16)
        mval = _mm(ckvn, wv_ref[...])
        mv_ref[...] = mval.astype(BF16)
        mvt_ref[...] = mval.T.astype(BF16)

    out_shape = (
        jax.ShapeDtypeStruct((s, 512), BF16), jax.ShapeDtypeStruct((s, 512), BF16), jax.ShapeDtypeStruct((s, 512), BF16),
        jax.ShapeDtypeStruct((s, 512), F32), jax.ShapeDtypeStruct((s, 512), F32),
        jax.ShapeDtypeStruct((s, Q_LORA), F32), jax.ShapeDtypeStruct((s, KV_LORA), F32),
        jax.ShapeDtypeStruct((s, 1024), BF16), jax.ShapeDtypeStruct((s, 1024), BF16), jax.ShapeDtypeStruct((s, 512), BF16),
        jax.ShapeDtypeStruct((512, s), BF16), jax.ShapeDtypeStruct((512, s), BF16), jax.ShapeDtypeStruct((1024, s), BF16),
        jax.ShapeDtypeStruct((512, s), BF16),
    )
    return pl.pallas_call(
        body, name="pre_fwd", grid=(s // TM,), out_shape=out_shape,
        in_specs=[_rows(D_MODEL), _rows(LANES), _rows(LANES), _rows(LANES), _full((1, D_MODEL)), _full((D_MODEL, D_EXT)),
                  _full((1, Q_LORA)), _full((Q_LORA, 1024)), _full((1, KV_LORA)), _full((KV_LORA, 1024)), _full((KV_LORA, 512))],
        out_specs=(_rows(512), _rows(512), _rows(512), _rows(512), _rows(512), _rows(Q_LORA), _rows(KV_LORA),
                   _rows(1024), _rows(1024), _rows(512), _cols(512), _cols(512), _cols(1024), _cols(512)),
        compiler_params=pltpu.CompilerParams(vmem_limit_bytes=VMEM_DENSE),
    )(x, c_t, sa_t, sb_t, gpre, win, gq, wuq, gkv, wk, wv)


def _softplus(z):
    neg_abs = lax.bitcast_convert_type(lax.bitcast_convert_type(z, jnp.uint32) | jnp.uint32(0x80000000), F32)
    return jnp.maximum(z, 0.0) + jnp.log(1.0 + jnp.exp(neg_abs))


def _sum_matrix(kind, terms):
    r, c = np.arange(2 * BK)[:, None], np.arange(2 * BK * terms)[None, :] % (2 * BK)
    rk, ck = r % BK, c % BK
    return _const(((r // BK) == (c // BK)) & {"suffix": ck >= rk, "prefix": ck <= rk}[kind])


def _split_rows(a):
    hi = a.astype(BF16)
    return jnp.concatenate([hi, (a - hi.astype(F32)).astype(BF16)], axis=0)


def _heads_t(blk, rowi):
    zero = jnp.zeros_like(blk)
    return jnp.concatenate([jnp.where(rowi < 64, blk, zero), jnp.where(rowi >= 64, blk, zero)], axis=1)


def _mask_keys(a, valid, fill=0.0):
    return jnp.concatenate([jnp.where(valid, a[0:BK], fill), jnp.where(valid, a[BK:2 * BK], fill)], axis=0)


def _split2(a):
    hi = a.astype(BF16)
    lo = (a - hi.astype(F32)).astype(BF16)
    return jnp.concatenate([hi, lo], axis=1)


def _pair_stack(b, lane):
    zero = jnp.zeros_like(b)
    return jnp.concatenate([jnp.where(lane < 64, b, zero), jnp.where(lane >= 64, b, zero)], axis=0)


def _sb_fwd(q, k, vt):
    s = q.shape[0]

    def body(q_ref, k_ref, vt_ref, usuf_ref, o_ref, acc_scr, run_scr):
        i = pl.program_id(1)
        lane = lax.broadcasted_iota(jnp.int32, (1, LANES), 1)
        rowi = lax.broadcasted_iota(jnp.int32, (LANES, 1), 0)
        keyi = lax.broadcasted_iota(jnp.int32, (BK, WQ), 0)
        qryi = lax.broadcasted_iota(jnp.int32, (BK, WQ), 1) + i * WQ
        qs = q_ref[...] * (HEAD_DIM ** -0.5)

        def group(blocks, masked):
            starts = [pl.multiple_of(j * BK, BK) for j in blocks]
            valid = [(keyi + j * BK) < qryi if m else None for j, m in zip(blocks, masked)]
            zs = [_mm_nt(_pair_stack(k_ref[pl.ds(ks, BK), :], lane), qs) for ks in starts]
            sps = [_softplus(z) for z in zs]
            sps = [sp if ok is None else _mask_keys(sp, ok) for sp, ok in zip(sps, valid)]
            cums = [_mm(usuf_ref[...], _split_rows(sp)) for sp in sps]
            ws = [jnp.exp(z - c) for z, c in zip(zs, cums)]
            ws = [w if ok is None else _mask_keys(w, ok) for w, ok in zip(ws, valid)]
            pvs = [_mm(_heads_t(vt_ref[:, pl.ds(ks, BK)], rowi), w.astype(BF16)) for ks, w in zip(starts, ws)]
            for pv, c in zip(pvs, cums):
                r0, r1 = run_scr[0:1, :], run_scr[1:2, :]
                acc_scr[...] += jnp.where(rowi < 64, jnp.exp(-r0), jnp.exp(-r1)) * pv
                run_scr[0:1, :] = r0 + c[0:1]
                run_scr[1:2, :] = r1 + c[BK:BK + 1]

        assert WQ == 2 * BK
        acc_scr[...] = jnp.zeros_like(acc_scr)
        run_scr[...] = jnp.zeros_like(run_scr)

        @pl.when(i == 0)
        def _():
            group([1, 0], [True, True])

        @pl.when(i > 0)
        def _():
            group([2 * i + 1, 2 * i, 2 * i - 1, 2 * i - 2], [True, True, False, False])

        def unfinished():
            return (jnp.min(run_scr[0:2, :]) < SB_CUTOFF).astype(jnp.int32)

        def step(c):
            group([2 * i - 1 - 2 * c[0], 2 * i - 2 - 2 * c[0]], [False, False])
            return c[0] + 1, unfinished()

        lax.while_loop(lambda c: (c[0] < i) & (c[1] > 0), step, (jnp.int32(1), unfinished()))
        o_ref[...] = acc_scr[...].T

    qspec = pl.BlockSpec((WQ, LANES), lambda p, i: (i, p))
    kspec = pl.BlockSpec((s, LANES), lambda p, i: (0, p))
    tspec = pl.BlockSpec((LANES, s), lambda p, i: (p, 0))
    return pl.pallas_call(
        body, name="sb_fwd", grid=(4, s // WQ),
        out_shape=jax.ShapeDtypeStruct((s, 512), F32),
        in_specs=[qspec, kspec, tspec, _full2((2 * BK, 4 * BK))], out_specs=qspec,
        scratch_shapes=[pltpu.VMEM((LANES, WQ), F32), pltpu.VMEM((8, WQ), F32)],
        compiler_params=pltpu.CompilerParams(vmem_limit_bytes=VMEM_ATTN),
    )(q, k, vt, _sum_matrix("suffix", 2))


def _sb_bwd(q, k, kt, v, do):
    s = q.shape[0]

    def body(q_ref, k_ref, kt_ref, v_ref, do_ref, usuf_ref, upre_ref, dq_ref, dk_ref, dv_ref, later_scr, dqt_scr, st_scr):
        i = pl.program_id(1)

        @pl.when(i == 0)
        def _():
            dk_ref[...] = jnp.zeros_like(dk_ref)
            dv_ref[...] = jnp.zeros_like(dv_ref)

        lane = lax.broadcasted_iota(jnp.int32, (1, LANES), 1)
        rowi = lax.broadcasted_iota(jnp.int32, (LANES, 1), 0)
        keyi = lax.broadcasted_iota(jnp.int32, (BK, WQ), 0)
        qryi = lax.broadcasted_iota(jnp.int32, (BK, WQ), 1) + i * WQ
        qs = q_ref[...] * (HEAD_DIM ** -0.5)
        dob = do_ref[...]
        dot = dob.astype(F32).T.astype(BF16)

        def scores(j):
            return _mm_nt(_pair_stack(k_ref[pl.ds(pl.multiple_of(j * BK, BK), BK), :], lane), qs)

        def scan(blocks, masked):
            sps = [_softplus(scores(j)) for j in blocks]
            sps = [_mask_keys(sp, (keyi + j * BK) < qryi) if m else sp for sp, j, m in zip(sps, blocks, masked)]
            for sp, j in zip(sps, blocks):
                run = st_scr[0:2, :]
                later_scr[j, 0:2, :] = run
                st_scr[0:2, :] = run + jnp.concatenate([jnp.sum(sp[0:BK], axis=0, keepdims=True),
                                                        jnp.sum(sp[BK:2 * BK], axis=0, keepdims=True)], axis=0)

        def sweep(blocks, masked):
            starts = [pl.multiple_of(j * BK, BK) for j in blocks]
            valid = [(keyi + j * BK) < qryi if m else None for j, m in zip(blocks, masked)]
            zs = [scores(j) for j in blocks]
            us = [jnp.exp(lax.bitcast_convert_type(lax.bitcast_convert_type(z, jnp.uint32) | jnp.uint32(0x80000000), F32))
                  for z in zs]
            sps = [jnp.maximum(z, 0.0) + jnp.log(1.0 + u) for z, u in zip(zs, us)]
            sps = [sp if ok is None else _mask_keys(sp, ok) for sp, ok in zip(sps, valid)]
            sigs = [jnp.where(z >= 0.0, 1.0, u) / (1.0 + u) for z, u in zip(zs, us)]
            cums = [_mm(usuf_ref[...], _split_rows(sp)) for sp in sps]
            dws = [_mm(_pair_stack(v_ref[pl.ds(ks, BK), :], lane), dot) for ks in starts]
            wfs = []
            for z, c, j, ok in zip(zs, cums, blocks, valid):
                f = jnp.exp(-later_scr[j, 0:2, :])
                wf = jnp.exp(z - c) * jnp.concatenate([jnp.broadcast_to(f[0:1], (BK, WQ)), jnp.broadcast_to(f[1:2], (BK, WQ))], axis=0)
                wfs.append(wf if ok is None else _mask_keys(wf, ok))
            es = [dw * wf for dw, wf in zip(dws, wfs)]
            pres = [_mm(upre_ref[...], e.astype(BF16)) for e in es]
            dzs = []
            for e, pre, sig, ok in zip(es, pres, sigs, valid):
                e0 = pre[0:BK] + st_scr[0:1, :]
                e1 = pre[BK:2 * BK] + st_scr[1:2, :]
                st_scr[0:1, :] = e0[BK - 1:BK]
                st_scr[1:2, :] = e1[BK - 1:BK]
                dz = e - sig * jnp.concatenate([e0, e1], axis=0)
                dzs.append((dz if ok is None else _mask_keys(dz, ok)).astype(BF16))
            dqt_scr[...] += _mm(jnp.concatenate([_heads_t(kt_ref[:, pl.ds(ks, BK)], rowi) for ks in starts], axis=1),
                                jnp.concatenate(dzs, axis=0))
            for ks, dz, wf in zip(starts, dzs, wfs):
                rk = _mm(dz, qs)
                dk_ref[pl.ds(ks, BK), :] += jnp.where(lane < 64, rk[0:BK], rk[BK:2 * BK])
                rv = _mm(wf.astype(BF16), dob)
                dv_ref[pl.ds(ks, BK), :] += jnp.where(lane < 64, rv[0:BK], rv[BK:2 * BK])

        assert WQ == 2 * BK
        st_scr[...] = jnp.zeros_like(st_scr)

        @pl.when(i == 0)
        def _():
            scan([1, 0], [True, True])

        @pl.when(i > 0)
        def _():
            scan([2 * i + 1, 2 * i, 2 * i - 1, 2 * i - 2], [True, True, False, False])

        def unfinished():
            return (jnp.min(st_scr[0:2, :]) < SB_CUTOFF).astype(jnp.int32)

        def step(c):
            scan([2 * i - 1 - 2 * c[0], 2 * i - 2 - 2 * c[0]], [False, False])
            return c[0] + 1, unfinished()

        npairs, _ = lax.while_loop(lambda c: (c[0] < i) & (c[1] > 0), step, (jnp.minimum(i, 1), unfinished()))

        st_scr[...] = jnp.zeros_like(st_scr)
        dqt_scr[...] = jnp.zeros_like(dqt_scr)
        first = 2 * (i - npairs)

        def early(t, carry):
            sweep([first + 2 * t, first + 2 * t + 1], [False, False])
            return carry

        lax.fori_loop(0, npairs - 1, early, 0)

        @pl.when(i == 0)
        def _():
            sweep([0, 1], [True, True])

        @pl.when(i > 0)
        def _():
            sweep([2 * i - 2, 2 * i - 1, 2 * i, 2 * i + 1], [False, False, True, True])

        dq_ref[...] = (dqt_scr[...].T * (HEAD_DIM ** -0.5)).astype(BF16)

    qspec = pl.BlockSpec((WQ, LANES), lambda p, i: (i, p))
    kspec = pl.BlockSpec((s, LANES), lambda p, i: (0, p))
    tspec = pl.BlockSpec((LANES, s), lambda p, i: (p, 0))
    return pl.pallas_call(
        body, name="sb_bwd", grid=(4, s // WQ),
        out_shape=(jax.ShapeDtypeStruct((s, 512), BF16), jax.ShapeDtypeStruct((s, 512), F32),
                   jax.ShapeDtypeStruct((s, 512), F32)),
        in_specs=[qspec, kspec, tspec, kspec, qspec, _full2((2 * BK, 4 * BK)), _full2((2 * BK, 2 * BK))],
        out_specs=(qspec, kspec, kspec),
        scratch_shapes=[pltpu.VMEM((s // BK, 8, WQ), F32), pltpu.VMEM((LANES, WQ), F32), pltpu.VMEM((8, WQ), F32)],
        compiler_params=pltpu.CompilerParams(vmem_limit_bytes=VMEM_ATTN),
    )(q, k, kt, v, do, _sum_matrix("suffix", 2), _sum_matrix("prefix", 1))


MLA_SCALE = (QK_NOPE + QK_ROPE) ** -0.5
LOG2E = 1.4426950408889634


def _mla_keys(kb):
    zero = jnp.zeros((BK, LANES), kb.dtype)
    return jnp.concatenate([jnp.concatenate([kb[:, 0:LANES], zero], axis=1),
                            jnp.concatenate([zero, kb[:, LANES:2 * LANES]], axis=1)], axis=0)


def _mla_fwd(qc, kc, vt):
    s = qc.shape[0]
    rows_l = 16

    def body(q_ref, k_ref, vt_ref, o_ref, l_ref, p_scr, ot_scr, st_scr):
        i = pl.program_id(1)
        keyc = lax.broadcasted_iota(jnp.int32, (BK, MQ), 0)
        qryc = (lax.broadcasted_iota(jnp.int32, (BK, MQ), 1) + i * MQ) // 64
        row = lax.broadcasted_iota(jnp.int32, (LANES, 1), 0)
        qw = q_ref[...]
        orow = lax.broadcasted_iota(jnp.int32, (rows_l, 2 * BK), 0)
        ocol = lax.broadcasted_iota(jnp.int32, (rows_l, 2 * BK), 1)
        ones = jnp.where(((orow == 0) & (ocol < BK)) | ((orow == 1) & (ocol >= BK)), 1.0, 0.0).astype(BF16)

        def scores(j):
            ks = pl.multiple_of(j * BK, BK)
            return _mm_nt(_mla_keys(k_ref[pl.ds(ks, BK), :]), qw)

        def values_t(j):
            vtb = vt_ref[:, pl.ds(pl.multiple_of(j * BK, BK), BK)]
            zero = jnp.zeros_like(vtb)
            top = jnp.concatenate([jnp.where(row < 64, vtb, zero), jnp.where(row >= 64, vtb, zero)], axis=1)
            return jnp.concatenate([top, ones], axis=0)

        def softmax(ja, za, zb, masked):
            c = MLA_SCALE * LOG2E
            parts = [za[0:BK] * c, za[BK:2 * BK] * c, zb[0:BK] * c, zb[BK:2 * BK] * c]
            if masked:
                va = ((keyc + ja * BK) // 64) <= qryc
                vb = ((keyc + (ja + 1) * BK) // 64) <= qryc
                parts = [jnp.where(va, parts[0], -1e30), jnp.where(va, parts[1], -1e30),
                         jnp.where(vb, parts[2], -1e30), jnp.where(vb, parts[3], -1e30)]
            m0, m1 = st_scr[0:1, :], st_scr[1:2, :]
            n0 = jnp.maximum(m0, jnp.max(jnp.maximum(parts[0], parts[2]), axis=0, keepdims=True))
            n1 = jnp.maximum(m1, jnp.max(jnp.maximum(parts[1], parts[3]), axis=0, keepdims=True))
            st_scr[2:3, :] = jnp.exp2(m0 - n0)
            st_scr[3:4, :] = jnp.exp2(m1 - n1)
            st_scr[0:1, :] = n0
            st_scr[1:2, :] = n1
            p_scr[...] = jnp.concatenate([jnp.exp2(parts[0] - n0), jnp.exp2(parts[1] - n1),
                                          jnp.exp2(parts[2] - n0), jnp.exp2(parts[3] - n1)], axis=0).astype(BF16)

        def accumulate(ja):
            pv = _mm(jnp.concatenate([values_t(ja), values_t(ja + 1)], axis=1), p_scr[...])
            a = jnp.where(row < 64, st_scr[2:3, :], st_scr[3:4, :])
            ot_scr[0:LANES, :] = a * ot_scr[0:LANES, :] + pv[0:LANES]
            ot_scr[LANES:LANES + 8, :] = st_scr[2:10, :] * ot_scr[LANES:LANES + 8, :] + pv[LANES:LANES + 8]

        def step(n, masked):
            za, zb = scores(2 * n), scores(2 * n + 1)
            accumulate(2 * n - 2)
            softmax(2 * n, za, zb, masked)

        def first(masked):
            softmax(0, scores(0), scores(1), masked)

        st_scr[...] = jnp.concatenate([jnp.full((2, MQ), -1e30, F32), jnp.ones((14, MQ), F32)], axis=0)
        ot_scr[...] = jnp.zeros_like(ot_scr)

        npq = MQ // (2 * BK)

        @pl.when(i == 0)
        def _():
            first(True)
            for d in range(1, npq):
                step(d, True)

        @pl.when(i > 0)
        def _():
            first(False)
            lax.fori_loop(1, npq * i, lambda n, c: (step(n, False), c)[1], 0)
            for d in range(npq):
                step(npq * i + d, True)

        accumulate(2 * (npq * (i + 1) - 1))
        l0, l1 = ot_scr[LANES:LANES + 1, :], ot_scr[LANES + 1:LANES + 2, :]
        o_ref[...] = (ot_scr[0:LANES, :] / jnp.where(row < 64, l0, l1)).T
        l_ref[...] = jnp.where(row < 64, st_scr[0:1, :] + jnp.log2(l0), st_scr[1:2, :] + jnp.log2(l1)).T

    qspec = pl.BlockSpec((MQ, 2 * LANES), lambda p, i: (i, p))
    kspec = pl.BlockSpec((s, 2 * LANES), lambda p, i: (0, p))
    vtspec = pl.BlockSpec((LANES, s), lambda p, i: (p, 0))
    ospec = pl.BlockSpec((MQ, LANES), lambda p, i: (i, p))
    return pl.pallas_call(
        body, name="mla_fwd", grid=(4, s // MQ),
        out_shape=(jax.ShapeDtypeStruct((s, 512), F32), jax.ShapeDtypeStruct((s, 512), F32)),
        in_specs=[qspec, kspec, vtspec], out_specs=(ospec, ospec),
        scratch_shapes=[pltpu.VMEM((4 * BK, MQ), BF16), pltpu.VMEM((LANES + 8, MQ), F32), pltpu.VMEM((16, MQ), F32)],
        compiler_params=pltpu.CompilerParams(vmem_limit_bytes=VMEM_ATTN),
    )(qc, kc, vt)


def _mla_bwd(qc, kc, kct, v, do, lse, delta):
    s = qc.shape[0]

    def body(q_ref, k_ref, kt_ref, v_ref, do_ref, l_ref, d_ref, dq_ref, dk_ref, dv_ref, dqt_scr, p_scr, dz_scr):
        i = pl.program_id(1)

        @pl.when(i == 0)
        def _():
            dk_ref[...] = jnp.zeros_like(dk_ref)
            dv_ref[...] = jnp.zeros_like(dv_ref)

        lane = lax.broadcasted_iota(jnp.int32, (1, LANES), 1)
        keyc = lax.broadcasted_iota(jnp.int32, (BK, MQ), 0)
        qryc = (lax.broadcasted_iota(jnp.int32, (BK, MQ), 1) + i * MQ) // 64
        qw = q_ref[...]
        dob = do_ref[...]
        dost = (dob.astype(F32) * MLA_SCALE).T.astype(BF16)
        lt = l_ref[...].T
        dt = (d_ref[...] * MLA_SCALE).T
        lse0, lse1 = lt[0:1], lt[64:65]
        dl0, dl1 = dt[0:1], dt[64:65]
        dqt_scr[...] = jnp.zeros_like(dqt_scr)

        def products(j):
            ks = pl.multiple_of(j * BK, BK)
            return (_mm_nt(_mla_keys(k_ref[pl.ds(ks, BK), :]), qw), _mm(_pair_stack(v_ref[pl.ds(ks, BK), :], lane), dost))

        def grads(j, slot, zt, dwt, masked):
            zt = zt * (MLA_SCALE * LOG2E)
            p0 = jnp.exp2(zt[0:BK] - lse0)
            p1 = jnp.exp2(zt[BK:2 * BK] - lse1)
            if masked:
                valid = ((keyc + j * BK) // 64) <= qryc
                p0, p1 = jnp.where(valid, p0, 0.0), jnp.where(valid, p1, 0.0)
            p_scr[slot] = jnp.concatenate([p0, p1], axis=0).astype(BF16)
            dz_scr[slot] = jnp.concatenate([p0 * (dwt[0:BK] - dl0), p1 * (dwt[BK:2 * BK] - dl1)], axis=0).astype(BF16)

        def keys_t(ks):
            ktb = kt_ref[:, pl.ds(ks, BK)]
            zero = jnp.zeros((LANES, BK), ktb.dtype)
            return jnp.concatenate([jnp.concatenate([ktb[0:LANES], zero], axis=1),
                                    jnp.concatenate([zero, ktb[LANES:2 * LANES]], axis=1)], axis=0)

        def scatter(ja):
            ksa, ksb = pl.multiple_of(ja * BK, BK), pl.multiple_of((ja + 1) * BK, BK)
            dqt_scr[...] += _mm(jnp.concatenate([keys_t(ksa), keys_t(ksb)], axis=1),
                                jnp.concatenate([dz_scr[0], dz_scr[1]], axis=0))
            for slot, ks in ((0, ksa), (1, ksb)):
                rk = _mm(dz_scr[slot], qw)
                dk_ref[pl.ds(ks, BK), :] += jnp.concatenate([rk[0:BK, 0:LANES], rk[BK:2 * BK, LANES:2 * LANES]], axis=1)
                rv = _mm(p_scr[slot], dob)
                dv_ref[pl.ds(ks, BK), :] += jnp.where(lane < 64, rv[0:BK], rv[BK:2 * BK])

        def step(n, masked):
            za, wa = products(2 * n)
            zb, wb = products(2 * n + 1)
            scatter(2 * n - 2)
            grads(2 * n, 0, za, wa, masked)
            grads(2 * n + 1, 1, zb, wb, masked)

        def first(masked):
            za, wa = products(0)
            zb, wb = products(1)
            grads(0, 0, za, wa, masked)
            grads(1, 1, zb, wb, masked)

        npq = MQ // (2 * BK)

        @pl.when(i == 0)
        def _():
            first(True)
            for d in range(1, npq):
                step(d, True)

        @pl.when(i > 0)
        def _():
            first(False)
            lax.fori_loop(1, npq * i, lambda n, c: (step(n, False), c)[1], 0)
            for d in range(npq):
                step(npq * i + d, True)

        scatter(2 * (npq * (i + 1) - 1))
        dq_ref[...] = dqt_scr[...].T

    qspec = pl.BlockSpec((MQ, 2 * LANES), lambda p, i: (i, p))
    kspec = pl.BlockSpec((s, 2 * LANES), lambda p, i: (0, p))
    ktspec = pl.BlockSpec((2 * LANES, s), lambda p, i: (p, 0))
    vspec = pl.BlockSpec((s, LANES), lambda p, i: (0, p))
    ospec = pl.BlockSpec((MQ, LANES), lambda p, i: (i, p))
    return pl.pallas_call(
        body, name="mla_bwd", grid=(4, s // MQ),
        out_shape=(jax.ShapeDtypeStruct((s, 1024), F32), jax.ShapeDtypeStruct((s, 1024), F32),
                   jax.ShapeDtypeStruct((s, 512), F32)),
        in_specs=[qspec, kspec, ktspec, vspec, ospec, ospec, ospec], out_specs=(qspec, kspec, vspec),
        scratch_shapes=[pltpu.VMEM((2 * LANES, MQ), F32), pltpu.VMEM((2, 2 * BK, MQ), BF16), pltpu.VMEM((2, 2 * BK, MQ), BF16)],
        compiler_params=pltpu.CompilerParams(vmem_limit_bytes=VMEM_ATTN),
    )(qc, kc, kct, v, do, lse, delta)


def _post(x, p, tgt, sbo, mlao, sbg, mlag, gsb, gmla, wout, gpost, wple, gple, wpg, bpg):
    s = x.shape[0]

    def body(x_ref, p_ref, t_ref, sbo_ref, mlao_ref, sbg_ref, mlag_ref, gsb_ref, gmla_ref, wout_ref,
             gpost_ref, wple_ref, gple_ref, wpg_ref, bpg_ref, bd_ref,
             dsbo_ref, dmlao_ref, delta_ref, dsbg_ref, dmlag_ref, dxres_ref, dwout_ref, dwpg_ref, dwple_ref, vec_ref):
        i = pl.program_id(0)

        @pl.when(i == 0)
        def _():
            dwout_ref[...] = jnp.zeros_like(dwout_ref)
            dwpg_ref[...] = jnp.zeros_like(dwpg_ref)
            dwple_ref[...] = jnp.zeros_like(dwple_ref)
            vec_ref[...] = jnp.zeros_like(vec_ref)

        inv_hd = 1.0 / HEAD_DIM

        def head_fwd(o, g, gate):
            r = lax.rsqrt(_seg(o * o, bd_ref[...]) * inv_hd + EPS)
            hat = o * r
            n = hat * g
            sg = _sigmoid(gate)
            return hat, r, n, sg, n * (gate * sg)

        sbo, mlao, sbg_v, mlag_v = sbo_ref[...], mlao_ref[...], sbg_ref[...], mlag_ref[...]
        gsb_v, gmla_v = gsb_ref[...], gmla_ref[...]
        sb_hat, sb_r, sb_n, sb_sg, sb_y = head_fwd(sbo, gsb_v, sbg_v)
        ml_hat, ml_r, ml_n, ml_sg, ml_y = head_fwd(mlao, gmla_v, mlag_v)
        mix = jnp.concatenate([sb_y, ml_y], axis=1).astype(BF16)
        y = _mm(mix, wout_ref[...])
        ry = lax.rsqrt(_rowmean(y * y) + EPS)
        y_hat = y * ry
        gpost_v = gpost_ref[...]
        x1 = x_ref[...] + y_hat * gpost_v
        pb = p_ref[...].astype(BF16)
        pl_ = _mm(pb, wple_ref[...])
        rp = lax.rsqrt(_rowmean(pl_ * pl_) + EPS)
        pl_hat = pl_ * rp
        gple_v = gple_ref[...]
        ple = pl_hat * gple_v
        x1b = x1.astype(BF16)
        gate = _sigmoid(_mm(x1b, wpg_ref[...]) + bpg_ref[...])
        err = x1 + ple * gate - t_ref[...]
        loss = 0.5 * jnp.sum(_rowmean(err * err))
        dout = err * (1.0 / D_MODEL)

        du = dout * ple * gate * (1.0 - gate)
        dub = du.astype(BF16)
        dple = dout * gate
        dx1 = dout + _mm_nt(dub, wpg_ref[...])
        dwpg_ref[...] += _mm_tn(x1b, dub)
        dplh = dple * gple_v
        dpl = rp * (dplh - pl_hat * _rowmean(dplh * pl_hat))
        dwple_ref[...] += _mm_tn(pb, dpl.astype(BF16))
        dxres_ref[...] = dx1
        dyh = dx1 * gpost_v
        dy = ry * (dyh - y_hat * _rowmean(dyh * y_hat))
        dyb = dy.astype(BF16)
        dwout_ref[...] += _mm_tn(mix, dyb)
        dmix = _mm_nt(dyb, wout_ref[...])

        def head_bwd(dyv, hat, r, n, sg, g, gate):
            dn = dyv * (gate * sg)
            dgate = dyv * n * (sg * (1.0 + gate * (1.0 - sg)))
            dhat = dn * g
            do = r * (dhat - hat * (_seg(dhat * hat, bd_ref[...]) * inv_hd))
            return do, dgate, _colsum(dn * hat)

        dsbo, dsbg, dg_sb = head_bwd(dmix[:, 0:512], sb_hat, sb_r, sb_n, sb_sg, gsb_v, sbg_v)
        dmlao, dmlag, dg_ml = head_bwd(dmix[:, 512:1024], ml_hat, ml_r, ml_n, ml_sg, gmla_v, mlag_v)
        dsbo_ref[...] = dsbo.astype(BF16)
        dmlao_ref[...] = dmlao.astype(BF16)
        delta_ref[...] = _seg(dmlao * mlao, bd_ref[...])
        dsbg_ref[...] = dsbg.astype(BF16)
        dmlag_ref[...] = dmlag.astype(BF16)
        vec_ref[pl.ds(0, 1), :] += _colsum(dx1 * y_hat)
        vec_ref[pl.ds(1, 1), :] += _colsum(dple * pl_hat)
        vec_ref[pl.ds(2, 1), :] += _colsum(du)
        vec_ref[pl.ds(3, 1), :] += jnp.concatenate([dg_sb, dg_ml], axis=1)
        vec_ref[pl.ds(4, 1), :] += jnp.full((1, D_MODEL), loss, F32)

    out_shape = (
        jax.ShapeDtypeStruct((s, 512), BF16), jax.ShapeDtypeStruct((s, 512), BF16), jax.ShapeDtypeStruct((s, 512), F32),
        jax.ShapeDtypeStruct((s, 512), BF16), jax.ShapeDtypeStruct((s, 512), BF16), jax.ShapeDtypeStruct((s, D_MODEL), F32),
        jax.ShapeDtypeStruct((D_MODEL, D_MODEL), F32), jax.ShapeDtypeStruct((D_MODEL, D_MODEL), F32),
        jax.ShapeDtypeStruct((PLE_DIM, D_MODEL), F32), jax.ShapeDtypeStruct((8, D_MODEL), F32),
    )
    return pl.pallas_call(
        body, name="post_fwd_bwd", grid=(s // TM,), out_shape=out_shape,
        in_specs=[_rows(D_MODEL), _rows(PLE_DIM), _rows(D_MODEL), _rows(512), _rows(512), _rows(512), _rows(512),
                  _full((1, 512)), _full((1, 512)), _full((D_MODEL, D_MODEL)),
                  _full((1, D_MODEL)), _full((PLE_DIM, D_MODEL)), _full((1, D_MODEL)), _full((D_MODEL, D_MODEL)),
                  _full((1, D_MODEL)), _full((1024, 512))],
        out_specs=(_rows(512), _rows(512), _rows(512), _rows(512), _rows(512), _rows(D_MODEL),
                   _full((D_MODEL, D_MODEL)), _full((D_MODEL, D_MODEL)), _full((PLE_DIM, D_MODEL)), _full((8, D_MODEL))),
        compiler_params=pltpu.CompilerParams(vmem_limit_bytes=VMEM_DENSE),
    )(x, p, tgt, sbo, mlao, sbg, mlag, gsb, gmla, wout, gpost, wple, gple, wpg, bpg, _blockdiag2(512, HEAD_DIM))


def _pre_bwd(x, dxres, dsbq, dsbk, dsbv, dsbg, dmlag, dqc, dkc, dmv, cq, ckv, tabs, gpre, win, gq, wuq, gkv, wk, wv):
    s = x.shape[0]
    c_t, sa_t, sb_t = tabs

    def body(x_ref, dxres_ref, dsbq_ref, dsbk_ref, dsbv_ref, dsbg_ref, dmlag_ref, dqc_ref, dkc_ref, dmv_ref, cq_ref,
             ckv_ref, c_ref, sa_ref, sb_ref, gpre_ref, win_ref, gq_ref, wuq_ref, gkv_ref, wk_ref, wv_ref,
             gx_ref, dwin_ref, dwuq_ref, dwk_ref, dwv_ref, vec_ref, dwin_acc):
        i = pl.program_id(0)

        @pl.when(i == 0)
        def _():
            dwin_acc[...] = jnp.zeros_like(dwin_acc)
            dwuq_ref[...] = jnp.zeros_like(dwuq_ref)
            dwk_ref[...] = jnp.zeros_like(dwk_ref)
            dwv_ref[...] = jnp.zeros_like(dwv_ref)
            vec_ref[...] = jnp.zeros_like(vec_ref)

        lane = lax.broadcasted_iota(jnp.int32, (1, LANES), 1)
        c1, sa1, sb1 = c_ref[...], sa_ref[...], sb_ref[...]
        c8, sa8, sb8 = jnp.tile(c1, (1, 8)), jnp.tile(sa1, (1, 8)), jnp.tile(sb1, (1, 8))

        def norm_bwd(dn, hat, r, g):
            t = dn * g
            return r * (t - hat * _rowmean(t * hat)), _colsum(dn * hat)

        dqeb = _rope_bwd(dqc_ref[...], c8, sa8, sb8).astype(BF16)
        cq = cq_ref[...]
        rq = lax.rsqrt(_rowmean(cq * cq) + EPS)
        cq_hat = cq * rq
        gq_v = gq_ref[...]
        dwuq_ref[...] += _mm_tn((cq_hat * gq_v).astype(BF16), dqeb)
        dcq, dg_q = norm_bwd(_mm_nt(dqeb, wuq_ref[...]), cq_hat, rq, gq_v)

        dkc = dkc_ref[...]
        dkcb = dkc.astype(BF16)
        dmvb = dmv_ref[...].astype(BF16)
        ckv = ckv_ref[...]
        rkv = lax.rsqrt(_rowmean(ckv * ckv) + EPS)
        ckv_hat = ckv * rkv
        gkv_v = gkv_ref[...]
        ckvnb = (ckv_hat * gkv_v).astype(BF16)
        dwk_ref[...] += _mm_tn(ckvnb, dkcb)
        dwv_ref[...] += _mm_tn(ckvnb, dmvb)
        dckv, dg_kv = norm_bwd(_mm_nt(dkcb, wk_ref[...]) + _mm_nt(dmvb, wv_ref[...]), ckv_hat, rkv, gkv_v)

        dkr = dkc[:, 0:LANES]
        for hh in range(1, 8):
            dkr = dkr + dkc[:, LANES * hh:LANES * (hh + 1)]
        dkr = _rope_bwd(dkr, c1, sa1, sb1)
        dkr = jnp.where((lane >= 64) & (lane < 96), dkr, 0.0)

        dproj = jnp.concatenate([dsbq_ref[...], dsbk_ref[...].astype(BF16), dsbv_ref[...].astype(BF16), dsbg_ref[...],
                                 dcq.astype(BF16), dckv.astype(BF16), dkr.astype(BF16), dmlag_ref[...]], axis=1)
        xv = x_ref[...]
        r1 = lax.rsqrt(_rowmean(xv * xv) + EPS)
        x_hat = xv * r1
        gpre_v = gpre_ref[...]
        dwin_acc[...] += _mm_tn((x_hat * gpre_v).astype(BF16), dproj)
        dx, dg_pre = norm_bwd(_mm_nt(dproj, win_ref[...]), x_hat, r1, gpre_v)
        gx_ref[...] = dxres_ref[...] + dx
        vec_ref[pl.ds(0, 1), :] += dg_pre
        vec_ref[pl.ds(1, 1), :] += jnp.concatenate([dg_q, dg_kv, jnp.zeros((1, D_MODEL - Q_LORA - KV_LORA), F32)], axis=1)

        @pl.when(i == pl.num_programs(0) - 1)
        def _():
            pltpu.sync_copy(dwin_acc, dwin_ref)

    out_shape = (
        jax.ShapeDtypeStruct((s, D_MODEL), F32), jax.ShapeDtypeStruct((D_MODEL, D_EXT), F32),
        jax.ShapeDtypeStruct((Q_LORA, 1024), F32), jax.ShapeDtypeStruct((KV_LORA, 1024), F32),
        jax.ShapeDtypeStruct((KV_LORA, 512), F32), jax.ShapeDtypeStruct((8, D_MODEL), F32),
    )
    return pl.pallas_call(
        body, name="pre_bwd", grid=(s // TM,), out_shape=out_shape,
        in_specs=[_rows(D_MODEL), _rows(D_MODEL), _rows(512), _rows(512), _rows(512), _rows(512), _rows(512),
                  _rows(1024), _rows(1024), _rows(512), _rows(Q_LORA), _rows(KV_LORA), _rows(LANES), _rows(LANES),
                  _rows(LANES), _full((1, D_MODEL)), _full((D_MODEL, D_EXT)), _full((1, Q_LORA)), _full((Q_LORA, 1024)),
                  _full((1, KV_LORA)), _full((KV_LORA, 1024)), _full((KV_LORA, 512))],
        out_specs=(_rows(D_MODEL), pl.BlockSpec(memory_space=pl.ANY), _full((Q_LORA, 1024)), _full((KV_LORA, 1024)),
                   _full((KV_LORA, 512)), _full((8, D_MODEL))),
        scratch_shapes=[pltpu.VMEM((D_MODEL, D_EXT), F32)],
        compiler_params=pltpu.CompilerParams(vmem_limit_bytes=VMEM_DENSE),
    )(x, dxres, dsbq, dsbk, dsbv, dsbg, dmlag, dqc, dkc, dmv, cq, ckv, c_t, sa_t, sb_t, gpre, win, gq, wuq, gkv, wk, wv)


def _place():
    return lax.axis_index("x"), lax.axis_index("y"), lax.axis_index("c")


def _allgather_weights(shards):
    n = len(shards)

    def body(*refs):
        ins, outs, send_sems, recv_sems = refs[:n], refs[n:2 * n], refs[2 * n], refs[2 * n + 1]
        x, y, c = _place()
        me, sib = (x, y, c), (x, y, 1 - c)
        chips = [(1 - x, y), (x, 1 - y), (1 - x, 1 - y)]

        def half(t, chip, hc):
            rows = shards[t].shape[0] // 2
            return outs[t].at[2 * chip[0] + chip[1], pl.ds(pl.multiple_of(hc * rows, 16), rows), :]

        def copy(k, t, chip, hc, to):
            return pltpu.make_async_remote_copy(src_ref=half(t, chip, hc), dst_ref=half(t, chip, hc), send_sem=send_sems.at[k],
                                                recv_sem=recv_sems.at[k], device_id=to, device_id_type=MESH)

        first, passed = [], []
        for t in range(n):
            outs[t][2 * x + y] = ins[t][...].astype(BF16)
            for j, chip in enumerate(chips):
                cp = copy(6 * t + j, t, (x, y), c, (*chip, c))
                cp.start()
                first.append(cp)
        for t in range(n):
            for j, chip in enumerate(chips):
                copy(6 * t + j, t, chip, c, me).wait_recv()
                cp = copy(6 * t + 3 + j, t, chip, c, sib)
                cp.start()
                passed.append(cp)
        for t in range(n):
            for j, chip in enumerate(chips):
                copy(6 * t + 3 + j, t, chip, 1 - c, me).wait_recv()
        for cp in first + passed:
            cp.wait_send()

    return pl.pallas_call(
        body, name="allgather_weights",
        out_shape=tuple(jax.ShapeDtypeStruct((N_SHARD,) + a.shape, BF16) for a in shards),
        in_specs=[pl.BlockSpec(memory_space=pltpu.VMEM)] * n, out_specs=(pl.BlockSpec(memory_space=pltpu.VMEM),) * n,
        scratch_shapes=[pltpu.SemaphoreType.DMA((6 * n,)), pltpu.SemaphoreType.DMA((6 * n,))],
        compiler_params=pltpu.CompilerParams(vmem_limit_bytes=VMEM_ATTN),
    )(*shards)


def _reduce_scatter_grads(gsh, vec):
    n = len(gsh)
    halves = [a.shape[1] // 2 for a in gsh]

    def body(*refs):
        g_refs, vec_ref, f_refs, vsum_ref = refs[:n], refs[n], refs[n + 1:2 * n + 1], refs[2 * n + 1]
        scr = refs[2 * n + 2:]
        accs, sibs, sbufs, rbufs = scr[0:n], scr[n:2 * n], scr[2 * n:3 * n], scr[3 * n:4 * n]
        vrecv, local_sems, send_sems, recv_sems = scr[4 * n:4 * n + 4]
        x, y, c = _place()
        me, sib = (x, y, c), (x, y, 1 - c)
        mine = 2 * x + y
        chips = [(1 - x, y), (x, 1 - y), (1 - x, 1 - y)]

        def remote(k, src, dst, to):
            return pltpu.make_async_remote_copy(src_ref=src, dst_ref=dst, send_sem=send_sems.at[k], recv_sem=recv_sems.at[k],
                                                device_id=to, device_id_type=MESH)

        def half3(ref, t, hc):
            return ref.at[:, pl.ds(pl.multiple_of(hc * halves[t], 8), halves[t]), :]

        def half2(ref, t, hc):
            return ref.at[pl.ds(pl.multiple_of(hc * halves[t], 8), halves[t]), :]

        loads, sends = [], []
        for t in range(n):
            ld = pltpu.make_async_copy(half3(g_refs[t], t, c), accs[t], local_sems.at[t])
            ld.start()
            loads.append(ld)
            cp = remote(t, half3(g_refs[t], t, 1 - c), sibs[t], sib)
            cp.start()
            sends.append(cp)

        my_dev = 4 * x + 2 * y + c
        vrecv[my_dev] = vec_ref[...]
        for k in range(1, 8):
            to = (x ^ ((k >> 2) & 1), y ^ ((k >> 1) & 1), c ^ (k & 1))
            cp = remote(n + k - 1, vec_ref, vrecv.at[my_dev], to)
            cp.start()
            sends.append(cp)

        for t in range(n):
            loads[t].wait()
            remote(t, half3(g_refs[t], t, 1 - c), sibs[t], me).wait_recv()
            for k in range(N_SHARD):
                accs[t][k] = accs[t][k] + sibs[t][k]
            for j, chip in enumerate(chips):
                idx = 2 * chip[0] + chip[1]
                sbufs[t][idx] = accs[t][idx].astype(BF16)
                cp = remote(n + 7 + 3 * t + j, sbufs[t].at[idx], rbufs[t].at[mine], (*chip, c))
                cp.start()
                sends.append(cp)

        for t in range(n):
            total = accs[t][mine]
            for j, chip in enumerate(chips):
                idx = 2 * chip[0] + chip[1]
                remote(n + 7 + 3 * t + j, sbufs[t].at[idx], rbufs[t].at[idx], me).wait_recv()
                total = total + rbufs[t][idx].astype(F32)
            half2(f_refs[t], t, c)[...] = total
            cp = remote(4 * n + 7 + t, half2(f_refs[t], t, c), half2(f_refs[t], t, c), sib)
            cp.start()
            sends.append(cp)
        for t in range(n):
            remote(4 * n + 7 + t, half2(f_refs[t], t, 1 - c), half2(f_refs[t], t, 1 - c), me).wait_recv()

        for k in range(1, 8):
            src_dev = 4 * (x ^ ((k >> 2) & 1)) + 2 * (y ^ ((k >> 1) & 1)) + (c ^ (k & 1))
            remote(n + k - 1, vec_ref, vrecv.at[src_dev], me).wait_recv()
        vs = vrecv[0]
        for d in range(1, 8):
            vs = vs + vrecv[d]
        vsum_ref[...] = vs

        for cp in sends:
            cp.wait_send()

    nsem = 5 * n + 7
    half_shapes = [(N_SHARD, h, a.shape[2]) for h, a in zip(halves, gsh)]
    return pl.pallas_call(
        body, name="reduce_scatter_grads",
        out_shape=tuple(jax.ShapeDtypeStruct(a.shape[1:], F32) for a in gsh) + (jax.ShapeDtypeStruct((VEC_ROWS, 1024), F32),),
        in_specs=[pl.BlockSpec(memory_space=pl.ANY)] * n + [pl.BlockSpec(memory_space=pltpu.VMEM)],
        out_specs=(pl.BlockSpec(memory_space=pltpu.VMEM),) * (n + 1),
        scratch_shapes=([pltpu.VMEM(s_, F32) for s_ in half_shapes] * 2 + [pltpu.VMEM(s_, BF16) for s_ in half_shapes] * 2
                        + [pltpu.VMEM((8, VEC_ROWS, 1024), F32), pltpu.SemaphoreType.DMA((n,)),
                           pltpu.SemaphoreType.DMA((nsem,)), pltpu.SemaphoreType.DMA((nsem,))]),
        compiler_params=pltpu.CompilerParams(vmem_limit_bytes=56 * 1024 * 1024),
    )(*gsh, vec)


def _adamw(w, g, m, v):
    rows, cols = w.shape
    tr = rows if rows <= 256 else 256

    def body(w_ref, g_ref, m_ref, v_ref, d_ref, nm_ref, nv_ref):
        d_ref[...], nm_ref[...], nv_ref[...] = _adam_math(w_ref[...], g_ref[...], m_ref[...], v_ref[...])

    spec = pl.BlockSpec((tr, cols), lambda i: (i, 0))
    shp = jax.ShapeDtypeStruct((rows, cols), F32)
    return pl.pallas_call(body, name="adamw", grid=(rows // tr,), out_shape=(shp, shp, shp),
                          in_specs=[spec] * 4, out_specs=(spec,) * 3)(w, g, m, v)


def _adam_math(w, g, m, v):
    m2 = ADAM_B1 * m + (1.0 - ADAM_B1) * g
    v2 = ADAM_B2 * v + (1.0 - ADAM_B2) * (g * g)
    m_hat = m2 / (1.0 - ADAM_B1 ** ADAM_STEP)
    v_hat = v2 / (1.0 - ADAM_B2 ** ADAM_STEP)
    return -ADAM_LR * (m_hat / (jnp.sqrt(v_hat) + ADAM_EPS) + ADAM_WD * w), m2, v2


def _adamw_small(vsum, w, m, v):
    names = [name for name, _, _, _ in _VEC_LAYOUT]
    k = len(names)

    def body(*refs):
        vs_ref, w_refs, m_refs, v_refs = refs[0], refs[1:1 + k], refs[1 + k:1 + 2 * k], refs[1 + 2 * k:1 + 3 * k]
        outs = refs[1 + 3 * k:]
        for idx, (_, r, c0, width) in enumerate(_VEC_LAYOUT):
            gv = vs_ref[pl.ds(r, 1), pl.ds(c0, width)]
            d, m2, v2 = _adam_math(w_refs[idx][...], gv, m_refs[idx][...], v_refs[idx][...])
            outs[idx][...], outs[k + idx][...], outs[2 * k + idx][...], outs[3 * k + idx][...] = gv, d, m2, v2

    shapes = tuple(jax.ShapeDtypeStruct(w[name].shape, F32) for name in names)
    res = pl.pallas_call(
        body, name="adamw_small", out_shape=shapes * 4,
        in_specs=[pl.BlockSpec(memory_space=pltpu.VMEM)] * (1 + 3 * k), out_specs=(pl.BlockSpec(memory_space=pltpu.VMEM),) * (4 * k),
    )(vsum, *[w[name] for name in names], *[m[name] for name in names], *[v[name] for name in names])
    return tuple({name: res[part * k + idx] for idx, name in enumerate(names)} for part in range(4))


_BIG = ("w_in", "w_uq", "w_ukv", "w_out", "w_ple", "w_ple_gate")
_KR_LOCAL = 2432 - 3 * (D_IN // N_SHARD)


def _extend_weights(parts):
    cols = lambda a: a.transpose(1, 0, 2).reshape(a.shape[1], N_SHARD * a.shape[2])
    rows = lambda a: a.reshape(N_SHARD * a.shape[1], a.shape[2])
    g = parts["w_in"]
    zeros = lambda n: jnp.zeros((D_MODEL, n), g.dtype)
    win_ext = jnp.concatenate([g[0], g[1], g[2], g[3][:, :_KR_LOCAL], zeros(64), g[3][:, _KR_LOCAL:_KR_LOCAL + QK_ROPE],
                               zeros(32), g[3][:, _KR_LOCAL + QK_ROPE:]], axis=1)
    wuq_ext = jnp.pad(cols(parts["w_uq"]).reshape(Q_LORA, 8, 96), ((0, 0), (0, 0), (0, 32))).reshape(Q_LORA, 1024)
    wukv = cols(parts["w_ukv"]).reshape(KV_LORA, 8, 128)
    wk_ext = jnp.pad(wukv[:, :, :64], ((0, 0), (0, 0), (0, 64))).reshape(KV_LORA, 1024)
    wv = wukv[:, :, 64:].reshape(KV_LORA, 512)
    return win_ext, wuq_ext, wk_ext, wv, rows(parts["w_out"]), cols(parts["w_ple"]), rows(parts["w_ple_gate"])


def _shard_grads(dwin_ext, dwuq_ext, dwk_ext, dwv, dwout, dwple, dwpg):
    cols = lambda a: a.reshape(a.shape[0], N_SHARD, a.shape[1] // N_SHARD).transpose(1, 0, 2)
    rows = lambda a: a.reshape(N_SHARD, a.shape[0] // N_SHARD, a.shape[1])
    e, w = dwin_ext, D_IN // N_SHARD
    last = jnp.concatenate([e[:, 3 * w:2432], e[:, 2496:2528], e[:, 2560:]], axis=1)
    dwuq = dwuq_ext.reshape(Q_LORA, 8, 128)[:, :, :96].reshape(Q_LORA, 768)
    dwukv = jnp.concatenate([dwk_ext.reshape(KV_LORA, 8, 128)[:, :, :64], dwv.reshape(KV_LORA, 8, 64)], axis=2)
    return {"w_in": jnp.stack([e[:, 0:w], e[:, w:2 * w], e[:, 2 * w:3 * w], last]), "w_uq": cols(dwuq),
            "w_ukv": cols(dwukv.reshape(KV_LORA, 1024)), "w_out": rows(dwout), "w_ple": cols(dwple), "w_ple_gate": rows(dwpg)}


def _rope_tables(positions):
    half = QK_ROPE // 2
    freq = ROPE_THETA ** (-jnp.arange(half, dtype=F32) / half)
    ang = positions.astype(F32)[:, None] * freq
    cos, sin = jnp.cos(ang), jnp.sin(ang)
    s = positions.shape[0]
    z = lambda n: jnp.zeros((s, n), F32)
    c_t = jnp.concatenate([jnp.ones((s, 64), F32), cos, cos, z(32)], axis=1)
    sa_t = jnp.concatenate([z(64), -sin, z(16), z(32)], axis=1)
    sb_t = jnp.concatenate([z(64), z(16), sin, z(32)], axis=1)
    return c_t, sa_t, sb_t


def _local_grads(x, p, positions, tgt, gains, parts):
    win_ext, wuq_ext, wk_ext, wv, wout, wple, wpg = _extend_weights(parts)
    tabs = _rope_tables(positions)
    g = gains
    sbq, sbk, sbv, sbg, mlag, cq, ckv, qc, kc, mv, sbkt, sbvt, kct, mvt = _pre_fwd(
        x, tabs, g["norm_pre_g"], win_ext, g["q_norm_g"], wuq_ext, g["kv_norm_g"], wk_ext, wv)
    sbo = _sb_fwd(sbq, sbk, sbvt)
    mlao, lse = _mla_fwd(qc, kc, mvt)
    dsbo, dmlao, delta, dsbg, dmlag, dxres, dwout, dwpg, dwple, vec_c = _post(
        x, p, tgt, sbo, mlao, sbg, mlag, g["sb_out_norm_g"], g["mla_out_norm_g"], wout, g["norm_post_g"], wple,
        g["ple_norm_g"], wpg, g["b_ple_gate"])
    dsbq, dsbk, dsbv = _sb_bwd(sbq, sbk, sbkt, sbv, dsbo)
    dqc, dkc, dmv = _mla_bwd(qc, kc, kct, mv, dmlao, lse, delta)
    gx, dwin_ext, dwuq_ext, dwk_ext, dwv, vec_d = _pre_bwd(
        x, dxres, dsbq, dsbk, dsbv, dsbg, dmlag, dqc, dkc, dmv, cq, ckv, tabs, g["norm_pre_g"], win_ext, g["q_norm_g"],
        wuq_ext, g["kv_norm_g"], wk_ext, wv)
    return gx, _shard_grads(dwin_ext, dwuq_ext, dwk_ext, dwv, dwout, dwple, dwpg), jnp.concatenate([vec_c, vec_d], axis=0)


_VEC_LAYOUT = (("norm_post_g", 0, 0, 1024), ("ple_norm_g", 1, 0, 1024), ("b_ple_gate", 2, 0, 1024), ("sb_out_norm_g", 3, 0, 512),
               ("mla_out_norm_g", 3, 512, 512), ("norm_pre_g", 8, 0, 1024), ("q_norm_g", 9, 0, 256), ("kv_norm_g", 9, 256, 128))
_LOSS_ROW = 4
_WEIGHT_ORDER = ("norm_pre_g", "w_in", "q_norm_g", "w_uq", "kv_norm_g", "w_ukv", "sb_out_norm_g", "mla_out_norm_g", "w_out",
                 "norm_post_g", "w_ple", "ple_norm_g", "w_ple_gate", "b_ple_gate")


def kernel(x, p, positions, norm_pre_g, w_in, q_norm_g, w_uq, kv_norm_g, w_ukv, sb_out_norm_g, mla_out_norm_g, w_out, norm_post_g, w_ple, ple_norm_g, w_ple_gate, b_ple_gate, loss_target, m_norm_pre_g, m_w_in, m_q_norm_g, m_w_uq, m_kv_norm_g, m_w_ukv, m_sb_out_norm_g, m_mla_out_norm_g, m_w_out, m_norm_post_g, m_w_ple, m_ple_norm_g, m_w_ple_gate, m_b_ple_gate, v_norm_pre_g, v_w_in, v_q_norm_g, v_w_uq, v_kv_norm_g, v_w_ukv, v_sb_out_norm_g, v_mla_out_norm_g, v_w_out, v_norm_post_g, v_w_ple, v_ple_norm_g, v_w_ple_gate, v_b_ple_gate):
    w = {"norm_pre_g": norm_pre_g, "w_in": w_in[0], "q_norm_g": q_norm_g, "w_uq": w_uq[0], "kv_norm_g": kv_norm_g, "w_ukv": w_ukv[0],
         "sb_out_norm_g": sb_out_norm_g, "mla_out_norm_g": mla_out_norm_g, "w_out": w_out[0], "norm_post_g": norm_post_g,
         "w_ple": w_ple[0], "ple_norm_g": ple_norm_g, "w_ple_gate": w_ple_gate[0], "b_ple_gate": b_ple_gate}
    m = {"norm_pre_g": m_norm_pre_g, "w_in": m_w_in[0], "q_norm_g": m_q_norm_g, "w_uq": m_w_uq[0], "kv_norm_g": m_kv_norm_g,
         "w_ukv": m_w_ukv[0], "sb_out_norm_g": m_sb_out_norm_g, "mla_out_norm_g": m_mla_out_norm_g, "w_out": m_w_out[0],
         "norm_post_g": m_norm_post_g, "w_ple": m_w_ple[0], "ple_norm_g": m_ple_norm_g, "w_ple_gate": m_w_ple_gate[0],
         "b_ple_gate": m_b_ple_gate}
    v = {"norm_pre_g": v_norm_pre_g, "w_in": v_w_in[0], "q_norm_g": v_q_norm_g, "w_uq": v_w_uq[0], "kv_norm_g": v_kv_norm_g,
         "w_ukv": v_w_ukv[0], "sb_out_norm_g": v_sb_out_norm_g, "mla_out_norm_g": v_mla_out_norm_g, "w_out": v_w_out[0],
         "norm_post_g": v_norm_post_g, "w_ple": v_w_ple[0], "ple_norm_g": v_ple_norm_g, "w_ple_gate": v_w_ple_gate[0],
         "b_ple_gate": v_b_ple_gate}
    gathered = _allgather_weights([w[n] for n in _BIG])
    gx, gsh, vec = _local_grads(x[0], p[0, 0], positions[0], loss_target[0], w, dict(zip(_BIG, gathered)))
    *gred, vsum = _reduce_scatter_grads([gsh[n] for n in _BIG], vec)
    loss = vsum[_LOSS_ROW, 0]

    g, delta, new_m, new_v = _adamw_small(vsum, w, m, v)
    for n, gn in zip(_BIG, gred):
        g[n] = gn
        delta[n], new_m[n], new_v[n] = _adamw(w[n], gn, m[n], v[n])

    lead = lambda n, a: a[None] if n in _BIG else a
    return (loss, gx[None],
            *[lead(n, g[n]) for n in _WEIGHT_ORDER], *[lead(n, delta[n]) for n in _WEIGHT_ORDER],
            *[lead(n, new_m[n]) for n in _WEIGHT_ORDER], *[lead(n, new_v[n]) for n in _WEIGHT_ORDER])
```

```python
import numpy as np
import jax
import jax.numpy as jnp
from jax import lax
from jax.experimental import pallas as pl
from jax.experimental.pallas import tpu as pltpu

F32 = jnp.float32
BF16 = jnp.bfloat16
MESH = pl.DeviceIdType.MESH

D_MODEL = 1024
HEAD_DIM = 64
D_SB = 512
D_MLA = 512
Q_LORA = 256
KV_LORA = 128
QK_NOPE = 64
QK_ROPE = 32
PLE_DIM = 256
D_IN = 2976
D_EXT = 3072
ROPE_THETA = 10000.0
EPS = 1e-6
N_SHARD = 4

ADAM_LR = 0.001
ADAM_B1 = 0.9
ADAM_B2 = 0.999
ADAM_EPS = 1e-08
ADAM_WD = 0.01
ADAM_STEP = 10

LANES = 128
BK = 128
WQ = 256
MQ = 512
SB_CUTOFF = 120.0
TM = 256
TM_PRE = 256
VEC_ROWS = 16
VMEM_DENSE = 52 * 1024 * 1024
VMEM_ATTN = 40 * 1024 * 1024


def _mm(a, b):
    return jnp.dot(a, b, preferred_element_type=F32)


def _mm_nt(a, b):
    return lax.dot_general(a, b, (((1,), (1,)), ((), ())), preferred_element_type=F32)


def _mm_tn(a, b):
    return lax.dot_general(a, b, (((0,), (0,)), ((), ())), preferred_element_type=F32)


def _seg(a, bd2):
    return _mm(_split2(a), bd2)


def _const(mask):
    return jnp.asarray(np.asarray(mask, np.float32), dtype=BF16)


def _blockdiag2(n, seg):
    r = (np.arange(2 * n)[:, None] % n) // seg
    c = np.arange(n)[None, :] // seg
    return _const(r == c)


def _sigmoid(a):
    return 1.0 / (1.0 + jnp.exp(-a))


def _rowmean(a):
    return jnp.mean(a, axis=-1, keepdims=True)


def _colsum(a):
    return jnp.sum(a, axis=0, keepdims=True)


def _rope_fwd(a, c, sa, sb):
    w = a.shape[-1]
    return a * c + pltpu.roll(a, w - 16, 1) * sa + pltpu.roll(a, 16, 1) * sb


def _rope_bwd(g, c, sa, sb):
    w = g.shape[-1]
    return g * c + pltpu.roll(g * sa, 16, 1) + pltpu.roll(g * sb, w - 16, 1)


def _full(shape):
    return pl.BlockSpec(shape, lambda *_: (0,) * len(shape))


def _acc(shape):
    return pl.BlockSpec(shape, lambda *_: (0,) * len(shape))


def _full2(shape):
    return pl.BlockSpec(shape, lambda p, i: (0, 0))


def _cols(height, tm=TM):
    return pl.BlockSpec((height, tm), lambda i: (0, i))


def _rows(width, tm=TM):
    return pl.BlockSpec((tm, width), lambda i: (i, 0))


def _pre_fwd(x, tabs, gpre, win, gq, wuq, gkv, wk, wv):
    s = x.shape[0]
    c_t, sa_t, sb_t = tabs
    rw, cl = (lambda width: _rows(width, TM_PRE)), (lambda height: _cols(height, TM_PRE))

    def body(x_ref, c_ref, sa_ref, sb_ref, gpre_ref, win_ref, gq_ref, wuq_ref, gkv_ref, wk_ref, wv_ref,
             sbq_ref, sbk_ref, sbv_ref, sbg_ref, mlag_ref, cq_ref, ckv_ref, qc_ref, kc_ref, mv_ref,
             sbkt_ref, sbvt_ref, kct_ref, mvt_ref):
        xv = x_ref[...]
        r1 = lax.rsqrt(_rowmean(xv * xv) + EPS)
        h = (xv * r1 * gpre_ref[...]).astype(BF16)
        proj = _mm(h, win_ref[...])
        sbq_ref[...] = proj[:, 0:512].astype(BF16)
        sbk_ref[...] = proj[:, 512:1024].astype(BF16)
        sbv_ref[...] = proj[:, 1024:1536].astype(BF16)
        sbkt_ref[...] = proj[:, 512:1024].T.astype(BF16)
        sbvt_ref[...] = proj[:, 1024:1536].T.astype(BF16)
        sbg_ref[...] = proj[:, 1536:2048]
        cq = proj[:, 2048:2304]
        ckv = proj[:, 2304:2432]
        kr = proj[:, 2432:2560]
        mlag_ref[...] = proj[:, 2560:3072]
        cq_ref[...] = cq
        ckv_ref[...] = ckv
        c1, sa1, sb1 = c_ref[...], sa_ref[...], sb_ref[...]
        c8, sa8, sb8 = jnp.tile(c1, (1, 8)), jnp.tile(sa1, (1, 8)), jnp.tile(sb1, (1, 8))
        cqn = (cq * lax.rsqrt(_rowmean(cq * cq) + EPS) * gq_ref[...]).astype(BF16)
        qe = _mm(cqn, wuq_ref[...])
        qc_ref[...] = _rope_fwd(qe, c8, sa8, sb8).astype(BF16)
        ckvn = (ckv * lax.rsqrt(_rowmean(ckv * ckv) + EPS) * gkv_ref[...]).astype(BF16)
        ke = _mm(ckvn, wk_ref[...])
        krr = _rope_fwd(kr, c1, sa1, sb1)
        kcat = ke + jnp.tile(krr, (1, 8))
        kc_ref[...] = kcat.astype(BF16)
        kct_ref[...] = kcat.T.astype(BF16)
        mval = _mm(ckvn, wv_ref[...])
        mv_ref[...] = mval.astype(BF16)
        mvt_ref[...] = mval.T.astype(BF16)

    out_shape = (
        jax.ShapeDtypeStruct((s, 512), BF16), jax.ShapeDtypeStruct((s, 512), BF16), jax.ShapeDtypeStruct((s, 512), BF16),
        jax.ShapeDtypeStruct((s, 512), F32), jax.ShapeDtypeStruct((s, 512), F32),
        jax.ShapeDtypeStruct((s, Q_LORA), F32), jax.ShapeDtypeStruct((s, KV_LORA), F32),
        jax.ShapeDtypeStruct((s, 1024), BF16), jax.ShapeDtypeStruct((s, 1024), BF16), jax.ShapeDtypeStruct((s, 512), BF16),
        jax.ShapeDtypeStruct((512, s), BF16), jax.ShapeDtypeStruct((512, s), BF16), jax.ShapeDtypeStruct((1024, s), BF16),
        jax.ShapeDtypeStruct((512, s), BF16),
    )
    return pl.pallas_call(
        body, name="pre_fwd", grid=(s // TM_PRE,), out_shape=out_shape,
        in_specs=[rw(D_MODEL), rw(LANES), rw(LANES), rw(LANES), _full((1, D_MODEL)), _full((D_MODEL, D_EXT)),
                  _full((1, Q_LORA)), _full((Q_LORA, 1024)), _full((1, KV_LORA)), _full((KV_LORA, 1024)), _full((KV_LORA, 512))],
        out_specs=(rw(512), rw(512), rw(512), rw(512), rw(512), rw(Q_LORA), rw(KV_LORA),
                   rw(1024), rw(1024), rw(512), cl(512), cl(512), cl(1024), cl(512)),
        compiler_params=pltpu.CompilerParams(vmem_limit_bytes=VMEM_DENSE),
    )(x, c_t, sa_t, sb_t, gpre, win, gq, wuq, gkv, wk, wv)


def _softplus(z):
    neg_abs = lax.bitcast_convert_type(lax.bitcast_convert_type(z, jnp.uint32) | jnp.uint32(0x80000000), F32)
    return jnp.maximum(z, 0.0) + jnp.log(1.0 + jnp.exp(neg_abs))


def _sum_matrix(kind, terms):
    r, c = np.arange(2 * BK)[:, None], np.arange(2 * BK * terms)[None, :] % (2 * BK)
    rk, ck = r % BK, c % BK
    return _const(((r // BK) == (c // BK)) & {"suffix": ck >= rk, "prefix": ck <= rk}[kind])


def _split_rows(a):
    hi = a.astype(BF16)
    return jnp.concatenate([hi, (a - hi.astype(F32)).astype(BF16)], axis=0)


def _heads_t(blk, rowi):
    zero = jnp.zeros_like(blk)
    return jnp.concatenate([jnp.where(rowi < 64, blk, zero), jnp.where(rowi >= 64, blk, zero)], axis=1)


def _mask_keys(a, valid, fill=0.0):
    return jnp.concatenate([jnp.where(valid, a[0:BK], fill), jnp.where(valid, a[BK:2 * BK], fill)], axis=0)


def _split2(a):
    hi = a.astype(BF16)
    lo = (a - hi.astype(F32)).astype(BF16)
    return jnp.concatenate([hi, lo], axis=1)


def _pair_stack(b, lane):
    zero = jnp.zeros_like(b)
    return jnp.concatenate([jnp.where(lane < 64, b, zero), jnp.where(lane >= 64, b, zero)], axis=0)


def _sb_fwd(q, k, vt, late):
    s = q.shape[0]
    n = len(late)

    def body(q_ref, k_ref, vt_ref, usuf_ref, *rest):
        ins, o_ref, outs = rest[:n], rest[n], rest[n + 1:2 * n + 1]
        acc_scr, run_scr = rest[2 * n + 1:2 * n + 3]
        bufs, (send_sems, recv_sems, out_sems) = rest[2 * n + 3:3 * n + 3], rest[3 * n + 3:]
        p, i = pl.program_id(0), pl.program_id(1)
        gather_start, gather_forward, gather_finish = _gather_steps([a.shape for a in late], ins, bufs, send_sems, recv_sems)

        @pl.when((p == 0) & (i == 0))
        def _():
            gather_start()

        @pl.when((p == 2) & (i == 0))
        def _():
            gather_forward()

        lane = lax.broadcasted_iota(jnp.int32, (1, LANES), 1)
        rowi = lax.broadcasted_iota(jnp.int32, (LANES, 1), 0)
        keyi = lax.broadcasted_iota(jnp.int32, (BK, WQ), 0)
        qryi = lax.broadcasted_iota(jnp.int32, (BK, WQ), 1) + i * WQ
        qs = q_ref[...] * (HEAD_DIM ** -0.5)

        def group(blocks, masked):
            starts = [pl.multiple_of(j * BK, BK) for j in blocks]
            valid = [(keyi + j * BK) < qryi if m else None for j, m in zip(blocks, masked)]
            zs = [_mm_nt(_pair_stack(k_ref[pl.ds(ks, BK), :], lane), qs) for ks in starts]
            sps = [_softplus(z) for z in zs]
            sps = [sp if ok is None else _mask_keys(sp, ok) for sp, ok in zip(sps, valid)]
            cums = [_mm(usuf_ref[...], _split_rows(sp)) for sp in sps]
            ws = [jnp.exp(z - c) for z, c in zip(zs, cums)]
            ws = [w if ok is None else _mask_keys(w, ok) for w, ok in zip(ws, valid)]
            pvs = [_mm(_heads_t(vt_ref[:, pl.ds(ks, BK)], rowi), w.astype(BF16)) for ks, w in zip(starts, ws)]
            for pv, c in zip(pvs, cums):
                r0, r1 = run_scr[0:1, :], run_scr[1:2, :]
                acc_scr[...] += jnp.where(rowi < 64, jnp.exp(-r0), jnp.exp(-r1)) * pv
                run_scr[0:1, :] = r0 + c[0:1]
                run_scr[1:2, :] = r1 + c[BK:BK + 1]

        assert WQ == 2 * BK
        acc_scr[...] = jnp.zeros_like(acc_scr)
        run_scr[...] = jnp.zeros_like(run_scr)

        @pl.when(i == 0)
        def _():
            group([1, 0], [True, True])

        @pl.when(i > 0)
        def _():
            group([2 * i + 1, 2 * i, 2 * i - 1, 2 * i - 2], [True, True, False, False])

        def unfinished():
            return (jnp.min(run_scr[0:2, :]) < SB_CUTOFF).astype(jnp.int32)

        def step(c):
            group([2 * i - 1 - 2 * c[0], 2 * i - 2 - 2 * c[0]], [False, False])
            return c[0] + 1, unfinished()

        lax.while_loop(lambda c: (c[0] < i) & (c[1] > 0), step, (jnp.int32(1), unfinished()))
        o_ref[...] = acc_scr[...].T

        @pl.when((p == pl.num_programs(0) - 1) & (i == pl.num_programs(1) - 1))
        def _():
            gather_finish()
            copies = [pltpu.make_async_copy(bufs[t], outs[t], out_sems.at[t]) for t in range(n)]
            for cp in copies:
                cp.start()
            for cp in copies:
                cp.wait()

    qspec = pl.BlockSpec((WQ, LANES), lambda p, i: (i, p))
    kspec = pl.BlockSpec((s, LANES), lambda p, i: (0, p))
    tspec = pl.BlockSpec((LANES, s), lambda p, i: (p, 0))
    gathered = [jax.ShapeDtypeStruct((N_SHARD,) + a.shape, BF16) for a in late]
    return pl.pallas_call(
        body, name="sb_fwd", grid=(4, s // WQ),
        out_shape=(jax.ShapeDtypeStruct((s, 512), F32), *gathered),
        in_specs=[qspec, kspec, tspec, _full2((2 * BK, 4 * BK))] + [_full2(a.shape) for a in late],
        out_specs=(qspec,) + (pl.BlockSpec(memory_space=pl.ANY),) * n,
        scratch_shapes=[pltpu.VMEM((LANES, WQ), F32), pltpu.VMEM((8, WQ), F32)] + [pltpu.VMEM(g.shape, BF16) for g in gathered]
                       + [pltpu.SemaphoreType.DMA((6 * n,)), pltpu.SemaphoreType.DMA((6 * n,)), pltpu.SemaphoreType.DMA((n,))],
        compiler_params=pltpu.CompilerParams(vmem_limit_bytes=VMEM_ATTN),
    )(q, k, vt, _sum_matrix("suffix", 2), *late)


def _sb_bwd(q, k, kt, v, do, late):
    s = q.shape[0]
    n = len(late)
    halves = [a.shape[1] // 2 for a in late]

    def body(q_ref, k_ref, kt_ref, v_ref, do_ref, usuf_ref, upre_ref, *rest):
        g_refs, (dq_ref, dk_ref, dv_ref), outs = rest[:n], rest[n:n + 3], rest[n + 3:2 * n + 3]
        later_scr, dqt_scr, st_scr = rest[2 * n + 3:2 * n + 6]
        f_scr, reduce_scr, out_sems = rest[2 * n + 6:3 * n + 6], rest[3 * n + 6:-1], rest[-1]
        p, i = pl.program_id(0), pl.program_id(1)
        reduce_load, reduce_partial, reduce_total, reduce_finish = _reduce_steps(halves, g_refs, f_scr, reduce_scr)

        @pl.when((p == 0) & (i == 0))
        def _():
            reduce_load()

        @pl.when((p == 1) & (i == 0))
        def _():
            reduce_partial()

        @pl.when((p == 3) & (i == 0))
        def _():
            reduce_total()

        @pl.when(i == 0)
        def _():
            dk_ref[...] = jnp.zeros_like(dk_ref)
            dv_ref[...] = jnp.zeros_like(dv_ref)

        lane = lax.broadcasted_iota(jnp.int32, (1, LANES), 1)
        rowi = lax.broadcasted_iota(jnp.int32, (LANES, 1), 0)
        keyi = lax.broadcasted_iota(jnp.int32, (BK, WQ), 0)
        qryi = lax.broadcasted_iota(jnp.int32, (BK, WQ), 1) + i * WQ
        qs = q_ref[...] * (HEAD_DIM ** -0.5)
        dob = do_ref[...]
        dot = dob.astype(F32).T.astype(BF16)

        def scores(j):
            return _mm_nt(_pair_stack(k_ref[pl.ds(pl.multiple_of(j * BK, BK), BK), :], lane), qs)

        def scan(blocks, masked):
            sps = [_softplus(scores(j)) for j in blocks]
            sps = [_mask_keys(sp, (keyi + j * BK) < qryi) if m else sp for sp, j, m in zip(sps, blocks, masked)]
            for sp, j in zip(sps, blocks):
                run = st_scr[0:2, :]
                later_scr[j, 0:2, :] = run
                st_scr[0:2, :] = run + jnp.concatenate([jnp.sum(sp[0:BK], axis=0, keepdims=True),
                                                        jnp.sum(sp[BK:2 * BK], axis=0, keepdims=True)], axis=0)

        def sweep(blocks, masked):
            starts = [pl.multiple_of(j * BK, BK) for j in blocks]
            valid = [(keyi + j * BK) < qryi if m else None for j, m in zip(blocks, masked)]
            zs = [scores(j) for j in blocks]
            us = [jnp.exp(lax.bitcast_convert_type(lax.bitcast_convert_type(z, jnp.uint32) | jnp.uint32(0x80000000), F32))
                  for z in zs]
            sps = [jnp.maximum(z, 0.0) + jnp.log(1.0 + u) for z, u in zip(zs, us)]
            sps = [sp if ok is None else _mask_keys(sp, ok) for sp, ok in zip(sps, valid)]
            sigs = [jnp.where(z >= 0.0, 1.0, u) / (1.0 + u) for z, u in zip(zs, us)]
            cums = [_mm(usuf_ref[...], _split_rows(sp)) for sp in sps]
            dws = [_mm(_pair_stack(v_ref[pl.ds(ks, BK), :], lane), dot) for ks in starts]
            wfs = []
            for z, c, j, ok in zip(zs, cums, blocks, valid):
                f = jnp.exp(-later_scr[j, 0:2, :])
                wf = jnp.exp(z - c) * jnp.concatenate([jnp.broadcast_to(f[0:1], (BK, WQ)), jnp.broadcast_to(f[1:2], (BK, WQ))], axis=0)
                wfs.append(wf if ok is None else _mask_keys(wf, ok))
            es = [dw * wf for dw, wf in zip(dws, wfs)]
            pres = [_mm(upre_ref[...], e.astype(BF16)) for e in es]
            dzs = []
            for e, pre, sig, ok in zip(es, pres, sigs, valid):
                e0 = pre[0:BK] + st_scr[0:1, :]
                e1 = pre[BK:2 * BK] + st_scr[1:2, :]
                st_scr[0:1, :] = e0[BK - 1:BK]
                st_scr[1:2, :] = e1[BK - 1:BK]
                dz = e - sig * jnp.concatenate([e0, e1], axis=0)
                dzs.append((dz if ok is None else _mask_keys(dz, ok)).astype(BF16))
            dqt_scr[...] += _mm(jnp.concatenate([_heads_t(kt_ref[:, pl.ds(ks, BK)], rowi) for ks in starts], axis=1),
                                jnp.concatenate(dzs, axis=0))
            for ks, dz, wf in zip(starts, dzs, wfs):
                rk = _mm(dz, qs)
                dk_ref[pl.ds(ks, BK), :] += jnp.where(lane < 64, rk[0:BK], rk[BK:2 * BK])
                rv = _mm(wf.astype(BF16), dob)
                dv_ref[pl.ds(ks, BK), :] += jnp.where(lane < 64, rv[0:BK], rv[BK:2 * BK])

        assert WQ == 2 * BK
        st_scr[...] = jnp.zeros_like(st_scr)

        @pl.when(i == 0)
        def _():
            scan([1, 0], [True, True])

        @pl.when(i > 0)
        def _():
            scan([2 * i + 1, 2 * i, 2 * i - 1, 2 * i - 2], [True, True, False, False])

        def unfinished():
            return (jnp.min(st_scr[0:2, :]) < SB_CUTOFF).astype(jnp.int32)

        def step(c):
            scan([2 * i - 1 - 2 * c[0], 2 * i - 2 - 2 * c[0]], [False, False])
            return c[0] + 1, unfinished()

        npairs, _ = lax.while_loop(lambda c: (c[0] < i) & (c[1] > 0), step, (jnp.minimum(i, 1), unfinished()))

        st_scr[...] = jnp.zeros_like(st_scr)
        dqt_scr[...] = jnp.zeros_like(dqt_scr)
        first = 2 * (i - npairs)

        def early(t, carry):
            sweep([first + 2 * t, first + 2 * t + 1], [False, False])
            return carry

        lax.fori_loop(0, npairs - 1, early, 0)

        @pl.when(i == 0)
        def _():
            sweep([0, 1], [True, True])

        @pl.when(i > 0)
        def _():
            sweep([2 * i - 2, 2 * i - 1, 2 * i, 2 * i + 1], [False, False, True, True])

        dq_ref[...] = (dqt_scr[...].T * (HEAD_DIM ** -0.5)).astype(BF16)

        @pl.when((p == pl.num_programs(0) - 1) & (i == pl.num_programs(1) - 1))
        def _():
            reduce_finish()
            copies = [pltpu.make_async_copy(f_scr[t], outs[t], out_sems.at[t]) for t in range(n)]
            for cp in copies:
                cp.start()
            for cp in copies:
                cp.wait()

    qspec = pl.BlockSpec((WQ, LANES), lambda p, i: (i, p))
    kspec = pl.BlockSpec((s, LANES), lambda p, i: (0, p))
    tspec = pl.BlockSpec((LANES, s), lambda p, i: (p, 0))
    anywhere = pl.BlockSpec(memory_space=pl.ANY)
    reduced = [jax.ShapeDtypeStruct(a.shape[1:], F32) for a in late]
    return pl.pallas_call(
        body, name="sb_bwd", grid=(4, s // WQ),
        out_shape=(jax.ShapeDtypeStruct((s, 512), BF16), jax.ShapeDtypeStruct((s, 512), F32),
                   jax.ShapeDtypeStruct((s, 512), F32), *reduced),
        in_specs=[qspec, kspec, tspec, kspec, qspec, _full2((2 * BK, 4 * BK)), _full2((2 * BK, 2 * BK))] + [anywhere] * n,
        out_specs=(qspec, kspec, kspec) + (anywhere,) * n,
        scratch_shapes=[pltpu.VMEM((s // BK, 8, WQ), F32), pltpu.VMEM((LANES, WQ), F32), pltpu.VMEM((8, WQ), F32)]
                       + [pltpu.VMEM(r.shape, F32) for r in reduced] + _reduce_scratch(late) + [pltpu.SemaphoreType.DMA((n,))],
        compiler_params=pltpu.CompilerParams(vmem_limit_bytes=VMEM_ATTN),
    )(q, k, kt, v, do, _sum_matrix("suffix", 2), _sum_matrix("prefix", 1), *late)


MLA_SCALE = (QK_NOPE + QK_ROPE) ** -0.5
LOG2E = 1.4426950408889634


def _mla_keys(kb):
    zero = jnp.zeros((BK, LANES), kb.dtype)
    return jnp.concatenate([jnp.concatenate([kb[:, 0:LANES], zero], axis=1),
                            jnp.concatenate([zero, kb[:, LANES:2 * LANES]], axis=1)], axis=0)


def _mla_fwd(qc, kc, vt):
    s = qc.shape[0]
    rows_l = 16

    def body(q_ref, k_ref, vt_ref, o_ref, l_ref, p_scr, ot_scr, st_scr):
        i = pl.program_id(1)
        keyc = lax.broadcasted_iota(jnp.int32, (BK, MQ), 0)
        qryc = (lax.broadcasted_iota(jnp.int32, (BK, MQ), 1) + i * MQ) // 64
        row = lax.broadcasted_iota(jnp.int32, (LANES, 1), 0)
        qw = q_ref[...]
        orow = lax.broadcasted_iota(jnp.int32, (rows_l, 2 * BK), 0)
        ocol = lax.broadcasted_iota(jnp.int32, (rows_l, 2 * BK), 1)
        ones = jnp.where(((orow == 0) & (ocol < BK)) | ((orow == 1) & (ocol >= BK)), 1.0, 0.0).astype(BF16)

        def scores(j):
            ks = pl.multiple_of(j * BK, BK)
            return _mm_nt(_mla_keys(k_ref[pl.ds(ks, BK), :]), qw)

        def values_t(j):
            vtb = vt_ref[:, pl.ds(pl.multiple_of(j * BK, BK), BK)]
            zero = jnp.zeros_like(vtb)
            top = jnp.concatenate([jnp.where(row < 64, vtb, zero), jnp.where(row >= 64, vtb, zero)], axis=1)
            return jnp.concatenate([top, ones], axis=0)

        def softmax(ja, za, zb, masked):
            c = MLA_SCALE * LOG2E
            parts = [za[0:BK] * c, za[BK:2 * BK] * c, zb[0:BK] * c, zb[BK:2 * BK] * c]
            if masked:
                va = ((keyc + ja * BK) // 64) <= qryc
                vb = ((keyc + (ja + 1) * BK) // 64) <= qryc
                parts = [jnp.where(va, parts[0], -1e30), jnp.where(va, parts[1], -1e30),
                         jnp.where(vb, parts[2], -1e30), jnp.where(vb, parts[3], -1e30)]
            m0, m1 = st_scr[0:1, :], st_scr[1:2, :]
            n0 = jnp.maximum(m0, jnp.max(jnp.maximum(parts[0], parts[2]), axis=0, keepdims=True))
            n1 = jnp.maximum(m1, jnp.max(jnp.maximum(parts[1], parts[3]), axis=0, keepdims=True))
            st_scr[2:3, :] = jnp.exp2(m0 - n0)
            st_scr[3:4, :] = jnp.exp2(m1 - n1)
            st_scr[0:1, :] = n0
            st_scr[1:2, :] = n1
            p_scr[...] = jnp.concatenate([jnp.exp2(parts[0] - n0), jnp.exp2(parts[1] - n1),
                                          jnp.exp2(parts[2] - n0), jnp.exp2(parts[3] - n1)], axis=0).astype(BF16)

        def accumulate(ja):
            pv = _mm(jnp.concatenate([values_t(ja), values_t(ja + 1)], axis=1), p_scr[...])
            a = jnp.where(row < 64, st_scr[2:3, :], st_scr[3:4, :])
            ot_scr[0:LANES, :] = a * ot_scr[0:LANES, :] + pv[0:LANES]
            ot_scr[LANES:LANES + 8, :] = st_scr[2:10, :] * ot_scr[LANES:LANES + 8, :] + pv[LANES:LANES + 8]

        def step(n, masked):
            za, zb = scores(2 * n), scores(2 * n + 1)
            accumulate(2 * n - 2)
            softmax(2 * n, za, zb, masked)

        def first(masked):
            softmax(0, scores(0), scores(1), masked)

        st_scr[...] = jnp.concatenate([jnp.full((2, MQ), -1e30, F32), jnp.ones((14, MQ), F32)], axis=0)
        ot_scr[...] = jnp.zeros_like(ot_scr)

        npq = MQ // (2 * BK)

        @pl.when(i == 0)
        def _():
            first(True)
            for d in range(1, npq):
                step(d, True)

        @pl.when(i > 0)
        def _():
            first(False)
            lax.fori_loop(1, npq * i, lambda n, c: (step(n, False), c)[1], 0)
            for d in range(npq):
                step(npq * i + d, True)

        accumulate(2 * (npq * (i + 1) - 1))
        l0, l1 = ot_scr[LANES:LANES + 1, :], ot_scr[LANES + 1:LANES + 2, :]
        o_ref[...] = (ot_scr[0:LANES, :] / jnp.where(row < 64, l0, l1)).T
        l_ref[...] = jnp.where(row < 64, st_scr[0:1, :] + jnp.log2(l0), st_scr[1:2, :] + jnp.log2(l1)).T

    qspec = pl.BlockSpec((MQ, 2 * LANES), lambda p, i: (i, p))
    kspec = pl.BlockSpec((s, 2 * LANES), lambda p, i: (0, p))
    vtspec = pl.BlockSpec((LANES, s), lambda p, i: (p, 0))
    ospec = pl.BlockSpec((MQ, LANES), lambda p, i: (i, p))
    return pl.pallas_call(
        body, name="mla_fwd", grid=(4, s // MQ),
        out_shape=(jax.ShapeDtypeStruct((s, 512), F32), jax.ShapeDtypeStruct((s, 512), F32)),
        in_specs=[qspec, kspec, vtspec], out_specs=(ospec, ospec),
        scratch_shapes=[pltpu.VMEM((4 * BK, MQ), BF16), pltpu.VMEM((LANES + 8, MQ), F32), pltpu.VMEM((16, MQ), F32)],
        compiler_params=pltpu.CompilerParams(vmem_limit_bytes=VMEM_ATTN),
    )(qc, kc, vt)


def _mla_bwd(qc, kc, kct, v, do, lse, delta):
    s = qc.shape[0]

    def body(q_ref, k_ref, kt_ref, v_ref, do_ref, l_ref, d_ref, dq_ref, dk_ref, dv_ref, dqt_scr, p_scr, dz_scr):
        i = pl.program_id(1)

        @pl.when(i == 0)
        def _():
            dk_ref[...] = jnp.zeros_like(dk_ref)
            dv_ref[...] = jnp.zeros_like(dv_ref)

        lane = lax.broadcasted_iota(jnp.int32, (1, LANES), 1)
        keyc = lax.broadcasted_iota(jnp.int32, (BK, MQ), 0)
        qryc = (lax.broadcasted_iota(jnp.int32, (BK, MQ), 1) + i * MQ) // 64
        qw = q_ref[...]
        dob = do_ref[...]
        dost = (dob.astype(F32) * MLA_SCALE).T.astype(BF16)
        lt = l_ref[...].T
        dt = (d_ref[...] * MLA_SCALE).T
        lse0, lse1 = lt[0:1], lt[64:65]
        dl0, dl1 = dt[0:1], dt[64:65]
        dqt_scr[...] = jnp.zeros_like(dqt_scr)

        def products(j):
            ks = pl.multiple_of(j * BK, BK)
            return (_mm_nt(_mla_keys(k_ref[pl.ds(ks, BK), :]), qw), _mm(_pair_stack(v_ref[pl.ds(ks, BK), :], lane), dost))

        def grads(j, slot, zt, dwt, masked):
            zt = zt * (MLA_SCALE * LOG2E)
            p0 = jnp.exp2(zt[0:BK] - lse0)
            p1 = jnp.exp2(zt[BK:2 * BK] - lse1)
            if masked:
                valid = ((keyc + j * BK) // 64) <= qryc
                p0, p1 = jnp.where(valid, p0, 0.0), jnp.where(valid, p1, 0.0)
            p_scr[slot] = jnp.concatenate([p0, p1], axis=0).astype(BF16)
            dz_scr[slot] = jnp.concatenate([p0 * (dwt[0:BK] - dl0), p1 * (dwt[BK:2 * BK] - dl1)], axis=0).astype(BF16)

        def keys_t(ks):
            ktb = kt_ref[:, pl.ds(ks, BK)]
            zero = jnp.zeros((LANES, BK), ktb.dtype)
            return jnp.concatenate([jnp.concatenate([ktb[0:LANES], zero], axis=1),
                                    jnp.concatenate([zero, ktb[LANES:2 * LANES]], axis=1)], axis=0)

        def scatter(ja):
            ksa, ksb = pl.multiple_of(ja * BK, BK), pl.multiple_of((ja + 1) * BK, BK)
            dqt_scr[...] += _mm(jnp.concatenate([keys_t(ksa), keys_t(ksb)], axis=1),
                                jnp.concatenate([dz_scr[0], dz_scr[1]], axis=0))
            for slot, ks in ((0, ksa), (1, ksb)):
                rk = _mm(dz_scr[slot], qw)
                dk_ref[pl.ds(ks, BK), :] += jnp.concatenate([rk[0:BK, 0:LANES], rk[BK:2 * BK, LANES:2 * LANES]], axis=1)
                rv = _mm(p_scr[slot], dob)
                dv_ref[pl.ds(ks, BK), :] += jnp.where(lane < 64, rv[0:BK], rv[BK:2 * BK])

        def step(n, masked):
            za, wa = products(2 * n)
            zb, wb = products(2 * n + 1)
            scatter(2 * n - 2)
            grads(2 * n, 0, za, wa, masked)
            grads(2 * n + 1, 1, zb, wb, masked)

        def first(masked):
            za, wa = products(0)
            zb, wb = products(1)
            grads(0, 0, za, wa, masked)
            grads(1, 1, zb, wb, masked)

        npq = MQ // (2 * BK)

        @pl.when(i == 0)
        def _():
            first(True)
            for d in range(1, npq):
                step(d, True)

        @pl.when(i > 0)
        def _():
            first(False)
            lax.fori_loop(1, npq * i, lambda n, c: (step(n, False), c)[1], 0)
            for d in range(npq):
                step(npq * i + d, True)

        scatter(2 * (npq * (i + 1) - 1))
        dq_ref[...] = dqt_scr[...].T

    qspec = pl.BlockSpec((MQ, 2 * LANES), lambda p, i: (i, p))
    kspec = pl.BlockSpec((s, 2 * LANES), lambda p, i: (0, p))
    ktspec = pl.BlockSpec((2 * LANES, s), lambda p, i: (p, 0))
    vspec = pl.BlockSpec((s, LANES), lambda p, i: (0, p))
    ospec = pl.BlockSpec((MQ, LANES), lambda p, i: (i, p))
    return pl.pallas_call(
        body, name="mla_bwd", grid=(4, s // MQ),
        out_shape=(jax.ShapeDtypeStruct((s, 1024), F32), jax.ShapeDtypeStruct((s, 1024), F32),
                   jax.ShapeDtypeStruct((s, 512), F32)),
        in_specs=[qspec, kspec, ktspec, vspec, ospec, ospec, ospec], out_specs=(qspec, kspec, vspec),
        scratch_shapes=[pltpu.VMEM((2 * LANES, MQ), F32), pltpu.VMEM((2, 2 * BK, MQ), BF16), pltpu.VMEM((2, 2 * BK, MQ), BF16)],
        compiler_params=pltpu.CompilerParams(vmem_limit_bytes=VMEM_ATTN),
    )(qc, kc, kct, v, do, lse, delta)


def _post(x, p, tgt, sbo, mlao, sbg, mlag, gsb, gmla, wout, gpost, wple, gple, wpg, bpg):
    s = x.shape[0]

    def body(x_ref, p_ref, t_ref, sbo_ref, mlao_ref, sbg_ref, mlag_ref, gsb_ref, gmla_ref, wout_ref,
             gpost_ref, wple_ref, gple_ref, wpg_ref, bpg_ref, bd_ref,
             dsbo_ref, dmlao_ref, delta_ref, dsbg_ref, dmlag_ref, dxres_ref, dwout_ref, dwpg_ref, dwple_ref, vec_ref):
        i = pl.program_id(0)

        @pl.when(i == 0)
        def _():
            dwout_ref[...] = jnp.zeros_like(dwout_ref)
            dwpg_ref[...] = jnp.zeros_like(dwpg_ref)
            dwple_ref[...] = jnp.zeros_like(dwple_ref)
            vec_ref[...] = jnp.zeros_like(vec_ref)

        inv_hd = 1.0 / HEAD_DIM

        def head_fwd(o, g, gate):
            r = lax.rsqrt(_seg(o * o, bd_ref[...]) * inv_hd + EPS)
            hat = o * r
            n = hat * g
            sg = _sigmoid(gate)
            return hat, r, n, sg, n * (gate * sg)

        sbo, mlao, sbg_v, mlag_v = sbo_ref[...], mlao_ref[...], sbg_ref[...], mlag_ref[...]
        gsb_v, gmla_v = gsb_ref[...], gmla_ref[...]
        sb_hat, sb_r, sb_n, sb_sg, sb_y = head_fwd(sbo, gsb_v, sbg_v)
        ml_hat, ml_r, ml_n, ml_sg, ml_y = head_fwd(mlao, gmla_v, mlag_v)
        mix = jnp.concatenate([sb_y, ml_y], axis=1).astype(BF16)
        y = _mm(mix, wout_ref[...])
        ry = lax.rsqrt(_rowmean(y * y) + EPS)
        y_hat = y * ry
        gpost_v = gpost_ref[...]
        x1 = x_ref[...] + y_hat * gpost_v
        pb = p_ref[...].astype(BF16)
        pl_ = _mm(pb, wple_ref[...])
        rp = lax.rsqrt(_rowmean(pl_ * pl_) + EPS)
        pl_hat = pl_ * rp
        gple_v = gple_ref[...]
        ple = pl_hat * gple_v
        x1b = x1.astype(BF16)
        gate = _sigmoid(_mm(x1b, wpg_ref[...]) + bpg_ref[...])
        err = x1 + ple * gate - t_ref[...]
        loss = 0.5 * jnp.sum(_rowmean(err * err))
        dout = err * (1.0 / D_MODEL)

        du = dout * ple * gate * (1.0 - gate)
        dub = du.astype(BF16)
        dple = dout * gate
        dx1 = dout + _mm_nt(dub, wpg_ref[...])
        dwpg_ref[...] += _mm_tn(x1b, dub)
        dplh = dple * gple_v
        dpl = rp * (dplh - pl_hat * _rowmean(dplh * pl_hat))
        dwple_ref[...] += _mm_tn(pb, dpl.astype(BF16))
        dxres_ref[...] = dx1
        dyh = dx1 * gpost_v
        dy = ry * (dyh - y_hat * _rowmean(dyh * y_hat))
        dyb = dy.astype(BF16)
        dwout_ref[...] += _mm_tn(mix, dyb)
        dmix = _mm_nt(dyb, wout_ref[...])

        def head_bwd(dyv, hat, r, n, sg, g, gate):
            dn = dyv * (gate * sg)
            dgate = dyv * n * (sg * (1.0 + gate * (1.0 - sg)))
            dhat = dn * g
            do = r * (dhat - hat * (_seg(dhat * hat, bd_ref[...]) * inv_hd))
            return do, dgate, _colsum(dn * hat)

        dsbo, dsbg, dg_sb = head_bwd(dmix[:, 0:512], sb_hat, sb_r, sb_n, sb_sg, gsb_v, sbg_v)
        dmlao, dmlag, dg_ml = head_bwd(dmix[:, 512:1024], ml_hat, ml_r, ml_n, ml_sg, gmla_v, mlag_v)
        dsbo_ref[...] = dsbo.astype(BF16)
        dmlao_ref[...] = dmlao.astype(BF16)
        delta_ref[...] = _seg(dmlao * mlao, bd_ref[...])
        dsbg_ref[...] = dsbg.astype(BF16)
        dmlag_ref[...] = dmlag.astype(BF16)
        vec_ref[pl.ds(0, 1), :] += _colsum(dx1 * y_hat)
        vec_ref[pl.ds(1, 1), :] += _colsum(dple * pl_hat)
        vec_ref[pl.ds(2, 1), :] += _colsum(du)
        vec_ref[pl.ds(3, 1), :] += jnp.concatenate([dg_sb, dg_ml], axis=1)
        vec_ref[pl.ds(4, 1), :] += jnp.full((1, D_MODEL), loss, F32)

    out_shape = (
        jax.ShapeDtypeStruct((s, 512), BF16), jax.ShapeDtypeStruct((s, 512), BF16), jax.ShapeDtypeStruct((s, 512), F32),
        jax.ShapeDtypeStruct((s, 512), BF16), jax.ShapeDtypeStruct((s, 512), BF16), jax.ShapeDtypeStruct((s, D_MODEL), F32),
        jax.ShapeDtypeStruct((D_MODEL, D_MODEL), F32), jax.ShapeDtypeStruct((D_MODEL, D_MODEL), F32),
        jax.ShapeDtypeStruct((PLE_DIM, D_MODEL), F32), jax.ShapeDtypeStruct((8, D_MODEL), F32),
    )
    return pl.pallas_call(
        body, name="post_fwd_bwd", grid=(s // TM,), out_shape=out_shape,
        in_specs=[_rows(D_MODEL), _rows(PLE_DIM), _rows(D_MODEL), _rows(512), _rows(512), _rows(512), _rows(512),
                  _full((1, 512)), _full((1, 512)), _full((D_MODEL, D_MODEL)),
                  _full((1, D_MODEL)), _full((PLE_DIM, D_MODEL)), _full((1, D_MODEL)), _full((D_MODEL, D_MODEL)),
                  _full((1, D_MODEL)), _full((1024, 512))],
        out_specs=(_rows(512), _rows(512), _rows(512), _rows(512), _rows(512), _rows(D_MODEL),
                   _acc((D_MODEL, D_MODEL)), _acc((D_MODEL, D_MODEL)), _acc((PLE_DIM, D_MODEL)), _acc((8, D_MODEL))),
        compiler_params=pltpu.CompilerParams(vmem_limit_bytes=VMEM_DENSE),
    )(x, p, tgt, sbo, mlao, sbg, mlag, gsb, gmla, wout, gpost, wple, gple, wpg, bpg, _blockdiag2(512, HEAD_DIM))


def _pre_bwd(x, dxres, dsbq, dsbk, dsbv, dsbg, dmlag, dqc, dkc, dmv, cq, ckv, tabs, gpre, win, gq, wuq, gkv, wk, wv):
    s = x.shape[0]
    c_t, sa_t, sb_t = tabs
    rw = _rows

    def body(x_ref, dxres_ref, dsbq_ref, dsbk_ref, dsbv_ref, dsbg_ref, dmlag_ref, dqc_ref, dkc_ref, dmv_ref, cq_ref,
             ckv_ref, c_ref, sa_ref, sb_ref, gpre_ref, win_ref, gq_ref, wuq_ref, gkv_ref, wk_ref, wv_ref,
             gx_ref, dwin_ref, dwuq_ref, dwk_ref, dwv_ref, vec_ref, dwin_acc):
        i = pl.program_id(0)

        @pl.when(i == 0)
        def _():
            dwin_acc[...] = jnp.zeros_like(dwin_acc)
            dwuq_ref[...] = jnp.zeros_like(dwuq_ref)
            dwk_ref[...] = jnp.zeros_like(dwk_ref)
            dwv_ref[...] = jnp.zeros_like(dwv_ref)
            vec_ref[...] = jnp.zeros_like(vec_ref)

        lane = lax.broadcasted_iota(jnp.int32, (1, LANES), 1)
        c1, sa1, sb1 = c_ref[...], sa_ref[...], sb_ref[...]
        c8, sa8, sb8 = jnp.tile(c1, (1, 8)), jnp.tile(sa1, (1, 8)), jnp.tile(sb1, (1, 8))

        def norm_bwd(dn, hat, r, g):
            t = dn * g
            return r * (t - hat * _rowmean(t * hat)), _colsum(dn * hat)

        xv = x_ref[...]
        r1 = lax.rsqrt(_rowmean(xv * xv) + EPS)
        x_hat = xv * r1
        gpre_v = gpre_ref[...]
        hb = (x_hat * gpre_v).astype(BF16)
        ready = jnp.concatenate([dsbq_ref[...], dsbk_ref[...].astype(BF16), dsbv_ref[...].astype(BF16), dsbg_ref[...]], axis=1)
        dmlag = dmlag_ref[...]
        dwin_acc[:, 0:2048] += _mm_tn(hb, ready)
        dwin_acc[:, 2560:3072] += _mm_tn(hb, dmlag)
        dh = _mm_nt(ready, win_ref[:, 0:2048]) + _mm_nt(dmlag, win_ref[:, 2560:3072])

        dqeb = _rope_bwd(dqc_ref[...], c8, sa8, sb8).astype(BF16)
        cq = cq_ref[...]
        rq = lax.rsqrt(_rowmean(cq * cq) + EPS)
        cq_hat = cq * rq
        gq_v = gq_ref[...]
        dwuq_ref[...] += _mm_tn((cq_hat * gq_v).astype(BF16), dqeb)
        dcq, dg_q = norm_bwd(_mm_nt(dqeb, wuq_ref[...]), cq_hat, rq, gq_v)

        dkc = dkc_ref[...]
        dkcb = dkc.astype(BF16)
        dmvb = dmv_ref[...].astype(BF16)
        ckv = ckv_ref[...]
        rkv = lax.rsqrt(_rowmean(ckv * ckv) + EPS)
        ckv_hat = ckv * rkv
        gkv_v = gkv_ref[...]
        ckvnb = (ckv_hat * gkv_v).astype(BF16)
        dwk_ref[...] += _mm_tn(ckvnb, dkcb)
        dwv_ref[...] += _mm_tn(ckvnb, dmvb)
        dckv, dg_kv = norm_bwd(_mm_nt(dkcb, wk_ref[...]) + _mm_nt(dmvb, wv_ref[...]), ckv_hat, rkv, gkv_v)

        dkr = dkc[:, 0:LANES]
        for hh in range(1, 8):
            dkr = dkr + dkc[:, LANES * hh:LANES * (hh + 1)]
        dkr = _rope_bwd(dkr, c1, sa1, sb1)
        dkr = jnp.where((lane >= 64) & (lane < 96), dkr, 0.0)

        late = jnp.concatenate([dcq.astype(BF16), dckv.astype(BF16), dkr.astype(BF16)], axis=1)
        dwin_acc[:, 2048:2560] += _mm_tn(hb, late)
        dx, dg_pre = norm_bwd(dh + _mm_nt(late, win_ref[:, 2048:2560]), x_hat, r1, gpre_v)
        gx_ref[...] = dxres_ref[...] + dx
        vec_ref[pl.ds(0, 1), :] += dg_pre
        vec_ref[pl.ds(1, 1), :] += jnp.concatenate([dg_q, dg_kv, jnp.zeros((1, D_MODEL - Q_LORA - KV_LORA), F32)], axis=1)

        @pl.when(i == pl.num_programs(0) - 1)
        def _():
            pltpu.sync_copy(dwin_acc, dwin_ref)

    out_shape = (
        jax.ShapeDtypeStruct((s, D_MODEL), F32), jax.ShapeDtypeStruct((D_MODEL, D_EXT), F32),
        jax.ShapeDtypeStruct((Q_LORA, 1024), F32), jax.ShapeDtypeStruct((KV_LORA, 1024), F32),
        jax.ShapeDtypeStruct((KV_LORA, 512), F32), jax.ShapeDtypeStruct((8, D_MODEL), F32),
    )
    return pl.pallas_call(
        body, name="pre_bwd", grid=(s // TM,), out_shape=out_shape,
        in_specs=[rw(D_MODEL), rw(D_MODEL), rw(512), rw(512), rw(512), rw(512), rw(512),
                  rw(1024), rw(1024), rw(512), rw(Q_LORA), rw(KV_LORA), rw(LANES), rw(LANES),
                  rw(LANES), _full((1, D_MODEL)), _full((D_MODEL, D_EXT)), _full((1, Q_LORA)), _full((Q_LORA, 1024)),
                  _full((1, KV_LORA)), _full((KV_LORA, 1024)), _full((KV_LORA, 512))],
        out_specs=(rw(D_MODEL), pl.BlockSpec(memory_space=pl.ANY), _acc((Q_LORA, 1024)), _acc((KV_LORA, 1024)),
                   _acc((KV_LORA, 512)), _acc((8, D_MODEL))),
        scratch_shapes=[pltpu.VMEM((D_MODEL, D_EXT), F32)],
        compiler_params=pltpu.CompilerParams(vmem_limit_bytes=VMEM_DENSE),
    )(x, dxres, dsbq, dsbk, dsbv, dsbg, dmlag, dqc, dkc, dmv, cq, ckv, c_t, sa_t, sb_t, gpre, win, gq, wuq, gkv, wk, wv)


def _place():
    return lax.axis_index("x"), lax.axis_index("y"), lax.axis_index("c")


def _gather_steps(shapes, ins, bufs, send_sems, recv_sems):
    n = len(shapes)
    x, y, c = _place()
    me, sib = (x, y, c), (x, y, 1 - c)
    chips = [(1 - x, y), (x, 1 - y), (1 - x, 1 - y)]

    def half(t, chip, hc):
        rows = shapes[t][0] // 2
        return bufs[t].at[2 * chip[0] + chip[1], pl.ds(pl.multiple_of(hc * rows, 16), rows), :]

    def copy(k, t, chip, hc, to):
        return pltpu.make_async_remote_copy(src_ref=half(t, chip, hc), dst_ref=half(t, chip, hc), send_sem=send_sems.at[k],
                                            recv_sem=recv_sems.at[k], device_id=to, device_id_type=MESH)

    def start():
        for t in range(n):
            bufs[t][2 * x + y] = ins[t][...].astype(BF16)
            for j, chip in enumerate(chips):
                copy(6 * t + j, t, (x, y), c, (*chip, c)).start()

    def forward():
        for t in range(n):
            for j, chip in enumerate(chips):
                copy(6 * t + j, t, chip, c, me).wait_recv()
                copy(6 * t + 3 + j, t, chip, c, sib).start()

    def finish():
        for t in range(n):
            for j, chip in enumerate(chips):
                copy(6 * t + 3 + j, t, chip, 1 - c, me).wait_recv()
        for t in range(n):
            for j, chip in enumerate(chips):
                copy(6 * t + j, t, (x, y), c, (*chip, c)).wait_send()
                copy(6 * t + 3 + j, t, chip, c, sib).wait_send()

    return start, forward, finish


def _allgather_weights(shards):
    n = len(shards)

    def body(*refs):
        start, forward, finish = _gather_steps([a.shape for a in shards], refs[:n], refs[n:2 * n], refs[2 * n], refs[2 * n + 1])
        start()
        forward()
        finish()

    return pl.pallas_call(
        body, name="allgather_weights",
        out_shape=tuple(jax.ShapeDtypeStruct((N_SHARD,) + a.shape, BF16) for a in shards),
        in_specs=[pl.BlockSpec(memory_space=pltpu.VMEM)] * n, out_specs=(pl.BlockSpec(memory_space=pltpu.VMEM),) * n,
        scratch_shapes=[pltpu.SemaphoreType.DMA((6 * n,)), pltpu.SemaphoreType.DMA((6 * n,))],
        compiler_params=pltpu.CompilerParams(vmem_limit_bytes=VMEM_ATTN),
    )(*shards)


def _reduce_scratch(gsh):
    n = len(gsh)
    half_shapes = [(N_SHARD, a.shape[1] // 2, a.shape[2]) for a in gsh]
    return ([pltpu.VMEM(s_, F32) for s_ in half_shapes] * 2 + [pltpu.VMEM(s_, BF16) for s_ in half_shapes] * 2
            + [pltpu.SemaphoreType.DMA((n,)), pltpu.SemaphoreType.DMA((5 * n,)), pltpu.SemaphoreType.DMA((5 * n,))])


def _reduce_steps(halves, g_refs, f_refs, scratch):
    n = len(halves)
    accs, sibs, sbufs, rbufs = scratch[0:n], scratch[n:2 * n], scratch[2 * n:3 * n], scratch[3 * n:4 * n]
    local_sems, send_sems, recv_sems = scratch[4 * n:4 * n + 3]
    x, y, c = _place()
    me, sib = (x, y, c), (x, y, 1 - c)
    mine = 2 * x + y
    chips = [(1 - x, y), (x, 1 - y), (1 - x, 1 - y)]

    def remote(k, src, dst, to):
        return pltpu.make_async_remote_copy(src_ref=src, dst_ref=dst, send_sem=send_sems.at[k], recv_sem=recv_sems.at[k],
                                            device_id=to, device_id_type=MESH)

    def half3(ref, t, hc):
        return ref.at[:, pl.ds(pl.multiple_of(hc * halves[t], 8), halves[t]), :]

    def half2(ref, t, hc):
        return ref.at[pl.ds(pl.multiple_of(hc * halves[t], 8), halves[t]), :]

    def mine_load(t):
        return pltpu.make_async_copy(half3(g_refs[t], t, c), accs[t], local_sems.at[t])

    def to_sibling(t, to):
        return remote(t, half3(g_refs[t], t, 1 - c), sibs[t], to)

    def to_chip(t, j, chip, to):
        idx = 2 * chip[0] + chip[1]
        return remote(n + 3 * t + j, sbufs[t].at[idx], rbufs[t].at[mine if to is not me else idx], to)

    def swap(t, hc, to):
        return remote(4 * n + t, half2(f_refs[t], t, hc), half2(f_refs[t], t, hc), to)

    def load():
        for t in range(n):
            mine_load(t).start()
            to_sibling(t, sib).start()

    def partial():
        for t in range(n):
            mine_load(t).wait()
            to_sibling(t, me).wait_recv()
            for k in range(N_SHARD):
                accs[t][k] = accs[t][k] + sibs[t][k]
            for j, chip in enumerate(chips):
                idx = 2 * chip[0] + chip[1]
                sbufs[t][idx] = accs[t][idx].astype(BF16)
                to_chip(t, j, chip, (*chip, c)).start()

    def total():
        for t in range(n):
            acc = accs[t][mine]
            for j, chip in enumerate(chips):
                to_chip(t, j, chip, me).wait_recv()
                acc = acc + rbufs[t][2 * chip[0] + chip[1]].astype(F32)
            half2(f_refs[t], t, c)[...] = acc
            swap(t, c, sib).start()

    def finish():
        for t in range(n):
            swap(t, 1 - c, me).wait_recv()
        for t in range(n):
            to_sibling(t, sib).wait_send()
            for j, chip in enumerate(chips):
                to_chip(t, j, chip, (*chip, c)).wait_send()
            swap(t, c, sib).wait_send()

    return load, partial, total, finish


def _reduce_scatter_grads(gsh, vec):
    n = len(gsh)
    halves = [a.shape[1] // 2 for a in gsh]

    def body(*refs):
        g_refs, vec_ref, f_refs, vsum_ref = refs[:n], refs[n], refs[n + 1:2 * n + 1], refs[2 * n + 1]
        scratch = refs[2 * n + 2:]
        vrecv, vsend_sems, vrecv_sems = scratch[4 * n + 3:]
        load, partial, total, finish = _reduce_steps(halves, g_refs, f_refs, scratch)
        x, y, c = _place()
        my_dev = 4 * x + 2 * y + c

        def flip(k):
            return x ^ ((k >> 2) & 1), y ^ ((k >> 1) & 1), c ^ (k & 1)

        def vcopy(k, slot, to):
            return pltpu.make_async_remote_copy(src_ref=vec_ref, dst_ref=vrecv.at[slot], send_sem=vsend_sems.at[k - 1],
                                                recv_sem=vrecv_sems.at[k - 1], device_id=to, device_id_type=MESH)

        load()
        vrecv[my_dev] = vec_ref[...]
        for k in range(1, 8):
            vcopy(k, my_dev, flip(k)).start()
        partial()
        total()
        finish()
        for k in range(1, 8):
            fx, fy, fc = flip(k)
            vcopy(k, 4 * fx + 2 * fy + fc, (x, y, c)).wait_recv()
        vs = vrecv[0]
        for d in range(1, 8):
            vs = vs + vrecv[d]
        vsum_ref[...] = vs
        for k in range(1, 8):
            vcopy(k, my_dev, flip(k)).wait_send()

    return pl.pallas_call(
        body, name="reduce_scatter_grads",
        out_shape=tuple(jax.ShapeDtypeStruct(a.shape[1:], F32) for a in gsh) + (jax.ShapeDtypeStruct((VEC_ROWS, 1024), F32),),
        in_specs=[pl.BlockSpec(memory_space=pl.ANY)] * n + [pl.BlockSpec(memory_space=pltpu.VMEM)],
        out_specs=(pl.BlockSpec(memory_space=pltpu.VMEM),) * (n + 1),
        scratch_shapes=_reduce_scratch(gsh) + [pltpu.VMEM((8, VEC_ROWS, 1024), F32), pltpu.SemaphoreType.DMA((7,)),
                                               pltpu.SemaphoreType.DMA((7,))],
        compiler_params=pltpu.CompilerParams(vmem_limit_bytes=56 * 1024 * 1024),
    )(*gsh, vec)


def _adamw(w, g, m, v):
    rows, cols = w.shape
    tr = rows if rows <= 256 else 256

    def body(w_ref, g_ref, m_ref, v_ref, d_ref, nm_ref, nv_ref):
        d_ref[...], nm_ref[...], nv_ref[...] = _adam_math(w_ref[...], g_ref[...], m_ref[...], v_ref[...])

    spec = pl.BlockSpec((tr, cols), lambda i: (i, 0))
    shp = jax.ShapeDtypeStruct((rows, cols), F32)
    return pl.pallas_call(body, name="adamw", grid=(rows // tr,), out_shape=(shp, shp, shp),
                          in_specs=[spec] * 4, out_specs=(spec,) * 3)(w, g, m, v)


def _adam_math(w, g, m, v):
    m2 = ADAM_B1 * m + (1.0 - ADAM_B1) * g
    v2 = ADAM_B2 * v + (1.0 - ADAM_B2) * (g * g)
    m_hat = m2 / (1.0 - ADAM_B1 ** ADAM_STEP)
    v_hat = v2 / (1.0 - ADAM_B2 ** ADAM_STEP)
    return -ADAM_LR * (m_hat / (jnp.sqrt(v_hat) + ADAM_EPS) + ADAM_WD * w), m2, v2


def _adamw_small(vsum, w, m, v):
    names = [name for name, _, _, _ in _VEC_LAYOUT]
    k = len(names)

    def body(*refs):
        vs_ref, w_refs, m_refs, v_refs = refs[0], refs[1:1 + k], refs[1 + k:1 + 2 * k], refs[1 + 2 * k:1 + 3 * k]
        outs = refs[1 + 3 * k:]
        for idx, (_, r, c0, width) in enumerate(_VEC_LAYOUT):
            gv = vs_ref[pl.ds(r, 1), pl.ds(c0, width)]
            d, m2, v2 = _adam_math(w_refs[idx][...], gv, m_refs[idx][...], v_refs[idx][...])
            outs[idx][...], outs[k + idx][...], outs[2 * k + idx][...], outs[3 * k + idx][...] = gv, d, m2, v2

    shapes = tuple(jax.ShapeDtypeStruct(w[name].shape, F32) for name in names)
    res = pl.pallas_call(
        body, name="adamw_small", out_shape=shapes * 4,
        in_specs=[pl.BlockSpec(memory_space=pltpu.VMEM)] * (1 + 3 * k), out_specs=(pl.BlockSpec(memory_space=pltpu.VMEM),) * (4 * k),
    )(vsum, *[w[name] for name in names], *[m[name] for name in names], *[v[name] for name in names])
    return tuple({name: res[part * k + idx] for idx, name in enumerate(names)} for part in range(4))


_EARLY = ("w_in", "w_uq", "w_ukv")
_LATE = ("w_out", "w_ple", "w_ple_gate")
_BIG = _EARLY + _LATE
_KR_LOCAL = 2432 - 3 * (D_IN // N_SHARD)


def _extend_early(parts):
    cols = lambda a: a.transpose(1, 0, 2).reshape(a.shape[1], N_SHARD * a.shape[2])
    g = parts["w_in"]
    zeros = lambda n: jnp.zeros((D_MODEL, n), g.dtype)
    win_ext = jnp.concatenate([g[0], g[1], g[2], g[3][:, :_KR_LOCAL], zeros(64), g[3][:, _KR_LOCAL:_KR_LOCAL + QK_ROPE],
                               zeros(32), g[3][:, _KR_LOCAL + QK_ROPE:]], axis=1)
    wuq_ext = jnp.pad(cols(parts["w_uq"]).reshape(Q_LORA, 8, 96), ((0, 0), (0, 0), (0, 32))).reshape(Q_LORA, 1024)
    wukv = cols(parts["w_ukv"]).reshape(KV_LORA, 8, 128)
    wk_ext = jnp.pad(wukv[:, :, :64], ((0, 0), (0, 0), (0, 64))).reshape(KV_LORA, 1024)
    wv = wukv[:, :, 64:].reshape(KV_LORA, 512)
    return win_ext, wuq_ext, wk_ext, wv


def _shard_cols(a):
    return a.reshape(a.shape[0], N_SHARD, a.shape[1] // N_SHARD).transpose(1, 0, 2)


def _shard_rows(a):
    return a.reshape(N_SHARD, a.shape[0] // N_SHARD, a.shape[1])


def _shard_early_grads(dwin_ext, dwuq_ext, dwk_ext, dwv):
    e, w = dwin_ext, D_IN // N_SHARD
    last = jnp.concatenate([e[:, 3 * w:2432], e[:, 2496:2528], e[:, 2560:]], axis=1)
    dwuq = dwuq_ext.reshape(Q_LORA, 8, 128)[:, :, :96].reshape(Q_LORA, 768)
    dwukv = jnp.concatenate([dwk_ext.reshape(KV_LORA, 8, 128)[:, :, :64], dwv.reshape(KV_LORA, 8, 64)], axis=2)
    return [jnp.stack([e[:, 0:w], e[:, w:2 * w], e[:, 2 * w:3 * w], last]), _shard_cols(dwuq),
            _shard_cols(dwukv.reshape(KV_LORA, 1024))]


def _rope_tables(positions):
    half = QK_ROPE // 2
    freq = ROPE_THETA ** (-jnp.arange(half, dtype=F32) / half)
    ang = positions.astype(F32)[:, None] * freq
    cos, sin = jnp.cos(ang), jnp.sin(ang)
    s = positions.shape[0]
    z = lambda n: jnp.zeros((s, n), F32)
    c_t = jnp.concatenate([jnp.ones((s, 64), F32), cos, cos, z(32)], axis=1)
    sa_t = jnp.concatenate([z(64), -sin, z(16), z(32)], axis=1)
    sb_t = jnp.concatenate([z(64), z(16), sin, z(32)], axis=1)
    return c_t, sa_t, sb_t


def _local_grads(x, p, positions, tgt, gains, early, late):
    win_ext, wuq_ext, wk_ext, wv = _extend_early(early)
    tabs = _rope_tables(positions)
    g = gains
    sbq, sbk, sbv, sbg, mlag, cq, ckv, qc, kc, mv, sbkt, sbvt, kct, mvt = _pre_fwd(
        x, tabs, g["norm_pre_g"], win_ext, g["q_norm_g"], wuq_ext, g["kv_norm_g"], wk_ext, wv)
    sbo, wout4, wple4, wpg4 = _sb_fwd(sbq, sbk, sbvt, late)
    wout, wpg = wout4.reshape(D_MODEL, D_MODEL), wpg4.reshape(D_MODEL, D_MODEL)
    wple = wple4.transpose(1, 0, 2).reshape(PLE_DIM, D_MODEL)
    mlao, lse = _mla_fwd(qc, kc, mvt)
    dsbo, dmlao, delta, dsbg, dmlag, dxres, dwout, dwpg, dwple, vec_c = _post(
        x, p, tgt, sbo, mlao, sbg, mlag, g["sb_out_norm_g"], g["mla_out_norm_g"], wout, g["norm_post_g"], wple,
        g["ple_norm_g"], wpg, g["b_ple_gate"])
    dsbq, dsbk, dsbv, *late_grads = _sb_bwd(sbq, sbk, sbkt, sbv, dsbo, [_shard_rows(dwout), _shard_cols(dwple), _shard_rows(dwpg)])
    dqc, dkc, dmv = _mla_bwd(qc, kc, kct, mv, dmlao, lse, delta)
    gx, dwin_ext, dwuq_ext, dwk_ext, dwv, vec_d = _pre_bwd(
        x, dxres, dsbq, dsbk, dsbv, dsbg, dmlag, dqc, dkc, dmv, cq, ckv, tabs, g["norm_pre_g"], win_ext, g["q_norm_g"],
        wuq_ext, g["kv_norm_g"], wk_ext, wv)
    return gx, _shard_early_grads(dwin_ext, dwuq_ext, dwk_ext, dwv), late_grads, jnp.concatenate([vec_c, vec_d], axis=0)


_VEC_LAYOUT = (("norm_post_g", 0, 0, 1024), ("ple_norm_g", 1, 0, 1024), ("b_ple_gate", 2, 0, 1024), ("sb_out_norm_g", 3, 0, 512),
               ("mla_out_norm_g", 3, 512, 512), ("norm_pre_g", 8, 0, 1024), ("q_norm_g", 9, 0, 256), ("kv_norm_g", 9, 256, 128))
_LOSS_ROW = 4
_WEIGHT_ORDER = ("norm_pre_g", "w_in", "q_norm_g", "w_uq", "kv_norm_g", "w_ukv", "sb_out_norm_g", "mla_out_norm_g", "w_out",
                 "norm_post_g", "w_ple", "ple_norm_g", "w_ple_gate", "b_ple_gate")


def kernel(x, p, positions, norm_pre_g, w_in, q_norm_g, w_uq, kv_norm_g, w_ukv, sb_out_norm_g, mla_out_norm_g, w_out, norm_post_g, w_ple, ple_norm_g, w_ple_gate, b_ple_gate, loss_target, m_norm_pre_g, m_w_in, m_q_norm_g, m_w_uq, m_kv_norm_g, m_w_ukv, m_sb_out_norm_g, m_mla_out_norm_g, m_w_out, m_norm_post_g, m_w_ple, m_ple_norm_g, m_w_ple_gate, m_b_ple_gate, v_norm_pre_g, v_w_in, v_q_norm_g, v_w_uq, v_kv_norm_g, v_w_ukv, v_sb_out_norm_g, v_mla_out_norm_g, v_w_out, v_norm_post_g, v_w_ple, v_ple_norm_g, v_w_ple_gate, v_b_ple_gate):
    w = {"norm_pre_g": norm_pre_g, "w_in": w_in[0], "q_norm_g": q_norm_g, "w_uq": w_uq[0], "kv_norm_g": kv_norm_g, "w_ukv": w_ukv[0],
         "sb_out_norm_g": sb_out_norm_g, "mla_out_norm_g": mla_out_norm_g, "w_out": w_out[0], "norm_post_g": norm_post_g,
         "w_ple": w_ple[0], "ple_norm_g": ple_norm_g, "w_ple_gate": w_ple_gate[0], "b_ple_gate": b_ple_gate}
    m = {"norm_pre_g": m_norm_pre_g, "w_in": m_w_in[0], "q_norm_g": m_q_norm_g, "w_uq": m_w_uq[0], "kv_norm_g": m_kv_norm_g,
         "w_ukv": m_w_ukv[0], "sb_out_norm_g": m_sb_out_norm_g, "mla_out_norm_g": m_mla_out_norm_g, "w_out": m_w_out[0],
         "norm_post_g": m_norm_post_g, "w_ple": m_w_ple[0], "ple_norm_g": m_ple_norm_g, "w_ple_gate": m_w_ple_gate[0],
         "b_ple_gate": m_b_ple_gate}
    v = {"norm_pre_g": v_norm_pre_g, "w_in": v_w_in[0], "q_norm_g": v_q_norm_g, "w_uq": v_w_uq[0], "kv_norm_g": v_kv_norm_g,
         "w_ukv": v_w_ukv[0], "sb_out_norm_g": v_sb_out_norm_g, "mla_out_norm_g": v_mla_out_norm_g, "w_out": v_w_out[0],
         "norm_post_g": v_norm_post_g, "w_ple": v_w_ple[0], "ple_norm_g": v_ple_norm_g, "w_ple_gate": v_w_ple_gate[0],
         "b_ple_gate": v_b_ple_gate}
    gathered = _allgather_weights([w[n] for n in _EARLY])
    gx, early_grads, late_red, vec = _local_grads(x[0], p[0, 0], positions[0], loss_target[0], w, dict(zip(_EARLY, gathered)),
                                                  [w[n] for n in _LATE])
    *early_red, vsum = _reduce_scatter_grads(early_grads, vec)
    gred = early_red + late_red
    loss = vsum[_LOSS_ROW, 0]

    g, delta, new_m, new_v = _adamw_small(vsum, w, m, v)
    for n, gn in zip(_BIG, gred):
        g[n] = gn
        delta[n], new_m[n], new_v[n] = _adamw(w[n], gn, m[n], v[n])

    lead = lambda n, a: a[None] if n in _BIG else a
    return (loss, gx[None],
            *[lead(n, g[n]) for n in _WEIGHT_ORDER], *[lead(n, delta[n]) for n in _WEIGHT_ORDER],
            *[lead(n, new_m[n]) for n in _WEIGHT_ORDER], *[lead(n, new_v[n]) for n in _WEIGHT_ORDER])
```

```python
import numpy as np
import jax
import jax.numpy as jnp
from jax import lax
from jax.experimental import pallas as pl
from jax.experimental.pallas import tpu as pltpu

F32 = jnp.float32
BF16 = jnp.bfloat16
MESH = pl.DeviceIdType.MESH

D_MODEL = 1024
HEAD_DIM = 64
D_SB = 512
D_MLA = 512
Q_LORA = 256
KV_LORA = 128
QK_NOPE = 64
QK_ROPE = 32
PLE_DIM = 256
D_IN = 2976
D_EXT = 3072
ROPE_THETA = 10000.0
EPS = 1e-6
N_SHARD = 4

ADAM_LR = 0.001
ADAM_B1 = 0.9
ADAM_B2 = 0.999
ADAM_EPS = 1e-08
ADAM_WD = 0.01
ADAM_STEP = 10

LANES = 128
BK = 128
WQ = 256
MQ = 512
SB_CUTOFF = 120.0
TM = 256
TM_PRE = 256
VEC_ROWS = 16
VMEM_DENSE = 52 * 1024 * 1024
VMEM_ATTN = 40 * 1024 * 1024


def _mm(a, b):
    return jnp.dot(a, b, preferred_element_type=F32)


def _mm_nt(a, b):
    return lax.dot_general(a, b, (((1,), (1,)), ((), ())), preferred_element_type=F32)


def _mm_tn(a, b):
    return lax.dot_general(a, b, (((0,), (0,)), ((), ())), preferred_element_type=F32)


def _seg(a, bd2):
    return _mm(_split2(a), bd2)


def _const(mask):
    return jnp.asarray(np.asarray(mask, np.float32), dtype=BF16)


def _blockdiag2(n, seg):
    r = (np.arange(2 * n)[:, None] % n) // seg
    c = np.arange(n)[None, :] // seg
    return _const(r == c)


def _sigmoid(a):
    return 1.0 / (1.0 + jnp.exp(-a))


def _rowmean(a):
    return jnp.mean(a, axis=-1, keepdims=True)


def _colsum(a):
    return jnp.sum(a, axis=0, keepdims=True)


def _rope_fwd(a, c, sa, sb):
    w = a.shape[-1]
    return a * c + pltpu.roll(a, w - 16, 1) * sa + pltpu.roll(a, 16, 1) * sb


def _rope_bwd(g, c, sa, sb):
    w = g.shape[-1]
    return g * c + pltpu.roll(g * sa, 16, 1) + pltpu.roll(g * sb, w - 16, 1)


def _full(shape):
    return pl.BlockSpec(shape, lambda *_: (0,) * len(shape))


def _acc(shape):
    return pl.BlockSpec(shape, lambda *_: (0,) * len(shape))


def _full2(shape):
    return pl.BlockSpec(shape, lambda p, i: (0, 0))


def _cols(height, tm=TM):
    return pl.BlockSpec((height, tm), lambda i: (0, i))


def _rows(width, tm=TM):
    return pl.BlockSpec((tm, width), lambda i: (i, 0))


def _pre_fwd(x, tabs, gpre, win, gq, wuq, gkv, wk, wv):
    s = x.shape[0]
    c_t, sa_t, sb_t = tabs
    rw, cl = (lambda width: _rows(width, TM_PRE)), (lambda height: _cols(height, TM_PRE))

    def body(x_ref, c_ref, sa_ref, sb_ref, gpre_ref, win_ref, gq_ref, wuq_ref, gkv_ref, wk_ref, wv_ref,
             sbq_ref, sbk_ref, sbv_ref, sbg_ref, mlag_ref, cq_ref, ckv_ref, qc_ref, kc_ref, mv_ref,
             sbkt_ref, sbvt_ref, kct_ref, mvt_ref):
        xv = x_ref[...]
        r1 = lax.rsqrt(_rowmean(xv * xv) + EPS)
        h = (xv * r1 * gpre_ref[...]).astype(BF16)
        proj = _mm(h, win_ref[...])
        sbq_ref[...] = proj[:, 0:512].astype(BF16)
        sbk_ref[...] = proj[:, 512:1024].astype(BF16)
        sbv_ref[...] = proj[:, 1024:1536].astype(BF16)
        sbkt_ref[...] = proj[:, 512:1024].T.astype(BF16)
        sbvt_ref[...] = proj[:, 1024:1536].T.astype(BF16)
        sbg_ref[...] = proj[:, 1536:2048]
        cq = proj[:, 2048:2304]
        ckv = proj[:, 2304:2432]
        kr = proj[:, 2432:2560]
        mlag_ref[...] = proj[:, 2560:3072]
        cq_ref[...] = cq
        ckv_ref[...] = ckv
        c1, sa1, sb1 = c_ref[...], sa_ref[...], sb_ref[...]
        c8, sa8, sb8 = jnp.tile(c1, (1, 8)), jnp.tile(sa1, (1, 8)), jnp.tile(sb1, (1, 8))
        cqn = (cq * lax.rsqrt(_rowmean(cq * cq) + EPS) * gq_ref[...]).astype(BF16)
        qe = _mm(cqn, wuq_ref[...])
        qc_ref[...] = _rope_fwd(qe, c8, sa8, sb8).astype(BF16)
        ckvn = (ckv * lax.rsqrt(_rowmean(ckv * ckv) + EPS) * gkv_ref[...]).astype(BF16)
        ke = _mm(ckvn, wk_ref[...])
        krr = _rope_fwd(kr, c1, sa1, sb1)
        kcat = ke + jnp.tile(krr, (1, 8))
        kc_ref[...] = kcat.astype(BF16)
        kct_ref[...] = kcat.T.astype(BF16)
        mval = _mm(ckvn, wv_ref[...])
        mv_ref[...] = mval.astype(BF16)
        mvt_ref[...] = mval.T.astype(BF16)

    out_shape = (
        jax.ShapeDtypeStruct((s, 512), BF16), jax.ShapeDtypeStruct((s, 512), BF16), jax.ShapeDtypeStruct((s, 512), BF16),
        jax.ShapeDtypeStruct((s, 512), F32), jax.ShapeDtypeStruct((s, 512), F32),
        jax.ShapeDtypeStruct((s, Q_LORA), F32), jax.ShapeDtypeStruct((s, KV_LORA), F32),
        jax.ShapeDtypeStruct((s, 1024), BF16), jax.ShapeDtypeStruct((s, 1024), BF16), jax.ShapeDtypeStruct((s, 512), BF16),
        jax.ShapeDtypeStruct((512, s), BF16), jax.ShapeDtypeStruct((512, s), BF16), jax.ShapeDtypeStruct((1024, s), BF16),
        jax.ShapeDtypeStruct((512, s), BF16),
    )
    return pl.pallas_call(
        body, name="pre_fwd", grid=(s // TM_PRE,), out_shape=out_shape,
        in_specs=[rw(D_MODEL), rw(LANES), rw(LANES), rw(LANES), _full((1, D_MODEL)), _full((D_MODEL, D_EXT)),
                  _full((1, Q_LORA)), _full((Q_LORA, 1024)), _full((1, KV_LORA)), _full((KV_LORA, 1024)), _full((KV_LORA, 512))],
        out_specs=(rw(512), rw(512), rw(512), rw(512), rw(512), rw(Q_LORA), rw(KV_LORA),
                   rw(1024), rw(1024), rw(512), cl(512), cl(512), cl(1024), cl(512)),
        compiler_params=pltpu.CompilerParams(vmem_limit_bytes=VMEM_DENSE),
    )(x, c_t, sa_t, sb_t, gpre, win, gq, wuq, gkv, wk, wv)


def _softplus(z):
    neg_abs = lax.bitcast_convert_type(lax.bitcast_convert_type(z, jnp.uint32) | jnp.uint32(0x80000000), F32)
    return jnp.maximum(z, 0.0) + jnp.log(1.0 + jnp.exp(neg_abs))


def _sum_matrix(kind, terms):
    r, c = np.arange(2 * BK)[:, None], np.arange(2 * BK * terms)[None, :] % (2 * BK)
    rk, ck = r % BK, c % BK
    return _const(((r // BK) == (c // BK)) & {"suffix": ck >= rk, "prefix": ck <= rk}[kind])


def _split_rows(a):
    hi = a.astype(BF16)
    return jnp.concatenate([hi, (a - hi.astype(F32)).astype(BF16)], axis=0)


def _heads_t(blk, rowi):
    zero = jnp.zeros_like(blk)
    return jnp.concatenate([jnp.where(rowi < 64, blk, zero), jnp.where(rowi >= 64, blk, zero)], axis=1)


def _mask_keys(a, valid, fill=0.0):
    return jnp.concatenate([jnp.where(valid, a[0:BK], fill), jnp.where(valid, a[BK:2 * BK], fill)], axis=0)


def _split2(a):
    hi = a.astype(BF16)
    lo = (a - hi.astype(F32)).astype(BF16)
    return jnp.concatenate([hi, lo], axis=1)


def _pair_stack(b, lane):
    zero = jnp.zeros_like(b)
    return jnp.concatenate([jnp.where(lane < 64, b, zero), jnp.where(lane >= 64, b, zero)], axis=0)


def _sb_fwd(q, k, vt, late):
    s = q.shape[0]
    n = len(late)

    def body(q_ref, k_ref, vt_ref, usuf_ref, *rest):
        ins, o_ref, outs = rest[:n], rest[n], rest[n + 1:2 * n + 1]
        acc_scr, run_scr = rest[2 * n + 1:2 * n + 3]
        bufs, (send_sems, recv_sems, out_sems) = rest[2 * n + 3:3 * n + 3], rest[3 * n + 3:]
        p, i = pl.program_id(0), pl.program_id(1)
        gather_start, gather_forward, gather_finish = _gather_steps([a.shape for a in late], ins, bufs, send_sems, recv_sems)

        @pl.when((p == 0) & (i == 0))
        def _():
            gather_start()

        @pl.when((p == 2) & (i == 0))
        def _():
            gather_forward()

        lane = lax.broadcasted_iota(jnp.int32, (1, LANES), 1)
        rowi = lax.broadcasted_iota(jnp.int32, (LANES, 1), 0)
        keyi = lax.broadcasted_iota(jnp.int32, (BK, WQ), 0)
        qryi = lax.broadcasted_iota(jnp.int32, (BK, WQ), 1) + i * WQ
        qs = q_ref[...] * (HEAD_DIM ** -0.5)

        def group(blocks, masked):
            starts = [pl.multiple_of(j * BK, BK) for j in blocks]
            valid = [(keyi + j * BK) < qryi if m else None for j, m in zip(blocks, masked)]
            zs = [_mm_nt(_pair_stack(k_ref[pl.ds(ks, BK), :], lane), qs) for ks in starts]
            sps = [_softplus(z) for z in zs]
            sps = [sp if ok is None else _mask_keys(sp, ok) for sp, ok in zip(sps, valid)]
            cums = [_mm(usuf_ref[...], _split_rows(sp)) for sp in sps]
            ws = [jnp.exp(z - c) for z, c in zip(zs, cums)]
            ws = [w if ok is None else _mask_keys(w, ok) for w, ok in zip(ws, valid)]
            pvs = [_mm(_heads_t(vt_ref[:, pl.ds(ks, BK)], rowi), w.astype(BF16)) for ks, w in zip(starts, ws)]
            for pv, c in zip(pvs, cums):
                r0, r1 = run_scr[0:1, :], run_scr[1:2, :]
                acc_scr[...] += jnp.where(rowi < 64, jnp.exp(-r0), jnp.exp(-r1)) * pv
                run_scr[0:1, :] = r0 + c[0:1]
                run_scr[1:2, :] = r1 + c[BK:BK + 1]

        assert WQ == 2 * BK
        acc_scr[...] = jnp.zeros_like(acc_scr)
        run_scr[...] = jnp.zeros_like(run_scr)

        @pl.when(i == 0)
        def _():
            group([1, 0], [True, True])

        @pl.when(i > 0)
        def _():
            group([2 * i + 1, 2 * i, 2 * i - 1, 2 * i - 2], [True, True, False, False])

        def unfinished():
            return (jnp.min(run_scr[0:2, :]) < SB_CUTOFF).astype(jnp.int32)

        def step(c):
            group([2 * i - 1 - 2 * c[0], 2 * i - 2 - 2 * c[0]], [False, False])
            return c[0] + 1, unfinished()

        lax.while_loop(lambda c: (c[0] < i) & (c[1] > 0), step, (jnp.int32(1), unfinished()))
        o_ref[...] = acc_scr[...].T

        @pl.when((p == pl.num_programs(0) - 1) & (i == pl.num_programs(1) - 1))
        def _():
            gather_finish()
            copies = [pltpu.make_async_copy(bufs[t], outs[t], out_sems.at[t]) for t in range(n)]
            for cp in copies:
                cp.start()
            for cp in copies:
                cp.wait()

    qspec = pl.BlockSpec((WQ, LANES), lambda p, i: (i, p))
    kspec = pl.BlockSpec((s, LANES), lambda p, i: (0, p))
    tspec = pl.BlockSpec((LANES, s), lambda p, i: (p, 0))
    gathered = [jax.ShapeDtypeStruct((N_SHARD,) + a.shape, BF16) for a in late]
    return pl.pallas_call(
        body, name="sb_fwd", grid=(4, s // WQ),
        out_shape=(jax.ShapeDtypeStruct((s, 512), F32), *gathered),
        in_specs=[qspec, kspec, tspec, _full2((2 * BK, 4 * BK))] + [_full2(a.shape) for a in late],
        out_specs=(qspec,) + (pl.BlockSpec(memory_space=pl.ANY),) * n,
        scratch_shapes=[pltpu.VMEM((LANES, WQ), F32), pltpu.VMEM((8, WQ), F32)] + [pltpu.VMEM(g.shape, BF16) for g in gathered]
                       + [pltpu.SemaphoreType.DMA((6 * n,)), pltpu.SemaphoreType.DMA((6 * n,)), pltpu.SemaphoreType.DMA((n,))],
        compiler_params=pltpu.CompilerParams(vmem_limit_bytes=VMEM_ATTN),
    )(q, k, vt, _sum_matrix("suffix", 2), *late)


def _sb_bwd(q, k, kt, v, do, late):
    s = q.shape[0]
    n = len(late)
    halves = [a.shape[1] // 2 for a in late]

    def body(q_ref, k_ref, kt_ref, v_ref, do_ref, usuf_ref, upre_ref, *rest):
        g_refs, (dq_ref, dk_ref, dv_ref), outs = rest[:n], rest[n:n + 3], rest[n + 3:2 * n + 3]
        later_scr, dqt_scr, st_scr = rest[2 * n + 3:2 * n + 6]
        f_scr, reduce_scr, out_sems = rest[2 * n + 6:3 * n + 6], rest[3 * n + 6:-1], rest[-1]
        p, i = pl.program_id(0), pl.program_id(1)
        reduce_load, reduce_partial, reduce_total, reduce_finish = _reduce_steps(halves, g_refs, f_scr, reduce_scr)

        @pl.when((p == 0) & (i == 0))
        def _():
            reduce_load()

        @pl.when((p == 1) & (i == 0))
        def _():
            reduce_partial()

        @pl.when((p == 3) & (i == 0))
        def _():
            reduce_total()

        @pl.when(i == 0)
        def _():
            dk_ref[...] = jnp.zeros_like(dk_ref)
            dv_ref[...] = jnp.zeros_like(dv_ref)

        lane = lax.broadcasted_iota(jnp.int32, (1, LANES), 1)
        rowi = lax.broadcasted_iota(jnp.int32, (LANES, 1), 0)
        keyi = lax.broadcasted_iota(jnp.int32, (BK, WQ), 0)
        qryi = lax.broadcasted_iota(jnp.int32, (BK, WQ), 1) + i * WQ
        qs = q_ref[...] * (HEAD_DIM ** -0.5)
        dob = do_ref[...]
        dot = dob.astype(F32).T.astype(BF16)

        def scores(j):
            return _mm_nt(_pair_stack(k_ref[pl.ds(pl.multiple_of(j * BK, BK), BK), :], lane), qs)

        def scan(blocks, masked):
            sps = [_softplus(scores(j)) for j in blocks]
            sps = [_mask_keys(sp, (keyi + j * BK) < qryi) if m else sp for sp, j, m in zip(sps, blocks, masked)]
            for sp, j in zip(sps, blocks):
                run = st_scr[0:2, :]
                later_scr[j, 0:2, :] = run
                st_scr[0:2, :] = run + jnp.concatenate([jnp.sum(sp[0:BK], axis=0, keepdims=True),
                                                        jnp.sum(sp[BK:2 * BK], axis=0, keepdims=True)], axis=0)

        def sweep(blocks, masked):
            starts = [pl.multiple_of(j * BK, BK) for j in blocks]
            valid = [(keyi + j * BK) < qryi if m else None for j, m in zip(blocks, masked)]
            zs = [scores(j) for j in blocks]
            us = [jnp.exp(lax.bitcast_convert_type(lax.bitcast_convert_type(z, jnp.uint32) | jnp.uint32(0x80000000), F32))
                  for z in zs]
            sps = [jnp.maximum(z, 0.0) + jnp.log(1.0 + u) for z, u in zip(zs, us)]
            sps = [sp if ok is None else _mask_keys(sp, ok) for sp, ok in zip(sps, valid)]
            sigs = [jnp.where(z >= 0.0, 1.0, u) / (1.0 + u) for z, u in zip(zs, us)]
            cums = [_mm(usuf_ref[...], _split_rows(sp)) for sp in sps]
            dws = [_mm(_pair_stack(v_ref[pl.ds(ks, BK), :], lane), dot) for ks in starts]
            wfs = []
            for z, c, j, ok in zip(zs, cums, blocks, valid):
                f = jnp.exp(-later_scr[j, 0:2, :])
                wf = jnp.exp(z - c) * jnp.concatenate([jnp.broadcast_to(f[0:1], (BK, WQ)), jnp.broadcast_to(f[1:2], (BK, WQ))], axis=0)
                wfs.append(wf if ok is None else _mask_keys(wf, ok))
            es = [dw * wf for dw, wf in zip(dws, wfs)]
            pres = [_mm(upre_ref[...], e.astype(BF16)) for e in es]
            dzs = []
            for e, pre, sig, ok in zip(es, pres, sigs, valid):
                e0 = pre[0:BK] + st_scr[0:1, :]
                e1 = pre[BK:2 * BK] + st_scr[1:2, :]
                st_scr[0:1, :] = e0[BK - 1:BK]
                st_scr[1:2, :] = e1[BK - 1:BK]
                dz = e - sig * jnp.concatenate([e0, e1], axis=0)
                dzs.append((dz if ok is None else _mask_keys(dz, ok)).astype(BF16))
            dqt_scr[...] += _mm(jnp.concatenate([_heads_t(kt_ref[:, pl.ds(ks, BK)], rowi) for ks in starts], axis=1),
                                jnp.concatenate(dzs, axis=0))
            for ks, dz, wf in zip(starts, dzs, wfs):
                rk = _mm(dz, qs)
                dk_ref[pl.ds(ks, BK), :] += jnp.where(lane < 64, rk[0:BK], rk[BK:2 * BK])
                rv = _mm(wf.astype(BF16), dob)
                dv_ref[pl.ds(ks, BK), :] += jnp.where(lane < 64, rv[0:BK], rv[BK:2 * BK])

        assert WQ == 2 * BK
        st_scr[...] = jnp.zeros_like(st_scr)

        @pl.when(i == 0)
        def _():
            scan([1, 0], [True, True])

        @pl.when(i > 0)
        def _():
            scan([2 * i + 1, 2 * i, 2 * i - 1, 2 * i - 2], [True, True, False, False])

        def unfinished():
            return (jnp.min(st_scr[0:2, :]) < SB_CUTOFF).astype(jnp.int32)

        def step(c):
            scan([2 * i - 1 - 2 * c[0], 2 * i - 2 - 2 * c[0]], [False, False])
            return c[0] + 1, unfinished()

        npairs, _ = lax.while_loop(lambda c: (c[0] < i) & (c[1] > 0), step, (jnp.minimum(i, 1), unfinished()))

        st_scr[...] = jnp.zeros_like(st_scr)
        dqt_scr[...] = jnp.zeros_like(dqt_scr)
        first = 2 * (i - npairs)

        def early(t, carry):
            sweep([first + 2 * t, first + 2 * t + 1], [False, False])
            return carry

        lax.fori_loop(0, npairs - 1, early, 0)

        @pl.when(i == 0)
        def _():
            sweep([0, 1], [True, True])

        @pl.when(i > 0)
        def _():
            sweep([2 * i - 2, 2 * i - 1, 2 * i, 2 * i + 1], [False, False, True, True])

        dq_ref[...] = (dqt_scr[...].T * (HEAD_DIM ** -0.5)).astype(BF16)

        @pl.when((p == pl.num_programs(0) - 1) & (i == pl.num_programs(1) - 1))
        def _():
            reduce_finish()
            copies = [pltpu.make_async_copy(f_scr[t], outs[t], out_sems.at[t]) for t in range(n)]
            for cp in copies:
                cp.start()
            for cp in copies:
                cp.wait()

    qspec = pl.BlockSpec((WQ, LANES), lambda p, i: (i, p))
    kspec = pl.BlockSpec((s, LANES), lambda p, i: (0, p))
    tspec = pl.BlockSpec((LANES, s), lambda p, i: (p, 0))
    anywhere = pl.BlockSpec(memory_space=pl.ANY)
    reduced = [jax.ShapeDtypeStruct(a.shape[1:], F32) for a in late]
    return pl.pallas_call(
        body, name="sb_bwd", grid=(4, s // WQ),
        out_shape=(jax.ShapeDtypeStruct((s, 512), BF16), jax.ShapeDtypeStruct((s, 512), F32),
                   jax.ShapeDtypeStruct((s, 512), F32), *reduced),
        in_specs=[qspec, kspec, tspec, kspec, qspec, _full2((2 * BK, 4 * BK)), _full2((2 * BK, 2 * BK))] + [anywhere] * n,
        out_specs=(qspec, kspec, kspec) + (anywhere,) * n,
        scratch_shapes=[pltpu.VMEM((s // BK, 8, WQ), F32), pltpu.VMEM((LANES, WQ), F32), pltpu.VMEM((8, WQ), F32)]
                       + [pltpu.VMEM(r.shape, F32) for r in reduced] + _reduce_scratch(late) + [pltpu.SemaphoreType.DMA((n,))],
        compiler_params=pltpu.CompilerParams(vmem_limit_bytes=VMEM_ATTN),
    )(q, k, kt, v, do, _sum_matrix("suffix", 2), _sum_matrix("prefix", 1), *late)


MLA_SCALE = (QK_NOPE + QK_ROPE) ** -0.5
LOG2E = 1.4426950408889634


def _mla_keys(kb):
    zero = jnp.zeros((BK, LANES), kb.dtype)
    return jnp.concatenate([jnp.concatenate([kb[:, 0:LANES], zero], axis=1),
                            jnp.concatenate([zero, kb[:, LANES:2 * LANES]], axis=1)], axis=0)


def _mla_fwd(qc, kc, vt):
    s = qc.shape[0]
    rows_l = 16

    def body(q_ref, k_ref, vt_ref, o_ref, l_ref, p_scr, ot_scr, st_scr):
        i = pl.program_id(1)
        keyc = lax.broadcasted_iota(jnp.int32, (BK, MQ), 0)
        qryc = (lax.broadcasted_iota(jnp.int32, (BK, MQ), 1) + i * MQ) // 64
        row = lax.broadcasted_iota(jnp.int32, (LANES, 1), 0)
        qw = q_ref[...]
        orow = lax.broadcasted_iota(jnp.int32, (rows_l, 2 * BK), 0)
        ocol = lax.broadcasted_iota(jnp.int32, (rows_l, 2 * BK), 1)
        ones = jnp.where(((orow == 0) & (ocol < BK)) | ((orow == 1) & (ocol >= BK)), 1.0, 0.0).astype(BF16)

        def scores(j, lo=0):
            ks = pl.multiple_of(j * BK, BK)
            return _mm_nt(_mla_keys(k_ref[pl.ds(ks, BK), :]), qw[lo:])

        def values_t(j):
            vtb = vt_ref[:, pl.ds(pl.multiple_of(j * BK, BK), BK)]
            zero = jnp.zeros_like(vtb)
            top = jnp.concatenate([jnp.where(row < 64, vtb, zero), jnp.where(row >= 64, vtb, zero)], axis=1)
            return jnp.concatenate([top, ones], axis=0)

        def softmax(ja, za, zb, masked, lo=0):
            c = MLA_SCALE * LOG2E
            parts = [za[0:BK] * c, za[BK:2 * BK] * c, zb[0:BK] * c, zb[BK:2 * BK] * c]
            if masked:
                va = ((keyc[:, lo:] + ja * BK) // 64) <= qryc[:, lo:]
                vb = ((keyc[:, lo:] + (ja + 1) * BK) // 64) <= qryc[:, lo:]
                parts = [jnp.where(va, parts[0], -1e30), jnp.where(va, parts[1], -1e30),
                         jnp.where(vb, parts[2], -1e30), jnp.where(vb, parts[3], -1e30)]
            m0, m1 = st_scr[0:1, lo:], st_scr[1:2, lo:]
            n0 = jnp.maximum(m0, jnp.max(jnp.maximum(parts[0], parts[2]), axis=0, keepdims=True))
            n1 = jnp.maximum(m1, jnp.max(jnp.maximum(parts[1], parts[3]), axis=0, keepdims=True))
            st_scr[2:3, lo:] = jnp.exp2(m0 - n0)
            st_scr[3:4, lo:] = jnp.exp2(m1 - n1)
            st_scr[0:1, lo:] = n0
            st_scr[1:2, lo:] = n1
            p_scr[:, lo:] = jnp.concatenate([jnp.exp2(parts[0] - n0), jnp.exp2(parts[1] - n1),
                                             jnp.exp2(parts[2] - n0), jnp.exp2(parts[3] - n1)], axis=0).astype(BF16)

        def accumulate(ja, lo=0):
            pv = _mm(jnp.concatenate([values_t(ja), values_t(ja + 1)], axis=1), p_scr[:, lo:])
            a = jnp.where(row < 64, st_scr[2:3, lo:], st_scr[3:4, lo:])
            ot_scr[0:LANES, lo:] = a * ot_scr[0:LANES, lo:] + pv[0:LANES]
            ot_scr[LANES:LANES + 8, lo:] = st_scr[2:10, lo:] * ot_scr[LANES:LANES + 8, lo:] + pv[LANES:LANES + 8]

        def step(n, masked, lo=0, prev_lo=0):
            za, zb = scores(2 * n, lo), scores(2 * n + 1, lo)
            accumulate(2 * n - 2, prev_lo)
            softmax(2 * n, za, zb, masked, lo)

        def first(masked):
            softmax(0, scores(0), scores(1), masked)

        st_scr[...] = jnp.concatenate([jnp.full((2, MQ), -1e30, F32), jnp.ones((14, MQ), F32)], axis=0)
        ot_scr[...] = jnp.zeros_like(ot_scr)

        npq = MQ // (2 * BK)
        seen = lambda d: 2 * BK * max(d, 0)

        @pl.when(i == 0)
        def _():
            first(True)
            for d in range(1, npq):
                step(d, True, seen(d), seen(d - 1))

        @pl.when(i > 0)
        def _():
            first(False)
            lax.fori_loop(1, npq * i, lambda n, c: (step(n, False), c)[1], 0)
            for d in range(npq):
                step(npq * i + d, True, seen(d), seen(d - 1))

        accumulate(2 * (npq * (i + 1) - 1), seen(npq - 1))
        l0, l1 = ot_scr[LANES:LANES + 1, :], ot_scr[LANES + 1:LANES + 2, :]
        o_ref[...] = (ot_scr[0:LANES, :] / jnp.where(row < 64, l0, l1)).T
        l_ref[...] = jnp.where(row < 64, st_scr[0:1, :] + jnp.log2(l0), st_scr[1:2, :] + jnp.log2(l1)).T

    qspec = pl.BlockSpec((MQ, 2 * LANES), lambda p, i: (i, p))
    kspec = pl.BlockSpec((s, 2 * LANES), lambda p, i: (0, p))
    vtspec = pl.BlockSpec((LANES, s), lambda p, i: (p, 0))
    ospec = pl.BlockSpec((MQ, LANES), lambda p, i: (i, p))
    return pl.pallas_call(
        body, name="mla_fwd", grid=(4, s // MQ),
        out_shape=(jax.ShapeDtypeStruct((s, 512), F32), jax.ShapeDtypeStruct((s, 512), F32)),
        in_specs=[qspec, kspec, vtspec], out_specs=(ospec, ospec),
        scratch_shapes=[pltpu.VMEM((4 * BK, MQ), BF16), pltpu.VMEM((LANES + 8, MQ), F32), pltpu.VMEM((16, MQ), F32)],
        compiler_params=pltpu.CompilerParams(vmem_limit_bytes=VMEM_ATTN),
    )(qc, kc, vt)


def _mla_bwd(qc, kc, kct, v, do, lse, delta):
    s = qc.shape[0]

    def body(q_ref, k_ref, kt_ref, v_ref, do_ref, l_ref, d_ref, dq_ref, dk_ref, dv_ref, dqt_scr, p_scr, dz_scr):
        i = pl.program_id(1)

        @pl.when(i == 0)
        def _():
            dk_ref[...] = jnp.zeros_like(dk_ref)
            dv_ref[...] = jnp.zeros_like(dv_ref)

        lane = lax.broadcasted_iota(jnp.int32, (1, LANES), 1)
        keyc = lax.broadcasted_iota(jnp.int32, (BK, MQ), 0)
        qryc = (lax.broadcasted_iota(jnp.int32, (BK, MQ), 1) + i * MQ) // 64
        qw = q_ref[...]
        dob = do_ref[...]
        dost = (dob.astype(F32) * MLA_SCALE).T.astype(BF16)
        lt = l_ref[...].T
        dt = (d_ref[...] * MLA_SCALE).T
        lse0, lse1 = lt[0:1], lt[64:65]
        dl0, dl1 = dt[0:1], dt[64:65]
        dqt_scr[...] = jnp.zeros_like(dqt_scr)

        def products(j, lo=0):
            ks = pl.multiple_of(j * BK, BK)
            return (_mm_nt(_mla_keys(k_ref[pl.ds(ks, BK), :]), qw[lo:]),
                    _mm(_pair_stack(v_ref[pl.ds(ks, BK), :], lane), dost[:, lo:]))

        def grads(j, slot, zt, dwt, masked, lo=0):
            zt = zt * (MLA_SCALE * LOG2E)
            p0 = jnp.exp2(zt[0:BK] - lse0[:, lo:])
            p1 = jnp.exp2(zt[BK:2 * BK] - lse1[:, lo:])
            if masked:
                valid = ((keyc[:, lo:] + j * BK) // 64) <= qryc[:, lo:]
                p0, p1 = jnp.where(valid, p0, 0.0), jnp.where(valid, p1, 0.0)
            p_scr[slot, :, lo:] = jnp.concatenate([p0, p1], axis=0).astype(BF16)
            dz_scr[slot, :, lo:] = jnp.concatenate([p0 * (dwt[0:BK] - dl0[:, lo:]), p1 * (dwt[BK:2 * BK] - dl1[:, lo:])],
                                                   axis=0).astype(BF16)

        def keys_t(ks):
            ktb = kt_ref[:, pl.ds(ks, BK)]
            zero = jnp.zeros((LANES, BK), ktb.dtype)
            return jnp.concatenate([jnp.concatenate([ktb[0:LANES], zero], axis=1),
                                    jnp.concatenate([zero, ktb[LANES:2 * LANES]], axis=1)], axis=0)

        def scatter(ja, lo=0):
            ksa, ksb = pl.multiple_of(ja * BK, BK), pl.multiple_of((ja + 1) * BK, BK)
            dqt_scr[:, lo:] += _mm(jnp.concatenate([keys_t(ksa), keys_t(ksb)], axis=1),
                                   jnp.concatenate([dz_scr[0, :, lo:], dz_scr[1, :, lo:]], axis=0))
            for slot, ks in ((0, ksa), (1, ksb)):
                rk = _mm(dz_scr[slot, :, lo:], qw[lo:])
                dk_ref[pl.ds(ks, BK), :] += jnp.concatenate([rk[0:BK, 0:LANES], rk[BK:2 * BK, LANES:2 * LANES]], axis=1)
                rv = _mm(p_scr[slot, :, lo:], dob[lo:])
                dv_ref[pl.ds(ks, BK), :] += jnp.where(lane < 64, rv[0:BK], rv[BK:2 * BK])

        def step(n, masked, lo=0, prev_lo=0):
            za, wa = products(2 * n, lo)
            zb, wb = products(2 * n + 1, lo)
            scatter(2 * n - 2, prev_lo)
            grads(2 * n, 0, za, wa, masked, lo)
            grads(2 * n + 1, 1, zb, wb, masked, lo)

        def first(masked):
            za, wa = products(0)
            zb, wb = products(1)
            grads(0, 0, za, wa, masked)
            grads(1, 1, zb, wb, masked)

        npq = MQ // (2 * BK)
        seen = lambda d: 2 * BK * max(d, 0)

        @pl.when(i == 0)
        def _():
            first(True)
            for d in range(1, npq):
                step(d, True, seen(d), seen(d - 1))

        @pl.when(i > 0)
        def _():
            first(False)
            lax.fori_loop(1, npq * i, lambda n, c: (step(n, False), c)[1], 0)
            for d in range(npq):
                step(npq * i + d, True, seen(d), seen(d - 1))

        scatter(2 * (npq * (i + 1) - 1), seen(npq - 1))
        dq_ref[...] = dqt_scr[...].T

    qspec = pl.BlockSpec((MQ, 2 * LANES), lambda p, i: (i, p))
    kspec = pl.BlockSpec((s, 2 * LANES), lambda p, i: (0, p))
    ktspec = pl.BlockSpec((2 * LANES, s), lambda p, i: (p, 0))
    vspec = pl.BlockSpec((s, LANES), lambda p, i: (0, p))
    ospec = pl.BlockSpec((MQ, LANES), lambda p, i: (i, p))
    return pl.pallas_call(
        body, name="mla_bwd", grid=(4, s // MQ),
        out_shape=(jax.ShapeDtypeStruct((s, 1024), F32), jax.ShapeDtypeStruct((s, 1024), F32),
                   jax.ShapeDtypeStruct((s, 512), F32)),
        in_specs=[qspec, kspec, ktspec, vspec, ospec, ospec, ospec], out_specs=(qspec, kspec, vspec),
        scratch_shapes=[pltpu.VMEM((2 * LANES, MQ), F32), pltpu.VMEM((2, 2 * BK, MQ), BF16), pltpu.VMEM((2, 2 * BK, MQ), BF16)],
        compiler_params=pltpu.CompilerParams(vmem_limit_bytes=VMEM_ATTN),
    )(qc, kc, kct, v, do, lse, delta)


def _post(x, p, tgt, sbo, mlao, sbg, mlag, gsb, gmla, wout, gpost, wple, gple, wpg, bpg):
    s = x.shape[0]

    def body(x_ref, p_ref, t_ref, sbo_ref, mlao_ref, sbg_ref, mlag_ref, gsb_ref, gmla_ref, wout_ref,
             gpost_ref, wple_ref, gple_ref, wpg_ref, bpg_ref, bd_ref,
             dsbo_ref, dmlao_ref, delta_ref, dsbg_ref, dmlag_ref, dxres_ref, dwout_ref, dwpg_ref, dwple_ref, vec_ref):
        i = pl.program_id(0)

        @pl.when(i == 0)
        def _():
            dwout_ref[...] = jnp.zeros_like(dwout_ref)
            dwpg_ref[...] = jnp.zeros_like(dwpg_ref)
            dwple_ref[...] = jnp.zeros_like(dwple_ref)
            vec_ref[...] = jnp.zeros_like(vec_ref)

        inv_hd = 1.0 / HEAD_DIM

        def head_fwd(o, g, gate):
            r = lax.rsqrt(_seg(o * o, bd_ref[...]) * inv_hd + EPS)
            hat = o * r
            n = hat * g
            sg = _sigmoid(gate)
            return hat, r, n, sg, n * (gate * sg)

        sbo, mlao, sbg_v, mlag_v = sbo_ref[...], mlao_ref[...], sbg_ref[...], mlag_ref[...]
        gsb_v, gmla_v = gsb_ref[...], gmla_ref[...]
        sb_hat, sb_r, sb_n, sb_sg, sb_y = head_fwd(sbo, gsb_v, sbg_v)
        ml_hat, ml_r, ml_n, ml_sg, ml_y = head_fwd(mlao, gmla_v, mlag_v)
        mix = jnp.concatenate([sb_y, ml_y], axis=1).astype(BF16)
        y = _mm(mix, wout_ref[...])
        ry = lax.rsqrt(_rowmean(y * y) + EPS)
        y_hat = y * ry
        gpost_v = gpost_ref[...]
        x1 = x_ref[...] + y_hat * gpost_v
        pb = p_ref[...].astype(BF16)
        pl_ = _mm(pb, wple_ref[...])
        rp = lax.rsqrt(_rowmean(pl_ * pl_) + EPS)
        pl_hat = pl_ * rp
        gple_v = gple_ref[...]
        ple = pl_hat * gple_v
        x1b = x1.astype(BF16)
        gate = _sigmoid(_mm(x1b, wpg_ref[...]) + bpg_ref[...])
        err = x1 + ple * gate - t_ref[...]
        loss = 0.5 * jnp.sum(_rowmean(err * err))
        dout = err * (1.0 / D_MODEL)

        du = dout * ple * gate * (1.0 - gate)
        dub = du.astype(BF16)
        dple = dout * gate
        dx1 = dout + _mm_nt(dub, wpg_ref[...])
        dwpg_ref[...] += _mm_tn(x1b, dub)
        dplh = dple * gple_v
        dpl = rp * (dplh - pl_hat * _rowmean(dplh * pl_hat))
        dwple_ref[...] += _mm_tn(pb, dpl.astype(BF16))
        dxres_ref[...] = dx1
        dyh = dx1 * gpost_v
        dy = ry * (dyh - y_hat * _rowmean(dyh * y_hat))
        dyb = dy.astype(BF16)
        dwout_ref[...] += _mm_tn(mix, dyb)
        dmix = _mm_nt(dyb, wout_ref[...])

        def head_bwd(dyv, hat, r, n, sg, g, gate):
            dn = dyv * (gate * sg)
            dgate = dyv * n * (sg * (1.0 + gate * (1.0 - sg)))
            dhat = dn * g
            do = r * (dhat - hat * (_seg(dhat * hat, bd_ref[...]) * inv_hd))
            return do, dgate, _colsum(dn * hat)

        dsbo, dsbg, dg_sb = head_bwd(dmix[:, 0:512], sb_hat, sb_r, sb_n, sb_sg, gsb_v, sbg_v)
        dmlao, dmlag, dg_ml = head_bwd(dmix[:, 512:1024], ml_hat, ml_r, ml_n, ml_sg, gmla_v, mlag_v)
        dsbo_ref[...] = dsbo.astype(BF16)
        dmlao_ref[...] = dmlao.astype(BF16)
        delta_ref[...] = _seg(dmlao * mlao, bd_ref[...])
        dsbg_ref[...] = dsbg.astype(BF16)
        dmlag_ref[...] = dmlag.astype(BF16)
        vec_ref[pl.ds(0, 1), :] += _colsum(dx1 * y_hat)
        vec_ref[pl.ds(1, 1), :] += _colsum(dple * pl_hat)
        vec_ref[pl.ds(2, 1), :] += _colsum(du)
        vec_ref[pl.ds(3, 1), :] += jnp.concatenate([dg_sb, dg_ml], axis=1)
        vec_ref[pl.ds(4, 1), :] += jnp.full((1, D_MODEL), loss, F32)

    out_shape = (
        jax.ShapeDtypeStruct((s, 512), BF16), jax.ShapeDtypeStruct((s, 512), BF16), jax.ShapeDtypeStruct((s, 512), F32),
        jax.ShapeDtypeStruct((s, 512), BF16), jax.ShapeDtypeStruct((s, 512), BF16), jax.ShapeDtypeStruct((s, D_MODEL), F32),
        jax.ShapeDtypeStruct((D_MODEL, D_MODEL), F32), jax.ShapeDtypeStruct((D_MODEL, D_MODEL), F32),
        jax.ShapeDtypeStruct((PLE_DIM, D_MODEL), F32), jax.ShapeDtypeStruct((8, D_MODEL), F32),
    )
    return pl.pallas_call(
        body, name="post_fwd_bwd", grid=(s // TM,), out_shape=out_shape,
        in_specs=[_rows(D_MODEL), _rows(PLE_DIM), _rows(D_MODEL), _rows(512), _rows(512), _rows(512), _rows(512),
                  _full((1, 512)), _full((1, 512)), _full((D_MODEL, D_MODEL)),
                  _full((1, D_MODEL)), _full((PLE_DIM, D_MODEL)), _full((1, D_MODEL)), _full((D_MODEL, D_MODEL)),
                  _full((1, D_MODEL)), _full((1024, 512))],
        out_specs=(_rows(512), _rows(512), _rows(512), _rows(512), _rows(512), _rows(D_MODEL),
                   _acc((D_MODEL, D_MODEL)), _acc((D_MODEL, D_MODEL)), _acc((PLE_DIM, D_MODEL)), _acc((8, D_MODEL))),
        compiler_params=pltpu.CompilerParams(vmem_limit_bytes=VMEM_DENSE),
    )(x, p, tgt, sbo, mlao, sbg, mlag, gsb, gmla, wout, gpost, wple, gple, wpg, bpg, _blockdiag2(512, HEAD_DIM))


def _pre_bwd(x, dxres, dsbq, dsbk, dsbv, dsbg, dmlag, dqc, dkc, dmv, cq, ckv, tabs, gpre, win, gq, wuq, gkv, wk, wv):
    s = x.shape[0]
    c_t, sa_t, sb_t = tabs
    rw = _rows

    def body(x_ref, dxres_ref, dsbq_ref, dsbk_ref, dsbv_ref, dsbg_ref, dmlag_ref, dqc_ref, dkc_ref, dmv_ref, cq_ref,
             ckv_ref, c_ref, sa_ref, sb_ref, gpre_ref, win_ref, gq_ref, wuq_ref, gkv_ref, wk_ref, wv_ref,
             gx_ref, dwin_ref, dwuq_ref, dwk_ref, dwv_ref, vec_ref, dwin_acc):
        i = pl.program_id(0)

        @pl.when(i == 0)
        def _():
            dwin_acc[...] = jnp.zeros_like(dwin_acc)
            dwuq_ref[...] = jnp.zeros_like(dwuq_ref)
            dwk_ref[...] = jnp.zeros_like(dwk_ref)
            dwv_ref[...] = jnp.zeros_like(dwv_ref)
            vec_ref[...] = jnp.zeros_like(vec_ref)

        lane = lax.broadcasted_iota(jnp.int32, (1, LANES), 1)
        c1, sa1, sb1 = c_ref[...], sa_ref[...], sb_ref[...]
        c8, sa8, sb8 = jnp.tile(c1, (1, 8)), jnp.tile(sa1, (1, 8)), jnp.tile(sb1, (1, 8))

        def norm_bwd(dn, hat, r, g):
            t = dn * g
            return r * (t - hat * _rowmean(t * hat)), _colsum(dn * hat)

        xv = x_ref[...]
        r1 = lax.rsqrt(_rowmean(xv * xv) + EPS)
        x_hat = xv * r1
        gpre_v = gpre_ref[...]
        hb = (x_hat * gpre_v).astype(BF16)
        ready = jnp.concatenate([dsbq_ref[...], dsbk_ref[...].astype(BF16), dsbv_ref[...].astype(BF16), dsbg_ref[...]], axis=1)
        dmlag = dmlag_ref[...]
        dwin_acc[:, 0:2048] += _mm_tn(hb, ready)
        dwin_acc[:, 2560:3072] += _mm_tn(hb, dmlag)
        dh = _mm_nt(ready, win_ref[:, 0:2048]) + _mm_nt(dmlag, win_ref[:, 2560:3072])

        dqeb = _rope_bwd(dqc_ref[...], c8, sa8, sb8).astype(BF16)
        cq = cq_ref[...]
        rq = lax.rsqrt(_rowmean(cq * cq) + EPS)
        cq_hat = cq * rq
        gq_v = gq_ref[...]
        dwuq_ref[...] += _mm_tn((cq_hat * gq_v).astype(BF16), dqeb)
        dcq, dg_q = norm_bwd(_mm_nt(dqeb, wuq_ref[...]), cq_hat, rq, gq_v)

        dkc = dkc_ref[...]
        dkcb = dkc.astype(BF16)
        dmvb = dmv_ref[...].astype(BF16)
        ckv = ckv_ref[...]
        rkv = lax.rsqrt(_rowmean(ckv * ckv) + EPS)
        ckv_hat = ckv * rkv
        gkv_v = gkv_ref[...]
        ckvnb = (ckv_hat * gkv_v).astype(BF16)
        dwk_ref[...] += _mm_tn(ckvnb, dkcb)
        dwv_ref[...] += _mm_tn(ckvnb, dmvb)
        dckv, dg_kv = norm_bwd(_mm_nt(dkcb, wk_ref[...]) + _mm_nt(dmvb, wv_ref[...]), ckv_hat, rkv, gkv_v)

        dkr = dkc[:, 0:LANES]
        for hh in range(1, 8):
            dkr = dkr + dkc[:, LANES * hh:LANES * (hh + 1)]
        dkr = _rope_bwd(dkr, c1, sa1, sb1)
        dkr = jnp.where((lane >= 64) & (lane < 96), dkr, 0.0)

        late = jnp.concatenate([dcq.astype(BF16), dckv.astype(BF16), dkr.astype(BF16)], axis=1)
        dwin_acc[:, 2048:2560] += _mm_tn(hb, late)
        dx, dg_pre = norm_bwd(dh + _mm_nt(late, win_ref[:, 2048:2560]), x_hat, r1, gpre_v)
        gx_ref[...] = dxres_ref[...] + dx
        vec_ref[pl.ds(0, 1), :] += dg_pre
        vec_ref[pl.ds(1, 1), :] += jnp.concatenate([dg_q, dg_kv, jnp.zeros((1, D_MODEL - Q_LORA - KV_LORA), F32)], axis=1)

        @pl.when(i == pl.num_programs(0) - 1)
        def _():
            pltpu.sync_copy(dwin_acc, dwin_ref)

    out_shape = (
        jax.ShapeDtypeStruct((s, D_MODEL), F32), jax.ShapeDtypeStruct((D_MODEL, D_EXT), F32),
        jax.ShapeDtypeStruct((Q_LORA, 1024), F32), jax.ShapeDtypeStruct((KV_LORA, 1024), F32),
        jax.ShapeDtypeStruct((KV_LORA, 512), F32), jax.ShapeDtypeStruct((8, D_MODEL), F32),
    )
    return pl.pallas_call(
        body, name="pre_bwd", grid=(s // TM,), out_shape=out_shape,
        in_specs=[rw(D_MODEL), rw(D_MODEL), rw(512), rw(512), rw(512), rw(512), rw(512),
                  rw(1024), rw(1024), rw(512), rw(Q_LORA), rw(KV_LORA), rw(LANES), rw(LANES),
                  rw(LANES), _full((1, D_MODEL)), _full((D_MODEL, D_EXT)), _full((1, Q_LORA)), _full((Q_LORA, 1024)),
                  _full((1, KV_LORA)), _full((KV_LORA, 1024)), _full((KV_LORA, 512))],
        out_specs=(rw(D_MODEL), pl.BlockSpec(memory_space=pl.ANY), _acc((Q_LORA, 1024)), _acc((KV_LORA, 1024)),
                   _acc((KV_LORA, 512)), _acc((8, D_MODEL))),
        scratch_shapes=[pltpu.VMEM((D_MODEL, D_EXT), F32)],
        compiler_params=pltpu.CompilerParams(vmem_limit_bytes=VMEM_DENSE),
    )(x, dxres, dsbq, dsbk, dsbv, dsbg, dmlag, dqc, dkc, dmv, cq, ckv, c_t, sa_t, sb_t, gpre, win, gq, wuq, gkv, wk, wv)


def _place():
    return lax.axis_index("x"), lax.axis_index("y"), lax.axis_index("c")


def _gather_steps(shapes, ins, bufs, send_sems, recv_sems):
    n = len(shapes)
    x, y, c = _place()
    me, sib = (x, y, c), (x, y, 1 - c)
    chips = [(1 - x, y), (x, 1 - y), (1 - x, 1 - y)]

    def half(t, chip, hc):
        rows = shapes[t][0] // 2
        return bufs[t].at[2 * chip[0] + chip[1], pl.ds(pl.multiple_of(hc * rows, 16), rows), :]

    def copy(k, t, chip, hc, to):
        return pltpu.make_async_remote_copy(src_ref=half(t, chip, hc), dst_ref=half(t, chip, hc), send_sem=send_sems.at[k],
                                            recv_sem=recv_sems.at[k], device_id=to, device_id_type=MESH)

    def start():
        for t in range(n):
            bufs[t][2 * x + y] = ins[t][...].astype(BF16)
            for j, chip in enumerate(chips):
                copy(6 * t + j, t, (x, y), c, (*chip, c)).start()

    def forward():
        for t in range(n):
            for j, chip in enumerate(chips):
                copy(6 * t + j, t, chip, c, me).wait_recv()
                copy(6 * t + 3 + j, t, chip, c, sib).start()

    def finish():
        for t in range(n):
            for j, chip in enumerate(chips):
                copy(6 * t + 3 + j, t, chip, 1 - c, me).wait_recv()
        for t in range(n):
            for j, chip in enumerate(chips):
                copy(6 * t + j, t, (x, y), c, (*chip, c)).wait_send()
                copy(6 * t + 3 + j, t, chip, c, sib).wait_send()

    return start, forward, finish


def _allgather_weights(shards):
    n = len(shards)

    def body(*refs):
        start, forward, finish = _gather_steps([a.shape for a in shards], refs[:n], refs[n:2 * n], refs[2 * n], refs[2 * n + 1])
        start()
        forward()
        finish()

    return pl.pallas_call(
        body, name="allgather_weights",
        out_shape=tuple(jax.ShapeDtypeStruct((N_SHARD,) + a.shape, BF16) for a in shards),
        in_specs=[pl.BlockSpec(memory_space=pltpu.VMEM)] * n, out_specs=(pl.BlockSpec(memory_space=pltpu.VMEM),) * n,
        scratch_shapes=[pltpu.SemaphoreType.DMA((6 * n,)), pltpu.SemaphoreType.DMA((6 * n,))],
        compiler_params=pltpu.CompilerParams(vmem_limit_bytes=VMEM_ATTN),
    )(*shards)


def _reduce_scratch(gsh):
    n = len(gsh)
    half_shapes = [(N_SHARD, a.shape[1] // 2, a.shape[2]) for a in gsh]
    return ([pltpu.VMEM(s_, F32) for s_ in half_shapes] * 2 + [pltpu.VMEM(s_, BF16) for s_ in half_shapes] * 2
            + [pltpu.SemaphoreType.DMA((n,)), pltpu.SemaphoreType.DMA((5 * n,)), pltpu.SemaphoreType.DMA((5 * n,))])


def _reduce_steps(halves, g_refs, f_refs, scratch):
    n = len(halves)
    accs, sibs, sbufs, rbufs = scratch[0:n], scratch[n:2 * n], scratch[2 * n:3 * n], scratch[3 * n:4 * n]
    local_sems, send_sems, recv_sems = scratch[4 * n:4 * n + 3]
    x, y, c = _place()
    me, sib = (x, y, c), (x, y, 1 - c)
    mine = 2 * x + y
    chips = [(1 - x, y), (x, 1 - y), (1 - x, 1 - y)]

    def remote(k, src, dst, to):
        return pltpu.make_async_remote_copy(src_ref=src, dst_ref=dst, send_sem=send_sems.at[k], recv_sem=recv_sems.at[k],
                                            device_id=to, device_id_type=MESH)

    def half3(ref, t, hc):
        return ref.at[:, pl.ds(pl.multiple_of(hc * halves[t], 8), halves[t]), :]

    def half2(ref, t, hc):
        return ref.at[pl.ds(pl.multiple_of(hc * halves[t], 8), halves[t]), :]

    def mine_load(t):
        return pltpu.make_async_copy(half3(g_refs[t], t, c), accs[t], local_sems.at[t])

    def to_sibling(t, to):
        return remote(t, half3(g_refs[t], t, 1 - c), sibs[t], to)

    def to_chip(t, j, chip, to):
        idx = 2 * chip[0] + chip[1]
        return remote(n + 3 * t + j, sbufs[t].at[idx], rbufs[t].at[mine if to is not me else idx], to)

    def swap(t, hc, to):
        return remote(4 * n + t, half2(f_refs[t], t, hc), half2(f_refs[t], t, hc), to)

    def load():
        for t in range(n):
            mine_load(t).start()
            to_sibling(t, sib).start()

    def partial():
        for t in range(n):
            mine_load(t).wait()
            to_sibling(t, me).wait_recv()
            for k in range(N_SHARD):
                accs[t][k] = accs[t][k] + sibs[t][k]
            for j, chip in enumerate(chips):
                idx = 2 * chip[0] + chip[1]
                sbufs[t][idx] = accs[t][idx].astype(BF16)
                to_chip(t, j, chip, (*chip, c)).start()

    def total():
        for t in range(n):
            acc = accs[t][mine]
            for j, chip in enumerate(chips):
                to_chip(t, j, chip, me).wait_recv()
                acc = acc + rbufs[t][2 * chip[0] + chip[1]].astype(F32)
            half2(f_refs[t], t, c)[...] = acc
            swap(t, c, sib).start()

    def finish():
        for t in range(n):
            swap(t, 1 - c, me).wait_recv()
        for t in range(n):
            to_sibling(t, sib).wait_send()
            for j, chip in enumerate(chips):
                to_chip(t, j, chip, (*chip, c)).wait_send()
            swap(t, c, sib).wait_send()

    return load, partial, total, finish


def _reduce_scatter_grads(gsh, vec):
    n = len(gsh)
    halves = [a.shape[1] // 2 for a in gsh]

    def body(*refs):
        g_refs, vec_ref, f_refs, vsum_ref = refs[:n], refs[n], refs[n + 1:2 * n + 1], refs[2 * n + 1]
        scratch = refs[2 * n + 2:]
        vrecv, vsend_sems, vrecv_sems = scratch[4 * n + 3:]
        load, partial, total, finish = _reduce_steps(halves, g_refs, f_refs, scratch)
        x, y, c = _place()
        my_dev = 4 * x + 2 * y + c

        def flip(k):
            return x ^ ((k >> 2) & 1), y ^ ((k >> 1) & 1), c ^ (k & 1)

        def vcopy(k, slot, to):
            return pltpu.make_async_remote_copy(src_ref=vec_ref, dst_ref=vrecv.at[slot], send_sem=vsend_sems.at[k - 1],
                                                recv_sem=vrecv_sems.at[k - 1], device_id=to, device_id_type=MESH)

        load()
        vrecv[my_dev] = vec_ref[...]
        for k in range(1, 8):
            vcopy(k, my_dev, flip(k)).start()
        partial()
        total()
        finish()
        for k in range(1, 8):
            fx, fy, fc = flip(k)
            vcopy(k, 4 * fx + 2 * fy + fc, (x, y, c)).wait_recv()
        vs = vrecv[0]
        for d in range(1, 8):
            vs = vs + vrecv[d]
        vsum_ref[...] = vs
        for k in range(1, 8):
            vcopy(k, my_dev, flip(k)).wait_send()

    return pl.pallas_call(
        body, name="reduce_scatter_grads",
        out_shape=tuple(jax.ShapeDtypeStruct(a.shape[1:], F32) for a in gsh) + (jax.ShapeDtypeStruct((VEC_ROWS, 1024), F32),),
        in_specs=[pl.BlockSpec(memory_space=pl.ANY)] * n + [pl.BlockSpec(memory_space=pltpu.VMEM)],
        out_specs=(pl.BlockSpec(memory_space=pltpu.VMEM),) * (n + 1),
        scratch_shapes=_reduce_scratch(gsh) + [pltpu.VMEM((8, VEC_ROWS, 1024), F32), pltpu.SemaphoreType.DMA((7,)),
                                               pltpu.SemaphoreType.DMA((7,))],
        compiler_params=pltpu.CompilerParams(vmem_limit_bytes=56 * 1024 * 1024),
    )(*gsh, vec)


def _adamw(w, g, m, v):
    rows, cols = w.shape
    tr = rows if rows <= 256 else 256

    def body(w_ref, g_ref, m_ref, v_ref, d_ref, nm_ref, nv_ref):
        d_ref[...], nm_ref[...], nv_ref[...] = _adam_math(w_ref[...], g_ref[...], m_ref[...], v_ref[...])

    spec = pl.BlockSpec((tr, cols), lambda i: (i, 0))
    shp = jax.ShapeDtypeStruct((rows, cols), F32)
    return pl.pallas_call(body, name="adamw", grid=(rows // tr,), out_shape=(shp, shp, shp),
                          in_specs=[spec] * 4, out_specs=(spec,) * 3)(w, g, m, v)


def _adam_math(w, g, m, v):
    m2 = ADAM_B1 * m + (1.0 - ADAM_B1) * g
    v2 = ADAM_B2 * v + (1.0 - ADAM_B2) * (g * g)
    m_hat = m2 / (1.0 - ADAM_B1 ** ADAM_STEP)
    v_hat = v2 / (1.0 - ADAM_B2 ** ADAM_STEP)
    return -ADAM_LR * (m_hat / (jnp.sqrt(v_hat) + ADAM_EPS) + ADAM_WD * w), m2, v2


def _adamw_small(vsum, w, m, v):
    names = [name for name, _, _, _ in _VEC_LAYOUT]
    k = len(names)

    def body(*refs):
        vs_ref, w_refs, m_refs, v_refs = refs[0], refs[1:1 + k], refs[1 + k:1 + 2 * k], refs[1 + 2 * k:1 + 3 * k]
        outs = refs[1 + 3 * k:]
        for idx, (_, r, c0, width) in enumerate(_VEC_LAYOUT):
            gv = vs_ref[pl.ds(r, 1), pl.ds(c0, width)]
            d, m2, v2 = _adam_math(w_refs[idx][...], gv, m_refs[idx][...], v_refs[idx][...])
            outs[idx][...], outs[k + idx][...], outs[2 * k + idx][...], outs[3 * k + idx][...] = gv, d, m2, v2

    shapes = tuple(jax.ShapeDtypeStruct(w[name].shape, F32) for name in names)
    res = pl.pallas_call(
        body, name="adamw_small", out_shape=shapes * 4,
        in_specs=[pl.BlockSpec(memory_space=pltpu.VMEM)] * (1 + 3 * k), out_specs=(pl.BlockSpec(memory_space=pltpu.VMEM),) * (4 * k),
    )(vsum, *[w[name] for name in names], *[m[name] for name in names], *[v[name] for name in names])
    return tuple({name: res[part * k + idx] for idx, name in enumerate(names)} for part in range(4))


_EARLY = ("w_in", "w_uq", "w_ukv")
_LATE = ("w_out", "w_ple", "w_ple_gate")
_BIG = _EARLY + _LATE
_KR_LOCAL = 2432 - 3 * (D_IN // N_SHARD)


def _extend_early(parts):
    cols = lambda a: a.transpose(1, 0, 2).reshape(a.shape[1], N_SHARD * a.shape[2])
    g = parts["w_in"]
    zeros = lambda n: jnp.zeros((D_MODEL, n), g.dtype)
    win_ext = jnp.concatenate([g[0], g[1], g[2], g[3][:, :_KR_LOCAL], zeros(64), g[3][:, _KR_LOCAL:_KR_LOCAL + QK_ROPE],
                               zeros(32), g[3][:, _KR_LOCAL + QK_ROPE:]], axis=1)
    wuq_ext = jnp.pad(cols(parts["w_uq"]).reshape(Q_LORA, 8, 96), ((0, 0), (0, 0), (0, 32))).reshape(Q_LORA, 1024)
    wukv = cols(parts["w_ukv"]).reshape(KV_LORA, 8, 128)
    wk_ext = jnp.pad(wukv[:, :, :64], ((0, 0), (0, 0), (0, 64))).reshape(KV_LORA, 1024)
    wv = wukv[:, :, 64:].reshape(KV_LORA, 512)
    return win_ext, wuq_ext, wk_ext, wv


def _shard_cols(a):
    return a.reshape(a.shape[0], N_SHARD, a.shape[1] // N_SHARD).transpose(1, 0, 2)


def _shard_rows(a):
    return a.reshape(N_SHARD, a.shape[0] // N_SHARD, a.shape[1])


def _shard_early_grads(dwin_ext, dwuq_ext, dwk_ext, dwv):
    e, w = dwin_ext, D_IN // N_SHARD
    last = jnp.concatenate([e[:, 3 * w:2432], e[:, 2496:2528], e[:, 2560:]], axis=1)
    dwuq = dwuq_ext.reshape(Q_LORA, 8, 128)[:, :, :96].reshape(Q_LORA, 768)
    dwukv = jnp.concatenate([dwk_ext.reshape(KV_LORA, 8, 128)[:, :, :64], dwv.reshape(KV_LORA, 8, 64)], axis=2)
    return [jnp.stack([e[:, 0:w], e[:, w:2 * w], e[:, 2 * w:3 * w], last]), _shard_cols(dwuq),
            _shard_cols(dwukv.reshape(KV_LORA, 1024))]


def _rope_tables(positions):
    half = QK_ROPE // 2
    freq = ROPE_THETA ** (-jnp.arange(half, dtype=F32) / half)
    ang = positions.astype(F32)[:, None] * freq
    cos, sin = jnp.cos(ang), jnp.sin(ang)
    s = positions.shape[0]
    z = lambda n: jnp.zeros((s, n), F32)
    c_t = jnp.concatenate([jnp.ones((s, 64), F32), cos, cos, z(32)], axis=1)
    sa_t = jnp.concatenate([z(64), -sin, z(16), z(32)], axis=1)
    sb_t = jnp.concatenate([z(64), z(16), sin, z(32)], axis=1)
    return c_t, sa_t, sb_t


def _local_grads(x, p, positions, tgt, gains, early, late):
    win_ext, wuq_ext, wk_ext, wv = _extend_early(early)
    tabs = _rope_tables(positions)
    g = gains
    sbq, sbk, sbv, sbg, mlag, cq, ckv, qc, kc, mv, sbkt, sbvt, kct, mvt = _pre_fwd(
        x, tabs, g["norm_pre_g"], win_ext, g["q_norm_g"], wuq_ext, g["kv_norm_g"], wk_ext, wv)
    sbo, wout4, wple4, wpg4 = _sb_fwd(sbq, sbk, sbvt, late)
    wout, wpg = wout4.reshape(D_MODEL, D_MODEL), wpg4.reshape(D_MODEL, D_MODEL)
    wple = wple4.transpose(1, 0, 2).reshape(PLE_DIM, D_MODEL)
    mlao, lse = _mla_fwd(qc, kc, mvt)
    dsbo, dmlao, delta, dsbg, dmlag, dxres, dwout, dwpg, dwple, vec_c = _post(
        x, p, tgt, sbo, mlao, sbg, mlag, g["sb_out_norm_g"], g["mla_out_norm_g"], wout, g["norm_post_g"], wple,
        g["ple_norm_g"], wpg, g["b_ple_gate"])
    dsbq, dsbk, dsbv, *late_grads = _sb_bwd(sbq, sbk, sbkt, sbv, dsbo, [_shard_rows(dwout), _shard_cols(dwple), _shard_rows(dwpg)])
    dqc, dkc, dmv = _mla_bwd(qc, kc, kct, mv, dmlao, lse, delta)
    gx, dwin_ext, dwuq_ext, dwk_ext, dwv, vec_d = _pre_bwd(
        x, dxres, dsbq, dsbk, dsbv, dsbg, dmlag, dqc, dkc, dmv, cq, ckv, tabs, g["norm_pre_g"], win_ext, g["q_norm_g"],
        wuq_ext, g["kv_norm_g"], wk_ext, wv)
    return gx, _shard_early_grads(dwin_ext, dwuq_ext, dwk_ext, dwv), late_grads, jnp.concatenate([vec_c, vec_d], axis=0)


_VEC_LAYOUT = (("norm_post_g", 0, 0, 1024), ("ple_norm_g", 1, 0, 1024), ("b_ple_gate", 2, 0, 1024), ("sb_out_norm_g", 3, 0, 512),
               ("mla_out_norm_g", 3, 512, 512), ("norm_pre_g", 8, 0, 1024), ("q_norm_g", 9, 0, 256), ("kv_norm_g", 9, 256, 128))
_LOSS_ROW = 4
_WEIGHT_ORDER = ("norm_pre_g", "w_in", "q_norm_g", "w_uq", "kv_norm_g", "w_ukv", "sb_out_norm_g", "mla_out_norm_g", "w_out",
                 "norm_post_g", "w_ple", "ple_norm_g", "w_ple_gate", "b_ple_gate")


def kernel(x, p, positions, norm_pre_g, w_in, q_norm_g, w_uq, kv_norm_g, w_ukv, sb_out_norm_g, mla_out_norm_g, w_out, norm_post_g, w_ple, ple_norm_g, w_ple_gate, b_ple_gate, loss_target, m_norm_pre_g, m_w_in, m_q_norm_g, m_w_uq, m_kv_norm_g, m_w_ukv, m_sb_out_norm_g, m_mla_out_norm_g, m_w_out, m_norm_post_g, m_w_ple, m_ple_norm_g, m_w_ple_gate, m_b_ple_gate, v_norm_pre_g, v_w_in, v_q_norm_g, v_w_uq, v_kv_norm_g, v_w_ukv, v_sb_out_norm_g, v_mla_out_norm_g, v_w_out, v_norm_post_g, v_w_ple, v_ple_norm_g, v_w_ple_gate, v_b_ple_gate):
    w = {"norm_pre_g": norm_pre_g, "w_in": w_in[0], "q_norm_g": q_norm_g, "w_uq": w_uq[0], "kv_norm_g": kv_norm_g, "w_ukv": w_ukv[0],
         "sb_out_norm_g": sb_out_norm_g, "mla_out_norm_g": mla_out_norm_g, "w_out": w_out[0], "norm_post_g": norm_post_g,
         "w_ple": w_ple[0], "ple_norm_g": ple_norm_g, "w_ple_gate": w_ple_gate[0], "b_ple_gate": b_ple_gate}
    m = {"norm_pre_g": m_norm_pre_g, "w_in": m_w_in[0], "q_norm_g": m_q_norm_g, "w_uq": m_w_uq[0], "kv_norm_g": m_kv_norm_g,
         "w_ukv": m_w_ukv[0], "sb_out_norm_g": m_sb_out_norm_g, "mla_out_norm_g": m_mla_out_norm_g, "w_out": m_w_out[0],
         "norm_post_g": m_norm_post_g, "w_ple": m_w_ple[0], "ple_norm_g": m_ple_norm_g, "w_ple_gate": m_w_ple_gate[0],
         "b_ple_gate": m_b_ple_gate}
    v = {"norm_pre_g": v_norm_pre_g, "w_in": v_w_in[0], "q_norm_g": v_q_norm_g, "w_uq": v_w_uq[0], "kv_norm_g": v_kv_norm_g,
         "w_ukv": v_w_ukv[0], "sb_out_norm_g": v_sb_out_norm_g, "mla_out_norm_g": v_mla_out_norm_g, "w_out": v_w_out[0],
         "norm_post_g": v_norm_post_g, "w_ple": v_w_ple[0], "ple_norm_g": v_ple_norm_g, "w_ple_gate": v_w_ple_gate[0],
         "b_ple_gate": v_b_ple_gate}
    gathered = _allgather_weights([w[n] for n in _EARLY])
    gx, early_grads, late_red, vec = _local_grads(x[0], p[0, 0], positions[0], loss_target[0], w, dict(zip(_EARLY, gathered)),
                                                  [w[n] for n in _LATE])
    *early_red, vsum = _reduce_scatter_grads(early_grads, vec)
    gred = early_red + late_red
    loss = vsum[_LOSS_ROW, 0]

    g, delta, new_m, new_v = _adamw_small(vsum, w, m, v)
    for n, gn in zip(_BIG, gred):
        g[n] = gn
        delta[n], new_m[n], new_v[n] = _adamw(w[n], gn, m[n], v[n])

    lead = lambda n, a: a[None] if n in _BIG else a
    return (loss, gx[None],
            *[lead(n, g[n]) for n in _WEIGHT_ORDER], *[lead(n, delta[n]) for n in _WEIGHT_ORDER],
            *[lead(n, new_m[n]) for n in _WEIGHT_ORDER], *[lead(n, new_v[n]) for n in _WEIGHT_ORDER])
```

```python
import numpy as np
import jax
import jax.numpy as jnp
from jax import lax
from jax.experimental import pallas as pl
from jax.experimental.pallas import tpu as pltpu

F32 = jnp.float32
BF16 = jnp.bfloat16
MESH = pl.DeviceIdType.MESH

D_MODEL = 1024
HEAD_DIM = 64
D_SB = 512
D_MLA = 512
Q_LORA = 256
KV_LORA = 128
QK_NOPE = 64
QK_ROPE = 32
PLE_DIM = 256
D_IN = 2976
D_EXT = 3072
ROPE_THETA = 10000.0
EPS = 1e-6
N_SHARD = 4

ADAM_LR = 0.001
ADAM_B1 = 0.9
ADAM_B2 = 0.999
ADAM_EPS = 1e-08
ADAM_WD = 0.01
ADAM_STEP = 10

LANES = 128
BK = 128
WQ = 256
MQ = 512
SB_CUTOFF = 120.0
TM = 256
TM_PRE = 256
VEC_ROWS = 16
VMEM_DENSE = 52 * 1024 * 1024
VMEM_ATTN = 40 * 1024 * 1024


def _mm(a, b):
    return jnp.dot(a, b, preferred_element_type=F32)


def _mm_nt(a, b):
    return lax.dot_general(a, b, (((1,), (1,)), ((), ())), preferred_element_type=F32)


def _mm_tn(a, b):
    return lax.dot_general(a, b, (((0,), (0,)), ((), ())), preferred_element_type=F32)


def _seg(a, bd2):
    return _mm(_split2(a), bd2)


def _const(mask):
    return jnp.asarray(np.asarray(mask, np.float32), dtype=BF16)


def _blockdiag2(n, seg):
    r = (np.arange(2 * n)[:, None] % n) // seg
    c = np.arange(n)[None, :] // seg
    return _const(r == c)


def _sigmoid(a):
    return 1.0 / (1.0 + jnp.exp(-a))


def _rowmean(a):
    return jnp.mean(a, axis=-1, keepdims=True)


def _colsum(a):
    return jnp.sum(a, axis=0, keepdims=True)


def _rope_fwd(a, c, sa, sb):
    w = a.shape[-1]
    return a * c + pltpu.roll(a, w - 16, 1) * sa + pltpu.roll(a, 16, 1) * sb


def _rope_bwd(g, c, sa, sb):
    w = g.shape[-1]
    return g * c + pltpu.roll(g * sa, 16, 1) + pltpu.roll(g * sb, w - 16, 1)


def _full(shape):
    return pl.BlockSpec(shape, lambda *_: (0,) * len(shape))


def _acc(shape):
    return pl.BlockSpec(shape, lambda *_: (0,) * len(shape))


def _full2(shape):
    return pl.BlockSpec(shape, lambda p, i: (0, 0))


def _cols(height, tm=TM):
    return pl.BlockSpec((height, tm), lambda i: (0, i))


def _rows(width, tm=TM):
    return pl.BlockSpec((tm, width), lambda i: (i, 0))


def _pre_fwd(x, tabs, gpre, win, gq, wuq, gkv, wk, wv):
    s = x.shape[0]
    c_t, sa_t, sb_t = tabs
    rw, cl = (lambda width: _rows(width, TM_PRE)), (lambda height: _cols(height, TM_PRE))

    def body(x_ref, c_ref, sa_ref, sb_ref, gpre_ref, win_ref, gq_ref, wuq_ref, gkv_ref, wk_ref, wv_ref,
             sbq_ref, sbk_ref, sbv_ref, sbg_ref, mlag_ref, cq_ref, ckv_ref, qc_ref, kc_ref, mv_ref,
             sbkt_ref, sbvt_ref, kct_ref, mvt_ref):
        xv = x_ref[...]
        r1 = lax.rsqrt(_rowmean(xv * xv) + EPS)
        h = (xv * r1 * gpre_ref[...]).astype(BF16)
        proj = _mm(h, win_ref[...])
        sbq_ref[...] = proj[:, 0:512].astype(BF16)
        sbk_ref[...] = proj[:, 512:1024].astype(BF16)
        sbv_ref[...] = proj[:, 1024:1536].astype(BF16)
        sbkt_ref[...] = proj[:, 512:1024].T.astype(BF16)
        sbvt_ref[...] = proj[:, 1024:1536].T.astype(BF16)
        sbg_ref[...] = proj[:, 1536:2048]
        cq = proj[:, 2048:2304]
        ckv = proj[:, 2304:2432]
        kr = proj[:, 2432:2560]
        mlag_ref[...] = proj[:, 2560:3072]
        cq_ref[...] = cq
        ckv_ref[...] = ckv
        c1, sa1, sb1 = c_ref[...], sa_ref[...], sb_ref[...]
        c8, sa8, sb8 = jnp.tile(c1, (1, 8)), jnp.tile(sa1, (1, 8)), jnp.tile(sb1, (1, 8))
        cqn = (cq * lax.rsqrt(_rowmean(cq * cq) + EPS) * gq_ref[...]).astype(BF16)
        qe = _mm(cqn, wuq_ref[...])
        qc_ref[...] = _rope_fwd(qe, c8, sa8, sb8).astype(BF16)
        ckvn = (ckv * lax.rsqrt(_rowmean(ckv * ckv) + EPS) * gkv_ref[...]).astype(BF16)
        ke = _mm(ckvn, wk_ref[...])
        krr = _rope_fwd(kr, c1, sa1, sb1)
        kcat = ke + jnp.tile(krr, (1, 8))
        kc_ref[...] = kcat.astype(BF16)
        kct_ref[...] = kcat.T.astype(BF16)
        mval = _mm(ckvn, wv_ref[...])
        mv_ref[...] = mval.astype(BF16)
        mvt_ref[...] = mval.T.astype(BF16)

    out_shape = (
        jax.ShapeDtypeStruct((s, 512), BF16), jax.ShapeDtypeStruct((s, 512), BF16), jax.ShapeDtypeStruct((s, 512), BF16),
        jax.ShapeDtypeStruct((s, 512), F32), jax.ShapeDtypeStruct((s, 512), F32),
        jax.ShapeDtypeStruct((s, Q_LORA), F32), jax.ShapeDtypeStruct((s, KV_LORA), F32),
        jax.ShapeDtypeStruct((s, 1024), BF16), jax.ShapeDtypeStruct((s, 1024), BF16), jax.ShapeDtypeStruct((s, 512), BF16),
        jax.ShapeDtypeStruct((512, s), BF16), jax.ShapeDtypeStruct((512, s), BF16), jax.ShapeDtypeStruct((1024, s), BF16),
        jax.ShapeDtypeStruct((512, s), BF16),
    )
    return pl.pallas_call(
        body, name="pre_fwd", grid=(s // TM_PRE,), out_shape=out_shape,
        in_specs=[rw(D_MODEL), rw(LANES), rw(LANES), rw(LANES), _full((1, D_MODEL)), _full((D_MODEL, D_EXT)),
                  _full((1, Q_LORA)), _full((Q_LORA, 1024)), _full((1, KV_LORA)), _full((KV_LORA, 1024)), _full((KV_LORA, 512))],
        out_specs=(rw(512), rw(512), rw(512), rw(512), rw(512), rw(Q_LORA), rw(KV_LORA),
                   rw(1024), rw(1024), rw(512), cl(512), cl(512), cl(1024), cl(512)),
        compiler_params=pltpu.CompilerParams(vmem_limit_bytes=VMEM_DENSE),
    )(x, c_t, sa_t, sb_t, gpre, win, gq, wuq, gkv, wk, wv)


def _softplus(z):
    neg_abs = lax.bitcast_convert_type(lax.bitcast_convert_type(z, jnp.uint32) | jnp.uint32(0x80000000), F32)
    return jnp.maximum(z, 0.0) + jnp.log(1.0 + jnp.exp(neg_abs))


def _sum_matrix(kind, terms):
    r, c = np.arange(2 * BK)[:, None], np.arange(2 * BK * terms)[None, :] % (2 * BK)
    rk, ck = r % BK, c % BK
    return _const(((r // BK) == (c // BK)) & {"suffix": ck >= rk, "prefix": ck <= rk}[kind])


def _split_rows(a):
    hi = a.astype(BF16)
    return jnp.concatenate([hi, (a - hi.astype(F32)).astype(BF16)], axis=0)


def _heads_t(blk, rowi):
    zero = jnp.zeros_like(blk)
    return jnp.concatenate([jnp.where(rowi < 64, blk, zero), jnp.where(rowi >= 64, blk, zero)], axis=1)


def _mask_keys(a, valid, fill=0.0):
    return jnp.concatenate([jnp.where(valid, a[0:BK], fill), jnp.where(valid, a[BK:2 * BK], fill)], axis=0)


def _split2(a):
    hi = a.astype(BF16)
    lo = (a - hi.astype(F32)).astype(BF16)
    return jnp.concatenate([hi, lo], axis=1)


def _pair_stack(b, lane):
    zero = jnp.zeros_like(b)
    return jnp.concatenate([jnp.where(lane < 64, b, zero), jnp.where(lane >= 64, b, zero)], axis=0)


def _sb_fwd(q, k, vt, late):
    s = q.shape[0]
    n = len(late)

    def body(q_ref, k_ref, vt_ref, usuf_ref, *rest):
        ins, o_ref, outs = rest[:n], rest[n], rest[n + 1:2 * n + 1]
        acc_scr, run_scr = rest[2 * n + 1:2 * n + 3]
        bufs, (send_sems, recv_sems, out_sems) = rest[2 * n + 3:3 * n + 3], rest[3 * n + 3:]
        p, i = pl.program_id(0), pl.program_id(1)
        gather_start, gather_forward, gather_finish = _gather_steps([a.shape for a in late], ins, bufs, send_sems, recv_sems)

        @pl.when((p == 0) & (i == 0))
        def _():
            gather_start()

        @pl.when((p == 2) & (i == 0))
        def _():
            gather_forward()

        lane = lax.broadcasted_iota(jnp.int32, (1, LANES), 1)
        rowi = lax.broadcasted_iota(jnp.int32, (LANES, 1), 0)
        keyi = lax.broadcasted_iota(jnp.int32, (BK, WQ), 0)
        qryi = lax.broadcasted_iota(jnp.int32, (BK, WQ), 1) + i * WQ
        qs = q_ref[...] * (HEAD_DIM ** -0.5)

        def group(blocks, masked, seen=None):
            seen = seen or [0] * len(blocks)
            starts = [pl.multiple_of(j * BK, BK) for j in blocks]
            valid = [(keyi[:, lo:] + j * BK) < qryi[:, lo:] if m else None for j, m, lo in zip(blocks, masked, seen)]
            zs = [_mm_nt(_pair_stack(k_ref[pl.ds(ks, BK), :], lane), qs[lo:]) for ks, lo in zip(starts, seen)]
            sps = [_softplus(z) for z in zs]
            sps = [sp if ok is None else _mask_keys(sp, ok) for sp, ok in zip(sps, valid)]
            cums = [_mm(usuf_ref[...], _split_rows(sp)) for sp in sps]
            ws = [jnp.exp(z - c) for z, c in zip(zs, cums)]
            ws = [w if ok is None else _mask_keys(w, ok) for w, ok in zip(ws, valid)]
            pvs = [_mm(_heads_t(vt_ref[:, pl.ds(ks, BK)], rowi), w.astype(BF16)) for ks, w in zip(starts, ws)]
            for pv, c, lo in zip(pvs, cums, seen):
                r0, r1 = run_scr[0:1, lo:], run_scr[1:2, lo:]
                acc_scr[:, lo:] += jnp.where(rowi < 64, jnp.exp(-r0), jnp.exp(-r1)) * pv
                run_scr[0:1, lo:] = r0 + c[0:1]
                run_scr[1:2, lo:] = r1 + c[BK:BK + 1]

        assert WQ == 2 * BK
        acc_scr[...] = jnp.zeros_like(acc_scr)
        run_scr[...] = jnp.zeros_like(run_scr)

        @pl.when(i == 0)
        def _():
            group([1, 0], [True, True], [BK, 0])

        @pl.when(i > 0)
        def _():
            group([2 * i + 1, 2 * i, 2 * i - 1, 2 * i - 2], [True, True, False, False], [BK, 0, 0, 0])

        def unfinished():
            return (jnp.min(run_scr[0:2, :]) < SB_CUTOFF).astype(jnp.int32)

        def step(c):
            group([2 * i - 1 - 2 * c[0], 2 * i - 2 - 2 * c[0]], [False, False])
            return c[0] + 1, unfinished()

        lax.while_loop(lambda c: (c[0] < i) & (c[1] > 0), step, (jnp.int32(1), unfinished()))
        o_ref[...] = acc_scr[...].T

        @pl.when((p == pl.num_programs(0) - 1) & (i == pl.num_programs(1) - 1))
        def _():
            gather_finish()
            copies = [pltpu.make_async_copy(bufs[t], outs[t], out_sems.at[t]) for t in range(n)]
            for cp in copies:
                cp.start()
            for cp in copies:
                cp.wait()

    qspec = pl.BlockSpec((WQ, LANES), lambda p, i: (i, p))
    kspec = pl.BlockSpec((s, LANES), lambda p, i: (0, p))
    tspec = pl.BlockSpec((LANES, s), lambda p, i: (p, 0))
    gathered = [jax.ShapeDtypeStruct((N_SHARD,) + a.shape, BF16) for a in late]
    return pl.pallas_call(
        body, name="sb_fwd", grid=(4, s // WQ),
        out_shape=(jax.ShapeDtypeStruct((s, 512), F32), *gathered),
        in_specs=[qspec, kspec, tspec, _full2((2 * BK, 4 * BK))] + [_full2(a.shape) for a in late],
        out_specs=(qspec,) + (pl.BlockSpec(memory_space=pl.ANY),) * n,
        scratch_shapes=[pltpu.VMEM((LANES, WQ), F32), pltpu.VMEM((8, WQ), F32)] + [pltpu.VMEM(g.shape, BF16) for g in gathered]
                       + [pltpu.SemaphoreType.DMA((6 * n,)), pltpu.SemaphoreType.DMA((6 * n,)), pltpu.SemaphoreType.DMA((n,))],
        compiler_params=pltpu.CompilerParams(vmem_limit_bytes=VMEM_ATTN),
    )(q, k, vt, _sum_matrix("suffix", 2), *late)


def _sb_bwd(q, k, kt, v, do, late):
    s = q.shape[0]
    n = len(late)
    halves = [a.shape[1] // 2 for a in late]

    def body(q_ref, k_ref, kt_ref, v_ref, do_ref, usuf_ref, upre_ref, *rest):
        g_refs, (dq_ref, dk_ref, dv_ref), outs = rest[:n], rest[n:n + 3], rest[n + 3:2 * n + 3]
        later_scr, dqt_scr, st_scr = rest[2 * n + 3:2 * n + 6]
        f_scr, reduce_scr, out_sems = rest[2 * n + 6:3 * n + 6], rest[3 * n + 6:-1], rest[-1]
        p, i = pl.program_id(0), pl.program_id(1)
        reduce_load, reduce_partial, reduce_total, reduce_finish = _reduce_steps(halves, g_refs, f_scr, reduce_scr)

        @pl.when((p == 0) & (i == 0))
        def _():
            reduce_load()

        @pl.when((p == 1) & (i == 0))
        def _():
            reduce_partial()

        @pl.when((p == 3) & (i == 0))
        def _():
            reduce_total()

        @pl.when(i == 0)
        def _():
            dk_ref[...] = jnp.zeros_like(dk_ref)
            dv_ref[...] = jnp.zeros_like(dv_ref)

        lane = lax.broadcasted_iota(jnp.int32, (1, LANES), 1)
        rowi = lax.broadcasted_iota(jnp.int32, (LANES, 1), 0)
        keyi = lax.broadcasted_iota(jnp.int32, (BK, WQ), 0)
        qryi = lax.broadcasted_iota(jnp.int32, (BK, WQ), 1) + i * WQ
        qs = q_ref[...] * (HEAD_DIM ** -0.5)
        dob = do_ref[...]
        dot = dob.astype(F32).T.astype(BF16)

        def scores(j, lo=0):
            return _mm_nt(_pair_stack(k_ref[pl.ds(pl.multiple_of(j * BK, BK), BK), :], lane), qs[lo:])

        def scan(blocks, masked, seen=None):
            seen = seen or [0] * len(blocks)
            sps = [_softplus(scores(j, lo)) for j, lo in zip(blocks, seen)]
            sps = [_mask_keys(sp, (keyi[:, lo:] + j * BK) < qryi[:, lo:]) if m else sp
                   for sp, j, m, lo in zip(sps, blocks, masked, seen)]
            for sp, j, lo in zip(sps, blocks, seen):
                run = st_scr[0:2, :]
                later_scr[j, 0:2, :] = run
                st_scr[0:2, lo:] = run[:, lo:] + jnp.concatenate([jnp.sum(sp[0:BK], axis=0, keepdims=True),
                                                                  jnp.sum(sp[BK:2 * BK], axis=0, keepdims=True)], axis=0)

        def sweep(blocks, masked, seen=None):
            seen = seen or [0] * len(blocks)
            starts = [pl.multiple_of(j * BK, BK) for j in blocks]
            valid = [(keyi[:, lo:] + j * BK) < qryi[:, lo:] if m else None for j, m, lo in zip(blocks, masked, seen)]
            zs = [scores(j, lo) for j, lo in zip(blocks, seen)]
            us = [jnp.exp(lax.bitcast_convert_type(lax.bitcast_convert_type(z, jnp.uint32) | jnp.uint32(0x80000000), F32))
                  for z in zs]
            sps = [jnp.maximum(z, 0.0) + jnp.log(1.0 + u) for z, u in zip(zs, us)]
            sps = [sp if ok is None else _mask_keys(sp, ok) for sp, ok in zip(sps, valid)]
            sigs = [jnp.where(z >= 0.0, 1.0, u) / (1.0 + u) for z, u in zip(zs, us)]
            cums = [_mm(usuf_ref[...], _split_rows(sp)) for sp in sps]
            dws = [_mm(_pair_stack(v_ref[pl.ds(ks, BK), :], lane), dot[:, lo:]) for ks, lo in zip(starts, seen)]
            wfs = []
            for z, c, j, ok, lo in zip(zs, cums, blocks, valid, seen):
                f = jnp.exp(-later_scr[j, 0:2, lo:])
                wide = (BK, WQ - lo)
                wf = jnp.exp(z - c) * jnp.concatenate([jnp.broadcast_to(f[0:1], wide), jnp.broadcast_to(f[1:2], wide)], axis=0)
                wfs.append(wf if ok is None else _mask_keys(wf, ok))
            es = [dw * wf for dw, wf in zip(dws, wfs)]
            pres = [_mm(upre_ref[...], e.astype(BF16)) for e in es]
            dzs = []
            for e, pre, sig, ok, lo in zip(es, pres, sigs, valid, seen):
                e0 = pre[0:BK] + st_scr[0:1, lo:]
                e1 = pre[BK:2 * BK] + st_scr[1:2, lo:]
                st_scr[0:1, lo:] = e0[BK - 1:BK]
                st_scr[1:2, lo:] = e1[BK - 1:BK]
                dz = e - sig * jnp.concatenate([e0, e1], axis=0)
                dzs.append((dz if ok is None else _mask_keys(dz, ok)).astype(BF16))
            whole = [b for b, lo in enumerate(seen) if lo == 0]
            dqt_scr[...] += _mm(jnp.concatenate([_heads_t(kt_ref[:, pl.ds(starts[b], BK)], rowi) for b in whole], axis=1),
                                jnp.concatenate([dzs[b] for b in whole], axis=0))
            for b, lo in enumerate(seen):
                if lo:
                    dqt_scr[:, lo:] += _mm(_heads_t(kt_ref[:, pl.ds(starts[b], BK)], rowi), dzs[b])
            for ks, dz, wf, lo in zip(starts, dzs, wfs, seen):
                rk = _mm(dz, qs[lo:])
                dk_ref[pl.ds(ks, BK), :] += jnp.where(lane < 64, rk[0:BK], rk[BK:2 * BK])
                rv = _mm(wf.astype(BF16), dob[lo:])
                dv_ref[pl.ds(ks, BK), :] += jnp.where(lane < 64, rv[0:BK], rv[BK:2 * BK])

        assert WQ == 2 * BK
        st_scr[...] = jnp.zeros_like(st_scr)

        @pl.when(i == 0)
        def _():
            scan([1, 0], [True, True], [BK, 0])

        @pl.when(i > 0)
        def _():
            scan([2 * i + 1, 2 * i, 2 * i - 1, 2 * i - 2], [True, True, False, False], [BK, 0, 0, 0])

        def unfinished():
            return (jnp.min(st_scr[0:2, :]) < SB_CUTOFF).astype(jnp.int32)

        def step(c):
            scan([2 * i - 1 - 2 * c[0], 2 * i - 2 - 2 * c[0]], [False, False])
            return c[0] + 1, unfinished()

        npairs, _ = lax.while_loop(lambda c: (c[0] < i) & (c[1] > 0), step, (jnp.minimum(i, 1), unfinished()))

        st_scr[...] = jnp.zeros_like(st_scr)
        dqt_scr[...] = jnp.zeros_like(dqt_scr)
        first = 2 * (i - npairs)

        def early(t, carry):
            sweep([first + 2 * t, first + 2 * t + 1], [False, False])
            return carry

        lax.fori_loop(0, npairs - 1, early, 0)

        @pl.when(i == 0)
        def _():
            sweep([0, 1], [True, True], [0, BK])

        @pl.when(i > 0)
        def _():
            sweep([2 * i - 2, 2 * i - 1, 2 * i, 2 * i + 1], [False, False, True, True], [0, 0, 0, BK])

        dq_ref[...] = (dqt_scr[...].T * (HEAD_DIM ** -0.5)).astype(BF16)

        @pl.when((p == pl.num_programs(0) - 1) & (i == pl.num_programs(1) - 1))
        def _():
            reduce_finish()
            copies = [pltpu.make_async_copy(f_scr[t], outs[t], out_sems.at[t]) for t in range(n)]
            for cp in copies:
                cp.start()
            for cp in copies:
                cp.wait()

    qspec = pl.BlockSpec((WQ, LANES), lambda p, i: (i, p))
    kspec = pl.BlockSpec((s, LANES), lambda p, i: (0, p))
    tspec = pl.BlockSpec((LANES, s), lambda p, i: (p, 0))
    anywhere = pl.BlockSpec(memory_space=pl.ANY)
    reduced = [jax.ShapeDtypeStruct(a.shape[1:], F32) for a in late]
    return pl.pallas_call(
        body, name="sb_bwd", grid=(4, s // WQ),
        out_shape=(jax.ShapeDtypeStruct((s, 512), BF16), jax.ShapeDtypeStruct((s, 512), F32),
                   jax.ShapeDtypeStruct((s, 512), F32), *reduced),
        in_specs=[qspec, kspec, tspec, kspec, qspec, _full2((2 * BK, 4 * BK)), _full2((2 * BK, 2 * BK))] + [anywhere] * n,
        out_specs=(qspec, kspec, kspec) + (anywhere,) * n,
        scratch_shapes=[pltpu.VMEM((s // BK, 8, WQ), F32), pltpu.VMEM((LANES, WQ), F32), pltpu.VMEM((8, WQ), F32)]
                       + [pltpu.VMEM(r.shape, F32) for r in reduced] + _reduce_scratch(late) + [pltpu.SemaphoreType.DMA((n,))],
        compiler_params=pltpu.CompilerParams(vmem_limit_bytes=VMEM_ATTN),
    )(q, k, kt, v, do, _sum_matrix("suffix", 2), _sum_matrix("prefix", 1), *late)


MLA_SCALE = (QK_NOPE + QK_ROPE) ** -0.5
LOG2E = 1.4426950408889634


def _mla_keys(kb):
    zero = jnp.zeros((BK, LANES), kb.dtype)
    return jnp.concatenate([jnp.concatenate([kb[:, 0:LANES], zero], axis=1),
                            jnp.concatenate([zero, kb[:, LANES:2 * LANES]], axis=1)], axis=0)


def _mla_fwd(qc, kc, vt):
    s = qc.shape[0]
    rows_l = 16

    def body(q_ref, k_ref, vt_ref, o_ref, l_ref, p_scr, ot_scr, st_scr):
        i = pl.program_id(1)
        keyc = lax.broadcasted_iota(jnp.int32, (BK, MQ), 0)
        qryc = (lax.broadcasted_iota(jnp.int32, (BK, MQ), 1) + i * MQ) // 64
        row = lax.broadcasted_iota(jnp.int32, (LANES, 1), 0)
        qw = q_ref[...]
        orow = lax.broadcasted_iota(jnp.int32, (rows_l, 2 * BK), 0)
        ocol = lax.broadcasted_iota(jnp.int32, (rows_l, 2 * BK), 1)
        ones = jnp.where(((orow == 0) & (ocol < BK)) | ((orow == 1) & (ocol >= BK)), 1.0, 0.0).astype(BF16)

        def scores(j, lo=0):
            ks = pl.multiple_of(j * BK, BK)
            return _mm_nt(_mla_keys(k_ref[pl.ds(ks, BK), :]), qw[lo:])

        def values_t(j):
            vtb = vt_ref[:, pl.ds(pl.multiple_of(j * BK, BK), BK)]
            zero = jnp.zeros_like(vtb)
            top = jnp.concatenate([jnp.where(row < 64, vtb, zero), jnp.where(row >= 64, vtb, zero)], axis=1)
            return jnp.concatenate([top, ones], axis=0)

        def softmax(ja, za, zb, masked, lo=0):
            c = MLA_SCALE * LOG2E
            parts = [za[0:BK] * c, za[BK:2 * BK] * c, zb[0:BK] * c, zb[BK:2 * BK] * c]
            if masked:
                va = ((keyc[:, lo:] + ja * BK) // 64) <= qryc[:, lo:]
                vb = ((keyc[:, lo:] + (ja + 1) * BK) // 64) <= qryc[:, lo:]
                parts = [jnp.where(va, parts[0], -1e30), jnp.where(va, parts[1], -1e30),
                         jnp.where(vb, parts[2], -1e30), jnp.where(vb, parts[3], -1e30)]
            m0, m1 = st_scr[0:1, lo:], st_scr[1:2, lo:]
            n0 = jnp.maximum(m0, jnp.max(jnp.maximum(parts[0], parts[2]), axis=0, keepdims=True))
            n1 = jnp.maximum(m1, jnp.max(jnp.maximum(parts[1], parts[3]), axis=0, keepdims=True))
            st_scr[2:3, lo:] = jnp.exp2(m0 - n0)
            st_scr[3:4, lo:] = jnp.exp2(m1 - n1)
            st_scr[0:1, lo:] = n0
            st_scr[1:2, lo:] = n1
            p_scr[:, lo:] = jnp.concatenate([jnp.exp2(parts[0] - n0), jnp.exp2(parts[1] - n1),
                                             jnp.exp2(parts[2] - n0), jnp.exp2(parts[3] - n1)], axis=0).astype(BF16)

        def accumulate(ja, lo=0):
            pv = _mm(jnp.concatenate([values_t(ja), values_t(ja + 1)], axis=1), p_scr[:, lo:])
            a = jnp.where(row < 64, st_scr[2:3, lo:], st_scr[3:4, lo:])
            ot_scr[0:LANES, lo:] = a * ot_scr[0:LANES, lo:] + pv[0:LANES]
            ot_scr[LANES:LANES + 8, lo:] = st_scr[2:10, lo:] * ot_scr[LANES:LANES + 8, lo:] + pv[LANES:LANES + 8]

        def step(n, masked, lo=0, prev_lo=0):
            za, zb = scores(2 * n, lo), scores(2 * n + 1, lo)
            accumulate(2 * n - 2, prev_lo)
            softmax(2 * n, za, zb, masked, lo)

        def first(masked):
            softmax(0, scores(0), scores(1), masked)

        st_scr[...] = jnp.concatenate([jnp.full((2, MQ), -1e30, F32), jnp.ones((14, MQ), F32)], axis=0)
        ot_scr[...] = jnp.zeros_like(ot_scr)

        npq = MQ // (2 * BK)
        seen = lambda d: 2 * BK * max(d, 0)

        @pl.when(i == 0)
        def _():
            first(True)
            for d in range(1, npq):
                step(d, True, seen(d), seen(d - 1))

        @pl.when(i > 0)
        def _():
            first(False)
            lax.fori_loop(1, npq * i, lambda n, c: (step(n, False), c)[1], 0)
            for d in range(npq):
                step(npq * i + d, True, seen(d), seen(d - 1))

        accumulate(2 * (npq * (i + 1) - 1), seen(npq - 1))
        l0, l1 = ot_scr[LANES:LANES + 1, :], ot_scr[LANES + 1:LANES + 2, :]
        o_ref[...] = (ot_scr[0:LANES, :] / jnp.where(row < 64, l0, l1)).T
        l_ref[...] = jnp.where(row < 64, st_scr[0:1, :] + jnp.log2(l0), st_scr[1:2, :] + jnp.log2(l1)).T

    qspec = pl.BlockSpec((MQ, 2 * LANES), lambda p, i: (i, p))
    kspec = pl.BlockSpec((s, 2 * LANES), lambda p, i: (0, p))
    vtspec = pl.BlockSpec((LANES, s), lambda p, i: (p, 0))
    ospec = pl.BlockSpec((MQ, LANES), lambda p, i: (i, p))
    return pl.pallas_call(
        body, name="mla_fwd", grid=(4, s // MQ),
        out_shape=(jax.ShapeDtypeStruct((s, 512), F32), jax.ShapeDtypeStruct((s, 512), F32)),
        in_specs=[qspec, kspec, vtspec], out_specs=(ospec, ospec),
        scratch_shapes=[pltpu.VMEM((4 * BK, MQ), BF16), pltpu.VMEM((LANES + 8, MQ), F32), pltpu.VMEM((16, MQ), F32)],
        compiler_params=pltpu.CompilerParams(vmem_limit_bytes=VMEM_ATTN),
    )(qc, kc, vt)


def _mla_bwd(qc, kc, kct, v, do, lse, delta):
    s = qc.shape[0]

    def body(q_ref, k_ref, kt_ref, v_ref, do_ref, l_ref, d_ref, dq_ref, dk_ref, dv_ref, dqt_scr, p_scr, dz_scr):
        i = pl.program_id(1)

        @pl.when(i == 0)
        def _():
            dk_ref[...] = jnp.zeros_like(dk_ref)
            dv_ref[...] = jnp.zeros_like(dv_ref)

        lane = lax.broadcasted_iota(jnp.int32, (1, LANES), 1)
        keyc = lax.broadcasted_iota(jnp.int32, (BK, MQ), 0)
        qryc = (lax.broadcasted_iota(jnp.int32, (BK, MQ), 1) + i * MQ) // 64
        qw = q_ref[...]
        dob = do_ref[...]
        dost = (dob.astype(F32) * MLA_SCALE).T.astype(BF16)
        lt = l_ref[...].T
        dt = (d_ref[...] * MLA_SCALE).T
        lse0, lse1 = lt[0:1], lt[64:65]
        dl0, dl1 = dt[0:1], dt[64:65]
        dqt_scr[...] = jnp.zeros_like(dqt_scr)

        def products(j, lo=0):
            ks = pl.multiple_of(j * BK, BK)
            return (_mm_nt(_mla_keys(k_ref[pl.ds(ks, BK), :]), qw[lo:]),
                    _mm(_pair_stack(v_ref[pl.ds(ks, BK), :], lane), dost[:, lo:]))

        def grads(j, slot, zt, dwt, masked, lo=0):
            zt = zt * (MLA_SCALE * LOG2E)
            p0 = jnp.exp2(zt[0:BK] - lse0[:, lo:])
            p1 = jnp.exp2(zt[BK:2 * BK] - lse1[:, lo:])
            if masked:
                valid = ((keyc[:, lo:] + j * BK) // 64) <= qryc[:, lo:]
                p0, p1 = jnp.where(valid, p0, 0.0), jnp.where(valid, p1, 0.0)
            p_scr[slot, :, lo:] = jnp.concatenate([p0, p1], axis=0).astype(BF16)
            dz_scr[slot, :, lo:] = jnp.concatenate([p0 * (dwt[0:BK] - dl0[:, lo:]), p1 * (dwt[BK:2 * BK] - dl1[:, lo:])],
                                                   axis=0).astype(BF16)

        def keys_t(ks):
            ktb = kt_ref[:, pl.ds(ks, BK)]
            zero = jnp.zeros((LANES, BK), ktb.dtype)
            return jnp.concatenate([jnp.concatenate([ktb[0:LANES], zero], axis=1),
                                    jnp.concatenate([zero, ktb[LANES:2 * LANES]], axis=1)], axis=0)

        def scatter(ja, lo=0):
            ksa, ksb = pl.multiple_of(ja * BK, BK), pl.multiple_of((ja + 1) * BK, BK)
            dqt_scr[:, lo:] += _mm(jnp.concatenate([keys_t(ksa), keys_t(ksb)], axis=1),
                                   jnp.concatenate([dz_scr[0, :, lo:], dz_scr[1, :, lo:]], axis=0))
            for slot, ks in ((0, ksa), (1, ksb)):
                rk = _mm(dz_scr[slot, :, lo:], qw[lo:])
                dk_ref[pl.ds(ks, BK), :] += jnp.concatenate([rk[0:BK, 0:LANES], rk[BK:2 * BK, LANES:2 * LANES]], axis=1)
                rv = _mm(p_scr[slot, :, lo:], dob[lo:])
                dv_ref[pl.ds(ks, BK), :] += jnp.where(lane < 64, rv[0:BK], rv[BK:2 * BK])

        def step(n, masked, lo=0, prev_lo=0):
            za, wa = products(2 * n, lo)
            zb, wb = products(2 * n + 1, lo)
            scatter(2 * n - 2, prev_lo)
            grads(2 * n, 0, za, wa, masked, lo)
            grads(2 * n + 1, 1, zb, wb, masked, lo)

        def first(masked):
            za, wa = products(0)
            zb, wb = products(1)
            grads(0, 0, za, wa, masked)
            grads(1, 1, zb, wb, masked)

        npq = MQ // (2 * BK)
        seen = lambda d: 2 * BK * max(d, 0)

        @pl.when(i == 0)
        def _():
            first(True)
            for d in range(1, npq):
                step(d, True, seen(d), seen(d - 1))

        @pl.when(i > 0)
        def _():
            first(False)
            lax.fori_loop(1, npq * i, lambda n, c: (step(n, False), c)[1], 0)
            for d in range(npq):
                step(npq * i + d, True, seen(d), seen(d - 1))

        scatter(2 * (npq * (i + 1) - 1), seen(npq - 1))
        dq_ref[...] = dqt_scr[...].T

    qspec = pl.BlockSpec((MQ, 2 * LANES), lambda p, i: (i, p))
    kspec = pl.BlockSpec((s, 2 * LANES), lambda p, i: (0, p))
    ktspec = pl.BlockSpec((2 * LANES, s), lambda p, i: (p, 0))
    vspec = pl.BlockSpec((s, LANES), lambda p, i: (0, p))
    ospec = pl.BlockSpec((MQ, LANES), lambda p, i: (i, p))
    return pl.pallas_call(
        body, name="mla_bwd", grid=(4, s // MQ),
        out_shape=(jax.ShapeDtypeStruct((s, 1024), F32), jax.ShapeDtypeStruct((s, 1024), F32),
                   jax.ShapeDtypeStruct((s, 512), F32)),
        in_specs=[qspec, kspec, ktspec, vspec, ospec, ospec, ospec], out_specs=(qspec, kspec, vspec),
        scratch_shapes=[pltpu.VMEM((2 * LANES, MQ), F32), pltpu.VMEM((2, 2 * BK, MQ), BF16), pltpu.VMEM((2, 2 * BK, MQ), BF16)],
        compiler_params=pltpu.CompilerParams(vmem_limit_bytes=VMEM_ATTN),
    )(qc, kc, kct, v, do, lse, delta)


def _post(x, p, tgt, sbo, mlao, sbg, mlag, gsb, gmla, wout, gpost, wple, gple, wpg, bpg):
    s = x.shape[0]

    def body(x_ref, p_ref, t_ref, sbo_ref, mlao_ref, sbg_ref, mlag_ref, gsb_ref, gmla_ref, wout_ref,
             gpost_ref, wple_ref, gple_ref, wpg_ref, bpg_ref, bd_ref,
             dsbo_ref, dmlao_ref, delta_ref, dsbg_ref, dmlag_ref, dxres_ref, dwout_ref, dwpg_ref, dwple_ref, vec_ref):
        i = pl.program_id(0)

        @pl.when(i == 0)
        def _():
            dwout_ref[...] = jnp.zeros_like(dwout_ref)
            dwpg_ref[...] = jnp.zeros_like(dwpg_ref)
            dwple_ref[...] = jnp.zeros_like(dwple_ref)
            vec_ref[...] = jnp.zeros_like(vec_ref)

        inv_hd = 1.0 / HEAD_DIM

        def head_fwd(o, g, gate):
            r = lax.rsqrt(_seg(o * o, bd_ref[...]) * inv_hd + EPS)
            hat = o * r
            n = hat * g
            sg = _sigmoid(gate)
            return hat, r, n, sg, n * (gate * sg)

        sbo, mlao, sbg_v, mlag_v = sbo_ref[...], mlao_ref[...], sbg_ref[...], mlag_ref[...]
        gsb_v, gmla_v = gsb_ref[...], gmla_ref[...]
        sb_hat, sb_r, sb_n, sb_sg, sb_y = head_fwd(sbo, gsb_v, sbg_v)
        ml_hat, ml_r, ml_n, ml_sg, ml_y = head_fwd(mlao, gmla_v, mlag_v)
        mix = jnp.concatenate([sb_y, ml_y], axis=1).astype(BF16)
        y = _mm(mix, wout_ref[...])
        ry = lax.rsqrt(_rowmean(y * y) + EPS)
        y_hat = y * ry
        gpost_v = gpost_ref[...]
        x1 = x_ref[...] + y_hat * gpost_v
        pb = p_ref[...].astype(BF16)
        pl_ = _mm(pb, wple_ref[...])
        rp = lax.rsqrt(_rowmean(pl_ * pl_) + EPS)
        pl_hat = pl_ * rp
        gple_v = gple_ref[...]
        ple = pl_hat * gple_v
        x1b = x1.astype(BF16)
        gate = _sigmoid(_mm(x1b, wpg_ref[...]) + bpg_ref[...])
        err = x1 + ple * gate - t_ref[...]
        loss = 0.5 * jnp.sum(_rowmean(err * err))
        dout = err * (1.0 / D_MODEL)

        du = dout * ple * gate * (1.0 - gate)
        dub = du.astype(BF16)
        dple = dout * gate
        dx1 = dout + _mm_nt(dub, wpg_ref[...])
        dwpg_ref[...] += _mm_tn(x1b, dub)
        dplh = dple * gple_v
        dpl = rp * (dplh - pl_hat * _rowmean(dplh * pl_hat))
        dwple_ref[...] += _mm_tn(pb, dpl.astype(BF16))
        dxres_ref[...] = dx1
        dyh = dx1 * gpost_v
        dy = ry * (dyh - y_hat * _rowmean(dyh * y_hat))
        dyb = dy.astype(BF16)
        dwout_ref[...] += _mm_tn(mix, dyb)
        dmix = _mm_nt(dyb, wout_ref[...])

        def head_bwd(dyv, hat, r, n, sg, g, gate):
            dn = dyv * (gate * sg)
            dgate = dyv * n * (sg * (1.0 + gate * (1.0 - sg)))
            dhat = dn * g
            do = r * (dhat - hat * (_seg(dhat * hat, bd_ref[...]) * inv_hd))
            return do, dgate, _colsum(dn * hat)

        dsbo, dsbg, dg_sb = head_bwd(dmix[:, 0:512], sb_hat, sb_r, sb_n, sb_sg, gsb_v, sbg_v)
        dmlao, dmlag, dg_ml = head_bwd(dmix[:, 512:1024], ml_hat, ml_r, ml_n, ml_sg, gmla_v, mlag_v)
        dsbo_ref[...] = dsbo.astype(BF16)
        dmlao_ref[...] = dmlao.astype(BF16)
        delta_ref[...] = _seg(dmlao * mlao, bd_ref[...])
        dsbg_ref[...] = dsbg.astype(BF16)
        dmlag_ref[...] = dmlag.astype(BF16)
        vec_ref[pl.ds(0, 1), :] += _colsum(dx1 * y_hat)
        vec_ref[pl.ds(1, 1), :] += _colsum(dple * pl_hat)
        vec_ref[pl.ds(2, 1), :] += _colsum(du)
        vec_ref[pl.ds(3, 1), :] += jnp.concatenate([dg_sb, dg_ml], axis=1)
        vec_ref[pl.ds(4, 1), :] += jnp.full((1, D_MODEL), loss, F32)

    out_shape = (
        jax.ShapeDtypeStruct((s, 512), BF16), jax.ShapeDtypeStruct((s, 512), BF16), jax.ShapeDtypeStruct((s, 512), F32),
        jax.ShapeDtypeStruct((s, 512), BF16), jax.ShapeDtypeStruct((s, 512), BF16), jax.ShapeDtypeStruct((s, D_MODEL), F32),
        jax.ShapeDtypeStruct((D_MODEL, D_MODEL), F32), jax.ShapeDtypeStruct((D_MODEL, D_MODEL), F32),
        jax.ShapeDtypeStruct((PLE_DIM, D_MODEL), F32), jax.ShapeDtypeStruct((8, D_MODEL), F32),
    )
    return pl.pallas_call(
        body, name="post_fwd_bwd", grid=(s // TM,), out_shape=out_shape,
        in_specs=[_rows(D_MODEL), _rows(PLE_DIM), _rows(D_MODEL), _rows(512), _rows(512), _rows(512), _rows(512),
                  _full((1, 512)), _full((1, 512)), _full((D_MODEL, D_MODEL)),
                  _full((1, D_MODEL)), _full((PLE_DIM, D_MODEL)), _full((1, D_MODEL)), _full((D_MODEL, D_MODEL)),
                  _full((1, D_MODEL)), _full((1024, 512))],
        out_specs=(_rows(512), _rows(512), _rows(512), _rows(512), _rows(512), _rows(D_MODEL),
                   _acc((D_MODEL, D_MODEL)), _acc((D_MODEL, D_MODEL)), _acc((PLE_DIM, D_MODEL)), _acc((8, D_MODEL))),
        compiler_params=pltpu.CompilerParams(vmem_limit_bytes=VMEM_DENSE),
    )(x, p, tgt, sbo, mlao, sbg, mlag, gsb, gmla, wout, gpost, wple, gple, wpg, bpg, _blockdiag2(512, HEAD_DIM))


def _pre_bwd(x, dxres, dsbq, dsbk, dsbv, dsbg, dmlag, dqc, dkc, dmv, cq, ckv, tabs, gpre, win, gq, wuq, gkv, wk, wv):
    s = x.shape[0]
    c_t, sa_t, sb_t = tabs
    rw = _rows

    def body(x_ref, dxres_ref, dsbq_ref, dsbk_ref, dsbv_ref, dsbg_ref, dmlag_ref, dqc_ref, dkc_ref, dmv_ref, cq_ref,
             ckv_ref, c_ref, sa_ref, sb_ref, gpre_ref, win_ref, gq_ref, wuq_ref, gkv_ref, wk_ref, wv_ref,
             gx_ref, dwin_ref, dwuq_ref, dwk_ref, dwv_ref, vec_ref, dwin_acc):
        i = pl.program_id(0)

        @pl.when(i == 0)
        def _():
            dwin_acc[...] = jnp.zeros_like(dwin_acc)
            dwuq_ref[...] = jnp.zeros_like(dwuq_ref)
            dwk_ref[...] = jnp.zeros_like(dwk_ref)
            dwv_ref[...] = jnp.zeros_like(dwv_ref)
            vec_ref[...] = jnp.zeros_like(vec_ref)

        lane = lax.broadcasted_iota(jnp.int32, (1, LANES), 1)
        c1, sa1, sb1 = c_ref[...], sa_ref[...], sb_ref[...]
        c8, sa8, sb8 = jnp.tile(c1, (1, 8)), jnp.tile(sa1, (1, 8)), jnp.tile(sb1, (1, 8))

        def norm_bwd(dn, hat, r, g):
            t = dn * g
            return r * (t - hat * _rowmean(t * hat)), _colsum(dn * hat)

        xv = x_ref[...]
        r1 = lax.rsqrt(_rowmean(xv * xv) + EPS)
        x_hat = xv * r1
        gpre_v = gpre_ref[...]
        hb = (x_hat * gpre_v).astype(BF16)
        ready = jnp.concatenate([dsbq_ref[...], dsbk_ref[...].astype(BF16), dsbv_ref[...].astype(BF16), dsbg_ref[...]], axis=1)
        dmlag = dmlag_ref[...]
        dwin_acc[:, 0:2048] += _mm_tn(hb, ready)
        dwin_acc[:, 2560:3072] += _mm_tn(hb, dmlag)
        dh = _mm_nt(ready, win_ref[:, 0:2048]) + _mm_nt(dmlag, win_ref[:, 2560:3072])

        dqeb = _rope_bwd(dqc_ref[...], c8, sa8, sb8).astype(BF16)
        cq = cq_ref[...]
        rq = lax.rsqrt(_rowmean(cq * cq) + EPS)
        cq_hat = cq * rq
        gq_v = gq_ref[...]
        dwuq_ref[...] += _mm_tn((cq_hat * gq_v).astype(BF16), dqeb)
        dcq, dg_q = norm_bwd(_mm_nt(dqeb, wuq_ref[...]), cq_hat, rq, gq_v)

        dkc = dkc_ref[...]
        dkcb = dkc.astype(BF16)
        dmvb = dmv_ref[...].astype(BF16)
        ckv = ckv_ref[...]
        rkv = lax.rsqrt(_rowmean(ckv * ckv) + EPS)
        ckv_hat = ckv * rkv
        gkv_v = gkv_ref[...]
        ckvnb = (ckv_hat * gkv_v).astype(BF16)
        dwk_ref[...] += _mm_tn(ckvnb, dkcb)
        dwv_ref[...] += _mm_tn(ckvnb, dmvb)
        dckv, dg_kv = norm_bwd(_mm_nt(dkcb, wk_ref[...]) + _mm_nt(dmvb, wv_ref[...]), ckv_hat, rkv, gkv_v)

        dkr = dkc[:, 0:LANES]
        for hh in range(1, 8):
            dkr = dkr + dkc[:, LANES * hh:LANES * (hh + 1)]
        dkr = _rope_bwd(dkr, c1, sa1, sb1)
        dkr = jnp.where((lane >= 64) & (lane < 96), dkr, 0.0)

        late = jnp.concatenate([dcq.astype(BF16), dckv.astype(BF16), dkr.astype(BF16)], axis=1)
        dwin_acc[:, 2048:2560] += _mm_tn(hb, late)
        dx, dg_pre = norm_bwd(dh + _mm_nt(late, win_ref[:, 2048:2560]), x_hat, r1, gpre_v)
        gx_ref[...] = dxres_ref[...] + dx
        vec_ref[pl.ds(0, 1), :] += dg_pre
        vec_ref[pl.ds(1, 1), :] += jnp.concatenate([dg_q, dg_kv, jnp.zeros((1, D_MODEL - Q_LORA - KV_LORA), F32)], axis=1)

        @pl.when(i == pl.num_programs(0) - 1)
        def _():
            pltpu.sync_copy(dwin_acc, dwin_ref)

    out_shape = (
        jax.ShapeDtypeStruct((s, D_MODEL), F32), jax.ShapeDtypeStruct((D_MODEL, D_EXT), F32),
        jax.ShapeDtypeStruct((Q_LORA, 1024), F32), jax.ShapeDtypeStruct((KV_LORA, 1024), F32),
        jax.ShapeDtypeStruct((KV_LORA, 512), F32), jax.ShapeDtypeStruct((8, D_MODEL), F32),
    )
    return pl.pallas_call(
        body, name="pre_bwd", grid=(s // TM,), out_shape=out_shape,
        in_specs=[rw(D_MODEL), rw(D_MODEL), rw(512), rw(512), rw(512), rw(512), rw(512),
                  rw(1024), rw(1024), rw(512), rw(Q_LORA), rw(KV_LORA), rw(LANES), rw(LANES),
                  rw(LANES), _full((1, D_MODEL)), _full((D_MODEL, D_EXT)), _full((1, Q_LORA)), _full((Q_LORA, 1024)),
                  _full((1, KV_LORA)), _full((KV_LORA, 1024)), _full((KV_LORA, 512))],
        out_specs=(rw(D_MODEL), pl.BlockSpec(memory_space=pl.ANY), _acc((Q_LORA, 1024)), _acc((KV_LORA, 1024)),
                   _acc((KV_LORA, 512)), _acc((8, D_MODEL))),
        scratch_shapes=[pltpu.VMEM((D_MODEL, D_EXT), F32)],
        compiler_params=pltpu.CompilerParams(vmem_limit_bytes=VMEM_DENSE),
    )(x, dxres, dsbq, dsbk, dsbv, dsbg, dmlag, dqc, dkc, dmv, cq, ckv, c_t, sa_t, sb_t, gpre, win, gq, wuq, gkv, wk, wv)


def _place():
    return lax.axis_index("x"), lax.axis_index("y"), lax.axis_index("c")


def _gather_steps(shapes, ins, bufs, send_sems, recv_sems):
    n = len(shapes)
    x, y, c = _place()
    me, sib = (x, y, c), (x, y, 1 - c)
    chips = [(1 - x, y), (x, 1 - y), (1 - x, 1 - y)]

    def half(t, chip, hc):
        rows = shapes[t][0] // 2
        return bufs[t].at[2 * chip[0] + chip[1], pl.ds(pl.multiple_of(hc * rows, 16), rows), :]

    def copy(k, t, chip, hc, to):
        return pltpu.make_async_remote_copy(src_ref=half(t, chip, hc), dst_ref=half(t, chip, hc), send_sem=send_sems.at[k],
                                            recv_sem=recv_sems.at[k], device_id=to, device_id_type=MESH)

    def start():
        for t in range(n):
            bufs[t][2 * x + y] = ins[t][...].astype(BF16)
            for j, chip in enumerate(chips):
                copy(6 * t + j, t, (x, y), c, (*chip, c)).start()

    def forward():
        for t in range(n):
            for j, chip in enumerate(chips):
                copy(6 * t + j, t, chip, c, me).wait_recv()
                copy(6 * t + 3 + j, t, chip, c, sib).start()

    def finish():
        for t in range(n):
            for j, chip in enumerate(chips):
                copy(6 * t + 3 + j, t, chip, 1 - c, me).wait_recv()
        for t in range(n):
            for j, chip in enumerate(chips):
                copy(6 * t + j, t, (x, y), c, (*chip, c)).wait_send()
                copy(6 * t + 3 + j, t, chip, c, sib).wait_send()

    return start, forward, finish


def _allgather_weights(shards):
    n = len(shards)

    def body(*refs):
        start, forward, finish = _gather_steps([a.shape for a in shards], refs[:n], refs[n:2 * n], refs[2 * n], refs[2 * n + 1])
        start()
        forward()
        finish()

    return pl.pallas_call(
        body, name="allgather_weights",
        out_shape=tuple(jax.ShapeDtypeStruct((N_SHARD,) + a.shape, BF16) for a in shards),
        in_specs=[pl.BlockSpec(memory_space=pltpu.VMEM)] * n, out_specs=(pl.BlockSpec(memory_space=pltpu.VMEM),) * n,
        scratch_shapes=[pltpu.SemaphoreType.DMA((6 * n,)), pltpu.SemaphoreType.DMA((6 * n,))],
        compiler_params=pltpu.CompilerParams(vmem_limit_bytes=VMEM_ATTN),
    )(*shards)


def _reduce_scratch(gsh):
    n = len(gsh)
    half_shapes = [(N_SHARD, a.shape[1] // 2, a.shape[2]) for a in gsh]
    return ([pltpu.VMEM(s_, F32) for s_ in half_shapes] * 2 + [pltpu.VMEM(s_, BF16) for s_ in half_shapes] * 2
            + [pltpu.SemaphoreType.DMA((n,)), pltpu.SemaphoreType.DMA((5 * n,)), pltpu.SemaphoreType.DMA((5 * n,))])


def _reduce_steps(halves, g_refs, f_refs, scratch):
    n = len(halves)
    accs, sibs, sbufs, rbufs = scratch[0:n], scratch[n:2 * n], scratch[2 * n:3 * n], scratch[3 * n:4 * n]
    local_sems, send_sems, recv_sems = scratch[4 * n:4 * n + 3]
    x, y, c = _place()
    me, sib = (x, y, c), (x, y, 1 - c)
    mine = 2 * x + y
    chips = [(1 - x, y), (x, 1 - y), (1 - x, 1 - y)]

    def remote(k, src, dst, to):
        return pltpu.make_async_remote_copy(src_ref=src, dst_ref=dst, send_sem=send_sems.at[k], recv_sem=recv_sems.at[k],
                                            device_id=to, device_id_type=MESH)

    def half3(ref, t, hc):
        return ref.at[:, pl.ds(pl.multiple_of(hc * halves[t], 8), halves[t]), :]

    def half2(ref, t, hc):
        return ref.at[pl.ds(pl.multiple_of(hc * halves[t], 8), halves[t]), :]

    def mine_load(t):
        return pltpu.make_async_copy(half3(g_refs[t], t, c), accs[t], local_sems.at[t])

    def to_sibling(t, to):
        return remote(t, half3(g_refs[t], t, 1 - c), sibs[t], to)

    def to_chip(t, j, chip, to):
        idx = 2 * chip[0] + chip[1]
        return remote(n + 3 * t + j, sbufs[t].at[idx], rbufs[t].at[mine if to is not me else idx], to)

    def swap(t, hc, to):
        return remote(4 * n + t, half2(f_refs[t], t, hc), half2(f_refs[t], t, hc), to)

    def load():
        for t in range(n):
            mine_load(t).start()
            to_sibling(t, sib).start()

    def partial():
        for t in range(n):
            mine_load(t).wait()
            to_sibling(t, me).wait_recv()
            for k in range(N_SHARD):
                accs[t][k] = accs[t][k] + sibs[t][k]
            for j, chip in enumerate(chips):
                idx = 2 * chip[0] + chip[1]
                sbufs[t][idx] = accs[t][idx].astype(BF16)
                to_chip(t, j, chip, (*chip, c)).start()

    def total():
        for t in range(n):
            acc = accs[t][mine]
            for j, chip in enumerate(chips):
                to_chip(t, j, chip, me).wait_recv()
                acc = acc + rbufs[t][2 * chip[0] + chip[1]].astype(F32)
            half2(f_refs[t], t, c)[...] = acc
            swap(t, c, sib).start()

    def finish():
        for t in range(n):
            swap(t, 1 - c, me).wait_recv()
        for t in range(n):
            to_sibling(t, sib).wait_send()
            for j, chip in enumerate(chips):
                to_chip(t, j, chip, (*chip, c)).wait_send()
            swap(t, c, sib).wait_send()

    return load, partial, total, finish


def _reduce_scatter_grads(gsh, vec):
    n = len(gsh)
    halves = [a.shape[1] // 2 for a in gsh]

    def body(*refs):
        g_refs, vec_ref, f_refs, vsum_ref = refs[:n], refs[n], refs[n + 1:2 * n + 1], refs[2 * n + 1]
        scratch = refs[2 * n + 2:]
        vrecv, vsend_sems, vrecv_sems = scratch[4 * n + 3:]
        load, partial, total, finish = _reduce_steps(halves, g_refs, f_refs, scratch)
        x, y, c = _place()
        my_dev = 4 * x + 2 * y + c

        def flip(k):
            return x ^ ((k >> 2) & 1), y ^ ((k >> 1) & 1), c ^ (k & 1)

        def vcopy(k, slot, to):
            return pltpu.make_async_remote_copy(src_ref=vec_ref, dst_ref=vrecv.at[slot], send_sem=vsend_sems.at[k - 1],
                                                recv_sem=vrecv_sems.at[k - 1], device_id=to, device_id_type=MESH)

        load()
        vrecv[my_dev] = vec_ref[...]
        for k in range(1, 8):
            vcopy(k, my_dev, flip(k)).start()
        partial()
        total()
        finish()
        for k in range(1, 8):
            fx, fy, fc = flip(k)
            vcopy(k, 4 * fx + 2 * fy + fc, (x, y, c)).wait_recv()
        vs = vrecv[0]
        for d in range(1, 8):
            vs = vs + vrecv[d]
        vsum_ref[...] = vs
        for k in range(1, 8):
            vcopy(k, my_dev, flip(k)).wait_send()

    return pl.pallas_call(
        body, name="reduce_scatter_grads",
        out_shape=tuple(jax.ShapeDtypeStruct(a.shape[1:], F32) for a in gsh) + (jax.ShapeDtypeStruct((VEC_ROWS, 1024), F32),),
        in_specs=[pl.BlockSpec(memory_space=pl.ANY)] * n + [pl.BlockSpec(memory_space=pltpu.VMEM)],
        out_specs=(pl.BlockSpec(memory_space=pltpu.VMEM),) * (n + 1),
        scratch_shapes=_reduce_scratch(gsh) + [pltpu.VMEM((8, VEC_ROWS, 1024), F32), pltpu.SemaphoreType.DMA((7,)),
                                               pltpu.SemaphoreType.DMA((7,))],
        compiler_params=pltpu.CompilerParams(vmem_limit_bytes=56 * 1024 * 1024),
    )(*gsh, vec)


def _adamw(w, g, m, v):
    rows, cols = w.shape
    tr = rows if rows <= 256 else 256

    def body(w_ref, g_ref, m_ref, v_ref, d_ref, nm_ref, nv_ref):
        d_ref[...], nm_ref[...], nv_ref[...] = _adam_math(w_ref[...], g_ref[...], m_ref[...], v_ref[...])

    spec = pl.BlockSpec((tr, cols), lambda i: (i, 0))
    shp = jax.ShapeDtypeStruct((rows, cols), F32)
    return pl.pallas_call(body, name="adamw", grid=(rows // tr,), out_shape=(shp, shp, shp),
                          in_specs=[spec] * 4, out_specs=(spec,) * 3)(w, g, m, v)


def _adam_math(w, g, m, v):
    m2 = ADAM_B1 * m + (1.0 - ADAM_B1) * g
    v2 = ADAM_B2 * v + (1.0 - ADAM_B2) * (g * g)
    m_hat = m2 / (1.0 - ADAM_B1 ** ADAM_STEP)
    v_hat = v2 / (1.0 - ADAM_B2 ** ADAM_STEP)
    return -ADAM_LR * (m_hat / (jnp.sqrt(v_hat) + ADAM_EPS) + ADAM_WD * w), m2, v2


def _adamw_small(vsum, w, m, v):
    names = [name for name, _, _, _ in _VEC_LAYOUT]
    k = len(names)

    def body(*refs):
        vs_ref, w_refs, m_refs, v_refs = refs[0], refs[1:1 + k], refs[1 + k:1 + 2 * k], refs[1 + 2 * k:1 + 3 * k]
        outs = refs[1 + 3 * k:]
        for idx, (_, r, c0, width) in enumerate(_VEC_LAYOUT):
            gv = vs_ref[pl.ds(r, 1), pl.ds(c0, width)]
            d, m2, v2 = _adam_math(w_refs[idx][...], gv, m_refs[idx][...], v_refs[idx][...])
            outs[idx][...], outs[k + idx][...], outs[2 * k + idx][...], outs[3 * k + idx][...] = gv, d, m2, v2

    shapes = tuple(jax.ShapeDtypeStruct(w[name].shape, F32) for name in names)
    res = pl.pallas_call(
        body, name="adamw_small", out_shape=shapes * 4,
        in_specs=[pl.BlockSpec(memory_space=pltpu.VMEM)] * (1 + 3 * k), out_specs=(pl.BlockSpec(memory_space=pltpu.VMEM),) * (4 * k),
    )(vsum, *[w[name] for name in names], *[m[name] for name in names], *[v[name] for name in names])
    return tuple({name: res[part * k + idx] for idx, name in enumerate(names)} for part in range(4))


_EARLY = ("w_in", "w_uq", "w_ukv")
_LATE = ("w_out", "w_ple", "w_ple_gate")
_BIG = _EARLY + _LATE
_KR_LOCAL = 2432 - 3 * (D_IN // N_SHARD)


def _extend_early(parts):
    cols = lambda a: a.transpose(1, 0, 2).reshape(a.shape[1], N_SHARD * a.shape[2])
    g = parts["w_in"]
    zeros = lambda n: jnp.zeros((D_MODEL, n), g.dtype)
    win_ext = jnp.concatenate([g[0], g[1], g[2], g[3][:, :_KR_LOCAL], zeros(64), g[3][:, _KR_LOCAL:_KR_LOCAL + QK_ROPE],
                               zeros(32), g[3][:, _KR_LOCAL + QK_ROPE:]], axis=1)
    wuq_ext = jnp.pad(cols(parts["w_uq"]).reshape(Q_LORA, 8, 96), ((0, 0), (0, 0), (0, 32))).reshape(Q_LORA, 1024)
    wukv = cols(parts["w_ukv"]).reshape(KV_LORA, 8, 128)
    wk_ext = jnp.pad(wukv[:, :, :64], ((0, 0), (0, 0), (0, 64))).reshape(KV_LORA, 1024)
    wv = wukv[:, :, 64:].reshape(KV_LORA, 512)
    return win_ext, wuq_ext, wk_ext, wv


def _shard_cols(a):
    return a.reshape(a.shape[0], N_SHARD, a.shape[1] // N_SHARD).transpose(1, 0, 2)


def _shard_rows(a):
    return a.reshape(N_SHARD, a.shape[0] // N_SHARD, a.shape[1])


def _shard_early_grads(dwin_ext, dwuq_ext, dwk_ext, dwv):
    e, w = dwin_ext, D_IN // N_SHARD
    last = jnp.concatenate([e[:, 3 * w:2432], e[:, 2496:2528], e[:, 2560:]], axis=1)
    dwuq = dwuq_ext.reshape(Q_LORA, 8, 128)[:, :, :96].reshape(Q_LORA, 768)
    dwukv = jnp.concatenate([dwk_ext.reshape(KV_LORA, 8, 128)[:, :, :64], dwv.reshape(KV_LORA, 8, 64)], axis=2)
    return [jnp.stack([e[:, 0:w], e[:, w:2 * w], e[:, 2 * w:3 * w], last]), _shard_cols(dwuq),
            _shard_cols(dwukv.reshape(KV_LORA, 1024))]


def _rope_tables(positions):
    half = QK_ROPE // 2
    freq = ROPE_THETA ** (-jnp.arange(half, dtype=F32) / half)
    ang = positions.astype(F32)[:, None] * freq
    cos, sin = jnp.cos(ang), jnp.sin(ang)
    s = positions.shape[0]
    z = lambda n: jnp.zeros((s, n), F32)
    c_t = jnp.concatenate([jnp.ones((s, 64), F32), cos, cos, z(32)], axis=1)
    sa_t = jnp.concatenate([z(64), -sin, z(16), z(32)], axis=1)
    sb_t = jnp.concatenate([z(64), z(16), sin, z(32)], axis=1)
    return c_t, sa_t, sb_t


def _local_grads(x, p, positions, tgt, gains, early, late):
    win_ext, wuq_ext, wk_ext, wv = _extend_early(early)
    tabs = _rope_tables(positions)
    g = gains
    sbq, sbk, sbv, sbg, mlag, cq, ckv, qc, kc, mv, sbkt, sbvt, kct, mvt = _pre_fwd(
        x, tabs, g["norm_pre_g"], win_ext, g["q_norm_g"], wuq_ext, g["kv_norm_g"], wk_ext, wv)
    sbo, wout4, wple4, wpg4 = _sb_fwd(sbq, sbk, sbvt, late)
    wout, wpg = wout4.reshape(D_MODEL, D_MODEL), wpg4.reshape(D_MODEL, D_MODEL)
    wple = wple4.transpose(1, 0, 2).reshape(PLE_DIM, D_MODEL)
    mlao, lse = _mla_fwd(qc, kc, mvt)
    dsbo, dmlao, delta, dsbg, dmlag, dxres, dwout, dwpg, dwple, vec_c = _post(
        x, p, tgt, sbo, mlao, sbg, mlag, g["sb_out_norm_g"], g["mla_out_norm_g"], wout, g["norm_post_g"], wple,
        g["ple_norm_g"], wpg, g["b_ple_gate"])
    dsbq, dsbk, dsbv, *late_grads = _sb_bwd(sbq, sbk, sbkt, sbv, dsbo, [_shard_rows(dwout), _shard_cols(dwple), _shard_rows(dwpg)])
    dqc, dkc, dmv = _mla_bwd(qc, kc, kct, mv, dmlao, lse, delta)
    gx, dwin_ext, dwuq_ext, dwk_ext, dwv, vec_d = _pre_bwd(
        x, dxres, dsbq, dsbk, dsbv, dsbg, dmlag, dqc, dkc, dmv, cq, ckv, tabs, g["norm_pre_g"], win_ext, g["q_norm_g"],
        wuq_ext, g["kv_norm_g"], wk_ext, wv)
    return gx, _shard_early_grads(dwin_ext, dwuq_ext, dwk_ext, dwv), late_grads, jnp.concatenate([vec_c, vec_d], axis=0)


_VEC_LAYOUT = (("norm_post_g", 0, 0, 1024), ("ple_norm_g", 1, 0, 1024), ("b_ple_gate", 2, 0, 1024), ("sb_out_norm_g", 3, 0, 512),
               ("mla_out_norm_g", 3, 512, 512), ("norm_pre_g", 8, 0, 1024), ("q_norm_g", 9, 0, 256), ("kv_norm_g", 9, 256, 128))
_LOSS_ROW = 4
_WEIGHT_ORDER = ("norm_pre_g", "w_in", "q_norm_g", "w_uq", "kv_norm_g", "w_ukv", "sb_out_norm_g", "mla_out_norm_g", "w_out",
                 "norm_post_g", "w_ple", "ple_norm_g", "w_ple_gate", "b_ple_gate")


def kernel(x, p, positions, norm_pre_g, w_in, q_norm_g, w_uq, kv_norm_g, w_ukv, sb_out_norm_g, mla_out_norm_g, w_out, norm_post_g, w_ple, ple_norm_g, w_ple_gate, b_ple_gate, loss_target, m_norm_pre_g, m_w_in, m_q_norm_g, m_w_uq, m_kv_norm_g, m_w_ukv, m_sb_out_norm_g, m_mla_out_norm_g, m_w_out, m_norm_post_g, m_w_ple, m_ple_norm_g, m_w_ple_gate, m_b_ple_gate, v_norm_pre_g, v_w_in, v_q_norm_g, v_w_uq, v_kv_norm_g, v_w_ukv, v_sb_out_norm_g, v_mla_out_norm_g, v_w_out, v_norm_post_g, v_w_ple, v_ple_norm_g, v_w_ple_gate, v_b_ple_gate):
    w = {"norm_pre_g": norm_pre_g, "w_in": w_in[0], "q_norm_g": q_norm_g, "w_uq": w_uq[0], "kv_norm_g": kv_norm_g, "w_ukv": w_ukv[0],
         "sb_out_norm_g": sb_out_norm_g, "mla_out_norm_g": mla_out_norm_g, "w_out": w_out[0], "norm_post_g": norm_post_g,
         "w_ple": w_ple[0], "ple_norm_g": ple_norm_g, "w_ple_gate": w_ple_gate[0], "b_ple_gate": b_ple_gate}
    m = {"norm_pre_g": m_norm_pre_g, "w_in": m_w_in[0], "q_norm_g": m_q_norm_g, "w_uq": m_w_uq[0], "kv_norm_g": m_kv_norm_g,
         "w_ukv": m_w_ukv[0], "sb_out_norm_g": m_sb_out_norm_g, "mla_out_norm_g": m_mla_out_norm_g, "w_out": m_w_out[0],
         "norm_post_g": m_norm_post_g, "w_ple": m_w_ple[0], "ple_norm_g": m_ple_norm_g, "w_ple_gate": m_w_ple_gate[0],
         "b_ple_gate": m_b_ple_gate}
    v = {"norm_pre_g": v_norm_pre_g, "w_in": v_w_in[0], "q_norm_g": v_q_norm_g, "w_uq": v_w_uq[0], "kv_norm_g": v_kv_norm_g,
         "w_ukv": v_w_ukv[0], "sb_out_norm_g": v_sb_out_norm_g, "mla_out_norm_g": v_mla_out_norm_g, "w_out": v_w_out[0],
         "norm_post_g": v_norm_post_g, "w_ple": v_w_ple[0], "ple_norm_g": v_ple_norm_g, "w_ple_gate": v_w_ple_gate[0],
         "b_ple_gate": v_b_ple_gate}
    gathered = _allgather_weights([w[n] for n in _EARLY])
    gx, early_grads, late_red, vec = _local_grads(x[0], p[0, 0], positions[0], loss_target[0], w, dict(zip(_EARLY, gathered)),
                                                  [w[n] for n in _LATE])
    *early_red, vsum = _reduce_scatter_grads(early_grads, vec)
    gred = early_red + late_red
    loss = vsum[_LOSS_ROW, 0]

    g, delta, new_m, new_v = _adamw_small(vsum, w, m, v)
    for n, gn in zip(_BIG, gred):
        g[n] = gn
        delta[n], new_m[n], new_v[n] = _adamw(w[n], gn, m[n], v[n])

    lead = lambda n, a: a[None] if n in _BIG else a
    return (loss, gx[None],
            *[lead(n, g[n]) for n in _WEIGHT_ORDER], *[lead(n, delta[n]) for n in _WEIGHT_ORDER],
            *[lead(n, new_m[n]) for n in _WEIGHT_ORDER], *[lead(n, new_v[n]) for n in _WEIGHT_ORDER])
```

```python
import numpy as np
import jax
import jax.numpy as jnp
from jax import lax
from jax.experimental import pallas as pl
from jax.experimental.pallas import tpu as pltpu

F32 = jnp.float32
BF16 = jnp.bfloat16
MESH = pl.DeviceIdType.MESH

D_MODEL = 1024
HEAD_DIM = 64
D_SB = 512
D_MLA = 512
Q_LORA = 256
KV_LORA = 128
QK_NOPE = 64
QK_ROPE = 32
PLE_DIM = 256
D_IN = 2976
D_EXT = 3072
ROPE_THETA = 10000.0
EPS = 1e-6
N_SHARD = 4

ADAM_LR = 0.001
ADAM_B1 = 0.9
ADAM_B2 = 0.999
ADAM_EPS = 1e-08
ADAM_WD = 0.01
ADAM_STEP = 10

LANES = 128
BK = 128
WQ = 256
MQ = 512
SB_CUTOFF = 120.0
TM = 256
TM_PRE = 256
VEC_ROWS = 16
VMEM_DENSE = 52 * 1024 * 1024
VMEM_ATTN = 40 * 1024 * 1024


def _mm(a, b):
    return jnp.dot(a, b, preferred_element_type=F32)


def _mm_nt(a, b):
    return lax.dot_general(a, b, (((1,), (1,)), ((), ())), preferred_element_type=F32)


def _mm_tn(a, b):
    return lax.dot_general(a, b, (((0,), (0,)), ((), ())), preferred_element_type=F32)


def _seg(a, bd2):
    return _mm(_split2(a), bd2)


def _const(mask):
    return jnp.asarray(np.asarray(mask, np.float32), dtype=BF16)


def _blockdiag2(n, seg):
    r = (np.arange(2 * n)[:, None] % n) // seg
    c = np.arange(n)[None, :] // seg
    return _const(r == c)


def _sigmoid(a):
    return 1.0 / (1.0 + jnp.exp(-a))


def _rowmean(a):
    return jnp.mean(a, axis=-1, keepdims=True)


def _colsum(a):
    return jnp.sum(a, axis=0, keepdims=True)


def _rope_fwd(a, c, sa, sb):
    w = a.shape[-1]
    return a * c + pltpu.roll(a, w - 16, 1) * sa + pltpu.roll(a, 16, 1) * sb


def _rope_bwd(g, c, sa, sb):
    w = g.shape[-1]
    return g * c + pltpu.roll(g * sa, 16, 1) + pltpu.roll(g * sb, w - 16, 1)


def _full(shape):
    return pl.BlockSpec(shape, lambda *_: (0,) * len(shape))


def _acc(shape):
    return pl.BlockSpec(shape, lambda *_: (0,) * len(shape))


def _full2(shape):
    return pl.BlockSpec(shape, lambda p, i: (0, 0))


def _cols(height, tm=TM):
    return pl.BlockSpec((height, tm), lambda i: (0, i))


def _rows(width, tm=TM):
    return pl.BlockSpec((tm, width), lambda i: (i, 0))


def _pre_fwd(x, tabs, gpre, win, gq, wuq, gkv, wk, wv):
    s = x.shape[0]
    c_t, sa_t, sb_t = tabs
    rw, cl = (lambda width: _rows(width, TM_PRE)), (lambda height: _cols(height, TM_PRE))

    def body(x_ref, c_ref, sa_ref, sb_ref, gpre_ref, win_ref, gq_ref, wuq_ref, gkv_ref, wk_ref, wv_ref,
             sbq_ref, sbk_ref, sbv_ref, sbg_ref, mlag_ref, cq_ref, ckv_ref, qc_ref, kc_ref, mv_ref,
             sbkt_ref, sbvt_ref, kct_ref, mvt_ref):
        xv = x_ref[...]
        r1 = lax.rsqrt(_rowmean(xv * xv) + EPS)
        h = (xv * r1 * gpre_ref[...]).astype(BF16)
        proj = _mm(h, win_ref[...])
        sbq_ref[...] = proj[:, 0:512].astype(BF16)
        sbk_ref[...] = proj[:, 512:1024].astype(BF16)
        sbv_ref[...] = proj[:, 1024:1536].astype(BF16)
        sbkt_ref[...] = proj[:, 512:1024].T.astype(BF16)
        sbvt_ref[...] = proj[:, 1024:1536].T.astype(BF16)
        sbg_ref[...] = proj[:, 1536:2048]
        cq = proj[:, 2048:2304]
        ckv = proj[:, 2304:2432]
        kr = proj[:, 2432:2560]
        mlag_ref[...] = proj[:, 2560:3072]
        cq_ref[...] = cq
        ckv_ref[...] = ckv
        c1, sa1, sb1 = c_ref[...], sa_ref[...], sb_ref[...]
        c8, sa8, sb8 = jnp.tile(c1, (1, 8)), jnp.tile(sa1, (1, 8)), jnp.tile(sb1, (1, 8))
        cqn = (cq * lax.rsqrt(_rowmean(cq * cq) + EPS) * gq_ref[...]).astype(BF16)
        qe = _mm(cqn, wuq_ref[...])
        qc_ref[...] = _rope_fwd(qe, c8, sa8, sb8).astype(BF16)
        ckvn = (ckv * lax.rsqrt(_rowmean(ckv * ckv) + EPS) * gkv_ref[...]).astype(BF16)
        ke = _mm(ckvn, wk_ref[...])
        krr = _rope_fwd(kr, c1, sa1, sb1)
        kcat = ke + jnp.tile(krr, (1, 8))
        kc_ref[...] = kcat.astype(BF16)
        kct_ref[...] = kcat.T.astype(BF16)
        mval = _mm(ckvn, wv_ref[...])
        mv_ref[...] = mval.astype(BF16)
        mvt_ref[...] = mval.T.astype(BF16)

    out_shape = (
        jax.ShapeDtypeStruct((s, 512), BF16), jax.ShapeDtypeStruct((s, 512), BF16), jax.ShapeDtypeStruct((s, 512), BF16),
        jax.ShapeDtypeStruct((s, 512), F32), jax.ShapeDtypeStruct((s, 512), F32),
        jax.ShapeDtypeStruct((s, Q_LORA), F32), jax.ShapeDtypeStruct((s, KV_LORA), F32),
        jax.ShapeDtypeStruct((s, 1024), BF16), jax.ShapeDtypeStruct((s, 1024), BF16), jax.ShapeDtypeStruct((s, 512), BF16),
        jax.ShapeDtypeStruct((512, s), BF16), jax.ShapeDtypeStruct((512, s), BF16), jax.ShapeDtypeStruct((1024, s), BF16),
        jax.ShapeDtypeStruct((512, s), BF16),
    )
    return pl.pallas_call(
        body, name="pre_fwd", grid=(s // TM_PRE,), out_shape=out_shape,
        in_specs=[rw(D_MODEL), rw(LANES), rw(LANES), rw(LANES), _full((1, D_MODEL)), _full((D_MODEL, D_EXT)),
                  _full((1, Q_LORA)), _full((Q_LORA, 1024)), _full((1, KV_LORA)), _full((KV_LORA, 1024)), _full((KV_LORA, 512))],
        out_specs=(rw(512), rw(512), rw(512), rw(512), rw(512), rw(Q_LORA), rw(KV_LORA),
                   rw(1024), rw(1024), rw(512), cl(512), cl(512), cl(1024), cl(512)),
        compiler_params=pltpu.CompilerParams(vmem_limit_bytes=VMEM_DENSE),
    )(x, c_t, sa_t, sb_t, gpre, win, gq, wuq, gkv, wk, wv)


def _softplus(z):
    neg_abs = lax.bitcast_convert_type(lax.bitcast_convert_type(z, jnp.uint32) | jnp.uint32(0x80000000), F32)
    return jnp.maximum(z, 0.0) + jnp.log(1.0 + jnp.exp(neg_abs))


def _sum_matrix(kind, terms):
    r, c = np.arange(2 * BK)[:, None], np.arange(2 * BK * terms)[None, :] % (2 * BK)
    rk, ck = r % BK, c % BK
    return _const(((r // BK) == (c // BK)) & {"suffix": ck >= rk, "prefix": ck <= rk}[kind])


def _split_rows(a):
    hi = a.astype(BF16)
    return jnp.concatenate([hi, (a - hi.astype(F32)).astype(BF16)], axis=0)


def _heads_t(blk, rowi):
    zero = jnp.zeros_like(blk)
    return jnp.concatenate([jnp.where(rowi < 64, blk, zero), jnp.where(rowi >= 64, blk, zero)], axis=1)


def _mask_keys(a, valid, fill=0.0):
    return jnp.concatenate([jnp.where(valid, a[0:BK], fill), jnp.where(valid, a[BK:2 * BK], fill)], axis=0)


def _split2(a):
    hi = a.astype(BF16)
    lo = (a - hi.astype(F32)).astype(BF16)
    return jnp.concatenate([hi, lo], axis=1)


def _pair_stack(b, lane):
    zero = jnp.zeros_like(b)
    return jnp.concatenate([jnp.where(lane < 64, b, zero), jnp.where(lane >= 64, b, zero)], axis=0)


def _sb_fwd(q, k, vt, late):
    s = q.shape[0]
    n = len(late)

    def body(q_ref, k_ref, vt_ref, usuf_ref, *rest):
        ins, o_ref, outs = rest[:n], rest[n], rest[n + 1:2 * n + 1]
        acc_scr, run_scr = rest[2 * n + 1:2 * n + 3]
        bufs, (send_sems, recv_sems, out_sems) = rest[2 * n + 3:3 * n + 3], rest[3 * n + 3:]
        p, i = pl.program_id(0), pl.program_id(1)
        gather_start, gather_forward, gather_finish = _gather_steps([a.shape for a in late], ins, bufs, send_sems, recv_sems)

        @pl.when((p == 0) & (i == 0))
        def _():
            gather_start()

        @pl.when((p == 2) & (i == 0))
        def _():
            gather_forward()

        lane = lax.broadcasted_iota(jnp.int32, (1, LANES), 1)
        rowi = lax.broadcasted_iota(jnp.int32, (LANES, 1), 0)
        keyi = lax.broadcasted_iota(jnp.int32, (BK, WQ), 0)
        qryi = lax.broadcasted_iota(jnp.int32, (BK, WQ), 1) + i * WQ
        qs = q_ref[...] * (HEAD_DIM ** -0.5)

        def group(blocks, masked, seen=None):
            seen = seen or [0] * len(blocks)
            starts = [pl.multiple_of(j * BK, BK) for j in blocks]
            valid = [(keyi[:, lo:] + j * BK) < qryi[:, lo:] if m else None for j, m, lo in zip(blocks, masked, seen)]
            zs = [_mm_nt(_pair_stack(k_ref[pl.ds(ks, BK), :], lane), qs[lo:]) for ks, lo in zip(starts, seen)]
            sps = [_softplus(z) for z in zs]
            sps = [sp if ok is None else _mask_keys(sp, ok) for sp, ok in zip(sps, valid)]
            cums = [_mm(usuf_ref[...], _split_rows(sp)) for sp in sps]
            ws = [jnp.exp(z - c) for z, c in zip(zs, cums)]
            ws = [w if ok is None else _mask_keys(w, ok) for w, ok in zip(ws, valid)]
            pvs = [_mm(_heads_t(vt_ref[:, pl.ds(ks, BK)], rowi), w.astype(BF16)) for ks, w in zip(starts, ws)]
            for pv, c, lo in zip(pvs, cums, seen):
                r0, r1 = run_scr[0:1, lo:], run_scr[1:2, lo:]
                acc_scr[:, lo:] += jnp.where(rowi < 64, jnp.exp(-r0), jnp.exp(-r1)) * pv
                run_scr[0:1, lo:] = r0 + c[0:1]
                run_scr[1:2, lo:] = r1 + c[BK:BK + 1]

        assert WQ == 2 * BK
        acc_scr[...] = jnp.zeros_like(acc_scr)
        run_scr[...] = jnp.zeros_like(run_scr)

        @pl.when(i == 0)
        def _():
            group([1, 0], [True, True], [BK, 0])

        @pl.when(i > 0)
        def _():
            group([2 * i + 1, 2 * i, 2 * i - 1, 2 * i - 2], [True, True, False, False], [BK, 0, 0, 0])

        def unfinished():
            return (jnp.min(run_scr[0:2, :]) < SB_CUTOFF).astype(jnp.int32)

        def step(c):
            group([2 * i - 1 - 2 * c[0], 2 * i - 2 - 2 * c[0]], [False, False])
            return c[0] + 1, unfinished()

        lax.while_loop(lambda c: (c[0] < i) & (c[1] > 0), step, (jnp.int32(1), unfinished()))
        o_ref[...] = acc_scr[...].T

        @pl.when((p == pl.num_programs(0) - 1) & (i == pl.num_programs(1) - 1))
        def _():
            gather_finish()
            copies = [pltpu.make_async_copy(bufs[t], outs[t], out_sems.at[t]) for t in range(n)]
            for cp in copies:
                cp.start()
            for cp in copies:
                cp.wait()

    qspec = pl.BlockSpec((WQ, LANES), lambda p, i: (i, p))
    kspec = pl.BlockSpec((s, LANES), lambda p, i: (0, p))
    tspec = pl.BlockSpec((LANES, s), lambda p, i: (p, 0))
    gathered = [jax.ShapeDtypeStruct((N_SHARD,) + a.shape, BF16) for a in late]
    return pl.pallas_call(
        body, name="sb_fwd", grid=(4, s // WQ),
        out_shape=(jax.ShapeDtypeStruct((s, 512), F32), *gathered),
        in_specs=[qspec, kspec, tspec, _full2((2 * BK, 4 * BK))] + [_full2(a.shape) for a in late],
        out_specs=(qspec,) + (pl.BlockSpec(memory_space=pl.ANY),) * n,
        scratch_shapes=[pltpu.VMEM((LANES, WQ), F32), pltpu.VMEM((8, WQ), F32)] + [pltpu.VMEM(g.shape, BF16) for g in gathered]
                       + [pltpu.SemaphoreType.DMA((6 * n,)), pltpu.SemaphoreType.DMA((6 * n,)), pltpu.SemaphoreType.DMA((n,))],
        compiler_params=pltpu.CompilerParams(vmem_limit_bytes=VMEM_ATTN),
    )(q, k, vt, _sum_matrix("suffix", 2), *late)


def _sb_bwd(q, k, kt, v, do, late):
    s = q.shape[0]
    n = len(late)
    halves = [a.shape[1] // 2 for a in late]

    def body(q_ref, k_ref, kt_ref, v_ref, do_ref, usuf_ref, upre_ref, *rest):
        g_refs, (dq_ref, dk_ref, dv_ref), outs = rest[:n], rest[n:n + 3], rest[n + 3:2 * n + 3]
        later_scr, dqt_scr, st_scr = rest[2 * n + 3:2 * n + 6]
        f_scr, reduce_scr, out_sems = rest[2 * n + 6:3 * n + 6], rest[3 * n + 6:-1], rest[-1]
        p, i = pl.program_id(0), pl.program_id(1)
        reduce_load, reduce_partial, reduce_total, reduce_finish = _reduce_steps(halves, g_refs, f_scr, reduce_scr)

        @pl.when((p == 0) & (i == 0))
        def _():
            reduce_load()

        @pl.when((p == 1) & (i == 0))
        def _():
            reduce_partial()

        @pl.when((p == 3) & (i == 0))
        def _():
            reduce_total()

        @pl.when(i == 0)
        def _():
            dk_ref[...] = jnp.zeros_like(dk_ref)
            dv_ref[...] = jnp.zeros_like(dv_ref)

        lane = lax.broadcasted_iota(jnp.int32, (1, LANES), 1)
        rowi = lax.broadcasted_iota(jnp.int32, (LANES, 1), 0)
        keyi = lax.broadcasted_iota(jnp.int32, (BK, WQ), 0)
        qryi = lax.broadcasted_iota(jnp.int32, (BK, WQ), 1) + i * WQ
        qs = q_ref[...] * (HEAD_DIM ** -0.5)
        dob = do_ref[...]
        dot = dob.astype(F32).T.astype(BF16)

        def scores(j, lo=0):
            return _mm_nt(_pair_stack(k_ref[pl.ds(pl.multiple_of(j * BK, BK), BK), :], lane), qs[lo:])

        def scan(blocks, masked, seen=None):
            seen = seen or [0] * len(blocks)
            sps = [_softplus(scores(j, lo)) for j, lo in zip(blocks, seen)]
            sps = [_mask_keys(sp, (keyi[:, lo:] + j * BK) < qryi[:, lo:]) if m else sp
                   for sp, j, m, lo in zip(sps, blocks, masked, seen)]
            for sp, j, lo in zip(sps, blocks, seen):
                run = st_scr[0:2, :]
                later_scr[j, 0:2, :] = run
                st_scr[0:2, lo:] = run[:, lo:] + jnp.concatenate([jnp.sum(sp[0:BK], axis=0, keepdims=True),
                                                                  jnp.sum(sp[BK:2 * BK], axis=0, keepdims=True)], axis=0)

        def sweep(blocks, masked, seen=None):
            seen = seen or [0] * len(blocks)
            starts = [pl.multiple_of(j * BK, BK) for j in blocks]
            valid = [(keyi[:, lo:] + j * BK) < qryi[:, lo:] if m else None for j, m, lo in zip(blocks, masked, seen)]
            zs = [scores(j, lo) for j, lo in zip(blocks, seen)]
            us = [jnp.exp(lax.bitcast_convert_type(lax.bitcast_convert_type(z, jnp.uint32) | jnp.uint32(0x80000000), F32))
                  for z in zs]
            sps = [jnp.maximum(z, 0.0) + jnp.log(1.0 + u) for z, u in zip(zs, us)]
            sps = [sp if ok is None else _mask_keys(sp, ok) for sp, ok in zip(sps, valid)]
            sigs = [jnp.where(z >= 0.0, 1.0, u) / (1.0 + u) for z, u in zip(zs, us)]
            cums = [_mm(usuf_ref[...], _split_rows(sp)) for sp in sps]
            dws = [_mm(_pair_stack(v_ref[pl.ds(ks, BK), :], lane), dot[:, lo:]) for ks, lo in zip(starts, seen)]
            wfs = []
            for z, c, j, ok, lo in zip(zs, cums, blocks, valid, seen):
                f = jnp.exp(-later_scr[j, 0:2, lo:])
                wide = (BK, WQ - lo)
                wf = jnp.exp(z - c) * jnp.concatenate([jnp.broadcast_to(f[0:1], wide), jnp.broadcast_to(f[1:2], wide)], axis=0)
                wfs.append(wf if ok is None else _mask_keys(wf, ok))
            es = [dw * wf for dw, wf in zip(dws, wfs)]
            pres = [_mm(upre_ref[...], e.astype(BF16)) for e in es]
            dzs = []
            for e, pre, sig, ok, lo in zip(es, pres, sigs, valid, seen):
                e0 = pre[0:BK] + st_scr[0:1, lo:]
                e1 = pre[BK:2 * BK] + st_scr[1:2, lo:]
                st_scr[0:1, lo:] = e0[BK - 1:BK]
                st_scr[1:2, lo:] = e1[BK - 1:BK]
                dz = e - sig * jnp.concatenate([e0, e1], axis=0)
                dzs.append((dz if ok is None else _mask_keys(dz, ok)).astype(BF16))
            whole = [b for b, lo in enumerate(seen) if lo == 0]
            dqt_scr[...] += _mm(jnp.concatenate([_heads_t(kt_ref[:, pl.ds(starts[b], BK)], rowi) for b in whole], axis=1),
                                jnp.concatenate([dzs[b] for b in whole], axis=0))
            for b, lo in enumerate(seen):
                if lo:
                    dqt_scr[:, lo:] += _mm(_heads_t(kt_ref[:, pl.ds(starts[b], BK)], rowi), dzs[b])
            for ks, dz, wf, lo in zip(starts, dzs, wfs, seen):
                rk = _mm(dz, qs[lo:])
                dk_ref[pl.ds(ks, BK), :] += jnp.where(lane < 64, rk[0:BK], rk[BK:2 * BK])
                rv = _mm(wf.astype(BF16), dob[lo:])
                dv_ref[pl.ds(ks, BK), :] += jnp.where(lane < 64, rv[0:BK], rv[BK:2 * BK])

        assert WQ == 2 * BK
        st_scr[...] = jnp.zeros_like(st_scr)

        @pl.when(i == 0)
        def _():
            scan([1, 0], [True, True], [BK, 0])

        @pl.when(i > 0)
        def _():
            scan([2 * i + 1, 2 * i, 2 * i - 1, 2 * i - 2], [True, True, False, False], [BK, 0, 0, 0])

        def unfinished():
            return (jnp.min(st_scr[0:2, :]) < SB_CUTOFF).astype(jnp.int32)

        def step(c):
            scan([2 * i - 1 - 2 * c[0], 2 * i - 2 - 2 * c[0]], [False, False])
            return c[0] + 1, unfinished()

        npairs, _ = lax.while_loop(lambda c: (c[0] < i) & (c[1] > 0), step, (jnp.minimum(i, 1), unfinished()))

        st_scr[...] = jnp.zeros_like(st_scr)
        dqt_scr[...] = jnp.zeros_like(dqt_scr)
        first = 2 * (i - npairs)

        def early(t, carry):
            sweep([first + 2 * t, first + 2 * t + 1], [False, False])
            return carry

        lax.fori_loop(0, npairs - 1, early, 0)

        @pl.when(i == 0)
        def _():
            sweep([0, 1], [True, True], [0, BK])

        @pl.when(i > 0)
        def _():
            sweep([2 * i - 2, 2 * i - 1, 2 * i, 2 * i + 1], [False, False, True, True], [0, 0, 0, BK])

        dq_ref[...] = (dqt_scr[...].T * (HEAD_DIM ** -0.5)).astype(BF16)

        @pl.when((p == pl.num_programs(0) - 1) & (i == pl.num_programs(1) - 1))
        def _():
            reduce_finish()
            copies = [pltpu.make_async_copy(f_scr[t], outs[t], out_sems.at[t]) for t in range(n)]
            for cp in copies:
                cp.start()
            for cp in copies:
                cp.wait()

    qspec = pl.BlockSpec((WQ, LANES), lambda p, i: (i, p))
    kspec = pl.BlockSpec((s, LANES), lambda p, i: (0, p))
    tspec = pl.BlockSpec((LANES, s), lambda p, i: (p, 0))
    anywhere = pl.BlockSpec(memory_space=pl.ANY)
    reduced = [jax.ShapeDtypeStruct(a.shape[1:], F32) for a in late]
    return pl.pallas_call(
        body, name="sb_bwd", grid=(4, s // WQ),
        out_shape=(jax.ShapeDtypeStruct((s, 512), BF16), jax.ShapeDtypeStruct((s, 512), F32),
                   jax.ShapeDtypeStruct((s, 512), F32), *reduced),
        in_specs=[qspec, kspec, tspec, kspec, qspec, _full2((2 * BK, 4 * BK)), _full2((2 * BK, 2 * BK))] + [anywhere] * n,
        out_specs=(qspec, kspec, kspec) + (anywhere,) * n,
        scratch_shapes=[pltpu.VMEM((s // BK, 8, WQ), F32), pltpu.VMEM((LANES, WQ), F32), pltpu.VMEM((8, WQ), F32)]
                       + [pltpu.VMEM(r.shape, F32) for r in reduced] + _reduce_scratch(late) + [pltpu.SemaphoreType.DMA((n,))],
        compiler_params=pltpu.CompilerParams(vmem_limit_bytes=VMEM_ATTN),
    )(q, k, kt, v, do, _sum_matrix("suffix", 2), _sum_matrix("prefix", 1), *late)


MLA_SCALE = (QK_NOPE + QK_ROPE) ** -0.5
LOG2E = 1.4426950408889634


def _mla_keys(kb):
    zero = jnp.zeros((BK, LANES), kb.dtype)
    return jnp.concatenate([jnp.concatenate([kb[:, 0:LANES], zero], axis=1),
                            jnp.concatenate([zero, kb[:, LANES:2 * LANES]], axis=1)], axis=0)


def _mla_fwd(qc, kc, vt):
    s = qc.shape[0]
    rows_l = 16

    def body(q_ref, k_ref, vt_ref, o_ref, l_ref, p_scr, ot_scr, st_scr):
        i = pl.program_id(1)
        keyc = lax.broadcasted_iota(jnp.int32, (BK, MQ), 0)
        qryc = (lax.broadcasted_iota(jnp.int32, (BK, MQ), 1) + i * MQ) // 64
        row = lax.broadcasted_iota(jnp.int32, (LANES, 1), 0)
        qw = q_ref[...]
        orow = lax.broadcasted_iota(jnp.int32, (rows_l, 2 * BK), 0)
        ocol = lax.broadcasted_iota(jnp.int32, (rows_l, 2 * BK), 1)
        ones = jnp.where(((orow == 0) & (ocol < BK)) | ((orow == 1) & (ocol >= BK)), 1.0, 0.0).astype(BF16)

        def scores(j, lo=0):
            ks = pl.multiple_of(j * BK, BK)
            return _mm_nt(_mla_keys(k_ref[pl.ds(ks, BK), :]), qw[lo:])

        def values_t(j):
            vtb = vt_ref[:, pl.ds(pl.multiple_of(j * BK, BK), BK)]
            zero = jnp.zeros_like(vtb)
            top = jnp.concatenate([jnp.where(row < 64, vtb, zero), jnp.where(row >= 64, vtb, zero)], axis=1)
            return jnp.concatenate([top, ones], axis=0)

        def softmax(ja, za, zb, masked, lo=0):
            c = MLA_SCALE * LOG2E
            parts = [za[0:BK] * c, za[BK:2 * BK] * c, zb[0:BK] * c, zb[BK:2 * BK] * c]
            if masked:
                va = ((keyc[:, lo:] + ja * BK) // 64) <= qryc[:, lo:]
                vb = ((keyc[:, lo:] + (ja + 1) * BK) // 64) <= qryc[:, lo:]
                parts = [jnp.where(va, parts[0], -1e30), jnp.where(va, parts[1], -1e30),
                         jnp.where(vb, parts[2], -1e30), jnp.where(vb, parts[3], -1e30)]
            m0, m1 = st_scr[0:1, lo:], st_scr[1:2, lo:]
            n0 = jnp.maximum(m0, jnp.max(jnp.maximum(parts[0], parts[2]), axis=0, keepdims=True))
            n1 = jnp.maximum(m1, jnp.max(jnp.maximum(parts[1], parts[3]), axis=0, keepdims=True))
            st_scr[2:3, lo:] = jnp.exp2(m0 - n0)
            st_scr[3:4, lo:] = jnp.exp2(m1 - n1)
            st_scr[0:1, lo:] = n0
            st_scr[1:2, lo:] = n1
            p_scr[:, lo:] = jnp.concatenate([jnp.exp2(parts[0] - n0), jnp.exp2(parts[1] - n1),
                                             jnp.exp2(parts[2] - n0), jnp.exp2(parts[3] - n1)], axis=0).astype(BF16)

        def accumulate(ja, lo=0):
            pv = _mm(jnp.concatenate([values_t(ja), values_t(ja + 1)], axis=1), p_scr[:, lo:])
            a = jnp.where(row < 64, st_scr[2:3, lo:], st_scr[3:4, lo:])
            ot_scr[0:LANES, lo:] = a * ot_scr[0:LANES, lo:] + pv[0:LANES]
            ot_scr[LANES:LANES + 8, lo:] = st_scr[2:10, lo:] * ot_scr[LANES:LANES + 8, lo:] + pv[LANES:LANES + 8]

        def step(n, masked, lo=0, prev_lo=0):
            za, zb = scores(2 * n, lo), scores(2 * n + 1, lo)
            accumulate(2 * n - 2, prev_lo)
            softmax(2 * n, za, zb, masked, lo)

        def first(masked):
            softmax(0, scores(0), scores(1), masked)

        st_scr[...] = jnp.concatenate([jnp.full((2, MQ), -1e30, F32), jnp.ones((14, MQ), F32)], axis=0)
        ot_scr[...] = jnp.zeros_like(ot_scr)

        npq = MQ // (2 * BK)
        seen = lambda d: 2 * BK * max(d, 0)

        @pl.when(i == 0)
        def _():
            first(True)
            for d in range(1, npq):
                step(d, True, seen(d), seen(d - 1))

        @pl.when(i > 0)
        def _():
            first(False)
            lax.fori_loop(1, npq * i, lambda n, c: (step(n, False), c)[1], 0)
            for d in range(npq):
                step(npq * i + d, True, seen(d), seen(d - 1))

        accumulate(2 * (npq * (i + 1) - 1), seen(npq - 1))
        l0, l1 = ot_scr[LANES:LANES + 1, :], ot_scr[LANES + 1:LANES + 2, :]
        o_ref[...] = (ot_scr[0:LANES, :] / jnp.where(row < 64, l0, l1)).T
        l_ref[...] = jnp.where(row < 64, st_scr[0:1, :] + jnp.log2(l0), st_scr[1:2, :] + jnp.log2(l1)).T

    qspec = pl.BlockSpec((MQ, 2 * LANES), lambda p, i: (i, p))
    kspec = pl.BlockSpec((s, 2 * LANES), lambda p, i: (0, p))
    vtspec = pl.BlockSpec((LANES, s), lambda p, i: (p, 0))
    ospec = pl.BlockSpec((MQ, LANES), lambda p, i: (i, p))
    return pl.pallas_call(
        body, name="mla_fwd", grid=(4, s // MQ),
        out_shape=(jax.ShapeDtypeStruct((s, 512), F32), jax.ShapeDtypeStruct((s, 512), F32)),
        in_specs=[qspec, kspec, vtspec], out_specs=(ospec, ospec),
        scratch_shapes=[pltpu.VMEM((4 * BK, MQ), BF16), pltpu.VMEM((LANES + 8, MQ), F32), pltpu.VMEM((16, MQ), F32)],
        compiler_params=pltpu.CompilerParams(vmem_limit_bytes=VMEM_ATTN),
    )(qc, kc, vt)


def _mla_bwd(qc, kc, kct, v, do, lse, delta):
    s = qc.shape[0]

    def body(q_ref, k_ref, kt_ref, v_ref, do_ref, l_ref, d_ref, dq_ref, dk_ref, dv_ref, dqt_scr, p_scr, dz_scr):
        i = pl.program_id(1)

        @pl.when(i == 0)
        def _():
            dk_ref[...] = jnp.zeros_like(dk_ref)
            dv_ref[...] = jnp.zeros_like(dv_ref)

        lane = lax.broadcasted_iota(jnp.int32, (1, LANES), 1)
        keyc = lax.broadcasted_iota(jnp.int32, (BK, MQ), 0)
        qryc = (lax.broadcasted_iota(jnp.int32, (BK, MQ), 1) + i * MQ) // 64
        qw = q_ref[...]
        dob = do_ref[...]
        dost = (dob.astype(F32) * MLA_SCALE).T.astype(BF16)
        lt = l_ref[...].T
        dt = (d_ref[...] * MLA_SCALE).T
        lse0, lse1 = lt[0:1], lt[64:65]
        dl0, dl1 = dt[0:1], dt[64:65]
        dqt_scr[...] = jnp.zeros_like(dqt_scr)

        def products(j, lo=0):
            ks = pl.multiple_of(j * BK, BK)
            return (_mm_nt(_mla_keys(k_ref[pl.ds(ks, BK), :]), qw[lo:]),
                    _mm(_pair_stack(v_ref[pl.ds(ks, BK), :], lane), dost[:, lo:]))

        def grads(j, slot, zt, dwt, masked, lo=0):
            zt = zt * (MLA_SCALE * LOG2E)
            p0 = jnp.exp2(zt[0:BK] - lse0[:, lo:])
            p1 = jnp.exp2(zt[BK:2 * BK] - lse1[:, lo:])
            if masked:
                valid = ((keyc[:, lo:] + j * BK) // 64) <= qryc[:, lo:]
                p0, p1 = jnp.where(valid, p0, 0.0), jnp.where(valid, p1, 0.0)
            p_scr[slot, :, lo:] = jnp.concatenate([p0, p1], axis=0).astype(BF16)
            dz_scr[slot, :, lo:] = jnp.concatenate([p0 * (dwt[0:BK] - dl0[:, lo:]), p1 * (dwt[BK:2 * BK] - dl1[:, lo:])],
                                                   axis=0).astype(BF16)

        def keys_t(ks):
            ktb = kt_ref[:, pl.ds(ks, BK)]
            zero = jnp.zeros((LANES, BK), ktb.dtype)
            return jnp.concatenate([jnp.concatenate([ktb[0:LANES], zero], axis=1),
                                    jnp.concatenate([zero, ktb[LANES:2 * LANES]], axis=1)], axis=0)

        def scatter(ja, lo=0):
            ksa, ksb = pl.multiple_of(ja * BK, BK), pl.multiple_of((ja + 1) * BK, BK)
            dqt_scr[:, lo:] += _mm(jnp.concatenate([keys_t(ksa), keys_t(ksb)], axis=1),
                                   jnp.concatenate([dz_scr[0, :, lo:], dz_scr[1, :, lo:]], axis=0))
            for slot, ks in ((0, ksa), (1, ksb)):
                rk = _mm(dz_scr[slot, :, lo:], qw[lo:])
                dk_ref[pl.ds(ks, BK), :] += jnp.concatenate([rk[0:BK, 0:LANES], rk[BK:2 * BK, LANES:2 * LANES]], axis=1)
                rv = _mm(p_scr[slot, :, lo:], dob[lo:])
                dv_ref[pl.ds(ks, BK), :] += jnp.where(lane < 64, rv[0:BK], rv[BK:2 * BK])

        def step(n, masked, lo=0, prev_lo=0):
            za, wa = products(2 * n, lo)
            zb, wb = products(2 * n + 1, lo)
            scatter(2 * n - 2, prev_lo)
            grads(2 * n, 0, za, wa, masked, lo)
            grads(2 * n + 1, 1, zb, wb, masked, lo)

        def first(masked):
            za, wa = products(0)
            zb, wb = products(1)
            grads(0, 0, za, wa, masked)
            grads(1, 1, zb, wb, masked)

        npq = MQ // (2 * BK)
        seen = lambda d: 2 * BK * max(d, 0)

        @pl.when(i == 0)
        def _():
            first(True)
            for d in range(1, npq):
                step(d, True, seen(d), seen(d - 1))

        @pl.when(i > 0)
        def _():
            first(False)
            lax.fori_loop(1, npq * i, lambda n, c: (step(n, False), c)[1], 0)
            for d in range(npq):
                step(npq * i + d, True, seen(d), seen(d - 1))

        scatter(2 * (npq * (i + 1) - 1), seen(npq - 1))
        dq_ref[...] = dqt_scr[...].T

    qspec = pl.BlockSpec((MQ, 2 * LANES), lambda p, i: (i, p))
    kspec = pl.BlockSpec((s, 2 * LANES), lambda p, i: (0, p))
    ktspec = pl.BlockSpec((2 * LANES, s), lambda p, i: (p, 0))
    vspec = pl.BlockSpec((s, LANES), lambda p, i: (0, p))
    ospec = pl.BlockSpec((MQ, LANES), lambda p, i: (i, p))
    return pl.pallas_call(
        body, name="mla_bwd", grid=(4, s // MQ),
        out_shape=(jax.ShapeDtypeStruct((s, 1024), F32), jax.ShapeDtypeStruct((s, 1024), F32),
                   jax.ShapeDtypeStruct((s, 512), F32)),
        in_specs=[qspec, kspec, ktspec, vspec, ospec, ospec, ospec], out_specs=(qspec, kspec, vspec),
        scratch_shapes=[pltpu.VMEM((2 * LANES, MQ), F32), pltpu.VMEM((2, 2 * BK, MQ), BF16), pltpu.VMEM((2, 2 * BK, MQ), BF16)],
        compiler_params=pltpu.CompilerParams(vmem_limit_bytes=VMEM_ATTN),
    )(qc, kc, kct, v, do, lse, delta)


def _post(x, p, tgt, sbo, mlao, sbg, mlag, gsb, gmla, wout, gpost, wple, gple, wpg, bpg):
    s = x.shape[0]

    def body(x_ref, p_ref, t_ref, sbo_ref, mlao_ref, sbg_ref, mlag_ref, gsb_ref, gmla_ref, wout_ref,
             gpost_ref, wple_ref, gple_ref, wpg_ref, bpg_ref, bd_ref,
             dsbo_ref, dmlao_ref, delta_ref, dsbg_ref, dmlag_ref, dxres_ref, dwout_ref, dwpg_ref, dwple_ref, vec_ref):
        i = pl.program_id(0)

        @pl.when(i == 0)
        def _():
            dwout_ref[...] = jnp.zeros_like(dwout_ref)
            dwpg_ref[...] = jnp.zeros_like(dwpg_ref)
            dwple_ref[...] = jnp.zeros_like(dwple_ref)
            vec_ref[...] = jnp.zeros_like(vec_ref)

        inv_hd = 1.0 / HEAD_DIM

        def head_fwd(o, g, gate):
            r = lax.rsqrt(_seg(o * o, bd_ref[...]) * inv_hd + EPS)
            hat = o * r
            n = hat * g
            sg = _sigmoid(gate)
            return hat, r, n, sg, n * (gate * sg)

        sbo, mlao, sbg_v, mlag_v = sbo_ref[...], mlao_ref[...], sbg_ref[...], mlag_ref[...]
        gsb_v, gmla_v = gsb_ref[...], gmla_ref[...]
        sb_hat, sb_r, sb_n, sb_sg, sb_y = head_fwd(sbo, gsb_v, sbg_v)
        ml_hat, ml_r, ml_n, ml_sg, ml_y = head_fwd(mlao, gmla_v, mlag_v)
        mix = jnp.concatenate([sb_y, ml_y], axis=1).astype(BF16)
        y = _mm(mix, wout_ref[...])
        ry = lax.rsqrt(_rowmean(y * y) + EPS)
        y_hat = y * ry
        gpost_v = gpost_ref[...]
        x1 = x_ref[...] + y_hat * gpost_v
        pb = p_ref[...].astype(BF16)
        pl_ = _mm(pb, wple_ref[...])
        rp = lax.rsqrt(_rowmean(pl_ * pl_) + EPS)
        pl_hat = pl_ * rp
        gple_v = gple_ref[...]
        ple = pl_hat * gple_v
        x1b = x1.astype(BF16)
        gate = _sigmoid(_mm(x1b, wpg_ref[...]) + bpg_ref[...])
        err = x1 + ple * gate - t_ref[...]
        loss = 0.5 * jnp.sum(_rowmean(err * err))
        dout = err * (1.0 / D_MODEL)

        du = dout * ple * gate * (1.0 - gate)
        dub = du.astype(BF16)
        dple = dout * gate
        dx1 = dout + _mm_nt(dub, wpg_ref[...])
        dwpg_ref[...] += _mm_tn(x1b, dub)
        dplh = dple * gple_v
        dpl = rp * (dplh - pl_hat * _rowmean(dplh * pl_hat))
        dwple_ref[...] += _mm_tn(pb, dpl.astype(BF16))
        dxres_ref[...] = dx1
        dyh = dx1 * gpost_v
        dy = ry * (dyh - y_hat * _rowmean(dyh * y_hat))
        dyb = dy.astype(BF16)
        dwout_ref[...] += _mm_tn(mix, dyb)
        dmix = _mm_nt(dyb, wout_ref[...])

        def head_bwd(dyv, hat, r, n, sg, g, gate):
            dn = dyv * (gate * sg)
            dgate = dyv * n * (sg * (1.0 + gate * (1.0 - sg)))
            dhat = dn * g
            do = r * (dhat - hat * (_seg(dhat * hat, bd_ref[...]) * inv_hd))
            return do, dgate, _colsum(dn * hat)

        dsbo, dsbg, dg_sb = head_bwd(dmix[:, 0:512], sb_hat, sb_r, sb_n, sb_sg, gsb_v, sbg_v)
        dmlao, dmlag, dg_ml = head_bwd(dmix[:, 512:1024], ml_hat, ml_r, ml_n, ml_sg, gmla_v, mlag_v)
        dsbo_ref[...] = dsbo.astype(BF16)
        dmlao_ref[...] = dmlao.astype(BF16)
        delta_ref[...] = _seg(dmlao * mlao, bd_ref[...])
        dsbg_ref[...] = dsbg.astype(BF16)
        dmlag_ref[...] = dmlag.astype(BF16)
        vec_ref[pl.ds(0, 1), :] += _colsum(dx1 * y_hat)
        vec_ref[pl.ds(1, 1), :] += _colsum(dple * pl_hat)
        vec_ref[pl.ds(2, 1), :] += _colsum(du)
        vec_ref[pl.ds(3, 1), :] += jnp.concatenate([dg_sb, dg_ml], axis=1)
        vec_ref[pl.ds(4, 1), :] += jnp.full((1, D_MODEL), loss, F32)

    out_shape = (
        jax.ShapeDtypeStruct((s, 512), BF16), jax.ShapeDtypeStruct((s, 512), BF16), jax.ShapeDtypeStruct((s, 512), F32),
        jax.ShapeDtypeStruct((s, 512), BF16), jax.ShapeDtypeStruct((s, 512), BF16), jax.ShapeDtypeStruct((s, D_MODEL), F32),
        jax.ShapeDtypeStruct((D_MODEL, D_MODEL), F32), jax.ShapeDtypeStruct((D_MODEL, D_MODEL), F32),
        jax.ShapeDtypeStruct((PLE_DIM, D_MODEL), F32), jax.ShapeDtypeStruct((8, D_MODEL), F32),
    )
    return pl.pallas_call(
        body, name="post_fwd_bwd", grid=(s // TM,), out_shape=out_shape,
        in_specs=[_rows(D_MODEL), _rows(PLE_DIM), _rows(D_MODEL), _rows(512), _rows(512), _rows(512), _rows(512),
                  _full((1, 512)), _full((1, 512)), _full((D_MODEL, D_MODEL)),
                  _full((1, D_MODEL)), _full((PLE_DIM, D_MODEL)), _full((1, D_MODEL)), _full((D_MODEL, D_MODEL)),
                  _full((1, D_MODEL)), _full((1024, 512))],
        out_specs=(_rows(512), _rows(512), _rows(512), _rows(512), _rows(512), _rows(D_MODEL),
                   _acc((D_MODEL, D_MODEL)), _acc((D_MODEL, D_MODEL)), _acc((PLE_DIM, D_MODEL)), _acc((8, D_MODEL))),
        compiler_params=pltpu.CompilerParams(vmem_limit_bytes=VMEM_DENSE),
    )(x, p, tgt, sbo, mlao, sbg, mlag, gsb, gmla, wout, gpost, wple, gple, wpg, bpg, _blockdiag2(512, HEAD_DIM))


def _pre_bwd(x, dxres, dsbq, dsbk, dsbv, dsbg, dmlag, dqc, dkc, dmv, cq, ckv, tabs, gpre, win, gq, wuq, gkv, wk, wv):
    s = x.shape[0]
    c_t, sa_t, sb_t = tabs
    rw = _rows

    def body(x_ref, dxres_ref, dsbq_ref, dsbk_ref, dsbv_ref, dsbg_ref, dmlag_ref, dqc_ref, dkc_ref, dmv_ref, cq_ref,
             ckv_ref, c_ref, sa_ref, sb_ref, gpre_ref, win_ref, gq_ref, wuq_ref, gkv_ref, wk_ref, wv_ref,
             gx_ref, dwin_ref, dwuq_ref, dwk_ref, dwv_ref, vec_ref, dwin_acc):
        i = pl.program_id(0)

        @pl.when(i == 0)
        def _():
            dwin_acc[...] = jnp.zeros_like(dwin_acc)
            dwuq_ref[...] = jnp.zeros_like(dwuq_ref)
            dwk_ref[...] = jnp.zeros_like(dwk_ref)
            dwv_ref[...] = jnp.zeros_like(dwv_ref)
            vec_ref[...] = jnp.zeros_like(vec_ref)

        lane = lax.broadcasted_iota(jnp.int32, (1, LANES), 1)
        c1, sa1, sb1 = c_ref[...], sa_ref[...], sb_ref[...]
        c8, sa8, sb8 = jnp.tile(c1, (1, 8)), jnp.tile(sa1, (1, 8)), jnp.tile(sb1, (1, 8))

        def norm_bwd(dn, hat, r, g):
            t = dn * g
            return r * (t - hat * _rowmean(t * hat)), _colsum(dn * hat)

        xv = x_ref[...]
        r1 = lax.rsqrt(_rowmean(xv * xv) + EPS)
        x_hat = xv * r1
        gpre_v = gpre_ref[...]
        hb = (x_hat * gpre_v).astype(BF16)
        ready = jnp.concatenate([dsbq_ref[...], dsbk_ref[...].astype(BF16), dsbv_ref[...].astype(BF16), dsbg_ref[...]], axis=1)
        dmlag = dmlag_ref[...]
        dwin_acc[:, 0:2048] += _mm_tn(hb, ready)
        dwin_acc[:, 2560:3072] += _mm_tn(hb, dmlag)
        dh = _mm_nt(ready, win_ref[:, 0:2048]) + _mm_nt(dmlag, win_ref[:, 2560:3072])

        dqeb = _rope_bwd(dqc_ref[...], c8, sa8, sb8).astype(BF16)
        cq = cq_ref[...]
        rq = lax.rsqrt(_rowmean(cq * cq) + EPS)
        cq_hat = cq * rq
        gq_v = gq_ref[...]
        dwuq_ref[...] += _mm_tn((cq_hat * gq_v).astype(BF16), dqeb)
        dcq, dg_q = norm_bwd(_mm_nt(dqeb, wuq_ref[...]), cq_hat, rq, gq_v)

        dkc = dkc_ref[...]
        dkcb = dkc.astype(BF16)
        dmvb = dmv_ref[...].astype(BF16)
        ckv = ckv_ref[...]
        rkv = lax.rsqrt(_rowmean(ckv * ckv) + EPS)
        ckv_hat = ckv * rkv
        gkv_v = gkv_ref[...]
        ckvnb = (ckv_hat * gkv_v).astype(BF16)
        dwk_ref[...] += _mm_tn(ckvnb, dkcb)
        dwv_ref[...] += _mm_tn(ckvnb, dmvb)
        dckv, dg_kv = norm_bwd(_mm_nt(dkcb, wk_ref[...]) + _mm_nt(dmvb, wv_ref[...]), ckv_hat, rkv, gkv_v)

        dkr = dkc[:, 0:LANES]
        for hh in range(1, 8):
            dkr = dkr + dkc[:, LANES * hh:LANES * (hh + 1)]
        dkr = _rope_bwd(dkr, c1, sa1, sb1)
        dkr = jnp.where((lane >= 64) & (lane < 96), dkr, 0.0)

        late = jnp.concatenate([dcq.astype(BF16), dckv.astype(BF16), dkr.astype(BF16)], axis=1)
        dwin_acc[:, 2048:2560] += _mm_tn(hb, late)
        dx, dg_pre = norm_bwd(dh + _mm_nt(late, win_ref[:, 2048:2560]), x_hat, r1, gpre_v)
        gx_ref[...] = dxres_ref[...] + dx
        vec_ref[pl.ds(0, 1), :] += dg_pre
        vec_ref[pl.ds(1, 1), :] += jnp.concatenate([dg_q, dg_kv, jnp.zeros((1, D_MODEL - Q_LORA - KV_LORA), F32)], axis=1)

        @pl.when(i == pl.num_programs(0) - 1)
        def _():
            pltpu.sync_copy(dwin_acc, dwin_ref)

    out_shape = (
        jax.ShapeDtypeStruct((s, D_MODEL), F32), jax.ShapeDtypeStruct((D_MODEL, D_EXT), F32),
        jax.ShapeDtypeStruct((Q_LORA, 1024), F32), jax.ShapeDtypeStruct((KV_LORA, 1024), F32),
        jax.ShapeDtypeStruct((KV_LORA, 512), F32), jax.ShapeDtypeStruct((8, D_MODEL), F32),
    )
    return pl.pallas_call(
        body, name="pre_bwd", grid=(s // TM,), out_shape=out_shape,
        in_specs=[rw(D_MODEL), rw(D_MODEL), rw(512), rw(512), rw(512), rw(512), rw(512),
                  rw(1024), rw(1024), rw(512), rw(Q_LORA), rw(KV_LORA), rw(LANES), rw(LANES),
                  rw(LANES), _full((1, D_MODEL)), _full((D_MODEL, D_EXT)), _full((1, Q_LORA)), _full((Q_LORA, 1024)),
                  _full((1, KV_LORA)), _full((KV_LORA, 1024)), _full((KV_LORA, 512))],
        out_specs=(rw(D_MODEL), pl.BlockSpec(memory_space=pl.ANY), _acc((Q_LORA, 1024)), _acc((KV_LORA, 1024)),
                   _acc((KV_LORA, 512)), _acc((8, D_MODEL))),
        scratch_shapes=[pltpu.VMEM((D_MODEL, D_EXT), F32)],
        compiler_params=pltpu.CompilerParams(vmem_limit_bytes=VMEM_DENSE),
    )(x, dxres, dsbq, dsbk, dsbv, dsbg, dmlag, dqc, dkc, dmv, cq, ckv, c_t, sa_t, sb_t, gpre, win, gq, wuq, gkv, wk, wv)


def _place():
    return lax.axis_index("x"), lax.axis_index("y"), lax.axis_index("c")


def _gather_steps(shapes, ins, bufs, send_sems, recv_sems):
    n = len(shapes)
    x, y, c = _place()
    me, sib = (x, y, c), (x, y, 1 - c)
    chips = [(1 - x, y), (x, 1 - y), (1 - x, 1 - y)]

    def half(t, chip, hc):
        rows = shapes[t][0] // 2
        return bufs[t].at[2 * chip[0] + chip[1], pl.ds(pl.multiple_of(hc * rows, 16), rows), :]

    def copy(k, t, chip, hc, to):
        return pltpu.make_async_remote_copy(src_ref=half(t, chip, hc), dst_ref=half(t, chip, hc), send_sem=send_sems.at[k],
                                            recv_sem=recv_sems.at[k], device_id=to, device_id_type=MESH)

    def start():
        for t in range(n):
            bufs[t][2 * x + y] = ins[t][...].astype(BF16)
            for j, chip in enumerate(chips):
                copy(6 * t + j, t, (x, y), c, (*chip, c)).start()

    def forward():
        for t in range(n):
            for j, chip in enumerate(chips):
                copy(6 * t + j, t, chip, c, me).wait_recv()
                copy(6 * t + 3 + j, t, chip, c, sib).start()

    def finish():
        for t in range(n):
            for j, chip in enumerate(chips):
                copy(6 * t + 3 + j, t, chip, 1 - c, me).wait_recv()
        for t in range(n):
            for j, chip in enumerate(chips):
                copy(6 * t + j, t, (x, y), c, (*chip, c)).wait_send()
                copy(6 * t + 3 + j, t, chip, c, sib).wait_send()

    return start, forward, finish


def _allgather_weights(shards):
    n = len(shards)

    def body(*refs):
        start, forward, finish = _gather_steps([a.shape for a in shards], refs[:n], refs[n:2 * n], refs[2 * n], refs[2 * n + 1])
        start()
        forward()
        finish()

    return pl.pallas_call(
        body, name="allgather_weights",
        out_shape=tuple(jax.ShapeDtypeStruct((N_SHARD,) + a.shape, BF16) for a in shards),
        in_specs=[pl.BlockSpec(memory_space=pltpu.VMEM)] * n, out_specs=(pl.BlockSpec(memory_space=pltpu.VMEM),) * n,
        scratch_shapes=[pltpu.SemaphoreType.DMA((6 * n,)), pltpu.SemaphoreType.DMA((6 * n,))],
        compiler_params=pltpu.CompilerParams(vmem_limit_bytes=VMEM_ATTN),
    )(*shards)


def _reduce_scratch(gsh):
    n = len(gsh)
    half_shapes = [(N_SHARD, a.shape[1] // 2, a.shape[2]) for a in gsh]
    return ([pltpu.VMEM(s_, F32) for s_ in half_shapes] * 2 + [pltpu.VMEM(s_, BF16) for s_ in half_shapes] * 2
            + [pltpu.SemaphoreType.DMA((n,)), pltpu.SemaphoreType.DMA((5 * n,)), pltpu.SemaphoreType.DMA((5 * n,))])


def _reduce_steps(halves, g_refs, f_refs, scratch):
    n = len(halves)
    accs, sibs, sbufs, rbufs = scratch[0:n], scratch[n:2 * n], scratch[2 * n:3 * n], scratch[3 * n:4 * n]
    local_sems, send_sems, recv_sems = scratch[4 * n:4 * n + 3]
    x, y, c = _place()
    me, sib = (x, y, c), (x, y, 1 - c)
    mine = 2 * x + y
    chips = [(1 - x, y), (x, 1 - y), (1 - x, 1 - y)]

    def remote(k, src, dst, to):
        return pltpu.make_async_remote_copy(src_ref=src, dst_ref=dst, send_sem=send_sems.at[k], recv_sem=recv_sems.at[k],
                                            device_id=to, device_id_type=MESH)

    def half3(ref, t, hc):
        return ref.at[:, pl.ds(pl.multiple_of(hc * halves[t], 8), halves[t]), :]

    def half2(ref, t, hc):
        return ref.at[pl.ds(pl.multiple_of(hc * halves[t], 8), halves[t]), :]

    def mine_load(t):
        return pltpu.make_async_copy(half3(g_refs[t], t, c), accs[t], local_sems.at[t])

    def to_sibling(t, to):
        return remote(t, half3(g_refs[t], t, 1 - c), sibs[t], to)

    def to_chip(t, j, chip, to):
        idx = 2 * chip[0] + chip[1]
        return remote(n + 3 * t + j, sbufs[t].at[idx], rbufs[t].at[mine if to is not me else idx], to)

    def swap(t, hc, to):
        return remote(4 * n + t, half2(f_refs[t], t, hc), half2(f_refs[t], t, hc), to)

    def load():
        for t in range(n):
            mine_load(t).start()
            to_sibling(t, sib).start()

    def partial():
        for t in range(n):
            mine_load(t).wait()
            to_sibling(t, me).wait_recv()
            for k in range(N_SHARD):
                accs[t][k] = accs[t][k] + sibs[t][k]
            for j, chip in enumerate(chips):
                idx = 2 * chip[0] + chip[1]
                sbufs[t][idx] = accs[t][idx].astype(BF16)
                to_chip(t, j, chip, (*chip, c)).start()

    def total():
        for t in range(n):
            acc = accs[t][mine]
            for j, chip in enumerate(chips):
                to_chip(t, j, chip, me).wait_recv()
                acc = acc + rbufs[t][2 * chip[0] + chip[1]].astype(F32)
            half2(f_refs[t], t, c)[...] = acc
            swap(t, c, sib).start()

    def finish():
        for t in range(n):
            swap(t, 1 - c, me).wait_recv()
        for t in range(n):
            to_sibling(t, sib).wait_send()
            for j, chip in enumerate(chips):
                to_chip(t, j, chip, (*chip, c)).wait_send()
            swap(t, c, sib).wait_send()

    return load, partial, total, finish


def _reduce_scatter_grads(gsh, vec):
    n = len(gsh)
    halves = [a.shape[1] // 2 for a in gsh]

    def body(*refs):
        g_refs, vec_ref, f_refs, vsum_ref = refs[:n], refs[n], refs[n + 1:2 * n + 1], refs[2 * n + 1]
        scratch = refs[2 * n + 2:]
        vrecv, vsend_sems, vrecv_sems = scratch[4 * n + 3:]
        load, partial, total, finish = _reduce_steps(halves, g_refs, f_refs, scratch)
        x, y, c = _place()
        my_dev = 4 * x + 2 * y + c

        def flip(k):
            return x ^ ((k >> 2) & 1), y ^ ((k >> 1) & 1), c ^ (k & 1)

        def vcopy(k, slot, to):
            return pltpu.make_async_remote_copy(src_ref=vec_ref, dst_ref=vrecv.at[slot], send_sem=vsend_sems.at[k - 1],
                                                recv_sem=vrecv_sems.at[k - 1], device_id=to, device_id_type=MESH)

        load()
        vrecv[my_dev] = vec_ref[...]
        for k in range(1, 8):
            vcopy(k, my_dev, flip(k)).start()
        partial()
        total()
        finish()
        for k in range(1, 8):
            fx, fy, fc = flip(k)
            vcopy(k, 4 * fx + 2 * fy + fc, (x, y, c)).wait_recv()
        vs = vrecv[0]
        for d in range(1, 8):
            vs = vs + vrecv[d]
        vsum_ref[...] = vs
        for k in range(1, 8):
            vcopy(k, my_dev, flip(k)).wait_send()

    return pl.pallas_call(
        body, name="reduce_scatter_grads",
        out_shape=tuple(jax.ShapeDtypeStruct(a.shape[1:], F32) for a in gsh) + (jax.ShapeDtypeStruct((VEC_ROWS, 1024), F32),),
        in_specs=[pl.BlockSpec(memory_space=pl.ANY)] * n + [pl.BlockSpec(memory_space=pltpu.VMEM)],
        out_specs=(pl.BlockSpec(memory_space=pltpu.VMEM),) * (n + 1),
        scratch_shapes=_reduce_scratch(gsh) + [pltpu.VMEM((8, VEC_ROWS, 1024), F32), pltpu.SemaphoreType.DMA((7,)),
                                               pltpu.SemaphoreType.DMA((7,))],
        compiler_params=pltpu.CompilerParams(vmem_limit_bytes=56 * 1024 * 1024),
    )(*gsh, vec)


def _adamw(w, g, m, v):
    rows, cols = w.shape
    tr = rows if rows <= 256 else 256
    flip = cols % LANES != 0

    def body(w_ref, g_ref, m_ref, v_ref, g_out, d_ref, nm_ref, nv_ref):
        gv = g_ref[...]
        outs = (gv,) + _adam_math(w_ref[...], gv, m_ref[...], v_ref[...])
        for ref, val in zip((g_out, d_ref, nm_ref, nv_ref), outs):
            ref[...] = val.T if flip else val

    spec = pl.BlockSpec((tr, cols), lambda i: (i, 0))
    ospec = pl.BlockSpec((cols, tr), lambda i: (0, i)) if flip else spec
    shp = jax.ShapeDtypeStruct((cols, rows) if flip else (rows, cols), F32)
    outs = pl.pallas_call(body, name="adamw", grid=(rows // tr,), out_shape=(shp,) * 4,
                          in_specs=[spec] * 4, out_specs=(ospec,) * 4)(w, g, m, v)
    return tuple(o.T for o in outs) if flip else outs


def _adam_math(w, g, m, v):
    m2 = ADAM_B1 * m + (1.0 - ADAM_B1) * g
    v2 = ADAM_B2 * v + (1.0 - ADAM_B2) * (g * g)
    m_hat = m2 / (1.0 - ADAM_B1 ** ADAM_STEP)
    v_hat = v2 / (1.0 - ADAM_B2 ** ADAM_STEP)
    return -ADAM_LR * (m_hat / (jnp.sqrt(v_hat) + ADAM_EPS) + ADAM_WD * w), m2, v2


def _adamw_small(vsum, w, m, v):
    names = [name for name, _, _, _ in _VEC_LAYOUT]
    k = len(names)

    def body(*refs):
        vs_ref, w_refs, m_refs, v_refs = refs[0], refs[1:1 + k], refs[1 + k:1 + 2 * k], refs[1 + 2 * k:1 + 3 * k]
        outs = refs[1 + 3 * k:]
        for idx, (_, r, c0, width) in enumerate(_VEC_LAYOUT):
            gv = vs_ref[pl.ds(r, 1), pl.ds(c0, width)]
            d, m2, v2 = _adam_math(w_refs[idx][...], gv, m_refs[idx][...], v_refs[idx][...])
            outs[idx][...], outs[k + idx][...], outs[2 * k + idx][...], outs[3 * k + idx][...] = gv, d, m2, v2

    shapes = tuple(jax.ShapeDtypeStruct(w[name].shape, F32) for name in names)
    res = pl.pallas_call(
        body, name="adamw_small", out_shape=shapes * 4,
        in_specs=[pl.BlockSpec(memory_space=pltpu.VMEM)] * (1 + 3 * k), out_specs=(pl.BlockSpec(memory_space=pltpu.VMEM),) * (4 * k),
    )(vsum, *[w[name] for name in names], *[m[name] for name in names], *[v[name] for name in names])
    return tuple({name: res[part * k + idx] for idx, name in enumerate(names)} for part in range(4))


_EARLY = ("w_in", "w_uq", "w_ukv")
_LATE = ("w_out", "w_ple", "w_ple_gate")
_BIG = _EARLY + _LATE
_KR_LOCAL = 2432 - 3 * (D_IN // N_SHARD)


def _extend_early(parts):
    cols = lambda a: a.transpose(1, 0, 2).reshape(a.shape[1], N_SHARD * a.shape[2])
    g = parts["w_in"]
    zeros = lambda n: jnp.zeros((D_MODEL, n), g.dtype)
    win_ext = jnp.concatenate([g[0], g[1], g[2], g[3][:, :_KR_LOCAL], zeros(64), g[3][:, _KR_LOCAL:_KR_LOCAL + QK_ROPE],
                               zeros(32), g[3][:, _KR_LOCAL + QK_ROPE:]], axis=1)
    wuq_ext = jnp.pad(cols(parts["w_uq"]).reshape(Q_LORA, 8, 96), ((0, 0), (0, 0), (0, 32))).reshape(Q_LORA, 1024)
    wukv = cols(parts["w_ukv"]).reshape(KV_LORA, 8, 128)
    wk_ext = jnp.pad(wukv[:, :, :64], ((0, 0), (0, 0), (0, 64))).reshape(KV_LORA, 1024)
    wv = wukv[:, :, 64:].reshape(KV_LORA, 512)
    return win_ext, wuq_ext, wk_ext, wv


def _shard_cols(a):
    return a.reshape(a.shape[0], N_SHARD, a.shape[1] // N_SHARD).transpose(1, 0, 2)


def _shard_rows(a):
    return a.reshape(N_SHARD, a.shape[0] // N_SHARD, a.shape[1])


def _shard_early_grads(dwin_ext, dwuq_ext, dwk_ext, dwv):
    e, w = dwin_ext, D_IN // N_SHARD
    last = jnp.concatenate([e[:, 3 * w:2432], e[:, 2496:2528], e[:, 2560:]], axis=1)
    dwuq = dwuq_ext.reshape(Q_LORA, 8, 128)[:, :, :96].reshape(Q_LORA, 768)
    dwukv = jnp.concatenate([dwk_ext.reshape(KV_LORA, 8, 128)[:, :, :64], dwv.reshape(KV_LORA, 8, 64)], axis=2)
    return [jnp.stack([e[:, 0:w], e[:, w:2 * w], e[:, 2 * w:3 * w], last]), _shard_cols(dwuq),
            _shard_cols(dwukv.reshape(KV_LORA, 1024))]


def _rope_tables(positions):
    half = QK_ROPE // 2
    freq = ROPE_THETA ** (-jnp.arange(half, dtype=F32) / half)
    ang = positions.astype(F32)[:, None] * freq
    cos, sin = jnp.cos(ang), jnp.sin(ang)
    s = positions.shape[0]
    z = lambda n: jnp.zeros((s, n), F32)
    c_t = jnp.concatenate([jnp.ones((s, 64), F32), cos, cos, z(32)], axis=1)
    sa_t = jnp.concatenate([z(64), -sin, z(16), z(32)], axis=1)
    sb_t = jnp.concatenate([z(64), z(16), sin, z(32)], axis=1)
    return c_t, sa_t, sb_t


def _local_grads(x, p, positions, tgt, gains, early, late):
    win_ext, wuq_ext, wk_ext, wv = _extend_early(early)
    tabs = _rope_tables(positions)
    g = gains
    sbq, sbk, sbv, sbg, mlag, cq, ckv, qc, kc, mv, sbkt, sbvt, kct, mvt = _pre_fwd(
        x, tabs, g["norm_pre_g"], win_ext, g["q_norm_g"], wuq_ext, g["kv_norm_g"], wk_ext, wv)
    sbo, wout4, wple4, wpg4 = _sb_fwd(sbq, sbk, sbvt, late)
    wout, wpg = wout4.reshape(D_MODEL, D_MODEL), wpg4.reshape(D_MODEL, D_MODEL)
    wple = wple4.transpose(1, 0, 2).reshape(PLE_DIM, D_MODEL)
    mlao, lse = _mla_fwd(qc, kc, mvt)
    dsbo, dmlao, delta, dsbg, dmlag, dxres, dwout, dwpg, dwple, vec_c = _post(
        x, p, tgt, sbo, mlao, sbg, mlag, g["sb_out_norm_g"], g["mla_out_norm_g"], wout, g["norm_post_g"], wple,
        g["ple_norm_g"], wpg, g["b_ple_gate"])
    dsbq, dsbk, dsbv, *late_grads = _sb_bwd(sbq, sbk, sbkt, sbv, dsbo, [_shard_rows(dwout), _shard_cols(dwple), _shard_rows(dwpg)])
    dqc, dkc, dmv = _mla_bwd(qc, kc, kct, mv, dmlao, lse, delta)
    gx, dwin_ext, dwuq_ext, dwk_ext, dwv, vec_d = _pre_bwd(
        x, dxres, dsbq, dsbk, dsbv, dsbg, dmlag, dqc, dkc, dmv, cq, ckv, tabs, g["norm_pre_g"], win_ext, g["q_norm_g"],
        wuq_ext, g["kv_norm_g"], wk_ext, wv)
    return gx, _shard_early_grads(dwin_ext, dwuq_ext, dwk_ext, dwv), late_grads, jnp.concatenate([vec_c, vec_d], axis=0)


_VEC_LAYOUT = (("norm_post_g", 0, 0, 1024), ("ple_norm_g", 1, 0, 1024), ("b_ple_gate", 2, 0, 1024), ("sb_out_norm_g", 3, 0, 512),
               ("mla_out_norm_g", 3, 512, 512), ("norm_pre_g", 8, 0, 1024), ("q_norm_g", 9, 0, 256), ("kv_norm_g", 9, 256, 128))
_LOSS_ROW = 4
_WEIGHT_ORDER = ("norm_pre_g", "w_in", "q_norm_g", "w_uq", "kv_norm_g", "w_ukv", "sb_out_norm_g", "mla_out_norm_g", "w_out",
                 "norm_post_g", "w_ple", "ple_norm_g", "w_ple_gate", "b_ple_gate")


def kernel(x, p, positions, norm_pre_g, w_in, q_norm_g, w_uq, kv_norm_g, w_ukv, sb_out_norm_g, mla_out_norm_g, w_out, norm_post_g, w_ple, ple_norm_g, w_ple_gate, b_ple_gate, loss_target, m_norm_pre_g, m_w_in, m_q_norm_g, m_w_uq, m_kv_norm_g, m_w_ukv, m_sb_out_norm_g, m_mla_out_norm_g, m_w_out, m_norm_post_g, m_w_ple, m_ple_norm_g, m_w_ple_gate, m_b_ple_gate, v_norm_pre_g, v_w_in, v_q_norm_g, v_w_uq, v_kv_norm_g, v_w_ukv, v_sb_out_norm_g, v_mla_out_norm_g, v_w_out, v_norm_post_g, v_w_ple, v_ple_norm_g, v_w_ple_gate, v_b_ple_gate):
    w = {"norm_pre_g": norm_pre_g, "w_in": w_in[0], "q_norm_g": q_norm_g, "w_uq": w_uq[0], "kv_norm_g": kv_norm_g, "w_ukv": w_ukv[0],
         "sb_out_norm_g": sb_out_norm_g, "mla_out_norm_g": mla_out_norm_g, "w_out": w_out[0], "norm_post_g": norm_post_g,
         "w_ple": w_ple[0], "ple_norm_g": ple_norm_g, "w_ple_gate": w_ple_gate[0], "b_ple_gate": b_ple_gate}
    m = {"norm_pre_g": m_norm_pre_g, "w_in": m_w_in[0], "q_norm_g": m_q_norm_g, "w_uq": m_w_uq[0], "kv_norm_g": m_kv_norm_g,
         "w_ukv": m_w_ukv[0], "sb_out_norm_g": m_sb_out_norm_g, "mla_out_norm_g": m_mla_out_norm_g, "w_out": m_w_out[0],
         "norm_post_g": m_norm_post_g, "w_ple": m_w_ple[0], "ple_norm_g": m_ple_norm_g, "w_ple_gate": m_w_ple_gate[0],
         "b_ple_gate": m_b_ple_gate}
    v = {"norm_pre_g": v_norm_pre_g, "w_in": v_w_in[0], "q_norm_g": v_q_norm_g, "w_uq": v_w_uq[0], "kv_norm_g": v_kv_norm_g,
         "w_ukv": v_w_ukv[0], "sb_out_norm_g": v_sb_out_norm_g, "mla_out_norm_g": v_mla_out_norm_g, "w_out": v_w_out[0],
         "norm_post_g": v_norm_post_g, "w_ple": v_w_ple[0], "ple_norm_g": v_ple_norm_g, "w_ple_gate": v_w_ple_gate[0],
         "b_ple_gate": v_b_ple_gate}
    gathered = _allgather_weights([w[n] for n in _EARLY])
    gx, early_grads, late_red, vec = _local_grads(x[0], p[0, 0], positions[0], loss_target[0], w, dict(zip(_EARLY, gathered)),
                                                  [w[n] for n in _LATE])
    *early_red, vsum = _reduce_scatter_grads(early_grads, vec)
    gred = early_red + late_red
    loss = vsum[_LOSS_ROW, 0]

    g, delta, new_m, new_v = _adamw_small(vsum, w, m, v)
    for n, gn in zip(_BIG, gred):
        g[n], delta[n], new_m[n], new_v[n] = _adamw(w[n], gn, m[n], v[n])

    lead = lambda n, a: a[None] if n in _BIG else a
    return (loss, gx[None],
            *[lead(n, g[n]) for n in _WEIGHT_ORDER], *[lead(n, delta[n]) for n in _WEIGHT_ORDER],
            *[lead(n, new_m[n]) for n in _WEIGHT_ORDER], *[lead(n, new_v[n]) for n in _WEIGHT_ORDER])
```

```python
import numpy as np
import jax
import jax.numpy as jnp
from jax import lax
from jax.experimental import pallas as pl
from jax.experimental.pallas import tpu as pltpu

F32 = jnp.float32
BF16 = jnp.bfloat16
MESH = pl.DeviceIdType.MESH

D_MODEL = 1024
HEAD_DIM = 64
D_SB = 512
D_MLA = 512
Q_LORA = 256
KV_LORA = 128
QK_NOPE = 64
QK_ROPE = 32
PLE_DIM = 256
D_IN = 2976
D_EXT = 3072
ROPE_THETA = 10000.0
EPS = 1e-6
N_SHARD = 4

ADAM_LR = 0.001
ADAM_B1 = 0.9
ADAM_B2 = 0.999
ADAM_EPS = 1e-08
ADAM_WD = 0.01
ADAM_STEP = 10

LANES = 128
BK = 128
WQ = 256
MQ = 1024
SB_CUTOFF = 120.0
TM = 256
TM_PRE = 256
VEC_ROWS = 16
VMEM_DENSE = 52 * 1024 * 1024
VMEM_ATTN = 40 * 1024 * 1024


def _mm(a, b):
    return jnp.dot(a, b, preferred_element_type=F32)


def _mm_nt(a, b):
    return lax.dot_general(a, b, (((1,), (1,)), ((), ())), preferred_element_type=F32)


def _mm_tn(a, b):
    return lax.dot_general(a, b, (((0,), (0,)), ((), ())), preferred_element_type=F32)


def _seg(a, bd2):
    return _mm(_split2(a), bd2)


def _const(mask):
    return jnp.asarray(np.asarray(mask, np.float32), dtype=BF16)


def _blockdiag2(n, seg):
    r = (np.arange(2 * n)[:, None] % n) // seg
    c = np.arange(n)[None, :] // seg
    return _const(r == c)


def _sigmoid(a):
    return 1.0 / (1.0 + jnp.exp(-a))


def _rowmean(a):
    return jnp.mean(a, axis=-1, keepdims=True)


def _colsum(a):
    return jnp.sum(a, axis=0, keepdims=True)


def _rope_fwd(a, c, sa, sb):
    w = a.shape[-1]
    return a * c + pltpu.roll(a, w - 16, 1) * sa + pltpu.roll(a, 16, 1) * sb


def _rope_bwd(g, c, sa, sb):
    w = g.shape[-1]
    return g * c + pltpu.roll(g * sa, 16, 1) + pltpu.roll(g * sb, w - 16, 1)


def _full(shape):
    return pl.BlockSpec(shape, lambda *_: (0,) * len(shape))


def _acc(shape):
    return pl.BlockSpec(shape, lambda *_: (0,) * len(shape))


def _full2(shape):
    return pl.BlockSpec(shape, lambda p, i: (0, 0))


def _cols(height, tm=TM):
    return pl.BlockSpec((height, tm), lambda i: (0, i))


def _rows(width, tm=TM):
    return pl.BlockSpec((tm, width), lambda i: (i, 0))


def _pre_fwd(x, tabs, gpre, win, gq, wuq, gkv, wk, wv):
    s = x.shape[0]
    c_t, sa_t, sb_t = tabs
    rw, cl = (lambda width: _rows(width, TM_PRE)), (lambda height: _cols(height, TM_PRE))

    def body(x_ref, c_ref, sa_ref, sb_ref, gpre_ref, win_ref, gq_ref, wuq_ref, gkv_ref, wk_ref, wv_ref,
             sbq_ref, sbk_ref, sbv_ref, sbg_ref, mlag_ref, cq_ref, ckv_ref, qc_ref, kc_ref, mv_ref,
             sbkt_ref, sbvt_ref, kct_ref, mvt_ref):
        xv = x_ref[...]
        r1 = lax.rsqrt(_rowmean(xv * xv) + EPS)
        h = (xv * r1 * gpre_ref[...]).astype(BF16)
        proj = _mm(h, win_ref[...])
        sbq_ref[...] = proj[:, 0:512].astype(BF16)
        sbk_ref[...] = proj[:, 512:1024].astype(BF16)
        sbv_ref[...] = proj[:, 1024:1536].astype(BF16)
        sbkt_ref[...] = proj[:, 512:1024].T.astype(BF16)
        sbvt_ref[...] = proj[:, 1024:1536].T.astype(BF16)
        sbg_ref[...] = proj[:, 1536:2048]
        cq = proj[:, 2048:2304]
        ckv = proj[:, 2304:2432]
        kr = proj[:, 2432:2560]
        mlag_ref[...] = proj[:, 2560:3072]
        cq_ref[...] = cq
        ckv_ref[...] = ckv
        c1, sa1, sb1 = c_ref[...], sa_ref[...], sb_ref[...]
        c8, sa8, sb8 = jnp.tile(c1, (1, 8)), jnp.tile(sa1, (1, 8)), jnp.tile(sb1, (1, 8))
        cqn = (cq * lax.rsqrt(_rowmean(cq * cq) + EPS) * gq_ref[...]).astype(BF16)
        qe = _mm(cqn, wuq_ref[...])
        qc_ref[...] = _rope_fwd(qe, c8, sa8, sb8).astype(BF16)
        ckvn = (ckv * lax.rsqrt(_rowmean(ckv * ckv) + EPS) * gkv_ref[...]).astype(BF16)
        ke = _mm(ckvn, wk_ref[...])
        krr = _rope_fwd(kr, c1, sa1, sb1)
        kcat = ke + jnp.tile(krr, (1, 8))
        kc_ref[...] = kcat.astype(BF16)
        kct_ref[...] = kcat.T.astype(BF16)
        mval = _mm(ckvn, wv_ref[...])
        mv_ref[...] = mval.astype(BF16)
        mvt_ref[...] = mval.T.astype(BF16)

    out_shape = (
        jax.ShapeDtypeStruct((s, 512), BF16), jax.ShapeDtypeStruct((s, 512), BF16), jax.ShapeDtypeStruct((s, 512), BF16),
        jax.ShapeDtypeStruct((s, 512), F32), jax.ShapeDtypeStruct((s, 512), F32),
        jax.ShapeDtypeStruct((s, Q_LORA), F32), jax.ShapeDtypeStruct((s, KV_LORA), F32),
        jax.ShapeDtypeStruct((s, 1024), BF16), jax.ShapeDtypeStruct((s, 1024), BF16), jax.ShapeDtypeStruct((s, 512), BF16),
        jax.ShapeDtypeStruct((512, s), BF16), jax.ShapeDtypeStruct((512, s), BF16), jax.ShapeDtypeStruct((1024, s), BF16),
        jax.ShapeDtypeStruct((512, s), BF16),
    )
    return pl.pallas_call(
        body, name="pre_fwd", grid=(s // TM_PRE,), out_shape=out_shape,
        in_specs=[rw(D_MODEL), rw(LANES), rw(LANES), rw(LANES), _full((1, D_MODEL)), _full((D_MODEL, D_EXT)),
                  _full((1, Q_LORA)), _full((Q_LORA, 1024)), _full((1, KV_LORA)), _full((KV_LORA, 1024)), _full((KV_LORA, 512))],
        out_specs=(rw(512), rw(512), rw(512), rw(512), rw(512), rw(Q_LORA), rw(KV_LORA),
                   rw(1024), rw(1024), rw(512), cl(512), cl(512), cl(1024), cl(512)),
        compiler_params=pltpu.CompilerParams(vmem_limit_bytes=VMEM_DENSE),
    )(x, c_t, sa_t, sb_t, gpre, win, gq, wuq, gkv, wk, wv)


def _softplus(z):
    neg_abs = lax.bitcast_convert_type(lax.bitcast_convert_type(z, jnp.uint32) | jnp.uint32(0x80000000), F32)
    return jnp.maximum(z, 0.0) + jnp.log(1.0 + jnp.exp(neg_abs))


def _sum_matrix(kind, terms):
    r, c = np.arange(2 * BK)[:, None], np.arange(2 * BK * terms)[None, :] % (2 * BK)
    rk, ck = r % BK, c % BK
    return _const(((r // BK) == (c // BK)) & {"suffix": ck >= rk, "prefix": ck <= rk}[kind])


def _split_rows(a):
    hi = a.astype(BF16)
    return jnp.concatenate([hi, (a - hi.astype(F32)).astype(BF16)], axis=0)


def _heads_t(blk, rowi):
    zero = jnp.zeros_like(blk)
    return jnp.concatenate([jnp.where(rowi < 64, blk, zero), jnp.where(rowi >= 64, blk, zero)], axis=1)


def _mask_keys(a, valid, fill=0.0):
    return jnp.concatenate([jnp.where(valid, a[0:BK], fill), jnp.where(valid, a[BK:2 * BK], fill)], axis=0)


def _split2(a):
    hi = a.astype(BF16)
    lo = (a - hi.astype(F32)).astype(BF16)
    return jnp.concatenate([hi, lo], axis=1)


def _pair_stack(b, lane):
    zero = jnp.zeros_like(b)
    return jnp.concatenate([jnp.where(lane < 64, b, zero), jnp.where(lane >= 64, b, zero)], axis=0)


def _sb_fwd(q, k, vt, late):
    s = q.shape[0]
    n = len(late)

    def body(q_ref, k_ref, vt_ref, usuf_ref, *rest):
        ins, o_ref, outs = rest[:n], rest[n], rest[n + 1:2 * n + 1]
        acc_scr, run_scr = rest[2 * n + 1:2 * n + 3]
        bufs, (send_sems, recv_sems, out_sems) = rest[2 * n + 3:3 * n + 3], rest[3 * n + 3:]
        p, i = pl.program_id(0), pl.program_id(1)
        gather_start, gather_forward, gather_finish = _gather_steps([a.shape for a in late], ins, bufs, send_sems, recv_sems)

        @pl.when((p == 0) & (i == 0))
        def _():
            gather_start()

        @pl.when((p == 2) & (i == 0))
        def _():
            gather_forward()

        lane = lax.broadcasted_iota(jnp.int32, (1, LANES), 1)
        rowi = lax.broadcasted_iota(jnp.int32, (LANES, 1), 0)
        keyi = lax.broadcasted_iota(jnp.int32, (BK, WQ), 0)
        qryi = lax.broadcasted_iota(jnp.int32, (BK, WQ), 1) + i * WQ
        qs = q_ref[...] * (HEAD_DIM ** -0.5)

        def group(blocks, masked, seen=None):
            seen = seen or [0] * len(blocks)
            starts = [pl.multiple_of(j * BK, BK) for j in blocks]
            valid = [(keyi[:, lo:] + j * BK) < qryi[:, lo:] if m else None for j, m, lo in zip(blocks, masked, seen)]
            zs = [_mm_nt(_pair_stack(k_ref[pl.ds(ks, BK), :], lane), qs[lo:]) for ks, lo in zip(starts, seen)]
            sps = [_softplus(z) for z in zs]
            sps = [sp if ok is None else _mask_keys(sp, ok) for sp, ok in zip(sps, valid)]
            cums = [_mm(usuf_ref[...], _split_rows(sp)) for sp in sps]
            ws = [jnp.exp(z - c) for z, c in zip(zs, cums)]
            ws = [w if ok is None else _mask_keys(w, ok) for w, ok in zip(ws, valid)]
            pvs = [_mm(_heads_t(vt_ref[:, pl.ds(ks, BK)], rowi), w.astype(BF16)) for ks, w in zip(starts, ws)]
            for pv, c, lo in zip(pvs, cums, seen):
                r0, r1 = run_scr[0:1, lo:], run_scr[1:2, lo:]
                acc_scr[:, lo:] += jnp.where(rowi < 64, jnp.exp(-r0), jnp.exp(-r1)) * pv
                run_scr[0:1, lo:] = r0 + c[0:1]
                run_scr[1:2, lo:] = r1 + c[BK:BK + 1]

        assert WQ == 2 * BK
        acc_scr[...] = jnp.zeros_like(acc_scr)
        run_scr[...] = jnp.zeros_like(run_scr)

        @pl.when(i == 0)
        def _():
            group([1, 0], [True, True], [BK, 0])

        @pl.when(i > 0)
        def _():
            group([2 * i + 1, 2 * i, 2 * i - 1, 2 * i - 2], [True, True, False, False], [BK, 0, 0, 0])

        def unfinished():
            return (jnp.min(run_scr[0:2, :]) < SB_CUTOFF).astype(jnp.int32)

        def step(c):
            group([2 * i - 1 - 2 * c[0], 2 * i - 2 - 2 * c[0]], [False, False])
            return c[0] + 1, unfinished()

        lax.while_loop(lambda c: (c[0] < i) & (c[1] > 0), step, (jnp.int32(1), unfinished()))
        o_ref[...] = acc_scr[...].T

        @pl.when((p == pl.num_programs(0) - 1) & (i == pl.num_programs(1) - 1))
        def _():
            gather_finish()
            copies = [pltpu.make_async_copy(bufs[t], outs[t], out_sems.at[t]) for t in range(n)]
            for cp in copies:
                cp.start()
            for cp in copies:
                cp.wait()

    qspec = pl.BlockSpec((WQ, LANES), lambda p, i: (i, p))
    kspec = pl.BlockSpec((s, LANES), lambda p, i: (0, p))
    tspec = pl.BlockSpec((LANES, s), lambda p, i: (p, 0))
    gathered = [jax.ShapeDtypeStruct((N_SHARD,) + a.shape, BF16) for a in late]
    return pl.pallas_call(
        body, name="sb_fwd", grid=(4, s // WQ),
        out_shape=(jax.ShapeDtypeStruct((s, 512), F32), *gathered),
        in_specs=[qspec, kspec, tspec, _full2((2 * BK, 4 * BK))] + [_full2(a.shape) for a in late],
        out_specs=(qspec,) + (pl.BlockSpec(memory_space=pl.ANY),) * n,
        scratch_shapes=[pltpu.VMEM((LANES, WQ), F32), pltpu.VMEM((8, WQ), F32)] + [pltpu.VMEM(g.shape, BF16) for g in gathered]
                       + [pltpu.SemaphoreType.DMA((6 * n,)), pltpu.SemaphoreType.DMA((6 * n,)), pltpu.SemaphoreType.DMA((n,))],
        compiler_params=pltpu.CompilerParams(vmem_limit_bytes=VMEM_ATTN),
    )(q, k, vt, _sum_matrix("suffix", 2), *late)


def _sb_bwd(q, k, kt, v, do, late):
    s = q.shape[0]
    n = len(late)
    halves = [a.shape[1] // 2 for a in late]

    def body(q_ref, k_ref, kt_ref, v_ref, do_ref, usuf_ref, upre_ref, *rest):
        g_refs, (dq_ref, dk_ref, dv_ref), outs = rest[:n], rest[n:n + 3], rest[n + 3:2 * n + 3]
        later_scr, dqt_scr, st_scr = rest[2 * n + 3:2 * n + 6]
        f_scr, reduce_scr, out_sems = rest[2 * n + 6:3 * n + 6], rest[3 * n + 6:-1], rest[-1]
        p, i = pl.program_id(0), pl.program_id(1)
        reduce_load, reduce_partial, reduce_total, reduce_finish = _reduce_steps(halves, g_refs, f_scr, reduce_scr)

        @pl.when((p == 0) & (i == 0))
        def _():
            reduce_load()

        @pl.when((p == 1) & (i == 0))
        def _():
            reduce_partial()

        @pl.when((p == 3) & (i == 0))
        def _():
            reduce_total()

        @pl.when(i == 0)
        def _():
            dk_ref[...] = jnp.zeros_like(dk_ref)
            dv_ref[...] = jnp.zeros_like(dv_ref)

        lane = lax.broadcasted_iota(jnp.int32, (1, LANES), 1)
        rowi = lax.broadcasted_iota(jnp.int32, (LANES, 1), 0)
        keyi = lax.broadcasted_iota(jnp.int32, (BK, WQ), 0)
        qryi = lax.broadcasted_iota(jnp.int32, (BK, WQ), 1) + i * WQ
        qs = q_ref[...] * (HEAD_DIM ** -0.5)
        dob = do_ref[...]
        dot = dob.astype(F32).T.astype(BF16)

        def scores(j, lo=0):
            return _mm_nt(_pair_stack(k_ref[pl.ds(pl.multiple_of(j * BK, BK), BK), :], lane), qs[lo:])

        def scan(blocks, masked, seen=None):
            seen = seen or [0] * len(blocks)
            sps = [_softplus(scores(j, lo)) for j, lo in zip(blocks, seen)]
            sps = [_mask_keys(sp, (keyi[:, lo:] + j * BK) < qryi[:, lo:]) if m else sp
                   for sp, j, m, lo in zip(sps, blocks, masked, seen)]
            for sp, j, lo in zip(sps, blocks, seen):
                run = st_scr[0:2, :]
                later_scr[j, 0:2, :] = run
                st_scr[0:2, lo:] = run[:, lo:] + jnp.concatenate([jnp.sum(sp[0:BK], axis=0, keepdims=True),
                                                                  jnp.sum(sp[BK:2 * BK], axis=0, keepdims=True)], axis=0)

        def sweep(blocks, masked, seen=None):
            seen = seen or [0] * len(blocks)
            starts = [pl.multiple_of(j * BK, BK) for j in blocks]
            valid = [(keyi[:, lo:] + j * BK) < qryi[:, lo:] if m else None for j, m, lo in zip(blocks, masked, seen)]
            zs = [scores(j, lo) for j, lo in zip(blocks, seen)]
            us = [jnp.exp(lax.bitcast_convert_type(lax.bitcast_convert_type(z, jnp.uint32) | jnp.uint32(0x80000000), F32))
                  for z in zs]
            sps = [jnp.maximum(z, 0.0) + jnp.log(1.0 + u) for z, u in zip(zs, us)]
            sps = [sp if ok is None else _mask_keys(sp, ok) for sp, ok in zip(sps, valid)]
            sigs = [jnp.where(z >= 0.0, 1.0, u) / (1.0 + u) for z, u in zip(zs, us)]
            cums = [_mm(usuf_ref[...], _split_rows(sp)) for sp in sps]
            dws = [_mm(_pair_stack(v_ref[pl.ds(ks, BK), :], lane), dot[:, lo:]) for ks, lo in zip(starts, seen)]
            wfs = []
            for z, c, j, ok, lo in zip(zs, cums, blocks, valid, seen):
                f = jnp.exp(-later_scr[j, 0:2, lo:])
                wide = (BK, WQ - lo)
                wf = jnp.exp(z - c) * jnp.concatenate([jnp.broadcast_to(f[0:1], wide), jnp.broadcast_to(f[1:2], wide)], axis=0)
                wfs.append(wf if ok is None else _mask_keys(wf, ok))
            es = [dw * wf for dw, wf in zip(dws, wfs)]
            pres = [_mm(upre_ref[...], e.astype(BF16)) for e in es]
            dzs = []
            for e, pre, sig, ok, lo in zip(es, pres, sigs, valid, seen):
                e0 = pre[0:BK] + st_scr[0:1, lo:]
                e1 = pre[BK:2 * BK] + st_scr[1:2, lo:]
                st_scr[0:1, lo:] = e0[BK - 1:BK]
                st_scr[1:2, lo:] = e1[BK - 1:BK]
                dz = e - sig * jnp.concatenate([e0, e1], axis=0)
                dzs.append((dz if ok is None else _mask_keys(dz, ok)).astype(BF16))
            whole = [b for b, lo in enumerate(seen) if lo == 0]
            dqt_scr[...] += _mm(jnp.concatenate([_heads_t(kt_ref[:, pl.ds(starts[b], BK)], rowi) for b in whole], axis=1),
                                jnp.concatenate([dzs[b] for b in whole], axis=0))
            for b, lo in enumerate(seen):
                if lo:
                    dqt_scr[:, lo:] += _mm(_heads_t(kt_ref[:, pl.ds(starts[b], BK)], rowi), dzs[b])
            for ks, dz, wf, lo in zip(starts, dzs, wfs, seen):
                rk = _mm(dz, qs[lo:])
                dk_ref[pl.ds(ks, BK), :] += jnp.where(lane < 64, rk[0:BK], rk[BK:2 * BK])
                rv = _mm(wf.astype(BF16), dob[lo:])
                dv_ref[pl.ds(ks, BK), :] += jnp.where(lane < 64, rv[0:BK], rv[BK:2 * BK])

        assert WQ == 2 * BK
        st_scr[...] = jnp.zeros_like(st_scr)

        @pl.when(i == 0)
        def _():
            scan([1, 0], [True, True], [BK, 0])

        @pl.when(i > 0)
        def _():
            scan([2 * i + 1, 2 * i, 2 * i - 1, 2 * i - 2], [True, True, False, False], [BK, 0, 0, 0])

        def unfinished():
            return (jnp.min(st_scr[0:2, :]) < SB_CUTOFF).astype(jnp.int32)

        def step(c):
            scan([2 * i - 1 - 2 * c[0], 2 * i - 2 - 2 * c[0]], [False, False])
            return c[0] + 1, unfinished()

        npairs, _ = lax.while_loop(lambda c: (c[0] < i) & (c[1] > 0), step, (jnp.minimum(i, 1), unfinished()))

        st_scr[...] = jnp.zeros_like(st_scr)
        dqt_scr[...] = jnp.zeros_like(dqt_scr)
        first = 2 * (i - npairs)

        def early(t, carry):
            sweep([first + 2 * t, first + 2 * t + 1], [False, False])
            return carry

        lax.fori_loop(0, npairs - 1, early, 0)

        @pl.when(i == 0)
        def _():
            sweep([0, 1], [True, True], [0, BK])

        @pl.when(i > 0)
        def _():
            sweep([2 * i - 2, 2 * i - 1, 2 * i, 2 * i + 1], [False, False, True, True], [0, 0, 0, BK])

        dq_ref[...] = (dqt_scr[...].T * (HEAD_DIM ** -0.5)).astype(BF16)

        @pl.when((p == pl.num_programs(0) - 1) & (i == pl.num_programs(1) - 1))
        def _():
            reduce_finish()
            copies = [pltpu.make_async_copy(f_scr[t], outs[t], out_sems.at[t]) for t in range(n)]
            for cp in copies:
                cp.start()
            for cp in copies:
                cp.wait()

    qspec = pl.BlockSpec((WQ, LANES), lambda p, i: (i, p))
    kspec = pl.BlockSpec((s, LANES), lambda p, i: (0, p))
    tspec = pl.BlockSpec((LANES, s), lambda p, i: (p, 0))
    anywhere = pl.BlockSpec(memory_space=pl.ANY)
    reduced = [jax.ShapeDtypeStruct(a.shape[1:], F32) for a in late]
    return pl.pallas_call(
        body, name="sb_bwd", grid=(4, s // WQ),
        out_shape=(jax.ShapeDtypeStruct((s, 512), BF16), jax.ShapeDtypeStruct((s, 512), F32),
                   jax.ShapeDtypeStruct((s, 512), F32), *reduced),
        in_specs=[qspec, kspec, tspec, kspec, qspec, _full2((2 * BK, 4 * BK)), _full2((2 * BK, 2 * BK))] + [anywhere] * n,
        out_specs=(qspec, kspec, kspec) + (anywhere,) * n,
        scratch_shapes=[pltpu.VMEM((s // BK, 8, WQ), F32), pltpu.VMEM((LANES, WQ), F32), pltpu.VMEM((8, WQ), F32)]
                       + [pltpu.VMEM(r.shape, F32) for r in reduced] + _reduce_scratch(late) + [pltpu.SemaphoreType.DMA((n,))],
        compiler_params=pltpu.CompilerParams(vmem_limit_bytes=VMEM_ATTN),
    )(q, k, kt, v, do, _sum_matrix("suffix", 2), _sum_matrix("prefix", 1), *late)


MLA_SCALE = (QK_NOPE + QK_ROPE) ** -0.5
LOG2E = 1.4426950408889634


def _mla_keys(kb):
    zero = jnp.zeros((BK, LANES), kb.dtype)
    return jnp.concatenate([jnp.concatenate([kb[:, 0:LANES], zero], axis=1),
                            jnp.concatenate([zero, kb[:, LANES:2 * LANES]], axis=1)], axis=0)


def _mla_fwd(qc, kc, vt):
    s = qc.shape[0]
    rows_l = 16

    def body(q_ref, k_ref, vt_ref, o_ref, l_ref, p_scr, ot_scr, st_scr):
        i = pl.program_id(1)
        keyc = lax.broadcasted_iota(jnp.int32, (BK, MQ), 0)
        qryc = (lax.broadcasted_iota(jnp.int32, (BK, MQ), 1) + i * MQ) // 64
        row = lax.broadcasted_iota(jnp.int32, (LANES, 1), 0)
        qw = q_ref[...]
        orow = lax.broadcasted_iota(jnp.int32, (rows_l, 2 * BK), 0)
        ocol = lax.broadcasted_iota(jnp.int32, (rows_l, 2 * BK), 1)
        ones = jnp.where(((orow == 0) & (ocol < BK)) | ((orow == 1) & (ocol >= BK)), 1.0, 0.0).astype(BF16)

        def scores(j, lo=0):
            ks = pl.multiple_of(j * BK, BK)
            return _mm_nt(_mla_keys(k_ref[pl.ds(ks, BK), :]), qw[lo:])

        def values_t(j):
            vtb = vt_ref[:, pl.ds(pl.multiple_of(j * BK, BK), BK)]
            zero = jnp.zeros_like(vtb)
            top = jnp.concatenate([jnp.where(row < 64, vtb, zero), jnp.where(row >= 64, vtb, zero)], axis=1)
            return jnp.concatenate([top, ones], axis=0)

        def softmax(ja, za, zb, masked, lo=0):
            c = MLA_SCALE * LOG2E
            parts = [za[0:BK] * c, za[BK:2 * BK] * c, zb[0:BK] * c, zb[BK:2 * BK] * c]
            if masked:
                va = ((keyc[:, lo:] + ja * BK) // 64) <= qryc[:, lo:]
                vb = ((keyc[:, lo:] + (ja + 1) * BK) // 64) <= qryc[:, lo:]
                parts = [jnp.where(va, parts[0], -1e30), jnp.where(va, parts[1], -1e30),
                         jnp.where(vb, parts[2], -1e30), jnp.where(vb, parts[3], -1e30)]
            m0, m1 = st_scr[0:1, lo:], st_scr[1:2, lo:]
            n0 = jnp.maximum(m0, jnp.max(jnp.maximum(parts[0], parts[2]), axis=0, keepdims=True))
            n1 = jnp.maximum(m1, jnp.max(jnp.maximum(parts[1], parts[3]), axis=0, keepdims=True))
            st_scr[2:3, lo:] = jnp.exp2(m0 - n0)
            st_scr[3:4, lo:] = jnp.exp2(m1 - n1)
            st_scr[0:1, lo:] = n0
            st_scr[1:2, lo:] = n1
            p_scr[:, lo:] = jnp.concatenate([jnp.exp2(parts[0] - n0), jnp.exp2(parts[1] - n1),
                                             jnp.exp2(parts[2] - n0), jnp.exp2(parts[3] - n1)], axis=0).astype(BF16)

        def accumulate(ja, lo=0):
            pv = _mm(jnp.concatenate([values_t(ja), values_t(ja + 1)], axis=1), p_scr[:, lo:])
            a = jnp.where(row < 64, st_scr[2:3, lo:], st_scr[3:4, lo:])
            ot_scr[0:LANES, lo:] = a * ot_scr[0:LANES, lo:] + pv[0:LANES]
            ot_scr[LANES:LANES + 8, lo:] = st_scr[2:10, lo:] * ot_scr[LANES:LANES + 8, lo:] + pv[LANES:LANES + 8]

        def step(n, masked, lo=0, prev_lo=0):
            za, zb = scores(2 * n, lo), scores(2 * n + 1, lo)
            accumulate(2 * n - 2, prev_lo)
            softmax(2 * n, za, zb, masked, lo)

        def first(masked):
            softmax(0, scores(0), scores(1), masked)

        st_scr[...] = jnp.concatenate([jnp.full((2, MQ), -1e30, F32), jnp.ones((14, MQ), F32)], axis=0)
        ot_scr[...] = jnp.zeros_like(ot_scr)

        npq = MQ // (2 * BK)
        seen = lambda d: 2 * BK * max(d, 0)

        @pl.when(i == 0)
        def _():
            first(True)
            for d in range(1, npq):
                step(d, True, seen(d), seen(d - 1))

        @pl.when(i > 0)
        def _():
            first(False)
            lax.fori_loop(1, npq * i, lambda n, c: (step(n, False), c)[1], 0)
            for d in range(npq):
                step(npq * i + d, True, seen(d), seen(d - 1))

        accumulate(2 * (npq * (i + 1) - 1), seen(npq - 1))
        l0, l1 = ot_scr[LANES:LANES + 1, :], ot_scr[LANES + 1:LANES + 2, :]
        o_ref[...] = (ot_scr[0:LANES, :] / jnp.where(row < 64, l0, l1)).T
        l_ref[...] = jnp.where(row < 64, st_scr[0:1, :] + jnp.log2(l0), st_scr[1:2, :] + jnp.log2(l1)).T

    qspec = pl.BlockSpec((MQ, 2 * LANES), lambda p, i: (i, p))
    kspec = pl.BlockSpec((s, 2 * LANES), lambda p, i: (0, p))
    vtspec = pl.BlockSpec((LANES, s), lambda p, i: (p, 0))
    ospec = pl.BlockSpec((MQ, LANES), lambda p, i: (i, p))
    return pl.pallas_call(
        body, name="mla_fwd", grid=(4, s // MQ),
        out_shape=(jax.ShapeDtypeStruct((s, 512), F32), jax.ShapeDtypeStruct((s, 512), F32)),
        in_specs=[qspec, kspec, vtspec], out_specs=(ospec, ospec),
        scratch_shapes=[pltpu.VMEM((4 * BK, MQ), BF16), pltpu.VMEM((LANES + 8, MQ), F32), pltpu.VMEM((16, MQ), F32)],
        compiler_params=pltpu.CompilerParams(vmem_limit_bytes=VMEM_ATTN),
    )(qc, kc, vt)


def _mla_bwd(qc, kc, kct, v, do, lse, delta):
    s = qc.shape[0]

    def body(q_ref, k_ref, kt_ref, v_ref, do_ref, l_ref, d_ref, dq_ref, dk_ref, dv_ref, dqt_scr, p_scr, dz_scr):
        i = pl.program_id(1)

        @pl.when(i == 0)
        def _():
            dk_ref[...] = jnp.zeros_like(dk_ref)
            dv_ref[...] = jnp.zeros_like(dv_ref)

        lane = lax.broadcasted_iota(jnp.int32, (1, LANES), 1)
        keyc = lax.broadcasted_iota(jnp.int32, (BK, MQ), 0)
        qryc = (lax.broadcasted_iota(jnp.int32, (BK, MQ), 1) + i * MQ) // 64
        qw = q_ref[...]
        dob = do_ref[...]
        dost = (dob.astype(F32) * MLA_SCALE).T.astype(BF16)
        lt = l_ref[...].T
        dt = (d_ref[...] * MLA_SCALE).T
        lse0, lse1 = lt[0:1], lt[64:65]
        dl0, dl1 = dt[0:1], dt[64:65]
        dqt_scr[...] = jnp.zeros_like(dqt_scr)

        def products(j, lo=0):
            ks = pl.multiple_of(j * BK, BK)
            return (_mm_nt(_mla_keys(k_ref[pl.ds(ks, BK), :]), qw[lo:]),
                    _mm(_pair_stack(v_ref[pl.ds(ks, BK), :], lane), dost[:, lo:]))

        def grads(j, slot, zt, dwt, masked, lo=0):
            zt = zt * (MLA_SCALE * LOG2E)
            p0 = jnp.exp2(zt[0:BK] - lse0[:, lo:])
            p1 = jnp.exp2(zt[BK:2 * BK] - lse1[:, lo:])
            if masked:
                valid = ((keyc[:, lo:] + j * BK) // 64) <= qryc[:, lo:]
                p0, p1 = jnp.where(valid, p0, 0.0), jnp.where(valid, p1, 0.0)
            p_scr[slot, :, lo:] = jnp.concatenate([p0, p1], axis=0).astype(BF16)
            dz_scr[slot, :, lo:] = jnp.concatenate([p0 * (dwt[0:BK] - dl0[:, lo:]), p1 * (dwt[BK:2 * BK] - dl1[:, lo:])],
                                                   axis=0).astype(BF16)

        def keys_t(ks):
            ktb = kt_ref[:, pl.ds(ks, BK)]
            zero = jnp.zeros((LANES, BK), ktb.dtype)
            return jnp.concatenate([jnp.concatenate([ktb[0:LANES], zero], axis=1),
                                    jnp.concatenate([zero, ktb[LANES:2 * LANES]], axis=1)], axis=0)

        def scatter(ja, lo=0):
            ksa, ksb = pl.multiple_of(ja * BK, BK), pl.multiple_of((ja + 1) * BK, BK)
            dqt_scr[:, lo:] += _mm(jnp.concatenate([keys_t(ksa), keys_t(ksb)], axis=1),
                                   jnp.concatenate([dz_scr[0, :, lo:], dz_scr[1, :, lo:]], axis=0))
            for slot, ks in ((0, ksa), (1, ksb)):
                rk = _mm(dz_scr[slot, :, lo:], qw[lo:])
                dk_ref[pl.ds(ks, BK), :] += jnp.concatenate([rk[0:BK, 0:LANES], rk[BK:2 * BK, LANES:2 * LANES]], axis=1)
                rv = _mm(p_scr[slot, :, lo:], dob[lo:])
                dv_ref[pl.ds(ks, BK), :] += jnp.where(lane < 64, rv[0:BK], rv[BK:2 * BK])

        def step(n, masked, lo=0, prev_lo=0):
            za, wa = products(2 * n, lo)
            zb, wb = products(2 * n + 1, lo)
            scatter(2 * n - 2, prev_lo)
            grads(2 * n, 0, za, wa, masked, lo)
            grads(2 * n + 1, 1, zb, wb, masked, lo)

        def first(masked):
            za, wa = products(0)
            zb, wb = products(1)
            grads(0, 0, za, wa, masked)
            grads(1, 1, zb, wb, masked)

        npq = MQ // (2 * BK)
        seen = lambda d: 2 * BK * max(d, 0)

        @pl.when(i == 0)
        def _():
            first(True)
            for d in range(1, npq):
                step(d, True, seen(d), seen(d - 1))

        @pl.when(i > 0)
        def _():
            first(False)
            lax.fori_loop(1, npq * i, lambda n, c: (step(n, False), c)[1], 0)
            for d in range(npq):
                step(npq * i + d, True, seen(d), seen(d - 1))

        scatter(2 * (npq * (i + 1) - 1), seen(npq - 1))
        dq_ref[...] = dqt_scr[...].T

    qspec = pl.BlockSpec((MQ, 2 * LANES), lambda p, i: (i, p))
    kspec = pl.BlockSpec((s, 2 * LANES), lambda p, i: (0, p))
    ktspec = pl.BlockSpec((2 * LANES, s), lambda p, i: (p, 0))
    vspec = pl.BlockSpec((s, LANES), lambda p, i: (0, p))
    ospec = pl.BlockSpec((MQ, LANES), lambda p, i: (i, p))
    return pl.pallas_call(
        body, name="mla_bwd", grid=(4, s // MQ),
        out_shape=(jax.ShapeDtypeStruct((s, 1024), F32), jax.ShapeDtypeStruct((s, 1024), F32),
                   jax.ShapeDtypeStruct((s, 512), F32)),
        in_specs=[qspec, kspec, ktspec, vspec, ospec, ospec, ospec], out_specs=(qspec, kspec, vspec),
        scratch_shapes=[pltpu.VMEM((2 * LANES, MQ), F32), pltpu.VMEM((2, 2 * BK, MQ), BF16), pltpu.VMEM((2, 2 * BK, MQ), BF16)],
        compiler_params=pltpu.CompilerParams(vmem_limit_bytes=VMEM_ATTN),
    )(qc, kc, kct, v, do, lse, delta)


def _post(x, p, tgt, sbo, mlao, sbg, mlag, gsb, gmla, wout, gpost, wple, gple, wpg, bpg):
    s = x.shape[0]

    def body(x_ref, p_ref, t_ref, sbo_ref, mlao_ref, sbg_ref, mlag_ref, gsb_ref, gmla_ref, wout_ref,
             gpost_ref, wple_ref, gple_ref, wpg_ref, bpg_ref, bd_ref,
             dsbo_ref, dmlao_ref, delta_ref, dsbg_ref, dmlag_ref, dxres_ref, dwout_ref, dwpg_ref, dwple_ref, vec_ref):
        i = pl.program_id(0)

        @pl.when(i == 0)
        def _():
            dwout_ref[...] = jnp.zeros_like(dwout_ref)
            dwpg_ref[...] = jnp.zeros_like(dwpg_ref)
            dwple_ref[...] = jnp.zeros_like(dwple_ref)
            vec_ref[...] = jnp.zeros_like(vec_ref)

        inv_hd = 1.0 / HEAD_DIM

        def head_fwd(o, g, gate):
            r = lax.rsqrt(_seg(o * o, bd_ref[...]) * inv_hd + EPS)
            hat = o * r
            n = hat * g
            sg = _sigmoid(gate)
            return hat, r, n, sg, n * (gate * sg)

        sbo, mlao, sbg_v, mlag_v = sbo_ref[...], mlao_ref[...], sbg_ref[...], mlag_ref[...]
        gsb_v, gmla_v = gsb_ref[...], gmla_ref[...]
        sb_hat, sb_r, sb_n, sb_sg, sb_y = head_fwd(sbo, gsb_v, sbg_v)
        ml_hat, ml_r, ml_n, ml_sg, ml_y = head_fwd(mlao, gmla_v, mlag_v)
        mix = jnp.concatenate([sb_y, ml_y], axis=1).astype(BF16)
        y = _mm(mix, wout_ref[...])
        ry = lax.rsqrt(_rowmean(y * y) + EPS)
        y_hat = y * ry
        gpost_v = gpost_ref[...]
        x1 = x_ref[...] + y_hat * gpost_v
        pb = p_ref[...].astype(BF16)
        pl_ = _mm(pb, wple_ref[...])
        rp = lax.rsqrt(_rowmean(pl_ * pl_) + EPS)
        pl_hat = pl_ * rp
        gple_v = gple_ref[...]
        ple = pl_hat * gple_v
        x1b = x1.astype(BF16)
        gate = _sigmoid(_mm(x1b, wpg_ref[...]) + bpg_ref[...])
        err = x1 + ple * gate - t_ref[...]
        loss = 0.5 * jnp.sum(_rowmean(err * err))
        dout = err * (1.0 / D_MODEL)

        du = dout * ple * gate * (1.0 - gate)
        dub = du.astype(BF16)
        dple = dout * gate
        dx1 = dout + _mm_nt(dub, wpg_ref[...])
        dwpg_ref[...] += _mm_tn(x1b, dub)
        dplh = dple * gple_v
        dpl = rp * (dplh - pl_hat * _rowmean(dplh * pl_hat))
        dwple_ref[...] += _mm_tn(pb, dpl.astype(BF16))
        dxres_ref[...] = dx1
        dyh = dx1 * gpost_v
        dy = ry * (dyh - y_hat * _rowmean(dyh * y_hat))
        dyb = dy.astype(BF16)
        dwout_ref[...] += _mm_tn(mix, dyb)
        dmix = _mm_nt(dyb, wout_ref[...])

        def head_bwd(dyv, hat, r, n, sg, g, gate):
            dn = dyv * (gate * sg)
            dgate = dyv * n * (sg * (1.0 + gate * (1.0 - sg)))
            dhat = dn * g
            do = r * (dhat - hat * (_seg(dhat * hat, bd_ref[...]) * inv_hd))
            return do, dgate, _colsum(dn * hat)

        dsbo, dsbg, dg_sb = head_bwd(dmix[:, 0:512], sb_hat, sb_r, sb_n, sb_sg, gsb_v, sbg_v)
        dmlao, dmlag, dg_ml = head_bwd(dmix[:, 512:1024], ml_hat, ml_r, ml_n, ml_sg, gmla_v, mlag_v)
        dsbo_ref[...] = dsbo.astype(BF16)
        dmlao_ref[...] = dmlao.astype(BF16)
        delta_ref[...] = _seg(dmlao * mlao, bd_ref[...])
        dsbg_ref[...] = dsbg.astype(BF16)
        dmlag_ref[...] = dmlag.astype(BF16)
        vec_ref[pl.ds(0, 1), :] += _colsum(dx1 * y_hat)
        vec_ref[pl.ds(1, 1), :] += _colsum(dple * pl_hat)
        vec_ref[pl.ds(2, 1), :] += _colsum(du)
        vec_ref[pl.ds(3, 1), :] += jnp.concatenate([dg_sb, dg_ml], axis=1)
        vec_ref[pl.ds(4, 1), :] += jnp.full((1, D_MODEL), loss, F32)

    out_shape = (
        jax.ShapeDtypeStruct((s, 512), BF16), jax.ShapeDtypeStruct((s, 512), BF16), jax.ShapeDtypeStruct((s, 512), F32),
        jax.ShapeDtypeStruct((s, 512), BF16), jax.ShapeDtypeStruct((s, 512), BF16), jax.ShapeDtypeStruct((s, D_MODEL), F32),
        jax.ShapeDtypeStruct((D_MODEL, D_MODEL), F32), jax.ShapeDtypeStruct((D_MODEL, D_MODEL), F32),
        jax.ShapeDtypeStruct((PLE_DIM, D_MODEL), F32), jax.ShapeDtypeStruct((8, D_MODEL), F32),
    )
    return pl.pallas_call(
        body, name="post_fwd_bwd", grid=(s // TM,), out_shape=out_shape,
        in_specs=[_rows(D_MODEL), _rows(PLE_DIM), _rows(D_MODEL), _rows(512), _rows(512), _rows(512), _rows(512),
                  _full((1, 512)), _full((1, 512)), _full((D_MODEL, D_MODEL)),
                  _full((1, D_MODEL)), _full((PLE_DIM, D_MODEL)), _full((1, D_MODEL)), _full((D_MODEL, D_MODEL)),
                  _full((1, D_MODEL)), _full((1024, 512))],
        out_specs=(_rows(512), _rows(512), _rows(512), _rows(512), _rows(512), _rows(D_MODEL),
                   _acc((D_MODEL, D_MODEL)), _acc((D_MODEL, D_MODEL)), _acc((PLE_DIM, D_MODEL)), _acc((8, D_MODEL))),
        compiler_params=pltpu.CompilerParams(vmem_limit_bytes=VMEM_DENSE),
    )(x, p, tgt, sbo, mlao, sbg, mlag, gsb, gmla, wout, gpost, wple, gple, wpg, bpg, _blockdiag2(512, HEAD_DIM))


def _pre_bwd(x, dxres, dsbq, dsbk, dsbv, dsbg, dmlag, dqc, dkc, dmv, cq, ckv, tabs, gpre, win, gq, wuq, gkv, wk, wv):
    s = x.shape[0]
    c_t, sa_t, sb_t = tabs
    rw = _rows

    def body(x_ref, dxres_ref, dsbq_ref, dsbk_ref, dsbv_ref, dsbg_ref, dmlag_ref, dqc_ref, dkc_ref, dmv_ref, cq_ref,
             ckv_ref, c_ref, sa_ref, sb_ref, gpre_ref, win_ref, gq_ref, wuq_ref, gkv_ref, wk_ref, wv_ref,
             gx_ref, dwin_ref, dwuq_ref, dwk_ref, dwv_ref, vec_ref, dwin_acc):
        i = pl.program_id(0)

        @pl.when(i == 0)
        def _():
            dwin_acc[...] = jnp.zeros_like(dwin_acc)
            dwuq_ref[...] = jnp.zeros_like(dwuq_ref)
            dwk_ref[...] = jnp.zeros_like(dwk_ref)
            dwv_ref[...] = jnp.zeros_like(dwv_ref)
            vec_ref[...] = jnp.zeros_like(vec_ref)

        lane = lax.broadcasted_iota(jnp.int32, (1, LANES), 1)
        c1, sa1, sb1 = c_ref[...], sa_ref[...], sb_ref[...]
        c8, sa8, sb8 = jnp.tile(c1, (1, 8)), jnp.tile(sa1, (1, 8)), jnp.tile(sb1, (1, 8))

        def norm_bwd(dn, hat, r, g):
            t = dn * g
            return r * (t - hat * _rowmean(t * hat)), _colsum(dn * hat)

        xv = x_ref[...]
        r1 = lax.rsqrt(_rowmean(xv * xv) + EPS)
        x_hat = xv * r1
        gpre_v = gpre_ref[...]
        hb = (x_hat * gpre_v).astype(BF16)
        ready = jnp.concatenate([dsbq_ref[...], dsbk_ref[...].astype(BF16), dsbv_ref[...].astype(BF16), dsbg_ref[...]], axis=1)
        dmlag = dmlag_ref[...]
        dwin_acc[:, 0:2048] += _mm_tn(hb, ready)
        dwin_acc[:, 2560:3072] += _mm_tn(hb, dmlag)
        dh = _mm_nt(ready, win_ref[:, 0:2048]) + _mm_nt(dmlag, win_ref[:, 2560:3072])

        dqeb = _rope_bwd(dqc_ref[...], c8, sa8, sb8).astype(BF16)
        cq = cq_ref[...]
        rq = lax.rsqrt(_rowmean(cq * cq) + EPS)
        cq_hat = cq * rq
        gq_v = gq_ref[...]
        dwuq_ref[...] += _mm_tn((cq_hat * gq_v).astype(BF16), dqeb)
        dcq, dg_q = norm_bwd(_mm_nt(dqeb, wuq_ref[...]), cq_hat, rq, gq_v)

        dkc = dkc_ref[...]
        dkcb = dkc.astype(BF16)
        dmvb = dmv_ref[...].astype(BF16)
        ckv = ckv_ref[...]
        rkv = lax.rsqrt(_rowmean(ckv * ckv) + EPS)
        ckv_hat = ckv * rkv
        gkv_v = gkv_ref[...]
        ckvnb = (ckv_hat * gkv_v).astype(BF16)
        dwk_ref[...] += _mm_tn(ckvnb, dkcb)
        dwv_ref[...] += _mm_tn(ckvnb, dmvb)
        dckv, dg_kv = norm_bwd(_mm_nt(dkcb, wk_ref[...]) + _mm_nt(dmvb, wv_ref[...]), ckv_hat, rkv, gkv_v)

        dkr = dkc[:, 0:LANES]
        for hh in range(1, 8):
            dkr = dkr + dkc[:, LANES * hh:LANES * (hh + 1)]
        dkr = _rope_bwd(dkr, c1, sa1, sb1)
        dkr = jnp.where((lane >= 64) & (lane < 96), dkr, 0.0)

        late = jnp.concatenate([dcq.astype(BF16), dckv.astype(BF16), dkr.astype(BF16)], axis=1)
        dwin_acc[:, 2048:2560] += _mm_tn(hb, late)
        dx, dg_pre = norm_bwd(dh + _mm_nt(late, win_ref[:, 2048:2560]), x_hat, r1, gpre_v)
        gx_ref[...] = dxres_ref[...] + dx
        vec_ref[pl.ds(0, 1), :] += dg_pre
        vec_ref[pl.ds(1, 1), :] += jnp.concatenate([dg_q, dg_kv, jnp.zeros((1, D_MODEL - Q_LORA - KV_LORA), F32)], axis=1)

        @pl.when(i == pl.num_programs(0) - 1)
        def _():
            pltpu.sync_copy(dwin_acc, dwin_ref)

    out_shape = (
        jax.ShapeDtypeStruct((s, D_MODEL), F32), jax.ShapeDtypeStruct((D_MODEL, D_EXT), F32),
        jax.ShapeDtypeStruct((Q_LORA, 1024), F32), jax.ShapeDtypeStruct((KV_LORA, 1024), F32),
        jax.ShapeDtypeStruct((KV_LORA, 512), F32), jax.ShapeDtypeStruct((8, D_MODEL), F32),
    )
    return pl.pallas_call(
        body, name="pre_bwd", grid=(s // TM,), out_shape=out_shape,
        in_specs=[rw(D_MODEL), rw(D_MODEL), rw(512), rw(512), rw(512), rw(512), rw(512),
                  rw(1024), rw(1024), rw(512), rw(Q_LORA), rw(KV_LORA), rw(LANES), rw(LANES),
                  rw(LANES), _full((1, D_MODEL)), _full((D_MODEL, D_EXT)), _full((1, Q_LORA)), _full((Q_LORA, 1024)),
                  _full((1, KV_LORA)), _full((KV_LORA, 1024)), _full((KV_LORA, 512))],
        out_specs=(rw(D_MODEL), pl.BlockSpec(memory_space=pl.ANY), _acc((Q_LORA, 1024)), _acc((KV_LORA, 1024)),
                   _acc((KV_LORA, 512)), _acc((8, D_MODEL))),
        scratch_shapes=[pltpu.VMEM((D_MODEL, D_EXT), F32)],
        compiler_params=pltpu.CompilerParams(vmem_limit_bytes=VMEM_DENSE),
    )(x, dxres, dsbq, dsbk, dsbv, dsbg, dmlag, dqc, dkc, dmv, cq, ckv, c_t, sa_t, sb_t, gpre, win, gq, wuq, gkv, wk, wv)


def _place():
    return lax.axis_index("x"), lax.axis_index("y"), lax.axis_index("c")


def _gather_steps(shapes, ins, bufs, send_sems, recv_sems):
    n = len(shapes)
    x, y, c = _place()
    me, sib = (x, y, c), (x, y, 1 - c)
    chips = [(1 - x, y), (x, 1 - y), (1 - x, 1 - y)]

    def half(t, chip, hc):
        rows = shapes[t][0] // 2
        return bufs[t].at[2 * chip[0] + chip[1], pl.ds(pl.multiple_of(hc * rows, 16), rows), :]

    def copy(k, t, chip, hc, to):
        return pltpu.make_async_remote_copy(src_ref=half(t, chip, hc), dst_ref=half(t, chip, hc), send_sem=send_sems.at[k],
                                            recv_sem=recv_sems.at[k], device_id=to, device_id_type=MESH)

    def start():
        for t in range(n):
            bufs[t][2 * x + y] = ins[t][...].astype(BF16)
            for j, chip in enumerate(chips):
                copy(6 * t + j, t, (x, y), c, (*chip, c)).start()

    def forward():
        for t in range(n):
            for j, chip in enumerate(chips):
                copy(6 * t + j, t, chip, c, me).wait_recv()
                copy(6 * t + 3 + j, t, chip, c, sib).start()

    def finish():
        for t in range(n):
            for j, chip in enumerate(chips):
                copy(6 * t + 3 + j, t, chip, 1 - c, me).wait_recv()
        for t in range(n):
            for j, chip in enumerate(chips):
                copy(6 * t + j, t, (x, y), c, (*chip, c)).wait_send()
                copy(6 * t + 3 + j, t, chip, c, sib).wait_send()

    return start, forward, finish


def _allgather_weights(shards):
    n = len(shards)

    def body(*refs):
        start, forward, finish = _gather_steps([a.shape for a in shards], refs[:n], refs[n:2 * n], refs[2 * n], refs[2 * n + 1])
        start()
        forward()
        finish()

    return pl.pallas_call(
        body, name="allgather_weights",
        out_shape=tuple(jax.ShapeDtypeStruct((N_SHARD,) + a.shape, BF16) for a in shards),
        in_specs=[pl.BlockSpec(memory_space=pltpu.VMEM)] * n, out_specs=(pl.BlockSpec(memory_space=pltpu.VMEM),) * n,
        scratch_shapes=[pltpu.SemaphoreType.DMA((6 * n,)), pltpu.SemaphoreType.DMA((6 * n,))],
        compiler_params=pltpu.CompilerParams(vmem_limit_bytes=VMEM_ATTN),
    )(*shards)


def _reduce_scratch(gsh):
    n = len(gsh)
    half_shapes = [(N_SHARD, a.shape[1] // 2, a.shape[2]) for a in gsh]
    return ([pltpu.VMEM(s_, F32) for s_ in half_shapes] * 2 + [pltpu.VMEM(s_, BF16) for s_ in half_shapes] * 2
            + [pltpu.SemaphoreType.DMA((n,)), pltpu.SemaphoreType.DMA((5 * n,)), pltpu.SemaphoreType.DMA((5 * n,))])


def _reduce_steps(halves, g_refs, f_refs, scratch):
    n = len(halves)
    accs, sibs, sbufs, rbufs = scratch[0:n], scratch[n:2 * n], scratch[2 * n:3 * n], scratch[3 * n:4 * n]
    local_sems, send_sems, recv_sems = scratch[4 * n:4 * n + 3]
    x, y, c = _place()
    me, sib = (x, y, c), (x, y, 1 - c)
    mine = 2 * x + y
    chips = [(1 - x, y), (x, 1 - y), (1 - x, 1 - y)]

    def remote(k, src, dst, to):
        return pltpu.make_async_remote_copy(src_ref=src, dst_ref=dst, send_sem=send_sems.at[k], recv_sem=recv_sems.at[k],
                                            device_id=to, device_id_type=MESH)

    def half3(ref, t, hc):
        return ref.at[:, pl.ds(pl.multiple_of(hc * halves[t], 8), halves[t]), :]

    def half2(ref, t, hc):
        return ref.at[pl.ds(pl.multiple_of(hc * halves[t], 8), halves[t]), :]

    def mine_load(t):
        return pltpu.make_async_copy(half3(g_refs[t], t, c), accs[t], local_sems.at[t])

    def to_sibling(t, to):
        return remote(t, half3(g_refs[t], t, 1 - c), sibs[t], to)

    def to_chip(t, j, chip, to):
        idx = 2 * chip[0] + chip[1]
        return remote(n + 3 * t + j, sbufs[t].at[idx], rbufs[t].at[mine if to is not me else idx], to)

    def swap(t, hc, to):
        return remote(4 * n + t, half2(f_refs[t], t, hc), half2(f_refs[t], t, hc), to)

    def load():
        for t in range(n):
            mine_load(t).start()
            to_sibling(t, sib).start()

    def partial():
        for t in range(n):
            mine_load(t).wait()
            to_sibling(t, me).wait_recv()
            for k in range(N_SHARD):
                accs[t][k] = accs[t][k] + sibs[t][k]
            for j, chip in enumerate(chips):
                idx = 2 * chip[0] + chip[1]
                sbufs[t][idx] = accs[t][idx].astype(BF16)
                to_chip(t, j, chip, (*chip, c)).start()

    def total():
        for t in range(n):
            acc = accs[t][mine]
            for j, chip in enumerate(chips):
                to_chip(t, j, chip, me).wait_recv()
                acc = acc + rbufs[t][2 * chip[0] + chip[1]].astype(F32)
            half2(f_refs[t], t, c)[...] = acc
            swap(t, c, sib).start()

    def finish():
        for t in range(n):
            swap(t, 1 - c, me).wait_recv()
        for t in range(n):
            to_sibling(t, sib).wait_send()
            for j, chip in enumerate(chips):
                to_chip(t, j, chip, (*chip, c)).wait_send()
            swap(t, c, sib).wait_send()

    return load, partial, total, finish


def _reduce_scatter_grads(gsh, vec):
    n = len(gsh)
    halves = [a.shape[1] // 2 for a in gsh]

    def body(*refs):
        g_refs, vec_ref, f_refs, vsum_ref = refs[:n], refs[n], refs[n + 1:2 * n + 1], refs[2 * n + 1]
        scratch = refs[2 * n + 2:]
        vrecv, vsend_sems, vrecv_sems = scratch[4 * n + 3:]
        load, partial, total, finish = _reduce_steps(halves, g_refs, f_refs, scratch)
        x, y, c = _place()
        my_dev = 4 * x + 2 * y + c

        def flip(k):
            return x ^ ((k >> 2) & 1), y ^ ((k >> 1) & 1), c ^ (k & 1)

        def vcopy(k, slot, to):
            return pltpu.make_async_remote_copy(src_ref=vec_ref, dst_ref=vrecv.at[slot], send_sem=vsend_sems.at[k - 1],
                                                recv_sem=vrecv_sems.at[k - 1], device_id=to, device_id_type=MESH)

        load()
        vrecv[my_dev] = vec_ref[...]
        for k in range(1, 8):
            vcopy(k, my_dev, flip(k)).start()
        partial()
        total()
        finish()
        for k in range(1, 8):
            fx, fy, fc = flip(k)
            vcopy(k, 4 * fx + 2 * fy + fc, (x, y, c)).wait_recv()
        vs = vrecv[0]
        for d in range(1, 8):
            vs = vs + vrecv[d]
        vsum_ref[...] = vs
        for k in range(1, 8):
            vcopy(k, my_dev, flip(k)).wait_send()

    return pl.pallas_call(
        body, name="reduce_scatter_grads",
        out_shape=tuple(jax.ShapeDtypeStruct(a.shape[1:], F32) for a in gsh) + (jax.ShapeDtypeStruct((VEC_ROWS, 1024), F32),),
        in_specs=[pl.BlockSpec(memory_space=pl.ANY)] * n + [pl.BlockSpec(memory_space=pltpu.VMEM)],
        out_specs=(pl.BlockSpec(memory_space=pltpu.VMEM),) * (n + 1),
        scratch_shapes=_reduce_scratch(gsh) + [pltpu.VMEM((8, VEC_ROWS, 1024), F32), pltpu.SemaphoreType.DMA((7,)),
                                               pltpu.SemaphoreType.DMA((7,))],
        compiler_params=pltpu.CompilerParams(vmem_limit_bytes=56 * 1024 * 1024),
    )(*gsh, vec)


def _adamw(w, g, m, v):
    rows, cols = w.shape
    tr = rows if rows <= 256 else 256
    flip = cols % LANES != 0

    def body(w_ref, g_ref, m_ref, v_ref, g_out, d_ref, nm_ref, nv_ref):
        gv = g_ref[...]
        outs = (gv,) + _adam_math(w_ref[...], gv, m_ref[...], v_ref[...])
        for ref, val in zip((g_out, d_ref, nm_ref, nv_ref), outs):
            ref[...] = val.T if flip else val

    spec = pl.BlockSpec((tr, cols), lambda i: (i, 0))
    ospec = pl.BlockSpec((cols, tr), lambda i: (0, i)) if flip else spec
    shp = jax.ShapeDtypeStruct((cols, rows) if flip else (rows, cols), F32)
    outs = pl.pallas_call(body, name="adamw", grid=(rows // tr,), out_shape=(shp,) * 4,
                          in_specs=[spec] * 4, out_specs=(ospec,) * 4)(w, g, m, v)
    return tuple(o.T for o in outs) if flip else outs


def _adam_math(w, g, m, v):
    m2 = ADAM_B1 * m + (1.0 - ADAM_B1) * g
    v2 = ADAM_B2 * v + (1.0 - ADAM_B2) * (g * g)
    m_hat = m2 / (1.0 - ADAM_B1 ** ADAM_STEP)
    v_hat = v2 / (1.0 - ADAM_B2 ** ADAM_STEP)
    return -ADAM_LR * (m_hat / (jnp.sqrt(v_hat) + ADAM_EPS) + ADAM_WD * w), m2, v2


def _adamw_small(vsum, w, m, v):
    names = [name for name, _, _, _ in _VEC_LAYOUT]
    k = len(names)

    def body(*refs):
        vs_ref, w_refs, m_refs, v_refs = refs[0], refs[1:1 + k], refs[1 + k:1 + 2 * k], refs[1 + 2 * k:1 + 3 * k]
        outs = refs[1 + 3 * k:]
        for idx, (_, r, c0, width) in enumerate(_VEC_LAYOUT):
            gv = vs_ref[pl.ds(r, 1), pl.ds(c0, width)]
            d, m2, v2 = _adam_math(w_refs[idx][...], gv, m_refs[idx][...], v_refs[idx][...])
            outs[idx][...], outs[k + idx][...], outs[2 * k + idx][...], outs[3 * k + idx][...] = gv, d, m2, v2

    shapes = tuple(jax.ShapeDtypeStruct(w[name].shape, F32) for name in names)
    res = pl.pallas_call(
        body, name="adamw_small", out_shape=shapes * 4,
        in_specs=[pl.BlockSpec(memory_space=pltpu.VMEM)] * (1 + 3 * k), out_specs=(pl.BlockSpec(memory_space=pltpu.VMEM),) * (4 * k),
    )(vsum, *[w[name] for name in names], *[m[name] for name in names], *[v[name] for name in names])
    return tuple({name: res[part * k + idx] for idx, name in enumerate(names)} for part in range(4))


_EARLY = ("w_in", "w_uq", "w_ukv")
_LATE = ("w_out", "w_ple", "w_ple_gate")
_BIG = _EARLY + _LATE
_KR_LOCAL = 2432 - 3 * (D_IN // N_SHARD)


def _extend_early(parts):
    cols = lambda a: a.transpose(1, 0, 2).reshape(a.shape[1], N_SHARD * a.shape[2])
    g = parts["w_in"]
    zeros = lambda n: jnp.zeros((D_MODEL, n), g.dtype)
    win_ext = jnp.concatenate([g[0], g[1], g[2], g[3][:, :_KR_LOCAL], zeros(64), g[3][:, _KR_LOCAL:_KR_LOCAL + QK_ROPE],
                               zeros(32), g[3][:, _KR_LOCAL + QK_ROPE:]], axis=1)
    wuq_ext = jnp.pad(cols(parts["w_uq"]).reshape(Q_LORA, 8, 96), ((0, 0), (0, 0), (0, 32))).reshape(Q_LORA, 1024)
    wukv = cols(parts["w_ukv"]).reshape(KV_LORA, 8, 128)
    wk_ext = jnp.pad(wukv[:, :, :64], ((0, 0), (0, 0), (0, 64))).reshape(KV_LORA, 1024)
    wv = wukv[:, :, 64:].reshape(KV_LORA, 512)
    return win_ext, wuq_ext, wk_ext, wv


def _shard_cols(a):
    return a.reshape(a.shape[0], N_SHARD, a.shape[1] // N_SHARD).transpose(1, 0, 2)


def _shard_rows(a):
    return a.reshape(N_SHARD, a.shape[0] // N_SHARD, a.shape[1])


def _shard_early_grads(dwin_ext, dwuq_ext, dwk_ext, dwv):
    e, w = dwin_ext, D_IN // N_SHARD
    last = jnp.concatenate([e[:, 3 * w:2432], e[:, 2496:2528], e[:, 2560:]], axis=1)
    dwuq = dwuq_ext.reshape(Q_LORA, 8, 128)[:, :, :96].reshape(Q_LORA, 768)
    dwukv = jnp.concatenate([dwk_ext.reshape(KV_LORA, 8, 128)[:, :, :64], dwv.reshape(KV_LORA, 8, 64)], axis=2)
    return [jnp.stack([e[:, 0:w], e[:, w:2 * w], e[:, 2 * w:3 * w], last]), _shard_cols(dwuq),
            _shard_cols(dwukv.reshape(KV_LORA, 1024))]


def _rope_tables(positions):
    half = QK_ROPE // 2
    freq = ROPE_THETA ** (-jnp.arange(half, dtype=F32) / half)
    ang = positions.astype(F32)[:, None] * freq
    cos, sin = jnp.cos(ang), jnp.sin(ang)
    s = positions.shape[0]
    z = lambda n: jnp.zeros((s, n), F32)
    c_t = jnp.concatenate([jnp.ones((s, 64), F32), cos, cos, z(32)], axis=1)
    sa_t = jnp.concatenate([z(64), -sin, z(16), z(32)], axis=1)
    sb_t = jnp.concatenate([z(64), z(16), sin, z(32)], axis=1)
    return c_t, sa_t, sb_t


def _local_grads(x, p, positions, tgt, gains, early, late):
    win_ext, wuq_ext, wk_ext, wv = _extend_early(early)
    tabs = _rope_tables(positions)
    g = gains
    sbq, sbk, sbv, sbg, mlag, cq, ckv, qc, kc, mv, sbkt, sbvt, kct, mvt = _pre_fwd(
        x, tabs, g["norm_pre_g"], win_ext, g["q_norm_g"], wuq_ext, g["kv_norm_g"], wk_ext, wv)
    sbo, wout4, wple4, wpg4 = _sb_fwd(sbq, sbk, sbvt, late)
    wout, wpg = wout4.reshape(D_MODEL, D_MODEL), wpg4.reshape(D_MODEL, D_MODEL)
    wple = wple4.transpose(1, 0, 2).reshape(PLE_DIM, D_MODEL)
    mlao, lse = _mla_fwd(qc, kc, mvt)
    dsbo, dmlao, delta, dsbg, dmlag, dxres, dwout, dwpg, dwple, vec_c = _post(
        x, p, tgt, sbo, mlao, sbg, mlag, g["sb_out_norm_g"], g["mla_out_norm_g"], wout, g["norm_post_g"], wple,
        g["ple_norm_g"], wpg, g["b_ple_gate"])
    dsbq, dsbk, dsbv, *late_grads = _sb_bwd(sbq, sbk, sbkt, sbv, dsbo, [_shard_rows(dwout), _shard_cols(dwple), _shard_rows(dwpg)])
    dqc, dkc, dmv = _mla_bwd(qc, kc, kct, mv, dmlao, lse, delta)
    gx, dwin_ext, dwuq_ext, dwk_ext, dwv, vec_d = _pre_bwd(
        x, dxres, dsbq, dsbk, dsbv, dsbg, dmlag, dqc, dkc, dmv, cq, ckv, tabs, g["norm_pre_g"], win_ext, g["q_norm_g"],
        wuq_ext, g["kv_norm_g"], wk_ext, wv)
    return gx, _shard_early_grads(dwin_ext, dwuq_ext, dwk_ext, dwv), late_grads, jnp.concatenate([vec_c, vec_d], axis=0)


_VEC_LAYOUT = (("norm_post_g", 0, 0, 1024), ("ple_norm_g", 1, 0, 1024), ("b_ple_gate", 2, 0, 1024), ("sb_out_norm_g", 3, 0, 512),
               ("mla_out_norm_g", 3, 512, 512), ("norm_pre_g", 8, 0, 1024), ("q_norm_g", 9, 0, 256), ("kv_norm_g", 9, 256, 128))
_LOSS_ROW = 4
_WEIGHT_ORDER = ("norm_pre_g", "w_in", "q_norm_g", "w_uq", "kv_norm_g", "w_ukv", "sb_out_norm_g", "mla_out_norm_g", "w_out",
                 "norm_post_g", "w_ple", "ple_norm_g", "w_ple_gate", "b_ple_gate")


def kernel(x, p, positions, norm_pre_g, w_in, q_norm_g, w_uq, kv_norm_g, w_ukv, sb_out_norm_g, mla_out_norm_g, w_out, norm_post_g, w_ple, ple_norm_g, w_ple_gate, b_ple_gate, loss_target, m_norm_pre_g, m_w_in, m_q_norm_g, m_w_uq, m_kv_norm_g, m_w_ukv, m_sb_out_norm_g, m_mla_out_norm_g, m_w_out, m_norm_post_g, m_w_ple, m_ple_norm_g, m_w_ple_gate, m_b_ple_gate, v_norm_pre_g, v_w_in, v_q_norm_g, v_w_uq, v_kv_norm_g, v_w_ukv, v_sb_out_norm_g, v_mla_out_norm_g, v_w_out, v_norm_post_g, v_w_ple, v_ple_norm_g, v_w_ple_gate, v_b_ple_gate):
    w = {"norm_pre_g": norm_pre_g, "w_in": w_in[0], "q_norm_g": q_norm_g, "w_uq": w_uq[0], "kv_norm_g": kv_norm_g, "w_ukv": w_ukv[0],
         "sb_out_norm_g": sb_out_norm_g, "mla_out_norm_g": mla_out_norm_g, "w_out": w_out[0], "norm_post_g": norm_post_g,
         "w_ple": w_ple[0], "ple_norm_g": ple_norm_g, "w_ple_gate": w_ple_gate[0], "b_ple_gate": b_ple_gate}
    m = {"norm_pre_g": m_norm_pre_g, "w_in": m_w_in[0], "q_norm_g": m_q_norm_g, "w_uq": m_w_uq[0], "kv_norm_g": m_kv_norm_g,
         "w_ukv": m_w_ukv[0], "sb_out_norm_g": m_sb_out_norm_g, "mla_out_norm_g": m_mla_out_norm_g, "w_out": m_w_out[0],
         "norm_post_g": m_norm_post_g, "w_ple": m_w_ple[0], "ple_norm_g": m_ple_norm_g, "w_ple_gate": m_w_ple_gate[0],
         "b_ple_gate": m_b_ple_gate}
    v = {"norm_pre_g": v_norm_pre_g, "w_in": v_w_in[0], "q_norm_g": v_q_norm_g, "w_uq": v_w_uq[0], "kv_norm_g": v_kv_norm_g,
         "w_ukv": v_w_ukv[0], "sb_out_norm_g": v_sb_out_norm_g, "mla_out_norm_g": v_mla_out_norm_g, "w_out": v_w_out[0],
         "norm_post_g": v_norm_post_g, "w_ple": v_w_ple[0], "ple_norm_g": v_ple_norm_g, "w_ple_gate": v_w_ple_gate[0],
         "b_ple_gate": v_b_ple_gate}
    gathered = _allgather_weights([w[n] for n in _EARLY])
    gx, early_grads, late_red, vec = _local_grads(x[0], p[0, 0], positions[0], loss_target[0], w, dict(zip(_EARLY, gathered)),
                                                  [w[n] for n in _LATE])
    *early_red, vsum = _reduce_scatter_grads(early_grads, vec)
    gred = early_red + late_red
    loss = vsum[_LOSS_ROW, 0]

    g, delta, new_m, new_v = _adamw_small(vsum, w, m, v)
    for n, gn in zip(_BIG, gred):
        g[n], delta[n], new_m[n], new_v[n] = _adamw(w[n], gn, m[n], v[n])

    lead = lambda n, a: a[None] if n in _BIG else a
    return (loss, gx[None],
            *[lead(n, g[n]) for n in _WEIGHT_ORDER], *[lead(n, delta[n]) for n in _WEIGHT_ORDER],
            *[lead(n, new_m[n]) for n in _WEIGHT_ORDER], *[lead(n, new_v[n]) for n in _WEIGHT_ORDER])
```

```python
import numpy as np
import jax
import jax.numpy as jnp
from jax import lax
from jax.experimental import pallas as pl
from jax.experimental.pallas import tpu as pltpu

F32 = jnp.float32
BF16 = jnp.bfloat16
MESH = pl.DeviceIdType.MESH

D_MODEL = 1024
HEAD_DIM = 64
D_SB = 512
D_MLA = 512
Q_LORA = 256
KV_LORA = 128
QK_NOPE = 64
QK_ROPE = 32
PLE_DIM = 256
D_IN = 2976
D_EXT = 3072
ROPE_THETA = 10000.0
EPS = 1e-6
N_SHARD = 4

ADAM_LR = 0.001
ADAM_B1 = 0.9
ADAM_B2 = 0.999
ADAM_EPS = 1e-08
ADAM_WD = 0.01
ADAM_STEP = 10

LANES = 128
BK = 128
WQ = 256
MQ = 2048
SB_CUTOFF = 120.0
TM = 256
TM_PRE = 256
VEC_ROWS = 16
VMEM_DENSE = 52 * 1024 * 1024
VMEM_ATTN = 56 * 1024 * 1024


def _mm(a, b):
    return jnp.dot(a, b, preferred_element_type=F32)


def _mm_nt(a, b):
    return lax.dot_general(a, b, (((1,), (1,)), ((), ())), preferred_element_type=F32)


def _mm_tn(a, b):
    return lax.dot_general(a, b, (((0,), (0,)), ((), ())), preferred_element_type=F32)


def _seg(a, bd2):
    return _mm(_split2(a), bd2)


def _const(mask):
    return jnp.asarray(np.asarray(mask, np.float32), dtype=BF16)


def _blockdiag2(n, seg):
    r = (np.arange(2 * n)[:, None] % n) // seg
    c = np.arange(n)[None, :] // seg
    return _const(r == c)


def _sigmoid(a):
    return 1.0 / (1.0 + jnp.exp(-a))


def _rowmean(a):
    return jnp.mean(a, axis=-1, keepdims=True)


def _colsum(a):
    return jnp.sum(a, axis=0, keepdims=True)


def _rope_fwd(a, c, sa, sb):
    w = a.shape[-1]
    return a * c + pltpu.roll(a, w - 16, 1) * sa + pltpu.roll(a, 16, 1) * sb


def _rope_bwd(g, c, sa, sb):
    w = g.shape[-1]
    return g * c + pltpu.roll(g * sa, 16, 1) + pltpu.roll(g * sb, w - 16, 1)


def _full(shape):
    return pl.BlockSpec(shape, lambda *_: (0,) * len(shape))


def _acc(shape):
    return pl.BlockSpec(shape, lambda *_: (0,) * len(shape))


def _full2(shape):
    return pl.BlockSpec(shape, lambda p, i: (0, 0))


def _cols(height, tm=TM):
    return pl.BlockSpec((height, tm), lambda i: (0, i))


def _rows(width, tm=TM):
    return pl.BlockSpec((tm, width), lambda i: (i, 0))


def _pre_fwd(x, tabs, gpre, win, gq, wuq, gkv, wk, wv):
    s = x.shape[0]
    c_t, sa_t, sb_t = tabs
    rw, cl = (lambda width: _rows(width, TM_PRE)), (lambda height: _cols(height, TM_PRE))

    def body(x_ref, c_ref, sa_ref, sb_ref, gpre_ref, win_ref, gq_ref, wuq_ref, gkv_ref, wk_ref, wv_ref,
             sbq_ref, sbk_ref, sbv_ref, sbg_ref, mlag_ref, cq_ref, ckv_ref, qc_ref, kc_ref, mv_ref,
             sbkt_ref, sbvt_ref, kct_ref, mvt_ref):
        xv = x_ref[...]
        r1 = lax.rsqrt(_rowmean(xv * xv) + EPS)
        h = (xv * r1 * gpre_ref[...]).astype(BF16)
        proj = _mm(h, win_ref[...])
        sbq_ref[...] = proj[:, 0:512].astype(BF16)
        sbk_ref[...] = proj[:, 512:1024].astype(BF16)
        sbv_ref[...] = proj[:, 1024:1536].astype(BF16)
        sbkt_ref[...] = proj[:, 512:1024].T.astype(BF16)
        sbvt_ref[...] = proj[:, 1024:1536].T.astype(BF16)
        sbg_ref[...] = proj[:, 1536:2048]
        cq = proj[:, 2048:2304]
        ckv = proj[:, 2304:2432]
        kr = proj[:, 2432:2560]
        mlag_ref[...] = proj[:, 2560:3072]
        cq_ref[...] = cq
        ckv_ref[...] = ckv
        c1, sa1, sb1 = c_ref[...], sa_ref[...], sb_ref[...]
        c8, sa8, sb8 = jnp.tile(c1, (1, 8)), jnp.tile(sa1, (1, 8)), jnp.tile(sb1, (1, 8))
        cqn = (cq * lax.rsqrt(_rowmean(cq * cq) + EPS) * gq_ref[...]).astype(BF16)
        qe = _mm(cqn, wuq_ref[...])
        qc_ref[...] = _rope_fwd(qe, c8, sa8, sb8).astype(BF16)
        ckvn = (ckv * lax.rsqrt(_rowmean(ckv * ckv) + EPS) * gkv_ref[...]).astype(BF16)
        ke = _mm(ckvn, wk_ref[...])
        krr = _rope_fwd(kr, c1, sa1, sb1)
        kcat = ke + jnp.tile(krr, (1, 8))
        kc_ref[...] = kcat.astype(BF16)
        kct_ref[...] = kcat.T.astype(BF16)
        mval = _mm(ckvn, wv_ref[...])
        mv_ref[...] = mval.astype(BF16)
        mvt_ref[...] = mval.T.astype(BF16)

    out_shape = (
        jax.ShapeDtypeStruct((s, 512), BF16), jax.ShapeDtypeStruct((s, 512), BF16), jax.ShapeDtypeStruct((s, 512), BF16),
        jax.ShapeDtypeStruct((s, 512), F32), jax.ShapeDtypeStruct((s, 512), F32),
        jax.ShapeDtypeStruct((s, Q_LORA), F32), jax.ShapeDtypeStruct((s, KV_LORA), F32),
        jax.ShapeDtypeStruct((s, 1024), BF16), jax.ShapeDtypeStruct((s, 1024), BF16), jax.ShapeDtypeStruct((s, 512), BF16),
        jax.ShapeDtypeStruct((512, s), BF16), jax.ShapeDtypeStruct((512, s), BF16), jax.ShapeDtypeStruct((1024, s), BF16),
        jax.ShapeDtypeStruct((512, s), BF16),
    )
    return pl.pallas_call(
        body, name="pre_fwd", grid=(s // TM_PRE,), out_shape=out_shape,
        in_specs=[rw(D_MODEL), rw(LANES), rw(LANES), rw(LANES), _full((1, D_MODEL)), _full((D_MODEL, D_EXT)),
                  _full((1, Q_LORA)), _full((Q_LORA, 1024)), _full((1, KV_LORA)), _full((KV_LORA, 1024)), _full((KV_LORA, 512))],
        out_specs=(rw(512), rw(512), rw(512), rw(512), rw(512), rw(Q_LORA), rw(KV_LORA),
                   rw(1024), rw(1024), rw(512), cl(512), cl(512), cl(1024), cl(512)),
        compiler_params=pltpu.CompilerParams(vmem_limit_bytes=VMEM_DENSE),
    )(x, c_t, sa_t, sb_t, gpre, win, gq, wuq, gkv, wk, wv)


def _softplus(z):
    neg_abs = lax.bitcast_convert_type(lax.bitcast_convert_type(z, jnp.uint32) | jnp.uint32(0x80000000), F32)
    return jnp.maximum(z, 0.0) + jnp.log(1.0 + jnp.exp(neg_abs))


def _sum_matrix(kind, terms):
    r, c = np.arange(2 * BK)[:, None], np.arange(2 * BK * terms)[None, :] % (2 * BK)
    rk, ck = r % BK, c % BK
    return _const(((r // BK) == (c // BK)) & {"suffix": ck >= rk, "prefix": ck <= rk}[kind])


def _split_rows(a):
    hi = a.astype(BF16)
    return jnp.concatenate([hi, (a - hi.astype(F32)).astype(BF16)], axis=0)


def _heads_t(blk, rowi):
    zero = jnp.zeros_like(blk)
    return jnp.concatenate([jnp.where(rowi < 64, blk, zero), jnp.where(rowi >= 64, blk, zero)], axis=1)


def _mask_keys(a, valid, fill=0.0):
    return jnp.concatenate([jnp.where(valid, a[0:BK], fill), jnp.where(valid, a[BK:2 * BK], fill)], axis=0)


def _split2(a):
    hi = a.astype(BF16)
    lo = (a - hi.astype(F32)).astype(BF16)
    return jnp.concatenate([hi, lo], axis=1)


def _pair_stack(b, lane):
    zero = jnp.zeros_like(b)
    return jnp.concatenate([jnp.where(lane < 64, b, zero), jnp.where(lane >= 64, b, zero)], axis=0)


def _sb_fwd(q, k, vt, late):
    s = q.shape[0]
    n = len(late)

    def body(q_ref, k_ref, vt_ref, usuf_ref, *rest):
        ins, o_ref, outs = rest[:n], rest[n], rest[n + 1:2 * n + 1]
        acc_scr, run_scr = rest[2 * n + 1:2 * n + 3]
        bufs, (send_sems, recv_sems, out_sems) = rest[2 * n + 3:3 * n + 3], rest[3 * n + 3:]
        p, i = pl.program_id(0), pl.program_id(1)
        gather_start, gather_forward, gather_finish = _gather_steps([a.shape for a in late], ins, bufs, send_sems, recv_sems)

        @pl.when((p == 0) & (i == 0))
        def _():
            gather_start()

        @pl.when((p == 2) & (i == 0))
        def _():
            gather_forward()

        lane = lax.broadcasted_iota(jnp.int32, (1, LANES), 1)
        rowi = lax.broadcasted_iota(jnp.int32, (LANES, 1), 0)
        keyi = lax.broadcasted_iota(jnp.int32, (BK, WQ), 0)
        qryi = lax.broadcasted_iota(jnp.int32, (BK, WQ), 1) + i * WQ
        qs = q_ref[...] * (HEAD_DIM ** -0.5)

        def group(blocks, masked, seen=None):
            seen = seen or [0] * len(blocks)
            starts = [pl.multiple_of(j * BK, BK) for j in blocks]
            valid = [(keyi[:, lo:] + j * BK) < qryi[:, lo:] if m else None for j, m, lo in zip(blocks, masked, seen)]
            zs = [_mm_nt(_pair_stack(k_ref[pl.ds(ks, BK), :], lane), qs[lo:]) for ks, lo in zip(starts, seen)]
            sps = [_softplus(z) for z in zs]
            sps = [sp if ok is None else _mask_keys(sp, ok) for sp, ok in zip(sps, valid)]
            cums = [_mm(usuf_ref[...], _split_rows(sp)) for sp in sps]
            ws = [jnp.exp(z - c) for z, c in zip(zs, cums)]
            ws = [w if ok is None else _mask_keys(w, ok) for w, ok in zip(ws, valid)]
            pvs = [_mm(_heads_t(vt_ref[:, pl.ds(ks, BK)], rowi), w.astype(BF16)) for ks, w in zip(starts, ws)]
            for pv, c, lo in zip(pvs, cums, seen):
                r0, r1 = run_scr[0:1, lo:], run_scr[1:2, lo:]
                acc_scr[:, lo:] += jnp.where(rowi < 64, jnp.exp(-r0), jnp.exp(-r1)) * pv
                run_scr[0:1, lo:] = r0 + c[0:1]
                run_scr[1:2, lo:] = r1 + c[BK:BK + 1]

        assert WQ == 2 * BK
        acc_scr[...] = jnp.zeros_like(acc_scr)
        run_scr[...] = jnp.zeros_like(run_scr)

        @pl.when(i == 0)
        def _():
            group([1, 0], [True, True], [BK, 0])

        @pl.when(i > 0)
        def _():
            group([2 * i + 1, 2 * i, 2 * i - 1, 2 * i - 2], [True, True, False, False], [BK, 0, 0, 0])

        def unfinished():
            return (jnp.min(run_scr[0:2, :]) < SB_CUTOFF).astype(jnp.int32)

        def step(c):
            group([2 * i - 1 - 2 * c[0], 2 * i - 2 - 2 * c[0]], [False, False])
            return c[0] + 1, unfinished()

        lax.while_loop(lambda c: (c[0] < i) & (c[1] > 0), step, (jnp.int32(1), unfinished()))
        o_ref[...] = acc_scr[...].T

        @pl.when((p == pl.num_programs(0) - 1) & (i == pl.num_programs(1) - 1))
        def _():
            gather_finish()
            copies = [pltpu.make_async_copy(bufs[t], outs[t], out_sems.at[t]) for t in range(n)]
            for cp in copies:
                cp.start()
            for cp in copies:
                cp.wait()

    qspec = pl.BlockSpec((WQ, LANES), lambda p, i: (i, p))
    kspec = pl.BlockSpec((s, LANES), lambda p, i: (0, p))
    tspec = pl.BlockSpec((LANES, s), lambda p, i: (p, 0))
    gathered = [jax.ShapeDtypeStruct((N_SHARD,) + a.shape, BF16) for a in late]
    return pl.pallas_call(
        body, name="sb_fwd", grid=(4, s // WQ),
        out_shape=(jax.ShapeDtypeStruct((s, 512), F32), *gathered),
        in_specs=[qspec, kspec, tspec, _full2((2 * BK, 4 * BK))] + [_full2(a.shape) for a in late],
        out_specs=(qspec,) + (pl.BlockSpec(memory_space=pl.ANY),) * n,
        scratch_shapes=[pltpu.VMEM((LANES, WQ), F32), pltpu.VMEM((8, WQ), F32)] + [pltpu.VMEM(g.shape, BF16) for g in gathered]
                       + [pltpu.SemaphoreType.DMA((6 * n,)), pltpu.SemaphoreType.DMA((6 * n,)), pltpu.SemaphoreType.DMA((n,))],
        compiler_params=pltpu.CompilerParams(vmem_limit_bytes=VMEM_ATTN),
    )(q, k, vt, _sum_matrix("suffix", 2), *late)


def _sb_bwd(q, k, kt, v, do, late):
    s = q.shape[0]
    n = len(late)
    halves = [a.shape[1] // 2 for a in late]

    def body(q_ref, k_ref, kt_ref, v_ref, do_ref, usuf_ref, upre_ref, *rest):
        g_refs, (dq_ref, dk_ref, dv_ref), outs = rest[:n], rest[n:n + 3], rest[n + 3:2 * n + 3]
        later_scr, dqt_scr, st_scr = rest[2 * n + 3:2 * n + 6]
        f_scr, reduce_scr, out_sems = rest[2 * n + 6:3 * n + 6], rest[3 * n + 6:-1], rest[-1]
        p, i = pl.program_id(0), pl.program_id(1)
        reduce_load, reduce_partial, reduce_total, reduce_finish = _reduce_steps(halves, g_refs, f_scr, reduce_scr)

        @pl.when((p == 0) & (i == 0))
        def _():
            reduce_load()

        @pl.when((p == 1) & (i == 0))
        def _():
            reduce_partial()

        @pl.when((p == 3) & (i == 0))
        def _():
            reduce_total()

        @pl.when(i == 0)
        def _():
            dk_ref[...] = jnp.zeros_like(dk_ref)
            dv_ref[...] = jnp.zeros_like(dv_ref)

        lane = lax.broadcasted_iota(jnp.int32, (1, LANES), 1)
        rowi = lax.broadcasted_iota(jnp.int32, (LANES, 1), 0)
        keyi = lax.broadcasted_iota(jnp.int32, (BK, WQ), 0)
        qryi = lax.broadcasted_iota(jnp.int32, (BK, WQ), 1) + i * WQ
        qs = q_ref[...] * (HEAD_DIM ** -0.5)
        dob = do_ref[...]
        dot = dob.astype(F32).T.astype(BF16)

        def scores(j, lo=0):
            return _mm_nt(_pair_stack(k_ref[pl.ds(pl.multiple_of(j * BK, BK), BK), :], lane), qs[lo:])

        def scan(blocks, masked, seen=None):
            seen = seen or [0] * len(blocks)
            sps = [_softplus(scores(j, lo)) for j, lo in zip(blocks, seen)]
            sps = [_mask_keys(sp, (keyi[:, lo:] + j * BK) < qryi[:, lo:]) if m else sp
                   for sp, j, m, lo in zip(sps, blocks, masked, seen)]
            for sp, j, lo in zip(sps, blocks, seen):
                run = st_scr[0:2, :]
                later_scr[j, 0:2, :] = run
                st_scr[0:2, lo:] = run[:, lo:] + jnp.concatenate([jnp.sum(sp[0:BK], axis=0, keepdims=True),
                                                                  jnp.sum(sp[BK:2 * BK], axis=0, keepdims=True)], axis=0)

        def sweep(blocks, masked, seen=None):
            seen = seen or [0] * len(blocks)
            starts = [pl.multiple_of(j * BK, BK) for j in blocks]
            valid = [(keyi[:, lo:] + j * BK) < qryi[:, lo:] if m else None for j, m, lo in zip(blocks, masked, seen)]
            zs = [scores(j, lo) for j, lo in zip(blocks, seen)]
            us = [jnp.exp(lax.bitcast_convert_type(lax.bitcast_convert_type(z, jnp.uint32) | jnp.uint32(0x80000000), F32))
                  for z in zs]
            sps = [jnp.maximum(z, 0.0) + jnp.log(1.0 + u) for z, u in zip(zs, us)]
            sps = [sp if ok is None else _mask_keys(sp, ok) for sp, ok in zip(sps, valid)]
            sigs = [jnp.where(z >= 0.0, 1.0, u) / (1.0 + u) for z, u in zip(zs, us)]
            cums = [_mm(usuf_ref[...], _split_rows(sp)) for sp in sps]
            dws = [_mm(_pair_stack(v_ref[pl.ds(ks, BK), :], lane), dot[:, lo:]) for ks, lo in zip(starts, seen)]
            wfs = []
            for z, c, j, ok, lo in zip(zs, cums, blocks, valid, seen):
                f = jnp.exp(-later_scr[j, 0:2, lo:])
                wide = (BK, WQ - lo)
                wf = jnp.exp(z - c) * jnp.concatenate([jnp.broadcast_to(f[0:1], wide), jnp.broadcast_to(f[1:2], wide)], axis=0)
                wfs.append(wf if ok is None else _mask_keys(wf, ok))
            es = [dw * wf for dw, wf in zip(dws, wfs)]
            pres = [_mm(upre_ref[...], e.astype(BF16)) for e in es]
            dzs = []
            for e, pre, sig, ok, lo in zip(es, pres, sigs, valid, seen):
                e0 = pre[0:BK] + st_scr[0:1, lo:]
                e1 = pre[BK:2 * BK] + st_scr[1:2, lo:]
                st_scr[0:1, lo:] = e0[BK - 1:BK]
                st_scr[1:2, lo:] = e1[BK - 1:BK]
                dz = e - sig * jnp.concatenate([e0, e1], axis=0)
                dzs.append((dz if ok is None else _mask_keys(dz, ok)).astype(BF16))
            whole = [b for b, lo in enumerate(seen) if lo == 0]
            dqt_scr[...] += _mm(jnp.concatenate([_heads_t(kt_ref[:, pl.ds(starts[b], BK)], rowi) for b in whole], axis=1),
                                jnp.concatenate([dzs[b] for b in whole], axis=0))
            for b, lo in enumerate(seen):
                if lo:
                    dqt_scr[:, lo:] += _mm(_heads_t(kt_ref[:, pl.ds(starts[b], BK)], rowi), dzs[b])
            for ks, dz, wf, lo in zip(starts, dzs, wfs, seen):
                rk = _mm(dz, qs[lo:])
                dk_ref[pl.ds(ks, BK), :] += jnp.where(lane < 64, rk[0:BK], rk[BK:2 * BK])
                rv = _mm(wf.astype(BF16), dob[lo:])
                dv_ref[pl.ds(ks, BK), :] += jnp.where(lane < 64, rv[0:BK], rv[BK:2 * BK])

        assert WQ == 2 * BK
        st_scr[...] = jnp.zeros_like(st_scr)

        @pl.when(i == 0)
        def _():
            scan([1, 0], [True, True], [BK, 0])

        @pl.when(i > 0)
        def _():
            scan([2 * i + 1, 2 * i, 2 * i - 1, 2 * i - 2], [True, True, False, False], [BK, 0, 0, 0])

        def unfinished():
            return (jnp.min(st_scr[0:2, :]) < SB_CUTOFF).astype(jnp.int32)

        def step(c):
            scan([2 * i - 1 - 2 * c[0], 2 * i - 2 - 2 * c[0]], [False, False])
            return c[0] + 1, unfinished()

        npairs, _ = lax.while_loop(lambda c: (c[0] < i) & (c[1] > 0), step, (jnp.minimum(i, 1), unfinished()))

        st_scr[...] = jnp.zeros_like(st_scr)
        dqt_scr[...] = jnp.zeros_like(dqt_scr)
        first = 2 * (i - npairs)

        def early(t, carry):
            sweep([first + 2 * t, first + 2 * t + 1], [False, False])
            return carry

        lax.fori_loop(0, npairs - 1, early, 0)

        @pl.when(i == 0)
        def _():
            sweep([0, 1], [True, True], [0, BK])

        @pl.when(i > 0)
        def _():
            sweep([2 * i - 2, 2 * i - 1, 2 * i, 2 * i + 1], [False, False, True, True], [0, 0, 0, BK])

        dq_ref[...] = (dqt_scr[...].T * (HEAD_DIM ** -0.5)).astype(BF16)

        @pl.when((p == pl.num_programs(0) - 1) & (i == pl.num_programs(1) - 1))
        def _():
            reduce_finish()
            copies = [pltpu.make_async_copy(f_scr[t], outs[t], out_sems.at[t]) for t in range(n)]
            for cp in copies:
                cp.start()
            for cp in copies:
                cp.wait()

    qspec = pl.BlockSpec((WQ, LANES), lambda p, i: (i, p))
    kspec = pl.BlockSpec((s, LANES), lambda p, i: (0, p))
    tspec = pl.BlockSpec((LANES, s), lambda p, i: (p, 0))
    anywhere = pl.BlockSpec(memory_space=pl.ANY)
    reduced = [jax.ShapeDtypeStruct(a.shape[1:], F32) for a in late]
    return pl.pallas_call(
        body, name="sb_bwd", grid=(4, s // WQ),
        out_shape=(jax.ShapeDtypeStruct((s, 512), BF16), jax.ShapeDtypeStruct((s, 512), F32),
                   jax.ShapeDtypeStruct((s, 512), F32), *reduced),
        in_specs=[qspec, kspec, tspec, kspec, qspec, _full2((2 * BK, 4 * BK)), _full2((2 * BK, 2 * BK))] + [anywhere] * n,
        out_specs=(qspec, kspec, kspec) + (anywhere,) * n,
        scratch_shapes=[pltpu.VMEM((s // BK, 8, WQ), F32), pltpu.VMEM((LANES, WQ), F32), pltpu.VMEM((8, WQ), F32)]
                       + [pltpu.VMEM(r.shape, F32) for r in reduced] + _reduce_scratch(late) + [pltpu.SemaphoreType.DMA((n,))],
        compiler_params=pltpu.CompilerParams(vmem_limit_bytes=VMEM_ATTN),
    )(q, k, kt, v, do, _sum_matrix("suffix", 2), _sum_matrix("prefix", 1), *late)


MLA_SCALE = (QK_NOPE + QK_ROPE) ** -0.5
LOG2E = 1.4426950408889634


def _mla_keys(kb):
    zero = jnp.zeros((BK, LANES), kb.dtype)
    return jnp.concatenate([jnp.concatenate([kb[:, 0:LANES], zero], axis=1),
                            jnp.concatenate([zero, kb[:, LANES:2 * LANES]], axis=1)], axis=0)


def _mla_fwd(qc, kc, vt):
    s = qc.shape[0]
    rows_l = 16

    def body(q_ref, k_ref, vt_ref, o_ref, l_ref, p_scr, ot_scr, st_scr):
        i = pl.program_id(1)
        keyc = lax.broadcasted_iota(jnp.int32, (BK, MQ), 0)
        qryc = (lax.broadcasted_iota(jnp.int32, (BK, MQ), 1) + i * MQ) // 64
        row = lax.broadcasted_iota(jnp.int32, (LANES, 1), 0)
        qw = q_ref[...]
        orow = lax.broadcasted_iota(jnp.int32, (rows_l, 2 * BK), 0)
        ocol = lax.broadcasted_iota(jnp.int32, (rows_l, 2 * BK), 1)
        ones = jnp.where(((orow == 0) & (ocol < BK)) | ((orow == 1) & (ocol >= BK)), 1.0, 0.0).astype(BF16)

        def scores(j, lo=0):
            ks = pl.multiple_of(j * BK, BK)
            return _mm_nt(_mla_keys(k_ref[pl.ds(ks, BK), :]), qw[lo:])

        def values_t(j):
            vtb = vt_ref[:, pl.ds(pl.multiple_of(j * BK, BK), BK)]
            zero = jnp.zeros_like(vtb)
            top = jnp.concatenate([jnp.where(row < 64, vtb, zero), jnp.where(row >= 64, vtb, zero)], axis=1)
            return jnp.concatenate([top, ones], axis=0)

        def softmax(ja, za, zb, masked, lo=0):
            c = MLA_SCALE * LOG2E
            parts = [za[0:BK] * c, za[BK:2 * BK] * c, zb[0:BK] * c, zb[BK:2 * BK] * c]
            if masked:
                va = ((keyc[:, lo:] + ja * BK) // 64) <= qryc[:, lo:]
                vb = ((keyc[:, lo:] + (ja + 1) * BK) // 64) <= qryc[:, lo:]
                parts = [jnp.where(va, parts[0], -1e30), jnp.where(va, parts[1], -1e30),
                         jnp.where(vb, parts[2], -1e30), jnp.where(vb, parts[3], -1e30)]
            m0, m1 = st_scr[0:1, lo:], st_scr[1:2, lo:]
            n0 = jnp.maximum(m0, jnp.max(jnp.maximum(parts[0], parts[2]), axis=0, keepdims=True))
            n1 = jnp.maximum(m1, jnp.max(jnp.maximum(parts[1], parts[3]), axis=0, keepdims=True))
            st_scr[2:3, lo:] = jnp.exp2(m0 - n0)
            st_scr[3:4, lo:] = jnp.exp2(m1 - n1)
            st_scr[0:1, lo:] = n0
            st_scr[1:2, lo:] = n1
            p_scr[:, lo:] = jnp.concatenate([jnp.exp2(parts[0] - n0), jnp.exp2(parts[1] - n1),
                                             jnp.exp2(parts[2] - n0), jnp.exp2(parts[3] - n1)], axis=0).astype(BF16)

        def accumulate(ja, lo=0):
            pv = _mm(jnp.concatenate([values_t(ja), values_t(ja + 1)], axis=1), p_scr[:, lo:])
            a = jnp.where(row < 64, st_scr[2:3, lo:], st_scr[3:4, lo:])
            ot_scr[0:LANES, lo:] = a * ot_scr[0:LANES, lo:] + pv[0:LANES]
            ot_scr[LANES:LANES + 8, lo:] = st_scr[2:10, lo:] * ot_scr[LANES:LANES + 8, lo:] + pv[LANES:LANES + 8]

        def step(n, masked, lo=0, prev_lo=0):
            za, zb = scores(2 * n, lo), scores(2 * n + 1, lo)
            accumulate(2 * n - 2, prev_lo)
            softmax(2 * n, za, zb, masked, lo)

        def first(masked):
            softmax(0, scores(0), scores(1), masked)

        st_scr[...] = jnp.concatenate([jnp.full((2, MQ), -1e30, F32), jnp.ones((14, MQ), F32)], axis=0)
        ot_scr[...] = jnp.zeros_like(ot_scr)

        npq = MQ // (2 * BK)
        seen = lambda d: 2 * BK * max(d, 0)

        @pl.when(i == 0)
        def _():
            first(True)
            for d in range(1, npq):
                step(d, True, seen(d), seen(d - 1))

        @pl.when(i > 0)
        def _():
            first(False)
            lax.fori_loop(1, npq * i, lambda n, c: (step(n, False), c)[1], 0)
            for d in range(npq):
                step(npq * i + d, True, seen(d), seen(d - 1))

        accumulate(2 * (npq * (i + 1) - 1), seen(npq - 1))
        l0, l1 = ot_scr[LANES:LANES + 1, :], ot_scr[LANES + 1:LANES + 2, :]
        o_ref[...] = (ot_scr[0:LANES, :] / jnp.where(row < 64, l0, l1)).T
        l_ref[...] = jnp.where(row < 64, st_scr[0:1, :] + jnp.log2(l0), st_scr[1:2, :] + jnp.log2(l1)).T

    qspec = pl.BlockSpec((MQ, 2 * LANES), lambda p, i: (i, p))
    kspec = pl.BlockSpec((s, 2 * LANES), lambda p, i: (0, p))
    vtspec = pl.BlockSpec((LANES, s), lambda p, i: (p, 0))
    ospec = pl.BlockSpec((MQ, LANES), lambda p, i: (i, p))
    return pl.pallas_call(
        body, name="mla_fwd", grid=(4, s // MQ),
        out_shape=(jax.ShapeDtypeStruct((s, 512), F32), jax.ShapeDtypeStruct((s, 512), F32)),
        in_specs=[qspec, kspec, vtspec], out_specs=(ospec, ospec),
        scratch_shapes=[pltpu.VMEM((4 * BK, MQ), BF16), pltpu.VMEM((LANES + 8, MQ), F32), pltpu.VMEM((16, MQ), F32)],
        compiler_params=pltpu.CompilerParams(vmem_limit_bytes=VMEM_ATTN),
    )(qc, kc, vt)


def _mla_bwd(qc, kc, kct, v, do, lse, delta):
    s = qc.shape[0]

    def body(q_ref, k_ref, kt_ref, v_ref, do_ref, l_ref, d_ref, dq_ref, dk_ref, dv_ref, dqt_scr, p_scr, dz_scr):
        i = pl.program_id(1)

        @pl.when(i == 0)
        def _():
            dk_ref[...] = jnp.zeros_like(dk_ref)
            dv_ref[...] = jnp.zeros_like(dv_ref)

        lane = lax.broadcasted_iota(jnp.int32, (1, LANES), 1)
        keyc = lax.broadcasted_iota(jnp.int32, (BK, MQ), 0)
        qryc = (lax.broadcasted_iota(jnp.int32, (BK, MQ), 1) + i * MQ) // 64
        qw = q_ref[...]
        dob = do_ref[...]
        dost = (dob.astype(F32) * MLA_SCALE).T.astype(BF16)
        lt = l_ref[...].T
        dt = (d_ref[...] * MLA_SCALE).T
        lse0, lse1 = lt[0:1], lt[64:65]
        dl0, dl1 = dt[0:1], dt[64:65]
        dqt_scr[...] = jnp.zeros_like(dqt_scr)

        def products(j, lo=0):
            ks = pl.multiple_of(j * BK, BK)
            return (_mm_nt(_mla_keys(k_ref[pl.ds(ks, BK), :]), qw[lo:]),
                    _mm(_pair_stack(v_ref[pl.ds(ks, BK), :], lane), dost[:, lo:]))

        def grads(j, slot, zt, dwt, masked, lo=0):
            zt = zt * (MLA_SCALE * LOG2E)
            p0 = jnp.exp2(zt[0:BK] - lse0[:, lo:])
            p1 = jnp.exp2(zt[BK:2 * BK] - lse1[:, lo:])
            if masked:
                valid = ((keyc[:, lo:] + j * BK) // 64) <= qryc[:, lo:]
                p0, p1 = jnp.where(valid, p0, 0.0), jnp.where(valid, p1, 0.0)
            p_scr[slot, :, lo:] = jnp.concatenate([p0, p1], axis=0).astype(BF16)
            dz_scr[slot, :, lo:] = jnp.concatenate([p0 * (dwt[0:BK] - dl0[:, lo:]), p1 * (dwt[BK:2 * BK] - dl1[:, lo:])],
                                                   axis=0).astype(BF16)

        def keys_t(ks):
            ktb = kt_ref[:, pl.ds(ks, BK)]
            zero = jnp.zeros((LANES, BK), ktb.dtype)
            return jnp.concatenate([jnp.concatenate([ktb[0:LANES], zero], axis=1),
                                    jnp.concatenate([zero, ktb[LANES:2 * LANES]], axis=1)], axis=0)

        def scatter(ja, lo=0):
            ksa, ksb = pl.multiple_of(ja * BK, BK), pl.multiple_of((ja + 1) * BK, BK)
            dqt_scr[:, lo:] += _mm(jnp.concatenate([keys_t(ksa), keys_t(ksb)], axis=1),
                                   jnp.concatenate([dz_scr[0, :, lo:], dz_scr[1, :, lo:]], axis=0))
            for slot, ks in ((0, ksa), (1, ksb)):
                rk = _mm(dz_scr[slot, :, lo:], qw[lo:])
                dk_ref[pl.ds(ks, BK), :] += jnp.concatenate([rk[0:BK, 0:LANES], rk[BK:2 * BK, LANES:2 * LANES]], axis=1)
                rv = _mm(p_scr[slot, :, lo:], dob[lo:])
                dv_ref[pl.ds(ks, BK), :] += jnp.where(lane < 64, rv[0:BK], rv[BK:2 * BK])

        def step(n, masked, lo=0, prev_lo=0):
            za, wa = products(2 * n, lo)
            zb, wb = products(2 * n + 1, lo)
            scatter(2 * n - 2, prev_lo)
            grads(2 * n, 0, za, wa, masked, lo)
            grads(2 * n + 1, 1, zb, wb, masked, lo)

        def first(masked):
            za, wa = products(0)
            zb, wb = products(1)
            grads(0, 0, za, wa, masked)
            grads(1, 1, zb, wb, masked)

        npq = MQ // (2 * BK)
        seen = lambda d: 2 * BK * max(d, 0)

        @pl.when(i == 0)
        def _():
            first(True)
            for d in range(1, npq):
                step(d, True, seen(d), seen(d - 1))

        @pl.when(i > 0)
        def _():
            first(False)
            lax.fori_loop(1, npq * i, lambda n, c: (step(n, False), c)[1], 0)
            for d in range(npq):
                step(npq * i + d, True, seen(d), seen(d - 1))

        scatter(2 * (npq * (i + 1) - 1), seen(npq - 1))
        dq_ref[...] = dqt_scr[...].T

    qspec = pl.BlockSpec((MQ, 2 * LANES), lambda p, i: (i, p))
    kspec = pl.BlockSpec((s, 2 * LANES), lambda p, i: (0, p))
    ktspec = pl.BlockSpec((2 * LANES, s), lambda p, i: (p, 0))
    vspec = pl.BlockSpec((s, LANES), lambda p, i: (0, p))
    ospec = pl.BlockSpec((MQ, LANES), lambda p, i: (i, p))
    return pl.pallas_call(
        body, name="mla_bwd", grid=(4, s // MQ),
        out_shape=(jax.ShapeDtypeStruct((s, 1024), F32), jax.ShapeDtypeStruct((s, 1024), F32),
                   jax.ShapeDtypeStruct((s, 512), F32)),
        in_specs=[qspec, kspec, ktspec, vspec, ospec, ospec, ospec], out_specs=(qspec, kspec, vspec),
        scratch_shapes=[pltpu.VMEM((2 * LANES, MQ), F32), pltpu.VMEM((2, 2 * BK, MQ), BF16), pltpu.VMEM((2, 2 * BK, MQ), BF16)],
        compiler_params=pltpu.CompilerParams(vmem_limit_bytes=VMEM_ATTN),
    )(qc, kc, kct, v, do, lse, delta)


def _post(x, p, tgt, sbo, mlao, sbg, mlag, gsb, gmla, wout, gpost, wple, gple, wpg, bpg):
    s = x.shape[0]

    def body(x_ref, p_ref, t_ref, sbo_ref, mlao_ref, sbg_ref, mlag_ref, gsb_ref, gmla_ref, wout_ref,
             gpost_ref, wple_ref, gple_ref, wpg_ref, bpg_ref, bd_ref,
             dsbo_ref, dmlao_ref, delta_ref, dsbg_ref, dmlag_ref, dxres_ref, dwout_ref, dwpg_ref, dwple_ref, vec_ref):
        i = pl.program_id(0)

        @pl.when(i == 0)
        def _():
            dwout_ref[...] = jnp.zeros_like(dwout_ref)
            dwpg_ref[...] = jnp.zeros_like(dwpg_ref)
            dwple_ref[...] = jnp.zeros_like(dwple_ref)
            vec_ref[...] = jnp.zeros_like(vec_ref)

        inv_hd = 1.0 / HEAD_DIM

        def head_fwd(o, g, gate):
            r = lax.rsqrt(_seg(o * o, bd_ref[...]) * inv_hd + EPS)
            hat = o * r
            n = hat * g
            sg = _sigmoid(gate)
            return hat, r, n, sg, n * (gate * sg)

        sbo, mlao, sbg_v, mlag_v = sbo_ref[...], mlao_ref[...], sbg_ref[...], mlag_ref[...]
        gsb_v, gmla_v = gsb_ref[...], gmla_ref[...]
        sb_hat, sb_r, sb_n, sb_sg, sb_y = head_fwd(sbo, gsb_v, sbg_v)
        ml_hat, ml_r, ml_n, ml_sg, ml_y = head_fwd(mlao, gmla_v, mlag_v)
        mix = jnp.concatenate([sb_y, ml_y], axis=1).astype(BF16)
        y = _mm(mix, wout_ref[...])
        ry = lax.rsqrt(_rowmean(y * y) + EPS)
        y_hat = y * ry
        gpost_v = gpost_ref[...]
        x1 = x_ref[...] + y_hat * gpost_v
        pb = p_ref[...].astype(BF16)
        pl_ = _mm(pb, wple_ref[...])
        rp = lax.rsqrt(_rowmean(pl_ * pl_) + EPS)
        pl_hat = pl_ * rp
        gple_v = gple_ref[...]
        ple = pl_hat * gple_v
        x1b = x1.astype(BF16)
        gate = _sigmoid(_mm(x1b, wpg_ref[...]) + bpg_ref[...])
        err = x1 + ple * gate - t_ref[...]
        loss = 0.5 * jnp.sum(_rowmean(err * err))
        dout = err * (1.0 / D_MODEL)

        du = dout * ple * gate * (1.0 - gate)
        dub = du.astype(BF16)
        dple = dout * gate
        dx1 = dout + _mm_nt(dub, wpg_ref[...])
        dwpg_ref[...] += _mm_tn(x1b, dub)
        dplh = dple * gple_v
        dpl = rp * (dplh - pl_hat * _rowmean(dplh * pl_hat))
        dwple_ref[...] += _mm_tn(pb, dpl.astype(BF16))
        dxres_ref[...] = dx1
        dyh = dx1 * gpost_v
        dy = ry * (dyh - y_hat * _rowmean(dyh * y_hat))
        dyb = dy.astype(BF16)
        dwout_ref[...] += _mm_tn(mix, dyb)
        dmix = _mm_nt(dyb, wout_ref[...])

        def head_bwd(dyv, hat, r, n, sg, g, gate):
            dn = dyv * (gate * sg)
            dgate = dyv * n * (sg * (1.0 + gate * (1.0 - sg)))
            dhat = dn * g
            do = r * (dhat - hat * (_seg(dhat * hat, bd_ref[...]) * inv_hd))
            return do, dgate, _colsum(dn * hat)

        dsbo, dsbg, dg_sb = head_bwd(dmix[:, 0:512], sb_hat, sb_r, sb_n, sb_sg, gsb_v, sbg_v)
        dmlao, dmlag, dg_ml = head_bwd(dmix[:, 512:1024], ml_hat, ml_r, ml_n, ml_sg, gmla_v, mlag_v)
        dsbo_ref[...] = dsbo.astype(BF16)
        dmlao_ref[...] = dmlao.astype(BF16)
        delta_ref[...] = _seg(dmlao * mlao, bd_ref[...])
        dsbg_ref[...] = dsbg.astype(BF16)
        dmlag_ref[...] = dmlag.astype(BF16)
        vec_ref[pl.ds(0, 1), :] += _colsum(dx1 * y_hat)
        vec_ref[pl.ds(1, 1), :] += _colsum(dple * pl_hat)
        vec_ref[pl.ds(2, 1), :] += _colsum(du)
        vec_ref[pl.ds(3, 1), :] += jnp.concatenate([dg_sb, dg_ml], axis=1)
        vec_ref[pl.ds(4, 1), :] += jnp.full((1, D_MODEL), loss, F32)

    out_shape = (
        jax.ShapeDtypeStruct((s, 512), BF16), jax.ShapeDtypeStruct((s, 512), BF16), jax.ShapeDtypeStruct((s, 512), F32),
        jax.ShapeDtypeStruct((s, 512), BF16), jax.ShapeDtypeStruct((s, 512), BF16), jax.ShapeDtypeStruct((s, D_MODEL), F32),
        jax.ShapeDtypeStruct((D_MODEL, D_MODEL), F32), jax.ShapeDtypeStruct((D_MODEL, D_MODEL), F32),
        jax.ShapeDtypeStruct((PLE_DIM, D_MODEL), F32), jax.ShapeDtypeStruct((8, D_MODEL), F32),
    )
    return pl.pallas_call(
        body, name="post_fwd_bwd", grid=(s // TM,), out_shape=out_shape,
        in_specs=[_rows(D_MODEL), _rows(PLE_DIM), _rows(D_MODEL), _rows(512), _rows(512), _rows(512), _rows(512),
                  _full((1, 512)), _full((1, 512)), _full((D_MODEL, D_MODEL)),
                  _full((1, D_MODEL)), _full((PLE_DIM, D_MODEL)), _full((1, D_MODEL)), _full((D_MODEL, D_MODEL)),
                  _full((1, D_MODEL)), _full((1024, 512))],
        out_specs=(_rows(512), _rows(512), _rows(512), _rows(512), _rows(512), _rows(D_MODEL),
                   _acc((D_MODEL, D_MODEL)), _acc((D_MODEL, D_MODEL)), _acc((PLE_DIM, D_MODEL)), _acc((8, D_MODEL))),
        compiler_params=pltpu.CompilerParams(vmem_limit_bytes=VMEM_DENSE),
    )(x, p, tgt, sbo, mlao, sbg, mlag, gsb, gmla, wout, gpost, wple, gple, wpg, bpg, _blockdiag2(512, HEAD_DIM))


def _pre_bwd(x, dxres, dsbq, dsbk, dsbv, dsbg, dmlag, dqc, dkc, dmv, cq, ckv, tabs, gpre, win, gq, wuq, gkv, wk, wv):
    s = x.shape[0]
    c_t, sa_t, sb_t = tabs
    rw = _rows

    def body(x_ref, dxres_ref, dsbq_ref, dsbk_ref, dsbv_ref, dsbg_ref, dmlag_ref, dqc_ref, dkc_ref, dmv_ref, cq_ref,
             ckv_ref, c_ref, sa_ref, sb_ref, gpre_ref, win_ref, gq_ref, wuq_ref, gkv_ref, wk_ref, wv_ref,
             gx_ref, dwin_ref, dwuq_ref, dwk_ref, dwv_ref, vec_ref, dwin_acc):
        i = pl.program_id(0)

        @pl.when(i == 0)
        def _():
            dwin_acc[...] = jnp.zeros_like(dwin_acc)
            dwuq_ref[...] = jnp.zeros_like(dwuq_ref)
            dwk_ref[...] = jnp.zeros_like(dwk_ref)
            dwv_ref[...] = jnp.zeros_like(dwv_ref)
            vec_ref[...] = jnp.zeros_like(vec_ref)

        lane = lax.broadcasted_iota(jnp.int32, (1, LANES), 1)
        c1, sa1, sb1 = c_ref[...], sa_ref[...], sb_ref[...]
        c8, sa8, sb8 = jnp.tile(c1, (1, 8)), jnp.tile(sa1, (1, 8)), jnp.tile(sb1, (1, 8))

        def norm_bwd(dn, hat, r, g):
            t = dn * g
            return r * (t - hat * _rowmean(t * hat)), _colsum(dn * hat)

        xv = x_ref[...]
        r1 = lax.rsqrt(_rowmean(xv * xv) + EPS)
        x_hat = xv * r1
        gpre_v = gpre_ref[...]
        hb = (x_hat * gpre_v).astype(BF16)
        ready = jnp.concatenate([dsbq_ref[...], dsbk_ref[...].astype(BF16), dsbv_ref[...].astype(BF16), dsbg_ref[...]], axis=1)
        dmlag = dmlag_ref[...]
        dwin_acc[:, 0:2048] += _mm_tn(hb, ready)
        dwin_acc[:, 2560:3072] += _mm_tn(hb, dmlag)
        dh = _mm_nt(ready, win_ref[:, 0:2048]) + _mm_nt(dmlag, win_ref[:, 2560:3072])

        dqeb = _rope_bwd(dqc_ref[...], c8, sa8, sb8).astype(BF16)
        cq = cq_ref[...]
        rq = lax.rsqrt(_rowmean(cq * cq) + EPS)
        cq_hat = cq * rq
        gq_v = gq_ref[...]
        dwuq_ref[...] += _mm_tn((cq_hat * gq_v).astype(BF16), dqeb)
        dcq, dg_q = norm_bwd(_mm_nt(dqeb, wuq_ref[...]), cq_hat, rq, gq_v)

        dkc = dkc_ref[...]
        dkcb = dkc.astype(BF16)
        dmvb = dmv_ref[...].astype(BF16)
        ckv = ckv_ref[...]
        rkv = lax.rsqrt(_rowmean(ckv * ckv) + EPS)
        ckv_hat = ckv * rkv
        gkv_v = gkv_ref[...]
        ckvnb = (ckv_hat * gkv_v).astype(BF16)
        dwk_ref[...] += _mm_tn(ckvnb, dkcb)
        dwv_ref[...] += _mm_tn(ckvnb, dmvb)
        dckv, dg_kv = norm_bwd(_mm_nt(dkcb, wk_ref[...]) + _mm_nt(dmvb, wv_ref[...]), ckv_hat, rkv, gkv_v)

        dkr = dkc[:, 0:LANES]
        for hh in range(1, 8):
            dkr = dkr + dkc[:, LANES * hh:LANES * (hh + 1)]
        dkr = _rope_bwd(dkr, c1, sa1, sb1)
        dkr = jnp.where((lane >= 64) & (lane < 96), dkr, 0.0)

        late = jnp.concatenate([dcq.astype(BF16), dckv.astype(BF16), dkr.astype(BF16)], axis=1)
        dwin_acc[:, 2048:2560] += _mm_tn(hb, late)
        dx, dg_pre = norm_bwd(dh + _mm_nt(late, win_ref[:, 2048:2560]), x_hat, r1, gpre_v)
        gx_ref[...] = dxres_ref[...] + dx
        vec_ref[pl.ds(0, 1), :] += dg_pre
        vec_ref[pl.ds(1, 1), :] += jnp.concatenate([dg_q, dg_kv, jnp.zeros((1, D_MODEL - Q_LORA - KV_LORA), F32)], axis=1)

        @pl.when(i == pl.num_programs(0) - 1)
        def _():
            pltpu.sync_copy(dwin_acc, dwin_ref)

    out_shape = (
        jax.ShapeDtypeStruct((s, D_MODEL), F32), jax.ShapeDtypeStruct((D_MODEL, D_EXT), F32),
        jax.ShapeDtypeStruct((Q_LORA, 1024), F32), jax.ShapeDtypeStruct((KV_LORA, 1024), F32),
        jax.ShapeDtypeStruct((KV_LORA, 512), F32), jax.ShapeDtypeStruct((8, D_MODEL), F32),
    )
    return pl.pallas_call(
        body, name="pre_bwd", grid=(s // TM,), out_shape=out_shape,
        in_specs=[rw(D_MODEL), rw(D_MODEL), rw(512), rw(512), rw(512), rw(512), rw(512),
                  rw(1024), rw(1024), rw(512), rw(Q_LORA), rw(KV_LORA), rw(LANES), rw(LANES),
                  rw(LANES), _full((1, D_MODEL)), _full((D_MODEL, D_EXT)), _full((1, Q_LORA)), _full((Q_LORA, 1024)),
                  _full((1, KV_LORA)), _full((KV_LORA, 1024)), _full((KV_LORA, 512))],
        out_specs=(rw(D_MODEL), pl.BlockSpec(memory_space=pl.ANY), _acc((Q_LORA, 1024)), _acc((KV_LORA, 1024)),
                   _acc((KV_LORA, 512)), _acc((8, D_MODEL))),
        scratch_shapes=[pltpu.VMEM((D_MODEL, D_EXT), F32)],
        compiler_params=pltpu.CompilerParams(vmem_limit_bytes=VMEM_DENSE),
    )(x, dxres, dsbq, dsbk, dsbv, dsbg, dmlag, dqc, dkc, dmv, cq, ckv, c_t, sa_t, sb_t, gpre, win, gq, wuq, gkv, wk, wv)


def _place():
    return lax.axis_index("x"), lax.axis_index("y"), lax.axis_index("c")


def _gather_steps(shapes, ins, bufs, send_sems, recv_sems):
    n = len(shapes)
    x, y, c = _place()
    me, sib = (x, y, c), (x, y, 1 - c)
    chips = [(1 - x, y), (x, 1 - y), (1 - x, 1 - y)]

    def half(t, chip, hc):
        rows = shapes[t][0] // 2
        return bufs[t].at[2 * chip[0] + chip[1], pl.ds(pl.multiple_of(hc * rows, 16), rows), :]

    def copy(k, t, chip, hc, to):
        return pltpu.make_async_remote_copy(src_ref=half(t, chip, hc), dst_ref=half(t, chip, hc), send_sem=send_sems.at[k],
                                            recv_sem=recv_sems.at[k], device_id=to, device_id_type=MESH)

    def start():
        for t in range(n):
            bufs[t][2 * x + y] = ins[t][...].astype(BF16)
            for j, chip in enumerate(chips):
                copy(6 * t + j, t, (x, y), c, (*chip, c)).start()

    def forward():
        for t in range(n):
            for j, chip in enumerate(chips):
                copy(6 * t + j, t, chip, c, me).wait_recv()
                copy(6 * t + 3 + j, t, chip, c, sib).start()

    def finish():
        for t in range(n):
            for j, chip in enumerate(chips):
                copy(6 * t + 3 + j, t, chip, 1 - c, me).wait_recv()
        for t in range(n):
            for j, chip in enumerate(chips):
                copy(6 * t + j, t, (x, y), c, (*chip, c)).wait_send()
                copy(6 * t + 3 + j, t, chip, c, sib).wait_send()

    return start, forward, finish


def _allgather_weights(shards):
    n = len(shards)

    def body(*refs):
        start, forward, finish = _gather_steps([a.shape for a in shards], refs[:n], refs[n:2 * n], refs[2 * n], refs[2 * n + 1])
        start()
        forward()
        finish()

    return pl.pallas_call(
        body, name="allgather_weights",
        out_shape=tuple(jax.ShapeDtypeStruct((N_SHARD,) + a.shape, BF16) for a in shards),
        in_specs=[pl.BlockSpec(memory_space=pltpu.VMEM)] * n, out_specs=(pl.BlockSpec(memory_space=pltpu.VMEM),) * n,
        scratch_shapes=[pltpu.SemaphoreType.DMA((6 * n,)), pltpu.SemaphoreType.DMA((6 * n,))],
        compiler_params=pltpu.CompilerParams(vmem_limit_bytes=VMEM_ATTN),
    )(*shards)


def _reduce_scratch(gsh):
    n = len(gsh)
    half_shapes = [(N_SHARD, a.shape[1] // 2, a.shape[2]) for a in gsh]
    return ([pltpu.VMEM(s_, F32) for s_ in half_shapes] * 2 + [pltpu.VMEM(s_, BF16) for s_ in half_shapes] * 2
            + [pltpu.SemaphoreType.DMA((n,)), pltpu.SemaphoreType.DMA((5 * n,)), pltpu.SemaphoreType.DMA((5 * n,))])


def _reduce_steps(halves, g_refs, f_refs, scratch):
    n = len(halves)
    accs, sibs, sbufs, rbufs = scratch[0:n], scratch[n:2 * n], scratch[2 * n:3 * n], scratch[3 * n:4 * n]
    local_sems, send_sems, recv_sems = scratch[4 * n:4 * n + 3]
    x, y, c = _place()
    me, sib = (x, y, c), (x, y, 1 - c)
    mine = 2 * x + y
    chips = [(1 - x, y), (x, 1 - y), (1 - x, 1 - y)]

    def remote(k, src, dst, to):
        return pltpu.make_async_remote_copy(src_ref=src, dst_ref=dst, send_sem=send_sems.at[k], recv_sem=recv_sems.at[k],
                                            device_id=to, device_id_type=MESH)

    def half3(ref, t, hc):
        return ref.at[:, pl.ds(pl.multiple_of(hc * halves[t], 8), halves[t]), :]

    def half2(ref, t, hc):
        return ref.at[pl.ds(pl.multiple_of(hc * halves[t], 8), halves[t]), :]

    def mine_load(t):
        return pltpu.make_async_copy(half3(g_refs[t], t, c), accs[t], local_sems.at[t])

    def to_sibling(t, to):
        return remote(t, half3(g_refs[t], t, 1 - c), sibs[t], to)

    def to_chip(t, j, chip, to):
        idx = 2 * chip[0] + chip[1]
        return remote(n + 3 * t + j, sbufs[t].at[idx], rbufs[t].at[mine if to is not me else idx], to)

    def swap(t, hc, to):
        return remote(4 * n + t, half2(f_refs[t], t, hc), half2(f_refs[t], t, hc), to)

    def load():
        for t in range(n):
            mine_load(t).start()
            to_sibling(t, sib).start()

    def partial():
        for t in range(n):
            mine_load(t).wait()
            to_sibling(t, me).wait_recv()
            for k in range(N_SHARD):
                accs[t][k] = accs[t][k] + sibs[t][k]
            for j, chip in enumerate(chips):
                idx = 2 * chip[0] + chip[1]
                sbufs[t][idx] = accs[t][idx].astype(BF16)
                to_chip(t, j, chip, (*chip, c)).start()

    def total():
        for t in range(n):
            acc = accs[t][mine]
            for j, chip in enumerate(chips):
                to_chip(t, j, chip, me).wait_recv()
                acc = acc + rbufs[t][2 * chip[0] + chip[1]].astype(F32)
            half2(f_refs[t], t, c)[...] = acc
            swap(t, c, sib).start()

    def finish():
        for t in range(n):
            swap(t, 1 - c, me).wait_recv()
        for t in range(n):
            to_sibling(t, sib).wait_send()
            for j, chip in enumerate(chips):
                to_chip(t, j, chip, (*chip, c)).wait_send()
            swap(t, c, sib).wait_send()

    return load, partial, total, finish


def _reduce_scatter_grads(gsh, vec):
    n = len(gsh)
    halves = [a.shape[1] // 2 for a in gsh]

    def body(*refs):
        g_refs, vec_ref, f_refs, vsum_ref = refs[:n], refs[n], refs[n + 1:2 * n + 1], refs[2 * n + 1]
        scratch = refs[2 * n + 2:]
        vrecv, vsend_sems, vrecv_sems = scratch[4 * n + 3:]
        load, partial, total, finish = _reduce_steps(halves, g_refs, f_refs, scratch)
        x, y, c = _place()
        my_dev = 4 * x + 2 * y + c

        def flip(k):
            return x ^ ((k >> 2) & 1), y ^ ((k >> 1) & 1), c ^ (k & 1)

        def vcopy(k, slot, to):
            return pltpu.make_async_remote_copy(src_ref=vec_ref, dst_ref=vrecv.at[slot], send_sem=vsend_sems.at[k - 1],
                                                recv_sem=vrecv_sems.at[k - 1], device_id=to, device_id_type=MESH)

        load()
        vrecv[my_dev] = vec_ref[...]
        for k in range(1, 8):
            vcopy(k, my_dev, flip(k)).start()
        partial()
        total()
        finish()
        for k in range(1, 8):
            fx, fy, fc = flip(k)
            vcopy(k, 4 * fx + 2 * fy + fc, (x, y, c)).wait_recv()
        vs = vrecv[0]
        for d in range(1, 8):
            vs = vs + vrecv[d]
        vsum_ref[...] = vs
        for k in range(1, 8):
            vcopy(k, my_dev, flip(k)).wait_send()

    return pl.pallas_call(
        body, name="reduce_scatter_grads",
        out_shape=tuple(jax.ShapeDtypeStruct(a.shape[1:], F32) for a in gsh) + (jax.ShapeDtypeStruct((VEC_ROWS, 1024), F32),),
        in_specs=[pl.BlockSpec(memory_space=pl.ANY)] * n + [pl.BlockSpec(memory_space=pltpu.VMEM)],
        out_specs=(pl.BlockSpec(memory_space=pltpu.VMEM),) * (n + 1),
        scratch_shapes=_reduce_scratch(gsh) + [pltpu.VMEM((8, VEC_ROWS, 1024), F32), pltpu.SemaphoreType.DMA((7,)),
                                               pltpu.SemaphoreType.DMA((7,))],
        compiler_params=pltpu.CompilerParams(vmem_limit_bytes=56 * 1024 * 1024),
    )(*gsh, vec)


def _adamw(w, g, m, v):
    rows, cols = w.shape
    tr = rows if rows <= 256 else 256
    flip = cols % LANES != 0

    def body(w_ref, g_ref, m_ref, v_ref, g_out, d_ref, nm_ref, nv_ref):
        gv = g_ref[...]
        outs = (gv,) + _adam_math(w_ref[...], gv, m_ref[...], v_ref[...])
        for ref, val in zip((g_out, d_ref, nm_ref, nv_ref), outs):
            ref[...] = val.T if flip else val

    spec = pl.BlockSpec((tr, cols), lambda i: (i, 0))
    ospec = pl.BlockSpec((cols, tr), lambda i: (0, i)) if flip else spec
    shp = jax.ShapeDtypeStruct((cols, rows) if flip else (rows, cols), F32)
    outs = pl.pallas_call(body, name="adamw", grid=(rows // tr,), out_shape=(shp,) * 4,
                          in_specs=[spec] * 4, out_specs=(ospec,) * 4)(w, g, m, v)
    return tuple(o.T for o in outs) if flip else outs


def _adam_math(w, g, m, v):
    m2 = ADAM_B1 * m + (1.0 - ADAM_B1) * g
    v2 = ADAM_B2 * v + (1.0 - ADAM_B2) * (g * g)
    m_hat = m2 / (1.0 - ADAM_B1 ** ADAM_STEP)
    v_hat = v2 / (1.0 - ADAM_B2 ** ADAM_STEP)
    return -ADAM_LR * (m_hat / (jnp.sqrt(v_hat) + ADAM_EPS) + ADAM_WD * w), m2, v2


def _adamw_small(vsum, w, m, v):
    names = [name for name, _, _, _ in _VEC_LAYOUT]
    k = len(names)

    def body(*refs):
        vs_ref, w_refs, m_refs, v_refs = refs[0], refs[1:1 + k], refs[1 + k:1 + 2 * k], refs[1 + 2 * k:1 + 3 * k]
        outs = refs[1 + 3 * k:]
        for idx, (_, r, c0, width) in enumerate(_VEC_LAYOUT):
            gv = vs_ref[pl.ds(r, 1), pl.ds(c0, width)]
            d, m2, v2 = _adam_math(w_refs[idx][...], gv, m_refs[idx][...], v_refs[idx][...])
            outs[idx][...], outs[k + idx][...], outs[2 * k + idx][...], outs[3 * k + idx][...] = gv, d, m2, v2

    shapes = tuple(jax.ShapeDtypeStruct(w[name].shape, F32) for name in names)
    res = pl.pallas_call(
        body, name="adamw_small", out_shape=shapes * 4,
        in_specs=[pl.BlockSpec(memory_space=pltpu.VMEM)] * (1 + 3 * k), out_specs=(pl.BlockSpec(memory_space=pltpu.VMEM),) * (4 * k),
    )(vsum, *[w[name] for name in names], *[m[name] for name in names], *[v[name] for name in names])
    return tuple({name: res[part * k + idx] for idx, name in enumerate(names)} for part in range(4))


_EARLY = ("w_in", "w_uq", "w_ukv")
_LATE = ("w_out", "w_ple", "w_ple_gate")
_BIG = _EARLY + _LATE
_KR_LOCAL = 2432 - 3 * (D_IN // N_SHARD)


def _extend_early(parts):
    cols = lambda a: a.transpose(1, 0, 2).reshape(a.shape[1], N_SHARD * a.shape[2])
    g = parts["w_in"]
    zeros = lambda n: jnp.zeros((D_MODEL, n), g.dtype)
    win_ext = jnp.concatenate([g[0], g[1], g[2], g[3][:, :_KR_LOCAL], zeros(64), g[3][:, _KR_LOCAL:_KR_LOCAL + QK_ROPE],
                               zeros(32), g[3][:, _KR_LOCAL + QK_ROPE:]], axis=1)
    wuq_ext = jnp.pad(cols(parts["w_uq"]).reshape(Q_LORA, 8, 96), ((0, 0), (0, 0), (0, 32))).reshape(Q_LORA, 1024)
    wukv = cols(parts["w_ukv"]).reshape(KV_LORA, 8, 128)
    wk_ext = jnp.pad(wukv[:, :, :64], ((0, 0), (0, 0), (0, 64))).reshape(KV_LORA, 1024)
    wv = wukv[:, :, 64:].reshape(KV_LORA, 512)
    return win_ext, wuq_ext, wk_ext, wv


def _shard_cols(a):
    return a.reshape(a.shape[0], N_SHARD, a.shape[1] // N_SHARD).transpose(1, 0, 2)


def _shard_rows(a):
    return a.reshape(N_SHARD, a.shape[0] // N_SHARD, a.shape[1])


def _shard_early_grads(dwin_ext, dwuq_ext, dwk_ext, dwv):
    e, w = dwin_ext, D_IN // N_SHARD
    last = jnp.concatenate([e[:, 3 * w:2432], e[:, 2496:2528], e[:, 2560:]], axis=1)
    dwuq = dwuq_ext.reshape(Q_LORA, 8, 128)[:, :, :96].reshape(Q_LORA, 768)
    dwukv = jnp.concatenate([dwk_ext.reshape(KV_LORA, 8, 128)[:, :, :64], dwv.reshape(KV_LORA, 8, 64)], axis=2)
    return [jnp.stack([e[:, 0:w], e[:, w:2 * w], e[:, 2 * w:3 * w], last]), _shard_cols(dwuq),
            _shard_cols(dwukv.reshape(KV_LORA, 1024))]


def _rope_tables(positions):
    half = QK_ROPE // 2
    freq = ROPE_THETA ** (-jnp.arange(half, dtype=F32) / half)
    ang = positions.astype(F32)[:, None] * freq
    cos, sin = jnp.cos(ang), jnp.sin(ang)
    s = positions.shape[0]
    z = lambda n: jnp.zeros((s, n), F32)
    c_t = jnp.concatenate([jnp.ones((s, 64), F32), cos, cos, z(32)], axis=1)
    sa_t = jnp.concatenate([z(64), -sin, z(16), z(32)], axis=1)
    sb_t = jnp.concatenate([z(64), z(16), sin, z(32)], axis=1)
    return c_t, sa_t, sb_t


def _local_grads(x, p, positions, tgt, gains, early, late):
    win_ext, wuq_ext, wk_ext, wv = _extend_early(early)
    tabs = _rope_tables(positions)
    g = gains
    sbq, sbk, sbv, sbg, mlag, cq, ckv, qc, kc, mv, sbkt, sbvt, kct, mvt = _pre_fwd(
        x, tabs, g["norm_pre_g"], win_ext, g["q_norm_g"], wuq_ext, g["kv_norm_g"], wk_ext, wv)
    sbo, wout4, wple4, wpg4 = _sb_fwd(sbq, sbk, sbvt, late)
    wout, wpg = wout4.reshape(D_MODEL, D_MODEL), wpg4.reshape(D_MODEL, D_MODEL)
    wple = wple4.transpose(1, 0, 2).reshape(PLE_DIM, D_MODEL)
    mlao, lse = _mla_fwd(qc, kc, mvt)
    dsbo, dmlao, delta, dsbg, dmlag, dxres, dwout, dwpg, dwple, vec_c = _post(
        x, p, tgt, sbo, mlao, sbg, mlag, g["sb_out_norm_g"], g["mla_out_norm_g"], wout, g["norm_post_g"], wple,
        g["ple_norm_g"], wpg, g["b_ple_gate"])
    dsbq, dsbk, dsbv, *late_grads = _sb_bwd(sbq, sbk, sbkt, sbv, dsbo, [_shard_rows(dwout), _shard_cols(dwple), _shard_rows(dwpg)])
    dqc, dkc, dmv = _mla_bwd(qc, kc, kct, mv, dmlao, lse, delta)
    gx, dwin_ext, dwuq_ext, dwk_ext, dwv, vec_d = _pre_bwd(
        x, dxres, dsbq, dsbk, dsbv, dsbg, dmlag, dqc, dkc, dmv, cq, ckv, tabs, g["norm_pre_g"], win_ext, g["q_norm_g"],
        wuq_ext, g["kv_norm_g"], wk_ext, wv)
    return gx, _shard_early_grads(dwin_ext, dwuq_ext, dwk_ext, dwv), late_grads, jnp.concatenate([vec_c, vec_d], axis=0)


_VEC_LAYOUT = (("norm_post_g", 0, 0, 1024), ("ple_norm_g", 1, 0, 1024), ("b_ple_gate", 2, 0, 1024), ("sb_out_norm_g", 3, 0, 512),
               ("mla_out_norm_g", 3, 512, 512), ("norm_pre_g", 8, 0, 1024), ("q_norm_g", 9, 0, 256), ("kv_norm_g", 9, 256, 128))
_LOSS_ROW = 4
_WEIGHT_ORDER = ("norm_pre_g", "w_in", "q_norm_g", "w_uq", "kv_norm_g", "w_ukv", "sb_out_norm_g", "mla_out_norm_g", "w_out",
                 "norm_post_g", "w_ple", "ple_norm_g", "w_ple_gate", "b_ple_gate")


def kernel(x, p, positions, norm_pre_g, w_in, q_norm_g, w_uq, kv_norm_g, w_ukv, sb_out_norm_g, mla_out_norm_g, w_out, norm_post_g, w_ple, ple_norm_g, w_ple_gate, b_ple_gate, loss_target, m_norm_pre_g, m_w_in, m_q_norm_g, m_w_uq, m_kv_norm_g, m_w_ukv, m_sb_out_norm_g, m_mla_out_norm_g, m_w_out, m_norm_post_g, m_w_ple, m_ple_norm_g, m_w_ple_gate, m_b_ple_gate, v_norm_pre_g, v_w_in, v_q_norm_g, v_w_uq, v_kv_norm_g, v_w_ukv, v_sb_out_norm_g, v_mla_out_norm_g, v_w_out, v_norm_post_g, v_w_ple, v_ple_norm_g, v_w_ple_gate, v_b_ple_gate):
    w = {"norm_pre_g": norm_pre_g, "w_in": w_in[0], "q_norm_g": q_norm_g, "w_uq": w_uq[0], "kv_norm_g": kv_norm_g, "w_ukv": w_ukv[0],
         "sb_out_norm_g": sb_out_norm_g, "mla_out_norm_g": mla_out_norm_g, "w_out": w_out[0], "norm_post_g": norm_post_g,
         "w_ple": w_ple[0], "ple_norm_g": ple_norm_g, "w_ple_gate": w_ple_gate[0], "b_ple_gate": b_ple_gate}
    m = {"norm_pre_g": m_norm_pre_g, "w_in": m_w_in[0], "q_norm_g": m_q_norm_g, "w_uq": m_w_uq[0], "kv_norm_g": m_kv_norm_g,
         "w_ukv": m_w_ukv[0], "sb_out_norm_g": m_sb_out_norm_g, "mla_out_norm_g": m_mla_out_norm_g, "w_out": m_w_out[0],
         "norm_post_g": m_norm_post_g, "w_ple": m_w_ple[0], "ple_norm_g": m_ple_norm_g, "w_ple_gate": m_w_ple_gate[0],
         "b_ple_gate": m_b_ple_gate}
    v = {"norm_pre_g": v_norm_pre_g, "w_in": v_w_in[0], "q_norm_g": v_q_norm_g, "w_uq": v_w_uq[0], "kv_norm_g": v_kv_norm_g,
         "w_ukv": v_w_ukv[0], "sb_out_norm_g": v_sb_out_norm_g, "mla_out_norm_g": v_mla_out_norm_g, "w_out": v_w_out[0],
         "norm_post_g": v_norm_post_g, "w_ple": v_w_ple[0], "ple_norm_g": v_ple_norm_g, "w_ple_gate": v_w_ple_gate[0],
         "b_ple_gate": v_b_ple_gate}
    gathered = _allgather_weights([w[n] for n in _EARLY])
    gx, early_grads, late_red, vec = _local_grads(x[0], p[0, 0], positions[0], loss_target[0], w, dict(zip(_EARLY, gathered)),
                                                  [w[n] for n in _LATE])
    *early_red, vsum = _reduce_scatter_grads(early_grads, vec)
    gred = early_red + late_red
    loss = vsum[_LOSS_ROW, 0]

    g, delta, new_m, new_v = _adamw_small(vsum, w, m, v)
    for n, gn in zip(_BIG, gred):
        g[n], delta[n], new_m[n], new_v[n] = _adamw(w[n], gn, m[n], v[n])

    lead = lambda n, a: a[None] if n in _BIG else a
    return (loss, gx[None],
            *[lead(n, g[n]) for n in _WEIGHT_ORDER], *[lead(n, delta[n]) for n in _WEIGHT_ORDER],
            *[lead(n, new_m[n]) for n in _WEIGHT_ORDER], *[lead(n, new_v[n]) for n in _WEIGHT_ORDER])
```

```python
import numpy as np
import jax
import jax.numpy as jnp
from jax import lax
from jax.experimental import pallas as pl
from jax.experimental.pallas import tpu as pltpu

F32 = jnp.float32
BF16 = jnp.bfloat16
MESH = pl.DeviceIdType.MESH

D_MODEL = 1024
HEAD_DIM = 64
D_SB = 512
D_MLA = 512
Q_LORA = 256
KV_LORA = 128
QK_NOPE = 64
QK_ROPE = 32
PLE_DIM = 256
D_IN = 2976
D_EXT = 3072
ROPE_THETA = 10000.0
EPS = 1e-6
N_SHARD = 4

ADAM_LR = 0.001
ADAM_B1 = 0.9
ADAM_B2 = 0.999
ADAM_EPS = 1e-08
ADAM_WD = 0.01
ADAM_STEP = 10

LANES = 128
BK = 128
WQ = 256
MQ_FWD = 4096
MQ_BWD = 1024
SB_CUTOFF = 120.0
TM = 256
TM_PRE = 256
VEC_ROWS = 16
VMEM_DENSE = 52 * 1024 * 1024
VMEM_ATTN = 40 * 1024 * 1024


def _mm(a, b):
    return jnp.dot(a, b, preferred_element_type=F32)


def _mm_nt(a, b):
    return lax.dot_general(a, b, (((1,), (1,)), ((), ())), preferred_element_type=F32)


def _mm_tn(a, b):
    return lax.dot_general(a, b, (((0,), (0,)), ((), ())), preferred_element_type=F32)


def _seg(a, bd2):
    return _mm(_split2(a), bd2)


def _const(mask):
    return jnp.asarray(np.asarray(mask, np.float32), dtype=BF16)


def _blockdiag2(n, seg):
    r = (np.arange(2 * n)[:, None] % n) // seg
    c = np.arange(n)[None, :] // seg
    return _const(r == c)


def _sigmoid(a):
    return 1.0 / (1.0 + jnp.exp(-a))


def _rowmean(a):
    return jnp.mean(a, axis=-1, keepdims=True)


def _colsum(a):
    return jnp.sum(a, axis=0, keepdims=True)


def _rope_fwd(a, c, sa, sb):
    w = a.shape[-1]
    return a * c + pltpu.roll(a, w - 16, 1) * sa + pltpu.roll(a, 16, 1) * sb


def _rope_bwd(g, c, sa, sb):
    w = g.shape[-1]
    return g * c + pltpu.roll(g * sa, 16, 1) + pltpu.roll(g * sb, w - 16, 1)


def _full(shape):
    return pl.BlockSpec(shape, lambda *_: (0,) * len(shape))


def _acc(shape):
    return pl.BlockSpec(shape, lambda *_: (0,) * len(shape))


def _full2(shape):
    return pl.BlockSpec(shape, lambda p, i: (0, 0))


def _cols(height, tm=TM):
    return pl.BlockSpec((height, tm), lambda i: (0, i))


def _rows(width, tm=TM):
    return pl.BlockSpec((tm, width), lambda i: (i, 0))


def _pre_fwd(x, tabs, gpre, win, gq, wuq, gkv, wk, wv):
    s = x.shape[0]
    c_t, sa_t, sb_t = tabs
    rw, cl = (lambda width: _rows(width, TM_PRE)), (lambda height: _cols(height, TM_PRE))

    def body(x_ref, c_ref, sa_ref, sb_ref, gpre_ref, win_ref, gq_ref, wuq_ref, gkv_ref, wk_ref, wv_ref,
             sbq_ref, sbk_ref, sbv_ref, sbg_ref, mlag_ref, cq_ref, ckv_ref, qc_ref, kc_ref, mv_ref,
             sbkt_ref, sbvt_ref, kct_ref, mvt_ref):
        xv = x_ref[...]
        r1 = lax.rsqrt(_rowmean(xv * xv) + EPS)
        h = (xv * r1 * gpre_ref[...]).astype(BF16)
        proj = _mm(h, win_ref[...])
        sbq_ref[...] = proj[:, 0:512].astype(BF16)
        sbk_ref[...] = proj[:, 512:1024].astype(BF16)
        sbv_ref[...] = proj[:, 1024:1536].astype(BF16)
        sbkt_ref[...] = proj[:, 512:1024].T.astype(BF16)
        sbvt_ref[...] = proj[:, 1024:1536].T.astype(BF16)
        sbg_ref[...] = proj[:, 1536:2048]
        cq = proj[:, 2048:2304]
        ckv = proj[:, 2304:2432]
        kr = proj[:, 2432:2560]
        mlag_ref[...] = proj[:, 2560:3072]
        cq_ref[...] = cq
        ckv_ref[...] = ckv
        c1, sa1, sb1 = c_ref[...], sa_ref[...], sb_ref[...]
        c8, sa8, sb8 = jnp.tile(c1, (1, 8)), jnp.tile(sa1, (1, 8)), jnp.tile(sb1, (1, 8))
        cqn = (cq * lax.rsqrt(_rowmean(cq * cq) + EPS) * gq_ref[...]).astype(BF16)
        qe = _mm(cqn, wuq_ref[...])
        qc_ref[...] = _rope_fwd(qe, c8, sa8, sb8).astype(BF16)
        ckvn = (ckv * lax.rsqrt(_rowmean(ckv * ckv) + EPS) * gkv_ref[...]).astype(BF16)
        ke = _mm(ckvn, wk_ref[...])
        krr = _rope_fwd(kr, c1, sa1, sb1)
        kcat = ke + jnp.tile(krr, (1, 8))
        kc_ref[...] = kcat.astype(BF16)
        kct_ref[...] = kcat.T.astype(BF16)
        mval = _mm(ckvn, wv_ref[...])
        mv_ref[...] = mval.astype(BF16)
        mvt_ref[...] = mval.T.astype(BF16)

    out_shape = (
        jax.ShapeDtypeStruct((s, 512), BF16), jax.ShapeDtypeStruct((s, 512), BF16), jax.ShapeDtypeStruct((s, 512), BF16),
        jax.ShapeDtypeStruct((s, 512), F32), jax.ShapeDtypeStruct((s, 512), F32),
        jax.ShapeDtypeStruct((s, Q_LORA), F32), jax.ShapeDtypeStruct((s, KV_LORA), F32),
        jax.ShapeDtypeStruct((s, 1024), BF16), jax.ShapeDtypeStruct((s, 1024), BF16), jax.ShapeDtypeStruct((s, 512), BF16),
        jax.ShapeDtypeStruct((512, s), BF16), jax.ShapeDtypeStruct((512, s), BF16), jax.ShapeDtypeStruct((1024, s), BF16),
        jax.ShapeDtypeStruct((512, s), BF16),
    )
    return pl.pallas_call(
        body, name="pre_fwd", grid=(s // TM_PRE,), out_shape=out_shape,
        in_specs=[rw(D_MODEL), rw(LANES), rw(LANES), rw(LANES), _full((1, D_MODEL)), _full((D_MODEL, D_EXT)),
                  _full((1, Q_LORA)), _full((Q_LORA, 1024)), _full((1, KV_LORA)), _full((KV_LORA, 1024)), _full((KV_LORA, 512))],
        out_specs=(rw(512), rw(512), rw(512), rw(512), rw(512), rw(Q_LORA), rw(KV_LORA),
                   rw(1024), rw(1024), rw(512), cl(512), cl(512), cl(1024), cl(512)),
        compiler_params=pltpu.CompilerParams(vmem_limit_bytes=VMEM_DENSE),
    )(x, c_t, sa_t, sb_t, gpre, win, gq, wuq, gkv, wk, wv)


def _softplus(z):
    neg_abs = lax.bitcast_convert_type(lax.bitcast_convert_type(z, jnp.uint32) | jnp.uint32(0x80000000), F32)
    return jnp.maximum(z, 0.0) + jnp.log(1.0 + jnp.exp(neg_abs))


def _sum_matrix(kind, terms):
    r, c = np.arange(2 * BK)[:, None], np.arange(2 * BK * terms)[None, :] % (2 * BK)
    rk, ck = r % BK, c % BK
    return _const(((r // BK) == (c // BK)) & {"suffix": ck >= rk, "prefix": ck <= rk}[kind])


def _split_rows(a):
    hi = a.astype(BF16)
    return jnp.concatenate([hi, (a - hi.astype(F32)).astype(BF16)], axis=0)


def _heads_t(blk, rowi):
    zero = jnp.zeros_like(blk)
    return jnp.concatenate([jnp.where(rowi < 64, blk, zero), jnp.where(rowi >= 64, blk, zero)], axis=1)


def _mask_keys(a, valid, fill=0.0):
    return jnp.concatenate([jnp.where(valid, a[0:BK], fill), jnp.where(valid, a[BK:2 * BK], fill)], axis=0)


def _split2(a):
    hi = a.astype(BF16)
    lo = (a - hi.astype(F32)).astype(BF16)
    return jnp.concatenate([hi, lo], axis=1)


def _pair_stack(b, lane):
    zero = jnp.zeros_like(b)
    return jnp.concatenate([jnp.where(lane < 64, b, zero), jnp.where(lane >= 64, b, zero)], axis=0)


def _sb_fwd(q, k, vt, late):
    s = q.shape[0]
    n = len(late)

    def body(q_ref, k_ref, vt_ref, usuf_ref, *rest):
        ins, o_ref, outs = rest[:n], rest[n], rest[n + 1:2 * n + 1]
        acc_scr, run_scr = rest[2 * n + 1:2 * n + 3]
        bufs, (send_sems, recv_sems, out_sems) = rest[2 * n + 3:3 * n + 3], rest[3 * n + 3:]
        p, i = pl.program_id(0), pl.program_id(1)
        gather_start, gather_forward, gather_finish = _gather_steps([a.shape for a in late], ins, bufs, send_sems, recv_sems)

        @pl.when((p == 0) & (i == 0))
        def _():
            gather_start()

        @pl.when((p == 2) & (i == 0))
        def _():
            gather_forward()

        lane = lax.broadcasted_iota(jnp.int32, (1, LANES), 1)
        rowi = lax.broadcasted_iota(jnp.int32, (LANES, 1), 0)
        keyi = lax.broadcasted_iota(jnp.int32, (BK, WQ), 0)
        qryi = lax.broadcasted_iota(jnp.int32, (BK, WQ), 1) + i * WQ
        qs = q_ref[...] * (HEAD_DIM ** -0.5)

        def group(blocks, masked, seen=None):
            seen = seen or [0] * len(blocks)
            starts = [pl.multiple_of(j * BK, BK) for j in blocks]
            valid = [(keyi[:, lo:] + j * BK) < qryi[:, lo:] if m else None for j, m, lo in zip(blocks, masked, seen)]
            zs = [_mm_nt(_pair_stack(k_ref[pl.ds(ks, BK), :], lane), qs[lo:]) for ks, lo in zip(starts, seen)]
            sps = [_softplus(z) for z in zs]
            sps = [sp if ok is None else _mask_keys(sp, ok) for sp, ok in zip(sps, valid)]
            cums = [_mm(usuf_ref[...], _split_rows(sp)) for sp in sps]
            ws = [jnp.exp(z - c) for z, c in zip(zs, cums)]
            ws = [w if ok is None else _mask_keys(w, ok) for w, ok in zip(ws, valid)]
            pvs = [_mm(_heads_t(vt_ref[:, pl.ds(ks, BK)], rowi), w.astype(BF16)) for ks, w in zip(starts, ws)]
            for pv, c, lo in zip(pvs, cums, seen):
                r0, r1 = run_scr[0:1, lo:], run_scr[1:2, lo:]
                acc_scr[:, lo:] += jnp.where(rowi < 64, jnp.exp(-r0), jnp.exp(-r1)) * pv
                run_scr[0:1, lo:] = r0 + c[0:1]
                run_scr[1:2, lo:] = r1 + c[BK:BK + 1]

        assert WQ == 2 * BK
        acc_scr[...] = jnp.zeros_like(acc_scr)
        run_scr[...] = jnp.zeros_like(run_scr)

        @pl.when(i == 0)
        def _():
            group([1, 0], [True, True], [BK, 0])

        @pl.when(i > 0)
        def _():
            group([2 * i + 1, 2 * i, 2 * i - 1, 2 * i - 2], [True, True, False, False], [BK, 0, 0, 0])

        def unfinished():
            return (jnp.min(run_scr[0:2, :]) < SB_CUTOFF).astype(jnp.int32)

        def step(c):
            group([2 * i - 1 - 2 * c[0], 2 * i - 2 - 2 * c[0]], [False, False])
            return c[0] + 1, unfinished()

        lax.while_loop(lambda c: (c[0] < i) & (c[1] > 0), step, (jnp.int32(1), unfinished()))
        o_ref[...] = acc_scr[...].T

        @pl.when((p == pl.num_programs(0) - 1) & (i == pl.num_programs(1) - 1))
        def _():
            gather_finish()
            copies = [pltpu.make_async_copy(bufs[t], outs[t], out_sems.at[t]) for t in range(n)]
            for cp in copies:
                cp.start()
            for cp in copies:
                cp.wait()

    qspec = pl.BlockSpec((WQ, LANES), lambda p, i: (i, p))
    kspec = pl.BlockSpec((s, LANES), lambda p, i: (0, p))
    tspec = pl.BlockSpec((LANES, s), lambda p, i: (p, 0))
    gathered = [jax.ShapeDtypeStruct((N_SHARD,) + a.shape, BF16) for a in late]
    return pl.pallas_call(
        body, name="sb_fwd", grid=(4, s // WQ),
        out_shape=(jax.ShapeDtypeStruct((s, 512), F32), *gathered),
        in_specs=[qspec, kspec, tspec, _full2((2 * BK, 4 * BK))] + [_full2(a.shape) for a in late],
        out_specs=(qspec,) + (pl.BlockSpec(memory_space=pl.ANY),) * n,
        scratch_shapes=[pltpu.VMEM((LANES, WQ), F32), pltpu.VMEM((8, WQ), F32)] + [pltpu.VMEM(g.shape, BF16) for g in gathered]
                       + [pltpu.SemaphoreType.DMA((6 * n,)), pltpu.SemaphoreType.DMA((6 * n,)), pltpu.SemaphoreType.DMA((n,))],
        compiler_params=pltpu.CompilerParams(vmem_limit_bytes=VMEM_ATTN),
    )(q, k, vt, _sum_matrix("suffix", 2), *late)


def _sb_bwd(q, k, kt, v, do, late):
    s = q.shape[0]
    n = len(late)
    halves = [a.shape[1] // 2 for a in late]

    def body(q_ref, k_ref, kt_ref, v_ref, do_ref, usuf_ref, upre_ref, *rest):
        g_refs, (dq_ref, dk_ref, dv_ref), outs = rest[:n], rest[n:n + 3], rest[n + 3:2 * n + 3]
        later_scr, dqt_scr, st_scr = rest[2 * n + 3:2 * n + 6]
        f_scr, reduce_scr, out_sems = rest[2 * n + 6:3 * n + 6], rest[3 * n + 6:-1], rest[-1]
        p, i = pl.program_id(0), pl.program_id(1)
        reduce_load, reduce_partial, reduce_total, reduce_finish = _reduce_steps(halves, g_refs, f_scr, reduce_scr)

        @pl.when((p == 0) & (i == 0))
        def _():
            reduce_load()

        @pl.when((p == 1) & (i == 0))
        def _():
            reduce_partial()

        @pl.when((p == 3) & (i == 0))
        def _():
            reduce_total()

        @pl.when(i == 0)
        def _():
            dk_ref[...] = jnp.zeros_like(dk_ref)
            dv_ref[...] = jnp.zeros_like(dv_ref)

        lane = lax.broadcasted_iota(jnp.int32, (1, LANES), 1)
        rowi = lax.broadcasted_iota(jnp.int32, (LANES, 1), 0)
        keyi = lax.broadcasted_iota(jnp.int32, (BK, WQ), 0)
        qryi = lax.broadcasted_iota(jnp.int32, (BK, WQ), 1) + i * WQ
        qs = q_ref[...] * (HEAD_DIM ** -0.5)
        dob = do_ref[...]
        dot = dob.astype(F32).T.astype(BF16)

        def scores(j, lo=0):
            return _mm_nt(_pair_stack(k_ref[pl.ds(pl.multiple_of(j * BK, BK), BK), :], lane), qs[lo:])

        def scan(blocks, masked, seen=None):
            seen = seen or [0] * len(blocks)
            sps = [_softplus(scores(j, lo)) for j, lo in zip(blocks, seen)]
            sps = [_mask_keys(sp, (keyi[:, lo:] + j * BK) < qryi[:, lo:]) if m else sp
                   for sp, j, m, lo in zip(sps, blocks, masked, seen)]
            for sp, j, lo in zip(sps, blocks, seen):
                run = st_scr[0:2, :]
                later_scr[j, 0:2, :] = run
                st_scr[0:2, lo:] = run[:, lo:] + jnp.concatenate([jnp.sum(sp[0:BK], axis=0, keepdims=True),
                                                                  jnp.sum(sp[BK:2 * BK], axis=0, keepdims=True)], axis=0)

        def sweep(blocks, masked, seen=None):
            seen = seen or [0] * len(blocks)
            starts = [pl.multiple_of(j * BK, BK) for j in blocks]
            valid = [(keyi[:, lo:] + j * BK) < qryi[:, lo:] if m else None for j, m, lo in zip(blocks, masked, seen)]
            zs = [scores(j, lo) for j, lo in zip(blocks, seen)]
            us = [jnp.exp(lax.bitcast_convert_type(lax.bitcast_convert_type(z, jnp.uint32) | jnp.uint32(0x80000000), F32))
                  for z in zs]
            sps = [jnp.maximum(z, 0.0) + jnp.log(1.0 + u) for z, u in zip(zs, us)]
            sps = [sp if ok is None else _mask_keys(sp, ok) for sp, ok in zip(sps, valid)]
            sigs = [jnp.where(z >= 0.0, 1.0, u) / (1.0 + u) for z, u in zip(zs, us)]
            cums = [_mm(usuf_ref[...], _split_rows(sp)) for sp in sps]
            dws = [_mm(_pair_stack(v_ref[pl.ds(ks, BK), :], lane), dot[:, lo:]) for ks, lo in zip(starts, seen)]
            wfs = []
            for z, c, j, ok, lo in zip(zs, cums, blocks, valid, seen):
                f = jnp.exp(-later_scr[j, 0:2, lo:])
                wide = (BK, WQ - lo)
                wf = jnp.exp(z - c) * jnp.concatenate([jnp.broadcast_to(f[0:1], wide), jnp.broadcast_to(f[1:2], wide)], axis=0)
                wfs.append(wf if ok is None else _mask_keys(wf, ok))
            es = [dw * wf for dw, wf in zip(dws, wfs)]
            pres = [_mm(upre_ref[...], e.astype(BF16)) for e in es]
            dzs = []
            for e, pre, sig, ok, lo in zip(es, pres, sigs, valid, seen):
                e0 = pre[0:BK] + st_scr[0:1, lo:]
                e1 = pre[BK:2 * BK] + st_scr[1:2, lo:]
                st_scr[0:1, lo:] = e0[BK - 1:BK]
                st_scr[1:2, lo:] = e1[BK - 1:BK]
                dz = e - sig * jnp.concatenate([e0, e1], axis=0)
                dzs.append((dz if ok is None else _mask_keys(dz, ok)).astype(BF16))
            whole = [b for b, lo in enumerate(seen) if lo == 0]
            dqt_scr[...] += _mm(jnp.concatenate([_heads_t(kt_ref[:, pl.ds(starts[b], BK)], rowi) for b in whole], axis=1),
                                jnp.concatenate([dzs[b] for b in whole], axis=0))
            for b, lo in enumerate(seen):
                if lo:
                    dqt_scr[:, lo:] += _mm(_heads_t(kt_ref[:, pl.ds(starts[b], BK)], rowi), dzs[b])
            for ks, dz, wf, lo in zip(starts, dzs, wfs, seen):
                rk = _mm(dz, qs[lo:])
                dk_ref[pl.ds(ks, BK), :] += jnp.where(lane < 64, rk[0:BK], rk[BK:2 * BK])
                rv = _mm(wf.astype(BF16), dob[lo:])
                dv_ref[pl.ds(ks, BK), :] += jnp.where(lane < 64, rv[0:BK], rv[BK:2 * BK])

        assert WQ == 2 * BK
        st_scr[...] = jnp.zeros_like(st_scr)

        @pl.when(i == 0)
        def _():
            scan([1, 0], [True, True], [BK, 0])

        @pl.when(i > 0)
        def _():
            scan([2 * i + 1, 2 * i, 2 * i - 1, 2 * i - 2], [True, True, False, False], [BK, 0, 0, 0])

        def unfinished():
            return (jnp.min(st_scr[0:2, :]) < SB_CUTOFF).astype(jnp.int32)

        def step(c):
            scan([2 * i - 1 - 2 * c[0], 2 * i - 2 - 2 * c[0]], [False, False])
            return c[0] + 1, unfinished()

        npairs, _ = lax.while_loop(lambda c: (c[0] < i) & (c[1] > 0), step, (jnp.minimum(i, 1), unfinished()))

        st_scr[...] = jnp.zeros_like(st_scr)
        dqt_scr[...] = jnp.zeros_like(dqt_scr)
        first = 2 * (i - npairs)

        def early(t, carry):
            sweep([first + 2 * t, first + 2 * t + 1], [False, False])
            return carry

        lax.fori_loop(0, npairs - 1, early, 0)

        @pl.when(i == 0)
        def _():
            sweep([0, 1], [True, True], [0, BK])

        @pl.when(i > 0)
        def _():
            sweep([2 * i - 2, 2 * i - 1, 2 * i, 2 * i + 1], [False, False, True, True], [0, 0, 0, BK])

        dq_ref[...] = (dqt_scr[...].T * (HEAD_DIM ** -0.5)).astype(BF16)

        @pl.when((p == pl.num_programs(0) - 1) & (i == pl.num_programs(1) - 1))
        def _():
            reduce_finish()
            copies = [pltpu.make_async_copy(f_scr[t], outs[t], out_sems.at[t]) for t in range(n)]
            for cp in copies:
                cp.start()
            for cp in copies:
                cp.wait()

    qspec = pl.BlockSpec((WQ, LANES), lambda p, i: (i, p))
    kspec = pl.BlockSpec((s, LANES), lambda p, i: (0, p))
    tspec = pl.BlockSpec((LANES, s), lambda p, i: (p, 0))
    anywhere = pl.BlockSpec(memory_space=pl.ANY)
    reduced = [jax.ShapeDtypeStruct(a.shape[1:], F32) for a in late]
    return pl.pallas_call(
        body, name="sb_bwd", grid=(4, s // WQ),
        out_shape=(jax.ShapeDtypeStruct((s, 512), BF16), jax.ShapeDtypeStruct((s, 512), F32),
                   jax.ShapeDtypeStruct((s, 512), F32), *reduced),
        in_specs=[qspec, kspec, tspec, kspec, qspec, _full2((2 * BK, 4 * BK)), _full2((2 * BK, 2 * BK))] + [anywhere] * n,
        out_specs=(qspec, kspec, kspec) + (anywhere,) * n,
        scratch_shapes=[pltpu.VMEM((s // BK, 8, WQ), F32), pltpu.VMEM((LANES, WQ), F32), pltpu.VMEM((8, WQ), F32)]
                       + [pltpu.VMEM(r.shape, F32) for r in reduced] + _reduce_scratch(late) + [pltpu.SemaphoreType.DMA((n,))],
        compiler_params=pltpu.CompilerParams(vmem_limit_bytes=VMEM_ATTN),
    )(q, k, kt, v, do, _sum_matrix("suffix", 2), _sum_matrix("prefix", 1), *late)


MLA_SCALE = (QK_NOPE + QK_ROPE) ** -0.5
LOG2E = 1.4426950408889634


def _mla_keys(kb):
    zero = jnp.zeros((BK, LANES), kb.dtype)
    return jnp.concatenate([jnp.concatenate([kb[:, 0:LANES], zero], axis=1),
                            jnp.concatenate([zero, kb[:, LANES:2 * LANES]], axis=1)], axis=0)


def _mla_fwd(qc, kc, vt):
    s = qc.shape[0]
    mq = min(MQ_FWD, s)
    rows_l = 16

    def body(q_ref, k_ref, vt_ref, o_ref, l_ref, p_scr, ot_scr, st_scr):
        i = pl.program_id(1)
        keyc = lax.broadcasted_iota(jnp.int32, (BK, mq), 0)
        qryc = (lax.broadcasted_iota(jnp.int32, (BK, mq), 1) + i * mq) // 64
        row = lax.broadcasted_iota(jnp.int32, (LANES, 1), 0)
        qw = q_ref[...]
        orow = lax.broadcasted_iota(jnp.int32, (rows_l, 2 * BK), 0)
        ocol = lax.broadcasted_iota(jnp.int32, (rows_l, 2 * BK), 1)
        ones = jnp.where(((orow == 0) & (ocol < BK)) | ((orow == 1) & (ocol >= BK)), 1.0, 0.0).astype(BF16)

        def scores(j, lo=0):
            ks = pl.multiple_of(j * BK, BK)
            return _mm_nt(_mla_keys(k_ref[pl.ds(ks, BK), :]), qw[lo:])

        def values_t(j):
            vtb = vt_ref[:, pl.ds(pl.multiple_of(j * BK, BK), BK)]
            zero = jnp.zeros_like(vtb)
            top = jnp.concatenate([jnp.where(row < 64, vtb, zero), jnp.where(row >= 64, vtb, zero)], axis=1)
            return jnp.concatenate([top, ones], axis=0)

        def softmax(ja, za, zb, masked, lo=0):
            c = MLA_SCALE * LOG2E
            parts = [za[0:BK] * c, za[BK:2 * BK] * c, zb[0:BK] * c, zb[BK:2 * BK] * c]
            if masked:
                va = ((keyc[:, lo:] + ja * BK) // 64) <= qryc[:, lo:]
                vb = ((keyc[:, lo:] + (ja + 1) * BK) // 64) <= qryc[:, lo:]
                parts = [jnp.where(va, parts[0], -1e30), jnp.where(va, parts[1], -1e30),
                         jnp.where(vb, parts[2], -1e30), jnp.where(vb, parts[3], -1e30)]
            m0, m1 = st_scr[0:1, lo:], st_scr[1:2, lo:]
            n0 = jnp.maximum(m0, jnp.max(jnp.maximum(parts[0], parts[2]), axis=0, keepdims=True))
            n1 = jnp.maximum(m1, jnp.max(jnp.maximum(parts[1], parts[3]), axis=0, keepdims=True))
            st_scr[2:3, lo:] = jnp.exp2(m0 - n0)
            st_scr[3:4, lo:] = jnp.exp2(m1 - n1)
            st_scr[0:1, lo:] = n0
            st_scr[1:2, lo:] = n1
            p_scr[:, lo:] = jnp.concatenate([jnp.exp2(parts[0] - n0), jnp.exp2(parts[1] - n1),
                                             jnp.exp2(parts[2] - n0), jnp.exp2(parts[3] - n1)], axis=0).astype(BF16)

        def accumulate(ja, lo=0):
            pv = _mm(jnp.concatenate([values_t(ja), values_t(ja + 1)], axis=1), p_scr[:, lo:])
            a = jnp.where(row < 64, st_scr[2:3, lo:], st_scr[3:4, lo:])
            ot_scr[0:LANES, lo:] = a * ot_scr[0:LANES, lo:] + pv[0:LANES]
            ot_scr[LANES:LANES + 8, lo:] = st_scr[2:10, lo:] * ot_scr[LANES:LANES + 8, lo:] + pv[LANES:LANES + 8]

        def step(n, masked, lo=0, prev_lo=0):
            za, zb = scores(2 * n, lo), scores(2 * n + 1, lo)
            accumulate(2 * n - 2, prev_lo)
            softmax(2 * n, za, zb, masked, lo)

        def first(masked):
            softmax(0, scores(0), scores(1), masked)

        st_scr[...] = jnp.concatenate([jnp.full((2, mq), -1e30, F32), jnp.ones((14, mq), F32)], axis=0)
        ot_scr[...] = jnp.zeros_like(ot_scr)

        npq = mq // (2 * BK)
        seen = lambda d: 2 * BK * max(d, 0)

        @pl.when(i == 0)
        def _():
            first(True)
            for d in range(1, npq):
                step(d, True, seen(d), seen(d - 1))

        @pl.when(i > 0)
        def _():
            first(False)
            lax.fori_loop(1, npq * i, lambda n, c: (step(n, False), c)[1], 0)
            for d in range(npq):
                step(npq * i + d, True, seen(d), seen(d - 1))

        accumulate(2 * (npq * (i + 1) - 1), seen(npq - 1))
        l0, l1 = ot_scr[LANES:LANES + 1, :], ot_scr[LANES + 1:LANES + 2, :]
        o_ref[...] = (ot_scr[0:LANES, :] / jnp.where(row < 64, l0, l1)).T
        l_ref[...] = jnp.where(row < 64, st_scr[0:1, :] + jnp.log2(l0), st_scr[1:2, :] + jnp.log2(l1)).T

    qspec = pl.BlockSpec((mq, 2 * LANES), lambda p, i: (i, p))
    kspec = pl.BlockSpec((s, 2 * LANES), lambda p, i: (0, p))
    vtspec = pl.BlockSpec((LANES, s), lambda p, i: (p, 0))
    ospec = pl.BlockSpec((mq, LANES), lambda p, i: (i, p))
    return pl.pallas_call(
        body, name="mla_fwd", grid=(4, s // mq),
        out_shape=(jax.ShapeDtypeStruct((s, 512), F32), jax.ShapeDtypeStruct((s, 512), F32)),
        in_specs=[qspec, kspec, vtspec], out_specs=(ospec, ospec),
        scratch_shapes=[pltpu.VMEM((4 * BK, mq), BF16), pltpu.VMEM((LANES + 8, mq), F32), pltpu.VMEM((16, mq), F32)],
        compiler_params=pltpu.CompilerParams(vmem_limit_bytes=VMEM_ATTN),
    )(qc, kc, vt)


def _mla_bwd(qc, kc, kct, v, do, lse, delta):
    s = qc.shape[0]
    mq = min(MQ_BWD, s)

    def body(q_ref, k_ref, kt_ref, v_ref, do_ref, l_ref, d_ref, dq_ref, dk_ref, dv_ref, dqt_scr, p_scr, dz_scr):
        i = pl.program_id(1)

        @pl.when(i == 0)
        def _():
            dk_ref[...] = jnp.zeros_like(dk_ref)
            dv_ref[...] = jnp.zeros_like(dv_ref)

        lane = lax.broadcasted_iota(jnp.int32, (1, LANES), 1)
        keyc = lax.broadcasted_iota(jnp.int32, (BK, mq), 0)
        qryc = (lax.broadcasted_iota(jnp.int32, (BK, mq), 1) + i * mq) // 64
        qw = q_ref[...]
        dob = do_ref[...]
        dost = (dob.astype(F32) * MLA_SCALE).T.astype(BF16)
        lt = l_ref[...].T
        dt = (d_ref[...] * MLA_SCALE).T
        lse0, lse1 = lt[0:1], lt[64:65]
        dl0, dl1 = dt[0:1], dt[64:65]
        dqt_scr[...] = jnp.zeros_like(dqt_scr)

        def products(j, lo=0):
            ks = pl.multiple_of(j * BK, BK)
            return (_mm_nt(_mla_keys(k_ref[pl.ds(ks, BK), :]), qw[lo:]),
                    _mm(_pair_stack(v_ref[pl.ds(ks, BK), :], lane), dost[:, lo:]))

        def grads(j, slot, zt, dwt, masked, lo=0):
            zt = zt * (MLA_SCALE * LOG2E)
            p0 = jnp.exp2(zt[0:BK] - lse0[:, lo:])
            p1 = jnp.exp2(zt[BK:2 * BK] - lse1[:, lo:])
            if masked:
                valid = ((keyc[:, lo:] + j * BK) // 64) <= qryc[:, lo:]
                p0, p1 = jnp.where(valid, p0, 0.0), jnp.where(valid, p1, 0.0)
            p_scr[slot, :, lo:] = jnp.concatenate([p0, p1], axis=0).astype(BF16)
            dz_scr[slot, :, lo:] = jnp.concatenate([p0 * (dwt[0:BK] - dl0[:, lo:]), p1 * (dwt[BK:2 * BK] - dl1[:, lo:])],
                                                   axis=0).astype(BF16)

        def keys_t(ks):
            ktb = kt_ref[:, pl.ds(ks, BK)]
            zero = jnp.zeros((LANES, BK), ktb.dtype)
            return jnp.concatenate([jnp.concatenate([ktb[0:LANES], zero], axis=1),
                                    jnp.concatenate([zero, ktb[LANES:2 * LANES]], axis=1)], axis=0)

        def scatter(ja, lo=0):
            ksa, ksb = pl.multiple_of(ja * BK, BK), pl.multiple_of((ja + 1) * BK, BK)
            dqt_scr[:, lo:] += _mm(jnp.concatenate([keys_t(ksa), keys_t(ksb)], axis=1),
                                   jnp.concatenate([dz_scr[0, :, lo:], dz_scr[1, :, lo:]], axis=0))
            for slot, ks in ((0, ksa), (1, ksb)):
                rk = _mm(dz_scr[slot, :, lo:], qw[lo:])
                dk_ref[pl.ds(ks, BK), :] += jnp.concatenate([rk[0:BK, 0:LANES], rk[BK:2 * BK, LANES:2 * LANES]], axis=1)
                rv = _mm(p_scr[slot, :, lo:], dob[lo:])
                dv_ref[pl.ds(ks, BK), :] += jnp.where(lane < 64, rv[0:BK], rv[BK:2 * BK])

        def step(n, masked, lo=0, prev_lo=0):
            za, wa = products(2 * n, lo)
            zb, wb = products(2 * n + 1, lo)
            scatter(2 * n - 2, prev_lo)
            grads(2 * n, 0, za, wa, masked, lo)
            grads(2 * n + 1, 1, zb, wb, masked, lo)

        def first(masked):
            za, wa = products(0)
            zb, wb = products(1)
            grads(0, 0, za, wa, masked)
            grads(1, 1, zb, wb, masked)

        npq = mq // (2 * BK)
        seen = lambda d: 2 * BK * max(d, 0)

        @pl.when(i == 0)
        def _():
            first(True)
            for d in range(1, npq):
                step(d, True, seen(d), seen(d - 1))

        @pl.when(i > 0)
        def _():
            first(False)
            lax.fori_loop(1, npq * i, lambda n, c: (step(n, False), c)[1], 0)
            for d in range(npq):
                step(npq * i + d, True, seen(d), seen(d - 1))

        scatter(2 * (npq * (i + 1) - 1), seen(npq - 1))
        dq_ref[...] = dqt_scr[...].T

    qspec = pl.BlockSpec((mq, 2 * LANES), lambda p, i: (i, p))
    kspec = pl.BlockSpec((s, 2 * LANES), lambda p, i: (0, p))
    ktspec = pl.BlockSpec((2 * LANES, s), lambda p, i: (p, 0))
    vspec = pl.BlockSpec((s, LANES), lambda p, i: (0, p))
    ospec = pl.BlockSpec((mq, LANES), lambda p, i: (i, p))
    return pl.pallas_call(
        body, name="mla_bwd", grid=(4, s // mq),
        out_shape=(jax.ShapeDtypeStruct((s, 1024), F32), jax.ShapeDtypeStruct((s, 1024), F32),
                   jax.ShapeDtypeStruct((s, 512), F32)),
        in_specs=[qspec, kspec, ktspec, vspec, ospec, ospec, ospec], out_specs=(qspec, kspec, vspec),
        scratch_shapes=[pltpu.VMEM((2 * LANES, mq), F32), pltpu.VMEM((2, 2 * BK, mq), BF16), pltpu.VMEM((2, 2 * BK, mq), BF16)],
        compiler_params=pltpu.CompilerParams(vmem_limit_bytes=VMEM_ATTN),
    )(qc, kc, kct, v, do, lse, delta)


def _post(x, p, tgt, sbo, mlao, sbg, mlag, gsb, gmla, wout, gpost, wple, gple, wpg, bpg):
    s = x.shape[0]

    def body(x_ref, p_ref, t_ref, sbo_ref, mlao_ref, sbg_ref, mlag_ref, gsb_ref, gmla_ref, wout_ref,
             gpost_ref, wple_ref, gple_ref, wpg_ref, bpg_ref, bd_ref,
             dsbo_ref, dmlao_ref, delta_ref, dsbg_ref, dmlag_ref, dxres_ref, dwout_ref, dwpg_ref, dwple_ref, vec_ref):
        i = pl.program_id(0)

        @pl.when(i == 0)
        def _():
            dwout_ref[...] = jnp.zeros_like(dwout_ref)
            dwpg_ref[...] = jnp.zeros_like(dwpg_ref)
            dwple_ref[...] = jnp.zeros_like(dwple_ref)
            vec_ref[...] = jnp.zeros_like(vec_ref)

        inv_hd = 1.0 / HEAD_DIM

        def head_fwd(o, g, gate):
            r = lax.rsqrt(_seg(o * o, bd_ref[...]) * inv_hd + EPS)
            hat = o * r
            n = hat * g
            sg = _sigmoid(gate)
            return hat, r, n, sg, n * (gate * sg)

        sbo, mlao, sbg_v, mlag_v = sbo_ref[...], mlao_ref[...], sbg_ref[...], mlag_ref[...]
        gsb_v, gmla_v = gsb_ref[...], gmla_ref[...]
        sb_hat, sb_r, sb_n, sb_sg, sb_y = head_fwd(sbo, gsb_v, sbg_v)
        ml_hat, ml_r, ml_n, ml_sg, ml_y = head_fwd(mlao, gmla_v, mlag_v)
        mix = jnp.concatenate([sb_y, ml_y], axis=1).astype(BF16)
        y = _mm(mix, wout_ref[...])
        ry = lax.rsqrt(_rowmean(y * y) + EPS)
        y_hat = y * ry
        gpost_v = gpost_ref[...]
        x1 = x_ref[...] + y_hat * gpost_v
        pb = p_ref[...].astype(BF16)
        pl_ = _mm(pb, wple_ref[...])
        rp = lax.rsqrt(_rowmean(pl_ * pl_) + EPS)
        pl_hat = pl_ * rp
        gple_v = gple_ref[...]
        ple = pl_hat * gple_v
        x1b = x1.astype(BF16)
        gate = _sigmoid(_mm(x1b, wpg_ref[...]) + bpg_ref[...])
        err = x1 + ple * gate - t_ref[...]
        loss = 0.5 * jnp.sum(_rowmean(err * err))
        dout = err * (1.0 / D_MODEL)

        du = dout * ple * gate * (1.0 - gate)
        dub = du.astype(BF16)
        dple = dout * gate
        dx1 = dout + _mm_nt(dub, wpg_ref[...])
        dwpg_ref[...] += _mm_tn(x1b, dub)
        dplh = dple * gple_v
        dpl = rp * (dplh - pl_hat * _rowmean(dplh * pl_hat))
        dwple_ref[...] += _mm_tn(pb, dpl.astype(BF16))
        dxres_ref[...] = dx1
        dyh = dx1 * gpost_v
        dy = ry * (dyh - y_hat * _rowmean(dyh * y_hat))
        dyb = dy.astype(BF16)
        dwout_ref[...] += _mm_tn(mix, dyb)
        dmix = _mm_nt(dyb, wout_ref[...])

        def head_bwd(dyv, hat, r, n, sg, g, gate):
            dn = dyv * (gate * sg)
            dgate = dyv * n * (sg * (1.0 + gate * (1.0 - sg)))
            dhat = dn * g
            do = r * (dhat - hat * (_seg(dhat * hat, bd_ref[...]) * inv_hd))
            return do, dgate, _colsum(dn * hat)

        dsbo, dsbg, dg_sb = head_bwd(dmix[:, 0:512], sb_hat, sb_r, sb_n, sb_sg, gsb_v, sbg_v)
        dmlao, dmlag, dg_ml = head_bwd(dmix[:, 512:1024], ml_hat, ml_r, ml_n, ml_sg, gmla_v, mlag_v)
        dsbo_ref[...] = dsbo.astype(BF16)
        dmlao_ref[...] = dmlao.astype(BF16)
        delta_ref[...] = _seg(dmlao * mlao, bd_ref[...])
        dsbg_ref[...] = dsbg.astype(BF16)
        dmlag_ref[...] = dmlag.astype(BF16)
        vec_ref[pl.ds(0, 1), :] += _colsum(dx1 * y_hat)
        vec_ref[pl.ds(1, 1), :] += _colsum(dple * pl_hat)
        vec_ref[pl.ds(2, 1), :] += _colsum(du)
        vec_ref[pl.ds(3, 1), :] += jnp.concatenate([dg_sb, dg_ml], axis=1)
        vec_ref[pl.ds(4, 1), :] += jnp.full((1, D_MODEL), loss, F32)

    out_shape = (
        jax.ShapeDtypeStruct((s, 512), BF16), jax.ShapeDtypeStruct((s, 512), BF16), jax.ShapeDtypeStruct((s, 512), F32),
        jax.ShapeDtypeStruct((s, 512), BF16), jax.ShapeDtypeStruct((s, 512), BF16), jax.ShapeDtypeStruct((s, D_MODEL), F32),
        jax.ShapeDtypeStruct((D_MODEL, D_MODEL), F32), jax.ShapeDtypeStruct((D_MODEL, D_MODEL), F32),
        jax.ShapeDtypeStruct((PLE_DIM, D_MODEL), F32), jax.ShapeDtypeStruct((8, D_MODEL), F32),
    )
    return pl.pallas_call(
        body, name="post_fwd_bwd", grid=(s // TM,), out_shape=out_shape,
        in_specs=[_rows(D_MODEL), _rows(PLE_DIM), _rows(D_MODEL), _rows(512), _rows(512), _rows(512), _rows(512),
                  _full((1, 512)), _full((1, 512)), _full((D_MODEL, D_MODEL)),
                  _full((1, D_MODEL)), _full((PLE_DIM, D_MODEL)), _full((1, D_MODEL)), _full((D_MODEL, D_MODEL)),
                  _full((1, D_MODEL)), _full((1024, 512))],
        out_specs=(_rows(512), _rows(512), _rows(512), _rows(512), _rows(512), _rows(D_MODEL),
                   _acc((D_MODEL, D_MODEL)), _acc((D_MODEL, D_MODEL)), _acc((PLE_DIM, D_MODEL)), _acc((8, D_MODEL))),
        compiler_params=pltpu.CompilerParams(vmem_limit_bytes=VMEM_DENSE),
    )(x, p, tgt, sbo, mlao, sbg, mlag, gsb, gmla, wout, gpost, wple, gple, wpg, bpg, _blockdiag2(512, HEAD_DIM))


def _pre_bwd(x, dxres, dsbq, dsbk, dsbv, dsbg, dmlag, dqc, dkc, dmv, cq, ckv, tabs, gpre, win, gq, wuq, gkv, wk, wv):
    s = x.shape[0]
    c_t, sa_t, sb_t = tabs
    rw = _rows

    def body(x_ref, dxres_ref, dsbq_ref, dsbk_ref, dsbv_ref, dsbg_ref, dmlag_ref, dqc_ref, dkc_ref, dmv_ref, cq_ref,
             ckv_ref, c_ref, sa_ref, sb_ref, gpre_ref, win_ref, gq_ref, wuq_ref, gkv_ref, wk_ref, wv_ref,
             gx_ref, dwin_ref, dwuq_ref, dwk_ref, dwv_ref, vec_ref, dwin_acc):
        i = pl.program_id(0)

        @pl.when(i == 0)
        def _():
            dwin_acc[...] = jnp.zeros_like(dwin_acc)
            dwuq_ref[...] = jnp.zeros_like(dwuq_ref)
            dwk_ref[...] = jnp.zeros_like(dwk_ref)
            dwv_ref[...] = jnp.zeros_like(dwv_ref)
            vec_ref[...] = jnp.zeros_like(vec_ref)

        lane = lax.broadcasted_iota(jnp.int32, (1, LANES), 1)
        c1, sa1, sb1 = c_ref[...], sa_ref[...], sb_ref[...]
        c8, sa8, sb8 = jnp.tile(c1, (1, 8)), jnp.tile(sa1, (1, 8)), jnp.tile(sb1, (1, 8))

        def norm_bwd(dn, hat, r, g):
            t = dn * g
            return r * (t - hat * _rowmean(t * hat)), _colsum(dn * hat)

        xv = x_ref[...]
        r1 = lax.rsqrt(_rowmean(xv * xv) + EPS)
        x_hat = xv * r1
        gpre_v = gpre_ref[...]
        hb = (x_hat * gpre_v).astype(BF16)
        ready = jnp.concatenate([dsbq_ref[...], dsbk_ref[...].astype(BF16), dsbv_ref[...].astype(BF16), dsbg_ref[...]], axis=1)
        dmlag = dmlag_ref[...]
        dwin_acc[:, 0:2048] += _mm_tn(hb, ready)
        dwin_acc[:, 2560:3072] += _mm_tn(hb, dmlag)
        dh = _mm_nt(ready, win_ref[:, 0:2048]) + _mm_nt(dmlag, win_ref[:, 2560:3072])

        dqeb = _rope_bwd(dqc_ref[...], c8, sa8, sb8).astype(BF16)
        cq = cq_ref[...]
        rq = lax.rsqrt(_rowmean(cq * cq) + EPS)
        cq_hat = cq * rq
        gq_v = gq_ref[...]
        dwuq_ref[...] += _mm_tn((cq_hat * gq_v).astype(BF16), dqeb)
        dcq, dg_q = norm_bwd(_mm_nt(dqeb, wuq_ref[...]), cq_hat, rq, gq_v)

        dkc = dkc_ref[...]
        dkcb = dkc.astype(BF16)
        dmvb = dmv_ref[...].astype(BF16)
        ckv = ckv_ref[...]
        rkv = lax.rsqrt(_rowmean(ckv * ckv) + EPS)
        ckv_hat = ckv * rkv
        gkv_v = gkv_ref[...]
        ckvnb = (ckv_hat * gkv_v).astype(BF16)
        dwk_ref[...] += _mm_tn(ckvnb, dkcb)
        dwv_ref[...] += _mm_tn(ckvnb, dmvb)
        dckv, dg_kv = norm_bwd(_mm_nt(dkcb, wk_ref[...]) + _mm_nt(dmvb, wv_ref[...]), ckv_hat, rkv, gkv_v)

        dkr = dkc[:, 0:LANES]
        for hh in range(1, 8):
            dkr = dkr + dkc[:, LANES * hh:LANES * (hh + 1)]
        dkr = _rope_bwd(dkr, c1, sa1, sb1)
        dkr = jnp.where((lane >= 64) & (lane < 96), dkr, 0.0)

        late = jnp.concatenate([dcq.astype(BF16), dckv.astype(BF16), dkr.astype(BF16)], axis=1)
        dwin_acc[:, 2048:2560] += _mm_tn(hb, late)
        dx, dg_pre = norm_bwd(dh + _mm_nt(late, win_ref[:, 2048:2560]), x_hat, r1, gpre_v)
        gx_ref[...] = dxres_ref[...] + dx
        vec_ref[pl.ds(0, 1), :] += dg_pre
        vec_ref[pl.ds(1, 1), :] += jnp.concatenate([dg_q, dg_kv, jnp.zeros((1, D_MODEL - Q_LORA - KV_LORA), F32)], axis=1)

        @pl.when(i == pl.num_programs(0) - 1)
        def _():
            pltpu.sync_copy(dwin_acc, dwin_ref)

    out_shape = (
        jax.ShapeDtypeStruct((s, D_MODEL), F32), jax.ShapeDtypeStruct((D_MODEL, D_EXT), F32),
        jax.ShapeDtypeStruct((Q_LORA, 1024), F32), jax.ShapeDtypeStruct((KV_LORA, 1024), F32),
        jax.ShapeDtypeStruct((KV_LORA, 512), F32), jax.ShapeDtypeStruct((8, D_MODEL), F32),
    )
    return pl.pallas_call(
        body, name="pre_bwd", grid=(s // TM,), out_shape=out_shape,
        in_specs=[rw(D_MODEL), rw(D_MODEL), rw(512), rw(512), rw(512), rw(512), rw(512),
                  rw(1024), rw(1024), rw(512), rw(Q_LORA), rw(KV_LORA), rw(LANES), rw(LANES),
                  rw(LANES), _full((1, D_MODEL)), _full((D_MODEL, D_EXT)), _full((1, Q_LORA)), _full((Q_LORA, 1024)),
                  _full((1, KV_LORA)), _full((KV_LORA, 1024)), _full((KV_LORA, 512))],
        out_specs=(rw(D_MODEL), pl.BlockSpec(memory_space=pl.ANY), _acc((Q_LORA, 1024)), _acc((KV_LORA, 1024)),
                   _acc((KV_LORA, 512)), _acc((8, D_MODEL))),
        scratch_shapes=[pltpu.VMEM((D_MODEL, D_EXT), F32)],
        compiler_params=pltpu.CompilerParams(vmem_limit_bytes=VMEM_DENSE),
    )(x, dxres, dsbq, dsbk, dsbv, dsbg, dmlag, dqc, dkc, dmv, cq, ckv, c_t, sa_t, sb_t, gpre, win, gq, wuq, gkv, wk, wv)


def _place():
    return lax.axis_index("x"), lax.axis_index("y"), lax.axis_index("c")


def _gather_steps(shapes, ins, bufs, send_sems, recv_sems):
    n = len(shapes)
    x, y, c = _place()
    me, sib = (x, y, c), (x, y, 1 - c)
    chips = [(1 - x, y), (x, 1 - y), (1 - x, 1 - y)]

    def half(t, chip, hc):
        rows = shapes[t][0] // 2
        return bufs[t].at[2 * chip[0] + chip[1], pl.ds(pl.multiple_of(hc * rows, 16), rows), :]

    def copy(k, t, chip, hc, to):
        return pltpu.make_async_remote_copy(src_ref=half(t, chip, hc), dst_ref=half(t, chip, hc), send_sem=send_sems.at[k],
                                            recv_sem=recv_sems.at[k], device_id=to, device_id_type=MESH)

    def start():
        for t in range(n):
            bufs[t][2 * x + y] = ins[t][...].astype(BF16)
            for j, chip in enumerate(chips):
                copy(6 * t + j, t, (x, y), c, (*chip, c)).start()

    def forward():
        for t in range(n):
            for j, chip in enumerate(chips):
                copy(6 * t + j, t, chip, c, me).wait_recv()
                copy(6 * t + 3 + j, t, chip, c, sib).start()

    def finish():
        for t in range(n):
            for j, chip in enumerate(chips):
                copy(6 * t + 3 + j, t, chip, 1 - c, me).wait_recv()
        for t in range(n):
            for j, chip in enumerate(chips):
                copy(6 * t + j, t, (x, y), c, (*chip, c)).wait_send()
                copy(6 * t + 3 + j, t, chip, c, sib).wait_send()

    return start, forward, finish


def _allgather_weights(shards):
    n = len(shards)

    def body(*refs):
        start, forward, finish = _gather_steps([a.shape for a in shards], refs[:n], refs[n:2 * n], refs[2 * n], refs[2 * n + 1])
        start()
        forward()
        finish()

    return pl.pallas_call(
        body, name="allgather_weights",
        out_shape=tuple(jax.ShapeDtypeStruct((N_SHARD,) + a.shape, BF16) for a in shards),
        in_specs=[pl.BlockSpec(memory_space=pltpu.VMEM)] * n, out_specs=(pl.BlockSpec(memory_space=pltpu.VMEM),) * n,
        scratch_shapes=[pltpu.SemaphoreType.DMA((6 * n,)), pltpu.SemaphoreType.DMA((6 * n,))],
        compiler_params=pltpu.CompilerParams(vmem_limit_bytes=VMEM_ATTN),
    )(*shards)


def _reduce_scratch(gsh):
    n = len(gsh)
    half_shapes = [(N_SHARD, a.shape[1] // 2, a.shape[2]) for a in gsh]
    return ([pltpu.VMEM(s_, F32) for s_ in half_shapes] * 2 + [pltpu.VMEM(s_, BF16) for s_ in half_shapes] * 2
            + [pltpu.SemaphoreType.DMA((n,)), pltpu.SemaphoreType.DMA((5 * n,)), pltpu.SemaphoreType.DMA((5 * n,))])


def _reduce_steps(halves, g_refs, f_refs, scratch):
    n = len(halves)
    accs, sibs, sbufs, rbufs = scratch[0:n], scratch[n:2 * n], scratch[2 * n:3 * n], scratch[3 * n:4 * n]
    local_sems, send_sems, recv_sems = scratch[4 * n:4 * n + 3]
    x, y, c = _place()
    me, sib = (x, y, c), (x, y, 1 - c)
    mine = 2 * x + y
    chips = [(1 - x, y), (x, 1 - y), (1 - x, 1 - y)]

    def remote(k, src, dst, to):
        return pltpu.make_async_remote_copy(src_ref=src, dst_ref=dst, send_sem=send_sems.at[k], recv_sem=recv_sems.at[k],
                                            device_id=to, device_id_type=MESH)

    def half3(ref, t, hc):
        return ref.at[:, pl.ds(pl.multiple_of(hc * halves[t], 8), halves[t]), :]

    def half2(ref, t, hc):
        return ref.at[pl.ds(pl.multiple_of(hc * halves[t], 8), halves[t]), :]

    def mine_load(t):
        return pltpu.make_async_copy(half3(g_refs[t], t, c), accs[t], local_sems.at[t])

    def to_sibling(t, to):
        return remote(t, half3(g_refs[t], t, 1 - c), sibs[t], to)

    def to_chip(t, j, chip, to):
        idx = 2 * chip[0] + chip[1]
        return remote(n + 3 * t + j, sbufs[t].at[idx], rbufs[t].at[mine if to is not me else idx], to)

    def swap(t, hc, to):
        return remote(4 * n + t, half2(f_refs[t], t, hc), half2(f_refs[t], t, hc), to)

    def load():
        for t in range(n):
            mine_load(t).start()
            to_sibling(t, sib).start()

    def partial():
        for t in range(n):
            mine_load(t).wait()
            to_sibling(t, me).wait_recv()
            for k in range(N_SHARD):
                accs[t][k] = accs[t][k] + sibs[t][k]
            for j, chip in enumerate(chips):
                idx = 2 * chip[0] + chip[1]
                sbufs[t][idx] = accs[t][idx].astype(BF16)
                to_chip(t, j, chip, (*chip, c)).start()

    def total():
        for t in range(n):
            acc = accs[t][mine]
            for j, chip in enumerate(chips):
                to_chip(t, j, chip, me).wait_recv()
                acc = acc + rbufs[t][2 * chip[0] + chip[1]].astype(F32)
            half2(f_refs[t], t, c)[...] = acc
            swap(t, c, sib).start()

    def finish():
        for t in range(n):
            swap(t, 1 - c, me).wait_recv()
        for t in range(n):
            to_sibling(t, sib).wait_send()
            for j, chip in enumerate(chips):
                to_chip(t, j, chip, (*chip, c)).wait_send()
            swap(t, c, sib).wait_send()

    return load, partial, total, finish


def _reduce_scatter_grads(gsh, vec):
    n = len(gsh)
    halves = [a.shape[1] // 2 for a in gsh]

    def body(*refs):
        g_refs, vec_ref, f_refs, vsum_ref = refs[:n], refs[n], refs[n + 1:2 * n + 1], refs[2 * n + 1]
        scratch = refs[2 * n + 2:]
        vrecv, vsend_sems, vrecv_sems = scratch[4 * n + 3:]
        load, partial, total, finish = _reduce_steps(halves, g_refs, f_refs, scratch)
        x, y, c = _place()
        my_dev = 4 * x + 2 * y + c

        def flip(k):
            return x ^ ((k >> 2) & 1), y ^ ((k >> 1) & 1), c ^ (k & 1)

        def vcopy(k, slot, to):
            return pltpu.make_async_remote_copy(src_ref=vec_ref, dst_ref=vrecv.at[slot], send_sem=vsend_sems.at[k - 1],
                                                recv_sem=vrecv_sems.at[k - 1], device_id=to, device_id_type=MESH)

        load()
        vrecv[my_dev] = vec_ref[...]
        for k in range(1, 8):
            vcopy(k, my_dev, flip(k)).start()
        partial()
        total()
        finish()
        for k in range(1, 8):
            fx, fy, fc = flip(k)
            vcopy(k, 4 * fx + 2 * fy + fc, (x, y, c)).wait_recv()
        vs = vrecv[0]
        for d in range(1, 8):
            vs = vs + vrecv[d]
        vsum_ref[...] = vs
        for k in range(1, 8):
            vcopy(k, my_dev, flip(k)).wait_send()

    return pl.pallas_call(
        body, name="reduce_scatter_grads",
        out_shape=tuple(jax.ShapeDtypeStruct(a.shape[1:], F32) for a in gsh) + (jax.ShapeDtypeStruct((VEC_ROWS, 1024), F32),),
        in_specs=[pl.BlockSpec(memory_space=pl.ANY)] * n + [pl.BlockSpec(memory_space=pltpu.VMEM)],
        out_specs=(pl.BlockSpec(memory_space=pltpu.VMEM),) * (n + 1),
        scratch_shapes=_reduce_scratch(gsh) + [pltpu.VMEM((8, VEC_ROWS, 1024), F32), pltpu.SemaphoreType.DMA((7,)),
                                               pltpu.SemaphoreType.DMA((7,))],
        compiler_params=pltpu.CompilerParams(vmem_limit_bytes=56 * 1024 * 1024),
    )(*gsh, vec)


def _adamw(w, g, m, v):
    rows, cols = w.shape
    tr = rows if rows <= 256 else 256
    flip = cols % LANES != 0

    def body(w_ref, g_ref, m_ref, v_ref, g_out, d_ref, nm_ref, nv_ref):
        gv = g_ref[...]
        outs = (gv,) + _adam_math(w_ref[...], gv, m_ref[...], v_ref[...])
        for ref, val in zip((g_out, d_ref, nm_ref, nv_ref), outs):
            ref[...] = val.T if flip else val

    spec = pl.BlockSpec((tr, cols), lambda i: (i, 0))
    ospec = pl.BlockSpec((cols, tr), lambda i: (0, i)) if flip else spec
    shp = jax.ShapeDtypeStruct((cols, rows) if flip else (rows, cols), F32)
    outs = pl.pallas_call(body, name="adamw", grid=(rows // tr,), out_shape=(shp,) * 4,
                          in_specs=[spec] * 4, out_specs=(ospec,) * 4)(w, g, m, v)
    return tuple(o.T for o in outs) if flip else outs


def _adam_math(w, g, m, v):
    m2 = ADAM_B1 * m + (1.0 - ADAM_B1) * g
    v2 = ADAM_B2 * v + (1.0 - ADAM_B2) * (g * g)
    m_hat = m2 / (1.0 - ADAM_B1 ** ADAM_STEP)
    v_hat = v2 / (1.0 - ADAM_B2 ** ADAM_STEP)
    return -ADAM_LR * (m_hat / (jnp.sqrt(v_hat) + ADAM_EPS) + ADAM_WD * w), m2, v2


def _adamw_small(vsum, w, m, v):
    names = [name for name, _, _, _ in _VEC_LAYOUT]
    k = len(names)

    def body(*refs):
        vs_ref, w_refs, m_refs, v_refs = refs[0], refs[1:1 + k], refs[1 + k:1 + 2 * k], refs[1 + 2 * k:1 + 3 * k]
        outs = refs[1 + 3 * k:]
        for idx, (_, r, c0, width) in enumerate(_VEC_LAYOUT):
            gv = vs_ref[pl.ds(r, 1), pl.ds(c0, width)]
            d, m2, v2 = _adam_math(w_refs[idx][...], gv, m_refs[idx][...], v_refs[idx][...])
            outs[idx][...], outs[k + idx][...], outs[2 * k + idx][...], outs[3 * k + idx][...] = gv, d, m2, v2

    shapes = tuple(jax.ShapeDtypeStruct(w[name].shape, F32) for name in names)
    res = pl.pallas_call(
        body, name="adamw_small", out_shape=shapes * 4,
        in_specs=[pl.BlockSpec(memory_space=pltpu.VMEM)] * (1 + 3 * k), out_specs=(pl.BlockSpec(memory_space=pltpu.VMEM),) * (4 * k),
    )(vsum, *[w[name] for name in names], *[m[name] for name in names], *[v[name] for name in names])
    return tuple({name: res[part * k + idx] for idx, name in enumerate(names)} for part in range(4))


_EARLY = ("w_in", "w_uq", "w_ukv")
_LATE = ("w_out", "w_ple", "w_ple_gate")
_BIG = _EARLY + _LATE
_KR_LOCAL = 2432 - 3 * (D_IN // N_SHARD)


def _extend_early(parts):
    cols = lambda a: a.transpose(1, 0, 2).reshape(a.shape[1], N_SHARD * a.shape[2])
    g = parts["w_in"]
    zeros = lambda n: jnp.zeros((D_MODEL, n), g.dtype)
    win_ext = jnp.concatenate([g[0], g[1], g[2], g[3][:, :_KR_LOCAL], zeros(64), g[3][:, _KR_LOCAL:_KR_LOCAL + QK_ROPE],
                               zeros(32), g[3][:, _KR_LOCAL + QK_ROPE:]], axis=1)
    wuq_ext = jnp.pad(cols(parts["w_uq"]).reshape(Q_LORA, 8, 96), ((0, 0), (0, 0), (0, 32))).reshape(Q_LORA, 1024)
    wukv = cols(parts["w_ukv"]).reshape(KV_LORA, 8, 128)
    wk_ext = jnp.pad(wukv[:, :, :64], ((0, 0), (0, 0), (0, 64))).reshape(KV_LORA, 1024)
    wv = wukv[:, :, 64:].reshape(KV_LORA, 512)
    return win_ext, wuq_ext, wk_ext, wv


def _shard_cols(a):
    return a.reshape(a.shape[0], N_SHARD, a.shape[1] // N_SHARD).transpose(1, 0, 2)


def _shard_rows(a):
    return a.reshape(N_SHARD, a.shape[0] // N_SHARD, a.shape[1])


def _shard_early_grads(dwin_ext, dwuq_ext, dwk_ext, dwv):
    e, w = dwin_ext, D_IN // N_SHARD
    last = jnp.concatenate([e[:, 3 * w:2432], e[:, 2496:2528], e[:, 2560:]], axis=1)
    dwuq = dwuq_ext.reshape(Q_LORA, 8, 128)[:, :, :96].reshape(Q_LORA, 768)
    dwukv = jnp.concatenate([dwk_ext.reshape(KV_LORA, 8, 128)[:, :, :64], dwv.reshape(KV_LORA, 8, 64)], axis=2)
    return [jnp.stack([e[:, 0:w], e[:, w:2 * w], e[:, 2 * w:3 * w], last]), _shard_cols(dwuq),
            _shard_cols(dwukv.reshape(KV_LORA, 1024))]


def _rope_tables(positions):
    half = QK_ROPE // 2
    freq = ROPE_THETA ** (-jnp.arange(half, dtype=F32) / half)
    ang = positions.astype(F32)[:, None] * freq
    cos, sin = jnp.cos(ang), jnp.sin(ang)
    s = positions.shape[0]
    z = lambda n: jnp.zeros((s, n), F32)
    c_t = jnp.concatenate([jnp.ones((s, 64), F32), cos, cos, z(32)], axis=1)
    sa_t = jnp.concatenate([z(64), -sin, z(16), z(32)], axis=1)
    sb_t = jnp.concatenate([z(64), z(16), sin, z(32)], axis=1)
    return c_t, sa_t, sb_t


def _local_grads(x, p, positions, tgt, gains, early, late):
    win_ext, wuq_ext, wk_ext, wv = _extend_early(early)
    tabs = _rope_tables(positions)
    g = gains
    sbq, sbk, sbv, sbg, mlag, cq, ckv, qc, kc, mv, sbkt, sbvt, kct, mvt = _pre_fwd(
        x, tabs, g["norm_pre_g"], win_ext, g["q_norm_g"], wuq_ext, g["kv_norm_g"], wk_ext, wv)
    sbo, wout4, wple4, wpg4 = _sb_fwd(sbq, sbk, sbvt, late)
    wout, wpg = wout4.reshape(D_MODEL, D_MODEL), wpg4.reshape(D_MODEL, D_MODEL)
    wple = wple4.transpose(1, 0, 2).reshape(PLE_DIM, D_MODEL)
    mlao, lse = _mla_fwd(qc, kc, mvt)
    dsbo, dmlao, delta, dsbg, dmlag, dxres, dwout, dwpg, dwple, vec_c = _post(
        x, p, tgt, sbo, mlao, sbg, mlag, g["sb_out_norm_g"], g["mla_out_norm_g"], wout, g["norm_post_g"], wple,
        g["ple_norm_g"], wpg, g["b_ple_gate"])
    dsbq, dsbk, dsbv, *late_grads = _sb_bwd(sbq, sbk, sbkt, sbv, dsbo, [_shard_rows(dwout), _shard_cols(dwple), _shard_rows(dwpg)])
    dqc, dkc, dmv = _mla_bwd(qc, kc, kct, mv, dmlao, lse, delta)
    gx, dwin_ext, dwuq_ext, dwk_ext, dwv, vec_d = _pre_bwd(
        x, dxres, dsbq, dsbk, dsbv, dsbg, dmlag, dqc, dkc, dmv, cq, ckv, tabs, g["norm_pre_g"], win_ext, g["q_norm_g"],
        wuq_ext, g["kv_norm_g"], wk_ext, wv)
    return gx, _shard_early_grads(dwin_ext, dwuq_ext, dwk_ext, dwv), late_grads, jnp.concatenate([vec_c, vec_d], axis=0)


_VEC_LAYOUT = (("norm_post_g", 0, 0, 1024), ("ple_norm_g", 1, 0, 1024), ("b_ple_gate", 2, 0, 1024), ("sb_out_norm_g", 3, 0, 512),
               ("mla_out_norm_g", 3, 512, 512), ("norm_pre_g", 8, 0, 1024), ("q_norm_g", 9, 0, 256), ("kv_norm_g", 9, 256, 128))
_LOSS_ROW = 4
_WEIGHT_ORDER = ("norm_pre_g", "w_in", "q_norm_g", "w_uq", "kv_norm_g", "w_ukv", "sb_out_norm_g", "mla_out_norm_g", "w_out",
                 "norm_post_g", "w_ple", "ple_norm_g", "w_ple_gate", "b_ple_gate")


def kernel(x, p, positions, norm_pre_g, w_in, q_norm_g, w_uq, kv_norm_g, w_ukv, sb_out_norm_g, mla_out_norm_g, w_out, norm_post_g, w_ple, ple_norm_g, w_ple_gate, b_ple_gate, loss_target, m_norm_pre_g, m_w_in, m_q_norm_g, m_w_uq, m_kv_norm_g, m_w_ukv, m_sb_out_norm_g, m_mla_out_norm_g, m_w_out, m_norm_post_g, m_w_ple, m_ple_norm_g, m_w_ple_gate, m_b_ple_gate, v_norm_pre_g, v_w_in, v_q_norm_g, v_w_uq, v_kv_norm_g, v_w_ukv, v_sb_out_norm_g, v_mla_out_norm_g, v_w_out, v_norm_post_g, v_w_ple, v_ple_norm_g, v_w_ple_gate, v_b_ple_gate):
    w = {"norm_pre_g": norm_pre_g, "w_in": w_in[0], "q_norm_g": q_norm_g, "w_uq": w_uq[0], "kv_norm_g": kv_norm_g, "w_ukv": w_ukv[0],
         "sb_out_norm_g": sb_out_norm_g, "mla_out_norm_g": mla_out_norm_g, "w_out": w_out[0], "norm_post_g": norm_post_g,
         "w_ple": w_ple[0], "ple_norm_g": ple_norm_g, "w_ple_gate": w_ple_gate[0], "b_ple_gate": b_ple_gate}
    m = {"norm_pre_g": m_norm_pre_g, "w_in": m_w_in[0], "q_norm_g": m_q_norm_g, "w_uq": m_w_uq[0], "kv_norm_g": m_kv_norm_g,
         "w_ukv": m_w_ukv[0], "sb_out_norm_g": m_sb_out_norm_g, "mla_out_norm_g": m_mla_out_norm_g, "w_out": m_w_out[0],
         "norm_post_g": m_norm_post_g, "w_ple": m_w_ple[0], "ple_norm_g": m_ple_norm_g, "w_ple_gate": m_w_ple_gate[0],
         "b_ple_gate": m_b_ple_gate}
    v = {"norm_pre_g": v_norm_pre_g, "w_in": v_w_in[0], "q_norm_g": v_q_norm_g, "w_uq": v_w_uq[0], "kv_norm_g": v_kv_norm_g,
         "w_ukv": v_w_ukv[0], "sb_out_norm_g": v_sb_out_norm_g, "mla_out_norm_g": v_mla_out_norm_g, "w_out": v_w_out[0],
         "norm_post_g": v_norm_post_g, "w_ple": v_w_ple[0], "ple_norm_g": v_ple_norm_g, "w_ple_gate": v_w_ple_gate[0],
         "b_ple_gate": v_b_ple_gate}
    gathered = _allgather_weights([w[n] for n in _EARLY])
    gx, early_grads, late_red, vec = _local_grads(x[0], p[0, 0], positions[0], loss_target[0], w, dict(zip(_EARLY, gathered)),
                                                  [w[n] for n in _LATE])
    *early_red, vsum = _reduce_scatter_grads(early_grads, vec)
    gred = early_red + late_red
    loss = vsum[_LOSS_ROW, 0]

    g, delta, new_m, new_v = _adamw_small(vsum, w, m, v)
    for n, gn in zip(_BIG, gred):
        g[n], delta[n], new_m[n], new_v[n] = _adamw(w[n], gn, m[n], v[n])

    lead = lambda n, a: a[None] if n in _BIG else a
    return (loss, gx[None],
            *[lead(n, g[n]) for n in _WEIGHT_ORDER], *[lead(n, delta[n]) for n in _WEIGHT_ORDER],
            *[lead(n, new_m[n]) for n in _WEIGHT_ORDER], *[lead(n, new_v[n]) for n in _WEIGHT_ORDER])
```

```python
import numpy as np
import jax
import jax.numpy as jnp
from jax import lax
from jax.experimental import pallas as pl
from jax.experimental.pallas import tpu as pltpu

F32 = jnp.float32
BF16 = jnp.bfloat16
MESH = pl.DeviceIdType.MESH

D_MODEL = 1024
HEAD_DIM = 64
D_SB = 512
D_MLA = 512
Q_LORA = 256
KV_LORA = 128
QK_NOPE = 64
QK_ROPE = 32
PLE_DIM = 256
D_IN = 2976
D_EXT = 3072
ROPE_THETA = 10000.0
EPS = 1e-6
N_SHARD = 4

ADAM_LR = 0.001
ADAM_B1 = 0.9
ADAM_B2 = 0.999
ADAM_EPS = 1e-08
ADAM_WD = 0.01
ADAM_STEP = 10

LANES = 128
BK = 128
WQ = 256
MQ_FWD = 4096
MQ_BWD = 1024
SB_CUTOFF = 120.0
TM = 256
TM_PRE = 256
VEC_ROWS = 16
VMEM_DENSE = 52 * 1024 * 1024
VMEM_ATTN = 40 * 1024 * 1024


def _mm(a, b):
    return jnp.dot(a, b, preferred_element_type=F32)


def _mm_nt(a, b):
    return lax.dot_general(a, b, (((1,), (1,)), ((), ())), preferred_element_type=F32)


def _mm_tn(a, b):
    return lax.dot_general(a, b, (((0,), (0,)), ((), ())), preferred_element_type=F32)


def _seg(a, bd2):
    return _mm(_split2(a), bd2)


def _const(mask):
    return jnp.asarray(np.asarray(mask, np.float32), dtype=BF16)


def _blockdiag2(n, seg):
    r = (np.arange(2 * n)[:, None] % n) // seg
    c = np.arange(n)[None, :] // seg
    return _const(r == c)


def _sigmoid(a):
    return 1.0 / (1.0 + jnp.exp(-a))


def _rowmean(a):
    return jnp.mean(a, axis=-1, keepdims=True)


def _colsum(a):
    return jnp.sum(a, axis=0, keepdims=True)


def _rope_fwd(a, c, sa, sb):
    w = a.shape[-1]
    return a * c + pltpu.roll(a, w - 16, 1) * sa + pltpu.roll(a, 16, 1) * sb


def _rope_bwd(g, c, sa, sb):
    w = g.shape[-1]
    return g * c + pltpu.roll(g * sa, 16, 1) + pltpu.roll(g * sb, w - 16, 1)


def _full(shape):
    return pl.BlockSpec(shape, lambda *_: (0,) * len(shape))


def _acc(shape):
    return pl.BlockSpec(shape, lambda *_: (0,) * len(shape))


def _full2(shape):
    return pl.BlockSpec(shape, lambda p, i: (0, 0))


def _cols(height, tm=TM):
    return pl.BlockSpec((height, tm), lambda i: (0, i))


def _rows(width, tm=TM):
    return pl.BlockSpec((tm, width), lambda i: (i, 0))


def _pre_fwd(x, tabs, gpre, win, gq, wuq, gkv, wk, wv):
    s = x.shape[0]
    c_t, sa_t, sb_t = tabs
    rw, cl = (lambda width: _rows(width, TM_PRE)), (lambda height: _cols(height, TM_PRE))

    def body(x_ref, c_ref, sa_ref, sb_ref, gpre_ref, win_ref, gq_ref, wuq_ref, gkv_ref, wk_ref, wv_ref,
             sbq_ref, sbk_ref, sbv_ref, sbg_ref, mlag_ref, cq_ref, ckv_ref, qc_ref, kc_ref, mv_ref,
             sbkt_ref, sbvt_ref, kct_ref, mvt_ref):
        xv = x_ref[...]
        r1 = lax.rsqrt(_rowmean(xv * xv) + EPS)
        h = (xv * r1 * gpre_ref[...]).astype(BF16)
        proj = _mm(h, win_ref[...])
        sbq_ref[...] = proj[:, 0:512].astype(BF16)
        sbk_ref[...] = proj[:, 512:1024].astype(BF16)
        sbv_ref[...] = proj[:, 1024:1536].astype(BF16)
        sbkt_ref[...] = proj[:, 512:1024].T.astype(BF16)
        sbvt_ref[...] = proj[:, 1024:1536].T.astype(BF16)
        sbg_ref[...] = proj[:, 1536:2048]
        cq = proj[:, 2048:2304]
        ckv = proj[:, 2304:2432]
        kr = proj[:, 2432:2560]
        mlag_ref[...] = proj[:, 2560:3072]
        cq_ref[...] = cq
        ckv_ref[...] = ckv
        c1, sa1, sb1 = c_ref[...], sa_ref[...], sb_ref[...]
        c8, sa8, sb8 = jnp.tile(c1, (1, 8)), jnp.tile(sa1, (1, 8)), jnp.tile(sb1, (1, 8))
        cqn = (cq * lax.rsqrt(_rowmean(cq * cq) + EPS) * gq_ref[...]).astype(BF16)
        qe = _mm(cqn, wuq_ref[...])
        qc_ref[...] = _rope_fwd(qe, c8, sa8, sb8).astype(BF16)
        ckvn = (ckv * lax.rsqrt(_rowmean(ckv * ckv) + EPS) * gkv_ref[...]).astype(BF16)
        ke = _mm(ckvn, wk_ref[...])
        krr = _rope_fwd(kr, c1, sa1, sb1)
        kcat = ke + jnp.tile(krr, (1, 8))
        kc_ref[...] = kcat.astype(BF16)
        kct_ref[...] = kcat.T.astype(BF16)
        mval = _mm(ckvn, wv_ref[...])
        mv_ref[...] = mval.astype(BF16)
        mvt_ref[...] = mval.T.astype(BF16)

    out_shape = (
        jax.ShapeDtypeStruct((s, 512), BF16), jax.ShapeDtypeStruct((s, 512), BF16), jax.ShapeDtypeStruct((s, 512), BF16),
        jax.ShapeDtypeStruct((s, 512), F32), jax.ShapeDtypeStruct((s, 512), F32),
        jax.ShapeDtypeStruct((s, Q_LORA), F32), jax.ShapeDtypeStruct((s, KV_LORA), F32),
        jax.ShapeDtypeStruct((s, 1024), BF16), jax.ShapeDtypeStruct((s, 1024), BF16), jax.ShapeDtypeStruct((s, 512), BF16),
        jax.ShapeDtypeStruct((512, s), BF16), jax.ShapeDtypeStruct((512, s), BF16), jax.ShapeDtypeStruct((1024, s), BF16),
        jax.ShapeDtypeStruct((512, s), BF16),
    )
    return pl.pallas_call(
        body, name="pre_fwd", grid=(s // TM_PRE,), out_shape=out_shape,
        in_specs=[rw(D_MODEL), rw(LANES), rw(LANES), rw(LANES), _full((1, D_MODEL)), _full((D_MODEL, D_EXT)),
                  _full((1, Q_LORA)), _full((Q_LORA, 1024)), _full((1, KV_LORA)), _full((KV_LORA, 1024)), _full((KV_LORA, 512))],
        out_specs=(rw(512), rw(512), rw(512), rw(512), rw(512), rw(Q_LORA), rw(KV_LORA),
                   rw(1024), rw(1024), rw(512), cl(512), cl(512), cl(1024), cl(512)),
        compiler_params=pltpu.CompilerParams(vmem_limit_bytes=VMEM_DENSE),
    )(x, c_t, sa_t, sb_t, gpre, win, gq, wuq, gkv, wk, wv)


def _softplus(z):
    neg_abs = lax.bitcast_convert_type(lax.bitcast_convert_type(z, jnp.uint32) | jnp.uint32(0x80000000), F32)
    return jnp.maximum(z, 0.0) + jnp.log(1.0 + jnp.exp(neg_abs))


def _sum_matrix(kind, terms):
    r, c = np.arange(2 * BK)[:, None], np.arange(2 * BK * terms)[None, :] % (2 * BK)
    rk, ck = r % BK, c % BK
    return _const(((r // BK) == (c // BK)) & {"suffix": ck >= rk, "prefix": ck <= rk}[kind])


def _split_rows(a):
    hi = a.astype(BF16)
    return jnp.concatenate([hi, (a - hi.astype(F32)).astype(BF16)], axis=0)


def _heads_t(blk, rowi):
    zero = jnp.zeros_like(blk)
    return jnp.concatenate([jnp.where(rowi < 64, blk, zero), jnp.where(rowi >= 64, blk, zero)], axis=1)


def _mask_keys(a, valid, fill=0.0):
    return jnp.concatenate([jnp.where(valid, a[0:BK], fill), jnp.where(valid, a[BK:2 * BK], fill)], axis=0)


def _split2(a):
    hi = a.astype(BF16)
    lo = (a - hi.astype(F32)).astype(BF16)
    return jnp.concatenate([hi, lo], axis=1)


def _pair_stack(b, lane):
    zero = jnp.zeros_like(b)
    return jnp.concatenate([jnp.where(lane < 64, b, zero), jnp.where(lane >= 64, b, zero)], axis=0)


def _sb_fwd(q, k, vt, late):
    s = q.shape[0]
    n = len(late)

    def body(q_ref, k_ref, vt_ref, usuf_ref, *rest):
        ins, o_ref, outs = rest[:n], rest[n], rest[n + 1:2 * n + 1]
        acc_scr, run_scr = rest[2 * n + 1:2 * n + 3]
        bufs, (send_sems, recv_sems, out_sems) = rest[2 * n + 3:3 * n + 3], rest[3 * n + 3:]
        p, i = pl.program_id(0), pl.program_id(1)
        gather_start, gather_forward, gather_finish = _gather_steps([a.shape for a in late], ins, bufs, send_sems, recv_sems)

        @pl.when((p == 0) & (i == 0))
        def _():
            gather_start()

        @pl.when((p == 2) & (i == 0))
        def _():
            gather_forward()

        lane = lax.broadcasted_iota(jnp.int32, (1, LANES), 1)
        rowi = lax.broadcasted_iota(jnp.int32, (LANES, 1), 0)
        keyi = lax.broadcasted_iota(jnp.int32, (BK, WQ), 0)
        qryi = lax.broadcasted_iota(jnp.int32, (BK, WQ), 1) + i * WQ
        qs = q_ref[...] * (HEAD_DIM ** -0.5)

        def group(blocks, masked, seen=None):
            seen = seen or [0] * len(blocks)
            starts = [pl.multiple_of(j * BK, BK) for j in blocks]
            valid = [(keyi[:, lo:] + j * BK) < qryi[:, lo:] if m else None for j, m, lo in zip(blocks, masked, seen)]
            zs = [_mm_nt(_pair_stack(k_ref[pl.ds(ks, BK), :], lane), qs[lo:]) for ks, lo in zip(starts, seen)]
            sps = [_softplus(z) for z in zs]
            sps = [sp if ok is None else _mask_keys(sp, ok) for sp, ok in zip(sps, valid)]
            cums = [_mm(usuf_ref[...], _split_rows(sp)) for sp in sps]
            ws = [jnp.exp(z - c) for z, c in zip(zs, cums)]
            ws = [w if ok is None else _mask_keys(w, ok) for w, ok in zip(ws, valid)]
            pvs = [_mm(_heads_t(vt_ref[:, pl.ds(ks, BK)], rowi), w.astype(BF16)) for ks, w in zip(starts, ws)]
            for pv, c, lo in zip(pvs, cums, seen):
                r0, r1 = run_scr[0:1, lo:], run_scr[1:2, lo:]
                acc_scr[:, lo:] += jnp.where(rowi < 64, jnp.exp(-r0), jnp.exp(-r1)) * pv
                run_scr[0:1, lo:] = r0 + c[0:1]
                run_scr[1:2, lo:] = r1 + c[BK:BK + 1]

        assert WQ == 2 * BK
        acc_scr[...] = jnp.zeros_like(acc_scr)
        run_scr[...] = jnp.zeros_like(run_scr)

        @pl.when(i == 0)
        def _():
            group([1, 0], [True, True], [BK, 0])

        @pl.when(i > 0)
        def _():
            group([2 * i + 1, 2 * i, 2 * i - 1, 2 * i - 2], [True, True, False, False], [BK, 0, 0, 0])

        def unfinished():
            return (jnp.min(run_scr[0:2, :]) < SB_CUTOFF).astype(jnp.int32)

        def step(c):
            group([2 * i - 1 - 2 * c[0], 2 * i - 2 - 2 * c[0]], [False, False])
            return c[0] + 1, unfinished()

        lax.while_loop(lambda c: (c[0] < i) & (c[1] > 0), step, (jnp.int32(1), unfinished()))
        o_ref[...] = acc_scr[...].T

        @pl.when((p == pl.num_programs(0) - 1) & (i == pl.num_programs(1) - 1))
        def _():
            gather_finish()
            copies = [pltpu.make_async_copy(bufs[t], outs[t], out_sems.at[t]) for t in range(n)]
            for cp in copies:
                cp.start()
            for cp in copies:
                cp.wait()

    qspec = pl.BlockSpec((WQ, LANES), lambda p, i: (i, p))
    kspec = pl.BlockSpec((s, LANES), lambda p, i: (0, p))
    tspec = pl.BlockSpec((LANES, s), lambda p, i: (p, 0))
    gathered = [jax.ShapeDtypeStruct((N_SHARD,) + a.shape, BF16) for a in late]
    return pl.pallas_call(
        body, name="sb_fwd", grid=(4, s // WQ),
        out_shape=(jax.ShapeDtypeStruct((s, 512), F32), *gathered),
        in_specs=[qspec, kspec, tspec, _full2((2 * BK, 4 * BK))] + [_full2(a.shape) for a in late],
        out_specs=(qspec,) + (pl.BlockSpec(memory_space=pl.ANY),) * n,
        scratch_shapes=[pltpu.VMEM((LANES, WQ), F32), pltpu.VMEM((8, WQ), F32)] + [pltpu.VMEM(g.shape, BF16) for g in gathered]
                       + [pltpu.SemaphoreType.DMA((6 * n,)), pltpu.SemaphoreType.DMA((6 * n,)), pltpu.SemaphoreType.DMA((n,))],
        compiler_params=pltpu.CompilerParams(vmem_limit_bytes=VMEM_ATTN),
    )(q, k, vt, _sum_matrix("suffix", 2), *late)


def _sb_bwd(q, k, kt, v, do, late):
    s = q.shape[0]
    n = len(late)
    halves = [a.shape[1] // 2 for a in late]

    def body(q_ref, k_ref, kt_ref, v_ref, do_ref, usuf_ref, upre_ref, *rest):
        g_refs, (dq_ref, dk_ref, dv_ref), outs = rest[:n], rest[n:n + 3], rest[n + 3:2 * n + 3]
        later_scr, dqt_scr, st_scr = rest[2 * n + 3:2 * n + 6]
        f_scr, reduce_scr, out_sems = rest[2 * n + 6:3 * n + 6], rest[3 * n + 6:-1], rest[-1]
        p, i = pl.program_id(0), pl.program_id(1)
        reduce_load, reduce_partial, reduce_total, reduce_finish = _reduce_steps(halves, g_refs, f_scr, reduce_scr)

        @pl.when((p == 0) & (i == 0))
        def _():
            reduce_load()

        @pl.when((p == 1) & (i == 0))
        def _():
            reduce_partial()

        @pl.when((p == 3) & (i == 0))
        def _():
            reduce_total()

        @pl.when(i == 0)
        def _():
            dk_ref[...] = jnp.zeros_like(dk_ref)
            dv_ref[...] = jnp.zeros_like(dv_ref)

        lane = lax.broadcasted_iota(jnp.int32, (1, LANES), 1)
        rowi = lax.broadcasted_iota(jnp.int32, (LANES, 1), 0)
        keyi = lax.broadcasted_iota(jnp.int32, (BK, WQ), 0)
        qryi = lax.broadcasted_iota(jnp.int32, (BK, WQ), 1) + i * WQ
        qs = q_ref[...] * (HEAD_DIM ** -0.5)
        dob = do_ref[...]
        dot = dob.astype(F32).T.astype(BF16)

        def scores(j, lo=0):
            return _mm_nt(_pair_stack(k_ref[pl.ds(pl.multiple_of(j * BK, BK), BK), :], lane), qs[lo:])

        def scan(blocks, masked, seen=None):
            seen = seen or [0] * len(blocks)
            sps = [_softplus(scores(j, lo)) for j, lo in zip(blocks, seen)]
            sps = [_mask_keys(sp, (keyi[:, lo:] + j * BK) < qryi[:, lo:]) if m else sp
                   for sp, j, m, lo in zip(sps, blocks, masked, seen)]
            for sp, j, lo in zip(sps, blocks, seen):
                run = st_scr[0:2, :]
                later_scr[j, 0:2, :] = run
                st_scr[0:2, lo:] = run[:, lo:] + jnp.concatenate([jnp.sum(sp[0:BK], axis=0, keepdims=True),
                                                                  jnp.sum(sp[BK:2 * BK], axis=0, keepdims=True)], axis=0)

        def sweep(blocks, masked, seen=None):
            seen = seen or [0] * len(blocks)
            starts = [pl.multiple_of(j * BK, BK) for j in blocks]
            valid = [(keyi[:, lo:] + j * BK) < qryi[:, lo:] if m else None for j, m, lo in zip(blocks, masked, seen)]
            zs = [scores(j, lo) for j, lo in zip(blocks, seen)]
            us = [jnp.exp(lax.bitcast_convert_type(lax.bitcast_convert_type(z, jnp.uint32) | jnp.uint32(0x80000000), F32))
                  for z in zs]
            sps = [jnp.maximum(z, 0.0) + jnp.log(1.0 + u) for z, u in zip(zs, us)]
            sps = [sp if ok is None else _mask_keys(sp, ok) for sp, ok in zip(sps, valid)]
            sigs = [jnp.where(z >= 0.0, 1.0, u) / (1.0 + u) for z, u in zip(zs, us)]
            cums = [_mm(usuf_ref[...], _split_rows(sp)) for sp in sps]
            dws = [_mm(_pair_stack(v_ref[pl.ds(ks, BK), :], lane), dot[:, lo:]) for ks, lo in zip(starts, seen)]
            wfs = []
            for z, c, j, ok, lo in zip(zs, cums, blocks, valid, seen):
                f = jnp.exp(-later_scr[j, 0:2, lo:])
                wide = (BK, WQ - lo)
                wf = jnp.exp(z - c) * jnp.concatenate([jnp.broadcast_to(f[0:1], wide), jnp.broadcast_to(f[1:2], wide)], axis=0)
                wfs.append(wf if ok is None else _mask_keys(wf, ok))
            es = [dw * wf for dw, wf in zip(dws, wfs)]
            pres = [_mm(upre_ref[...], e.astype(BF16)) for e in es]
            dzs = []
            for e, pre, sig, ok, lo in zip(es, pres, sigs, valid, seen):
                e0 = pre[0:BK] + st_scr[0:1, lo:]
                e1 = pre[BK:2 * BK] + st_scr[1:2, lo:]
                st_scr[0:1, lo:] = e0[BK - 1:BK]
                st_scr[1:2, lo:] = e1[BK - 1:BK]
                dz = e - sig * jnp.concatenate([e0, e1], axis=0)
                dzs.append((dz if ok is None else _mask_keys(dz, ok)).astype(BF16))
            whole = [b for b, lo in enumerate(seen) if lo == 0]
            dqt_scr[...] += _mm(jnp.concatenate([_heads_t(kt_ref[:, pl.ds(starts[b], BK)], rowi) for b in whole], axis=1),
                                jnp.concatenate([dzs[b] for b in whole], axis=0))
            for b, lo in enumerate(seen):
                if lo:
                    dqt_scr[:, lo:] += _mm(_heads_t(kt_ref[:, pl.ds(starts[b], BK)], rowi), dzs[b])
            for ks, dz, wf, lo in zip(starts, dzs, wfs, seen):
                rk = _mm(dz, qs[lo:])
                dk_ref[pl.ds(ks, BK), :] += jnp.where(lane < 64, rk[0:BK], rk[BK:2 * BK])
                rv = _mm(wf.astype(BF16), dob[lo:])
                dv_ref[pl.ds(ks, BK), :] += jnp.where(lane < 64, rv[0:BK], rv[BK:2 * BK])

        assert WQ == 2 * BK
        st_scr[...] = jnp.zeros_like(st_scr)

        @pl.when(i == 0)
        def _():
            scan([1, 0], [True, True], [BK, 0])

        @pl.when(i > 0)
        def _():
            scan([2 * i + 1, 2 * i, 2 * i - 1, 2 * i - 2], [True, True, False, False], [BK, 0, 0, 0])

        def unfinished():
            return (jnp.min(st_scr[0:2, :]) < SB_CUTOFF).astype(jnp.int32)

        def step(c):
            scan([2 * i - 1 - 2 * c[0], 2 * i - 2 - 2 * c[0]], [False, False])
            return c[0] + 1, unfinished()

        npairs, _ = lax.while_loop(lambda c: (c[0] < i) & (c[1] > 0), step, (jnp.minimum(i, 1), unfinished()))

        st_scr[...] = jnp.zeros_like(st_scr)
        dqt_scr[...] = jnp.zeros_like(dqt_scr)
        first = 2 * (i - npairs)

        def early(t, carry):
            sweep([first + 2 * t, first + 2 * t + 1], [False, False])
            return carry

        lax.fori_loop(0, npairs - 1, early, 0)

        @pl.when(i == 0)
        def _():
            sweep([0, 1], [True, True], [0, BK])

        @pl.when(i > 0)
        def _():
            sweep([2 * i - 2, 2 * i - 1, 2 * i, 2 * i + 1], [False, False, True, True], [0, 0, 0, BK])

        dq_ref[...] = (dqt_scr[...].T * (HEAD_DIM ** -0.5)).astype(BF16)

        @pl.when((p == pl.num_programs(0) - 1) & (i == pl.num_programs(1) - 1))
        def _():
            reduce_finish()
            copies = [pltpu.make_async_copy(f_scr[t], outs[t], out_sems.at[t]) for t in range(n)]
            for cp in copies:
                cp.start()
            for cp in copies:
                cp.wait()

    qspec = pl.BlockSpec((WQ, LANES), lambda p, i: (i, p))
    kspec = pl.BlockSpec((s, LANES), lambda p, i: (0, p))
    tspec = pl.BlockSpec((LANES, s), lambda p, i: (p, 0))
    anywhere = pl.BlockSpec(memory_space=pl.ANY)
    reduced = [jax.ShapeDtypeStruct(a.shape[1:], F32) for a in late]
    return pl.pallas_call(
        body, name="sb_bwd", grid=(4, s // WQ),
        out_shape=(jax.ShapeDtypeStruct((s, 512), BF16), jax.ShapeDtypeStruct((s, 512), F32),
                   jax.ShapeDtypeStruct((s, 512), F32), *reduced),
        in_specs=[qspec, kspec, tspec, kspec, qspec, _full2((2 * BK, 4 * BK)), _full2((2 * BK, 2 * BK))] + [anywhere] * n,
        out_specs=(qspec, kspec, kspec) + (anywhere,) * n,
        scratch_shapes=[pltpu.VMEM((s // BK, 8, WQ), F32), pltpu.VMEM((LANES, WQ), F32), pltpu.VMEM((8, WQ), F32)]
                       + [pltpu.VMEM(r.shape, F32) for r in reduced] + _reduce_scratch(late) + [pltpu.SemaphoreType.DMA((n,))],
        compiler_params=pltpu.CompilerParams(vmem_limit_bytes=VMEM_ATTN),
    )(q, k, kt, v, do, _sum_matrix("suffix", 2), _sum_matrix("prefix", 1), *late)


MLA_SCALE = (QK_NOPE + QK_ROPE) ** -0.5
LOG2E = 1.4426950408889634


def _mla_keys(kb):
    zero = jnp.zeros((BK, LANES), kb.dtype)
    return jnp.concatenate([jnp.concatenate([kb[:, 0:LANES], zero], axis=1),
                            jnp.concatenate([zero, kb[:, LANES:2 * LANES]], axis=1)], axis=0)


def _mla_fwd(qc, kc, vt):
    s = qc.shape[0]
    mq = min(MQ_FWD, s)
    rows_l = 16

    def body(q_ref, k_ref, vt_ref, o_ref, l_ref, p_scr, ot_scr, st_scr):
        i = pl.program_id(1)
        keyc = lax.broadcasted_iota(jnp.int32, (BK, mq), 0)
        qryc = (lax.broadcasted_iota(jnp.int32, (BK, mq), 1) + i * mq) // 64
        row = lax.broadcasted_iota(jnp.int32, (LANES, 1), 0)
        qw = q_ref[...]
        orow = lax.broadcasted_iota(jnp.int32, (rows_l, 2 * BK), 0)
        ocol = lax.broadcasted_iota(jnp.int32, (rows_l, 2 * BK), 1)
        ones = jnp.where(((orow == 0) & (ocol < BK)) | ((orow == 1) & (ocol >= BK)), 1.0, 0.0).astype(BF16)

        def scores(j, lo=0):
            ks = pl.multiple_of(j * BK, BK)
            return _mm_nt(_mla_keys(k_ref[pl.ds(ks, BK), :]), qw[lo:])

        def values_t(j):
            vtb = vt_ref[:, pl.ds(pl.multiple_of(j * BK, BK), BK)]
            zero = jnp.zeros_like(vtb)
            top = jnp.concatenate([jnp.where(row < 64, vtb, zero), jnp.where(row >= 64, vtb, zero)], axis=1)
            return jnp.concatenate([top, ones], axis=0)

        def softmax(ja, za, zb, masked, lo=0):
            c = MLA_SCALE * LOG2E
            parts = [za[0:BK] * c, za[BK:2 * BK] * c, zb[0:BK] * c, zb[BK:2 * BK] * c]
            if masked:
                va = ((keyc[:, lo:] + ja * BK) // 64) <= qryc[:, lo:]
                vb = ((keyc[:, lo:] + (ja + 1) * BK) // 64) <= qryc[:, lo:]
                parts = [jnp.where(va, parts[0], -1e30), jnp.where(va, parts[1], -1e30),
                         jnp.where(vb, parts[2], -1e30), jnp.where(vb, parts[3], -1e30)]
            m0, m1 = st_scr[0:1, lo:], st_scr[1:2, lo:]
            n0 = jnp.maximum(m0, jnp.max(jnp.maximum(parts[0], parts[2]), axis=0, keepdims=True))
            n1 = jnp.maximum(m1, jnp.max(jnp.maximum(parts[1], parts[3]), axis=0, keepdims=True))
            st_scr[2:3, lo:] = jnp.exp2(m0 - n0)
            st_scr[3:4, lo:] = jnp.exp2(m1 - n1)
            st_scr[0:1, lo:] = n0
            st_scr[1:2, lo:] = n1
            p_scr[:, lo:] = jnp.concatenate([jnp.exp2(parts[0] - n0), jnp.exp2(parts[1] - n1),
                                             jnp.exp2(parts[2] - n0), jnp.exp2(parts[3] - n1)], axis=0).astype(BF16)

        def accumulate(ja, lo=0):
            pv = _mm(jnp.concatenate([values_t(ja), values_t(ja + 1)], axis=1), p_scr[:, lo:])
            a = jnp.where(row < 64, st_scr[2:3, lo:], st_scr[3:4, lo:])
            ot_scr[0:LANES, lo:] = a * ot_scr[0:LANES, lo:] + pv[0:LANES]
            ot_scr[LANES:LANES + 8, lo:] = st_scr[2:10, lo:] * ot_scr[LANES:LANES + 8, lo:] + pv[LANES:LANES + 8]

        def step(n, masked, lo=0, prev_lo=0):
            za, zb = scores(2 * n, lo), scores(2 * n + 1, lo)
            accumulate(2 * n - 2, prev_lo)
            softmax(2 * n, za, zb, masked, lo)

        def first(masked):
            softmax(0, scores(0), scores(1), masked)

        st_scr[...] = jnp.concatenate([jnp.full((2, mq), -1e30, F32), jnp.ones((14, mq), F32)], axis=0)
        ot_scr[...] = jnp.zeros_like(ot_scr)

        npq = mq // (2 * BK)
        seen = lambda d: 2 * BK * max(d, 0)

        @pl.when(i == 0)
        def _():
            first(True)
            for d in range(1, npq):
                step(d, True, seen(d), seen(d - 1))

        @pl.when(i > 0)
        def _():
            first(False)
            lax.fori_loop(1, npq * i, lambda n, c: (step(n, False), c)[1], 0)
            for d in range(npq):
                step(npq * i + d, True, seen(d), seen(d - 1))

        accumulate(2 * (npq * (i + 1) - 1), seen(npq - 1))
        l0, l1 = ot_scr[LANES:LANES + 1, :], ot_scr[LANES + 1:LANES + 2, :]
        o_ref[...] = (ot_scr[0:LANES, :] / jnp.where(row < 64, l0, l1)).T
        l_ref[...] = jnp.where(row < 64, st_scr[0:1, :] + jnp.log2(l0), st_scr[1:2, :] + jnp.log2(l1)).T

    qspec = pl.BlockSpec((mq, 2 * LANES), lambda p, i: (i, p))
    kspec = pl.BlockSpec((s, 2 * LANES), lambda p, i: (0, p))
    vtspec = pl.BlockSpec((LANES, s), lambda p, i: (p, 0))
    ospec = pl.BlockSpec((mq, LANES), lambda p, i: (i, p))
    return pl.pallas_call(
        body, name="mla_fwd", grid=(4, s // mq),
        out_shape=(jax.ShapeDtypeStruct((s, 512), F32), jax.ShapeDtypeStruct((s, 512), F32)),
        in_specs=[qspec, kspec, vtspec], out_specs=(ospec, ospec),
        scratch_shapes=[pltpu.VMEM((4 * BK, mq), BF16), pltpu.VMEM((LANES + 8, mq), F32), pltpu.VMEM((16, mq), F32)],
        compiler_params=pltpu.CompilerParams(vmem_limit_bytes=VMEM_ATTN),
    )(qc, kc, vt)


def _mla_bwd(qc, kc, kct, v, do, lse, delta):
    s = qc.shape[0]
    mq = min(MQ_BWD, s)

    def body(q_ref, k_ref, kt_ref, v_ref, do_ref, l_ref, d_ref, dq_ref, dk_ref, dv_ref, dqt_scr, p_scr, dz_scr):
        i = pl.program_id(1)

        @pl.when(i == 0)
        def _():
            dk_ref[...] = jnp.zeros_like(dk_ref)
            dv_ref[...] = jnp.zeros_like(dv_ref)

        lane = lax.broadcasted_iota(jnp.int32, (1, LANES), 1)
        keyc = lax.broadcasted_iota(jnp.int32, (BK, mq), 0)
        qryc = (lax.broadcasted_iota(jnp.int32, (BK, mq), 1) + i * mq) // 64
        qw = q_ref[...]
        dob = do_ref[...]
        dost = (dob.astype(F32) * MLA_SCALE).T.astype(BF16)
        lt = l_ref[...].T
        dt = (d_ref[...] * MLA_SCALE).T
        lse0, lse1 = lt[0:1], lt[64:65]
        dl0, dl1 = dt[0:1], dt[64:65]
        dqt_scr[...] = jnp.zeros_like(dqt_scr)

        def products(j, lo=0):
            ks = pl.multiple_of(j * BK, BK)
            return (_mm_nt(_mla_keys(k_ref[pl.ds(ks, BK), :]), qw[lo:]),
                    _mm(_pair_stack(v_ref[pl.ds(ks, BK), :], lane), dost[:, lo:]))

        def grads(j, slot, zt, dwt, masked, lo=0):
            zt = zt * (MLA_SCALE * LOG2E)
            p0 = jnp.exp2(zt[0:BK] - lse0[:, lo:])
            p1 = jnp.exp2(zt[BK:2 * BK] - lse1[:, lo:])
            if masked:
                valid = ((keyc[:, lo:] + j * BK) // 64) <= qryc[:, lo:]
                p0, p1 = jnp.where(valid, p0, 0.0), jnp.where(valid, p1, 0.0)
            p_scr[slot, :, lo:] = jnp.concatenate([p0, p1], axis=0).astype(BF16)
            dz_scr[slot, :, lo:] = jnp.concatenate([p0 * (dwt[0:BK] - dl0[:, lo:]), p1 * (dwt[BK:2 * BK] - dl1[:, lo:])],
                                                   axis=0).astype(BF16)

        def keys_t(ks):
            ktb = kt_ref[:, pl.ds(ks, BK)]
            zero = jnp.zeros((LANES, BK), ktb.dtype)
            return jnp.concatenate([jnp.concatenate([ktb[0:LANES], zero], axis=1),
                                    jnp.concatenate([zero, ktb[LANES:2 * LANES]], axis=1)], axis=0)

        def scatter(ja, lo=0):
            ksa, ksb = pl.multiple_of(ja * BK, BK), pl.multiple_of((ja + 1) * BK, BK)
            dqt_scr[:, lo:] += _mm(jnp.concatenate([keys_t(ksa), keys_t(ksb)], axis=1),
                                   jnp.concatenate([dz_scr[0, :, lo:], dz_scr[1, :, lo:]], axis=0))
            for slot, ks in ((0, ksa), (1, ksb)):
                rk = _mm(dz_scr[slot, :, lo:], qw[lo:])
                dk_ref[pl.ds(ks, BK), :] += jnp.concatenate([rk[0:BK, 0:LANES], rk[BK:2 * BK, LANES:2 * LANES]], axis=1)
                rv = _mm(p_scr[slot, :, lo:], dob[lo:])
                dv_ref[pl.ds(ks, BK), :] += jnp.where(lane < 64, rv[0:BK], rv[BK:2 * BK])

        def step(n, masked, lo=0, prev_lo=0):
            za, wa = products(2 * n, lo)
            zb, wb = products(2 * n + 1, lo)
            scatter(2 * n - 2, prev_lo)
            grads(2 * n, 0, za, wa, masked, lo)
            grads(2 * n + 1, 1, zb, wb, masked, lo)

        def first(masked):
            za, wa = products(0)
            zb, wb = products(1)
            grads(0, 0, za, wa, masked)
            grads(1, 1, zb, wb, masked)

        npq = mq // (2 * BK)
        seen = lambda d: 2 * BK * max(d, 0)

        @pl.when(i == 0)
        def _():
            first(True)
            for d in range(1, npq):
                step(d, True, seen(d), seen(d - 1))

        @pl.when(i > 0)
        def _():
            first(False)
            lax.fori_loop(1, npq * i, lambda n, c: (step(n, False), c)[1], 0)
            for d in range(npq):
                step(npq * i + d, True, seen(d), seen(d - 1))

        scatter(2 * (npq * (i + 1) - 1), seen(npq - 1))
        dq_ref[...] = dqt_scr[...].T

    qspec = pl.BlockSpec((mq, 2 * LANES), lambda p, i: (i, p))
    kspec = pl.BlockSpec((s, 2 * LANES), lambda p, i: (0, p))
    ktspec = pl.BlockSpec((2 * LANES, s), lambda p, i: (p, 0))
    vspec = pl.BlockSpec((s, LANES), lambda p, i: (0, p))
    ospec = pl.BlockSpec((mq, LANES), lambda p, i: (i, p))
    return pl.pallas_call(
        body, name="mla_bwd", grid=(4, s // mq),
        out_shape=(jax.ShapeDtypeStruct((s, 1024), F32), jax.ShapeDtypeStruct((s, 1024), F32),
                   jax.ShapeDtypeStruct((s, 512), F32)),
        in_specs=[qspec, kspec, ktspec, vspec, ospec, ospec, ospec], out_specs=(qspec, kspec, vspec),
        scratch_shapes=[pltpu.VMEM((2 * LANES, mq), F32), pltpu.VMEM((2, 2 * BK, mq), BF16), pltpu.VMEM((2, 2 * BK, mq), BF16)],
        compiler_params=pltpu.CompilerParams(vmem_limit_bytes=VMEM_ATTN),
    )(qc, kc, kct, v, do, lse, delta)


def _post(x, p, tgt, sbo, mlao, sbg, mlag, gsb, gmla, wout, gpost, wple, gple, wpg, bpg):
    s = x.shape[0]

    def body(x_ref, p_ref, t_ref, sbo_ref, mlao_ref, sbg_ref, mlag_ref, gsb_ref, gmla_ref, wout_ref,
             gpost_ref, wple_ref, gple_ref, wpg_ref, bpg_ref, bd_ref,
             dsbo_ref, dmlao_ref, delta_ref, dsbg_ref, dmlag_ref, dxres_ref, dwout_ref, dwpg_ref, dwple_ref, vec_ref):
        i = pl.program_id(0)

        @pl.when(i == 0)
        def _():
            dwout_ref[...] = jnp.zeros_like(dwout_ref)
            dwpg_ref[...] = jnp.zeros_like(dwpg_ref)
            dwple_ref[...] = jnp.zeros_like(dwple_ref)
            vec_ref[...] = jnp.zeros_like(vec_ref)

        inv_hd = 1.0 / HEAD_DIM

        def head_fwd(o, g, gate):
            r = lax.rsqrt(_seg(o * o, bd_ref[...]) * inv_hd + EPS)
            hat = o * r
            n = hat * g
            sg = _sigmoid(gate)
            return hat, r, n, sg, n * (gate * sg)

        sbo, mlao, sbg_v, mlag_v = sbo_ref[...], mlao_ref[...], sbg_ref[...], mlag_ref[...]
        gsb_v, gmla_v = gsb_ref[...], gmla_ref[...]
        sb_hat, sb_r, sb_n, sb_sg, sb_y = head_fwd(sbo, gsb_v, sbg_v)
        ml_hat, ml_r, ml_n, ml_sg, ml_y = head_fwd(mlao, gmla_v, mlag_v)
        mix = jnp.concatenate([sb_y, ml_y], axis=1).astype(BF16)
        y = _mm(mix, wout_ref[...])
        ry = lax.rsqrt(_rowmean(y * y) + EPS)
        y_hat = y * ry
        gpost_v = gpost_ref[...]
        x1 = x_ref[...] + y_hat * gpost_v
        pb = p_ref[...].astype(BF16)
        pl_ = _mm(pb, wple_ref[...])
        rp = lax.rsqrt(_rowmean(pl_ * pl_) + EPS)
        pl_hat = pl_ * rp
        gple_v = gple_ref[...]
        ple = pl_hat * gple_v
        x1b = x1.astype(BF16)
        gate = _sigmoid(_mm(x1b, wpg_ref[...]) + bpg_ref[...])
        err = x1 + ple * gate - t_ref[...]
        loss = 0.5 * jnp.sum(_rowmean(err * err))
        dout = err * (1.0 / D_MODEL)

        du = dout * ple * gate * (1.0 - gate)
        dub = du.astype(BF16)
        dple = dout * gate
        dx1 = dout + _mm_nt(dub, wpg_ref[...])
        dwpg_ref[...] += _mm_tn(x1b, dub)
        dplh = dple * gple_v
        dpl = rp * (dplh - pl_hat * _rowmean(dplh * pl_hat))
        dwple_ref[...] += _mm_tn(pb, dpl.astype(BF16))
        dxres_ref[...] = dx1
        dyh = dx1 * gpost_v
        dy = ry * (dyh - y_hat * _rowmean(dyh * y_hat))
        dyb = dy.astype(BF16)
        dwout_ref[...] += _mm_tn(mix, dyb)
        dmix = _mm_nt(dyb, wout_ref[...])

        def head_bwd(dyv, hat, r, n, sg, g, gate):
            dn = dyv * (gate * sg)
            dgate = dyv * n * (sg * (1.0 + gate * (1.0 - sg)))
            dhat = dn * g
            do = r * (dhat - hat * (_seg(dhat * hat, bd_ref[...]) * inv_hd))
            return do, dgate, _colsum(dn * hat)

        dsbo, dsbg, dg_sb = head_bwd(dmix[:, 0:512], sb_hat, sb_r, sb_n, sb_sg, gsb_v, sbg_v)
        dmlao, dmlag, dg_ml = head_bwd(dmix[:, 512:1024], ml_hat, ml_r, ml_n, ml_sg, gmla_v, mlag_v)
        dsbo_ref[...] = dsbo.astype(BF16)
        dmlao_ref[...] = dmlao.astype(BF16)
        delta_ref[...] = _seg(dmlao * mlao, bd_ref[...])
        dsbg_ref[...] = dsbg.astype(BF16)
        dmlag_ref[...] = dmlag.astype(BF16)
        vec_ref[pl.ds(0, 1), :] += _colsum(dx1 * y_hat)
        vec_ref[pl.ds(1, 1), :] += _colsum(dple * pl_hat)
        vec_ref[pl.ds(2, 1), :] += _colsum(du)
        vec_ref[pl.ds(3, 1), :] += jnp.concatenate([dg_sb, dg_ml], axis=1)
        vec_ref[pl.ds(4, 1), :] += jnp.full((1, D_MODEL), loss, F32)

    out_shape = (
        jax.ShapeDtypeStruct((s, 512), BF16), jax.ShapeDtypeStruct((s, 512), BF16), jax.ShapeDtypeStruct((s, 512), F32),
        jax.ShapeDtypeStruct((s, 512), BF16), jax.ShapeDtypeStruct((s, 512), BF16), jax.ShapeDtypeStruct((s, D_MODEL), F32),
        jax.ShapeDtypeStruct((D_MODEL, D_MODEL), F32), jax.ShapeDtypeStruct((D_MODEL, D_MODEL), F32),
        jax.ShapeDtypeStruct((PLE_DIM, D_MODEL), F32), jax.ShapeDtypeStruct((8, D_MODEL), F32),
    )
    return pl.pallas_call(
        body, name="post_fwd_bwd", grid=(s // TM,), out_shape=out_shape,
        in_specs=[_rows(D_MODEL), _rows(PLE_DIM), _rows(D_MODEL), _rows(512), _rows(512), _rows(512), _rows(512),
                  _full((1, 512)), _full((1, 512)), _full((D_MODEL, D_MODEL)),
                  _full((1, D_MODEL)), _full((PLE_DIM, D_MODEL)), _full((1, D_MODEL)), _full((D_MODEL, D_MODEL)),
                  _full((1, D_MODEL)), _full((1024, 512))],
        out_specs=(_rows(512), _rows(512), _rows(512), _rows(512), _rows(512), _rows(D_MODEL),
                   _acc((D_MODEL, D_MODEL)), _acc((D_MODEL, D_MODEL)), _acc((PLE_DIM, D_MODEL)), _acc((8, D_MODEL))),
        compiler_params=pltpu.CompilerParams(vmem_limit_bytes=VMEM_DENSE),
    )(x, p, tgt, sbo, mlao, sbg, mlag, gsb, gmla, wout, gpost, wple, gple, wpg, bpg, _blockdiag2(512, HEAD_DIM))


def _pre_bwd(x, dxres, dsbq, dsbk, dsbv, dsbg, dmlag, dqc, dkc, dmv, cq, ckv, tabs, gpre, win, gq, wuq, gkv, wk, wv):
    s = x.shape[0]
    c_t, sa_t, sb_t = tabs
    rw = _rows

    def body(x_ref, dxres_ref, dsbq_ref, dsbk_ref, dsbv_ref, dsbg_ref, dmlag_ref, dqc_ref, dkc_ref, dmv_ref, cq_ref,
             ckv_ref, c_ref, sa_ref, sb_ref, gpre_ref, win_ref, gq_ref, wuq_ref, gkv_ref, wk_ref, wv_ref,
             gx_ref, dwin_ref, dwuq_ref, dwk_ref, dwv_ref, vec_ref, dwin_acc):
        i = pl.program_id(0)

        @pl.when(i == 0)
        def _():
            dwin_acc[...] = jnp.zeros_like(dwin_acc)
            dwuq_ref[...] = jnp.zeros_like(dwuq_ref)
            dwk_ref[...] = jnp.zeros_like(dwk_ref)
            dwv_ref[...] = jnp.zeros_like(dwv_ref)
            vec_ref[...] = jnp.zeros_like(vec_ref)

        lane = lax.broadcasted_iota(jnp.int32, (1, LANES), 1)
        c1, sa1, sb1 = c_ref[...], sa_ref[...], sb_ref[...]
        c8, sa8, sb8 = jnp.tile(c1, (1, 8)), jnp.tile(sa1, (1, 8)), jnp.tile(sb1, (1, 8))

        def norm_bwd(dn, hat, r, g):
            t = dn * g
            return r * (t - hat * _rowmean(t * hat)), _colsum(dn * hat)

        xv = x_ref[...]
        r1 = lax.rsqrt(_rowmean(xv * xv) + EPS)
        x_hat = xv * r1
        gpre_v = gpre_ref[...]
        hb = (x_hat * gpre_v).astype(BF16)
        ready = jnp.concatenate([dsbq_ref[...], dsbk_ref[...].astype(BF16), dsbv_ref[...].astype(BF16), dsbg_ref[...]], axis=1)
        dmlag = dmlag_ref[...]
        dwin_acc[:, 0:2048] += _mm_tn(hb, ready)
        dwin_acc[:, 2560:3072] += _mm_tn(hb, dmlag)
        dh = _mm_nt(ready, win_ref[:, 0:2048]) + _mm_nt(dmlag, win_ref[:, 2560:3072])

        dqeb = _rope_bwd(dqc_ref[...], c8, sa8, sb8).astype(BF16)
        cq = cq_ref[...]
        rq = lax.rsqrt(_rowmean(cq * cq) + EPS)
        cq_hat = cq * rq
        gq_v = gq_ref[...]
        dwuq_ref[...] += _mm_tn((cq_hat * gq_v).astype(BF16), dqeb)
        dcq, dg_q = norm_bwd(_mm_nt(dqeb, wuq_ref[...]), cq_hat, rq, gq_v)

        dkc = dkc_ref[...]
        dkcb = dkc.astype(BF16)
        dmvb = dmv_ref[...].astype(BF16)
        ckv = ckv_ref[...]
        rkv = lax.rsqrt(_rowmean(ckv * ckv) + EPS)
        ckv_hat = ckv * rkv
        gkv_v = gkv_ref[...]
        ckvnb = (ckv_hat * gkv_v).astype(BF16)
        dwk_ref[...] += _mm_tn(ckvnb, dkcb)
        dwv_ref[...] += _mm_tn(ckvnb, dmvb)
        dckv, dg_kv = norm_bwd(_mm_nt(dkcb, wk_ref[...]) + _mm_nt(dmvb, wv_ref[...]), ckv_hat, rkv, gkv_v)

        dkr = dkc[:, 0:LANES]
        for hh in range(1, 8):
            dkr = dkr + dkc[:, LANES * hh:LANES * (hh + 1)]
        dkr = _rope_bwd(dkr, c1, sa1, sb1)
        dkr = jnp.where((lane >= 64) & (lane < 96), dkr, 0.0)

        late = jnp.concatenate([dcq.astype(BF16), dckv.astype(BF16), dkr.astype(BF16)], axis=1)
        dwin_acc[:, 2048:2560] += _mm_tn(hb, late)
        dx, dg_pre = norm_bwd(dh + _mm_nt(late, win_ref[:, 2048:2560]), x_hat, r1, gpre_v)
        gx_ref[...] = dxres_ref[...] + dx
        vec_ref[pl.ds(0, 1), :] += dg_pre
        vec_ref[pl.ds(1, 1), :] += jnp.concatenate([dg_q, dg_kv, jnp.zeros((1, D_MODEL - Q_LORA - KV_LORA), F32)], axis=1)

        @pl.when(i == pl.num_programs(0) - 1)
        def _():
            pltpu.sync_copy(dwin_acc, dwin_ref)

    out_shape = (
        jax.ShapeDtypeStruct((s, D_MODEL), F32), jax.ShapeDtypeStruct((D_MODEL, D_EXT), F32),
        jax.ShapeDtypeStruct((Q_LORA, 1024), F32), jax.ShapeDtypeStruct((KV_LORA, 1024), F32),
        jax.ShapeDtypeStruct((KV_LORA, 512), F32), jax.ShapeDtypeStruct((8, D_MODEL), F32),
    )
    return pl.pallas_call(
        body, name="pre_bwd", grid=(s // TM,), out_shape=out_shape,
        in_specs=[rw(D_MODEL), rw(D_MODEL), rw(512), rw(512), rw(512), rw(512), rw(512),
                  rw(1024), rw(1024), rw(512), rw(Q_LORA), rw(KV_LORA), rw(LANES), rw(LANES),
                  rw(LANES), _full((1, D_MODEL)), _full((D_MODEL, D_EXT)), _full((1, Q_LORA)), _full((Q_LORA, 1024)),
                  _full((1, KV_LORA)), _full((KV_LORA, 1024)), _full((KV_LORA, 512))],
        out_specs=(rw(D_MODEL), pl.BlockSpec(memory_space=pl.ANY), _acc((Q_LORA, 1024)), _acc((KV_LORA, 1024)),
                   _acc((KV_LORA, 512)), _acc((8, D_MODEL))),
        scratch_shapes=[pltpu.VMEM((D_MODEL, D_EXT), F32)],
        compiler_params=pltpu.CompilerParams(vmem_limit_bytes=VMEM_DENSE),
    )(x, dxres, dsbq, dsbk, dsbv, dsbg, dmlag, dqc, dkc, dmv, cq, ckv, c_t, sa_t, sb_t, gpre, win, gq, wuq, gkv, wk, wv)


def _place():
    return lax.axis_index("x"), lax.axis_index("y"), lax.axis_index("c")


def _gather_steps(shapes, ins, bufs, send_sems, recv_sems):
    n = len(shapes)
    x, y, c = _place()
    me, sib = (x, y, c), (x, y, 1 - c)
    chips = [(1 - x, y), (x, 1 - y), (1 - x, 1 - y)]

    def half(t, chip, hc):
        rows = shapes[t][0] // 2
        return bufs[t].at[2 * chip[0] + chip[1], pl.ds(pl.multiple_of(hc * rows, 16), rows), :]

    def copy(k, t, chip, hc, to):
        return pltpu.make_async_remote_copy(src_ref=half(t, chip, hc), dst_ref=half(t, chip, hc), send_sem=send_sems.at[k],
                                            recv_sem=recv_sems.at[k], device_id=to, device_id_type=MESH)

    def start():
        for t in range(n):
            bufs[t][2 * x + y] = ins[t][...].astype(BF16)
            for j, chip in enumerate(chips):
                copy(6 * t + j, t, (x, y), c, (*chip, c)).start()

    def forward():
        for t in range(n):
            for j, chip in enumerate(chips):
                copy(6 * t + j, t, chip, c, me).wait_recv()
                copy(6 * t + 3 + j, t, chip, c, sib).start()

    def finish():
        for t in range(n):
            for j, chip in enumerate(chips):
                copy(6 * t + 3 + j, t, chip, 1 - c, me).wait_recv()
        for t in range(n):
            for j, chip in enumerate(chips):
                copy(6 * t + j, t, (x, y), c, (*chip, c)).wait_send()
                copy(6 * t + 3 + j, t, chip, c, sib).wait_send()

    return start, forward, finish


def _allgather_weights(shards):
    n = len(shards)

    def body(*refs):
        start, forward, finish = _gather_steps([a.shape for a in shards], refs[:n], refs[n:2 * n], refs[2 * n], refs[2 * n + 1])
        start()
        forward()
        finish()

    return pl.pallas_call(
        body, name="allgather_weights",
        out_shape=tuple(jax.ShapeDtypeStruct((N_SHARD,) + a.shape, BF16) for a in shards),
        in_specs=[pl.BlockSpec(memory_space=pltpu.VMEM)] * n, out_specs=(pl.BlockSpec(memory_space=pltpu.VMEM),) * n,
        scratch_shapes=[pltpu.SemaphoreType.DMA((6 * n,)), pltpu.SemaphoreType.DMA((6 * n,))],
        compiler_params=pltpu.CompilerParams(vmem_limit_bytes=VMEM_ATTN),
    )(*shards)


def _reduce_scratch(gsh):
    n = len(gsh)
    half_shapes = [(N_SHARD, a.shape[1] // 2, a.shape[2]) for a in gsh]
    return ([pltpu.VMEM(s_, F32) for s_ in half_shapes] * 2 + [pltpu.VMEM(s_, BF16) for s_ in half_shapes] * 2
            + [pltpu.SemaphoreType.DMA((n,)), pltpu.SemaphoreType.DMA((5 * n,)), pltpu.SemaphoreType.DMA((5 * n,))])


def _reduce_steps(halves, g_refs, f_refs, scratch):
    n = len(halves)
    accs, sibs, sbufs, rbufs = scratch[0:n], scratch[n:2 * n], scratch[2 * n:3 * n], scratch[3 * n:4 * n]
    local_sems, send_sems, recv_sems = scratch[4 * n:4 * n + 3]
    x, y, c = _place()
    me, sib = (x, y, c), (x, y, 1 - c)
    mine = 2 * x + y
    chips = [(1 - x, y), (x, 1 - y), (1 - x, 1 - y)]

    def remote(k, src, dst, to):
        return pltpu.make_async_remote_copy(src_ref=src, dst_ref=dst, send_sem=send_sems.at[k], recv_sem=recv_sems.at[k],
                                            device_id=to, device_id_type=MESH)

    def half3(ref, t, hc):
        return ref.at[:, pl.ds(pl.multiple_of(hc * halves[t], 8), halves[t]), :]

    def half2(ref, t, hc):
        return ref.at[pl.ds(pl.multiple_of(hc * halves[t], 8), halves[t]), :]

    def mine_load(t):
        return pltpu.make_async_copy(half3(g_refs[t], t, c), accs[t], local_sems.at[t])

    def to_sibling(t, to):
        return remote(t, half3(g_refs[t], t, 1 - c), sibs[t], to)

    def to_chip(t, j, chip, to):
        idx = 2 * chip[0] + chip[1]
        return remote(n + 3 * t + j, sbufs[t].at[idx], rbufs[t].at[mine if to is not me else idx], to)

    def swap(t, hc, to):
        return remote(4 * n + t, half2(f_refs[t], t, hc), half2(f_refs[t], t, hc), to)

    def load():
        for t in range(n):
            mine_load(t).start()
            to_sibling(t, sib).start()

    def partial():
        for t in range(n):
            mine_load(t).wait()
            to_sibling(t, me).wait_recv()
            for k in range(N_SHARD):
                accs[t][k] = accs[t][k] + sibs[t][k]
            for j, chip in enumerate(chips):
                idx = 2 * chip[0] + chip[1]
                sbufs[t][idx] = accs[t][idx].astype(BF16)
                to_chip(t, j, chip, (*chip, c)).start()

    def total():
        for t in range(n):
            acc = accs[t][mine]
            for j, chip in enumerate(chips):
                to_chip(t, j, chip, me).wait_recv()
                acc = acc + rbufs[t][2 * chip[0] + chip[1]].astype(F32)
            half2(f_refs[t], t, c)[...] = acc
            swap(t, c, sib).start()

    def finish():
        for t in range(n):
            swap(t, 1 - c, me).wait_recv()
        for t in range(n):
            to_sibling(t, sib).wait_send()
            for j, chip in enumerate(chips):
                to_chip(t, j, chip, (*chip, c)).wait_send()
            swap(t, c, sib).wait_send()

    return load, partial, total, finish


def _reduce_scatter_grads(gsh, vec):
    n = len(gsh)
    halves = [a.shape[1] // 2 for a in gsh]

    def body(*refs):
        g_refs, vec_ref, f_refs, vsum_ref = refs[:n], refs[n], refs[n + 1:2 * n + 1], refs[2 * n + 1]
        scratch = refs[2 * n + 2:]
        vrecv, vsend_sems, vrecv_sems = scratch[4 * n + 3:]
        load, partial, total, finish = _reduce_steps(halves, g_refs, f_refs, scratch)
        x, y, c = _place()
        my_dev = 4 * x + 2 * y + c

        def flip(k):
            return x ^ ((k >> 2) & 1), y ^ ((k >> 1) & 1), c ^ (k & 1)

        def vcopy(k, slot, to):
            return pltpu.make_async_remote_copy(src_ref=vec_ref, dst_ref=vrecv.at[slot], send_sem=vsend_sems.at[k - 1],
                                                recv_sem=vrecv_sems.at[k - 1], device_id=to, device_id_type=MESH)

        load()
        vrecv[my_dev] = vec_ref[...]
        for k in range(1, 8):
            vcopy(k, my_dev, flip(k)).start()
        partial()
        total()
        finish()
        for k in range(1, 8):
            fx, fy, fc = flip(k)
            vcopy(k, 4 * fx + 2 * fy + fc, (x, y, c)).wait_recv()
        vs = vrecv[0]
        for d in range(1, 8):
            vs = vs + vrecv[d]
        vsum_ref[...] = vs
        for k in range(1, 8):
            vcopy(k, my_dev, flip(k)).wait_send()

    return pl.pallas_call(
        body, name="reduce_scatter_grads",
        out_shape=tuple(jax.ShapeDtypeStruct(a.shape[1:], F32) for a in gsh) + (jax.ShapeDtypeStruct((VEC_ROWS, 1024), F32),),
        in_specs=[pl.BlockSpec(memory_space=pl.ANY)] * n + [pl.BlockSpec(memory_space=pltpu.VMEM)],
        out_specs=(pl.BlockSpec(memory_space=pltpu.VMEM),) * (n + 1),
        scratch_shapes=_reduce_scratch(gsh) + [pltpu.VMEM((8, VEC_ROWS, 1024), F32), pltpu.SemaphoreType.DMA((7,)),
                                               pltpu.SemaphoreType.DMA((7,))],
        compiler_params=pltpu.CompilerParams(vmem_limit_bytes=56 * 1024 * 1024),
    )(*gsh, vec)


def _adamw(w, g, m, v):
    rows, cols = w.shape
    tr = rows if rows <= 256 else 256
    flip = cols % LANES != 0

    def body(w_ref, g_ref, m_ref, v_ref, g_out, d_ref, nm_ref, nv_ref):
        gv = g_ref[...].T if flip else g_ref[...]
        outs = (gv,) + _adam_math(w_ref[...], gv, m_ref[...], v_ref[...])
        for ref, val in zip((g_out, d_ref, nm_ref, nv_ref), outs):
            ref[...] = val

    spec = pl.BlockSpec((tr, cols), lambda i: (i, 0))
    tspec = pl.BlockSpec((cols, tr), lambda i: (0, i)) if flip else spec
    shp = jax.ShapeDtypeStruct((cols, rows) if flip else (rows, cols), F32)
    if flip:
        w, m, v = w.T, m.T, v.T
    outs = pl.pallas_call(body, name="adamw", grid=(rows // tr,), out_shape=(shp,) * 4,
                          in_specs=[tspec, spec, tspec, tspec], out_specs=(tspec,) * 4)(w, g, m, v)
    return tuple(o.T for o in outs) if flip else outs


def _adam_math(w, g, m, v):
    m2 = ADAM_B1 * m + (1.0 - ADAM_B1) * g
    v2 = ADAM_B2 * v + (1.0 - ADAM_B2) * (g * g)
    m_hat = m2 / (1.0 - ADAM_B1 ** ADAM_STEP)
    v_hat = v2 / (1.0 - ADAM_B2 ** ADAM_STEP)
    return -ADAM_LR * (m_hat / (jnp.sqrt(v_hat) + ADAM_EPS) + ADAM_WD * w), m2, v2


def _adamw_small(vsum, w, m, v):
    names = [name for name, _, _, _ in _VEC_LAYOUT]
    k = len(names)

    def body(*refs):
        vs_ref, w_refs, m_refs, v_refs = refs[0], refs[1:1 + k], refs[1 + k:1 + 2 * k], refs[1 + 2 * k:1 + 3 * k]
        outs = refs[1 + 3 * k:]
        for idx, (_, r, c0, width) in enumerate(_VEC_LAYOUT):
            gv = vs_ref[pl.ds(r, 1), pl.ds(c0, width)]
            d, m2, v2 = _adam_math(w_refs[idx][...], gv, m_refs[idx][...], v_refs[idx][...])
            outs[idx][...], outs[k + idx][...], outs[2 * k + idx][...], outs[3 * k + idx][...] = gv, d, m2, v2

    shapes = tuple(jax.ShapeDtypeStruct(w[name].shape, F32) for name in names)
    res = pl.pallas_call(
        body, name="adamw_small", out_shape=shapes * 4,
        in_specs=[pl.BlockSpec(memory_space=pltpu.VMEM)] * (1 + 3 * k), out_specs=(pl.BlockSpec(memory_space=pltpu.VMEM),) * (4 * k),
    )(vsum, *[w[name] for name in names], *[m[name] for name in names], *[v[name] for name in names])
    return tuple({name: res[part * k + idx] for idx, name in enumerate(names)} for part in range(4))


_EARLY = ("w_in", "w_uq", "w_ukv")
_LATE = ("w_out", "w_ple", "w_ple_gate")
_BIG = _EARLY + _LATE
_KR_LOCAL = 2432 - 3 * (D_IN // N_SHARD)


def _extend_early(parts):
    cols = lambda a: a.transpose(1, 0, 2).reshape(a.shape[1], N_SHARD * a.shape[2])
    g = parts["w_in"]
    zeros = lambda n: jnp.zeros((D_MODEL, n), g.dtype)
    win_ext = jnp.concatenate([g[0], g[1], g[2], g[3][:, :_KR_LOCAL], zeros(64), g[3][:, _KR_LOCAL:_KR_LOCAL + QK_ROPE],
                               zeros(32), g[3][:, _KR_LOCAL + QK_ROPE:]], axis=1)
    wuq_ext = jnp.pad(cols(parts["w_uq"]).reshape(Q_LORA, 8, 96), ((0, 0), (0, 0), (0, 32))).reshape(Q_LORA, 1024)
    wukv = cols(parts["w_ukv"]).reshape(KV_LORA, 8, 128)
    wk_ext = jnp.pad(wukv[:, :, :64], ((0, 0), (0, 0), (0, 64))).reshape(KV_LORA, 1024)
    wv = wukv[:, :, 64:].reshape(KV_LORA, 512)
    return win_ext, wuq_ext, wk_ext, wv


def _shard_cols(a):
    return a.reshape(a.shape[0], N_SHARD, a.shape[1] // N_SHARD).transpose(1, 0, 2)


def _shard_rows(a):
    return a.reshape(N_SHARD, a.shape[0] // N_SHARD, a.shape[1])


def _shard_early_grads(dwin_ext, dwuq_ext, dwk_ext, dwv):
    e, w = dwin_ext, D_IN // N_SHARD
    last = jnp.concatenate([e[:, 3 * w:2432], e[:, 2496:2528], e[:, 2560:]], axis=1)
    dwuq = dwuq_ext.reshape(Q_LORA, 8, 128)[:, :, :96].reshape(Q_LORA, 768)
    dwukv = jnp.concatenate([dwk_ext.reshape(KV_LORA, 8, 128)[:, :, :64], dwv.reshape(KV_LORA, 8, 64)], axis=2)
    return [jnp.stack([e[:, 0:w], e[:, w:2 * w], e[:, 2 * w:3 * w], last]), _shard_cols(dwuq),
            _shard_cols(dwukv.reshape(KV_LORA, 1024))]


def _rope_tables(positions):
    half = QK_ROPE // 2
    freq = ROPE_THETA ** (-jnp.arange(half, dtype=F32) / half)
    s = positions.shape[0]
    per = LANES // half
    ang = jnp.repeat(positions.astype(F32).reshape(s // per, per), half, axis=1) * jnp.tile(freq, per)
    cos, sin = lax.optimization_barrier((jnp.cos(ang), jnp.sin(ang)))
    cos, sin = cos.reshape(s, half), sin.reshape(s, half)
    z = lambda n: jnp.zeros((s, n), F32)
    c_t = jnp.concatenate([jnp.ones((s, 64), F32), cos, cos, z(32)], axis=1)
    sa_t = jnp.concatenate([z(64), -sin, z(16), z(32)], axis=1)
    sb_t = jnp.concatenate([z(64), z(16), sin, z(32)], axis=1)
    return c_t, sa_t, sb_t


def _local_grads(x, p, positions, tgt, gains, early, late):
    win_ext, wuq_ext, wk_ext, wv = _extend_early(early)
    tabs = _rope_tables(positions)
    g = gains
    sbq, sbk, sbv, sbg, mlag, cq, ckv, qc, kc, mv, sbkt, sbvt, kct, mvt = _pre_fwd(
        x, tabs, g["norm_pre_g"], win_ext, g["q_norm_g"], wuq_ext, g["kv_norm_g"], wk_ext, wv)
    sbo, wout4, wple4, wpg4 = _sb_fwd(sbq, sbk, sbvt, late)
    wout, wpg = wout4.reshape(D_MODEL, D_MODEL), wpg4.reshape(D_MODEL, D_MODEL)
    wple = wple4.transpose(1, 0, 2).reshape(PLE_DIM, D_MODEL)
    mlao, lse = _mla_fwd(qc, kc, mvt)
    dsbo, dmlao, delta, dsbg, dmlag, dxres, dwout, dwpg, dwple, vec_c = _post(
        x, p, tgt, sbo, mlao, sbg, mlag, g["sb_out_norm_g"], g["mla_out_norm_g"], wout, g["norm_post_g"], wple,
        g["ple_norm_g"], wpg, g["b_ple_gate"])
    dsbq, dsbk, dsbv, *late_grads = _sb_bwd(sbq, sbk, sbkt, sbv, dsbo, [_shard_rows(dwout), _shard_cols(dwple), _shard_rows(dwpg)])
    dqc, dkc, dmv = _mla_bwd(qc, kc, kct, mv, dmlao, lse, delta)
    gx, dwin_ext, dwuq_ext, dwk_ext, dwv, vec_d = _pre_bwd(
        x, dxres, dsbq, dsbk, dsbv, dsbg, dmlag, dqc, dkc, dmv, cq, ckv, tabs, g["norm_pre_g"], win_ext, g["q_norm_g"],
        wuq_ext, g["kv_norm_g"], wk_ext, wv)
    return gx, _shard_early_grads(dwin_ext, dwuq_ext, dwk_ext, dwv), late_grads, jnp.concatenate([vec_c, vec_d], axis=0)


_VEC_LAYOUT = (("norm_post_g", 0, 0, 1024), ("ple_norm_g", 1, 0, 1024), ("b_ple_gate", 2, 0, 1024), ("sb_out_norm_g", 3, 0, 512),
               ("mla_out_norm_g", 3, 512, 512), ("norm_pre_g", 8, 0, 1024), ("q_norm_g", 9, 0, 256), ("kv_norm_g", 9, 256, 128))
_LOSS_ROW = 4
_WEIGHT_ORDER = ("norm_pre_g", "w_in", "q_norm_g", "w_uq", "kv_norm_g", "w_ukv", "sb_out_norm_g", "mla_out_norm_g", "w_out",
                 "norm_post_g", "w_ple", "ple_norm_g", "w_ple_gate", "b_ple_gate")


def kernel(x, p, positions, norm_pre_g, w_in, q_norm_g, w_uq, kv_norm_g, w_ukv, sb_out_norm_g, mla_out_norm_g, w_out, norm_post_g, w_ple, ple_norm_g, w_ple_gate, b_ple_gate, loss_target, m_norm_pre_g, m_w_in, m_q_norm_g, m_w_uq, m_kv_norm_g, m_w_ukv, m_sb_out_norm_g, m_mla_out_norm_g, m_w_out, m_norm_post_g, m_w_ple, m_ple_norm_g, m_w_ple_gate, m_b_ple_gate, v_norm_pre_g, v_w_in, v_q_norm_g, v_w_uq, v_kv_norm_g, v_w_ukv, v_sb_out_norm_g, v_mla_out_norm_g, v_w_out, v_norm_post_g, v_w_ple, v_ple_norm_g, v_w_ple_gate, v_b_ple_gate):
    w = {"norm_pre_g": norm_pre_g, "w_in": w_in[0], "q_norm_g": q_norm_g, "w_uq": w_uq[0], "kv_norm_g": kv_norm_g, "w_ukv": w_ukv[0],
         "sb_out_norm_g": sb_out_norm_g, "mla_out_norm_g": mla_out_norm_g, "w_out": w_out[0], "norm_post_g": norm_post_g,
         "w_ple": w_ple[0], "ple_norm_g": ple_norm_g, "w_ple_gate": w_ple_gate[0], "b_ple_gate": b_ple_gate}
    m = {"norm_pre_g": m_norm_pre_g, "w_in": m_w_in[0], "q_norm_g": m_q_norm_g, "w_uq": m_w_uq[0], "kv_norm_g": m_kv_norm_g,
         "w_ukv": m_w_ukv[0], "sb_out_norm_g": m_sb_out_norm_g, "mla_out_norm_g": m_mla_out_norm_g, "w_out": m_w_out[0],
         "norm_post_g": m_norm_post_g, "w_ple": m_w_ple[0], "ple_norm_g": m_ple_norm_g, "w_ple_gate": m_w_ple_gate[0],
         "b_ple_gate": m_b_ple_gate}
    v = {"norm_pre_g": v_norm_pre_g, "w_in": v_w_in[0], "q_norm_g": v_q_norm_g, "w_uq": v_w_uq[0], "kv_norm_g": v_kv_norm_g,
         "w_ukv": v_w_ukv[0], "sb_out_norm_g": v_sb_out_norm_g, "mla_out_norm_g": v_mla_out_norm_g, "w_out": v_w_out[0],
         "norm_post_g": v_norm_post_g, "w_ple": v_w_ple[0], "ple_norm_g": v_ple_norm_g, "w_ple_gate": v_w_ple_gate[0],
         "b_ple_gate": v_b_ple_gate}
    gathered = _allgather_weights([w[n] for n in _EARLY])
    gx, early_grads, late_red, vec = _local_grads(x[0], p[0, 0], positions[0], loss_target[0], w, dict(zip(_EARLY, gathered)),
                                                  [w[n] for n in _LATE])
    *early_red, vsum = _reduce_scatter_grads(early_grads, vec)
    gred = early_red + late_red
    loss = vsum[_LOSS_ROW, 0]

    g, delta, new_m, new_v = _adamw_small(vsum, w, m, v)
    for n, gn in zip(_BIG, gred):
        g[n], delta[n], new_m[n], new_v[n] = _adamw(w[n], gn, m[n], v[n])

    lead = lambda n, a: a[None] if n in _BIG else a
    return (loss, gx[None],
            *[lead(n, g[n]) for n in _WEIGHT_ORDER], *[lead(n, delta[n]) for n in _WEIGHT_ORDER],
            *[lead(n, new_m[n]) for n in _WEIGHT_ORDER], *[lead(n, new_v[n]) for n in _WEIGHT_ORDER])
```

```python
import numpy as np
import jax
import jax.numpy as jnp
from jax import lax
from jax.experimental import pallas as pl
from jax.experimental.pallas import tpu as pltpu

F32 = jnp.float32
BF16 = jnp.bfloat16
MESH = pl.DeviceIdType.MESH

D_MODEL = 1024
HEAD_DIM = 64
D_SB = 512
D_MLA = 512
Q_LORA = 256
KV_LORA = 128
QK_NOPE = 64
QK_ROPE = 32
PLE_DIM = 256
D_IN = 2976
D_EXT = 3072
ROPE_THETA = 10000.0
EPS = 1e-6
N_SHARD = 4

ADAM_LR = 0.001
ADAM_B1 = 0.9
ADAM_B2 = 0.999
ADAM_EPS = 1e-08
ADAM_WD = 0.01
ADAM_STEP = 10

LANES = 128
BK = 128
WQ = 256
MQ_FWD = 4096
MQ_BWD = 1024
MLA_CW = 256
SB_CUTOFF = 120.0
TM = 256
TM_PRE = 256
VEC_ROWS = 16
VMEM_DENSE = 52 * 1024 * 1024
VMEM_ATTN = 40 * 1024 * 1024


def _mm(a, b):
    return jnp.dot(a, b, preferred_element_type=F32)


def _mm_nt(a, b):
    return lax.dot_general(a, b, (((1,), (1,)), ((), ())), preferred_element_type=F32)


def _mm_tn(a, b):
    return lax.dot_general(a, b, (((0,), (0,)), ((), ())), preferred_element_type=F32)


def _seg(a, bd2):
    return _mm(_split2(a), bd2)


def _const(mask):
    return jnp.asarray(np.asarray(mask, np.float32), dtype=BF16)


def _blockdiag2(n, seg):
    r = (np.arange(2 * n)[:, None] % n) // seg
    c = np.arange(n)[None, :] // seg
    return _const(r == c)


def _sigmoid(a):
    return 1.0 / (1.0 + jnp.exp(-a))


def _rowmean(a):
    return jnp.mean(a, axis=-1, keepdims=True)


def _colsum(a):
    return jnp.sum(a, axis=0, keepdims=True)


def _rope_fwd(a, c, sa, sb):
    w = a.shape[-1]
    return a * c + pltpu.roll(a, w - 16, 1) * sa + pltpu.roll(a, 16, 1) * sb


def _rope_bwd(g, c, sa, sb):
    w = g.shape[-1]
    return g * c + pltpu.roll(g * sa, 16, 1) + pltpu.roll(g * sb, w - 16, 1)


def _full(shape):
    return pl.BlockSpec(shape, lambda *_: (0,) * len(shape))


def _acc(shape):
    return pl.BlockSpec(shape, lambda *_: (0,) * len(shape))


def _full2(shape):
    return pl.BlockSpec(shape, lambda p, i: (0, 0))


def _cols(height, tm=TM):
    return pl.BlockSpec((height, tm), lambda i: (0, i))


def _rows(width, tm=TM):
    return pl.BlockSpec((tm, width), lambda i: (i, 0))


def _pre_fwd(x, tabs, gpre, win, gq, wuq, gkv, wk, wv):
    s = x.shape[0]
    c_t, sa_t, sb_t = tabs
    rw, cl = (lambda width: _rows(width, TM_PRE)), (lambda height: _cols(height, TM_PRE))

    def body(x_ref, c_ref, sa_ref, sb_ref, gpre_ref, win_ref, gq_ref, wuq_ref, gkv_ref, wk_ref, wv_ref,
             sbq_ref, sbk_ref, sbv_ref, sbg_ref, mlag_ref, cq_ref, ckv_ref, qc_ref, kc_ref, mv_ref,
             sbkt_ref, sbvt_ref, kct_ref, mvt_ref):
        xv = x_ref[...]
        r1 = lax.rsqrt(_rowmean(xv * xv) + EPS)
        h = (xv * r1 * gpre_ref[...]).astype(BF16)
        proj = _mm(h, win_ref[...])
        sbq_ref[...] = proj[:, 0:512].astype(BF16)
        sbk_ref[...] = proj[:, 512:1024].astype(BF16)
        sbv_ref[...] = proj[:, 1024:1536].astype(BF16)
        sbkt_ref[...] = proj[:, 512:1024].T.astype(BF16)
        sbvt_ref[...] = proj[:, 1024:1536].T.astype(BF16)
        sbg_ref[...] = proj[:, 1536:2048]
        cq = proj[:, 2048:2304]
        ckv = proj[:, 2304:2432]
        kr = proj[:, 2432:2560]
        mlag_ref[...] = proj[:, 2560:3072]
        cq_ref[...] = cq
        ckv_ref[...] = ckv
        c1, sa1, sb1 = c_ref[...], sa_ref[...], sb_ref[...]
        c8, sa8, sb8 = jnp.tile(c1, (1, 8)), jnp.tile(sa1, (1, 8)), jnp.tile(sb1, (1, 8))
        cqn = (cq * lax.rsqrt(_rowmean(cq * cq) + EPS) * gq_ref[...]).astype(BF16)
        qe = _mm(cqn, wuq_ref[...])
        qc_ref[...] = _rope_fwd(qe, c8, sa8, sb8).astype(BF16)
        ckvn = (ckv * lax.rsqrt(_rowmean(ckv * ckv) + EPS) * gkv_ref[...]).astype(BF16)
        ke = _mm(ckvn, wk_ref[...])
        krr = _rope_fwd(kr, c1, sa1, sb1)
        kcat = ke + jnp.tile(krr, (1, 8))
        kc_ref[...] = kcat.astype(BF16)
        kct_ref[...] = kcat.T.astype(BF16)
        mval = _mm(ckvn, wv_ref[...])
        mv_ref[...] = mval.astype(BF16)
        mvt_ref[...] = mval.T.astype(BF16)

    out_shape = (
        jax.ShapeDtypeStruct((s, 512), BF16), jax.ShapeDtypeStruct((s, 512), BF16), jax.ShapeDtypeStruct((s, 512), BF16),
        jax.ShapeDtypeStruct((s, 512), F32), jax.ShapeDtypeStruct((s, 512), F32),
        jax.ShapeDtypeStruct((s, Q_LORA), F32), jax.ShapeDtypeStruct((s, KV_LORA), F32),
        jax.ShapeDtypeStruct((s, 1024), BF16), jax.ShapeDtypeStruct((s, 1024), BF16), jax.ShapeDtypeStruct((s, 512), BF16),
        jax.ShapeDtypeStruct((512, s), BF16), jax.ShapeDtypeStruct((512, s), BF16), jax.ShapeDtypeStruct((1024, s), BF16),
        jax.ShapeDtypeStruct((512, s), BF16),
    )
    return pl.pallas_call(
        body, name="pre_fwd", grid=(s // TM_PRE,), out_shape=out_shape,
        in_specs=[rw(D_MODEL), rw(LANES), rw(LANES), rw(LANES), _full((1, D_MODEL)), _full((D_MODEL, D_EXT)),
                  _full((1, Q_LORA)), _full((Q_LORA, 1024)), _full((1, KV_LORA)), _full((KV_LORA, 1024)), _full((KV_LORA, 512))],
        out_specs=(rw(512), rw(512), rw(512), rw(512), rw(512), rw(Q_LORA), rw(KV_LORA),
                   rw(1024), rw(1024), rw(512), cl(512), cl(512), cl(1024), cl(512)),
        compiler_params=pltpu.CompilerParams(vmem_limit_bytes=VMEM_DENSE),
    )(x, c_t, sa_t, sb_t, gpre, win, gq, wuq, gkv, wk, wv)


def _softplus(z):
    neg_abs = lax.bitcast_convert_type(lax.bitcast_convert_type(z, jnp.uint32) | jnp.uint32(0x80000000), F32)
    return jnp.maximum(z, 0.0) + jnp.log(1.0 + jnp.exp(neg_abs))


def _sum_matrix(kind, terms):
    r, c = np.arange(2 * BK)[:, None], np.arange(2 * BK * terms)[None, :] % (2 * BK)
    rk, ck = r % BK, c % BK
    return _const(((r // BK) == (c // BK)) & {"suffix": ck >= rk, "prefix": ck <= rk}[kind])


def _split_rows(a):
    hi = a.astype(BF16)
    return jnp.concatenate([hi, (a - hi.astype(F32)).astype(BF16)], axis=0)


def _heads_t(blk, rowi):
    zero = jnp.zeros_like(blk)
    return jnp.concatenate([jnp.where(rowi < 64, blk, zero), jnp.where(rowi >= 64, blk, zero)], axis=1)


def _mask_keys(a, valid, fill=0.0):
    return jnp.concatenate([jnp.where(valid, a[0:BK], fill), jnp.where(valid, a[BK:2 * BK], fill)], axis=0)


def _split2(a):
    hi = a.astype(BF16)
    lo = (a - hi.astype(F32)).astype(BF16)
    return jnp.concatenate([hi, lo], axis=1)


def _pair_stack(b, lane):
    zero = jnp.zeros_like(b)
    return jnp.concatenate([jnp.where(lane < 64, b, zero), jnp.where(lane >= 64, b, zero)], axis=0)


def _sb_fwd(q, k, vt, late):
    s = q.shape[0]
    n = len(late)

    def body(q_ref, k_ref, vt_ref, usuf_ref, *rest):
        ins, o_ref, outs = rest[:n], rest[n], rest[n + 1:2 * n + 1]
        acc_scr, run_scr = rest[2 * n + 1:2 * n + 3]
        bufs, (send_sems, recv_sems, out_sems) = rest[2 * n + 3:3 * n + 3], rest[3 * n + 3:]
        p, i = pl.program_id(0), pl.program_id(1)
        gather_start, gather_forward, gather_finish = _gather_steps([a.shape for a in late], ins, bufs, send_sems, recv_sems)

        @pl.when((p == 0) & (i == 0))
        def _():
            gather_start()

        @pl.when((p == 2) & (i == 0))
        def _():
            gather_forward()

        lane = lax.broadcasted_iota(jnp.int32, (1, LANES), 1)
        rowi = lax.broadcasted_iota(jnp.int32, (LANES, 1), 0)
        keyi = lax.broadcasted_iota(jnp.int32, (BK, WQ), 0)
        qryi = lax.broadcasted_iota(jnp.int32, (BK, WQ), 1) + i * WQ
        qs = q_ref[...] * (HEAD_DIM ** -0.5)

        def group(blocks, masked, seen=None):
            seen = seen or [0] * len(blocks)
            starts = [pl.multiple_of(j * BK, BK) for j in blocks]
            valid = [(keyi[:, lo:] + j * BK) < qryi[:, lo:] if m else None for j, m, lo in zip(blocks, masked, seen)]
            zs = [_mm_nt(_pair_stack(k_ref[pl.ds(ks, BK), :], lane), qs[lo:]) for ks, lo in zip(starts, seen)]
            sps = [_softplus(z) for z in zs]
            sps = [sp if ok is None else _mask_keys(sp, ok) for sp, ok in zip(sps, valid)]
            cums = [_mm(usuf_ref[...], _split_rows(sp)) for sp in sps]
            ws = [jnp.exp(z - c) for z, c in zip(zs, cums)]
            ws = [w if ok is None else _mask_keys(w, ok) for w, ok in zip(ws, valid)]
            pvs = [_mm(_heads_t(vt_ref[:, pl.ds(ks, BK)], rowi), w.astype(BF16)) for ks, w in zip(starts, ws)]
            for pv, c, lo in zip(pvs, cums, seen):
                r0, r1 = run_scr[0:1, lo:], run_scr[1:2, lo:]
                acc_scr[:, lo:] += jnp.where(rowi < 64, jnp.exp(-r0), jnp.exp(-r1)) * pv
                run_scr[0:1, lo:] = r0 + c[0:1]
                run_scr[1:2, lo:] = r1 + c[BK:BK + 1]

        assert WQ == 2 * BK
        acc_scr[...] = jnp.zeros_like(acc_scr)
        run_scr[...] = jnp.zeros_like(run_scr)

        @pl.when(i == 0)
        def _():
            group([1, 0], [True, True], [BK, 0])

        @pl.when(i > 0)
        def _():
            group([2 * i + 1, 2 * i, 2 * i - 1, 2 * i - 2], [True, True, False, False], [BK, 0, 0, 0])

        def unfinished():
            return (jnp.min(run_scr[0:2, :]) < SB_CUTOFF).astype(jnp.int32)

        def step(c):
            group([2 * i - 1 - 2 * c[0], 2 * i - 2 - 2 * c[0]], [False, False])
            return c[0] + 1, unfinished()

        lax.while_loop(lambda c: (c[0] < i) & (c[1] > 0), step, (jnp.int32(1), unfinished()))
        o_ref[...] = acc_scr[...].T

        @pl.when((p == pl.num_programs(0) - 1) & (i == pl.num_programs(1) - 1))
        def _():
            gather_finish()
            copies = [pltpu.make_async_copy(bufs[t], outs[t], out_sems.at[t]) for t in range(n)]
            for cp in copies:
                cp.start()
            for cp in copies:
                cp.wait()

    qspec = pl.BlockSpec((WQ, LANES), lambda p, i: (i, p))
    kspec = pl.BlockSpec((s, LANES), lambda p, i: (0, p))
    tspec = pl.BlockSpec((LANES, s), lambda p, i: (p, 0))
    gathered = [jax.ShapeDtypeStruct((N_SHARD,) + a.shape, BF16) for a in late]
    return pl.pallas_call(
        body, name="sb_fwd", grid=(4, s // WQ),
        out_shape=(jax.ShapeDtypeStruct((s, 512), F32), *gathered),
        in_specs=[qspec, kspec, tspec, _full2((2 * BK, 4 * BK))] + [_full2(a.shape) for a in late],
        out_specs=(qspec,) + (pl.BlockSpec(memory_space=pl.ANY),) * n,
        scratch_shapes=[pltpu.VMEM((LANES, WQ), F32), pltpu.VMEM((8, WQ), F32)] + [pltpu.VMEM(g.shape, BF16) for g in gathered]
                       + [pltpu.SemaphoreType.DMA((6 * n,)), pltpu.SemaphoreType.DMA((6 * n,)), pltpu.SemaphoreType.DMA((n,))],
        compiler_params=pltpu.CompilerParams(vmem_limit_bytes=VMEM_ATTN),
    )(q, k, vt, _sum_matrix("suffix", 2), *late)


def _sb_bwd(q, k, kt, v, do, late):
    s = q.shape[0]
    n = len(late)
    halves = [a.shape[1] // 2 for a in late]

    def body(q_ref, k_ref, kt_ref, v_ref, do_ref, usuf_ref, upre_ref, *rest):
        g_refs, (dq_ref, dk_ref, dv_ref), outs = rest[:n], rest[n:n + 3], rest[n + 3:2 * n + 3]
        later_scr, dqt_scr, st_scr = rest[2 * n + 3:2 * n + 6]
        f_scr, reduce_scr, out_sems = rest[2 * n + 6:3 * n + 6], rest[3 * n + 6:-1], rest[-1]
        p, i = pl.program_id(0), pl.program_id(1)
        reduce_load, reduce_partial, reduce_total, reduce_finish = _reduce_steps(halves, g_refs, f_scr, reduce_scr)

        @pl.when((p == 0) & (i == 0))
        def _():
            reduce_load()

        @pl.when((p == 1) & (i == 0))
        def _():
            reduce_partial()

        @pl.when((p == 3) & (i == 0))
        def _():
            reduce_total()

        @pl.when(i == 0)
        def _():
            dk_ref[...] = jnp.zeros_like(dk_ref)
            dv_ref[...] = jnp.zeros_like(dv_ref)

        lane = lax.broadcasted_iota(jnp.int32, (1, LANES), 1)
        rowi = lax.broadcasted_iota(jnp.int32, (LANES, 1), 0)
        keyi = lax.broadcasted_iota(jnp.int32, (BK, WQ), 0)
        qryi = lax.broadcasted_iota(jnp.int32, (BK, WQ), 1) + i * WQ
        qs = q_ref[...] * (HEAD_DIM ** -0.5)
        dob = do_ref[...]
        dot = dob.astype(F32).T.astype(BF16)

        def scores(j, lo=0):
            return _mm_nt(_pair_stack(k_ref[pl.ds(pl.multiple_of(j * BK, BK), BK), :], lane), qs[lo:])

        def scan(blocks, masked, seen=None):
            seen = seen or [0] * len(blocks)
            sps = [_softplus(scores(j, lo)) for j, lo in zip(blocks, seen)]
            sps = [_mask_keys(sp, (keyi[:, lo:] + j * BK) < qryi[:, lo:]) if m else sp
                   for sp, j, m, lo in zip(sps, blocks, masked, seen)]
            for sp, j, lo in zip(sps, blocks, seen):
                run = st_scr[0:2, :]
                later_scr[j, 0:2, :] = run
                st_scr[0:2, lo:] = run[:, lo:] + jnp.concatenate([jnp.sum(sp[0:BK], axis=0, keepdims=True),
                                                                  jnp.sum(sp[BK:2 * BK], axis=0, keepdims=True)], axis=0)

        def sweep(blocks, masked, seen=None):
            seen = seen or [0] * len(blocks)
            starts = [pl.multiple_of(j * BK, BK) for j in blocks]
            valid = [(keyi[:, lo:] + j * BK) < qryi[:, lo:] if m else None for j, m, lo in zip(blocks, masked, seen)]
            zs = [scores(j, lo) for j, lo in zip(blocks, seen)]
            us = [jnp.exp(lax.bitcast_convert_type(lax.bitcast_convert_type(z, jnp.uint32) | jnp.uint32(0x80000000), F32))
                  for z in zs]
            sps = [jnp.maximum(z, 0.0) + jnp.log(1.0 + u) for z, u in zip(zs, us)]
            sps = [sp if ok is None else _mask_keys(sp, ok) for sp, ok in zip(sps, valid)]
            sigs = [jnp.where(z >= 0.0, 1.0, u) / (1.0 + u) for z, u in zip(zs, us)]
            cums = [_mm(usuf_ref[...], _split_rows(sp)) for sp in sps]
            dws = [_mm(_pair_stack(v_ref[pl.ds(ks, BK), :], lane), dot[:, lo:]) for ks, lo in zip(starts, seen)]
            wfs = []
            for z, c, j, ok, lo in zip(zs, cums, blocks, valid, seen):
                f = jnp.exp(-later_scr[j, 0:2, lo:])
                wide = (BK, WQ - lo)
                wf = jnp.exp(z - c) * jnp.concatenate([jnp.broadcast_to(f[0:1], wide), jnp.broadcast_to(f[1:2], wide)], axis=0)
                wfs.append(wf if ok is None else _mask_keys(wf, ok))
            es = [dw * wf for dw, wf in zip(dws, wfs)]
            pres = [_mm(upre_ref[...], e.astype(BF16)) for e in es]
            dzs = []
            for e, pre, sig, ok, lo in zip(es, pres, sigs, valid, seen):
                e0 = pre[0:BK] + st_scr[0:1, lo:]
                e1 = pre[BK:2 * BK] + st_scr[1:2, lo:]
                st_scr[0:1, lo:] = e0[BK - 1:BK]
                st_scr[1:2, lo:] = e1[BK - 1:BK]
                dz = e - sig * jnp.concatenate([e0, e1], axis=0)
                dzs.append((dz if ok is None else _mask_keys(dz, ok)).astype(BF16))
            whole = [b for b, lo in enumerate(seen) if lo == 0]
            dqt_scr[...] += _mm(jnp.concatenate([_heads_t(kt_ref[:, pl.ds(starts[b], BK)], rowi) for b in whole], axis=1),
                                jnp.concatenate([dzs[b] for b in whole], axis=0))
            for b, lo in enumerate(seen):
                if lo:
                    dqt_scr[:, lo:] += _mm(_heads_t(kt_ref[:, pl.ds(starts[b], BK)], rowi), dzs[b])
            for ks, dz, wf, lo in zip(starts, dzs, wfs, seen):
                rk = _mm(dz, qs[lo:])
                dk_ref[pl.ds(ks, BK), :] += jnp.where(lane < 64, rk[0:BK], rk[BK:2 * BK])
                rv = _mm(wf.astype(BF16), dob[lo:])
                dv_ref[pl.ds(ks, BK), :] += jnp.where(lane < 64, rv[0:BK], rv[BK:2 * BK])

        assert WQ == 2 * BK
        st_scr[...] = jnp.zeros_like(st_scr)

        @pl.when(i == 0)
        def _():
            scan([1, 0], [True, True], [BK, 0])

        @pl.when(i > 0)
        def _():
            scan([2 * i + 1, 2 * i, 2 * i - 1, 2 * i - 2], [True, True, False, False], [BK, 0, 0, 0])

        def unfinished():
            return (jnp.min(st_scr[0:2, :]) < SB_CUTOFF).astype(jnp.int32)

        def step(c):
            scan([2 * i - 1 - 2 * c[0], 2 * i - 2 - 2 * c[0]], [False, False])
            return c[0] + 1, unfinished()

        npairs, _ = lax.while_loop(lambda c: (c[0] < i) & (c[1] > 0), step, (jnp.minimum(i, 1), unfinished()))

        st_scr[...] = jnp.zeros_like(st_scr)
        dqt_scr[...] = jnp.zeros_like(dqt_scr)
        first = 2 * (i - npairs)

        def early(t, carry):
            sweep([first + 2 * t, first + 2 * t + 1], [False, False])
            return carry

        lax.fori_loop(0, npairs - 1, early, 0)

        @pl.when(i == 0)
        def _():
            sweep([0, 1], [True, True], [0, BK])

        @pl.when(i > 0)
        def _():
            sweep([2 * i - 2, 2 * i - 1, 2 * i, 2 * i + 1], [False, False, True, True], [0, 0, 0, BK])

        dq_ref[...] = (dqt_scr[...].T * (HEAD_DIM ** -0.5)).astype(BF16)

        @pl.when((p == pl.num_programs(0) - 1) & (i == pl.num_programs(1) - 1))
        def _():
            reduce_finish()
            copies = [pltpu.make_async_copy(f_scr[t], outs[t], out_sems.at[t]) for t in range(n)]
            for cp in copies:
                cp.start()
            for cp in copies:
                cp.wait()

    qspec = pl.BlockSpec((WQ, LANES), lambda p, i: (i, p))
    kspec = pl.BlockSpec((s, LANES), lambda p, i: (0, p))
    tspec = pl.BlockSpec((LANES, s), lambda p, i: (p, 0))
    anywhere = pl.BlockSpec(memory_space=pl.ANY)
    reduced = [jax.ShapeDtypeStruct(a.shape[1:], F32) for a in late]
    return pl.pallas_call(
        body, name="sb_bwd", grid=(4, s // WQ),
        out_shape=(jax.ShapeDtypeStruct((s, 512), BF16), jax.ShapeDtypeStruct((s, 512), F32),
                   jax.ShapeDtypeStruct((s, 512), F32), *reduced),
        in_specs=[qspec, kspec, tspec, kspec, qspec, _full2((2 * BK, 4 * BK)), _full2((2 * BK, 2 * BK))] + [anywhere] * n,
        out_specs=(qspec, kspec, kspec) + (anywhere,) * n,
        scratch_shapes=[pltpu.VMEM((s // BK, 8, WQ), F32), pltpu.VMEM((LANES, WQ), F32), pltpu.VMEM((8, WQ), F32)]
                       + [pltpu.VMEM(r.shape, F32) for r in reduced] + _reduce_scratch(late) + [pltpu.SemaphoreType.DMA((n,))],
        compiler_params=pltpu.CompilerParams(vmem_limit_bytes=VMEM_ATTN),
    )(q, k, kt, v, do, _sum_matrix("suffix", 2), _sum_matrix("prefix", 1), *late)


MLA_SCALE = (QK_NOPE + QK_ROPE) ** -0.5
LOG2E = 1.4426950408889634


def _mla_keys(kb):
    zero = jnp.zeros((BK, LANES), kb.dtype)
    return jnp.concatenate([jnp.concatenate([kb[:, 0:LANES], zero], axis=1),
                            jnp.concatenate([zero, kb[:, LANES:2 * LANES]], axis=1)], axis=0)


def _mla_fwd(qc, kc, vt):
    s = qc.shape[0]
    mq = min(MQ_FWD, s)
    rows_l = 16

    def body(q_ref, k_ref, vt_ref, o_ref, l_ref, p_scr, ot_scr, st_scr):
        i = pl.program_id(1)
        row = lax.broadcasted_iota(jnp.int32, (LANES, 1), 0)
        orow = lax.broadcasted_iota(jnp.int32, (rows_l, 2 * BK), 0)
        ocol = lax.broadcasted_iota(jnp.int32, (rows_l, 2 * BK), 1)
        ones = jnp.where(((orow == 0) & (ocol < BK)) | ((orow == 1) & (ocol >= BK)), 1.0, 0.0).astype(BF16)

        def chunks(lo, hi):
            return [(a, min(a + MLA_CW, hi)) for a in range(lo, hi, MLA_CW)]

        def keys(j):
            return _mla_keys(k_ref[pl.ds(pl.multiple_of(j * BK, BK), BK), :])

        def values_t(j):
            vtb = vt_ref[:, pl.ds(pl.multiple_of(j * BK, BK), BK)]
            zero = jnp.zeros_like(vtb)
            top = jnp.concatenate([jnp.where(row < 64, vtb, zero), jnp.where(row >= 64, vtb, zero)], axis=1)
            return jnp.concatenate([top, ones], axis=0)

        def pair_values(ja):
            return jnp.concatenate([values_t(ja), values_t(ja + 1)], axis=1)

        def softmax(ja, za, zb, masked, a, b):
            c = MLA_SCALE * LOG2E
            parts = [za[0:BK] * c, za[BK:2 * BK] * c, zb[0:BK] * c, zb[BK:2 * BK] * c]
            if masked:
                keyc = lax.broadcasted_iota(jnp.int32, (BK, b - a), 0)
                qryc = (lax.broadcasted_iota(jnp.int32, (BK, b - a), 1) + (i * mq + a)) // 64
                va = ((keyc + ja * BK) // 64) <= qryc
                vb = ((keyc + (ja + 1) * BK) // 64) <= qryc
                parts = [jnp.where(va, parts[0], -1e30), jnp.where(va, parts[1], -1e30),
                         jnp.where(vb, parts[2], -1e30), jnp.where(vb, parts[3], -1e30)]
            m0, m1 = st_scr[0:1, a:b], st_scr[1:2, a:b]
            n0 = jnp.maximum(m0, jnp.max(jnp.maximum(parts[0], parts[2]), axis=0, keepdims=True))
            n1 = jnp.maximum(m1, jnp.max(jnp.maximum(parts[1], parts[3]), axis=0, keepdims=True))
            st_scr[2:3, a:b] = jnp.exp2(m0 - n0)
            st_scr[3:4, a:b] = jnp.exp2(m1 - n1)
            st_scr[0:1, a:b] = n0
            st_scr[1:2, a:b] = n1
            p_scr[:, a:b] = jnp.concatenate([jnp.exp2(parts[0] - n0), jnp.exp2(parts[1] - n1),
                                             jnp.exp2(parts[2] - n0), jnp.exp2(parts[3] - n1)], axis=0).astype(BF16)

        def accumulate(vals, a, b):
            pv = _mm(vals, p_scr[:, a:b])
            f = jnp.where(row < 64, st_scr[2:3, a:b], st_scr[3:4, a:b])
            ot_scr[0:LANES, a:b] = f * ot_scr[0:LANES, a:b] + pv[0:LANES]
            ot_scr[LANES:LANES + 8, a:b] = st_scr[2:10, a:b] * ot_scr[LANES:LANES + 8, a:b] + pv[LANES:LANES + 8]

        def step(n, diag, lo=0, prev_lo=0):
            ka, kb = keys(2 * n), keys(2 * n + 1)
            vals = pair_values(2 * n - 2)
            for a, b in chunks(prev_lo, lo):
                accumulate(vals, a, b)
            for a, b in chunks(lo, mq):
                qc_ = q_ref[a:b, :]
                za, zb = _mm_nt(ka, qc_), _mm_nt(kb, qc_)
                accumulate(vals, a, b)
                softmax(2 * n, za, zb, diag and a < lo + 2 * BK, a, b)

        def first(diag):
            ka, kb = keys(0), keys(1)
            for a, b in chunks(0, mq):
                qc_ = q_ref[a:b, :]
                softmax(0, _mm_nt(ka, qc_), _mm_nt(kb, qc_), diag and a < 2 * BK, a, b)

        st_scr[...] = jnp.concatenate([jnp.full((2, mq), -1e30, F32), jnp.ones((14, mq), F32)], axis=0)
        ot_scr[...] = jnp.zeros_like(ot_scr)

        npq = mq // (2 * BK)
        seen = lambda d: 2 * BK * max(d, 0)

        @pl.when(i == 0)
        def _():
            first(True)
            for d in range(1, npq):
                step(d, True, seen(d), seen(d - 1))

        if s > mq:
            @pl.when(i > 0)
            def _():
                first(False)
                lax.fori_loop(1, npq * i, lambda n, c: (step(n, False), c)[1], 0)
                for d in range(npq):
                    step(npq * i + d, True, seen(d), seen(d - 1))

        vals = pair_values(2 * (npq * (i + 1) - 1))
        for a, b in chunks(seen(npq - 1), mq):
            accumulate(vals, a, b)
        for a, b in chunks(0, mq):
            l0, l1 = ot_scr[LANES:LANES + 1, a:b], ot_scr[LANES + 1:LANES + 2, a:b]
            o_ref[a:b, :] = (ot_scr[0:LANES, a:b] / jnp.where(row < 64, l0, l1)).T
            l_ref[a:b, :] = jnp.where(row < 64, st_scr[0:1, a:b] + jnp.log2(l0), st_scr[1:2, a:b] + jnp.log2(l1)).T

    qspec = pl.BlockSpec((mq, 2 * LANES), lambda p, i: (i, p))
    kspec = pl.BlockSpec((s, 2 * LANES), lambda p, i: (0, p))
    vtspec = pl.BlockSpec((LANES, s), lambda p, i: (p, 0))
    ospec = pl.BlockSpec((mq, LANES), lambda p, i: (i, p))
    return pl.pallas_call(
        body, name="mla_fwd", grid=(4, s // mq),
        out_shape=(jax.ShapeDtypeStruct((s, 512), F32), jax.ShapeDtypeStruct((s, 512), F32)),
        in_specs=[qspec, kspec, vtspec], out_specs=(ospec, ospec),
        scratch_shapes=[pltpu.VMEM((4 * BK, mq), BF16), pltpu.VMEM((LANES + 8, mq), F32), pltpu.VMEM((16, mq), F32)],
        compiler_params=pltpu.CompilerParams(vmem_limit_bytes=VMEM_ATTN),
    )(qc, kc, vt)


def _mla_bwd(qc, kc, kct, v, do, lse, delta):
    s = qc.shape[0]
    mq = min(MQ_BWD, s)

    def body(q_ref, k_ref, kt_ref, v_ref, do_ref, l_ref, d_ref, dq_ref, dk_ref, dv_ref, dqt_scr, p_scr, dz_scr):
        i = pl.program_id(1)

        @pl.when(i == 0)
        def _():
            dk_ref[...] = jnp.zeros_like(dk_ref)
            dv_ref[...] = jnp.zeros_like(dv_ref)

        lane = lax.broadcasted_iota(jnp.int32, (1, LANES), 1)
        keyc = lax.broadcasted_iota(jnp.int32, (BK, mq), 0)
        qryc = (lax.broadcasted_iota(jnp.int32, (BK, mq), 1) + i * mq) // 64
        qw = q_ref[...]
        dob = do_ref[...]
        dost = (dob.astype(F32) * MLA_SCALE).T.astype(BF16)
        lt = l_ref[...].T
        dt = (d_ref[...] * MLA_SCALE).T
        lse0, lse1 = lt[0:1], lt[64:65]
        dl0, dl1 = dt[0:1], dt[64:65]
        dqt_scr[...] = jnp.zeros_like(dqt_scr)

        def products(j, lo=0):
            ks = pl.multiple_of(j * BK, BK)
            return (_mm_nt(_mla_keys(k_ref[pl.ds(ks, BK), :]), qw[lo:]),
                    _mm(_pair_stack(v_ref[pl.ds(ks, BK), :], lane), dost[:, lo:]))

        def grads(j, slot, zt, dwt, masked, lo=0):
            zt = zt * (MLA_SCALE * LOG2E)
            p0 = jnp.exp2(zt[0:BK] - lse0[:, lo:])
            p1 = jnp.exp2(zt[BK:2 * BK] - lse1[:, lo:])
            if masked:
                valid = ((keyc[:, lo:] + j * BK) // 64) <= qryc[:, lo:]
                p0, p1 = jnp.where(valid, p0, 0.0), jnp.where(valid, p1, 0.0)
            p_scr[slot, :, lo:] = jnp.concatenate([p0, p1], axis=0).astype(BF16)
            dz_scr[slot, :, lo:] = jnp.concatenate([p0 * (dwt[0:BK] - dl0[:, lo:]), p1 * (dwt[BK:2 * BK] - dl1[:, lo:])],
                                                   axis=0).astype(BF16)

        def keys_t(ks):
            ktb = kt_ref[:, pl.ds(ks, BK)]
            zero = jnp.zeros((LANES, BK), ktb.dtype)
            return jnp.concatenate([jnp.concatenate([ktb[0:LANES], zero], axis=1),
                                    jnp.concatenate([zero, ktb[LANES:2 * LANES]], axis=1)], axis=0)

        def scatter(ja, lo=0):
            ksa, ksb = pl.multiple_of(ja * BK, BK), pl.multiple_of((ja + 1) * BK, BK)
            dqt_scr[:, lo:] += _mm(jnp.concatenate([keys_t(ksa), keys_t(ksb)], axis=1),
                                   jnp.concatenate([dz_scr[0, :, lo:], dz_scr[1, :, lo:]], axis=0))
            for slot, ks in ((0, ksa), (1, ksb)):
                rk = _mm(dz_scr[slot, :, lo:], qw[lo:])
                dk_ref[pl.ds(ks, BK), :] += jnp.concatenate([rk[0:BK, 0:LANES], rk[BK:2 * BK, LANES:2 * LANES]], axis=1)
                rv = _mm(p_scr[slot, :, lo:], dob[lo:])
                dv_ref[pl.ds(ks, BK), :] += jnp.where(lane < 64, rv[0:BK], rv[BK:2 * BK])

        def step(n, masked, lo=0, prev_lo=0):
            za, wa = products(2 * n, lo)
            zb, wb = products(2 * n + 1, lo)
            scatter(2 * n - 2, prev_lo)
            grads(2 * n, 0, za, wa, masked, lo)
            grads(2 * n + 1, 1, zb, wb, masked, lo)

        def first(masked):
            za, wa = products(0)
            zb, wb = products(1)
            grads(0, 0, za, wa, masked)
            grads(1, 1, zb, wb, masked)

        npq = mq // (2 * BK)
        seen = lambda d: 2 * BK * max(d, 0)

        @pl.when(i == 0)
        def _():
            first(True)
            for d in range(1, npq):
                step(d, True, seen(d), seen(d - 1))

        @pl.when(i > 0)
        def _():
            first(False)
            lax.fori_loop(1, npq * i, lambda n, c: (step(n, False), c)[1], 0)
            for d in range(npq):
                step(npq * i + d, True, seen(d), seen(d - 1))

        scatter(2 * (npq * (i + 1) - 1), seen(npq - 1))
        dq_ref[...] = dqt_scr[...].T

    qspec = pl.BlockSpec((mq, 2 * LANES), lambda p, i: (i, p))
    kspec = pl.BlockSpec((s, 2 * LANES), lambda p, i: (0, p))
    ktspec = pl.BlockSpec((2 * LANES, s), lambda p, i: (p, 0))
    vspec = pl.BlockSpec((s, LANES), lambda p, i: (0, p))
    ospec = pl.BlockSpec((mq, LANES), lambda p, i: (i, p))
    return pl.pallas_call(
        body, name="mla_bwd", grid=(4, s // mq),
        out_shape=(jax.ShapeDtypeStruct((s, 1024), F32), jax.ShapeDtypeStruct((s, 1024), F32),
                   jax.ShapeDtypeStruct((s, 512), F32)),
        in_specs=[qspec, kspec, ktspec, vspec, ospec, ospec, ospec], out_specs=(qspec, kspec, vspec),
        scratch_shapes=[pltpu.VMEM((2 * LANES, mq), F32), pltpu.VMEM((2, 2 * BK, mq), BF16), pltpu.VMEM((2, 2 * BK, mq), BF16)],
        compiler_params=pltpu.CompilerParams(vmem_limit_bytes=VMEM_ATTN),
    )(qc, kc, kct, v, do, lse, delta)


def _post(x, p, tgt, sbo, mlao, sbg, mlag, gsb, gmla, wout, gpost, wple, gple, wpg, bpg):
    s = x.shape[0]

    def body(x_ref, p_ref, t_ref, sbo_ref, mlao_ref, sbg_ref, mlag_ref, gsb_ref, gmla_ref, wout_ref,
             gpost_ref, wple_ref, gple_ref, wpg_ref, bpg_ref, bd_ref,
             dsbo_ref, dmlao_ref, delta_ref, dsbg_ref, dmlag_ref, dxres_ref, dwout_ref, dwpg_ref, dwple_ref, vec_ref):
        i = pl.program_id(0)

        @pl.when(i == 0)
        def _():
            dwout_ref[...] = jnp.zeros_like(dwout_ref)
            dwpg_ref[...] = jnp.zeros_like(dwpg_ref)
            dwple_ref[...] = jnp.zeros_like(dwple_ref)
            vec_ref[...] = jnp.zeros_like(vec_ref)

        inv_hd = 1.0 / HEAD_DIM

        def head_fwd(o, g, gate):
            r = lax.rsqrt(_seg(o * o, bd_ref[...]) * inv_hd + EPS)
            hat = o * r
            n = hat * g
            sg = _sigmoid(gate)
            return hat, r, n, sg, n * (gate * sg)

        sbo, mlao, sbg_v, mlag_v = sbo_ref[...], mlao_ref[...], sbg_ref[...], mlag_ref[...]
        gsb_v, gmla_v = gsb_ref[...], gmla_ref[...]
        sb_hat, sb_r, sb_n, sb_sg, sb_y = head_fwd(sbo, gsb_v, sbg_v)
        ml_hat, ml_r, ml_n, ml_sg, ml_y = head_fwd(mlao, gmla_v, mlag_v)
        mix = jnp.concatenate([sb_y, ml_y], axis=1).astype(BF16)
        y = _mm(mix, wout_ref[...])
        ry = lax.rsqrt(_rowmean(y * y) + EPS)
        y_hat = y * ry
        gpost_v = gpost_ref[...]
        x1 = x_ref[...] + y_hat * gpost_v
        pb = p_ref[...].astype(BF16)
        pl_ = _mm(pb, wple_ref[...])
        rp = lax.rsqrt(_rowmean(pl_ * pl_) + EPS)
        pl_hat = pl_ * rp
        gple_v = gple_ref[...]
        ple = pl_hat * gple_v
        x1b = x1.astype(BF16)
        gate = _sigmoid(_mm(x1b, wpg_ref[...]) + bpg_ref[...])
        err = x1 + ple * gate - t_ref[...]
        loss = 0.5 * jnp.sum(_rowmean(err * err))
        dout = err * (1.0 / D_MODEL)

        du = dout * ple * gate * (1.0 - gate)
        dub = du.astype(BF16)
        dple = dout * gate
        dx1 = dout + _mm_nt(dub, wpg_ref[...])
        dwpg_ref[...] += _mm_tn(x1b, dub)
        dplh = dple * gple_v
        dpl = rp * (dplh - pl_hat * _rowmean(dplh * pl_hat))
        dwple_ref[...] += _mm_tn(pb, dpl.astype(BF16))
        dxres_ref[...] = dx1
        dyh = dx1 * gpost_v
        dy = ry * (dyh - y_hat * _rowmean(dyh * y_hat))
        dyb = dy.astype(BF16)
        dwout_ref[...] += _mm_tn(mix, dyb)
        dmix = _mm_nt(dyb, wout_ref[...])

        def head_bwd(dyv, hat, r, n, sg, g, gate):
            dn = dyv * (gate * sg)
            dgate = dyv * n * (sg * (1.0 + gate * (1.0 - sg)))
            dhat = dn * g
            do = r * (dhat - hat * (_seg(dhat * hat, bd_ref[...]) * inv_hd))
            return do, dgate, _colsum(dn * hat)

        dsbo, dsbg, dg_sb = head_bwd(dmix[:, 0:512], sb_hat, sb_r, sb_n, sb_sg, gsb_v, sbg_v)
        dmlao, dmlag, dg_ml = head_bwd(dmix[:, 512:1024], ml_hat, ml_r, ml_n, ml_sg, gmla_v, mlag_v)
        dsbo_ref[...] = dsbo.astype(BF16)
        dmlao_ref[...] = dmlao.astype(BF16)
        delta_ref[...] = _seg(dmlao * mlao, bd_ref[...])
        dsbg_ref[...] = dsbg.astype(BF16)
        dmlag_ref[...] = dmlag.astype(BF16)
        vec_ref[pl.ds(0, 1), :] += _colsum(dx1 * y_hat)
        vec_ref[pl.ds(1, 1), :] += _colsum(dple * pl_hat)
        vec_ref[pl.ds(2, 1), :] += _colsum(du)
        vec_ref[pl.ds(3, 1), :] += jnp.concatenate([dg_sb, dg_ml], axis=1)
        vec_ref[pl.ds(4, 1), :] += jnp.full((1, D_MODEL), loss, F32)

    out_shape = (
        jax.ShapeDtypeStruct((s, 512), BF16), jax.ShapeDtypeStruct((s, 512), BF16), jax.ShapeDtypeStruct((s, 512), F32),
        jax.ShapeDtypeStruct((s, 512), BF16), jax.ShapeDtypeStruct((s, 512), BF16), jax.ShapeDtypeStruct((s, D_MODEL), F32),
        jax.ShapeDtypeStruct((D_MODEL, D_MODEL), F32), jax.ShapeDtypeStruct((D_MODEL, D_MODEL), F32),
        jax.ShapeDtypeStruct((PLE_DIM, D_MODEL), F32), jax.ShapeDtypeStruct((8, D_MODEL), F32),
    )
    return pl.pallas_call(
        body, name="post_fwd_bwd", grid=(s // TM,), out_shape=out_shape,
        in_specs=[_rows(D_MODEL), _rows(PLE_DIM), _rows(D_MODEL), _rows(512), _rows(512), _rows(512), _rows(512),
                  _full((1, 512)), _full((1, 512)), _full((D_MODEL, D_MODEL)),
                  _full((1, D_MODEL)), _full((PLE_DIM, D_MODEL)), _full((1, D_MODEL)), _full((D_MODEL, D_MODEL)),
                  _full((1, D_MODEL)), _full((1024, 512))],
        out_specs=(_rows(512), _rows(512), _rows(512), _rows(512), _rows(512), _rows(D_MODEL),
                   _acc((D_MODEL, D_MODEL)), _acc((D_MODEL, D_MODEL)), _acc((PLE_DIM, D_MODEL)), _acc((8, D_MODEL))),
        compiler_params=pltpu.CompilerParams(vmem_limit_bytes=VMEM_DENSE),
    )(x, p, tgt, sbo, mlao, sbg, mlag, gsb, gmla, wout, gpost, wple, gple, wpg, bpg, _blockdiag2(512, HEAD_DIM))


def _pre_bwd(x, dxres, dsbq, dsbk, dsbv, dsbg, dmlag, dqc, dkc, dmv, cq, ckv, tabs, gpre, win, gq, wuq, gkv, wk, wv):
    s = x.shape[0]
    c_t, sa_t, sb_t = tabs
    rw = _rows

    def body(x_ref, dxres_ref, dsbq_ref, dsbk_ref, dsbv_ref, dsbg_ref, dmlag_ref, dqc_ref, dkc_ref, dmv_ref, cq_ref,
             ckv_ref, c_ref, sa_ref, sb_ref, gpre_ref, win_ref, gq_ref, wuq_ref, gkv_ref, wk_ref, wv_ref,
             gx_ref, dwin_ref, dwuq_ref, dwk_ref, dwv_ref, vec_ref, dwin_acc):
        i = pl.program_id(0)

        @pl.when(i == 0)
        def _():
            dwin_acc[...] = jnp.zeros_like(dwin_acc)
            dwuq_ref[...] = jnp.zeros_like(dwuq_ref)
            dwk_ref[...] = jnp.zeros_like(dwk_ref)
            dwv_ref[...] = jnp.zeros_like(dwv_ref)
            vec_ref[...] = jnp.zeros_like(vec_ref)

        lane = lax.broadcasted_iota(jnp.int32, (1, LANES), 1)
        c1, sa1, sb1 = c_ref[...], sa_ref[...], sb_ref[...]
        c8, sa8, sb8 = jnp.tile(c1, (1, 8)), jnp.tile(sa1, (1, 8)), jnp.tile(sb1, (1, 8))

        def norm_bwd(dn, hat, r, g):
            t = dn * g
            return r * (t - hat * _rowmean(t * hat)), _colsum(dn * hat)

        xv = x_ref[...]
        r1 = lax.rsqrt(_rowmean(xv * xv) + EPS)
        x_hat = xv * r1
        gpre_v = gpre_ref[...]
        hb = (x_hat * gpre_v).astype(BF16)
        ready = jnp.concatenate([dsbq_ref[...], dsbk_ref[...].astype(BF16), dsbv_ref[...].astype(BF16), dsbg_ref[...]], axis=1)
        dmlag = dmlag_ref[...]
        dwin_acc[:, 0:2048] += _mm_tn(hb, ready)
        dwin_acc[:, 2560:3072] += _mm_tn(hb, dmlag)
        dh = _mm_nt(ready, win_ref[:, 0:2048]) + _mm_nt(dmlag, win_ref[:, 2560:3072])

        dqeb = _rope_bwd(dqc_ref[...], c8, sa8, sb8).astype(BF16)
        cq = cq_ref[...]
        rq = lax.rsqrt(_rowmean(cq * cq) + EPS)
        cq_hat = cq * rq
        gq_v = gq_ref[...]
        dwuq_ref[...] += _mm_tn((cq_hat * gq_v).astype(BF16), dqeb)
        dcq, dg_q = norm_bwd(_mm_nt(dqeb, wuq_ref[...]), cq_hat, rq, gq_v)

        dkc = dkc_ref[...]
        dkcb = dkc.astype(BF16)
        dmvb = dmv_ref[...].astype(BF16)
        ckv = ckv_ref[...]
        rkv = lax.rsqrt(_rowmean(ckv * ckv) + EPS)
        ckv_hat = ckv * rkv
        gkv_v = gkv_ref[...]
        ckvnb = (ckv_hat * gkv_v).astype(BF16)
        dwk_ref[...] += _mm_tn(ckvnb, dkcb)
        dwv_ref[...] += _mm_tn(ckvnb, dmvb)
        dckv, dg_kv = norm_bwd(_mm_nt(dkcb, wk_ref[...]) + _mm_nt(dmvb, wv_ref[...]), ckv_hat, rkv, gkv_v)

        dkr = dkc[:, 0:LANES]
        for hh in range(1, 8):
            dkr = dkr + dkc[:, LANES * hh:LANES * (hh + 1)]
        dkr = _rope_bwd(dkr, c1, sa1, sb1)
        dkr = jnp.where((lane >= 64) & (lane < 96), dkr, 0.0)

        late = jnp.concatenate([dcq.astype(BF16), dckv.astype(BF16), dkr.astype(BF16)], axis=1)
        dwin_acc[:, 2048:2560] += _mm_tn(hb, late)
        dx, dg_pre = norm_bwd(dh + _mm_nt(late, win_ref[:, 2048:2560]), x_hat, r1, gpre_v)
        gx_ref[...] = dxres_ref[...] + dx
        vec_ref[pl.ds(0, 1), :] += dg_pre
        vec_ref[pl.ds(1, 1), :] += jnp.concatenate([dg_q, dg_kv, jnp.zeros((1, D_MODEL - Q_LORA - KV_LORA), F32)], axis=1)

        @pl.when(i == pl.num_programs(0) - 1)
        def _():
            pltpu.sync_copy(dwin_acc, dwin_ref)

    out_shape = (
        jax.ShapeDtypeStruct((s, D_MODEL), F32), jax.ShapeDtypeStruct((D_MODEL, D_EXT), F32),
        jax.ShapeDtypeStruct((Q_LORA, 1024), F32), jax.ShapeDtypeStruct((KV_LORA, 1024), F32),
        jax.ShapeDtypeStruct((KV_LORA, 512), F32), jax.ShapeDtypeStruct((8, D_MODEL), F32),
    )
    return pl.pallas_call(
        body, name="pre_bwd", grid=(s // TM,), out_shape=out_shape,
        in_specs=[rw(D_MODEL), rw(D_MODEL), rw(512), rw(512), rw(512), rw(512), rw(512),
                  rw(1024), rw(1024), rw(512), rw(Q_LORA), rw(KV_LORA), rw(LANES), rw(LANES),
                  rw(LANES), _full((1, D_MODEL)), _full((D_MODEL, D_EXT)), _full((1, Q_LORA)), _full((Q_LORA, 1024)),
                  _full((1, KV_LORA)), _full((KV_LORA, 1024)), _full((KV_LORA, 512))],
        out_specs=(rw(D_MODEL), pl.BlockSpec(memory_space=pl.ANY), _acc((Q_LORA, 1024)), _acc((KV_LORA, 1024)),
                   _acc((KV_LORA, 512)), _acc((8, D_MODEL))),
        scratch_shapes=[pltpu.VMEM((D_MODEL, D_EXT), F32)],
        compiler_params=pltpu.CompilerParams(vmem_limit_bytes=VMEM_DENSE),
    )(x, dxres, dsbq, dsbk, dsbv, dsbg, dmlag, dqc, dkc, dmv, cq, ckv, c_t, sa_t, sb_t, gpre, win, gq, wuq, gkv, wk, wv)


def _place():
    return lax.axis_index("x"), lax.axis_index("y"), lax.axis_index("c")


def _gather_steps(shapes, ins, bufs, send_sems, recv_sems):
    n = len(shapes)
    x, y, c = _place()
    me, sib = (x, y, c), (x, y, 1 - c)
    chips = [(1 - x, y), (x, 1 - y), (1 - x, 1 - y)]

    def half(t, chip, hc):
        rows = shapes[t][0] // 2
        return bufs[t].at[2 * chip[0] + chip[1], pl.ds(pl.multiple_of(hc * rows, 16), rows), :]

    def copy(k, t, chip, hc, to):
        return pltpu.make_async_remote_copy(src_ref=half(t, chip, hc), dst_ref=half(t, chip, hc), send_sem=send_sems.at[k],
                                            recv_sem=recv_sems.at[k], device_id=to, device_id_type=MESH)

    def start():
        for t in range(n):
            bufs[t][2 * x + y] = ins[t][...].astype(BF16)
            for j, chip in enumerate(chips):
                copy(6 * t + j, t, (x, y), c, (*chip, c)).start()

    def forward():
        for t in range(n):
            for j, chip in enumerate(chips):
                copy(6 * t + j, t, chip, c, me).wait_recv()
                copy(6 * t + 3 + j, t, chip, c, sib).start()

    def finish():
        for t in range(n):
            for j, chip in enumerate(chips):
                copy(6 * t + 3 + j, t, chip, 1 - c, me).wait_recv()
        for t in range(n):
            for j, chip in enumerate(chips):
                copy(6 * t + j, t, (x, y), c, (*chip, c)).wait_send()
                copy(6 * t + 3 + j, t, chip, c, sib).wait_send()

    return start, forward, finish


def _allgather_weights(shards):
    n = len(shards)

    def body(*refs):
        start, forward, finish = _gather_steps([a.shape for a in shards], refs[:n], refs[n:2 * n], refs[2 * n], refs[2 * n + 1])
        start()
        forward()
        finish()

    return pl.pallas_call(
        body, name="allgather_weights",
        out_shape=tuple(jax.ShapeDtypeStruct((N_SHARD,) + a.shape, BF16) for a in shards),
        in_specs=[pl.BlockSpec(memory_space=pltpu.VMEM)] * n, out_specs=(pl.BlockSpec(memory_space=pltpu.VMEM),) * n,
        scratch_shapes=[pltpu.SemaphoreType.DMA((6 * n,)), pltpu.SemaphoreType.DMA((6 * n,))],
        compiler_params=pltpu.CompilerParams(vmem_limit_bytes=VMEM_ATTN),
    )(*shards)


def _reduce_scratch(gsh):
    n = len(gsh)
    half_shapes = [(N_SHARD, a.shape[1] // 2, a.shape[2]) for a in gsh]
    return ([pltpu.VMEM(s_, F32) for s_ in half_shapes] * 2 + [pltpu.VMEM(s_, BF16) for s_ in half_shapes] * 2
            + [pltpu.SemaphoreType.DMA((n,)), pltpu.SemaphoreType.DMA((5 * n,)), pltpu.SemaphoreType.DMA((5 * n,))])


def _reduce_steps(halves, g_refs, f_refs, scratch):
    n = len(halves)
    accs, sibs, sbufs, rbufs = scratch[0:n], scratch[n:2 * n], scratch[2 * n:3 * n], scratch[3 * n:4 * n]
    local_sems, send_sems, recv_sems = scratch[4 * n:4 * n + 3]
    x, y, c = _place()
    me, sib = (x, y, c), (x, y, 1 - c)
    mine = 2 * x + y
    chips = [(1 - x, y), (x, 1 - y), (1 - x, 1 - y)]

    def remote(k, src, dst, to):
        return pltpu.make_async_remote_copy(src_ref=src, dst_ref=dst, send_sem=send_sems.at[k], recv_sem=recv_sems.at[k],
                                            device_id=to, device_id_type=MESH)

    def half3(ref, t, hc):
        return ref.at[:, pl.ds(pl.multiple_of(hc * halves[t], 8), halves[t]), :]

    def half2(ref, t, hc):
        return ref.at[pl.ds(pl.multiple_of(hc * halves[t], 8), halves[t]), :]

    def mine_load(t):
        return pltpu.make_async_copy(half3(g_refs[t], t, c), accs[t], local_sems.at[t])

    def to_sibling(t, to):
        return remote(t, half3(g_refs[t], t, 1 - c), sibs[t], to)

    def to_chip(t, j, chip, to):
        idx = 2 * chip[0] + chip[1]
        return remote(n + 3 * t + j, sbufs[t].at[idx], rbufs[t].at[mine if to is not me else idx], to)

    def swap(t, hc, to):
        return remote(4 * n + t, half2(f_refs[t], t, hc), half2(f_refs[t], t, hc), to)

    def load():
        for t in range(n):
            mine_load(t).start()
            to_sibling(t, sib).start()

    def partial():
        for t in range(n):
            mine_load(t).wait()
            to_sibling(t, me).wait_recv()
            for k in range(N_SHARD):
                accs[t][k] = accs[t][k] + sibs[t][k]
            for j, chip in enumerate(chips):
                idx = 2 * chip[0] + chip[1]
                sbufs[t][idx] = accs[t][idx].astype(BF16)
                to_chip(t, j, chip, (*chip, c)).start()

    def total():
        for t in range(n):
            acc = accs[t][mine]
            for j, chip in enumerate(chips):
                to_chip(t, j, chip, me).wait_recv()
                acc = acc + rbufs[t][2 * chip[0] + chip[1]].astype(F32)
            half2(f_refs[t], t, c)[...] = acc
            swap(t, c, sib).start()

    def finish():
        for t in range(n):
            swap(t, 1 - c, me).wait_recv()
        for t in range(n):
            to_sibling(t, sib).wait_send()
            for j, chip in enumerate(chips):
                to_chip(t, j, chip, (*chip, c)).wait_send()
            swap(t, c, sib).wait_send()

    return load, partial, total, finish


def _reduce_scatter_grads(gsh, vec):
    n = len(gsh)
    halves = [a.shape[1] // 2 for a in gsh]

    def body(*refs):
        g_refs, vec_ref, f_refs, vsum_ref = refs[:n], refs[n], refs[n + 1:2 * n + 1], refs[2 * n + 1]
        scratch = refs[2 * n + 2:]
        vrecv, vsend_sems, vrecv_sems = scratch[4 * n + 3:]
        load, partial, total, finish = _reduce_steps(halves, g_refs, f_refs, scratch)
        x, y, c = _place()
        my_dev = 4 * x + 2 * y + c

        def flip(k):
            return x ^ ((k >> 2) & 1), y ^ ((k >> 1) & 1), c ^ (k & 1)

        def vcopy(k, slot, to):
            return pltpu.make_async_remote_copy(src_ref=vec_ref, dst_ref=vrecv.at[slot], send_sem=vsend_sems.at[k - 1],
                                                recv_sem=vrecv_sems.at[k - 1], device_id=to, device_id_type=MESH)

        load()
        vrecv[my_dev] = vec_ref[...]
        for k in range(1, 8):
            vcopy(k, my_dev, flip(k)).start()
        partial()
        total()
        finish()
        for k in range(1, 8):
            fx, fy, fc = flip(k)
            vcopy(k, 4 * fx + 2 * fy + fc, (x, y, c)).wait_recv()
        vs = vrecv[0]
        for d in range(1, 8):
            vs = vs + vrecv[d]
        vsum_ref[...] = vs
        for k in range(1, 8):
            vcopy(k, my_dev, flip(k)).wait_send()

    return pl.pallas_call(
        body, name="reduce_scatter_grads",
        out_shape=tuple(jax.ShapeDtypeStruct(a.shape[1:], F32) for a in gsh) + (jax.ShapeDtypeStruct((VEC_ROWS, 1024), F32),),
        in_specs=[pl.BlockSpec(memory_space=pl.ANY)] * n + [pl.BlockSpec(memory_space=pltpu.VMEM)],
        out_specs=(pl.BlockSpec(memory_space=pltpu.VMEM),) * (n + 1),
        scratch_shapes=_reduce_scratch(gsh) + [pltpu.VMEM((8, VEC_ROWS, 1024), F32), pltpu.SemaphoreType.DMA((7,)),
                                               pltpu.SemaphoreType.DMA((7,))],
        compiler_params=pltpu.CompilerParams(vmem_limit_bytes=56 * 1024 * 1024),
    )(*gsh, vec)


def _adamw(w, g, m, v):
    rows, cols = w.shape
    tr = rows if rows <= 256 else 256
    flip = cols % LANES != 0

    def body(w_ref, g_ref, m_ref, v_ref, g_out, d_ref, nm_ref, nv_ref):
        gv = g_ref[...].T if flip else g_ref[...]
        outs = (gv,) + _adam_math(w_ref[...], gv, m_ref[...], v_ref[...])
        for ref, val in zip((g_out, d_ref, nm_ref, nv_ref), outs):
            ref[...] = val

    spec = pl.BlockSpec((tr, cols), lambda i: (i, 0))
    tspec = pl.BlockSpec((cols, tr), lambda i: (0, i)) if flip else spec
    shp = jax.ShapeDtypeStruct((cols, rows) if flip else (rows, cols), F32)
    if flip:
        w, m, v = w.T, m.T, v.T
    outs = pl.pallas_call(body, name="adamw", grid=(rows // tr,), out_shape=(shp,) * 4,
                          in_specs=[tspec, spec, tspec, tspec], out_specs=(tspec,) * 4)(w, g, m, v)
    return tuple(o.T for o in outs) if flip else outs


def _adam_math(w, g, m, v):
    m2 = ADAM_B1 * m + (1.0 - ADAM_B1) * g
    v2 = ADAM_B2 * v + (1.0 - ADAM_B2) * (g * g)
    m_hat = m2 / (1.0 - ADAM_B1 ** ADAM_STEP)
    v_hat = v2 / (1.0 - ADAM_B2 ** ADAM_STEP)
    return -ADAM_LR * (m_hat / (jnp.sqrt(v_hat) + ADAM_EPS) + ADAM_WD * w), m2, v2


def _adamw_small(vsum, w, m, v):
    names = [name for name, _, _, _ in _VEC_LAYOUT]
    k = len(names)

    def body(*refs):
        vs_ref, w_refs, m_refs, v_refs = refs[0], refs[1:1 + k], refs[1 + k:1 + 2 * k], refs[1 + 2 * k:1 + 3 * k]
        outs = refs[1 + 3 * k:]
        for idx, (_, r, c0, width) in enumerate(_VEC_LAYOUT):
            gv = vs_ref[pl.ds(r, 1), pl.ds(c0, width)]
            d, m2, v2 = _adam_math(w_refs[idx][...], gv, m_refs[idx][...], v_refs[idx][...])
            outs[idx][...], outs[k + idx][...], outs[2 * k + idx][...], outs[3 * k + idx][...] = gv, d, m2, v2

    shapes = tuple(jax.ShapeDtypeStruct(w[name].shape, F32) for name in names)
    res = pl.pallas_call(
        body, name="adamw_small", out_shape=shapes * 4,
        in_specs=[pl.BlockSpec(memory_space=pltpu.VMEM)] * (1 + 3 * k), out_specs=(pl.BlockSpec(memory_space=pltpu.VMEM),) * (4 * k),
    )(vsum, *[w[name] for name in names], *[m[name] for name in names], *[v[name] for name in names])
    return tuple({name: res[part * k + idx] for idx, name in enumerate(names)} for part in range(4))


_EARLY = ("w_in", "w_uq", "w_ukv")
_LATE = ("w_out", "w_ple", "w_ple_gate")
_BIG = _EARLY + _LATE
_KR_LOCAL = 2432 - 3 * (D_IN // N_SHARD)


def _extend_early(parts):
    cols = lambda a: a.transpose(1, 0, 2).reshape(a.shape[1], N_SHARD * a.shape[2])
    g = parts["w_in"]
    zeros = lambda n: jnp.zeros((D_MODEL, n), g.dtype)
    win_ext = jnp.concatenate([g[0], g[1], g[2], g[3][:, :_KR_LOCAL], zeros(64), g[3][:, _KR_LOCAL:_KR_LOCAL + QK_ROPE],
                               zeros(32), g[3][:, _KR_LOCAL + QK_ROPE:]], axis=1)
    wuq_ext = jnp.pad(cols(parts["w_uq"]).reshape(Q_LORA, 8, 96), ((0, 0), (0, 0), (0, 32))).reshape(Q_LORA, 1024)
    wukv = cols(parts["w_ukv"]).reshape(KV_LORA, 8, 128)
    wk_ext = jnp.pad(wukv[:, :, :64], ((0, 0), (0, 0), (0, 64))).reshape(KV_LORA, 1024)
    wv = wukv[:, :, 64:].reshape(KV_LORA, 512)
    return win_ext, wuq_ext, wk_ext, wv


def _shard_cols(a):
    return a.reshape(a.shape[0], N_SHARD, a.shape[1] // N_SHARD).transpose(1, 0, 2)


def _shard_rows(a):
    return a.reshape(N_SHARD, a.shape[0] // N_SHARD, a.shape[1])


def _shard_early_grads(dwin_ext, dwuq_ext, dwk_ext, dwv):
    e, w = dwin_ext, D_IN // N_SHARD
    last = jnp.concatenate([e[:, 3 * w:2432], e[:, 2496:2528], e[:, 2560:]], axis=1)
    dwuq = dwuq_ext.reshape(Q_LORA, 8, 128)[:, :, :96].reshape(Q_LORA, 768)
    dwukv = jnp.concatenate([dwk_ext.reshape(KV_LORA, 8, 128)[:, :, :64], dwv.reshape(KV_LORA, 8, 64)], axis=2)
    return [jnp.stack([e[:, 0:w], e[:, w:2 * w], e[:, 2 * w:3 * w], last]), _shard_cols(dwuq),
            _shard_cols(dwukv.reshape(KV_LORA, 1024))]


def _rope_tables(positions):
    half = QK_ROPE // 2
    freq = ROPE_THETA ** (-jnp.arange(half, dtype=F32) / half)
    s = positions.shape[0]
    per = LANES // half
    ang = jnp.repeat(positions.astype(F32).reshape(s // per, per), half, axis=1) * jnp.tile(freq, per)
    cos, sin = lax.optimization_barrier((jnp.cos(ang), jnp.sin(ang)))
    cos, sin = cos.reshape(s, half), sin.reshape(s, half)
    z = lambda n: jnp.zeros((s, n), F32)
    c_t = jnp.concatenate([jnp.ones((s, 64), F32), cos, cos, z(32)], axis=1)
    sa_t = jnp.concatenate([z(64), -sin, z(16), z(32)], axis=1)
    sb_t = jnp.concatenate([z(64), z(16), sin, z(32)], axis=1)
    return c_t, sa_t, sb_t


def _local_grads(x, p, positions, tgt, gains, early, late):
    win_ext, wuq_ext, wk_ext, wv = _extend_early(early)
    tabs = _rope_tables(positions)
    g = gains
    sbq, sbk, sbv, sbg, mlag, cq, ckv, qc, kc, mv, sbkt, sbvt, kct, mvt = _pre_fwd(
        x, tabs, g["norm_pre_g"], win_ext, g["q_norm_g"], wuq_ext, g["kv_norm_g"], wk_ext, wv)
    sbo, wout4, wple4, wpg4 = _sb_fwd(sbq, sbk, sbvt, late)
    wout, wpg = wout4.reshape(D_MODEL, D_MODEL), wpg4.reshape(D_MODEL, D_MODEL)
    wple = wple4.transpose(1, 0, 2).reshape(PLE_DIM, D_MODEL)
    mlao, lse = _mla_fwd(qc, kc, mvt)
    dsbo, dmlao, delta, dsbg, dmlag, dxres, dwout, dwpg, dwple, vec_c = _post(
        x, p, tgt, sbo, mlao, sbg, mlag, g["sb_out_norm_g"], g["mla_out_norm_g"], wout, g["norm_post_g"], wple,
        g["ple_norm_g"], wpg, g["b_ple_gate"])
    dsbq, dsbk, dsbv, *late_grads = _sb_bwd(sbq, sbk, sbkt, sbv, dsbo, [_shard_rows(dwout), _shard_cols(dwple), _shard_rows(dwpg)])
    dqc, dkc, dmv = _mla_bwd(qc, kc, kct, mv, dmlao, lse, delta)
    gx, dwin_ext, dwuq_ext, dwk_ext, dwv, vec_d = _pre_bwd(
        x, dxres, dsbq, dsbk, dsbv, dsbg, dmlag, dqc, dkc, dmv, cq, ckv, tabs, g["norm_pre_g"], win_ext, g["q_norm_g"],
        wuq_ext, g["kv_norm_g"], wk_ext, wv)
    return gx, _shard_early_grads(dwin_ext, dwuq_ext, dwk_ext, dwv), late_grads, jnp.concatenate([vec_c, vec_d], axis=0)


_VEC_LAYOUT = (("norm_post_g", 0, 0, 1024), ("ple_norm_g", 1, 0, 1024), ("b_ple_gate", 2, 0, 1024), ("sb_out_norm_g", 3, 0, 512),
               ("mla_out_norm_g", 3, 512, 512), ("norm_pre_g", 8, 0, 1024), ("q_norm_g", 9, 0, 256), ("kv_norm_g", 9, 256, 128))
_LOSS_ROW = 4
_WEIGHT_ORDER = ("norm_pre_g", "w_in", "q_norm_g", "w_uq", "kv_norm_g", "w_ukv", "sb_out_norm_g", "mla_out_norm_g", "w_out",
                 "norm_post_g", "w_ple", "ple_norm_g", "w_ple_gate", "b_ple_gate")


def kernel(x, p, positions, norm_pre_g, w_in, q_norm_g, w_uq, kv_norm_g, w_ukv, sb_out_norm_g, mla_out_norm_g, w_out, norm_post_g, w_ple, ple_norm_g, w_ple_gate, b_ple_gate, loss_target, m_norm_pre_g, m_w_in, m_q_norm_g, m_w_uq, m_kv_norm_g, m_w_ukv, m_sb_out_norm_g, m_mla_out_norm_g, m_w_out, m_norm_post_g, m_w_ple, m_ple_norm_g, m_w_ple_gate, m_b_ple_gate, v_norm_pre_g, v_w_in, v_q_norm_g, v_w_uq, v_kv_norm_g, v_w_ukv, v_sb_out_norm_g, v_mla_out_norm_g, v_w_out, v_norm_post_g, v_w_ple, v_ple_norm_g, v_w_ple_gate, v_b_ple_gate):
    w = {"norm_pre_g": norm_pre_g, "w_in": w_in[0], "q_norm_g": q_norm_g, "w_uq": w_uq[0], "kv_norm_g": kv_norm_g, "w_ukv": w_ukv[0],
         "sb_out_norm_g": sb_out_norm_g, "mla_out_norm_g": mla_out_norm_g, "w_out": w_out[0], "norm_post_g": norm_post_g,
         "w_ple": w_ple[0], "ple_norm_g": ple_norm_g, "w_ple_gate": w_ple_gate[0], "b_ple_gate": b_ple_gate}
    m = {"norm_pre_g": m_norm_pre_g, "w_in": m_w_in[0], "q_norm_g": m_q_norm_g, "w_uq": m_w_uq[0], "kv_norm_g": m_kv_norm_g,
         "w_ukv": m_w_ukv[0], "sb_out_norm_g": m_sb_out_norm_g, "mla_out_norm_g": m_mla_out_norm_g, "w_out": m_w_out[0],
         "norm_post_g": m_norm_post_g, "w_ple": m_w_ple[0], "ple_norm_g": m_ple_norm_g, "w_ple_gate": m_w_ple_gate[0],
         "b_ple_gate": m_b_ple_gate}
    v = {"norm_pre_g": v_norm_pre_g, "w_in": v_w_in[0], "q_norm_g": v_q_norm_g, "w_uq": v_w_uq[0], "kv_norm_g": v_kv_norm_g,
         "w_ukv": v_w_ukv[0], "sb_out_norm_g": v_sb_out_norm_g, "mla_out_norm_g": v_mla_out_norm_g, "w_out": v_w_out[0],
         "norm_post_g": v_norm_post_g, "w_ple": v_w_ple[0], "ple_norm_g": v_ple_norm_g, "w_ple_gate": v_w_ple_gate[0],
         "b_ple_gate": v_b_ple_gate}
    gathered = _allgather_weights([w[n] for n in _EARLY])
    gx, early_grads, late_red, vec = _local_grads(x[0], p[0, 0], positions[0], loss_target[0], w, dict(zip(_EARLY, gathered)),
                                                  [w[n] for n in _LATE])
    *early_red, vsum = _reduce_scatter_grads(early_grads, vec)
    gred = early_red + late_red
    loss = vsum[_LOSS_ROW, 0]

    g, delta, new_m, new_v = _adamw_small(vsum, w, m, v)
    for n, gn in zip(_BIG, gred):
        g[n], delta[n], new_m[n], new_v[n] = _adamw(w[n], gn, m[n], v[n])

    lead = lambda n, a: a[None] if n in _BIG else a
    return (loss, gx[None],
            *[lead(n, g[n]) for n in _WEIGHT_ORDER], *[lead(n, delta[n]) for n in _WEIGHT_ORDER],
            *[lead(n, new_m[n]) for n in _WEIGHT_ORDER], *[lead(n, new_v[n]) for n in _WEIGHT_ORDER])
```

```python
import numpy as np
import jax
import jax.numpy as jnp
from jax import lax
from jax.experimental import pallas as pl
from jax.experimental.pallas import tpu as pltpu

F32 = jnp.float32
BF16 = jnp.bfloat16
MESH = pl.DeviceIdType.MESH

D_MODEL = 1024
HEAD_DIM = 64
D_SB = 512
D_MLA = 512
Q_LORA = 256
KV_LORA = 128
QK_NOPE = 64
QK_ROPE = 32
PLE_DIM = 256
D_IN = 2976
D_EXT = 3072
ROPE_THETA = 10000.0
EPS = 1e-6
N_SHARD = 4

ADAM_LR = 0.001
ADAM_B1 = 0.9
ADAM_B2 = 0.999
ADAM_EPS = 1e-08
ADAM_WD = 0.01
ADAM_STEP = 10

LANES = 128
BK = 128
WQ = 256
MQ_FWD = 4096
MQ_BWD = 1024
MLA_CW = 256
SB_CUTOFF = 120.0
TM = 256
TM_PRE = 256
VEC_ROWS = 16
VMEM_DENSE = 52 * 1024 * 1024
VMEM_ATTN = 40 * 1024 * 1024


def _mm(a, b):
    return jnp.dot(a, b, preferred_element_type=F32)


def _mm_nt(a, b):
    return lax.dot_general(a, b, (((1,), (1,)), ((), ())), preferred_element_type=F32)


def _mm_tn(a, b):
    return lax.dot_general(a, b, (((0,), (0,)), ((), ())), preferred_element_type=F32)


def _seg(a, bd2):
    return _mm(_split2(a), bd2)


def _const(mask):
    return jnp.asarray(np.asarray(mask, np.float32), dtype=BF16)


def _blockdiag2(n, seg):
    r = (np.arange(2 * n)[:, None] % n) // seg
    c = np.arange(n)[None, :] // seg
    return _const(r == c)


def _sigmoid(a):
    return 1.0 / (1.0 + jnp.exp(-a))


def _rowmean(a):
    return jnp.mean(a, axis=-1, keepdims=True)


def _colsum(a):
    return jnp.sum(a, axis=0, keepdims=True)


def _rope_fwd(a, c, sa, sb):
    w = a.shape[-1]
    return a * c + pltpu.roll(a, w - 16, 1) * sa + pltpu.roll(a, 16, 1) * sb


def _rope_bwd(g, c, sa, sb):
    w = g.shape[-1]
    return g * c + pltpu.roll(g * sa, 16, 1) + pltpu.roll(g * sb, w - 16, 1)


def _full(shape):
    return pl.BlockSpec(shape, lambda *_: (0,) * len(shape))


def _acc(shape):
    return pl.BlockSpec(shape, lambda *_: (0,) * len(shape))


def _full2(shape):
    return pl.BlockSpec(shape, lambda p, i: (0, 0))


def _cols(height, tm=TM):
    return pl.BlockSpec((height, tm), lambda i: (0, i))


def _rows(width, tm=TM):
    return pl.BlockSpec((tm, width), lambda i: (i, 0))


def _pre_fwd(x, tabs, gpre, win, gq, wuq, gkv, wk, wv):
    s = x.shape[0]
    c_t, sa_t, sb_t = tabs
    rw, cl = (lambda width: _rows(width, TM_PRE)), (lambda height: _cols(height, TM_PRE))

    def body(x_ref, c_ref, sa_ref, sb_ref, gpre_ref, win_ref, gq_ref, wuq_ref, gkv_ref, wk_ref, wv_ref,
             sbq_ref, sbk_ref, sbv_ref, sbg_ref, mlag_ref, cq_ref, ckv_ref, qc_ref, kc_ref, mv_ref,
             sbkt_ref, sbvt_ref, kct_ref, mvt_ref):
        xv = x_ref[...]
        r1 = lax.rsqrt(_rowmean(xv * xv) + EPS)
        h = (xv * r1 * gpre_ref[...]).astype(BF16)
        proj = _mm(h, win_ref[...])
        sbq_ref[...] = proj[:, 0:512].astype(BF16)
        sbk_ref[...] = proj[:, 512:1024].astype(BF16)
        sbv_ref[...] = proj[:, 1024:1536].astype(BF16)
        sbkt_ref[...] = proj[:, 512:1024].T.astype(BF16)
        sbvt_ref[...] = proj[:, 1024:1536].T.astype(BF16)
        sbg_ref[...] = proj[:, 1536:2048]
        cq = proj[:, 2048:2304]
        ckv = proj[:, 2304:2432]
        kr = proj[:, 2432:2560]
        mlag_ref[...] = proj[:, 2560:3072]
        cq_ref[...] = cq
        ckv_ref[...] = ckv
        c1, sa1, sb1 = c_ref[...], sa_ref[...], sb_ref[...]
        c8, sa8, sb8 = jnp.tile(c1, (1, 8)), jnp.tile(sa1, (1, 8)), jnp.tile(sb1, (1, 8))
        cqn = (cq * lax.rsqrt(_rowmean(cq * cq) + EPS) * gq_ref[...]).astype(BF16)
        qe = _mm(cqn, wuq_ref[...])
        qc_ref[...] = _rope_fwd(qe, c8, sa8, sb8).astype(BF16)
        ckvn = (ckv * lax.rsqrt(_rowmean(ckv * ckv) + EPS) * gkv_ref[...]).astype(BF16)
        ke = _mm(ckvn, wk_ref[...])
        krr = _rope_fwd(kr, c1, sa1, sb1)
        kcat = ke + jnp.tile(krr, (1, 8))
        kc_ref[...] = kcat.astype(BF16)
        kct_ref[...] = kcat.T.astype(BF16)
        mval = _mm(ckvn, wv_ref[...])
        mv_ref[...] = mval.astype(BF16)
        mvt_ref[...] = mval.T.astype(BF16)

    out_shape = (
        jax.ShapeDtypeStruct((s, 512), BF16), jax.ShapeDtypeStruct((s, 512), BF16), jax.ShapeDtypeStruct((s, 512), BF16),
        jax.ShapeDtypeStruct((s, 512), F32), jax.ShapeDtypeStruct((s, 512), F32),
        jax.ShapeDtypeStruct((s, Q_LORA), F32), jax.ShapeDtypeStruct((s, KV_LORA), F32),
        jax.ShapeDtypeStruct((s, 1024), BF16), jax.ShapeDtypeStruct((s, 1024), BF16), jax.ShapeDtypeStruct((s, 512), BF16),
        jax.ShapeDtypeStruct((512, s), BF16), jax.ShapeDtypeStruct((512, s), BF16), jax.ShapeDtypeStruct((1024, s), BF16),
        jax.ShapeDtypeStruct((512, s), BF16),
    )
    return pl.pallas_call(
        body, name="pre_fwd", grid=(s // TM_PRE,), out_shape=out_shape,
        in_specs=[rw(D_MODEL), rw(LANES), rw(LANES), rw(LANES), _full((1, D_MODEL)), _full((D_MODEL, D_EXT)),
                  _full((1, Q_LORA)), _full((Q_LORA, 1024)), _full((1, KV_LORA)), _full((KV_LORA, 1024)), _full((KV_LORA, 512))],
        out_specs=(rw(512), rw(512), rw(512), rw(512), rw(512), rw(Q_LORA), rw(KV_LORA),
                   rw(1024), rw(1024), rw(512), cl(512), cl(512), cl(1024), cl(512)),
        compiler_params=pltpu.CompilerParams(vmem_limit_bytes=VMEM_DENSE),
    )(x, c_t, sa_t, sb_t, gpre, win, gq, wuq, gkv, wk, wv)


def _softplus(z):
    neg_abs = lax.bitcast_convert_type(lax.bitcast_convert_type(z, jnp.uint32) | jnp.uint32(0x80000000), F32)
    return jnp.maximum(z, 0.0) + jnp.log(1.0 + jnp.exp(neg_abs))


def _sum_matrix(kind, terms):
    r, c = np.arange(2 * BK)[:, None], np.arange(2 * BK * terms)[None, :] % (2 * BK)
    rk, ck = r % BK, c % BK
    return _const(((r // BK) == (c // BK)) & {"suffix": ck >= rk, "prefix": ck <= rk}[kind])


def _split_rows(a):
    hi = a.astype(BF16)
    return jnp.concatenate([hi, (a - hi.astype(F32)).astype(BF16)], axis=0)


def _heads_t(blk, rowi):
    zero = jnp.zeros_like(blk)
    return jnp.concatenate([jnp.where(rowi < 64, blk, zero), jnp.where(rowi >= 64, blk, zero)], axis=1)


def _mask_keys(a, valid, fill=0.0):
    return jnp.concatenate([jnp.where(valid, a[0:BK], fill), jnp.where(valid, a[BK:2 * BK], fill)], axis=0)


def _split2(a):
    hi = a.astype(BF16)
    lo = (a - hi.astype(F32)).astype(BF16)
    return jnp.concatenate([hi, lo], axis=1)


def _pair_stack(b, lane):
    zero = jnp.zeros_like(b)
    return jnp.concatenate([jnp.where(lane < 64, b, zero), jnp.where(lane >= 64, b, zero)], axis=0)


def _sb_fwd(q, k, vt, late):
    s = q.shape[0]
    n = len(late)

    def body(q_ref, k_ref, vt_ref, usuf_ref, *rest):
        ins, o_ref, outs = rest[:n], rest[n], rest[n + 1:2 * n + 1]
        acc_scr, run_scr = rest[2 * n + 1:2 * n + 3]
        bufs, (send_sems, recv_sems, out_sems) = rest[2 * n + 3:3 * n + 3], rest[3 * n + 3:]
        p, i = pl.program_id(0), pl.program_id(1)
        gather_start, gather_forward, gather_finish = _gather_steps([a.shape for a in late], ins, bufs, send_sems, recv_sems)

        @pl.when((p == 0) & (i == 0))
        def _():
            gather_start()

        @pl.when((p == 2) & (i == 0))
        def _():
            gather_forward()

        lane = lax.broadcasted_iota(jnp.int32, (1, LANES), 1)
        rowi = lax.broadcasted_iota(jnp.int32, (LANES, 1), 0)
        keyi = lax.broadcasted_iota(jnp.int32, (BK, WQ), 0)
        qryi = lax.broadcasted_iota(jnp.int32, (BK, WQ), 1) + i * WQ
        qs = q_ref[...] * (HEAD_DIM ** -0.5)

        def group(blocks, masked, seen=None):
            seen = seen or [0] * len(blocks)
            starts = [pl.multiple_of(j * BK, BK) for j in blocks]
            valid = [(keyi[:, lo:] + j * BK) < qryi[:, lo:] if m else None for j, m, lo in zip(blocks, masked, seen)]
            zs = [_mm_nt(_pair_stack(k_ref[pl.ds(ks, BK), :], lane), qs[lo:]) for ks, lo in zip(starts, seen)]
            sps = [_softplus(z) for z in zs]
            sps = [sp if ok is None else _mask_keys(sp, ok) for sp, ok in zip(sps, valid)]
            cums = [_mm(usuf_ref[...], _split_rows(sp)) for sp in sps]
            ws = [jnp.exp(z - c) for z, c in zip(zs, cums)]
            ws = [w if ok is None else _mask_keys(w, ok) for w, ok in zip(ws, valid)]
            pvs = [_mm(_heads_t(vt_ref[:, pl.ds(ks, BK)], rowi), w.astype(BF16)) for ks, w in zip(starts, ws)]
            for pv, c, lo in zip(pvs, cums, seen):
                r0, r1 = run_scr[0:1, lo:], run_scr[1:2, lo:]
                acc_scr[:, lo:] += jnp.where(rowi < 64, jnp.exp(-r0), jnp.exp(-r1)) * pv
                run_scr[0:1, lo:] = r0 + c[0:1]
                run_scr[1:2, lo:] = r1 + c[BK:BK + 1]

        assert WQ == 2 * BK
        acc_scr[...] = jnp.zeros_like(acc_scr)
        run_scr[...] = jnp.zeros_like(run_scr)

        @pl.when(i == 0)
        def _():
            group([1, 0], [True, True], [BK, 0])

        @pl.when(i > 0)
        def _():
            group([2 * i + 1, 2 * i, 2 * i - 1, 2 * i - 2], [True, True, False, False], [BK, 0, 0, 0])

        def unfinished():
            return (jnp.min(run_scr[0:2, :]) < SB_CUTOFF).astype(jnp.int32)

        def step(c):
            group([2 * i - 1 - 2 * c[0], 2 * i - 2 - 2 * c[0]], [False, False])
            return c[0] + 1, unfinished()

        lax.while_loop(lambda c: (c[0] < i) & (c[1] > 0), step, (jnp.int32(1), unfinished()))
        o_ref[...] = acc_scr[...].T

        @pl.when((p == pl.num_programs(0) - 1) & (i == pl.num_programs(1) - 1))
        def _():
            gather_finish()
            copies = [pltpu.make_async_copy(bufs[t], outs[t], out_sems.at[t]) for t in range(n)]
            for cp in copies:
                cp.start()
            for cp in copies:
                cp.wait()

    qspec = pl.BlockSpec((WQ, LANES), lambda p, i: (i, p))
    kspec = pl.BlockSpec((s, LANES), lambda p, i: (0, p))
    tspec = pl.BlockSpec((LANES, s), lambda p, i: (p, 0))
    gathered = [jax.ShapeDtypeStruct((N_SHARD,) + a.shape, BF16) for a in late]
    return pl.pallas_call(
        body, name="sb_fwd", grid=(4, s // WQ),
        out_shape=(jax.ShapeDtypeStruct((s, 512), F32), *gathered),
        in_specs=[qspec, kspec, tspec, _full2((2 * BK, 4 * BK))] + [_full2(a.shape) for a in late],
        out_specs=(qspec,) + (pl.BlockSpec(memory_space=pl.ANY),) * n,
        scratch_shapes=[pltpu.VMEM((LANES, WQ), F32), pltpu.VMEM((8, WQ), F32)] + [pltpu.VMEM(g.shape, BF16) for g in gathered]
                       + [pltpu.SemaphoreType.DMA((6 * n,)), pltpu.SemaphoreType.DMA((6 * n,)), pltpu.SemaphoreType.DMA((n,))],
        compiler_params=pltpu.CompilerParams(vmem_limit_bytes=VMEM_ATTN),
    )(q, k, vt, _sum_matrix("suffix", 2), *late)


def _sb_bwd(q, k, kt, v, do, late):
    s = q.shape[0]
    n = len(late)
    halves = [a.shape[1] // 2 for a in late]

    def body(q_ref, k_ref, kt_ref, v_ref, do_ref, usuf_ref, upre_ref, *rest):
        g_refs, (dq_ref, dk_ref, dv_ref), outs = rest[:n], rest[n:n + 3], rest[n + 3:2 * n + 3]
        later_scr, dqt_scr, st_scr = rest[2 * n + 3:2 * n + 6]
        f_scr, reduce_scr, out_sems = rest[2 * n + 6:3 * n + 6], rest[3 * n + 6:-1], rest[-1]
        p, i = pl.program_id(0), pl.program_id(1)
        reduce_load, reduce_partial, reduce_total, reduce_finish = _reduce_steps(halves, g_refs, f_scr, reduce_scr)

        @pl.when((p == 0) & (i == 0))
        def _():
            reduce_load()

        @pl.when((p == 1) & (i == 0))
        def _():
            reduce_partial()

        @pl.when((p == 3) & (i == 0))
        def _():
            reduce_total()

        @pl.when(i == 0)
        def _():
            dk_ref[...] = jnp.zeros_like(dk_ref)
            dv_ref[...] = jnp.zeros_like(dv_ref)

        lane = lax.broadcasted_iota(jnp.int32, (1, LANES), 1)
        rowi = lax.broadcasted_iota(jnp.int32, (LANES, 1), 0)
        keyi = lax.broadcasted_iota(jnp.int32, (BK, WQ), 0)
        qryi = lax.broadcasted_iota(jnp.int32, (BK, WQ), 1) + i * WQ
        qs = q_ref[...] * (HEAD_DIM ** -0.5)
        dob = do_ref[...]
        dot = dob.astype(F32).T.astype(BF16)

        def scores(j, lo=0):
            return _mm_nt(_pair_stack(k_ref[pl.ds(pl.multiple_of(j * BK, BK), BK), :], lane), qs[lo:])

        def scan(blocks, masked, seen=None):
            seen = seen or [0] * len(blocks)
            sps = [_softplus(scores(j, lo)) for j, lo in zip(blocks, seen)]
            sps = [_mask_keys(sp, (keyi[:, lo:] + j * BK) < qryi[:, lo:]) if m else sp
                   for sp, j, m, lo in zip(sps, blocks, masked, seen)]
            for sp, j, lo in zip(sps, blocks, seen):
                run = st_scr[0:2, :]
                later_scr[j, 0:2, :] = run
                st_scr[0:2, lo:] = run[:, lo:] + jnp.concatenate([jnp.sum(sp[0:BK], axis=0, keepdims=True),
                                                                  jnp.sum(sp[BK:2 * BK], axis=0, keepdims=True)], axis=0)

        def sweep(blocks, masked, seen=None):
            seen = seen or [0] * len(blocks)
            starts = [pl.multiple_of(j * BK, BK) for j in blocks]
            valid = [(keyi[:, lo:] + j * BK) < qryi[:, lo:] if m else None for j, m, lo in zip(blocks, masked, seen)]
            zs = [scores(j, lo) for j, lo in zip(blocks, seen)]
            us = [jnp.exp(lax.bitcast_convert_type(lax.bitcast_convert_type(z, jnp.uint32) | jnp.uint32(0x80000000), F32))
                  for z in zs]
            sps = [jnp.maximum(z, 0.0) + jnp.log(1.0 + u) for z, u in zip(zs, us)]
            sps = [sp if ok is None else _mask_keys(sp, ok) for sp, ok in zip(sps, valid)]
            sigs = [jnp.where(z >= 0.0, 1.0, u) / (1.0 + u) for z, u in zip(zs, us)]
            cums = [_mm(usuf_ref[...], _split_rows(sp)) for sp in sps]
            dws = [_mm(_pair_stack(v_ref[pl.ds(ks, BK), :], lane), dot[:, lo:]) for ks, lo in zip(starts, seen)]
            wfs = []
            for z, c, j, ok, lo in zip(zs, cums, blocks, valid, seen):
                f = jnp.exp(-later_scr[j, 0:2, lo:])
                wide = (BK, WQ - lo)
                wf = jnp.exp(z - c) * jnp.concatenate([jnp.broadcast_to(f[0:1], wide), jnp.broadcast_to(f[1:2], wide)], axis=0)
                wfs.append(wf if ok is None else _mask_keys(wf, ok))
            es = [dw * wf for dw, wf in zip(dws, wfs)]
            pres = [_mm(upre_ref[...], e.astype(BF16)) for e in es]
            dzs = []
            for e, pre, sig, ok, lo in zip(es, pres, sigs, valid, seen):
                e0 = pre[0:BK] + st_scr[0:1, lo:]
                e1 = pre[BK:2 * BK] + st_scr[1:2, lo:]
                st_scr[0:1, lo:] = e0[BK - 1:BK]
                st_scr[1:2, lo:] = e1[BK - 1:BK]
                dz = e - sig * jnp.concatenate([e0, e1], axis=0)
                dzs.append((dz if ok is None else _mask_keys(dz, ok)).astype(BF16))
            whole = [b for b, lo in enumerate(seen) if lo == 0]
            dqt_scr[...] += _mm(jnp.concatenate([_heads_t(kt_ref[:, pl.ds(starts[b], BK)], rowi) for b in whole], axis=1),
                                jnp.concatenate([dzs[b] for b in whole], axis=0))
            for b, lo in enumerate(seen):
                if lo:
                    dqt_scr[:, lo:] += _mm(_heads_t(kt_ref[:, pl.ds(starts[b], BK)], rowi), dzs[b])
            for ks, dz, wf, lo in zip(starts, dzs, wfs, seen):
                rk = _mm(dz, qs[lo:])
                dk_ref[pl.ds(ks, BK), :] += jnp.where(lane < 64, rk[0:BK], rk[BK:2 * BK])
                rv = _mm(wf.astype(BF16), dob[lo:])
                dv_ref[pl.ds(ks, BK), :] += jnp.where(lane < 64, rv[0:BK], rv[BK:2 * BK])

        assert WQ == 2 * BK
        st_scr[...] = jnp.zeros_like(st_scr)

        @pl.when(i == 0)
        def _():
            scan([1, 0], [True, True], [BK, 0])

        @pl.when(i > 0)
        def _():
            scan([2 * i + 1, 2 * i, 2 * i - 1, 2 * i - 2], [True, True, False, False], [BK, 0, 0, 0])

        def unfinished():
            return (jnp.min(st_scr[0:2, :]) < SB_CUTOFF).astype(jnp.int32)

        def step(c):
            scan([2 * i - 1 - 2 * c[0], 2 * i - 2 - 2 * c[0]], [False, False])
            return c[0] + 1, unfinished()

        npairs, _ = lax.while_loop(lambda c: (c[0] < i) & (c[1] > 0), step, (jnp.minimum(i, 1), unfinished()))

        st_scr[...] = jnp.zeros_like(st_scr)
        dqt_scr[...] = jnp.zeros_like(dqt_scr)
        first = 2 * (i - npairs)

        def early(t, carry):
            sweep([first + 2 * t, first + 2 * t + 1], [False, False])
            return carry

        lax.fori_loop(0, npairs - 1, early, 0)

        @pl.when(i == 0)
        def _():
            sweep([0, 1], [True, True], [0, BK])

        @pl.when(i > 0)
        def _():
            sweep([2 * i - 2, 2 * i - 1, 2 * i, 2 * i + 1], [False, False, True, True], [0, 0, 0, BK])

        dq_ref[...] = (dqt_scr[...].T * (HEAD_DIM ** -0.5)).astype(BF16)

        @pl.when((p == pl.num_programs(0) - 1) & (i == pl.num_programs(1) - 1))
        def _():
            reduce_finish()
            copies = [pltpu.make_async_copy(f_scr[t], outs[t], out_sems.at[t]) for t in range(n)]
            for cp in copies:
                cp.start()
            for cp in copies:
                cp.wait()

    qspec = pl.BlockSpec((WQ, LANES), lambda p, i: (i, p))
    kspec = pl.BlockSpec((s, LANES), lambda p, i: (0, p))
    tspec = pl.BlockSpec((LANES, s), lambda p, i: (p, 0))
    anywhere = pl.BlockSpec(memory_space=pl.ANY)
    reduced = [jax.ShapeDtypeStruct(a.shape[1:], F32) for a in late]
    return pl.pallas_call(
        body, name="sb_bwd", grid=(4, s // WQ),
        out_shape=(jax.ShapeDtypeStruct((s, 512), BF16), jax.ShapeDtypeStruct((s, 512), F32),
                   jax.ShapeDtypeStruct((s, 512), F32), *reduced),
        in_specs=[qspec, kspec, tspec, kspec, qspec, _full2((2 * BK, 4 * BK)), _full2((2 * BK, 2 * BK))] + [anywhere] * n,
        out_specs=(qspec, kspec, kspec) + (anywhere,) * n,
        scratch_shapes=[pltpu.VMEM((s // BK, 8, WQ), F32), pltpu.VMEM((LANES, WQ), F32), pltpu.VMEM((8, WQ), F32)]
                       + [pltpu.VMEM(r.shape, F32) for r in reduced] + _reduce_scratch(late) + [pltpu.SemaphoreType.DMA((n,))],
        compiler_params=pltpu.CompilerParams(vmem_limit_bytes=VMEM_ATTN),
    )(q, k, kt, v, do, _sum_matrix("suffix", 2), _sum_matrix("prefix", 1), *late)


MLA_SCALE = (QK_NOPE + QK_ROPE) ** -0.5
LOG2E = 1.4426950408889634


def _mla_keys(kb):
    zero = jnp.zeros((BK, LANES), kb.dtype)
    return jnp.concatenate([jnp.concatenate([kb[:, 0:LANES], zero], axis=1),
                            jnp.concatenate([zero, kb[:, LANES:2 * LANES]], axis=1)], axis=0)


def _mla_fwd(qc, kc, vt):
    s = qc.shape[0]
    mq = min(MQ_FWD, s)
    rows_l = 16

    def body(q_ref, k_ref, vt_ref, o_ref, l_ref, p_scr, ot_scr, st_scr):
        i = pl.program_id(1)
        row = lax.broadcasted_iota(jnp.int32, (LANES, 1), 0)
        orow = lax.broadcasted_iota(jnp.int32, (rows_l, 2 * BK), 0)
        ocol = lax.broadcasted_iota(jnp.int32, (rows_l, 2 * BK), 1)
        ones = jnp.where(((orow == 0) & (ocol < BK)) | ((orow == 1) & (ocol >= BK)), 1.0, 0.0).astype(BF16)

        def chunks(lo, hi):
            return [(a, min(a + MLA_CW, hi)) for a in range(lo, hi, MLA_CW)]

        def keys(j):
            return _mla_keys(k_ref[pl.ds(pl.multiple_of(j * BK, BK), BK), :])

        def values_t(j):
            vtb = vt_ref[:, pl.ds(pl.multiple_of(j * BK, BK), BK)]
            zero = jnp.zeros_like(vtb)
            top = jnp.concatenate([jnp.where(row < 64, vtb, zero), jnp.where(row >= 64, vtb, zero)], axis=1)
            return jnp.concatenate([top, ones], axis=0)

        def pair_values(ja):
            return jnp.concatenate([values_t(ja), values_t(ja + 1)], axis=1)

        def softmax(ja, za, zb, masked, a, b):
            c = MLA_SCALE * LOG2E
            parts = [za[0:BK] * c, za[BK:2 * BK] * c, zb[0:BK] * c, zb[BK:2 * BK] * c]
            if masked:
                keyc = lax.broadcasted_iota(jnp.int32, (BK, b - a), 0)
                qryc = (lax.broadcasted_iota(jnp.int32, (BK, b - a), 1) + (i * mq + a)) // 64
                va = ((keyc + ja * BK) // 64) <= qryc
                vb = ((keyc + (ja + 1) * BK) // 64) <= qryc
                parts = [jnp.where(va, parts[0], -1e30), jnp.where(va, parts[1], -1e30),
                         jnp.where(vb, parts[2], -1e30), jnp.where(vb, parts[3], -1e30)]
            m0, m1 = st_scr[0:1, a:b], st_scr[1:2, a:b]
            n0 = jnp.maximum(m0, jnp.max(jnp.maximum(parts[0], parts[2]), axis=0, keepdims=True))
            n1 = jnp.maximum(m1, jnp.max(jnp.maximum(parts[1], parts[3]), axis=0, keepdims=True))
            st_scr[2:3, a:b] = jnp.exp2(m0 - n0)
            st_scr[3:4, a:b] = jnp.exp2(m1 - n1)
            st_scr[0:1, a:b] = n0
            st_scr[1:2, a:b] = n1
            p_scr[:, a:b] = jnp.concatenate([jnp.exp2(parts[0] - n0), jnp.exp2(parts[1] - n1),
                                             jnp.exp2(parts[2] - n0), jnp.exp2(parts[3] - n1)], axis=0).astype(BF16)

        def accumulate(vals, a, b):
            pv = _mm(vals, p_scr[:, a:b])
            f = jnp.where(row < 64, st_scr[2:3, a:b], st_scr[3:4, a:b])
            ot_scr[0:LANES, a:b] = f * ot_scr[0:LANES, a:b] + pv[0:LANES]
            ot_scr[LANES:LANES + 8, a:b] = st_scr[2:10, a:b] * ot_scr[LANES:LANES + 8, a:b] + pv[LANES:LANES + 8]

        def step(n, diag, lo=0, prev_lo=0):
            ka, kb = keys(2 * n), keys(2 * n + 1)
            vals = pair_values(2 * n - 2)
            for a, b in chunks(prev_lo, lo):
                accumulate(vals, a, b)
            for a, b in chunks(lo, mq):
                qc_ = q_ref[a:b, :]
                za, zb = _mm_nt(ka, qc_), _mm_nt(kb, qc_)
                accumulate(vals, a, b)
                softmax(2 * n, za, zb, diag and a < lo + 2 * BK, a, b)

        def first(diag):
            ka, kb = keys(0), keys(1)
            for a, b in chunks(0, mq):
                qc_ = q_ref[a:b, :]
                softmax(0, _mm_nt(ka, qc_), _mm_nt(kb, qc_), diag and a < 2 * BK, a, b)

        st_scr[...] = jnp.concatenate([jnp.full((2, mq), -1e30, F32), jnp.ones((14, mq), F32)], axis=0)
        ot_scr[...] = jnp.zeros_like(ot_scr)

        npq = mq // (2 * BK)
        seen = lambda d: 2 * BK * max(d, 0)

        @pl.when(i == 0)
        def _():
            first(True)
            for d in range(1, npq):
                step(d, True, seen(d), seen(d - 1))

        if s > mq:
            @pl.when(i > 0)
            def _():
                first(False)
                lax.fori_loop(1, npq * i, lambda n, c: (step(n, False), c)[1], 0)
                for d in range(npq):
                    step(npq * i + d, True, seen(d), seen(d - 1))

        vals = pair_values(2 * (npq * (i + 1) - 1))
        for a, b in chunks(seen(npq - 1), mq):
            accumulate(vals, a, b)
        for a, b in chunks(0, mq):
            l0, l1 = ot_scr[LANES:LANES + 1, a:b], ot_scr[LANES + 1:LANES + 2, a:b]
            o_ref[a:b, :] = (ot_scr[0:LANES, a:b] / jnp.where(row < 64, l0, l1)).T
            l_ref[a:b, :] = jnp.where(row < 64, st_scr[0:1, a:b] + jnp.log2(l0), st_scr[1:2, a:b] + jnp.log2(l1)).T

    qspec = pl.BlockSpec((mq, 2 * LANES), lambda p, i: (i, p))
    kspec = pl.BlockSpec((s, 2 * LANES), lambda p, i: (0, p))
    vtspec = pl.BlockSpec((LANES, s), lambda p, i: (p, 0))
    ospec = pl.BlockSpec((mq, LANES), lambda p, i: (i, p))
    return pl.pallas_call(
        body, name="mla_fwd", grid=(4, s // mq),
        out_shape=(jax.ShapeDtypeStruct((s, 512), F32), jax.ShapeDtypeStruct((s, 512), F32)),
        in_specs=[qspec, kspec, vtspec], out_specs=(ospec, ospec),
        scratch_shapes=[pltpu.VMEM((4 * BK, mq), BF16), pltpu.VMEM((LANES + 8, mq), F32), pltpu.VMEM((16, mq), F32)],
        compiler_params=pltpu.CompilerParams(vmem_limit_bytes=VMEM_ATTN),
    )(qc, kc, vt)


def _mla_bwd(qc, kc, kct, v, do, lse, delta):
    s = qc.shape[0]
    mq = min(MQ_BWD, s)

    def body(q_ref, k_ref, kt_ref, v_ref, do_ref, l_ref, d_ref, dq_ref, dk_ref, dv_ref, dqt_scr, p_scr, dz_scr):
        i = pl.program_id(1)

        @pl.when(i == 0)
        def _():
            dk_ref[...] = jnp.zeros_like(dk_ref)
            dv_ref[...] = jnp.zeros_like(dv_ref)

        lane = lax.broadcasted_iota(jnp.int32, (1, LANES), 1)
        keyc = lax.broadcasted_iota(jnp.int32, (BK, mq), 0)
        qryc = (lax.broadcasted_iota(jnp.int32, (BK, mq), 1) + i * mq) // 64
        qw = q_ref[...]
        dob = do_ref[...]
        dost = (dob.astype(F32) * MLA_SCALE).T.astype(BF16)
        lt = l_ref[...].T
        dt = (d_ref[...] * MLA_SCALE).T
        lse0, lse1 = lt[0:1], lt[64:65]
        dl0, dl1 = dt[0:1], dt[64:65]
        dqt_scr[...] = jnp.zeros_like(dqt_scr)

        def products(j, lo=0):
            ks = pl.multiple_of(j * BK, BK)
            return (_mm_nt(_mla_keys(k_ref[pl.ds(ks, BK), :]), qw[lo:]),
                    _mm(_pair_stack(v_ref[pl.ds(ks, BK), :], lane), dost[:, lo:]))

        def grads(j, slot, zt, dwt, masked, lo=0):
            zt = zt * (MLA_SCALE * LOG2E)
            p0 = jnp.exp2(zt[0:BK] - lse0[:, lo:])
            p1 = jnp.exp2(zt[BK:2 * BK] - lse1[:, lo:])
            if masked:
                valid = ((keyc[:, lo:] + j * BK) // 64) <= qryc[:, lo:]
                p0, p1 = jnp.where(valid, p0, 0.0), jnp.where(valid, p1, 0.0)
            rows = slice(slot * BK, (slot + 1) * BK)
            p_scr[0, rows, lo:] = p0.astype(BF16)
            p_scr[1, rows, lo:] = p1.astype(BF16)
            dz_scr[0, rows, lo:] = (p0 * (dwt[0:BK] - dl0[:, lo:])).astype(BF16)
            dz_scr[1, rows, lo:] = (p1 * (dwt[BK:2 * BK] - dl1[:, lo:])).astype(BF16)

        def scatter(ja, lo=0):
            ks = pl.multiple_of(ja * BK, 2 * BK)
            rv = []
            for h in range(2):
                cols = slice(h * LANES, (h + 1) * LANES)
                dzh = dz_scr[h, :, lo:]
                dqt_scr[cols, lo:] += _mm(kt_ref[cols, pl.ds(ks, 2 * BK)], dzh)
                dk_ref[pl.ds(ks, 2 * BK), cols] += _mm(dzh, qw[lo:, cols])
                rv.append(_mm(p_scr[h, :, lo:], dob[lo:]))
            dv_ref[pl.ds(ks, 2 * BK), :] += jnp.where(lane < 64, rv[0], rv[1])

        def step(n, masked, lo=0, prev_lo=0):
            za, wa = products(2 * n, lo)
            zb, wb = products(2 * n + 1, lo)
            scatter(2 * n - 2, prev_lo)
            grads(2 * n, 0, za, wa, masked, lo)
            grads(2 * n + 1, 1, zb, wb, masked, lo)

        def first(masked):
            za, wa = products(0)
            zb, wb = products(1)
            grads(0, 0, za, wa, masked)
            grads(1, 1, zb, wb, masked)

        npq = mq // (2 * BK)
        seen = lambda d: 2 * BK * max(d, 0)

        @pl.when(i == 0)
        def _():
            first(True)
            for d in range(1, npq):
                step(d, True, seen(d), seen(d - 1))

        @pl.when(i > 0)
        def _():
            first(False)
            lax.fori_loop(1, npq * i, lambda n, c: (step(n, False), c)[1], 0)
            for d in range(npq):
                step(npq * i + d, True, seen(d), seen(d - 1))

        scatter(2 * (npq * (i + 1) - 1), seen(npq - 1))
        dq_ref[...] = dqt_scr[...].T

    qspec = pl.BlockSpec((mq, 2 * LANES), lambda p, i: (i, p))
    kspec = pl.BlockSpec((s, 2 * LANES), lambda p, i: (0, p))
    ktspec = pl.BlockSpec((2 * LANES, s), lambda p, i: (p, 0))
    vspec = pl.BlockSpec((s, LANES), lambda p, i: (0, p))
    ospec = pl.BlockSpec((mq, LANES), lambda p, i: (i, p))
    return pl.pallas_call(
        body, name="mla_bwd", grid=(4, s // mq),
        out_shape=(jax.ShapeDtypeStruct((s, 1024), F32), jax.ShapeDtypeStruct((s, 1024), F32),
                   jax.ShapeDtypeStruct((s, 512), F32)),
        in_specs=[qspec, kspec, ktspec, vspec, ospec, ospec, ospec], out_specs=(qspec, kspec, vspec),
        scratch_shapes=[pltpu.VMEM((2 * LANES, mq), F32), pltpu.VMEM((2, 2 * BK, mq), BF16), pltpu.VMEM((2, 2 * BK, mq), BF16)],
        compiler_params=pltpu.CompilerParams(vmem_limit_bytes=VMEM_ATTN),
    )(qc, kc, kct, v, do, lse, delta)


def _post(x, p, tgt, sbo, mlao, sbg, mlag, gsb, gmla, wout, gpost, wple, gple, wpg, bpg):
    s = x.shape[0]

    def body(x_ref, p_ref, t_ref, sbo_ref, mlao_ref, sbg_ref, mlag_ref, gsb_ref, gmla_ref, wout_ref,
             gpost_ref, wple_ref, gple_ref, wpg_ref, bpg_ref, bd_ref,
             dsbo_ref, dmlao_ref, delta_ref, dsbg_ref, dmlag_ref, dxres_ref, dwout_ref, dwpg_ref, dwple_ref, vec_ref):
        i = pl.program_id(0)

        @pl.when(i == 0)
        def _():
            dwout_ref[...] = jnp.zeros_like(dwout_ref)
            dwpg_ref[...] = jnp.zeros_like(dwpg_ref)
            dwple_ref[...] = jnp.zeros_like(dwple_ref)
            vec_ref[...] = jnp.zeros_like(vec_ref)

        inv_hd = 1.0 / HEAD_DIM

        def head_fwd(o, g, gate):
            r = lax.rsqrt(_seg(o * o, bd_ref[...]) * inv_hd + EPS)
            hat = o * r
            n = hat * g
            sg = _sigmoid(gate)
            return hat, r, n, sg, n * (gate * sg)

        sbo, mlao, sbg_v, mlag_v = sbo_ref[...], mlao_ref[...], sbg_ref[...], mlag_ref[...]
        gsb_v, gmla_v = gsb_ref[...], gmla_ref[...]
        sb_hat, sb_r, sb_n, sb_sg, sb_y = head_fwd(sbo, gsb_v, sbg_v)
        ml_hat, ml_r, ml_n, ml_sg, ml_y = head_fwd(mlao, gmla_v, mlag_v)
        mix = jnp.concatenate([sb_y, ml_y], axis=1).astype(BF16)
        y = _mm(mix, wout_ref[...])
        ry = lax.rsqrt(_rowmean(y * y) + EPS)
        y_hat = y * ry
        gpost_v = gpost_ref[...]
        x1 = x_ref[...] + y_hat * gpost_v
        pb = p_ref[...].astype(BF16)
        pl_ = _mm(pb, wple_ref[...])
        rp = lax.rsqrt(_rowmean(pl_ * pl_) + EPS)
        pl_hat = pl_ * rp
        gple_v = gple_ref[...]
        ple = pl_hat * gple_v
        x1b = x1.astype(BF16)
        gate = _sigmoid(_mm(x1b, wpg_ref[...]) + bpg_ref[...])
        err = x1 + ple * gate - t_ref[...]
        loss = 0.5 * jnp.sum(_rowmean(err * err))
        dout = err * (1.0 / D_MODEL)

        du = dout * ple * gate * (1.0 - gate)
        dub = du.astype(BF16)
        dple = dout * gate
        dx1 = dout + _mm_nt(dub, wpg_ref[...])
        dwpg_ref[...] += _mm_tn(x1b, dub)
        dplh = dple * gple_v
        dpl = rp * (dplh - pl_hat * _rowmean(dplh * pl_hat))
        dwple_ref[...] += _mm_tn(pb, dpl.astype(BF16))
        dxres_ref[...] = dx1
        dyh = dx1 * gpost_v
        dy = ry * (dyh - y_hat * _rowmean(dyh * y_hat))
        dyb = dy.astype(BF16)
        dwout_ref[...] += _mm_tn(mix, dyb)
        dmix = _mm_nt(dyb, wout_ref[...])

        def head_bwd(dyv, hat, r, n, sg, g, gate):
            dn = dyv * (gate * sg)
            dgate = dyv * n * (sg * (1.0 + gate * (1.0 - sg)))
            dhat = dn * g
            do = r * (dhat - hat * (_seg(dhat * hat, bd_ref[...]) * inv_hd))
            return do, dgate, _colsum(dn * hat)

        dsbo, dsbg, dg_sb = head_bwd(dmix[:, 0:512], sb_hat, sb_r, sb_n, sb_sg, gsb_v, sbg_v)
        dmlao, dmlag, dg_ml = head_bwd(dmix[:, 512:1024], ml_hat, ml_r, ml_n, ml_sg, gmla_v, mlag_v)
        dsbo_ref[...] = dsbo.astype(BF16)
        dmlao_ref[...] = dmlao.astype(BF16)
        delta_ref[...] = _seg(dmlao * mlao, bd_ref[...])
        dsbg_ref[...] = dsbg.astype(BF16)
        dmlag_ref[...] = dmlag.astype(BF16)
        vec_ref[pl.ds(0, 1), :] += _colsum(dx1 * y_hat)
        vec_ref[pl.ds(1, 1), :] += _colsum(dple * pl_hat)
        vec_ref[pl.ds(2, 1), :] += _colsum(du)
        vec_ref[pl.ds(3, 1), :] += jnp.concatenate([dg_sb, dg_ml], axis=1)
        vec_ref[pl.ds(4, 1), :] += jnp.full((1, D_MODEL), loss, F32)

    out_shape = (
        jax.ShapeDtypeStruct((s, 512), BF16), jax.ShapeDtypeStruct((s, 512), BF16), jax.ShapeDtypeStruct((s, 512), F32),
        jax.ShapeDtypeStruct((s, 512), BF16), jax.ShapeDtypeStruct((s, 512), BF16), jax.ShapeDtypeStruct((s, D_MODEL), F32),
        jax.ShapeDtypeStruct((D_MODEL, D_MODEL), F32), jax.ShapeDtypeStruct((D_MODEL, D_MODEL), F32),
        jax.ShapeDtypeStruct((PLE_DIM, D_MODEL), F32), jax.ShapeDtypeStruct((8, D_MODEL), F32),
    )
    return pl.pallas_call(
        body, name="post_fwd_bwd", grid=(s // TM,), out_shape=out_shape,
        in_specs=[_rows(D_MODEL), _rows(PLE_DIM), _rows(D_MODEL), _rows(512), _rows(512), _rows(512), _rows(512),
                  _full((1, 512)), _full((1, 512)), _full((D_MODEL, D_MODEL)),
                  _full((1, D_MODEL)), _full((PLE_DIM, D_MODEL)), _full((1, D_MODEL)), _full((D_MODEL, D_MODEL)),
                  _full((1, D_MODEL)), _full((1024, 512))],
        out_specs=(_rows(512), _rows(512), _rows(512), _rows(512), _rows(512), _rows(D_MODEL),
                   _acc((D_MODEL, D_MODEL)), _acc((D_MODEL, D_MODEL)), _acc((PLE_DIM, D_MODEL)), _acc((8, D_MODEL))),
        compiler_params=pltpu.CompilerParams(vmem_limit_bytes=VMEM_DENSE),
    )(x, p, tgt, sbo, mlao, sbg, mlag, gsb, gmla, wout, gpost, wple, gple, wpg, bpg, _blockdiag2(512, HEAD_DIM))


def _pre_bwd(x, dxres, dsbq, dsbk, dsbv, dsbg, dmlag, dqc, dkc, dmv, cq, ckv, tabs, gpre, win, gq, wuq, gkv, wk, wv):
    s = x.shape[0]
    c_t, sa_t, sb_t = tabs
    rw = _rows

    def body(x_ref, dxres_ref, dsbq_ref, dsbk_ref, dsbv_ref, dsbg_ref, dmlag_ref, dqc_ref, dkc_ref, dmv_ref, cq_ref,
             ckv_ref, c_ref, sa_ref, sb_ref, gpre_ref, win_ref, gq_ref, wuq_ref, gkv_ref, wk_ref, wv_ref,
             gx_ref, dwin_ref, dwuq_ref, dwk_ref, dwv_ref, vec_ref, dwin_acc):
        i = pl.program_id(0)

        @pl.when(i == 0)
        def _():
            dwin_acc[...] = jnp.zeros_like(dwin_acc)
            dwuq_ref[...] = jnp.zeros_like(dwuq_ref)
            dwk_ref[...] = jnp.zeros_like(dwk_ref)
            dwv_ref[...] = jnp.zeros_like(dwv_ref)
            vec_ref[...] = jnp.zeros_like(vec_ref)

        lane = lax.broadcasted_iota(jnp.int32, (1, LANES), 1)
        c1, sa1, sb1 = c_ref[...], sa_ref[...], sb_ref[...]
        c8, sa8, sb8 = jnp.tile(c1, (1, 8)), jnp.tile(sa1, (1, 8)), jnp.tile(sb1, (1, 8))

        def norm_bwd(dn, hat, r, g):
            t = dn * g
            return r * (t - hat * _rowmean(t * hat)), _colsum(dn * hat)

        xv = x_ref[...]
        r1 = lax.rsqrt(_rowmean(xv * xv) + EPS)
        x_hat = xv * r1
        gpre_v = gpre_ref[...]
        hb = (x_hat * gpre_v).astype(BF16)
        ready = jnp.concatenate([dsbq_ref[...], dsbk_ref[...].astype(BF16), dsbv_ref[...].astype(BF16), dsbg_ref[...]], axis=1)
        dmlag = dmlag_ref[...]
        dwin_acc[:, 0:2048] += _mm_tn(hb, ready)
        dwin_acc[:, 2560:3072] += _mm_tn(hb, dmlag)
        dh = _mm_nt(ready, win_ref[:, 0:2048]) + _mm_nt(dmlag, win_ref[:, 2560:3072])

        dqeb = _rope_bwd(dqc_ref[...], c8, sa8, sb8).astype(BF16)
        cq = cq_ref[...]
        rq = lax.rsqrt(_rowmean(cq * cq) + EPS)
        cq_hat = cq * rq
        gq_v = gq_ref[...]
        dwuq_ref[...] += _mm_tn((cq_hat * gq_v).astype(BF16), dqeb)
        dcq, dg_q = norm_bwd(_mm_nt(dqeb, wuq_ref[...]), cq_hat, rq, gq_v)

        dkc = dkc_ref[...]
        dkcb = dkc.astype(BF16)
        dmvb = dmv_ref[...].astype(BF16)
        ckv = ckv_ref[...]
        rkv = lax.rsqrt(_rowmean(ckv * ckv) + EPS)
        ckv_hat = ckv * rkv
        gkv_v = gkv_ref[...]
        ckvnb = (ckv_hat * gkv_v).astype(BF16)
        dwk_ref[...] += _mm_tn(ckvnb, dkcb)
        dwv_ref[...] += _mm_tn(ckvnb, dmvb)
        dckv, dg_kv = norm_bwd(_mm_nt(dkcb, wk_ref[...]) + _mm_nt(dmvb, wv_ref[...]), ckv_hat, rkv, gkv_v)

        dkr = dkc[:, 0:LANES]
        for hh in range(1, 8):
            dkr = dkr + dkc[:, LANES * hh:LANES * (hh + 1)]
        dkr = _rope_bwd(dkr, c1, sa1, sb1)
        dkr = jnp.where((lane >= 64) & (lane < 96), dkr, 0.0)

        late = jnp.concatenate([dcq.astype(BF16), dckv.astype(BF16), dkr.astype(BF16)], axis=1)
        dwin_acc[:, 2048:2560] += _mm_tn(hb, late)
        dx, dg_pre = norm_bwd(dh + _mm_nt(late, win_ref[:, 2048:2560]), x_hat, r1, gpre_v)
        gx_ref[...] = dxres_ref[...] + dx
        vec_ref[pl.ds(0, 1), :] += dg_pre
        vec_ref[pl.ds(1, 1), :] += jnp.concatenate([dg_q, dg_kv, jnp.zeros((1, D_MODEL - Q_LORA - KV_LORA), F32)], axis=1)

        @pl.when(i == pl.num_programs(0) - 1)
        def _():
            pltpu.sync_copy(dwin_acc, dwin_ref)

    out_shape = (
        jax.ShapeDtypeStruct((s, D_MODEL), F32), jax.ShapeDtypeStruct((D_MODEL, D_EXT), F32),
        jax.ShapeDtypeStruct((Q_LORA, 1024), F32), jax.ShapeDtypeStruct((KV_LORA, 1024), F32),
        jax.ShapeDtypeStruct((KV_LORA, 512), F32), jax.ShapeDtypeStruct((8, D_MODEL), F32),
    )
    return pl.pallas_call(
        body, name="pre_bwd", grid=(s // TM,), out_shape=out_shape,
        in_specs=[rw(D_MODEL), rw(D_MODEL), rw(512), rw(512), rw(512), rw(512), rw(512),
                  rw(1024), rw(1024), rw(512), rw(Q_LORA), rw(KV_LORA), rw(LANES), rw(LANES),
                  rw(LANES), _full((1, D_MODEL)), _full((D_MODEL, D_EXT)), _full((1, Q_LORA)), _full((Q_LORA, 1024)),
                  _full((1, KV_LORA)), _full((KV_LORA, 1024)), _full((KV_LORA, 512))],
        out_specs=(rw(D_MODEL), pl.BlockSpec(memory_space=pl.ANY), _acc((Q_LORA, 1024)), _acc((KV_LORA, 1024)),
                   _acc((KV_LORA, 512)), _acc((8, D_MODEL))),
        scratch_shapes=[pltpu.VMEM((D_MODEL, D_EXT), F32)],
        compiler_params=pltpu.CompilerParams(vmem_limit_bytes=VMEM_DENSE),
    )(x, dxres, dsbq, dsbk, dsbv, dsbg, dmlag, dqc, dkc, dmv, cq, ckv, c_t, sa_t, sb_t, gpre, win, gq, wuq, gkv, wk, wv)


def _place():
    return lax.axis_index("x"), lax.axis_index("y"), lax.axis_index("c")


def _gather_steps(shapes, ins, bufs, send_sems, recv_sems):
    n = len(shapes)
    x, y, c = _place()
    me, sib = (x, y, c), (x, y, 1 - c)
    chips = [(1 - x, y), (x, 1 - y), (1 - x, 1 - y)]

    def half(t, chip, hc):
        rows = shapes[t][0] // 2
        return bufs[t].at[2 * chip[0] + chip[1], pl.ds(pl.multiple_of(hc * rows, 16), rows), :]

    def copy(k, t, chip, hc, to):
        return pltpu.make_async_remote_copy(src_ref=half(t, chip, hc), dst_ref=half(t, chip, hc), send_sem=send_sems.at[k],
                                            recv_sem=recv_sems.at[k], device_id=to, device_id_type=MESH)

    def start():
        for t in range(n):
            bufs[t][2 * x + y] = ins[t][...].astype(BF16)
            for j, chip in enumerate(chips):
                copy(6 * t + j, t, (x, y), c, (*chip, c)).start()

    def forward():
        for t in range(n):
            for j, chip in enumerate(chips):
                copy(6 * t + j, t, chip, c, me).wait_recv()
                copy(6 * t + 3 + j, t, chip, c, sib).start()

    def finish():
        for t in range(n):
            for j, chip in enumerate(chips):
                copy(6 * t + 3 + j, t, chip, 1 - c, me).wait_recv()
        for t in range(n):
            for j, chip in enumerate(chips):
                copy(6 * t + j, t, (x, y), c, (*chip, c)).wait_send()
                copy(6 * t + 3 + j, t, chip, c, sib).wait_send()

    return start, forward, finish


def _allgather_weights(shards):
    n = len(shards)

    def body(*refs):
        start, forward, finish = _gather_steps([a.shape for a in shards], refs[:n], refs[n:2 * n], refs[2 * n], refs[2 * n + 1])
        start()
        forward()
        finish()

    return pl.pallas_call(
        body, name="allgather_weights",
        out_shape=tuple(jax.ShapeDtypeStruct((N_SHARD,) + a.shape, BF16) for a in shards),
        in_specs=[pl.BlockSpec(memory_space=pltpu.VMEM)] * n, out_specs=(pl.BlockSpec(memory_space=pltpu.VMEM),) * n,
        scratch_shapes=[pltpu.SemaphoreType.DMA((6 * n,)), pltpu.SemaphoreType.DMA((6 * n,))],
        compiler_params=pltpu.CompilerParams(vmem_limit_bytes=VMEM_ATTN),
    )(*shards)


def _reduce_scratch(gsh):
    n = len(gsh)
    half_shapes = [(N_SHARD, a.shape[1] // 2, a.shape[2]) for a in gsh]
    return ([pltpu.VMEM(s_, F32) for s_ in half_shapes] * 2 + [pltpu.VMEM(s_, BF16) for s_ in half_shapes] * 2
            + [pltpu.SemaphoreType.DMA((n,)), pltpu.SemaphoreType.DMA((5 * n,)), pltpu.SemaphoreType.DMA((5 * n,))])


def _reduce_steps(halves, g_refs, f_refs, scratch):
    n = len(halves)
    accs, sibs, sbufs, rbufs = scratch[0:n], scratch[n:2 * n], scratch[2 * n:3 * n], scratch[3 * n:4 * n]
    local_sems, send_sems, recv_sems = scratch[4 * n:4 * n + 3]
    x, y, c = _place()
    me, sib = (x, y, c), (x, y, 1 - c)
    mine = 2 * x + y
    chips = [(1 - x, y), (x, 1 - y), (1 - x, 1 - y)]

    def remote(k, src, dst, to):
        return pltpu.make_async_remote_copy(src_ref=src, dst_ref=dst, send_sem=send_sems.at[k], recv_sem=recv_sems.at[k],
                                            device_id=to, device_id_type=MESH)

    def half3(ref, t, hc):
        return ref.at[:, pl.ds(pl.multiple_of(hc * halves[t], 8), halves[t]), :]

    def half2(ref, t, hc):
        return ref.at[pl.ds(pl.multiple_of(hc * halves[t], 8), halves[t]), :]

    def mine_load(t):
        return pltpu.make_async_copy(half3(g_refs[t], t, c), accs[t], local_sems.at[t])

    def to_sibling(t, to):
        return remote(t, half3(g_refs[t], t, 1 - c), sibs[t], to)

    def to_chip(t, j, chip, to):
        idx = 2 * chip[0] + chip[1]
        return remote(n + 3 * t + j, sbufs[t].at[idx], rbufs[t].at[mine if to is not me else idx], to)

    def swap(t, hc, to):
        return remote(4 * n + t, half2(f_refs[t], t, hc), half2(f_refs[t], t, hc), to)

    def load():
        for t in range(n):
            mine_load(t).start()
            to_sibling(t, sib).start()

    def partial():
        for t in range(n):
            mine_load(t).wait()
            to_sibling(t, me).wait_recv()
            for k in range(N_SHARD):
                accs[t][k] = accs[t][k] + sibs[t][k]
            for j, chip in enumerate(chips):
                idx = 2 * chip[0] + chip[1]
                sbufs[t][idx] = accs[t][idx].astype(BF16)
                to_chip(t, j, chip, (*chip, c)).start()

    def total():
        for t in range(n):
            acc = accs[t][mine]
            for j, chip in enumerate(chips):
                to_chip(t, j, chip, me).wait_recv()
                acc = acc + rbufs[t][2 * chip[0] + chip[1]].astype(F32)
            half2(f_refs[t], t, c)[...] = acc
            swap(t, c, sib).start()

    def finish():
        for t in range(n):
            swap(t, 1 - c, me).wait_recv()
        for t in range(n):
            to_sibling(t, sib).wait_send()
            for j, chip in enumerate(chips):
                to_chip(t, j, chip, (*chip, c)).wait_send()
            swap(t, c, sib).wait_send()

    return load, partial, total, finish


def _reduce_scatter_grads(gsh, vec):
    n = len(gsh)
    halves = [a.shape[1] // 2 for a in gsh]

    def body(*refs):
        g_refs, vec_ref, f_refs, vsum_ref = refs[:n], refs[n], refs[n + 1:2 * n + 1], refs[2 * n + 1]
        scratch = refs[2 * n + 2:]
        vrecv, vsend_sems, vrecv_sems = scratch[4 * n + 3:]
        load, partial, total, finish = _reduce_steps(halves, g_refs, f_refs, scratch)
        x, y, c = _place()
        my_dev = 4 * x + 2 * y + c

        def flip(k):
            return x ^ ((k >> 2) & 1), y ^ ((k >> 1) & 1), c ^ (k & 1)

        def vcopy(k, slot, to):
            return pltpu.make_async_remote_copy(src_ref=vec_ref, dst_ref=vrecv.at[slot], send_sem=vsend_sems.at[k - 1],
                                                recv_sem=vrecv_sems.at[k - 1], device_id=to, device_id_type=MESH)

        load()
        vrecv[my_dev] = vec_ref[...]
        for k in range(1, 8):
            vcopy(k, my_dev, flip(k)).start()
        partial()
        total()
        finish()
        for k in range(1, 8):
            fx, fy, fc = flip(k)
            vcopy(k, 4 * fx + 2 * fy + fc, (x, y, c)).wait_recv()
        vs = vrecv[0]
        for d in range(1, 8):
            vs = vs + vrecv[d]
        vsum_ref[...] = vs
        for k in range(1, 8):
            vcopy(k, my_dev, flip(k)).wait_send()

    return pl.pallas_call(
        body, name="reduce_scatter_grads",
        out_shape=tuple(jax.ShapeDtypeStruct(a.shape[1:], F32) for a in gsh) + (jax.ShapeDtypeStruct((VEC_ROWS, 1024), F32),),
        in_specs=[pl.BlockSpec(memory_space=pl.ANY)] * n + [pl.BlockSpec(memory_space=pltpu.VMEM)],
        out_specs=(pl.BlockSpec(memory_space=pltpu.VMEM),) * (n + 1),
        scratch_shapes=_reduce_scratch(gsh) + [pltpu.VMEM((8, VEC_ROWS, 1024), F32), pltpu.SemaphoreType.DMA((7,)),
                                               pltpu.SemaphoreType.DMA((7,))],
        compiler_params=pltpu.CompilerParams(vmem_limit_bytes=56 * 1024 * 1024),
    )(*gsh, vec)


def _adamw(w, g, m, v):
    rows, cols = w.shape
    tr = rows if rows <= 256 else 256
    flip = cols % LANES != 0

    def body(w_ref, g_ref, m_ref, v_ref, g_out, d_ref, nm_ref, nv_ref):
        gv = g_ref[...].T if flip else g_ref[...]
        outs = (gv,) + _adam_math(w_ref[...], gv, m_ref[...], v_ref[...])
        for ref, val in zip((g_out, d_ref, nm_ref, nv_ref), outs):
            ref[...] = val

    spec = pl.BlockSpec((tr, cols), lambda i: (i, 0))
    tspec = pl.BlockSpec((cols, tr), lambda i: (0, i)) if flip else spec
    shp = jax.ShapeDtypeStruct((cols, rows) if flip else (rows, cols), F32)
    if flip:
        w, m, v = w.T, m.T, v.T
    outs = pl.pallas_call(body, name="adamw", grid=(rows // tr,), out_shape=(shp,) * 4,
                          in_specs=[tspec, spec, tspec, tspec], out_specs=(tspec,) * 4)(w, g, m, v)
    return tuple(o.T for o in outs) if flip else outs


def _adam_math(w, g, m, v):
    m2 = ADAM_B1 * m + (1.0 - ADAM_B1) * g
    v2 = ADAM_B2 * v + (1.0 - ADAM_B2) * (g * g)
    m_hat = m2 / (1.0 - ADAM_B1 ** ADAM_STEP)
    v_hat = v2 / (1.0 - ADAM_B2 ** ADAM_STEP)
    return -ADAM_LR * (m_hat / (jnp.sqrt(v_hat) + ADAM_EPS) + ADAM_WD * w), m2, v2


def _adamw_small(vsum, w, m, v):
    names = [name for name, _, _, _ in _VEC_LAYOUT]
    k = len(names)

    def body(*refs):
        vs_ref, w_refs, m_refs, v_refs = refs[0], refs[1:1 + k], refs[1 + k:1 + 2 * k], refs[1 + 2 * k:1 + 3 * k]
        outs = refs[1 + 3 * k:]
        for idx, (_, r, c0, width) in enumerate(_VEC_LAYOUT):
            gv = vs_ref[pl.ds(r, 1), pl.ds(c0, width)]
            d, m2, v2 = _adam_math(w_refs[idx][...], gv, m_refs[idx][...], v_refs[idx][...])
            outs[idx][...], outs[k + idx][...], outs[2 * k + idx][...], outs[3 * k + idx][...] = gv, d, m2, v2

    shapes = tuple(jax.ShapeDtypeStruct(w[name].shape, F32) for name in names)
    res = pl.pallas_call(
        body, name="adamw_small", out_shape=shapes * 4,
        in_specs=[pl.BlockSpec(memory_space=pltpu.VMEM)] * (1 + 3 * k), out_specs=(pl.BlockSpec(memory_space=pltpu.VMEM),) * (4 * k),
    )(vsum, *[w[name] for name in names], *[m[name] for name in names], *[v[name] for name in names])
    return tuple({name: res[part * k + idx] for idx, name in enumerate(names)} for part in range(4))


_EARLY = ("w_in", "w_uq", "w_ukv")
_LATE = ("w_out", "w_ple", "w_ple_gate")
_BIG = _EARLY + _LATE
_KR_LOCAL = 2432 - 3 * (D_IN // N_SHARD)


def _extend_early(parts):
    cols = lambda a: a.transpose(1, 0, 2).reshape(a.shape[1], N_SHARD * a.shape[2])
    g = parts["w_in"]
    zeros = lambda n: jnp.zeros((D_MODEL, n), g.dtype)
    win_ext = jnp.concatenate([g[0], g[1], g[2], g[3][:, :_KR_LOCAL], zeros(64), g[3][:, _KR_LOCAL:_KR_LOCAL + QK_ROPE],
                               zeros(32), g[3][:, _KR_LOCAL + QK_ROPE:]], axis=1)
    wuq_ext = jnp.pad(cols(parts["w_uq"]).reshape(Q_LORA, 8, 96), ((0, 0), (0, 0), (0, 32))).reshape(Q_LORA, 1024)
    wukv = cols(parts["w_ukv"]).reshape(KV_LORA, 8, 128)
    wk_ext = jnp.pad(wukv[:, :, :64], ((0, 0), (0, 0), (0, 64))).reshape(KV_LORA, 1024)
    wv = wukv[:, :, 64:].reshape(KV_LORA, 512)
    return win_ext, wuq_ext, wk_ext, wv


def _shard_cols(a):
    return a.reshape(a.shape[0], N_SHARD, a.shape[1] // N_SHARD).transpose(1, 0, 2)


def _shard_rows(a):
    return a.reshape(N_SHARD, a.shape[0] // N_SHARD, a.shape[1])


def _shard_early_grads(dwin_ext, dwuq_ext, dwk_ext, dwv):
    e, w = dwin_ext, D_IN // N_SHARD
    last = jnp.concatenate([e[:, 3 * w:2432], e[:, 2496:2528], e[:, 2560:]], axis=1)
    dwuq = dwuq_ext.reshape(Q_LORA, 8, 128)[:, :, :96].reshape(Q_LORA, 768)
    dwukv = jnp.concatenate([dwk_ext.reshape(KV_LORA, 8, 128)[:, :, :64], dwv.reshape(KV_LORA, 8, 64)], axis=2)
    return [jnp.stack([e[:, 0:w], e[:, w:2 * w], e[:, 2 * w:3 * w], last]), _shard_cols(dwuq),
            _shard_cols(dwukv.reshape(KV_LORA, 1024))]


def _rope_tables(positions):
    half = QK_ROPE // 2
    freq = ROPE_THETA ** (-jnp.arange(half, dtype=F32) / half)
    s = positions.shape[0]
    per = LANES // half
    ang = jnp.repeat(positions.astype(F32).reshape(s // per, per), half, axis=1) * jnp.tile(freq, per)
    cos, sin = lax.optimization_barrier((jnp.cos(ang), jnp.sin(ang)))
    cos, sin = cos.reshape(s, half), sin.reshape(s, half)
    z = lambda n: jnp.zeros((s, n), F32)
    c_t = jnp.concatenate([jnp.ones((s, 64), F32), cos, cos, z(32)], axis=1)
    sa_t = jnp.concatenate([z(64), -sin, z(16), z(32)], axis=1)
    sb_t = jnp.concatenate([z(64), z(16), sin, z(32)], axis=1)
    return c_t, sa_t, sb_t


def _local_grads(x, p, positions, tgt, gains, early, late):
    win_ext, wuq_ext, wk_ext, wv = _extend_early(early)
    tabs = _rope_tables(positions)
    g = gains
    sbq, sbk, sbv, sbg, mlag, cq, ckv, qc, kc, mv, sbkt, sbvt, kct, mvt = _pre_fwd(
        x, tabs, g["norm_pre_g"], win_ext, g["q_norm_g"], wuq_ext, g["kv_norm_g"], wk_ext, wv)
    sbo, wout4, wple4, wpg4 = _sb_fwd(sbq, sbk, sbvt, late)
    wout, wpg = wout4.reshape(D_MODEL, D_MODEL), wpg4.reshape(D_MODEL, D_MODEL)
    wple = wple4.transpose(1, 0, 2).reshape(PLE_DIM, D_MODEL)
    mlao, lse = _mla_fwd(qc, kc, mvt)
    dsbo, dmlao, delta, dsbg, dmlag, dxres, dwout, dwpg, dwple, vec_c = _post(
        x, p, tgt, sbo, mlao, sbg, mlag, g["sb_out_norm_g"], g["mla_out_norm_g"], wout, g["norm_post_g"], wple,
        g["ple_norm_g"], wpg, g["b_ple_gate"])
    dsbq, dsbk, dsbv, *late_grads = _sb_bwd(sbq, sbk, sbkt, sbv, dsbo, [_shard_rows(dwout), _shard_cols(dwple), _shard_rows(dwpg)])
    dqc, dkc, dmv = _mla_bwd(qc, kc, kct, mv, dmlao, lse, delta)
    gx, dwin_ext, dwuq_ext, dwk_ext, dwv, vec_d = _pre_bwd(
        x, dxres, dsbq, dsbk, dsbv, dsbg, dmlag, dqc, dkc, dmv, cq, ckv, tabs, g["norm_pre_g"], win_ext, g["q_norm_g"],
        wuq_ext, g["kv_norm_g"], wk_ext, wv)
    return gx, _shard_early_grads(dwin_ext, dwuq_ext, dwk_ext, dwv), late_grads, jnp.concatenate([vec_c, vec_d], axis=0)


_VEC_LAYOUT = (("norm_post_g", 0, 0, 1024), ("ple_norm_g", 1, 0, 1024), ("b_ple_gate", 2, 0, 1024), ("sb_out_norm_g", 3, 0, 512),
               ("mla_out_norm_g", 3, 512, 512), ("norm_pre_g", 8, 0, 1024), ("q_norm_g", 9, 0, 256), ("kv_norm_g", 9, 256, 128))
_LOSS_ROW = 4
_WEIGHT_ORDER = ("norm_pre_g", "w_in", "q_norm_g", "w_uq", "kv_norm_g", "w_ukv", "sb_out_norm_g", "mla_out_norm_g", "w_out",
                 "norm_post_g", "w_ple", "ple_norm_g", "w_ple_gate", "b_ple_gate")


def kernel(x, p, positions, norm_pre_g, w_in, q_norm_g, w_uq, kv_norm_g, w_ukv, sb_out_norm_g, mla_out_norm_g, w_out, norm_post_g, w_ple, ple_norm_g, w_ple_gate, b_ple_gate, loss_target, m_norm_pre_g, m_w_in, m_q_norm_g, m_w_uq, m_kv_norm_g, m_w_ukv, m_sb_out_norm_g, m_mla_out_norm_g, m_w_out, m_norm_post_g, m_w_ple, m_ple_norm_g, m_w_ple_gate, m_b_ple_gate, v_norm_pre_g, v_w_in, v_q_norm_g, v_w_uq, v_kv_norm_g, v_w_ukv, v_sb_out_norm_g, v_mla_out_norm_g, v_w_out, v_norm_post_g, v_w_ple, v_ple_norm_g, v_w_ple_gate, v_b_ple_gate):
    w = {"norm_pre_g": norm_pre_g, "w_in": w_in[0], "q_norm_g": q_norm_g, "w_uq": w_uq[0], "kv_norm_g": kv_norm_g, "w_ukv": w_ukv[0],
         "sb_out_norm_g": sb_out_norm_g, "mla_out_norm_g": mla_out_norm_g, "w_out": w_out[0], "norm_post_g": norm_post_g,
         "w_ple": w_ple[0], "ple_norm_g": ple_norm_g, "w_ple_gate": w_ple_gate[0], "b_ple_gate": b_ple_gate}
    m = {"norm_pre_g": m_norm_pre_g, "w_in": m_w_in[0], "q_norm_g": m_q_norm_g, "w_uq": m_w_uq[0], "kv_norm_g": m_kv_norm_g,
         "w_ukv": m_w_ukv[0], "sb_out_norm_g": m_sb_out_norm_g, "mla_out_norm_g": m_mla_out_norm_g, "w_out": m_w_out[0],
         "norm_post_g": m_norm_post_g, "w_ple": m_w_ple[0], "ple_norm_g": m_ple_norm_g, "w_ple_gate": m_w_ple_gate[0],
         "b_ple_gate": m_b_ple_gate}
    v = {"norm_pre_g": v_norm_pre_g, "w_in": v_w_in[0], "q_norm_g": v_q_norm_g, "w_uq": v_w_uq[0], "kv_norm_g": v_kv_norm_g,
         "w_ukv": v_w_ukv[0], "sb_out_norm_g": v_sb_out_norm_g, "mla_out_norm_g": v_mla_out_norm_g, "w_out": v_w_out[0],
         "norm_post_g": v_norm_post_g, "w_ple": v_w_ple[0], "ple_norm_g": v_ple_norm_g, "w_ple_gate": v_w_ple_gate[0],
         "b_ple_gate": v_b_ple_gate}
    gathered = _allgather_weights([w[n] for n in _EARLY])
    gx, early_grads, late_red, vec = _local_grads(x[0], p[0, 0], positions[0], loss_target[0], w, dict(zip(_EARLY, gathered)),
                                                  [w[n] for n in _LATE])
    *early_red, vsum = _reduce_scatter_grads(early_grads, vec)
    gred = early_red + late_red
    loss = vsum[_LOSS_ROW, 0]

    g, delta, new_m, new_v = _adamw_small(vsum, w, m, v)
    for n, gn in zip(_BIG, gred):
        g[n], delta[n], new_m[n], new_v[n] = _adamw(w[n], gn, m[n], v[n])

    lead = lambda n, a: a[None] if n in _BIG else a
    return (loss, gx[None],
            *[lead(n, g[n]) for n in _WEIGHT_ORDER], *[lead(n, delta[n]) for n in _WEIGHT_ORDER],
            *[lead(n, new_m[n]) for n in _WEIGHT_ORDER], *[lead(n, new_v[n]) for n in _WEIGHT_ORDER])
```

```python
import numpy as np
import jax
import jax.numpy as jnp
from jax import lax
from jax.experimental import pallas as pl
from jax.experimental.pallas import tpu as pltpu

F32 = jnp.float32
BF16 = jnp.bfloat16
MESH = pl.DeviceIdType.MESH

D_MODEL = 1024
HEAD_DIM = 64
D_SB = 512
D_MLA = 512
Q_LORA = 256
KV_LORA = 128
QK_NOPE = 64
QK_ROPE = 32
PLE_DIM = 256
D_IN = 2976
D_EXT = 3072
ROPE_THETA = 10000.0
EPS = 1e-6
N_SHARD = 4

ADAM_LR = 0.001
ADAM_B1 = 0.9
ADAM_B2 = 0.999
ADAM_EPS = 1e-08
ADAM_WD = 0.01
ADAM_STEP = 10

LANES = 128
BK = 128
WQ = 256
MQ_FWD = 4096
MQ_BWD = 1024
MLA_CW = 256
SB_CUTOFF = 120.0
TM = 256
TM_PRE = 256
VEC_ROWS = 16
VMEM_DENSE = 52 * 1024 * 1024
VMEM_ATTN = 40 * 1024 * 1024


def _mm(a, b):
    return jnp.dot(a, b, preferred_element_type=F32)


def _mm_nt(a, b):
    return lax.dot_general(a, b, (((1,), (1,)), ((), ())), preferred_element_type=F32)


def _mm_tn(a, b):
    return lax.dot_general(a, b, (((0,), (0,)), ((), ())), preferred_element_type=F32)


def _seg(a, bd2):
    return _mm(_split2(a), bd2)


def _const(mask):
    return jnp.asarray(np.asarray(mask, np.float32), dtype=BF16)


def _blockdiag2(n, seg):
    r = (np.arange(2 * n)[:, None] % n) // seg
    c = np.arange(n)[None, :] // seg
    return _const(r == c)


def _sigmoid(a):
    return 1.0 / (1.0 + jnp.exp(-a))


def _rowmean(a):
    return jnp.mean(a, axis=-1, keepdims=True)


def _colsum(a):
    return jnp.sum(a, axis=0, keepdims=True)


def _rope_fwd(a, c, sa, sb):
    w = a.shape[-1]
    return a * c + pltpu.roll(a, w - 16, 1) * sa + pltpu.roll(a, 16, 1) * sb


def _rope_bwd(g, c, sa, sb):
    w = g.shape[-1]
    return g * c + pltpu.roll(g * sa, 16, 1) + pltpu.roll(g * sb, w - 16, 1)


def _full(shape):
    return pl.BlockSpec(shape, lambda *_: (0,) * len(shape))


def _acc(shape):
    return pl.BlockSpec(shape, lambda *_: (0,) * len(shape))


def _full2(shape):
    return pl.BlockSpec(shape, lambda p, i: (0, 0))


def _cols(height, tm=TM):
    return pl.BlockSpec((height, tm), lambda i: (0, i))


def _rows(width, tm=TM):
    return pl.BlockSpec((tm, width), lambda i: (i, 0))


def _pre_fwd(x, tabs, gpre, win, gq, wuq, gkv, wk, wv):
    s = x.shape[0]
    c_t, sa_t, sb_t = tabs
    rw, cl = (lambda width: _rows(width, TM_PRE)), (lambda height: _cols(height, TM_PRE))

    def body(x_ref, c_ref, sa_ref, sb_ref, gpre_ref, win_ref, gq_ref, wuq_ref, gkv_ref, wk_ref, wv_ref,
             sbq_ref, sbk_ref, sbv_ref, sbg_ref, mlag_ref, cq_ref, ckv_ref, qc_ref, kc_ref, mv_ref,
             sbkt_ref, sbvt_ref, kct_ref, mvt_ref):
        xv = x_ref[...]
        r1 = lax.rsqrt(_rowmean(xv * xv) + EPS)
        h = (xv * r1 * gpre_ref[...]).astype(BF16)
        proj = _mm(h, win_ref[...])
        sbq_ref[...] = proj[:, 0:512].astype(BF16)
        sbk_ref[...] = proj[:, 512:1024].astype(BF16)
        sbv_ref[...] = proj[:, 1024:1536].astype(BF16)
        sbkt_ref[...] = proj[:, 512:1024].T.astype(BF16)
        sbvt_ref[...] = proj[:, 1024:1536].T.astype(BF16)
        sbg_ref[...] = proj[:, 1536:2048]
        cq = proj[:, 2048:2304]
        ckv = proj[:, 2304:2432]
        kr = proj[:, 2432:2560]
        mlag_ref[...] = proj[:, 2560:3072]
        cq_ref[...] = cq
        ckv_ref[...] = ckv
        c1, sa1, sb1 = c_ref[...], sa_ref[...], sb_ref[...]
        c8, sa8, sb8 = jnp.tile(c1, (1, 8)), jnp.tile(sa1, (1, 8)), jnp.tile(sb1, (1, 8))
        cqn = (cq * lax.rsqrt(_rowmean(cq * cq) + EPS) * gq_ref[...]).astype(BF16)
        qe = _mm(cqn, wuq_ref[...])
        qc_ref[...] = _rope_fwd(qe, c8, sa8, sb8).astype(BF16)
        ckvn = (ckv * lax.rsqrt(_rowmean(ckv * ckv) + EPS) * gkv_ref[...]).astype(BF16)
        ke = _mm(ckvn, wk_ref[...])
        krr = _rope_fwd(kr, c1, sa1, sb1)
        kcat = ke + jnp.tile(krr, (1, 8))
        kc_ref[...] = kcat.astype(BF16)
        kct_ref[...] = kcat.T.astype(BF16)
        mval = _mm(ckvn, wv_ref[...])
        mv_ref[...] = mval.astype(BF16)
        mvt_ref[...] = mval.T.astype(BF16)

    out_shape = (
        jax.ShapeDtypeStruct((s, 512), BF16), jax.ShapeDtypeStruct((s, 512), BF16), jax.ShapeDtypeStruct((s, 512), BF16),
        jax.ShapeDtypeStruct((s, 512), F32), jax.ShapeDtypeStruct((s, 512), F32),
        jax.ShapeDtypeStruct((s, Q_LORA), F32), jax.ShapeDtypeStruct((s, KV_LORA), F32),
        jax.ShapeDtypeStruct((s, 1024), BF16), jax.ShapeDtypeStruct((s, 1024), BF16), jax.ShapeDtypeStruct((s, 512), BF16),
        jax.ShapeDtypeStruct((512, s), BF16), jax.ShapeDtypeStruct((512, s), BF16), jax.ShapeDtypeStruct((1024, s), BF16),
        jax.ShapeDtypeStruct((512, s), BF16),
    )
    return pl.pallas_call(
        body, name="pre_fwd", grid=(s // TM_PRE,), out_shape=out_shape,
        in_specs=[rw(D_MODEL), rw(LANES), rw(LANES), rw(LANES), _full((1, D_MODEL)), _full((D_MODEL, D_EXT)),
                  _full((1, Q_LORA)), _full((Q_LORA, 1024)), _full((1, KV_LORA)), _full((KV_LORA, 1024)), _full((KV_LORA, 512))],
        out_specs=(rw(512), rw(512), rw(512), rw(512), rw(512), rw(Q_LORA), rw(KV_LORA),
                   rw(1024), rw(1024), rw(512), cl(512), cl(512), cl(1024), cl(512)),
        compiler_params=pltpu.CompilerParams(vmem_limit_bytes=VMEM_DENSE),
    )(x, c_t, sa_t, sb_t, gpre, win, gq, wuq, gkv, wk, wv)


def _softplus(z):
    neg_abs = lax.bitcast_convert_type(lax.bitcast_convert_type(z, jnp.uint32) | jnp.uint32(0x80000000), F32)
    return jnp.maximum(z, 0.0) + jnp.log(1.0 + jnp.exp(neg_abs))


def _sum_matrix(kind, terms):
    r, c = np.arange(2 * BK)[:, None], np.arange(2 * BK * terms)[None, :] % (2 * BK)
    rk, ck = r % BK, c % BK
    return _const(((r // BK) == (c // BK)) & {"suffix": ck >= rk, "prefix": ck <= rk}[kind])


def _split_rows(a):
    hi = a.astype(BF16)
    return jnp.concatenate([hi, (a - hi.astype(F32)).astype(BF16)], axis=0)


def _heads_t(blk, rowi):
    zero = jnp.zeros_like(blk)
    return jnp.concatenate([jnp.where(rowi < 64, blk, zero), jnp.where(rowi >= 64, blk, zero)], axis=1)


def _mask_keys(a, valid, fill=0.0):
    return jnp.concatenate([jnp.where(valid, a[0:BK], fill), jnp.where(valid, a[BK:2 * BK], fill)], axis=0)


def _split2(a):
    hi = a.astype(BF16)
    lo = (a - hi.astype(F32)).astype(BF16)
    return jnp.concatenate([hi, lo], axis=1)


def _pair_stack(b, lane):
    zero = jnp.zeros_like(b)
    return jnp.concatenate([jnp.where(lane < 64, b, zero), jnp.where(lane >= 64, b, zero)], axis=0)


def _sb_fwd(q, k, vt, late):
    s = q.shape[0]
    n = len(late)

    def body(q_ref, k_ref, vt_ref, usuf_ref, *rest):
        ins, o_ref, outs = rest[:n], rest[n], rest[n + 1:2 * n + 1]
        acc_scr, run_scr = rest[2 * n + 1:2 * n + 3]
        bufs, (send_sems, recv_sems, out_sems) = rest[2 * n + 3:3 * n + 3], rest[3 * n + 3:]
        p, i = pl.program_id(0), pl.program_id(1)
        gather_start, gather_forward, gather_finish = _gather_steps([a.shape for a in late], ins, bufs, send_sems, recv_sems)

        @pl.when((p == 0) & (i == 0))
        def _():
            gather_start()

        @pl.when((p == 2) & (i == 0))
        def _():
            gather_forward()

        lane = lax.broadcasted_iota(jnp.int32, (1, LANES), 1)
        rowi = lax.broadcasted_iota(jnp.int32, (LANES, 1), 0)
        keyi = lax.broadcasted_iota(jnp.int32, (BK, WQ), 0)
        qryi = lax.broadcasted_iota(jnp.int32, (BK, WQ), 1) + i * WQ
        qs = q_ref[...] * (HEAD_DIM ** -0.5)

        def group(blocks, masked, seen=None):
            seen = seen or [0] * len(blocks)
            starts = [pl.multiple_of(j * BK, BK) for j in blocks]
            valid = [(keyi[:, lo:] + j * BK) < qryi[:, lo:] if m else None for j, m, lo in zip(blocks, masked, seen)]
            zs = [_mm_nt(_pair_stack(k_ref[pl.ds(ks, BK), :], lane), qs[lo:]) for ks, lo in zip(starts, seen)]
            sps = [_softplus(z) for z in zs]
            sps = [sp if ok is None else _mask_keys(sp, ok) for sp, ok in zip(sps, valid)]
            cums = [_mm(usuf_ref[...], _split_rows(sp)) for sp in sps]
            ws = [jnp.exp(z - c) for z, c in zip(zs, cums)]
            ws = [w if ok is None else _mask_keys(w, ok) for w, ok in zip(ws, valid)]
            pvs = [_mm(_heads_t(vt_ref[:, pl.ds(ks, BK)], rowi), w.astype(BF16)) for ks, w in zip(starts, ws)]
            for pv, c, lo in zip(pvs, cums, seen):
                r0, r1 = run_scr[0:1, lo:], run_scr[1:2, lo:]
                acc_scr[:, lo:] += jnp.where(rowi < 64, jnp.exp(-r0), jnp.exp(-r1)) * pv
                run_scr[0:1, lo:] = r0 + c[0:1]
                run_scr[1:2, lo:] = r1 + c[BK:BK + 1]

        assert WQ == 2 * BK
        acc_scr[...] = jnp.zeros_like(acc_scr)
        run_scr[...] = jnp.zeros_like(run_scr)

        @pl.when(i == 0)
        def _():
            group([1, 0], [True, True], [BK, 0])

        @pl.when(i > 0)
        def _():
            group([2 * i + 1, 2 * i, 2 * i - 1, 2 * i - 2], [True, True, False, False], [BK, 0, 0, 0])

        def unfinished():
            return (jnp.min(run_scr[0:2, :]) < SB_CUTOFF).astype(jnp.int32)

        def step(c):
            group([2 * i - 1 - 2 * c[0], 2 * i - 2 - 2 * c[0]], [False, False])
            return c[0] + 1, unfinished()

        lax.while_loop(lambda c: (c[0] < i) & (c[1] > 0), step, (jnp.int32(1), unfinished()))
        o_ref[...] = acc_scr[...].T

        @pl.when((p == pl.num_programs(0) - 1) & (i == pl.num_programs(1) - 1))
        def _():
            gather_finish()
            copies = [pltpu.make_async_copy(bufs[t], outs[t], out_sems.at[t]) for t in range(n)]
            for cp in copies:
                cp.start()
            for cp in copies:
                cp.wait()

    qspec = pl.BlockSpec((WQ, LANES), lambda p, i: (i, p))
    kspec = pl.BlockSpec((s, LANES), lambda p, i: (0, p))
    tspec = pl.BlockSpec((LANES, s), lambda p, i: (p, 0))
    gathered = [jax.ShapeDtypeStruct((N_SHARD,) + a.shape, BF16) for a in late]
    return pl.pallas_call(
        body, name="sb_fwd", grid=(4, s // WQ),
        out_shape=(jax.ShapeDtypeStruct((s, 512), F32), *gathered),
        in_specs=[qspec, kspec, tspec, _full2((2 * BK, 4 * BK))] + [_full2(a.shape) for a in late],
        out_specs=(qspec,) + (pl.BlockSpec(memory_space=pl.ANY),) * n,
        scratch_shapes=[pltpu.VMEM((LANES, WQ), F32), pltpu.VMEM((8, WQ), F32)] + [pltpu.VMEM(g.shape, BF16) for g in gathered]
                       + [pltpu.SemaphoreType.DMA((6 * n,)), pltpu.SemaphoreType.DMA((6 * n,)), pltpu.SemaphoreType.DMA((n,))],
        compiler_params=pltpu.CompilerParams(vmem_limit_bytes=VMEM_ATTN),
    )(q, k, vt, _sum_matrix("suffix", 2), *late)


def _sb_bwd(q, k, kt, v, do, late):
    s = q.shape[0]
    n = len(late)
    halves = [a.shape[1] // 2 for a in late]

    def body(q_ref, k_ref, kt_ref, v_ref, do_ref, usuf_ref, upre_ref, *rest):
        g_refs, (dq_ref, dk_ref, dv_ref), outs = rest[:n], rest[n:n + 3], rest[n + 3:2 * n + 3]
        later_scr, dqt_scr, st_scr = rest[2 * n + 3:2 * n + 6]
        f_scr, reduce_scr, out_sems = rest[2 * n + 6:3 * n + 6], rest[3 * n + 6:-1], rest[-1]
        p, i = pl.program_id(0), pl.program_id(1)
        reduce_load, reduce_partial, reduce_total, reduce_finish = _reduce_steps(halves, g_refs, f_scr, reduce_scr)

        @pl.when((p == 0) & (i == 0))
        def _():
            reduce_load()

        @pl.when((p == 1) & (i == 0))
        def _():
            reduce_partial()

        @pl.when((p == 3) & (i == 0))
        def _():
            reduce_total()

        @pl.when(i == 0)
        def _():
            dk_ref[...] = jnp.zeros_like(dk_ref)
            dv_ref[...] = jnp.zeros_like(dv_ref)

        lane = lax.broadcasted_iota(jnp.int32, (1, LANES), 1)
        rowi = lax.broadcasted_iota(jnp.int32, (LANES, 1), 0)
        keyi = lax.broadcasted_iota(jnp.int32, (BK, WQ), 0)
        qryi = lax.broadcasted_iota(jnp.int32, (BK, WQ), 1) + i * WQ
        qs = q_ref[...] * (HEAD_DIM ** -0.5)
        dob = do_ref[...]
        dot = dob.astype(F32).T.astype(BF16)

        def scores(j, lo=0):
            return _mm_nt(_pair_stack(k_ref[pl.ds(pl.multiple_of(j * BK, BK), BK), :], lane), qs[lo:])

        def scan(blocks, masked, seen=None):
            seen = seen or [0] * len(blocks)
            sps = [_softplus(scores(j, lo)) for j, lo in zip(blocks, seen)]
            sps = [_mask_keys(sp, (keyi[:, lo:] + j * BK) < qryi[:, lo:]) if m else sp
                   for sp, j, m, lo in zip(sps, blocks, masked, seen)]
            for sp, j, lo in zip(sps, blocks, seen):
                run = st_scr[0:2, :]
                later_scr[j, 0:2, :] = run
                st_scr[0:2, lo:] = run[:, lo:] + jnp.concatenate([jnp.sum(sp[0:BK], axis=0, keepdims=True),
                                                                  jnp.sum(sp[BK:2 * BK], axis=0, keepdims=True)], axis=0)

        def sweep(blocks, masked, seen=None):
            seen = seen or [0] * len(blocks)
            starts = [pl.multiple_of(j * BK, BK) for j in blocks]
            valid = [(keyi[:, lo:] + j * BK) < qryi[:, lo:] if m else None for j, m, lo in zip(blocks, masked, seen)]
            zs = [scores(j, lo) for j, lo in zip(blocks, seen)]
            us = [jnp.exp(lax.bitcast_convert_type(lax.bitcast_convert_type(z, jnp.uint32) | jnp.uint32(0x80000000), F32))
                  for z in zs]
            sps = [jnp.maximum(z, 0.0) + jnp.log(1.0 + u) for z, u in zip(zs, us)]
            sps = [sp if ok is None else _mask_keys(sp, ok) for sp, ok in zip(sps, valid)]
            sigs = [jnp.where(z >= 0.0, 1.0, u) / (1.0 + u) for z, u in zip(zs, us)]
            cums = [_mm(usuf_ref[...], _split_rows(sp)) for sp in sps]
            dws = [_mm(_pair_stack(v_ref[pl.ds(ks, BK), :], lane), dot[:, lo:]) for ks, lo in zip(starts, seen)]
            wfs = []
            for z, c, j, ok, lo in zip(zs, cums, blocks, valid, seen):
                f = jnp.exp(-later_scr[j, 0:2, lo:])
                wide = (BK, WQ - lo)
                wf = jnp.exp(z - c) * jnp.concatenate([jnp.broadcast_to(f[0:1], wide), jnp.broadcast_to(f[1:2], wide)], axis=0)
                wfs.append(wf if ok is None else _mask_keys(wf, ok))
            es = [dw * wf for dw, wf in zip(dws, wfs)]
            pres = [_mm(upre_ref[...], e.astype(BF16)) for e in es]
            dzs = []
            for e, pre, sig, ok, lo in zip(es, pres, sigs, valid, seen):
                e0 = pre[0:BK] + st_scr[0:1, lo:]
                e1 = pre[BK:2 * BK] + st_scr[1:2, lo:]
                st_scr[0:1, lo:] = e0[BK - 1:BK]
                st_scr[1:2, lo:] = e1[BK - 1:BK]
                dz = e - sig * jnp.concatenate([e0, e1], axis=0)
                dzs.append((dz if ok is None else _mask_keys(dz, ok)).astype(BF16))
            whole = [b for b, lo in enumerate(seen) if lo == 0]
            dqt_scr[...] += _mm(jnp.concatenate([_heads_t(kt_ref[:, pl.ds(starts[b], BK)], rowi) for b in whole], axis=1),
                                jnp.concatenate([dzs[b] for b in whole], axis=0))
            for b, lo in enumerate(seen):
                if lo:
                    dqt_scr[:, lo:] += _mm(_heads_t(kt_ref[:, pl.ds(starts[b], BK)], rowi), dzs[b])
            for ks, dz, wf, lo in zip(starts, dzs, wfs, seen):
                rk = _mm(dz, qs[lo:])
                dk_ref[pl.ds(ks, BK), :] += jnp.where(lane < 64, rk[0:BK], rk[BK:2 * BK])
                rv = _mm(wf.astype(BF16), dob[lo:])
                dv_ref[pl.ds(ks, BK), :] += jnp.where(lane < 64, rv[0:BK], rv[BK:2 * BK])

        assert WQ == 2 * BK
        st_scr[...] = jnp.zeros_like(st_scr)

        @pl.when(i == 0)
        def _():
            scan([1, 0], [True, True], [BK, 0])

        @pl.when(i > 0)
        def _():
            scan([2 * i + 1, 2 * i, 2 * i - 1, 2 * i - 2], [True, True, False, False], [BK, 0, 0, 0])

        def unfinished():
            return (jnp.min(st_scr[0:2, :]) < SB_CUTOFF).astype(jnp.int32)

        def step(c):
            scan([2 * i - 1 - 2 * c[0], 2 * i - 2 - 2 * c[0]], [False, False])
            return c[0] + 1, unfinished()

        npairs, _ = lax.while_loop(lambda c: (c[0] < i) & (c[1] > 0), step, (jnp.minimum(i, 1), unfinished()))

        st_scr[...] = jnp.zeros_like(st_scr)
        dqt_scr[...] = jnp.zeros_like(dqt_scr)
        first = 2 * (i - npairs)

        def early(t, carry):
            sweep([first + 2 * t, first + 2 * t + 1], [False, False])
            return carry

        lax.fori_loop(0, npairs - 1, early, 0)

        @pl.when(i == 0)
        def _():
            sweep([0, 1], [True, True], [0, BK])

        @pl.when(i > 0)
        def _():
            sweep([2 * i - 2, 2 * i - 1, 2 * i, 2 * i + 1], [False, False, True, True], [0, 0, 0, BK])

        dq_ref[...] = (dqt_scr[...].T * (HEAD_DIM ** -0.5)).astype(BF16)

        @pl.when((p == pl.num_programs(0) - 1) & (i == pl.num_programs(1) - 1))
        def _():
            reduce_finish()
            copies = [pltpu.make_async_copy(f_scr[t], outs[t], out_sems.at[t]) for t in range(n)]
            for cp in copies:
                cp.start()
            for cp in copies:
                cp.wait()

    qspec = pl.BlockSpec((WQ, LANES), lambda p, i: (i, p))
    kspec = pl.BlockSpec((s, LANES), lambda p, i: (0, p))
    tspec = pl.BlockSpec((LANES, s), lambda p, i: (p, 0))
    anywhere = pl.BlockSpec(memory_space=pl.ANY)
    reduced = [jax.ShapeDtypeStruct(a.shape[1:], F32) for a in late]
    return pl.pallas_call(
        body, name="sb_bwd", grid=(4, s // WQ),
        out_shape=(jax.ShapeDtypeStruct((s, 512), BF16), jax.ShapeDtypeStruct((s, 512), F32),
                   jax.ShapeDtypeStruct((s, 512), F32), *reduced),
        in_specs=[qspec, kspec, tspec, kspec, qspec, _full2((2 * BK, 4 * BK)), _full2((2 * BK, 2 * BK))] + [anywhere] * n,
        out_specs=(qspec, kspec, kspec) + (anywhere,) * n,
        scratch_shapes=[pltpu.VMEM((s // BK, 8, WQ), F32), pltpu.VMEM((LANES, WQ), F32), pltpu.VMEM((8, WQ), F32)]
                       + [pltpu.VMEM(r.shape, F32) for r in reduced] + _reduce_scratch(late) + [pltpu.SemaphoreType.DMA((n,))],
        compiler_params=pltpu.CompilerParams(vmem_limit_bytes=VMEM_ATTN),
    )(q, k, kt, v, do, _sum_matrix("suffix", 2), _sum_matrix("prefix", 1), *late)


MLA_SCALE = (QK_NOPE + QK_ROPE) ** -0.5
LOG2E = 1.4426950408889634


def _mla_keys(kb):
    zero = jnp.zeros((BK, LANES), kb.dtype)
    return jnp.concatenate([jnp.concatenate([kb[:, 0:LANES], zero], axis=1),
                            jnp.concatenate([zero, kb[:, LANES:2 * LANES]], axis=1)], axis=0)


def _mla_fwd(qc, kc, vt):
    s = qc.shape[0]
    mq = min(MQ_FWD, s)
    rows_l = 16

    def body(q_ref, k_ref, vt_ref, o_ref, l_ref, p_scr, ot_scr, st_scr):
        i = pl.program_id(1)
        row = lax.broadcasted_iota(jnp.int32, (LANES, 1), 0)
        orow = lax.broadcasted_iota(jnp.int32, (rows_l, 2 * BK), 0)
        ocol = lax.broadcasted_iota(jnp.int32, (rows_l, 2 * BK), 1)
        ones = jnp.where(((orow == 0) & (ocol < BK)) | ((orow == 1) & (ocol >= BK)), 1.0, 0.0).astype(BF16)

        def chunks(lo, hi):
            return [(a, min(a + MLA_CW, hi)) for a in range(lo, hi, MLA_CW)]

        def keys(j):
            return _mla_keys(k_ref[pl.ds(pl.multiple_of(j * BK, BK), BK), :])

        def values_t(j):
            vtb = vt_ref[:, pl.ds(pl.multiple_of(j * BK, BK), BK)]
            zero = jnp.zeros_like(vtb)
            top = jnp.concatenate([jnp.where(row < 64, vtb, zero), jnp.where(row >= 64, vtb, zero)], axis=1)
            return jnp.concatenate([top, ones], axis=0)

        def pair_values(ja):
            return jnp.concatenate([values_t(ja), values_t(ja + 1)], axis=1)

        def softmax(ja, za, zb, masked, a, b):
            c = MLA_SCALE * LOG2E
            parts = [za[0:BK] * c, za[BK:2 * BK] * c, zb[0:BK] * c, zb[BK:2 * BK] * c]
            if masked:
                keyc = lax.broadcasted_iota(jnp.int32, (BK, b - a), 0)
                qryc = (lax.broadcasted_iota(jnp.int32, (BK, b - a), 1) + (i * mq + a)) // 64
                va = ((keyc + ja * BK) // 64) <= qryc
                vb = ((keyc + (ja + 1) * BK) // 64) <= qryc
                parts = [jnp.where(va, parts[0], -1e30), jnp.where(va, parts[1], -1e30),
                         jnp.where(vb, parts[2], -1e30), jnp.where(vb, parts[3], -1e30)]
            m0, m1 = st_scr[0:1, a:b], st_scr[1:2, a:b]
            n0 = jnp.maximum(m0, jnp.max(jnp.maximum(parts[0], parts[2]), axis=0, keepdims=True))
            n1 = jnp.maximum(m1, jnp.max(jnp.maximum(parts[1], parts[3]), axis=0, keepdims=True))
            st_scr[2:3, a:b] = jnp.exp2(m0 - n0)
            st_scr[3:4, a:b] = jnp.exp2(m1 - n1)
            st_scr[0:1, a:b] = n0
            st_scr[1:2, a:b] = n1
            p_scr[:, a:b] = jnp.concatenate([jnp.exp2(parts[0] - n0), jnp.exp2(parts[1] - n1),
                                             jnp.exp2(parts[2] - n0), jnp.exp2(parts[3] - n1)], axis=0).astype(BF16)

        def accumulate(vals, a, b):
            pv = _mm(vals, p_scr[:, a:b])
            f = jnp.where(row < 64, st_scr[2:3, a:b], st_scr[3:4, a:b])
            ot_scr[0:LANES, a:b] = f * ot_scr[0:LANES, a:b] + pv[0:LANES]
            ot_scr[LANES:LANES + 8, a:b] = st_scr[2:10, a:b] * ot_scr[LANES:LANES + 8, a:b] + pv[LANES:LANES + 8]

        def step(n, diag, lo=0, prev_lo=0):
            ka, kb = keys(2 * n), keys(2 * n + 1)
            vals = pair_values(2 * n - 2)
            for a, b in chunks(prev_lo, lo):
                accumulate(vals, a, b)
            for a, b in chunks(lo, mq):
                qc_ = q_ref[a:b, :]
                za, zb = _mm_nt(ka, qc_), _mm_nt(kb, qc_)
                accumulate(vals, a, b)
                softmax(2 * n, za, zb, diag and a < lo + 2 * BK, a, b)

        def first(diag):
            ka, kb = keys(0), keys(1)
            for a, b in chunks(0, mq):
                qc_ = q_ref[a:b, :]
                softmax(0, _mm_nt(ka, qc_), _mm_nt(kb, qc_), diag and a < 2 * BK, a, b)

        st_scr[...] = jnp.concatenate([jnp.full((2, mq), -1e30, F32), jnp.ones((14, mq), F32)], axis=0)
        ot_scr[...] = jnp.zeros_like(ot_scr)

        npq = mq // (2 * BK)
        seen = lambda d: 2 * BK * max(d, 0)

        @pl.when(i == 0)
        def _():
            first(True)
            for d in range(1, npq):
                step(d, True, seen(d), seen(d - 1))

        if s > mq:
            @pl.when(i > 0)
            def _():
                first(False)
                lax.fori_loop(1, npq * i, lambda n, c: (step(n, False), c)[1], 0)
                for d in range(npq):
                    step(npq * i + d, True, seen(d), seen(d - 1))

        vals = pair_values(2 * (npq * (i + 1) - 1))
        for a, b in chunks(seen(npq - 1), mq):
            accumulate(vals, a, b)
        for a, b in chunks(0, mq):
            l0, l1 = ot_scr[LANES:LANES + 1, a:b], ot_scr[LANES + 1:LANES + 2, a:b]
            o_ref[a:b, :] = (ot_scr[0:LANES, a:b] / jnp.where(row < 64, l0, l1)).T
            l_ref[a:b, :] = jnp.where(row < 64, st_scr[0:1, a:b] + jnp.log2(l0), st_scr[1:2, a:b] + jnp.log2(l1)).T

    qspec = pl.BlockSpec((mq, 2 * LANES), lambda p, i: (i, p))
    kspec = pl.BlockSpec((s, 2 * LANES), lambda p, i: (0, p))
    vtspec = pl.BlockSpec((LANES, s), lambda p, i: (p, 0))
    ospec = pl.BlockSpec((mq, LANES), lambda p, i: (i, p))
    return pl.pallas_call(
        body, name="mla_fwd", grid=(4, s // mq),
        out_shape=(jax.ShapeDtypeStruct((s, 512), F32), jax.ShapeDtypeStruct((s, 512), F32)),
        in_specs=[qspec, kspec, vtspec], out_specs=(ospec, ospec),
        scratch_shapes=[pltpu.VMEM((4 * BK, mq), BF16), pltpu.VMEM((LANES + 8, mq), F32), pltpu.VMEM((16, mq), F32)],
        compiler_params=pltpu.CompilerParams(vmem_limit_bytes=VMEM_ATTN),
    )(qc, kc, vt)


def _mla_bwd(qc, kc, kct, v, do, lse, delta):
    s = qc.shape[0]
    mq = min(MQ_BWD, s)

    def body(q_ref, k_ref, kt_ref, v_ref, do_ref, l_ref, d_ref, dq_ref, dk_ref, dv_ref, dqt_scr, p_scr, dz_scr,
             dvt_scr):
        i = pl.program_id(1)

        @pl.when(i == 0)
        def _():
            dk_ref[...] = jnp.zeros_like(dk_ref)
            dvt_scr[...] = jnp.zeros_like(dvt_scr)

        lane = lax.broadcasted_iota(jnp.int32, (1, LANES), 1)
        keyc = lax.broadcasted_iota(jnp.int32, (BK, mq), 0)
        qryc = (lax.broadcasted_iota(jnp.int32, (BK, mq), 1) + i * mq) // 64
        qw = q_ref[...]
        dob = do_ref[...]
        dost = (dob.astype(F32) * MLA_SCALE).T.astype(BF16)
        dot_ = dob.astype(F32).T.astype(BF16)
        lt = l_ref[...].T
        dt = (d_ref[...] * MLA_SCALE).T
        lse0, lse1 = lt[0:1], lt[64:65]
        dl0, dl1 = dt[0:1], dt[64:65]
        dqt_scr[...] = jnp.zeros_like(dqt_scr)

        def products(j, lo=0):
            ks = pl.multiple_of(j * BK, BK)
            return (_mm_nt(_mla_keys(k_ref[pl.ds(ks, BK), :]), qw[lo:]),
                    _mm(_pair_stack(v_ref[pl.ds(ks, BK), :], lane), dost[:, lo:]))

        def grads(j, slot, zt, dwt, masked, lo=0):
            zt = zt * (MLA_SCALE * LOG2E)
            p0 = jnp.exp2(zt[0:BK] - lse0[:, lo:])
            p1 = jnp.exp2(zt[BK:2 * BK] - lse1[:, lo:])
            if masked:
                valid = ((keyc[:, lo:] + j * BK) // 64) <= qryc[:, lo:]
                p0, p1 = jnp.where(valid, p0, 0.0), jnp.where(valid, p1, 0.0)
            rows = slice(slot * BK, (slot + 1) * BK)
            p_scr[0, rows, lo:] = p0.astype(BF16)
            p_scr[1, rows, lo:] = p1.astype(BF16)
            dz_scr[0, rows, lo:] = (p0 * (dwt[0:BK] - dl0[:, lo:])).astype(BF16)
            dz_scr[1, rows, lo:] = (p1 * (dwt[BK:2 * BK] - dl1[:, lo:])).astype(BF16)

        def scatter(ja, lo=0):
            ks = pl.multiple_of(ja * BK, 2 * BK)
            for h in range(2):
                cols, vrows = slice(h * LANES, (h + 1) * LANES), slice(h * 64, (h + 1) * 64)
                dzh = dz_scr[h, :, lo:]
                dqt_scr[cols, lo:] += _mm(kt_ref[cols, pl.ds(ks, 2 * BK)], dzh)
                dk_ref[pl.ds(ks, 2 * BK), cols] += _mm(dzh, qw[lo:, cols])
                dvt_scr[vrows, pl.ds(ks, 2 * BK)] += _mm_nt(dot_[vrows, lo:], p_scr[h, :, lo:])

        def step(n, masked, lo=0, prev_lo=0):
            za, wa = products(2 * n, lo)
            zb, wb = products(2 * n + 1, lo)
            scatter(2 * n - 2, prev_lo)
            grads(2 * n, 0, za, wa, masked, lo)
            grads(2 * n + 1, 1, zb, wb, masked, lo)

        def first(masked):
            za, wa = products(0)
            zb, wb = products(1)
            grads(0, 0, za, wa, masked)
            grads(1, 1, zb, wb, masked)

        npq = mq // (2 * BK)
        seen = lambda d: 2 * BK * max(d, 0)

        @pl.when(i == 0)
        def _():
            first(True)
            for d in range(1, npq):
                step(d, True, seen(d), seen(d - 1))

        @pl.when(i > 0)
        def _():
            first(False)
            step(1, False)
            lax.fori_loop(1, npq * i // 2, lambda m, c: (step(2 * m, False), step(2 * m + 1, False), c)[2], 0)
            for d in range(npq):
                step(npq * i + d, True, seen(d), seen(d - 1))

        scatter(2 * (npq * (i + 1) - 1), seen(npq - 1))
        dq_ref[...] = dqt_scr[...].T

        @pl.when(i == s // mq - 1)
        def _():
            for a in range(0, s, 512):
                dv_ref[a:a + 512, :] = dvt_scr[:, a:a + 512].T

    qspec = pl.BlockSpec((mq, 2 * LANES), lambda p, i: (i, p))
    kspec = pl.BlockSpec((s, 2 * LANES), lambda p, i: (0, p))
    ktspec = pl.BlockSpec((2 * LANES, s), lambda p, i: (p, 0))
    vspec = pl.BlockSpec((s, LANES), lambda p, i: (0, p))
    ospec = pl.BlockSpec((mq, LANES), lambda p, i: (i, p))
    return pl.pallas_call(
        body, name="mla_bwd", grid=(4, s // mq),
        out_shape=(jax.ShapeDtypeStruct((s, 1024), F32), jax.ShapeDtypeStruct((s, 1024), F32),
                   jax.ShapeDtypeStruct((s, 512), F32)),
        in_specs=[qspec, kspec, ktspec, vspec, ospec, ospec, ospec], out_specs=(qspec, kspec, vspec),
        scratch_shapes=[pltpu.VMEM((2 * LANES, mq), F32), pltpu.VMEM((2, 2 * BK, mq), BF16), pltpu.VMEM((2, 2 * BK, mq), BF16),
                        pltpu.VMEM((LANES, s), F32)],
        compiler_params=pltpu.CompilerParams(vmem_limit_bytes=VMEM_ATTN),
    )(qc, kc, kct, v, do, lse, delta)


def _post(x, p, tgt, sbo, mlao, sbg, mlag, gsb, gmla, wout, gpost, wple, gple, wpg, bpg):
    s = x.shape[0]

    def body(x_ref, p_ref, t_ref, sbo_ref, mlao_ref, sbg_ref, mlag_ref, gsb_ref, gmla_ref, wout_ref,
             gpost_ref, wple_ref, gple_ref, wpg_ref, bpg_ref, bd_ref,
             dsbo_ref, dmlao_ref, delta_ref, dsbg_ref, dmlag_ref, dxres_ref, dwout_ref, dwpg_ref, dwple_ref, vec_ref):
        i = pl.program_id(0)

        @pl.when(i == 0)
        def _():
            dwout_ref[...] = jnp.zeros_like(dwout_ref)
            dwpg_ref[...] = jnp.zeros_like(dwpg_ref)
            dwple_ref[...] = jnp.zeros_like(dwple_ref)
            vec_ref[...] = jnp.zeros_like(vec_ref)

        inv_hd = 1.0 / HEAD_DIM

        def head_fwd(o, g, gate):
            r = lax.rsqrt(_seg(o * o, bd_ref[...]) * inv_hd + EPS)
            hat = o * r
            n = hat * g
            sg = _sigmoid(gate)
            return hat, r, n, sg, n * (gate * sg)

        sbo, mlao, sbg_v, mlag_v = sbo_ref[...], mlao_ref[...], sbg_ref[...], mlag_ref[...]
        gsb_v, gmla_v = gsb_ref[...], gmla_ref[...]
        sb_hat, sb_r, sb_n, sb_sg, sb_y = head_fwd(sbo, gsb_v, sbg_v)
        ml_hat, ml_r, ml_n, ml_sg, ml_y = head_fwd(mlao, gmla_v, mlag_v)
        mix = jnp.concatenate([sb_y, ml_y], axis=1).astype(BF16)
        y = _mm(mix, wout_ref[...])
        ry = lax.rsqrt(_rowmean(y * y) + EPS)
        y_hat = y * ry
        gpost_v = gpost_ref[...]
        x1 = x_ref[...] + y_hat * gpost_v
        pb = p_ref[...].astype(BF16)
        pl_ = _mm(pb, wple_ref[...])
        rp = lax.rsqrt(_rowmean(pl_ * pl_) + EPS)
        pl_hat = pl_ * rp
        gple_v = gple_ref[...]
        ple = pl_hat * gple_v
        x1b = x1.astype(BF16)
        gate = _sigmoid(_mm(x1b, wpg_ref[...]) + bpg_ref[...])
        err = x1 + ple * gate - t_ref[...]
        loss = 0.5 * jnp.sum(_rowmean(err * err))
        dout = err * (1.0 / D_MODEL)

        du = dout * ple * gate * (1.0 - gate)
        dub = du.astype(BF16)
        dple = dout * gate
        dx1 = dout + _mm_nt(dub, wpg_ref[...])
        dwpg_ref[...] += _mm_tn(x1b, dub)
        dplh = dple * gple_v
        dpl = rp * (dplh - pl_hat * _rowmean(dplh * pl_hat))
        dwple_ref[...] += _mm_tn(pb, dpl.astype(BF16))
        dxres_ref[...] = dx1
        dyh = dx1 * gpost_v
        dy = ry * (dyh - y_hat * _rowmean(dyh * y_hat))
        dyb = dy.astype(BF16)
        dwout_ref[...] += _mm_tn(mix, dyb)
        dmix = _mm_nt(dyb, wout_ref[...])

        def head_bwd(dyv, hat, r, n, sg, g, gate):
            dn = dyv * (gate * sg)
            dgate = dyv * n * (sg * (1.0 + gate * (1.0 - sg)))
            dhat = dn * g
            do = r * (dhat - hat * (_seg(dhat * hat, bd_ref[...]) * inv_hd))
            return do, dgate, _colsum(dn * hat)

        dsbo, dsbg, dg_sb = head_bwd(dmix[:, 0:512], sb_hat, sb_r, sb_n, sb_sg, gsb_v, sbg_v)
        dmlao, dmlag, dg_ml = head_bwd(dmix[:, 512:1024], ml_hat, ml_r, ml_n, ml_sg, gmla_v, mlag_v)
        dsbo_ref[...] = dsbo.astype(BF16)
        dmlao_ref[...] = dmlao.astype(BF16)
        delta_ref[...] = _seg(dmlao * mlao, bd_ref[...])
        dsbg_ref[...] = dsbg.astype(BF16)
        dmlag_ref[...] = dmlag.astype(BF16)
        vec_ref[pl.ds(0, 1), :] += _colsum(dx1 * y_hat)
        vec_ref[pl.ds(1, 1), :] += _colsum(dple * pl_hat)
        vec_ref[pl.ds(2, 1), :] += _colsum(du)
        vec_ref[pl.ds(3, 1), :] += jnp.concatenate([dg_sb, dg_ml], axis=1)
        vec_ref[pl.ds(4, 1), :] += jnp.full((1, D_MODEL), loss, F32)

    out_shape = (
        jax.ShapeDtypeStruct((s, 512), BF16), jax.ShapeDtypeStruct((s, 512), BF16), jax.ShapeDtypeStruct((s, 512), F32),
        jax.ShapeDtypeStruct((s, 512), BF16), jax.ShapeDtypeStruct((s, 512), BF16), jax.ShapeDtypeStruct((s, D_MODEL), F32),
        jax.ShapeDtypeStruct((D_MODEL, D_MODEL), F32), jax.ShapeDtypeStruct((D_MODEL, D_MODEL), F32),
        jax.ShapeDtypeStruct((PLE_DIM, D_MODEL), F32), jax.ShapeDtypeStruct((8, D_MODEL), F32),
    )
    return pl.pallas_call(
        body, name="post_fwd_bwd", grid=(s // TM,), out_shape=out_shape,
        in_specs=[_rows(D_MODEL), _rows(PLE_DIM), _rows(D_MODEL), _rows(512), _rows(512), _rows(512), _rows(512),
                  _full((1, 512)), _full((1, 512)), _full((D_MODEL, D_MODEL)),
                  _full((1, D_MODEL)), _full((PLE_DIM, D_MODEL)), _full((1, D_MODEL)), _full((D_MODEL, D_MODEL)),
                  _full((1, D_MODEL)), _full((1024, 512))],
        out_specs=(_rows(512), _rows(512), _rows(512), _rows(512), _rows(512), _rows(D_MODEL),
                   _acc((D_MODEL, D_MODEL)), _acc((D_MODEL, D_MODEL)), _acc((PLE_DIM, D_MODEL)), _acc((8, D_MODEL))),
        compiler_params=pltpu.CompilerParams(vmem_limit_bytes=VMEM_DENSE),
    )(x, p, tgt, sbo, mlao, sbg, mlag, gsb, gmla, wout, gpost, wple, gple, wpg, bpg, _blockdiag2(512, HEAD_DIM))


def _pre_bwd(x, dxres, dsbq, dsbk, dsbv, dsbg, dmlag, dqc, dkc, dmv, cq, ckv, tabs, gpre, win, gq, wuq, gkv, wk, wv):
    s = x.shape[0]
    c_t, sa_t, sb_t = tabs
    rw = _rows

    def body(x_ref, dxres_ref, dsbq_ref, dsbk_ref, dsbv_ref, dsbg_ref, dmlag_ref, dqc_ref, dkc_ref, dmv_ref, cq_ref,
             ckv_ref, c_ref, sa_ref, sb_ref, gpre_ref, win_ref, gq_ref, wuq_ref, gkv_ref, wk_ref, wv_ref,
             gx_ref, dwin_ref, dwuq_ref, dwk_ref, dwv_ref, vec_ref, dwin_acc):
        i = pl.program_id(0)

        @pl.when(i == 0)
        def _():
            dwin_acc[...] = jnp.zeros_like(dwin_acc)
            dwuq_ref[...] = jnp.zeros_like(dwuq_ref)
            dwk_ref[...] = jnp.zeros_like(dwk_ref)
            dwv_ref[...] = jnp.zeros_like(dwv_ref)
            vec_ref[...] = jnp.zeros_like(vec_ref)

        lane = lax.broadcasted_iota(jnp.int32, (1, LANES), 1)
        c1, sa1, sb1 = c_ref[...], sa_ref[...], sb_ref[...]
        c8, sa8, sb8 = jnp.tile(c1, (1, 8)), jnp.tile(sa1, (1, 8)), jnp.tile(sb1, (1, 8))

        def norm_bwd(dn, hat, r, g):
            t = dn * g
            return r * (t - hat * _rowmean(t * hat)), _colsum(dn * hat)

        xv = x_ref[...]
        r1 = lax.rsqrt(_rowmean(xv * xv) + EPS)
        x_hat = xv * r1
        gpre_v = gpre_ref[...]
        hb = (x_hat * gpre_v).astype(BF16)
        ready = jnp.concatenate([dsbq_ref[...], dsbk_ref[...].astype(BF16), dsbv_ref[...].astype(BF16), dsbg_ref[...]], axis=1)
        dmlag = dmlag_ref[...]
        dwin_acc[:, 0:2048] += _mm_tn(hb, ready)
        dwin_acc[:, 2560:3072] += _mm_tn(hb, dmlag)
        dh = _mm_nt(ready, win_ref[:, 0:2048]) + _mm_nt(dmlag, win_ref[:, 2560:3072])

        dqeb = _rope_bwd(dqc_ref[...], c8, sa8, sb8).astype(BF16)
        cq = cq_ref[...]
        rq = lax.rsqrt(_rowmean(cq * cq) + EPS)
        cq_hat = cq * rq
        gq_v = gq_ref[...]
        dwuq_ref[...] += _mm_tn((cq_hat * gq_v).astype(BF16), dqeb)
        dcq, dg_q = norm_bwd(_mm_nt(dqeb, wuq_ref[...]), cq_hat, rq, gq_v)

        dkc = dkc_ref[...]
        dkcb = dkc.astype(BF16)
        dmvb = dmv_ref[...].astype(BF16)
        ckv = ckv_ref[...]
        rkv = lax.rsqrt(_rowmean(ckv * ckv) + EPS)
        ckv_hat = ckv * rkv
        gkv_v = gkv_ref[...]
        ckvnb = (ckv_hat * gkv_v).astype(BF16)
        dwk_ref[...] += _mm_tn(ckvnb, dkcb)
        dwv_ref[...] += _mm_tn(ckvnb, dmvb)
        dckv, dg_kv = norm_bwd(_mm_nt(dkcb, wk_ref[...]) + _mm_nt(dmvb, wv_ref[...]), ckv_hat, rkv, gkv_v)

        dkr = dkc[:, 0:LANES]
        for hh in range(1, 8):
            dkr = dkr + dkc[:, LANES * hh:LANES * (hh + 1)]
        dkr = _rope_bwd(dkr, c1, sa1, sb1)
        dkr = jnp.where((lane >= 64) & (lane < 96), dkr, 0.0)

        late = jnp.concatenate([dcq.astype(BF16), dckv.astype(BF16), dkr.astype(BF16)], axis=1)
        dwin_acc[:, 2048:2560] += _mm_tn(hb, late)
        dx, dg_pre = norm_bwd(dh + _mm_nt(late, win_ref[:, 2048:2560]), x_hat, r1, gpre_v)
        gx_ref[...] = dxres_ref[...] + dx
        vec_ref[pl.ds(0, 1), :] += dg_pre
        vec_ref[pl.ds(1, 1), :] += jnp.concatenate([dg_q, dg_kv, jnp.zeros((1, D_MODEL - Q_LORA - KV_LORA), F32)], axis=1)

        @pl.when(i == pl.num_programs(0) - 1)
        def _():
            pltpu.sync_copy(dwin_acc, dwin_ref)

    out_shape = (
        jax.ShapeDtypeStruct((s, D_MODEL), F32), jax.ShapeDtypeStruct((D_MODEL, D_EXT), F32),
        jax.ShapeDtypeStruct((Q_LORA, 1024), F32), jax.ShapeDtypeStruct((KV_LORA, 1024), F32),
        jax.ShapeDtypeStruct((KV_LORA, 512), F32), jax.ShapeDtypeStruct((8, D_MODEL), F32),
    )
    return pl.pallas_call(
        body, name="pre_bwd", grid=(s // TM,), out_shape=out_shape,
        in_specs=[rw(D_MODEL), rw(D_MODEL), rw(512), rw(512), rw(512), rw(512), rw(512),
                  rw(1024), rw(1024), rw(512), rw(Q_LORA), rw(KV_LORA), rw(LANES), rw(LANES),
                  rw(LANES), _full((1, D_MODEL)), _full((D_MODEL, D_EXT)), _full((1, Q_LORA)), _full((Q_LORA, 1024)),
                  _full((1, KV_LORA)), _full((KV_LORA, 1024)), _full((KV_LORA, 512))],
        out_specs=(rw(D_MODEL), pl.BlockSpec(memory_space=pl.ANY), _acc((Q_LORA, 1024)), _acc((KV_LORA, 1024)),
                   _acc((KV_LORA, 512)), _acc((8, D_MODEL))),
        scratch_shapes=[pltpu.VMEM((D_MODEL, D_EXT), F32)],
        compiler_params=pltpu.CompilerParams(vmem_limit_bytes=VMEM_DENSE),
    )(x, dxres, dsbq, dsbk, dsbv, dsbg, dmlag, dqc, dkc, dmv, cq, ckv, c_t, sa_t, sb_t, gpre, win, gq, wuq, gkv, wk, wv)


def _place():
    return lax.axis_index("x"), lax.axis_index("y"), lax.axis_index("c")


def _gather_steps(shapes, ins, bufs, send_sems, recv_sems):
    n = len(shapes)
    x, y, c = _place()
    me, sib = (x, y, c), (x, y, 1 - c)
    chips = [(1 - x, y), (x, 1 - y), (1 - x, 1 - y)]

    def half(t, chip, hc):
        rows = shapes[t][0] // 2
        return bufs[t].at[2 * chip[0] + chip[1], pl.ds(pl.multiple_of(hc * rows, 16), rows), :]

    def copy(k, t, chip, hc, to):
        return pltpu.make_async_remote_copy(src_ref=half(t, chip, hc), dst_ref=half(t, chip, hc), send_sem=send_sems.at[k],
                                            recv_sem=recv_sems.at[k], device_id=to, device_id_type=MESH)

    def start():
        for t in range(n):
            bufs[t][2 * x + y] = ins[t][...].astype(BF16)
            for j, chip in enumerate(chips):
                copy(6 * t + j, t, (x, y), c, (*chip, c)).start()

    def forward():
        for t in range(n):
            for j, chip in enumerate(chips):
                copy(6 * t + j, t, chip, c, me).wait_recv()
                copy(6 * t + 3 + j, t, chip, c, sib).start()

    def finish():
        for t in range(n):
            for j, chip in enumerate(chips):
                copy(6 * t + 3 + j, t, chip, 1 - c, me).wait_recv()
        for t in range(n):
            for j, chip in enumerate(chips):
                copy(6 * t + j, t, (x, y), c, (*chip, c)).wait_send()
                copy(6 * t + 3 + j, t, chip, c, sib).wait_send()

    return start, forward, finish


def _allgather_weights(shards):
    n = len(shards)

    def body(*refs):
        start, forward, finish = _gather_steps([a.shape for a in shards], refs[:n], refs[n:2 * n], refs[2 * n], refs[2 * n + 1])
        start()
        forward()
        finish()

    return pl.pallas_call(
        body, name="allgather_weights",
        out_shape=tuple(jax.ShapeDtypeStruct((N_SHARD,) + a.shape, BF16) for a in shards),
        in_specs=[pl.BlockSpec(memory_space=pltpu.VMEM)] * n, out_specs=(pl.BlockSpec(memory_space=pltpu.VMEM),) * n,
        scratch_shapes=[pltpu.SemaphoreType.DMA((6 * n,)), pltpu.SemaphoreType.DMA((6 * n,))],
        compiler_params=pltpu.CompilerParams(vmem_limit_bytes=VMEM_ATTN),
    )(*shards)


def _reduce_scratch(gsh):
    n = len(gsh)
    half_shapes = [(N_SHARD, a.shape[1] // 2, a.shape[2]) for a in gsh]
    return ([pltpu.VMEM(s_, F32) for s_ in half_shapes] * 2 + [pltpu.VMEM(s_, BF16) for s_ in half_shapes] * 2
            + [pltpu.SemaphoreType.DMA((n,)), pltpu.SemaphoreType.DMA((5 * n,)), pltpu.SemaphoreType.DMA((5 * n,))])


def _reduce_steps(halves, g_refs, f_refs, scratch):
    n = len(halves)
    accs, sibs, sbufs, rbufs = scratch[0:n], scratch[n:2 * n], scratch[2 * n:3 * n], scratch[3 * n:4 * n]
    local_sems, send_sems, recv_sems = scratch[4 * n:4 * n + 3]
    x, y, c = _place()
    me, sib = (x, y, c), (x, y, 1 - c)
    mine = 2 * x + y
    chips = [(1 - x, y), (x, 1 - y), (1 - x, 1 - y)]

    def remote(k, src, dst, to):
        return pltpu.make_async_remote_copy(src_ref=src, dst_ref=dst, send_sem=send_sems.at[k], recv_sem=recv_sems.at[k],
                                            device_id=to, device_id_type=MESH)

    def half3(ref, t, hc):
        return ref.at[:, pl.ds(pl.multiple_of(hc * halves[t], 8), halves[t]), :]

    def half2(ref, t, hc):
        return ref.at[pl.ds(pl.multiple_of(hc * halves[t], 8), halves[t]), :]

    def mine_load(t):
        return pltpu.make_async_copy(half3(g_refs[t], t, c), accs[t], local_sems.at[t])

    def to_sibling(t, to):
        return remote(t, half3(g_refs[t], t, 1 - c), sibs[t], to)

    def to_chip(t, j, chip, to):
        idx = 2 * chip[0] + chip[1]
        return remote(n + 3 * t + j, sbufs[t].at[idx], rbufs[t].at[mine if to is not me else idx], to)

    def swap(t, hc, to):
        return remote(4 * n + t, half2(f_refs[t], t, hc), half2(f_refs[t], t, hc), to)

    def load():
        for t in range(n):
            mine_load(t).start()
            to_sibling(t, sib).start()

    def partial():
        for t in range(n):
            mine_load(t).wait()
            to_sibling(t, me).wait_recv()
            for k in range(N_SHARD):
                accs[t][k] = accs[t][k] + sibs[t][k]
            for j, chip in enumerate(chips):
                idx = 2 * chip[0] + chip[1]
                sbufs[t][idx] = accs[t][idx].astype(BF16)
                to_chip(t, j, chip, (*chip, c)).start()

    def total():
        for t in range(n):
            acc = accs[t][mine]
            for j, chip in enumerate(chips):
                to_chip(t, j, chip, me).wait_recv()
                acc = acc + rbufs[t][2 * chip[0] + chip[1]].astype(F32)
            half2(f_refs[t], t, c)[...] = acc
            swap(t, c, sib).start()

    def finish():
        for t in range(n):
            swap(t, 1 - c, me).wait_recv()
        for t in range(n):
            to_sibling(t, sib).wait_send()
            for j, chip in enumerate(chips):
                to_chip(t, j, chip, (*chip, c)).wait_send()
            swap(t, c, sib).wait_send()

    return load, partial, total, finish


def _reduce_scatter_grads(gsh, vec):
    n = len(gsh)
    halves = [a.shape[1] // 2 for a in gsh]

    def body(*refs):
        g_refs, vec_ref, f_refs, vsum_ref = refs[:n], refs[n], refs[n + 1:2 * n + 1], refs[2 * n + 1]
        scratch = refs[2 * n + 2:]
        vrecv, vsend_sems, vrecv_sems = scratch[4 * n + 3:]
        load, partial, total, finish = _reduce_steps(halves, g_refs, f_refs, scratch)
        x, y, c = _place()
        my_dev = 4 * x + 2 * y + c

        def flip(k):
            return x ^ ((k >> 2) & 1), y ^ ((k >> 1) & 1), c ^ (k & 1)

        def vcopy(k, slot, to):
            return pltpu.make_async_remote_copy(src_ref=vec_ref, dst_ref=vrecv.at[slot], send_sem=vsend_sems.at[k - 1],
                                                recv_sem=vrecv_sems.at[k - 1], device_id=to, device_id_type=MESH)

        load()
        vrecv[my_dev] = vec_ref[...]
        for k in range(1, 8):
            vcopy(k, my_dev, flip(k)).start()
        partial()
        total()
        finish()
        for k in range(1, 8):
            fx, fy, fc = flip(k)
            vcopy(k, 4 * fx + 2 * fy + fc, (x, y, c)).wait_recv()
        vs = vrecv[0]
        for d in range(1, 8):
            vs = vs + vrecv[d]
        vsum_ref[...] = vs
        for k in range(1, 8):
            vcopy(k, my_dev, flip(k)).wait_send()

    return pl.pallas_call(
        body, name="reduce_scatter_grads",
        out_shape=tuple(jax.ShapeDtypeStruct(a.shape[1:], F32) for a in gsh) + (jax.ShapeDtypeStruct((VEC_ROWS, 1024), F32),),
        in_specs=[pl.BlockSpec(memory_space=pl.ANY)] * n + [pl.BlockSpec(memory_space=pltpu.VMEM)],
        out_specs=(pl.BlockSpec(memory_space=pltpu.VMEM),) * (n + 1),
        scratch_shapes=_reduce_scratch(gsh) + [pltpu.VMEM((8, VEC_ROWS, 1024), F32), pltpu.SemaphoreType.DMA((7,)),
                                               pltpu.SemaphoreType.DMA((7,))],
        compiler_params=pltpu.CompilerParams(vmem_limit_bytes=56 * 1024 * 1024),
    )(*gsh, vec)


def _adamw(w, g, m, v):
    rows, cols = w.shape
    tr = rows if rows <= 256 else 256
    flip = cols % LANES != 0

    def body(w_ref, g_ref, m_ref, v_ref, g_out, d_ref, nm_ref, nv_ref):
        gv = g_ref[...].T if flip else g_ref[...]
        outs = (gv,) + _adam_math(w_ref[...], gv, m_ref[...], v_ref[...])
        for ref, val in zip((g_out, d_ref, nm_ref, nv_ref), outs):
            ref[...] = val

    spec = pl.BlockSpec((tr, cols), lambda i: (i, 0))
    tspec = pl.BlockSpec((cols, tr), lambda i: (0, i)) if flip else spec
    shp = jax.ShapeDtypeStruct((cols, rows) if flip else (rows, cols), F32)
    if flip:
        w, m, v = w.T, m.T, v.T
    outs = pl.pallas_call(body, name="adamw", grid=(rows // tr,), out_shape=(shp,) * 4,
                          in_specs=[tspec, spec, tspec, tspec], out_specs=(tspec,) * 4)(w, g, m, v)
    return tuple(o.T for o in outs) if flip else outs


def _adam_math(w, g, m, v):
    m2 = ADAM_B1 * m + (1.0 - ADAM_B1) * g
    v2 = ADAM_B2 * v + (1.0 - ADAM_B2) * (g * g)
    m_hat = m2 / (1.0 - ADAM_B1 ** ADAM_STEP)
    v_hat = v2 / (1.0 - ADAM_B2 ** ADAM_STEP)
    return -ADAM_LR * (m_hat / (jnp.sqrt(v_hat) + ADAM_EPS) + ADAM_WD * w), m2, v2


def _adamw_small(vsum, w, m, v):
    names = [name for name, _, _, _ in _VEC_LAYOUT]
    k = len(names)

    def body(*refs):
        vs_ref, w_refs, m_refs, v_refs = refs[0], refs[1:1 + k], refs[1 + k:1 + 2 * k], refs[1 + 2 * k:1 + 3 * k]
        outs = refs[1 + 3 * k:]
        for idx, (_, r, c0, width) in enumerate(_VEC_LAYOUT):
            gv = vs_ref[pl.ds(r, 1), pl.ds(c0, width)]
            d, m2, v2 = _adam_math(w_refs[idx][...], gv, m_refs[idx][...], v_refs[idx][...])
            outs[idx][...], outs[k + idx][...], outs[2 * k + idx][...], outs[3 * k + idx][...] = gv, d, m2, v2

    shapes = tuple(jax.ShapeDtypeStruct(w[name].shape, F32) for name in names)
    res = pl.pallas_call(
        body, name="adamw_small", out_shape=shapes * 4,
        in_specs=[pl.BlockSpec(memory_space=pltpu.VMEM)] * (1 + 3 * k), out_specs=(pl.BlockSpec(memory_space=pltpu.VMEM),) * (4 * k),
    )(vsum, *[w[name] for name in names], *[m[name] for name in names], *[v[name] for name in names])
    return tuple({name: res[part * k + idx] for idx, name in enumerate(names)} for part in range(4))


_EARLY = ("w_in", "w_uq", "w_ukv")
_LATE = ("w_out", "w_ple", "w_ple_gate")
_BIG = _EARLY + _LATE
_KR_LOCAL = 2432 - 3 * (D_IN // N_SHARD)


def _extend_early(parts):
    cols = lambda a: a.transpose(1, 0, 2).reshape(a.shape[1], N_SHARD * a.shape[2])
    g = parts["w_in"]
    zeros = lambda n: jnp.zeros((D_MODEL, n), g.dtype)
    win_ext = jnp.concatenate([g[0], g[1], g[2], g[3][:, :_KR_LOCAL], zeros(64), g[3][:, _KR_LOCAL:_KR_LOCAL + QK_ROPE],
                               zeros(32), g[3][:, _KR_LOCAL + QK_ROPE:]], axis=1)
    wuq_ext = jnp.pad(cols(parts["w_uq"]).reshape(Q_LORA, 8, 96), ((0, 0), (0, 0), (0, 32))).reshape(Q_LORA, 1024)
    wukv = cols(parts["w_ukv"]).reshape(KV_LORA, 8, 128)
    wk_ext = jnp.pad(wukv[:, :, :64], ((0, 0), (0, 0), (0, 64))).reshape(KV_LORA, 1024)
    wv = wukv[:, :, 64:].reshape(KV_LORA, 512)
    return win_ext, wuq_ext, wk_ext, wv


def _shard_cols(a):
    return a.reshape(a.shape[0], N_SHARD, a.shape[1] // N_SHARD).transpose(1, 0, 2)


def _shard_rows(a):
    return a.reshape(N_SHARD, a.shape[0] // N_SHARD, a.shape[1])


def _shard_early_grads(dwin_ext, dwuq_ext, dwk_ext, dwv):
    e, w = dwin_ext, D_IN // N_SHARD
    last = jnp.concatenate([e[:, 3 * w:2432], e[:, 2496:2528], e[:, 2560:]], axis=1)
    dwuq = dwuq_ext.reshape(Q_LORA, 8, 128)[:, :, :96].reshape(Q_LORA, 768)
    dwukv = jnp.concatenate([dwk_ext.reshape(KV_LORA, 8, 128)[:, :, :64], dwv.reshape(KV_LORA, 8, 64)], axis=2)
    return [jnp.stack([e[:, 0:w], e[:, w:2 * w], e[:, 2 * w:3 * w], last]), _shard_cols(dwuq),
            _shard_cols(dwukv.reshape(KV_LORA, 1024))]


def _rope_tables(positions):
    half = QK_ROPE // 2
    freq = ROPE_THETA ** (-jnp.arange(half, dtype=F32) / half)
    s = positions.shape[0]
    per = LANES // half
    ang = jnp.repeat(positions.astype(F32).reshape(s // per, per), half, axis=1) * jnp.tile(freq, per)
    cos, sin = lax.optimization_barrier((jnp.cos(ang), jnp.sin(ang)))
    cos, sin = cos.reshape(s, half), sin.reshape(s, half)
    z = lambda n: jnp.zeros((s, n), F32)
    c_t = jnp.concatenate([jnp.ones((s, 64), F32), cos, cos, z(32)], axis=1)
    sa_t = jnp.concatenate([z(64), -sin, z(16), z(32)], axis=1)
    sb_t = jnp.concatenate([z(64), z(16), sin, z(32)], axis=1)
    return c_t, sa_t, sb_t


def _local_grads(x, p, positions, tgt, gains, early, late):
    win_ext, wuq_ext, wk_ext, wv = _extend_early(early)
    tabs = _rope_tables(positions)
    g = gains
    sbq, sbk, sbv, sbg, mlag, cq, ckv, qc, kc, mv, sbkt, sbvt, kct, mvt = _pre_fwd(
        x, tabs, g["norm_pre_g"], win_ext, g["q_norm_g"], wuq_ext, g["kv_norm_g"], wk_ext, wv)
    sbo, wout4, wple4, wpg4 = _sb_fwd(sbq, sbk, sbvt, late)
    wout, wpg = wout4.reshape(D_MODEL, D_MODEL), wpg4.reshape(D_MODEL, D_MODEL)
    wple = wple4.transpose(1, 0, 2).reshape(PLE_DIM, D_MODEL)
    mlao, lse = _mla_fwd(qc, kc, mvt)
    dsbo, dmlao, delta, dsbg, dmlag, dxres, dwout, dwpg, dwple, vec_c = _post(
        x, p, tgt, sbo, mlao, sbg, mlag, g["sb_out_norm_g"], g["mla_out_norm_g"], wout, g["norm_post_g"], wple,
        g["ple_norm_g"], wpg, g["b_ple_gate"])
    dsbq, dsbk, dsbv, *late_grads = _sb_bwd(sbq, sbk, sbkt, sbv, dsbo, [_shard_rows(dwout), _shard_cols(dwple), _shard_rows(dwpg)])
    dqc, dkc, dmv = _mla_bwd(qc, kc, kct, mv, dmlao, lse, delta)
    gx, dwin_ext, dwuq_ext, dwk_ext, dwv, vec_d = _pre_bwd(
        x, dxres, dsbq, dsbk, dsbv, dsbg, dmlag, dqc, dkc, dmv, cq, ckv, tabs, g["norm_pre_g"], win_ext, g["q_norm_g"],
        wuq_ext, g["kv_norm_g"], wk_ext, wv)
    return gx, _shard_early_grads(dwin_ext, dwuq_ext, dwk_ext, dwv), late_grads, jnp.concatenate([vec_c, vec_d], axis=0)


_VEC_LAYOUT = (("norm_post_g", 0, 0, 1024), ("ple_norm_g", 1, 0, 1024), ("b_ple_gate", 2, 0, 1024), ("sb_out_norm_g", 3, 0, 512),
               ("mla_out_norm_g", 3, 512, 512), ("norm_pre_g", 8, 0, 1024), ("q_norm_g", 9, 0, 256), ("kv_norm_g", 9, 256, 128))
_LOSS_ROW = 4
_WEIGHT_ORDER = ("norm_pre_g", "w_in", "q_norm_g", "w_uq", "kv_norm_g", "w_ukv", "sb_out_norm_g", "mla_out_norm_g", "w_out",
                 "norm_post_g", "w_ple", "ple_norm_g", "w_ple_gate", "b_ple_gate")


def kernel(x, p, positions, norm_pre_g, w_in, q_norm_g, w_uq, kv_norm_g, w_ukv, sb_out_norm_g, mla_out_norm_g, w_out, norm_post_g, w_ple, ple_norm_g, w_ple_gate, b_ple_gate, loss_target, m_norm_pre_g, m_w_in, m_q_norm_g, m_w_uq, m_kv_norm_g, m_w_ukv, m_sb_out_norm_g, m_mla_out_norm_g, m_w_out, m_norm_post_g, m_w_ple, m_ple_norm_g, m_w_ple_gate, m_b_ple_gate, v_norm_pre_g, v_w_in, v_q_norm_g, v_w_uq, v_kv_norm_g, v_w_ukv, v_sb_out_norm_g, v_mla_out_norm_g, v_w_out, v_norm_post_g, v_w_ple, v_ple_norm_g, v_w_ple_gate, v_b_ple_gate):
    w = {"norm_pre_g": norm_pre_g, "w_in": w_in[0], "q_norm_g": q_norm_g, "w_uq": w_uq[0], "kv_norm_g": kv_norm_g, "w_ukv": w_ukv[0],
         "sb_out_norm_g": sb_out_norm_g, "mla_out_norm_g": mla_out_norm_g, "w_out": w_out[0], "norm_post_g": norm_post_g,
         "w_ple": w_ple[0], "ple_norm_g": ple_norm_g, "w_ple_gate": w_ple_gate[0], "b_ple_gate": b_ple_gate}
    m = {"norm_pre_g": m_norm_pre_g, "w_in": m_w_in[0], "q_norm_g": m_q_norm_g, "w_uq": m_w_uq[0], "kv_norm_g": m_kv_norm_g,
         "w_ukv": m_w_ukv[0], "sb_out_norm_g": m_sb_out_norm_g, "mla_out_norm_g": m_mla_out_norm_g, "w_out": m_w_out[0],
         "norm_post_g": m_norm_post_g, "w_ple": m_w_ple[0], "ple_norm_g": m_ple_norm_g, "w_ple_gate": m_w_ple_gate[0],
         "b_ple_gate": m_b_ple_gate}
    v = {"norm_pre_g": v_norm_pre_g, "w_in": v_w_in[0], "q_norm_g": v_q_norm_g, "w_uq": v_w_uq[0], "kv_norm_g": v_kv_norm_g,
         "w_ukv": v_w_ukv[0], "sb_out_norm_g": v_sb_out_norm_g, "mla_out_norm_g": v_mla_out_norm_g, "w_out": v_w_out[0],
         "norm_post_g": v_norm_post_g, "w_ple": v_w_ple[0], "ple_norm_g": v_ple_norm_g, "w_ple_gate": v_w_ple_gate[0],
         "b_ple_gate": v_b_ple_gate}
    gathered = _allgather_weights([w[n] for n in _EARLY])
    gx, early_grads, late_red, vec = _local_grads(x[0], p[0, 0], positions[0], loss_target[0], w, dict(zip(_EARLY, gathered)),
                                                  [w[n] for n in _LATE])
    *early_red, vsum = _reduce_scatter_grads(early_grads, vec)
    gred = early_red + late_red
    loss = vsum[_LOSS_ROW, 0]

    g, delta, new_m, new_v = _adamw_small(vsum, w, m, v)
    for n, gn in zip(_BIG, gred):
        g[n], delta[n], new_m[n], new_v[n] = _adamw(w[n], gn, m[n], v[n])

    lead = lambda n, a: a[None] if n in _BIG else a
    return (loss, gx[None],
            *[lead(n, g[n]) for n in _WEIGHT_ORDER], *[lead(n, delta[n]) for n in _WEIGHT_ORDER],
            *[lead(n, new_m[n]) for n in _WEIGHT_ORDER], *[lead(n, new_v[n]) for n in _WEIGHT_ORDER])
```

```python
import numpy as np
import jax
import jax.numpy as jnp
from jax import lax
from jax.experimental import pallas as pl
from jax.experimental.pallas import tpu as pltpu

F32 = jnp.float32
BF16 = jnp.bfloat16
MESH = pl.DeviceIdType.MESH

D_MODEL = 1024
HEAD_DIM = 64
D_SB = 512
D_MLA = 512
Q_LORA = 256
KV_LORA = 128
QK_NOPE = 64
QK_ROPE = 32
PLE_DIM = 256
D_IN = 2976
D_EXT = 3072
ROPE_THETA = 10000.0
EPS = 1e-6
N_SHARD = 4

ADAM_LR = 0.001
ADAM_B1 = 0.9
ADAM_B2 = 0.999
ADAM_EPS = 1e-08
ADAM_WD = 0.01
ADAM_STEP = 10

LANES = 128
BK = 128
WQ = 256
MQ_FWD = 4096
MQ_BWD = 1024
MLA_CW = 256
SB_CUTOFF = 120.0
TM = 256
TM_PRE = 256
VEC_ROWS = 16
VMEM_DENSE = 52 * 1024 * 1024
VMEM_ATTN = 40 * 1024 * 1024


def _mm(a, b):
    return jnp.dot(a, b, preferred_element_type=F32)


def _mm_nt(a, b):
    return lax.dot_general(a, b, (((1,), (1,)), ((), ())), preferred_element_type=F32)


def _mm_tn(a, b):
    return lax.dot_general(a, b, (((0,), (0,)), ((), ())), preferred_element_type=F32)


def _seg(a, bd2):
    return _mm(_split2(a), bd2)


def _const(mask):
    return jnp.asarray(np.asarray(mask, np.float32), dtype=BF16)


def _blockdiag2(n, seg):
    r = (np.arange(2 * n)[:, None] % n) // seg
    c = np.arange(n)[None, :] // seg
    return _const(r == c)


def _sigmoid(a):
    return 1.0 / (1.0 + jnp.exp(-a))


def _rowmean(a):
    return jnp.mean(a, axis=-1, keepdims=True)


def _colsum(a):
    return jnp.sum(a, axis=0, keepdims=True)


def _rope_fwd(a, c, sa, sb):
    w = a.shape[-1]
    return a * c + pltpu.roll(a, w - 16, 1) * sa + pltpu.roll(a, 16, 1) * sb


def _rope_bwd(g, c, sa, sb):
    w = g.shape[-1]
    return g * c + pltpu.roll(g * sa, 16, 1) + pltpu.roll(g * sb, w - 16, 1)


def _full(shape):
    return pl.BlockSpec(shape, lambda *_: (0,) * len(shape))


def _acc(shape):
    return pl.BlockSpec(shape, lambda *_: (0,) * len(shape))


def _full2(shape):
    return pl.BlockSpec(shape, lambda p, i: (0, 0))


def _cols(height, tm=TM):
    return pl.BlockSpec((height, tm), lambda i: (0, i))


def _rows(width, tm=TM):
    return pl.BlockSpec((tm, width), lambda i: (i, 0))


def _pre_fwd(x, tabs, gpre, win, gq, wuq, gkv, wk, wv):
    s = x.shape[0]
    c_t, sa_t, sb_t = tabs
    rw, cl = (lambda width: _rows(width, TM_PRE)), (lambda height: _cols(height, TM_PRE))

    def body(x_ref, c_ref, sa_ref, sb_ref, gpre_ref, win_ref, gq_ref, wuq_ref, gkv_ref, wk_ref, wv_ref,
             sbq_ref, sbk_ref, sbv_ref, sbg_ref, mlag_ref, cq_ref, ckv_ref, qc_ref, kc_ref, mv_ref,
             sbkt_ref, sbvt_ref, kct_ref, mvt_ref):
        xv = x_ref[...]
        r1 = lax.rsqrt(_rowmean(xv * xv) + EPS)
        h = (xv * r1 * gpre_ref[...]).astype(BF16)
        proj = _mm(h, win_ref[...])
        sbq_ref[...] = proj[:, 0:512].astype(BF16)
        sbk_ref[...] = proj[:, 512:1024].astype(BF16)
        sbv_ref[...] = proj[:, 1024:1536].astype(BF16)
        sbkt_ref[...] = proj[:, 512:1024].T.astype(BF16)
        sbvt_ref[...] = proj[:, 1024:1536].T.astype(BF16)
        sbg_ref[...] = proj[:, 1536:2048]
        cq = proj[:, 2048:2304]
        ckv = proj[:, 2304:2432]
        kr = proj[:, 2432:2560]
        mlag_ref[...] = proj[:, 2560:3072]
        cq_ref[...] = cq
        ckv_ref[...] = ckv
        c1, sa1, sb1 = c_ref[...], sa_ref[...], sb_ref[...]
        c8, sa8, sb8 = jnp.tile(c1, (1, 8)), jnp.tile(sa1, (1, 8)), jnp.tile(sb1, (1, 8))
        cqn = (cq * lax.rsqrt(_rowmean(cq * cq) + EPS) * gq_ref[...]).astype(BF16)
        qe = _mm(cqn, wuq_ref[...])
        qc_ref[...] = _rope_fwd(qe, c8, sa8, sb8).astype(BF16)
        ckvn = (ckv * lax.rsqrt(_rowmean(ckv * ckv) + EPS) * gkv_ref[...]).astype(BF16)
        ke = _mm(ckvn, wk_ref[...])
        krr = _rope_fwd(kr, c1, sa1, sb1)
        kcat = ke + jnp.tile(krr, (1, 8))
        kc_ref[...] = kcat.astype(BF16)
        kct_ref[...] = kcat.T.astype(BF16)
        mval = _mm(ckvn, wv_ref[...])
        mv_ref[...] = mval.astype(BF16)
        mvt_ref[...] = mval.T.astype(BF16)

    out_shape = (
        jax.ShapeDtypeStruct((s, 512), BF16), jax.ShapeDtypeStruct((s, 512), BF16), jax.ShapeDtypeStruct((s, 512), BF16),
        jax.ShapeDtypeStruct((s, 512), F32), jax.ShapeDtypeStruct((s, 512), F32),
        jax.ShapeDtypeStruct((s, Q_LORA), F32), jax.ShapeDtypeStruct((s, KV_LORA), F32),
        jax.ShapeDtypeStruct((s, 1024), BF16), jax.ShapeDtypeStruct((s, 1024), BF16), jax.ShapeDtypeStruct((s, 512), BF16),
        jax.ShapeDtypeStruct((512, s), BF16), jax.ShapeDtypeStruct((512, s), BF16), jax.ShapeDtypeStruct((1024, s), BF16),
        jax.ShapeDtypeStruct((512, s), BF16),
    )
    return pl.pallas_call(
        body, name="pre_fwd", grid=(s // TM_PRE,), out_shape=out_shape,
        in_specs=[rw(D_MODEL), rw(LANES), rw(LANES), rw(LANES), _full((1, D_MODEL)), _full((D_MODEL, D_EXT)),
                  _full((1, Q_LORA)), _full((Q_LORA, 1024)), _full((1, KV_LORA)), _full((KV_LORA, 1024)), _full((KV_LORA, 512))],
        out_specs=(rw(512), rw(512), rw(512), rw(512), rw(512), rw(Q_LORA), rw(KV_LORA),
                   rw(1024), rw(1024), rw(512), cl(512), cl(512), cl(1024), cl(512)),
        compiler_params=pltpu.CompilerParams(vmem_limit_bytes=VMEM_DENSE),
    )(x, c_t, sa_t, sb_t, gpre, win, gq, wuq, gkv, wk, wv)


def _softplus(z):
    neg_abs = lax.bitcast_convert_type(lax.bitcast_convert_type(z, jnp.uint32) | jnp.uint32(0x80000000), F32)
    return jnp.maximum(z, 0.0) + jnp.log(1.0 + jnp.exp(neg_abs))


def _sum_matrix(kind, terms):
    r, c = np.arange(2 * BK)[:, None], np.arange(2 * BK * terms)[None, :] % (2 * BK)
    rk, ck = r % BK, c % BK
    return _const(((r // BK) == (c // BK)) & {"suffix": ck >= rk, "prefix": ck <= rk}[kind])


def _split_rows(a):
    hi = a.astype(BF16)
    return jnp.concatenate([hi, (a - hi.astype(F32)).astype(BF16)], axis=0)


def _heads_t(blk, rowi):
    zero = jnp.zeros_like(blk)
    return jnp.concatenate([jnp.where(rowi < 64, blk, zero), jnp.where(rowi >= 64, blk, zero)], axis=1)


def _mask_keys(a, valid, fill=0.0):
    return jnp.concatenate([jnp.where(valid, a[0:BK], fill), jnp.where(valid, a[BK:2 * BK], fill)], axis=0)


def _split2(a):
    hi = a.astype(BF16)
    lo = (a - hi.astype(F32)).astype(BF16)
    return jnp.concatenate([hi, lo], axis=1)


def _pair_stack(b, lane):
    zero = jnp.zeros_like(b)
    return jnp.concatenate([jnp.where(lane < 64, b, zero), jnp.where(lane >= 64, b, zero)], axis=0)


def _sb_fwd(q, k, vt, late):
    s = q.shape[0]
    n = len(late)

    def body(q_ref, k_ref, vt_ref, usuf_ref, *rest):
        ins, o_ref, outs = rest[:n], rest[n], rest[n + 1:2 * n + 1]
        acc_scr, run_scr = rest[2 * n + 1:2 * n + 3]
        bufs, (send_sems, recv_sems, out_sems) = rest[2 * n + 3:3 * n + 3], rest[3 * n + 3:]
        p, i = pl.program_id(0), pl.program_id(1)
        gather_start, gather_forward, gather_finish = _gather_steps([a.shape for a in late], ins, bufs, send_sems, recv_sems)

        @pl.when((p == 0) & (i == 0))
        def _():
            gather_start()

        @pl.when((p == 2) & (i == 0))
        def _():
            gather_forward()

        lane = lax.broadcasted_iota(jnp.int32, (1, LANES), 1)
        rowi = lax.broadcasted_iota(jnp.int32, (LANES, 1), 0)
        keyi = lax.broadcasted_iota(jnp.int32, (BK, WQ), 0)
        qryi = lax.broadcasted_iota(jnp.int32, (BK, WQ), 1) + i * WQ
        qs = q_ref[...] * (HEAD_DIM ** -0.5)

        def group(blocks, masked, seen=None):
            seen = seen or [0] * len(blocks)
            starts = [pl.multiple_of(j * BK, BK) for j in blocks]
            valid = [(keyi[:, lo:] + j * BK) < qryi[:, lo:] if m else None for j, m, lo in zip(blocks, masked, seen)]
            zs = [_mm_nt(_pair_stack(k_ref[pl.ds(ks, BK), :], lane), qs[lo:]) for ks, lo in zip(starts, seen)]
            sps = [_softplus(z) for z in zs]
            sps = [sp if ok is None else _mask_keys(sp, ok) for sp, ok in zip(sps, valid)]
            cums = [_mm(usuf_ref[...], _split_rows(sp)) for sp in sps]
            ws = [jnp.exp(z - c) for z, c in zip(zs, cums)]
            ws = [w if ok is None else _mask_keys(w, ok) for w, ok in zip(ws, valid)]
            pvs = [_mm(_heads_t(vt_ref[:, pl.ds(ks, BK)], rowi), w.astype(BF16)) for ks, w in zip(starts, ws)]
            for pv, c, lo in zip(pvs, cums, seen):
                r0, r1 = run_scr[0:1, lo:], run_scr[1:2, lo:]
                acc_scr[:, lo:] += jnp.where(rowi < 64, jnp.exp(-r0), jnp.exp(-r1)) * pv
                run_scr[0:1, lo:] = r0 + c[0:1]
                run_scr[1:2, lo:] = r1 + c[BK:BK + 1]

        assert WQ == 2 * BK
        acc_scr[...] = jnp.zeros_like(acc_scr)
        run_scr[...] = jnp.zeros_like(run_scr)

        @pl.when(i == 0)
        def _():
            group([1, 0], [True, True], [BK, 0])

        @pl.when(i > 0)
        def _():
            group([2 * i + 1, 2 * i, 2 * i - 1, 2 * i - 2], [True, True, False, False], [BK, 0, 0, 0])

        def unfinished():
            return (jnp.min(run_scr[0:2, :]) < SB_CUTOFF).astype(jnp.int32)

        def step(c):
            group([2 * i - 1 - 2 * c[0], 2 * i - 2 - 2 * c[0]], [False, False])
            return c[0] + 1, unfinished()

        lax.while_loop(lambda c: (c[0] < i) & (c[1] > 0), step, (jnp.int32(1), unfinished()))
        o_ref[...] = acc_scr[...].T

        @pl.when((p == pl.num_programs(0) - 1) & (i == pl.num_programs(1) - 1))
        def _():
            gather_finish()
            copies = [pltpu.make_async_copy(bufs[t], outs[t], out_sems.at[t]) for t in range(n)]
            for cp in copies:
                cp.start()
            for cp in copies:
                cp.wait()

    qspec = pl.BlockSpec((WQ, LANES), lambda p, i: (i, p))
    kspec = pl.BlockSpec((s, LANES), lambda p, i: (0, p))
    tspec = pl.BlockSpec((LANES, s), lambda p, i: (p, 0))
    gathered = [jax.ShapeDtypeStruct((N_SHARD,) + a.shape, BF16) for a in late]
    return pl.pallas_call(
        body, name="sb_fwd", grid=(4, s // WQ),
        out_shape=(jax.ShapeDtypeStruct((s, 512), F32), *gathered),
        in_specs=[qspec, kspec, tspec, _full2((2 * BK, 4 * BK))] + [_full2(a.shape) for a in late],
        out_specs=(qspec,) + (pl.BlockSpec(memory_space=pl.ANY),) * n,
        scratch_shapes=[pltpu.VMEM((LANES, WQ), F32), pltpu.VMEM((8, WQ), F32)] + [pltpu.VMEM(g.shape, BF16) for g in gathered]
                       + [pltpu.SemaphoreType.DMA((6 * n,)), pltpu.SemaphoreType.DMA((6 * n,)), pltpu.SemaphoreType.DMA((n,))],
        compiler_params=pltpu.CompilerParams(vmem_limit_bytes=VMEM_ATTN),
    )(q, k, vt, _sum_matrix("suffix", 2), *late)


def _sb_bwd(q, k, kt, v, do, late):
    s = q.shape[0]
    n = len(late)
    halves = [a.shape[1] // 2 for a in late]

    def body(q_ref, k_ref, kt_ref, v_ref, do_ref, usuf_ref, upre_ref, *rest):
        g_refs, (dq_ref, dk_ref, dv_ref), outs = rest[:n], rest[n:n + 3], rest[n + 3:2 * n + 3]
        later_scr, dqt_scr, st_scr = rest[2 * n + 3:2 * n + 6]
        f_scr, reduce_scr, out_sems = rest[2 * n + 6:3 * n + 6], rest[3 * n + 6:-1], rest[-1]
        p, i = pl.program_id(0), pl.program_id(1)
        reduce_load, reduce_partial, reduce_total, reduce_finish = _reduce_steps(halves, g_refs, f_scr, reduce_scr)

        @pl.when((p == 0) & (i == 0))
        def _():
            reduce_load()

        @pl.when((p == 1) & (i == 0))
        def _():
            reduce_partial()

        @pl.when((p == 3) & (i == 0))
        def _():
            reduce_total()

        @pl.when(i == 0)
        def _():
            dk_ref[...] = jnp.zeros_like(dk_ref)
            dv_ref[...] = jnp.zeros_like(dv_ref)

        lane = lax.broadcasted_iota(jnp.int32, (1, LANES), 1)
        rowi = lax.broadcasted_iota(jnp.int32, (LANES, 1), 0)
        keyi = lax.broadcasted_iota(jnp.int32, (BK, WQ), 0)
        qryi = lax.broadcasted_iota(jnp.int32, (BK, WQ), 1) + i * WQ
        qs = q_ref[...] * (HEAD_DIM ** -0.5)
        dob = do_ref[...]
        dot = dob.astype(F32).T.astype(BF16)

        def scores(j, lo=0):
            return _mm_nt(_pair_stack(k_ref[pl.ds(pl.multiple_of(j * BK, BK), BK), :], lane), qs[lo:])

        def scan(blocks, masked, seen=None):
            seen = seen or [0] * len(blocks)
            sps = [_softplus(scores(j, lo)) for j, lo in zip(blocks, seen)]
            sps = [_mask_keys(sp, (keyi[:, lo:] + j * BK) < qryi[:, lo:]) if m else sp
                   for sp, j, m, lo in zip(sps, blocks, masked, seen)]
            for sp, j, lo in zip(sps, blocks, seen):
                run = st_scr[0:2, :]
                later_scr[j, 0:2, :] = run
                st_scr[0:2, lo:] = run[:, lo:] + jnp.concatenate([jnp.sum(sp[0:BK], axis=0, keepdims=True),
                                                                  jnp.sum(sp[BK:2 * BK], axis=0, keepdims=True)], axis=0)

        def sweep(blocks, masked, seen=None):
            seen = seen or [0] * len(blocks)
            starts = [pl.multiple_of(j * BK, BK) for j in blocks]
            valid = [(keyi[:, lo:] + j * BK) < qryi[:, lo:] if m else None for j, m, lo in zip(blocks, masked, seen)]
            zs = [scores(j, lo) for j, lo in zip(blocks, seen)]
            us = [jnp.exp(lax.bitcast_convert_type(lax.bitcast_convert_type(z, jnp.uint32) | jnp.uint32(0x80000000), F32))
                  for z in zs]
            sps = [jnp.maximum(z, 0.0) + jnp.log(1.0 + u) for z, u in zip(zs, us)]
            sps = [sp if ok is None else _mask_keys(sp, ok) for sp, ok in zip(sps, valid)]
            sigs = [jnp.where(z >= 0.0, 1.0, u) / (1.0 + u) for z, u in zip(zs, us)]
            cums = [_mm(usuf_ref[...], _split_rows(sp)) for sp in sps]
            dws = [_mm(_pair_stack(v_ref[pl.ds(ks, BK), :], lane), dot[:, lo:]) for ks, lo in zip(starts, seen)]
            wfs = []
            for z, c, j, ok, lo in zip(zs, cums, blocks, valid, seen):
                f = jnp.exp(-later_scr[j, 0:2, lo:])
                wide = (BK, WQ - lo)
                wf = jnp.exp(z - c) * jnp.concatenate([jnp.broadcast_to(f[0:1], wide), jnp.broadcast_to(f[1:2], wide)], axis=0)
                wfs.append(wf if ok is None else _mask_keys(wf, ok))
            es = [dw * wf for dw, wf in zip(dws, wfs)]
            pres = [_mm(upre_ref[...], e.astype(BF16)) for e in es]
            dzs = []
            for e, pre, sig, ok, lo in zip(es, pres, sigs, valid, seen):
                e0 = pre[0:BK] + st_scr[0:1, lo:]
                e1 = pre[BK:2 * BK] + st_scr[1:2, lo:]
                st_scr[0:1, lo:] = e0[BK - 1:BK]
                st_scr[1:2, lo:] = e1[BK - 1:BK]
                dz = e - sig * jnp.concatenate([e0, e1], axis=0)
                dzs.append((dz if ok is None else _mask_keys(dz, ok)).astype(BF16))
            whole = [b for b, lo in enumerate(seen) if lo == 0]
            dqt_scr[...] += _mm(jnp.concatenate([_heads_t(kt_ref[:, pl.ds(starts[b], BK)], rowi) for b in whole], axis=1),
                                jnp.concatenate([dzs[b] for b in whole], axis=0))
            for b, lo in enumerate(seen):
                if lo:
                    dqt_scr[:, lo:] += _mm(_heads_t(kt_ref[:, pl.ds(starts[b], BK)], rowi), dzs[b])
            for ks, dz, wf, lo in zip(starts, dzs, wfs, seen):
                rk = _mm(dz, qs[lo:])
                dk_ref[pl.ds(ks, BK), :] += jnp.where(lane < 64, rk[0:BK], rk[BK:2 * BK])
                rv = _mm(wf.astype(BF16), dob[lo:])
                dv_ref[pl.ds(ks, BK), :] += jnp.where(lane < 64, rv[0:BK], rv[BK:2 * BK])

        assert WQ == 2 * BK
        st_scr[...] = jnp.zeros_like(st_scr)

        @pl.when(i == 0)
        def _():
            scan([1, 0], [True, True], [BK, 0])

        @pl.when(i > 0)
        def _():
            scan([2 * i + 1, 2 * i, 2 * i - 1, 2 * i - 2], [True, True, False, False], [BK, 0, 0, 0])

        def unfinished():
            return (jnp.min(st_scr[0:2, :]) < SB_CUTOFF).astype(jnp.int32)

        def step(c):
            scan([2 * i - 1 - 2 * c[0], 2 * i - 2 - 2 * c[0]], [False, False])
            return c[0] + 1, unfinished()

        npairs, _ = lax.while_loop(lambda c: (c[0] < i) & (c[1] > 0), step, (jnp.minimum(i, 1), unfinished()))

        st_scr[...] = jnp.zeros_like(st_scr)
        dqt_scr[...] = jnp.zeros_like(dqt_scr)
        first = 2 * (i - npairs)

        def early(t, carry):
            sweep([first + 2 * t, first + 2 * t + 1], [False, False])
            return carry

        lax.fori_loop(0, npairs - 1, early, 0)

        @pl.when(i == 0)
        def _():
            sweep([0, 1], [True, True], [0, BK])

        @pl.when(i > 0)
        def _():
            sweep([2 * i - 2, 2 * i - 1, 2 * i, 2 * i + 1], [False, False, True, True], [0, 0, 0, BK])

        dq_ref[...] = (dqt_scr[...].T * (HEAD_DIM ** -0.5)).astype(BF16)

        @pl.when((p == pl.num_programs(0) - 1) & (i == pl.num_programs(1) - 1))
        def _():
            reduce_finish()
            copies = [pltpu.make_async_copy(f_scr[t], outs[t], out_sems.at[t]) for t in range(n)]
            for cp in copies:
                cp.start()
            for cp in copies:
                cp.wait()

    qspec = pl.BlockSpec((WQ, LANES), lambda p, i: (i, p))
    kspec = pl.BlockSpec((s, LANES), lambda p, i: (0, p))
    tspec = pl.BlockSpec((LANES, s), lambda p, i: (p, 0))
    anywhere = pl.BlockSpec(memory_space=pl.ANY)
    reduced = [jax.ShapeDtypeStruct(a.shape[1:], F32) for a in late]
    return pl.pallas_call(
        body, name="sb_bwd", grid=(4, s // WQ),
        out_shape=(jax.ShapeDtypeStruct((s, 512), BF16), jax.ShapeDtypeStruct((s, 512), F32),
                   jax.ShapeDtypeStruct((s, 512), F32), *reduced),
        in_specs=[qspec, kspec, tspec, kspec, qspec, _full2((2 * BK, 4 * BK)), _full2((2 * BK, 2 * BK))] + [anywhere] * n,
        out_specs=(qspec, kspec, kspec) + (anywhere,) * n,
        scratch_shapes=[pltpu.VMEM((s // BK, 8, WQ), F32), pltpu.VMEM((LANES, WQ), F32), pltpu.VMEM((8, WQ), F32)]
                       + [pltpu.VMEM(r.shape, F32) for r in reduced] + _reduce_scratch(late) + [pltpu.SemaphoreType.DMA((n,))],
        compiler_params=pltpu.CompilerParams(vmem_limit_bytes=VMEM_ATTN),
    )(q, k, kt, v, do, _sum_matrix("suffix", 2), _sum_matrix("prefix", 1), *late)


MLA_SCALE = (QK_NOPE + QK_ROPE) ** -0.5
LOG2E = 1.4426950408889634


def _mla_keys(kb):
    zero = jnp.zeros((BK, LANES), kb.dtype)
    return jnp.concatenate([jnp.concatenate([kb[:, 0:LANES], zero], axis=1),
                            jnp.concatenate([zero, kb[:, LANES:2 * LANES]], axis=1)], axis=0)


def _mla_fwd(qc, kc, vt):
    s = qc.shape[0]
    mq = min(MQ_FWD, s)
    rows_l = 16

    def body(q_ref, k_ref, vt_ref, o_ref, l_ref, p_scr, ot_scr, st_scr):
        i = pl.program_id(1)
        row = lax.broadcasted_iota(jnp.int32, (LANES, 1), 0)
        orow = lax.broadcasted_iota(jnp.int32, (rows_l, 2 * BK), 0)
        ocol = lax.broadcasted_iota(jnp.int32, (rows_l, 2 * BK), 1)
        ones = jnp.where(((orow == 0) & (ocol < BK)) | ((orow == 1) & (ocol >= BK)), 1.0, 0.0).astype(BF16)

        def chunks(lo, hi):
            return [(a, min(a + MLA_CW, hi)) for a in range(lo, hi, MLA_CW)]

        def keys(j):
            return _mla_keys(k_ref[pl.ds(pl.multiple_of(j * BK, BK), BK), :])

        def values_t(j):
            vtb = vt_ref[:, pl.ds(pl.multiple_of(j * BK, BK), BK)]
            zero = jnp.zeros_like(vtb)
            top = jnp.concatenate([jnp.where(row < 64, vtb, zero), jnp.where(row >= 64, vtb, zero)], axis=1)
            return jnp.concatenate([top, ones], axis=0)

        def pair_values(ja):
            return jnp.concatenate([values_t(ja), values_t(ja + 1)], axis=1)

        def softmax(ja, za, zb, masked, a, b):
            c = MLA_SCALE * LOG2E
            parts = [za[0:BK] * c, za[BK:2 * BK] * c, zb[0:BK] * c, zb[BK:2 * BK] * c]
            if masked:
                keyc = lax.broadcasted_iota(jnp.int32, (BK, b - a), 0)
                qryc = (lax.broadcasted_iota(jnp.int32, (BK, b - a), 1) + (i * mq + a)) // 64
                va = ((keyc + ja * BK) // 64) <= qryc
                vb = ((keyc + (ja + 1) * BK) // 64) <= qryc
                parts = [jnp.where(va, parts[0], -1e30), jnp.where(va, parts[1], -1e30),
                         jnp.where(vb, parts[2], -1e30), jnp.where(vb, parts[3], -1e30)]
            m0, m1 = st_scr[0:1, a:b], st_scr[1:2, a:b]
            n0 = jnp.maximum(m0, jnp.max(jnp.maximum(parts[0], parts[2]), axis=0, keepdims=True))
            n1 = jnp.maximum(m1, jnp.max(jnp.maximum(parts[1], parts[3]), axis=0, keepdims=True))
            st_scr[2:3, a:b] = jnp.exp2(m0 - n0)
            st_scr[3:4, a:b] = jnp.exp2(m1 - n1)
            st_scr[0:1, a:b] = n0
            st_scr[1:2, a:b] = n1
            p_scr[:, a:b] = jnp.concatenate([jnp.exp2(parts[0] - n0), jnp.exp2(parts[1] - n1),
                                             jnp.exp2(parts[2] - n0), jnp.exp2(parts[3] - n1)], axis=0).astype(BF16)

        def accumulate(vals, a, b):
            pv = _mm(vals, p_scr[:, a:b])
            f = jnp.where(row < 64, st_scr[2:3, a:b], st_scr[3:4, a:b])
            ot_scr[0:LANES, a:b] = f * ot_scr[0:LANES, a:b] + pv[0:LANES]
            ot_scr[LANES:LANES + 8, a:b] = st_scr[2:10, a:b] * ot_scr[LANES:LANES + 8, a:b] + pv[LANES:LANES + 8]

        def step(n, diag, lo=0, prev_lo=0):
            kab = jnp.concatenate([keys(2 * n), keys(2 * n + 1)], axis=0)
            vals = pair_values(2 * n - 2)
            for a, b in chunks(prev_lo, lo):
                accumulate(vals, a, b)
            for a, b in chunks(lo, mq):
                zab = _mm_nt(kab, q_ref[a:b, :])
                accumulate(vals, a, b)
                softmax(2 * n, zab[0:2 * BK], zab[2 * BK:4 * BK], diag and a < lo + 2 * BK, a, b)

        def first(diag):
            kab = jnp.concatenate([keys(0), keys(1)], axis=0)
            for a, b in chunks(0, mq):
                zab = _mm_nt(kab, q_ref[a:b, :])
                softmax(0, zab[0:2 * BK], zab[2 * BK:4 * BK], diag and a < 2 * BK, a, b)

        st_scr[...] = jnp.concatenate([jnp.full((2, mq), -1e30, F32), jnp.ones((14, mq), F32)], axis=0)
        ot_scr[...] = jnp.zeros_like(ot_scr)

        npq = mq // (2 * BK)
        seen = lambda d: 2 * BK * max(d, 0)

        @pl.when(i == 0)
        def _():
            first(True)
            for d in range(1, npq):
                step(d, True, seen(d), seen(d - 1))

        if s > mq:
            @pl.when(i > 0)
            def _():
                first(False)
                lax.fori_loop(1, npq * i, lambda n, c: (step(n, False), c)[1], 0)
                for d in range(npq):
                    step(npq * i + d, True, seen(d), seen(d - 1))

        vals = pair_values(2 * (npq * (i + 1) - 1))
        for a, b in chunks(seen(npq - 1), mq):
            accumulate(vals, a, b)
        for a, b in chunks(0, mq):
            l0, l1 = ot_scr[LANES:LANES + 1, a:b], ot_scr[LANES + 1:LANES + 2, a:b]
            o_ref[a:b, :] = (ot_scr[0:LANES, a:b] / jnp.where(row < 64, l0, l1)).T
            l_ref[a:b, :] = jnp.where(row < 64, st_scr[0:1, a:b] + jnp.log2(l0), st_scr[1:2, a:b] + jnp.log2(l1)).T

    qspec = pl.BlockSpec((mq, 2 * LANES), lambda p, i: (i, p))
    kspec = pl.BlockSpec((s, 2 * LANES), lambda p, i: (0, p))
    vtspec = pl.BlockSpec((LANES, s), lambda p, i: (p, 0))
    ospec = pl.BlockSpec((mq, LANES), lambda p, i: (i, p))
    return pl.pallas_call(
        body, name="mla_fwd", grid=(4, s // mq),
        out_shape=(jax.ShapeDtypeStruct((s, 512), F32), jax.ShapeDtypeStruct((s, 512), F32)),
        in_specs=[qspec, kspec, vtspec], out_specs=(ospec, ospec),
        scratch_shapes=[pltpu.VMEM((4 * BK, mq), BF16), pltpu.VMEM((LANES + 8, mq), F32), pltpu.VMEM((16, mq), F32)],
        compiler_params=pltpu.CompilerParams(vmem_limit_bytes=VMEM_ATTN),
    )(qc, kc, vt)


def _mla_bwd(qc, kc, kct, v, do, lse, delta):
    s = qc.shape[0]
    mq = min(MQ_BWD, s)

    def body(q_ref, k_ref, kt_ref, v_ref, do_ref, l_ref, d_ref, dq_ref, dk_ref, dv_ref, dqt_scr, p_scr, dz_scr,
             dvt_scr):
        i = pl.program_id(1)

        @pl.when(i == 0)
        def _():
            dk_ref[...] = jnp.zeros_like(dk_ref)
            dvt_scr[...] = jnp.zeros_like(dvt_scr)

        lane = lax.broadcasted_iota(jnp.int32, (1, LANES), 1)
        keyc = lax.broadcasted_iota(jnp.int32, (BK, mq), 0)
        qryc = (lax.broadcasted_iota(jnp.int32, (BK, mq), 1) + i * mq) // 64
        qw = q_ref[...]
        dob = do_ref[...]
        dost = (dob.astype(F32) * MLA_SCALE).T.astype(BF16)
        dot_ = dob.astype(F32).T.astype(BF16)
        lt = l_ref[...].T
        dt = (d_ref[...] * MLA_SCALE).T
        lse0, lse1 = lt[0:1], lt[64:65]
        dl0, dl1 = dt[0:1], dt[64:65]
        dqt_scr[...] = jnp.zeros_like(dqt_scr)

        def products(j, lo=0):
            ks = pl.multiple_of(j * BK, BK)
            return (_mm_nt(_mla_keys(k_ref[pl.ds(ks, BK), :]), qw[lo:]),
                    _mm(_pair_stack(v_ref[pl.ds(ks, BK), :], lane), dost[:, lo:]))

        def grads(j, slot, zt, dwt, masked, lo=0):
            zt = zt * (MLA_SCALE * LOG2E)
            p0 = jnp.exp2(zt[0:BK] - lse0[:, lo:])
            p1 = jnp.exp2(zt[BK:2 * BK] - lse1[:, lo:])
            if masked:
                valid = ((keyc[:, lo:] + j * BK) // 64) <= qryc[:, lo:]
                p0, p1 = jnp.where(valid, p0, 0.0), jnp.where(valid, p1, 0.0)
            rows = slice(slot * BK, (slot + 1) * BK)
            p_scr[0, rows, lo:] = p0.astype(BF16)
            p_scr[1, rows, lo:] = p1.astype(BF16)
            dz_scr[0, rows, lo:] = (p0 * (dwt[0:BK] - dl0[:, lo:])).astype(BF16)
            dz_scr[1, rows, lo:] = (p1 * (dwt[BK:2 * BK] - dl1[:, lo:])).astype(BF16)

        def scatter(ja, lo=0):
            ks = pl.multiple_of(ja * BK, 2 * BK)
            for h in range(2):
                cols, vrows = slice(h * LANES, (h + 1) * LANES), slice(h * 64, (h + 1) * 64)
                dzh = dz_scr[h, :, lo:]
                dqt_scr[cols, lo:] += _mm(kt_ref[cols, pl.ds(ks, 2 * BK)], dzh)
                dk_ref[pl.ds(ks, 2 * BK), cols] += _mm(dzh, qw[lo:, cols])
                dvt_scr[vrows, pl.ds(ks, 2 * BK)] += _mm_nt(dot_[vrows, lo:], p_scr[h, :, lo:])

        def step(n, masked, lo=0, prev_lo=0):
            za, wa = products(2 * n, lo)
            zb, wb = products(2 * n + 1, lo)
            scatter(2 * n - 2, prev_lo)
            grads(2 * n, 0, za, wa, masked, lo)
            grads(2 * n + 1, 1, zb, wb, masked, lo)

        def first(masked):
            za, wa = products(0)
            zb, wb = products(1)
            grads(0, 0, za, wa, masked)
            grads(1, 1, zb, wb, masked)

        npq = mq // (2 * BK)
        seen = lambda d: 2 * BK * max(d, 0)

        @pl.when(i == 0)
        def _():
            first(True)
            for d in range(1, npq):
                step(d, True, seen(d), seen(d - 1))

        @pl.when(i > 0)
        def _():
            first(False)
            step(1, False)
            lax.fori_loop(1, npq * i // 2, lambda m, c: (step(2 * m, False), step(2 * m + 1, False), c)[2], 0)
            for d in range(npq):
                step(npq * i + d, True, seen(d), seen(d - 1))

        scatter(2 * (npq * (i + 1) - 1), seen(npq - 1))
        dq_ref[...] = dqt_scr[...].T

        @pl.when(i == s // mq - 1)
        def _():
            for a in range(0, s, 512):
                dv_ref[a:a + 512, :] = dvt_scr[:, a:a + 512].T

    qspec = pl.BlockSpec((mq, 2 * LANES), lambda p, i: (i, p))
    kspec = pl.BlockSpec((s, 2 * LANES), lambda p, i: (0, p))
    ktspec = pl.BlockSpec((2 * LANES, s), lambda p, i: (p, 0))
    vspec = pl.BlockSpec((s, LANES), lambda p, i: (0, p))
    ospec = pl.BlockSpec((mq, LANES), lambda p, i: (i, p))
    return pl.pallas_call(
        body, name="mla_bwd", grid=(4, s // mq),
        out_shape=(jax.ShapeDtypeStruct((s, 1024), F32), jax.ShapeDtypeStruct((s, 1024), F32),
                   jax.ShapeDtypeStruct((s, 512), F32)),
        in_specs=[qspec, kspec, ktspec, vspec, ospec, ospec, ospec], out_specs=(qspec, kspec, vspec),
        scratch_shapes=[pltpu.VMEM((2 * LANES, mq), F32), pltpu.VMEM((2, 2 * BK, mq), BF16), pltpu.VMEM((2, 2 * BK, mq), BF16),
                        pltpu.VMEM((LANES, s), F32)],
        compiler_params=pltpu.CompilerParams(vmem_limit_bytes=VMEM_ATTN),
    )(qc, kc, kct, v, do, lse, delta)


def _post(x, p, tgt, sbo, mlao, sbg, mlag, gsb, gmla, wout, gpost, wple, gple, wpg, bpg):
    s = x.shape[0]

    def body(x_ref, p_ref, t_ref, sbo_ref, mlao_ref, sbg_ref, mlag_ref, gsb_ref, gmla_ref, wout_ref,
             gpost_ref, wple_ref, gple_ref, wpg_ref, bpg_ref, bd_ref,
             dsbo_ref, dmlao_ref, delta_ref, dsbg_ref, dmlag_ref, dxres_ref, dwout_ref, dwpg_ref, dwple_ref, vec_ref):
        i = pl.program_id(0)

        @pl.when(i == 0)
        def _():
            dwout_ref[...] = jnp.zeros_like(dwout_ref)
            dwpg_ref[...] = jnp.zeros_like(dwpg_ref)
            dwple_ref[...] = jnp.zeros_like(dwple_ref)
            vec_ref[...] = jnp.zeros_like(vec_ref)

        inv_hd = 1.0 / HEAD_DIM

        def head_fwd(o, g, gate):
            r = lax.rsqrt(_seg(o * o, bd_ref[...]) * inv_hd + EPS)
            hat = o * r
            n = hat * g
            sg = _sigmoid(gate)
            return hat, r, n, sg, n * (gate * sg)

        sbo, mlao, sbg_v, mlag_v = sbo_ref[...], mlao_ref[...], sbg_ref[...], mlag_ref[...]
        gsb_v, gmla_v = gsb_ref[...], gmla_ref[...]
        sb_hat, sb_r, sb_n, sb_sg, sb_y = head_fwd(sbo, gsb_v, sbg_v)
        ml_hat, ml_r, ml_n, ml_sg, ml_y = head_fwd(mlao, gmla_v, mlag_v)
        mix = jnp.concatenate([sb_y, ml_y], axis=1).astype(BF16)
        y = _mm(mix, wout_ref[...])
        ry = lax.rsqrt(_rowmean(y * y) + EPS)
        y_hat = y * ry
        gpost_v = gpost_ref[...]
        x1 = x_ref[...] + y_hat * gpost_v
        pb = p_ref[...].astype(BF16)
        pl_ = _mm(pb, wple_ref[...])
        rp = lax.rsqrt(_rowmean(pl_ * pl_) + EPS)
        pl_hat = pl_ * rp
        gple_v = gple_ref[...]
        ple = pl_hat * gple_v
        x1b = x1.astype(BF16)
        gate = _sigmoid(_mm(x1b, wpg_ref[...]) + bpg_ref[...])
        err = x1 + ple * gate - t_ref[...]
        loss = 0.5 * jnp.sum(_rowmean(err * err))
        dout = err * (1.0 / D_MODEL)

        du = dout * ple * gate * (1.0 - gate)
        dub = du.astype(BF16)
        dple = dout * gate
        dx1 = dout + _mm_nt(dub, wpg_ref[...])
        dwpg_ref[...] += _mm_tn(x1b, dub)
        dplh = dple * gple_v
        dpl = rp * (dplh - pl_hat * _rowmean(dplh * pl_hat))
        dwple_ref[...] += _mm_tn(pb, dpl.astype(BF16))
        dxres_ref[...] = dx1
        dyh = dx1 * gpost_v
        dy = ry * (dyh - y_hat * _rowmean(dyh * y_hat))
        dyb = dy.astype(BF16)
        dwout_ref[...] += _mm_tn(mix, dyb)
        dmix = _mm_nt(dyb, wout_ref[...])

        def head_bwd(dyv, hat, r, n, sg, g, gate):
            dn = dyv * (gate * sg)
            dgate = dyv * n * (sg * (1.0 + gate * (1.0 - sg)))
            dhat = dn * g
            do = r * (dhat - hat * (_seg(dhat * hat, bd_ref[...]) * inv_hd))
            return do, dgate, _colsum(dn * hat)

        dsbo, dsbg, dg_sb = head_bwd(dmix[:, 0:512], sb_hat, sb_r, sb_n, sb_sg, gsb_v, sbg_v)
        dmlao, dmlag, dg_ml = head_bwd(dmix[:, 512:1024], ml_hat, ml_r, ml_n, ml_sg, gmla_v, mlag_v)
        dsbo_ref[...] = dsbo.astype(BF16)
        dmlao_ref[...] = dmlao.astype(BF16)
        delta_ref[...] = _seg(dmlao * mlao, bd_ref[...])
        dsbg_ref[...] = dsbg.astype(BF16)
        dmlag_ref[...] = dmlag.astype(BF16)
        vec_ref[pl.ds(0, 1), :] += _colsum(dx1 * y_hat)
        vec_ref[pl.ds(1, 1), :] += _colsum(dple * pl_hat)
        vec_ref[pl.ds(2, 1), :] += _colsum(du)
        vec_ref[pl.ds(3, 1), :] += jnp.concatenate([dg_sb, dg_ml], axis=1)
        vec_ref[pl.ds(4, 1), :] += jnp.full((1, D_MODEL), loss, F32)

    out_shape = (
        jax.ShapeDtypeStruct((s, 512), BF16), jax.ShapeDtypeStruct((s, 512), BF16), jax.ShapeDtypeStruct((s, 512), F32),
        jax.ShapeDtypeStruct((s, 512), BF16), jax.ShapeDtypeStruct((s, 512), BF16), jax.ShapeDtypeStruct((s, D_MODEL), F32),
        jax.ShapeDtypeStruct((D_MODEL, D_MODEL), F32), jax.ShapeDtypeStruct((D_MODEL, D_MODEL), F32),
        jax.ShapeDtypeStruct((PLE_DIM, D_MODEL), F32), jax.ShapeDtypeStruct((8, D_MODEL), F32),
    )
    return pl.pallas_call(
        body, name="post_fwd_bwd", grid=(s // TM,), out_shape=out_shape,
        in_specs=[_rows(D_MODEL), _rows(PLE_DIM), _rows(D_MODEL), _rows(512), _rows(512), _rows(512), _rows(512),
                  _full((1, 512)), _full((1, 512)), _full((D_MODEL, D_MODEL)),
                  _full((1, D_MODEL)), _full((PLE_DIM, D_MODEL)), _full((1, D_MODEL)), _full((D_MODEL, D_MODEL)),
                  _full((1, D_MODEL)), _full((1024, 512))],
        out_specs=(_rows(512), _rows(512), _rows(512), _rows(512), _rows(512), _rows(D_MODEL),
                   _acc((D_MODEL, D_MODEL)), _acc((D_MODEL, D_MODEL)), _acc((PLE_DIM, D_MODEL)), _acc((8, D_MODEL))),
        compiler_params=pltpu.CompilerParams(vmem_limit_bytes=VMEM_DENSE),
    )(x, p, tgt, sbo, mlao, sbg, mlag, gsb, gmla, wout, gpost, wple, gple, wpg, bpg, _blockdiag2(512, HEAD_DIM))


def _pre_bwd(x, dxres, dsbq, dsbk, dsbv, dsbg, dmlag, dqc, dkc, dmv, cq, ckv, tabs, gpre, win, gq, wuq, gkv, wk, wv):
    s = x.shape[0]
    c_t, sa_t, sb_t = tabs
    rw = _rows

    def body(x_ref, dxres_ref, dsbq_ref, dsbk_ref, dsbv_ref, dsbg_ref, dmlag_ref, dqc_ref, dkc_ref, dmv_ref, cq_ref,
             ckv_ref, c_ref, sa_ref, sb_ref, gpre_ref, win_ref, gq_ref, wuq_ref, gkv_ref, wk_ref, wv_ref,
             gx_ref, dwin_ref, dwuq_ref, dwk_ref, dwv_ref, vec_ref, dwin_acc):
        i = pl.program_id(0)

        @pl.when(i == 0)
        def _():
            dwin_acc[...] = jnp.zeros_like(dwin_acc)
            dwuq_ref[...] = jnp.zeros_like(dwuq_ref)
            dwk_ref[...] = jnp.zeros_like(dwk_ref)
            dwv_ref[...] = jnp.zeros_like(dwv_ref)
            vec_ref[...] = jnp.zeros_like(vec_ref)

        lane = lax.broadcasted_iota(jnp.int32, (1, LANES), 1)
        c1, sa1, sb1 = c_ref[...], sa_ref[...], sb_ref[...]
        c8, sa8, sb8 = jnp.tile(c1, (1, 8)), jnp.tile(sa1, (1, 8)), jnp.tile(sb1, (1, 8))

        def norm_bwd(dn, hat, r, g):
            t = dn * g
            return r * (t - hat * _rowmean(t * hat)), _colsum(dn * hat)

        xv = x_ref[...]
        r1 = lax.rsqrt(_rowmean(xv * xv) + EPS)
        x_hat = xv * r1
        gpre_v = gpre_ref[...]
        hb = (x_hat * gpre_v).astype(BF16)
        ready = jnp.concatenate([dsbq_ref[...], dsbk_ref[...].astype(BF16), dsbv_ref[...].astype(BF16), dsbg_ref[...]], axis=1)
        dmlag = dmlag_ref[...]
        dwin_acc[:, 0:2048] += _mm_tn(hb, ready)
        dwin_acc[:, 2560:3072] += _mm_tn(hb, dmlag)
        dh = _mm_nt(ready, win_ref[:, 0:2048]) + _mm_nt(dmlag, win_ref[:, 2560:3072])

        dqeb = _rope_bwd(dqc_ref[...], c8, sa8, sb8).astype(BF16)
        cq = cq_ref[...]
        rq = lax.rsqrt(_rowmean(cq * cq) + EPS)
        cq_hat = cq * rq
        gq_v = gq_ref[...]
        dwuq_ref[...] += _mm_tn((cq_hat * gq_v).astype(BF16), dqeb)
        dcq, dg_q = norm_bwd(_mm_nt(dqeb, wuq_ref[...]), cq_hat, rq, gq_v)

        dkc = dkc_ref[...]
        dkcb = dkc.astype(BF16)
        dmvb = dmv_ref[...].astype(BF16)
        ckv = ckv_ref[...]
        rkv = lax.rsqrt(_rowmean(ckv * ckv) + EPS)
        ckv_hat = ckv * rkv
        gkv_v = gkv_ref[...]
        ckvnb = (ckv_hat * gkv_v).astype(BF16)
        dwk_ref[...] += _mm_tn(ckvnb, dkcb)
        dwv_ref[...] += _mm_tn(ckvnb, dmvb)
        dckv, dg_kv = norm_bwd(_mm_nt(dkcb, wk_ref[...]) + _mm_nt(dmvb, wv_ref[...]), ckv_hat, rkv, gkv_v)

        dkr = dkc[:, 0:LANES]
        for hh in range(1, 8):
            dkr = dkr + dkc[:, LANES * hh:LANES * (hh + 1)]
        dkr = _rope_bwd(dkr, c1, sa1, sb1)
        dkr = jnp.where((lane >= 64) & (lane < 96), dkr, 0.0)

        late = jnp.concatenate([dcq.astype(BF16), dckv.astype(BF16), dkr.astype(BF16)], axis=1)
        dwin_acc[:, 2048:2560] += _mm_tn(hb, late)
        dx, dg_pre = norm_bwd(dh + _mm_nt(late, win_ref[:, 2048:2560]), x_hat, r1, gpre_v)
        gx_ref[...] = dxres_ref[...] + dx
        vec_ref[pl.ds(0, 1), :] += dg_pre
        vec_ref[pl.ds(1, 1), :] += jnp.concatenate([dg_q, dg_kv, jnp.zeros((1, D_MODEL - Q_LORA - KV_LORA), F32)], axis=1)

        @pl.when(i == pl.num_programs(0) - 1)
        def _():
            pltpu.sync_copy(dwin_acc, dwin_ref)

    out_shape = (
        jax.ShapeDtypeStruct((s, D_MODEL), F32), jax.ShapeDtypeStruct((D_MODEL, D_EXT), F32),
        jax.ShapeDtypeStruct((Q_LORA, 1024), F32), jax.ShapeDtypeStruct((KV_LORA, 1024), F32),
        jax.ShapeDtypeStruct((KV_LORA, 512), F32), jax.ShapeDtypeStruct((8, D_MODEL), F32),
    )
    return pl.pallas_call(
        body, name="pre_bwd", grid=(s // TM,), out_shape=out_shape,
        in_specs=[rw(D_MODEL), rw(D_MODEL), rw(512), rw(512), rw(512), rw(512), rw(512),
                  rw(1024), rw(1024), rw(512), rw(Q_LORA), rw(KV_LORA), rw(LANES), rw(LANES),
                  rw(LANES), _full((1, D_MODEL)), _full((D_MODEL, D_EXT)), _full((1, Q_LORA)), _full((Q_LORA, 1024)),
                  _full((1, KV_LORA)), _full((KV_LORA, 1024)), _full((KV_LORA, 512))],
        out_specs=(rw(D_MODEL), pl.BlockSpec(memory_space=pl.ANY), _acc((Q_LORA, 1024)), _acc((KV_LORA, 1024)),
                   _acc((KV_LORA, 512)), _acc((8, D_MODEL))),
        scratch_shapes=[pltpu.VMEM((D_MODEL, D_EXT), F32)],
        compiler_params=pltpu.CompilerParams(vmem_limit_bytes=VMEM_DENSE),
    )(x, dxres, dsbq, dsbk, dsbv, dsbg, dmlag, dqc, dkc, dmv, cq, ckv, c_t, sa_t, sb_t, gpre, win, gq, wuq, gkv, wk, wv)


def _place():
    return lax.axis_index("x"), lax.axis_index("y"), lax.axis_index("c")


def _gather_steps(shapes, ins, bufs, send_sems, recv_sems):
    n = len(shapes)
    x, y, c = _place()
    me, sib = (x, y, c), (x, y, 1 - c)
    chips = [(1 - x, y), (x, 1 - y), (1 - x, 1 - y)]

    def half(t, chip, hc):
        rows = shapes[t][0] // 2
        return bufs[t].at[2 * chip[0] + chip[1], pl.ds(pl.multiple_of(hc * rows, 16), rows), :]

    def copy(k, t, chip, hc, to):
        return pltpu.make_async_remote_copy(src_ref=half(t, chip, hc), dst_ref=half(t, chip, hc), send_sem=send_sems.at[k],
                                            recv_sem=recv_sems.at[k], device_id=to, device_id_type=MESH)

    def start():
        for t in range(n):
            bufs[t][2 * x + y] = ins[t][...].astype(BF16)
            for j, chip in enumerate(chips):
                copy(6 * t + j, t, (x, y), c, (*chip, c)).start()

    def forward():
        for t in range(n):
            for j, chip in enumerate(chips):
                copy(6 * t + j, t, chip, c, me).wait_recv()
                copy(6 * t + 3 + j, t, chip, c, sib).start()

    def finish():
        for t in range(n):
            for j, chip in enumerate(chips):
                copy(6 * t + 3 + j, t, chip, 1 - c, me).wait_recv()
        for t in range(n):
            for j, chip in enumerate(chips):
                copy(6 * t + j, t, (x, y), c, (*chip, c)).wait_send()
                copy(6 * t + 3 + j, t, chip, c, sib).wait_send()

    return start, forward, finish


def _allgather_weights(shards):
    n = len(shards)

    def body(*refs):
        start, forward, finish = _gather_steps([a.shape for a in shards], refs[:n], refs[n:2 * n], refs[2 * n], refs[2 * n + 1])
        start()
        forward()
        finish()

    return pl.pallas_call(
        body, name="allgather_weights",
        out_shape=tuple(jax.ShapeDtypeStruct((N_SHARD,) + a.shape, BF16) for a in shards),
        in_specs=[pl.BlockSpec(memory_space=pltpu.VMEM)] * n, out_specs=(pl.BlockSpec(memory_space=pltpu.VMEM),) * n,
        scratch_shapes=[pltpu.SemaphoreType.DMA((6 * n,)), pltpu.SemaphoreType.DMA((6 * n,))],
        compiler_params=pltpu.CompilerParams(vmem_limit_bytes=VMEM_ATTN),
    )(*shards)


def _reduce_scratch(gsh):
    n = len(gsh)
    half_shapes = [(N_SHARD, a.shape[1] // 2, a.shape[2]) for a in gsh]
    return ([pltpu.VMEM(s_, F32) for s_ in half_shapes] * 2 + [pltpu.VMEM(s_, BF16) for s_ in half_shapes] * 2
            + [pltpu.SemaphoreType.DMA((n,)), pltpu.SemaphoreType.DMA((5 * n,)), pltpu.SemaphoreType.DMA((5 * n,))])


def _reduce_steps(halves, g_refs, f_refs, scratch):
    n = len(halves)
    accs, sibs, sbufs, rbufs = scratch[0:n], scratch[n:2 * n], scratch[2 * n:3 * n], scratch[3 * n:4 * n]
    local_sems, send_sems, recv_sems = scratch[4 * n:4 * n + 3]
    x, y, c = _place()
    me, sib = (x, y, c), (x, y, 1 - c)
    mine = 2 * x + y
    chips = [(1 - x, y), (x, 1 - y), (1 - x, 1 - y)]

    def remote(k, src, dst, to):
        return pltpu.make_async_remote_copy(src_ref=src, dst_ref=dst, send_sem=send_sems.at[k], recv_sem=recv_sems.at[k],
                                            device_id=to, device_id_type=MESH)

    def half3(ref, t, hc):
        return ref.at[:, pl.ds(pl.multiple_of(hc * halves[t], 8), halves[t]), :]

    def half2(ref, t, hc):
        return ref.at[pl.ds(pl.multiple_of(hc * halves[t], 8), halves[t]), :]

    def mine_load(t):
        return pltpu.make_async_copy(half3(g_refs[t], t, c), accs[t], local_sems.at[t])

    def to_sibling(t, to):
        return remote(t, half3(g_refs[t], t, 1 - c), sibs[t], to)

    def to_chip(t, j, chip, to):
        idx = 2 * chip[0] + chip[1]
        return remote(n + 3 * t + j, sbufs[t].at[idx], rbufs[t].at[mine if to is not me else idx], to)

    def swap(t, hc, to):
        return remote(4 * n + t, half2(f_refs[t], t, hc), half2(f_refs[t], t, hc), to)

    def load():
        for t in range(n):
            mine_load(t).start()
            to_sibling(t, sib).start()

    def partial():
        for t in range(n):
            mine_load(t).wait()
            to_sibling(t, me).wait_recv()
            for k in range(N_SHARD):
                accs[t][k] = accs[t][k] + sibs[t][k]
            for j, chip in enumerate(chips):
                idx = 2 * chip[0] + chip[1]
                sbufs[t][idx] = accs[t][idx].astype(BF16)
                to_chip(t, j, chip, (*chip, c)).start()

    def total():
        for t in range(n):
            acc = accs[t][mine]
            for j, chip in enumerate(chips):
                to_chip(t, j, chip, me).wait_recv()
                acc = acc + rbufs[t][2 * chip[0] + chip[1]].astype(F32)
            half2(f_refs[t], t, c)[...] = acc
            swap(t, c, sib).start()

    def finish():
        for t in range(n):
            swap(t, 1 - c, me).wait_recv()
        for t in range(n):
            to_sibling(t, sib).wait_send()
            for j, chip in enumerate(chips):
                to_chip(t, j, chip, (*chip, c)).wait_send()
            swap(t, c, sib).wait_send()

    return load, partial, total, finish


def _reduce_scatter_grads(gsh, vec):
    n = len(gsh)
    halves = [a.shape[1] // 2 for a in gsh]

    def body(*refs):
        g_refs, vec_ref, f_refs, vsum_ref = refs[:n], refs[n], refs[n + 1:2 * n + 1], refs[2 * n + 1]
        scratch = refs[2 * n + 2:]
        vrecv, vsend_sems, vrecv_sems = scratch[4 * n + 3:]
        load, partial, total, finish = _reduce_steps(halves, g_refs, f_refs, scratch)
        x, y, c = _place()
        my_dev = 4 * x + 2 * y + c

        def flip(k):
            return x ^ ((k >> 2) & 1), y ^ ((k >> 1) & 1), c ^ (k & 1)

        def vcopy(k, slot, to):
            return pltpu.make_async_remote_copy(src_ref=vec_ref, dst_ref=vrecv.at[slot], send_sem=vsend_sems.at[k - 1],
                                                recv_sem=vrecv_sems.at[k - 1], device_id=to, device_id_type=MESH)

        load()
        vrecv[my_dev] = vec_ref[...]
        for k in range(1, 8):
            vcopy(k, my_dev, flip(k)).start()
        partial()
        total()
        finish()
        for k in range(1, 8):
            fx, fy, fc = flip(k)
            vcopy(k, 4 * fx + 2 * fy + fc, (x, y, c)).wait_recv()
        vs = vrecv[0]
        for d in range(1, 8):
            vs = vs + vrecv[d]
        vsum_ref[...] = vs
        for k in range(1, 8):
            vcopy(k, my_dev, flip(k)).wait_send()

    return pl.pallas_call(
        body, name="reduce_scatter_grads",
        out_shape=tuple(jax.ShapeDtypeStruct(a.shape[1:], F32) for a in gsh) + (jax.ShapeDtypeStruct((VEC_ROWS, 1024), F32),),
        in_specs=[pl.BlockSpec(memory_space=pl.ANY)] * n + [pl.BlockSpec(memory_space=pltpu.VMEM)],
        out_specs=(pl.BlockSpec(memory_space=pltpu.VMEM),) * (n + 1),
        scratch_shapes=_reduce_scratch(gsh) + [pltpu.VMEM((8, VEC_ROWS, 1024), F32), pltpu.SemaphoreType.DMA((7,)),
                                               pltpu.SemaphoreType.DMA((7,))],
        compiler_params=pltpu.CompilerParams(vmem_limit_bytes=56 * 1024 * 1024),
    )(*gsh, vec)


def _adamw(w, g, m, v):
    rows, cols = w.shape
    tr = rows if rows <= 256 else 256
    flip = cols % LANES != 0

    def body(w_ref, g_ref, m_ref, v_ref, g_out, d_ref, nm_ref, nv_ref):
        gv = g_ref[...].T if flip else g_ref[...]
        outs = (gv,) + _adam_math(w_ref[...], gv, m_ref[...], v_ref[...])
        for ref, val in zip((g_out, d_ref, nm_ref, nv_ref), outs):
            ref[...] = val

    spec = pl.BlockSpec((tr, cols), lambda i: (i, 0))
    tspec = pl.BlockSpec((cols, tr), lambda i: (0, i)) if flip else spec
    shp = jax.ShapeDtypeStruct((cols, rows) if flip else (rows, cols), F32)
    if flip:
        w, m, v = w.T, m.T, v.T
    outs = pl.pallas_call(body, name="adamw", grid=(rows // tr,), out_shape=(shp,) * 4,
                          in_specs=[tspec, spec, tspec, tspec], out_specs=(tspec,) * 4)(w, g, m, v)
    return tuple(o.T for o in outs) if flip else outs


def _adam_math(w, g, m, v):
    m2 = ADAM_B1 * m + (1.0 - ADAM_B1) * g
    v2 = ADAM_B2 * v + (1.0 - ADAM_B2) * (g * g)
    m_hat = m2 / (1.0 - ADAM_B1 ** ADAM_STEP)
    v_hat = v2 / (1.0 - ADAM_B2 ** ADAM_STEP)
    return -ADAM_LR * (m_hat / (jnp.sqrt(v_hat) + ADAM_EPS) + ADAM_WD * w), m2, v2


def _adamw_small(vsum, w, m, v):
    names = [name for name, _, _, _ in _VEC_LAYOUT]
    k = len(names)

    def body(*refs):
        vs_ref, w_refs, m_refs, v_refs = refs[0], refs[1:1 + k], refs[1 + k:1 + 2 * k], refs[1 + 2 * k:1 + 3 * k]
        outs = refs[1 + 3 * k:]
        for idx, (_, r, c0, width) in enumerate(_VEC_LAYOUT):
            gv = vs_ref[pl.ds(r, 1), pl.ds(c0, width)]
            d, m2, v2 = _adam_math(w_refs[idx][...], gv, m_refs[idx][...], v_refs[idx][...])
            outs[idx][...], outs[k + idx][...], outs[2 * k + idx][...], outs[3 * k + idx][...] = gv, d, m2, v2

    shapes = tuple(jax.ShapeDtypeStruct(w[name].shape, F32) for name in names)
    res = pl.pallas_call(
        body, name="adamw_small", out_shape=shapes * 4,
        in_specs=[pl.BlockSpec(memory_space=pltpu.VMEM)] * (1 + 3 * k), out_specs=(pl.BlockSpec(memory_space=pltpu.VMEM),) * (4 * k),
    )(vsum, *[w[name] for name in names], *[m[name] for name in names], *[v[name] for name in names])
    return tuple({name: res[part * k + idx] for idx, name in enumerate(names)} for part in range(4))


_EARLY = ("w_in", "w_uq", "w_ukv")
_LATE = ("w_out", "w_ple", "w_ple_gate")
_BIG = _EARLY + _LATE
_KR_LOCAL = 2432 - 3 * (D_IN // N_SHARD)


def _extend_early(parts):
    cols = lambda a: a.transpose(1, 0, 2).reshape(a.shape[1], N_SHARD * a.shape[2])
    g = parts["w_in"]
    zeros = lambda n: jnp.zeros((D_MODEL, n), g.dtype)
    win_ext = jnp.concatenate([g[0], g[1], g[2], g[3][:, :_KR_LOCAL], zeros(64), g[3][:, _KR_LOCAL:_KR_LOCAL + QK_ROPE],
                               zeros(32), g[3][:, _KR_LOCAL + QK_ROPE:]], axis=1)
    wuq_ext = jnp.pad(cols(parts["w_uq"]).reshape(Q_LORA, 8, 96), ((0, 0), (0, 0), (0, 32))).reshape(Q_LORA, 1024)
    wukv = cols(parts["w_ukv"]).reshape(KV_LORA, 8, 128)
    wk_ext = jnp.pad(wukv[:, :, :64], ((0, 0), (0, 0), (0, 64))).reshape(KV_LORA, 1024)
    wv = wukv[:, :, 64:].reshape(KV_LORA, 512)
    return win_ext, wuq_ext, wk_ext, wv


def _shard_cols(a):
    return a.reshape(a.shape[0], N_SHARD, a.shape[1] // N_SHARD).transpose(1, 0, 2)


def _shard_rows(a):
    return a.reshape(N_SHARD, a.shape[0] // N_SHARD, a.shape[1])


def _shard_early_grads(dwin_ext, dwuq_ext, dwk_ext, dwv):
    e, w = dwin_ext, D_IN // N_SHARD
    last = jnp.concatenate([e[:, 3 * w:2432], e[:, 2496:2528], e[:, 2560:]], axis=1)
    dwuq = dwuq_ext.reshape(Q_LORA, 8, 128)[:, :, :96].reshape(Q_LORA, 768)
    dwukv = jnp.concatenate([dwk_ext.reshape(KV_LORA, 8, 128)[:, :, :64], dwv.reshape(KV_LORA, 8, 64)], axis=2)
    return [jnp.stack([e[:, 0:w], e[:, w:2 * w], e[:, 2 * w:3 * w], last]), _shard_cols(dwuq),
            _shard_cols(dwukv.reshape(KV_LORA, 1024))]


def _rope_tables(positions):
    half = QK_ROPE // 2
    freq = ROPE_THETA ** (-jnp.arange(half, dtype=F32) / half)
    s = positions.shape[0]
    per = LANES // half
    ang = jnp.repeat(positions.astype(F32).reshape(s // per, per), half, axis=1) * jnp.tile(freq, per)
    cos, sin = lax.optimization_barrier((jnp.cos(ang), jnp.sin(ang)))
    cos, sin = cos.reshape(s, half), sin.reshape(s, half)
    z = lambda n: jnp.zeros((s, n), F32)
    c_t = jnp.concatenate([jnp.ones((s, 64), F32), cos, cos, z(32)], axis=1)
    sa_t = jnp.concatenate([z(64), -sin, z(16), z(32)], axis=1)
    sb_t = jnp.concatenate([z(64), z(16), sin, z(32)], axis=1)
    return c_t, sa_t, sb_t


def _local_grads(x, p, positions, tgt, gains, early, late):
    win_ext, wuq_ext, wk_ext, wv = _extend_early(early)
    tabs = _rope_tables(positions)
    g = gains
    sbq, sbk, sbv, sbg, mlag, cq, ckv, qc, kc, mv, sbkt, sbvt, kct, mvt = _pre_fwd(
        x, tabs, g["norm_pre_g"], win_ext, g["q_norm_g"], wuq_ext, g["kv_norm_g"], wk_ext, wv)
    sbo, wout4, wple4, wpg4 = _sb_fwd(sbq, sbk, sbvt, late)
    wout, wpg = wout4.reshape(D_MODEL, D_MODEL), wpg4.reshape(D_MODEL, D_MODEL)
    wple = wple4.transpose(1, 0, 2).reshape(PLE_DIM, D_MODEL)
    mlao, lse = _mla_fwd(qc, kc, mvt)
    dsbo, dmlao, delta, dsbg, dmlag, dxres, dwout, dwpg, dwple, vec_c = _post(
        x, p, tgt, sbo, mlao, sbg, mlag, g["sb_out_norm_g"], g["mla_out_norm_g"], wout, g["norm_post_g"], wple,
        g["ple_norm_g"], wpg, g["b_ple_gate"])
    dsbq, dsbk, dsbv, *late_grads = _sb_bwd(sbq, sbk, sbkt, sbv, dsbo, [_shard_rows(dwout), _shard_cols(dwple), _shard_rows(dwpg)])
    dqc, dkc, dmv = _mla_bwd(qc, kc, kct, mv, dmlao, lse, delta)
    gx, dwin_ext, dwuq_ext, dwk_ext, dwv, vec_d = _pre_bwd(
        x, dxres, dsbq, dsbk, dsbv, dsbg, dmlag, dqc, dkc, dmv, cq, ckv, tabs, g["norm_pre_g"], win_ext, g["q_norm_g"],
        wuq_ext, g["kv_norm_g"], wk_ext, wv)
    return gx, _shard_early_grads(dwin_ext, dwuq_ext, dwk_ext, dwv), late_grads, jnp.concatenate([vec_c, vec_d], axis=0)


_VEC_LAYOUT = (("norm_post_g", 0, 0, 1024), ("ple_norm_g", 1, 0, 1024), ("b_ple_gate", 2, 0, 1024), ("sb_out_norm_g", 3, 0, 512),
               ("mla_out_norm_g", 3, 512, 512), ("norm_pre_g", 8, 0, 1024), ("q_norm_g", 9, 0, 256), ("kv_norm_g", 9, 256, 128))
_LOSS_ROW = 4
_WEIGHT_ORDER = ("norm_pre_g", "w_in", "q_norm_g", "w_uq", "kv_norm_g", "w_ukv", "sb_out_norm_g", "mla_out_norm_g", "w_out",
                 "norm_post_g", "w_ple", "ple_norm_g", "w_ple_gate", "b_ple_gate")


def kernel(x, p, positions, norm_pre_g, w_in, q_norm_g, w_uq, kv_norm_g, w_ukv, sb_out_norm_g, mla_out_norm_g, w_out, norm_post_g, w_ple, ple_norm_g, w_ple_gate, b_ple_gate, loss_target, m_norm_pre_g, m_w_in, m_q_norm_g, m_w_uq, m_kv_norm_g, m_w_ukv, m_sb_out_norm_g, m_mla_out_norm_g, m_w_out, m_norm_post_g, m_w_ple, m_ple_norm_g, m_w_ple_gate, m_b_ple_gate, v_norm_pre_g, v_w_in, v_q_norm_g, v_w_uq, v_kv_norm_g, v_w_ukv, v_sb_out_norm_g, v_mla_out_norm_g, v_w_out, v_norm_post_g, v_w_ple, v_ple_norm_g, v_w_ple_gate, v_b_ple_gate):
    w = {"norm_pre_g": norm_pre_g, "w_in": w_in[0], "q_norm_g": q_norm_g, "w_uq": w_uq[0], "kv_norm_g": kv_norm_g, "w_ukv": w_ukv[0],
         "sb_out_norm_g": sb_out_norm_g, "mla_out_norm_g": mla_out_norm_g, "w_out": w_out[0], "norm_post_g": norm_post_g,
         "w_ple": w_ple[0], "ple_norm_g": ple_norm_g, "w_ple_gate": w_ple_gate[0], "b_ple_gate": b_ple_gate}
    m = {"norm_pre_g": m_norm_pre_g, "w_in": m_w_in[0], "q_norm_g": m_q_norm_g, "w_uq": m_w_uq[0], "kv_norm_g": m_kv_norm_g,
         "w_ukv": m_w_ukv[0], "sb_out_norm_g": m_sb_out_norm_g, "mla_out_norm_g": m_mla_out_norm_g, "w_out": m_w_out[0],
         "norm_post_g": m_norm_post_g, "w_ple": m_w_ple[0], "ple_norm_g": m_ple_norm_g, "w_ple_gate": m_w_ple_gate[0],
         "b_ple_gate": m_b_ple_gate}
    v = {"norm_pre_g": v_norm_pre_g, "w_in": v_w_in[0], "q_norm_g": v_q_norm_g, "w_uq": v_w_uq[0], "kv_norm_g": v_kv_norm_g,
         "w_ukv": v_w_ukv[0], "sb_out_norm_g": v_sb_out_norm_g, "mla_out_norm_g": v_mla_out_norm_g, "w_out": v_w_out[0],
         "norm_post_g": v_norm_post_g, "w_ple": v_w_ple[0], "ple_norm_g": v_ple_norm_g, "w_ple_gate": v_w_ple_gate[0],
         "b_ple_gate": v_b_ple_gate}
    gathered = _allgather_weights([w[n] for n in _EARLY])
    gx, early_grads, late_red, vec = _local_grads(x[0], p[0, 0], positions[0], loss_target[0], w, dict(zip(_EARLY, gathered)),
                                                  [w[n] for n in _LATE])
    *early_red, vsum = _reduce_scatter_grads(early_grads, vec)
    gred = early_red + late_red
    loss = vsum[_LOSS_ROW, 0]

    g, delta, new_m, new_v = _adamw_small(vsum, w, m, v)
    for n, gn in zip(_BIG, gred):
        g[n], delta[n], new_m[n], new_v[n] = _adamw(w[n], gn, m[n], v[n])

    lead = lambda n, a: a[None] if n in _BIG else a
    return (loss, gx[None],
            *[lead(n, g[n]) for n in _WEIGHT_ORDER], *[lead(n, delta[n]) for n in _WEIGHT_ORDER],
            *[lead(n, new_m[n]) for n in _WEIGHT_ORDER], *[lead(n, new_v[n]) for n in _WEIGHT_ORDER])
```

```python
import numpy as np
import jax
import jax.numpy as jnp
from jax import lax
from jax.experimental import pallas as pl
from jax.experimental.pallas import tpu as pltpu

F32 = jnp.float32
BF16 = jnp.bfloat16
MESH = pl.DeviceIdType.MESH

D_MODEL = 1024
HEAD_DIM = 64
D_SB = 512
D_MLA = 512
Q_LORA = 256
KV_LORA = 128
QK_NOPE = 64
QK_ROPE = 32
PLE_DIM = 256
D_IN = 2976
D_EXT = 3072
ROPE_THETA = 10000.0
EPS = 1e-6
N_SHARD = 4

ADAM_LR = 0.001
ADAM_B1 = 0.9
ADAM_B2 = 0.999
ADAM_EPS = 1e-08
ADAM_WD = 0.01
ADAM_STEP = 10

LANES = 128
BK = 128
WQ = 256
SB_NSUB = 2
MQ_FWD = 4096
MQ_BWD = 1024
MLA_CW = 256
SB_CUTOFF = 120.0
TM = 256
TM_PRE = 256
VEC_ROWS = 16
VMEM_DENSE = 52 * 1024 * 1024
VMEM_ATTN = 40 * 1024 * 1024


def _mm(a, b):
    return jnp.dot(a, b, preferred_element_type=F32)


def _mm_nt(a, b):
    return lax.dot_general(a, b, (((1,), (1,)), ((), ())), preferred_element_type=F32)


def _mm_tn(a, b):
    return lax.dot_general(a, b, (((0,), (0,)), ((), ())), preferred_element_type=F32)


def _seg(a, bd2):
    return _mm(_split2(a), bd2)


def _const(mask):
    return jnp.asarray(np.asarray(mask, np.float32), dtype=BF16)


def _blockdiag2(n, seg):
    r = (np.arange(2 * n)[:, None] % n) // seg
    c = np.arange(n)[None, :] // seg
    return _const(r == c)


def _sigmoid(a):
    return 1.0 / (1.0 + jnp.exp(-a))


def _rowmean(a):
    return jnp.mean(a, axis=-1, keepdims=True)


def _colsum(a):
    return jnp.sum(a, axis=0, keepdims=True)


def _rope_fwd(a, c, sa, sb):
    w = a.shape[-1]
    return a * c + pltpu.roll(a, w - 16, 1) * sa + pltpu.roll(a, 16, 1) * sb


def _rope_bwd(g, c, sa, sb):
    w = g.shape[-1]
    return g * c + pltpu.roll(g * sa, 16, 1) + pltpu.roll(g * sb, w - 16, 1)


def _full(shape):
    return pl.BlockSpec(shape, lambda *_: (0,) * len(shape))


def _acc(shape):
    return pl.BlockSpec(shape, lambda *_: (0,) * len(shape))


def _full2(shape):
    return pl.BlockSpec(shape, lambda p, i: (0, 0))


def _cols(height, tm=TM):
    return pl.BlockSpec((height, tm), lambda i: (0, i))


def _rows(width, tm=TM):
    return pl.BlockSpec((tm, width), lambda i: (i, 0))


def _pre_fwd(x, tabs, gpre, win, gq, wuq, gkv, wk, wv):
    s = x.shape[0]
    c_t, sa_t, sb_t = tabs
    rw, cl = (lambda width: _rows(width, TM_PRE)), (lambda height: _cols(height, TM_PRE))

    def body(x_ref, c_ref, sa_ref, sb_ref, gpre_ref, win_ref, gq_ref, wuq_ref, gkv_ref, wk_ref, wv_ref,
             sbq_ref, sbk_ref, sbv_ref, sbg_ref, mlag_ref, cq_ref, ckv_ref, qc_ref, kc_ref, mv_ref,
             sbkt_ref, sbvt_ref, kct_ref, mvt_ref):
        xv = x_ref[...]
        r1 = lax.rsqrt(_rowmean(xv * xv) + EPS)
        h = (xv * r1 * gpre_ref[...]).astype(BF16)
        proj = _mm(h, win_ref[...])
        sbq_ref[...] = proj[:, 0:512].astype(BF16)
        sbk_ref[...] = proj[:, 512:1024].astype(BF16)
        sbv_ref[...] = proj[:, 1024:1536].astype(BF16)
        sbkt_ref[...] = proj[:, 512:1024].T.astype(BF16)
        sbvt_ref[...] = proj[:, 1024:1536].T.astype(BF16)
        sbg_ref[...] = proj[:, 1536:2048]
        cq = proj[:, 2048:2304]
        ckv = proj[:, 2304:2432]
        kr = proj[:, 2432:2560]
        mlag_ref[...] = proj[:, 2560:3072]
        cq_ref[...] = cq
        ckv_ref[...] = ckv
        c1, sa1, sb1 = c_ref[...], sa_ref[...], sb_ref[...]
        c8, sa8, sb8 = jnp.tile(c1, (1, 8)), jnp.tile(sa1, (1, 8)), jnp.tile(sb1, (1, 8))
        cqn = (cq * lax.rsqrt(_rowmean(cq * cq) + EPS) * gq_ref[...]).astype(BF16)
        qe = _mm(cqn, wuq_ref[...])
        qc_ref[...] = _rope_fwd(qe, c8, sa8, sb8).astype(BF16)
        ckvn = (ckv * lax.rsqrt(_rowmean(ckv * ckv) + EPS) * gkv_ref[...]).astype(BF16)
        ke = _mm(ckvn, wk_ref[...])
        krr = _rope_fwd(kr, c1, sa1, sb1)
        kcat = ke + jnp.tile(krr, (1, 8))
        kc_ref[...] = kcat.astype(BF16)
        kct_ref[...] = kcat.T.astype(BF16)
        mval = _mm(ckvn, wv_ref[...])
        mv_ref[...] = mval.astype(BF16)
        mvt_ref[...] = mval.T.astype(BF16)

    out_shape = (
        jax.ShapeDtypeStruct((s, 512), BF16), jax.ShapeDtypeStruct((s, 512), BF16), jax.ShapeDtypeStruct((s, 512), BF16),
        jax.ShapeDtypeStruct((s, 512), F32), jax.ShapeDtypeStruct((s, 512), F32),
        jax.ShapeDtypeStruct((s, Q_LORA), F32), jax.ShapeDtypeStruct((s, KV_LORA), F32),
        jax.ShapeDtypeStruct((s, 1024), BF16), jax.ShapeDtypeStruct((s, 1024), BF16), jax.ShapeDtypeStruct((s, 512), BF16),
        jax.ShapeDtypeStruct((512, s), BF16), jax.ShapeDtypeStruct((512, s), BF16), jax.ShapeDtypeStruct((1024, s), BF16),
        jax.ShapeDtypeStruct((512, s), BF16),
    )
    return pl.pallas_call(
        body, name="pre_fwd", grid=(s // TM_PRE,), out_shape=out_shape,
        in_specs=[rw(D_MODEL), rw(LANES), rw(LANES), rw(LANES), _full((1, D_MODEL)), _full((D_MODEL, D_EXT)),
                  _full((1, Q_LORA)), _full((Q_LORA, 1024)), _full((1, KV_LORA)), _full((KV_LORA, 1024)), _full((KV_LORA, 512))],
        out_specs=(rw(512), rw(512), rw(512), rw(512), rw(512), rw(Q_LORA), rw(KV_LORA),
                   rw(1024), rw(1024), rw(512), cl(512), cl(512), cl(1024), cl(512)),
        compiler_params=pltpu.CompilerParams(vmem_limit_bytes=VMEM_DENSE),
    )(x, c_t, sa_t, sb_t, gpre, win, gq, wuq, gkv, wk, wv)


def _softplus(z):
    neg_abs = lax.bitcast_convert_type(lax.bitcast_convert_type(z, jnp.uint32) | jnp.uint32(0x80000000), F32)
    return jnp.maximum(z, 0.0) + jnp.log(1.0 + jnp.exp(neg_abs))


def _sum_matrix(kind, terms):
    r, c = np.arange(2 * BK)[:, None], np.arange(2 * BK * terms)[None, :] % (2 * BK)
    rk, ck = r % BK, c % BK
    return _const(((r // BK) == (c // BK)) & {"suffix": ck >= rk, "prefix": ck <= rk}[kind])


def _split_rows(a):
    hi = a.astype(BF16)
    return jnp.concatenate([hi, (a - hi.astype(F32)).astype(BF16)], axis=0)


def _heads_t(blk, rowi):
    zero = jnp.zeros_like(blk)
    return jnp.concatenate([jnp.where(rowi < 64, blk, zero), jnp.where(rowi >= 64, blk, zero)], axis=1)


def _mask_keys(a, valid, fill=0.0):
    return jnp.concatenate([jnp.where(valid, a[0:BK], fill), jnp.where(valid, a[BK:2 * BK], fill)], axis=0)


def _split2(a):
    hi = a.astype(BF16)
    lo = (a - hi.astype(F32)).astype(BF16)
    return jnp.concatenate([hi, lo], axis=1)


def _pair_stack(b, lane):
    zero = jnp.zeros_like(b)
    return jnp.concatenate([jnp.where(lane < 64, b, zero), jnp.where(lane >= 64, b, zero)], axis=0)


def _sb_fwd(q, k, vt, late):
    s = q.shape[0]
    n = len(late)

    def body(q_ref, k_ref, vt_ref, usuf_ref, *rest):
        ins, o_ref, outs = rest[:n], rest[n], rest[n + 1:2 * n + 1]
        acc_scr, run_scr = rest[2 * n + 1:2 * n + 3]
        bufs, (send_sems, recv_sems, out_sems) = rest[2 * n + 3:3 * n + 3], rest[3 * n + 3:]
        p, i = pl.program_id(0), pl.program_id(1)
        gather_start, gather_forward, gather_finish = _gather_steps([a.shape for a in late], ins, bufs, send_sems, recv_sems)

        @pl.when((p == 0) & (i == 0))
        def _():
            gather_start()

        @pl.when((p == 2) & (i == 0))
        def _():
            gather_forward()

        lane = lax.broadcasted_iota(jnp.int32, (1, LANES), 1)
        rowi = lax.broadcasted_iota(jnp.int32, (LANES, 1), 0)
        keyi = lax.broadcasted_iota(jnp.int32, (BK, WQ), 0)

        def group(i, qs, blocks, masked, seen=None):
            seen = seen or [0] * len(blocks)
            qryi = lax.broadcasted_iota(jnp.int32, (BK, WQ), 1) + i * WQ
            starts = [pl.multiple_of(j * BK, BK) for j in blocks]
            valid = [(keyi[:, lo:] + j * BK) < qryi[:, lo:] if m else None for j, m, lo in zip(blocks, masked, seen)]
            zs = [_mm_nt(_pair_stack(k_ref[pl.ds(ks, BK), :], lane), qs[lo:]) for ks, lo in zip(starts, seen)]
            sps = [_softplus(z) for z in zs]
            sps = [sp if ok is None else _mask_keys(sp, ok) for sp, ok in zip(sps, valid)]
            cums = [_mm(usuf_ref[...], _split_rows(sp)) for sp in sps]
            ws = [jnp.exp(z - c) for z, c in zip(zs, cums)]
            ws = [w if ok is None else _mask_keys(w, ok) for w, ok in zip(ws, valid)]
            pvs = [_mm(_heads_t(vt_ref[:, pl.ds(ks, BK)], rowi), w.astype(BF16)) for ks, w in zip(starts, ws)]
            for pv, c, lo in zip(pvs, cums, seen):
                r0, r1 = run_scr[0:1, lo:], run_scr[1:2, lo:]
                acc_scr[:, lo:] += jnp.where(rowi < 64, jnp.exp(-r0), jnp.exp(-r1)) * pv
                run_scr[0:1, lo:] = r0 + c[0:1]
                run_scr[1:2, lo:] = r1 + c[BK:BK + 1]

        assert WQ == 2 * BK

        def unfinished():
            return (jnp.min(run_scr[0:2, :]) < SB_CUTOFF).astype(jnp.int32)

        def query_block(sub):
            i = pl.program_id(1) * SB_NSUB + sub
            rows = pl.ds(pl.multiple_of(sub * WQ, WQ), WQ)
            qs = q_ref[rows, :] * (HEAD_DIM ** -0.5)
            acc_scr[...] = jnp.zeros_like(acc_scr)
            run_scr[...] = jnp.zeros_like(run_scr)

            @pl.when(i == 0)
            def _():
                group(i, qs, [1, 0], [True, True], [BK, 0])

            @pl.when(i > 0)
            def _():
                group(i, qs, [2 * i + 1, 2 * i, 2 * i - 1, 2 * i - 2], [True, True, False, False], [BK, 0, 0, 0])

            def step(c):
                group(i, qs, [2 * i - 1 - 2 * c[0], 2 * i - 2 - 2 * c[0]], [False, False])
                return c[0] + 1, unfinished()

            lax.while_loop(lambda c: (c[0] < i) & (c[1] > 0), step, (jnp.int32(1), unfinished()))
            o_ref[rows, :] = acc_scr[...].T

        lax.fori_loop(0, SB_NSUB, lambda sub, c: (query_block(sub), c)[1], 0)

        @pl.when((p == pl.num_programs(0) - 1) & (i == pl.num_programs(1) - 1))
        def _():
            gather_finish()
            copies = [pltpu.make_async_copy(bufs[t], outs[t], out_sems.at[t]) for t in range(n)]
            for cp in copies:
                cp.start()
            for cp in copies:
                cp.wait()

    qspec = pl.BlockSpec((SB_NSUB * WQ, LANES), lambda p, i: (i, p))
    kspec = pl.BlockSpec((s, LANES), lambda p, i: (0, p))
    tspec = pl.BlockSpec((LANES, s), lambda p, i: (p, 0))
    gathered = [jax.ShapeDtypeStruct((N_SHARD,) + a.shape, BF16) for a in late]
    return pl.pallas_call(
        body, name="sb_fwd", grid=(4, s // (SB_NSUB * WQ)),
        out_shape=(jax.ShapeDtypeStruct((s, 512), F32), *gathered),
        in_specs=[qspec, kspec, tspec, _full2((2 * BK, 4 * BK))] + [_full2(a.shape) for a in late],
        out_specs=(qspec,) + (pl.BlockSpec(memory_space=pl.ANY),) * n,
        scratch_shapes=[pltpu.VMEM((LANES, WQ), F32), pltpu.VMEM((8, WQ), F32)] + [pltpu.VMEM(g.shape, BF16) for g in gathered]
                       + [pltpu.SemaphoreType.DMA((6 * n,)), pltpu.SemaphoreType.DMA((6 * n,)), pltpu.SemaphoreType.DMA((n,))],
        compiler_params=pltpu.CompilerParams(vmem_limit_bytes=VMEM_ATTN),
    )(q, k, vt, _sum_matrix("suffix", 2), *late)


def _sb_bwd(q, k, kt, v, do, late):
    s = q.shape[0]
    n = len(late)
    halves = [a.shape[1] // 2 for a in late]

    def body(q_ref, k_ref, kt_ref, v_ref, do_ref, usuf_ref, upre_ref, *rest):
        g_refs, (dq_ref, dk_ref, dv_ref), outs = rest[:n], rest[n:n + 3], rest[n + 3:2 * n + 3]
        later_scr, dqt_scr, st_scr = rest[2 * n + 3:2 * n + 6]
        f_scr, reduce_scr, out_sems = rest[2 * n + 6:3 * n + 6], rest[3 * n + 6:-1], rest[-1]
        p, i = pl.program_id(0), pl.program_id(1)
        reduce_load, reduce_partial, reduce_total, reduce_finish = _reduce_steps(halves, g_refs, f_scr, reduce_scr)

        @pl.when((p == 0) & (i == 0))
        def _():
            reduce_load()

        @pl.when((p == 1) & (i == 0))
        def _():
            reduce_partial()

        @pl.when((p == 3) & (i == 0))
        def _():
            reduce_total()

        @pl.when(i == 0)
        def _():
            dk_ref[...] = jnp.zeros_like(dk_ref)
            dv_ref[...] = jnp.zeros_like(dv_ref)

        lane = lax.broadcasted_iota(jnp.int32, (1, LANES), 1)
        rowi = lax.broadcasted_iota(jnp.int32, (LANES, 1), 0)
        keyi = lax.broadcasted_iota(jnp.int32, (BK, WQ), 0)
        assert WQ == 2 * BK

        def query_block(sub):
            i = pl.program_id(1) * SB_NSUB + sub
            rows = pl.ds(pl.multiple_of(sub * WQ, WQ), WQ)
            qryi = lax.broadcasted_iota(jnp.int32, (BK, WQ), 1) + i * WQ
            qs = q_ref[rows, :] * (HEAD_DIM ** -0.5)
            dob = do_ref[rows, :]
            dot = dob.astype(F32).T.astype(BF16)

            def scores(j, lo=0):
                return _mm_nt(_pair_stack(k_ref[pl.ds(pl.multiple_of(j * BK, BK), BK), :], lane), qs[lo:])

            def scan(blocks, masked, seen=None):
                seen = seen or [0] * len(blocks)
                sps = [_softplus(scores(j, lo)) for j, lo in zip(blocks, seen)]
                sps = [_mask_keys(sp, (keyi[:, lo:] + j * BK) < qryi[:, lo:]) if m else sp
                       for sp, j, m, lo in zip(sps, blocks, masked, seen)]
                for sp, j, lo in zip(sps, blocks, seen):
                    run = st_scr[0:2, :]
                    later_scr[j, 0:2, :] = run
                    st_scr[0:2, lo:] = run[:, lo:] + jnp.concatenate([jnp.sum(sp[0:BK], axis=0, keepdims=True),
                                                                      jnp.sum(sp[BK:2 * BK], axis=0, keepdims=True)], axis=0)

            def sweep(blocks, masked, seen=None):
                seen = seen or [0] * len(blocks)
                starts = [pl.multiple_of(j * BK, BK) for j in blocks]
                valid = [(keyi[:, lo:] + j * BK) < qryi[:, lo:] if m else None for j, m, lo in zip(blocks, masked, seen)]
                zs = [scores(j, lo) for j, lo in zip(blocks, seen)]
                us = [jnp.exp(lax.bitcast_convert_type(lax.bitcast_convert_type(z, jnp.uint32) | jnp.uint32(0x80000000), F32))
                      for z in zs]
                sps = [jnp.maximum(z, 0.0) + jnp.log(1.0 + u) for z, u in zip(zs, us)]
                sps = [sp if ok is None else _mask_keys(sp, ok) for sp, ok in zip(sps, valid)]
                sigs = [jnp.where(z >= 0.0, 1.0, u) / (1.0 + u) for z, u in zip(zs, us)]
                cums = [_mm(usuf_ref[...], _split_rows(sp)) for sp in sps]
                dws = [_mm(_pair_stack(v_ref[pl.ds(ks, BK), :], lane), dot[:, lo:]) for ks, lo in zip(starts, seen)]
                wfs = []
                for z, c, j, ok, lo in zip(zs, cums, blocks, valid, seen):
                    f = jnp.exp(-later_scr[j, 0:2, lo:])
                    wide = (BK, WQ - lo)
                    wf = jnp.exp(z - c) * jnp.concatenate([jnp.broadcast_to(f[0:1], wide), jnp.broadcast_to(f[1:2], wide)], axis=0)
                    wfs.append(wf if ok is None else _mask_keys(wf, ok))
                es = [dw * wf for dw, wf in zip(dws, wfs)]
                pres = [_mm(upre_ref[...], e.astype(BF16)) for e in es]
                dzs = []
                for e, pre, sig, ok, lo in zip(es, pres, sigs, valid, seen):
                    e0 = pre[0:BK] + st_scr[0:1, lo:]
                    e1 = pre[BK:2 * BK] + st_scr[1:2, lo:]
                    st_scr[0:1, lo:] = e0[BK - 1:BK]
                    st_scr[1:2, lo:] = e1[BK - 1:BK]
                    dz = e - sig * jnp.concatenate([e0, e1], axis=0)
                    dzs.append((dz if ok is None else _mask_keys(dz, ok)).astype(BF16))
                whole = [b for b, lo in enumerate(seen) if lo == 0]
                dqt_scr[...] += _mm(jnp.concatenate([_heads_t(kt_ref[:, pl.ds(starts[b], BK)], rowi) for b in whole], axis=1),
                                    jnp.concatenate([dzs[b] for b in whole], axis=0))
                for b, lo in enumerate(seen):
                    if lo:
                        dqt_scr[:, lo:] += _mm(_heads_t(kt_ref[:, pl.ds(starts[b], BK)], rowi), dzs[b])
                for ks, dz, wf, lo in zip(starts, dzs, wfs, seen):
                    rk = _mm(dz, qs[lo:])
                    dk_ref[pl.ds(ks, BK), :] += jnp.where(lane < 64, rk[0:BK], rk[BK:2 * BK])
                    rv = _mm(wf.astype(BF16), dob[lo:])
                    dv_ref[pl.ds(ks, BK), :] += jnp.where(lane < 64, rv[0:BK], rv[BK:2 * BK])

            st_scr[...] = jnp.zeros_like(st_scr)

            @pl.when(i == 0)
            def _():
                scan([1, 0], [True, True], [BK, 0])

            @pl.when(i > 0)
            def _():
                scan([2 * i + 1, 2 * i, 2 * i - 1, 2 * i - 2], [True, True, False, False], [BK, 0, 0, 0])

            def unfinished():
                return (jnp.min(st_scr[0:2, :]) < SB_CUTOFF).astype(jnp.int32)

            def step(c):
                scan([2 * i - 1 - 2 * c[0], 2 * i - 2 - 2 * c[0]], [False, False])
                return c[0] + 1, unfinished()

            npairs, _ = lax.while_loop(lambda c: (c[0] < i) & (c[1] > 0), step, (jnp.minimum(i, 1), unfinished()))

            st_scr[...] = jnp.zeros_like(st_scr)
            dqt_scr[...] = jnp.zeros_like(dqt_scr)
            first = 2 * (i - npairs)

            def early(t, carry):
                sweep([first + 2 * t, first + 2 * t + 1], [False, False])
                return carry

            lax.fori_loop(0, npairs - 1, early, 0)

            @pl.when(i == 0)
            def _():
                sweep([0, 1], [True, True], [0, BK])

            @pl.when(i > 0)
            def _():
                sweep([2 * i - 2, 2 * i - 1, 2 * i, 2 * i + 1], [False, False, True, True], [0, 0, 0, BK])

            dq_ref[rows, :] = (dqt_scr[...].T * (HEAD_DIM ** -0.5)).astype(BF16)

        lax.fori_loop(0, SB_NSUB, lambda sub, c: (query_block(sub), c)[1], 0)

        @pl.when((p == pl.num_programs(0) - 1) & (i == pl.num_programs(1) - 1))
        def _():
            reduce_finish()
            copies = [pltpu.make_async_copy(f_scr[t], outs[t], out_sems.at[t]) for t in range(n)]
            for cp in copies:
                cp.start()
            for cp in copies:
                cp.wait()

    qspec = pl.BlockSpec((SB_NSUB * WQ, LANES), lambda p, i: (i, p))
    kspec = pl.BlockSpec((s, LANES), lambda p, i: (0, p))
    tspec = pl.BlockSpec((LANES, s), lambda p, i: (p, 0))
    anywhere = pl.BlockSpec(memory_space=pl.ANY)
    reduced = [jax.ShapeDtypeStruct(a.shape[1:], F32) for a in late]
    return pl.pallas_call(
        body, name="sb_bwd", grid=(4, s // (SB_NSUB * WQ)),
        out_shape=(jax.ShapeDtypeStruct((s, 512), BF16), jax.ShapeDtypeStruct((s, 512), F32),
                   jax.ShapeDtypeStruct((s, 512), F32), *reduced),
        in_specs=[qspec, kspec, tspec, kspec, qspec, _full2((2 * BK, 4 * BK)), _full2((2 * BK, 2 * BK))] + [anywhere] * n,
        out_specs=(qspec, kspec, kspec) + (anywhere,) * n,
        scratch_shapes=[pltpu.VMEM((s // BK, 8, WQ), F32), pltpu.VMEM((LANES, WQ), F32), pltpu.VMEM((8, WQ), F32)]
                       + [pltpu.VMEM(r.shape, F32) for r in reduced] + _reduce_scratch(late) + [pltpu.SemaphoreType.DMA((n,))],
        compiler_params=pltpu.CompilerParams(vmem_limit_bytes=VMEM_ATTN),
    )(q, k, kt, v, do, _sum_matrix("suffix", 2), _sum_matrix("prefix", 1), *late)


MLA_SCALE = (QK_NOPE + QK_ROPE) ** -0.5
LOG2E = 1.4426950408889634


def _mla_keys(kb):
    zero = jnp.zeros((BK, LANES), kb.dtype)
    return jnp.concatenate([jnp.concatenate([kb[:, 0:LANES], zero], axis=1),
                            jnp.concatenate([zero, kb[:, LANES:2 * LANES]], axis=1)], axis=0)


def _mla_fwd(qc, kc, vt):
    s = qc.shape[0]
    mq = min(MQ_FWD, s)
    rows_l = 16

    def body(q_ref, k_ref, vt_ref, o_ref, l_ref, p_scr, ot_scr, st_scr):
        i = pl.program_id(1)
        row = lax.broadcasted_iota(jnp.int32, (LANES, 1), 0)
        orow = lax.broadcasted_iota(jnp.int32, (rows_l, 2 * BK), 0)
        ocol = lax.broadcasted_iota(jnp.int32, (rows_l, 2 * BK), 1)
        ones = jnp.where(((orow == 0) & (ocol < BK)) | ((orow == 1) & (ocol >= BK)), 1.0, 0.0).astype(BF16)

        def chunks(lo, hi):
            return [(a, min(a + MLA_CW, hi)) for a in range(lo, hi, MLA_CW)]

        def keys(j):
            return _mla_keys(k_ref[pl.ds(pl.multiple_of(j * BK, BK), BK), :])

        def values_t(j):
            vtb = vt_ref[:, pl.ds(pl.multiple_of(j * BK, BK), BK)]
            zero = jnp.zeros_like(vtb)
            top = jnp.concatenate([jnp.where(row < 64, vtb, zero), jnp.where(row >= 64, vtb, zero)], axis=1)
            return jnp.concatenate([top, ones], axis=0)

        def pair_values(ja):
            return jnp.concatenate([values_t(ja), values_t(ja + 1)], axis=1)

        def softmax(ja, za, zb, masked, a, b):
            c = MLA_SCALE * LOG2E
            parts = [za[0:BK] * c, za[BK:2 * BK] * c, zb[0:BK] * c, zb[BK:2 * BK] * c]
            if masked:
                keyc = lax.broadcasted_iota(jnp.int32, (BK, b - a), 0)
                qryc = (lax.broadcasted_iota(jnp.int32, (BK, b - a), 1) + (i * mq + a)) // 64
                va = ((keyc + ja * BK) // 64) <= qryc
                vb = ((keyc + (ja + 1) * BK) // 64) <= qryc
                parts = [jnp.where(va, parts[0], -1e30), jnp.where(va, parts[1], -1e30),
                         jnp.where(vb, parts[2], -1e30), jnp.where(vb, parts[3], -1e30)]
            m0, m1 = st_scr[0:1, a:b], st_scr[1:2, a:b]
            n0 = jnp.maximum(m0, jnp.max(jnp.maximum(parts[0], parts[2]), axis=0, keepdims=True))
            n1 = jnp.maximum(m1, jnp.max(jnp.maximum(parts[1], parts[3]), axis=0, keepdims=True))
            st_scr[2:3, a:b] = jnp.exp2(m0 - n0)
            st_scr[3:4, a:b] = jnp.exp2(m1 - n1)
            st_scr[0:1, a:b] = n0
            st_scr[1:2, a:b] = n1
            p_scr[:, a:b] = jnp.concatenate([jnp.exp2(parts[0] - n0), jnp.exp2(parts[1] - n1),
                                             jnp.exp2(parts[2] - n0), jnp.exp2(parts[3] - n1)], axis=0).astype(BF16)

        def accumulate(vals, a, b):
            pv = _mm(vals, p_scr[:, a:b])
            f = jnp.where(row < 64, st_scr[2:3, a:b], st_scr[3:4, a:b])
            ot_scr[0:LANES, a:b] = f * ot_scr[0:LANES, a:b] + pv[0:LANES]
            ot_scr[LANES:LANES + 8, a:b] = st_scr[2:10, a:b] * ot_scr[LANES:LANES + 8, a:b] + pv[LANES:LANES + 8]

        def step(n, diag, lo=0, prev_lo=0):
            kab = jnp.concatenate([keys(2 * n), keys(2 * n + 1)], axis=0)
            vals = pair_values(2 * n - 2)
            for a, b in chunks(prev_lo, lo):
                accumulate(vals, a, b)
            for a, b in chunks(lo, mq):
                zab = _mm_nt(kab, q_ref[a:b, :])
                accumulate(vals, a, b)
                softmax(2 * n, zab[0:2 * BK], zab[2 * BK:4 * BK], diag and a < lo + 2 * BK, a, b)

        def first(diag):
            kab = jnp.concatenate([keys(0), keys(1)], axis=0)
            for a, b in chunks(0, mq):
                zab = _mm_nt(kab, q_ref[a:b, :])
                softmax(0, zab[0:2 * BK], zab[2 * BK:4 * BK], diag and a < 2 * BK, a, b)

        st_scr[...] = jnp.concatenate([jnp.full((2, mq), -1e30, F32), jnp.ones((14, mq), F32)], axis=0)
        ot_scr[...] = jnp.zeros_like(ot_scr)

        npq = mq // (2 * BK)
        seen = lambda d: 2 * BK * max(d, 0)

        @pl.when(i == 0)
        def _():
            first(True)
            for d in range(1, npq):
                step(d, True, seen(d), seen(d - 1))

        if s > mq:
            @pl.when(i > 0)
            def _():
                first(False)
                lax.fori_loop(1, npq * i, lambda n, c: (step(n, False), c)[1], 0)
                for d in range(npq):
                    step(npq * i + d, True, seen(d), seen(d - 1))

        vals = pair_values(2 * (npq * (i + 1) - 1))
        for a, b in chunks(seen(npq - 1), mq):
            accumulate(vals, a, b)
        for a, b in chunks(0, mq):
            l0, l1 = ot_scr[LANES:LANES + 1, a:b], ot_scr[LANES + 1:LANES + 2, a:b]
            o_ref[a:b, :] = (ot_scr[0:LANES, a:b] / jnp.where(row < 64, l0, l1)).T
            l_ref[a:b, :] = jnp.where(row < 64, st_scr[0:1, a:b] + jnp.log2(l0), st_scr[1:2, a:b] + jnp.log2(l1)).T

    qspec = pl.BlockSpec((mq, 2 * LANES), lambda p, i: (i, p))
    kspec = pl.BlockSpec((s, 2 * LANES), lambda p, i: (0, p))
    vtspec = pl.BlockSpec((LANES, s), lambda p, i: (p, 0))
    ospec = pl.BlockSpec((mq, LANES), lambda p, i: (i, p))
    return pl.pallas_call(
        body, name="mla_fwd", grid=(4, s // mq),
        out_shape=(jax.ShapeDtypeStruct((s, 512), F32), jax.ShapeDtypeStruct((s, 512), F32)),
        in_specs=[qspec, kspec, vtspec], out_specs=(ospec, ospec),
        scratch_shapes=[pltpu.VMEM((4 * BK, mq), BF16), pltpu.VMEM((LANES + 8, mq), F32), pltpu.VMEM((16, mq), F32)],
        compiler_params=pltpu.CompilerParams(vmem_limit_bytes=VMEM_ATTN),
    )(qc, kc, vt)


def _mla_bwd(qc, kc, kct, v, do, lse, delta):
    s = qc.shape[0]
    mq = min(MQ_BWD, s)

    def body(q_ref, k_ref, kt_ref, v_ref, do_ref, l_ref, d_ref, dq_ref, dk_ref, dv_ref, dqt_scr, p_scr, dz_scr,
             dvt_scr):
        i = pl.program_id(1)

        @pl.when(i == 0)
        def _():
            dk_ref[...] = jnp.zeros_like(dk_ref)
            dvt_scr[...] = jnp.zeros_like(dvt_scr)

        lane = lax.broadcasted_iota(jnp.int32, (1, LANES), 1)
        keyc = lax.broadcasted_iota(jnp.int32, (BK, mq), 0)
        qryc = (lax.broadcasted_iota(jnp.int32, (BK, mq), 1) + i * mq) // 64
        qw = q_ref[...]
        dob = do_ref[...]
        dost = (dob.astype(F32) * MLA_SCALE).T.astype(BF16)
        dot_ = dob.astype(F32).T.astype(BF16)
        lt = l_ref[...].T
        dt = (d_ref[...] * MLA_SCALE).T
        lse0, lse1 = lt[0:1], lt[64:65]
        dl0, dl1 = dt[0:1], dt[64:65]
        dqt_scr[...] = jnp.zeros_like(dqt_scr)

        def products(j, lo=0):
            ks = pl.multiple_of(j * BK, BK)
            return (_mm_nt(_mla_keys(k_ref[pl.ds(ks, BK), :]), qw[lo:]),
                    _mm(_pair_stack(v_ref[pl.ds(ks, BK), :], lane), dost[:, lo:]))

        def grads(j, slot, zt, dwt, masked, lo=0):
            zt = zt * (MLA_SCALE * LOG2E)
            p0 = jnp.exp2(zt[0:BK] - lse0[:, lo:])
            p1 = jnp.exp2(zt[BK:2 * BK] - lse1[:, lo:])
            if masked:
                valid = ((keyc[:, lo:] + j * BK) // 64) <= qryc[:, lo:]
                p0, p1 = jnp.where(valid, p0, 0.0), jnp.where(valid, p1, 0.0)
            rows = slice(slot * BK, (slot + 1) * BK)
            p_scr[0, rows, lo:] = p0.astype(BF16)
            p_scr[1, rows, lo:] = p1.astype(BF16)
            dz_scr[0, rows, lo:] = (p0 * (dwt[0:BK] - dl0[:, lo:])).astype(BF16)
            dz_scr[1, rows, lo:] = (p1 * (dwt[BK:2 * BK] - dl1[:, lo:])).astype(BF16)

        def scatter(ja, lo=0):
            ks = pl.multiple_of(ja * BK, 2 * BK)
            for h in range(2):
                cols, vrows = slice(h * LANES, (h + 1) * LANES), slice(h * 64, (h + 1) * 64)
                dzh = dz_scr[h, :, lo:]
                dqt_scr[cols, lo:] += _mm(kt_ref[cols, pl.ds(ks, 2 * BK)], dzh)
                dk_ref[pl.ds(ks, 2 * BK), cols] += _mm(dzh, qw[lo:, cols])
                dvt_scr[vrows, pl.ds(ks, 2 * BK)] += _mm_nt(dot_[vrows, lo:], p_scr[h, :, lo:])

        def step(n, masked, lo=0, prev_lo=0):
            za, wa = products(2 * n, lo)
            zb, wb = products(2 * n + 1, lo)
            scatter(2 * n - 2, prev_lo)
            grads(2 * n, 0, za, wa, masked, lo)
            grads(2 * n + 1, 1, zb, wb, masked, lo)

        def first(masked):
            za, wa = products(0)
            zb, wb = products(1)
            grads(0, 0, za, wa, masked)
            grads(1, 1, zb, wb, masked)

        npq = mq // (2 * BK)
        seen = lambda d: 2 * BK * max(d, 0)

        @pl.when(i == 0)
        def _():
            first(True)
            for d in range(1, npq):
                step(d, True, seen(d), seen(d - 1))

        @pl.when(i > 0)
        def _():
            first(False)
            step(1, False)
            lax.fori_loop(1, npq * i // 2, lambda m, c: (step(2 * m, False), step(2 * m + 1, False), c)[2], 0)
            for d in range(npq):
                step(npq * i + d, True, seen(d), seen(d - 1))

        scatter(2 * (npq * (i + 1) - 1), seen(npq - 1))
        dq_ref[...] = dqt_scr[...].T

        @pl.when(i == s // mq - 1)
        def _():
            for a in range(0, s, 512):
                dv_ref[a:a + 512, :] = dvt_scr[:, a:a + 512].T

    qspec = pl.BlockSpec((mq, 2 * LANES), lambda p, i: (i, p))
    kspec = pl.BlockSpec((s, 2 * LANES), lambda p, i: (0, p))
    ktspec = pl.BlockSpec((2 * LANES, s), lambda p, i: (p, 0))
    vspec = pl.BlockSpec((s, LANES), lambda p, i: (0, p))
    ospec = pl.BlockSpec((mq, LANES), lambda p, i: (i, p))
    return pl.pallas_call(
        body, name="mla_bwd", grid=(4, s // mq),
        out_shape=(jax.ShapeDtypeStruct((s, 1024), F32), jax.ShapeDtypeStruct((s, 1024), F32),
                   jax.ShapeDtypeStruct((s, 512), F32)),
        in_specs=[qspec, kspec, ktspec, vspec, ospec, ospec, ospec], out_specs=(qspec, kspec, vspec),
        scratch_shapes=[pltpu.VMEM((2 * LANES, mq), F32), pltpu.VMEM((2, 2 * BK, mq), BF16), pltpu.VMEM((2, 2 * BK, mq), BF16),
                        pltpu.VMEM((LANES, s), F32)],
        compiler_params=pltpu.CompilerParams(vmem_limit_bytes=VMEM_ATTN),
    )(qc, kc, kct, v, do, lse, delta)


def _post(x, p, tgt, sbo, mlao, sbg, mlag, gsb, gmla, wout, gpost, wple, gple, wpg, bpg):
    s = x.shape[0]

    def body(x_ref, p_ref, t_ref, sbo_ref, mlao_ref, sbg_ref, mlag_ref, gsb_ref, gmla_ref, wout_ref,
             gpost_ref, wple_ref, gple_ref, wpg_ref, bpg_ref, bd_ref,
             dsbo_ref, dmlao_ref, delta_ref, dsbg_ref, dmlag_ref, dxres_ref, dwout_ref, dwpg_ref, dwple_ref, vec_ref):
        i = pl.program_id(0)

        @pl.when(i == 0)
        def _():
            dwout_ref[...] = jnp.zeros_like(dwout_ref)
            dwpg_ref[...] = jnp.zeros_like(dwpg_ref)
            dwple_ref[...] = jnp.zeros_like(dwple_ref)
            vec_ref[...] = jnp.zeros_like(vec_ref)

        inv_hd = 1.0 / HEAD_DIM

        def head_fwd(o, g, gate):
            r = lax.rsqrt(_seg(o * o, bd_ref[...]) * inv_hd + EPS)
            hat = o * r
            n = hat * g
            sg = _sigmoid(gate)
            return hat, r, n, sg, n * (gate * sg)

        sbo, mlao, sbg_v, mlag_v = sbo_ref[...], mlao_ref[...], sbg_ref[...], mlag_ref[...]
        gsb_v, gmla_v = gsb_ref[...], gmla_ref[...]
        sb_hat, sb_r, sb_n, sb_sg, sb_y = head_fwd(sbo, gsb_v, sbg_v)
        ml_hat, ml_r, ml_n, ml_sg, ml_y = head_fwd(mlao, gmla_v, mlag_v)
        mix = jnp.concatenate([sb_y, ml_y], axis=1).astype(BF16)
        y = _mm(mix, wout_ref[...])
        ry = lax.rsqrt(_rowmean(y * y) + EPS)
        y_hat = y * ry
        gpost_v = gpost_ref[...]
        x1 = x_ref[...] + y_hat * gpost_v
        pb = p_ref[...].astype(BF16)
        pl_ = _mm(pb, wple_ref[...])
        rp = lax.rsqrt(_rowmean(pl_ * pl_) + EPS)
        pl_hat = pl_ * rp
        gple_v = gple_ref[...]
        ple = pl_hat * gple_v
        x1b = x1.astype(BF16)
        gate = _sigmoid(_mm(x1b, wpg_ref[...]) + bpg_ref[...])
        err = x1 + ple * gate - t_ref[...]
        loss = 0.5 * jnp.sum(_rowmean(err * err))
        dout = err * (1.0 / D_MODEL)

        du = dout * ple * gate * (1.0 - gate)
        dub = du.astype(BF16)
        dple = dout * gate
        dx1 = dout + _mm_nt(dub, wpg_ref[...])
        dwpg_ref[...] += _mm_tn(x1b, dub)
        dplh = dple * gple_v
        dpl = rp * (dplh - pl_hat * _rowmean(dplh * pl_hat))
        dwple_ref[...] += _mm_tn(pb, dpl.astype(BF16))
        dxres_ref[...] = dx1
        dyh = dx1 * gpost_v
        dy = ry * (dyh - y_hat * _rowmean(dyh * y_hat))
        dyb = dy.astype(BF16)
        dwout_ref[...] += _mm_tn(mix, dyb)
        dmix = _mm_nt(dyb, wout_ref[...])

        def head_bwd(dyv, hat, r, n, sg, g, gate):
            dn = dyv * (gate * sg)
            dgate = dyv * n * (sg * (1.0 + gate * (1.0 - sg)))
            dhat = dn * g
            do = r * (dhat - hat * (_seg(dhat * hat, bd_ref[...]) * inv_hd))
            return do, dgate, _colsum(dn * hat)

        dsbo, dsbg, dg_sb = head_bwd(dmix[:, 0:512], sb_hat, sb_r, sb_n, sb_sg, gsb_v, sbg_v)
        dmlao, dmlag, dg_ml = head_bwd(dmix[:, 512:1024], ml_hat, ml_r, ml_n, ml_sg, gmla_v, mlag_v)
        dsbo_ref[...] = dsbo.astype(BF16)
        dmlao_ref[...] = dmlao.astype(BF16)
        delta_ref[...] = _seg(dmlao * mlao, bd_ref[...])
        dsbg_ref[...] = dsbg.astype(BF16)
        dmlag_ref[...] = dmlag.astype(BF16)
        vec_ref[pl.ds(0, 1), :] += _colsum(dx1 * y_hat)
        vec_ref[pl.ds(1, 1), :] += _colsum(dple * pl_hat)
        vec_ref[pl.ds(2, 1), :] += _colsum(du)
        vec_ref[pl.ds(3, 1), :] += jnp.concatenate([dg_sb, dg_ml], axis=1)
        vec_ref[pl.ds(4, 1), :] += jnp.full((1, D_MODEL), loss, F32)

    out_shape = (
        jax.ShapeDtypeStruct((s, 512), BF16), jax.ShapeDtypeStruct((s, 512), BF16), jax.ShapeDtypeStruct((s, 512), F32),
        jax.ShapeDtypeStruct((s, 512), BF16), jax.ShapeDtypeStruct((s, 512), BF16), jax.ShapeDtypeStruct((s, D_MODEL), F32),
        jax.ShapeDtypeStruct((D_MODEL, D_MODEL), F32), jax.ShapeDtypeStruct((D_MODEL, D_MODEL), F32),
        jax.ShapeDtypeStruct((PLE_DIM, D_MODEL), F32), jax.ShapeDtypeStruct((8, D_MODEL), F32),
    )
    return pl.pallas_call(
        body, name="post_fwd_bwd", grid=(s // TM,), out_shape=out_shape,
        in_specs=[_rows(D_MODEL), _rows(PLE_DIM), _rows(D_MODEL), _rows(512), _rows(512), _rows(512), _rows(512),
                  _full((1, 512)), _full((1, 512)), _full((D_MODEL, D_MODEL)),
                  _full((1, D_MODEL)), _full((PLE_DIM, D_MODEL)), _full((1, D_MODEL)), _full((D_MODEL, D_MODEL)),
                  _full((1, D_MODEL)), _full((1024, 512))],
        out_specs=(_rows(512), _rows(512), _rows(512), _rows(512), _rows(512), _rows(D_MODEL),
                   _acc((D_MODEL, D_MODEL)), _acc((D_MODEL, D_MODEL)), _acc((PLE_DIM, D_MODEL)), _acc((8, D_MODEL))),
        compiler_params=pltpu.CompilerParams(vmem_limit_bytes=VMEM_DENSE),
    )(x, p, tgt, sbo, mlao, sbg, mlag, gsb, gmla, wout, gpost, wple, gple, wpg, bpg, _blockdiag2(512, HEAD_DIM))


def _pre_bwd(x, dxres, dsbq, dsbk, dsbv, dsbg, dmlag, dqc, dkc, dmv, cq, ckv, tabs, gpre, win, gq, wuq, gkv, wk, wv):
    s = x.shape[0]
    c_t, sa_t, sb_t = tabs
    rw = _rows

    def body(x_ref, dxres_ref, dsbq_ref, dsbk_ref, dsbv_ref, dsbg_ref, dmlag_ref, dqc_ref, dkc_ref, dmv_ref, cq_ref,
             ckv_ref, c_ref, sa_ref, sb_ref, gpre_ref, win_ref, gq_ref, wuq_ref, gkv_ref, wk_ref, wv_ref,
             gx_ref, dwin_ref, dwuq_ref, dwk_ref, dwv_ref, vec_ref, dwin_acc):
        i = pl.program_id(0)

        @pl.when(i == 0)
        def _():
            dwin_acc[...] = jnp.zeros_like(dwin_acc)
            dwuq_ref[...] = jnp.zeros_like(dwuq_ref)
            dwk_ref[...] = jnp.zeros_like(dwk_ref)
            dwv_ref[...] = jnp.zeros_like(dwv_ref)
            vec_ref[...] = jnp.zeros_like(vec_ref)

        lane = lax.broadcasted_iota(jnp.int32, (1, LANES), 1)
        c1, sa1, sb1 = c_ref[...], sa_ref[...], sb_ref[...]
        c8, sa8, sb8 = jnp.tile(c1, (1, 8)), jnp.tile(sa1, (1, 8)), jnp.tile(sb1, (1, 8))

        def norm_bwd(dn, hat, r, g):
            t = dn * g
            return r * (t - hat * _rowmean(t * hat)), _colsum(dn * hat)

        xv = x_ref[...]
        r1 = lax.rsqrt(_rowmean(xv * xv) + EPS)
        x_hat = xv * r1
        gpre_v = gpre_ref[...]
        hb = (x_hat * gpre_v).astype(BF16)
        ready = jnp.concatenate([dsbq_ref[...], dsbk_ref[...].astype(BF16), dsbv_ref[...].astype(BF16), dsbg_ref[...]], axis=1)
        dmlag = dmlag_ref[...]
        dwin_acc[:, 0:2048] += _mm_tn(hb, ready)
        dwin_acc[:, 2560:3072] += _mm_tn(hb, dmlag)
        dh = _mm_nt(ready, win_ref[:, 0:2048]) + _mm_nt(dmlag, win_ref[:, 2560:3072])

        dqeb = _rope_bwd(dqc_ref[...], c8, sa8, sb8).astype(BF16)
        cq = cq_ref[...]
        rq = lax.rsqrt(_rowmean(cq * cq) + EPS)
        cq_hat = cq * rq
        gq_v = gq_ref[...]
        dwuq_ref[...] += _mm_tn((cq_hat * gq_v).astype(BF16), dqeb)
        dcq, dg_q = norm_bwd(_mm_nt(dqeb, wuq_ref[...]), cq_hat, rq, gq_v)

        dkc = dkc_ref[...]
        dkcb = dkc.astype(BF16)
        dmvb = dmv_ref[...].astype(BF16)
        ckv = ckv_ref[...]
        rkv = lax.rsqrt(_rowmean(ckv * ckv) + EPS)
        ckv_hat = ckv * rkv
        gkv_v = gkv_ref[...]
        ckvnb = (ckv_hat * gkv_v).astype(BF16)
        dwk_ref[...] += _mm_tn(ckvnb, dkcb)
        dwv_ref[...] += _mm_tn(ckvnb, dmvb)
        dckv, dg_kv = norm_bwd(_mm_nt(dkcb, wk_ref[...]) + _mm_nt(dmvb, wv_ref[...]), ckv_hat, rkv, gkv_v)

        dkr = dkc[:, 0:LANES]
        for hh in range(1, 8):
            dkr = dkr + dkc[:, LANES * hh:LANES * (hh + 1)]
        dkr = _rope_bwd(dkr, c1, sa1, sb1)
        dkr = jnp.where((lane >= 64) & (lane < 96), dkr, 0.0)

        late = jnp.concatenate([dcq.astype(BF16), dckv.astype(BF16), dkr.astype(BF16)], axis=1)
        dwin_acc[:, 2048:2560] += _mm_tn(hb, late)
        dx, dg_pre = norm_bwd(dh + _mm_nt(late, win_ref[:, 2048:2560]), x_hat, r1, gpre_v)
        gx_ref[...] = dxres_ref[...] + dx
        vec_ref[pl.ds(0, 1), :] += dg_pre
        vec_ref[pl.ds(1, 1), :] += jnp.concatenate([dg_q, dg_kv, jnp.zeros((1, D_MODEL - Q_LORA - KV_LORA), F32)], axis=1)

        @pl.when(i == pl.num_programs(0) - 1)
        def _():
            pltpu.sync_copy(dwin_acc, dwin_ref)

    out_shape = (
        jax.ShapeDtypeStruct((s, D_MODEL), F32), jax.ShapeDtypeStruct((D_MODEL, D_EXT), F32),
        jax.ShapeDtypeStruct((Q_LORA, 1024), F32), jax.ShapeDtypeStruct((KV_LORA, 1024), F32),
        jax.ShapeDtypeStruct((KV_LORA, 512), F32), jax.ShapeDtypeStruct((8, D_MODEL), F32),
    )
    return pl.pallas_call(
        body, name="pre_bwd", grid=(s // TM,), out_shape=out_shape,
        in_specs=[rw(D_MODEL), rw(D_MODEL), rw(512), rw(512), rw(512), rw(512), rw(512),
                  rw(1024), rw(1024), rw(512), rw(Q_LORA), rw(KV_LORA), rw(LANES), rw(LANES),
                  rw(LANES), _full((1, D_MODEL)), _full((D_MODEL, D_EXT)), _full((1, Q_LORA)), _full((Q_LORA, 1024)),
                  _full((1, KV_LORA)), _full((KV_LORA, 1024)), _full((KV_LORA, 512))],
        out_specs=(rw(D_MODEL), pl.BlockSpec(memory_space=pl.ANY), _acc((Q_LORA, 1024)), _acc((KV_LORA, 1024)),
                   _acc((KV_LORA, 512)), _acc((8, D_MODEL))),
        scratch_shapes=[pltpu.VMEM((D_MODEL, D_EXT), F32)],
        compiler_params=pltpu.CompilerParams(vmem_limit_bytes=VMEM_DENSE),
    )(x, dxres, dsbq, dsbk, dsbv, dsbg, dmlag, dqc, dkc, dmv, cq, ckv, c_t, sa_t, sb_t, gpre, win, gq, wuq, gkv, wk, wv)


def _place():
    return lax.axis_index("x"), lax.axis_index("y"), lax.axis_index("c")


def _gather_steps(shapes, ins, bufs, send_sems, recv_sems):
    n = len(shapes)
    x, y, c = _place()
    me, sib = (x, y, c), (x, y, 1 - c)
    chips = [(1 - x, y), (x, 1 - y), (1 - x, 1 - y)]

    def half(t, chip, hc):
        rows = shapes[t][0] // 2
        return bufs[t].at[2 * chip[0] + chip[1], pl.ds(pl.multiple_of(hc * rows, 16), rows), :]

    def copy(k, t, chip, hc, to):
        return pltpu.make_async_remote_copy(src_ref=half(t, chip, hc), dst_ref=half(t, chip, hc), send_sem=send_sems.at[k],
                                            recv_sem=recv_sems.at[k], device_id=to, device_id_type=MESH)

    def start():
        for t in range(n):
            bufs[t][2 * x + y] = ins[t][...].astype(BF16)
            for j, chip in enumerate(chips):
                copy(6 * t + j, t, (x, y), c, (*chip, c)).start()

    def forward():
        for t in range(n):
            for j, chip in enumerate(chips):
                copy(6 * t + j, t, chip, c, me).wait_recv()
                copy(6 * t + 3 + j, t, chip, c, sib).start()

    def finish():
        for t in range(n):
            for j, chip in enumerate(chips):
                copy(6 * t + 3 + j, t, chip, 1 - c, me).wait_recv()
        for t in range(n):
            for j, chip in enumerate(chips):
                copy(6 * t + j, t, (x, y), c, (*chip, c)).wait_send()
                copy(6 * t + 3 + j, t, chip, c, sib).wait_send()

    return start, forward, finish


def _allgather_weights(shards):
    n = len(shards)

    def body(*refs):
        start, forward, finish = _gather_steps([a.shape for a in shards], refs[:n], refs[n:2 * n], refs[2 * n], refs[2 * n + 1])
        start()
        forward()
        finish()

    return pl.pallas_call(
        body, name="allgather_weights",
        out_shape=tuple(jax.ShapeDtypeStruct((N_SHARD,) + a.shape, BF16) for a in shards),
        in_specs=[pl.BlockSpec(memory_space=pltpu.VMEM)] * n, out_specs=(pl.BlockSpec(memory_space=pltpu.VMEM),) * n,
        scratch_shapes=[pltpu.SemaphoreType.DMA((6 * n,)), pltpu.SemaphoreType.DMA((6 * n,))],
        compiler_params=pltpu.CompilerParams(vmem_limit_bytes=VMEM_ATTN),
    )(*shards)


def _reduce_scratch(gsh):
    n = len(gsh)
    half_shapes = [(N_SHARD, a.shape[1] // 2, a.shape[2]) for a in gsh]
    return ([pltpu.VMEM(s_, F32) for s_ in half_shapes] * 2 + [pltpu.VMEM(s_, BF16) for s_ in half_shapes] * 2
            + [pltpu.SemaphoreType.DMA((n,)), pltpu.SemaphoreType.DMA((5 * n,)), pltpu.SemaphoreType.DMA((5 * n,))])


def _reduce_steps(halves, g_refs, f_refs, scratch):
    n = len(halves)
    accs, sibs, sbufs, rbufs = scratch[0:n], scratch[n:2 * n], scratch[2 * n:3 * n], scratch[3 * n:4 * n]
    local_sems, send_sems, recv_sems = scratch[4 * n:4 * n + 3]
    x, y, c = _place()
    me, sib = (x, y, c), (x, y, 1 - c)
    mine = 2 * x + y
    chips = [(1 - x, y), (x, 1 - y), (1 - x, 1 - y)]

    def remote(k, src, dst, to):
        return pltpu.make_async_remote_copy(src_ref=src, dst_ref=dst, send_sem=send_sems.at[k], recv_sem=recv_sems.at[k],
                                            device_id=to, device_id_type=MESH)

    def half3(ref, t, hc):
        return ref.at[:, pl.ds(pl.multiple_of(hc * halves[t], 8), halves[t]), :]

    def half2(ref, t, hc):
        return ref.at[pl.ds(pl.multiple_of(hc * halves[t], 8), halves[t]), :]

    def mine_load(t):
        return pltpu.make_async_copy(half3(g_refs[t], t, c), accs[t], local_sems.at[t])

    def to_sibling(t, to):
        return remote(t, half3(g_refs[t], t, 1 - c), sibs[t], to)

    def to_chip(t, j, chip, to):
        idx = 2 * chip[0] + chip[1]
        return remote(n + 3 * t + j, sbufs[t].at[idx], rbufs[t].at[mine if to is not me else idx], to)

    def swap(t, hc, to):
        return remote(4 * n + t, half2(f_refs[t], t, hc), half2(f_refs[t], t, hc), to)

    def load():
        for t in range(n):
            mine_load(t).start()
            to_sibling(t, sib).start()

    def partial():
        for t in range(n):
            mine_load(t).wait()
            to_sibling(t, me).wait_recv()
            for k in range(N_SHARD):
                accs[t][k] = accs[t][k] + sibs[t][k]
            for j, chip in enumerate(chips):
                idx = 2 * chip[0] + chip[1]
                sbufs[t][idx] = accs[t][idx].astype(BF16)
                to_chip(t, j, chip, (*chip, c)).start()

    def total():
        for t in range(n):
            acc = accs[t][mine]
            for j, chip in enumerate(chips):
                to_chip(t, j, chip, me).wait_recv()
                acc = acc + rbufs[t][2 * chip[0] + chip[1]].astype(F32)
            half2(f_refs[t], t, c)[...] = acc
            swap(t, c, sib).start()

    def finish():
        for t in range(n):
            swap(t, 1 - c, me).wait_recv()
        for t in range(n):
            to_sibling(t, sib).wait_send()
            for j, chip in enumerate(chips):
                to_chip(t, j, chip, (*chip, c)).wait_send()
            swap(t, c, sib).wait_send()

    return load, partial, total, finish


def _reduce_scatter_grads(gsh, vec):
    n = len(gsh)
    halves = [a.shape[1] // 2 for a in gsh]

    def body(*refs):
        g_refs, vec_ref, f_refs, vsum_ref = refs[:n], refs[n], refs[n + 1:2 * n + 1], refs[2 * n + 1]
        scratch = refs[2 * n + 2:]
        vrecv, vsend_sems, vrecv_sems = scratch[4 * n + 3:]
        load, partial, total, finish = _reduce_steps(halves, g_refs, f_refs, scratch)
        x, y, c = _place()
        my_dev = 4 * x + 2 * y + c

        def flip(k):
            return x ^ ((k >> 2) & 1), y ^ ((k >> 1) & 1), c ^ (k & 1)

        def vcopy(k, slot, to):
            return pltpu.make_async_remote_copy(src_ref=vec_ref, dst_ref=vrecv.at[slot], send_sem=vsend_sems.at[k - 1],
                                                recv_sem=vrecv_sems.at[k - 1], device_id=to, device_id_type=MESH)

        load()
        vrecv[my_dev] = vec_ref[...]
        for k in range(1, 8):
            vcopy(k, my_dev, flip(k)).start()
        partial()
        total()
        finish()
        for k in range(1, 8):
            fx, fy, fc = flip(k)
            vcopy(k, 4 * fx + 2 * fy + fc, (x, y, c)).wait_recv()
        vs = vrecv[0]
        for d in range(1, 8):
            vs = vs + vrecv[d]
        vsum_ref[...] = vs
        for k in range(1, 8):
            vcopy(k, my_dev, flip(k)).wait_send()

    return pl.pallas_call(
        body, name="reduce_scatter_grads",
        out_shape=tuple(jax.ShapeDtypeStruct(a.shape[1:], F32) for a in gsh) + (jax.ShapeDtypeStruct((VEC_ROWS, 1024), F32),),
        in_specs=[pl.BlockSpec(memory_space=pl.ANY)] * n + [pl.BlockSpec(memory_space=pltpu.VMEM)],
        out_specs=(pl.BlockSpec(memory_space=pltpu.VMEM),) * (n + 1),
        scratch_shapes=_reduce_scratch(gsh) + [pltpu.VMEM((8, VEC_ROWS, 1024), F32), pltpu.SemaphoreType.DMA((7,)),
                                               pltpu.SemaphoreType.DMA((7,))],
        compiler_params=pltpu.CompilerParams(vmem_limit_bytes=56 * 1024 * 1024),
    )(*gsh, vec)


def _adamw(w, g, m, v):
    rows, cols = w.shape
    tr = rows if rows <= 256 else 256
    flip = cols % LANES != 0

    def body(w_ref, g_ref, m_ref, v_ref, g_out, d_ref, nm_ref, nv_ref):
        gv = g_ref[...].T if flip else g_ref[...]
        outs = (gv,) + _adam_math(w_ref[...], gv, m_ref[...], v_ref[...])
        for ref, val in zip((g_out, d_ref, nm_ref, nv_ref), outs):
            ref[...] = val

    spec = pl.BlockSpec((tr, cols), lambda i: (i, 0))
    tspec = pl.BlockSpec((cols, tr), lambda i: (0, i)) if flip else spec
    shp = jax.ShapeDtypeStruct((cols, rows) if flip else (rows, cols), F32)
    if flip:
        w, m, v = w.T, m.T, v.T
    outs = pl.pallas_call(body, name="adamw", grid=(rows // tr,), out_shape=(shp,) * 4,
                          in_specs=[tspec, spec, tspec, tspec], out_specs=(tspec,) * 4)(w, g, m, v)
    return tuple(o.T for o in outs) if flip else outs


def _adam_math(w, g, m, v):
    m2 = ADAM_B1 * m + (1.0 - ADAM_B1) * g
    v2 = ADAM_B2 * v + (1.0 - ADAM_B2) * (g * g)
    m_hat = m2 / (1.0 - ADAM_B1 ** ADAM_STEP)
    v_hat = v2 / (1.0 - ADAM_B2 ** ADAM_STEP)
    return -ADAM_LR * (m_hat / (jnp.sqrt(v_hat) + ADAM_EPS) + ADAM_WD * w), m2, v2


def _adamw_small(vsum, w, m, v):
    names = [name for name, _, _, _ in _VEC_LAYOUT]
    k = len(names)

    def body(*refs):
        vs_ref, w_refs, m_refs, v_refs = refs[0], refs[1:1 + k], refs[1 + k:1 + 2 * k], refs[1 + 2 * k:1 + 3 * k]
        outs = refs[1 + 3 * k:]
        for idx, (_, r, c0, width) in enumerate(_VEC_LAYOUT):
            gv = vs_ref[pl.ds(r, 1), pl.ds(c0, width)]
            d, m2, v2 = _adam_math(w_refs[idx][...], gv, m_refs[idx][...], v_refs[idx][...])
            outs[idx][...], outs[k + idx][...], outs[2 * k + idx][...], outs[3 * k + idx][...] = gv, d, m2, v2

    shapes = tuple(jax.ShapeDtypeStruct(w[name].shape, F32) for name in names)
    res = pl.pallas_call(
        body, name="adamw_small", out_shape=shapes * 4,
        in_specs=[pl.BlockSpec(memory_space=pltpu.VMEM)] * (1 + 3 * k), out_specs=(pl.BlockSpec(memory_space=pltpu.VMEM),) * (4 * k),
    )(vsum, *[w[name] for name in names], *[m[name] for name in names], *[v[name] for name in names])
    return tuple({name: res[part * k + idx] for idx, name in enumerate(names)} for part in range(4))


_EARLY = ("w_in", "w_uq", "w_ukv")
_LATE = ("w_out", "w_ple", "w_ple_gate")
_BIG = _EARLY + _LATE
_KR_LOCAL = 2432 - 3 * (D_IN // N_SHARD)


def _extend_early(parts):
    cols = lambda a: a.transpose(1, 0, 2).reshape(a.shape[1], N_SHARD * a.shape[2])
    g = parts["w_in"]
    zeros = lambda n: jnp.zeros((D_MODEL, n), g.dtype)
    win_ext = jnp.concatenate([g[0], g[1], g[2], g[3][:, :_KR_LOCAL], zeros(64), g[3][:, _KR_LOCAL:_KR_LOCAL + QK_ROPE],
                               zeros(32), g[3][:, _KR_LOCAL + QK_ROPE:]], axis=1)
    wuq_ext = jnp.pad(cols(parts["w_uq"]).reshape(Q_LORA, 8, 96), ((0, 0), (0, 0), (0, 32))).reshape(Q_LORA, 1024)
    wukv = cols(parts["w_ukv"]).reshape(KV_LORA, 8, 128)
    wk_ext = jnp.pad(wukv[:, :, :64], ((0, 0), (0, 0), (0, 64))).reshape(KV_LORA, 1024)
    wv = wukv[:, :, 64:].reshape(KV_LORA, 512)
    return win_ext, wuq_ext, wk_ext, wv


def _shard_cols(a):
    return a.reshape(a.shape[0], N_SHARD, a.shape[1] // N_SHARD).transpose(1, 0, 2)


def _shard_rows(a):
    return a.reshape(N_SHARD, a.shape[0] // N_SHARD, a.shape[1])


def _shard_early_grads(dwin_ext, dwuq_ext, dwk_ext, dwv):
    e, w = dwin_ext, D_IN // N_SHARD
    last = jnp.concatenate([e[:, 3 * w:2432], e[:, 2496:2528], e[:, 2560:]], axis=1)
    dwuq = dwuq_ext.reshape(Q_LORA, 8, 128)[:, :, :96].reshape(Q_LORA, 768)
    dwukv = jnp.concatenate([dwk_ext.reshape(KV_LORA, 8, 128)[:, :, :64], dwv.reshape(KV_LORA, 8, 64)], axis=2)
    return [jnp.stack([e[:, 0:w], e[:, w:2 * w], e[:, 2 * w:3 * w], last]), _shard_cols(dwuq),
            _shard_cols(dwukv.reshape(KV_LORA, 1024))]


def _rope_tables(positions):
    half = QK_ROPE // 2
    freq = ROPE_THETA ** (-jnp.arange(half, dtype=F32) / half)
    s = positions.shape[0]
    per = LANES // half
    ang = jnp.repeat(positions.astype(F32).reshape(s // per, per), half, axis=1) * jnp.tile(freq, per)
    cos, sin = lax.optimization_barrier((jnp.cos(ang), jnp.sin(ang)))
    cos, sin = cos.reshape(s, half), sin.reshape(s, half)
    z = lambda n: jnp.zeros((s, n), F32)
    c_t = jnp.concatenate([jnp.ones((s, 64), F32), cos, cos, z(32)], axis=1)
    sa_t = jnp.concatenate([z(64), -sin, z(16), z(32)], axis=1)
    sb_t = jnp.concatenate([z(64), z(16), sin, z(32)], axis=1)
    return c_t, sa_t, sb_t


def _local_grads(x, p, positions, tgt, gains, early, late):
    win_ext, wuq_ext, wk_ext, wv = _extend_early(early)
    tabs = _rope_tables(positions)
    g = gains
    sbq, sbk, sbv, sbg, mlag, cq, ckv, qc, kc, mv, sbkt, sbvt, kct, mvt = _pre_fwd(
        x, tabs, g["norm_pre_g"], win_ext, g["q_norm_g"], wuq_ext, g["kv_norm_g"], wk_ext, wv)
    sbo, wout4, wple4, wpg4 = _sb_fwd(sbq, sbk, sbvt, late)
    wout, wpg = wout4.reshape(D_MODEL, D_MODEL), wpg4.reshape(D_MODEL, D_MODEL)
    wple = wple4.transpose(1, 0, 2).reshape(PLE_DIM, D_MODEL)
    mlao, lse = _mla_fwd(qc, kc, mvt)
    dsbo, dmlao, delta, dsbg, dmlag, dxres, dwout, dwpg, dwple, vec_c = _post(
        x, p, tgt, sbo, mlao, sbg, mlag, g["sb_out_norm_g"], g["mla_out_norm_g"], wout, g["norm_post_g"], wple,
        g["ple_norm_g"], wpg, g["b_ple_gate"])
    dsbq, dsbk, dsbv, *late_grads = _sb_bwd(sbq, sbk, sbkt, sbv, dsbo, [_shard_rows(dwout), _shard_cols(dwple), _shard_rows(dwpg)])
    dqc, dkc, dmv = _mla_bwd(qc, kc, kct, mv, dmlao, lse, delta)
    gx, dwin_ext, dwuq_ext, dwk_ext, dwv, vec_d = _pre_bwd(
        x, dxres, dsbq, dsbk, dsbv, dsbg, dmlag, dqc, dkc, dmv, cq, ckv, tabs, g["norm_pre_g"], win_ext, g["q_norm_g"],
        wuq_ext, g["kv_norm_g"], wk_ext, wv)
    return gx, _shard_early_grads(dwin_ext, dwuq_ext, dwk_ext, dwv), late_grads, jnp.concatenate([vec_c, vec_d], axis=0)


_VEC_LAYOUT = (("norm_post_g", 0, 0, 1024), ("ple_norm_g", 1, 0, 1024), ("b_ple_gate", 2, 0, 1024), ("sb_out_norm_g", 3, 0, 512),
               ("mla_out_norm_g", 3, 512, 512), ("norm_pre_g", 8, 0, 1024), ("q_norm_g", 9, 0, 256), ("kv_norm_g", 9, 256, 128))
_LOSS_ROW = 4
_WEIGHT_ORDER = ("norm_pre_g", "w_in", "q_norm_g", "w_uq", "kv_norm_g", "w_ukv", "sb_out_norm_g", "mla_out_norm_g", "w_out",
                 "norm_post_g", "w_ple", "ple_norm_g", "w_ple_gate", "b_ple_gate")


def kernel(x, p, positions, norm_pre_g, w_in, q_norm_g, w_uq, kv_norm_g, w_ukv, sb_out_norm_g, mla_out_norm_g, w_out, norm_post_g, w_ple, ple_norm_g, w_ple_gate, b_ple_gate, loss_target, m_norm_pre_g, m_w_in, m_q_norm_g, m_w_uq, m_kv_norm_g, m_w_ukv, m_sb_out_norm_g, m_mla_out_norm_g, m_w_out, m_norm_post_g, m_w_ple, m_ple_norm_g, m_w_ple_gate, m_b_ple_gate, v_norm_pre_g, v_w_in, v_q_norm_g, v_w_uq, v_kv_norm_g, v_w_ukv, v_sb_out_norm_g, v_mla_out_norm_g, v_w_out, v_norm_post_g, v_w_ple, v_ple_norm_g, v_w_ple_gate, v_b_ple_gate):
    w = {"norm_pre_g": norm_pre_g, "w_in": w_in[0], "q_norm_g": q_norm_g, "w_uq": w_uq[0], "kv_norm_g": kv_norm_g, "w_ukv": w_ukv[0],
         "sb_out_norm_g": sb_out_norm_g, "mla_out_norm_g": mla_out_norm_g, "w_out": w_out[0], "norm_post_g": norm_post_g,
         "w_ple": w_ple[0], "ple_norm_g": ple_norm_g, "w_ple_gate": w_ple_gate[0], "b_ple_gate": b_ple_gate}
    m = {"norm_pre_g": m_norm_pre_g, "w_in": m_w_in[0], "q_norm_g": m_q_norm_g, "w_uq": m_w_uq[0], "kv_norm_g": m_kv_norm_g,
         "w_ukv": m_w_ukv[0], "sb_out_norm_g": m_sb_out_norm_g, "mla_out_norm_g": m_mla_out_norm_g, "w_out": m_w_out[0],
         "norm_post_g": m_norm_post_g, "w_ple": m_w_ple[0], "ple_norm_g": m_ple_norm_g, "w_ple_gate": m_w_ple_gate[0],
         "b_ple_gate": m_b_ple_gate}
    v = {"norm_pre_g": v_norm_pre_g, "w_in": v_w_in[0], "q_norm_g": v_q_norm_g, "w_uq": v_w_uq[0], "kv_norm_g": v_kv_norm_g,
         "w_ukv": v_w_ukv[0], "sb_out_norm_g": v_sb_out_norm_g, "mla_out_norm_g": v_mla_out_norm_g, "w_out": v_w_out[0],
         "norm_post_g": v_norm_post_g, "w_ple": v_w_ple[0], "ple_norm_g": v_ple_norm_g, "w_ple_gate": v_w_ple_gate[0],
         "b_ple_gate": v_b_ple_gate}
    gathered = _allgather_weights([w[n] for n in _EARLY])
    gx, early_grads, late_red, vec = _local_grads(x[0], p[0, 0], positions[0], loss_target[0], w, dict(zip(_EARLY, gathered)),
                                                  [w[n] for n in _LATE])
    *early_red, vsum = _reduce_scatter_grads(early_grads, vec)
    gred = early_red + late_red
    loss = vsum[_LOSS_ROW, 0]

    g, delta, new_m, new_v = _adamw_small(vsum, w, m, v)
    for n, gn in zip(_BIG, gred):
        g[n], delta[n], new_m[n], new_v[n] = _adamw(w[n], gn, m[n], v[n])

    lead = lambda n, a: a[None] if n in _BIG else a
    return (loss, gx[None],
            *[lead(n, g[n]) for n in _WEIGHT_ORDER], *[lead(n, delta[n]) for n in _WEIGHT_ORDER],
            *[lead(n, new_m[n]) for n in _WEIGHT_ORDER], *[lead(n, new_v[n]) for n in _WEIGHT_ORDER])
```

```python
import numpy as np
import jax
import jax.numpy as jnp
from jax import lax
from jax.experimental import pallas as pl
from jax.experimental.pallas import tpu as pltpu

F32 = jnp.float32
BF16 = jnp.bfloat16
MESH = pl.DeviceIdType.MESH

D_MODEL = 1024
HEAD_DIM = 64
D_SB = 512
D_MLA = 512
Q_LORA = 256
KV_LORA = 128
QK_NOPE = 64
QK_ROPE = 32
PLE_DIM = 256
D_IN = 2976
D_EXT = 3072
ROPE_THETA = 10000.0
EPS = 1e-6
N_SHARD = 4

ADAM_LR = 0.001
ADAM_B1 = 0.9
ADAM_B2 = 0.999
ADAM_EPS = 1e-08
ADAM_WD = 0.01
ADAM_STEP = 10

LANES = 128
BK = 128
WQ = 256
SB_NSUB = 2
MQ_FWD = 4096
MQ_BWD = 1024
MLA_CW = 256
SB_CUTOFF = 120.0
TM = 256
TM_PRE = 256
VEC_ROWS = 16
VMEM_DENSE = 52 * 1024 * 1024
VMEM_ATTN = 40 * 1024 * 1024


def _mm(a, b):
    return jnp.dot(a, b, preferred_element_type=F32)


def _mm_nt(a, b):
    return lax.dot_general(a, b, (((1,), (1,)), ((), ())), preferred_element_type=F32)


def _mm_tn(a, b):
    return lax.dot_general(a, b, (((0,), (0,)), ((), ())), preferred_element_type=F32)


def _seg(a, bd2):
    return _mm(_split2(a), bd2)


def _const(mask):
    return jnp.asarray(np.asarray(mask, np.float32), dtype=BF16)


def _blockdiag2(n, seg):
    r = (np.arange(2 * n)[:, None] % n) // seg
    c = np.arange(n)[None, :] // seg
    return _const(r == c)


def _sigmoid(a):
    return 1.0 / (1.0 + jnp.exp(-a))


def _rowmean(a):
    return jnp.mean(a, axis=-1, keepdims=True)


def _colsum(a):
    return jnp.sum(a, axis=0, keepdims=True)


def _rope_fwd(a, c, sa, sb):
    w = a.shape[-1]
    return a * c + pltpu.roll(a, w - 16, 1) * sa + pltpu.roll(a, 16, 1) * sb


def _rope_bwd(g, c, sa, sb):
    w = g.shape[-1]
    return g * c + pltpu.roll(g * sa, 16, 1) + pltpu.roll(g * sb, w - 16, 1)


def _full(shape):
    return pl.BlockSpec(shape, lambda *_: (0,) * len(shape))


def _acc(shape):
    return pl.BlockSpec(shape, lambda *_: (0,) * len(shape))


def _full2(shape):
    return pl.BlockSpec(shape, lambda p, i: (0, 0))


def _cols(height, tm=TM):
    return pl.BlockSpec((height, tm), lambda i: (0, i))


def _rows(width, tm=TM):
    return pl.BlockSpec((tm, width), lambda i: (i, 0))


def _pre_fwd(x, tabs, gpre, win, gq, wuq, gkv, wk, wv):
    s = x.shape[0]
    c_t, sa_t, sb_t = tabs
    rw, cl = (lambda width: _rows(width, TM_PRE)), (lambda height: _cols(height, TM_PRE))

    def body(x_ref, c_ref, sa_ref, sb_ref, gpre_ref, win_ref, gq_ref, wuq_ref, gkv_ref, wk_ref, wv_ref,
             sbq_ref, sbk_ref, sbv_ref, sbg_ref, mlag_ref, cq_ref, ckv_ref, qc_ref, kc_ref, mv_ref,
             sbkt_ref, sbvt_ref, kct_ref, mvt_ref):
        xv = x_ref[...]
        r1 = lax.rsqrt(_rowmean(xv * xv) + EPS)
        h = (xv * r1 * gpre_ref[...]).astype(BF16)
        proj = _mm(h, win_ref[...])
        sbq_ref[...] = proj[:, 0:512].astype(BF16)
        sbk_ref[...] = proj[:, 512:1024].astype(BF16)
        sbv_ref[...] = proj[:, 1024:1536].astype(BF16)
        sbkt_ref[...] = proj[:, 512:1024].T.astype(BF16)
        sbvt_ref[...] = proj[:, 1024:1536].T.astype(BF16)
        sbg_ref[...] = proj[:, 1536:2048]
        cq = proj[:, 2048:2304]
        ckv = proj[:, 2304:2432]
        kr = proj[:, 2432:2560]
        mlag_ref[...] = proj[:, 2560:3072]
        cq_ref[...] = cq
        ckv_ref[...] = ckv
        c1, sa1, sb1 = c_ref[...], sa_ref[...], sb_ref[...]
        c8, sa8, sb8 = jnp.tile(c1, (1, 8)), jnp.tile(sa1, (1, 8)), jnp.tile(sb1, (1, 8))
        cqn = (cq * lax.rsqrt(_rowmean(cq * cq) + EPS) * gq_ref[...]).astype(BF16)
        qe = _mm(cqn, wuq_ref[...])
        qc_ref[...] = _rope_fwd(qe, c8, sa8, sb8).astype(BF16)
        ckvn = (ckv * lax.rsqrt(_rowmean(ckv * ckv) + EPS) * gkv_ref[...]).astype(BF16)
        ke = _mm(ckvn, wk_ref[...])
        krr = _rope_fwd(kr, c1, sa1, sb1)
        kcat = ke + jnp.tile(krr, (1, 8))
        kc_ref[...] = kcat.astype(BF16)
        kct_ref[...] = kcat.T.astype(BF16)
        mval = _mm(ckvn, wv_ref[...])
        mv_ref[...] = mval.astype(BF16)
        mvt_ref[...] = mval.T.astype(BF16)

    out_shape = (
        jax.ShapeDtypeStruct((s, 512), BF16), jax.ShapeDtypeStruct((s, 512), BF16), jax.ShapeDtypeStruct((s, 512), BF16),
        jax.ShapeDtypeStruct((s, 512), F32), jax.ShapeDtypeStruct((s, 512), F32),
        jax.ShapeDtypeStruct((s, Q_LORA), F32), jax.ShapeDtypeStruct((s, KV_LORA), F32),
        jax.ShapeDtypeStruct((s, 1024), BF16), jax.ShapeDtypeStruct((s, 1024), BF16), jax.ShapeDtypeStruct((s, 512), BF16),
        jax.ShapeDtypeStruct((512, s), BF16), jax.ShapeDtypeStruct((512, s), BF16), jax.ShapeDtypeStruct((1024, s), BF16),
        jax.ShapeDtypeStruct((512, s), BF16),
    )
    return pl.pallas_call(
        body, name="pre_fwd", grid=(s // TM_PRE,), out_shape=out_shape,
        in_specs=[rw(D_MODEL), rw(LANES), rw(LANES), rw(LANES), _full((1, D_MODEL)), _full((D_MODEL, D_EXT)),
                  _full((1, Q_LORA)), _full((Q_LORA, 1024)), _full((1, KV_LORA)), _full((KV_LORA, 1024)), _full((KV_LORA, 512))],
        out_specs=(rw(512), rw(512), rw(512), rw(512), rw(512), rw(Q_LORA), rw(KV_LORA),
                   rw(1024), rw(1024), rw(512), cl(512), cl(512), cl(1024), cl(512)),
        compiler_params=pltpu.CompilerParams(vmem_limit_bytes=VMEM_DENSE),
    )(x, c_t, sa_t, sb_t, gpre, win, gq, wuq, gkv, wk, wv)


def _softplus(z):
    neg_abs = lax.bitcast_convert_type(lax.bitcast_convert_type(z, jnp.uint32) | jnp.uint32(0x80000000), F32)
    return jnp.maximum(z, 0.0) + jnp.log(1.0 + jnp.exp(neg_abs))


def _sum_matrix(kind, terms):
    r, c = np.arange(2 * BK)[:, None], np.arange(2 * BK * terms)[None, :] % (2 * BK)
    rk, ck = r % BK, c % BK
    return _const(((r // BK) == (c // BK)) & {"suffix": ck >= rk, "prefix": ck <= rk}[kind])


def _split_rows(a):
    hi = a.astype(BF16)
    return jnp.concatenate([hi, (a - hi.astype(F32)).astype(BF16)], axis=0)


def _heads_t(blk, rowi):
    zero = jnp.zeros_like(blk)
    return jnp.concatenate([jnp.where(rowi < 64, blk, zero), jnp.where(rowi >= 64, blk, zero)], axis=1)


def _mask_keys(a, valid, fill=0.0):
    return jnp.concatenate([jnp.where(valid, a[0:BK], fill), jnp.where(valid, a[BK:2 * BK], fill)], axis=0)


def _split2(a):
    hi = a.astype(BF16)
    lo = (a - hi.astype(F32)).astype(BF16)
    return jnp.concatenate([hi, lo], axis=1)


def _pair_stack(b, lane):
    zero = jnp.zeros_like(b)
    return jnp.concatenate([jnp.where(lane < 64, b, zero), jnp.where(lane >= 64, b, zero)], axis=0)


def _sb_fwd(q, k, vt, late):
    s = q.shape[0]
    n = len(late)

    def body(q_ref, k_ref, vt_ref, usuf_ref, *rest):
        ins, o_ref, outs = rest[:n], rest[n], rest[n + 1:2 * n + 1]
        acc_scr, run_scr = rest[2 * n + 1:2 * n + 3]
        bufs, (send_sems, recv_sems, out_sems) = rest[2 * n + 3:3 * n + 3], rest[3 * n + 3:]
        p, i = pl.program_id(0), pl.program_id(1)
        gather_start, gather_forward, gather_finish = _gather_steps([a.shape for a in late], ins, bufs, send_sems, recv_sems)

        @pl.when((p == 0) & (i == 0))
        def _():
            gather_start()

        @pl.when((p == 2) & (i == 0))
        def _():
            gather_forward()

        lane = lax.broadcasted_iota(jnp.int32, (1, LANES), 1)
        rowi = lax.broadcasted_iota(jnp.int32, (LANES, 1), 0)
        keyi = lax.broadcasted_iota(jnp.int32, (BK, WQ), 0)

        def group(i, qs, blocks, masked, seen=None):
            seen = seen or [0] * len(blocks)
            qryi = lax.broadcasted_iota(jnp.int32, (BK, WQ), 1) + i * WQ
            starts = [pl.multiple_of(j * BK, BK) for j in blocks]
            valid = [(keyi[:, lo:] + j * BK) < qryi[:, lo:] if m else None for j, m, lo in zip(blocks, masked, seen)]
            zs = [_mm_nt(_pair_stack(k_ref[pl.ds(ks, BK), :], lane), qs[lo:]) for ks, lo in zip(starts, seen)]
            sps = [_softplus(z) for z in zs]
            sps = [sp if ok is None else _mask_keys(sp, ok) for sp, ok in zip(sps, valid)]
            cums = [_mm(usuf_ref[...], _split_rows(sp)) for sp in sps]
            ws = [jnp.exp(z - c) for z, c in zip(zs, cums)]
            ws = [w if ok is None else _mask_keys(w, ok) for w, ok in zip(ws, valid)]
            pvs = [_mm(_heads_t(vt_ref[:, pl.ds(ks, BK)], rowi), w.astype(BF16)) for ks, w in zip(starts, ws)]
            for pv, c, lo in zip(pvs, cums, seen):
                r0, r1 = run_scr[0:1, lo:], run_scr[1:2, lo:]
                acc_scr[:, lo:] += jnp.where(rowi < 64, jnp.exp(-r0), jnp.exp(-r1)) * pv
                run_scr[0:1, lo:] = r0 + c[0:1]
                run_scr[1:2, lo:] = r1 + c[BK:BK + 1]

        assert WQ == 2 * BK

        def unfinished():
            return (jnp.min(run_scr[0:2, :]) < SB_CUTOFF).astype(jnp.int32)

        def query_block(sub):
            i = pl.program_id(1) * SB_NSUB + sub
            rows = pl.ds(pl.multiple_of(sub * WQ, WQ), WQ)
            qs = q_ref[rows, :] * (HEAD_DIM ** -0.5)
            acc_scr[...] = jnp.zeros_like(acc_scr)
            run_scr[...] = jnp.zeros_like(run_scr)

            @pl.when(i == 0)
            def _():
                group(i, qs, [1, 0], [True, True], [BK, 0])

            @pl.when(i > 0)
            def _():
                group(i, qs, [2 * i + 1, 2 * i, 2 * i - 1, 2 * i - 2], [True, True, False, False], [BK, 0, 0, 0])

            def step(c):
                group(i, qs, [2 * i - 1 - 2 * c[0], 2 * i - 2 - 2 * c[0]], [False, False])
                return c[0] + 1, unfinished()

            lax.while_loop(lambda c: (c[0] < i) & (c[1] > 0), step, (jnp.int32(1), unfinished()))
            o_ref[rows, :] = acc_scr[...].T

        lax.fori_loop(0, SB_NSUB, lambda sub, c: (query_block(sub), c)[1], 0)

        @pl.when((p == pl.num_programs(0) - 1) & (i == pl.num_programs(1) - 1))
        def _():
            gather_finish()
            copies = [pltpu.make_async_copy(bufs[t], outs[t], out_sems.at[t]) for t in range(n)]
            for cp in copies:
                cp.start()
            for cp in copies:
                cp.wait()

    qspec = pl.BlockSpec((SB_NSUB * WQ, LANES), lambda p, i: (i, p))
    kspec = pl.BlockSpec((s, LANES), lambda p, i: (0, p))
    tspec = pl.BlockSpec((LANES, s), lambda p, i: (p, 0))
    gathered = [jax.ShapeDtypeStruct((N_SHARD,) + a.shape, BF16) for a in late]
    return pl.pallas_call(
        body, name="sb_fwd", grid=(4, s // (SB_NSUB * WQ)),
        out_shape=(jax.ShapeDtypeStruct((s, 512), F32), *gathered),
        in_specs=[qspec, kspec, tspec, _full2((2 * BK, 4 * BK))] + [_full2(a.shape) for a in late],
        out_specs=(qspec,) + (pl.BlockSpec(memory_space=pl.ANY),) * n,
        scratch_shapes=[pltpu.VMEM((LANES, WQ), F32), pltpu.VMEM((8, WQ), F32)] + [pltpu.VMEM(g.shape, BF16) for g in gathered]
                       + [pltpu.SemaphoreType.DMA((6 * n,)), pltpu.SemaphoreType.DMA((6 * n,)), pltpu.SemaphoreType.DMA((n,))],
        compiler_params=pltpu.CompilerParams(vmem_limit_bytes=VMEM_ATTN),
    )(q, k, vt, _sum_matrix("suffix", 2), *late)


def _sb_bwd(q, k, kt, v, do, late):
    s = q.shape[0]
    n = len(late)
    halves = [a.shape[1] // 2 for a in late]

    def body(q_ref, k_ref, kt_ref, v_ref, do_ref, usuf_ref, upre_ref, *rest):
        g_refs, (dq_ref, dk_ref, dv_ref), outs = rest[:n], rest[n:n + 3], rest[n + 3:2 * n + 3]
        later_scr, dqt_scr, st_scr = rest[2 * n + 3:2 * n + 6]
        f_scr, reduce_scr, out_sems = rest[2 * n + 6:3 * n + 6], rest[3 * n + 6:-1], rest[-1]
        p, i = pl.program_id(0), pl.program_id(1)
        reduce_load, reduce_partial, reduce_total, reduce_finish = _reduce_steps(halves, g_refs, f_scr, reduce_scr)

        @pl.when((p == 0) & (i == 0))
        def _():
            reduce_load()

        @pl.when((p == 1) & (i == 0))
        def _():
            reduce_partial()

        @pl.when((p == 3) & (i == 0))
        def _():
            reduce_total()

        @pl.when(i == 0)
        def _():
            dk_ref[...] = jnp.zeros_like(dk_ref)
            dv_ref[...] = jnp.zeros_like(dv_ref)

        lane = lax.broadcasted_iota(jnp.int32, (1, LANES), 1)
        rowi = lax.broadcasted_iota(jnp.int32, (LANES, 1), 0)
        keyi = lax.broadcasted_iota(jnp.int32, (BK, WQ), 0)
        assert WQ == 2 * BK

        def query_block(sub):
            i = pl.program_id(1) * SB_NSUB + sub
            rows = pl.ds(pl.multiple_of(sub * WQ, WQ), WQ)
            qryi = lax.broadcasted_iota(jnp.int32, (BK, WQ), 1) + i * WQ
            qs = q_ref[rows, :] * (HEAD_DIM ** -0.5)
            dob = do_ref[rows, :]
            dot = dob.astype(F32).T.astype(BF16)

            def scores(j, lo=0):
                return _mm_nt(_pair_stack(k_ref[pl.ds(pl.multiple_of(j * BK, BK), BK), :], lane), qs[lo:])

            def scan(blocks, masked, seen=None):
                seen = seen or [0] * len(blocks)
                sps = [_softplus(scores(j, lo)) for j, lo in zip(blocks, seen)]
                sps = [_mask_keys(sp, (keyi[:, lo:] + j * BK) < qryi[:, lo:]) if m else sp
                       for sp, j, m, lo in zip(sps, blocks, masked, seen)]
                for sp, j, lo in zip(sps, blocks, seen):
                    run = st_scr[0:2, :]
                    later_scr[j, 0:2, :] = run
                    st_scr[0:2, lo:] = run[:, lo:] + jnp.concatenate([jnp.sum(sp[0:BK], axis=0, keepdims=True),
                                                                      jnp.sum(sp[BK:2 * BK], axis=0, keepdims=True)], axis=0)

            def sweep(blocks, masked, seen=None):
                seen = seen or [0] * len(blocks)
                starts = [pl.multiple_of(j * BK, BK) for j in blocks]
                valid = [(keyi[:, lo:] + j * BK) < qryi[:, lo:] if m else None for j, m, lo in zip(blocks, masked, seen)]
                zs = [scores(j, lo) for j, lo in zip(blocks, seen)]
                us = [jnp.exp(lax.bitcast_convert_type(lax.bitcast_convert_type(z, jnp.uint32) | jnp.uint32(0x80000000), F32))
                      for z in zs]
                sps = [jnp.maximum(z, 0.0) + jnp.log(1.0 + u) for z, u in zip(zs, us)]
                sps = [sp if ok is None else _mask_keys(sp, ok) for sp, ok in zip(sps, valid)]
                sigs = [jnp.where(z >= 0.0, 1.0, u) / (1.0 + u) for z, u in zip(zs, us)]
                cums = [_mm(usuf_ref[...], _split_rows(sp)) for sp in sps]
                dws = [_mm(_pair_stack(v_ref[pl.ds(ks, BK), :], lane), dot[:, lo:]) for ks, lo in zip(starts, seen)]
                wfs = []
                for z, c, j, ok, lo in zip(zs, cums, blocks, valid, seen):
                    f = jnp.exp(-later_scr[j, 0:2, lo:])
                    wide = (BK, WQ - lo)
                    wf = jnp.exp(z - c) * jnp.concatenate([jnp.broadcast_to(f[0:1], wide), jnp.broadcast_to(f[1:2], wide)], axis=0)
                    wfs.append(wf if ok is None else _mask_keys(wf, ok))
                es = [dw * wf for dw, wf in zip(dws, wfs)]
                pres = [_mm(upre_ref[...], e.astype(BF16)) for e in es]
                dzs = []
                for e, pre, sig, ok, lo in zip(es, pres, sigs, valid, seen):
                    e0 = pre[0:BK] + st_scr[0:1, lo:]
                    e1 = pre[BK:2 * BK] + st_scr[1:2, lo:]
                    st_scr[0:1, lo:] = e0[BK - 1:BK]
                    st_scr[1:2, lo:] = e1[BK - 1:BK]
                    dz = e - sig * jnp.concatenate([e0, e1], axis=0)
                    dzs.append((dz if ok is None else _mask_keys(dz, ok)).astype(BF16))
                whole = [b for b, lo in enumerate(seen) if lo == 0]
                dqt_scr[...] += _mm(jnp.concatenate([_heads_t(kt_ref[:, pl.ds(starts[b], BK)], rowi) for b in whole], axis=1),
                                    jnp.concatenate([dzs[b] for b in whole], axis=0))
                for b, lo in enumerate(seen):
                    if lo:
                        dqt_scr[:, lo:] += _mm(_heads_t(kt_ref[:, pl.ds(starts[b], BK)], rowi), dzs[b])
                for ks, dz, wf, lo in zip(starts, dzs, wfs, seen):
                    rk = _mm(dz, qs[lo:])
                    dk_ref[pl.ds(ks, BK), :] += jnp.where(lane < 64, rk[0:BK], rk[BK:2 * BK])
                    rv = _mm(wf.astype(BF16), dob[lo:])
                    dv_ref[pl.ds(ks, BK), :] += jnp.where(lane < 64, rv[0:BK], rv[BK:2 * BK])

            st_scr[...] = jnp.zeros_like(st_scr)

            @pl.when(i == 0)
            def _():
                scan([1, 0], [True, True], [BK, 0])

            @pl.when(i > 0)
            def _():
                scan([2 * i + 1, 2 * i, 2 * i - 1, 2 * i - 2], [True, True, False, False], [BK, 0, 0, 0])

            def unfinished():
                return (jnp.min(st_scr[0:2, :]) < SB_CUTOFF).astype(jnp.int32)

            def step(c):
                scan([2 * i - 1 - 2 * c[0], 2 * i - 2 - 2 * c[0]], [False, False])
                return c[0] + 1, unfinished()

            npairs, _ = lax.while_loop(lambda c: (c[0] < i) & (c[1] > 0), step, (jnp.minimum(i, 1), unfinished()))

            st_scr[...] = jnp.zeros_like(st_scr)
            dqt_scr[...] = jnp.zeros_like(dqt_scr)
            first = 2 * (i - npairs)

            def early(t, carry):
                sweep([first + 2 * t, first + 2 * t + 1], [False, False])
                return carry

            lax.fori_loop(0, npairs - 1, early, 0)

            @pl.when(i == 0)
            def _():
                sweep([0, 1], [True, True], [0, BK])

            @pl.when(i > 0)
            def _():
                sweep([2 * i - 2, 2 * i - 1, 2 * i, 2 * i + 1], [False, False, True, True], [0, 0, 0, BK])

            dq_ref[rows, :] = (dqt_scr[...].T * (HEAD_DIM ** -0.5)).astype(BF16)

        lax.fori_loop(0, SB_NSUB, lambda sub, c: (query_block(sub), c)[1], 0)

        @pl.when((p == pl.num_programs(0) - 1) & (i == pl.num_programs(1) - 1))
        def _():
            reduce_finish()
            copies = [pltpu.make_async_copy(f_scr[t], outs[t], out_sems.at[t]) for t in range(n)]
            for cp in copies:
                cp.start()
            for cp in copies:
                cp.wait()

    qspec = pl.BlockSpec((SB_NSUB * WQ, LANES), lambda p, i: (i, p))
    kspec = pl.BlockSpec((s, LANES), lambda p, i: (0, p))
    tspec = pl.BlockSpec((LANES, s), lambda p, i: (p, 0))
    anywhere = pl.BlockSpec(memory_space=pl.ANY)
    reduced = [jax.ShapeDtypeStruct(a.shape[1:], F32) for a in late]
    return pl.pallas_call(
        body, name="sb_bwd", grid=(4, s // (SB_NSUB * WQ)),
        out_shape=(jax.ShapeDtypeStruct((s, 512), BF16), jax.ShapeDtypeStruct((s, 512), F32),
                   jax.ShapeDtypeStruct((s, 512), F32), *reduced),
        in_specs=[qspec, kspec, tspec, kspec, qspec, _full2((2 * BK, 4 * BK)), _full2((2 * BK, 2 * BK))] + [anywhere] * n,
        out_specs=(qspec, kspec, kspec) + (anywhere,) * n,
        scratch_shapes=[pltpu.VMEM((s // BK, 8, WQ), F32), pltpu.VMEM((LANES, WQ), F32), pltpu.VMEM((8, WQ), F32)]
                       + [pltpu.VMEM(r.shape, F32) for r in reduced] + _reduce_scratch(late) + [pltpu.SemaphoreType.DMA((n,))],
        compiler_params=pltpu.CompilerParams(vmem_limit_bytes=VMEM_ATTN),
    )(q, k, kt, v, do, _sum_matrix("suffix", 2), _sum_matrix("prefix", 1), *late)


MLA_SCALE = (QK_NOPE + QK_ROPE) ** -0.5
LOG2E = 1.4426950408889634


def _mla_keys(kb):
    zero = jnp.zeros((BK, LANES), kb.dtype)
    return jnp.concatenate([jnp.concatenate([kb[:, 0:LANES], zero], axis=1),
                            jnp.concatenate([zero, kb[:, LANES:2 * LANES]], axis=1)], axis=0)


def _mla_fwd(qc, kc, vt):
    s = qc.shape[0]
    mq = min(MQ_FWD, s)
    rows_l = 16

    def body(q_ref, k_ref, vt_ref, o_ref, l_ref, p_scr, ot_scr, st_scr):
        i = pl.program_id(1)
        row = lax.broadcasted_iota(jnp.int32, (LANES, 1), 0)
        orow = lax.broadcasted_iota(jnp.int32, (rows_l, 2 * BK), 0)
        ocol = lax.broadcasted_iota(jnp.int32, (rows_l, 2 * BK), 1)
        ones = jnp.where(((orow == 0) & (ocol < BK)) | ((orow == 1) & (ocol >= BK)), 1.0, 0.0).astype(BF16)

        def chunks(lo, hi):
            return [(a, min(a + MLA_CW, hi)) for a in range(lo, hi, MLA_CW)]

        def keys(j):
            return _mla_keys(k_ref[pl.ds(pl.multiple_of(j * BK, BK), BK), :])

        def values_t(j):
            vtb = vt_ref[:, pl.ds(pl.multiple_of(j * BK, BK), BK)]
            zero = jnp.zeros_like(vtb)
            top = jnp.concatenate([jnp.where(row < 64, vtb, zero), jnp.where(row >= 64, vtb, zero)], axis=1)
            return jnp.concatenate([top, ones], axis=0)

        def pair_values(ja):
            return jnp.concatenate([values_t(ja), values_t(ja + 1)], axis=1)

        def softmax(ja, za, zb, masked, a, b):
            c = MLA_SCALE * LOG2E
            parts = [za[0:BK] * c, za[BK:2 * BK] * c, zb[0:BK] * c, zb[BK:2 * BK] * c]
            if masked:
                keyc = lax.broadcasted_iota(jnp.int32, (BK, b - a), 0)
                qryc = (lax.broadcasted_iota(jnp.int32, (BK, b - a), 1) + (i * mq + a)) // 64
                va = ((keyc + ja * BK) // 64) <= qryc
                vb = ((keyc + (ja + 1) * BK) // 64) <= qryc
                parts = [jnp.where(va, parts[0], -1e30), jnp.where(va, parts[1], -1e30),
                         jnp.where(vb, parts[2], -1e30), jnp.where(vb, parts[3], -1e30)]
            m0, m1 = st_scr[0:1, a:b], st_scr[1:2, a:b]
            n0 = jnp.maximum(m0, jnp.max(jnp.maximum(parts[0], parts[2]), axis=0, keepdims=True))
            n1 = jnp.maximum(m1, jnp.max(jnp.maximum(parts[1], parts[3]), axis=0, keepdims=True))
            st_scr[2:3, a:b] = jnp.exp2(m0 - n0)
            st_scr[3:4, a:b] = jnp.exp2(m1 - n1)
            st_scr[0:1, a:b] = n0
            st_scr[1:2, a:b] = n1
            p_scr[:, a:b] = jnp.concatenate([jnp.exp2(parts[0] - n0), jnp.exp2(parts[1] - n1),
                                             jnp.exp2(parts[2] - n0), jnp.exp2(parts[3] - n1)], axis=0).astype(BF16)

        def accumulate(vals, a, b):
            pv = _mm(vals, p_scr[:, a:b])
            f = jnp.where(row < 64, st_scr[2:3, a:b], st_scr[3:4, a:b])
            ot_scr[0:LANES, a:b] = f * ot_scr[0:LANES, a:b] + pv[0:LANES]
            ot_scr[LANES:LANES + 8, a:b] = st_scr[2:10, a:b] * ot_scr[LANES:LANES + 8, a:b] + pv[LANES:LANES + 8]

        def step(n, diag, lo=0, prev_lo=0):
            kab = jnp.concatenate([keys(2 * n), keys(2 * n + 1)], axis=0)
            vals = pair_values(2 * n - 2)
            for a, b in chunks(prev_lo, lo):
                accumulate(vals, a, b)
            for a, b in chunks(lo, mq):
                zab = _mm_nt(kab, q_ref[a:b, :])
                accumulate(vals, a, b)
                softmax(2 * n, zab[0:2 * BK], zab[2 * BK:4 * BK], diag and a < lo + 2 * BK, a, b)

        def first(diag):
            kab = jnp.concatenate([keys(0), keys(1)], axis=0)
            for a, b in chunks(0, mq):
                zab = _mm_nt(kab, q_ref[a:b, :])
                softmax(0, zab[0:2 * BK], zab[2 * BK:4 * BK], diag and a < 2 * BK, a, b)

        st_scr[...] = jnp.concatenate([jnp.full((2, mq), -1e30, F32), jnp.ones((14, mq), F32)], axis=0)
        ot_scr[...] = jnp.zeros_like(ot_scr)

        npq = mq // (2 * BK)
        seen = lambda d: 2 * BK * max(d, 0)

        @pl.when(i == 0)
        def _():
            first(True)
            for d in range(1, npq):
                step(d, True, seen(d), seen(d - 1))

        if s > mq:
            @pl.when(i > 0)
            def _():
                first(False)
                lax.fori_loop(1, npq * i, lambda n, c: (step(n, False), c)[1], 0)
                for d in range(npq):
                    step(npq * i + d, True, seen(d), seen(d - 1))

        vals = pair_values(2 * (npq * (i + 1) - 1))
        for a, b in chunks(seen(npq - 1), mq):
            accumulate(vals, a, b)
        for a, b in chunks(0, mq):
            l0, l1 = ot_scr[LANES:LANES + 1, a:b], ot_scr[LANES + 1:LANES + 2, a:b]
            o_ref[a:b, :] = (ot_scr[0:LANES, a:b] / jnp.where(row < 64, l0, l1)).T
            l_ref[a:b, :] = jnp.where(row < 64, st_scr[0:1, a:b] + jnp.log2(l0), st_scr[1:2, a:b] + jnp.log2(l1)).T

    qspec = pl.BlockSpec((mq, 2 * LANES), lambda p, i: (i, p))
    kspec = pl.BlockSpec((s, 2 * LANES), lambda p, i: (0, p))
    vtspec = pl.BlockSpec((LANES, s), lambda p, i: (p, 0))
    ospec = pl.BlockSpec((mq, LANES), lambda p, i: (i, p))
    return pl.pallas_call(
        body, name="mla_fwd", grid=(4, s // mq),
        out_shape=(jax.ShapeDtypeStruct((s, 512), F32), jax.ShapeDtypeStruct((s, 512), F32)),
        in_specs=[qspec, kspec, vtspec], out_specs=(ospec, ospec),
        scratch_shapes=[pltpu.VMEM((4 * BK, mq), BF16), pltpu.VMEM((LANES + 8, mq), F32), pltpu.VMEM((16, mq), F32)],
        compiler_params=pltpu.CompilerParams(vmem_limit_bytes=VMEM_ATTN),
    )(qc, kc, vt)


def _mla_bwd(qc, kc, kct, v, do, lse, delta):
    s = qc.shape[0]
    mq = min(MQ_BWD, s)

    def body(q_ref, k_ref, kt_ref, v_ref, do_ref, l_ref, d_ref, dq_ref, dk_ref, dv_ref, dqt_scr, p_scr, dz_scr,
             dvt_scr):
        i = pl.program_id(1)

        @pl.when(i == 0)
        def _():
            dk_ref[...] = jnp.zeros_like(dk_ref)
            dvt_scr[...] = jnp.zeros_like(dvt_scr)

        lane = lax.broadcasted_iota(jnp.int32, (1, LANES), 1)
        keyc = lax.broadcasted_iota(jnp.int32, (BK, mq), 0)
        qryc = (lax.broadcasted_iota(jnp.int32, (BK, mq), 1) + i * mq) // 64
        qw = q_ref[...]
        dob = do_ref[...]
        dost = (dob.astype(F32) * MLA_SCALE).T.astype(BF16)
        dot_ = dob.astype(F32).T.astype(BF16)
        lt = l_ref[...].T
        dt = (d_ref[...] * MLA_SCALE).T
        lse0, lse1 = lt[0:1], lt[64:65]
        dl0, dl1 = dt[0:1], dt[64:65]
        dqt_scr[...] = jnp.zeros_like(dqt_scr)

        def products(j, lo=0):
            ks = pl.multiple_of(j * BK, BK)
            return (_mm_nt(_mla_keys(k_ref[pl.ds(ks, BK), :]), qw[lo:]),
                    _mm(_pair_stack(v_ref[pl.ds(ks, BK), :], lane), dost[:, lo:]))

        def grads(j, slot, zt, dwt, masked, lo=0):
            zt = zt * (MLA_SCALE * LOG2E)
            p0 = jnp.exp2(zt[0:BK] - lse0[:, lo:])
            p1 = jnp.exp2(zt[BK:2 * BK] - lse1[:, lo:])
            if masked:
                valid = ((keyc[:, lo:] + j * BK) // 64) <= qryc[:, lo:]
                p0, p1 = jnp.where(valid, p0, 0.0), jnp.where(valid, p1, 0.0)
            rows = slice(slot * BK, (slot + 1) * BK)
            p_scr[0, rows, lo:] = p0.astype(BF16)
            p_scr[1, rows, lo:] = p1.astype(BF16)
            dz_scr[0, rows, lo:] = (p0 * (dwt[0:BK] - dl0[:, lo:])).astype(BF16)
            dz_scr[1, rows, lo:] = (p1 * (dwt[BK:2 * BK] - dl1[:, lo:])).astype(BF16)

        def scatter(ja, lo=0):
            ks = pl.multiple_of(ja * BK, 2 * BK)
            for h in range(2):
                cols, vrows = slice(h * LANES, (h + 1) * LANES), slice(h * 64, (h + 1) * 64)
                dzh = dz_scr[h, :, lo:]
                dqt_scr[cols, lo:] += _mm(kt_ref[cols, pl.ds(ks, 2 * BK)], dzh)
                dk_ref[pl.ds(ks, 2 * BK), cols] += _mm(dzh, qw[lo:, cols])
                dvt_scr[vrows, pl.ds(ks, 2 * BK)] += _mm_nt(dot_[vrows, lo:], p_scr[h, :, lo:])

        def step(n, masked, lo=0, prev_lo=0):
            za, wa = products(2 * n, lo)
            zb, wb = products(2 * n + 1, lo)
            scatter(2 * n - 2, prev_lo)
            grads(2 * n, 0, za, wa, masked, lo)
            grads(2 * n + 1, 1, zb, wb, masked, lo)

        def first(masked):
            za, wa = products(0)
            zb, wb = products(1)
            grads(0, 0, za, wa, masked)
            grads(1, 1, zb, wb, masked)

        npq = mq // (2 * BK)
        seen = lambda d: 2 * BK * max(d, 0)

        @pl.when(i == 0)
        def _():
            first(True)
            for d in range(1, npq):
                step(d, True, seen(d), seen(d - 1))

        @pl.when(i > 0)
        def _():
            first(False)
            step(1, False)
            lax.fori_loop(1, npq * i // 2, lambda m, c: (step(2 * m, False), step(2 * m + 1, False), c)[2], 0)
            for d in range(npq):
                step(npq * i + d, True, seen(d), seen(d - 1))

        scatter(2 * (npq * (i + 1) - 1), seen(npq - 1))
        dq_ref[...] = dqt_scr[...].T

        @pl.when(i == s // mq - 1)
        def _():
            for a in range(0, s, 512):
                dv_ref[a:a + 512, :] = dvt_scr[:, a:a + 512].T

    qspec = pl.BlockSpec((mq, 2 * LANES), lambda p, i: (i, p))
    kspec = pl.BlockSpec((s, 2 * LANES), lambda p, i: (0, p))
    ktspec = pl.BlockSpec((2 * LANES, s), lambda p, i: (p, 0))
    vspec = pl.BlockSpec((s, LANES), lambda p, i: (0, p))
    ospec = pl.BlockSpec((mq, LANES), lambda p, i: (i, p))
    return pl.pallas_call(
        body, name="mla_bwd", grid=(4, s // mq),
        out_shape=(jax.ShapeDtypeStruct((s, 1024), F32), jax.ShapeDtypeStruct((s, 1024), F32),
                   jax.ShapeDtypeStruct((s, 512), F32)),
        in_specs=[qspec, kspec, ktspec, vspec, ospec, ospec, ospec], out_specs=(qspec, kspec, vspec),
        scratch_shapes=[pltpu.VMEM((2 * LANES, mq), F32), pltpu.VMEM((2, 2 * BK, mq), BF16), pltpu.VMEM((2, 2 * BK, mq), BF16),
                        pltpu.VMEM((LANES, s), F32)],
        compiler_params=pltpu.CompilerParams(vmem_limit_bytes=VMEM_ATTN),
    )(qc, kc, kct, v, do, lse, delta)


def _post(x, p, tgt, sbo, mlao, sbg, mlag, gsb, gmla, wout, gpost, wple, gple, wpg, bpg):
    s = x.shape[0]

    def body(x_ref, p_ref, t_ref, sbo_ref, mlao_ref, sbg_ref, mlag_ref, gsb_ref, gmla_ref, wout_ref,
             gpost_ref, wple_ref, gple_ref, wpg_ref, bpg_ref, bd_ref,
             dsbo_ref, dmlao_ref, delta_ref, dsbg_ref, dmlag_ref, dxres_ref, dwout_ref, dwpg_ref, dwple_ref, vec_ref):
        i = pl.program_id(0)

        @pl.when(i == 0)
        def _():
            dwout_ref[...] = jnp.zeros_like(dwout_ref)
            dwpg_ref[...] = jnp.zeros_like(dwpg_ref)
            dwple_ref[...] = jnp.zeros_like(dwple_ref)
            vec_ref[...] = jnp.zeros_like(vec_ref)

        inv_hd = 1.0 / HEAD_DIM

        def head_fwd(o, g, gate):
            r = lax.rsqrt(_seg(o * o, bd_ref[...]) * inv_hd + EPS)
            hat = o * r
            n = hat * g
            sg = _sigmoid(gate)
            return hat, r, n, sg, n * (gate * sg)

        sbo, mlao, sbg_v, mlag_v = sbo_ref[...], mlao_ref[...], sbg_ref[...], mlag_ref[...]
        gsb_v, gmla_v = gsb_ref[...], gmla_ref[...]
        sb_hat, sb_r, sb_n, sb_sg, sb_y = head_fwd(sbo, gsb_v, sbg_v)
        ml_hat, ml_r, ml_n, ml_sg, ml_y = head_fwd(mlao, gmla_v, mlag_v)
        mix = jnp.concatenate([sb_y, ml_y], axis=1).astype(BF16)
        y = _mm(mix, wout_ref[...])
        ry = lax.rsqrt(_rowmean(y * y) + EPS)
        y_hat = y * ry
        gpost_v = gpost_ref[...]
        x1 = x_ref[...] + y_hat * gpost_v
        pb = p_ref[...].astype(BF16)
        pl_ = jnp.concatenate([_mm(pb, wple_ref[k]) for k in range(N_SHARD)], axis=1)
        rp = lax.rsqrt(_rowmean(pl_ * pl_) + EPS)
        pl_hat = pl_ * rp
        gple_v = gple_ref[...]
        ple = pl_hat * gple_v
        x1b = x1.astype(BF16)
        gate = _sigmoid(_mm(x1b, wpg_ref[...]) + bpg_ref[...])
        err = x1 + ple * gate - t_ref[...]
        loss = 0.5 * jnp.sum(_rowmean(err * err))
        dout = err * (1.0 / D_MODEL)

        du = dout * ple * gate * (1.0 - gate)
        dub = du.astype(BF16)
        dple = dout * gate
        dx1 = dout + _mm_nt(dub, wpg_ref[...])
        dwpg_ref[...] += _mm_tn(x1b, dub)
        dplh = dple * gple_v
        dpl = rp * (dplh - pl_hat * _rowmean(dplh * pl_hat))
        dplb = dpl.astype(BF16)
        wsh = D_MODEL // N_SHARD
        for k in range(N_SHARD):
            dwple_ref[k] += _mm_tn(pb, dplb[:, k * wsh:(k + 1) * wsh])
        dxres_ref[...] = dx1
        dyh = dx1 * gpost_v
        dy = ry * (dyh - y_hat * _rowmean(dyh * y_hat))
        dyb = dy.astype(BF16)
        dwout_ref[...] += _mm_tn(mix, dyb)
        dmix = _mm_nt(dyb, wout_ref[...])

        def head_bwd(dyv, hat, r, n, sg, g, gate):
            dn = dyv * (gate * sg)
            dgate = dyv * n * (sg * (1.0 + gate * (1.0 - sg)))
            dhat = dn * g
            do = r * (dhat - hat * (_seg(dhat * hat, bd_ref[...]) * inv_hd))
            return do, dgate, _colsum(dn * hat)

        dsbo, dsbg, dg_sb = head_bwd(dmix[:, 0:512], sb_hat, sb_r, sb_n, sb_sg, gsb_v, sbg_v)
        dmlao, dmlag, dg_ml = head_bwd(dmix[:, 512:1024], ml_hat, ml_r, ml_n, ml_sg, gmla_v, mlag_v)
        dsbo_ref[...] = dsbo.astype(BF16)
        dmlao_ref[...] = dmlao.astype(BF16)
        delta_ref[...] = _seg(dmlao * mlao, bd_ref[...])
        dsbg_ref[...] = dsbg.astype(BF16)
        dmlag_ref[...] = dmlag.astype(BF16)
        vec_ref[pl.ds(0, 1), :] += _colsum(dx1 * y_hat)
        vec_ref[pl.ds(1, 1), :] += _colsum(dple * pl_hat)
        vec_ref[pl.ds(2, 1), :] += _colsum(du)
        vec_ref[pl.ds(3, 1), :] += jnp.concatenate([dg_sb, dg_ml], axis=1)
        vec_ref[pl.ds(4, 1), :] += jnp.full((1, D_MODEL), loss, F32)

    out_shape = (
        jax.ShapeDtypeStruct((s, 512), BF16), jax.ShapeDtypeStruct((s, 512), BF16), jax.ShapeDtypeStruct((s, 512), F32),
        jax.ShapeDtypeStruct((s, 512), BF16), jax.ShapeDtypeStruct((s, 512), BF16), jax.ShapeDtypeStruct((s, D_MODEL), F32),
        jax.ShapeDtypeStruct((D_MODEL, D_MODEL), F32), jax.ShapeDtypeStruct((D_MODEL, D_MODEL), F32),
        jax.ShapeDtypeStruct(wple.shape, F32), jax.ShapeDtypeStruct((8, D_MODEL), F32),
    )
    return pl.pallas_call(
        body, name="post_fwd_bwd", grid=(s // TM,), out_shape=out_shape,
        in_specs=[_rows(D_MODEL), _rows(PLE_DIM), _rows(D_MODEL), _rows(512), _rows(512), _rows(512), _rows(512),
                  _full((1, 512)), _full((1, 512)), _full((D_MODEL, D_MODEL)),
                  _full((1, D_MODEL)), _full(wple.shape), _full((1, D_MODEL)), _full((D_MODEL, D_MODEL)),
                  _full((1, D_MODEL)), _full((1024, 512))],
        out_specs=(_rows(512), _rows(512), _rows(512), _rows(512), _rows(512), _rows(D_MODEL),
                   _acc((D_MODEL, D_MODEL)), _acc((D_MODEL, D_MODEL)), _acc(wple.shape), _acc((8, D_MODEL))),
        compiler_params=pltpu.CompilerParams(vmem_limit_bytes=VMEM_DENSE),
    )(x, p, tgt, sbo, mlao, sbg, mlag, gsb, gmla, wout, gpost, wple, gple, wpg, bpg, _blockdiag2(512, HEAD_DIM))


def _pre_bwd(x, dxres, dsbq, dsbk, dsbv, dsbg, dmlag, dqc, dkc, dmv, cq, ckv, tabs, gpre, win, gq, wuq, gkv, wk, wv):
    s = x.shape[0]
    c_t, sa_t, sb_t = tabs
    rw = _rows

    def body(x_ref, dxres_ref, dsbq_ref, dsbk_ref, dsbv_ref, dsbg_ref, dmlag_ref, dqc_ref, dkc_ref, dmv_ref, cq_ref,
             ckv_ref, c_ref, sa_ref, sb_ref, gpre_ref, win_ref, gq_ref, wuq_ref, gkv_ref, wk_ref, wv_ref,
             gx_ref, dwin_ref, dwuq_ref, dwk_ref, dwv_ref, vec_ref, dwin_acc):
        i = pl.program_id(0)

        @pl.when(i == 0)
        def _():
            dwin_acc[...] = jnp.zeros_like(dwin_acc)
            dwuq_ref[...] = jnp.zeros_like(dwuq_ref)
            dwk_ref[...] = jnp.zeros_like(dwk_ref)
            dwv_ref[...] = jnp.zeros_like(dwv_ref)
            vec_ref[...] = jnp.zeros_like(vec_ref)

        lane = lax.broadcasted_iota(jnp.int32, (1, LANES), 1)
        c1, sa1, sb1 = c_ref[...], sa_ref[...], sb_ref[...]
        c8, sa8, sb8 = jnp.tile(c1, (1, 8)), jnp.tile(sa1, (1, 8)), jnp.tile(sb1, (1, 8))

        def norm_bwd(dn, hat, r, g):
            t = dn * g
            return r * (t - hat * _rowmean(t * hat)), _colsum(dn * hat)

        xv = x_ref[...]
        r1 = lax.rsqrt(_rowmean(xv * xv) + EPS)
        x_hat = xv * r1
        gpre_v = gpre_ref[...]
        hb = (x_hat * gpre_v).astype(BF16)
        ready = jnp.concatenate([dsbq_ref[...], dsbk_ref[...].astype(BF16), dsbv_ref[...].astype(BF16), dsbg_ref[...]], axis=1)
        dmlag = dmlag_ref[...]
        dwin_acc[:, 0:2048] += _mm_tn(hb, ready)
        dwin_acc[:, 2560:3072] += _mm_tn(hb, dmlag)
        dh = _mm_nt(ready, win_ref[:, 0:2048]) + _mm_nt(dmlag, win_ref[:, 2560:3072])

        dqeb = _rope_bwd(dqc_ref[...], c8, sa8, sb8).astype(BF16)
        cq = cq_ref[...]
        rq = lax.rsqrt(_rowmean(cq * cq) + EPS)
        cq_hat = cq * rq
        gq_v = gq_ref[...]
        dwuq_ref[...] += _mm_tn((cq_hat * gq_v).astype(BF16), dqeb)
        dcq, dg_q = norm_bwd(_mm_nt(dqeb, wuq_ref[...]), cq_hat, rq, gq_v)

        dkc = dkc_ref[...]
        dkcb = dkc.astype(BF16)
        dmvb = dmv_ref[...].astype(BF16)
        ckv = ckv_ref[...]
        rkv = lax.rsqrt(_rowmean(ckv * ckv) + EPS)
        ckv_hat = ckv * rkv
        gkv_v = gkv_ref[...]
        ckvnb = (ckv_hat * gkv_v).astype(BF16)
        dwk_ref[...] += _mm_tn(ckvnb, dkcb)
        dwv_ref[...] += _mm_tn(ckvnb, dmvb)
        dckv, dg_kv = norm_bwd(_mm_nt(dkcb, wk_ref[...]) + _mm_nt(dmvb, wv_ref[...]), ckv_hat, rkv, gkv_v)

        dkr = dkc[:, 0:LANES]
        for hh in range(1, 8):
            dkr = dkr + dkc[:, LANES * hh:LANES * (hh + 1)]
        dkr = _rope_bwd(dkr, c1, sa1, sb1)
        dkr = jnp.where((lane >= 64) & (lane < 96), dkr, 0.0)

        late = jnp.concatenate([dcq.astype(BF16), dckv.astype(BF16), dkr.astype(BF16)], axis=1)
        dwin_acc[:, 2048:2560] += _mm_tn(hb, late)
        dx, dg_pre = norm_bwd(dh + _mm_nt(late, win_ref[:, 2048:2560]), x_hat, r1, gpre_v)
        gx_ref[...] = dxres_ref[...] + dx
        vec_ref[pl.ds(0, 1), :] += dg_pre
        vec_ref[pl.ds(1, 1), :] += jnp.concatenate([dg_q, dg_kv, jnp.zeros((1, D_MODEL - Q_LORA - KV_LORA), F32)], axis=1)

        @pl.when(i == pl.num_programs(0) - 1)
        def _():
            pltpu.sync_copy(dwin_acc, dwin_ref)

    out_shape = (
        jax.ShapeDtypeStruct((s, D_MODEL), F32), jax.ShapeDtypeStruct((D_MODEL, D_EXT), F32),
        jax.ShapeDtypeStruct((Q_LORA, 1024), F32), jax.ShapeDtypeStruct((KV_LORA, 1024), F32),
        jax.ShapeDtypeStruct((KV_LORA, 512), F32), jax.ShapeDtypeStruct((8, D_MODEL), F32),
    )
    return pl.pallas_call(
        body, name="pre_bwd", grid=(s // TM,), out_shape=out_shape,
        in_specs=[rw(D_MODEL), rw(D_MODEL), rw(512), rw(512), rw(512), rw(512), rw(512),
                  rw(1024), rw(1024), rw(512), rw(Q_LORA), rw(KV_LORA), rw(LANES), rw(LANES),
                  rw(LANES), _full((1, D_MODEL)), _full((D_MODEL, D_EXT)), _full((1, Q_LORA)), _full((Q_LORA, 1024)),
                  _full((1, KV_LORA)), _full((KV_LORA, 1024)), _full((KV_LORA, 512))],
        out_specs=(rw(D_MODEL), pl.BlockSpec(memory_space=pl.ANY), _acc((Q_LORA, 1024)), _acc((KV_LORA, 1024)),
                   _acc((KV_LORA, 512)), _acc((8, D_MODEL))),
        scratch_shapes=[pltpu.VMEM((D_MODEL, D_EXT), F32)],
        compiler_params=pltpu.CompilerParams(vmem_limit_bytes=VMEM_DENSE),
    )(x, dxres, dsbq, dsbk, dsbv, dsbg, dmlag, dqc, dkc, dmv, cq, ckv, c_t, sa_t, sb_t, gpre, win, gq, wuq, gkv, wk, wv)


def _place():
    return lax.axis_index("x"), lax.axis_index("y"), lax.axis_index("c")


def _gather_steps(shapes, ins, bufs, send_sems, recv_sems):
    n = len(shapes)
    x, y, c = _place()
    me, sib = (x, y, c), (x, y, 1 - c)
    chips = [(1 - x, y), (x, 1 - y), (1 - x, 1 - y)]

    def half(t, chip, hc):
        rows = shapes[t][0] // 2
        return bufs[t].at[2 * chip[0] + chip[1], pl.ds(pl.multiple_of(hc * rows, 16), rows), :]

    def copy(k, t, chip, hc, to):
        return pltpu.make_async_remote_copy(src_ref=half(t, chip, hc), dst_ref=half(t, chip, hc), send_sem=send_sems.at[k],
                                            recv_sem=recv_sems.at[k], device_id=to, device_id_type=MESH)

    def start():
        for t in range(n):
            bufs[t][2 * x + y] = ins[t][...].astype(BF16)
            for j, chip in enumerate(chips):
                copy(6 * t + j, t, (x, y), c, (*chip, c)).start()

    def forward():
        for t in range(n):
            for j, chip in enumerate(chips):
                copy(6 * t + j, t, chip, c, me).wait_recv()
                copy(6 * t + 3 + j, t, chip, c, sib).start()

    def finish():
        for t in range(n):
            for j, chip in enumerate(chips):
                copy(6 * t + 3 + j, t, chip, 1 - c, me).wait_recv()
        for t in range(n):
            for j, chip in enumerate(chips):
                copy(6 * t + j, t, (x, y), c, (*chip, c)).wait_send()
                copy(6 * t + 3 + j, t, chip, c, sib).wait_send()

    return start, forward, finish


def _allgather_weights(shards):
    n = len(shards)

    def body(*refs):
        start, forward, finish = _gather_steps([a.shape for a in shards], refs[:n], refs[n:2 * n], refs[2 * n], refs[2 * n + 1])
        start()
        forward()
        finish()

    return pl.pallas_call(
        body, name="allgather_weights",
        out_shape=tuple(jax.ShapeDtypeStruct((N_SHARD,) + a.shape, BF16) for a in shards),
        in_specs=[pl.BlockSpec(memory_space=pltpu.VMEM)] * n, out_specs=(pl.BlockSpec(memory_space=pltpu.VMEM),) * n,
        scratch_shapes=[pltpu.SemaphoreType.DMA((6 * n,)), pltpu.SemaphoreType.DMA((6 * n,))],
        compiler_params=pltpu.CompilerParams(vmem_limit_bytes=VMEM_ATTN),
    )(*shards)


def _reduce_scratch(gsh):
    n = len(gsh)
    half_shapes = [(N_SHARD, a.shape[1] // 2, a.shape[2]) for a in gsh]
    return ([pltpu.VMEM(s_, F32) for s_ in half_shapes] * 2 + [pltpu.VMEM(s_, BF16) for s_ in half_shapes] * 2
            + [pltpu.SemaphoreType.DMA((n,)), pltpu.SemaphoreType.DMA((5 * n,)), pltpu.SemaphoreType.DMA((5 * n,))])


def _reduce_steps(halves, g_refs, f_refs, scratch):
    n = len(halves)
    accs, sibs, sbufs, rbufs = scratch[0:n], scratch[n:2 * n], scratch[2 * n:3 * n], scratch[3 * n:4 * n]
    local_sems, send_sems, recv_sems = scratch[4 * n:4 * n + 3]
    x, y, c = _place()
    me, sib = (x, y, c), (x, y, 1 - c)
    mine = 2 * x + y
    chips = [(1 - x, y), (x, 1 - y), (1 - x, 1 - y)]

    def remote(k, src, dst, to):
        return pltpu.make_async_remote_copy(src_ref=src, dst_ref=dst, send_sem=send_sems.at[k], recv_sem=recv_sems.at[k],
                                            device_id=to, device_id_type=MESH)

    def half3(ref, t, hc):
        return ref.at[:, pl.ds(pl.multiple_of(hc * halves[t], 8), halves[t]), :]

    def half2(ref, t, hc):
        return ref.at[pl.ds(pl.multiple_of(hc * halves[t], 8), halves[t]), :]

    def mine_load(t):
        return pltpu.make_async_copy(half3(g_refs[t], t, c), accs[t], local_sems.at[t])

    def to_sibling(t, to):
        return remote(t, half3(g_refs[t], t, 1 - c), sibs[t], to)

    def to_chip(t, j, chip, to):
        idx = 2 * chip[0] + chip[1]
        return remote(n + 3 * t + j, sbufs[t].at[idx], rbufs[t].at[mine if to is not me else idx], to)

    def swap(t, hc, to):
        return remote(4 * n + t, half2(f_refs[t], t, hc), half2(f_refs[t], t, hc), to)

    def load():
        for t in range(n):
            mine_load(t).start()
            to_sibling(t, sib).start()

    def partial():
        for t in range(n):
            mine_load(t).wait()
            to_sibling(t, me).wait_recv()
            for k in range(N_SHARD):
                accs[t][k] = accs[t][k] + sibs[t][k]
            for j, chip in enumerate(chips):
                idx = 2 * chip[0] + chip[1]
                sbufs[t][idx] = accs[t][idx].astype(BF16)
                to_chip(t, j, chip, (*chip, c)).start()

    def total():
        for t in range(n):
            acc = accs[t][mine]
            for j, chip in enumerate(chips):
                to_chip(t, j, chip, me).wait_recv()
                acc = acc + rbufs[t][2 * chip[0] + chip[1]].astype(F32)
            half2(f_refs[t], t, c)[...] = acc
            swap(t, c, sib).start()

    def finish():
        for t in range(n):
            swap(t, 1 - c, me).wait_recv()
        for t in range(n):
            to_sibling(t, sib).wait_send()
            for j, chip in enumerate(chips):
                to_chip(t, j, chip, (*chip, c)).wait_send()
            swap(t, c, sib).wait_send()

    return load, partial, total, finish


def _reduce_scatter_grads(gsh, vec):
    n = len(gsh)
    halves = [a.shape[1] // 2 for a in gsh]

    def body(*refs):
        g_refs, vec_ref, f_refs, vsum_ref = refs[:n], refs[n], refs[n + 1:2 * n + 1], refs[2 * n + 1]
        scratch = refs[2 * n + 2:]
        vrecv, vsend_sems, vrecv_sems = scratch[4 * n + 3:]
        load, partial, total, finish = _reduce_steps(halves, g_refs, f_refs, scratch)
        x, y, c = _place()
        my_dev = 4 * x + 2 * y + c

        def flip(k):
            return x ^ ((k >> 2) & 1), y ^ ((k >> 1) & 1), c ^ (k & 1)

        def vcopy(k, slot, to):
            return pltpu.make_async_remote_copy(src_ref=vec_ref, dst_ref=vrecv.at[slot], send_sem=vsend_sems.at[k - 1],
                                                recv_sem=vrecv_sems.at[k - 1], device_id=to, device_id_type=MESH)

        load()
        vrecv[my_dev] = vec_ref[...]
        for k in range(1, 8):
            vcopy(k, my_dev, flip(k)).start()
        partial()
        total()
        finish()
        for k in range(1, 8):
            fx, fy, fc = flip(k)
            vcopy(k, 4 * fx + 2 * fy + fc, (x, y, c)).wait_recv()
        vs = vrecv[0]
        for d in range(1, 8):
            vs = vs + vrecv[d]
        vsum_ref[...] = vs
        for k in range(1, 8):
            vcopy(k, my_dev, flip(k)).wait_send()

    return pl.pallas_call(
        body, name="reduce_scatter_grads",
        out_shape=tuple(jax.ShapeDtypeStruct(a.shape[1:], F32) for a in gsh) + (jax.ShapeDtypeStruct((VEC_ROWS, 1024), F32),),
        in_specs=[pl.BlockSpec(memory_space=pl.ANY)] * n + [pl.BlockSpec(memory_space=pltpu.VMEM)],
        out_specs=(pl.BlockSpec(memory_space=pltpu.VMEM),) * (n + 1),
        scratch_shapes=_reduce_scratch(gsh) + [pltpu.VMEM((8, VEC_ROWS, 1024), F32), pltpu.SemaphoreType.DMA((7,)),
                                               pltpu.SemaphoreType.DMA((7,))],
        compiler_params=pltpu.CompilerParams(vmem_limit_bytes=56 * 1024 * 1024),
    )(*gsh, vec)


def _adamw(w, g, m, v):
    rows, cols = w.shape
    tr = rows if rows <= 256 else 256
    flip = cols % LANES != 0

    def body(w_ref, g_ref, m_ref, v_ref, g_out, d_ref, nm_ref, nv_ref):
        gv = g_ref[...].T if flip else g_ref[...]
        outs = (gv,) + _adam_math(w_ref[...], gv, m_ref[...], v_ref[...])
        for ref, val in zip((g_out, d_ref, nm_ref, nv_ref), outs):
            ref[...] = val

    spec = pl.BlockSpec((tr, cols), lambda i: (i, 0))
    tspec = pl.BlockSpec((cols, tr), lambda i: (0, i)) if flip else spec
    shp = jax.ShapeDtypeStruct((cols, rows) if flip else (rows, cols), F32)
    if flip:
        w, m, v = w.T, m.T, v.T
    outs = pl.pallas_call(body, name="adamw", grid=(rows // tr,), out_shape=(shp,) * 4,
                          in_specs=[tspec, spec, tspec, tspec], out_specs=(tspec,) * 4)(w, g, m, v)
    return tuple(o.T for o in outs) if flip else outs


def _adam_math(w, g, m, v):
    m2 = ADAM_B1 * m + (1.0 - ADAM_B1) * g
    v2 = ADAM_B2 * v + (1.0 - ADAM_B2) * (g * g)
    m_hat = m2 / (1.0 - ADAM_B1 ** ADAM_STEP)
    v_hat = v2 / (1.0 - ADAM_B2 ** ADAM_STEP)
    return -ADAM_LR * (m_hat / (jnp.sqrt(v_hat) + ADAM_EPS) + ADAM_WD * w), m2, v2


def _adamw_small(vsum, w, m, v):
    names = [name for name, _, _, _ in _VEC_LAYOUT]
    k = len(names)

    def body(*refs):
        vs_ref, w_refs, m_refs, v_refs = refs[0], refs[1:1 + k], refs[1 + k:1 + 2 * k], refs[1 + 2 * k:1 + 3 * k]
        outs = refs[1 + 3 * k:]
        for idx, (_, r, c0, width) in enumerate(_VEC_LAYOUT):
            gv = vs_ref[pl.ds(r, 1), pl.ds(c0, width)]
            d, m2, v2 = _adam_math(w_refs[idx][...], gv, m_refs[idx][...], v_refs[idx][...])
            outs[idx][...], outs[k + idx][...], outs[2 * k + idx][...], outs[3 * k + idx][...] = gv, d, m2, v2

    shapes = tuple(jax.ShapeDtypeStruct(w[name].shape, F32) for name in names)
    res = pl.pallas_call(
        body, name="adamw_small", out_shape=shapes * 4,
        in_specs=[pl.BlockSpec(memory_space=pltpu.VMEM)] * (1 + 3 * k), out_specs=(pl.BlockSpec(memory_space=pltpu.VMEM),) * (4 * k),
    )(vsum, *[w[name] for name in names], *[m[name] for name in names], *[v[name] for name in names])
    return tuple({name: res[part * k + idx] for idx, name in enumerate(names)} for part in range(4))


_EARLY = ("w_in", "w_uq", "w_ukv")
_LATE = ("w_out", "w_ple", "w_ple_gate")
_BIG = _EARLY + _LATE
_KR_LOCAL = 2432 - 3 * (D_IN // N_SHARD)


def _extend_early(parts):
    cols = lambda a: a.transpose(1, 0, 2).reshape(a.shape[1], N_SHARD * a.shape[2])
    g = parts["w_in"]
    zeros = lambda n: jnp.zeros((D_MODEL, n), g.dtype)
    win_ext = jnp.concatenate([g[0], g[1], g[2], g[3][:, :_KR_LOCAL], zeros(64), g[3][:, _KR_LOCAL:_KR_LOCAL + QK_ROPE],
                               zeros(32), g[3][:, _KR_LOCAL + QK_ROPE:]], axis=1)
    wuq_ext = jnp.pad(cols(parts["w_uq"]).reshape(Q_LORA, 8, 96), ((0, 0), (0, 0), (0, 32))).reshape(Q_LORA, 1024)
    wukv = cols(parts["w_ukv"]).reshape(KV_LORA, 8, 128)
    wk_ext = jnp.pad(wukv[:, :, :64], ((0, 0), (0, 0), (0, 64))).reshape(KV_LORA, 1024)
    wv = wukv[:, :, 64:].reshape(KV_LORA, 512)
    return win_ext, wuq_ext, wk_ext, wv


def _shard_cols(a):
    return a.reshape(a.shape[0], N_SHARD, a.shape[1] // N_SHARD).transpose(1, 0, 2)


def _shard_rows(a):
    return a.reshape(N_SHARD, a.shape[0] // N_SHARD, a.shape[1])


def _shard_early_grads(dwin_ext, dwuq_ext, dwk_ext, dwv):
    e, w = dwin_ext, D_IN // N_SHARD
    last = jnp.concatenate([e[:, 3 * w:2432], e[:, 2496:2528], e[:, 2560:]], axis=1)
    dwuq = dwuq_ext.reshape(Q_LORA, 8, 128)[:, :, :96].reshape(Q_LORA, 768)
    dwukv = jnp.concatenate([dwk_ext.reshape(KV_LORA, 8, 128)[:, :, :64], dwv.reshape(KV_LORA, 8, 64)], axis=2)
    return [jnp.stack([e[:, 0:w], e[:, w:2 * w], e[:, 2 * w:3 * w], last]), _shard_cols(dwuq),
            _shard_cols(dwukv.reshape(KV_LORA, 1024))]


def _rope_tables(positions):
    half = QK_ROPE // 2
    freq = ROPE_THETA ** (-jnp.arange(half, dtype=F32) / half)
    s = positions.shape[0]
    per = LANES // half
    ang = jnp.repeat(positions.astype(F32).reshape(s // per, per), half, axis=1) * jnp.tile(freq, per)
    cos, sin = lax.optimization_barrier((jnp.cos(ang), jnp.sin(ang)))
    cos, sin = cos.reshape(s, half), sin.reshape(s, half)
    z = lambda n: jnp.zeros((s, n), F32)
    c_t = jnp.concatenate([jnp.ones((s, 64), F32), cos, cos, z(32)], axis=1)
    sa_t = jnp.concatenate([z(64), -sin, z(16), z(32)], axis=1)
    sb_t = jnp.concatenate([z(64), z(16), sin, z(32)], axis=1)
    return c_t, sa_t, sb_t


def _local_grads(x, p, positions, tgt, gains, early, late):
    win_ext, wuq_ext, wk_ext, wv = _extend_early(early)
    tabs = _rope_tables(positions)
    g = gains
    sbq, sbk, sbv, sbg, mlag, cq, ckv, qc, kc, mv, sbkt, sbvt, kct, mvt = _pre_fwd(
        x, tabs, g["norm_pre_g"], win_ext, g["q_norm_g"], wuq_ext, g["kv_norm_g"], wk_ext, wv)
    sbo, wout4, wple4, wpg4 = _sb_fwd(sbq, sbk, sbvt, late)
    wout, wpg = wout4.reshape(D_MODEL, D_MODEL), wpg4.reshape(D_MODEL, D_MODEL)
    mlao, lse = _mla_fwd(qc, kc, mvt)
    dsbo, dmlao, delta, dsbg, dmlag, dxres, dwout, dwpg, dwple, vec_c = _post(
        x, p, tgt, sbo, mlao, sbg, mlag, g["sb_out_norm_g"], g["mla_out_norm_g"], wout, g["norm_post_g"], wple4,
        g["ple_norm_g"], wpg, g["b_ple_gate"])
    dsbq, dsbk, dsbv, *late_grads = _sb_bwd(sbq, sbk, sbkt, sbv, dsbo, [_shard_rows(dwout), dwple, _shard_rows(dwpg)])
    dqc, dkc, dmv = _mla_bwd(qc, kc, kct, mv, dmlao, lse, delta)
    gx, dwin_ext, dwuq_ext, dwk_ext, dwv, vec_d = _pre_bwd(
        x, dxres, dsbq, dsbk, dsbv, dsbg, dmlag, dqc, dkc, dmv, cq, ckv, tabs, g["norm_pre_g"], win_ext, g["q_norm_g"],
        wuq_ext, g["kv_norm_g"], wk_ext, wv)
    return gx, _shard_early_grads(dwin_ext, dwuq_ext, dwk_ext, dwv), late_grads, jnp.concatenate([vec_c, vec_d], axis=0)


_VEC_LAYOUT = (("norm_post_g", 0, 0, 1024), ("ple_norm_g", 1, 0, 1024), ("b_ple_gate", 2, 0, 1024), ("sb_out_norm_g", 3, 0, 512),
               ("mla_out_norm_g", 3, 512, 512), ("norm_pre_g", 8, 0, 1024), ("q_norm_g", 9, 0, 256), ("kv_norm_g", 9, 256, 128))
_LOSS_ROW = 4
_WEIGHT_ORDER = ("norm_pre_g", "w_in", "q_norm_g", "w_uq", "kv_norm_g", "w_ukv", "sb_out_norm_g", "mla_out_norm_g", "w_out",
                 "norm_post_g", "w_ple", "ple_norm_g", "w_ple_gate", "b_ple_gate")


def kernel(x, p, positions, norm_pre_g, w_in, q_norm_g, w_uq, kv_norm_g, w_ukv, sb_out_norm_g, mla_out_norm_g, w_out, norm_post_g, w_ple, ple_norm_g, w_ple_gate, b_ple_gate, loss_target, m_norm_pre_g, m_w_in, m_q_norm_g, m_w_uq, m_kv_norm_g, m_w_ukv, m_sb_out_norm_g, m_mla_out_norm_g, m_w_out, m_norm_post_g, m_w_ple, m_ple_norm_g, m_w_ple_gate, m_b_ple_gate, v_norm_pre_g, v_w_in, v_q_norm_g, v_w_uq, v_kv_norm_g, v_w_ukv, v_sb_out_norm_g, v_mla_out_norm_g, v_w_out, v_norm_post_g, v_w_ple, v_ple_norm_g, v_w_ple_gate, v_b_ple_gate):
    w = {"norm_pre_g": norm_pre_g, "w_in": w_in[0], "q_norm_g": q_norm_g, "w_uq": w_uq[0], "kv_norm_g": kv_norm_g, "w_ukv": w_ukv[0],
         "sb_out_norm_g": sb_out_norm_g, "mla_out_norm_g": mla_out_norm_g, "w_out": w_out[0], "norm_post_g": norm_post_g,
         "w_ple": w_ple[0], "ple_norm_g": ple_norm_g, "w_ple_gate": w_ple_gate[0], "b_ple_gate": b_ple_gate}
    m = {"norm_pre_g": m_norm_pre_g, "w_in": m_w_in[0], "q_norm_g": m_q_norm_g, "w_uq": m_w_uq[0], "kv_norm_g": m_kv_norm_g,
         "w_ukv": m_w_ukv[0], "sb_out_norm_g": m_sb_out_norm_g, "mla_out_norm_g": m_mla_out_norm_g, "w_out": m_w_out[0],
         "norm_post_g": m_norm_post_g, "w_ple": m_w_ple[0], "ple_norm_g": m_ple_norm_g, "w_ple_gate": m_w_ple_gate[0],
         "b_ple_gate": m_b_ple_gate}
    v = {"norm_pre_g": v_norm_pre_g, "w_in": v_w_in[0], "q_norm_g": v_q_norm_g, "w_uq": v_w_uq[0], "kv_norm_g": v_kv_norm_g,
         "w_ukv": v_w_ukv[0], "sb_out_norm_g": v_sb_out_norm_g, "mla_out_norm_g": v_mla_out_norm_g, "w_out": v_w_out[0],
         "norm_post_g": v_norm_post_g, "w_ple": v_w_ple[0], "ple_norm_g": v_ple_norm_g, "w_ple_gate": v_w_ple_gate[0],
         "b_ple_gate": v_b_ple_gate}
    gathered = _allgather_weights([w[n] for n in _EARLY])
    gx, early_grads, late_red, vec = _local_grads(x[0], p[0, 0], positions[0], loss_target[0], w, dict(zip(_EARLY, gathered)),
                                                  [w[n] for n in _LATE])
    *early_red, vsum = _reduce_scatter_grads(early_grads, vec)
    gred = early_red + late_red
    loss = vsum[_LOSS_ROW, 0]

    g, delta, new_m, new_v = _adamw_small(vsum, w, m, v)
    for n, gn in zip(_BIG, gred):
        g[n], delta[n], new_m[n], new_v[n] = _adamw(w[n], gn, m[n], v[n])

    lead = lambda n, a: a[None] if n in _BIG else a
    return (loss, gx[None],
            *[lead(n, g[n]) for n in _WEIGHT_ORDER], *[lead(n, delta[n]) for n in _WEIGHT_ORDER],
            *[lead(n, new_m[n]) for n in _WEIGHT_ORDER], *[lead(n, new_v[n]) for n in _WEIGHT_ORDER])
```

```python
import numpy as np
import jax
import jax.numpy as jnp
from jax import lax
from jax.experimental import pallas as pl
from jax.experimental.pallas import tpu as pltpu

F32 = jnp.float32
BF16 = jnp.bfloat16
MESH = pl.DeviceIdType.MESH

D_MODEL = 1024
HEAD_DIM = 64
D_SB = 512
D_MLA = 512
Q_LORA = 256
KV_LORA = 128
QK_NOPE = 64
QK_ROPE = 32
PLE_DIM = 256
D_IN = 2976
D_EXT = 3072
ROPE_THETA = 10000.0
EPS = 1e-6
N_SHARD = 4

ADAM_LR = 0.001
ADAM_B1 = 0.9
ADAM_B2 = 0.999
ADAM_EPS = 1e-08
ADAM_WD = 0.01
ADAM_STEP = 10

LANES = 128
BK = 128
WQ = 256
SB_NSUB = 4
MQ_FWD = 4096
MQ_BWD = 1024
MLA_CW = 256
SB_CUTOFF = 120.0
TM = 256
TM_PRE = 256
VEC_ROWS = 16
VMEM_DENSE = 52 * 1024 * 1024
VMEM_ATTN = 40 * 1024 * 1024


def _mm(a, b):
    return jnp.dot(a, b, preferred_element_type=F32)


def _mm_nt(a, b):
    return lax.dot_general(a, b, (((1,), (1,)), ((), ())), preferred_element_type=F32)


def _mm_tn(a, b):
    return lax.dot_general(a, b, (((0,), (0,)), ((), ())), preferred_element_type=F32)


def _seg(a, bd2):
    return _mm(_split2(a), bd2)


def _const(mask):
    return jnp.asarray(np.asarray(mask, np.float32), dtype=BF16)


def _blockdiag2(n, seg):
    r = (np.arange(2 * n)[:, None] % n) // seg
    c = np.arange(n)[None, :] // seg
    return _const(r == c)


def _sigmoid(a):
    return 1.0 / (1.0 + jnp.exp(-a))


def _rowmean(a):
    return jnp.mean(a, axis=-1, keepdims=True)


def _colsum(a):
    return jnp.sum(a, axis=0, keepdims=True)


def _rope_fwd(a, c, sa, sb):
    w = a.shape[-1]
    return a * c + pltpu.roll(a, w - 16, 1) * sa + pltpu.roll(a, 16, 1) * sb


def _rope_bwd(g, c, sa, sb):
    w = g.shape[-1]
    return g * c + pltpu.roll(g * sa, 16, 1) + pltpu.roll(g * sb, w - 16, 1)


def _full(shape):
    return pl.BlockSpec(shape, lambda *_: (0,) * len(shape))


def _acc(shape):
    return pl.BlockSpec(shape, lambda *_: (0,) * len(shape))


def _full2(shape):
    return pl.BlockSpec(shape, lambda p, i: (0, 0))


def _cols(height, tm=TM):
    return pl.BlockSpec((height, tm), lambda i: (0, i))


def _rows(width, tm=TM):
    return pl.BlockSpec((tm, width), lambda i: (i, 0))


def _pre_fwd(x, tabs, gpre, win, gq, wuq, gkv, wk, wv):
    s = x.shape[0]
    c_t, sa_t, sb_t = tabs
    rw, cl = (lambda width: _rows(width, TM_PRE)), (lambda height: _cols(height, TM_PRE))

    def body(x_ref, c_ref, sa_ref, sb_ref, gpre_ref, win_ref, gq_ref, wuq_ref, gkv_ref, wk_ref, wv_ref,
             sbq_ref, sbk_ref, sbv_ref, sbg_ref, mlag_ref, cq_ref, ckv_ref, qc_ref, kc_ref, mv_ref,
             sbkt_ref, sbvt_ref, kct_ref, mvt_ref):
        xv = x_ref[...]
        r1 = lax.rsqrt(_rowmean(xv * xv) + EPS)
        h = (xv * r1 * gpre_ref[...]).astype(BF16)
        proj = _mm(h, win_ref[...])
        sbq_ref[...] = proj[:, 0:512].astype(BF16)
        sbk_ref[...] = proj[:, 512:1024].astype(BF16)
        sbv_ref[...] = proj[:, 1024:1536].astype(BF16)
        sbkt_ref[...] = proj[:, 512:1024].T.astype(BF16)
        sbvt_ref[...] = proj[:, 1024:1536].T.astype(BF16)
        sbg_ref[...] = proj[:, 1536:2048]
        cq = proj[:, 2048:2304]
        ckv = proj[:, 2304:2432]
        kr = proj[:, 2432:2560]
        mlag_ref[...] = proj[:, 2560:3072]
        cq_ref[...] = cq
        ckv_ref[...] = ckv
        c1, sa1, sb1 = c_ref[...], sa_ref[...], sb_ref[...]
        c8, sa8, sb8 = jnp.tile(c1, (1, 8)), jnp.tile(sa1, (1, 8)), jnp.tile(sb1, (1, 8))
        cqn = (cq * lax.rsqrt(_rowmean(cq * cq) + EPS) * gq_ref[...]).astype(BF16)
        qe = _mm(cqn, wuq_ref[...])
        qc_ref[...] = _rope_fwd(qe, c8, sa8, sb8).astype(BF16)
        ckvn = (ckv * lax.rsqrt(_rowmean(ckv * ckv) + EPS) * gkv_ref[...]).astype(BF16)
        ke = _mm(ckvn, wk_ref[...])
        krr = _rope_fwd(kr, c1, sa1, sb1)
        kcat = ke + jnp.tile(krr, (1, 8))
        kc_ref[...] = kcat.astype(BF16)
        kct_ref[...] = kcat.T.astype(BF16)
        mval = _mm(ckvn, wv_ref[...])
        mv_ref[...] = mval.astype(BF16)
        mvt_ref[...] = mval.T.astype(BF16)

    out_shape = (
        jax.ShapeDtypeStruct((s, 512), BF16), jax.ShapeDtypeStruct((s, 512), BF16), jax.ShapeDtypeStruct((s, 512), BF16),
        jax.ShapeDtypeStruct((s, 512), F32), jax.ShapeDtypeStruct((s, 512), F32),
        jax.ShapeDtypeStruct((s, Q_LORA), F32), jax.ShapeDtypeStruct((s, KV_LORA), F32),
        jax.ShapeDtypeStruct((s, 1024), BF16), jax.ShapeDtypeStruct((s, 1024), BF16), jax.ShapeDtypeStruct((s, 512), BF16),
        jax.ShapeDtypeStruct((512, s), BF16), jax.ShapeDtypeStruct((512, s), BF16), jax.ShapeDtypeStruct((1024, s), BF16),
        jax.ShapeDtypeStruct((512, s), BF16),
    )
    return pl.pallas_call(
        body, name="pre_fwd", grid=(s // TM_PRE,), out_shape=out_shape,
        in_specs=[rw(D_MODEL), rw(LANES), rw(LANES), rw(LANES), _full((1, D_MODEL)), _full((D_MODEL, D_EXT)),
                  _full((1, Q_LORA)), _full((Q_LORA, 1024)), _full((1, KV_LORA)), _full((KV_LORA, 1024)), _full((KV_LORA, 512))],
        out_specs=(rw(512), rw(512), rw(512), rw(512), rw(512), rw(Q_LORA), rw(KV_LORA),
                   rw(1024), rw(1024), rw(512), cl(512), cl(512), cl(1024), cl(512)),
        compiler_params=pltpu.CompilerParams(vmem_limit_bytes=VMEM_DENSE),
    )(x, c_t, sa_t, sb_t, gpre, win, gq, wuq, gkv, wk, wv)


def _softplus(z):
    neg_abs = lax.bitcast_convert_type(lax.bitcast_convert_type(z, jnp.uint32) | jnp.uint32(0x80000000), F32)
    return jnp.maximum(z, 0.0) + jnp.log(1.0 + jnp.exp(neg_abs))


def _sum_matrix(kind, terms):
    r, c = np.arange(2 * BK)[:, None], np.arange(2 * BK * terms)[None, :] % (2 * BK)
    rk, ck = r % BK, c % BK
    return _const(((r // BK) == (c // BK)) & {"suffix": ck >= rk, "prefix": ck <= rk}[kind])


def _split_rows(a):
    hi = a.astype(BF16)
    return jnp.concatenate([hi, (a - hi.astype(F32)).astype(BF16)], axis=0)


def _heads_t(blk, rowi):
    zero = jnp.zeros_like(blk)
    return jnp.concatenate([jnp.where(rowi < 64, blk, zero), jnp.where(rowi >= 64, blk, zero)], axis=1)


def _mask_keys(a, valid, fill=0.0):
    return jnp.concatenate([jnp.where(valid, a[0:BK], fill), jnp.where(valid, a[BK:2 * BK], fill)], axis=0)


def _split2(a):
    hi = a.astype(BF16)
    lo = (a - hi.astype(F32)).astype(BF16)
    return jnp.concatenate([hi, lo], axis=1)


def _pair_stack(b, lane):
    zero = jnp.zeros_like(b)
    return jnp.concatenate([jnp.where(lane < 64, b, zero), jnp.where(lane >= 64, b, zero)], axis=0)


def _sb_fwd(q, k, vt, late):
    s = q.shape[0]
    n = len(late)

    def body(q_ref, k_ref, vt_ref, usuf_ref, *rest):
        ins, o_ref, outs = rest[:n], rest[n], rest[n + 1:2 * n + 1]
        acc_scr, run_scr = rest[2 * n + 1:2 * n + 3]
        bufs, (send_sems, recv_sems, out_sems) = rest[2 * n + 3:3 * n + 3], rest[3 * n + 3:]
        p, i = pl.program_id(0), pl.program_id(1)
        gather_start, gather_forward, gather_finish = _gather_steps([a.shape for a in late], ins, bufs, send_sems, recv_sems)

        @pl.when((p == 0) & (i == 0))
        def _():
            gather_start()

        @pl.when((p == 2) & (i == 0))
        def _():
            gather_forward()

        lane = lax.broadcasted_iota(jnp.int32, (1, LANES), 1)
        rowi = lax.broadcasted_iota(jnp.int32, (LANES, 1), 0)
        keyi = lax.broadcasted_iota(jnp.int32, (BK, WQ), 0)

        def group(i, qs, blocks, masked, seen=None):
            seen = seen or [0] * len(blocks)
            qryi = lax.broadcasted_iota(jnp.int32, (BK, WQ), 1) + i * WQ
            starts = [pl.multiple_of(j * BK, BK) for j in blocks]
            valid = [(keyi[:, lo:] + j * BK) < qryi[:, lo:] if m else None for j, m, lo in zip(blocks, masked, seen)]
            zs = [_mm_nt(_pair_stack(k_ref[pl.ds(ks, BK), :], lane), qs[lo:]) for ks, lo in zip(starts, seen)]
            sps = [_softplus(z) for z in zs]
            sps = [sp if ok is None else _mask_keys(sp, ok) for sp, ok in zip(sps, valid)]
            cums = [_mm(usuf_ref[...], _split_rows(sp)) for sp in sps]
            ws = [jnp.exp(z - c) for z, c in zip(zs, cums)]
            ws = [w if ok is None else _mask_keys(w, ok) for w, ok in zip(ws, valid)]
            pvs = [_mm(_heads_t(vt_ref[:, pl.ds(ks, BK)], rowi), w.astype(BF16)) for ks, w in zip(starts, ws)]
            for pv, c, lo in zip(pvs, cums, seen):
                r0, r1 = run_scr[0:1, lo:], run_scr[1:2, lo:]
                acc_scr[:, lo:] += jnp.where(rowi < 64, jnp.exp(-r0), jnp.exp(-r1)) * pv
                run_scr[0:1, lo:] = r0 + c[0:1]
                run_scr[1:2, lo:] = r1 + c[BK:BK + 1]

        assert WQ == 2 * BK

        def unfinished():
            return (jnp.min(run_scr[0:2, :]) < SB_CUTOFF).astype(jnp.int32)

        def query_block(sub):
            i = pl.program_id(1) * SB_NSUB + sub
            rows = pl.ds(pl.multiple_of(sub * WQ, WQ), WQ)
            qs = q_ref[rows, :] * (HEAD_DIM ** -0.5)
            acc_scr[...] = jnp.zeros_like(acc_scr)
            run_scr[...] = jnp.zeros_like(run_scr)

            @pl.when(i == 0)
            def _():
                group(i, qs, [1, 0], [True, True], [BK, 0])

            @pl.when(i > 0)
            def _():
                group(i, qs, [2 * i + 1, 2 * i, 2 * i - 1, 2 * i - 2], [True, True, False, False], [BK, 0, 0, 0])

            def step(c):
                group(i, qs, [2 * i - 1 - 2 * c[0], 2 * i - 2 - 2 * c[0]], [False, False])
                return c[0] + 1, unfinished()

            lax.while_loop(lambda c: (c[0] < i) & (c[1] > 0), step, (jnp.int32(1), unfinished()))
            o_ref[rows, :] = acc_scr[...].T

        lax.fori_loop(0, SB_NSUB, lambda sub, c: (query_block(sub), c)[1], 0)

        @pl.when((p == pl.num_programs(0) - 1) & (i == pl.num_programs(1) - 1))
        def _():
            gather_finish()
            copies = [pltpu.make_async_copy(bufs[t], outs[t], out_sems.at[t]) for t in range(n)]
            for cp in copies:
                cp.start()
            for cp in copies:
                cp.wait()

    qspec = pl.BlockSpec((SB_NSUB * WQ, LANES), lambda p, i: (i, p))
    kspec = pl.BlockSpec((s, LANES), lambda p, i: (0, p))
    tspec = pl.BlockSpec((LANES, s), lambda p, i: (p, 0))
    gathered = [jax.ShapeDtypeStruct((N_SHARD,) + a.shape, BF16) for a in late]
    return pl.pallas_call(
        body, name="sb_fwd", grid=(4, s // (SB_NSUB * WQ)),
        out_shape=(jax.ShapeDtypeStruct((s, 512), F32), *gathered),
        in_specs=[qspec, kspec, tspec, _full2((2 * BK, 4 * BK))] + [_full2(a.shape) for a in late],
        out_specs=(qspec,) + (pl.BlockSpec(memory_space=pl.ANY),) * n,
        scratch_shapes=[pltpu.VMEM((LANES, WQ), F32), pltpu.VMEM((8, WQ), F32)] + [pltpu.VMEM(g.shape, BF16) for g in gathered]
                       + [pltpu.SemaphoreType.DMA((6 * n,)), pltpu.SemaphoreType.DMA((6 * n,)), pltpu.SemaphoreType.DMA((n,))],
        compiler_params=pltpu.CompilerParams(vmem_limit_bytes=VMEM_ATTN),
    )(q, k, vt, _sum_matrix("suffix", 2), *late)


def _sb_bwd(q, k, kt, v, do, late):
    s = q.shape[0]
    n = len(late)
    halves = [a.shape[1] // 2 for a in late]

    def body(q_ref, k_ref, kt_ref, v_ref, do_ref, usuf_ref, upre_ref, *rest):
        g_refs, (dq_ref, dk_ref, dv_ref), outs = rest[:n], rest[n:n + 3], rest[n + 3:2 * n + 3]
        later_scr, dqt_scr, st_scr = rest[2 * n + 3:2 * n + 6]
        f_scr, reduce_scr, out_sems = rest[2 * n + 6:3 * n + 6], rest[3 * n + 6:-1], rest[-1]
        p, i = pl.program_id(0), pl.program_id(1)
        reduce_load, reduce_partial, reduce_total, reduce_finish = _reduce_steps(halves, g_refs, f_scr, reduce_scr)

        @pl.when((p == 0) & (i == 0))
        def _():
            reduce_load()

        @pl.when((p == 1) & (i == 0))
        def _():
            reduce_partial()

        @pl.when((p == 3) & (i == 0))
        def _():
            reduce_total()

        @pl.when(i == 0)
        def _():
            dk_ref[...] = jnp.zeros_like(dk_ref)
            dv_ref[...] = jnp.zeros_like(dv_ref)

        lane = lax.broadcasted_iota(jnp.int32, (1, LANES), 1)
        rowi = lax.broadcasted_iota(jnp.int32, (LANES, 1), 0)
        keyi = lax.broadcasted_iota(jnp.int32, (BK, WQ), 0)
        assert WQ == 2 * BK

        def query_block(sub):
            i = pl.program_id(1) * SB_NSUB + sub
            rows = pl.ds(pl.multiple_of(sub * WQ, WQ), WQ)
            qryi = lax.broadcasted_iota(jnp.int32, (BK, WQ), 1) + i * WQ
            qs = q_ref[rows, :] * (HEAD_DIM ** -0.5)
            dob = do_ref[rows, :]
            dot = dob.astype(F32).T.astype(BF16)

            def scores(j, lo=0):
                return _mm_nt(_pair_stack(k_ref[pl.ds(pl.multiple_of(j * BK, BK), BK), :], lane), qs[lo:])

            def scan(blocks, masked, seen=None):
                seen = seen or [0] * len(blocks)
                sps = [_softplus(scores(j, lo)) for j, lo in zip(blocks, seen)]
                sps = [_mask_keys(sp, (keyi[:, lo:] + j * BK) < qryi[:, lo:]) if m else sp
                       for sp, j, m, lo in zip(sps, blocks, masked, seen)]
                for sp, j, lo in zip(sps, blocks, seen):
                    run = st_scr[0:2, :]
                    later_scr[j, 0:2, :] = run
                    st_scr[0:2, lo:] = run[:, lo:] + jnp.concatenate([jnp.sum(sp[0:BK], axis=0, keepdims=True),
                                                                      jnp.sum(sp[BK:2 * BK], axis=0, keepdims=True)], axis=0)

            def sweep(blocks, masked, seen=None):
                seen = seen or [0] * len(blocks)
                starts = [pl.multiple_of(j * BK, BK) for j in blocks]
                valid = [(keyi[:, lo:] + j * BK) < qryi[:, lo:] if m else None for j, m, lo in zip(blocks, masked, seen)]
                zs = [scores(j, lo) for j, lo in zip(blocks, seen)]
                us = [jnp.exp(lax.bitcast_convert_type(lax.bitcast_convert_type(z, jnp.uint32) | jnp.uint32(0x80000000), F32))
                      for z in zs]
                sps = [jnp.maximum(z, 0.0) + jnp.log(1.0 + u) for z, u in zip(zs, us)]
                sps = [sp if ok is None else _mask_keys(sp, ok) for sp, ok in zip(sps, valid)]
                sigs = [jnp.where(z >= 0.0, 1.0, u) / (1.0 + u) for z, u in zip(zs, us)]
                cums = [_mm(usuf_ref[...], _split_rows(sp)) for sp in sps]
                dws = [_mm(_pair_stack(v_ref[pl.ds(ks, BK), :], lane), dot[:, lo:]) for ks, lo in zip(starts, seen)]
                wfs = []
                for z, c, j, ok, lo in zip(zs, cums, blocks, valid, seen):
                    f = jnp.exp(-later_scr[j, 0:2, lo:])
                    wide = (BK, WQ - lo)
                    wf = jnp.exp(z - c) * jnp.concatenate([jnp.broadcast_to(f[0:1], wide), jnp.broadcast_to(f[1:2], wide)], axis=0)
                    wfs.append(wf if ok is None else _mask_keys(wf, ok))
                es = [dw * wf for dw, wf in zip(dws, wfs)]
                pres = [_mm(upre_ref[...], e.astype(BF16)) for e in es]
                dzs = []
                for e, pre, sig, ok, lo in zip(es, pres, sigs, valid, seen):
                    e0 = pre[0:BK] + st_scr[0:1, lo:]
                    e1 = pre[BK:2 * BK] + st_scr[1:2, lo:]
                    st_scr[0:1, lo:] = e0[BK - 1:BK]
                    st_scr[1:2, lo:] = e1[BK - 1:BK]
                    dz = e - sig * jnp.concatenate([e0, e1], axis=0)
                    dzs.append((dz if ok is None else _mask_keys(dz, ok)).astype(BF16))
                whole = [b for b, lo in enumerate(seen) if lo == 0]
                dqt_scr[...] += _mm(jnp.concatenate([_heads_t(kt_ref[:, pl.ds(starts[b], BK)], rowi) for b in whole], axis=1),
                                    jnp.concatenate([dzs[b] for b in whole], axis=0))
                for b, lo in enumerate(seen):
                    if lo:
                        dqt_scr[:, lo:] += _mm(_heads_t(kt_ref[:, pl.ds(starts[b], BK)], rowi), dzs[b])
                for ks, dz, wf, lo in zip(starts, dzs, wfs, seen):
                    rk = _mm(dz, qs[lo:])
                    dk_ref[pl.ds(ks, BK), :] += jnp.where(lane < 64, rk[0:BK], rk[BK:2 * BK])
                    rv = _mm(wf.astype(BF16), dob[lo:])
                    dv_ref[pl.ds(ks, BK), :] += jnp.where(lane < 64, rv[0:BK], rv[BK:2 * BK])

            st_scr[...] = jnp.zeros_like(st_scr)

            @pl.when(i == 0)
            def _():
                scan([1, 0], [True, True], [BK, 0])

            @pl.when(i > 0)
            def _():
                scan([2 * i + 1, 2 * i, 2 * i - 1, 2 * i - 2], [True, True, False, False], [BK, 0, 0, 0])

            def unfinished():
                return (jnp.min(st_scr[0:2, :]) < SB_CUTOFF).astype(jnp.int32)

            def step(c):
                scan([2 * i - 1 - 2 * c[0], 2 * i - 2 - 2 * c[0]], [False, False])
                return c[0] + 1, unfinished()

            npairs, _ = lax.while_loop(lambda c: (c[0] < i) & (c[1] > 0), step, (jnp.minimum(i, 1), unfinished()))

            st_scr[...] = jnp.zeros_like(st_scr)
            dqt_scr[...] = jnp.zeros_like(dqt_scr)
            first = 2 * (i - npairs)

            def early(t, carry):
                sweep([first + 2 * t, first + 2 * t + 1], [False, False])
                return carry

            lax.fori_loop(0, npairs - 1, early, 0)

            @pl.when(i == 0)
            def _():
                sweep([0, 1], [True, True], [0, BK])

            @pl.when(i > 0)
            def _():
                sweep([2 * i - 2, 2 * i - 1, 2 * i, 2 * i + 1], [False, False, True, True], [0, 0, 0, BK])

            dq_ref[rows, :] = (dqt_scr[...].T * (HEAD_DIM ** -0.5)).astype(BF16)

        lax.fori_loop(0, SB_NSUB, lambda sub, c: (query_block(sub), c)[1], 0)

        @pl.when((p == pl.num_programs(0) - 1) & (i == pl.num_programs(1) - 1))
        def _():
            reduce_finish()
            copies = [pltpu.make_async_copy(f_scr[t], outs[t], out_sems.at[t]) for t in range(n)]
            for cp in copies:
                cp.start()
            for cp in copies:
                cp.wait()

    qspec = pl.BlockSpec((SB_NSUB * WQ, LANES), lambda p, i: (i, p))
    kspec = pl.BlockSpec((s, LANES), lambda p, i: (0, p))
    tspec = pl.BlockSpec((LANES, s), lambda p, i: (p, 0))
    anywhere = pl.BlockSpec(memory_space=pl.ANY)
    reduced = [jax.ShapeDtypeStruct(a.shape[1:], F32) for a in late]
    return pl.pallas_call(
        body, name="sb_bwd", grid=(4, s // (SB_NSUB * WQ)),
        out_shape=(jax.ShapeDtypeStruct((s, 512), BF16), jax.ShapeDtypeStruct((s, 512), F32),
                   jax.ShapeDtypeStruct((s, 512), F32), *reduced),
        in_specs=[qspec, kspec, tspec, kspec, qspec, _full2((2 * BK, 4 * BK)), _full2((2 * BK, 2 * BK))] + [anywhere] * n,
        out_specs=(qspec, kspec, kspec) + (anywhere,) * n,
        scratch_shapes=[pltpu.VMEM((s // BK, 8, WQ), F32), pltpu.VMEM((LANES, WQ), F32), pltpu.VMEM((8, WQ), F32)]
                       + [pltpu.VMEM(r.shape, F32) for r in reduced] + _reduce_scratch(late) + [pltpu.SemaphoreType.DMA((n,))],
        compiler_params=pltpu.CompilerParams(vmem_limit_bytes=VMEM_ATTN),
    )(q, k, kt, v, do, _sum_matrix("suffix", 2), _sum_matrix("prefix", 1), *late)


MLA_SCALE = (QK_NOPE + QK_ROPE) ** -0.5
LOG2E = 1.4426950408889634


def _mla_keys(kb):
    zero = jnp.zeros((BK, LANES), kb.dtype)
    return jnp.concatenate([jnp.concatenate([kb[:, 0:LANES], zero], axis=1),
                            jnp.concatenate([zero, kb[:, LANES:2 * LANES]], axis=1)], axis=0)


def _mla_fwd(qc, kc, vt):
    s = qc.shape[0]
    mq = min(MQ_FWD, s)
    rows_l = 16

    def body(q_ref, k_ref, vt_ref, o_ref, l_ref, p_scr, ot_scr, st_scr):
        i = pl.program_id(1)
        row = lax.broadcasted_iota(jnp.int32, (LANES, 1), 0)
        orow = lax.broadcasted_iota(jnp.int32, (rows_l, 2 * BK), 0)
        ocol = lax.broadcasted_iota(jnp.int32, (rows_l, 2 * BK), 1)
        ones = jnp.where(((orow == 0) & (ocol < BK)) | ((orow == 1) & (ocol >= BK)), 1.0, 0.0).astype(BF16)

        def chunks(lo, hi):
            return [(a, min(a + MLA_CW, hi)) for a in range(lo, hi, MLA_CW)]

        def keys(j):
            return _mla_keys(k_ref[pl.ds(pl.multiple_of(j * BK, BK), BK), :])

        def values_t(j):
            vtb = vt_ref[:, pl.ds(pl.multiple_of(j * BK, BK), BK)]
            zero = jnp.zeros_like(vtb)
            top = jnp.concatenate([jnp.where(row < 64, vtb, zero), jnp.where(row >= 64, vtb, zero)], axis=1)
            return jnp.concatenate([top, ones], axis=0)

        def pair_values(ja):
            return jnp.concatenate([values_t(ja), values_t(ja + 1)], axis=1)

        def softmax(ja, za, zb, masked, a, b):
            c = MLA_SCALE * LOG2E
            parts = [za[0:BK] * c, za[BK:2 * BK] * c, zb[0:BK] * c, zb[BK:2 * BK] * c]
            if masked:
                keyc = lax.broadcasted_iota(jnp.int32, (BK, b - a), 0)
                qryc = (lax.broadcasted_iota(jnp.int32, (BK, b - a), 1) + (i * mq + a)) // 64
                va = ((keyc + ja * BK) // 64) <= qryc
                vb = ((keyc + (ja + 1) * BK) // 64) <= qryc
                parts = [jnp.where(va, parts[0], -1e30), jnp.where(va, parts[1], -1e30),
                         jnp.where(vb, parts[2], -1e30), jnp.where(vb, parts[3], -1e30)]
            m0, m1 = st_scr[0:1, a:b], st_scr[1:2, a:b]
            n0 = jnp.maximum(m0, jnp.max(jnp.maximum(parts[0], parts[2]), axis=0, keepdims=True))
            n1 = jnp.maximum(m1, jnp.max(jnp.maximum(parts[1], parts[3]), axis=0, keepdims=True))
            st_scr[2:3, a:b] = jnp.exp2(m0 - n0)
            st_scr[3:4, a:b] = jnp.exp2(m1 - n1)
            st_scr[0:1, a:b] = n0
            st_scr[1:2, a:b] = n1
            p_scr[:, a:b] = jnp.concatenate([jnp.exp2(parts[0] - n0), jnp.exp2(parts[1] - n1),
                                             jnp.exp2(parts[2] - n0), jnp.exp2(parts[3] - n1)], axis=0).astype(BF16)

        def accumulate(vals, a, b):
            pv = _mm(vals, p_scr[:, a:b])
            f = jnp.where(row < 64, st_scr[2:3, a:b], st_scr[3:4, a:b])
            ot_scr[0:LANES, a:b] = f * ot_scr[0:LANES, a:b] + pv[0:LANES]
            ot_scr[LANES:LANES + 8, a:b] = st_scr[2:10, a:b] * ot_scr[LANES:LANES + 8, a:b] + pv[LANES:LANES + 8]

        def step(n, diag, lo=0, prev_lo=0):
            kab = jnp.concatenate([keys(2 * n), keys(2 * n + 1)], axis=0)
            vals = pair_values(2 * n - 2)
            for a, b in chunks(prev_lo, lo):
                accumulate(vals, a, b)
            for a, b in chunks(lo, mq):
                zab = _mm_nt(kab, q_ref[a:b, :])
                accumulate(vals, a, b)
                softmax(2 * n, zab[0:2 * BK], zab[2 * BK:4 * BK], diag and a < lo + 2 * BK, a, b)

        def first(diag):
            kab = jnp.concatenate([keys(0), keys(1)], axis=0)
            for a, b in chunks(0, mq):
                zab = _mm_nt(kab, q_ref[a:b, :])
                softmax(0, zab[0:2 * BK], zab[2 * BK:4 * BK], diag and a < 2 * BK, a, b)

        st_scr[...] = jnp.concatenate([jnp.full((2, mq), -1e30, F32), jnp.ones((14, mq), F32)], axis=0)
        ot_scr[...] = jnp.zeros_like(ot_scr)

        npq = mq // (2 * BK)
        seen = lambda d: 2 * BK * max(d, 0)

        @pl.when(i == 0)
        def _():
            first(True)
            for d in range(1, npq):
                step(d, True, seen(d), seen(d - 1))

        if s > mq:
            @pl.when(i > 0)
            def _():
                first(False)
                lax.fori_loop(1, npq * i, lambda n, c: (step(n, False), c)[1], 0)
                for d in range(npq):
                    step(npq * i + d, True, seen(d), seen(d - 1))

        vals = pair_values(2 * (npq * (i + 1) - 1))
        for a, b in chunks(seen(npq - 1), mq):
            accumulate(vals, a, b)
        for a, b in chunks(0, mq):
            l0, l1 = ot_scr[LANES:LANES + 1, a:b], ot_scr[LANES + 1:LANES + 2, a:b]
            o_ref[a:b, :] = (ot_scr[0:LANES, a:b] / jnp.where(row < 64, l0, l1)).T
            l_ref[a:b, :] = jnp.where(row < 64, st_scr[0:1, a:b] + jnp.log2(l0), st_scr[1:2, a:b] + jnp.log2(l1)).T

    qspec = pl.BlockSpec((mq, 2 * LANES), lambda p, i: (i, p))
    kspec = pl.BlockSpec((s, 2 * LANES), lambda p, i: (0, p))
    vtspec = pl.BlockSpec((LANES, s), lambda p, i: (p, 0))
    ospec = pl.BlockSpec((mq, LANES), lambda p, i: (i, p))
    return pl.pallas_call(
        body, name="mla_fwd", grid=(4, s // mq),
        out_shape=(jax.ShapeDtypeStruct((s, 512), F32), jax.ShapeDtypeStruct((s, 512), F32)),
        in_specs=[qspec, kspec, vtspec], out_specs=(ospec, ospec),
        scratch_shapes=[pltpu.VMEM((4 * BK, mq), BF16), pltpu.VMEM((LANES + 8, mq), F32), pltpu.VMEM((16, mq), F32)],
        compiler_params=pltpu.CompilerParams(vmem_limit_bytes=VMEM_ATTN),
    )(qc, kc, vt)


def _mla_bwd(qc, kc, kct, v, do, lse, delta):
    s = qc.shape[0]
    mq = min(MQ_BWD, s)

    def body(q_ref, k_ref, kt_ref, v_ref, do_ref, l_ref, d_ref, dq_ref, dk_ref, dv_ref, dqt_scr, p_scr, dz_scr,
             dvt_scr):
        i = pl.program_id(1)

        @pl.when(i == 0)
        def _():
            dk_ref[...] = jnp.zeros_like(dk_ref)
            dvt_scr[...] = jnp.zeros_like(dvt_scr)

        lane = lax.broadcasted_iota(jnp.int32, (1, LANES), 1)
        keyc = lax.broadcasted_iota(jnp.int32, (BK, mq), 0)
        qryc = (lax.broadcasted_iota(jnp.int32, (BK, mq), 1) + i * mq) // 64
        qw = q_ref[...]
        dob = do_ref[...]
        dost = (dob.astype(F32) * MLA_SCALE).T.astype(BF16)
        dot_ = dob.astype(F32).T.astype(BF16)
        lt = l_ref[...].T
        dt = (d_ref[...] * MLA_SCALE).T
        lse0, lse1 = lt[0:1], lt[64:65]
        dl0, dl1 = dt[0:1], dt[64:65]
        dqt_scr[...] = jnp.zeros_like(dqt_scr)

        def products(j, lo=0):
            ks = pl.multiple_of(j * BK, BK)
            return (_mm_nt(_mla_keys(k_ref[pl.ds(ks, BK), :]), qw[lo:]),
                    _mm(_pair_stack(v_ref[pl.ds(ks, BK), :], lane), dost[:, lo:]))

        def grads(j, slot, zt, dwt, masked, lo=0):
            zt = zt * (MLA_SCALE * LOG2E)
            p0 = jnp.exp2(zt[0:BK] - lse0[:, lo:])
            p1 = jnp.exp2(zt[BK:2 * BK] - lse1[:, lo:])
            if masked:
                valid = ((keyc[:, lo:] + j * BK) // 64) <= qryc[:, lo:]
                p0, p1 = jnp.where(valid, p0, 0.0), jnp.where(valid, p1, 0.0)
            rows = slice(slot * BK, (slot + 1) * BK)
            p_scr[0, rows, lo:] = p0.astype(BF16)
            p_scr[1, rows, lo:] = p1.astype(BF16)
            dz_scr[0, rows, lo:] = (p0 * (dwt[0:BK] - dl0[:, lo:])).astype(BF16)
            dz_scr[1, rows, lo:] = (p1 * (dwt[BK:2 * BK] - dl1[:, lo:])).astype(BF16)

        def scatter(ja, lo=0):
            ks = pl.multiple_of(ja * BK, 2 * BK)
            for h in range(2):
                cols, vrows = slice(h * LANES, (h + 1) * LANES), slice(h * 64, (h + 1) * 64)
                dzh = dz_scr[h, :, lo:]
                dqt_scr[cols, lo:] += _mm(kt_ref[cols, pl.ds(ks, 2 * BK)], dzh)
                dk_ref[pl.ds(ks, 2 * BK), cols] += _mm(dzh, qw[lo:, cols])
                dvt_scr[vrows, pl.ds(ks, 2 * BK)] += _mm_nt(dot_[vrows, lo:], p_scr[h, :, lo:])

        def step(n, masked, lo=0, prev_lo=0):
            za, wa = products(2 * n, lo)
            zb, wb = products(2 * n + 1, lo)
            scatter(2 * n - 2, prev_lo)
            grads(2 * n, 0, za, wa, masked, lo)
            grads(2 * n + 1, 1, zb, wb, masked, lo)

        def first(masked):
            za, wa = products(0)
            zb, wb = products(1)
            grads(0, 0, za, wa, masked)
            grads(1, 1, zb, wb, masked)

        npq = mq // (2 * BK)
        seen = lambda d: 2 * BK * max(d, 0)

        @pl.when(i == 0)
        def _():
            first(True)
            for d in range(1, npq):
                step(d, True, seen(d), seen(d - 1))

        @pl.when(i > 0)
        def _():
            first(False)
            step(1, False)
            lax.fori_loop(1, npq * i // 2, lambda m, c: (step(2 * m, False), step(2 * m + 1, False), c)[2], 0)
            for d in range(npq):
                step(npq * i + d, True, seen(d), seen(d - 1))

        scatter(2 * (npq * (i + 1) - 1), seen(npq - 1))
        dq_ref[...] = dqt_scr[...].T

        @pl.when(i == s // mq - 1)
        def _():
            for a in range(0, s, 512):
                dv_ref[a:a + 512, :] = dvt_scr[:, a:a + 512].T

    qspec = pl.BlockSpec((mq, 2 * LANES), lambda p, i: (i, p))
    kspec = pl.BlockSpec((s, 2 * LANES), lambda p, i: (0, p))
    ktspec = pl.BlockSpec((2 * LANES, s), lambda p, i: (p, 0))
    vspec = pl.BlockSpec((s, LANES), lambda p, i: (0, p))
    ospec = pl.BlockSpec((mq, LANES), lambda p, i: (i, p))
    return pl.pallas_call(
        body, name="mla_bwd", grid=(4, s // mq),
        out_shape=(jax.ShapeDtypeStruct((s, 1024), F32), jax.ShapeDtypeStruct((s, 1024), F32),
                   jax.ShapeDtypeStruct((s, 512), F32)),
        in_specs=[qspec, kspec, ktspec, vspec, ospec, ospec, ospec], out_specs=(qspec, kspec, vspec),
        scratch_shapes=[pltpu.VMEM((2 * LANES, mq), F32), pltpu.VMEM((2, 2 * BK, mq), BF16), pltpu.VMEM((2, 2 * BK, mq), BF16),
                        pltpu.VMEM((LANES, s), F32)],
        compiler_params=pltpu.CompilerParams(vmem_limit_bytes=VMEM_ATTN),
    )(qc, kc, kct, v, do, lse, delta)


def _post(x, p, tgt, sbo, mlao, sbg, mlag, gsb, gmla, wout, gpost, wple, gple, wpg, bpg):
    s = x.shape[0]

    def body(x_ref, p_ref, t_ref, sbo_ref, mlao_ref, sbg_ref, mlag_ref, gsb_ref, gmla_ref, wout_ref,
             gpost_ref, wple_ref, gple_ref, wpg_ref, bpg_ref, bd_ref,
             dsbo_ref, dmlao_ref, delta_ref, dsbg_ref, dmlag_ref, dxres_ref, dwout_ref, dwpg_ref, dwple_ref, vec_ref):
        i = pl.program_id(0)

        @pl.when(i == 0)
        def _():
            dwout_ref[...] = jnp.zeros_like(dwout_ref)
            dwpg_ref[...] = jnp.zeros_like(dwpg_ref)
            dwple_ref[...] = jnp.zeros_like(dwple_ref)
            vec_ref[...] = jnp.zeros_like(vec_ref)

        inv_hd = 1.0 / HEAD_DIM

        def head_fwd(o, g, gate):
            r = lax.rsqrt(_seg(o * o, bd_ref[...]) * inv_hd + EPS)
            hat = o * r
            n = hat * g
            sg = _sigmoid(gate)
            return hat, r, n, sg, n * (gate * sg)

        sbo, mlao, sbg_v, mlag_v = sbo_ref[...], mlao_ref[...], sbg_ref[...], mlag_ref[...]
        gsb_v, gmla_v = gsb_ref[...], gmla_ref[...]
        sb_hat, sb_r, sb_n, sb_sg, sb_y = head_fwd(sbo, gsb_v, sbg_v)
        ml_hat, ml_r, ml_n, ml_sg, ml_y = head_fwd(mlao, gmla_v, mlag_v)
        mix = jnp.concatenate([sb_y, ml_y], axis=1).astype(BF16)
        y = _mm(mix, wout_ref[...])
        ry = lax.rsqrt(_rowmean(y * y) + EPS)
        y_hat = y * ry
        gpost_v = gpost_ref[...]
        x1 = x_ref[...] + y_hat * gpost_v
        pb = p_ref[...].astype(BF16)
        pl_ = jnp.concatenate([_mm(pb, wple_ref[k]) for k in range(N_SHARD)], axis=1)
        rp = lax.rsqrt(_rowmean(pl_ * pl_) + EPS)
        pl_hat = pl_ * rp
        gple_v = gple_ref[...]
        ple = pl_hat * gple_v
        x1b = x1.astype(BF16)
        gate = _sigmoid(_mm(x1b, wpg_ref[...]) + bpg_ref[...])
        err = x1 + ple * gate - t_ref[...]
        loss = 0.5 * jnp.sum(_rowmean(err * err))
        dout = err * (1.0 / D_MODEL)

        du = dout * ple * gate * (1.0 - gate)
        dub = du.astype(BF16)
        dple = dout * gate
        dx1 = dout + _mm_nt(dub, wpg_ref[...])
        dwpg_ref[...] += _mm_tn(x1b, dub)
        dplh = dple * gple_v
        dpl = rp * (dplh - pl_hat * _rowmean(dplh * pl_hat))
        dplb = dpl.astype(BF16)
        wsh = D_MODEL // N_SHARD
        for k in range(N_SHARD):
            dwple_ref[k] += _mm_tn(pb, dplb[:, k * wsh:(k + 1) * wsh])
        dxres_ref[...] = dx1
        dyh = dx1 * gpost_v
        dy = ry * (dyh - y_hat * _rowmean(dyh * y_hat))
        dyb = dy.astype(BF16)
        dwout_ref[...] += _mm_tn(mix, dyb)
        dmix = _mm_nt(dyb, wout_ref[...])

        def head_bwd(dyv, hat, r, n, sg, g, gate):
            dn = dyv * (gate * sg)
            dgate = dyv * n * (sg * (1.0 + gate * (1.0 - sg)))
            dhat = dn * g
            do = r * (dhat - hat * (_seg(dhat * hat, bd_ref[...]) * inv_hd))
            return do, dgate, _colsum(dn * hat)

        dsbo, dsbg, dg_sb = head_bwd(dmix[:, 0:512], sb_hat, sb_r, sb_n, sb_sg, gsb_v, sbg_v)
        dmlao, dmlag, dg_ml = head_bwd(dmix[:, 512:1024], ml_hat, ml_r, ml_n, ml_sg, gmla_v, mlag_v)
        dsbo_ref[...] = dsbo.astype(BF16)
        dmlao_ref[...] = dmlao.astype(BF16)
        delta_ref[...] = _seg(dmlao * mlao, bd_ref[...])
        dsbg_ref[...] = dsbg.astype(BF16)
        dmlag_ref[...] = dmlag.astype(BF16)
        vec_ref[pl.ds(0, 1), :] += _colsum(dx1 * y_hat)
        vec_ref[pl.ds(1, 1), :] += _colsum(dple * pl_hat)
        vec_ref[pl.ds(2, 1), :] += _colsum(du)
        vec_ref[pl.ds(3, 1), :] += jnp.concatenate([dg_sb, dg_ml], axis=1)
        vec_ref[pl.ds(4, 1), :] += jnp.full((1, D_MODEL), loss, F32)

    out_shape = (
        jax.ShapeDtypeStruct((s, 512), BF16), jax.ShapeDtypeStruct((s, 512), BF16), jax.ShapeDtypeStruct((s, 512), F32),
        jax.ShapeDtypeStruct((s, 512), BF16), jax.ShapeDtypeStruct((s, 512), BF16), jax.ShapeDtypeStruct((s, D_MODEL), F32),
        jax.ShapeDtypeStruct((D_MODEL, D_MODEL), F32), jax.ShapeDtypeStruct((D_MODEL, D_MODEL), F32),
        jax.ShapeDtypeStruct(wple.shape, F32), jax.ShapeDtypeStruct((8, D_MODEL), F32),
    )
    return pl.pallas_call(
        body, name="post_fwd_bwd", grid=(s // TM,), out_shape=out_shape,
        in_specs=[_rows(D_MODEL), _rows(PLE_DIM), _rows(D_MODEL), _rows(512), _rows(512), _rows(512), _rows(512),
                  _full((1, 512)), _full((1, 512)), _full((D_MODEL, D_MODEL)),
                  _full((1, D_MODEL)), _full(wple.shape), _full((1, D_MODEL)), _full((D_MODEL, D_MODEL)),
                  _full((1, D_MODEL)), _full((1024, 512))],
        out_specs=(_rows(512), _rows(512), _rows(512), _rows(512), _rows(512), _rows(D_MODEL),
                   _acc((D_MODEL, D_MODEL)), _acc((D_MODEL, D_MODEL)), _acc(wple.shape), _acc((8, D_MODEL))),
        compiler_params=pltpu.CompilerParams(vmem_limit_bytes=VMEM_DENSE),
    )(x, p, tgt, sbo, mlao, sbg, mlag, gsb, gmla, wout, gpost, wple, gple, wpg, bpg, _blockdiag2(512, HEAD_DIM))


def _pre_bwd(x, dxres, dsbq, dsbk, dsbv, dsbg, dmlag, dqc, dkc, dmv, cq, ckv, tabs, gpre, win, gq, wuq, gkv, wk, wv):
    s = x.shape[0]
    c_t, sa_t, sb_t = tabs
    rw = _rows

    def body(x_ref, dxres_ref, dsbq_ref, dsbk_ref, dsbv_ref, dsbg_ref, dmlag_ref, dqc_ref, dkc_ref, dmv_ref, cq_ref,
             ckv_ref, c_ref, sa_ref, sb_ref, gpre_ref, win_ref, gq_ref, wuq_ref, gkv_ref, wk_ref, wv_ref,
             gx_ref, dwin_ref, dwuq_ref, dwk_ref, dwv_ref, vec_ref, dwin_acc):
        i = pl.program_id(0)

        @pl.when(i == 0)
        def _():
            dwin_acc[...] = jnp.zeros_like(dwin_acc)
            dwuq_ref[...] = jnp.zeros_like(dwuq_ref)
            dwk_ref[...] = jnp.zeros_like(dwk_ref)
            dwv_ref[...] = jnp.zeros_like(dwv_ref)
            vec_ref[...] = jnp.zeros_like(vec_ref)

        lane = lax.broadcasted_iota(jnp.int32, (1, LANES), 1)
        c1, sa1, sb1 = c_ref[...], sa_ref[...], sb_ref[...]
        c8, sa8, sb8 = jnp.tile(c1, (1, 8)), jnp.tile(sa1, (1, 8)), jnp.tile(sb1, (1, 8))

        def norm_bwd(dn, hat, r, g):
            t = dn * g
            return r * (t - hat * _rowmean(t * hat)), _colsum(dn * hat)

        xv = x_ref[...]
        r1 = lax.rsqrt(_rowmean(xv * xv) + EPS)
        x_hat = xv * r1
        gpre_v = gpre_ref[...]
        hb = (x_hat * gpre_v).astype(BF16)
        ready = jnp.concatenate([dsbq_ref[...], dsbk_ref[...].astype(BF16), dsbv_ref[...].astype(BF16), dsbg_ref[...]], axis=1)
        dmlag = dmlag_ref[...]
        dwin_acc[:, 0:2048] += _mm_tn(hb, ready)
        dwin_acc[:, 2560:3072] += _mm_tn(hb, dmlag)
        dh = _mm_nt(ready, win_ref[:, 0:2048]) + _mm_nt(dmlag, win_ref[:, 2560:3072])

        dqeb = _rope_bwd(dqc_ref[...], c8, sa8, sb8).astype(BF16)
        cq = cq_ref[...]
        rq = lax.rsqrt(_rowmean(cq * cq) + EPS)
        cq_hat = cq * rq
        gq_v = gq_ref[...]
        dwuq_ref[...] += _mm_tn((cq_hat * gq_v).astype(BF16), dqeb)
        dcq, dg_q = norm_bwd(_mm_nt(dqeb, wuq_ref[...]), cq_hat, rq, gq_v)

        dkc = dkc_ref[...]
        dkcb = dkc.astype(BF16)
        dmvb = dmv_ref[...].astype(BF16)
        ckv = ckv_ref[...]
        rkv = lax.rsqrt(_rowmean(ckv * ckv) + EPS)
        ckv_hat = ckv * rkv
        gkv_v = gkv_ref[...]
        ckvnb = (ckv_hat * gkv_v).astype(BF16)
        dwk_ref[...] += _mm_tn(ckvnb, dkcb)
        dwv_ref[...] += _mm_tn(ckvnb, dmvb)
        dckv, dg_kv = norm_bwd(_mm_nt(dkcb, wk_ref[...]) + _mm_nt(dmvb, wv_ref[...]), ckv_hat, rkv, gkv_v)

        dkr = dkc[:, 0:LANES]
        for hh in range(1, 8):
            dkr = dkr + dkc[:, LANES * hh:LANES * (hh + 1)]
        dkr = _rope_bwd(dkr, c1, sa1, sb1)
        dkr = jnp.where((lane >= 64) & (lane < 96), dkr, 0.0)

        late = jnp.concatenate([dcq.astype(BF16), dckv.astype(BF16), dkr.astype(BF16)], axis=1)
        dwin_acc[:, 2048:2560] += _mm_tn(hb, late)
        dx, dg_pre = norm_bwd(dh + _mm_nt(late, win_ref[:, 2048:2560]), x_hat, r1, gpre_v)
        gx_ref[...] = dxres_ref[...] + dx
        vec_ref[pl.ds(0, 1), :] += dg_pre
        vec_ref[pl.ds(1, 1), :] += jnp.concatenate([dg_q, dg_kv, jnp.zeros((1, D_MODEL - Q_LORA - KV_LORA), F32)], axis=1)

        @pl.when(i == pl.num_programs(0) - 1)
        def _():
            pltpu.sync_copy(dwin_acc, dwin_ref)

    out_shape = (
        jax.ShapeDtypeStruct((s, D_MODEL), F32), jax.ShapeDtypeStruct((D_MODEL, D_EXT), F32),
        jax.ShapeDtypeStruct((Q_LORA, 1024), F32), jax.ShapeDtypeStruct((KV_LORA, 1024), F32),
        jax.ShapeDtypeStruct((KV_LORA, 512), F32), jax.ShapeDtypeStruct((8, D_MODEL), F32),
    )
    return pl.pallas_call(
        body, name="pre_bwd", grid=(s // TM,), out_shape=out_shape,
        in_specs=[rw(D_MODEL), rw(D_MODEL), rw(512), rw(512), rw(512), rw(512), rw(512),
                  rw(1024), rw(1024), rw(512), rw(Q_LORA), rw(KV_LORA), rw(LANES), rw(LANES),
                  rw(LANES), _full((1, D_MODEL)), _full((D_MODEL, D_EXT)), _full((1, Q_LORA)), _full((Q_LORA, 1024)),
                  _full((1, KV_LORA)), _full((KV_LORA, 1024)), _full((KV_LORA, 512))],
        out_specs=(rw(D_MODEL), pl.BlockSpec(memory_space=pl.ANY), _acc((Q_LORA, 1024)), _acc((KV_LORA, 1024)),
                   _acc((KV_LORA, 512)), _acc((8, D_MODEL))),
        scratch_shapes=[pltpu.VMEM((D_MODEL, D_EXT), F32)],
        compiler_params=pltpu.CompilerParams(vmem_limit_bytes=VMEM_DENSE),
    )(x, dxres, dsbq, dsbk, dsbv, dsbg, dmlag, dqc, dkc, dmv, cq, ckv, c_t, sa_t, sb_t, gpre, win, gq, wuq, gkv, wk, wv)


def _place():
    return lax.axis_index("x"), lax.axis_index("y"), lax.axis_index("c")


def _gather_steps(shapes, ins, bufs, send_sems, recv_sems):
    n = len(shapes)
    x, y, c = _place()
    me, sib = (x, y, c), (x, y, 1 - c)
    chips = [(1 - x, y), (x, 1 - y), (1 - x, 1 - y)]

    def half(t, chip, hc):
        rows = shapes[t][0] // 2
        return bufs[t].at[2 * chip[0] + chip[1], pl.ds(pl.multiple_of(hc * rows, 16), rows), :]

    def copy(k, t, chip, hc, to):
        return pltpu.make_async_remote_copy(src_ref=half(t, chip, hc), dst_ref=half(t, chip, hc), send_sem=send_sems.at[k],
                                            recv_sem=recv_sems.at[k], device_id=to, device_id_type=MESH)

    def start():
        for t in range(n):
            bufs[t][2 * x + y] = ins[t][...].astype(BF16)
            for j, chip in enumerate(chips):
                copy(6 * t + j, t, (x, y), c, (*chip, c)).start()

    def forward():
        for t in range(n):
            for j, chip in enumerate(chips):
                copy(6 * t + j, t, chip, c, me).wait_recv()
                copy(6 * t + 3 + j, t, chip, c, sib).start()

    def finish():
        for t in range(n):
            for j, chip in enumerate(chips):
                copy(6 * t + 3 + j, t, chip, 1 - c, me).wait_recv()
        for t in range(n):
            for j, chip in enumerate(chips):
                copy(6 * t + j, t, (x, y), c, (*chip, c)).wait_send()
                copy(6 * t + 3 + j, t, chip, c, sib).wait_send()

    return start, forward, finish


def _allgather_weights(shards):
    n = len(shards)

    def body(*refs):
        start, forward, finish = _gather_steps([a.shape for a in shards], refs[:n], refs[n:2 * n], refs[2 * n], refs[2 * n + 1])
        start()
        forward()
        finish()

    return pl.pallas_call(
        body, name="allgather_weights",
        out_shape=tuple(jax.ShapeDtypeStruct((N_SHARD,) + a.shape, BF16) for a in shards),
        in_specs=[pl.BlockSpec(memory_space=pltpu.VMEM)] * n, out_specs=(pl.BlockSpec(memory_space=pltpu.VMEM),) * n,
        scratch_shapes=[pltpu.SemaphoreType.DMA((6 * n,)), pltpu.SemaphoreType.DMA((6 * n,))],
        compiler_params=pltpu.CompilerParams(vmem_limit_bytes=VMEM_ATTN),
    )(*shards)


def _reduce_scratch(gsh):
    n = len(gsh)
    half_shapes = [(N_SHARD, a.shape[1] // 2, a.shape[2]) for a in gsh]
    return ([pltpu.VMEM(s_, F32) for s_ in half_shapes] * 2 + [pltpu.VMEM(s_, BF16) for s_ in half_shapes] * 2
            + [pltpu.SemaphoreType.DMA((n,)), pltpu.SemaphoreType.DMA((5 * n,)), pltpu.SemaphoreType.DMA((5 * n,))])


def _reduce_steps(halves, g_refs, f_refs, scratch):
    n = len(halves)
    accs, sibs, sbufs, rbufs = scratch[0:n], scratch[n:2 * n], scratch[2 * n:3 * n], scratch[3 * n:4 * n]
    local_sems, send_sems, recv_sems = scratch[4 * n:4 * n + 3]
    x, y, c = _place()
    me, sib = (x, y, c), (x, y, 1 - c)
    mine = 2 * x + y
    chips = [(1 - x, y), (x, 1 - y), (1 - x, 1 - y)]

    def remote(k, src, dst, to):
        return pltpu.make_async_remote_copy(src_ref=src, dst_ref=dst, send_sem=send_sems.at[k], recv_sem=recv_sems.at[k],
                                            device_id=to, device_id_type=MESH)

    def half3(ref, t, hc):
        return ref.at[:, pl.ds(pl.multiple_of(hc * halves[t], 8), halves[t]), :]

    def half2(ref, t, hc):
        return ref.at[pl.ds(pl.multiple_of(hc * halves[t], 8), halves[t]), :]

    def mine_load(t):
        return pltpu.make_async_copy(half3(g_refs[t], t, c), accs[t], local_sems.at[t])

    def to_sibling(t, to):
        return remote(t, half3(g_refs[t], t, 1 - c), sibs[t], to)

    def to_chip(t, j, chip, to):
        idx = 2 * chip[0] + chip[1]
        return remote(n + 3 * t + j, sbufs[t].at[idx], rbufs[t].at[mine if to is not me else idx], to)

    def swap(t, hc, to):
        return remote(4 * n + t, half2(f_refs[t], t, hc), half2(f_refs[t], t, hc), to)

    def load():
        for t in range(n):
            mine_load(t).start()
            to_sibling(t, sib).start()

    def partial():
        for t in range(n):
            mine_load(t).wait()
            to_sibling(t, me).wait_recv()
            for k in range(N_SHARD):
                accs[t][k] = accs[t][k] + sibs[t][k]
            for j, chip in enumerate(chips):
                idx = 2 * chip[0] + chip[1]
                sbufs[t][idx] = accs[t][idx].astype(BF16)
                to_chip(t, j, chip, (*chip, c)).start()

    def total():
        for t in range(n):
            acc = accs[t][mine]
            for j, chip in enumerate(chips):
                to_chip(t, j, chip, me).wait_recv()
                acc = acc + rbufs[t][2 * chip[0] + chip[1]].astype(F32)
            half2(f_refs[t], t, c)[...] = acc
            swap(t, c, sib).start()

    def finish():
        for t in range(n):
            swap(t, 1 - c, me).wait_recv()
        for t in range(n):
            to_sibling(t, sib).wait_send()
            for j, chip in enumerate(chips):
                to_chip(t, j, chip, (*chip, c)).wait_send()
            swap(t, c, sib).wait_send()

    return load, partial, total, finish


def _reduce_scatter_grads(gsh, vec):
    n = len(gsh)
    halves = [a.shape[1] // 2 for a in gsh]

    def body(*refs):
        g_refs, vec_ref, f_refs, vsum_ref = refs[:n], refs[n], refs[n + 1:2 * n + 1], refs[2 * n + 1]
        scratch = refs[2 * n + 2:]
        vrecv, vsend_sems, vrecv_sems = scratch[4 * n + 3:]
        load, partial, total, finish = _reduce_steps(halves, g_refs, f_refs, scratch)
        x, y, c = _place()
        my_dev = 4 * x + 2 * y + c

        def flip(k):
            return x ^ ((k >> 2) & 1), y ^ ((k >> 1) & 1), c ^ (k & 1)

        def vcopy(k, slot, to):
            return pltpu.make_async_remote_copy(src_ref=vec_ref, dst_ref=vrecv.at[slot], send_sem=vsend_sems.at[k - 1],
                                                recv_sem=vrecv_sems.at[k - 1], device_id=to, device_id_type=MESH)

        load()
        vrecv[my_dev] = vec_ref[...]
        for k in range(1, 8):
            vcopy(k, my_dev, flip(k)).start()
        partial()
        total()
        finish()
        for k in range(1, 8):
            fx, fy, fc = flip(k)
            vcopy(k, 4 * fx + 2 * fy + fc, (x, y, c)).wait_recv()
        vs = vrecv[0]
        for d in range(1, 8):
            vs = vs + vrecv[d]
        vsum_ref[...] = vs
        for k in range(1, 8):
            vcopy(k, my_dev, flip(k)).wait_send()

    return pl.pallas_call(
        body, name="reduce_scatter_grads",
        out_shape=tuple(jax.ShapeDtypeStruct(a.shape[1:], F32) for a in gsh) + (jax.ShapeDtypeStruct((VEC_ROWS, 1024), F32),),
        in_specs=[pl.BlockSpec(memory_space=pl.ANY)] * n + [pl.BlockSpec(memory_space=pltpu.VMEM)],
        out_specs=(pl.BlockSpec(memory_space=pltpu.VMEM),) * (n + 1),
        scratch_shapes=_reduce_scratch(gsh) + [pltpu.VMEM((8, VEC_ROWS, 1024), F32), pltpu.SemaphoreType.DMA((7,)),
                                               pltpu.SemaphoreType.DMA((7,))],
        compiler_params=pltpu.CompilerParams(vmem_limit_bytes=56 * 1024 * 1024),
    )(*gsh, vec)


def _adamw(w, g, m, v):
    rows, cols = w.shape
    tr = rows if rows <= 256 else 256
    flip = cols % LANES != 0

    def body(w_ref, g_ref, m_ref, v_ref, g_out, d_ref, nm_ref, nv_ref):
        gv = g_ref[...].T if flip else g_ref[...]
        outs = (gv,) + _adam_math(w_ref[...], gv, m_ref[...], v_ref[...])
        for ref, val in zip((g_out, d_ref, nm_ref, nv_ref), outs):
            ref[...] = val

    spec = pl.BlockSpec((tr, cols), lambda i: (i, 0))
    tspec = pl.BlockSpec((cols, tr), lambda i: (0, i)) if flip else spec
    shp = jax.ShapeDtypeStruct((cols, rows) if flip else (rows, cols), F32)
    if flip:
        w, m, v = w.T, m.T, v.T
    outs = pl.pallas_call(body, name="adamw", grid=(rows // tr,), out_shape=(shp,) * 4,
                          in_specs=[tspec, spec, tspec, tspec], out_specs=(tspec,) * 4)(w, g, m, v)
    return tuple(o.T for o in outs) if flip else outs


def _adam_math(w, g, m, v):
    m2 = ADAM_B1 * m + (1.0 - ADAM_B1) * g
    v2 = ADAM_B2 * v + (1.0 - ADAM_B2) * (g * g)
    m_hat = m2 / (1.0 - ADAM_B1 ** ADAM_STEP)
    v_hat = v2 / (1.0 - ADAM_B2 ** ADAM_STEP)
    return -ADAM_LR * (m_hat / (jnp.sqrt(v_hat) + ADAM_EPS) + ADAM_WD * w), m2, v2


def _adamw_small(vsum, w, m, v):
    names = [name for name, _, _, _ in _VEC_LAYOUT]
    k = len(names)

    def body(*refs):
        vs_ref, w_refs, m_refs, v_refs = refs[0], refs[1:1 + k], refs[1 + k:1 + 2 * k], refs[1 + 2 * k:1 + 3 * k]
        outs = refs[1 + 3 * k:]
        for idx, (_, r, c0, width) in enumerate(_VEC_LAYOUT):
            gv = vs_ref[pl.ds(r, 1), pl.ds(c0, width)]
            d, m2, v2 = _adam_math(w_refs[idx][...], gv, m_refs[idx][...], v_refs[idx][...])
            outs[idx][...], outs[k + idx][...], outs[2 * k + idx][...], outs[3 * k + idx][...] = gv, d, m2, v2

    shapes = tuple(jax.ShapeDtypeStruct(w[name].shape, F32) for name in names)
    res = pl.pallas_call(
        body, name="adamw_small", out_shape=shapes * 4,
        in_specs=[pl.BlockSpec(memory_space=pltpu.VMEM)] * (1 + 3 * k), out_specs=(pl.BlockSpec(memory_space=pltpu.VMEM),) * (4 * k),
    )(vsum, *[w[name] for name in names], *[m[name] for name in names], *[v[name] for name in names])
    return tuple({name: res[part * k + idx] for idx, name in enumerate(names)} for part in range(4))


_EARLY = ("w_in", "w_uq", "w_ukv")
_LATE = ("w_out", "w_ple", "w_ple_gate")
_BIG = _EARLY + _LATE
_KR_LOCAL = 2432 - 3 * (D_IN // N_SHARD)


def _extend_early(parts):
    cols = lambda a: a.transpose(1, 0, 2).reshape(a.shape[1], N_SHARD * a.shape[2])
    g = parts["w_in"]
    zeros = lambda n: jnp.zeros((D_MODEL, n), g.dtype)
    win_ext = jnp.concatenate([g[0], g[1], g[2], g[3][:, :_KR_LOCAL], zeros(64), g[3][:, _KR_LOCAL:_KR_LOCAL + QK_ROPE],
                               zeros(32), g[3][:, _KR_LOCAL + QK_ROPE:]], axis=1)
    wuq_ext = jnp.pad(cols(parts["w_uq"]).reshape(Q_LORA, 8, 96), ((0, 0), (0, 0), (0, 32))).reshape(Q_LORA, 1024)
    wukv = cols(parts["w_ukv"]).reshape(KV_LORA, 8, 128)
    wk_ext = jnp.pad(wukv[:, :, :64], ((0, 0), (0, 0), (0, 64))).reshape(KV_LORA, 1024)
    wv = wukv[:, :, 64:].reshape(KV_LORA, 512)
    return win_ext, wuq_ext, wk_ext, wv


def _shard_cols(a):
    return a.reshape(a.shape[0], N_SHARD, a.shape[1] // N_SHARD).transpose(1, 0, 2)


def _shard_rows(a):
    return a.reshape(N_SHARD, a.shape[0] // N_SHARD, a.shape[1])


def _shard_early_grads(dwin_ext, dwuq_ext, dwk_ext, dwv):
    e, w = dwin_ext, D_IN // N_SHARD
    last = jnp.concatenate([e[:, 3 * w:2432], e[:, 2496:2528], e[:, 2560:]], axis=1)
    dwuq = dwuq_ext.reshape(Q_LORA, 8, 128)[:, :, :96].reshape(Q_LORA, 768)
    dwukv = jnp.concatenate([dwk_ext.reshape(KV_LORA, 8, 128)[:, :, :64], dwv.reshape(KV_LORA, 8, 64)], axis=2)
    return [jnp.stack([e[:, 0:w], e[:, w:2 * w], e[:, 2 * w:3 * w], last]), _shard_cols(dwuq),
            _shard_cols(dwukv.reshape(KV_LORA, 1024))]


def _rope_tables(positions):
    half = QK_ROPE // 2
    freq = ROPE_THETA ** (-jnp.arange(half, dtype=F32) / half)
    s = positions.shape[0]
    per = LANES // half
    ang = jnp.repeat(positions.astype(F32).reshape(s // per, per), half, axis=1) * jnp.tile(freq, per)
    cos, sin = lax.optimization_barrier((jnp.cos(ang), jnp.sin(ang)))
    cos, sin = cos.reshape(s, half), sin.reshape(s, half)
    z = lambda n: jnp.zeros((s, n), F32)
    c_t = jnp.concatenate([jnp.ones((s, 64), F32), cos, cos, z(32)], axis=1)
    sa_t = jnp.concatenate([z(64), -sin, z(16), z(32)], axis=1)
    sb_t = jnp.concatenate([z(64), z(16), sin, z(32)], axis=1)
    return c_t, sa_t, sb_t


def _local_grads(x, p, positions, tgt, gains, early, late):
    win_ext, wuq_ext, wk_ext, wv = _extend_early(early)
    tabs = _rope_tables(positions)
    g = gains
    sbq, sbk, sbv, sbg, mlag, cq, ckv, qc, kc, mv, sbkt, sbvt, kct, mvt = _pre_fwd(
        x, tabs, g["norm_pre_g"], win_ext, g["q_norm_g"], wuq_ext, g["kv_norm_g"], wk_ext, wv)
    sbo, wout4, wple4, wpg4 = _sb_fwd(sbq, sbk, sbvt, late)
    wout, wpg = wout4.reshape(D_MODEL, D_MODEL), wpg4.reshape(D_MODEL, D_MODEL)
    mlao, lse = _mla_fwd(qc, kc, mvt)
    dsbo, dmlao, delta, dsbg, dmlag, dxres, dwout, dwpg, dwple, vec_c = _post(
        x, p, tgt, sbo, mlao, sbg, mlag, g["sb_out_norm_g"], g["mla_out_norm_g"], wout, g["norm_post_g"], wple4,
        g["ple_norm_g"], wpg, g["b_ple_gate"])
    dsbq, dsbk, dsbv, *late_grads = _sb_bwd(sbq, sbk, sbkt, sbv, dsbo, [_shard_rows(dwout), dwple, _shard_rows(dwpg)])
    dqc, dkc, dmv = _mla_bwd(qc, kc, kct, mv, dmlao, lse, delta)
    gx, dwin_ext, dwuq_ext, dwk_ext, dwv, vec_d = _pre_bwd(
        x, dxres, dsbq, dsbk, dsbv, dsbg, dmlag, dqc, dkc, dmv, cq, ckv, tabs, g["norm_pre_g"], win_ext, g["q_norm_g"],
        wuq_ext, g["kv_norm_g"], wk_ext, wv)
    return gx, _shard_early_grads(dwin_ext, dwuq_ext, dwk_ext, dwv), late_grads, jnp.concatenate([vec_c, vec_d], axis=0)


_VEC_LAYOUT = (("norm_post_g", 0, 0, 1024), ("ple_norm_g", 1, 0, 1024), ("b_ple_gate", 2, 0, 1024), ("sb_out_norm_g", 3, 0, 512),
               ("mla_out_norm_g", 3, 512, 512), ("norm_pre_g", 8, 0, 1024), ("q_norm_g", 9, 0, 256), ("kv_norm_g", 9, 256, 128))
_LOSS_ROW = 4
_WEIGHT_ORDER = ("norm_pre_g", "w_in", "q_norm_g", "w_uq", "kv_norm_g", "w_ukv", "sb_out_norm_g", "mla_out_norm_g", "w_out",
                 "norm_post_g", "w_ple", "ple_norm_g", "w_ple_gate", "b_ple_gate")


def kernel(x, p, positions, norm_pre_g, w_in, q_norm_g, w_uq, kv_norm_g, w_ukv, sb_out_norm_g, mla_out_norm_g, w_out, norm_post_g, w_ple, ple_norm_g, w_ple_gate, b_ple_gate, loss_target, m_norm_pre_g, m_w_in, m_q_norm_g, m_w_uq, m_kv_norm_g, m_w_ukv, m_sb_out_norm_g, m_mla_out_norm_g, m_w_out, m_norm_post_g, m_w_ple, m_ple_norm_g, m_w_ple_gate, m_b_ple_gate, v_norm_pre_g, v_w_in, v_q_norm_g, v_w_uq, v_kv_norm_g, v_w_ukv, v_sb_out_norm_g, v_mla_out_norm_g, v_w_out, v_norm_post_g, v_w_ple, v_ple_norm_g, v_w_ple_gate, v_b_ple_gate):
    w = {"norm_pre_g": norm_pre_g, "w_in": w_in[0], "q_norm_g": q_norm_g, "w_uq": w_uq[0], "kv_norm_g": kv_norm_g, "w_ukv": w_ukv[0],
         "sb_out_norm_g": sb_out_norm_g, "mla_out_norm_g": mla_out_norm_g, "w_out": w_out[0], "norm_post_g": norm_post_g,
         "w_ple": w_ple[0], "ple_norm_g": ple_norm_g, "w_ple_gate": w_ple_gate[0], "b_ple_gate": b_ple_gate}
    m = {"norm_pre_g": m_norm_pre_g, "w_in": m_w_in[0], "q_norm_g": m_q_norm_g, "w_uq": m_w_uq[0], "kv_norm_g": m_kv_norm_g,
         "w_ukv": m_w_ukv[0], "sb_out_norm_g": m_sb_out_norm_g, "mla_out_norm_g": m_mla_out_norm_g, "w_out": m_w_out[0],
         "norm_post_g": m_norm_post_g, "w_ple": m_w_ple[0], "ple_norm_g": m_ple_norm_g, "w_ple_gate": m_w_ple_gate[0],
         "b_ple_gate": m_b_ple_gate}
    v = {"norm_pre_g": v_norm_pre_g, "w_in": v_w_in[0], "q_norm_g": v_q_norm_g, "w_uq": v_w_uq[0], "kv_norm_g": v_kv_norm_g,
         "w_ukv": v_w_ukv[0], "sb_out_norm_g": v_sb_out_norm_g, "mla_out_norm_g": v_mla_out_norm_g, "w_out": v_w_out[0],
         "norm_post_g": v_norm_post_g, "w_ple": v_w_ple[0], "ple_norm_g": v_ple_norm_g, "w_ple_gate": v_w_ple_gate[0],
         "b_ple_gate": v_b_ple_gate}
    gathered = _allgather_weights([w[n] for n in _EARLY])
    gx, early_grads, late_red, vec = _local_grads(x[0], p[0, 0], positions[0], loss_target[0], w, dict(zip(_EARLY, gathered)),
                                                  [w[n] for n in _LATE])
    *early_red, vsum = _reduce_scatter_grads(early_grads, vec)
    gred = early_red + late_red
    loss = vsum[_LOSS_ROW, 0]

    g, delta, new_m, new_v = _adamw_small(vsum, w, m, v)
    for n, gn in zip(_BIG, gred):
        g[n], delta[n], new_m[n], new_v[n] = _adamw(w[n], gn, m[n], v[n])

    lead = lambda n, a: a[None] if n in _BIG else a
    return (loss, gx[None],
            *[lead(n, g[n]) for n in _WEIGHT_ORDER], *[lead(n, delta[n]) for n in _WEIGHT_ORDER],
            *[lead(n, new_m[n]) for n in _WEIGHT_ORDER], *[lead(n, new_v[n]) for n in _WEIGHT_ORDER])
```

```python
import numpy as np
import jax
import jax.numpy as jnp
from jax import lax
from jax.experimental import pallas as pl
from jax.experimental.pallas import tpu as pltpu

F32 = jnp.float32
BF16 = jnp.bfloat16
MESH = pl.DeviceIdType.MESH

D_MODEL = 1024
HEAD_DIM = 64
D_SB = 512
D_MLA = 512
Q_LORA = 256
KV_LORA = 128
QK_NOPE = 64
QK_ROPE = 32
PLE_DIM = 256
D_IN = 2976
D_EXT = 3072
ROPE_THETA = 10000.0
EPS = 1e-6
N_SHARD = 4

ADAM_LR = 0.001
ADAM_B1 = 0.9
ADAM_B2 = 0.999
ADAM_EPS = 1e-08
ADAM_WD = 0.01
ADAM_STEP = 10

LANES = 128
BK = 128
WQ = 256
SB_NSUB = 8
MQ_FWD = 4096
MQ_BWD = 1024
MLA_CW = 256
SB_CUTOFF = 120.0
TM = 256
TM_PRE = 256
VEC_ROWS = 16
VMEM_DENSE = 52 * 1024 * 1024
VMEM_ATTN = 40 * 1024 * 1024


def _mm(a, b):
    return jnp.dot(a, b, preferred_element_type=F32)


def _mm_nt(a, b):
    return lax.dot_general(a, b, (((1,), (1,)), ((), ())), preferred_element_type=F32)


def _mm_tn(a, b):
    return lax.dot_general(a, b, (((0,), (0,)), ((), ())), preferred_element_type=F32)


def _seg(a, bd2):
    return _mm(_split2(a), bd2)


def _const(mask):
    return jnp.asarray(np.asarray(mask, np.float32), dtype=BF16)


def _blockdiag2(n, seg):
    r = (np.arange(2 * n)[:, None] % n) // seg
    c = np.arange(n)[None, :] // seg
    return _const(r == c)


def _sigmoid(a):
    return 1.0 / (1.0 + jnp.exp(-a))


def _rowmean(a):
    return jnp.mean(a, axis=-1, keepdims=True)


def _colsum(a):
    return jnp.sum(a, axis=0, keepdims=True)


def _rope_fwd(a, c, sa, sb):
    w = a.shape[-1]
    return a * c + pltpu.roll(a, w - 16, 1) * sa + pltpu.roll(a, 16, 1) * sb


def _rope_bwd(g, c, sa, sb):
    w = g.shape[-1]
    return g * c + pltpu.roll(g * sa, 16, 1) + pltpu.roll(g * sb, w - 16, 1)


def _full(shape):
    return pl.BlockSpec(shape, lambda *_: (0,) * len(shape))


def _acc(shape):
    return pl.BlockSpec(shape, lambda *_: (0,) * len(shape))


def _full2(shape):
    return pl.BlockSpec(shape, lambda p, i: (0, 0))


def _cols(height, tm=TM):
    return pl.BlockSpec((height, tm), lambda i: (0, i))


def _rows(width, tm=TM):
    return pl.BlockSpec((tm, width), lambda i: (i, 0))


def _pre_fwd(x, tabs, gpre, win, gq, wuq, gkv, wk, wv):
    s = x.shape[0]
    c_t, sa_t, sb_t = tabs
    rw, cl = (lambda width: _rows(width, TM_PRE)), (lambda height: _cols(height, TM_PRE))

    def body(x_ref, c_ref, sa_ref, sb_ref, gpre_ref, win_ref, gq_ref, wuq_ref, gkv_ref, wk_ref, wv_ref,
             sbq_ref, sbk_ref, sbv_ref, sbg_ref, mlag_ref, cq_ref, ckv_ref, qc_ref, kc_ref, mv_ref,
             sbkt_ref, sbvt_ref, kct_ref, mvt_ref):
        xv = x_ref[...]
        r1 = lax.rsqrt(_rowmean(xv * xv) + EPS)
        h = (xv * r1 * gpre_ref[...]).astype(BF16)
        proj = _mm(h, win_ref[...])
        sbq_ref[...] = proj[:, 0:512].astype(BF16)
        sbk_ref[...] = proj[:, 512:1024].astype(BF16)
        sbv_ref[...] = proj[:, 1024:1536].astype(BF16)
        sbkt_ref[...] = proj[:, 512:1024].T.astype(BF16)
        sbvt_ref[...] = proj[:, 1024:1536].T.astype(BF16)
        sbg_ref[...] = proj[:, 1536:2048]
        cq = proj[:, 2048:2304]
        ckv = proj[:, 2304:2432]
        kr = proj[:, 2432:2560]
        mlag_ref[...] = proj[:, 2560:3072]
        cq_ref[...] = cq
        ckv_ref[...] = ckv
        c1, sa1, sb1 = c_ref[...], sa_ref[...], sb_ref[...]
        c8, sa8, sb8 = jnp.tile(c1, (1, 8)), jnp.tile(sa1, (1, 8)), jnp.tile(sb1, (1, 8))
        cqn = (cq * lax.rsqrt(_rowmean(cq * cq) + EPS) * gq_ref[...]).astype(BF16)
        qe = _mm(cqn, wuq_ref[...])
        qc_ref[...] = _rope_fwd(qe, c8, sa8, sb8).astype(BF16)
        ckvn = (ckv * lax.rsqrt(_rowmean(ckv * ckv) + EPS) * gkv_ref[...]).astype(BF16)
        ke = _mm(ckvn, wk_ref[...])
        krr = _rope_fwd(kr, c1, sa1, sb1)
        kcat = ke + jnp.tile(krr, (1, 8))
        kc_ref[...] = kcat.astype(BF16)
        kct_ref[...] = kcat.T.astype(BF16)
        mval = _mm(ckvn, wv_ref[...])
        mv_ref[...] = mval.astype(BF16)
        mvt_ref[...] = mval.T.astype(BF16)

    out_shape = (
        jax.ShapeDtypeStruct((s, 512), BF16), jax.ShapeDtypeStruct((s, 512), BF16), jax.ShapeDtypeStruct((s, 512), BF16),
        jax.ShapeDtypeStruct((s, 512), F32), jax.ShapeDtypeStruct((s, 512), F32),
        jax.ShapeDtypeStruct((s, Q_LORA), F32), jax.ShapeDtypeStruct((s, KV_LORA), F32),
        jax.ShapeDtypeStruct((s, 1024), BF16), jax.ShapeDtypeStruct((s, 1024), BF16), jax.ShapeDtypeStruct((s, 512), BF16),
        jax.ShapeDtypeStruct((512, s), BF16), jax.ShapeDtypeStruct((512, s), BF16), jax.ShapeDtypeStruct((1024, s), BF16),
        jax.ShapeDtypeStruct((512, s), BF16),
    )
    return pl.pallas_call(
        body, name="pre_fwd", grid=(s // TM_PRE,), out_shape=out_shape,
        in_specs=[rw(D_MODEL), rw(LANES), rw(LANES), rw(LANES), _full((1, D_MODEL)), _full((D_MODEL, D_EXT)),
                  _full((1, Q_LORA)), _full((Q_LORA, 1024)), _full((1, KV_LORA)), _full((KV_LORA, 1024)), _full((KV_LORA, 512))],
        out_specs=(rw(512), rw(512), rw(512), rw(512), rw(512), rw(Q_LORA), rw(KV_LORA),
                   rw(1024), rw(1024), rw(512), cl(512), cl(512), cl(1024), cl(512)),
        compiler_params=pltpu.CompilerParams(vmem_limit_bytes=VMEM_DENSE),
    )(x, c_t, sa_t, sb_t, gpre, win, gq, wuq, gkv, wk, wv)


def _softplus(z):
    neg_abs = lax.bitcast_convert_type(lax.bitcast_convert_type(z, jnp.uint32) | jnp.uint32(0x80000000), F32)
    return jnp.maximum(z, 0.0) + jnp.log(1.0 + jnp.exp(neg_abs))


def _sum_matrix(kind, terms):
    r, c = np.arange(2 * BK)[:, None], np.arange(2 * BK * terms)[None, :] % (2 * BK)
    rk, ck = r % BK, c % BK
    return _const(((r // BK) == (c // BK)) & {"suffix": ck >= rk, "prefix": ck <= rk}[kind])


def _split_rows(a):
    hi = a.astype(BF16)
    return jnp.concatenate([hi, (a - hi.astype(F32)).astype(BF16)], axis=0)


def _heads_t(blk, rowi):
    zero = jnp.zeros_like(blk)
    return jnp.concatenate([jnp.where(rowi < 64, blk, zero), jnp.where(rowi >= 64, blk, zero)], axis=1)


def _mask_keys(a, valid, fill=0.0):
    return jnp.concatenate([jnp.where(valid, a[0:BK], fill), jnp.where(valid, a[BK:2 * BK], fill)], axis=0)


def _split2(a):
    hi = a.astype(BF16)
    lo = (a - hi.astype(F32)).astype(BF16)
    return jnp.concatenate([hi, lo], axis=1)


def _pair_stack(b, lane):
    zero = jnp.zeros_like(b)
    return jnp.concatenate([jnp.where(lane < 64, b, zero), jnp.where(lane >= 64, b, zero)], axis=0)


def _sb_fwd(q, k, vt, late):
    s = q.shape[0]
    n = len(late)

    def body(q_ref, k_ref, vt_ref, usuf_ref, *rest):
        ins, o_ref, outs = rest[:n], rest[n], rest[n + 1:2 * n + 1]
        acc_scr, run_scr = rest[2 * n + 1:2 * n + 3]
        bufs, (send_sems, recv_sems, out_sems) = rest[2 * n + 3:3 * n + 3], rest[3 * n + 3:]
        p, i = pl.program_id(0), pl.program_id(1)
        gather_start, gather_forward, gather_finish = _gather_steps([a.shape for a in late], ins, bufs, send_sems, recv_sems)

        @pl.when((p == 0) & (i == 0))
        def _():
            gather_start()

        @pl.when((p == 2) & (i == 0))
        def _():
            gather_forward()

        lane = lax.broadcasted_iota(jnp.int32, (1, LANES), 1)
        rowi = lax.broadcasted_iota(jnp.int32, (LANES, 1), 0)
        keyi = lax.broadcasted_iota(jnp.int32, (BK, WQ), 0)

        def group(i, qs, blocks, masked, seen=None):
            seen = seen or [0] * len(blocks)
            qryi = lax.broadcasted_iota(jnp.int32, (BK, WQ), 1) + i * WQ
            starts = [pl.multiple_of(j * BK, BK) for j in blocks]
            valid = [(keyi[:, lo:] + j * BK) < qryi[:, lo:] if m else None for j, m, lo in zip(blocks, masked, seen)]
            zs = [_mm_nt(_pair_stack(k_ref[pl.ds(ks, BK), :], lane), qs[lo:]) for ks, lo in zip(starts, seen)]
            sps = [_softplus(z) for z in zs]
            sps = [sp if ok is None else _mask_keys(sp, ok) for sp, ok in zip(sps, valid)]
            cums = [_mm(usuf_ref[...], _split_rows(sp)) for sp in sps]
            ws = [jnp.exp(z - c) for z, c in zip(zs, cums)]
            ws = [w if ok is None else _mask_keys(w, ok) for w, ok in zip(ws, valid)]
            pvs = [_mm(_heads_t(vt_ref[:, pl.ds(ks, BK)], rowi), w.astype(BF16)) for ks, w in zip(starts, ws)]
            for pv, c, lo in zip(pvs, cums, seen):
                r0, r1 = run_scr[0:1, lo:], run_scr[1:2, lo:]
                acc_scr[:, lo:] += jnp.where(rowi < 64, jnp.exp(-r0), jnp.exp(-r1)) * pv
                run_scr[0:1, lo:] = r0 + c[0:1]
                run_scr[1:2, lo:] = r1 + c[BK:BK + 1]

        assert WQ == 2 * BK

        def unfinished():
            return (jnp.min(run_scr[0:2, :]) < SB_CUTOFF).astype(jnp.int32)

        def query_block(sub):
            i = pl.program_id(1) * SB_NSUB + sub
            rows = pl.ds(pl.multiple_of(sub * WQ, WQ), WQ)
            qs = q_ref[rows, :] * (HEAD_DIM ** -0.5)
            acc_scr[...] = jnp.zeros_like(acc_scr)
            run_scr[...] = jnp.zeros_like(run_scr)

            @pl.when(i == 0)
            def _():
                group(i, qs, [1, 0], [True, True], [BK, 0])

            @pl.when(i > 0)
            def _():
                group(i, qs, [2 * i + 1, 2 * i, 2 * i - 1, 2 * i - 2], [True, True, False, False], [BK, 0, 0, 0])

            def step(c):
                group(i, qs, [2 * i - 1 - 2 * c[0], 2 * i - 2 - 2 * c[0]], [False, False])
                return c[0] + 1, unfinished()

            lax.while_loop(lambda c: (c[0] < i) & (c[1] > 0), step, (jnp.int32(1), unfinished()))
            o_ref[rows, :] = acc_scr[...].T

        lax.fori_loop(0, SB_NSUB, lambda sub, c: (query_block(sub), c)[1], 0)

        @pl.when((p == pl.num_programs(0) - 1) & (i == pl.num_programs(1) - 1))
        def _():
            gather_finish()
            copies = [pltpu.make_async_copy(bufs[t], outs[t], out_sems.at[t]) for t in range(n)]
            for cp in copies:
                cp.start()
            for cp in copies:
                cp.wait()

    qspec = pl.BlockSpec((SB_NSUB * WQ, LANES), lambda p, i: (i, p))
    kspec = pl.BlockSpec((s, LANES), lambda p, i: (0, p))
    tspec = pl.BlockSpec((LANES, s), lambda p, i: (p, 0))
    gathered = [jax.ShapeDtypeStruct((N_SHARD,) + a.shape, BF16) for a in late]
    return pl.pallas_call(
        body, name="sb_fwd", grid=(4, s // (SB_NSUB * WQ)),
        out_shape=(jax.ShapeDtypeStruct((s, 512), F32), *gathered),
        in_specs=[qspec, kspec, tspec, _full2((2 * BK, 4 * BK))] + [_full2(a.shape) for a in late],
        out_specs=(qspec,) + (pl.BlockSpec(memory_space=pl.ANY),) * n,
        scratch_shapes=[pltpu.VMEM((LANES, WQ), F32), pltpu.VMEM((8, WQ), F32)] + [pltpu.VMEM(g.shape, BF16) for g in gathered]
                       + [pltpu.SemaphoreType.DMA((6 * n,)), pltpu.SemaphoreType.DMA((6 * n,)), pltpu.SemaphoreType.DMA((n,))],
        compiler_params=pltpu.CompilerParams(vmem_limit_bytes=VMEM_ATTN),
    )(q, k, vt, _sum_matrix("suffix", 2), *late)


def _sb_bwd(q, k, kt, v, do, late):
    s = q.shape[0]
    n = len(late)
    halves = [a.shape[1] // 2 for a in late]

    def body(q_ref, k_ref, kt_ref, v_ref, do_ref, usuf_ref, upre_ref, *rest):
        g_refs, (dq_ref, dk_ref, dv_ref), outs = rest[:n], rest[n:n + 3], rest[n + 3:2 * n + 3]
        later_scr, dqt_scr, st_scr = rest[2 * n + 3:2 * n + 6]
        f_scr, reduce_scr, out_sems = rest[2 * n + 6:3 * n + 6], rest[3 * n + 6:-1], rest[-1]
        p, i = pl.program_id(0), pl.program_id(1)
        reduce_load, reduce_partial, reduce_total, reduce_finish = _reduce_steps(halves, g_refs, f_scr, reduce_scr)

        @pl.when((p == 0) & (i == 0))
        def _():
            reduce_load()

        @pl.when((p == 1) & (i == 0))
        def _():
            reduce_partial()

        @pl.when((p == 3) & (i == 0))
        def _():
            reduce_total()

        @pl.when(i == 0)
        def _():
            dk_ref[...] = jnp.zeros_like(dk_ref)
            dv_ref[...] = jnp.zeros_like(dv_ref)

        lane = lax.broadcasted_iota(jnp.int32, (1, LANES), 1)
        rowi = lax.broadcasted_iota(jnp.int32, (LANES, 1), 0)
        keyi = lax.broadcasted_iota(jnp.int32, (BK, WQ), 0)
        assert WQ == 2 * BK

        def query_block(sub):
            i = pl.program_id(1) * SB_NSUB + sub
            rows = pl.ds(pl.multiple_of(sub * WQ, WQ), WQ)
            qryi = lax.broadcasted_iota(jnp.int32, (BK, WQ), 1) + i * WQ
            qs = q_ref[rows, :] * (HEAD_DIM ** -0.5)
            dob = do_ref[rows, :]
            dot = dob.astype(F32).T.astype(BF16)

            def scores(j, lo=0):
                return _mm_nt(_pair_stack(k_ref[pl.ds(pl.multiple_of(j * BK, BK), BK), :], lane), qs[lo:])

            def scan(blocks, masked, seen=None):
                seen = seen or [0] * len(blocks)
                sps = [_softplus(scores(j, lo)) for j, lo in zip(blocks, seen)]
                sps = [_mask_keys(sp, (keyi[:, lo:] + j * BK) < qryi[:, lo:]) if m else sp
                       for sp, j, m, lo in zip(sps, blocks, masked, seen)]
                for sp, j, lo in zip(sps, blocks, seen):
                    run = st_scr[0:2, :]
                    later_scr[j, 0:2, :] = run
                    st_scr[0:2, lo:] = run[:, lo:] + jnp.concatenate([jnp.sum(sp[0:BK], axis=0, keepdims=True),
                                                                      jnp.sum(sp[BK:2 * BK], axis=0, keepdims=True)], axis=0)

            def sweep(blocks, masked, seen=None):
                seen = seen or [0] * len(blocks)
                starts = [pl.multiple_of(j * BK, BK) for j in blocks]
                valid = [(keyi[:, lo:] + j * BK) < qryi[:, lo:] if m else None for j, m, lo in zip(blocks, masked, seen)]
                zs = [scores(j, lo) for j, lo in zip(blocks, seen)]
                us = [jnp.exp(lax.bitcast_convert_type(lax.bitcast_convert_type(z, jnp.uint32) | jnp.uint32(0x80000000), F32))
                      for z in zs]
                sps = [jnp.maximum(z, 0.0) + jnp.log(1.0 + u) for z, u in zip(zs, us)]
                sps = [sp if ok is None else _mask_keys(sp, ok) for sp, ok in zip(sps, valid)]
                sigs = [jnp.where(z >= 0.0, 1.0, u) / (1.0 + u) for z, u in zip(zs, us)]
                cums = [_mm(usuf_ref[...], _split_rows(sp)) for sp in sps]
                dws = [_mm(_pair_stack(v_ref[pl.ds(ks, BK), :], lane), dot[:, lo:]) for ks, lo in zip(starts, seen)]
                wfs = []
                for z, c, j, ok, lo in zip(zs, cums, blocks, valid, seen):
                    f = jnp.exp(-later_scr[j, 0:2, lo:])
                    wide = (BK, WQ - lo)
                    wf = jnp.exp(z - c) * jnp.concatenate([jnp.broadcast_to(f[0:1], wide), jnp.broadcast_to(f[1:2], wide)], axis=0)
                    wfs.append(wf if ok is None else _mask_keys(wf, ok))
                es = [dw * wf for dw, wf in zip(dws, wfs)]
                pres = [_mm(upre_ref[...], e.astype(BF16)) for e in es]
                dzs = []
                for e, pre, sig, ok, lo in zip(es, pres, sigs, valid, seen):
                    e0 = pre[0:BK] + st_scr[0:1, lo:]
                    e1 = pre[BK:2 * BK] + st_scr[1:2, lo:]
                    st_scr[0:1, lo:] = e0[BK - 1:BK]
                    st_scr[1:2, lo:] = e1[BK - 1:BK]
                    dz = e - sig * jnp.concatenate([e0, e1], axis=0)
                    dzs.append((dz if ok is None else _mask_keys(dz, ok)).astype(BF16))
                whole = [b for b, lo in enumerate(seen) if lo == 0]
                dqt_scr[...] += _mm(jnp.concatenate([_heads_t(kt_ref[:, pl.ds(starts[b], BK)], rowi) for b in whole], axis=1),
                                    jnp.concatenate([dzs[b] for b in whole], axis=0))
                for b, lo in enumerate(seen):
                    if lo:
                        dqt_scr[:, lo:] += _mm(_heads_t(kt_ref[:, pl.ds(starts[b], BK)], rowi), dzs[b])
                for ks, dz, wf, lo in zip(starts, dzs, wfs, seen):
                    rk = _mm(dz, qs[lo:])
                    dk_ref[pl.ds(ks, BK), :] += jnp.where(lane < 64, rk[0:BK], rk[BK:2 * BK])
                    rv = _mm(wf.astype(BF16), dob[lo:])
                    dv_ref[pl.ds(ks, BK), :] += jnp.where(lane < 64, rv[0:BK], rv[BK:2 * BK])

            st_scr[...] = jnp.zeros_like(st_scr)

            @pl.when(i == 0)
            def _():
                scan([1, 0], [True, True], [BK, 0])

            @pl.when(i > 0)
            def _():
                scan([2 * i + 1, 2 * i, 2 * i - 1, 2 * i - 2], [True, True, False, False], [BK, 0, 0, 0])

            def unfinished():
                return (jnp.min(st_scr[0:2, :]) < SB_CUTOFF).astype(jnp.int32)

            def step(c):
                scan([2 * i - 1 - 2 * c[0], 2 * i - 2 - 2 * c[0]], [False, False])
                return c[0] + 1, unfinished()

            npairs, _ = lax.while_loop(lambda c: (c[0] < i) & (c[1] > 0), step, (jnp.minimum(i, 1), unfinished()))

            st_scr[...] = jnp.zeros_like(st_scr)
            dqt_scr[...] = jnp.zeros_like(dqt_scr)
            first = 2 * (i - npairs)

            def early(t, carry):
                sweep([first + 2 * t, first + 2 * t + 1], [False, False])
                return carry

            lax.fori_loop(0, npairs - 1, early, 0)

            @pl.when(i == 0)
            def _():
                sweep([0, 1], [True, True], [0, BK])

            @pl.when(i > 0)
            def _():
                sweep([2 * i - 2, 2 * i - 1, 2 * i, 2 * i + 1], [False, False, True, True], [0, 0, 0, BK])

            dq_ref[rows, :] = (dqt_scr[...].T * (HEAD_DIM ** -0.5)).astype(BF16)

        lax.fori_loop(0, SB_NSUB, lambda sub, c: (query_block(sub), c)[1], 0)

        @pl.when((p == pl.num_programs(0) - 1) & (i == pl.num_programs(1) - 1))
        def _():
            reduce_finish()
            copies = [pltpu.make_async_copy(f_scr[t], outs[t], out_sems.at[t]) for t in range(n)]
            for cp in copies:
                cp.start()
            for cp in copies:
                cp.wait()

    qspec = pl.BlockSpec((SB_NSUB * WQ, LANES), lambda p, i: (i, p))
    kspec = pl.BlockSpec((s, LANES), lambda p, i: (0, p))
    tspec = pl.BlockSpec((LANES, s), lambda p, i: (p, 0))
    anywhere = pl.BlockSpec(memory_space=pl.ANY)
    reduced = [jax.ShapeDtypeStruct(a.shape[1:], F32) for a in late]
    return pl.pallas_call(
        body, name="sb_bwd", grid=(4, s // (SB_NSUB * WQ)),
        out_shape=(jax.ShapeDtypeStruct((s, 512), BF16), jax.ShapeDtypeStruct((s, 512), F32),
                   jax.ShapeDtypeStruct((s, 512), F32), *reduced),
        in_specs=[qspec, kspec, tspec, kspec, qspec, _full2((2 * BK, 4 * BK)), _full2((2 * BK, 2 * BK))] + [anywhere] * n,
        out_specs=(qspec, kspec, kspec) + (anywhere,) * n,
        scratch_shapes=[pltpu.VMEM((s // BK, 8, WQ), F32), pltpu.VMEM((LANES, WQ), F32), pltpu.VMEM((8, WQ), F32)]
                       + [pltpu.VMEM(r.shape, F32) for r in reduced] + _reduce_scratch(late) + [pltpu.SemaphoreType.DMA((n,))],
        compiler_params=pltpu.CompilerParams(vmem_limit_bytes=VMEM_ATTN),
    )(q, k, kt, v, do, _sum_matrix("suffix", 2), _sum_matrix("prefix", 1), *late)


MLA_SCALE = (QK_NOPE + QK_ROPE) ** -0.5
LOG2E = 1.4426950408889634


def _mla_keys(kb):
    zero = jnp.zeros((BK, LANES), kb.dtype)
    return jnp.concatenate([jnp.concatenate([kb[:, 0:LANES], zero], axis=1),
                            jnp.concatenate([zero, kb[:, LANES:2 * LANES]], axis=1)], axis=0)


def _mla_fwd(qc, kc, vt):
    s = qc.shape[0]
    mq = min(MQ_FWD, s)
    rows_l = 16

    def body(q_ref, k_ref, vt_ref, o_ref, l_ref, p_scr, ot_scr, st_scr):
        i = pl.program_id(1)
        row = lax.broadcasted_iota(jnp.int32, (LANES, 1), 0)
        orow = lax.broadcasted_iota(jnp.int32, (rows_l, 2 * BK), 0)
        ocol = lax.broadcasted_iota(jnp.int32, (rows_l, 2 * BK), 1)
        ones = jnp.where(((orow == 0) & (ocol < BK)) | ((orow == 1) & (ocol >= BK)), 1.0, 0.0).astype(BF16)

        def chunks(lo, hi):
            return [(a, min(a + MLA_CW, hi)) for a in range(lo, hi, MLA_CW)]

        def keys(j):
            return _mla_keys(k_ref[pl.ds(pl.multiple_of(j * BK, BK), BK), :])

        def values_t(j):
            vtb = vt_ref[:, pl.ds(pl.multiple_of(j * BK, BK), BK)]
            zero = jnp.zeros_like(vtb)
            top = jnp.concatenate([jnp.where(row < 64, vtb, zero), jnp.where(row >= 64, vtb, zero)], axis=1)
            return jnp.concatenate([top, ones], axis=0)

        def pair_values(ja):
            return jnp.concatenate([values_t(ja), values_t(ja + 1)], axis=1)

        def softmax(ja, za, zb, masked, a, b):
            c = MLA_SCALE * LOG2E
            parts = [za[0:BK] * c, za[BK:2 * BK] * c, zb[0:BK] * c, zb[BK:2 * BK] * c]
            if masked:
                keyc = lax.broadcasted_iota(jnp.int32, (BK, b - a), 0)
                qryc = (lax.broadcasted_iota(jnp.int32, (BK, b - a), 1) + (i * mq + a)) // 64
                va = ((keyc + ja * BK) // 64) <= qryc
                vb = ((keyc + (ja + 1) * BK) // 64) <= qryc
                parts = [jnp.where(va, parts[0], -1e30), jnp.where(va, parts[1], -1e30),
                         jnp.where(vb, parts[2], -1e30), jnp.where(vb, parts[3], -1e30)]
            m0, m1 = st_scr[0:1, a:b], st_scr[1:2, a:b]
            n0 = jnp.maximum(m0, jnp.max(jnp.maximum(parts[0], parts[2]), axis=0, keepdims=True))
            n1 = jnp.maximum(m1, jnp.max(jnp.maximum(parts[1], parts[3]), axis=0, keepdims=True))
            st_scr[2:3, a:b] = jnp.exp2(m0 - n0)
            st_scr[3:4, a:b] = jnp.exp2(m1 - n1)
            st_scr[0:1, a:b] = n0
            st_scr[1:2, a:b] = n1
            p_scr[:, a:b] = jnp.concatenate([jnp.exp2(parts[0] - n0), jnp.exp2(parts[1] - n1),
                                             jnp.exp2(parts[2] - n0), jnp.exp2(parts[3] - n1)], axis=0).astype(BF16)

        def accumulate(vals, a, b):
            pv = _mm(vals, p_scr[:, a:b])
            f = jnp.where(row < 64, st_scr[2:3, a:b], st_scr[3:4, a:b])
            ot_scr[0:LANES, a:b] = f * ot_scr[0:LANES, a:b] + pv[0:LANES]
            ot_scr[LANES:LANES + 8, a:b] = st_scr[2:10, a:b] * ot_scr[LANES:LANES + 8, a:b] + pv[LANES:LANES + 8]

        def step(n, diag, lo=0, prev_lo=0):
            kab = jnp.concatenate([keys(2 * n), keys(2 * n + 1)], axis=0)
            vals = pair_values(2 * n - 2)
            for a, b in chunks(prev_lo, lo):
                accumulate(vals, a, b)
            for a, b in chunks(lo, mq):
                zab = _mm_nt(kab, q_ref[a:b, :])
                accumulate(vals, a, b)
                softmax(2 * n, zab[0:2 * BK], zab[2 * BK:4 * BK], diag and a < lo + 2 * BK, a, b)

        def first(diag):
            kab = jnp.concatenate([keys(0), keys(1)], axis=0)
            for a, b in chunks(0, mq):
                zab = _mm_nt(kab, q_ref[a:b, :])
                softmax(0, zab[0:2 * BK], zab[2 * BK:4 * BK], diag and a < 2 * BK, a, b)

        st_scr[...] = jnp.concatenate([jnp.full((2, mq), -1e30, F32), jnp.ones((14, mq), F32)], axis=0)
        ot_scr[...] = jnp.zeros_like(ot_scr)

        npq = mq // (2 * BK)
        seen = lambda d: 2 * BK * max(d, 0)

        @pl.when(i == 0)
        def _():
            first(True)
            for d in range(1, npq):
                step(d, True, seen(d), seen(d - 1))

        if s > mq:
            @pl.when(i > 0)
            def _():
                first(False)
                lax.fori_loop(1, npq * i, lambda n, c: (step(n, False), c)[1], 0)
                for d in range(npq):
                    step(npq * i + d, True, seen(d), seen(d - 1))

        vals = pair_values(2 * (npq * (i + 1) - 1))
        for a, b in chunks(seen(npq - 1), mq):
            accumulate(vals, a, b)
        for a, b in chunks(0, mq):
            l0, l1 = ot_scr[LANES:LANES + 1, a:b], ot_scr[LANES + 1:LANES + 2, a:b]
            o_ref[a:b, :] = (ot_scr[0:LANES, a:b] / jnp.where(row < 64, l0, l1)).T
            l_ref[a:b, :] = jnp.where(row < 64, st_scr[0:1, a:b] + jnp.log2(l0), st_scr[1:2, a:b] + jnp.log2(l1)).T

    qspec = pl.BlockSpec((mq, 2 * LANES), lambda p, i: (i, p))
    kspec = pl.BlockSpec((s, 2 * LANES), lambda p, i: (0, p))
    vtspec = pl.BlockSpec((LANES, s), lambda p, i: (p, 0))
    ospec = pl.BlockSpec((mq, LANES), lambda p, i: (i, p))
    return pl.pallas_call(
        body, name="mla_fwd", grid=(4, s // mq),
        out_shape=(jax.ShapeDtypeStruct((s, 512), F32), jax.ShapeDtypeStruct((s, 512), F32)),
        in_specs=[qspec, kspec, vtspec], out_specs=(ospec, ospec),
        scratch_shapes=[pltpu.VMEM((4 * BK, mq), BF16), pltpu.VMEM((LANES + 8, mq), F32), pltpu.VMEM((16, mq), F32)],
        compiler_params=pltpu.CompilerParams(vmem_limit_bytes=VMEM_ATTN),
    )(qc, kc, vt)


def _mla_bwd(qc, kc, kct, v, do, lse, delta):
    s = qc.shape[0]
    mq = min(MQ_BWD, s)

    def body(q_ref, k_ref, kt_ref, v_ref, do_ref, l_ref, d_ref, dq_ref, dk_ref, dv_ref, dqt_scr, p_scr, dz_scr,
             dvt_scr):
        i = pl.program_id(1)

        @pl.when(i == 0)
        def _():
            dk_ref[...] = jnp.zeros_like(dk_ref)
            dvt_scr[...] = jnp.zeros_like(dvt_scr)

        lane = lax.broadcasted_iota(jnp.int32, (1, LANES), 1)
        keyc = lax.broadcasted_iota(jnp.int32, (BK, mq), 0)
        qryc = (lax.broadcasted_iota(jnp.int32, (BK, mq), 1) + i * mq) // 64
        qw = q_ref[...]
        dob = do_ref[...]
        dost = (dob.astype(F32) * MLA_SCALE).T.astype(BF16)
        dot_ = dob.astype(F32).T.astype(BF16)
        lt = l_ref[...].T
        dt = (d_ref[...] * MLA_SCALE).T
        lse0, lse1 = lt[0:1], lt[64:65]
        dl0, dl1 = dt[0:1], dt[64:65]
        dqt_scr[...] = jnp.zeros_like(dqt_scr)

        def products(j, lo=0):
            ks = pl.multiple_of(j * BK, BK)
            return (_mm_nt(_mla_keys(k_ref[pl.ds(ks, BK), :]), qw[lo:]),
                    _mm(_pair_stack(v_ref[pl.ds(ks, BK), :], lane), dost[:, lo:]))

        def grads(j, slot, zt, dwt, masked, lo=0):
            zt = zt * (MLA_SCALE * LOG2E)
            p0 = jnp.exp2(zt[0:BK] - lse0[:, lo:])
            p1 = jnp.exp2(zt[BK:2 * BK] - lse1[:, lo:])
            if masked:
                valid = ((keyc[:, lo:] + j * BK) // 64) <= qryc[:, lo:]
                p0, p1 = jnp.where(valid, p0, 0.0), jnp.where(valid, p1, 0.0)
            rows = slice(slot * BK, (slot + 1) * BK)
            p_scr[0, rows, lo:] = p0.astype(BF16)
            p_scr[1, rows, lo:] = p1.astype(BF16)
            dz_scr[0, rows, lo:] = (p0 * (dwt[0:BK] - dl0[:, lo:])).astype(BF16)
            dz_scr[1, rows, lo:] = (p1 * (dwt[BK:2 * BK] - dl1[:, lo:])).astype(BF16)

        def scatter(ja, lo=0):
            ks = pl.multiple_of(ja * BK, 2 * BK)
            for h in range(2):
                cols, vrows = slice(h * LANES, (h + 1) * LANES), slice(h * 64, (h + 1) * 64)
                dzh = dz_scr[h, :, lo:]
                dqt_scr[cols, lo:] += _mm(kt_ref[cols, pl.ds(ks, 2 * BK)], dzh)
                dk_ref[pl.ds(ks, 2 * BK), cols] += _mm(dzh, qw[lo:, cols])
                dvt_scr[vrows, pl.ds(ks, 2 * BK)] += _mm_nt(dot_[vrows, lo:], p_scr[h, :, lo:])

        def step(n, masked, lo=0, prev_lo=0):
            za, wa = products(2 * n, lo)
            zb, wb = products(2 * n + 1, lo)
            scatter(2 * n - 2, prev_lo)
            grads(2 * n, 0, za, wa, masked, lo)
            grads(2 * n + 1, 1, zb, wb, masked, lo)

        def first(masked):
            za, wa = products(0)
            zb, wb = products(1)
            grads(0, 0, za, wa, masked)
            grads(1, 1, zb, wb, masked)

        npq = mq // (2 * BK)
        seen = lambda d: 2 * BK * max(d, 0)

        @pl.when(i == 0)
        def _():
            first(True)
            for d in range(1, npq):
                step(d, True, seen(d), seen(d - 1))

        @pl.when(i > 0)
        def _():
            first(False)
            step(1, False)
            lax.fori_loop(1, npq * i // 2, lambda m, c: (step(2 * m, False), step(2 * m + 1, False), c)[2], 0)
            for d in range(npq):
                step(npq * i + d, True, seen(d), seen(d - 1))

        scatter(2 * (npq * (i + 1) - 1), seen(npq - 1))
        dq_ref[...] = dqt_scr[...].T

        @pl.when(i == s // mq - 1)
        def _():
            for a in range(0, s, 512):
                dv_ref[a:a + 512, :] = dvt_scr[:, a:a + 512].T

    qspec = pl.BlockSpec((mq, 2 * LANES), lambda p, i: (i, p))
    kspec = pl.BlockSpec((s, 2 * LANES), lambda p, i: (0, p))
    ktspec = pl.BlockSpec((2 * LANES, s), lambda p, i: (p, 0))
    vspec = pl.BlockSpec((s, LANES), lambda p, i: (0, p))
    ospec = pl.BlockSpec((mq, LANES), lambda p, i: (i, p))
    return pl.pallas_call(
        body, name="mla_bwd", grid=(4, s // mq),
        out_shape=(jax.ShapeDtypeStruct((s, 1024), F32), jax.ShapeDtypeStruct((s, 1024), F32),
                   jax.ShapeDtypeStruct((s, 512), F32)),
        in_specs=[qspec, kspec, ktspec, vspec, ospec, ospec, ospec], out_specs=(qspec, kspec, vspec),
        scratch_shapes=[pltpu.VMEM((2 * LANES, mq), F32), pltpu.VMEM((2, 2 * BK, mq), BF16), pltpu.VMEM((2, 2 * BK, mq), BF16),
                        pltpu.VMEM((LANES, s), F32)],
        compiler_params=pltpu.CompilerParams(vmem_limit_bytes=VMEM_ATTN),
    )(qc, kc, kct, v, do, lse, delta)


def _post(x, p, tgt, sbo, mlao, sbg, mlag, gsb, gmla, wout, gpost, wple, gple, wpg, bpg):
    s = x.shape[0]

    def body(x_ref, p_ref, t_ref, sbo_ref, mlao_ref, sbg_ref, mlag_ref, gsb_ref, gmla_ref, wout_ref,
             gpost_ref, wple_ref, gple_ref, wpg_ref, bpg_ref, bd_ref,
             dsbo_ref, dmlao_ref, delta_ref, dsbg_ref, dmlag_ref, dxres_ref, dwout_ref, dwpg_ref, dwple_ref, vec_ref):
        i = pl.program_id(0)

        @pl.when(i == 0)
        def _():
            dwout_ref[...] = jnp.zeros_like(dwout_ref)
            dwpg_ref[...] = jnp.zeros_like(dwpg_ref)
            dwple_ref[...] = jnp.zeros_like(dwple_ref)
            vec_ref[...] = jnp.zeros_like(vec_ref)

        inv_hd = 1.0 / HEAD_DIM

        def head_fwd(o, g, gate):
            r = lax.rsqrt(_seg(o * o, bd_ref[...]) * inv_hd + EPS)
            hat = o * r
            n = hat * g
            sg = _sigmoid(gate)
            return hat, r, n, sg, n * (gate * sg)

        sbo, mlao, sbg_v, mlag_v = sbo_ref[...], mlao_ref[...], sbg_ref[...], mlag_ref[...]
        gsb_v, gmla_v = gsb_ref[...], gmla_ref[...]
        sb_hat, sb_r, sb_n, sb_sg, sb_y = head_fwd(sbo, gsb_v, sbg_v)
        ml_hat, ml_r, ml_n, ml_sg, ml_y = head_fwd(mlao, gmla_v, mlag_v)
        mix = jnp.concatenate([sb_y, ml_y], axis=1).astype(BF16)
        y = _mm(mix, wout_ref[...])
        ry = lax.rsqrt(_rowmean(y * y) + EPS)
        y_hat = y * ry
        gpost_v = gpost_ref[...]
        x1 = x_ref[...] + y_hat * gpost_v
        pb = p_ref[...].astype(BF16)
        pl_ = jnp.concatenate([_mm(pb, wple_ref[k]) for k in range(N_SHARD)], axis=1)
        rp = lax.rsqrt(_rowmean(pl_ * pl_) + EPS)
        pl_hat = pl_ * rp
        gple_v = gple_ref[...]
        ple = pl_hat * gple_v
        x1b = x1.astype(BF16)
        gate = _sigmoid(_mm(x1b, wpg_ref[...]) + bpg_ref[...])
        err = x1 + ple * gate - t_ref[...]
        loss = 0.5 * jnp.sum(_rowmean(err * err))
        dout = err * (1.0 / D_MODEL)

        du = dout * ple * gate * (1.0 - gate)
        dub = du.astype(BF16)
        dple = dout * gate
        dx1 = dout + _mm_nt(dub, wpg_ref[...])
        dwpg_ref[...] += _mm_tn(x1b, dub)
        dplh = dple * gple_v
        dpl = rp * (dplh - pl_hat * _rowmean(dplh * pl_hat))
        dplb = dpl.astype(BF16)
        wsh = D_MODEL // N_SHARD
        for k in range(N_SHARD):
            dwple_ref[k] += _mm_tn(pb, dplb[:, k * wsh:(k + 1) * wsh])
        dxres_ref[...] = dx1
        dyh = dx1 * gpost_v
        dy = ry * (dyh - y_hat * _rowmean(dyh * y_hat))
        dyb = dy.astype(BF16)
        dwout_ref[...] += _mm_tn(mix, dyb)
        dmix = _mm_nt(dyb, wout_ref[...])

        def head_bwd(dyv, hat, r, n, sg, g, gate):
            dn = dyv * (gate * sg)
            dgate = dyv * n * (sg * (1.0 + gate * (1.0 - sg)))
            dhat = dn * g
            do = r * (dhat - hat * (_seg(dhat * hat, bd_ref[...]) * inv_hd))
            return do, dgate, _colsum(dn * hat)

        dsbo, dsbg, dg_sb = head_bwd(dmix[:, 0:512], sb_hat, sb_r, sb_n, sb_sg, gsb_v, sbg_v)
        dmlao, dmlag, dg_ml = head_bwd(dmix[:, 512:1024], ml_hat, ml_r, ml_n, ml_sg, gmla_v, mlag_v)
        dsbo_ref[...] = dsbo.astype(BF16)
        dmlao_ref[...] = dmlao.astype(BF16)
        delta_ref[...] = _seg(dmlao * mlao, bd_ref[...])
        dsbg_ref[...] = dsbg.astype(BF16)
        dmlag_ref[...] = dmlag.astype(BF16)
        vec_ref[pl.ds(0, 1), :] += _colsum(dx1 * y_hat)
        vec_ref[pl.ds(1, 1), :] += _colsum(dple * pl_hat)
        vec_ref[pl.ds(2, 1), :] += _colsum(du)
        vec_ref[pl.ds(3, 1), :] += jnp.concatenate([dg_sb, dg_ml], axis=1)
        vec_ref[pl.ds(4, 1), :] += jnp.full((1, D_MODEL), loss, F32)

    out_shape = (
        jax.ShapeDtypeStruct((s, 512), BF16), jax.ShapeDtypeStruct((s, 512), BF16), jax.ShapeDtypeStruct((s, 512), F32),
        jax.ShapeDtypeStruct((s, 512), BF16), jax.ShapeDtypeStruct((s, 512), BF16), jax.ShapeDtypeStruct((s, D_MODEL), F32),
        jax.ShapeDtypeStruct((D_MODEL, D_MODEL), F32), jax.ShapeDtypeStruct((D_MODEL, D_MODEL), F32),
        jax.ShapeDtypeStruct(wple.shape, F32), jax.ShapeDtypeStruct((8, D_MODEL), F32),
    )
    return pl.pallas_call(
        body, name="post_fwd_bwd", grid=(s // TM,), out_shape=out_shape,
        in_specs=[_rows(D_MODEL), _rows(PLE_DIM), _rows(D_MODEL), _rows(512), _rows(512), _rows(512), _rows(512),
                  _full((1, 512)), _full((1, 512)), _full((D_MODEL, D_MODEL)),
                  _full((1, D_MODEL)), _full(wple.shape), _full((1, D_MODEL)), _full((D_MODEL, D_MODEL)),
                  _full((1, D_MODEL)), _full((1024, 512))],
        out_specs=(_rows(512), _rows(512), _rows(512), _rows(512), _rows(512), _rows(D_MODEL),
                   _acc((D_MODEL, D_MODEL)), _acc((D_MODEL, D_MODEL)), _acc(wple.shape), _acc((8, D_MODEL))),
        compiler_params=pltpu.CompilerParams(vmem_limit_bytes=VMEM_DENSE),
    )(x, p, tgt, sbo, mlao, sbg, mlag, gsb, gmla, wout, gpost, wple, gple, wpg, bpg, _blockdiag2(512, HEAD_DIM))


def _pre_bwd(x, dxres, dsbq, dsbk, dsbv, dsbg, dmlag, dqc, dkc, dmv, cq, ckv, tabs, gpre, win, gq, wuq, gkv, wk, wv):
    s = x.shape[0]
    c_t, sa_t, sb_t = tabs
    rw = _rows

    def body(x_ref, dxres_ref, dsbq_ref, dsbk_ref, dsbv_ref, dsbg_ref, dmlag_ref, dqc_ref, dkc_ref, dmv_ref, cq_ref,
             ckv_ref, c_ref, sa_ref, sb_ref, gpre_ref, win_ref, gq_ref, wuq_ref, gkv_ref, wk_ref, wv_ref,
             gx_ref, dwin_ref, dwuq_ref, dwk_ref, dwv_ref, vec_ref, dwin_acc):
        i = pl.program_id(0)

        @pl.when(i == 0)
        def _():
            dwin_acc[...] = jnp.zeros_like(dwin_acc)
            dwuq_ref[...] = jnp.zeros_like(dwuq_ref)
            dwk_ref[...] = jnp.zeros_like(dwk_ref)
            dwv_ref[...] = jnp.zeros_like(dwv_ref)
            vec_ref[...] = jnp.zeros_like(vec_ref)

        lane = lax.broadcasted_iota(jnp.int32, (1, LANES), 1)
        c1, sa1, sb1 = c_ref[...], sa_ref[...], sb_ref[...]
        c8, sa8, sb8 = jnp.tile(c1, (1, 8)), jnp.tile(sa1, (1, 8)), jnp.tile(sb1, (1, 8))

        def norm_bwd(dn, hat, r, g):
            t = dn * g
            return r * (t - hat * _rowmean(t * hat)), _colsum(dn * hat)

        xv = x_ref[...]
        r1 = lax.rsqrt(_rowmean(xv * xv) + EPS)
        x_hat = xv * r1
        gpre_v = gpre_ref[...]
        hb = (x_hat * gpre_v).astype(BF16)
        ready = jnp.concatenate([dsbq_ref[...], dsbk_ref[...].astype(BF16), dsbv_ref[...].astype(BF16), dsbg_ref[...]], axis=1)
        dmlag = dmlag_ref[...]
        dwin_acc[:, 0:2048] += _mm_tn(hb, ready)
        dwin_acc[:, 2560:3072] += _mm_tn(hb, dmlag)
        dh = _mm_nt(ready, win_ref[:, 0:2048]) + _mm_nt(dmlag, win_ref[:, 2560:3072])

        dqeb = _rope_bwd(dqc_ref[...], c8, sa8, sb8).astype(BF16)
        cq = cq_ref[...]
        rq = lax.rsqrt(_rowmean(cq * cq) + EPS)
        cq_hat = cq * rq
        gq_v = gq_ref[...]
        dwuq_ref[...] += _mm_tn((cq_hat * gq_v).astype(BF16), dqeb)
        dcq, dg_q = norm_bwd(_mm_nt(dqeb, wuq_ref[...]), cq_hat, rq, gq_v)

        dkc = dkc_ref[...]
        dkcb = dkc.astype(BF16)
        dmvb = dmv_ref[...].astype(BF16)
        ckv = ckv_ref[...]
        rkv = lax.rsqrt(_rowmean(ckv * ckv) + EPS)
        ckv_hat = ckv * rkv
        gkv_v = gkv_ref[...]
        ckvnb = (ckv_hat * gkv_v).astype(BF16)
        dwk_ref[...] += _mm_tn(ckvnb, dkcb)
        dwv_ref[...] += _mm_tn(ckvnb, dmvb)
        dckv, dg_kv = norm_bwd(_mm_nt(dkcb, wk_ref[...]) + _mm_nt(dmvb, wv_ref[...]), ckv_hat, rkv, gkv_v)

        dkr = dkc[:, 0:LANES]
        for hh in range(1, 8):
            dkr = dkr + dkc[:, LANES * hh:LANES * (hh + 1)]
        dkr = _rope_bwd(dkr, c1, sa1, sb1)
        dkr = jnp.where((lane >= 64) & (lane < 96), dkr, 0.0)

        late = jnp.concatenate([dcq.astype(BF16), dckv.astype(BF16), dkr.astype(BF16)], axis=1)
        dwin_acc[:, 2048:2560] += _mm_tn(hb, late)
        dx, dg_pre = norm_bwd(dh + _mm_nt(late, win_ref[:, 2048:2560]), x_hat, r1, gpre_v)
        gx_ref[...] = dxres_ref[...] + dx
        vec_ref[pl.ds(0, 1), :] += dg_pre
        vec_ref[pl.ds(1, 1), :] += jnp.concatenate([dg_q, dg_kv, jnp.zeros((1, D_MODEL - Q_LORA - KV_LORA), F32)], axis=1)

        @pl.when(i == pl.num_programs(0) - 1)
        def _():
            pltpu.sync_copy(dwin_acc, dwin_ref)

    out_shape = (
        jax.ShapeDtypeStruct((s, D_MODEL), F32), jax.ShapeDtypeStruct((D_MODEL, D_EXT), F32),
        jax.ShapeDtypeStruct((Q_LORA, 1024), F32), jax.ShapeDtypeStruct((KV_LORA, 1024), F32),
        jax.ShapeDtypeStruct((KV_LORA, 512), F32), jax.ShapeDtypeStruct((8, D_MODEL), F32),
    )
    return pl.pallas_call(
        body, name="pre_bwd", grid=(s // TM,), out_shape=out_shape,
        in_specs=[rw(D_MODEL), rw(D_MODEL), rw(512), rw(512), rw(512), rw(512), rw(512),
                  rw(1024), rw(1024), rw(512), rw(Q_LORA), rw(KV_LORA), rw(LANES), rw(LANES),
                  rw(LANES), _full((1, D_MODEL)), _full((D_MODEL, D_EXT)), _full((1, Q_LORA)), _full((Q_LORA, 1024)),
                  _full((1, KV_LORA)), _full((KV_LORA, 1024)), _full((KV_LORA, 512))],
        out_specs=(rw(D_MODEL), pl.BlockSpec(memory_space=pl.ANY), _acc((Q_LORA, 1024)), _acc((KV_LORA, 1024)),
                   _acc((KV_LORA, 512)), _acc((8, D_MODEL))),
        scratch_shapes=[pltpu.VMEM((D_MODEL, D_EXT), F32)],
        compiler_params=pltpu.CompilerParams(vmem_limit_bytes=VMEM_DENSE),
    )(x, dxres, dsbq, dsbk, dsbv, dsbg, dmlag, dqc, dkc, dmv, cq, ckv, c_t, sa_t, sb_t, gpre, win, gq, wuq, gkv, wk, wv)


def _place():
    return lax.axis_index("x"), lax.axis_index("y"), lax.axis_index("c")


def _gather_steps(shapes, ins, bufs, send_sems, recv_sems):
    n = len(shapes)
    x, y, c = _place()
    me, sib = (x, y, c), (x, y, 1 - c)
    chips = [(1 - x, y), (x, 1 - y), (1 - x, 1 - y)]

    def half(t, chip, hc):
        rows = shapes[t][0] // 2
        return bufs[t].at[2 * chip[0] + chip[1], pl.ds(pl.multiple_of(hc * rows, 16), rows), :]

    def copy(k, t, chip, hc, to):
        return pltpu.make_async_remote_copy(src_ref=half(t, chip, hc), dst_ref=half(t, chip, hc), send_sem=send_sems.at[k],
                                            recv_sem=recv_sems.at[k], device_id=to, device_id_type=MESH)

    def start():
        for t in range(n):
            bufs[t][2 * x + y] = ins[t][...].astype(BF16)
            for j, chip in enumerate(chips):
                copy(6 * t + j, t, (x, y), c, (*chip, c)).start()

    def forward():
        for t in range(n):
            for j, chip in enumerate(chips):
                copy(6 * t + j, t, chip, c, me).wait_recv()
                copy(6 * t + 3 + j, t, chip, c, sib).start()

    def finish():
        for t in range(n):
            for j, chip in enumerate(chips):
                copy(6 * t + 3 + j, t, chip, 1 - c, me).wait_recv()
        for t in range(n):
            for j, chip in enumerate(chips):
                copy(6 * t + j, t, (x, y), c, (*chip, c)).wait_send()
                copy(6 * t + 3 + j, t, chip, c, sib).wait_send()

    return start, forward, finish


def _allgather_weights(shards):
    n = len(shards)

    def body(*refs):
        start, forward, finish = _gather_steps([a.shape for a in shards], refs[:n], refs[n:2 * n], refs[2 * n], refs[2 * n + 1])
        start()
        forward()
        finish()

    return pl.pallas_call(
        body, name="allgather_weights",
        out_shape=tuple(jax.ShapeDtypeStruct((N_SHARD,) + a.shape, BF16) for a in shards),
        in_specs=[pl.BlockSpec(memory_space=pltpu.VMEM)] * n, out_specs=(pl.BlockSpec(memory_space=pltpu.VMEM),) * n,
        scratch_shapes=[pltpu.SemaphoreType.DMA((6 * n,)), pltpu.SemaphoreType.DMA((6 * n,))],
        compiler_params=pltpu.CompilerParams(vmem_limit_bytes=VMEM_ATTN),
    )(*shards)


def _reduce_scratch(gsh):
    n = len(gsh)
    half_shapes = [(N_SHARD, a.shape[1] // 2, a.shape[2]) for a in gsh]
    return ([pltpu.VMEM(s_, F32) for s_ in half_shapes] * 2 + [pltpu.VMEM(s_, BF16) for s_ in half_shapes] * 2
            + [pltpu.SemaphoreType.DMA((n,)), pltpu.SemaphoreType.DMA((5 * n,)), pltpu.SemaphoreType.DMA((5 * n,))])


def _reduce_steps(halves, g_refs, f_refs, scratch):
    n = len(halves)
    accs, sibs, sbufs, rbufs = scratch[0:n], scratch[n:2 * n], scratch[2 * n:3 * n], scratch[3 * n:4 * n]
    local_sems, send_sems, recv_sems = scratch[4 * n:4 * n + 3]
    x, y, c = _place()
    me, sib = (x, y, c), (x, y, 1 - c)
    mine = 2 * x + y
    chips = [(1 - x, y), (x, 1 - y), (1 - x, 1 - y)]

    def remote(k, src, dst, to):
        return pltpu.make_async_remote_copy(src_ref=src, dst_ref=dst, send_sem=send_sems.at[k], recv_sem=recv_sems.at[k],
                                            device_id=to, device_id_type=MESH)

    def half3(ref, t, hc):
        return ref.at[:, pl.ds(pl.multiple_of(hc * halves[t], 8), halves[t]), :]

    def half2(ref, t, hc):
        return ref.at[pl.ds(pl.multiple_of(hc * halves[t], 8), halves[t]), :]

    def mine_load(t):
        return pltpu.make_async_copy(half3(g_refs[t], t, c), accs[t], local_sems.at[t])

    def to_sibling(t, to):
        return remote(t, half3(g_refs[t], t, 1 - c), sibs[t], to)

    def to_chip(t, j, chip, to):
        idx = 2 * chip[0] + chip[1]
        return remote(n + 3 * t + j, sbufs[t].at[idx], rbufs[t].at[mine if to is not me else idx], to)

    def swap(t, hc, to):
        return remote(4 * n + t, half2(f_refs[t], t, hc), half2(f_refs[t], t, hc), to)

    def load():
        for t in range(n):
            mine_load(t).start()
            to_sibling(t, sib).start()

    def partial():
        for t in range(n):
            mine_load(t).wait()
            to_sibling(t, me).wait_recv()
            for k in range(N_SHARD):
                accs[t][k] = accs[t][k] + sibs[t][k]
            for j, chip in enumerate(chips):
                idx = 2 * chip[0] + chip[1]
                sbufs[t][idx] = accs[t][idx].astype(BF16)
                to_chip(t, j, chip, (*chip, c)).start()

    def total():
        for t in range(n):
            acc = accs[t][mine]
            for j, chip in enumerate(chips):
                to_chip(t, j, chip, me).wait_recv()
                acc = acc + rbufs[t][2 * chip[0] + chip[1]].astype(F32)
            half2(f_refs[t], t, c)[...] = acc
            swap(t, c, sib).start()

    def finish():
        for t in range(n):
            swap(t, 1 - c, me).wait_recv()
        for t in range(n):
            to_sibling(t, sib).wait_send()
            for j, chip in enumerate(chips):
                to_chip(t, j, chip, (*chip, c)).wait_send()
            swap(t, c, sib).wait_send()

    return load, partial, total, finish


def _reduce_scatter_grads(gsh, vec):
    n = len(gsh)
    halves = [a.shape[1] // 2 for a in gsh]

    def body(*refs):
        g_refs, vec_ref, f_refs, vsum_ref = refs[:n], refs[n], refs[n + 1:2 * n + 1], refs[2 * n + 1]
        scratch = refs[2 * n + 2:]
        vrecv, vsend_sems, vrecv_sems = scratch[4 * n + 3:]
        load, partial, total, finish = _reduce_steps(halves, g_refs, f_refs, scratch)
        x, y, c = _place()
        my_dev = 4 * x + 2 * y + c

        def flip(k):
            return x ^ ((k >> 2) & 1), y ^ ((k >> 1) & 1), c ^ (k & 1)

        def vcopy(k, slot, to):
            return pltpu.make_async_remote_copy(src_ref=vec_ref, dst_ref=vrecv.at[slot], send_sem=vsend_sems.at[k - 1],
                                                recv_sem=vrecv_sems.at[k - 1], device_id=to, device_id_type=MESH)

        load()
        vrecv[my_dev] = vec_ref[...]
        for k in range(1, 8):
            vcopy(k, my_dev, flip(k)).start()
        partial()
        total()
        finish()
        for k in range(1, 8):
            fx, fy, fc = flip(k)
            vcopy(k, 4 * fx + 2 * fy + fc, (x, y, c)).wait_recv()
        vs = vrecv[0]
        for d in range(1, 8):
            vs = vs + vrecv[d]
        vsum_ref[...] = vs
        for k in range(1, 8):
            vcopy(k, my_dev, flip(k)).wait_send()

    return pl.pallas_call(
        body, name="reduce_scatter_grads",
        out_shape=tuple(jax.ShapeDtypeStruct(a.shape[1:], F32) for a in gsh) + (jax.ShapeDtypeStruct((VEC_ROWS, 1024), F32),),
        in_specs=[pl.BlockSpec(memory_space=pl.ANY)] * n + [pl.BlockSpec(memory_space=pltpu.VMEM)],
        out_specs=(pl.BlockSpec(memory_space=pltpu.VMEM),) * (n + 1),
        scratch_shapes=_reduce_scratch(gsh) + [pltpu.VMEM((8, VEC_ROWS, 1024), F32), pltpu.SemaphoreType.DMA((7,)),
                                               pltpu.SemaphoreType.DMA((7,))],
        compiler_params=pltpu.CompilerParams(vmem_limit_bytes=56 * 1024 * 1024),
    )(*gsh, vec)


def _adamw(w, g, m, v):
    rows, cols = w.shape
    tr = rows if rows <= 256 else 256
    flip = cols % LANES != 0

    def body(w_ref, g_ref, m_ref, v_ref, g_out, d_ref, nm_ref, nv_ref):
        gv = g_ref[...].T if flip else g_ref[...]
        outs = (gv,) + _adam_math(w_ref[...], gv, m_ref[...], v_ref[...])
        for ref, val in zip((g_out, d_ref, nm_ref, nv_ref), outs):
            ref[...] = val

    spec = pl.BlockSpec((tr, cols), lambda i: (i, 0))
    tspec = pl.BlockSpec((cols, tr), lambda i: (0, i)) if flip else spec
    shp = jax.ShapeDtypeStruct((cols, rows) if flip else (rows, cols), F32)
    if flip:
        w, m, v = w.T, m.T, v.T
    outs = pl.pallas_call(body, name="adamw", grid=(rows // tr,), out_shape=(shp,) * 4,
                          in_specs=[tspec, spec, tspec, tspec], out_specs=(tspec,) * 4)(w, g, m, v)
    return tuple(o.T for o in outs) if flip else outs


def _adam_math(w, g, m, v):
    m2 = ADAM_B1 * m + (1.0 - ADAM_B1) * g
    v2 = ADAM_B2 * v + (1.0 - ADAM_B2) * (g * g)
    m_hat = m2 / (1.0 - ADAM_B1 ** ADAM_STEP)
    v_hat = v2 / (1.0 - ADAM_B2 ** ADAM_STEP)
    return -ADAM_LR * (m_hat / (jnp.sqrt(v_hat) + ADAM_EPS) + ADAM_WD * w), m2, v2


def _adamw_small(vsum, w, m, v):
    names = [name for name, _, _, _ in _VEC_LAYOUT]
    k = len(names)

    def body(*refs):
        vs_ref, w_refs, m_refs, v_refs = refs[0], refs[1:1 + k], refs[1 + k:1 + 2 * k], refs[1 + 2 * k:1 + 3 * k]
        outs = refs[1 + 3 * k:]
        for idx, (_, r, c0, width) in enumerate(_VEC_LAYOUT):
            gv = vs_ref[pl.ds(r, 1), pl.ds(c0, width)]
            d, m2, v2 = _adam_math(w_refs[idx][...], gv, m_refs[idx][...], v_refs[idx][...])
            outs[idx][...], outs[k + idx][...], outs[2 * k + idx][...], outs[3 * k + idx][...] = gv, d, m2, v2

    shapes = tuple(jax.ShapeDtypeStruct(w[name].shape, F32) for name in names)
    res = pl.pallas_call(
        body, name="adamw_small", out_shape=shapes * 4,
        in_specs=[pl.BlockSpec(memory_space=pltpu.VMEM)] * (1 + 3 * k), out_specs=(pl.BlockSpec(memory_space=pltpu.VMEM),) * (4 * k),
    )(vsum, *[w[name] for name in names], *[m[name] for name in names], *[v[name] for name in names])
    return tuple({name: res[part * k + idx] for idx, name in enumerate(names)} for part in range(4))


_EARLY = ("w_in", "w_uq", "w_ukv")
_LATE = ("w_out", "w_ple", "w_ple_gate")
_BIG = _EARLY + _LATE
_KR_LOCAL = 2432 - 3 * (D_IN // N_SHARD)


def _extend_early(parts):
    cols = lambda a: a.transpose(1, 0, 2).reshape(a.shape[1], N_SHARD * a.shape[2])
    g = parts["w_in"]
    zeros = lambda n: jnp.zeros((D_MODEL, n), g.dtype)
    win_ext = jnp.concatenate([g[0], g[1], g[2], g[3][:, :_KR_LOCAL], zeros(64), g[3][:, _KR_LOCAL:_KR_LOCAL + QK_ROPE],
                               zeros(32), g[3][:, _KR_LOCAL + QK_ROPE:]], axis=1)
    wuq_ext = jnp.pad(cols(parts["w_uq"]).reshape(Q_LORA, 8, 96), ((0, 0), (0, 0), (0, 32))).reshape(Q_LORA, 1024)
    wukv = cols(parts["w_ukv"]).reshape(KV_LORA, 8, 128)
    wk_ext = jnp.pad(wukv[:, :, :64], ((0, 0), (0, 0), (0, 64))).reshape(KV_LORA, 1024)
    wv = wukv[:, :, 64:].reshape(KV_LORA, 512)
    return win_ext, wuq_ext, wk_ext, wv


def _shard_cols(a):
    return a.reshape(a.shape[0], N_SHARD, a.shape[1] // N_SHARD).transpose(1, 0, 2)


def _shard_rows(a):
    return a.reshape(N_SHARD, a.shape[0] // N_SHARD, a.shape[1])


def _shard_early_grads(dwin_ext, dwuq_ext, dwk_ext, dwv):
    e, w = dwin_ext, D_IN // N_SHARD
    last = jnp.concatenate([e[:, 3 * w:2432], e[:, 2496:2528], e[:, 2560:]], axis=1)
    dwuq = dwuq_ext.reshape(Q_LORA, 8, 128)[:, :, :96].reshape(Q_LORA, 768)
    dwukv = jnp.concatenate([dwk_ext.reshape(KV_LORA, 8, 128)[:, :, :64], dwv.reshape(KV_LORA, 8, 64)], axis=2)
    return [jnp.stack([e[:, 0:w], e[:, w:2 * w], e[:, 2 * w:3 * w], last]), _shard_cols(dwuq),
            _shard_cols(dwukv.reshape(KV_LORA, 1024))]


def _rope_tables(positions):
    half = QK_ROPE // 2
    freq = ROPE_THETA ** (-jnp.arange(half, dtype=F32) / half)
    s = positions.shape[0]
    per = LANES // half
    ang = jnp.repeat(positions.astype(F32).reshape(s // per, per), half, axis=1) * jnp.tile(freq, per)
    cos, sin = lax.optimization_barrier((jnp.cos(ang), jnp.sin(ang)))
    cos, sin = cos.reshape(s, half), sin.reshape(s, half)
    z = lambda n: jnp.zeros((s, n), F32)
    c_t = jnp.concatenate([jnp.ones((s, 64), F32), cos, cos, z(32)], axis=1)
    sa_t = jnp.concatenate([z(64), -sin, z(16), z(32)], axis=1)
    sb_t = jnp.concatenate([z(64), z(16), sin, z(32)], axis=1)
    return c_t, sa_t, sb_t


def _local_grads(x, p, positions, tgt, gains, early, late):
    win_ext, wuq_ext, wk_ext, wv = _extend_early(early)
    tabs = _rope_tables(positions)
    g = gains
    sbq, sbk, sbv, sbg, mlag, cq, ckv, qc, kc, mv, sbkt, sbvt, kct, mvt = _pre_fwd(
        x, tabs, g["norm_pre_g"], win_ext, g["q_norm_g"], wuq_ext, g["kv_norm_g"], wk_ext, wv)
    sbo, wout4, wple4, wpg4 = _sb_fwd(sbq, sbk, sbvt, late)
    wout, wpg = wout4.reshape(D_MODEL, D_MODEL), wpg4.reshape(D_MODEL, D_MODEL)
    mlao, lse = _mla_fwd(qc, kc, mvt)
    dsbo, dmlao, delta, dsbg, dmlag, dxres, dwout, dwpg, dwple, vec_c = _post(
        x, p, tgt, sbo, mlao, sbg, mlag, g["sb_out_norm_g"], g["mla_out_norm_g"], wout, g["norm_post_g"], wple4,
        g["ple_norm_g"], wpg, g["b_ple_gate"])
    dsbq, dsbk, dsbv, *late_grads = _sb_bwd(sbq, sbk, sbkt, sbv, dsbo, [_shard_rows(dwout), dwple, _shard_rows(dwpg)])
    dqc, dkc, dmv = _mla_bwd(qc, kc, kct, mv, dmlao, lse, delta)
    gx, dwin_ext, dwuq_ext, dwk_ext, dwv, vec_d = _pre_bwd(
        x, dxres, dsbq, dsbk, dsbv, dsbg, dmlag, dqc, dkc, dmv, cq, ckv, tabs, g["norm_pre_g"], win_ext, g["q_norm_g"],
        wuq_ext, g["kv_norm_g"], wk_ext, wv)
    return gx, _shard_early_grads(dwin_ext, dwuq_ext, dwk_ext, dwv), late_grads, jnp.concatenate([vec_c, vec_d], axis=0)


_VEC_LAYOUT = (("norm_post_g", 0, 0, 1024), ("ple_norm_g", 1, 0, 1024), ("b_ple_gate", 2, 0, 1024), ("sb_out_norm_g", 3, 0, 512),
               ("mla_out_norm_g", 3, 512, 512), ("norm_pre_g", 8, 0, 1024), ("q_norm_g", 9, 0, 256), ("kv_norm_g", 9, 256, 128))
_LOSS_ROW = 4
_WEIGHT_ORDER = ("norm_pre_g", "w_in", "q_norm_g", "w_uq", "kv_norm_g", "w_ukv", "sb_out_norm_g", "mla_out_norm_g", "w_out",
                 "norm_post_g", "w_ple", "ple_norm_g", "w_ple_gate", "b_ple_gate")


def kernel(x, p, positions, norm_pre_g, w_in, q_norm_g, w_uq, kv_norm_g, w_ukv, sb_out_norm_g, mla_out_norm_g, w_out, norm_post_g, w_ple, ple_norm_g, w_ple_gate, b_ple_gate, loss_target, m_norm_pre_g, m_w_in, m_q_norm_g, m_w_uq, m_kv_norm_g, m_w_ukv, m_sb_out_norm_g, m_mla_out_norm_g, m_w_out, m_norm_post_g, m_w_ple, m_ple_norm_g, m_w_ple_gate, m_b_ple_gate, v_norm_pre_g, v_w_in, v_q_norm_g, v_w_uq, v_kv_norm_g, v_w_ukv, v_sb_out_norm_g, v_mla_out_norm_g, v_w_out, v_norm_post_g, v_w_ple, v_ple_norm_g, v_w_ple_gate, v_b_ple_gate):
    w = {"norm_pre_g": norm_pre_g, "w_in": w_in[0], "q_norm_g": q_norm_g, "w_uq": w_uq[0], "kv_norm_g": kv_norm_g, "w_ukv": w_ukv[0],
         "sb_out_norm_g": sb_out_norm_g, "mla_out_norm_g": mla_out_norm_g, "w_out": w_out[0], "norm_post_g": norm_post_g,
         "w_ple": w_ple[0], "ple_norm_g": ple_norm_g, "w_ple_gate": w_ple_gate[0], "b_ple_gate": b_ple_gate}
    m = {"norm_pre_g": m_norm_pre_g, "w_in": m_w_in[0], "q_norm_g": m_q_norm_g, "w_uq": m_w_uq[0], "kv_norm_g": m_kv_norm_g,
         "w_ukv": m_w_ukv[0], "sb_out_norm_g": m_sb_out_norm_g, "mla_out_norm_g": m_mla_out_norm_g, "w_out": m_w_out[0],
         "norm_post_g": m_norm_post_g, "w_ple": m_w_ple[0], "ple_norm_g": m_ple_norm_g, "w_ple_gate": m_w_ple_gate[0],
         "b_ple_gate": m_b_ple_gate}
    v = {"norm_pre_g": v_norm_pre_g, "w_in": v_w_in[0], "q_norm_g": v_q_norm_g, "w_uq": v_w_uq[0], "kv_norm_g": v_kv_norm_g,
         "w_ukv": v_w_ukv[0], "sb_out_norm_g": v_sb_out_norm_g, "mla_out_norm_g": v_mla_out_norm_g, "w_out": v_w_out[0],
         "norm_post_g": v_norm_post_g, "w_ple": v_w_ple[0], "ple_norm_g": v_ple_norm_g, "w_ple_gate": v_w_ple_gate[0],
         "b_ple_gate": v_b_ple_gate}
    gathered = _allgather_weights([w[n] for n in _EARLY])
    gx, early_grads, late_red, vec = _local_grads(x[0], p[0, 0], positions[0], loss_target[0], w, dict(zip(_EARLY, gathered)),
                                                  [w[n] for n in _LATE])
    *early_red, vsum = _reduce_scatter_grads(early_grads, vec)
    gred = early_red + late_red
    loss = vsum[_LOSS_ROW, 0]

    g, delta, new_m, new_v = _adamw_small(vsum, w, m, v)
    for n, gn in zip(_BIG, gred):
        g[n], delta[n], new_m[n], new_v[n] = _adamw(w[n], gn, m[n], v[n])

    lead = lambda n, a: a[None] if n in _BIG else a
    return (loss, gx[None],
            *[lead(n, g[n]) for n in _WEIGHT_ORDER], *[lead(n, delta[n]) for n in _WEIGHT_ORDER],
            *[lead(n, new_m[n]) for n in _WEIGHT_ORDER], *[lead(n, new_v[n]) for n in _WEIGHT_ORDER])
```

```python
import numpy as np
import jax
import jax.numpy as jnp
from jax import lax
from jax.experimental import pallas as pl
from jax.experimental.pallas import tpu as pltpu

F32 = jnp.float32
BF16 = jnp.bfloat16
MESH = pl.DeviceIdType.MESH

D_MODEL = 1024
HEAD_DIM = 64
D_SB = 512
D_MLA = 512
Q_LORA = 256
KV_LORA = 128
QK_NOPE = 64
QK_ROPE = 32
PLE_DIM = 256
D_IN = 2976
D_EXT = 3072
ROPE_THETA = 10000.0
EPS = 1e-6
N_SHARD = 4

ADAM_LR = 0.001
ADAM_B1 = 0.9
ADAM_B2 = 0.999
ADAM_EPS = 1e-08
ADAM_WD = 0.01
ADAM_STEP = 10

LANES = 128
BK = 128
WQ = 256
SB_NSUB = 8
MQ_FWD = 4096
MQ_BWD = 1024
MLA_CW = 512
SB_CUTOFF = 120.0
TM = 256
TM_PRE = 256
VEC_ROWS = 16
VMEM_DENSE = 52 * 1024 * 1024
VMEM_ATTN = 40 * 1024 * 1024


def _mm(a, b):
    return jnp.dot(a, b, preferred_element_type=F32)


def _mm_nt(a, b):
    return lax.dot_general(a, b, (((1,), (1,)), ((), ())), preferred_element_type=F32)


def _mm_tn(a, b):
    return lax.dot_general(a, b, (((0,), (0,)), ((), ())), preferred_element_type=F32)


def _seg(a, bd2):
    return _mm(_split2(a), bd2)


def _const(mask):
    return jnp.asarray(np.asarray(mask, np.float32), dtype=BF16)


def _blockdiag2(n, seg):
    r = (np.arange(2 * n)[:, None] % n) // seg
    c = np.arange(n)[None, :] // seg
    return _const(r == c)


def _sigmoid(a):
    return 1.0 / (1.0 + jnp.exp(-a))


def _rowmean(a):
    return jnp.mean(a, axis=-1, keepdims=True)


def _colsum(a):
    return jnp.sum(a, axis=0, keepdims=True)


def _rope_fwd(a, c, sa, sb):
    w = a.shape[-1]
    return a * c + pltpu.roll(a, w - 16, 1) * sa + pltpu.roll(a, 16, 1) * sb


def _rope_bwd(g, c, sa, sb):
    w = g.shape[-1]
    return g * c + pltpu.roll(g * sa, 16, 1) + pltpu.roll(g * sb, w - 16, 1)


def _full(shape):
    return pl.BlockSpec(shape, lambda *_: (0,) * len(shape))


def _acc(shape):
    return pl.BlockSpec(shape, lambda *_: (0,) * len(shape))


def _full2(shape):
    return pl.BlockSpec(shape, lambda p, i: (0, 0))


def _cols(height, tm=TM):
    return pl.BlockSpec((height, tm), lambda i: (0, i))


def _rows(width, tm=TM):
    return pl.BlockSpec((tm, width), lambda i: (i, 0))


def _pre_fwd(x, tabs, gpre, win, gq, wuq, gkv, wk, wv):
    s = x.shape[0]
    c_t, sa_t, sb_t = tabs
    rw, cl = (lambda width: _rows(width, TM_PRE)), (lambda height: _cols(height, TM_PRE))

    def body(x_ref, c_ref, sa_ref, sb_ref, gpre_ref, win_ref, gq_ref, wuq_ref, gkv_ref, wk_ref, wv_ref,
             sbq_ref, sbk_ref, sbv_ref, sbg_ref, mlag_ref, cq_ref, ckv_ref, qc_ref, kc_ref, mv_ref,
             sbkt_ref, sbvt_ref, kct_ref, mvt_ref):
        xv = x_ref[...]
        r1 = lax.rsqrt(_rowmean(xv * xv) + EPS)
        h = (xv * r1 * gpre_ref[...]).astype(BF16)
        proj = _mm(h, win_ref[...])
        sbq_ref[...] = proj[:, 0:512].astype(BF16)
        sbk_ref[...] = proj[:, 512:1024].astype(BF16)
        sbv_ref[...] = proj[:, 1024:1536].astype(BF16)
        sbkt_ref[...] = proj[:, 512:1024].T.astype(BF16)
        sbvt_ref[...] = proj[:, 1024:1536].T.astype(BF16)
        sbg_ref[...] = proj[:, 1536:2048]
        cq = proj[:, 2048:2304]
        ckv = proj[:, 2304:2432]
        kr = proj[:, 2432:2560]
        mlag_ref[...] = proj[:, 2560:3072]
        cq_ref[...] = cq
        ckv_ref[...] = ckv
        c1, sa1, sb1 = c_ref[...], sa_ref[...], sb_ref[...]
        c8, sa8, sb8 = jnp.tile(c1, (1, 8)), jnp.tile(sa1, (1, 8)), jnp.tile(sb1, (1, 8))
        cqn = (cq * lax.rsqrt(_rowmean(cq * cq) + EPS) * gq_ref[...]).astype(BF16)
        qe = _mm(cqn, wuq_ref[...])
        qc_ref[...] = _rope_fwd(qe, c8, sa8, sb8).astype(BF16)
        ckvn = (ckv * lax.rsqrt(_rowmean(ckv * ckv) + EPS) * gkv_ref[...]).astype(BF16)
        ke = _mm(ckvn, wk_ref[...])
        krr = _rope_fwd(kr, c1, sa1, sb1)
        kcat = ke + jnp.tile(krr, (1, 8))
        kc_ref[...] = kcat.astype(BF16)
        kct_ref[...] = kcat.T.astype(BF16)
        mval = _mm(ckvn, wv_ref[...])
        mv_ref[...] = mval.astype(BF16)
        mvt_ref[...] = mval.T.astype(BF16)

    out_shape = (
        jax.ShapeDtypeStruct((s, 512), BF16), jax.ShapeDtypeStruct((s, 512), BF16), jax.ShapeDtypeStruct((s, 512), BF16),
        jax.ShapeDtypeStruct((s, 512), F32), jax.ShapeDtypeStruct((s, 512), F32),
        jax.ShapeDtypeStruct((s, Q_LORA), F32), jax.ShapeDtypeStruct((s, KV_LORA), F32),
        jax.ShapeDtypeStruct((s, 1024), BF16), jax.ShapeDtypeStruct((s, 1024), BF16), jax.ShapeDtypeStruct((s, 512), BF16),
        jax.ShapeDtypeStruct((512, s), BF16), jax.ShapeDtypeStruct((512, s), BF16), jax.ShapeDtypeStruct((1024, s), BF16),
        jax.ShapeDtypeStruct((512, s), BF16),
    )
    return pl.pallas_call(
        body, name="pre_fwd", grid=(s // TM_PRE,), out_shape=out_shape,
        in_specs=[rw(D_MODEL), rw(LANES), rw(LANES), rw(LANES), _full((1, D_MODEL)), _full((D_MODEL, D_EXT)),
                  _full((1, Q_LORA)), _full((Q_LORA, 1024)), _full((1, KV_LORA)), _full((KV_LORA, 1024)), _full((KV_LORA, 512))],
        out_specs=(rw(512), rw(512), rw(512), rw(512), rw(512), rw(Q_LORA), rw(KV_LORA),
                   rw(1024), rw(1024), rw(512), cl(512), cl(512), cl(1024), cl(512)),
        compiler_params=pltpu.CompilerParams(vmem_limit_bytes=VMEM_DENSE),
    )(x, c_t, sa_t, sb_t, gpre, win, gq, wuq, gkv, wk, wv)


def _softplus(z):
    neg_abs = lax.bitcast_convert_type(lax.bitcast_convert_type(z, jnp.uint32) | jnp.uint32(0x80000000), F32)
    return jnp.maximum(z, 0.0) + jnp.log(1.0 + jnp.exp(neg_abs))


def _sum_matrix(kind, terms):
    r, c = np.arange(2 * BK)[:, None], np.arange(2 * BK * terms)[None, :] % (2 * BK)
    rk, ck = r % BK, c % BK
    return _const(((r // BK) == (c // BK)) & {"suffix": ck >= rk, "prefix": ck <= rk}[kind])


def _split_rows(a):
    hi = a.astype(BF16)
    return jnp.concatenate([hi, (a - hi.astype(F32)).astype(BF16)], axis=0)


def _heads_t(blk, rowi):
    zero = jnp.zeros_like(blk)
    return jnp.concatenate([jnp.where(rowi < 64, blk, zero), jnp.where(rowi >= 64, blk, zero)], axis=1)


def _mask_keys(a, valid, fill=0.0):
    return jnp.concatenate([jnp.where(valid, a[0:BK], fill), jnp.where(valid, a[BK:2 * BK], fill)], axis=0)


def _split2(a):
    hi = a.astype(BF16)
    lo = (a - hi.astype(F32)).astype(BF16)
    return jnp.concatenate([hi, lo], axis=1)


def _pair_stack(b, lane):
    zero = jnp.zeros_like(b)
    return jnp.concatenate([jnp.where(lane < 64, b, zero), jnp.where(lane >= 64, b, zero)], axis=0)


def _sb_fwd(q, k, vt, late):
    s = q.shape[0]
    n = len(late)

    def body(q_ref, k_ref, vt_ref, usuf_ref, *rest):
        ins, o_ref, outs = rest[:n], rest[n], rest[n + 1:2 * n + 1]
        acc_scr, run_scr = rest[2 * n + 1:2 * n + 3]
        bufs, (send_sems, recv_sems, out_sems) = rest[2 * n + 3:3 * n + 3], rest[3 * n + 3:]
        p, i = pl.program_id(0), pl.program_id(1)
        gather_start, gather_forward, gather_finish = _gather_steps([a.shape for a in late], ins, bufs, send_sems, recv_sems)

        @pl.when((p == 0) & (i == 0))
        def _():
            gather_start()

        @pl.when((p == 2) & (i == 0))
        def _():
            gather_forward()

        lane = lax.broadcasted_iota(jnp.int32, (1, LANES), 1)
        rowi = lax.broadcasted_iota(jnp.int32, (LANES, 1), 0)
        keyi = lax.broadcasted_iota(jnp.int32, (BK, WQ), 0)

        def group(i, qs, blocks, masked, seen=None):
            seen = seen or [0] * len(blocks)
            qryi = lax.broadcasted_iota(jnp.int32, (BK, WQ), 1) + i * WQ
            starts = [pl.multiple_of(j * BK, BK) for j in blocks]
            valid = [(keyi[:, lo:] + j * BK) < qryi[:, lo:] if m else None for j, m, lo in zip(blocks, masked, seen)]
            zs = [_mm_nt(_pair_stack(k_ref[pl.ds(ks, BK), :], lane), qs[lo:]) for ks, lo in zip(starts, seen)]
            sps = [_softplus(z) for z in zs]
            sps = [sp if ok is None else _mask_keys(sp, ok) for sp, ok in zip(sps, valid)]
            cums = [_mm(usuf_ref[...], _split_rows(sp)) for sp in sps]
            ws = [jnp.exp(z - c) for z, c in zip(zs, cums)]
            ws = [w if ok is None else _mask_keys(w, ok) for w, ok in zip(ws, valid)]
            pvs = [_mm(_heads_t(vt_ref[:, pl.ds(ks, BK)], rowi), w.astype(BF16)) for ks, w in zip(starts, ws)]
            for pv, c, lo in zip(pvs, cums, seen):
                r0, r1 = run_scr[0:1, lo:], run_scr[1:2, lo:]
                acc_scr[:, lo:] += jnp.where(rowi < 64, jnp.exp(-r0), jnp.exp(-r1)) * pv
                run_scr[0:1, lo:] = r0 + c[0:1]
                run_scr[1:2, lo:] = r1 + c[BK:BK + 1]

        assert WQ == 2 * BK

        def unfinished():
            return (jnp.min(run_scr[0:2, :]) < SB_CUTOFF).astype(jnp.int32)

        def query_block(sub):
            i = pl.program_id(1) * SB_NSUB + sub
            rows = pl.ds(pl.multiple_of(sub * WQ, WQ), WQ)
            qs = q_ref[rows, :] * (HEAD_DIM ** -0.5)
            acc_scr[...] = jnp.zeros_like(acc_scr)
            run_scr[...] = jnp.zeros_like(run_scr)

            @pl.when(i == 0)
            def _():
                group(i, qs, [1, 0], [True, True], [BK, 0])

            @pl.when(i > 0)
            def _():
                group(i, qs, [2 * i + 1, 2 * i, 2 * i - 1, 2 * i - 2], [True, True, False, False], [BK, 0, 0, 0])

            def step(c):
                group(i, qs, [2 * i - 1 - 2 * c[0], 2 * i - 2 - 2 * c[0]], [False, False])
                return c[0] + 1, unfinished()

            lax.while_loop(lambda c: (c[0] < i) & (c[1] > 0), step, (jnp.int32(1), unfinished()))
            o_ref[rows, :] = acc_scr[...].T

        lax.fori_loop(0, SB_NSUB, lambda sub, c: (query_block(sub), c)[1], 0)

        @pl.when((p == pl.num_programs(0) - 1) & (i == pl.num_programs(1) - 1))
        def _():
            gather_finish()
            copies = [pltpu.make_async_copy(bufs[t], outs[t], out_sems.at[t]) for t in range(n)]
            for cp in copies:
                cp.start()
            for cp in copies:
                cp.wait()

    qspec = pl.BlockSpec((SB_NSUB * WQ, LANES), lambda p, i: (i, p))
    kspec = pl.BlockSpec((s, LANES), lambda p, i: (0, p))
    tspec = pl.BlockSpec((LANES, s), lambda p, i: (p, 0))
    gathered = [jax.ShapeDtypeStruct((N_SHARD,) + a.shape, BF16) for a in late]
    return pl.pallas_call(
        body, name="sb_fwd", grid=(4, s // (SB_NSUB * WQ)),
        out_shape=(jax.ShapeDtypeStruct((s, 512), F32), *gathered),
        in_specs=[qspec, kspec, tspec, _full2((2 * BK, 4 * BK))] + [_full2(a.shape) for a in late],
        out_specs=(qspec,) + (pl.BlockSpec(memory_space=pl.ANY),) * n,
        scratch_shapes=[pltpu.VMEM((LANES, WQ), F32), pltpu.VMEM((8, WQ), F32)] + [pltpu.VMEM(g.shape, BF16) for g in gathered]
                       + [pltpu.SemaphoreType.DMA((6 * n,)), pltpu.SemaphoreType.DMA((6 * n,)), pltpu.SemaphoreType.DMA((n,))],
        compiler_params=pltpu.CompilerParams(vmem_limit_bytes=VMEM_ATTN),
    )(q, k, vt, _sum_matrix("suffix", 2), *late)


def _sb_bwd(q, k, kt, v, do, late):
    s = q.shape[0]
    n = len(late)
    halves = [a.shape[1] // 2 for a in late]

    def body(q_ref, k_ref, kt_ref, v_ref, do_ref, usuf_ref, upre_ref, *rest):
        g_refs, (dq_ref, dk_ref, dv_ref), outs = rest[:n], rest[n:n + 3], rest[n + 3:2 * n + 3]
        later_scr, dqt_scr, st_scr = rest[2 * n + 3:2 * n + 6]
        f_scr, reduce_scr, out_sems = rest[2 * n + 6:3 * n + 6], rest[3 * n + 6:-1], rest[-1]
        p, i = pl.program_id(0), pl.program_id(1)
        reduce_load, reduce_partial, reduce_total, reduce_finish = _reduce_steps(halves, g_refs, f_scr, reduce_scr)

        @pl.when((p == 0) & (i == 0))
        def _():
            reduce_load()

        @pl.when((p == 1) & (i == 0))
        def _():
            reduce_partial()

        @pl.when((p == 3) & (i == 0))
        def _():
            reduce_total()

        @pl.when(i == 0)
        def _():
            dk_ref[...] = jnp.zeros_like(dk_ref)
            dv_ref[...] = jnp.zeros_like(dv_ref)

        lane = lax.broadcasted_iota(jnp.int32, (1, LANES), 1)
        rowi = lax.broadcasted_iota(jnp.int32, (LANES, 1), 0)
        keyi = lax.broadcasted_iota(jnp.int32, (BK, WQ), 0)
        assert WQ == 2 * BK

        def query_block(sub):
            i = pl.program_id(1) * SB_NSUB + sub
            rows = pl.ds(pl.multiple_of(sub * WQ, WQ), WQ)
            qryi = lax.broadcasted_iota(jnp.int32, (BK, WQ), 1) + i * WQ
            qs = q_ref[rows, :] * (HEAD_DIM ** -0.5)
            dob = do_ref[rows, :]
            dot = dob.astype(F32).T.astype(BF16)

            def scores(j, lo=0):
                return _mm_nt(_pair_stack(k_ref[pl.ds(pl.multiple_of(j * BK, BK), BK), :], lane), qs[lo:])

            def scan(blocks, masked, seen=None):
                seen = seen or [0] * len(blocks)
                sps = [_softplus(scores(j, lo)) for j, lo in zip(blocks, seen)]
                sps = [_mask_keys(sp, (keyi[:, lo:] + j * BK) < qryi[:, lo:]) if m else sp
                       for sp, j, m, lo in zip(sps, blocks, masked, seen)]
                for sp, j, lo in zip(sps, blocks, seen):
                    run = st_scr[0:2, :]
                    later_scr[j, 0:2, :] = run
                    st_scr[0:2, lo:] = run[:, lo:] + jnp.concatenate([jnp.sum(sp[0:BK], axis=0, keepdims=True),
                                                                      jnp.sum(sp[BK:2 * BK], axis=0, keepdims=True)], axis=0)

            def sweep(blocks, masked, seen=None):
                seen = seen or [0] * len(blocks)
                starts = [pl.multiple_of(j * BK, BK) for j in blocks]
                valid = [(keyi[:, lo:] + j * BK) < qryi[:, lo:] if m else None for j, m, lo in zip(blocks, masked, seen)]
                zs = [scores(j, lo) for j, lo in zip(blocks, seen)]
                us = [jnp.exp(lax.bitcast_convert_type(lax.bitcast_convert_type(z, jnp.uint32) | jnp.uint32(0x80000000), F32))
                      for z in zs]
                sps = [jnp.maximum(z, 0.0) + jnp.log(1.0 + u) for z, u in zip(zs, us)]
                sps = [sp if ok is None else _mask_keys(sp, ok) for sp, ok in zip(sps, valid)]
                sigs = [jnp.where(z >= 0.0, 1.0, u) / (1.0 + u) for z, u in zip(zs, us)]
                cums = [_mm(usuf_ref[...], _split_rows(sp)) for sp in sps]
                dws = [_mm(_pair_stack(v_ref[pl.ds(ks, BK), :], lane), dot[:, lo:]) for ks, lo in zip(starts, seen)]
                wfs = []
                for z, c, j, ok, lo in zip(zs, cums, blocks, valid, seen):
                    f = jnp.exp(-later_scr[j, 0:2, lo:])
                    wide = (BK, WQ - lo)
                    wf = jnp.exp(z - c) * jnp.concatenate([jnp.broadcast_to(f[0:1], wide), jnp.broadcast_to(f[1:2], wide)], axis=0)
                    wfs.append(wf if ok is None else _mask_keys(wf, ok))
                es = [dw * wf for dw, wf in zip(dws, wfs)]
                pres = [_mm(upre_ref[...], e.astype(BF16)) for e in es]
                dzs = []
                for e, pre, sig, ok, lo in zip(es, pres, sigs, valid, seen):
                    e0 = pre[0:BK] + st_scr[0:1, lo:]
                    e1 = pre[BK:2 * BK] + st_scr[1:2, lo:]
                    st_scr[0:1, lo:] = e0[BK - 1:BK]
                    st_scr[1:2, lo:] = e1[BK - 1:BK]
                    dz = e - sig * jnp.concatenate([e0, e1], axis=0)
                    dzs.append((dz if ok is None else _mask_keys(dz, ok)).astype(BF16))
                whole = [b for b, lo in enumerate(seen) if lo == 0]
                dqt_scr[...] += _mm(jnp.concatenate([_heads_t(kt_ref[:, pl.ds(starts[b], BK)], rowi) for b in whole], axis=1),
                                    jnp.concatenate([dzs[b] for b in whole], axis=0))
                for b, lo in enumerate(seen):
                    if lo:
                        dqt_scr[:, lo:] += _mm(_heads_t(kt_ref[:, pl.ds(starts[b], BK)], rowi), dzs[b])
                for ks, dz, wf, lo in zip(starts, dzs, wfs, seen):
                    rk = _mm(dz, qs[lo:])
                    dk_ref[pl.ds(ks, BK), :] += jnp.where(lane < 64, rk[0:BK], rk[BK:2 * BK])
                    rv = _mm(wf.astype(BF16), dob[lo:])
                    dv_ref[pl.ds(ks, BK), :] += jnp.where(lane < 64, rv[0:BK], rv[BK:2 * BK])

            st_scr[...] = jnp.zeros_like(st_scr)

            @pl.when(i == 0)
            def _():
                scan([1, 0], [True, True], [BK, 0])

            @pl.when(i > 0)
            def _():
                scan([2 * i + 1, 2 * i, 2 * i - 1, 2 * i - 2], [True, True, False, False], [BK, 0, 0, 0])

            def unfinished():
                return (jnp.min(st_scr[0:2, :]) < SB_CUTOFF).astype(jnp.int32)

            def step(c):
                scan([2 * i - 1 - 2 * c[0], 2 * i - 2 - 2 * c[0]], [False, False])
                return c[0] + 1, unfinished()

            npairs, _ = lax.while_loop(lambda c: (c[0] < i) & (c[1] > 0), step, (jnp.minimum(i, 1), unfinished()))

            st_scr[...] = jnp.zeros_like(st_scr)
            dqt_scr[...] = jnp.zeros_like(dqt_scr)
            first = 2 * (i - npairs)

            def early(t, carry):
                sweep([first + 2 * t, first + 2 * t + 1], [False, False])
                return carry

            lax.fori_loop(0, npairs - 1, early, 0)

            @pl.when(i == 0)
            def _():
                sweep([0, 1], [True, True], [0, BK])

            @pl.when(i > 0)
            def _():
                sweep([2 * i - 2, 2 * i - 1, 2 * i, 2 * i + 1], [False, False, True, True], [0, 0, 0, BK])

            dq_ref[rows, :] = (dqt_scr[...].T * (HEAD_DIM ** -0.5)).astype(BF16)

        lax.fori_loop(0, SB_NSUB, lambda sub, c: (query_block(sub), c)[1], 0)

        @pl.when((p == pl.num_programs(0) - 1) & (i == pl.num_programs(1) - 1))
        def _():
            reduce_finish()
            copies = [pltpu.make_async_copy(f_scr[t], outs[t], out_sems.at[t]) for t in range(n)]
            for cp in copies:
                cp.start()
            for cp in copies:
                cp.wait()

    qspec = pl.BlockSpec((SB_NSUB * WQ, LANES), lambda p, i: (i, p))
    kspec = pl.BlockSpec((s, LANES), lambda p, i: (0, p))
    tspec = pl.BlockSpec((LANES, s), lambda p, i: (p, 0))
    anywhere = pl.BlockSpec(memory_space=pl.ANY)
    reduced = [jax.ShapeDtypeStruct(a.shape[1:], F32) for a in late]
    return pl.pallas_call(
        body, name="sb_bwd", grid=(4, s // (SB_NSUB * WQ)),
        out_shape=(jax.ShapeDtypeStruct((s, 512), BF16), jax.ShapeDtypeStruct((s, 512), F32),
                   jax.ShapeDtypeStruct((s, 512), F32), *reduced),
        in_specs=[qspec, kspec, tspec, kspec, qspec, _full2((2 * BK, 4 * BK)), _full2((2 * BK, 2 * BK))] + [anywhere] * n,
        out_specs=(qspec, kspec, kspec) + (anywhere,) * n,
        scratch_shapes=[pltpu.VMEM((s // BK, 8, WQ), F32), pltpu.VMEM((LANES, WQ), F32), pltpu.VMEM((8, WQ), F32)]
                       + [pltpu.VMEM(r.shape, F32) for r in reduced] + _reduce_scratch(late) + [pltpu.SemaphoreType.DMA((n,))],
        compiler_params=pltpu.CompilerParams(vmem_limit_bytes=VMEM_ATTN),
    )(q, k, kt, v, do, _sum_matrix("suffix", 2), _sum_matrix("prefix", 1), *late)


MLA_SCALE = (QK_NOPE + QK_ROPE) ** -0.5
LOG2E = 1.4426950408889634


def _mla_keys(kb):
    zero = jnp.zeros((BK, LANES), kb.dtype)
    return jnp.concatenate([jnp.concatenate([kb[:, 0:LANES], zero], axis=1),
                            jnp.concatenate([zero, kb[:, LANES:2 * LANES]], axis=1)], axis=0)


def _mla_fwd(qc, kc, vt):
    s = qc.shape[0]
    mq = min(MQ_FWD, s)
    rows_l = 16

    def body(q_ref, k_ref, vt_ref, o_ref, l_ref, p_scr, ot_scr, st_scr):
        i = pl.program_id(1)
        row = lax.broadcasted_iota(jnp.int32, (LANES, 1), 0)
        orow = lax.broadcasted_iota(jnp.int32, (rows_l, 2 * BK), 0)
        ocol = lax.broadcasted_iota(jnp.int32, (rows_l, 2 * BK), 1)
        ones = jnp.where(((orow == 0) & (ocol < BK)) | ((orow == 1) & (ocol >= BK)), 1.0, 0.0).astype(BF16)

        def chunks(lo, hi):
            return [(a, min(a + MLA_CW, hi)) for a in range(lo, hi, MLA_CW)]

        def keys(j):
            return _mla_keys(k_ref[pl.ds(pl.multiple_of(j * BK, BK), BK), :])

        def values_t(j):
            vtb = vt_ref[:, pl.ds(pl.multiple_of(j * BK, BK), BK)]
            zero = jnp.zeros_like(vtb)
            top = jnp.concatenate([jnp.where(row < 64, vtb, zero), jnp.where(row >= 64, vtb, zero)], axis=1)
            return jnp.concatenate([top, ones], axis=0)

        def pair_values(ja):
            return jnp.concatenate([values_t(ja), values_t(ja + 1)], axis=1)

        def softmax(ja, za, zb, masked, a, b):
            c = MLA_SCALE * LOG2E
            parts = [za[0:BK] * c, za[BK:2 * BK] * c, zb[0:BK] * c, zb[BK:2 * BK] * c]
            if masked:
                keyc = lax.broadcasted_iota(jnp.int32, (BK, b - a), 0)
                qryc = (lax.broadcasted_iota(jnp.int32, (BK, b - a), 1) + (i * mq + a)) // 64
                va = ((keyc + ja * BK) // 64) <= qryc
                vb = ((keyc + (ja + 1) * BK) // 64) <= qryc
                parts = [jnp.where(va, parts[0], -1e30), jnp.where(va, parts[1], -1e30),
                         jnp.where(vb, parts[2], -1e30), jnp.where(vb, parts[3], -1e30)]
            m0, m1 = st_scr[0:1, a:b], st_scr[1:2, a:b]
            n0 = jnp.maximum(m0, jnp.max(jnp.maximum(parts[0], parts[2]), axis=0, keepdims=True))
            n1 = jnp.maximum(m1, jnp.max(jnp.maximum(parts[1], parts[3]), axis=0, keepdims=True))
            st_scr[2:3, a:b] = jnp.exp2(m0 - n0)
            st_scr[3:4, a:b] = jnp.exp2(m1 - n1)
            st_scr[0:1, a:b] = n0
            st_scr[1:2, a:b] = n1
            p_scr[:, a:b] = jnp.concatenate([jnp.exp2(parts[0] - n0), jnp.exp2(parts[1] - n1),
                                             jnp.exp2(parts[2] - n0), jnp.exp2(parts[3] - n1)], axis=0).astype(BF16)

        def accumulate(vals, a, b):
            pv = _mm(vals, p_scr[:, a:b])
            f = jnp.where(row < 64, st_scr[2:3, a:b], st_scr[3:4, a:b])
            ot_scr[0:LANES, a:b] = f * ot_scr[0:LANES, a:b] + pv[0:LANES]
            ot_scr[LANES:LANES + 8, a:b] = st_scr[2:10, a:b] * ot_scr[LANES:LANES + 8, a:b] + pv[LANES:LANES + 8]

        def step(n, diag, lo=0, prev_lo=0):
            kab = jnp.concatenate([keys(2 * n), keys(2 * n + 1)], axis=0)
            vals = pair_values(2 * n - 2)
            for a, b in chunks(prev_lo, lo):
                accumulate(vals, a, b)
            for a, b in chunks(lo, mq):
                zab = _mm_nt(kab, q_ref[a:b, :])
                accumulate(vals, a, b)
                softmax(2 * n, zab[0:2 * BK], zab[2 * BK:4 * BK], diag and a < lo + 2 * BK, a, b)

        def first(diag):
            kab = jnp.concatenate([keys(0), keys(1)], axis=0)
            for a, b in chunks(0, mq):
                zab = _mm_nt(kab, q_ref[a:b, :])
                softmax(0, zab[0:2 * BK], zab[2 * BK:4 * BK], diag and a < 2 * BK, a, b)

        st_scr[...] = jnp.concatenate([jnp.full((2, mq), -1e30, F32), jnp.ones((14, mq), F32)], axis=0)
        ot_scr[...] = jnp.zeros_like(ot_scr)

        npq = mq // (2 * BK)
        seen = lambda d: 2 * BK * max(d, 0)

        @pl.when(i == 0)
        def _():
            first(True)
            for d in range(1, npq):
                step(d, True, seen(d), seen(d - 1))

        if s > mq:
            @pl.when(i > 0)
            def _():
                first(False)
                lax.fori_loop(1, npq * i, lambda n, c: (step(n, False), c)[1], 0)
                for d in range(npq):
                    step(npq * i + d, True, seen(d), seen(d - 1))

        vals = pair_values(2 * (npq * (i + 1) - 1))
        for a, b in chunks(seen(npq - 1), mq):
            accumulate(vals, a, b)
        for a, b in chunks(0, mq):
            l0, l1 = ot_scr[LANES:LANES + 1, a:b], ot_scr[LANES + 1:LANES + 2, a:b]
            o_ref[a:b, :] = (ot_scr[0:LANES, a:b] / jnp.where(row < 64, l0, l1)).T
            l_ref[a:b, :] = jnp.where(row < 64, st_scr[0:1, a:b] + jnp.log2(l0), st_scr[1:2, a:b] + jnp.log2(l1)).T

    qspec = pl.BlockSpec((mq, 2 * LANES), lambda p, i: (i, p))
    kspec = pl.BlockSpec((s, 2 * LANES), lambda p, i: (0, p))
    vtspec = pl.BlockSpec((LANES, s), lambda p, i: (p, 0))
    ospec = pl.BlockSpec((mq, LANES), lambda p, i: (i, p))
    return pl.pallas_call(
        body, name="mla_fwd", grid=(4, s // mq),
        out_shape=(jax.ShapeDtypeStruct((s, 512), F32), jax.ShapeDtypeStruct((s, 512), F32)),
        in_specs=[qspec, kspec, vtspec], out_specs=(ospec, ospec),
        scratch_shapes=[pltpu.VMEM((4 * BK, mq), BF16), pltpu.VMEM((LANES + 8, mq), F32), pltpu.VMEM((16, mq), F32)],
        compiler_params=pltpu.CompilerParams(vmem_limit_bytes=VMEM_ATTN),
    )(qc, kc, vt)


def _mla_bwd(qc, kc, kct, v, do, lse, delta):
    s = qc.shape[0]
    mq = min(MQ_BWD, s)

    def body(q_ref, k_ref, kt_ref, v_ref, do_ref, l_ref, d_ref, dq_ref, dk_ref, dv_ref, dqt_scr, p_scr, dz_scr,
             dvt_scr):
        i = pl.program_id(1)

        @pl.when(i == 0)
        def _():
            dk_ref[...] = jnp.zeros_like(dk_ref)
            dvt_scr[...] = jnp.zeros_like(dvt_scr)

        lane = lax.broadcasted_iota(jnp.int32, (1, LANES), 1)
        keyc = lax.broadcasted_iota(jnp.int32, (BK, mq), 0)
        qryc = (lax.broadcasted_iota(jnp.int32, (BK, mq), 1) + i * mq) // 64
        qw = q_ref[...]
        dob = do_ref[...]
        dost = (dob.astype(F32) * MLA_SCALE).T.astype(BF16)
        dot_ = dob.astype(F32).T.astype(BF16)
        lt = l_ref[...].T
        dt = (d_ref[...] * MLA_SCALE).T
        lse0, lse1 = lt[0:1], lt[64:65]
        dl0, dl1 = dt[0:1], dt[64:65]
        dqt_scr[...] = jnp.zeros_like(dqt_scr)

        def products(j, lo=0):
            ks = pl.multiple_of(j * BK, BK)
            return (_mm_nt(_mla_keys(k_ref[pl.ds(ks, BK), :]), qw[lo:]),
                    _mm(_pair_stack(v_ref[pl.ds(ks, BK), :], lane), dost[:, lo:]))

        def grads(j, slot, zt, dwt, masked, lo=0):
            zt = zt * (MLA_SCALE * LOG2E)
            p0 = jnp.exp2(zt[0:BK] - lse0[:, lo:])
            p1 = jnp.exp2(zt[BK:2 * BK] - lse1[:, lo:])
            if masked:
                valid = ((keyc[:, lo:] + j * BK) // 64) <= qryc[:, lo:]
                p0, p1 = jnp.where(valid, p0, 0.0), jnp.where(valid, p1, 0.0)
            rows = slice(slot * BK, (slot + 1) * BK)
            p_scr[0, rows, lo:] = p0.astype(BF16)
            p_scr[1, rows, lo:] = p1.astype(BF16)
            dz_scr[0, rows, lo:] = (p0 * (dwt[0:BK] - dl0[:, lo:])).astype(BF16)
            dz_scr[1, rows, lo:] = (p1 * (dwt[BK:2 * BK] - dl1[:, lo:])).astype(BF16)

        def scatter(ja, lo=0):
            ks = pl.multiple_of(ja * BK, 2 * BK)
            for h in range(2):
                cols, vrows = slice(h * LANES, (h + 1) * LANES), slice(h * 64, (h + 1) * 64)
                dzh = dz_scr[h, :, lo:]
                dqt_scr[cols, lo:] += _mm(kt_ref[cols, pl.ds(ks, 2 * BK)], dzh)
                dk_ref[pl.ds(ks, 2 * BK), cols] += _mm(dzh, qw[lo:, cols])
                dvt_scr[vrows, pl.ds(ks, 2 * BK)] += _mm_nt(dot_[vrows, lo:], p_scr[h, :, lo:])

        def step(n, masked, lo=0, prev_lo=0):
            za, wa = products(2 * n, lo)
            zb, wb = products(2 * n + 1, lo)
            scatter(2 * n - 2, prev_lo)
            grads(2 * n, 0, za, wa, masked, lo)
            grads(2 * n + 1, 1, zb, wb, masked, lo)

        def first(masked):
            za, wa = products(0)
            zb, wb = products(1)
            grads(0, 0, za, wa, masked)
            grads(1, 1, zb, wb, masked)

        npq = mq // (2 * BK)
        seen = lambda d: 2 * BK * max(d, 0)

        @pl.when(i == 0)
        def _():
            first(True)
            for d in range(1, npq):
                step(d, True, seen(d), seen(d - 1))

        @pl.when(i > 0)
        def _():
            first(False)
            step(1, False)
            lax.fori_loop(1, npq * i // 2, lambda m, c: (step(2 * m, False), step(2 * m + 1, False), c)[2], 0)
            for d in range(npq):
                step(npq * i + d, True, seen(d), seen(d - 1))

        scatter(2 * (npq * (i + 1) - 1), seen(npq - 1))
        dq_ref[...] = dqt_scr[...].T

        @pl.when(i == s // mq - 1)
        def _():
            for a in range(0, s, 512):
                dv_ref[a:a + 512, :] = dvt_scr[:, a:a + 512].T

    qspec = pl.BlockSpec((mq, 2 * LANES), lambda p, i: (i, p))
    kspec = pl.BlockSpec((s, 2 * LANES), lambda p, i: (0, p))
    ktspec = pl.BlockSpec((2 * LANES, s), lambda p, i: (p, 0))
    vspec = pl.BlockSpec((s, LANES), lambda p, i: (0, p))
    ospec = pl.BlockSpec((mq, LANES), lambda p, i: (i, p))
    return pl.pallas_call(
        body, name="mla_bwd", grid=(4, s // mq),
        out_shape=(jax.ShapeDtypeStruct((s, 1024), F32), jax.ShapeDtypeStruct((s, 1024), F32),
                   jax.ShapeDtypeStruct((s, 512), F32)),
        in_specs=[qspec, kspec, ktspec, vspec, ospec, ospec, ospec], out_specs=(qspec, kspec, vspec),
        scratch_shapes=[pltpu.VMEM((2 * LANES, mq), F32), pltpu.VMEM((2, 2 * BK, mq), BF16), pltpu.VMEM((2, 2 * BK, mq), BF16),
                        pltpu.VMEM((LANES, s), F32)],
        compiler_params=pltpu.CompilerParams(vmem_limit_bytes=VMEM_ATTN),
    )(qc, kc, kct, v, do, lse, delta)


def _post(x, p, tgt, sbo, mlao, sbg, mlag, gsb, gmla, wout, gpost, wple, gple, wpg, bpg):
    s = x.shape[0]

    def body(x_ref, p_ref, t_ref, sbo_ref, mlao_ref, sbg_ref, mlag_ref, gsb_ref, gmla_ref, wout_ref,
             gpost_ref, wple_ref, gple_ref, wpg_ref, bpg_ref, bd_ref,
             dsbo_ref, dmlao_ref, delta_ref, dsbg_ref, dmlag_ref, dxres_ref, dwout_ref, dwpg_ref, dwple_ref, vec_ref):
        i = pl.program_id(0)

        @pl.when(i == 0)
        def _():
            dwout_ref[...] = jnp.zeros_like(dwout_ref)
            dwpg_ref[...] = jnp.zeros_like(dwpg_ref)
            dwple_ref[...] = jnp.zeros_like(dwple_ref)
            vec_ref[...] = jnp.zeros_like(vec_ref)

        inv_hd = 1.0 / HEAD_DIM

        def head_fwd(o, g, gate):
            r = lax.rsqrt(_seg(o * o, bd_ref[...]) * inv_hd + EPS)
            hat = o * r
            n = hat * g
            sg = _sigmoid(gate)
            return hat, r, n, sg, n * (gate * sg)

        sbo, mlao, sbg_v, mlag_v = sbo_ref[...], mlao_ref[...], sbg_ref[...], mlag_ref[...]
        gsb_v, gmla_v = gsb_ref[...], gmla_ref[...]
        sb_hat, sb_r, sb_n, sb_sg, sb_y = head_fwd(sbo, gsb_v, sbg_v)
        ml_hat, ml_r, ml_n, ml_sg, ml_y = head_fwd(mlao, gmla_v, mlag_v)
        mix = jnp.concatenate([sb_y, ml_y], axis=1).astype(BF16)
        y = _mm(mix, wout_ref[...])
        ry = lax.rsqrt(_rowmean(y * y) + EPS)
        y_hat = y * ry
        gpost_v = gpost_ref[...]
        x1 = x_ref[...] + y_hat * gpost_v
        pb = p_ref[...].astype(BF16)
        pl_ = jnp.concatenate([_mm(pb, wple_ref[k]) for k in range(N_SHARD)], axis=1)
        rp = lax.rsqrt(_rowmean(pl_ * pl_) + EPS)
        pl_hat = pl_ * rp
        gple_v = gple_ref[...]
        ple = pl_hat * gple_v
        x1b = x1.astype(BF16)
        gate = _sigmoid(_mm(x1b, wpg_ref[...]) + bpg_ref[...])
        err = x1 + ple * gate - t_ref[...]
        loss = 0.5 * jnp.sum(_rowmean(err * err))
        dout = err * (1.0 / D_MODEL)

        du = dout * ple * gate * (1.0 - gate)
        dub = du.astype(BF16)
        dple = dout * gate
        dx1 = dout + _mm_nt(dub, wpg_ref[...])
        dwpg_ref[...] += _mm_tn(x1b, dub)
        dplh = dple * gple_v
        dpl = rp * (dplh - pl_hat * _rowmean(dplh * pl_hat))
        dplb = dpl.astype(BF16)
        wsh = D_MODEL // N_SHARD
        for k in range(N_SHARD):
            dwple_ref[k] += _mm_tn(pb, dplb[:, k * wsh:(k + 1) * wsh])
        dxres_ref[...] = dx1
        dyh = dx1 * gpost_v
        dy = ry * (dyh - y_hat * _rowmean(dyh * y_hat))
        dyb = dy.astype(BF16)
        dwout_ref[...] += _mm_tn(mix, dyb)
        dmix = _mm_nt(dyb, wout_ref[...])

        def head_bwd(dyv, hat, r, n, sg, g, gate):
            dn = dyv * (gate * sg)
            dgate = dyv * n * (sg * (1.0 + gate * (1.0 - sg)))
            dhat = dn * g
            do = r * (dhat - hat * (_seg(dhat * hat, bd_ref[...]) * inv_hd))
            return do, dgate, _colsum(dn * hat)

        dsbo, dsbg, dg_sb = head_bwd(dmix[:, 0:512], sb_hat, sb_r, sb_n, sb_sg, gsb_v, sbg_v)
        dmlao, dmlag, dg_ml = head_bwd(dmix[:, 512:1024], ml_hat, ml_r, ml_n, ml_sg, gmla_v, mlag_v)
        dsbo_ref[...] = dsbo.astype(BF16)
        dmlao_ref[...] = dmlao.astype(BF16)
        delta_ref[...] = _seg(dmlao * mlao, bd_ref[...])
        dsbg_ref[...] = dsbg.astype(BF16)
        dmlag_ref[...] = dmlag.astype(BF16)
        vec_ref[pl.ds(0, 1), :] += _colsum(dx1 * y_hat)
        vec_ref[pl.ds(1, 1), :] += _colsum(dple * pl_hat)
        vec_ref[pl.ds(2, 1), :] += _colsum(du)
        vec_ref[pl.ds(3, 1), :] += jnp.concatenate([dg_sb, dg_ml], axis=1)
        vec_ref[pl.ds(4, 1), :] += jnp.full((1, D_MODEL), loss, F32)

    out_shape = (
        jax.ShapeDtypeStruct((s, 512), BF16), jax.ShapeDtypeStruct((s, 512), BF16), jax.ShapeDtypeStruct((s, 512), F32),
        jax.ShapeDtypeStruct((s, 512), BF16), jax.ShapeDtypeStruct((s, 512), BF16), jax.ShapeDtypeStruct((s, D_MODEL), F32),
        jax.ShapeDtypeStruct((D_MODEL, D_MODEL), F32), jax.ShapeDtypeStruct((D_MODEL, D_MODEL), F32),
        jax.ShapeDtypeStruct(wple.shape, F32), jax.ShapeDtypeStruct((8, D_MODEL), F32),
    )
    return pl.pallas_call(
        body, name="post_fwd_bwd", grid=(s // TM,), out_shape=out_shape,
        in_specs=[_rows(D_MODEL), _rows(PLE_DIM), _rows(D_MODEL), _rows(512), _rows(512), _rows(512), _rows(512),
                  _full((1, 512)), _full((1, 512)), _full((D_MODEL, D_MODEL)),
                  _full((1, D_MODEL)), _full(wple.shape), _full((1, D_MODEL)), _full((D_MODEL, D_MODEL)),
                  _full((1, D_MODEL)), _full((1024, 512))],
        out_specs=(_rows(512), _rows(512), _rows(512), _rows(512), _rows(512), _rows(D_MODEL),
                   _acc((D_MODEL, D_MODEL)), _acc((D_MODEL, D_MODEL)), _acc(wple.shape), _acc((8, D_MODEL))),
        compiler_params=pltpu.CompilerParams(vmem_limit_bytes=VMEM_DENSE),
    )(x, p, tgt, sbo, mlao, sbg, mlag, gsb, gmla, wout, gpost, wple, gple, wpg, bpg, _blockdiag2(512, HEAD_DIM))


def _pre_bwd(x, dxres, dsbq, dsbk, dsbv, dsbg, dmlag, dqc, dkc, dmv, cq, ckv, tabs, gpre, win, gq, wuq, gkv, wk, wv):
    s = x.shape[0]
    c_t, sa_t, sb_t = tabs
    rw = _rows

    def body(x_ref, dxres_ref, dsbq_ref, dsbk_ref, dsbv_ref, dsbg_ref, dmlag_ref, dqc_ref, dkc_ref, dmv_ref, cq_ref,
             ckv_ref, c_ref, sa_ref, sb_ref, gpre_ref, win_ref, gq_ref, wuq_ref, gkv_ref, wk_ref, wv_ref,
             gx_ref, dwin_ref, dwuq_ref, dwk_ref, dwv_ref, vec_ref, dwin_acc):
        i = pl.program_id(0)

        @pl.when(i == 0)
        def _():
            dwin_acc[...] = jnp.zeros_like(dwin_acc)
            dwuq_ref[...] = jnp.zeros_like(dwuq_ref)
            dwk_ref[...] = jnp.zeros_like(dwk_ref)
            dwv_ref[...] = jnp.zeros_like(dwv_ref)
            vec_ref[...] = jnp.zeros_like(vec_ref)

        lane = lax.broadcasted_iota(jnp.int32, (1, LANES), 1)
        c1, sa1, sb1 = c_ref[...], sa_ref[...], sb_ref[...]
        c8, sa8, sb8 = jnp.tile(c1, (1, 8)), jnp.tile(sa1, (1, 8)), jnp.tile(sb1, (1, 8))

        def norm_bwd(dn, hat, r, g):
            t = dn * g
            return r * (t - hat * _rowmean(t * hat)), _colsum(dn * hat)

        xv = x_ref[...]
        r1 = lax.rsqrt(_rowmean(xv * xv) + EPS)
        x_hat = xv * r1
        gpre_v = gpre_ref[...]
        hb = (x_hat * gpre_v).astype(BF16)
        ready = jnp.concatenate([dsbq_ref[...], dsbk_ref[...].astype(BF16), dsbv_ref[...].astype(BF16), dsbg_ref[...]], axis=1)
        dmlag = dmlag_ref[...]
        dwin_acc[:, 0:2048] += _mm_tn(hb, ready)
        dwin_acc[:, 2560:3072] += _mm_tn(hb, dmlag)
        dh = _mm_nt(ready, win_ref[:, 0:2048]) + _mm_nt(dmlag, win_ref[:, 2560:3072])

        dqeb = _rope_bwd(dqc_ref[...], c8, sa8, sb8).astype(BF16)
        cq = cq_ref[...]
        rq = lax.rsqrt(_rowmean(cq * cq) + EPS)
        cq_hat = cq * rq
        gq_v = gq_ref[...]
        dwuq_ref[...] += _mm_tn((cq_hat * gq_v).astype(BF16), dqeb)
        dcq, dg_q = norm_bwd(_mm_nt(dqeb, wuq_ref[...]), cq_hat, rq, gq_v)

        dkc = dkc_ref[...]
        dkcb = dkc.astype(BF16)
        dmvb = dmv_ref[...].astype(BF16)
        ckv = ckv_ref[...]
        rkv = lax.rsqrt(_rowmean(ckv * ckv) + EPS)
        ckv_hat = ckv * rkv
        gkv_v = gkv_ref[...]
        ckvnb = (ckv_hat * gkv_v).astype(BF16)
        dwk_ref[...] += _mm_tn(ckvnb, dkcb)
        dwv_ref[...] += _mm_tn(ckvnb, dmvb)
        dckv, dg_kv = norm_bwd(_mm_nt(dkcb, wk_ref[...]) + _mm_nt(dmvb, wv_ref[...]), ckv_hat, rkv, gkv_v)

        dkr = dkc[:, 0:LANES]
        for hh in range(1, 8):
            dkr = dkr + dkc[:, LANES * hh:LANES * (hh + 1)]
        dkr = _rope_bwd(dkr, c1, sa1, sb1)
        dkr = jnp.where((lane >= 64) & (lane < 96), dkr, 0.0)

        late = jnp.concatenate([dcq.astype(BF16), dckv.astype(BF16), dkr.astype(BF16)], axis=1)
        dwin_acc[:, 2048:2560] += _mm_tn(hb, late)
        dx, dg_pre = norm_bwd(dh + _mm_nt(late, win_ref[:, 2048:2560]), x_hat, r1, gpre_v)
        gx_ref[...] = dxres_ref[...] + dx
        vec_ref[pl.ds(0, 1), :] += dg_pre
        vec_ref[pl.ds(1, 1), :] += jnp.concatenate([dg_q, dg_kv, jnp.zeros((1, D_MODEL - Q_LORA - KV_LORA), F32)], axis=1)

        @pl.when(i == pl.num_programs(0) - 1)
        def _():
            pltpu.sync_copy(dwin_acc, dwin_ref)

    out_shape = (
        jax.ShapeDtypeStruct((s, D_MODEL), F32), jax.ShapeDtypeStruct((D_MODEL, D_EXT), F32),
        jax.ShapeDtypeStruct((Q_LORA, 1024), F32), jax.ShapeDtypeStruct((KV_LORA, 1024), F32),
        jax.ShapeDtypeStruct((KV_LORA, 512), F32), jax.ShapeDtypeStruct((8, D_MODEL), F32),
    )
    return pl.pallas_call(
        body, name="pre_bwd", grid=(s // TM,), out_shape=out_shape,
        in_specs=[rw(D_MODEL), rw(D_MODEL), rw(512), rw(512), rw(512), rw(512), rw(512),
                  rw(1024), rw(1024), rw(512), rw(Q_LORA), rw(KV_LORA), rw(LANES), rw(LANES),
                  rw(LANES), _full((1, D_MODEL)), _full((D_MODEL, D_EXT)), _full((1, Q_LORA)), _full((Q_LORA, 1024)),
                  _full((1, KV_LORA)), _full((KV_LORA, 1024)), _full((KV_LORA, 512))],
        out_specs=(rw(D_MODEL), pl.BlockSpec(memory_space=pl.ANY), _acc((Q_LORA, 1024)), _acc((KV_LORA, 1024)),
                   _acc((KV_LORA, 512)), _acc((8, D_MODEL))),
        scratch_shapes=[pltpu.VMEM((D_MODEL, D_EXT), F32)],
        compiler_params=pltpu.CompilerParams(vmem_limit_bytes=VMEM_DENSE),
    )(x, dxres, dsbq, dsbk, dsbv, dsbg, dmlag, dqc, dkc, dmv, cq, ckv, c_t, sa_t, sb_t, gpre, win, gq, wuq, gkv, wk, wv)


def _place():
    return lax.axis_index("x"), lax.axis_index("y"), lax.axis_index("c")


def _gather_steps(shapes, ins, bufs, send_sems, recv_sems):
    n = len(shapes)
    x, y, c = _place()
    me, sib = (x, y, c), (x, y, 1 - c)
    chips = [(1 - x, y), (x, 1 - y), (1 - x, 1 - y)]

    def half(t, chip, hc):
        rows = shapes[t][0] // 2
        return bufs[t].at[2 * chip[0] + chip[1], pl.ds(pl.multiple_of(hc * rows, 16), rows), :]

    def copy(k, t, chip, hc, to):
        return pltpu.make_async_remote_copy(src_ref=half(t, chip, hc), dst_ref=half(t, chip, hc), send_sem=send_sems.at[k],
                                            recv_sem=recv_sems.at[k], device_id=to, device_id_type=MESH)

    def start():
        for t in range(n):
            bufs[t][2 * x + y] = ins[t][...].astype(BF16)
            for j, chip in enumerate(chips):
                copy(6 * t + j, t, (x, y), c, (*chip, c)).start()

    def forward():
        for t in range(n):
            for j, chip in enumerate(chips):
                copy(6 * t + j, t, chip, c, me).wait_recv()
                copy(6 * t + 3 + j, t, chip, c, sib).start()

    def finish():
        for t in range(n):
            for j, chip in enumerate(chips):
                copy(6 * t + 3 + j, t, chip, 1 - c, me).wait_recv()
        for t in range(n):
            for j, chip in enumerate(chips):
                copy(6 * t + j, t, (x, y), c, (*chip, c)).wait_send()
                copy(6 * t + 3 + j, t, chip, c, sib).wait_send()

    return start, forward, finish


def _allgather_weights(shards):
    n = len(shards)

    def body(*refs):
        start, forward, finish = _gather_steps([a.shape for a in shards], refs[:n], refs[n:2 * n], refs[2 * n], refs[2 * n + 1])
        start()
        forward()
        finish()

    return pl.pallas_call(
        body, name="allgather_weights",
        out_shape=tuple(jax.ShapeDtypeStruct((N_SHARD,) + a.shape, BF16) for a in shards),
        in_specs=[pl.BlockSpec(memory_space=pltpu.VMEM)] * n, out_specs=(pl.BlockSpec(memory_space=pltpu.VMEM),) * n,
        scratch_shapes=[pltpu.SemaphoreType.DMA((6 * n,)), pltpu.SemaphoreType.DMA((6 * n,))],
        compiler_params=pltpu.CompilerParams(vmem_limit_bytes=VMEM_ATTN),
    )(*shards)


def _reduce_scratch(gsh):
    n = len(gsh)
    half_shapes = [(N_SHARD, a.shape[1] // 2, a.shape[2]) for a in gsh]
    return ([pltpu.VMEM(s_, F32) for s_ in half_shapes] * 2 + [pltpu.VMEM(s_, BF16) for s_ in half_shapes] * 2
            + [pltpu.SemaphoreType.DMA((n,)), pltpu.SemaphoreType.DMA((5 * n,)), pltpu.SemaphoreType.DMA((5 * n,))])


def _reduce_steps(halves, g_refs, f_refs, scratch):
    n = len(halves)
    accs, sibs, sbufs, rbufs = scratch[0:n], scratch[n:2 * n], scratch[2 * n:3 * n], scratch[3 * n:4 * n]
    local_sems, send_sems, recv_sems = scratch[4 * n:4 * n + 3]
    x, y, c = _place()
    me, sib = (x, y, c), (x, y, 1 - c)
    mine = 2 * x + y
    chips = [(1 - x, y), (x, 1 - y), (1 - x, 1 - y)]

    def remote(k, src, dst, to):
        return pltpu.make_async_remote_copy(src_ref=src, dst_ref=dst, send_sem=send_sems.at[k], recv_sem=recv_sems.at[k],
                                            device_id=to, device_id_type=MESH)

    def half3(ref, t, hc):
        return ref.at[:, pl.ds(pl.multiple_of(hc * halves[t], 8), halves[t]), :]

    def half2(ref, t, hc):
        return ref.at[pl.ds(pl.multiple_of(hc * halves[t], 8), halves[t]), :]

    def mine_load(t):
        return pltpu.make_async_copy(half3(g_refs[t], t, c), accs[t], local_sems.at[t])

    def to_sibling(t, to):
        return remote(t, half3(g_refs[t], t, 1 - c), sibs[t], to)

    def to_chip(t, j, chip, to):
        idx = 2 * chip[0] + chip[1]
        return remote(n + 3 * t + j, sbufs[t].at[idx], rbufs[t].at[mine if to is not me else idx], to)

    def swap(t, hc, to):
        return remote(4 * n + t, half2(f_refs[t], t, hc), half2(f_refs[t], t, hc), to)

    def load():
        for t in range(n):
            mine_load(t).start()
            to_sibling(t, sib).start()

    def partial():
        for t in range(n):
            mine_load(t).wait()
            to_sibling(t, me).wait_recv()
            for k in range(N_SHARD):
                accs[t][k] = accs[t][k] + sibs[t][k]
            for j, chip in enumerate(chips):
                idx = 2 * chip[0] + chip[1]
                sbufs[t][idx] = accs[t][idx].astype(BF16)
                to_chip(t, j, chip, (*chip, c)).start()

    def total():
        for t in range(n):
            acc = accs[t][mine]
            for j, chip in enumerate(chips):
                to_chip(t, j, chip, me).wait_recv()
                acc = acc + rbufs[t][2 * chip[0] + chip[1]].astype(F32)
            half2(f_refs[t], t, c)[...] = acc
            swap(t, c, sib).start()

    def finish():
        for t in range(n):
            swap(t, 1 - c, me).wait_recv()
        for t in range(n):
            to_sibling(t, sib).wait_send()
            for j, chip in enumerate(chips):
                to_chip(t, j, chip, (*chip, c)).wait_send()
            swap(t, c, sib).wait_send()

    return load, partial, total, finish


def _reduce_scatter_grads(gsh, vec):
    n = len(gsh)
    halves = [a.shape[1] // 2 for a in gsh]

    def body(*refs):
        g_refs, vec_ref, f_refs, vsum_ref = refs[:n], refs[n], refs[n + 1:2 * n + 1], refs[2 * n + 1]
        scratch = refs[2 * n + 2:]
        vrecv, vsend_sems, vrecv_sems = scratch[4 * n + 3:]
        load, partial, total, finish = _reduce_steps(halves, g_refs, f_refs, scratch)
        x, y, c = _place()
        my_dev = 4 * x + 2 * y + c

        def flip(k):
            return x ^ ((k >> 2) & 1), y ^ ((k >> 1) & 1), c ^ (k & 1)

        def vcopy(k, slot, to):
            return pltpu.make_async_remote_copy(src_ref=vec_ref, dst_ref=vrecv.at[slot], send_sem=vsend_sems.at[k - 1],
                                                recv_sem=vrecv_sems.at[k - 1], device_id=to, device_id_type=MESH)

        load()
        vrecv[my_dev] = vec_ref[...]
        for k in range(1, 8):
            vcopy(k, my_dev, flip(k)).start()
        partial()
        total()
        finish()
        for k in range(1, 8):
            fx, fy, fc = flip(k)
            vcopy(k, 4 * fx + 2 * fy + fc, (x, y, c)).wait_recv()
        vs = vrecv[0]
        for d in range(1, 8):
            vs = vs + vrecv[d]
        vsum_ref[...] = vs
        for k in range(1, 8):
            vcopy(k, my_dev, flip(k)).wait_send()

    return pl.pallas_call(
        body, name="reduce_scatter_grads",
        out_shape=tuple(jax.ShapeDtypeStruct(a.shape[1:], F32) for a in gsh) + (jax.ShapeDtypeStruct((VEC_ROWS, 1024), F32),),
        in_specs=[pl.BlockSpec(memory_space=pl.ANY)] * n + [pl.BlockSpec(memory_space=pltpu.VMEM)],
        out_specs=(pl.BlockSpec(memory_space=pltpu.VMEM),) * (n + 1),
        scratch_shapes=_reduce_scratch(gsh) + [pltpu.VMEM((8, VEC_ROWS, 1024), F32), pltpu.SemaphoreType.DMA((7,)),
                                               pltpu.SemaphoreType.DMA((7,))],
        compiler_params=pltpu.CompilerParams(vmem_limit_bytes=56 * 1024 * 1024),
    )(*gsh, vec)


def _adamw(w, g, m, v):
    rows, cols = w.shape
    tr = rows if rows <= 256 else 256
    flip = cols % LANES != 0

    def body(w_ref, g_ref, m_ref, v_ref, g_out, d_ref, nm_ref, nv_ref):
        gv = g_ref[...].T if flip else g_ref[...]
        outs = (gv,) + _adam_math(w_ref[...], gv, m_ref[...], v_ref[...])
        for ref, val in zip((g_out, d_ref, nm_ref, nv_ref), outs):
            ref[...] = val

    spec = pl.BlockSpec((tr, cols), lambda i: (i, 0))
    tspec = pl.BlockSpec((cols, tr), lambda i: (0, i)) if flip else spec
    shp = jax.ShapeDtypeStruct((cols, rows) if flip else (rows, cols), F32)
    if flip:
        w, m, v = w.T, m.T, v.T
    outs = pl.pallas_call(body, name="adamw", grid=(rows // tr,), out_shape=(shp,) * 4,
                          in_specs=[tspec, spec, tspec, tspec], out_specs=(tspec,) * 4)(w, g, m, v)
    return tuple(o.T for o in outs) if flip else outs


def _adam_math(w, g, m, v):
    m2 = ADAM_B1 * m + (1.0 - ADAM_B1) * g
    v2 = ADAM_B2 * v + (1.0 - ADAM_B2) * (g * g)
    m_hat = m2 / (1.0 - ADAM_B1 ** ADAM_STEP)
    v_hat = v2 / (1.0 - ADAM_B2 ** ADAM_STEP)
    return -ADAM_LR * (m_hat / (jnp.sqrt(v_hat) + ADAM_EPS) + ADAM_WD * w), m2, v2


def _adamw_small(vsum, w, m, v):
    names = [name for name, _, _, _ in _VEC_LAYOUT]
    k = len(names)

    def body(*refs):
        vs_ref, w_refs, m_refs, v_refs = refs[0], refs[1:1 + k], refs[1 + k:1 + 2 * k], refs[1 + 2 * k:1 + 3 * k]
        outs = refs[1 + 3 * k:]
        for idx, (_, r, c0, width) in enumerate(_VEC_LAYOUT):
            gv = vs_ref[pl.ds(r, 1), pl.ds(c0, width)]
            d, m2, v2 = _adam_math(w_refs[idx][...], gv, m_refs[idx][...], v_refs[idx][...])
            outs[idx][...], outs[k + idx][...], outs[2 * k + idx][...], outs[3 * k + idx][...] = gv, d, m2, v2

    shapes = tuple(jax.ShapeDtypeStruct(w[name].shape, F32) for name in names)
    res = pl.pallas_call(
        body, name="adamw_small", out_shape=shapes * 4,
        in_specs=[pl.BlockSpec(memory_space=pltpu.VMEM)] * (1 + 3 * k), out_specs=(pl.BlockSpec(memory_space=pltpu.VMEM),) * (4 * k),
    )(vsum, *[w[name] for name in names], *[m[name] for name in names], *[v[name] for name in names])
    return tuple({name: res[part * k + idx] for idx, name in enumerate(names)} for part in range(4))


_EARLY = ("w_in", "w_uq", "w_ukv")
_LATE = ("w_out", "w_ple", "w_ple_gate")
_BIG = _EARLY + _LATE
_KR_LOCAL = 2432 - 3 * (D_IN // N_SHARD)


def _extend_early(parts):
    cols = lambda a: a.transpose(1, 0, 2).reshape(a.shape[1], N_SHARD * a.shape[2])
    g = parts["w_in"]
    zeros = lambda n: jnp.zeros((D_MODEL, n), g.dtype)
    win_ext = jnp.concatenate([g[0], g[1], g[2], g[3][:, :_KR_LOCAL], zeros(64), g[3][:, _KR_LOCAL:_KR_LOCAL + QK_ROPE],
                               zeros(32), g[3][:, _KR_LOCAL + QK_ROPE:]], axis=1)
    wuq_ext = jnp.pad(cols(parts["w_uq"]).reshape(Q_LORA, 8, 96), ((0, 0), (0, 0), (0, 32))).reshape(Q_LORA, 1024)
    wukv = cols(parts["w_ukv"]).reshape(KV_LORA, 8, 128)
    wk_ext = jnp.pad(wukv[:, :, :64], ((0, 0), (0, 0), (0, 64))).reshape(KV_LORA, 1024)
    wv = wukv[:, :, 64:].reshape(KV_LORA, 512)
    return win_ext, wuq_ext, wk_ext, wv


def _shard_cols(a):
    return a.reshape(a.shape[0], N_SHARD, a.shape[1] // N_SHARD).transpose(1, 0, 2)


def _shard_rows(a):
    return a.reshape(N_SHARD, a.shape[0] // N_SHARD, a.shape[1])


def _shard_early_grads(dwin_ext, dwuq_ext, dwk_ext, dwv):
    e, w = dwin_ext, D_IN // N_SHARD
    last = jnp.concatenate([e[:, 3 * w:2432], e[:, 2496:2528], e[:, 2560:]], axis=1)
    dwuq = dwuq_ext.reshape(Q_LORA, 8, 128)[:, :, :96].reshape(Q_LORA, 768)
    dwukv = jnp.concatenate([dwk_ext.reshape(KV_LORA, 8, 128)[:, :, :64], dwv.reshape(KV_LORA, 8, 64)], axis=2)
    return [jnp.stack([e[:, 0:w], e[:, w:2 * w], e[:, 2 * w:3 * w], last]), _shard_cols(dwuq),
            _shard_cols(dwukv.reshape(KV_LORA, 1024))]


def _rope_tables(positions):
    half = QK_ROPE // 2
    freq = ROPE_THETA ** (-jnp.arange(half, dtype=F32) / half)
    s = positions.shape[0]
    per = LANES // half
    ang = jnp.repeat(positions.astype(F32).reshape(s // per, per), half, axis=1) * jnp.tile(freq, per)
    cos, sin = lax.optimization_barrier((jnp.cos(ang), jnp.sin(ang)))
    cos, sin = cos.reshape(s, half), sin.reshape(s, half)
    z = lambda n: jnp.zeros((s, n), F32)
    c_t = jnp.concatenate([jnp.ones((s, 64), F32), cos, cos, z(32)], axis=1)
    sa_t = jnp.concatenate([z(64), -sin, z(16), z(32)], axis=1)
    sb_t = jnp.concatenate([z(64), z(16), sin, z(32)], axis=1)
    return c_t, sa_t, sb_t


def _local_grads(x, p, positions, tgt, gains, early, late):
    win_ext, wuq_ext, wk_ext, wv = _extend_early(early)
    tabs = _rope_tables(positions)
    g = gains
    sbq, sbk, sbv, sbg, mlag, cq, ckv, qc, kc, mv, sbkt, sbvt, kct, mvt = _pre_fwd(
        x, tabs, g["norm_pre_g"], win_ext, g["q_norm_g"], wuq_ext, g["kv_norm_g"], wk_ext, wv)
    sbo, wout4, wple4, wpg4 = _sb_fwd(sbq, sbk, sbvt, late)
    wout, wpg = wout4.reshape(D_MODEL, D_MODEL), wpg4.reshape(D_MODEL, D_MODEL)
    mlao, lse = _mla_fwd(qc, kc, mvt)
    dsbo, dmlao, delta, dsbg, dmlag, dxres, dwout, dwpg, dwple, vec_c = _post(
        x, p, tgt, sbo, mlao, sbg, mlag, g["sb_out_norm_g"], g["mla_out_norm_g"], wout, g["norm_post_g"], wple4,
        g["ple_norm_g"], wpg, g["b_ple_gate"])
    dsbq, dsbk, dsbv, *late_grads = _sb_bwd(sbq, sbk, sbkt, sbv, dsbo, [_shard_rows(dwout), dwple, _shard_rows(dwpg)])
    dqc, dkc, dmv = _mla_bwd(qc, kc, kct, mv, dmlao, lse, delta)
    gx, dwin_ext, dwuq_ext, dwk_ext, dwv, vec_d = _pre_bwd(
        x, dxres, dsbq, dsbk, dsbv, dsbg, dmlag, dqc, dkc, dmv, cq, ckv, tabs, g["norm_pre_g"], win_ext, g["q_norm_g"],
        wuq_ext, g["kv_norm_g"], wk_ext, wv)
    return gx, _shard_early_grads(dwin_ext, dwuq_ext, dwk_ext, dwv), late_grads, jnp.concatenate([vec_c, vec_d], axis=0)


_VEC_LAYOUT = (("norm_post_g", 0, 0, 1024), ("ple_norm_g", 1, 0, 1024), ("b_ple_gate", 2, 0, 1024), ("sb_out_norm_g", 3, 0, 512),
               ("mla_out_norm_g", 3, 512, 512), ("norm_pre_g", 8, 0, 1024), ("q_norm_g", 9, 0, 256), ("kv_norm_g", 9, 256, 128))
_LOSS_ROW = 4
_WEIGHT_ORDER = ("norm_pre_g", "w_in", "q_norm_g", "w_uq", "kv_norm_g", "w_ukv", "sb_out_norm_g", "mla_out_norm_g", "w_out",
                 "norm_post_g", "w_ple", "ple_norm_g", "w_ple_gate", "b_ple_gate")


def kernel(x, p, positions, norm_pre_g, w_in, q_norm_g, w_uq, kv_norm_g, w_ukv, sb_out_norm_g, mla_out_norm_g, w_out, norm_post_g, w_ple, ple_norm_g, w_ple_gate, b_ple_gate, loss_target, m_norm_pre_g, m_w_in, m_q_norm_g, m_w_uq, m_kv_norm_g, m_w_ukv, m_sb_out_norm_g, m_mla_out_norm_g, m_w_out, m_norm_post_g, m_w_ple, m_ple_norm_g, m_w_ple_gate, m_b_ple_gate, v_norm_pre_g, v_w_in, v_q_norm_g, v_w_uq, v_kv_norm_g, v_w_ukv, v_sb_out_norm_g, v_mla_out_norm_g, v_w_out, v_norm_post_g, v_w_ple, v_ple_norm_g, v_w_ple_gate, v_b_ple_gate):
    w = {"norm_pre_g": norm_pre_g, "w_in": w_in[0], "q_norm_g": q_norm_g, "w_uq": w_uq[0], "kv_norm_g": kv_norm_g, "w_ukv": w_ukv[0],
         "sb_out_norm_g": sb_out_norm_g, "mla_out_norm_g": mla_out_norm_g, "w_out": w_out[0], "norm_post_g": norm_post_g,
         "w_ple": w_ple[0], "ple_norm_g": ple_norm_g, "w_ple_gate": w_ple_gate[0], "b_ple_gate": b_ple_gate}
    m = {"norm_pre_g": m_norm_pre_g, "w_in": m_w_in[0], "q_norm_g": m_q_norm_g, "w_uq": m_w_uq[0], "kv_norm_g": m_kv_norm_g,
         "w_ukv": m_w_ukv[0], "sb_out_norm_g": m_sb_out_norm_g, "mla_out_norm_g": m_mla_out_norm_g, "w_out": m_w_out[0],
         "norm_post_g": m_norm_post_g, "w_ple": m_w_ple[0], "ple_norm_g": m_ple_norm_g, "w_ple_gate": m_w_ple_gate[0],
         "b_ple_gate": m_b_ple_gate}
    v = {"norm_pre_g": v_norm_pre_g, "w_in": v_w_in[0], "q_norm_g": v_q_norm_g, "w_uq": v_w_uq[0], "kv_norm_g": v_kv_norm_g,
         "w_ukv": v_w_ukv[0], "sb_out_norm_g": v_sb_out_norm_g, "mla_out_norm_g": v_mla_out_norm_g, "w_out": v_w_out[0],
         "norm_post_g": v_norm_post_g, "w_ple": v_w_ple[0], "ple_norm_g": v_ple_norm_g, "w_ple_gate": v_w_ple_gate[0],
         "b_ple_gate": v_b_ple_gate}
    gathered = _allgather_weights([w[n] for n in _EARLY])
    gx, early_grads, late_red, vec = _local_grads(x[0], p[0, 0], positions[0], loss_target[0], w, dict(zip(_EARLY, gathered)),
                                                  [w[n] for n in _LATE])
    *early_red, vsum = _reduce_scatter_grads(early_grads, vec)
    gred = early_red + late_red
    loss = vsum[_LOSS_ROW, 0]

    g, delta, new_m, new_v = _adamw_small(vsum, w, m, v)
    for n, gn in zip(_BIG, gred):
        g[n], delta[n], new_m[n], new_v[n] = _adamw(w[n], gn, m[n], v[n])

    lead = lambda n, a: a[None] if n in _BIG else a
    return (loss, gx[None],
            *[lead(n, g[n]) for n in _WEIGHT_ORDER], *[lead(n, delta[n]) for n in _WEIGHT_ORDER],
            *[lead(n, new_m[n]) for n in _WEIGHT_ORDER], *[lead(n, new_v[n]) for n in _WEIGHT_ORDER])
```

```python
import numpy as np
import jax
import jax.numpy as jnp
from jax import lax
from jax.experimental import pallas as pl
from jax.experimental.pallas import tpu as pltpu

F32 = jnp.float32
BF16 = jnp.bfloat16
MESH = pl.DeviceIdType.MESH

D_MODEL = 1024
HEAD_DIM = 64
D_SB = 512
D_MLA = 512
Q_LORA = 256
KV_LORA = 128
QK_NOPE = 64
QK_ROPE = 32
PLE_DIM = 256
D_IN = 2976
D_EXT = 3072
ROPE_THETA = 10000.0
EPS = 1e-6
N_SHARD = 4

ADAM_LR = 0.001
ADAM_B1 = 0.9
ADAM_B2 = 0.999
ADAM_EPS = 1e-08
ADAM_WD = 0.01
ADAM_STEP = 10

LANES = 128
BK = 128
WQ = 256
SB_NSUB = 8
MQ_FWD = 4096
MQ_BWD = 1024
MLA_CW = 256
SB_CUTOFF = 120.0
TM = 256
TM_PRE = 256
VEC_ROWS = 16
VMEM_DENSE = 52 * 1024 * 1024
VMEM_ATTN = 40 * 1024 * 1024


def _mm(a, b):
    return jnp.dot(a, b, preferred_element_type=F32)


def _mm_nt(a, b):
    return lax.dot_general(a, b, (((1,), (1,)), ((), ())), preferred_element_type=F32)


def _mm_tn(a, b):
    return lax.dot_general(a, b, (((0,), (0,)), ((), ())), preferred_element_type=F32)


def _seg(a, bd2):
    return _mm(_split2(a), bd2)


def _const(mask):
    return jnp.asarray(np.asarray(mask, np.float32), dtype=BF16)


def _blockdiag2(n, seg):
    r = (np.arange(2 * n)[:, None] % n) // seg
    c = np.arange(n)[None, :] // seg
    return _const(r == c)


def _sigmoid(a):
    return 1.0 / (1.0 + jnp.exp(-a))


def _rowmean(a):
    return jnp.mean(a, axis=-1, keepdims=True)


def _colsum(a):
    return jnp.sum(a, axis=0, keepdims=True)


def _rope_fwd(a, c, sa, sb):
    w = a.shape[-1]
    return a * c + pltpu.roll(a, w - 16, 1) * sa + pltpu.roll(a, 16, 1) * sb


def _rope_bwd(g, c, sa, sb):
    w = g.shape[-1]
    return g * c + pltpu.roll(g * sa, 16, 1) + pltpu.roll(g * sb, w - 16, 1)


def _full(shape):
    return pl.BlockSpec(shape, lambda *_: (0,) * len(shape))


def _acc(shape):
    return pl.BlockSpec(shape, lambda *_: (0,) * len(shape))


def _full2(shape):
    return pl.BlockSpec(shape, lambda p, i: (0, 0))


def _cols(height, tm=TM):
    return pl.BlockSpec((height, tm), lambda i: (0, i))


def _rows(width, tm=TM):
    return pl.BlockSpec((tm, width), lambda i: (i, 0))


def _pre_fwd(x, tabs, gpre, win, gq, wuq, gkv, wk, wv):
    s = x.shape[0]
    c_t, sa_t, sb_t = tabs
    rw, cl = (lambda width: _rows(width, TM_PRE)), (lambda height: _cols(height, TM_PRE))

    def body(x_ref, c_ref, sa_ref, sb_ref, gpre_ref, win_ref, gq_ref, wuq_ref, gkv_ref, wk_ref, wv_ref,
             sbq_ref, sbk_ref, sbv_ref, sbg_ref, mlag_ref, cq_ref, ckv_ref, qc_ref, kc_ref, mv_ref,
             sbkt_ref, sbvt_ref, kct_ref, mvt_ref):
        xv = x_ref[...]
        r1 = lax.rsqrt(_rowmean(xv * xv) + EPS)
        h = (xv * r1 * gpre_ref[...]).astype(BF16)
        proj = _mm(h, win_ref[...])
        sbq_ref[...] = proj[:, 0:512].astype(BF16)
        sbk_ref[...] = proj[:, 512:1024].astype(BF16)
        sbv_ref[...] = proj[:, 1024:1536].astype(BF16)
        sbkt_ref[...] = proj[:, 512:1024].T.astype(BF16)
        sbvt_ref[...] = proj[:, 1024:1536].T.astype(BF16)
        sbg_ref[...] = proj[:, 1536:2048]
        cq = proj[:, 2048:2304]
        ckv = proj[:, 2304:2432]
        kr = proj[:, 2432:2560]
        mlag_ref[...] = proj[:, 2560:3072]
        cq_ref[...] = cq
        ckv_ref[...] = ckv
        c1, sa1, sb1 = c_ref[...], sa_ref[...], sb_ref[...]
        c8, sa8, sb8 = jnp.tile(c1, (1, 8)), jnp.tile(sa1, (1, 8)), jnp.tile(sb1, (1, 8))
        cqn = (cq * lax.rsqrt(_rowmean(cq * cq) + EPS) * gq_ref[...]).astype(BF16)
        qe = _mm(cqn, wuq_ref[...])
        qc_ref[...] = _rope_fwd(qe, c8, sa8, sb8).astype(BF16)
        ckvn = (ckv * lax.rsqrt(_rowmean(ckv * ckv) + EPS) * gkv_ref[...]).astype(BF16)
        ke = _mm(ckvn, wk_ref[...])
        krr = _rope_fwd(kr, c1, sa1, sb1)
        kcat = ke + jnp.tile(krr, (1, 8))
        kc_ref[...] = kcat.astype(BF16)
        kct_ref[...] = kcat.T.astype(BF16)
        mval = _mm(ckvn, wv_ref[...])
        mv_ref[...] = mval.astype(BF16)
        mvt_ref[...] = mval.T.astype(BF16)

    out_shape = (
        jax.ShapeDtypeStruct((s, 512), BF16), jax.ShapeDtypeStruct((s, 512), BF16), jax.ShapeDtypeStruct((s, 512), BF16),
        jax.ShapeDtypeStruct((s, 512), F32), jax.ShapeDtypeStruct((s, 512), F32),
        jax.ShapeDtypeStruct((s, Q_LORA), F32), jax.ShapeDtypeStruct((s, KV_LORA), F32),
        jax.ShapeDtypeStruct((s, 1024), BF16), jax.ShapeDtypeStruct((s, 1024), BF16), jax.ShapeDtypeStruct((s, 512), BF16),
        jax.ShapeDtypeStruct((512, s), BF16), jax.ShapeDtypeStruct((512, s), BF16), jax.ShapeDtypeStruct((1024, s), BF16),
        jax.ShapeDtypeStruct((512, s), BF16),
    )
    return pl.pallas_call(
        body, name="pre_fwd", grid=(s // TM_PRE,), out_shape=out_shape,
        in_specs=[rw(D_MODEL), rw(LANES), rw(LANES), rw(LANES), _full((1, D_MODEL)), _full((D_MODEL, D_EXT)),
                  _full((1, Q_LORA)), _full((Q_LORA, 1024)), _full((1, KV_LORA)), _full((KV_LORA, 1024)), _full((KV_LORA, 512))],
        out_specs=(rw(512), rw(512), rw(512), rw(512), rw(512), rw(Q_LORA), rw(KV_LORA),
                   rw(1024), rw(1024), rw(512), cl(512), cl(512), cl(1024), cl(512)),
        compiler_params=pltpu.CompilerParams(vmem_limit_bytes=VMEM_DENSE),
    )(x, c_t, sa_t, sb_t, gpre, win, gq, wuq, gkv, wk, wv)


def _softplus(z):
    neg_abs = lax.bitcast_convert_type(lax.bitcast_convert_type(z, jnp.uint32) | jnp.uint32(0x80000000), F32)
    return jnp.maximum(z, 0.0) + jnp.log(1.0 + jnp.exp(neg_abs))


def _sum_matrix(kind, terms):
    r, c = np.arange(2 * BK)[:, None], np.arange(2 * BK * terms)[None, :] % (2 * BK)
    rk, ck = r % BK, c % BK
    return _const(((r // BK) == (c // BK)) & {"suffix": ck >= rk, "prefix": ck <= rk}[kind])


def _split_rows(a):
    hi = a.astype(BF16)
    return jnp.concatenate([hi, (a - hi.astype(F32)).astype(BF16)], axis=0)


def _heads_t(blk, rowi):
    zero = jnp.zeros_like(blk)
    return jnp.concatenate([jnp.where(rowi < 64, blk, zero), jnp.where(rowi >= 64, blk, zero)], axis=1)


def _mask_keys(a, valid, fill=0.0):
    return jnp.concatenate([jnp.where(valid, a[0:BK], fill), jnp.where(valid, a[BK:2 * BK], fill)], axis=0)


def _split2(a):
    hi = a.astype(BF16)
    lo = (a - hi.astype(F32)).astype(BF16)
    return jnp.concatenate([hi, lo], axis=1)


def _pair_stack(b, lane):
    zero = jnp.zeros_like(b)
    return jnp.concatenate([jnp.where(lane < 64, b, zero), jnp.where(lane >= 64, b, zero)], axis=0)


def _sb_fwd(q, k, vt, late):
    s = q.shape[0]
    n = len(late)

    def body(q_ref, k_ref, vt_ref, usuf_ref, *rest):
        ins, o_ref, outs = rest[:n], rest[n], rest[n + 1:2 * n + 1]
        acc_scr, run_scr = rest[2 * n + 1:2 * n + 3]
        bufs, (send_sems, recv_sems, out_sems) = rest[2 * n + 3:3 * n + 3], rest[3 * n + 3:]
        p, i = pl.program_id(0), pl.program_id(1)
        gather_start, gather_forward, gather_finish = _gather_steps([a.shape for a in late], ins, bufs, send_sems, recv_sems)

        @pl.when((p == 0) & (i == 0))
        def _():
            gather_start()

        @pl.when((p == 2) & (i == 0))
        def _():
            gather_forward()

        lane = lax.broadcasted_iota(jnp.int32, (1, LANES), 1)
        rowi = lax.broadcasted_iota(jnp.int32, (LANES, 1), 0)
        keyi = lax.broadcasted_iota(jnp.int32, (BK, WQ), 0)

        def group(i, qs, blocks, masked, seen=None):
            seen = seen or [0] * len(blocks)
            qryi = lax.broadcasted_iota(jnp.int32, (BK, WQ), 1) + i * WQ
            starts = [pl.multiple_of(j * BK, BK) for j in blocks]
            valid = [(keyi[:, lo:] + j * BK) < qryi[:, lo:] if m else None for j, m, lo in zip(blocks, masked, seen)]
            zs = [_mm_nt(_pair_stack(k_ref[pl.ds(ks, BK), :], lane), qs[lo:]) for ks, lo in zip(starts, seen)]
            sps = [_softplus(z) for z in zs]
            sps = [sp if ok is None else _mask_keys(sp, ok) for sp, ok in zip(sps, valid)]
            cums = [_mm(usuf_ref[...], _split_rows(sp)) for sp in sps]
            ws = [jnp.exp(z - c) for z, c in zip(zs, cums)]
            ws = [w if ok is None else _mask_keys(w, ok) for w, ok in zip(ws, valid)]
            pvs = [_mm(_heads_t(vt_ref[:, pl.ds(ks, BK)], rowi), w.astype(BF16)) for ks, w in zip(starts, ws)]
            for pv, c, lo in zip(pvs, cums, seen):
                r0, r1 = run_scr[0:1, lo:], run_scr[1:2, lo:]
                acc_scr[:, lo:] += jnp.where(rowi < 64, jnp.exp(-r0), jnp.exp(-r1)) * pv
                run_scr[0:1, lo:] = r0 + c[0:1]
                run_scr[1:2, lo:] = r1 + c[BK:BK + 1]

        assert WQ == 2 * BK

        def unfinished():
            return (jnp.min(run_scr[0:2, :]) < SB_CUTOFF).astype(jnp.int32)

        def query_block(sub):
            i = pl.program_id(1) * SB_NSUB + sub
            rows = pl.ds(pl.multiple_of(sub * WQ, WQ), WQ)
            qs = q_ref[rows, :] * (HEAD_DIM ** -0.5)
            acc_scr[...] = jnp.zeros_like(acc_scr)
            run_scr[...] = jnp.zeros_like(run_scr)

            @pl.when(i == 0)
            def _():
                group(i, qs, [1, 0], [True, True], [BK, 0])

            @pl.when(i > 0)
            def _():
                group(i, qs, [2 * i + 1, 2 * i, 2 * i - 1, 2 * i - 2], [True, True, False, False], [BK, 0, 0, 0])

            def step(c):
                group(i, qs, [2 * i - 1 - 2 * c[0], 2 * i - 2 - 2 * c[0]], [False, False])
                return c[0] + 1, unfinished()

            lax.while_loop(lambda c: (c[0] < i) & (c[1] > 0), step, (jnp.int32(1), unfinished()))
            o_ref[rows, :] = acc_scr[...].T

        lax.fori_loop(0, SB_NSUB, lambda sub, c: (query_block(sub), c)[1], 0)

        @pl.when((p == pl.num_programs(0) - 1) & (i == pl.num_programs(1) - 1))
        def _():
            gather_finish()
            copies = [pltpu.make_async_copy(bufs[t], outs[t], out_sems.at[t]) for t in range(n)]
            for cp in copies:
                cp.start()
            for cp in copies:
                cp.wait()

    qspec = pl.BlockSpec((SB_NSUB * WQ, LANES), lambda p, i: (i, p))
    kspec = pl.BlockSpec((s, LANES), lambda p, i: (0, p))
    tspec = pl.BlockSpec((LANES, s), lambda p, i: (p, 0))
    gathered = [jax.ShapeDtypeStruct((N_SHARD,) + a.shape, BF16) for a in late]
    return pl.pallas_call(
        body, name="sb_fwd", grid=(4, s // (SB_NSUB * WQ)),
        out_shape=(jax.ShapeDtypeStruct((s, 512), F32), *gathered),
        in_specs=[qspec, kspec, tspec, _full2((2 * BK, 4 * BK))] + [_full2(a.shape) for a in late],
        out_specs=(qspec,) + (pl.BlockSpec(memory_space=pl.ANY),) * n,
        scratch_shapes=[pltpu.VMEM((LANES, WQ), F32), pltpu.VMEM((8, WQ), F32)] + [pltpu.VMEM(g.shape, BF16) for g in gathered]
                       + [pltpu.SemaphoreType.DMA((6 * n,)), pltpu.SemaphoreType.DMA((6 * n,)), pltpu.SemaphoreType.DMA((n,))],
        compiler_params=pltpu.CompilerParams(vmem_limit_bytes=VMEM_ATTN),
    )(q, k, vt, _sum_matrix("suffix", 2), *late)


def _sb_bwd(q, k, kt, v, do, late):
    s = q.shape[0]
    n = len(late)
    halves = [a.shape[1] // 2 for a in late]

    def body(q_ref, k_ref, kt_ref, v_ref, do_ref, usuf_ref, upre_ref, *rest):
        g_refs, (dq_ref, dk_ref, dv_ref), outs = rest[:n], rest[n:n + 3], rest[n + 3:2 * n + 3]
        later_scr, dqt_scr, st_scr = rest[2 * n + 3:2 * n + 6]
        f_scr, reduce_scr, out_sems = rest[2 * n + 6:3 * n + 6], rest[3 * n + 6:-1], rest[-1]
        p, i = pl.program_id(0), pl.program_id(1)
        reduce_load, reduce_partial, reduce_total, reduce_finish = _reduce_steps(halves, g_refs, f_scr, reduce_scr)

        @pl.when((p == 0) & (i == 0))
        def _():
            reduce_load()

        @pl.when((p == 1) & (i == 0))
        def _():
            reduce_partial()

        @pl.when((p == 3) & (i == 0))
        def _():
            reduce_total()

        @pl.when(i == 0)
        def _():
            dk_ref[...] = jnp.zeros_like(dk_ref)
            dv_ref[...] = jnp.zeros_like(dv_ref)

        lane = lax.broadcasted_iota(jnp.int32, (1, LANES), 1)
        rowi = lax.broadcasted_iota(jnp.int32, (LANES, 1), 0)
        keyi = lax.broadcasted_iota(jnp.int32, (BK, WQ), 0)
        assert WQ == 2 * BK

        def query_block(sub):
            i = pl.program_id(1) * SB_NSUB + sub
            rows = pl.ds(pl.multiple_of(sub * WQ, WQ), WQ)
            qryi = lax.broadcasted_iota(jnp.int32, (BK, WQ), 1) + i * WQ
            qs = q_ref[rows, :] * (HEAD_DIM ** -0.5)
            dob = do_ref[rows, :]
            dot = dob.astype(F32).T.astype(BF16)

            def scores(j, lo=0):
                return _mm_nt(_pair_stack(k_ref[pl.ds(pl.multiple_of(j * BK, BK), BK), :], lane), qs[lo:])

            def scan(blocks, masked, seen=None):
                seen = seen or [0] * len(blocks)
                sps = [_softplus(scores(j, lo)) for j, lo in zip(blocks, seen)]
                sps = [_mask_keys(sp, (keyi[:, lo:] + j * BK) < qryi[:, lo:]) if m else sp
                       for sp, j, m, lo in zip(sps, blocks, masked, seen)]
                for sp, j, lo in zip(sps, blocks, seen):
                    run = st_scr[0:2, :]
                    later_scr[j, 0:2, :] = run
                    st_scr[0:2, lo:] = run[:, lo:] + jnp.concatenate([jnp.sum(sp[0:BK], axis=0, keepdims=True),
                                                                      jnp.sum(sp[BK:2 * BK], axis=0, keepdims=True)], axis=0)

            def sweep(blocks, masked, seen=None):
                seen = seen or [0] * len(blocks)
                starts = [pl.multiple_of(j * BK, BK) for j in blocks]
                valid = [(keyi[:, lo:] + j * BK) < qryi[:, lo:] if m else None for j, m, lo in zip(blocks, masked, seen)]
                zs = [scores(j, lo) for j, lo in zip(blocks, seen)]
                us = [jnp.exp(lax.bitcast_convert_type(lax.bitcast_convert_type(z, jnp.uint32) | jnp.uint32(0x80000000), F32))
                      for z in zs]
                sps = [jnp.maximum(z, 0.0) + jnp.log(1.0 + u) for z, u in zip(zs, us)]
                sps = [sp if ok is None else _mask_keys(sp, ok) for sp, ok in zip(sps, valid)]
                sigs = [jnp.where(z >= 0.0, 1.0, u) / (1.0 + u) for z, u in zip(zs, us)]
                cums = [_mm(usuf_ref[...], _split_rows(sp)) for sp in sps]
                dws = [_mm(_pair_stack(v_ref[pl.ds(ks, BK), :], lane), dot[:, lo:]) for ks, lo in zip(starts, seen)]
                wfs = []
                for z, c, j, ok, lo in zip(zs, cums, blocks, valid, seen):
                    f = jnp.exp(-later_scr[j, 0:2, lo:])
                    wide = (BK, WQ - lo)
                    wf = jnp.exp(z - c) * jnp.concatenate([jnp.broadcast_to(f[0:1], wide), jnp.broadcast_to(f[1:2], wide)], axis=0)
                    wfs.append(wf if ok is None else _mask_keys(wf, ok))
                es = [dw * wf for dw, wf in zip(dws, wfs)]
                pres = [_mm(upre_ref[...], e.astype(BF16)) for e in es]
                dzs = []
                for e, pre, sig, ok, lo in zip(es, pres, sigs, valid, seen):
                    e0 = pre[0:BK] + st_scr[0:1, lo:]
                    e1 = pre[BK:2 * BK] + st_scr[1:2, lo:]
                    st_scr[0:1, lo:] = e0[BK - 1:BK]
                    st_scr[1:2, lo:] = e1[BK - 1:BK]
                    dz = e - sig * jnp.concatenate([e0, e1], axis=0)
                    dzs.append((dz if ok is None else _mask_keys(dz, ok)).astype(BF16))
                whole = [b for b, lo in enumerate(seen) if lo == 0]
                dqt_scr[...] += _mm(jnp.concatenate([_heads_t(kt_ref[:, pl.ds(starts[b], BK)], rowi) for b in whole], axis=1),
                                    jnp.concatenate([dzs[b] for b in whole], axis=0))
                for b, lo in enumerate(seen):
                    if lo:
                        dqt_scr[:, lo:] += _mm(_heads_t(kt_ref[:, pl.ds(starts[b], BK)], rowi), dzs[b])
                for ks, dz, wf, lo in zip(starts, dzs, wfs, seen):
                    rk = _mm(dz, qs[lo:])
                    dk_ref[pl.ds(ks, BK), :] += jnp.where(lane < 64, rk[0:BK], rk[BK:2 * BK])
                    rv = _mm(wf.astype(BF16), dob[lo:])
                    dv_ref[pl.ds(ks, BK), :] += jnp.where(lane < 64, rv[0:BK], rv[BK:2 * BK])

            st_scr[...] = jnp.zeros_like(st_scr)

            @pl.when(i == 0)
            def _():
                scan([1, 0], [True, True], [BK, 0])

            @pl.when(i > 0)
            def _():
                scan([2 * i + 1, 2 * i, 2 * i - 1, 2 * i - 2], [True, True, False, False], [BK, 0, 0, 0])

            def unfinished():
                return (jnp.min(st_scr[0:2, :]) < SB_CUTOFF).astype(jnp.int32)

            def step(c):
                scan([2 * i - 1 - 2 * c[0], 2 * i - 2 - 2 * c[0]], [False, False])
                return c[0] + 1, unfinished()

            npairs, _ = lax.while_loop(lambda c: (c[0] < i) & (c[1] > 0), step, (jnp.minimum(i, 1), unfinished()))

            st_scr[...] = jnp.zeros_like(st_scr)
            dqt_scr[...] = jnp.zeros_like(dqt_scr)
            first = 2 * (i - npairs)

            def early(t, carry):
                sweep([first + 2 * t, first + 2 * t + 1], [False, False])
                return carry

            lax.fori_loop(0, npairs - 1, early, 0)

            @pl.when(i == 0)
            def _():
                sweep([0, 1], [True, True], [0, BK])

            @pl.when(i > 0)
            def _():
                sweep([2 * i - 2, 2 * i - 1, 2 * i, 2 * i + 1], [False, False, True, True], [0, 0, 0, BK])

            dq_ref[rows, :] = (dqt_scr[...].T * (HEAD_DIM ** -0.5)).astype(BF16)

        lax.fori_loop(0, SB_NSUB, lambda sub, c: (query_block(sub), c)[1], 0)

        @pl.when((p == pl.num_programs(0) - 1) & (i == pl.num_programs(1) - 1))
        def _():
            reduce_finish()
            copies = [pltpu.make_async_copy(f_scr[t], outs[t], out_sems.at[t]) for t in range(n)]
            for cp in copies:
                cp.start()
            for cp in copies:
                cp.wait()

    qspec = pl.BlockSpec((SB_NSUB * WQ, LANES), lambda p, i: (i, p))
    kspec = pl.BlockSpec((s, LANES), lambda p, i: (0, p))
    tspec = pl.BlockSpec((LANES, s), lambda p, i: (p, 0))
    anywhere = pl.BlockSpec(memory_space=pl.ANY)
    reduced = [jax.ShapeDtypeStruct(a.shape[1:], F32) for a in late]
    return pl.pallas_call(
        body, name="sb_bwd", grid=(4, s // (SB_NSUB * WQ)),
        out_shape=(jax.ShapeDtypeStruct((s, 512), BF16), jax.ShapeDtypeStruct((s, 512), F32),
                   jax.ShapeDtypeStruct((s, 512), F32), *reduced),
        in_specs=[qspec, kspec, tspec, kspec, qspec, _full2((2 * BK, 4 * BK)), _full2((2 * BK, 2 * BK))] + [anywhere] * n,
        out_specs=(qspec, kspec, kspec) + (anywhere,) * n,
        scratch_shapes=[pltpu.VMEM((s // BK, 8, WQ), F32), pltpu.VMEM((LANES, WQ), F32), pltpu.VMEM((8, WQ), F32)]
                       + [pltpu.VMEM(r.shape, F32) for r in reduced] + _reduce_scratch(late) + [pltpu.SemaphoreType.DMA((n,))],
        compiler_params=pltpu.CompilerParams(vmem_limit_bytes=VMEM_ATTN),
    )(q, k, kt, v, do, _sum_matrix("suffix", 2), _sum_matrix("prefix", 1), *late)


MLA_SCALE = (QK_NOPE + QK_ROPE) ** -0.5
LOG2E = 1.4426950408889634


def _mla_keys(kb):
    zero = jnp.zeros((BK, LANES), kb.dtype)
    return jnp.concatenate([jnp.concatenate([kb[:, 0:LANES], zero], axis=1),
                            jnp.concatenate([zero, kb[:, LANES:2 * LANES]], axis=1)], axis=0)


def _mla_fwd(qc, kc, vt):
    s = qc.shape[0]
    mq = min(MQ_FWD, s)
    rows_l = 16

    def body(q_ref, k_ref, vt_ref, o_ref, l_ref, p_scr, ot_scr, st_scr):
        i = pl.program_id(1)
        row = lax.broadcasted_iota(jnp.int32, (LANES, 1), 0)
        orow = lax.broadcasted_iota(jnp.int32, (rows_l, 2 * BK), 0)
        ocol = lax.broadcasted_iota(jnp.int32, (rows_l, 2 * BK), 1)
        ones = jnp.where(((orow == 0) & (ocol < BK)) | ((orow == 1) & (ocol >= BK)), 1.0, 0.0).astype(BF16)

        def chunks(lo, hi):
            return [(a, min(a + MLA_CW, hi)) for a in range(lo, hi, MLA_CW)]

        def keys(j):
            return _mla_keys(k_ref[pl.ds(pl.multiple_of(j * BK, BK), BK), :])

        def values_t(j):
            vtb = vt_ref[:, pl.ds(pl.multiple_of(j * BK, BK), BK)]
            zero = jnp.zeros_like(vtb)
            top = jnp.concatenate([jnp.where(row < 64, vtb, zero), jnp.where(row >= 64, vtb, zero)], axis=1)
            return jnp.concatenate([top, ones], axis=0)

        def pair_values(ja):
            return jnp.concatenate([values_t(ja), values_t(ja + 1)], axis=1)

        def softmax(ja, za, zb, masked, a, b):
            c = MLA_SCALE * LOG2E
            parts = [za[0:BK] * c, za[BK:2 * BK] * c, zb[0:BK] * c, zb[BK:2 * BK] * c]
            if masked:
                keyc = lax.broadcasted_iota(jnp.int32, (BK, b - a), 0)
                qryc = (lax.broadcasted_iota(jnp.int32, (BK, b - a), 1) + (i * mq + a)) // 64
                va = ((keyc + ja * BK) // 64) <= qryc
                vb = ((keyc + (ja + 1) * BK) // 64) <= qryc
                parts = [jnp.where(va, parts[0], -1e30), jnp.where(va, parts[1], -1e30),
                         jnp.where(vb, parts[2], -1e30), jnp.where(vb, parts[3], -1e30)]
            m0, m1 = st_scr[0:1, a:b], st_scr[1:2, a:b]
            n0 = jnp.maximum(m0, jnp.max(jnp.maximum(parts[0], parts[2]), axis=0, keepdims=True))
            n1 = jnp.maximum(m1, jnp.max(jnp.maximum(parts[1], parts[3]), axis=0, keepdims=True))
            st_scr[2:3, a:b] = jnp.exp2(m0 - n0)
            st_scr[3:4, a:b] = jnp.exp2(m1 - n1)
            st_scr[0:1, a:b] = n0
            st_scr[1:2, a:b] = n1
            p_scr[:, a:b] = jnp.concatenate([jnp.exp2(parts[0] - n0), jnp.exp2(parts[1] - n1),
                                             jnp.exp2(parts[2] - n0), jnp.exp2(parts[3] - n1)], axis=0).astype(BF16)

        def accumulate(vals, a, b):
            pv = _mm(vals, p_scr[:, a:b])
            f = jnp.where(row < 64, st_scr[2:3, a:b], st_scr[3:4, a:b])
            ot_scr[0:LANES, a:b] = f * ot_scr[0:LANES, a:b] + pv[0:LANES]
            ot_scr[LANES:LANES + 8, a:b] = st_scr[2:10, a:b] * ot_scr[LANES:LANES + 8, a:b] + pv[LANES:LANES + 8]

        def step(n, diag, lo=0, prev_lo=0):
            kab = jnp.concatenate([keys(2 * n), keys(2 * n + 1)], axis=0)
            vals = pair_values(2 * n - 2)
            for a, b in chunks(prev_lo, lo):
                accumulate(vals, a, b)
            for a, b in chunks(lo, mq):
                zab = _mm_nt(kab, q_ref[a:b, :])
                accumulate(vals, a, b)
                softmax(2 * n, zab[0:2 * BK], zab[2 * BK:4 * BK], diag and a < lo + 2 * BK, a, b)

        def first(diag):
            kab = jnp.concatenate([keys(0), keys(1)], axis=0)
            for a, b in chunks(0, mq):
                zab = _mm_nt(kab, q_ref[a:b, :])
                softmax(0, zab[0:2 * BK], zab[2 * BK:4 * BK], diag and a < 2 * BK, a, b)

        st_scr[...] = jnp.concatenate([jnp.full((2, mq), -1e30, F32), jnp.ones((14, mq), F32)], axis=0)
        ot_scr[...] = jnp.zeros_like(ot_scr)

        npq = mq // (2 * BK)
        seen = lambda d: 2 * BK * max(d, 0)

        @pl.when(i == 0)
        def _():
            first(True)
            for d in range(1, npq):
                step(d, True, seen(d), seen(d - 1))

        if s > mq:
            @pl.when(i > 0)
            def _():
                first(False)
                lax.fori_loop(1, npq * i, lambda n, c: (step(n, False), c)[1], 0)
                for d in range(npq):
                    step(npq * i + d, True, seen(d), seen(d - 1))

        vals = pair_values(2 * (npq * (i + 1) - 1))
        for a, b in chunks(seen(npq - 1), mq):
            accumulate(vals, a, b)
        for a, b in chunks(0, mq):
            l0, l1 = ot_scr[LANES:LANES + 1, a:b], ot_scr[LANES + 1:LANES + 2, a:b]
            o_ref[a:b, :] = (ot_scr[0:LANES, a:b] / jnp.where(row < 64, l0, l1)).T
            l_ref[a:b, :] = jnp.where(row < 64, st_scr[0:1, a:b] + jnp.log2(l0), st_scr[1:2, a:b] + jnp.log2(l1)).T

    qspec = pl.BlockSpec((mq, 2 * LANES), lambda p, i: (i, p))
    kspec = pl.BlockSpec((s, 2 * LANES), lambda p, i: (0, p))
    vtspec = pl.BlockSpec((LANES, s), lambda p, i: (p, 0))
    ospec = pl.BlockSpec((mq, LANES), lambda p, i: (i, p))
    return pl.pallas_call(
        body, name="mla_fwd", grid=(4, s // mq),
        out_shape=(jax.ShapeDtypeStruct((s, 512), F32), jax.ShapeDtypeStruct((s, 512), F32)),
        in_specs=[qspec, kspec, vtspec], out_specs=(ospec, ospec),
        scratch_shapes=[pltpu.VMEM((4 * BK, mq), BF16), pltpu.VMEM((LANES + 8, mq), F32), pltpu.VMEM((16, mq), F32)],
        compiler_params=pltpu.CompilerParams(vmem_limit_bytes=VMEM_ATTN),
    )(qc, kc, vt)


def _mla_bwd(qc, kc, kct, v, do, lse, delta):
    s = qc.shape[0]
    mq = min(MQ_BWD, s)

    def body(q_ref, k_ref, kt_ref, v_ref, do_ref, l_ref, d_ref, dq_ref, dk_ref, dv_ref, dqt_scr, p_scr, dz_scr,
             dvt_scr, dost_scr, dot_scr):
        i = pl.program_id(1)

        @pl.when(i == 0)
        def _():
            dk_ref[...] = jnp.zeros_like(dk_ref)
            dvt_scr[...] = jnp.zeros_like(dvt_scr)

        lane = lax.broadcasted_iota(jnp.int32, (1, LANES), 1)
        keyc = lax.broadcasted_iota(jnp.int32, (BK, mq), 0)
        qryc = (lax.broadcasted_iota(jnp.int32, (BK, mq), 1) + i * mq) // 64
        dob = do_ref[...].astype(F32)
        dost_scr[...] = (dob * MLA_SCALE).T.astype(BF16)
        dot_scr[...] = dob.T.astype(BF16)
        lt = l_ref[...].T
        dt = (d_ref[...] * MLA_SCALE).T
        lse0, lse1 = lt[0:1], lt[64:65]
        dl0, dl1 = dt[0:1], dt[64:65]
        dqt_scr[...] = jnp.zeros_like(dqt_scr)

        def products(j, lo=0):
            ks = pl.multiple_of(j * BK, BK)
            return (_mm_nt(_mla_keys(k_ref[pl.ds(ks, BK), :]), q_ref[lo:, :]),
                    _mm(_pair_stack(v_ref[pl.ds(ks, BK), :], lane), dost_scr[:, lo:]))

        def grads(j, slot, zt, dwt, masked, lo=0):
            zt = zt * (MLA_SCALE * LOG2E)
            p0 = jnp.exp2(zt[0:BK] - lse0[:, lo:])
            p1 = jnp.exp2(zt[BK:2 * BK] - lse1[:, lo:])
            if masked:
                valid = ((keyc[:, lo:] + j * BK) // 64) <= qryc[:, lo:]
                p0, p1 = jnp.where(valid, p0, 0.0), jnp.where(valid, p1, 0.0)
            rows = slice(slot * BK, (slot + 1) * BK)
            p_scr[0, rows, lo:] = p0.astype(BF16)
            p_scr[1, rows, lo:] = p1.astype(BF16)
            dz_scr[0, rows, lo:] = (p0 * (dwt[0:BK] - dl0[:, lo:])).astype(BF16)
            dz_scr[1, rows, lo:] = (p1 * (dwt[BK:2 * BK] - dl1[:, lo:])).astype(BF16)

        def scatter(ja, lo=0):
            ks = pl.multiple_of(ja * BK, 2 * BK)
            for h in range(2):
                cols, vrows = slice(h * LANES, (h + 1) * LANES), slice(h * 64, (h + 1) * 64)
                dzh = dz_scr[h, :, lo:]
                dqt_scr[cols, lo:] += _mm(kt_ref[cols, pl.ds(ks, 2 * BK)], dzh)
                dk_ref[pl.ds(ks, 2 * BK), cols] += _mm(dzh, q_ref[lo:, cols])
                dvt_scr[vrows, pl.ds(ks, 2 * BK)] += _mm_nt(dot_scr[vrows, lo:], p_scr[h, :, lo:])

        def step(n, masked, lo=0, prev_lo=0):
            za, wa = products(2 * n, lo)
            zb, wb = products(2 * n + 1, lo)
            scatter(2 * n - 2, prev_lo)
            grads(2 * n, 0, za, wa, masked, lo)
            grads(2 * n + 1, 1, zb, wb, masked, lo)

        def first(masked):
            za, wa = products(0)
            zb, wb = products(1)
            grads(0, 0, za, wa, masked)
            grads(1, 1, zb, wb, masked)

        npq = mq // (2 * BK)
        seen = lambda d: 2 * BK * max(d, 0)

        @pl.when(i == 0)
        def _():
            first(True)
            for d in range(1, npq):
                step(d, True, seen(d), seen(d - 1))

        @pl.when(i > 0)
        def _():
            first(False)
            step(1, False)
            lax.fori_loop(1, npq * i // 2, lambda m, c: (step(2 * m, False), step(2 * m + 1, False), c)[2], 0)
            for d in range(npq):
                step(npq * i + d, True, seen(d), seen(d - 1))

        scatter(2 * (npq * (i + 1) - 1), seen(npq - 1))
        dq_ref[...] = dqt_scr[...].T

        @pl.when(i == s // mq - 1)
        def _():
            for a in range(0, s, 512):
                dv_ref[a:a + 512, :] = dvt_scr[:, a:a + 512].T

    qspec = pl.BlockSpec((mq, 2 * LANES), lambda p, i: (i, p))
    kspec = pl.BlockSpec((s, 2 * LANES), lambda p, i: (0, p))
    ktspec = pl.BlockSpec((2 * LANES, s), lambda p, i: (p, 0))
    vspec = pl.BlockSpec((s, LANES), lambda p, i: (0, p))
    ospec = pl.BlockSpec((mq, LANES), lambda p, i: (i, p))
    return pl.pallas_call(
        body, name="mla_bwd", grid=(4, s // mq),
        out_shape=(jax.ShapeDtypeStruct((s, 1024), F32), jax.ShapeDtypeStruct((s, 1024), F32),
                   jax.ShapeDtypeStruct((s, 512), F32)),
        in_specs=[qspec, kspec, ktspec, vspec, ospec, ospec, ospec], out_specs=(qspec, kspec, vspec),
        scratch_shapes=[pltpu.VMEM((2 * LANES, mq), F32), pltpu.VMEM((2, 2 * BK, mq), BF16), pltpu.VMEM((2, 2 * BK, mq), BF16),
                        pltpu.VMEM((LANES, s), F32), pltpu.VMEM((LANES, mq), BF16), pltpu.VMEM((LANES, mq), BF16)],
        compiler_params=pltpu.CompilerParams(vmem_limit_bytes=VMEM_ATTN),
    )(qc, kc, kct, v, do, lse, delta)


def _post(x, p, tgt, sbo, mlao, sbg, mlag, gsb, gmla, wout, gpost, wple, gple, wpg, bpg):
    s = x.shape[0]

    def body(x_ref, p_ref, t_ref, sbo_ref, mlao_ref, sbg_ref, mlag_ref, gsb_ref, gmla_ref, wout_ref,
             gpost_ref, wple_ref, gple_ref, wpg_ref, bpg_ref, bd_ref,
             dsbo_ref, dmlao_ref, delta_ref, dsbg_ref, dmlag_ref, dxres_ref, dwout_ref, dwpg_ref, dwple_ref, vec_ref):
        i = pl.program_id(0)

        @pl.when(i == 0)
        def _():
            dwout_ref[...] = jnp.zeros_like(dwout_ref)
            dwpg_ref[...] = jnp.zeros_like(dwpg_ref)
            dwple_ref[...] = jnp.zeros_like(dwple_ref)
            vec_ref[...] = jnp.zeros_like(vec_ref)

        inv_hd = 1.0 / HEAD_DIM

        def head_fwd(o, g, gate):
            r = lax.rsqrt(_seg(o * o, bd_ref[...]) * inv_hd + EPS)
            hat = o * r
            n = hat * g
            sg = _sigmoid(gate)
            return hat, r, n, sg, n * (gate * sg)

        sbo, mlao, sbg_v, mlag_v = sbo_ref[...], mlao_ref[...], sbg_ref[...], mlag_ref[...]
        gsb_v, gmla_v = gsb_ref[...], gmla_ref[...]
        sb_hat, sb_r, sb_n, sb_sg, sb_y = head_fwd(sbo, gsb_v, sbg_v)
        ml_hat, ml_r, ml_n, ml_sg, ml_y = head_fwd(mlao, gmla_v, mlag_v)
        mix = jnp.concatenate([sb_y, ml_y], axis=1).astype(BF16)
        y = _mm(mix, wout_ref[...])
        ry = lax.rsqrt(_rowmean(y * y) + EPS)
        y_hat = y * ry
        gpost_v = gpost_ref[...]
        x1 = x_ref[...] + y_hat * gpost_v
        pb = p_ref[...].astype(BF16)
        pl_ = jnp.concatenate([_mm(pb, wple_ref[k]) for k in range(N_SHARD)], axis=1)
        rp = lax.rsqrt(_rowmean(pl_ * pl_) + EPS)
        pl_hat = pl_ * rp
        gple_v = gple_ref[...]
        ple = pl_hat * gple_v
        x1b = x1.astype(BF16)
        gate = _sigmoid(_mm(x1b, wpg_ref[...]) + bpg_ref[...])
        err = x1 + ple * gate - t_ref[...]
        loss = 0.5 * jnp.sum(_rowmean(err * err))
        dout = err * (1.0 / D_MODEL)

        du = dout * ple * gate * (1.0 - gate)
        dub = du.astype(BF16)
        dple = dout * gate
        dx1 = dout + _mm_nt(dub, wpg_ref[...])
        dwpg_ref[...] += _mm_tn(x1b, dub)
        dplh = dple * gple_v
        dpl = rp * (dplh - pl_hat * _rowmean(dplh * pl_hat))
        dplb = dpl.astype(BF16)
        wsh = D_MODEL // N_SHARD
        for k in range(N_SHARD):
            dwple_ref[k] += _mm_tn(pb, dplb[:, k * wsh:(k + 1) * wsh])
        dxres_ref[...] = dx1
        dyh = dx1 * gpost_v
        dy = ry * (dyh - y_hat * _rowmean(dyh * y_hat))
        dyb = dy.astype(BF16)
        dwout_ref[...] += _mm_tn(mix, dyb)
        dmix = _mm_nt(dyb, wout_ref[...])

        def head_bwd(dyv, hat, r, n, sg, g, gate):
            dn = dyv * (gate * sg)
            dgate = dyv * n * (sg * (1.0 + gate * (1.0 - sg)))
            dhat = dn * g
            do = r * (dhat - hat * (_seg(dhat * hat, bd_ref[...]) * inv_hd))
            return do, dgate, _colsum(dn * hat)

        dsbo, dsbg, dg_sb = head_bwd(dmix[:, 0:512], sb_hat, sb_r, sb_n, sb_sg, gsb_v, sbg_v)
        dmlao, dmlag, dg_ml = head_bwd(dmix[:, 512:1024], ml_hat, ml_r, ml_n, ml_sg, gmla_v, mlag_v)
        dsbo_ref[...] = dsbo.astype(BF16)
        dmlao_ref[...] = dmlao.astype(BF16)
        delta_ref[...] = _seg(dmlao * mlao, bd_ref[...])
        dsbg_ref[...] = dsbg.astype(BF16)
        dmlag_ref[...] = dmlag.astype(BF16)
        vec_ref[pl.ds(0, 1), :] += _colsum(dx1 * y_hat)
        vec_ref[pl.ds(1, 1), :] += _colsum(dple * pl_hat)
        vec_ref[pl.ds(2, 1), :] += _colsum(du)
        vec_ref[pl.ds(3, 1), :] += jnp.concatenate([dg_sb, dg_ml], axis=1)
        vec_ref[pl.ds(4, 1), :] += jnp.full((1, D_MODEL), loss, F32)

    out_shape = (
        jax.ShapeDtypeStruct((s, 512), BF16), jax.ShapeDtypeStruct((s, 512), BF16), jax.ShapeDtypeStruct((s, 512), F32),
        jax.ShapeDtypeStruct((s, 512), BF16), jax.ShapeDtypeStruct((s, 512), BF16), jax.ShapeDtypeStruct((s, D_MODEL), F32),
        jax.ShapeDtypeStruct((D_MODEL, D_MODEL), F32), jax.ShapeDtypeStruct((D_MODEL, D_MODEL), F32),
        jax.ShapeDtypeStruct(wple.shape, F32), jax.ShapeDtypeStruct((8, D_MODEL), F32),
    )
    return pl.pallas_call(
        body, name="post_fwd_bwd", grid=(s // TM,), out_shape=out_shape,
        in_specs=[_rows(D_MODEL), _rows(PLE_DIM), _rows(D_MODEL), _rows(512), _rows(512), _rows(512), _rows(512),
                  _full((1, 512)), _full((1, 512)), _full((D_MODEL, D_MODEL)),
                  _full((1, D_MODEL)), _full(wple.shape), _full((1, D_MODEL)), _full((D_MODEL, D_MODEL)),
                  _full((1, D_MODEL)), _full((1024, 512))],
        out_specs=(_rows(512), _rows(512), _rows(512), _rows(512), _rows(512), _rows(D_MODEL),
                   _acc((D_MODEL, D_MODEL)), _acc((D_MODEL, D_MODEL)), _acc(wple.shape), _acc((8, D_MODEL))),
        compiler_params=pltpu.CompilerParams(vmem_limit_bytes=VMEM_DENSE),
    )(x, p, tgt, sbo, mlao, sbg, mlag, gsb, gmla, wout, gpost, wple, gple, wpg, bpg, _blockdiag2(512, HEAD_DIM))


def _pre_bwd(x, dxres, dsbq, dsbk, dsbv, dsbg, dmlag, dqc, dkc, dmv, cq, ckv, tabs, gpre, win, gq, wuq, gkv, wk, wv):
    s = x.shape[0]
    c_t, sa_t, sb_t = tabs
    rw = _rows

    def body(x_ref, dxres_ref, dsbq_ref, dsbk_ref, dsbv_ref, dsbg_ref, dmlag_ref, dqc_ref, dkc_ref, dmv_ref, cq_ref,
             ckv_ref, c_ref, sa_ref, sb_ref, gpre_ref, win_ref, gq_ref, wuq_ref, gkv_ref, wk_ref, wv_ref,
             gx_ref, dwin_ref, dwuq_ref, dwk_ref, dwv_ref, vec_ref, dwin_acc):
        i = pl.program_id(0)

        @pl.when(i == 0)
        def _():
            dwin_acc[...] = jnp.zeros_like(dwin_acc)
            dwuq_ref[...] = jnp.zeros_like(dwuq_ref)
            dwk_ref[...] = jnp.zeros_like(dwk_ref)
            dwv_ref[...] = jnp.zeros_like(dwv_ref)
            vec_ref[...] = jnp.zeros_like(vec_ref)

        lane = lax.broadcasted_iota(jnp.int32, (1, LANES), 1)
        c1, sa1, sb1 = c_ref[...], sa_ref[...], sb_ref[...]
        c8, sa8, sb8 = jnp.tile(c1, (1, 8)), jnp.tile(sa1, (1, 8)), jnp.tile(sb1, (1, 8))

        def norm_bwd(dn, hat, r, g):
            t = dn * g
            return r * (t - hat * _rowmean(t * hat)), _colsum(dn * hat)

        xv = x_ref[...]
        r1 = lax.rsqrt(_rowmean(xv * xv) + EPS)
        x_hat = xv * r1
        gpre_v = gpre_ref[...]
        hb = (x_hat * gpre_v).astype(BF16)
        ready = jnp.concatenate([dsbq_ref[...], dsbk_ref[...].astype(BF16), dsbv_ref[...].astype(BF16), dsbg_ref[...]], axis=1)
        dmlag = dmlag_ref[...]
        dwin_acc[:, 0:2048] += _mm_tn(hb, ready)
        dwin_acc[:, 2560:3072] += _mm_tn(hb, dmlag)
        dh = _mm_nt(ready, win_ref[:, 0:2048]) + _mm_nt(dmlag, win_ref[:, 2560:3072])

        dqeb = _rope_bwd(dqc_ref[...], c8, sa8, sb8).astype(BF16)
        cq = cq_ref[...]
        rq = lax.rsqrt(_rowmean(cq * cq) + EPS)
        cq_hat = cq * rq
        gq_v = gq_ref[...]
        dwuq_ref[...] += _mm_tn((cq_hat * gq_v).astype(BF16), dqeb)
        dcq, dg_q = norm_bwd(_mm_nt(dqeb, wuq_ref[...]), cq_hat, rq, gq_v)

        dkc = dkc_ref[...]
        dkcb = dkc.astype(BF16)
        dmvb = dmv_ref[...].astype(BF16)
        ckv = ckv_ref[...]
        rkv = lax.rsqrt(_rowmean(ckv * ckv) + EPS)
        ckv_hat = ckv * rkv
        gkv_v = gkv_ref[...]
        ckvnb = (ckv_hat * gkv_v).astype(BF16)
        dwk_ref[...] += _mm_tn(ckvnb, dkcb)
        dwv_ref[...] += _mm_tn(ckvnb, dmvb)
        dckv, dg_kv = norm_bwd(_mm_nt(dkcb, wk_ref[...]) + _mm_nt(dmvb, wv_ref[...]), ckv_hat, rkv, gkv_v)

        dkr = dkc[:, 0:LANES]
        for hh in range(1, 8):
            dkr = dkr + dkc[:, LANES * hh:LANES * (hh + 1)]
        dkr = _rope_bwd(dkr, c1, sa1, sb1)
        dkr = jnp.where((lane >= 64) & (lane < 96), dkr, 0.0)

        late = jnp.concatenate([dcq.astype(BF16), dckv.astype(BF16), dkr.astype(BF16)], axis=1)
        dwin_acc[:, 2048:2560] += _mm_tn(hb, late)
        dx, dg_pre = norm_bwd(dh + _mm_nt(late, win_ref[:, 2048:2560]), x_hat, r1, gpre_v)
        gx_ref[...] = dxres_ref[...] + dx
        vec_ref[pl.ds(0, 1), :] += dg_pre
        vec_ref[pl.ds(1, 1), :] += jnp.concatenate([dg_q, dg_kv, jnp.zeros((1, D_MODEL - Q_LORA - KV_LORA), F32)], axis=1)

        @pl.when(i == pl.num_programs(0) - 1)
        def _():
            pltpu.sync_copy(dwin_acc, dwin_ref)

    out_shape = (
        jax.ShapeDtypeStruct((s, D_MODEL), F32), jax.ShapeDtypeStruct((D_MODEL, D_EXT), F32),
        jax.ShapeDtypeStruct((Q_LORA, 1024), F32), jax.ShapeDtypeStruct((KV_LORA, 1024), F32),
        jax.ShapeDtypeStruct((KV_LORA, 512), F32), jax.ShapeDtypeStruct((8, D_MODEL), F32),
    )
    return pl.pallas_call(
        body, name="pre_bwd", grid=(s // TM,), out_shape=out_shape,
        in_specs=[rw(D_MODEL), rw(D_MODEL), rw(512), rw(512), rw(512), rw(512), rw(512),
                  rw(1024), rw(1024), rw(512), rw(Q_LORA), rw(KV_LORA), rw(LANES), rw(LANES),
                  rw(LANES), _full((1, D_MODEL)), _full((D_MODEL, D_EXT)), _full((1, Q_LORA)), _full((Q_LORA, 1024)),
                  _full((1, KV_LORA)), _full((KV_LORA, 1024)), _full((KV_LORA, 512))],
        out_specs=(rw(D_MODEL), pl.BlockSpec(memory_space=pl.ANY), _acc((Q_LORA, 1024)), _acc((KV_LORA, 1024)),
                   _acc((KV_LORA, 512)), _acc((8, D_MODEL))),
        scratch_shapes=[pltpu.VMEM((D_MODEL, D_EXT), F32)],
        compiler_params=pltpu.CompilerParams(vmem_limit_bytes=VMEM_DENSE),
    )(x, dxres, dsbq, dsbk, dsbv, dsbg, dmlag, dqc, dkc, dmv, cq, ckv, c_t, sa_t, sb_t, gpre, win, gq, wuq, gkv, wk, wv)


def _place():
    return lax.axis_index("x"), lax.axis_index("y"), lax.axis_index("c")


def _gather_steps(shapes, ins, bufs, send_sems, recv_sems):
    n = len(shapes)
    x, y, c = _place()
    me, sib = (x, y, c), (x, y, 1 - c)
    chips = [(1 - x, y), (x, 1 - y), (1 - x, 1 - y)]

    def half(t, chip, hc):
        rows = shapes[t][0] // 2
        return bufs[t].at[2 * chip[0] + chip[1], pl.ds(pl.multiple_of(hc * rows, 16), rows), :]

    def copy(k, t, chip, hc, to):
        return pltpu.make_async_remote_copy(src_ref=half(t, chip, hc), dst_ref=half(t, chip, hc), send_sem=send_sems.at[k],
                                            recv_sem=recv_sems.at[k], device_id=to, device_id_type=MESH)

    def start():
        for t in range(n):
            bufs[t][2 * x + y] = ins[t][...].astype(BF16)
            for j, chip in enumerate(chips):
                copy(6 * t + j, t, (x, y), c, (*chip, c)).start()

    def forward():
        for t in range(n):
            for j, chip in enumerate(chips):
                copy(6 * t + j, t, chip, c, me).wait_recv()
                copy(6 * t + 3 + j, t, chip, c, sib).start()

    def finish():
        for t in range(n):
            for j, chip in enumerate(chips):
                copy(6 * t + 3 + j, t, chip, 1 - c, me).wait_recv()
        for t in range(n):
            for j, chip in enumerate(chips):
                copy(6 * t + j, t, (x, y), c, (*chip, c)).wait_send()
                copy(6 * t + 3 + j, t, chip, c, sib).wait_send()

    return start, forward, finish


def _allgather_weights(shards):
    n = len(shards)

    def body(*refs):
        start, forward, finish = _gather_steps([a.shape for a in shards], refs[:n], refs[n:2 * n], refs[2 * n], refs[2 * n + 1])
        start()
        forward()
        finish()

    return pl.pallas_call(
        body, name="allgather_weights",
        out_shape=tuple(jax.ShapeDtypeStruct((N_SHARD,) + a.shape, BF16) for a in shards),
        in_specs=[pl.BlockSpec(memory_space=pltpu.VMEM)] * n, out_specs=(pl.BlockSpec(memory_space=pltpu.VMEM),) * n,
        scratch_shapes=[pltpu.SemaphoreType.DMA((6 * n,)), pltpu.SemaphoreType.DMA((6 * n,))],
        compiler_params=pltpu.CompilerParams(vmem_limit_bytes=VMEM_ATTN),
    )(*shards)


def _reduce_scratch(gsh):
    n = len(gsh)
    half_shapes = [(N_SHARD, a.shape[1] // 2, a.shape[2]) for a in gsh]
    return ([pltpu.VMEM(s_, F32) for s_ in half_shapes] * 2 + [pltpu.VMEM(s_, BF16) for s_ in half_shapes] * 2
            + [pltpu.SemaphoreType.DMA((n,)), pltpu.SemaphoreType.DMA((5 * n,)), pltpu.SemaphoreType.DMA((5 * n,))])


def _reduce_steps(halves, g_refs, f_refs, scratch):
    n = len(halves)
    accs, sibs, sbufs, rbufs = scratch[0:n], scratch[n:2 * n], scratch[2 * n:3 * n], scratch[3 * n:4 * n]
    local_sems, send_sems, recv_sems = scratch[4 * n:4 * n + 3]
    x, y, c = _place()
    me, sib = (x, y, c), (x, y, 1 - c)
    mine = 2 * x + y
    chips = [(1 - x, y), (x, 1 - y), (1 - x, 1 - y)]

    def remote(k, src, dst, to):
        return pltpu.make_async_remote_copy(src_ref=src, dst_ref=dst, send_sem=send_sems.at[k], recv_sem=recv_sems.at[k],
                                            device_id=to, device_id_type=MESH)

    def half3(ref, t, hc):
        return ref.at[:, pl.ds(pl.multiple_of(hc * halves[t], 8), halves[t]), :]

    def half2(ref, t, hc):
        return ref.at[pl.ds(pl.multiple_of(hc * halves[t], 8), halves[t]), :]

    def mine_load(t):
        return pltpu.make_async_copy(half3(g_refs[t], t, c), accs[t], local_sems.at[t])

    def to_sibling(t, to):
        return remote(t, half3(g_refs[t], t, 1 - c), sibs[t], to)

    def to_chip(t, j, chip, to):
        idx = 2 * chip[0] + chip[1]
        return remote(n + 3 * t + j, sbufs[t].at[idx], rbufs[t].at[mine if to is not me else idx], to)

    def swap(t, hc, to):
        return remote(4 * n + t, half2(f_refs[t], t, hc), half2(f_refs[t], t, hc), to)

    def load():
        for t in range(n):
            mine_load(t).start()
            to_sibling(t, sib).start()

    def partial():
        for t in range(n):
            mine_load(t).wait()
            to_sibling(t, me).wait_recv()
            for k in range(N_SHARD):
                accs[t][k] = accs[t][k] + sibs[t][k]
            for j, chip in enumerate(chips):
                idx = 2 * chip[0] + chip[1]
                sbufs[t][idx] = accs[t][idx].astype(BF16)
                to_chip(t, j, chip, (*chip, c)).start()

    def total():
        for t in range(n):
            acc = accs[t][mine]
            for j, chip in enumerate(chips):
                to_chip(t, j, chip, me).wait_recv()
                acc = acc + rbufs[t][2 * chip[0] + chip[1]].astype(F32)
            half2(f_refs[t], t, c)[...] = acc
            swap(t, c, sib).start()

    def finish():
        for t in range(n):
            swap(t, 1 - c, me).wait_recv()
        for t in range(n):
            to_sibling(t, sib).wait_send()
            for j, chip in enumerate(chips):
                to_chip(t, j, chip, (*chip, c)).wait_send()
            swap(t, c, sib).wait_send()

    return load, partial, total, finish


def _reduce_scatter_grads(gsh, vec):
    n = len(gsh)
    halves = [a.shape[1] // 2 for a in gsh]

    def body(*refs):
        g_refs, vec_ref, f_refs, vsum_ref = refs[:n], refs[n], refs[n + 1:2 * n + 1], refs[2 * n + 1]
        scratch = refs[2 * n + 2:]
        vrecv, vsend_sems, vrecv_sems = scratch[4 * n + 3:]
        load, partial, total, finish = _reduce_steps(halves, g_refs, f_refs, scratch)
        x, y, c = _place()
        my_dev = 4 * x + 2 * y + c

        def flip(k):
            return x ^ ((k >> 2) & 1), y ^ ((k >> 1) & 1), c ^ (k & 1)

        def vcopy(k, slot, to):
            return pltpu.make_async_remote_copy(src_ref=vec_ref, dst_ref=vrecv.at[slot], send_sem=vsend_sems.at[k - 1],
                                                recv_sem=vrecv_sems.at[k - 1], device_id=to, device_id_type=MESH)

        load()
        vrecv[my_dev] = vec_ref[...]
        for k in range(1, 8):
            vcopy(k, my_dev, flip(k)).start()
        partial()
        total()
        finish()
        for k in range(1, 8):
            fx, fy, fc = flip(k)
            vcopy(k, 4 * fx + 2 * fy + fc, (x, y, c)).wait_recv()
        vs = vrecv[0]
        for d in range(1, 8):
            vs = vs + vrecv[d]
        vsum_ref[...] = vs
        for k in range(1, 8):
            vcopy(k, my_dev, flip(k)).wait_send()

    return pl.pallas_call(
        body, name="reduce_scatter_grads",
        out_shape=tuple(jax.ShapeDtypeStruct(a.shape[1:], F32) for a in gsh) + (jax.ShapeDtypeStruct((VEC_ROWS, 1024), F32),),
        in_specs=[pl.BlockSpec(memory_space=pl.ANY)] * n + [pl.BlockSpec(memory_space=pltpu.VMEM)],
        out_specs=(pl.BlockSpec(memory_space=pltpu.VMEM),) * (n + 1),
        scratch_shapes=_reduce_scratch(gsh) + [pltpu.VMEM((8, VEC_ROWS, 1024), F32), pltpu.SemaphoreType.DMA((7,)),
                                               pltpu.SemaphoreType.DMA((7,))],
        compiler_params=pltpu.CompilerParams(vmem_limit_bytes=56 * 1024 * 1024),
    )(*gsh, vec)


def _adamw(w, g, m, v):
    rows, cols = w.shape
    tr = rows if rows <= 256 else 256
    flip = cols % LANES != 0

    def body(w_ref, g_ref, m_ref, v_ref, g_out, d_ref, nm_ref, nv_ref):
        gv = g_ref[...].T if flip else g_ref[...]
        outs = (gv,) + _adam_math(w_ref[...], gv, m_ref[...], v_ref[...])
        for ref, val in zip((g_out, d_ref, nm_ref, nv_ref), outs):
            ref[...] = val

    spec = pl.BlockSpec((tr, cols), lambda i: (i, 0))
    tspec = pl.BlockSpec((cols, tr), lambda i: (0, i)) if flip else spec
    shp = jax.ShapeDtypeStruct((cols, rows) if flip else (rows, cols), F32)
    if flip:
        w, m, v = w.T, m.T, v.T
    outs = pl.pallas_call(body, name="adamw", grid=(rows // tr,), out_shape=(shp,) * 4,
                          in_specs=[tspec, spec, tspec, tspec], out_specs=(tspec,) * 4)(w, g, m, v)
    return tuple(o.T for o in outs) if flip else outs


def _adam_math(w, g, m, v):
    m2 = ADAM_B1 * m + (1.0 - ADAM_B1) * g
    v2 = ADAM_B2 * v + (1.0 - ADAM_B2) * (g * g)
    m_hat = m2 / (1.0 - ADAM_B1 ** ADAM_STEP)
    v_hat = v2 / (1.0 - ADAM_B2 ** ADAM_STEP)
    return -ADAM_LR * (m_hat / (jnp.sqrt(v_hat) + ADAM_EPS) + ADAM_WD * w), m2, v2


def _adamw_small(vsum, w, m, v):
    names = [name for name, _, _, _ in _VEC_LAYOUT]
    k = len(names)

    def body(*refs):
        vs_ref, w_refs, m_refs, v_refs = refs[0], refs[1:1 + k], refs[1 + k:1 + 2 * k], refs[1 + 2 * k:1 + 3 * k]
        outs = refs[1 + 3 * k:]
        for idx, (_, r, c0, width) in enumerate(_VEC_LAYOUT):
            gv = vs_ref[pl.ds(r, 1), pl.ds(c0, width)]
            d, m2, v2 = _adam_math(w_refs[idx][...], gv, m_refs[idx][...], v_refs[idx][...])
            outs[idx][...], outs[k + idx][...], outs[2 * k + idx][...], outs[3 * k + idx][...] = gv, d, m2, v2

    shapes = tuple(jax.ShapeDtypeStruct(w[name].shape, F32) for name in names)
    res = pl.pallas_call(
        body, name="adamw_small", out_shape=shapes * 4,
        in_specs=[pl.BlockSpec(memory_space=pltpu.VMEM)] * (1 + 3 * k), out_specs=(pl.BlockSpec(memory_space=pltpu.VMEM),) * (4 * k),
    )(vsum, *[w[name] for name in names], *[m[name] for name in names], *[v[name] for name in names])
    return tuple({name: res[part * k + idx] for idx, name in enumerate(names)} for part in range(4))


_EARLY = ("w_in", "w_uq", "w_ukv")
_LATE = ("w_out", "w_ple", "w_ple_gate")
_BIG = _EARLY + _LATE
_KR_LOCAL = 2432 - 3 * (D_IN // N_SHARD)


def _extend_early(parts):
    cols = lambda a: a.transpose(1, 0, 2).reshape(a.shape[1], N_SHARD * a.shape[2])
    g = parts["w_in"]
    zeros = lambda n: jnp.zeros((D_MODEL, n), g.dtype)
    win_ext = jnp.concatenate([g[0], g[1], g[2], g[3][:, :_KR_LOCAL], zeros(64), g[3][:, _KR_LOCAL:_KR_LOCAL + QK_ROPE],
                               zeros(32), g[3][:, _KR_LOCAL + QK_ROPE:]], axis=1)
    wuq_ext = jnp.pad(cols(parts["w_uq"]).reshape(Q_LORA, 8, 96), ((0, 0), (0, 0), (0, 32))).reshape(Q_LORA, 1024)
    wukv = cols(parts["w_ukv"]).reshape(KV_LORA, 8, 128)
    wk_ext = jnp.pad(wukv[:, :, :64], ((0, 0), (0, 0), (0, 64))).reshape(KV_LORA, 1024)
    wv = wukv[:, :, 64:].reshape(KV_LORA, 512)
    return win_ext, wuq_ext, wk_ext, wv


def _shard_cols(a):
    return a.reshape(a.shape[0], N_SHARD, a.shape[1] // N_SHARD).transpose(1, 0, 2)


def _shard_rows(a):
    return a.reshape(N_SHARD, a.shape[0] // N_SHARD, a.shape[1])


def _shard_early_grads(dwin_ext, dwuq_ext, dwk_ext, dwv):
    e, w = dwin_ext, D_IN // N_SHARD
    last = jnp.concatenate([e[:, 3 * w:2432], e[:, 2496:2528], e[:, 2560:]], axis=1)
    dwuq = dwuq_ext.reshape(Q_LORA, 8, 128)[:, :, :96].reshape(Q_LORA, 768)
    dwukv = jnp.concatenate([dwk_ext.reshape(KV_LORA, 8, 128)[:, :, :64], dwv.reshape(KV_LORA, 8, 64)], axis=2)
    return [jnp.stack([e[:, 0:w], e[:, w:2 * w], e[:, 2 * w:3 * w], last]), _shard_cols(dwuq),
            _shard_cols(dwukv.reshape(KV_LORA, 1024))]


def _rope_tables(positions):
    half = QK_ROPE // 2
    freq = ROPE_THETA ** (-jnp.arange(half, dtype=F32) / half)
    s = positions.shape[0]
    per = LANES // half
    ang = jnp.repeat(positions.astype(F32).reshape(s // per, per), half, axis=1) * jnp.tile(freq, per)
    cos, sin = lax.optimization_barrier((jnp.cos(ang), jnp.sin(ang)))
    cos, sin = cos.reshape(s, half), sin.reshape(s, half)
    z = lambda n: jnp.zeros((s, n), F32)
    c_t = jnp.concatenate([jnp.ones((s, 64), F32), cos, cos, z(32)], axis=1)
    sa_t = jnp.concatenate([z(64), -sin, z(16), z(32)], axis=1)
    sb_t = jnp.concatenate([z(64), z(16), sin, z(32)], axis=1)
    return c_t, sa_t, sb_t


def _local_grads(x, p, positions, tgt, gains, early, late):
    win_ext, wuq_ext, wk_ext, wv = _extend_early(early)
    tabs = _rope_tables(positions)
    g = gains
    sbq, sbk, sbv, sbg, mlag, cq, ckv, qc, kc, mv, sbkt, sbvt, kct, mvt = _pre_fwd(
        x, tabs, g["norm_pre_g"], win_ext, g["q_norm_g"], wuq_ext, g["kv_norm_g"], wk_ext, wv)
    sbo, wout4, wple4, wpg4 = _sb_fwd(sbq, sbk, sbvt, late)
    wout, wpg = wout4.reshape(D_MODEL, D_MODEL), wpg4.reshape(D_MODEL, D_MODEL)
    mlao, lse = _mla_fwd(qc, kc, mvt)
    dsbo, dmlao, delta, dsbg, dmlag, dxres, dwout, dwpg, dwple, vec_c = _post(
        x, p, tgt, sbo, mlao, sbg, mlag, g["sb_out_norm_g"], g["mla_out_norm_g"], wout, g["norm_post_g"], wple4,
        g["ple_norm_g"], wpg, g["b_ple_gate"])
    dsbq, dsbk, dsbv, *late_grads = _sb_bwd(sbq, sbk, sbkt, sbv, dsbo, [_shard_rows(dwout), dwple, _shard_rows(dwpg)])
    dqc, dkc, dmv = _mla_bwd(qc, kc, kct, mv, dmlao, lse, delta)
    gx, dwin_ext, dwuq_ext, dwk_ext, dwv, vec_d = _pre_bwd(
        x, dxres, dsbq, dsbk, dsbv, dsbg, dmlag, dqc, dkc, dmv, cq, ckv, tabs, g["norm_pre_g"], win_ext, g["q_norm_g"],
        wuq_ext, g["kv_norm_g"], wk_ext, wv)
    return gx, _shard_early_grads(dwin_ext, dwuq_ext, dwk_ext, dwv), late_grads, jnp.concatenate([vec_c, vec_d], axis=0)


_VEC_LAYOUT = (("norm_post_g", 0, 0, 1024), ("ple_norm_g", 1, 0, 1024), ("b_ple_gate", 2, 0, 1024), ("sb_out_norm_g", 3, 0, 512),
               ("mla_out_norm_g", 3, 512, 512), ("norm_pre_g", 8, 0, 1024), ("q_norm_g", 9, 0, 256), ("kv_norm_g", 9, 256, 128))
_LOSS_ROW = 4
_WEIGHT_ORDER = ("norm_pre_g", "w_in", "q_norm_g", "w_uq", "kv_norm_g", "w_ukv", "sb_out_norm_g", "mla_out_norm_g", "w_out",
                 "norm_post_g", "w_ple", "ple_norm_g", "w_ple_gate", "b_ple_gate")


def kernel(x, p, positions, norm_pre_g, w_in, q_norm_g, w_uq, kv_norm_g, w_ukv, sb_out_norm_g, mla_out_norm_g, w_out, norm_post_g, w_ple, ple_norm_g, w_ple_gate, b_ple_gate, loss_target, m_norm_pre_g, m_w_in, m_q_norm_g, m_w_uq, m_kv_norm_g, m_w_ukv, m_sb_out_norm_g, m_mla_out_norm_g, m_w_out, m_norm_post_g, m_w_ple, m_ple_norm_g, m_w_ple_gate, m_b_ple_gate, v_norm_pre_g, v_w_in, v_q_norm_g, v_w_uq, v_kv_norm_g, v_w_ukv, v_sb_out_norm_g, v_mla_out_norm_g, v_w_out, v_norm_post_g, v_w_ple, v_ple_norm_g, v_w_ple_gate, v_b_ple_gate):
    w = {"norm_pre_g": norm_pre_g, "w_in": w_in[0], "q_norm_g": q_norm_g, "w_uq": w_uq[0], "kv_norm_g": kv_norm_g, "w_ukv": w_ukv[0],
         "sb_out_norm_g": sb_out_norm_g, "mla_out_norm_g": mla_out_norm_g, "w_out": w_out[0], "norm_post_g": norm_post_g,
         "w_ple": w_ple[0], "ple_norm_g": ple_norm_g, "w_ple_gate": w_ple_gate[0], "b_ple_gate": b_ple_gate}
    m = {"norm_pre_g": m_norm_pre_g, "w_in": m_w_in[0], "q_norm_g": m_q_norm_g, "w_uq": m_w_uq[0], "kv_norm_g": m_kv_norm_g,
         "w_ukv": m_w_ukv[0], "sb_out_norm_g": m_sb_out_norm_g, "mla_out_norm_g": m_mla_out_norm_g, "w_out": m_w_out[0],
         "norm_post_g": m_norm_post_g, "w_ple": m_w_ple[0], "ple_norm_g": m_ple_norm_g, "w_ple_gate": m_w_ple_gate[0],
         "b_ple_gate": m_b_ple_gate}
    v = {"norm_pre_g": v_norm_pre_g, "w_in": v_w_in[0], "q_norm_g": v_q_norm_g, "w_uq": v_w_uq[0], "kv_norm_g": v_kv_norm_g,
         "w_ukv": v_w_ukv[0], "sb_out_norm_g": v_sb_out_norm_g, "mla_out_norm_g": v_mla_out_norm_g, "w_out": v_w_out[0],
         "norm_post_g": v_norm_post_g, "w_ple": v_w_ple[0], "ple_norm_g": v_ple_norm_g, "w_ple_gate": v_w_ple_gate[0],
         "b_ple_gate": v_b_ple_gate}
    gathered = _allgather_weights([w[n] for n in _EARLY])
    gx, early_grads, late_red, vec = _local_grads(x[0], p[0, 0], positions[0], loss_target[0], w, dict(zip(_EARLY, gathered)),
                                                  [w[n] for n in _LATE])
    *early_red, vsum = _reduce_scatter_grads(early_grads, vec)
    gred = early_red + late_red
    loss = vsum[_LOSS_ROW, 0]

    g, delta, new_m, new_v = _adamw_small(vsum, w, m, v)
    for n, gn in zip(_BIG, gred):
        g[n], delta[n], new_m[n], new_v[n] = _adamw(w[n], gn, m[n], v[n])

    lead = lambda n, a: a[None] if n in _BIG else a
    return (loss, gx[None],
            *[lead(n, g[n]) for n in _WEIGHT_ORDER], *[lead(n, delta[n]) for n in _WEIGHT_ORDER],
            *[lead(n, new_m[n]) for n in _WEIGHT_ORDER], *[lead(n, new_v[n]) for n in _WEIGHT_ORDER])
```

```python
import numpy as np
import jax
import jax.numpy as jnp
from jax import lax
from jax.experimental import pallas as pl
from jax.experimental.pallas import tpu as pltpu

F32 = jnp.float32
BF16 = jnp.bfloat16
MESH = pl.DeviceIdType.MESH

D_MODEL = 1024
HEAD_DIM = 64
D_SB = 512
D_MLA = 512
Q_LORA = 256
KV_LORA = 128
QK_NOPE = 64
QK_ROPE = 32
PLE_DIM = 256
D_IN = 2976
D_EXT = 3072
ROPE_THETA = 10000.0
EPS = 1e-6
N_SHARD = 4

ADAM_LR = 0.001
ADAM_B1 = 0.9
ADAM_B2 = 0.999
ADAM_EPS = 1e-08
ADAM_WD = 0.01
ADAM_STEP = 10

LANES = 128
BK = 128
WQ = 256
SB_NSUB = 8
MQ_FWD = 4096
MQ_BWD = 1024
MLA_CW = 256
SB_CUTOFF = 120.0
TM = 256
TM_PRE = 256
VEC_ROWS = 16
VMEM_DENSE = 52 * 1024 * 1024
VMEM_ATTN = 40 * 1024 * 1024


def _mm(a, b):
    return jnp.dot(a, b, preferred_element_type=F32)


def _mm_nt(a, b):
    return lax.dot_general(a, b, (((1,), (1,)), ((), ())), preferred_element_type=F32)


def _mm_tn(a, b):
    return lax.dot_general(a, b, (((0,), (0,)), ((), ())), preferred_element_type=F32)


def _seg(a, bd2):
    return _mm(_split2(a), bd2)


def _const(mask):
    return jnp.asarray(np.asarray(mask, np.float32), dtype=BF16)


def _blockdiag2(n, seg):
    r = (np.arange(2 * n)[:, None] % n) // seg
    c = np.arange(n)[None, :] // seg
    return _const(r == c)


def _sigmoid(a):
    return 1.0 / (1.0 + jnp.exp(-a))


def _rowmean(a):
    return jnp.mean(a, axis=-1, keepdims=True)


def _colsum(a):
    return jnp.sum(a, axis=0, keepdims=True)


def _rope_fwd(a, c, sa, sb):
    w = a.shape[-1]
    return a * c + pltpu.roll(a, w - 16, 1) * sa + pltpu.roll(a, 16, 1) * sb


def _rope_bwd(g, c, sa, sb):
    w = g.shape[-1]
    return g * c + pltpu.roll(g * sa, 16, 1) + pltpu.roll(g * sb, w - 16, 1)


def _full(shape):
    return pl.BlockSpec(shape, lambda *_: (0,) * len(shape))


def _acc(shape):
    return pl.BlockSpec(shape, lambda *_: (0,) * len(shape))


def _full2(shape):
    return pl.BlockSpec(shape, lambda p, i: (0, 0))


def _cols(height, tm=TM):
    return pl.BlockSpec((height, tm), lambda i: (0, i))


def _rows(width, tm=TM):
    return pl.BlockSpec((tm, width), lambda i: (i, 0))


def _pre_fwd(x, tabs, gpre, win, gq, wuq, gkv, wk, wv):
    s = x.shape[0]
    c_t, sa_t, sb_t = tabs
    rw, cl = (lambda width: _rows(width, TM_PRE)), (lambda height: _cols(height, TM_PRE))

    def body(x_ref, c_ref, sa_ref, sb_ref, gpre_ref, win_ref, gq_ref, wuq_ref, gkv_ref, wk_ref, wv_ref,
             sbq_ref, sbk_ref, sbv_ref, sbg_ref, mlag_ref, cq_ref, ckv_ref, qc_ref, kc_ref, mv_ref,
             sbkt_ref, sbvt_ref, kct_ref, mvt_ref):
        xv = x_ref[...]
        r1 = lax.rsqrt(_rowmean(xv * xv) + EPS)
        h = (xv * r1 * gpre_ref[...]).astype(BF16)
        proj = _mm(h, win_ref[...])
        sbq_ref[...] = proj[:, 0:512].astype(BF16)
        sbk_ref[...] = proj[:, 512:1024].astype(BF16)
        sbv_ref[...] = proj[:, 1024:1536].astype(BF16)
        sbkt_ref[...] = proj[:, 512:1024].T.astype(BF16)
        sbvt_ref[...] = proj[:, 1024:1536].T.astype(BF16)
        sbg_ref[...] = proj[:, 1536:2048]
        cq = proj[:, 2048:2304]
        ckv = proj[:, 2304:2432]
        kr = proj[:, 2432:2560]
        mlag_ref[...] = proj[:, 2560:3072]
        cq_ref[...] = cq
        ckv_ref[...] = ckv
        c1, sa1, sb1 = c_ref[...], sa_ref[...], sb_ref[...]
        c8, sa8, sb8 = jnp.tile(c1, (1, 8)), jnp.tile(sa1, (1, 8)), jnp.tile(sb1, (1, 8))
        cqn = (cq * lax.rsqrt(_rowmean(cq * cq) + EPS) * gq_ref[...]).astype(BF16)
        qe = _mm(cqn, wuq_ref[...])
        qc_ref[...] = _rope_fwd(qe, c8, sa8, sb8).astype(BF16)
        ckvn = (ckv * lax.rsqrt(_rowmean(ckv * ckv) + EPS) * gkv_ref[...]).astype(BF16)
        ke = _mm(ckvn, wk_ref[...])
        krr = _rope_fwd(kr, c1, sa1, sb1)
        kcat = ke + jnp.tile(krr, (1, 8))
        kc_ref[...] = kcat.astype(BF16)
        kct_ref[...] = kcat.T.astype(BF16)
        mval = _mm(ckvn, wv_ref[...])
        mv_ref[...] = mval.astype(BF16)
        mvt_ref[...] = mval.T.astype(BF16)

    out_shape = (
        jax.ShapeDtypeStruct((s, 512), BF16), jax.ShapeDtypeStruct((s, 512), BF16), jax.ShapeDtypeStruct((s, 512), BF16),
        jax.ShapeDtypeStruct((s, 512), F32), jax.ShapeDtypeStruct((s, 512), F32),
        jax.ShapeDtypeStruct((s, Q_LORA), F32), jax.ShapeDtypeStruct((s, KV_LORA), F32),
        jax.ShapeDtypeStruct((s, 1024), BF16), jax.ShapeDtypeStruct((s, 1024), BF16), jax.ShapeDtypeStruct((s, 512), BF16),
        jax.ShapeDtypeStruct((512, s), BF16), jax.ShapeDtypeStruct((512, s), BF16), jax.ShapeDtypeStruct((1024, s), BF16),
        jax.ShapeDtypeStruct((512, s), BF16),
    )
    return pl.pallas_call(
        body, name="pre_fwd", grid=(s // TM_PRE,), out_shape=out_shape,
        in_specs=[rw(D_MODEL), rw(LANES), rw(LANES), rw(LANES), _full((1, D_MODEL)), _full((D_MODEL, D_EXT)),
                  _full((1, Q_LORA)), _full((Q_LORA, 1024)), _full((1, KV_LORA)), _full((KV_LORA, 1024)), _full((KV_LORA, 512))],
        out_specs=(rw(512), rw(512), rw(512), rw(512), rw(512), rw(Q_LORA), rw(KV_LORA),
                   rw(1024), rw(1024), rw(512), cl(512), cl(512), cl(1024), cl(512)),
        compiler_params=pltpu.CompilerParams(vmem_limit_bytes=VMEM_DENSE),
    )(x, c_t, sa_t, sb_t, gpre, win, gq, wuq, gkv, wk, wv)


def _softplus(z):
    neg_abs = lax.bitcast_convert_type(lax.bitcast_convert_type(z, jnp.uint32) | jnp.uint32(0x80000000), F32)
    return jnp.maximum(z, 0.0) + jnp.log(1.0 + jnp.exp(neg_abs))


def _sum_matrix(kind, terms):
    r, c = np.arange(2 * BK)[:, None], np.arange(2 * BK * terms)[None, :] % (2 * BK)
    rk, ck = r % BK, c % BK
    return _const(((r // BK) == (c // BK)) & {"suffix": ck >= rk, "prefix": ck <= rk}[kind])


def _split_rows(a):
    hi = a.astype(BF16)
    return jnp.concatenate([hi, (a - hi.astype(F32)).astype(BF16)], axis=0)


def _heads_t(blk, rowi):
    zero = jnp.zeros_like(blk)
    return jnp.concatenate([jnp.where(rowi < 64, blk, zero), jnp.where(rowi >= 64, blk, zero)], axis=1)


def _mask_keys(a, valid, fill=0.0):
    return jnp.concatenate([jnp.where(valid, a[0:BK], fill), jnp.where(valid, a[BK:2 * BK], fill)], axis=0)


def _split2(a):
    hi = a.astype(BF16)
    lo = (a - hi.astype(F32)).astype(BF16)
    return jnp.concatenate([hi, lo], axis=1)


def _pair_stack(b, lane):
    zero = jnp.zeros_like(b)
    return jnp.concatenate([jnp.where(lane < 64, b, zero), jnp.where(lane >= 64, b, zero)], axis=0)


def _sb_fwd(q, k, vt, late):
    s = q.shape[0]
    n = len(late)

    def body(q_ref, k_ref, vt_ref, usuf_ref, *rest):
        ins, o_ref, outs = rest[:n], rest[n], rest[n + 1:2 * n + 1]
        acc_scr, run_scr = rest[2 * n + 1:2 * n + 3]
        bufs, (send_sems, recv_sems, out_sems) = rest[2 * n + 3:3 * n + 3], rest[3 * n + 3:]
        p, i = pl.program_id(0), pl.program_id(1)
        gather_start, gather_forward, gather_finish = _gather_steps([a.shape for a in late], ins, bufs, send_sems, recv_sems)

        @pl.when((p == 0) & (i == 0))
        def _():
            gather_start()

        @pl.when((p == 2) & (i == 0))
        def _():
            gather_forward()

        lane = lax.broadcasted_iota(jnp.int32, (1, LANES), 1)
        rowi = lax.broadcasted_iota(jnp.int32, (LANES, 1), 0)
        keyi = lax.broadcasted_iota(jnp.int32, (BK, WQ), 0)

        def group(i, qs, blocks, masked, seen=None):
            seen = seen or [0] * len(blocks)
            qryi = lax.broadcasted_iota(jnp.int32, (BK, WQ), 1) + i * WQ
            starts = [pl.multiple_of(j * BK, BK) for j in blocks]
            valid = [(keyi[:, lo:] + j * BK) < qryi[:, lo:] if m else None for j, m, lo in zip(blocks, masked, seen)]
            zs = [_mm_nt(_pair_stack(k_ref[pl.ds(ks, BK), :], lane), qs[lo:]) for ks, lo in zip(starts, seen)]
            sps = [_softplus(z) for z in zs]
            sps = [sp if ok is None else _mask_keys(sp, ok) for sp, ok in zip(sps, valid)]
            cums = [_mm(usuf_ref[...], _split_rows(sp)) for sp in sps]
            ws = [jnp.exp(z - c) for z, c in zip(zs, cums)]
            ws = [w if ok is None else _mask_keys(w, ok) for w, ok in zip(ws, valid)]
            pvs = [_mm(_heads_t(vt_ref[:, pl.ds(ks, BK)], rowi), w.astype(BF16)) for ks, w in zip(starts, ws)]
            for pv, c, lo in zip(pvs, cums, seen):
                r0, r1 = run_scr[0:1, lo:], run_scr[1:2, lo:]
                acc_scr[:, lo:] += jnp.where(rowi < 64, jnp.exp(-r0), jnp.exp(-r1)) * pv
                run_scr[0:1, lo:] = r0 + c[0:1]
                run_scr[1:2, lo:] = r1 + c[BK:BK + 1]

        assert WQ == 2 * BK

        def unfinished():
            return (jnp.min(run_scr[0:2, :]) < SB_CUTOFF).astype(jnp.int32)

        def query_block(sub):
            i = pl.program_id(1) * SB_NSUB + sub
            rows = pl.ds(pl.multiple_of(sub * WQ, WQ), WQ)
            qs = q_ref[rows, :] * (HEAD_DIM ** -0.5)
            acc_scr[...] = jnp.zeros_like(acc_scr)
            run_scr[...] = jnp.zeros_like(run_scr)

            @pl.when(i == 0)
            def _():
                group(i, qs, [1, 0], [True, True], [BK, 0])

            @pl.when(i > 0)
            def _():
                group(i, qs, [2 * i + 1, 2 * i, 2 * i - 1, 2 * i - 2], [True, True, False, False], [BK, 0, 0, 0])

            def step(c):
                group(i, qs, [2 * i - 1 - 2 * c[0], 2 * i - 2 - 2 * c[0]], [False, False])
                return c[0] + 1, unfinished()

            lax.while_loop(lambda c: (c[0] < i) & (c[1] > 0), step, (jnp.int32(1), unfinished()))
            o_ref[rows, :] = acc_scr[...].T

        lax.fori_loop(0, SB_NSUB, lambda sub, c: (query_block(sub), c)[1], 0)

        @pl.when((p == pl.num_programs(0) - 1) & (i == pl.num_programs(1) - 1))
        def _():
            gather_finish()
            copies = [pltpu.make_async_copy(bufs[t], outs[t], out_sems.at[t]) for t in range(n)]
            for cp in copies:
                cp.start()
            for cp in copies:
                cp.wait()

    qspec = pl.BlockSpec((SB_NSUB * WQ, LANES), lambda p, i: (i, p))
    kspec = pl.BlockSpec((s, LANES), lambda p, i: (0, p))
    tspec = pl.BlockSpec((LANES, s), lambda p, i: (p, 0))
    gathered = [jax.ShapeDtypeStruct((N_SHARD,) + a.shape, BF16) for a in late]
    return pl.pallas_call(
        body, name="sb_fwd", grid=(4, s // (SB_NSUB * WQ)),
        out_shape=(jax.ShapeDtypeStruct((s, 512), F32), *gathered),
        in_specs=[qspec, kspec, tspec, _full2((2 * BK, 4 * BK))] + [_full2(a.shape) for a in late],
        out_specs=(qspec,) + (pl.BlockSpec(memory_space=pl.ANY),) * n,
        scratch_shapes=[pltpu.VMEM((LANES, WQ), F32), pltpu.VMEM((8, WQ), F32)] + [pltpu.VMEM(g.shape, BF16) for g in gathered]
                       + [pltpu.SemaphoreType.DMA((6 * n,)), pltpu.SemaphoreType.DMA((6 * n,)), pltpu.SemaphoreType.DMA((n,))],
        compiler_params=pltpu.CompilerParams(vmem_limit_bytes=VMEM_ATTN),
    )(q, k, vt, _sum_matrix("suffix", 2), *late)


def _sb_bwd(q, k, kt, v, do, late):
    s = q.shape[0]
    n = len(late)
    halves = [a.shape[1] // 2 for a in late]

    def body(q_ref, k_ref, kt_ref, v_ref, do_ref, usuf_ref, upre_ref, *rest):
        g_refs, (dq_ref, dk_ref, dv_ref), outs = rest[:n], rest[n:n + 3], rest[n + 3:2 * n + 3]
        later_scr, dqt_scr, st_scr = rest[2 * n + 3:2 * n + 6]
        f_scr, reduce_scr, out_sems = rest[2 * n + 6:3 * n + 6], rest[3 * n + 6:-1], rest[-1]
        p, i = pl.program_id(0), pl.program_id(1)
        reduce_load, reduce_partial, reduce_total, reduce_finish = _reduce_steps(halves, g_refs, f_scr, reduce_scr)

        @pl.when((p == 0) & (i == 0))
        def _():
            reduce_load()

        @pl.when((p == 1) & (i == 0))
        def _():
            reduce_partial()

        @pl.when((p == 3) & (i == 0))
        def _():
            reduce_total()

        @pl.when(i == 0)
        def _():
            dk_ref[...] = jnp.zeros_like(dk_ref)
            dv_ref[...] = jnp.zeros_like(dv_ref)

        lane = lax.broadcasted_iota(jnp.int32, (1, LANES), 1)
        rowi = lax.broadcasted_iota(jnp.int32, (LANES, 1), 0)
        keyi = lax.broadcasted_iota(jnp.int32, (BK, WQ), 0)
        assert WQ == 2 * BK

        def query_block(sub):
            i = pl.program_id(1) * SB_NSUB + sub
            rows = pl.ds(pl.multiple_of(sub * WQ, WQ), WQ)
            qryi = lax.broadcasted_iota(jnp.int32, (BK, WQ), 1) + i * WQ
            qs = q_ref[rows, :] * (HEAD_DIM ** -0.5)
            dob = do_ref[rows, :]
            dot = dob.astype(F32).T.astype(BF16)

            def scores(j, lo=0):
                return _mm_nt(_pair_stack(k_ref[pl.ds(pl.multiple_of(j * BK, BK), BK), :], lane), qs[lo:])

            def scan(blocks, masked, seen=None):
                seen = seen or [0] * len(blocks)
                sps = [_softplus(scores(j, lo)) for j, lo in zip(blocks, seen)]
                sps = [_mask_keys(sp, (keyi[:, lo:] + j * BK) < qryi[:, lo:]) if m else sp
                       for sp, j, m, lo in zip(sps, blocks, masked, seen)]
                for sp, j, lo in zip(sps, blocks, seen):
                    run = st_scr[0:2, :]
                    later_scr[j, 0:2, :] = run
                    st_scr[0:2, lo:] = run[:, lo:] + jnp.concatenate([jnp.sum(sp[0:BK], axis=0, keepdims=True),
                                                                      jnp.sum(sp[BK:2 * BK], axis=0, keepdims=True)], axis=0)

            def sweep(blocks, masked, seen=None):
                seen = seen or [0] * len(blocks)
                starts = [pl.multiple_of(j * BK, BK) for j in blocks]
                valid = [(keyi[:, lo:] + j * BK) < qryi[:, lo:] if m else None for j, m, lo in zip(blocks, masked, seen)]
                zs = [scores(j, lo) for j, lo in zip(blocks, seen)]
                us = [jnp.exp(lax.bitcast_convert_type(lax.bitcast_convert_type(z, jnp.uint32) | jnp.uint32(0x80000000), F32))
                      for z in zs]
                sps = [jnp.maximum(z, 0.0) + jnp.log(1.0 + u) for z, u in zip(zs, us)]
                sps = [sp if ok is None else _mask_keys(sp, ok) for sp, ok in zip(sps, valid)]
                sigs = [jnp.where(z >= 0.0, 1.0, u) / (1.0 + u) for z, u in zip(zs, us)]
                cums = [_mm(usuf_ref[...], _split_rows(sp)) for sp in sps]
                dws = [_mm(_pair_stack(v_ref[pl.ds(ks, BK), :], lane), dot[:, lo:]) for ks, lo in zip(starts, seen)]
                wfs = []
                for z, c, j, ok, lo in zip(zs, cums, blocks, valid, seen):
                    f = jnp.exp(-later_scr[j, 0:2, lo:])
                    wide = (BK, WQ - lo)
                    wf = jnp.exp(z - c) * jnp.concatenate([jnp.broadcast_to(f[0:1], wide), jnp.broadcast_to(f[1:2], wide)], axis=0)
                    wfs.append(wf if ok is None else _mask_keys(wf, ok))
                es = [dw * wf for dw, wf in zip(dws, wfs)]
                pres = [_mm(upre_ref[...], e.astype(BF16)) for e in es]
                dzs = []
                for e, pre, sig, ok, lo in zip(es, pres, sigs, valid, seen):
                    e0 = pre[0:BK] + st_scr[0:1, lo:]
                    e1 = pre[BK:2 * BK] + st_scr[1:2, lo:]
                    st_scr[0:1, lo:] = e0[BK - 1:BK]
                    st_scr[1:2, lo:] = e1[BK - 1:BK]
                    dz = e - sig * jnp.concatenate([e0, e1], axis=0)
                    dzs.append((dz if ok is None else _mask_keys(dz, ok)).astype(BF16))
                whole = [b for b, lo in enumerate(seen) if lo == 0]
                dqt_scr[...] += _mm(jnp.concatenate([_heads_t(kt_ref[:, pl.ds(starts[b], BK)], rowi) for b in whole], axis=1),
                                    jnp.concatenate([dzs[b] for b in whole], axis=0))
                for b, lo in enumerate(seen):
                    if lo:
                        dqt_scr[:, lo:] += _mm(_heads_t(kt_ref[:, pl.ds(starts[b], BK)], rowi), dzs[b])
                for ks, dz, wf, lo in zip(starts, dzs, wfs, seen):
                    rk = _mm(dz, qs[lo:])
                    dk_ref[pl.ds(ks, BK), :] += jnp.where(lane < 64, rk[0:BK], rk[BK:2 * BK])
                    rv = _mm(wf.astype(BF16), dob[lo:])
                    dv_ref[pl.ds(ks, BK), :] += jnp.where(lane < 64, rv[0:BK], rv[BK:2 * BK])

            st_scr[...] = jnp.zeros_like(st_scr)

            @pl.when(i == 0)
            def _():
                scan([1, 0], [True, True], [BK, 0])

            @pl.when(i > 0)
            def _():
                scan([2 * i + 1, 2 * i, 2 * i - 1, 2 * i - 2], [True, True, False, False], [BK, 0, 0, 0])

            def unfinished():
                return (jnp.min(st_scr[0:2, :]) < SB_CUTOFF).astype(jnp.int32)

            def step(c):
                scan([2 * i - 1 - 2 * c[0], 2 * i - 2 - 2 * c[0]], [False, False])
                return c[0] + 1, unfinished()

            npairs, _ = lax.while_loop(lambda c: (c[0] < i) & (c[1] > 0), step, (jnp.minimum(i, 1), unfinished()))

            st_scr[...] = jnp.zeros_like(st_scr)
            dqt_scr[...] = jnp.zeros_like(dqt_scr)
            first = 2 * (i - npairs)

            def early(t, carry):
                sweep([first + 2 * t, first + 2 * t + 1], [False, False])
                return carry

            lax.fori_loop(0, npairs - 1, early, 0)

            @pl.when(i == 0)
            def _():
                sweep([0, 1], [True, True], [0, BK])

            @pl.when(i > 0)
            def _():
                sweep([2 * i - 2, 2 * i - 1, 2 * i, 2 * i + 1], [False, False, True, True], [0, 0, 0, BK])

            dq_ref[rows, :] = (dqt_scr[...].T * (HEAD_DIM ** -0.5)).astype(BF16)

        lax.fori_loop(0, SB_NSUB, lambda sub, c: (query_block(sub), c)[1], 0)

        @pl.when((p == pl.num_programs(0) - 1) & (i == pl.num_programs(1) - 1))
        def _():
            reduce_finish()
            copies = [pltpu.make_async_copy(f_scr[t], outs[t], out_sems.at[t]) for t in range(n)]
            for cp in copies:
                cp.start()
            for cp in copies:
                cp.wait()

    qspec = pl.BlockSpec((SB_NSUB * WQ, LANES), lambda p, i: (i, p))
    kspec = pl.BlockSpec((s, LANES), lambda p, i: (0, p))
    tspec = pl.BlockSpec((LANES, s), lambda p, i: (p, 0))
    anywhere = pl.BlockSpec(memory_space=pl.ANY)
    reduced = [jax.ShapeDtypeStruct(a.shape[1:], F32) for a in late]
    return pl.pallas_call(
        body, name="sb_bwd", grid=(4, s // (SB_NSUB * WQ)),
        out_shape=(jax.ShapeDtypeStruct((s, 512), BF16), jax.ShapeDtypeStruct((s, 512), F32),
                   jax.ShapeDtypeStruct((s, 512), F32), *reduced),
        in_specs=[qspec, kspec, tspec, kspec, qspec, _full2((2 * BK, 4 * BK)), _full2((2 * BK, 2 * BK))] + [anywhere] * n,
        out_specs=(qspec, kspec, kspec) + (anywhere,) * n,
        scratch_shapes=[pltpu.VMEM((s // BK, 8, WQ), F32), pltpu.VMEM((LANES, WQ), F32), pltpu.VMEM((8, WQ), F32)]
                       + [pltpu.VMEM(r.shape, F32) for r in reduced] + _reduce_scratch(late) + [pltpu.SemaphoreType.DMA((n,))],
        compiler_params=pltpu.CompilerParams(vmem_limit_bytes=VMEM_ATTN),
    )(q, k, kt, v, do, _sum_matrix("suffix", 2), _sum_matrix("prefix", 1), *late)


MLA_SCALE = (QK_NOPE + QK_ROPE) ** -0.5
LOG2E = 1.4426950408889634


def _mla_keys(kb):
    zero = jnp.zeros((BK, LANES), kb.dtype)
    return jnp.concatenate([jnp.concatenate([kb[:, 0:LANES], zero], axis=1),
                            jnp.concatenate([zero, kb[:, LANES:2 * LANES]], axis=1)], axis=0)


def _mla_fwd(qc, kc, vt):
    s = qc.shape[0]
    mq = min(MQ_FWD, s)
    rows_l = 16

    def body(q_ref, k_ref, vt_ref, o_ref, l_ref, p_scr, ot_scr, st_scr):
        i = pl.program_id(1)
        row = lax.broadcasted_iota(jnp.int32, (LANES, 1), 0)
        orow = lax.broadcasted_iota(jnp.int32, (rows_l, 2 * BK), 0)
        ocol = lax.broadcasted_iota(jnp.int32, (rows_l, 2 * BK), 1)
        ones = jnp.where(((orow == 0) & (ocol < BK)) | ((orow == 1) & (ocol >= BK)), 1.0, 0.0).astype(BF16)

        def chunks(lo, hi):
            return [(a, min(a + MLA_CW, hi)) for a in range(lo, hi, MLA_CW)]

        def keys(j):
            return _mla_keys(k_ref[pl.ds(pl.multiple_of(j * BK, BK), BK), :])

        def values_t(j):
            vtb = vt_ref[:, pl.ds(pl.multiple_of(j * BK, BK), BK)]
            zero = jnp.zeros_like(vtb)
            top = jnp.concatenate([jnp.where(row < 64, vtb, zero), jnp.where(row >= 64, vtb, zero)], axis=1)
            return jnp.concatenate([top, ones], axis=0)

        def pair_values(ja):
            return jnp.concatenate([values_t(ja), values_t(ja + 1)], axis=1)

        def softmax(ja, za, zb, masked, a, b):
            c = MLA_SCALE * LOG2E
            parts = [za[0:BK] * c, za[BK:2 * BK] * c, zb[0:BK] * c, zb[BK:2 * BK] * c]
            if masked:
                keyc = lax.broadcasted_iota(jnp.int32, (BK, b - a), 0)
                qryc = (lax.broadcasted_iota(jnp.int32, (BK, b - a), 1) + (i * mq + a)) // 64
                va = ((keyc + ja * BK) // 64) <= qryc
                vb = ((keyc + (ja + 1) * BK) // 64) <= qryc
                parts = [jnp.where(va, parts[0], -1e30), jnp.where(va, parts[1], -1e30),
                         jnp.where(vb, parts[2], -1e30), jnp.where(vb, parts[3], -1e30)]
            m0, m1 = st_scr[0:1, a:b], st_scr[1:2, a:b]
            n0 = jnp.maximum(m0, jnp.max(jnp.maximum(parts[0], parts[2]), axis=0, keepdims=True))
            n1 = jnp.maximum(m1, jnp.max(jnp.maximum(parts[1], parts[3]), axis=0, keepdims=True))
            st_scr[2:3, a:b] = jnp.exp2(m0 - n0)
            st_scr[3:4, a:b] = jnp.exp2(m1 - n1)
            st_scr[0:1, a:b] = n0
            st_scr[1:2, a:b] = n1
            p_scr[:, a:b] = jnp.concatenate([jnp.exp2(parts[0] - n0), jnp.exp2(parts[1] - n1),
                                             jnp.exp2(parts[2] - n0), jnp.exp2(parts[3] - n1)], axis=0).astype(BF16)

        def accumulate(vals, a, b):
            pv = _mm(vals, p_scr[:, a:b])
            f = jnp.where(row < 64, st_scr[2:3, a:b], st_scr[3:4, a:b])
            ot_scr[0:LANES, a:b] = f * ot_scr[0:LANES, a:b] + pv[0:LANES]
            ot_scr[LANES:LANES + 8, a:b] = st_scr[2:10, a:b] * ot_scr[LANES:LANES + 8, a:b] + pv[LANES:LANES + 8]

        def step(n, diag, lo=0, prev_lo=0):
            kab = jnp.concatenate([keys(2 * n), keys(2 * n + 1)], axis=0)
            vals = pair_values(2 * n - 2)
            for a, b in chunks(prev_lo, lo):
                accumulate(vals, a, b)
            for a, b in chunks(lo, mq):
                zab = _mm_nt(kab, q_ref[a:b, :])
                accumulate(vals, a, b)
                softmax(2 * n, zab[0:2 * BK], zab[2 * BK:4 * BK], diag and a < lo + 2 * BK, a, b)

        def first(diag):
            kab = jnp.concatenate([keys(0), keys(1)], axis=0)
            for a, b in chunks(0, mq):
                zab = _mm_nt(kab, q_ref[a:b, :])
                softmax(0, zab[0:2 * BK], zab[2 * BK:4 * BK], diag and a < 2 * BK, a, b)

        st_scr[...] = jnp.concatenate([jnp.full((2, mq), -1e30, F32), jnp.ones((14, mq), F32)], axis=0)
        ot_scr[...] = jnp.zeros_like(ot_scr)

        npq = mq // (2 * BK)
        seen = lambda d: 2 * BK * max(d, 0)

        @pl.when(i == 0)
        def _():
            first(True)
            for d in range(1, npq):
                step(d, True, seen(d), seen(d - 1))

        if s > mq:
            @pl.when(i > 0)
            def _():
                first(False)
                lax.fori_loop(1, npq * i, lambda n, c: (step(n, False), c)[1], 0)
                for d in range(npq):
                    step(npq * i + d, True, seen(d), seen(d - 1))

        vals = pair_values(2 * (npq * (i + 1) - 1))
        for a, b in chunks(seen(npq - 1), mq):
            accumulate(vals, a, b)
        for a, b in chunks(0, mq):
            l0, l1 = ot_scr[LANES:LANES + 1, a:b], ot_scr[LANES + 1:LANES + 2, a:b]
            o_ref[a:b, :] = (ot_scr[0:LANES, a:b] / jnp.where(row < 64, l0, l1)).T
            l_ref[a:b, :] = jnp.where(row < 64, st_scr[0:1, a:b] + jnp.log2(l0), st_scr[1:2, a:b] + jnp.log2(l1)).T

    qspec = pl.BlockSpec((mq, 2 * LANES), lambda p, i: (i, p))
    kspec = pl.BlockSpec((s, 2 * LANES), lambda p, i: (0, p))
    vtspec = pl.BlockSpec((LANES, s), lambda p, i: (p, 0))
    ospec = pl.BlockSpec((mq, LANES), lambda p, i: (i, p))
    return pl.pallas_call(
        body, name="mla_fwd", grid=(4, s // mq),
        out_shape=(jax.ShapeDtypeStruct((s, 512), F32), jax.ShapeDtypeStruct((s, 512), F32)),
        in_specs=[qspec, kspec, vtspec], out_specs=(ospec, ospec),
        scratch_shapes=[pltpu.VMEM((4 * BK, mq), BF16), pltpu.VMEM((LANES + 8, mq), F32), pltpu.VMEM((16, mq), F32)],
        compiler_params=pltpu.CompilerParams(vmem_limit_bytes=VMEM_ATTN),
    )(qc, kc, vt)


def _mla_bwd(qc, kc, kct, v, do, lse, delta):
    s = qc.shape[0]
    mq = min(MQ_BWD, s)

    def body(q_ref, k_ref, kt_ref, v_ref, do_ref, l_ref, d_ref, dq_ref, dk_ref, dv_ref, dqt_scr, p_scr, dz_scr,
             dvt_scr):
        i = pl.program_id(1)

        @pl.when(i == 0)
        def _():
            dk_ref[...] = jnp.zeros_like(dk_ref)
            dvt_scr[...] = jnp.zeros_like(dvt_scr)

        lane = lax.broadcasted_iota(jnp.int32, (1, LANES), 1)
        keyc = lax.broadcasted_iota(jnp.int32, (BK, mq), 0)
        qryc = (lax.broadcasted_iota(jnp.int32, (BK, mq), 1) + i * mq) // 64
        qw = q_ref[...]
        dob = do_ref[...]
        dost = (dob.astype(F32) * MLA_SCALE).T.astype(BF16)
        dot_ = dob.astype(F32).T.astype(BF16)
        lt = l_ref[...].T
        dt = (d_ref[...] * MLA_SCALE).T
        lse0, lse1 = lt[0:1], lt[64:65]
        dl0, dl1 = dt[0:1], dt[64:65]
        dqt_scr[...] = jnp.zeros_like(dqt_scr)

        def products(j, lo=0):
            ks = pl.multiple_of(j * BK, BK)
            return (_mm_nt(_mla_keys(k_ref[pl.ds(ks, BK), :]), qw[lo:]),
                    _mm(_pair_stack(v_ref[pl.ds(ks, BK), :], lane), dost[:, lo:]))

        def grads(j, slot, zt, dwt, masked, lo=0):
            rows = slice(slot * BK, (slot + 1) * BK)
            for a in range(lo, mq, 2 * BK):
                b, za = a + 2 * BK, zt[:, a - lo:a - lo + 2 * BK] * (MLA_SCALE * LOG2E)
                p0 = jnp.exp2(za[0:BK] - lse0[:, a:b])
                p1 = jnp.exp2(za[BK:2 * BK] - lse1[:, a:b])
                if masked:
                    valid = ((keyc[:, a:b] + j * BK) // 64) <= qryc[:, a:b]
                    p0, p1 = jnp.where(valid, p0, 0.0), jnp.where(valid, p1, 0.0)
                p_scr[0, rows, a:b] = p0.astype(BF16)
                p_scr[1, rows, a:b] = p1.astype(BF16)
                dz_scr[0, rows, a:b] = (p0 * (dwt[0:BK, a - lo:b - lo] - dl0[:, a:b])).astype(BF16)
                dz_scr[1, rows, a:b] = (p1 * (dwt[BK:2 * BK, a - lo:b - lo] - dl1[:, a:b])).astype(BF16)

        def scatter(ja, lo=0):
            ks = pl.multiple_of(ja * BK, 2 * BK)
            for h in range(2):
                cols, vrows = slice(h * LANES, (h + 1) * LANES), slice(h * 64, (h + 1) * 64)
                dzh = dz_scr[h, :, lo:]
                dqt_scr[cols, lo:] += _mm(kt_ref[cols, pl.ds(ks, 2 * BK)], dzh)
                dk_ref[pl.ds(ks, 2 * BK), cols] += _mm(dzh, qw[lo:, cols])
                dvt_scr[vrows, pl.ds(ks, 2 * BK)] += _mm_nt(dot_[vrows, lo:], p_scr[h, :, lo:])

        def step(n, masked, lo=0, prev_lo=0):
            za, wa = products(2 * n, lo)
            zb, wb = products(2 * n + 1, lo)
            scatter(2 * n - 2, prev_lo)
            grads(2 * n, 0, za, wa, masked, lo)
            grads(2 * n + 1, 1, zb, wb, masked, lo)

        def first(masked):
            za, wa = products(0)
            zb, wb = products(1)
            grads(0, 0, za, wa, masked)
            grads(1, 1, zb, wb, masked)

        npq = mq // (2 * BK)
        seen = lambda d: 2 * BK * max(d, 0)

        @pl.when(i == 0)
        def _():
            first(True)
            for d in range(1, npq):
                step(d, True, seen(d), seen(d - 1))

        @pl.when(i > 0)
        def _():
            first(False)
            step(1, False)
            lax.fori_loop(1, npq * i // 2, lambda m, c: (step(2 * m, False), step(2 * m + 1, False), c)[2], 0)
            for d in range(npq):
                step(npq * i + d, True, seen(d), seen(d - 1))

        scatter(2 * (npq * (i + 1) - 1), seen(npq - 1))
        dq_ref[...] = dqt_scr[...].T

        @pl.when(i == s // mq - 1)
        def _():
            for a in range(0, s, 512):
                dv_ref[a:a + 512, :] = dvt_scr[:, a:a + 512].T

    qspec = pl.BlockSpec((mq, 2 * LANES), lambda p, i: (i, p))
    kspec = pl.BlockSpec((s, 2 * LANES), lambda p, i: (0, p))
    ktspec = pl.BlockSpec((2 * LANES, s), lambda p, i: (p, 0))
    vspec = pl.BlockSpec((s, LANES), lambda p, i: (0, p))
    ospec = pl.BlockSpec((mq, LANES), lambda p, i: (i, p))
    return pl.pallas_call(
        body, name="mla_bwd", grid=(4, s // mq),
        out_shape=(jax.ShapeDtypeStruct((s, 1024), F32), jax.ShapeDtypeStruct((s, 1024), F32),
                   jax.ShapeDtypeStruct((s, 512), F32)),
        in_specs=[qspec, kspec, ktspec, vspec, ospec, ospec, ospec], out_specs=(qspec, kspec, vspec),
        scratch_shapes=[pltpu.VMEM((2 * LANES, mq), F32), pltpu.VMEM((2, 2 * BK, mq), BF16), pltpu.VMEM((2, 2 * BK, mq), BF16),
                        pltpu.VMEM((LANES, s), F32)],
        compiler_params=pltpu.CompilerParams(vmem_limit_bytes=VMEM_ATTN),
    )(qc, kc, kct, v, do, lse, delta)


def _post(x, p, tgt, sbo, mlao, sbg, mlag, gsb, gmla, wout, gpost, wple, gple, wpg, bpg):
    s = x.shape[0]

    def body(x_ref, p_ref, t_ref, sbo_ref, mlao_ref, sbg_ref, mlag_ref, gsb_ref, gmla_ref, wout_ref,
             gpost_ref, wple_ref, gple_ref, wpg_ref, bpg_ref, bd_ref,
             dsbo_ref, dmlao_ref, delta_ref, dsbg_ref, dmlag_ref, dxres_ref, dwout_ref, dwpg_ref, dwple_ref, vec_ref):
        i = pl.program_id(0)

        @pl.when(i == 0)
        def _():
            dwout_ref[...] = jnp.zeros_like(dwout_ref)
            dwpg_ref[...] = jnp.zeros_like(dwpg_ref)
            dwple_ref[...] = jnp.zeros_like(dwple_ref)
            vec_ref[...] = jnp.zeros_like(vec_ref)

        inv_hd = 1.0 / HEAD_DIM

        def head_fwd(o, g, gate):
            r = lax.rsqrt(_seg(o * o, bd_ref[...]) * inv_hd + EPS)
            hat = o * r
            n = hat * g
            sg = _sigmoid(gate)
            return hat, r, n, sg, n * (gate * sg)

        sbo, mlao, sbg_v, mlag_v = sbo_ref[...], mlao_ref[...], sbg_ref[...], mlag_ref[...]
        gsb_v, gmla_v = gsb_ref[...], gmla_ref[...]
        sb_hat, sb_r, sb_n, sb_sg, sb_y = head_fwd(sbo, gsb_v, sbg_v)
        ml_hat, ml_r, ml_n, ml_sg, ml_y = head_fwd(mlao, gmla_v, mlag_v)
        mix = jnp.concatenate([sb_y, ml_y], axis=1).astype(BF16)
        y = _mm(mix, wout_ref[...])
        ry = lax.rsqrt(_rowmean(y * y) + EPS)
        y_hat = y * ry
        gpost_v = gpost_ref[...]
        x1 = x_ref[...] + y_hat * gpost_v
        pb = p_ref[...].astype(BF16)
        pl_ = jnp.concatenate([_mm(pb, wple_ref[k]) for k in range(N_SHARD)], axis=1)
        rp = lax.rsqrt(_rowmean(pl_ * pl_) + EPS)
        pl_hat = pl_ * rp
        gple_v = gple_ref[...]
        ple = pl_hat * gple_v
        x1b = x1.astype(BF16)
        gate = _sigmoid(_mm(x1b, wpg_ref[...]) + bpg_ref[...])
        err = x1 + ple * gate - t_ref[...]
        loss = 0.5 * jnp.sum(_rowmean(err * err))
        dout = err * (1.0 / D_MODEL)

        du = dout * ple * gate * (1.0 - gate)
        dub = du.astype(BF16)
        dple = dout * gate
        dx1 = dout + _mm_nt(dub, wpg_ref[...])
        dwpg_ref[...] += _mm_tn(x1b, dub)
        dplh = dple * gple_v
        dpl = rp * (dplh - pl_hat * _rowmean(dplh * pl_hat))
        dplb = dpl.astype(BF16)
        wsh = D_MODEL // N_SHARD
        for k in range(N_SHARD):
            dwple_ref[k] += _mm_tn(pb, dplb[:, k * wsh:(k + 1) * wsh])
        dxres_ref[...] = dx1
        dyh = dx1 * gpost_v
        dy = ry * (dyh - y_hat * _rowmean(dyh * y_hat))
        dyb = dy.astype(BF16)
        dwout_ref[...] += _mm_tn(mix, dyb)
        dmix = _mm_nt(dyb, wout_ref[...])

        def head_bwd(dyv, hat, r, n, sg, g, gate):
            dn = dyv * (gate * sg)
            dgate = dyv * n * (sg * (1.0 + gate * (1.0 - sg)))
            dhat = dn * g
            do = r * (dhat - hat * (_seg(dhat * hat, bd_ref[...]) * inv_hd))
            return do, dgate, _colsum(dn * hat)

        dsbo, dsbg, dg_sb = head_bwd(dmix[:, 0:512], sb_hat, sb_r, sb_n, sb_sg, gsb_v, sbg_v)
        dmlao, dmlag, dg_ml = head_bwd(dmix[:, 512:1024], ml_hat, ml_r, ml_n, ml_sg, gmla_v, mlag_v)
        dsbo_ref[...] = dsbo.astype(BF16)
        dmlao_ref[...] = dmlao.astype(BF16)
        delta_ref[...] = _seg(dmlao * mlao, bd_ref[...])
        dsbg_ref[...] = dsbg.astype(BF16)
        dmlag_ref[...] = dmlag.astype(BF16)
        vec_ref[pl.ds(0, 1), :] += _colsum(dx1 * y_hat)
        vec_ref[pl.ds(1, 1), :] += _colsum(dple * pl_hat)
        vec_ref[pl.ds(2, 1), :] += _colsum(du)
        vec_ref[pl.ds(3, 1), :] += jnp.concatenate([dg_sb, dg_ml], axis=1)
        vec_ref[pl.ds(4, 1), :] += jnp.full((1, D_MODEL), loss, F32)

    out_shape = (
        jax.ShapeDtypeStruct((s, 512), BF16), jax.ShapeDtypeStruct((s, 512), BF16), jax.ShapeDtypeStruct((s, 512), F32),
        jax.ShapeDtypeStruct((s, 512), BF16), jax.ShapeDtypeStruct((s, 512), BF16), jax.ShapeDtypeStruct((s, D_MODEL), F32),
        jax.ShapeDtypeStruct((D_MODEL, D_MODEL), F32), jax.ShapeDtypeStruct((D_MODEL, D_MODEL), F32),
        jax.ShapeDtypeStruct(wple.shape, F32), jax.ShapeDtypeStruct((8, D_MODEL), F32),
    )
    return pl.pallas_call(
        body, name="post_fwd_bwd", grid=(s // TM,), out_shape=out_shape,
        in_specs=[_rows(D_MODEL), _rows(PLE_DIM), _rows(D_MODEL), _rows(512), _rows(512), _rows(512), _rows(512),
                  _full((1, 512)), _full((1, 512)), _full((D_MODEL, D_MODEL)),
                  _full((1, D_MODEL)), _full(wple.shape), _full((1, D_MODEL)), _full((D_MODEL, D_MODEL)),
                  _full((1, D_MODEL)), _full((1024, 512))],
        out_specs=(_rows(512), _rows(512), _rows(512), _rows(512), _rows(512), _rows(D_MODEL),
                   _acc((D_MODEL, D_MODEL)), _acc((D_MODEL, D_MODEL)), _acc(wple.shape), _acc((8, D_MODEL))),
        compiler_params=pltpu.CompilerParams(vmem_limit_bytes=VMEM_DENSE),
    )(x, p, tgt, sbo, mlao, sbg, mlag, gsb, gmla, wout, gpost, wple, gple, wpg, bpg, _blockdiag2(512, HEAD_DIM))


def _pre_bwd(x, dxres, dsbq, dsbk, dsbv, dsbg, dmlag, dqc, dkc, dmv, cq, ckv, tabs, gpre, win, gq, wuq, gkv, wk, wv):
    s = x.shape[0]
    c_t, sa_t, sb_t = tabs
    rw = _rows

    def body(x_ref, dxres_ref, dsbq_ref, dsbk_ref, dsbv_ref, dsbg_ref, dmlag_ref, dqc_ref, dkc_ref, dmv_ref, cq_ref,
             ckv_ref, c_ref, sa_ref, sb_ref, gpre_ref, win_ref, gq_ref, wuq_ref, gkv_ref, wk_ref, wv_ref,
             gx_ref, dwin_ref, dwuq_ref, dwk_ref, dwv_ref, vec_ref, dwin_acc):
        i = pl.program_id(0)

        @pl.when(i == 0)
        def _():
            dwin_acc[...] = jnp.zeros_like(dwin_acc)
            dwuq_ref[...] = jnp.zeros_like(dwuq_ref)
            dwk_ref[...] = jnp.zeros_like(dwk_ref)
            dwv_ref[...] = jnp.zeros_like(dwv_ref)
            vec_ref[...] = jnp.zeros_like(vec_ref)

        lane = lax.broadcasted_iota(jnp.int32, (1, LANES), 1)
        c1, sa1, sb1 = c_ref[...], sa_ref[...], sb_ref[...]
        c8, sa8, sb8 = jnp.tile(c1, (1, 8)), jnp.tile(sa1, (1, 8)), jnp.tile(sb1, (1, 8))

        def norm_bwd(dn, hat, r, g):
            t = dn * g
            return r * (t - hat * _rowmean(t * hat)), _colsum(dn * hat)

        xv = x_ref[...]
        r1 = lax.rsqrt(_rowmean(xv * xv) + EPS)
        x_hat = xv * r1
        gpre_v = gpre_ref[...]
        hb = (x_hat * gpre_v).astype(BF16)
        ready = jnp.concatenate([dsbq_ref[...], dsbk_ref[...].astype(BF16), dsbv_ref[...].astype(BF16), dsbg_ref[...]], axis=1)
        dmlag = dmlag_ref[...]
        dwin_acc[:, 0:2048] += _mm_tn(hb, ready)
        dwin_acc[:, 2560:3072] += _mm_tn(hb, dmlag)
        dh = _mm_nt(ready, win_ref[:, 0:2048]) + _mm_nt(dmlag, win_ref[:, 2560:3072])

        dqeb = _rope_bwd(dqc_ref[...], c8, sa8, sb8).astype(BF16)
        cq = cq_ref[...]
        rq = lax.rsqrt(_rowmean(cq * cq) + EPS)
        cq_hat = cq * rq
        gq_v = gq_ref[...]
        dwuq_ref[...] += _mm_tn((cq_hat * gq_v).astype(BF16), dqeb)
        dcq, dg_q = norm_bwd(_mm_nt(dqeb, wuq_ref[...]), cq_hat, rq, gq_v)

        dkc = dkc_ref[...]
        dkcb = dkc.astype(BF16)
        dmvb = dmv_ref[...].astype(BF16)
        ckv = ckv_ref[...]
        rkv = lax.rsqrt(_rowmean(ckv * ckv) + EPS)
        ckv_hat = ckv * rkv
        gkv_v = gkv_ref[...]
        ckvnb = (ckv_hat * gkv_v).astype(BF16)
        dwk_ref[...] += _mm_tn(ckvnb, dkcb)
        dwv_ref[...] += _mm_tn(ckvnb, dmvb)
        dckv, dg_kv = norm_bwd(_mm_nt(dkcb, wk_ref[...]) + _mm_nt(dmvb, wv_ref[...]), ckv_hat, rkv, gkv_v)

        dkr = dkc[:, 0:LANES]
        for hh in range(1, 8):
            dkr = dkr + dkc[:, LANES * hh:LANES * (hh + 1)]
        dkr = _rope_bwd(dkr, c1, sa1, sb1)
        dkr = jnp.where((lane >= 64) & (lane < 96), dkr, 0.0)

        late = jnp.concatenate([dcq.astype(BF16), dckv.astype(BF16), dkr.astype(BF16)], axis=1)
        dwin_acc[:, 2048:2560] += _mm_tn(hb, late)
        dx, dg_pre = norm_bwd(dh + _mm_nt(late, win_ref[:, 2048:2560]), x_hat, r1, gpre_v)
        gx_ref[...] = dxres_ref[...] + dx
        vec_ref[pl.ds(0, 1), :] += dg_pre
        vec_ref[pl.ds(1, 1), :] += jnp.concatenate([dg_q, dg_kv, jnp.zeros((1, D_MODEL - Q_LORA - KV_LORA), F32)], axis=1)

        @pl.when(i == pl.num_programs(0) - 1)
        def _():
            pltpu.sync_copy(dwin_acc, dwin_ref)

    out_shape = (
        jax.ShapeDtypeStruct((s, D_MODEL), F32), jax.ShapeDtypeStruct((D_MODEL, D_EXT), F32),
        jax.ShapeDtypeStruct((Q_LORA, 1024), F32), jax.ShapeDtypeStruct((KV_LORA, 1024), F32),
        jax.ShapeDtypeStruct((KV_LORA, 512), F32), jax.ShapeDtypeStruct((8, D_MODEL), F32),
    )
    return pl.pallas_call(
        body, name="pre_bwd", grid=(s // TM,), out_shape=out_shape,
        in_specs=[rw(D_MODEL), rw(D_MODEL), rw(512), rw(512), rw(512), rw(512), rw(512),
                  rw(1024), rw(1024), rw(512), rw(Q_LORA), rw(KV_LORA), rw(LANES), rw(LANES),
                  rw(LANES), _full((1, D_MODEL)), _full((D_MODEL, D_EXT)), _full((1, Q_LORA)), _full((Q_LORA, 1024)),
                  _full((1, KV_LORA)), _full((KV_LORA, 1024)), _full((KV_LORA, 512))],
        out_specs=(rw(D_MODEL), pl.BlockSpec(memory_space=pl.ANY), _acc((Q_LORA, 1024)), _acc((KV_LORA, 1024)),
                   _acc((KV_LORA, 512)), _acc((8, D_MODEL))),
        scratch_shapes=[pltpu.VMEM((D_MODEL, D_EXT), F32)],
        compiler_params=pltpu.CompilerParams(vmem_limit_bytes=VMEM_DENSE),
    )(x, dxres, dsbq, dsbk, dsbv, dsbg, dmlag, dqc, dkc, dmv, cq, ckv, c_t, sa_t, sb_t, gpre, win, gq, wuq, gkv, wk, wv)


def _place():
    return lax.axis_index("x"), lax.axis_index("y"), lax.axis_index("c")


def _gather_steps(shapes, ins, bufs, send_sems, recv_sems):
    n = len(shapes)
    x, y, c = _place()
    me, sib = (x, y, c), (x, y, 1 - c)
    chips = [(1 - x, y), (x, 1 - y), (1 - x, 1 - y)]

    def half(t, chip, hc):
        rows = shapes[t][0] // 2
        return bufs[t].at[2 * chip[0] + chip[1], pl.ds(pl.multiple_of(hc * rows, 16), rows), :]

    def copy(k, t, chip, hc, to):
        return pltpu.make_async_remote_copy(src_ref=half(t, chip, hc), dst_ref=half(t, chip, hc), send_sem=send_sems.at[k],
                                            recv_sem=recv_sems.at[k], device_id=to, device_id_type=MESH)

    def start():
        for t in range(n):
            bufs[t][2 * x + y] = ins[t][...].astype(BF16)
            for j, chip in enumerate(chips):
                copy(6 * t + j, t, (x, y), c, (*chip, c)).start()

    def forward():
        for t in range(n):
            for j, chip in enumerate(chips):
                copy(6 * t + j, t, chip, c, me).wait_recv()
                copy(6 * t + 3 + j, t, chip, c, sib).start()

    def finish():
        for t in range(n):
            for j, chip in enumerate(chips):
                copy(6 * t + 3 + j, t, chip, 1 - c, me).wait_recv()
        for t in range(n):
            for j, chip in enumerate(chips):
                copy(6 * t + j, t, (x, y), c, (*chip, c)).wait_send()
                copy(6 * t + 3 + j, t, chip, c, sib).wait_send()

    return start, forward, finish


def _allgather_weights(shards):
    n = len(shards)

    def body(*refs):
        start, forward, finish = _gather_steps([a.shape for a in shards], refs[:n], refs[n:2 * n], refs[2 * n], refs[2 * n + 1])
        start()
        forward()
        finish()

    return pl.pallas_call(
        body, name="allgather_weights",
        out_shape=tuple(jax.ShapeDtypeStruct((N_SHARD,) + a.shape, BF16) for a in shards),
        in_specs=[pl.BlockSpec(memory_space=pltpu.VMEM)] * n, out_specs=(pl.BlockSpec(memory_space=pltpu.VMEM),) * n,
        scratch_shapes=[pltpu.SemaphoreType.DMA((6 * n,)), pltpu.SemaphoreType.DMA((6 * n,))],
        compiler_params=pltpu.CompilerParams(vmem_limit_bytes=VMEM_ATTN),
    )(*shards)


def _reduce_scratch(gsh):
    n = len(gsh)
    half_shapes = [(N_SHARD, a.shape[1] // 2, a.shape[2]) for a in gsh]
    return ([pltpu.VMEM(s_, F32) for s_ in half_shapes] * 2 + [pltpu.VMEM(s_, BF16) for s_ in half_shapes] * 2
            + [pltpu.SemaphoreType.DMA((n,)), pltpu.SemaphoreType.DMA((5 * n,)), pltpu.SemaphoreType.DMA((5 * n,))])


def _reduce_steps(halves, g_refs, f_refs, scratch):
    n = len(halves)
    accs, sibs, sbufs, rbufs = scratch[0:n], scratch[n:2 * n], scratch[2 * n:3 * n], scratch[3 * n:4 * n]
    local_sems, send_sems, recv_sems = scratch[4 * n:4 * n + 3]
    x, y, c = _place()
    me, sib = (x, y, c), (x, y, 1 - c)
    mine = 2 * x + y
    chips = [(1 - x, y), (x, 1 - y), (1 - x, 1 - y)]

    def remote(k, src, dst, to):
        return pltpu.make_async_remote_copy(src_ref=src, dst_ref=dst, send_sem=send_sems.at[k], recv_sem=recv_sems.at[k],
                                            device_id=to, device_id_type=MESH)

    def half3(ref, t, hc):
        return ref.at[:, pl.ds(pl.multiple_of(hc * halves[t], 8), halves[t]), :]

    def half2(ref, t, hc):
        return ref.at[pl.ds(pl.multiple_of(hc * halves[t], 8), halves[t]), :]

    def mine_load(t):
        return pltpu.make_async_copy(half3(g_refs[t], t, c), accs[t], local_sems.at[t])

    def to_sibling(t, to):
        return remote(t, half3(g_refs[t], t, 1 - c), sibs[t], to)

    def to_chip(t, j, chip, to):
        idx = 2 * chip[0] + chip[1]
        return remote(n + 3 * t + j, sbufs[t].at[idx], rbufs[t].at[mine if to is not me else idx], to)

    def swap(t, hc, to):
        return remote(4 * n + t, half2(f_refs[t], t, hc), half2(f_refs[t], t, hc), to)

    def load():
        for t in range(n):
            mine_load(t).start()
            to_sibling(t, sib).start()

    def partial():
        for t in range(n):
            mine_load(t).wait()
            to_sibling(t, me).wait_recv()
            for k in range(N_SHARD):
                accs[t][k] = accs[t][k] + sibs[t][k]
            for j, chip in enumerate(chips):
                idx = 2 * chip[0] + chip[1]
                sbufs[t][idx] = accs[t][idx].astype(BF16)
                to_chip(t, j, chip, (*chip, c)).start()

    def total():
        for t in range(n):
            acc = accs[t][mine]
            for j, chip in enumerate(chips):
                to_chip(t, j, chip, me).wait_recv()
                acc = acc + rbufs[t][2 * chip[0] + chip[1]].astype(F32)
            half2(f_refs[t], t, c)[...] = acc
            swap(t, c, sib).start()

    def finish():
        for t in range(n):
            swap(t, 1 - c, me).wait_recv()
        for t in range(n):
            to_sibling(t, sib).wait_send()
            for j, chip in enumerate(chips):
                to_chip(t, j, chip, (*chip, c)).wait_send()
            swap(t, c, sib).wait_send()

    return load, partial, total, finish


def _reduce_scatter_grads(gsh, vec):
    n = len(gsh)
    halves = [a.shape[1] // 2 for a in gsh]

    def body(*refs):
        g_refs, vec_ref, f_refs, vsum_ref = refs[:n], refs[n], refs[n + 1:2 * n + 1], refs[2 * n + 1]
        scratch = refs[2 * n + 2:]
        vrecv, vsend_sems, vrecv_sems = scratch[4 * n + 3:]
        load, partial, total, finish = _reduce_steps(halves, g_refs, f_refs, scratch)
        x, y, c = _place()
        my_dev = 4 * x + 2 * y + c

        def flip(k):
            return x ^ ((k >> 2) & 1), y ^ ((k >> 1) & 1), c ^ (k & 1)

        def vcopy(k, slot, to):
            return pltpu.make_async_remote_copy(src_ref=vec_ref, dst_ref=vrecv.at[slot], send_sem=vsend_sems.at[k - 1],
                                                recv_sem=vrecv_sems.at[k - 1], device_id=to, device_id_type=MESH)

        load()
        vrecv[my_dev] = vec_ref[...]
        for k in range(1, 8):
            vcopy(k, my_dev, flip(k)).start()
        partial()
        total()
        finish()
        for k in range(1, 8):
            fx, fy, fc = flip(k)
            vcopy(k, 4 * fx + 2 * fy + fc, (x, y, c)).wait_recv()
        vs = vrecv[0]
        for d in range(1, 8):
            vs = vs + vrecv[d]
        vsum_ref[...] = vs
        for k in range(1, 8):
            vcopy(k, my_dev, flip(k)).wait_send()

    return pl.pallas_call(
        body, name="reduce_scatter_grads",
        out_shape=tuple(jax.ShapeDtypeStruct(a.shape[1:], F32) for a in gsh) + (jax.ShapeDtypeStruct((VEC_ROWS, 1024), F32),),
        in_specs=[pl.BlockSpec(memory_space=pl.ANY)] * n + [pl.BlockSpec(memory_space=pltpu.VMEM)],
        out_specs=(pl.BlockSpec(memory_space=pltpu.VMEM),) * (n + 1),
        scratch_shapes=_reduce_scratch(gsh) + [pltpu.VMEM((8, VEC_ROWS, 1024), F32), pltpu.SemaphoreType.DMA((7,)),
                                               pltpu.SemaphoreType.DMA((7,))],
        compiler_params=pltpu.CompilerParams(vmem_limit_bytes=56 * 1024 * 1024),
    )(*gsh, vec)


def _adamw(w, g, m, v):
    rows, cols = w.shape
    tr = rows if rows <= 256 else 256
    flip = cols % LANES != 0

    def body(w_ref, g_ref, m_ref, v_ref, g_out, d_ref, nm_ref, nv_ref):
        gv = g_ref[...].T if flip else g_ref[...]
        outs = (gv,) + _adam_math(w_ref[...], gv, m_ref[...], v_ref[...])
        for ref, val in zip((g_out, d_ref, nm_ref, nv_ref), outs):
            ref[...] = val

    spec = pl.BlockSpec((tr, cols), lambda i: (i, 0))
    tspec = pl.BlockSpec((cols, tr), lambda i: (0, i)) if flip else spec
    shp = jax.ShapeDtypeStruct((cols, rows) if flip else (rows, cols), F32)
    if flip:
        w, m, v = w.T, m.T, v.T
    outs = pl.pallas_call(body, name="adamw", grid=(rows // tr,), out_shape=(shp,) * 4,
                          in_specs=[tspec, spec, tspec, tspec], out_specs=(tspec,) * 4)(w, g, m, v)
    return tuple(o.T for o in outs) if flip else outs


def _adam_math(w, g, m, v):
    m2 = ADAM_B1 * m + (1.0 - ADAM_B1) * g
    v2 = ADAM_B2 * v + (1.0 - ADAM_B2) * (g * g)
    m_hat = m2 / (1.0 - ADAM_B1 ** ADAM_STEP)
    v_hat = v2 / (1.0 - ADAM_B2 ** ADAM_STEP)
    return -ADAM_LR * (m_hat / (jnp.sqrt(v_hat) + ADAM_EPS) + ADAM_WD * w), m2, v2


def _adamw_small(vsum, w, m, v):
    names = [name for name, _, _, _ in _VEC_LAYOUT]
    k = len(names)

    def body(*refs):
        vs_ref, w_refs, m_refs, v_refs = refs[0], refs[1:1 + k], refs[1 + k:1 + 2 * k], refs[1 + 2 * k:1 + 3 * k]
        outs = refs[1 + 3 * k:]
        for idx, (_, r, c0, width) in enumerate(_VEC_LAYOUT):
            gv = vs_ref[pl.ds(r, 1), pl.ds(c0, width)]
            d, m2, v2 = _adam_math(w_refs[idx][...], gv, m_refs[idx][...], v_refs[idx][...])
            outs[idx][...], outs[k + idx][...], outs[2 * k + idx][...], outs[3 * k + idx][...] = gv, d, m2, v2

    shapes = tuple(jax.ShapeDtypeStruct(w[name].shape, F32) for name in names)
    res = pl.pallas_call(
        body, name="adamw_small", out_shape=shapes * 4,
        in_specs=[pl.BlockSpec(memory_space=pltpu.VMEM)] * (1 + 3 * k), out_specs=(pl.BlockSpec(memory_space=pltpu.VMEM),) * (4 * k),
    )(vsum, *[w[name] for name in names], *[m[name] for name in names], *[v[name] for name in names])
    return tuple({name: res[part * k + idx] for idx, name in enumerate(names)} for part in range(4))


_EARLY = ("w_in", "w_uq", "w_ukv")
_LATE = ("w_out", "w_ple", "w_ple_gate")
_BIG = _EARLY + _LATE
_KR_LOCAL = 2432 - 3 * (D_IN // N_SHARD)


def _extend_early(parts):
    cols = lambda a: a.transpose(1, 0, 2).reshape(a.shape[1], N_SHARD * a.shape[2])
    g = parts["w_in"]
    zeros = lambda n: jnp.zeros((D_MODEL, n), g.dtype)
    win_ext = jnp.concatenate([g[0], g[1], g[2], g[3][:, :_KR_LOCAL], zeros(64), g[3][:, _KR_LOCAL:_KR_LOCAL + QK_ROPE],
                               zeros(32), g[3][:, _KR_LOCAL + QK_ROPE:]], axis=1)
    wuq_ext = jnp.pad(cols(parts["w_uq"]).reshape(Q_LORA, 8, 96), ((0, 0), (0, 0), (0, 32))).reshape(Q_LORA, 1024)
    wukv = cols(parts["w_ukv"]).reshape(KV_LORA, 8, 128)
    wk_ext = jnp.pad(wukv[:, :, :64], ((0, 0), (0, 0), (0, 64))).reshape(KV_LORA, 1024)
    wv = wukv[:, :, 64:].reshape(KV_LORA, 512)
    return win_ext, wuq_ext, wk_ext, wv


def _shard_cols(a):
    return a.reshape(a.shape[0], N_SHARD, a.shape[1] // N_SHARD).transpose(1, 0, 2)


def _shard_rows(a):
    return a.reshape(N_SHARD, a.shape[0] // N_SHARD, a.shape[1])


def _shard_early_grads(dwin_ext, dwuq_ext, dwk_ext, dwv):
    e, w = dwin_ext, D_IN // N_SHARD
    last = jnp.concatenate([e[:, 3 * w:2432], e[:, 2496:2528], e[:, 2560:]], axis=1)
    dwuq = dwuq_ext.reshape(Q_LORA, 8, 128)[:, :, :96].reshape(Q_LORA, 768)
    dwukv = jnp.concatenate([dwk_ext.reshape(KV_LORA, 8, 128)[:, :, :64], dwv.reshape(KV_LORA, 8, 64)], axis=2)
    return [jnp.stack([e[:, 0:w], e[:, w:2 * w], e[:, 2 * w:3 * w], last]), _shard_cols(dwuq),
            _shard_cols(dwukv.reshape(KV_LORA, 1024))]


def _rope_tables(positions):
    half = QK_ROPE // 2
    freq = ROPE_THETA ** (-jnp.arange(half, dtype=F32) / half)
    s = positions.shape[0]
    per = LANES // half
    ang = jnp.repeat(positions.astype(F32).reshape(s // per, per), half, axis=1) * jnp.tile(freq, per)
    cos, sin = lax.optimization_barrier((jnp.cos(ang), jnp.sin(ang)))
    cos, sin = cos.reshape(s, half), sin.reshape(s, half)
    z = lambda n: jnp.zeros((s, n), F32)
    c_t = jnp.concatenate([jnp.ones((s, 64), F32), cos, cos, z(32)], axis=1)
    sa_t = jnp.concatenate([z(64), -sin, z(16), z(32)], axis=1)
    sb_t = jnp.concatenate([z(64), z(16), sin, z(32)], axis=1)
    return c_t, sa_t, sb_t


def _local_grads(x, p, positions, tgt, gains, early, late):
    win_ext, wuq_ext, wk_ext, wv = _extend_early(early)
    tabs = _rope_tables(positions)
    g = gains
    sbq, sbk, sbv, sbg, mlag, cq, ckv, qc, kc, mv, sbkt, sbvt, kct, mvt = _pre_fwd(
        x, tabs, g["norm_pre_g"], win_ext, g["q_norm_g"], wuq_ext, g["kv_norm_g"], wk_ext, wv)
    sbo, wout4, wple4, wpg4 = _sb_fwd(sbq, sbk, sbvt, late)
    wout, wpg = wout4.reshape(D_MODEL, D_MODEL), wpg4.reshape(D_MODEL, D_MODEL)
    mlao, lse = _mla_fwd(qc, kc, mvt)
    dsbo, dmlao, delta, dsbg, dmlag, dxres, dwout, dwpg, dwple, vec_c = _post(
        x, p, tgt, sbo, mlao, sbg, mlag, g["sb_out_norm_g"], g["mla_out_norm_g"], wout, g["norm_post_g"], wple4,
        g["ple_norm_g"], wpg, g["b_ple_gate"])
    dsbq, dsbk, dsbv, *late_grads = _sb_bwd(sbq, sbk, sbkt, sbv, dsbo, [_shard_rows(dwout), dwple, _shard_rows(dwpg)])
    dqc, dkc, dmv = _mla_bwd(qc, kc, kct, mv, dmlao, lse, delta)
    gx, dwin_ext, dwuq_ext, dwk_ext, dwv, vec_d = _pre_bwd(
        x, dxres, dsbq, dsbk, dsbv, dsbg, dmlag, dqc, dkc, dmv, cq, ckv, tabs, g["norm_pre_g"], win_ext, g["q_norm_g"],
        wuq_ext, g["kv_norm_g"], wk_ext, wv)
    return gx, _shard_early_grads(dwin_ext, dwuq_ext, dwk_ext, dwv), late_grads, jnp.concatenate([vec_c, vec_d], axis=0)


_VEC_LAYOUT = (("norm_post_g", 0, 0, 1024), ("ple_norm_g", 1, 0, 1024), ("b_ple_gate", 2, 0, 1024), ("sb_out_norm_g", 3, 0, 512),
               ("mla_out_norm_g", 3, 512, 512), ("norm_pre_g", 8, 0, 1024), ("q_norm_g", 9, 0, 256), ("kv_norm_g", 9, 256, 128))
_LOSS_ROW = 4
_WEIGHT_ORDER = ("norm_pre_g", "w_in", "q_norm_g", "w_uq", "kv_norm_g", "w_ukv", "sb_out_norm_g", "mla_out_norm_g", "w_out",
                 "norm_post_g", "w_ple", "ple_norm_g", "w_ple_gate", "b_ple_gate")


def kernel(x, p, positions, norm_pre_g, w_in, q_norm_g, w_uq, kv_norm_g, w_ukv, sb_out_norm_g, mla_out_norm_g, w_out, norm_post_g, w_ple, ple_norm_g, w_ple_gate, b_ple_gate, loss_target, m_norm_pre_g, m_w_in, m_q_norm_g, m_w_uq, m_kv_norm_g, m_w_ukv, m_sb_out_norm_g, m_mla_out_norm_g, m_w_out, m_norm_post_g, m_w_ple, m_ple_norm_g, m_w_ple_gate, m_b_ple_gate, v_norm_pre_g, v_w_in, v_q_norm_g, v_w_uq, v_kv_norm_g, v_w_ukv, v_sb_out_norm_g, v_mla_out_norm_g, v_w_out, v_norm_post_g, v_w_ple, v_ple_norm_g, v_w_ple_gate, v_b_ple_gate):
    w = {"norm_pre_g": norm_pre_g, "w_in": w_in[0], "q_norm_g": q_norm_g, "w_uq": w_uq[0], "kv_norm_g": kv_norm_g, "w_ukv": w_ukv[0],
         "sb_out_norm_g": sb_out_norm_g, "mla_out_norm_g": mla_out_norm_g, "w_out": w_out[0], "norm_post_g": norm_post_g,
         "w_ple": w_ple[0], "ple_norm_g": ple_norm_g, "w_ple_gate": w_ple_gate[0], "b_ple_gate": b_ple_gate}
    m = {"norm_pre_g": m_norm_pre_g, "w_in": m_w_in[0], "q_norm_g": m_q_norm_g, "w_uq": m_w_uq[0], "kv_norm_g": m_kv_norm_g,
         "w_ukv": m_w_ukv[0], "sb_out_norm_g": m_sb_out_norm_g, "mla_out_norm_g": m_mla_out_norm_g, "w_out": m_w_out[0],
         "norm_post_g": m_norm_post_g, "w_ple": m_w_ple[0], "ple_norm_g": m_ple_norm_g, "w_ple_gate": m_w_ple_gate[0],
         "b_ple_gate": m_b_ple_gate}
    v = {"norm_pre_g": v_norm_pre_g, "w_in": v_w_in[0], "q_norm_g": v_q_norm_g, "w_uq": v_w_uq[0], "kv_norm_g": v_kv_norm_g,
         "w_ukv": v_w_ukv[0], "sb_out_norm_g": v_sb_out_norm_g, "mla_out_norm_g": v_mla_out_norm_g, "w_out": v_w_out[0],
         "norm_post_g": v_norm_post_g, "w_ple": v_w_ple[0], "ple_norm_g": v_ple_norm_g, "w_ple_gate": v_w_ple_gate[0],
         "b_ple_gate": v_b_ple_gate}
    gathered = _allgather_weights([w[n] for n in _EARLY])
    gx, early_grads, late_red, vec = _local_grads(x[0], p[0, 0], positions[0], loss_target[0], w, dict(zip(_EARLY, gathered)),
                                                  [w[n] for n in _LATE])
    *early_red, vsum = _reduce_scatter_grads(early_grads, vec)
    gred = early_red + late_red
    loss = vsum[_LOSS_ROW, 0]

    g, delta, new_m, new_v = _adamw_small(vsum, w, m, v)
    for n, gn in zip(_BIG, gred):
        g[n], delta[n], new_m[n], new_v[n] = _adamw(w[n], gn, m[n], v[n])

    lead = lambda n, a: a[None] if n in _BIG else a
    return (loss, gx[None],
            *[lead(n, g[n]) for n in _WEIGHT_ORDER], *[lead(n, delta[n]) for n in _WEIGHT_ORDER],
            *[lead(n, new_m[n]) for n in _WEIGHT_ORDER], *[lead(n, new_v[n]) for n in _WEIGHT_ORDER])
```
